```python
import jax, jax.numpy as jnp
from jax import lax
import numpy as np

D_MODEL = 2048
BATCH = 16
SEQ = 2048
DEPTH = 2

CHUNK = 128
GM_WIDTH = D_MODEL // 4
GM_HEAD_DIM = 128
GM_HEADS = GM_WIDTH // GM_HEAD_DIM
ATT_WIDTH = D_MODEL // 4
ATT_HEAD_DIM = 64
ATT_HEADS = ATT_WIDTH // ATT_HEAD_DIM
ATT_KV_HEADS = ATT_HEADS // 4
KV_WIDTH = ATT_KV_HEADS * ATT_HEAD_DIM
WINDOW = 128
SSM_WIDTH = D_MODEL // 2
SSM_HEAD_DIM = 64
SSM_HEADS = SSM_WIDTH // SSM_HEAD_DIM
SSM_GROUPS = 2
SSM_STATE = 128
CONV_WIDTH = 4
BC_WIDTH = SSM_GROUPS * SSM_STATE
CONV_CH = SSM_WIDTH + 2 * BC_WIDTH
MIX_WIDTH = GM_WIDTH + ATT_WIDTH + SSM_WIDTH
IN_SIZES = (GM_WIDTH, GM_WIDTH, ATT_WIDTH, KV_WIDTH, KV_WIDTH, SSM_WIDTH, CONV_CH, SSM_HEADS)
IN_WIDTH = 2 * GM_WIDTH + ATT_WIDTH + 2 * KV_WIDTH + SSM_WIDTH + CONV_CH + SSM_HEADS
D_FF = 4 * D_MODEL
NEG_INF = -1e30
EPS = 1e-6

kernel_name = 'hybrid_gmlp_swa_ssd_parallel_heads'


def rms_norm(x, g):
    xf = x.astype(jnp.float32)
    y = xf * lax.rsqrt(jnp.mean(xf * xf, axis=-1, keepdims=True) + EPS)
    return (y * g.astype(jnp.float32)).astype(x.dtype)


def layer_norm(x, g, b):
    xf = x.astype(jnp.float32)
    mu = jnp.mean(xf, axis=-1, keepdims=True)
    xc = xf - mu
    y = xc * lax.rsqrt(jnp.mean(xc * xc, axis=-1, keepdims=True) + 1e-5)
    return (y * g.astype(jnp.float32) + b.astype(jnp.float32)).astype(x.dtype)


def split_projection(p):
    offs = np.cumsum(np.array(IN_SIZES))[:-1].tolist()
    return jnp.split(p, offs, axis=-1)


def spatial_gating_mixer(u, v, ln_g, ln_b, w_s, b_s, out_g):
    bsz, s, _ = u.shape
    nc = s // CHUNK
    u = jax.nn.gelu(u).reshape(bsz, nc, CHUNK, GM_HEADS, GM_HEAD_DIM)
    v = jax.nn.gelu(v).reshape(bsz, s, GM_HEADS, GM_HEAD_DIM)
    v = layer_norm(v, ln_g, ln_b).reshape(bsz, nc, CHUNK, GM_HEADS, GM_HEAD_DIM)
    w = jnp.tril(w_s).astype(v.dtype)
    gate = jnp.einsum('hts,bcshe->bcthe', w, v) + b_s.T.astype(v.dtype)[None, None, :, :, None]
    y = (u * gate).reshape(bsz, s, GM_WIDTH)
    return rms_norm(y, out_g)


def sliding_window_sink_attention(q, k, v, sinks, out_g):
    bsz, s, _ = q.shape
    nb = s // WINDOW
    grp = ATT_HEADS // ATT_KV_HEADS
    qb = q.reshape(bsz, nb, WINDOW, ATT_KV_HEADS, grp, ATT_HEAD_DIM)
    pad = ((0, 0), (WINDOW, 0), (0, 0))
    kp = jnp.pad(k, pad).reshape(bsz, nb + 1, WINDOW, ATT_KV_HEADS, ATT_HEAD_DIM)
    vp = jnp.pad(v, pad).reshape(bsz, nb + 1, WINDOW, ATT_KV_HEADS, ATT_HEAD_DIM)
    kb = jnp.concatenate([kp[:, :-1], kp[:, 1:]], axis=2)
    vb = jnp.concatenate([vp[:, :-1], vp[:, 1:]], axis=2)
    scores = jnp.einsum('bnqkgd,bnjkd->bnkgqj', qb, kb).astype(jnp.float32) * (ATT_HEAD_DIM ** -0.5)
    qi = jnp.arange(WINDOW)[:, None]
    kj = jnp.arange(2 * WINDOW)[None, :]
    diff = qi + WINDOW - kj
    band = (diff >= 0) & (diff < WINDOW)
    blk = jnp.arange(nb)[:, None, None]
    valid = band[None] & ((blk * WINDOW + kj[None] - WINDOW) >= 0)
    scores = jnp.where(valid[None, :, None, None], scores, NEG_INF)
    sink = sinks.astype(jnp.float32).reshape(ATT_KV_HEADS, grp)[None, None, :, :, None, None]
    m = jnp.maximum(jnp.max(scores, axis=-1, keepdims=True), sink)
    e = jnp.exp(scores - m)
    p = e / (jnp.sum(e, axis=-1, keepdims=True) + jnp.exp(sink - m))
    o = jnp.einsum('bnkgqj,bnjkd->bnqkgd', p.astype(v.dtype), vb)
    return rms_norm(o.reshape(bsz, s, ATT_WIDTH), out_g)


def causal_depthwise_conv(x, w, b):
    kern = w[:, None, :].astype(x.dtype)
    out = lax.conv_general_dilated(x, kern, window_strides=(1,), padding=((CONV_WIDTH - 1, 0),),
                                   dimension_numbers=('NWC', 'WIO', 'NWC'),
                                   feature_group_count=x.shape[-1])
    return out + b.astype(x.dtype)


def ssd_chunked_scan(xs, dt, a_log, bm, cm, d_skip):
    bsz, s, nh, hp = xs.shape
    nc = s // CHUNK
    hg = nh // SSM_GROUPS
    a = -jnp.exp(a_log.astype(jnp.float32))
    xf = xs.astype(jnp.float32)
    xdt = (xf * dt[..., None]).reshape(bsz, nc, CHUNK, SSM_GROUPS, hg, hp)
    da = (dt * a).reshape(bsz, nc, CHUNK, SSM_GROUPS, hg).transpose(0, 1, 3, 4, 2)
    bc = bm.astype(jnp.float32).reshape(bsz, nc, CHUNK, SSM_GROUPS, SSM_STATE)
    cc = cm.astype(jnp.float32).reshape(bsz, nc, CHUNK, SSM_GROUPS, SSM_STATE)
    a_cs = jnp.cumsum(da, axis=-1)
    idx = jnp.arange(CHUNK)
    causal = idx[:, None] >= idx[None, :]
    decay = jnp.exp(jnp.where(causal, a_cs[..., :, None] - a_cs[..., None, :], -jnp.inf))
    cb = jnp.einsum('bclgn,bcsgn->bcgls', cc, bc)
    y_diag = jnp.einsum('bcghls,bcsghp->bclghp', cb[:, :, :, None] * decay, xdt)
    decay_states = jnp.exp(a_cs[..., -1:] - a_cs)
    states = jnp.einsum('bclgn,bcghl,bclghp->bcghpn', bc, decay_states, xdt)
    chunk_decay = jnp.exp(a_cs[..., -1])

    def step(carry, inp):
        st, dec = inp
        return carry * dec[..., None, None] + st, carry

    init = jnp.zeros((bsz, SSM_GROUPS, hg, hp, SSM_STATE), jnp.float32)
    _, prev = lax.scan(step, init, (jnp.moveaxis(states, 1, 0), jnp.moveaxis(chunk_decay, 1, 0)))
    prev = jnp.moveaxis(prev, 0, 1)
    y_off = jnp.einsum('bclgn,bcghpn,bcghl->bclghp', cc, prev, jnp.exp(a_cs))
    y = (y_diag + y_off).reshape(bsz, s, nh, hp)
    return y + xf * d_skip.astype(jnp.float32)[:, None]


def ssd_mixer(z, xbc, dt_raw, conv_w, conv_b, dt_bias, a_log, d_skip, norm_g):
    bsz, s, _ = z.shape
    xbc = jax.nn.silu(causal_depthwise_conv(xbc, conv_w, conv_b))
    xs, bm, cm = jnp.split(xbc, [SSM_WIDTH, SSM_WIDTH + BC_WIDTH], axis=-1)
    xs = xs.reshape(bsz, s, SSM_HEADS, SSM_HEAD_DIM)
    bm = bm.reshape(bsz, s, SSM_GROUPS, SSM_STATE)
    cm = cm.reshape(bsz, s, SSM_GROUPS, SSM_STATE)
    dt = jax.nn.softplus(dt_raw.astype(jnp.float32) + dt_bias.astype(jnp.float32))
    y = ssd_chunked_scan(xs, dt, a_log, bm, cm, d_skip)
    y = y.reshape(bsz, s, SSM_WIDTH) * jax.nn.silu(z.astype(jnp.float32))
    yg = y.reshape(bsz, s, SSM_GROUPS, SSM_WIDTH // SSM_GROUPS)
    yg = yg * lax.rsqrt(jnp.mean(yg * yg, axis=-1, keepdims=True) + EPS)
    return (yg.reshape(bsz, s, SSM_WIDTH) * norm_g.astype(jnp.float32)).astype(z.dtype)


def _fwd_setup_inputs(seed: int = 0) -> dict:
    key = jax.random.key(seed)
    ks = jax.random.split(key, 26)
    L, D = DEPTH, D_MODEL

    def nrm(k, shape, scale):
        return jax.random.normal(k, shape, jnp.float32) * scale

    dt0 = jnp.exp(jax.random.uniform(ks[15], (L, SSM_HEADS), jnp.float32, np.log(1e-3), np.log(1e-1)))
    return {
        'x': nrm(ks[0], (BATCH, SEQ, D), 1.0),
        'c': nrm(ks[1], (BATCH, D), 1.0),
        'ada_w': nrm(ks[2], (L, D, 6 * D), 0.5 * D ** -0.5),
        'ada_b': nrm(ks[3], (L, 6 * D), 0.02),
        'norm1_g': 1.0 + nrm(ks[4], (L, D), 0.05),
        'w_in': nrm(ks[5], (L, D, IN_WIDTH), D ** -0.5),
        'gm_ln_g': 1.0 + nrm(ks[6], (L, GM_HEADS, GM_HEAD_DIM), 0.05),
        'gm_ln_b': nrm(ks[7], (L, GM_HEADS, GM_HEAD_DIM), 0.02),
        'gm_ws': nrm(ks[8], (L, GM_HEADS, CHUNK, CHUNK), CHUNK ** -0.5),
        'gm_bs': 1.0 + nrm(ks[9], (L, GM_HEADS, CHUNK), 0.05),
        'gm_norm_g': 1.0 + nrm(ks[10], (L, GM_WIDTH), 0.05),
        'attn_sinks': nrm(ks[11], (L, ATT_HEADS), 0.5),
        'attn_norm_g': 1.0 + nrm(ks[12], (L, ATT_WIDTH), 0.05),
        'conv_w': nrm(ks[13], (L, CONV_WIDTH, CONV_CH), CONV_WIDTH ** -0.5),
        'conv_b': nrm(ks[14], (L, CONV_CH), 0.02),
        'dt_bias': dt0 + jnp.log(-jnp.expm1(-dt0)),
        'a_log': jnp.log(jax.random.uniform(ks[16], (L, SSM_HEADS), jnp.float32, 1.0, 16.0)),
        'd_skip': 1.0 + nrm(ks[17], (L, SSM_HEADS), 0.1),
        'ssm_norm_g': 1.0 + nrm(ks[18], (L, SSM_WIDTH), 0.05),
        'w_out': nrm(ks[19], (L, MIX_WIDTH, D), MIX_WIDTH ** -0.5),
        'norm2_g': 1.0 + nrm(ks[20], (L, D), 0.05),
        'w_mlp1': nrm(ks[21], (L, D, D_FF), D ** -0.5),
        'w_mlp2': nrm(ks[22], (L, D_FF, D), D_FF ** -0.5),
        'final_norm_g': 1.0 + nrm(ks[23], (D,), 0.05),
    }


def _fwd_reference(x, c, ada_w, ada_b, norm1_g, w_in, gm_ln_g, gm_ln_b, gm_ws, gm_bs, gm_norm_g,
              attn_sinks, attn_norm_g, conv_w, conv_b, dt_bias, a_log, d_skip, ssm_norm_g,
              w_out, norm2_g, w_mlp1, w_mlp2, final_norm_g):
    c_act = jax.nn.silu(c)
    for l in range(DEPTH):
        mod = c_act @ ada_w[l] + ada_b[l]
        sh1, sc1, g1, sh2, sc2, g2 = [m[:, None, :] for m in jnp.split(mod, 6, axis=-1)]
        h = rms_norm(x, norm1_g[l]) * (1.0 + sc1) + sh1
        u_a, v_a, q_b, k_b, v_b, z_c, xbc_c, dt_c = split_projection(h @ w_in[l])
        out_a = spatial_gating_mixer(u_a, v_a, gm_ln_g[l], gm_ln_b[l], gm_ws[l], gm_bs[l], gm_norm_g[l])
        out_b = sliding_window_sink_attention(q_b, k_b, v_b, attn_sinks[l], attn_norm_g[l])
        out_c = ssd_mixer(z_c, xbc_c, dt_c, conv_w[l], conv_b[l], dt_bias[l], a_log[l], d_skip[l], ssm_norm_g[l])
        mix = jnp.concatenate([out_a, out_b, out_c], axis=-1) @ w_out[l]
        x = x + g1 * mix
        h = rms_norm(x, norm2_g[l]) * (1.0 + sc2) + sh2
        x = x + g2 * (jnp.square(jax.nn.relu(h @ w_mlp1[l])) @ w_mlp2[l])
    return rms_norm(x, final_norm_g)


import jax as _jax
import jax.numpy as _jnp

TWIN_FORMAT = 'train_step'
FWD_PARAMS = ['x', 'c', 'ada_w', 'ada_b', 'norm1_g', 'w_in', 'gm_ln_g', 'gm_ln_b', 'gm_ws', 'gm_bs', 'gm_norm_g', 'attn_sinks', 'attn_norm_g', 'conv_w', 'conv_b', 'dt_bias', 'a_log', 'd_skip', 'ssm_norm_g', 'w_out', 'norm2_g', 'w_mlp1', 'w_mlp2', 'final_norm_g']
TWIN_WEIGHTS = ['ada_w', 'ada_b', 'norm1_g', 'w_in', 'gm_ln_g', 'gm_ln_b', 'gm_ws', 'gm_bs', 'gm_norm_g', 'attn_sinks', 'attn_norm_g', 'conv_w', 'conv_b', 'dt_bias', 'a_log', 'd_skip', 'ssm_norm_g', 'w_out', 'norm2_g', 'w_mlp1', 'w_mlp2', 'final_norm_g']
TWIN_DIFF_INPUT = 'x'
TWIN_INPUTS = ['x', 'c', 'ada_w', 'ada_b', 'norm1_g', 'w_in', 'gm_ln_g', 'gm_ln_b', 'gm_ws', 'gm_bs', 'gm_norm_g', 'attn_sinks', 'attn_norm_g', 'conv_w', 'conv_b', 'dt_bias', 'a_log', 'd_skip', 'ssm_norm_g', 'w_out', 'norm2_g', 'w_mlp1', 'w_mlp2', 'final_norm_g', 'loss_target', 'm_ada_w', 'm_ada_b', 'm_norm1_g', 'm_w_in', 'm_gm_ln_g', 'm_gm_ln_b', 'm_gm_ws', 'm_gm_bs', 'm_gm_norm_g', 'm_attn_sinks', 'm_attn_norm_g', 'm_conv_w', 'm_conv_b', 'm_dt_bias', 'm_a_log', 'm_d_skip', 'm_ssm_norm_g', 'm_w_out', 'm_norm2_g', 'm_w_mlp1', 'm_w_mlp2', 'm_final_norm_g', 'v_ada_w', 'v_ada_b', 'v_norm1_g', 'v_w_in', 'v_gm_ln_g', 'v_gm_ln_b', 'v_gm_ws', 'v_gm_bs', 'v_gm_norm_g', 'v_attn_sinks', 'v_attn_norm_g', 'v_conv_w', 'v_conv_b', 'v_dt_bias', 'v_a_log', 'v_d_skip', 'v_ssm_norm_g', 'v_w_out', 'v_norm2_g', 'v_w_mlp1', 'v_w_mlp2', 'v_final_norm_g']
TWIN_OUTPUTS = ['loss', 'grad_x', 'grad_ada_w', 'grad_ada_b', 'grad_norm1_g', 'grad_w_in', 'grad_gm_ln_g', 'grad_gm_ln_b', 'grad_gm_ws', 'grad_gm_bs', 'grad_gm_norm_g', 'grad_attn_sinks', 'grad_attn_norm_g', 'grad_conv_w', 'grad_conv_b', 'grad_dt_bias', 'grad_a_log', 'grad_d_skip', 'grad_ssm_norm_g', 'grad_w_out', 'grad_norm2_g', 'grad_w_mlp1', 'grad_w_mlp2', 'grad_final_norm_g', 'delta_ada_w', 'delta_ada_b', 'delta_norm1_g', 'delta_w_in', 'delta_gm_ln_g', 'delta_gm_ln_b', 'delta_gm_ws', 'delta_gm_bs', 'delta_gm_norm_g', 'delta_attn_sinks', 'delta_attn_norm_g', 'delta_conv_w', 'delta_conv_b', 'delta_dt_bias', 'delta_a_log', 'delta_d_skip', 'delta_ssm_norm_g', 'delta_w_out', 'delta_norm2_g', 'delta_w_mlp1', 'delta_w_mlp2', 'delta_final_norm_g', 'new_m_ada_w', 'new_m_ada_b', 'new_m_norm1_g', 'new_m_w_in', 'new_m_gm_ln_g', 'new_m_gm_ln_b', 'new_m_gm_ws', 'new_m_gm_bs', 'new_m_gm_norm_g', 'new_m_attn_sinks', 'new_m_attn_norm_g', 'new_m_conv_w', 'new_m_conv_b', 'new_m_dt_bias', 'new_m_a_log', 'new_m_d_skip', 'new_m_ssm_norm_g', 'new_m_w_out', 'new_m_norm2_g', 'new_m_w_mlp1', 'new_m_w_mlp2', 'new_m_final_norm_g', 'new_v_ada_w', 'new_v_ada_b', 'new_v_norm1_g', 'new_v_w_in', 'new_v_gm_ln_g', 'new_v_gm_ln_b', 'new_v_gm_ws', 'new_v_gm_bs', 'new_v_gm_norm_g', 'new_v_attn_sinks', 'new_v_attn_norm_g', 'new_v_conv_w', 'new_v_conv_b', 'new_v_dt_bias', 'new_v_a_log', 'new_v_d_skip', 'new_v_ssm_norm_g', 'new_v_w_out', 'new_v_norm2_g', 'new_v_w_mlp1', 'new_v_w_mlp2', 'new_v_final_norm_g']
TWIN_LEAF_KINDS = {'loss': 'loss', 'grad_x': 'grad_x', 'grad_ada_w': 'grad_w', 'grad_ada_b': 'grad_w', 'grad_norm1_g': 'grad_w', 'grad_w_in': 'grad_w', 'grad_gm_ln_g': 'grad_w', 'grad_gm_ln_b': 'grad_w', 'grad_gm_ws': 'grad_w', 'grad_gm_bs': 'grad_w', 'grad_gm_norm_g': 'grad_w', 'grad_attn_sinks': 'grad_w', 'grad_attn_norm_g': 'grad_w', 'grad_conv_w': 'grad_w', 'grad_conv_b': 'grad_w', 'grad_dt_bias': 'grad_w', 'grad_a_log': 'grad_w', 'grad_d_skip': 'grad_w', 'grad_ssm_norm_g': 'grad_w', 'grad_w_out': 'grad_w', 'grad_norm2_g': 'grad_w', 'grad_w_mlp1': 'grad_w', 'grad_w_mlp2': 'grad_w', 'grad_final_norm_g': 'grad_w', 'delta_ada_w': 'delta_w', 'delta_ada_b': 'delta_w', 'delta_norm1_g': 'delta_w', 'delta_w_in': 'delta_w', 'delta_gm_ln_g': 'delta_w', 'delta_gm_ln_b': 'delta_w', 'delta_gm_ws': 'delta_w', 'delta_gm_bs': 'delta_w', 'delta_gm_norm_g': 'delta_w', 'delta_attn_sinks': 'delta_w', 'delta_attn_norm_g': 'delta_w', 'delta_conv_w': 'delta_w', 'delta_conv_b': 'delta_w', 'delta_dt_bias': 'delta_w', 'delta_a_log': 'delta_w', 'delta_d_skip': 'delta_w', 'delta_ssm_norm_g': 'delta_w', 'delta_w_out': 'delta_w', 'delta_norm2_g': 'delta_w', 'delta_w_mlp1': 'delta_w', 'delta_w_mlp2': 'delta_w', 'delta_final_norm_g': 'delta_w', 'new_m_ada_w': 'new_m', 'new_m_ada_b': 'new_m', 'new_m_norm1_g': 'new_m', 'new_m_w_in': 'new_m', 'new_m_gm_ln_g': 'new_m', 'new_m_gm_ln_b': 'new_m', 'new_m_gm_ws': 'new_m', 'new_m_gm_bs': 'new_m', 'new_m_gm_norm_g': 'new_m', 'new_m_attn_sinks': 'new_m', 'new_m_attn_norm_g': 'new_m', 'new_m_conv_w': 'new_m', 'new_m_conv_b': 'new_m', 'new_m_dt_bias': 'new_m', 'new_m_a_log': 'new_m', 'new_m_d_skip': 'new_m', 'new_m_ssm_norm_g': 'new_m', 'new_m_w_out': 'new_m', 'new_m_norm2_g': 'new_m', 'new_m_w_mlp1': 'new_m', 'new_m_w_mlp2': 'new_m', 'new_m_final_norm_g': 'new_m', 'new_v_ada_w': 'new_v', 'new_v_ada_b': 'new_v', 'new_v_norm1_g': 'new_v', 'new_v_w_in': 'new_v', 'new_v_gm_ln_g': 'new_v', 'new_v_gm_ln_b': 'new_v', 'new_v_gm_ws': 'new_v', 'new_v_gm_bs': 'new_v', 'new_v_gm_norm_g': 'new_v', 'new_v_attn_sinks': 'new_v', 'new_v_attn_norm_g': 'new_v', 'new_v_conv_w': 'new_v', 'new_v_conv_b': 'new_v', 'new_v_dt_bias': 'new_v', 'new_v_a_log': 'new_v', 'new_v_d_skip': 'new_v', 'new_v_ssm_norm_g': 'new_v', 'new_v_w_out': 'new_v', 'new_v_norm2_g': 'new_v', 'new_v_w_mlp1': 'new_v', 'new_v_w_mlp2': 'new_v', 'new_v_final_norm_g': 'new_v'}


def _forward(args):
    return _fwd_reference(*[args[k] for k in FWD_PARAMS])


def _output_shape():
    out = _jax.eval_shape(lambda: _forward(_fwd_setup_inputs(0)))
    return out.shape, out.dtype

N_MICROBATCH = 1
ADAM_LR = 0.001
ADAM_B1 = 0.9
ADAM_B2 = 0.999
ADAM_EPS = 1e-08
ADAM_WD = 0.01
ADAM_STEP = 10
PER_EXAMPLE_BATCH_AXIS = {'x': 0, 'c': 0, 'loss_target': 0}
SHARED_INPUTS = []
_WEIGHT_DTYPES = {'ada_w': _jnp.float32, 'ada_b': _jnp.float32, 'norm1_g': _jnp.float32, 'w_in': _jnp.float32, 'gm_ln_g': _jnp.float32, 'gm_ln_b': _jnp.float32, 'gm_ws': _jnp.float32, 'gm_bs': _jnp.float32, 'gm_norm_g': _jnp.float32, 'attn_sinks': _jnp.float32, 'attn_norm_g': _jnp.float32, 'conv_w': _jnp.float32, 'conv_b': _jnp.float32, 'dt_bias': _jnp.float32, 'a_log': _jnp.float32, 'd_skip': _jnp.float32, 'ssm_norm_g': _jnp.float32, 'w_out': _jnp.float32, 'norm2_g': _jnp.float32, 'w_mlp1': _jnp.float32, 'w_mlp2': _jnp.float32, 'final_norm_g': _jnp.float32}
MOMENT_SCALE = {'ada_w': 7.777400e-02, 'ada_b': 1.460224e-01, 'norm1_g': 3.144489e-02, 'w_in': 2.613755e-02, 'gm_ln_g': 1.377308e-02, 'gm_ln_b': 1.428349e-02, 'gm_ws': 1.414913e-02, 'gm_bs': 1.965107e-02, 'gm_norm_g': 3.183761e-02, 'attn_sinks': 7.421805e-03, 'attn_norm_g': 5.231257e-02, 'conv_w': 2.280193e-02, 'conv_b': 3.059488e-02, 'dt_bias': 5.492288e-02, 'a_log': 9.573008e-02, 'd_skip': 1.638116e-01, 'ssm_norm_g': 3.170951e-02, 'w_out': 3.413593e-02, 'norm2_g': 4.028746e-02, 'w_mlp1': 2.073815e-02, 'w_mlp2': 4.856479e-02, 'final_norm_g': 1.619247e+01}


def _to_microbatches(a, axis):
    t = _jnp.moveaxis(a, axis, 0)
    t = t.reshape((N_MICROBATCH, t.shape[0] // N_MICROBATCH) + t.shape[1:])
    return _jnp.moveaxis(t, 1, axis + 1)


def setup_inputs(seed: int = 0) -> dict:
    inp = _fwd_setup_inputs(seed)
    key = _jax.random.fold_in(_jax.random.key(seed), 7919)
    shape, _ = _output_shape()
    out = dict(inp)
    out["loss_target"] = _jax.random.normal(_jax.random.fold_in(key, 0), shape, _jnp.float32)
    for i, name in enumerate(TWIN_WEIGHTS):
        w = inp[name].astype(_jnp.float32)
        if MOMENT_SCALE is None:
            s = _jnp.sqrt(_jnp.mean(_jnp.square(w)) + 1e-30)
        else:
            s = MOMENT_SCALE[name]
        km, kv = _jax.random.split(_jax.random.fold_in(key, i + 1))
        out[name] = w
        out["m_" + name] = s * _jax.random.normal(km, w.shape, _jnp.float32)
        out["v_" + name] = (s * s) * _jax.random.uniform(kv, w.shape, _jnp.float32, 0.5, 1.5)
    if N_MICROBATCH > 1:
        for name, axis in PER_EXAMPLE_BATCH_AXIS.items():
            out[name] = _to_microbatches(out[name], axis)
    return {'x': out['x'], 'c': out['c'], 'ada_w': out['ada_w'], 'ada_b': out['ada_b'], 'norm1_g': out['norm1_g'], 'w_in': out['w_in'], 'gm_ln_g': out['gm_ln_g'], 'gm_ln_b': out['gm_ln_b'], 'gm_ws': out['gm_ws'], 'gm_bs': out['gm_bs'], 'gm_norm_g': out['gm_norm_g'], 'attn_sinks': out['attn_sinks'], 'attn_norm_g': out['attn_norm_g'], 'conv_w': out['conv_w'], 'conv_b': out['conv_b'], 'dt_bias': out['dt_bias'], 'a_log': out['a_log'], 'd_skip': out['d_skip'], 'ssm_norm_g': out['ssm_norm_g'], 'w_out': out['w_out'], 'norm2_g': out['norm2_g'], 'w_mlp1': out['w_mlp1'], 'w_mlp2': out['w_mlp2'], 'final_norm_g': out['final_norm_g'], 'loss_target': out['loss_target'], 'm_ada_w': out['m_ada_w'], 'm_ada_b': out['m_ada_b'], 'm_norm1_g': out['m_norm1_g'], 'm_w_in': out['m_w_in'], 'm_gm_ln_g': out['m_gm_ln_g'], 'm_gm_ln_b': out['m_gm_ln_b'], 'm_gm_ws': out['m_gm_ws'], 'm_gm_bs': out['m_gm_bs'], 'm_gm_norm_g': out['m_gm_norm_g'], 'm_attn_sinks': out['m_attn_sinks'], 'm_attn_norm_g': out['m_attn_norm_g'], 'm_conv_w': out['m_conv_w'], 'm_conv_b': out['m_conv_b'], 'm_dt_bias': out['m_dt_bias'], 'm_a_log': out['m_a_log'], 'm_d_skip': out['m_d_skip'], 'm_ssm_norm_g': out['m_ssm_norm_g'], 'm_w_out': out['m_w_out'], 'm_norm2_g': out['m_norm2_g'], 'm_w_mlp1': out['m_w_mlp1'], 'm_w_mlp2': out['m_w_mlp2'], 'm_final_norm_g': out['m_final_norm_g'], 'v_ada_w': out['v_ada_w'], 'v_ada_b': out['v_ada_b'], 'v_norm1_g': out['v_norm1_g'], 'v_w_in': out['v_w_in'], 'v_gm_ln_g': out['v_gm_ln_g'], 'v_gm_ln_b': out['v_gm_ln_b'], 'v_gm_ws': out['v_gm_ws'], 'v_gm_bs': out['v_gm_bs'], 'v_gm_norm_g': out['v_gm_norm_g'], 'v_attn_sinks': out['v_attn_sinks'], 'v_attn_norm_g': out['v_attn_norm_g'], 'v_conv_w': out['v_conv_w'], 'v_conv_b': out['v_conv_b'], 'v_dt_bias': out['v_dt_bias'], 'v_a_log': out['v_a_log'], 'v_d_skip': out['v_d_skip'], 'v_ssm_norm_g': out['v_ssm_norm_g'], 'v_w_out': out['v_w_out'], 'v_norm2_g': out['v_norm2_g'], 'v_w_mlp1': out['v_w_mlp1'], 'v_w_mlp2': out['v_w_mlp2'], 'v_final_norm_g': out['v_final_norm_g']}


def _loss(weights, diff, rest, loss_target):
    with _jax.named_scope("forward"):
        args = {**rest, TWIN_DIFF_INPUT: diff, **{k: w.astype(_WEIGHT_DTYPES[k]) for k, w in weights.items()}}
        y = _forward(args)
    with _jax.named_scope("loss_head"):
        err = _jnp.square(y.astype(_jnp.float32) - loss_target)
        return 0.5 * _jnp.sum(_jnp.mean(err, axis=-1)) if err.ndim else 0.5 * err


def _adamw(w, g, m, v):
    m = ADAM_B1 * m + (1.0 - ADAM_B1) * g
    v = ADAM_B2 * v + (1.0 - ADAM_B2) * _jnp.square(g)
    m_hat = m / (1.0 - ADAM_B1 ** ADAM_STEP)
    v_hat = v / (1.0 - ADAM_B2 ** ADAM_STEP)
    delta = -ADAM_LR * (m_hat / (_jnp.sqrt(v_hat) + ADAM_EPS) + ADAM_WD * w)
    return delta, m, v


def reference(x, c, ada_w, ada_b, norm1_g, w_in, gm_ln_g, gm_ln_b, gm_ws, gm_bs, gm_norm_g, attn_sinks, attn_norm_g, conv_w, conv_b, dt_bias, a_log, d_skip, ssm_norm_g, w_out, norm2_g, w_mlp1, w_mlp2, final_norm_g, loss_target, m_ada_w, m_ada_b, m_norm1_g, m_w_in, m_gm_ln_g, m_gm_ln_b, m_gm_ws, m_gm_bs, m_gm_norm_g, m_attn_sinks, m_attn_norm_g, m_conv_w, m_conv_b, m_dt_bias, m_a_log, m_d_skip, m_ssm_norm_g, m_w_out, m_norm2_g, m_w_mlp1, m_w_mlp2, m_final_norm_g, v_ada_w, v_ada_b, v_norm1_g, v_w_in, v_gm_ln_g, v_gm_ln_b, v_gm_ws, v_gm_bs, v_gm_norm_g, v_attn_sinks, v_attn_norm_g, v_conv_w, v_conv_b, v_dt_bias, v_a_log, v_d_skip, v_ssm_norm_g, v_w_out, v_norm2_g, v_w_mlp1, v_w_mlp2, v_final_norm_g):
    given = dict(x=x, c=c, ada_w=ada_w, ada_b=ada_b, norm1_g=norm1_g, w_in=w_in, gm_ln_g=gm_ln_g, gm_ln_b=gm_ln_b, gm_ws=gm_ws, gm_bs=gm_bs, gm_norm_g=gm_norm_g, attn_sinks=attn_sinks, attn_norm_g=attn_norm_g, conv_w=conv_w, conv_b=conv_b, dt_bias=dt_bias, a_log=a_log, d_skip=d_skip, ssm_norm_g=ssm_norm_g, w_out=w_out, norm2_g=norm2_g, w_mlp1=w_mlp1, w_mlp2=w_mlp2, final_norm_g=final_norm_g, loss_target=loss_target, m_ada_w=m_ada_w, m_ada_b=m_ada_b, m_norm1_g=m_norm1_g, m_w_in=m_w_in, m_gm_ln_g=m_gm_ln_g, m_gm_ln_b=m_gm_ln_b, m_gm_ws=m_gm_ws, m_gm_bs=m_gm_bs, m_gm_norm_g=m_gm_norm_g, m_attn_sinks=m_attn_sinks, m_attn_norm_g=m_attn_norm_g, m_conv_w=m_conv_w, m_conv_b=m_conv_b, m_dt_bias=m_dt_bias, m_a_log=m_a_log, m_d_skip=m_d_skip, m_ssm_norm_g=m_ssm_norm_g, m_w_out=m_w_out, m_norm2_g=m_norm2_g, m_w_mlp1=m_w_mlp1, m_w_mlp2=m_w_mlp2, m_final_norm_g=m_final_norm_g, v_ada_w=v_ada_w, v_ada_b=v_ada_b, v_norm1_g=v_norm1_g, v_w_in=v_w_in, v_gm_ln_g=v_gm_ln_g, v_gm_ln_b=v_gm_ln_b, v_gm_ws=v_gm_ws, v_gm_bs=v_gm_bs, v_gm_norm_g=v_gm_norm_g, v_attn_sinks=v_attn_sinks, v_attn_norm_g=v_attn_norm_g, v_conv_w=v_conv_w, v_conv_b=v_conv_b, v_dt_bias=v_dt_bias, v_a_log=v_a_log, v_d_skip=v_d_skip, v_ssm_norm_g=v_ssm_norm_g, v_w_out=v_w_out, v_norm2_g=v_norm2_g, v_w_mlp1=v_w_mlp1, v_w_mlp2=v_w_mlp2, v_final_norm_g=v_final_norm_g)
    weights = {n: given[n] for n in TWIN_WEIGHTS}
    shared = {n: given[n] for n in SHARED_INPUTS}
    per_example = {n: given[n] for n in ['x', 'c']}
    grad_fn = _jax.value_and_grad(_loss, argnums=(0, 1))

    def one_microbatch(ex, loss_target):
        ex = dict(ex)
        diff = ex.pop(TWIN_DIFF_INPUT)
        return grad_fn(weights, diff, {**shared, **ex}, loss_target)

    if N_MICROBATCH == 1:
        loss, (grad_w, grad_x) = one_microbatch(per_example, given["loss_target"])
    else:
        def body(carry, xs):
            loss_sum, grad_sum = carry
            l_k, (gw_k, gx_k) = one_microbatch(xs[0], xs[1])
            with _jax.named_scope("update"):
                return (loss_sum + l_k, _jax.tree.map(_jnp.add, grad_sum, gw_k)), gx_k

        init = (_jnp.zeros((), _jnp.float32), _jax.tree.map(_jnp.zeros_like, weights))
        (loss, grad_w), grad_x = _jax.lax.scan(body, init, (per_example, given["loss_target"]))
    with _jax.named_scope("update"):
        delta_w, new_m, new_v = {}, {}, {}
        for n in TWIN_WEIGHTS:
            delta_w[n], new_m[n], new_v[n] = _adamw(weights[n], grad_w[n], given["m_" + n], given["v_" + n])
    return (loss, grad_x, *[grad_w[n] for n in TWIN_WEIGHTS], *[delta_w[n] for n in TWIN_WEIGHTS],
            *[new_m[n] for n in TWIN_WEIGHTS], *[new_v[n] for n in TWIN_WEIGHTS])
```

```python
import functools

import jax
import jax.numpy as jnp
import numpy as np
from jax import lax
from jax.experimental import pallas as pl
from jax.experimental.pallas import tpu as pltpu

F32, BF16 = jnp.float32, jnp.bfloat16
HI = lax.Precision.HIGHEST
MESH = pl.DeviceIdType.MESH
NDEV = 8

D = 2048
DEPTH = 2
CHUNK = 128
GM_W, GM_H = 512, 4
ATT_W, KV_W, ATT_H = 512, 128, 8
SSM_W, SSM_H, SSM_HD, SSM_G = 1024, 16, 64, 2
CONV_CH = 1536
IN_W = 4368
DFF = 8192
EPS = 1e-6
NEG_INF = -1e30
GELU_K = 0.7978845608028654
GELU_C = 0.044715

_ORIG = (("u", 512), ("v", 512), ("q", 512), ("k", 128), ("vv", 128), ("z", 1024), ("xbc", 1536), ("dt", 16))
OFF = dict(xbc=0, u=1536, z=2048, v=3072, q=3584, k=4096, vv=4224, dt=4352)
PW = 4608

ADAM_LR, ADAM_B1, ADAM_B2, ADAM_EPS, ADAM_WD, ADAM_STEP = 0.001, 0.9, 0.999, 1e-08, 0.01, 10


def _to_work_cols(w):
    parts, o = {}, 0
    for name, wd in _ORIG:
        parts[name] = w[..., o:o + wd]
        o += wd
    z = lambda n: jnp.zeros(w.shape[:-1] + (n,), w.dtype)
    return jnp.concatenate([parts["xbc"], parts["u"], parts["z"], parts["v"], parts["q"], parts["k"], parts["vv"],
                            parts["dt"], z(PW - OFF["dt"] - 16)], axis=-1)


def _from_work_cols(wp):
    return jnp.concatenate([wp[..., OFF[name]:OFF[name] + wd] for name, wd in _ORIG], axis=-1)


def _sigmoid(x):
    return 1.0 / (1.0 + jnp.exp(-x))


def _gelu(x):
    return 0.5 * x * (1.0 + jnp.tanh(GELU_K * (x + GELU_C * x * x * x)))


def _gelu_grad(x):
    t = jnp.tanh(GELU_K * (x + GELU_C * x * x * x))
    return 0.5 * (1.0 + t) + 0.5 * x * (1.0 - t * t) * GELU_K * (1.0 + 3.0 * GELU_C * x * x)


def _dot(a, b, prec=None):
    return jnp.dot(a, b, precision=prec, preferred_element_type=F32)


def _dot_nt(a, b, prec=None):
    return lax.dot_general(a, b, (((1,), (1,)), ((), ())), precision=prec, preferred_element_type=F32)


def _dot_tn(a, b, prec=None):
    return lax.dot_general(a, b, (((0,), (0,)), ((), ())), precision=prec, preferred_element_type=F32)


def _full(shape):
    return pl.BlockSpec(shape, lambda *_: (0,) * len(shape))


_HBM = pl.BlockSpec(memory_space=pltpu.HBM)


def _me():
    return lax.axis_index("x"), lax.axis_index("y"), lax.axis_index("c")


def _peer(k):
    x, y, c = _me()
    px = 1 - x if k & 4 else x
    py = 1 - y if k & 2 else y
    pc = 1 - c if k & 1 else c
    return (px, py, pc), 4 * px + 2 * py + pc


def _exchange(xs, name, scatter):
    n = len(xs)

    def body(*refs):
        ins, outs = refs[:n], refs[n:2 * n]
        send, recv, loc = refs[2 * n:]
        x, y, c = _me()
        me = 4 * x + 2 * y + c
        started = []
        for i in range(n):
            own = pltpu.make_async_copy(ins[i].at[me] if scatter else ins[i], outs[i].at[me], loc.at[i])
            own.start()
            started.append(own)
        for k in range(1, NDEV):
            dev, lin = _peer(k)
            for i in range(n):
                pltpu.make_async_remote_copy(
                    src_ref=ins[i].at[lin] if scatter else ins[i], dst_ref=outs[i].at[me],
                    send_sem=send.at[i, k - 1], recv_sem=recv.at[i, k - 1], device_id=dev, device_id_type=MESH).start()
        for k in range(1, NDEV):
            dev, lin = _peer(k)
            for i in range(n):
                pltpu.make_async_remote_copy(
                    src_ref=ins[i].at[lin] if scatter else ins[i], dst_ref=outs[i].at[lin],
                    send_sem=send.at[i, k - 1], recv_sem=recv.at[i, k - 1], device_id=dev, device_id_type=MESH).wait()
        for own in started:
            own.wait()

    out_shape = [jax.ShapeDtypeStruct(a.shape if scatter else (NDEV,) + a.shape, a.dtype) for a in xs]
    return pl.pallas_call(
        body, name=name, out_shape=out_shape, in_specs=[_HBM] * n, out_specs=[_HBM] * n,
        scratch_shapes=[pltpu.SemaphoreType.DMA((n, NDEV - 1)), pltpu.SemaphoreType.DMA((n, NDEV - 1)),
                        pltpu.SemaphoreType.DMA((n,))],
        compiler_params=pltpu.CompilerParams(has_side_effects=True),
    )(*xs)


def _mm(a, b, *, mode, name, out_dtypes=(F32,), epilogue=None, extras=(), tm=1024, tn=1024, tk=2048):
    if mode == "nn":
        (M, K), N = a.shape, b.shape[1]
    elif mode == "nt":
        (M, K), N = a.shape, b.shape[0]
    else:
        (K, M), N = a.shape, b.shape[1]
    tm, tn, tk = min(tm, M), min(tn, N), min(tk, K)
    assert M % tm == 0 and N % tn == 0 and K % tk == 0, (M, N, K, tm, tn, tk)
    nk = K // tk
    ne, no = len(extras), len(out_dtypes)
    dims = {"nn": (((1,), (0,)), ((), ())), "nt": (((1,), (1,)), ((), ())), "tn": (((0,), (0,)), ((), ()))}[mode]

    def body(a_ref, b_ref, *rest):
        ex, outs = rest[:ne], rest[ne:ne + no]

        def finish(acc):
            res = epilogue(acc, *[e[...] for e in ex]) if epilogue is not None else (acc,)
            for o, r in zip(outs, res):
                o[...] = r.astype(o.dtype)

        part = lax.dot_general(a_ref[...], b_ref[...], dims, preferred_element_type=F32)
        if nk == 1:
            finish(part)
        else:
            acc_ref = rest[-1]
            k = pl.program_id(2)

            @pl.when(k == 0)
            def _():
                acc_ref[...] = part

            @pl.when(k > 0)
            def _():
                acc_ref[...] += part

            @pl.when(k == nk - 1)
            def _():
                finish(acc_ref[...])

    a_spec = {"nn": pl.BlockSpec((tm, tk), lambda i, j, k: (i, k)), "nt": pl.BlockSpec((tm, tk), lambda i, j, k: (i, k)),
              "tn": pl.BlockSpec((tk, tm), lambda i, j, k: (k, i))}[mode]
    b_spec = {"nn": pl.BlockSpec((tk, tn), lambda i, j, k: (k, j)), "nt": pl.BlockSpec((tn, tk), lambda i, j, k: (j, k)),
              "tn": pl.BlockSpec((tk, tn), lambda i, j, k: (k, j))}[mode]
    o_spec = pl.BlockSpec((tm, tn), lambda i, j, k: (i, j))
    outs = pl.pallas_call(
        body, name=name, grid=(M // tm, N // tn, nk),
        in_specs=[a_spec, b_spec] + [o_spec] * ne, out_specs=[o_spec] * no,
        out_shape=[jax.ShapeDtypeStruct((M, N), dt) for dt in out_dtypes],
        scratch_shapes=[pltpu.VMEM((tm, tn), F32)] if nk > 1 else [],
        compiler_params=pltpu.CompilerParams(dimension_semantics=("parallel", "parallel", "arbitrary")),
    )(a, b, *extras)
    return outs if no > 1 else outs[0]


def _norm_fwd(x, g, sc, sh, name, resid=None):
    B, S, Dm = x.shape
    ts = min(S, 256)
    tok = pl.BlockSpec((None, ts, Dm), lambda b, i: (b, i, 0))
    row = pl.BlockSpec((None, 1, Dm), lambda b, i: (b, 0, 0))
    par = pl.BlockSpec((1, Dm), lambda b, i: (0, 0))

    def body(*refs):
        if resid is not None:
            x_ref, br_ref, gt_ref, g_ref, sc_ref, sh_ref, xo_ref, h_ref = refs
            xv = x_ref[...] + gt_ref[...] * br_ref[...]
            xo_ref[...] = xv
        else:
            x_ref, g_ref, sc_ref, sh_ref, h_ref = refs
            xv = x_ref[...]
        r = lax.rsqrt(jnp.mean(xv * xv, axis=-1, keepdims=True) + EPS)
        h_ref[...] = ((xv * r * g_ref[...]) * (1.0 + sc_ref[...]) + sh_ref[...]).astype(BF16)

    h_shape = jax.ShapeDtypeStruct((B, S, Dm), BF16)
    if resid is not None:
        return pl.pallas_call(body, name=name, grid=(B, S // ts), in_specs=[tok, tok, row, par, row, row],
                              out_specs=[tok, tok], out_shape=[jax.ShapeDtypeStruct((B, S, Dm), F32), h_shape],
                              )(x, resid[0], resid[1], g, sc, sh)
    return pl.pallas_call(body, name=name, grid=(B, S // ts), in_specs=[tok, par, row, row], out_specs=tok,
                          out_shape=h_shape)(x, g, sc, sh)


def _norm_bwd(x, g, name, *, sc=None, dh=None, dres=None, tgt=None, br=None, gate=None, x_is_prev=False):
    B, S, Dm = x.shape
    ts = min(S, 256)
    final = tgt is not None
    has_br = br is not None
    tok = pl.BlockSpec((None, ts, Dm), lambda b, i: (b, i, 0))
    row = pl.BlockSpec((None, 1, Dm), lambda b, i: (b, 0, 0))
    par = pl.BlockSpec((1, Dm), lambda b, i: (0, 0))
    ins, in_specs = [x, g], [tok, par]
    if final:
        ins, in_specs = ins + [tgt], in_specs + [tok]
    else:
        ins, in_specs = ins + [sc, dh], in_specs + [row, tok]
    if dres is not None:
        ins, in_specs = ins + [dres], in_specs + [tok]
    if has_br:
        ins, in_specs = ins + [br, gate], in_specs + [tok, row]
    n_in = len(ins)
    out_shape = [jax.ShapeDtypeStruct((B, S, Dm), F32), jax.ShapeDtypeStruct((1, Dm), F32)]
    out_specs = [tok, par]
    if final:
        out_shape.append(jax.ShapeDtypeStruct((1, 128), F32))
        out_specs.append(pl.BlockSpec((1, 128), lambda b, i: (0, 0)))
    else:
        out_shape += [jax.ShapeDtypeStruct((B, 1, Dm), F32)] * 2
        out_specs += [row, row]
    if has_br:
        out_shape += [jax.ShapeDtypeStruct((B, S, Dm), BF16), jax.ShapeDtypeStruct((B, 1, Dm), F32)]
        out_specs += [tok, row]

    def body(*refs):
        it = iter(refs[:n_in])
        outs = iter(refs[n_in:])
        x_ref, g_ref = next(it), next(it)
        b, i = pl.program_id(0), pl.program_id(1)
        first, first_row = (b == 0) & (i == 0), i == 0
        xv, gv = x_ref[...], g_ref[...]
        if x_is_prev:
            xv = xv + refs[n_in - 1][...] * refs[n_in - 2][...]
        r = lax.rsqrt(jnp.mean(xv * xv, axis=-1, keepdims=True) + EPS)
        n = xv * r
        dx_ref, dg_ref = next(outs), next(outs)

        def acc(ref, val, init):
            @pl.when(init)
            def _():
                ref[...] = val

            @pl.when(jnp.logical_not(init))
            def _():
                ref[...] += val

        if final:
            t_ref = next(it)
            loss_ref = next(outs)
            e = n * gv - t_ref[...]
            acc(loss_ref, jnp.zeros((1, 128), F32) + 0.5 * jnp.sum(e * e) / Dm, first)
            dyg = e * (1.0 / Dm)
        else:
            sc_ref, dh_ref = next(it), next(it)
            dsc_ref, dsh_ref = next(outs), next(outs)
            dhv = dh_ref[...]
            acc(dsh_ref, jnp.sum(dhv, axis=0, keepdims=True), first_row)
            acc(dsc_ref, jnp.sum(dhv * (n * gv), axis=0, keepdims=True), first_row)
            dyg = dhv * (1.0 + sc_ref[...])
        acc(dg_ref, jnp.sum(dyg * n, axis=0, keepdims=True), first)
        dn = dyg * gv
        dx = r * (dn - n * jnp.mean(dn * n, axis=-1, keepdims=True))
        if dres is not None:
            dx = dx + next(it)[...]
        dx_ref[...] = dx
        if has_br:
            br_ref, gt_ref = next(it), next(it)
            dbr_ref, dgt_ref = next(outs), next(outs)
            dbr_ref[...] = (dx * gt_ref[...]).astype(BF16)
            acc(dgt_ref, jnp.sum(dx * br_ref[...], axis=0, keepdims=True), first_row)

    outs = pl.pallas_call(body, name=name, grid=(B, S // ts), in_specs=in_specs, out_specs=out_specs, out_shape=out_shape,
                          compiler_params=pltpu.CompilerParams(dimension_semantics=("arbitrary", "arbitrary")))(*ins)
    res = dict(dx=outs[0], dg=outs[1])
    if final:
        res["loss"] = outs[2]
    else:
        res["dsc"], res["dsh"] = outs[2], outs[3]
    if has_br:
        res["dbr"], res["dgate"] = outs[-2], outs[-1]
    return res


def _gm_heads(vg, lng, lnb):
    res = []
    for h in range(GM_H):
        sl = slice(h * 128, (h + 1) * 128)
        vh = vg[:, sl]
        xc = vh - jnp.mean(vh, axis=-1, keepdims=True)
        rstd = lax.rsqrt(jnp.mean(xc * xc, axis=-1, keepdims=True) + 1e-5)
        xhat = xc * rstd
        res.append((xhat, rstd, xhat * lng[:, sl] + lnb[:, sl]))
    return res


def _gm_gate(heads, wt_ref, bsx, nch):
    cols = []
    for h in range(GM_H):
        vn = heads[h][2].astype(BF16)
        rows = [_dot(wt_ref[h], vn[c * CHUNK:(c + 1) * CHUNK]) + bsx[:, h * 128:(h + 1) * 128] for c in range(nch)]
        cols.append(jnp.concatenate(rows, axis=0) if nch > 1 else rows[0])
    return jnp.concatenate(cols, axis=1)


def _gm_specs(S):
    tb = min(S, 512)
    u = pl.BlockSpec((None, tb, GM_W), lambda b, i: (b, i, OFF["u"] // GM_W))
    v = pl.BlockSpec((None, tb, GM_W), lambda b, i: (b, i, OFF["v"] // GM_W))
    tok = pl.BlockSpec((None, tb, GM_W), lambda b, i: (b, i, 0))
    return tb, u, v, tok


def _gmlp_fwd(P, lng, lnb, wt, bsx, og, name):
    B, S, _ = P.shape
    tb, u_spec, v_spec, tok = _gm_specs(S)
    nch = tb // CHUNK

    def body(u_ref, v_ref, lng_ref, lnb_ref, wt_ref, bsx_ref, og_ref, o_ref):
        heads = _gm_heads(_gelu(v_ref[...]), lng_ref[...], lnb_ref[...])
        y = _gelu(u_ref[...]) * _gm_gate(heads, wt_ref, bsx_ref[...], nch)
        r = lax.rsqrt(jnp.mean(y * y, axis=-1, keepdims=True) + EPS)
        o_ref[...] = (y * r * og_ref[...]).astype(BF16)

    return pl.pallas_call(
        body, name=name, grid=(B, S // tb),
        in_specs=[u_spec, v_spec, _full((1, GM_W)), _full((1, GM_W)), _full((GM_H, 128, 128)), _full((128, GM_W)), _full((1, GM_W))],
        out_specs=tok, out_shape=jax.ShapeDtypeStruct((B, S, GM_W), BF16))(P, P, lng, lnb, wt, bsx, og)


def _gmlp_bwd(P, dcat, lng, lnb, wt, wtT, bsx, og, name):
    B, S, _ = P.shape
    tb, u_spec, v_spec, tok = _gm_specs(S)
    nch = tb // CHUNK
    do_spec = pl.BlockSpec((None, tb, GM_W), lambda b, i: (b, i, 0))

    def body(u_ref, v_ref, do_ref, lng_ref, lnb_ref, wt_ref, wtT_ref, bsx_ref, og_ref,
             du_ref, dv_ref, dlng_ref, dlnb_ref, dws_ref, dbsx_ref, dog_ref):
        first = (pl.program_id(0) == 0) & (pl.program_id(1) == 0)

        @pl.when(first)
        def _():
            for ref in (dlng_ref, dlnb_ref, dws_ref, dbsx_ref, dog_ref):
                ref[...] = jnp.zeros(ref.shape, F32)

        u, v, lng = u_ref[...], v_ref[...], lng_ref[...]
        ug = _gelu(u)
        heads = _gm_heads(_gelu(v), lng, lnb_ref[...])
        gate = _gm_gate(heads, wt_ref, bsx_ref[...], nch)
        y = ug * gate
        r = lax.rsqrt(jnp.mean(y * y, axis=-1, keepdims=True) + EPS)
        yn = y * r
        dout = do_ref[...]
        dog_ref[...] += jnp.sum(dout * yn, axis=0, keepdims=True)
        dyn = dout * og_ref[...]
        dy = r * (dyn - yn * jnp.mean(dyn * yn, axis=-1, keepdims=True))
        du_ref[...] = (dy * gate * _gelu_grad(u)).astype(BF16)
        dgate = dy * ug
        tril = lax.broadcasted_iota(jnp.int32, (128, 128), 0) >= lax.broadcasted_iota(jnp.int32, (128, 128), 1)
        dvg = []
        for h in range(GM_H):
            sl = slice(h * 128, (h + 1) * 128)
            xhat, rstd, vn = heads[h]
            vnb = vn.astype(BF16)
            dgh = dgate[:, sl]
            dgb = dgh.astype(BF16)
            dbs = jnp.zeros((128, 128), F32)
            dw = jnp.zeros((128, 128), F32)
            dvn = []
            for c in range(nch):
                rs = slice(c * CHUNK, (c + 1) * CHUNK)
                dbs = dbs + dgh[rs]
                dw = dw + _dot_nt(dgb[rs], vnb[rs])
                dvn.append(_dot(wtT_ref[h], dgb[rs]))
            dvn = jnp.concatenate(dvn, axis=0) if nch > 1 else dvn[0]
            dbsx_ref[:, sl] += dbs
            dws_ref[h] += jnp.where(tril, dw, 0.0)
            dlng_ref[:, sl] += jnp.sum(dvn * xhat, axis=0, keepdims=True)
            dlnb_ref[:, sl] += jnp.sum(dvn, axis=0, keepdims=True)
            dxh = dvn * lng[:, sl]
            dvg.append(rstd * (dxh - jnp.mean(dxh, axis=-1, keepdims=True) - xhat * jnp.mean(dxh * xhat, axis=-1, keepdims=True)))
        dv_ref[...] = (jnp.concatenate(dvg, axis=1) * _gelu_grad(v)).astype(BF16)

    p512, w3 = _full((1, GM_W)), _full((GM_H, 128, 128))
    return pl.pallas_call(
        body, name=name, grid=(B, S // tb),
        in_specs=[u_spec, v_spec, do_spec, p512, p512, w3, w3, _full((128, GM_W)), p512],
        out_specs=[tok, tok, p512, p512, w3, _full((128, GM_W)), p512],
        out_shape=[jax.ShapeDtypeStruct((B, S, GM_W), BF16)] * 2 + [
            jax.ShapeDtypeStruct((1, GM_W), F32), jax.ShapeDtypeStruct((1, GM_W), F32),
            jax.ShapeDtypeStruct((GM_H, 128, 128), F32), jax.ShapeDtypeStruct((128, GM_W), F32),
            jax.ShapeDtypeStruct((1, GM_W), F32)],
        compiler_params=pltpu.CompilerParams(dimension_semantics=("arbitrary", "arbitrary")),
    )(P, P, dcat, lng, lnb, wt, wtT, bsx, og)


def _lane_half():
    return lax.broadcasted_iota(jnp.int32, (128, 128), 1) // 64


def _att_stack(x, kvh, dtype):
    half = _lane_half()
    rows = []
    for g in range(4):
        i = kvh * 4 + g
        pair = x[:, (i // 2) * 128:(i // 2 + 1) * 128]
        if i % 2 != kvh:
            pair = pltpu.roll(pair, 64, 1)
        rows.append(jnp.where(half == kvh, pair, 0.0))
    return jnp.concatenate(rows, axis=0).astype(dtype)


def _att_unstack(pairs, y, kvh):
    half = _lane_half()
    for g in range(4):
        i = kvh * 4 + g
        piece = y[g * 128:(g + 1) * 128]
        if i % 2 != kvh:
            piece = pltpu.roll(piece, 64, 1)
        pairs[i // 2] = jnp.where(half == i % 2, piece, pairs[i // 2])
    return pairs


def _att_probs(qb, k2, st, sink_ref, kvh):
    qm = _att_stack(qb, kvh, BF16)
    s = _dot_nt(qm, k2) * (64 ** -0.5)
    qi = lax.broadcasted_iota(jnp.int32, (512, 256), 0) % 128
    kj = lax.broadcasted_iota(jnp.int32, (512, 256), 1)
    diff = qi + 128 - kj
    valid = (diff >= 0) & (diff < 128) & (st + kj - 128 >= 0)
    s = jnp.where(valid, s, NEG_INF)
    grp = lax.broadcasted_iota(jnp.int32, (512, 1), 0) // 128
    sink = jnp.zeros((512, 1), F32)
    for g in range(4):
        sink = jnp.where(grp == g, sink_ref[kvh * 4 + g], sink)
    m = jnp.maximum(jnp.max(s, axis=-1, keepdims=True), sink)
    e = jnp.exp(s - m)
    esink = jnp.exp(sink - m)
    inv = 1.0 / (jnp.sum(e, axis=-1, keepdims=True) + esink)
    return qm, e * inv, esink * inv


def _att_specs(S):
    q = pl.BlockSpec((None, S, ATT_W), lambda b: (b, 0, OFF["q"] // ATT_W))
    k = pl.BlockSpec((None, S, KV_W), lambda b: (b, 0, OFF["k"] // KV_W))
    v = pl.BlockSpec((None, S, KV_W), lambda b: (b, 0, OFF["vv"] // KV_W))
    tok = pl.BlockSpec((None, S, ATT_W), lambda b: (b, 0, 0))
    kv = pl.BlockSpec((None, S, KV_W), lambda b: (b, 0, 0))
    return q, k, v, tok, kv


_SMEM = pl.BlockSpec(memory_space=pltpu.SMEM)


def _attn_fwd(P, sinks, og, name):
    B, S, _ = P.shape
    q_spec, k_spec, v_spec, tok, _ = _att_specs(S)

    def body(q_ref, k_ref, v_ref, sink_ref, og_ref, o_ref, kpad, vpad):
        kpad[0:128, :] = jnp.zeros((128, KV_W), BF16)
        vpad[0:128, :] = jnp.zeros((128, KV_W), BF16)
        kpad[128:, :] = k_ref[...].astype(BF16)
        vpad[128:, :] = v_ref[...].astype(BF16)

        def step(n, carry):
            st = pl.multiple_of(n * 128, 128)
            qb = q_ref[pl.ds(st, 128), :]
            k2, v2 = kpad[pl.ds(st, 256), :], vpad[pl.ds(st, 256), :]
            pairs = [jnp.zeros((128, 128), F32)] * 4
            for kvh in range(2):
                _, p, _ = _att_probs(qb, k2, st, sink_ref, kvh)
                pairs = _att_unstack(pairs, _dot(p.astype(BF16), v2), kvh)
            o = jnp.concatenate(pairs, axis=1)
            r = lax.rsqrt(jnp.mean(o * o, axis=-1, keepdims=True) + EPS)
            o_ref[pl.ds(st, 128), :] = (o * r * og_ref[...]).astype(BF16)
            return carry

        lax.fori_loop(0, S // 128, step, 0)

    return pl.pallas_call(
        body, name=name, grid=(B,), in_specs=[q_spec, k_spec, v_spec, _SMEM, _full((1, ATT_W))], out_specs=tok,
        out_shape=jax.ShapeDtypeStruct((B, S, ATT_W), BF16),
        scratch_shapes=[pltpu.VMEM((S + 128, KV_W), BF16)] * 2)(P, P, P, sinks, og)


def _attn_bwd(P, dcat, sinks, og, name):
    B, S, _ = P.shape
    q_spec, k_spec, v_spec, tok, kv = _att_specs(S)
    do_spec = pl.BlockSpec((None, S, ATT_W), lambda b: (b, 0, GM_W // ATT_W))

    def body(q_ref, k_ref, v_ref, do_ref, sink_ref, og_ref, dq_ref, dk_ref, dv_ref, dsink_ref, dog_ref,
             kpad, vpad, dkpad, dvpad):
        @pl.when(pl.program_id(0) == 0)
        def _():
            dsink_ref[...] = jnp.zeros((8, 128), F32)
            dog_ref[...] = jnp.zeros((1, ATT_W), F32)

        kpad[0:128, :] = jnp.zeros((128, KV_W), BF16)
        vpad[0:128, :] = jnp.zeros((128, KV_W), BF16)
        kpad[128:, :] = k_ref[...].astype(BF16)
        vpad[128:, :] = v_ref[...].astype(BF16)
        dkpad[...] = jnp.zeros((S + 128, KV_W), F32)
        dvpad[...] = jnp.zeros((S + 128, KV_W), F32)
        half = _lane_half()
        head_row = lax.broadcasted_iota(jnp.int32, (8, 128), 0)

        def step(n, carry):
            st = pl.multiple_of(n * 128, 128)
            qb = q_ref[pl.ds(st, 128), :]
            k2, v2 = kpad[pl.ds(st, 256), :], vpad[pl.ds(st, 256), :]
            saved, pairs = [], [jnp.zeros((128, 128), F32)] * 4
            for kvh in range(2):
                qm, p, psink = _att_probs(qb, k2, st, sink_ref, kvh)
                o = _dot(p.astype(BF16), v2)
                saved.append((qm, p, psink, o))
                pairs = _att_unstack(pairs, o, kvh)
            o = jnp.concatenate(pairs, axis=1)
            r = lax.rsqrt(jnp.mean(o * o, axis=-1, keepdims=True) + EPS)
            on = o * r
            dout = do_ref[pl.ds(st, 128), :]
            dog_ref[...] += jnp.sum(dout * on, axis=0, keepdims=True)
            dyn = dout * og_ref[...]
            do = r * (dyn - on * jnp.mean(dyn * on, axis=-1, keepdims=True))
            dq_pairs = [jnp.zeros((128, 128), F32)] * 4
            dsink = jnp.zeros((8, 128), F32)
            for kvh in range(2):
                qm, p, psink, og_ = saved[kvh]
                dog = _att_stack(do, kvh, F32)
                delta = jnp.sum(dog * jnp.where(jnp.concatenate([half] * 4, axis=0) == kvh, og_, 0.0), axis=-1, keepdims=True)
                dogb, pb = dog.astype(BF16), p.astype(BF16)
                dvpad[pl.ds(st, 256), :] += _dot_tn(pb, dogb)
                dp = _dot_nt(dogb, v2)
                ds = (p * (dp - delta) * (64 ** -0.5)).astype(BF16)
                sd = psink * delta
                for g in range(4):
                    dsink = dsink - jnp.where(head_row == kvh * 4 + g, jnp.sum(sd[g * 128:(g + 1) * 128]), 0.0)
                dq_pairs = _att_unstack(dq_pairs, _dot(ds, k2), kvh)
                dkpad[pl.ds(st, 256), :] += _dot_tn(ds, qm)
            dsink_ref[...] += dsink
            dq_ref[pl.ds(st, 128), :] = jnp.concatenate(dq_pairs, axis=1).astype(BF16)
            return carry

        lax.fori_loop(0, S // 128, step, 0)
        dk_ref[...] = dkpad[128:, :].astype(BF16)
        dv_ref[...] = dvpad[128:, :].astype(BF16)

    return pl.pallas_call(
        body, name=name, grid=(B,),
        in_specs=[q_spec, k_spec, v_spec, do_spec, _SMEM, _full((1, ATT_W))],
        out_specs=[tok, kv, kv, _full((8, 128)), _full((1, ATT_W))],
        out_shape=[jax.ShapeDtypeStruct((B, S, ATT_W), BF16), jax.ShapeDtypeStruct((B, S, KV_W), BF16),
                   jax.ShapeDtypeStruct((B, S, KV_W), BF16), jax.ShapeDtypeStruct((8, 128), F32),
                   jax.ShapeDtypeStruct((1, ATT_W), F32)],
        scratch_shapes=[pltpu.VMEM((S + 128, KV_W), BF16)] * 2 + [pltpu.VMEM((S + 128, KV_W), F32)] * 2,
        compiler_params=pltpu.CompilerParams(dimension_semantics=("arbitrary",)),
    )(P, P, P, dcat, sinks, og)


CONV_TC = 256


def _conv_pre(ext, w_ref, b_ref, S):
    acc = b_ref[...] + w_ref[3:4, :] * ext[pl.ds(8, S), :]
    for k in range(1, 4):
        acc = acc + w_ref[3 - k:4 - k, :] * ext[pl.ds(8 - k, S), :]
    return acc


def _conv_fwd(P, w8, b, name):
    B, S, _ = P.shape
    nj = CONV_CH // CONV_TC
    x_spec = pl.BlockSpec((None, S, CONV_TC), lambda b_, j: (b_, 0, OFF["xbc"] // CONV_TC + j))
    tok = pl.BlockSpec((None, S, CONV_TC), lambda b_, j: (b_, 0, j))

    def body(x_ref, w_ref, b_ref, o_ref, ext):
        ext[0:8, :] = jnp.zeros((8, CONV_TC), F32)
        ext[8:, :] = x_ref[...]
        pre = _conv_pre(ext, w_ref, b_ref, S)
        o_ref[...] = pre * _sigmoid(pre)

    return pl.pallas_call(
        body, name=name, grid=(B, nj),
        in_specs=[x_spec, pl.BlockSpec((8, CONV_TC), lambda b_, j: (0, j)), pl.BlockSpec((1, CONV_TC), lambda b_, j: (0, j))],
        out_specs=tok, out_shape=jax.ShapeDtypeStruct((B, S, CONV_CH), F32),
        scratch_shapes=[pltpu.VMEM((S + 8, CONV_TC), F32)])(P, w8, b)


def _conv_bwd(P, dact, w8, b, name):
    B, S, _ = P.shape
    nj = CONV_CH // CONV_TC
    x_spec = pl.BlockSpec((None, S, CONV_TC), lambda j, b_: (b_, 0, OFF["xbc"] // CONV_TC + j))
    tok = pl.BlockSpec((None, S, CONV_TC), lambda j, b_: (b_, 0, j))
    w_spec = pl.BlockSpec((8, CONV_TC), lambda j, b_: (0, j))
    b_spec = pl.BlockSpec((1, CONV_TC), lambda j, b_: (0, j))

    def body(x_ref, d_ref, w_ref, b_ref, dx_ref, dw_ref, db_ref, ext, extd):
        @pl.when(pl.program_id(1) == 0)
        def _():
            dw_ref[...] = jnp.zeros((8, CONV_TC), F32)
            db_ref[...] = jnp.zeros((1, CONV_TC), F32)

        ext[0:8, :] = jnp.zeros((8, CONV_TC), F32)
        ext[8:, :] = x_ref[...]
        pre = _conv_pre(ext, w_ref, b_ref, S)
        sg = _sigmoid(pre)
        dpre = d_ref[...] * (sg * (1.0 + pre * (1.0 - sg)))
        extd[0:8, :] = jnp.zeros((8, CONV_TC), F32)
        extd[pl.ds(8, S), :] = dpre
        extd[pl.ds(8 + S, 8), :] = jnp.zeros((8, CONV_TC), F32)
        dx = w_ref[3:4, :] * dpre
        for k in range(1, 4):
            dx = dx + w_ref[3 - k:4 - k, :] * extd[pl.ds(8 + k, S), :]
        dx_ref[...] = dx.astype(BF16)
        db_ref[...] += jnp.sum(dpre, axis=0, keepdims=True)
        sub = lax.broadcasted_iota(jnp.int32, (8, CONV_TC), 0)
        dw = jnp.zeros((8, CONV_TC), F32)
        for i in range(4):
            dw = dw + jnp.where(sub == i, jnp.sum(dpre * ext[pl.ds(5 + i, S), :], axis=0, keepdims=True), 0.0)
        dw_ref[...] += dw

    return pl.pallas_call(
        body, name=name, grid=(nj, B), in_specs=[x_spec, tok, w_spec, b_spec], out_specs=[tok, w_spec, b_spec],
        out_shape=[jax.ShapeDtypeStruct((B, S, CONV_CH), BF16), jax.ShapeDtypeStruct((8, CONV_CH), F32),
                   jax.ShapeDtypeStruct((1, CONV_CH), F32)],
        scratch_shapes=[pltpu.VMEM((S + 8, CONV_TC), F32), pltpu.VMEM((S + 16, CONV_TC), F32)],
        compiler_params=pltpu.CompilerParams(dimension_semantics=("arbitrary", "arbitrary")),
    )(P, dact, w8, b)


def _ssd_consts():
    hd = np.arange(SSM_W) // SSM_HD
    E = (np.arange(128)[:, None] == hd[None, :]).astype(np.float32)
    tri = (np.arange(128)[:, None] >= np.arange(128)[None, :]).astype(np.float32)
    return jnp.asarray(E), jnp.asarray(E.T), jnp.asarray(tri), jnp.asarray(tri.T)


def _ssd_pre(xa, dtraw, bias, alog, E, ET, tri):
    lane = lax.broadcasted_iota(jnp.int32, (128, 128), 1)
    pre = dtraw + bias
    dtp = jnp.where(lane < SSM_H, jnp.maximum(pre, 0.0) + jnp.log(1.0 + jnp.exp(-jnp.abs(pre))), 0.0)
    a = -jnp.exp(alog)
    acs = _dot(tri, dtp * a, HI)
    acsT = acs.T
    dtE, acsE = _dot(dtp, E, HI), _dot(acs, E, HI)
    cdcol = jnp.exp(_dot(ET, acsT, HI)[:, 127:128])
    X = xa[:, :SSM_W]
    xdt = X * dtE
    wE = jnp.exp(acsE[127:128, :] - acsE)
    eE = jnp.exp(acsE)
    return dict(pre=pre, dtp=dtp, a=a, acs=acs, acsT=acsT, dtE=dtE, acsE=acsE, cdcol=cdcol, X=X, xdt=xdt, wE=wE, eE=eE)


def _ssd_decay(c, h):
    lm = lax.broadcasted_iota(jnp.int32, (128, 128), 0) >= lax.broadcasted_iota(jnp.int32, (128, 128), 1)
    return jnp.exp(jnp.where(lm, c["acs"][:, h:h + 1] - c["acsT"][h:h + 1, :], NEG_INF))


def _ssd_pair_operands(c, CB, h0):
    lane = lax.broadcasted_iota(jnp.int32, (128, 128), 1)
    L0, L1 = _ssd_decay(c, h0), _ssd_decay(c, h0 + 1)
    M = jnp.concatenate([CB * L0, CB * L1], axis=1).astype(BF16)
    xp = c["xdt"][:, h0 * 64:h0 * 64 + 128]
    BD = jnp.concatenate([jnp.where(lane < 64, xp, 0.0), jnp.where(lane >= 64, xp, 0.0)], axis=0).astype(BF16)
    return L0, L1, M, BD


def _ssd_y(c, xa, state_ref, dskipE):
    per_group, ys = [], []
    for g in range(SSM_G):
        gs = slice(g * 512, (g + 1) * 512)
        Bb = xa[:, SSM_W + g * 128:SSM_W + (g + 1) * 128].astype(BF16)
        Cb = xa[:, SSM_W + 256 + g * 128:SSM_W + 256 + (g + 1) * 128].astype(BF16)
        CB = _dot_nt(Cb, Bb)
        Sg = state_ref[gs, :]
        yoff = _dot_nt(Cb, Sg.astype(BF16)) * c["eE"][:, gs]
        ydiag, pairs = [], []
        for j in range(4):
            ops = _ssd_pair_operands(c, CB, g * 8 + 2 * j)
            pairs.append(ops)
            ydiag.append(_dot(ops[2], ops[3]))
        ys.append(jnp.concatenate(ydiag, axis=1) + yoff)
        per_group.append(dict(Bb=Bb, Cb=Cb, CB=CB, Sg=Sg, yoff=yoff, pairs=pairs))
    Y = jnp.concatenate(ys, axis=1) + c["X"] * dskipE
    return Y, per_group


def _ssd_specs(S, rev):
    nc = S // CHUNK
    cm = (lambda b, i: (b, nc - 1 - i)) if rev else (lambda b, i: (b, i))
    xa = pl.BlockSpec((None, CHUNK, CONV_CH), lambda b, i: cm(b, i) + (0,))
    z = pl.BlockSpec((None, CHUNK, SSM_W), lambda b, i: cm(b, i) + (OFF["z"] // SSM_W,))
    dt = pl.BlockSpec((None, CHUNK, 128), lambda b, i: cm(b, i) + (OFF["dt"] // 128,))
    tok = pl.BlockSpec((None, CHUNK, SSM_W), lambda b, i: cm(b, i) + (0,))
    st = pl.BlockSpec((None, None, SSM_W, 128), lambda b, i: cm(b, i) + (0, 0))
    return nc, xa, z, dt, tok, st


def _ssd_fwd(xact, P, bias, alog, dskipE, ng, name):
    B, S, _ = P.shape
    nc, xa_spec, z_spec, dt_spec, tok, st_spec = _ssd_specs(S, False)
    E, ET, tri, _ = _ssd_consts()

    def body(xa_ref, z_ref, dt_ref, bias_ref, alog_ref, dsk_ref, ng_ref, E_ref, ET_ref, tri_ref, o_ref, sp_ref, state):
        @pl.when(pl.program_id(1) == 0)
        def _():
            state[...] = jnp.zeros((SSM_W, 128), F32)

        sp_ref[...] = state[...]
        xa = xa_ref[...]
        c = _ssd_pre(xa, dt_ref[...], bias_ref[...], alog_ref[...], E_ref[...], ET_ref[...], tri_ref[...])
        Y, groups = _ssd_y(c, xa, state, dsk_ref[...])
        Z = (c["xdt"] * c["wE"]).astype(BF16)
        for g in range(SSM_G):
            gs = slice(g * 512, (g + 1) * 512)
            state[gs, :] = groups[g]["Sg"] * c["cdcol"][gs, :] + _dot_tn(Z[:, gs], groups[g]["Bb"])
        zv = z_ref[...]
        yz = Y * (zv * _sigmoid(zv))
        outs = []
        for g in range(SSM_G):
            yg = yz[:, g * 512:(g + 1) * 512]
            outs.append(yg * lax.rsqrt(jnp.mean(yg * yg, axis=-1, keepdims=True) + EPS))
        o_ref[...] = (jnp.concatenate(outs, axis=1) * ng_ref[...]).astype(BF16)

    return pl.pallas_call(
        body, name=name, grid=(B, nc),
        in_specs=[xa_spec, z_spec, dt_spec, _full((1, 128)), _full((1, 128)), _full((1, SSM_W)), _full((1, SSM_W)),
                  _full((128, SSM_W)), _full((SSM_W, 128)), _full((128, 128))],
        out_specs=[tok, st_spec],
        out_shape=[jax.ShapeDtypeStruct((B, S, SSM_W), BF16), jax.ShapeDtypeStruct((B, nc, SSM_W, 128), F32)],
        scratch_shapes=[pltpu.VMEM((SSM_W, 128), F32)],
        compiler_params=pltpu.CompilerParams(dimension_semantics=("arbitrary", "arbitrary")),
    )(xact, P, P, bias, alog, dskipE, ng, E, ET, tri)


def _ssd_bwd(xact, P, sprev, dcat, bias, alog, dskipE, ng, name):
    B, S, _ = P.shape
    nc, xa_spec, z_spec, dt_spec, tok, st_spec = _ssd_specs(S, True)
    do_spec = pl.BlockSpec((None, CHUNK, SSM_W), lambda b, i: (b, nc - 1 - i, 1))
    E, ET, tri, triT = _ssd_consts()
    dt_out = pl.BlockSpec((None, CHUNK, 128), lambda b, i: (b, nc - 1 - i, 0))

    def body(xa_ref, z_ref, dt_ref, sp_ref, do_ref, bias_ref, alog_ref, dsk_ref, ng_ref, E_ref, ET_ref, tri_ref, triT_ref,
             dxa_ref, dz_ref, ddt_ref, dbias_ref, dalog_ref, ddsk_ref, dng_ref, dstate):
        first = (pl.program_id(0) == 0) & (pl.program_id(1) == 0)

        @pl.when(first)
        def _():
            for ref in (dbias_ref, dalog_ref, ddsk_ref, dng_ref):
                ref[...] = jnp.zeros(ref.shape, F32)

        @pl.when(pl.program_id(1) == 0)
        def _():
            dstate[...] = jnp.zeros((SSM_W, 128), F32)

        xa, ETm = xa_ref[...], ET_ref[...]
        c = _ssd_pre(xa, dt_ref[...], bias_ref[...], alog_ref[...], E_ref[...], ETm, tri_ref[...])
        Y, groups = _ssd_y(c, xa, sp_ref, dsk_ref[...])
        X, xdt = c["X"], c["xdt"]
        zv = z_ref[...]
        sg = _sigmoid(zv)
        zs = zv * sg
        yz = Y * zs
        dout = do_ref[...]
        dyz = []
        for g in range(SSM_G):
            gs = slice(g * 512, (g + 1) * 512)
            yg = yz[:, gs]
            r = lax.rsqrt(jnp.mean(yg * yg, axis=-1, keepdims=True) + EPS)
            yn = yg * r
            dng_ref[:, gs] += jnp.sum(dout[:, gs] * yn, axis=0, keepdims=True)
            dyn = dout[:, gs] * ng_ref[:, gs]
            dyz.append(r * (dyn - yn * jnp.mean(dyn * yn, axis=-1, keepdims=True)))
        dyz = jnp.concatenate(dyz, axis=1)
        dz_ref[...] = (dyz * Y * (sg * (1.0 + zv * (1.0 - sg)))).astype(BF16)
        dY = dyz * zs
        ddsk_ref[...] += jnp.sum(dY * X, axis=0, keepdims=True)
        dX = dY * dsk_ref[...]
        lane = lax.broadcasted_iota(jnp.int32, (128, 128), 1)
        sub = lax.broadcasted_iota(jnp.int32, (128, 128), 0)
        colform = jnp.zeros((128, 128), F32)
        rowform = jnp.zeros((128, 128), F32)
        dxdt, gacsE, dBC = [], [], []
        for g in range(SSM_G):
            gs = slice(g * 512, (g + 1) * 512)
            G = groups[g]
            Bb, Cb, CB, Sg = G["Bb"], G["Cb"], G["CB"], G["Sg"]
            dYg = dY[:, gs]
            dQ = (dYg * c["eE"][:, gs]).astype(BF16)
            dSn = dstate[gs, :]
            dSnb = dSn.astype(BF16)
            cd = c["cdcol"][gs, :]
            dC = _dot(dQ, Sg.astype(BF16))
            dSprev = _dot_tn(dQ, Cb) + dSn * cd
            hcol = jnp.sum(_dot(E_ref[:, gs], dSn * Sg * cd, HI), axis=-1, keepdims=True)
            rowform = rowform + jnp.where(lane == 127, hcol, 0.0)
            Zg = xdt[:, gs] * c["wE"][:, gs]
            dZ = _dot_nt(Bb, dSnb)
            dB = _dot(Zg.astype(BF16), dSnb)
            U = dZ * Zg
            ga = dYg * G["yoff"] - U
            ga = ga + jnp.where(lax.broadcasted_iota(jnp.int32, (128, 512), 0) == 127, jnp.sum(U, axis=0, keepdims=True), 0.0)
            gacsE.append(ga)
            dxg = [None] * 4
            dCB = jnp.zeros((128, 128), F32)
            for j in range(4):
                h0 = g * 8 + 2 * j
                L0, L1, M, BD = G["pairs"][j]
                dYp = dYg[:, j * 128:(j + 1) * 128].astype(BF16)
                dM = _dot_nt(dYp, BD)
                dBD = _dot_tn(M, dYp)
                dxg[j] = jnp.where(lane < 64, dBD[:128], dBD[128:])
                for t, (h, L) in enumerate(((h0, L0), (h0 + 1, L1))):
                    dMh = dM[:, t * 128:(t + 1) * 128]
                    dCB = dCB + dMh * L
                    Gh = dMh * CB * L
                    colform = colform + jnp.where(lane == h, jnp.sum(Gh, axis=1, keepdims=True), 0.0)
                    rowform = rowform - jnp.where(sub == h, jnp.sum(Gh, axis=0, keepdims=True), 0.0)
            dCBb = dCB.astype(BF16)
            dC = dC + _dot(dCBb, Bb)
            dB = dB + _dot_tn(dCBb, Cb)
            dxdt.append(jnp.concatenate(dxg, axis=1) + dZ * c["wE"][:, gs])
            dBC.append((dB, dC))
            dstate[gs, :] = dSprev
        dxdt = jnp.concatenate(dxdt, axis=1)
        dX = dX + dxdt * c["dtE"]
        ddt = _dot(dxdt * X, ETm, HI)
        dacs = colform + rowform.T + _dot(jnp.concatenate(gacsE, axis=1), ETm, HI)
        dda = _dot(triT_ref[...], dacs, HI)
        ddt = ddt + dda * c["a"]
        dalog_ref[...] += jnp.sum(dda * c["dtp"], axis=0, keepdims=True) * c["a"]
        ddtraw = jnp.where(lane < SSM_H, ddt * _sigmoid(c["pre"]), 0.0)
        dbias_ref[...] += jnp.sum(ddtraw, axis=0, keepdims=True)
        ddt_ref[...] = ddtraw.astype(BF16)
        dxa_ref[...] = jnp.concatenate([dX, dBC[0][0], dBC[1][0], dBC[0][1], dBC[1][1]], axis=1)

    p128, p1k = _full((1, 128)), _full((1, SSM_W))
    return pl.pallas_call(
        body, name=name, grid=(B, nc),
        in_specs=[xa_spec, z_spec, dt_spec, st_spec, do_spec, p128, p128, p1k, p1k,
                  _full((128, SSM_W)), _full((SSM_W, 128)), _full((128, 128)), _full((128, 128))],
        out_specs=[xa_spec, tok, dt_out, p128, p128, p1k, p1k],
        out_shape=[jax.ShapeDtypeStruct((B, S, CONV_CH), F32), jax.ShapeDtypeStruct((B, S, SSM_W), BF16),
                   jax.ShapeDtypeStruct((B, S, 128), BF16), jax.ShapeDtypeStruct((1, 128), F32),
                   jax.ShapeDtypeStruct((1, 128), F32), jax.ShapeDtypeStruct((1, SSM_W), F32),
                   jax.ShapeDtypeStruct((1, SSM_W), F32)],
        scratch_shapes=[pltpu.VMEM((SSM_W, 128), F32)],
        compiler_params=pltpu.CompilerParams(dimension_semantics=("arbitrary", "arbitrary")),
    )(xact, P, P, sprev, dcat, bias, alog, dskipE, ng, E, ET, tri, triT)


def _adamw(w, parts, m, v, name, tr=512):
    R, C = w.shape
    ns = parts.shape[0]
    tr = min(tr, R)
    assert R % tr == 0
    c1 = 1.0 / (1.0 - ADAM_B1 ** ADAM_STEP)
    c2 = 1.0 / (1.0 - ADAM_B2 ** ADAM_STEP)

    def body(w_ref, p_ref, m_ref, v_ref, g_ref, d_ref, mo_ref, vo_ref):
        g = p_ref[0].astype(F32)
        for s in range(1, ns):
            g = g + p_ref[s].astype(F32)
        mn = ADAM_B1 * m_ref[...] + (1.0 - ADAM_B1) * g
        vn = ADAM_B2 * v_ref[...] + (1.0 - ADAM_B2) * (g * g)
        g_ref[...] = g
        mo_ref[...] = mn
        vo_ref[...] = vn
        d_ref[...] = -ADAM_LR * ((mn * c1) / (jnp.sqrt(vn * c2) + ADAM_EPS) + ADAM_WD * w_ref[...])

    blk = pl.BlockSpec((tr, C), lambda i: (i, 0))
    return pl.pallas_call(
        body, name=name, grid=(R // tr,),
        in_specs=[blk, pl.BlockSpec((ns, tr, C), lambda i: (0, i, 0)), blk, blk], out_specs=[blk] * 4,
        out_shape=[jax.ShapeDtypeStruct((R, C), F32)] * 4)(w, parts, m, v)


_SMALL = ("ada_b", "norm1_g", "gm_ln_g", "gm_ln_b", "gm_ws", "gm_bs", "gm_norm_g", "attn_sinks", "attn_norm_g", "conv_b",
          "dt_bias", "a_log", "d_skip", "ssm_norm_g", "norm2_g", "final_norm_g")


def _pack(arrs):
    flat = []
    for a in arrs:
        f = a.reshape(-1).astype(F32)
        flat.append(jnp.pad(f, (0, (-f.shape[0]) % 1024)))
    return jnp.concatenate(flat).reshape(-1, 128)


def _unpack(pack, like):
    out, o = [], 0
    flat = pack.reshape(-1)
    for a in like:
        n = int(np.prod(a.shape))
        out.append(flat[o:o + n].reshape(a.shape))
        o += n + (-n) % 1024
    return out


def kernel(x, c, ada_w, ada_b, norm1_g, w_in, gm_ln_g, gm_ln_b, gm_ws, gm_bs, gm_norm_g, attn_sinks, attn_norm_g, conv_w, conv_b, dt_bias, a_log, d_skip, ssm_norm_g, w_out, norm2_g, w_mlp1, w_mlp2, final_norm_g, loss_target, m_ada_w, m_ada_b, m_norm1_g, m_w_in, m_gm_ln_g, m_gm_ln_b, m_gm_ws, m_gm_bs, m_gm_norm_g, m_attn_sinks, m_attn_norm_g, m_conv_w, m_conv_b, m_dt_bias, m_a_log, m_d_skip, m_ssm_norm_g, m_w_out, m_norm2_g, m_w_mlp1, m_w_mlp2, m_final_norm_g, v_ada_w, v_ada_b, v_norm1_g, v_w_in, v_gm_ln_g, v_gm_ln_b, v_gm_ws, v_gm_bs, v_gm_norm_g, v_attn_sinks, v_attn_norm_g, v_conv_w, v_conv_b, v_dt_bias, v_a_log, v_d_skip, v_ssm_norm_g, v_w_out, v_norm2_g, v_w_mlp1, v_w_mlp2, v_final_norm_g):
    args = dict(locals())
    B, S, _ = x.shape
    T = B * S
    L = DEPTH
    me = 4 * lax.axis_index("x") + 2 * lax.axis_index("y") + lax.axis_index("c")

    gath = _exchange([c, conv_w, w_in.astype(BF16), w_out.astype(BF16), w_mlp1.astype(BF16), w_mlp2.astype(BF16)],
                     "ag_weights", False)
    c_all = gath[0].reshape(NDEV * B, D)
    c_act = (c_all * jax.nn.sigmoid(c_all)).astype(BF16)
    nb_rows = c_act.shape[0]
    c_pad = jnp.pad(c_act, ((0, 128 - nb_rows), (0, 0)))
    adw = ada_w.astype(BF16)
    mod_part = jnp.stack([_mm(c_pad, adw[l], mode="nn", name=f"mod{l}", tn=768)[:nb_rows] for l in range(L)])
    mod_all = _exchange([mod_part], "ag_mod", False)[0]
    mod_mine = lax.dynamic_slice_in_dim(mod_all, me * B, B, axis=2)
    mod = jnp.transpose(mod_mine, (1, 2, 0, 3)).reshape(L, B, 6 * D) + ada_b[:, None, :]
    mods = [[mod[l][:, None, i * D:(i + 1) * D] for i in range(6)] for l in range(L)]

    win_g = jnp.transpose(gath[2], (1, 2, 0, 3)).reshape(L, D, IN_W)
    win_g = _to_work_cols(win_g)
    wout_g = jnp.transpose(gath[3], (1, 0, 2, 3)).reshape(L, D, D)
    w1_g = jnp.transpose(gath[4], (1, 2, 0, 3)).reshape(L, D, DFF)
    w2_g = jnp.transpose(gath[5], (1, 0, 2, 3)).reshape(L, DFF, D)

    tril = jnp.tril(jnp.ones((128, 128), F32))
    row = lambda a: a.reshape(1, -1)
    pad128 = lambda a: jnp.pad(a.reshape(1, -1), ((0, 0), (0, 128 - a.shape[-1])))
    small = []
    for l in range(L):
        wt = gm_ws[l] * tril
        small.append(dict(
            lng=row(gm_ln_g[l]), lnb=row(gm_ln_b[l]), wt=wt.astype(BF16), wtT=jnp.swapaxes(wt, 1, 2).astype(BF16),
            bsx=jnp.repeat(gm_bs[l].T, 128, axis=1), gog=row(gm_norm_g[l]), sinks=attn_sinks[l], aog=row(attn_norm_g[l]),
            bias=pad128(dt_bias[l]), alog=pad128(a_log[l]), dskE=jnp.repeat(d_skip[l], SSM_HD).reshape(1, SSM_W),
            sng=row(ssm_norm_g[l]), cb=row(conv_b[l])))
    convw_all = jnp.transpose(gath[1], (1, 2, 0, 3)).reshape(L, 4, CONV_CH)
    convw8 = jnp.pad(convw_all, ((0, 0), (0, 4), (0, 0)))

    saved = []
    xl = x
    h = _norm_fwd(xl, row(norm1_g[0]), mods[0][1], mods[0][0], "norm1_f0")
    for l in range(L):
        sm = small[l]
        P = _mm(h.reshape(T, D), win_g[l], mode="nn", name=f"proj_in{l}", tn=1536).reshape(B, S, PW)
        out_a = _gmlp_fwd(P, sm["lng"], sm["lnb"], sm["wt"], sm["bsx"], sm["gog"], f"gmlp_f{l}")
        out_b = _attn_fwd(P, sm["sinks"], sm["aog"], f"attn_f{l}")
        xact = _conv_fwd(P, convw8[l], sm["cb"], f"conv_f{l}")
        out_c, sprev = _ssd_fwd(xact, P, sm["bias"], sm["alog"], sm["dskE"], sm["sng"], f"ssd_f{l}")
        cat = jnp.concatenate([out_a, out_b, out_c], axis=-1)
        mix = _mm(cat.reshape(T, D), wout_g[l], mode="nn", name=f"proj_out{l}").reshape(B, S, D)
        x_mid, h2 = _norm_fwd(xl, row(norm2_g[l]), mods[l][4], mods[l][3], f"norm2_f{l}", resid=(mix, mods[l][2]))
        a_act, r_act = _mm(h2.reshape(T, D), w1_g[l], mode="nn", name=f"mlp1_{l}", out_dtypes=(BF16, BF16),
                           epilogue=lambda acc: (acc, jnp.square(jnp.maximum(acc, 0.0))))
        m2 = _mm(r_act, w2_g[l], mode="nn", name=f"mlp2_{l}").reshape(B, S, D)
        saved.append(dict(x_in=xl, h=h, P=P, xact=xact, sprev=sprev, cat=cat, mix=mix, x_mid=x_mid, h2=h2, a=a_act, r=r_act, m2=m2))
        if l + 1 < L:
            xl, h = _norm_fwd(x_mid, row(norm1_g[l + 1]), mods[l + 1][1], mods[l + 1][0], f"norm1_f{l + 1}", resid=(m2, mods[l][5]))

    sv = saved[L - 1]
    nb = _norm_bwd(sv["x_mid"], row(final_norm_g), "final_b", tgt=loss_target, br=sv["m2"], gate=mods[L - 1][5], x_is_prev=True)
    loss_part, g_final = nb["loss"], nb["dg"]
    dmod, gsm, gconvw = [None] * L, [None] * L, [None] * L
    gW = dict(w_in=[None] * L, w_out=[None] * L, w_mlp1=[None] * L, w_mlp2=[None] * L)
    for l in reversed(range(L)):
        sv, sm = saved[l], small[l]
        dm2, dxo, dg2 = nb["dbr"].reshape(T, D), nb["dx"], nb["dgate"]
        da = _mm(dm2, w2_g[l], mode="nt", name=f"mlp2_dx{l}", out_dtypes=(BF16,), extras=(sv["a"],),
                 epilogue=lambda acc, a: (acc * (2.0 * jnp.maximum(a.astype(F32), 0.0)),))
        gW["w_mlp2"][l] = _mm(sv["r"], dm2, mode="tn", name=f"mlp2_dw{l}", out_dtypes=(BF16,), tk=1024).reshape(NDEV, DFF // NDEV, D)
        dh2 = _mm(da, w1_g[l], mode="nt", name=f"mlp1_dx{l}").reshape(B, S, D)
        dw1 = _mm(sv["h2"].reshape(T, D), da, mode="tn", name=f"mlp1_dw{l}", out_dtypes=(BF16,), tk=1024)
        gW["w_mlp1"][l] = jnp.transpose(dw1.reshape(D, NDEV, DFF // NDEV), (1, 0, 2))
        nb2 = _norm_bwd(sv["x_mid"], row(norm2_g[l]), f"norm2_b{l}", sc=mods[l][4], dh=dh2, dres=dxo, br=sv["mix"], gate=mods[l][2])
        dmix = nb2["dbr"].reshape(T, D)
        dcat = _mm(dmix, wout_g[l], mode="nt", name=f"proj_out_dx{l}").reshape(B, S, D)
        gW["w_out"][l] = _mm(sv["cat"].reshape(T, D), dmix, mode="tn", name=f"proj_out_dw{l}", out_dtypes=(BF16,),
                             tk=1024).reshape(NDEV, D // NDEV, D)
        du, dv, dlng, dlnb, dws, dbsx, dgog = _gmlp_bwd(sv["P"], dcat, sm["lng"], sm["lnb"], sm["wt"], sm["wtT"], sm["bsx"],
                                                        sm["gog"], f"gmlp_b{l}")
        dq, dk, dvv, dsink, daog = _attn_bwd(sv["P"], dcat, sm["sinks"], sm["aog"], f"attn_b{l}")
        dxa, dz, ddt, dbias, dalog, ddsk, dsng = _ssd_bwd(sv["xact"], sv["P"], sv["sprev"], dcat, sm["bias"], sm["alog"],
                                                          sm["dskE"], sm["sng"], f"ssd_b{l}")
        dxbc, dcw, dcb = _conv_bwd(sv["P"], dxa, convw8[l], sm["cb"], f"conv_b{l}")
        dP = jnp.concatenate([dxbc, du, dz, dv, dq, dk, dvv, ddt, jnp.zeros((B, S, PW - OFF["dt"] - 128), BF16)],
                             axis=-1).reshape(T, PW)
        dh = _mm(dP, win_g[l], mode="nt", name=f"proj_in_dx{l}", tk=1536).reshape(B, S, D)
        dwin = _mm(sv["h"].reshape(T, D), dP, mode="tn", name=f"proj_in_dw{l}", out_dtypes=(BF16,), tn=1536, tk=1024)
        gW["w_in"][l] = jnp.transpose(_from_work_cols(dwin).reshape(D, NDEV, IN_W // NDEV), (1, 0, 2))
        nb = _norm_bwd(sv["x_in"], row(norm1_g[l]), f"norm1_b{l}", sc=mods[l][1], dh=dh, dres=nb2["dx"],
                       br=saved[l - 1]["m2"] if l > 0 else None, gate=mods[l - 1][5] if l > 0 else None)
        dmod[l] = jnp.concatenate([nb["dsh"], nb["dsc"], nb2["dgate"], nb2["dsh"], nb2["dsc"], dg2], axis=-1)
        gconvw[l] = dcw[:4]
        gsm[l] = dict(
            ada_b=jnp.sum(dmod[l], axis=(0, 1)), norm1_g=nb["dg"], gm_ln_g=dlng, gm_ln_b=dlnb, gm_ws=dws,
            gm_bs=dbsx.reshape(128, GM_H, 128).sum(-1).T, gm_norm_g=dgog, attn_sinks=dsink[:, 0], attn_norm_g=daog,
            conv_b=dcb, dt_bias=dbias[0, :SSM_H], a_log=dalog[0, :SSM_H], d_skip=ddsk.reshape(SSM_H, SSM_HD).sum(-1),
            ssm_norm_g=dsng, norm2_g=nb2["dg"])
    grad_x = nb["dx"]

    per_layer = [n for n in _SMALL if n != "final_norm_g"]
    g_small = [jnp.stack([gsm[l][n].reshape(args[n].shape[1:]) for l in range(L)]) for n in per_layer] + [g_final.reshape(D)]
    zc = jnp.zeros((L, 4, CONV_CH), F32)
    z1 = jnp.zeros((1, 128), F32)
    gpack = _pack([loss_part] + g_small + [jnp.stack(gconvw)])
    got = _exchange([jnp.stack(dmod).reshape(L, B, 6 * D), gpack], "ag_small", False)
    like = [z1] + [args[n] for n in _SMALL] + [zc]
    packs = [_pack([z1] + [args[p + n] for n in _SMALL] + [zc]) for p in ("", "m_", "v_")]
    sres = [_unpack(p, like) for p in _adamw(packs[0], got[1], packs[1], packs[2], "adamw_small", tr=gpack.shape[0])]
    res = {n: [r[1 + i] for r in sres] for i, n in enumerate(_SMALL)}
    loss = sres[0][0][0, 0]
    gcw = lax.dynamic_slice_in_dim(sres[0][-1], me * (CONV_CH // NDEV), CONV_CH // NDEV, axis=2)

    def update(name, parts, tr):
        w = args[name]
        r = _adamw(w.reshape(-1, w.shape[-1]), parts, args["m_" + name].reshape(-1, w.shape[-1]),
                   args["v_" + name].reshape(-1, w.shape[-1]), "adamw_" + name, tr=tr)
        res[name] = [a.reshape(w.shape) for a in r]

    update("conv_w", gcw.reshape(1, L * 4, CONV_CH // NDEV), L * 4)

    dmod_all = jnp.transpose(got[0], (1, 0, 2, 3)).reshape(L, NDEV * B, 6 * D)
    dm_mine = lax.dynamic_slice_in_dim(dmod_all, me * (6 * D // NDEV), 6 * D // NDEV, axis=2)
    dm_pad = jnp.pad(dm_mine, ((0, 0), (0, 128 - nb_rows), (0, 0))).astype(BF16)
    g_adaw = jnp.stack([_mm(c_pad, dm_pad[l], mode="tn", name=f"ada_dw{l}", tn=768) for l in range(L)])
    update("ada_w", g_adaw.reshape(1, L * D, 6 * D // NDEV), 256)

    big = ("w_in", "w_out", "w_mlp1", "w_mlp2")
    recv = _exchange([jnp.stack(gW[n], axis=1) for n in big], "rs_grads", True)
    for n, r_, tr in zip(big, recv, (256, 256, 256, 128)):
        update(n, r_.reshape(NDEV, -1, r_.shape[-1]), tr)

    names = ['ada_w', 'ada_b', 'norm1_g', 'w_in', 'gm_ln_g', 'gm_ln_b', 'gm_ws', 'gm_bs', 'gm_norm_g', 'attn_sinks',
             'attn_norm_g', 'conv_w', 'conv_b', 'dt_bias', 'a_log', 'd_skip', 'ssm_norm_g', 'w_out', 'norm2_g', 'w_mlp1',
             'w_mlp2', 'final_norm_g']
    return (loss, grad_x, *[res[n][0] for n in names], *[res[n][1] for n in names], *[res[n][2] for n in names],
            *[res[n][3] for n in names])
```

```python
import functools

import jax
import jax.numpy as jnp
import numpy as np
from jax import lax
from jax.experimental import pallas as pl
from jax.experimental.pallas import tpu as pltpu

F32, BF16 = jnp.float32, jnp.bfloat16
HI = lax.Precision.HIGHEST
MESH = pl.DeviceIdType.MESH
NDEV = 8

D = 2048
DEPTH = 2
CHUNK = 128
GM_W, GM_H = 512, 4
ATT_W, KV_W, ATT_H = 512, 128, 8
SSM_W, SSM_H, SSM_HD, SSM_G = 1024, 16, 64, 2
CONV_CH = 1536
IN_W = 4368
DFF = 8192
EPS = 1e-6
NEG_INF = -1e30
GELU_K = 0.7978845608028654
GELU_C = 0.044715

_ORIG = (("u", 512), ("v", 512), ("q", 512), ("k", 128), ("vv", 128), ("z", 1024), ("xbc", 1536), ("dt", 16))
OFF = dict(xbc=0, u=1536, z=2048, v=3072, q=3584, k=4096, vv=4224, dt=4352)
PW = 4608

ADAM_LR, ADAM_B1, ADAM_B2, ADAM_EPS, ADAM_WD, ADAM_STEP = 0.001, 0.9, 0.999, 1e-08, 0.01, 10


def _to_work_cols(w):
    parts, o = {}, 0
    for name, wd in _ORIG:
        parts[name] = w[..., o:o + wd]
        o += wd
    z = lambda n: jnp.zeros(w.shape[:-1] + (n,), w.dtype)
    return jnp.concatenate([parts["xbc"], parts["u"], parts["z"], parts["v"], parts["q"], parts["k"], parts["vv"],
                            parts["dt"], z(PW - OFF["dt"] - 16)], axis=-1)


def _from_work_cols(wp):
    return jnp.concatenate([wp[..., OFF[name]:OFF[name] + wd] for name, wd in _ORIG], axis=-1)


def _sigmoid(x):
    return 1.0 / (1.0 + jnp.exp(-x))


def _gelu(x):
    return 0.5 * x * (1.0 + jnp.tanh(GELU_K * (x + GELU_C * x * x * x)))


def _gelu_grad(x):
    t = jnp.tanh(GELU_K * (x + GELU_C * x * x * x))
    return 0.5 * (1.0 + t) + 0.5 * x * (1.0 - t * t) * GELU_K * (1.0 + 3.0 * GELU_C * x * x)


def _dot(a, b, prec=None):
    return jnp.dot(a, b, precision=prec, preferred_element_type=F32)


def _dot_nt(a, b, prec=None):
    return lax.dot_general(a, b, (((1,), (1,)), ((), ())), precision=prec, preferred_element_type=F32)


def _dot_tn(a, b, prec=None):
    return lax.dot_general(a, b, (((0,), (0,)), ((), ())), precision=prec, preferred_element_type=F32)


def _full(shape):
    return pl.BlockSpec(shape, lambda *_: (0,) * len(shape))


_HBM = pl.BlockSpec(memory_space=pltpu.HBM)


def _me():
    return lax.axis_index("x"), lax.axis_index("y"), lax.axis_index("c")


def _peer(k):
    x, y, c = _me()
    px = 1 - x if k & 4 else x
    py = 1 - y if k & 2 else y
    pc = 1 - c if k & 1 else c
    return (px, py, pc), 4 * px + 2 * py + pc


def _exchange(xs, name, scatter):
    n = len(xs)

    def body(*refs):
        ins, outs = refs[:n], refs[n:2 * n]
        send, recv, loc = refs[2 * n:]
        x, y, c = _me()
        me = 4 * x + 2 * y + c
        started = []
        for i in range(n):
            own = pltpu.make_async_copy(ins[i].at[me] if scatter else ins[i], outs[i].at[me], loc.at[i])
            own.start()
            started.append(own)
        for k in range(1, NDEV):
            dev, lin = _peer(k)
            for i in range(n):
                pltpu.make_async_remote_copy(
                    src_ref=ins[i].at[lin] if scatter else ins[i], dst_ref=outs[i].at[me],
                    send_sem=send.at[i, k - 1], recv_sem=recv.at[i, k - 1], device_id=dev, device_id_type=MESH).start()
        for k in range(1, NDEV):
            dev, lin = _peer(k)
            for i in range(n):
                pltpu.make_async_remote_copy(
                    src_ref=ins[i].at[lin] if scatter else ins[i], dst_ref=outs[i].at[lin],
                    send_sem=send.at[i, k - 1], recv_sem=recv.at[i, k - 1], device_id=dev, device_id_type=MESH).wait()
        for own in started:
            own.wait()

    out_shape = [jax.ShapeDtypeStruct(a.shape if scatter else (NDEV,) + a.shape, a.dtype) for a in xs]
    return pl.pallas_call(
        body, name=name, out_shape=out_shape, in_specs=[_HBM] * n, out_specs=[_HBM] * n,
        scratch_shapes=[pltpu.SemaphoreType.DMA((n, NDEV - 1)), pltpu.SemaphoreType.DMA((n, NDEV - 1)),
                        pltpu.SemaphoreType.DMA((n,))],
        compiler_params=pltpu.CompilerParams(has_side_effects=True),
    )(*xs)


def _chips():
    x, y, c = _me()
    return x, y, c, [(1 - x, y), (x, 1 - y), (1 - x, 1 - y)]


def _gather2(xs, name):
    n = len(xs)

    def body(*refs):
        ins, outs = refs[:n], refs[n:2 * n]
        send, recv, loc = refs[2 * n:]
        x, y, c, chips = _chips()
        me, sib = (x, y, c), (x, y, 1 - c)

        def cp(i, k, block, to, src=None):
            slot = outs[i].at[4 * block[0] + 2 * block[1] + block[2]]
            return pltpu.make_async_remote_copy(src_ref=slot if src is None else src, dst_ref=slot, send_sem=send.at[i, k],
                                                recv_sem=recv.at[i, k], device_id=to, device_id_type=MESH)

        sent = []
        for i in range(n):
            for j, chip in enumerate(chips):
                sent.append(cp(i, 1 + j, me, (*chip, c), src=ins[i]))
            sent.append(cp(i, 0, me, sib, src=ins[i]))
        for s in sent:
            s.start()
        own = [pltpu.make_async_copy(ins[i], outs[i].at[4 * x + 2 * y + c], loc.at[i]) for i in range(n)]
        for o in own:
            o.start()
        for j, chip in enumerate(chips):
            for i in range(n):
                cp(i, 1 + j, (*chip, c), me).wait_recv()
                fwd = cp(i, 4 + j, (*chip, c), sib)
                fwd.start()
                sent.append(fwd)
        for i in range(n):
            cp(i, 0, sib, me).wait_recv()
            for j, chip in enumerate(chips):
                cp(i, 4 + j, (*chip, 1 - c), me).wait_recv()
        for s in sent:
            s.wait_send()
        for o in own:
            o.wait()

    return pl.pallas_call(
        body, name=name, out_shape=[jax.ShapeDtypeStruct((NDEV,) + a.shape, a.dtype) for a in xs],
        in_specs=[_HBM] * n, out_specs=[_HBM] * n,
        scratch_shapes=[pltpu.SemaphoreType.DMA((n, 7)), pltpu.SemaphoreType.DMA((n, 7)), pltpu.SemaphoreType.DMA((n,))],
        compiler_params=pltpu.CompilerParams(has_side_effects=True),
    )(*xs)


def _pair_exchange(ps, name):
    n = len(ps)

    def body(*refs):
        ins, outs = refs[:n], refs[n:2 * n]
        send, recv = refs[2 * n:]
        x, y, c = _me()
        cps = [pltpu.make_async_remote_copy(src_ref=ins[i].at[ch, 1 - c], dst_ref=outs[i].at[ch], send_sem=send.at[i, ch],
                                            recv_sem=recv.at[i, ch], device_id=(x, y, 1 - c), device_id_type=MESH)
               for i in range(n) for ch in range(4)]
        for cp in cps:
            cp.start()
        for cp in cps:
            cp.wait()

    return pl.pallas_call(
        body, name=name, out_shape=[jax.ShapeDtypeStruct((4,) + a.shape[2:], a.dtype) for a in ps],
        in_specs=[_HBM] * n, out_specs=[_HBM] * n,
        scratch_shapes=[pltpu.SemaphoreType.DMA((n, 4)), pltpu.SemaphoreType.DMA((n, 4))],
        compiler_params=pltpu.CompilerParams(has_side_effects=True),
    )(*ps)


def _pair_add(p, r1, core, name, tr=256):
    _, _, R, C = p.shape
    tr = min(tr, R)

    def body(core_ref, p_ref, r_ref, o_ref):
        o_ref[...] = (p_ref[...].astype(F32) + r_ref[...].astype(F32)).astype(o_ref.dtype)

    return pl.pallas_call(
        body, name=name, out_shape=jax.ShapeDtypeStruct((4, R, C), p.dtype),
        grid_spec=pltpu.PrefetchScalarGridSpec(
            num_scalar_prefetch=1, grid=(4, R // tr),
            in_specs=[pl.BlockSpec((None, None, tr, C), lambda ch, i, core_ref: (ch, core_ref[0], i, 0)),
                      pl.BlockSpec((None, tr, C), lambda ch, i, core_ref: (ch, i, 0))],
            out_specs=pl.BlockSpec((None, tr, C), lambda ch, i, core_ref: (ch, i, 0))),
    )(core, p, r1)


def _chip_exchange(ss, name):
    n = len(ss)

    def body(*refs):
        ins, outs = refs[:n], refs[n:2 * n]
        send, recv, loc = refs[2 * n:]
        x, y, c, chips = _chips()
        mine = 2 * x + y
        own = [pltpu.make_async_copy(ins[i].at[mine], outs[i].at[mine], loc.at[i]) for i in range(n)]
        cps = []
        for j, (px, py) in enumerate(chips):
            for i in range(n):
                cps.append(pltpu.make_async_remote_copy(
                    src_ref=ins[i].at[2 * px + py], dst_ref=outs[i].at[mine], send_sem=send.at[i, j], recv_sem=recv.at[i, j],
                    device_id=(px, py, c), device_id_type=MESH))
        for cp in cps + own:
            cp.start()
        for j, (px, py) in enumerate(chips):
            for i in range(n):
                pltpu.make_async_remote_copy(
                    src_ref=ins[i].at[2 * px + py], dst_ref=outs[i].at[2 * px + py], send_sem=send.at[i, j],
                    recv_sem=recv.at[i, j], device_id=(px, py, c), device_id_type=MESH).wait()
        for o in own:
            o.wait()

    return pl.pallas_call(
        body, name=name, out_shape=[jax.ShapeDtypeStruct(a.shape, a.dtype) for a in ss],
        in_specs=[_HBM] * n, out_specs=[_HBM] * n,
        scratch_shapes=[pltpu.SemaphoreType.DMA((n, 3)), pltpu.SemaphoreType.DMA((n, 3)), pltpu.SemaphoreType.DMA((n,))],
        compiler_params=pltpu.CompilerParams(has_side_effects=True),
    )(*ss)


def _mm(a, b, *, mode, name, out_dtypes=(F32,), epilogue=None, extras=(), tm=1024, tn=1024, tk=2048,
        col_blocked_b=False, col_blocked_out=False):
    CB = 1024
    if col_blocked_b:
        assert mode in ("nn", "nt") and b.shape[2] == CB
        (M, K), N = a.shape, (b.shape[0] * CB if mode == "nn" else b.shape[1])
        tn, tk = (CB, tk) if mode == "nn" else (tn, CB)
    elif mode == "nn":
        (M, K), N = a.shape, b.shape[1]
    elif mode == "nt":
        (M, K), N = a.shape, b.shape[0]
    else:
        (K, M), N = a.shape, b.shape[1]
    if col_blocked_out:
        assert len(out_dtypes) == 1 and N % CB == 0
        tn = CB
    tm, tn, tk = min(tm, M), min(tn, N), min(tk, K)
    assert M % tm == 0 and N % tn == 0 and K % tk == 0, (M, N, K, tm, tn, tk)
    nk = K // tk
    ne, no = len(extras), len(out_dtypes)
    dims = {"nn": (((1,), (0,)), ((), ())), "nt": (((1,), (1,)), ((), ())), "tn": (((0,), (0,)), ((), ()))}[mode]

    def body(a_ref, b_ref, *rest):
        ex, outs = rest[:ne], rest[ne:ne + no]

        def finish(acc):
            res = epilogue(acc, *[e[...] for e in ex]) if epilogue is not None else (acc,)
            for o, r in zip(outs, res):
                o[...] = r.astype(o.dtype)

        part = lax.dot_general(a_ref[...], b_ref[...], dims, preferred_element_type=F32)
        if nk == 1:
            finish(part)
        else:
            acc_ref = rest[-1]
            k = pl.program_id(2)

            @pl.when(k == 0)
            def _():
                acc_ref[...] = part

            @pl.when(k > 0)
            def _():
                acc_ref[...] += part

            @pl.when(k == nk - 1)
            def _():
                finish(acc_ref[...])

    a_spec = {"nn": pl.BlockSpec((tm, tk), lambda i, j, k: (i, k)), "nt": pl.BlockSpec((tm, tk), lambda i, j, k: (i, k)),
              "tn": pl.BlockSpec((tk, tm), lambda i, j, k: (k, i))}[mode]
    b_spec = {"nn": pl.BlockSpec((tk, tn), lambda i, j, k: (k, j)), "nt": pl.BlockSpec((tn, tk), lambda i, j, k: (j, k)),
              "tn": pl.BlockSpec((tk, tn), lambda i, j, k: (k, j))}[mode]
    if col_blocked_b:
        b_spec = (pl.BlockSpec((None, tk, CB), lambda i, j, k: (j, k, 0)) if mode == "nn"
                  else pl.BlockSpec((None, tn, CB), lambda i, j, k: (k, j, 0)))
    e_spec = pl.BlockSpec((tm, tn), lambda i, j, k: (i, j))
    o_spec, o_dims = e_spec, (M, N)
    if col_blocked_out:
        o_spec, o_dims = pl.BlockSpec((None, tm, CB), lambda i, j, k: (j, i, 0)), (N // CB, M, CB)
    outs = pl.pallas_call(
        body, name=name, grid=(M // tm, N // tn, nk),
        in_specs=[a_spec, b_spec] + [e_spec] * ne, out_specs=[o_spec] * no,
        out_shape=[jax.ShapeDtypeStruct(o_dims, dt) for dt in out_dtypes],
        scratch_shapes=[pltpu.VMEM((tm, tn), F32)] if nk > 1 else [],
        compiler_params=pltpu.CompilerParams(dimension_semantics=("parallel", "parallel", "arbitrary")),
    )(a, b, *extras)
    return outs if no > 1 else outs[0]


def _norm_fwd(x, g, sc, sh, name, resid=None):
    B, S, Dm = x.shape
    ts = min(S, 256)
    tok = pl.BlockSpec((None, ts, Dm), lambda b, i: (b, i, 0))
    row = pl.BlockSpec((None, 1, Dm), lambda b, i: (b, 0, 0))
    par = pl.BlockSpec((1, Dm), lambda b, i: (0, 0))

    def body(*refs):
        if resid is not None:
            x_ref, br_ref, gt_ref, g_ref, sc_ref, sh_ref, xo_ref, h_ref = refs
            xv = x_ref[...] + gt_ref[...] * br_ref[...]
            xo_ref[...] = xv
        else:
            x_ref, g_ref, sc_ref, sh_ref, h_ref = refs
            xv = x_ref[...]
        r = lax.rsqrt(jnp.mean(xv * xv, axis=-1, keepdims=True) + EPS)
        h_ref[...] = ((xv * r * g_ref[...]) * (1.0 + sc_ref[...]) + sh_ref[...]).astype(BF16)

    h_shape = jax.ShapeDtypeStruct((B, S, Dm), BF16)
    if resid is not None:
        return pl.pallas_call(body, name=name, grid=(B, S // ts), in_specs=[tok, tok, row, par, row, row],
                              out_specs=[tok, tok], out_shape=[jax.ShapeDtypeStruct((B, S, Dm), F32), h_shape],
                              )(x, resid[0], resid[1], g, sc, sh)
    return pl.pallas_call(body, name=name, grid=(B, S // ts), in_specs=[tok, par, row, row], out_specs=tok,
                          out_shape=h_shape)(x, g, sc, sh)


def _norm_bwd(x, g, name, *, sc=None, dh=None, dres=None, tgt=None, br=None, gate=None, x_is_prev=False):
    B, S, Dm = x.shape
    ts = min(S, 256)
    final = tgt is not None
    has_br = br is not None
    tok = pl.BlockSpec((None, ts, Dm), lambda b, i: (b, i, 0))
    row = pl.BlockSpec((None, 1, Dm), lambda b, i: (b, 0, 0))
    par = pl.BlockSpec((1, Dm), lambda b, i: (0, 0))
    ins, in_specs = [x, g], [tok, par]
    if final:
        ins, in_specs = ins + [tgt], in_specs + [tok]
    else:
        ins, in_specs = ins + [sc, dh], in_specs + [row, tok]
    if dres is not None:
        ins, in_specs = ins + [dres], in_specs + [tok]
    if has_br:
        ins, in_specs = ins + [br, gate], in_specs + [tok, row]
    n_in = len(ins)
    out_shape = [jax.ShapeDtypeStruct((B, S, Dm), F32), jax.ShapeDtypeStruct((1, Dm), F32)]
    out_specs = [tok, par]
    if final:
        out_shape.append(jax.ShapeDtypeStruct((1, 128), F32))
        out_specs.append(pl.BlockSpec((1, 128), lambda b, i: (0, 0)))
    else:
        out_shape += [jax.ShapeDtypeStruct((B, 1, Dm), F32)] * 2
        out_specs += [row, row]
    if has_br:
        out_shape += [jax.ShapeDtypeStruct((B, S, Dm), BF16), jax.ShapeDtypeStruct((B, 1, Dm), F32)]
        out_specs += [tok, row]

    def body(*refs):
        it = iter(refs[:n_in])
        outs = iter(refs[n_in:])
        x_ref, g_ref = next(it), next(it)
        b, i = pl.program_id(0), pl.program_id(1)
        first, first_row = (b == 0) & (i == 0), i == 0
        xv, gv = x_ref[...], g_ref[...]
        if x_is_prev:
            xv = xv + refs[n_in - 1][...] * refs[n_in - 2][...]
        r = lax.rsqrt(jnp.mean(xv * xv, axis=-1, keepdims=True) + EPS)
        n = xv * r
        dx_ref, dg_ref = next(outs), next(outs)

        def acc(ref, val, init):
            @pl.when(init)
            def _():
                ref[...] = val

            @pl.when(jnp.logical_not(init))
            def _():
                ref[...] += val

        if final:
            t_ref = next(it)
            loss_ref = next(outs)
            e = n * gv - t_ref[...]
            acc(loss_ref, jnp.zeros((1, 128), F32) + 0.5 * jnp.sum(e * e) / Dm, first)
            dyg = e * (1.0 / Dm)
        else:
            sc_ref, dh_ref = next(it), next(it)
            dsc_ref, dsh_ref = next(outs), next(outs)
            dhv = dh_ref[...]
            acc(dsh_ref, jnp.sum(dhv, axis=0, keepdims=True), first_row)
            acc(dsc_ref, jnp.sum(dhv * (n * gv), axis=0, keepdims=True), first_row)
            dyg = dhv * (1.0 + sc_ref[...])
        acc(dg_ref, jnp.sum(dyg * n, axis=0, keepdims=True), first)
        dn = dyg * gv
        dx = r * (dn - n * jnp.mean(dn * n, axis=-1, keepdims=True))
        if dres is not None:
            dx = dx + next(it)[...]
        dx_ref[...] = dx
        if has_br:
            br_ref, gt_ref = next(it), next(it)
            dbr_ref, dgt_ref = next(outs), next(outs)
            dbr_ref[...] = (dx * gt_ref[...]).astype(BF16)
            acc(dgt_ref, jnp.sum(dx * br_ref[...], axis=0, keepdims=True), first_row)

    outs = pl.pallas_call(body, name=name, grid=(B, S // ts), in_specs=in_specs, out_specs=out_specs, out_shape=out_shape,
                          compiler_params=pltpu.CompilerParams(dimension_semantics=("arbitrary", "arbitrary")))(*ins)
    res = dict(dx=outs[0], dg=outs[1])
    if final:
        res["loss"] = outs[2]
    else:
        res["dsc"], res["dsh"] = outs[2], outs[3]
    if has_br:
        res["dbr"], res["dgate"] = outs[-2], outs[-1]
    return res


def _gm_heads(vg, lng, lnb):
    res = []
    for h in range(GM_H):
        sl = slice(h * 128, (h + 1) * 128)
        vh = vg[:, sl]
        xc = vh - jnp.mean(vh, axis=-1, keepdims=True)
        rstd = lax.rsqrt(jnp.mean(xc * xc, axis=-1, keepdims=True) + 1e-5)
        xhat = xc * rstd
        res.append((xhat, rstd, xhat * lng[:, sl] + lnb[:, sl]))
    return res


def _gm_gate(heads, wt_ref, bsx, nch):
    cols = []
    for h in range(GM_H):
        vn = heads[h][2].astype(BF16)
        rows = [_dot(wt_ref[h], vn[c * CHUNK:(c + 1) * CHUNK]) + bsx[:, h * 128:(h + 1) * 128] for c in range(nch)]
        cols.append(jnp.concatenate(rows, axis=0) if nch > 1 else rows[0])
    return jnp.concatenate(cols, axis=1)


def _gm_specs(S):
    tb = min(S, 512)
    u = pl.BlockSpec((None, tb, GM_W), lambda b, i: (b, i, OFF["u"] // GM_W))
    v = pl.BlockSpec((None, tb, GM_W), lambda b, i: (b, i, OFF["v"] // GM_W))
    tok = pl.BlockSpec((None, tb, GM_W), lambda b, i: (b, i, 0))
    return tb, u, v, tok


def _gmlp_fwd(P, lng, lnb, wt, bsx, og, name):
    B, S, _ = P.shape
    tb, u_spec, v_spec, tok = _gm_specs(S)
    nch = tb // CHUNK

    def body(u_ref, v_ref, lng_ref, lnb_ref, wt_ref, bsx_ref, og_ref, o_ref):
        heads = _gm_heads(_gelu(v_ref[...]), lng_ref[...], lnb_ref[...])
        y = _gelu(u_ref[...]) * _gm_gate(heads, wt_ref, bsx_ref[...], nch)
        r = lax.rsqrt(jnp.mean(y * y, axis=-1, keepdims=True) + EPS)
        o_ref[...] = (y * r * og_ref[...]).astype(BF16)

    return pl.pallas_call(
        body, name=name, grid=(B, S // tb),
        in_specs=[u_spec, v_spec, _full((1, GM_W)), _full((1, GM_W)), _full((GM_H, 128, 128)), _full((128, GM_W)), _full((1, GM_W))],
        out_specs=tok, out_shape=jax.ShapeDtypeStruct((B, S, GM_W), BF16))(P, P, lng, lnb, wt, bsx, og)


def _gmlp_bwd(P, dcat, lng, lnb, wt, wtT, bsx, og, name):
    B, S, _ = P.shape
    tb, u_spec, v_spec, tok = _gm_specs(S)
    nch = tb // CHUNK
    do_spec = pl.BlockSpec((None, tb, GM_W), lambda b, i: (b, i, 0))

    def body(u_ref, v_ref, do_ref, lng_ref, lnb_ref, wt_ref, wtT_ref, bsx_ref, og_ref,
             du_ref, dv_ref, dlng_ref, dlnb_ref, dws_ref, dbsx_ref, dog_ref):
        first = (pl.program_id(0) == 0) & (pl.program_id(1) == 0)

        @pl.when(first)
        def _():
            for ref in (dlng_ref, dlnb_ref, dws_ref, dbsx_ref, dog_ref):
                ref[...] = jnp.zeros(ref.shape, F32)

        u, v, lng = u_ref[...], v_ref[...], lng_ref[...]
        ug = _gelu(u)
        heads = _gm_heads(_gelu(v), lng, lnb_ref[...])
        gate = _gm_gate(heads, wt_ref, bsx_ref[...], nch)
        y = ug * gate
        r = lax.rsqrt(jnp.mean(y * y, axis=-1, keepdims=True) + EPS)
        yn = y * r
        dout = do_ref[...]
        dog_ref[...] += jnp.sum(dout * yn, axis=0, keepdims=True)
        dyn = dout * og_ref[...]
        dy = r * (dyn - yn * jnp.mean(dyn * yn, axis=-1, keepdims=True))
        du_ref[...] = (dy * gate * _gelu_grad(u)).astype(BF16)
        dgate = dy * ug
        tril = lax.broadcasted_iota(jnp.int32, (128, 128), 0) >= lax.broadcasted_iota(jnp.int32, (128, 128), 1)
        dvg = []
        for h in range(GM_H):
            sl = slice(h * 128, (h + 1) * 128)
            xhat, rstd, vn = heads[h]
            vnb = vn.astype(BF16)
            dgh = dgate[:, sl]
            dgb = dgh.astype(BF16)
            dbs = jnp.zeros((128, 128), F32)
            dw = jnp.zeros((128, 128), F32)
            dvn = []
            for c in range(nch):
                rs = slice(c * CHUNK, (c + 1) * CHUNK)
                dbs = dbs + dgh[rs]
                dw = dw + _dot_nt(dgb[rs], vnb[rs])
                dvn.append(_dot(wtT_ref[h], dgb[rs]))
            dvn = jnp.concatenate(dvn, axis=0) if nch > 1 else dvn[0]
            dbsx_ref[:, sl] += dbs
            dws_ref[h] += jnp.where(tril, dw, 0.0)
            dlng_ref[:, sl] += jnp.sum(dvn * xhat, axis=0, keepdims=True)
            dlnb_ref[:, sl] += jnp.sum(dvn, axis=0, keepdims=True)
            dxh = dvn * lng[:, sl]
            dvg.append(rstd * (dxh - jnp.mean(dxh, axis=-1, keepdims=True) - xhat * jnp.mean(dxh * xhat, axis=-1, keepdims=True)))
        dv_ref[...] = (jnp.concatenate(dvg, axis=1) * _gelu_grad(v)).astype(BF16)

    p512, w3 = _full((1, GM_W)), _full((GM_H, 128, 128))
    return pl.pallas_call(
        body, name=name, grid=(B, S // tb),
        in_specs=[u_spec, v_spec, do_spec, p512, p512, w3, w3, _full((128, GM_W)), p512],
        out_specs=[tok, tok, p512, p512, w3, _full((128, GM_W)), p512],
        out_shape=[jax.ShapeDtypeStruct((B, S, GM_W), BF16)] * 2 + [
            jax.ShapeDtypeStruct((1, GM_W), F32), jax.ShapeDtypeStruct((1, GM_W), F32),
            jax.ShapeDtypeStruct((GM_H, 128, 128), F32), jax.ShapeDtypeStruct((128, GM_W), F32),
            jax.ShapeDtypeStruct((1, GM_W), F32)],
        compiler_params=pltpu.CompilerParams(dimension_semantics=("arbitrary", "arbitrary")),
    )(P, P, dcat, lng, lnb, wt, wtT, bsx, og)


def _lane_half():
    return lax.broadcasted_iota(jnp.int32, (128, 128), 1) // 64


def _att_stack(x, kvh, dtype):
    half = _lane_half()
    rows = []
    for g in range(4):
        i = kvh * 4 + g
        pair = x[:, (i // 2) * 128:(i // 2 + 1) * 128]
        if i % 2 != kvh:
            pair = pltpu.roll(pair, 64, 1)
        rows.append(jnp.where(half == kvh, pair, 0.0))
    return jnp.concatenate(rows, axis=0).astype(dtype)


def _att_unstack(pairs, y, kvh):
    half = _lane_half()
    for g in range(4):
        i = kvh * 4 + g
        piece = y[g * 128:(g + 1) * 128]
        if i % 2 != kvh:
            piece = pltpu.roll(piece, 64, 1)
        pairs[i // 2] = jnp.where(half == i % 2, piece, pairs[i // 2])
    return pairs


def _att_probs(qb, k2, st, sink_ref, kvh):
    qm = _att_stack(qb, kvh, BF16)
    s = _dot_nt(qm, k2) * (64 ** -0.5)
    qi = lax.broadcasted_iota(jnp.int32, (512, 256), 0) % 128
    kj = lax.broadcasted_iota(jnp.int32, (512, 256), 1)
    diff = qi + 128 - kj
    valid = (diff >= 0) & (diff < 128) & (st + kj - 128 >= 0)
    s = jnp.where(valid, s, NEG_INF)
    grp = lax.broadcasted_iota(jnp.int32, (512, 1), 0) // 128
    sink = jnp.zeros((512, 1), F32)
    for g in range(4):
        sink = jnp.where(grp == g, sink_ref[kvh * 4 + g], sink)
    m = jnp.maximum(jnp.max(s, axis=-1, keepdims=True), sink)
    e = jnp.exp(s - m)
    esink = jnp.exp(sink - m)
    inv = 1.0 / (jnp.sum(e, axis=-1, keepdims=True) + esink)
    return qm, e * inv, esink * inv


def _att_specs(S):
    q = pl.BlockSpec((None, S, ATT_W), lambda b: (b, 0, OFF["q"] // ATT_W))
    k = pl.BlockSpec((None, S, KV_W), lambda b: (b, 0, OFF["k"] // KV_W))
    v = pl.BlockSpec((None, S, KV_W), lambda b: (b, 0, OFF["vv"] // KV_W))
    tok = pl.BlockSpec((None, S, ATT_W), lambda b: (b, 0, 0))
    kv = pl.BlockSpec((None, S, KV_W), lambda b: (b, 0, 0))
    return q, k, v, tok, kv


_SMEM = pl.BlockSpec(memory_space=pltpu.SMEM)


def _attn_fwd(P, sinks, og, name):
    B, S, _ = P.shape
    q_spec, k_spec, v_spec, tok, _ = _att_specs(S)

    def body(q_ref, k_ref, v_ref, sink_ref, og_ref, o_ref, kpad, vpad):
        kpad[0:128, :] = jnp.zeros((128, KV_W), BF16)
        vpad[0:128, :] = jnp.zeros((128, KV_W), BF16)
        kpad[128:, :] = k_ref[...].astype(BF16)
        vpad[128:, :] = v_ref[...].astype(BF16)

        def step(n, carry):
            st = pl.multiple_of(n * 128, 128)
            qb = q_ref[pl.ds(st, 128), :]
            k2, v2 = kpad[pl.ds(st, 256), :], vpad[pl.ds(st, 256), :]
            pairs = [jnp.zeros((128, 128), F32)] * 4
            for kvh in range(2):
                _, p, _ = _att_probs(qb, k2, st, sink_ref, kvh)
                pairs = _att_unstack(pairs, _dot(p.astype(BF16), v2), kvh)
            o = jnp.concatenate(pairs, axis=1)
            r = lax.rsqrt(jnp.mean(o * o, axis=-1, keepdims=True) + EPS)
            o_ref[pl.ds(st, 128), :] = (o * r * og_ref[...]).astype(BF16)
            return carry

        lax.fori_loop(0, S // 128, step, 0)

    return pl.pallas_call(
        body, name=name, grid=(B,), in_specs=[q_spec, k_spec, v_spec, _SMEM, _full((1, ATT_W))], out_specs=tok,
        out_shape=jax.ShapeDtypeStruct((B, S, ATT_W), BF16),
        scratch_shapes=[pltpu.VMEM((S + 128, KV_W), BF16)] * 2)(P, P, P, sinks, og)


def _attn_bwd(P, dcat, sinks, og, name):
    B, S, _ = P.shape
    q_spec, k_spec, v_spec, tok, kv = _att_specs(S)
    do_spec = pl.BlockSpec((None, S, ATT_W), lambda b: (b, 0, GM_W // ATT_W))

    def body(q_ref, k_ref, v_ref, do_ref, sink_ref, og_ref, dq_ref, dk_ref, dv_ref, dsink_ref, dog_ref,
             kpad, vpad, dkpad, dvpad):
        @pl.when(pl.program_id(0) == 0)
        def _():
            dsink_ref[...] = jnp.zeros((8, 128), F32)
            dog_ref[...] = jnp.zeros((1, ATT_W), F32)

        kpad[0:128, :] = jnp.zeros((128, KV_W), BF16)
        vpad[0:128, :] = jnp.zeros((128, KV_W), BF16)
        kpad[128:, :] = k_ref[...].astype(BF16)
        vpad[128:, :] = v_ref[...].astype(BF16)
        dkpad[...] = jnp.zeros((S + 128, KV_W), F32)
        dvpad[...] = jnp.zeros((S + 128, KV_W), F32)
        half = _lane_half()
        head_row = lax.broadcasted_iota(jnp.int32, (8, 128), 0)

        def step(n, carry):
            st = pl.multiple_of(n * 128, 128)
            qb = q_ref[pl.ds(st, 128), :]
            k2, v2 = kpad[pl.ds(st, 256), :], vpad[pl.ds(st, 256), :]
            saved, pairs = [], [jnp.zeros((128, 128), F32)] * 4
            for kvh in range(2):
                qm, p, psink = _att_probs(qb, k2, st, sink_ref, kvh)
                o = _dot(p.astype(BF16), v2)
                saved.append((qm, p, psink, o))
                pairs = _att_unstack(pairs, o, kvh)
            o = jnp.concatenate(pairs, axis=1)
            r = lax.rsqrt(jnp.mean(o * o, axis=-1, keepdims=True) + EPS)
            on = o * r
            dout = do_ref[pl.ds(st, 128), :]
            dog_ref[...] += jnp.sum(dout * on, axis=0, keepdims=True)
            dyn = dout * og_ref[...]
            do = r * (dyn - on * jnp.mean(dyn * on, axis=-1, keepdims=True))
            dq_pairs = [jnp.zeros((128, 128), F32)] * 4
            dsink = jnp.zeros((8, 128), F32)
            for kvh in range(2):
                qm, p, psink, og_ = saved[kvh]
                dog = _att_stack(do, kvh, F32)
                delta = jnp.sum(dog * jnp.where(jnp.concatenate([half] * 4, axis=0) == kvh, og_, 0.0), axis=-1, keepdims=True)
                dogb, pb = dog.astype(BF16), p.astype(BF16)
                dvpad[pl.ds(st, 256), :] += _dot_tn(pb, dogb)
                dp = _dot_nt(dogb, v2)
                ds = (p * (dp - delta) * (64 ** -0.5)).astype(BF16)
                sd = psink * delta
                for g in range(4):
                    dsink = dsink - jnp.where(head_row == kvh * 4 + g, jnp.sum(sd[g * 128:(g + 1) * 128]), 0.0)
                dq_pairs = _att_unstack(dq_pairs, _dot(ds, k2), kvh)
                dkpad[pl.ds(st, 256), :] += _dot_tn(ds, qm)
            dsink_ref[...] += dsink
            dq_ref[pl.ds(st, 128), :] = jnp.concatenate(dq_pairs, axis=1).astype(BF16)
            return carry

        lax.fori_loop(0, S // 128, step, 0)
        dk_ref[...] = dkpad[128:, :].astype(BF16)
        dv_ref[...] = dvpad[128:, :].astype(BF16)

    return pl.pallas_call(
        body, name=name, grid=(B,),
        in_specs=[q_spec, k_spec, v_spec, do_spec, _SMEM, _full((1, ATT_W))],
        out_specs=[tok, kv, kv, _full((8, 128)), _full((1, ATT_W))],
        out_shape=[jax.ShapeDtypeStruct((B, S, ATT_W), BF16), jax.ShapeDtypeStruct((B, S, KV_W), BF16),
                   jax.ShapeDtypeStruct((B, S, KV_W), BF16), jax.ShapeDtypeStruct((8, 128), F32),
                   jax.ShapeDtypeStruct((1, ATT_W), F32)],
        scratch_shapes=[pltpu.VMEM((S + 128, KV_W), BF16)] * 2 + [pltpu.VMEM((S + 128, KV_W), F32)] * 2,
        compiler_params=pltpu.CompilerParams(dimension_semantics=("arbitrary",)),
    )(P, P, P, dcat, sinks, og)


CONV_TC = 256


def _conv_pre(ext, w_ref, b_ref, S):
    acc = b_ref[...] + w_ref[3:4, :] * ext[pl.ds(8, S), :]
    for k in range(1, 4):
        acc = acc + w_ref[3 - k:4 - k, :] * ext[pl.ds(8 - k, S), :]
    return acc


def _conv_fwd(P, w8, b, name):
    B, S, _ = P.shape
    nj = CONV_CH // CONV_TC
    x_spec = pl.BlockSpec((None, S, CONV_TC), lambda b_, j: (b_, 0, OFF["xbc"] // CONV_TC + j))
    tok = pl.BlockSpec((None, S, CONV_TC), lambda b_, j: (b_, 0, j))

    def body(x_ref, w_ref, b_ref, o_ref, ext):
        ext[0:8, :] = jnp.zeros((8, CONV_TC), F32)
        ext[8:, :] = x_ref[...]
        pre = _conv_pre(ext, w_ref, b_ref, S)
        o_ref[...] = pre * _sigmoid(pre)

    return pl.pallas_call(
        body, name=name, grid=(B, nj),
        in_specs=[x_spec, pl.BlockSpec((8, CONV_TC), lambda b_, j: (0, j)), pl.BlockSpec((1, CONV_TC), lambda b_, j: (0, j))],
        out_specs=tok, out_shape=jax.ShapeDtypeStruct((B, S, CONV_CH), F32),
        scratch_shapes=[pltpu.VMEM((S + 8, CONV_TC), F32)])(P, w8, b)


def _conv_bwd(P, dact, w8, b, name):
    B, S, _ = P.shape
    nj = CONV_CH // CONV_TC
    x_spec = pl.BlockSpec((None, S, CONV_TC), lambda j, b_: (b_, 0, OFF["xbc"] // CONV_TC + j))
    tok = pl.BlockSpec((None, S, CONV_TC), lambda j, b_: (b_, 0, j))
    w_spec = pl.BlockSpec((8, CONV_TC), lambda j, b_: (0, j))
    b_spec = pl.BlockSpec((1, CONV_TC), lambda j, b_: (0, j))

    def body(x_ref, d_ref, w_ref, b_ref, dx_ref, dw_ref, db_ref, ext, extd):
        @pl.when(pl.program_id(1) == 0)
        def _():
            dw_ref[...] = jnp.zeros((8, CONV_TC), F32)
            db_ref[...] = jnp.zeros((1, CONV_TC), F32)

        ext[0:8, :] = jnp.zeros((8, CONV_TC), F32)
        ext[8:, :] = x_ref[...]
        pre = _conv_pre(ext, w_ref, b_ref, S)
        sg = _sigmoid(pre)
        dpre = d_ref[...] * (sg * (1.0 + pre * (1.0 - sg)))
        extd[0:8, :] = jnp.zeros((8, CONV_TC), F32)
        extd[pl.ds(8, S), :] = dpre
        extd[pl.ds(8 + S, 8), :] = jnp.zeros((8, CONV_TC), F32)
        dx = w_ref[3:4, :] * dpre
        for k in range(1, 4):
            dx = dx + w_ref[3 - k:4 - k, :] * extd[pl.ds(8 + k, S), :]
        dx_ref[...] = dx.astype(BF16)
        db_ref[...] += jnp.sum(dpre, axis=0, keepdims=True)
        sub = lax.broadcasted_iota(jnp.int32, (8, CONV_TC), 0)
        dw = jnp.zeros((8, CONV_TC), F32)
        for i in range(4):
            dw = dw + jnp.where(sub == i, jnp.sum(dpre * ext[pl.ds(5 + i, S), :], axis=0, keepdims=True), 0.0)
        dw_ref[...] += dw

    return pl.pallas_call(
        body, name=name, grid=(nj, B), in_specs=[x_spec, tok, w_spec, b_spec], out_specs=[tok, w_spec, b_spec],
        out_shape=[jax.ShapeDtypeStruct((B, S, CONV_CH), BF16), jax.ShapeDtypeStruct((8, CONV_CH), F32),
                   jax.ShapeDtypeStruct((1, CONV_CH), F32)],
        scratch_shapes=[pltpu.VMEM((S + 8, CONV_TC), F32), pltpu.VMEM((S + 16, CONV_TC), F32)],
        compiler_params=pltpu.CompilerParams(dimension_semantics=("arbitrary", "arbitrary")),
    )(P, dact, w8, b)


def _ssd_consts():
    hd = np.arange(SSM_W) // SSM_HD
    E = (np.arange(128)[:, None] == hd[None, :]).astype(np.float32)
    tri = (np.arange(128)[:, None] >= np.arange(128)[None, :]).astype(np.float32)
    return jnp.asarray(E), jnp.asarray(E.T), jnp.asarray(tri), jnp.asarray(tri.T)


def _ssd_pre(xa, dtraw, bias, alog, E, ET, tri):
    lane = lax.broadcasted_iota(jnp.int32, (128, 128), 1)
    pre = dtraw + bias
    dtp = jnp.where(lane < SSM_H, jnp.maximum(pre, 0.0) + jnp.log(1.0 + jnp.exp(-jnp.abs(pre))), 0.0)
    a = -jnp.exp(alog)
    acs = _dot(tri, dtp * a, HI)
    acsT = acs.T
    dtE, acsE = _dot(dtp, E, HI), _dot(acs, E, HI)
    cdcol = jnp.exp(_dot(ET, acsT, HI)[:, 127:128])
    X = xa[:, :SSM_W]
    xdt = X * dtE
    wE = jnp.exp(acsE[127:128, :] - acsE)
    eE = jnp.exp(acsE)
    return dict(pre=pre, dtp=dtp, a=a, acs=acs, acsT=acsT, dtE=dtE, acsE=acsE, cdcol=cdcol, X=X, xdt=xdt, wE=wE, eE=eE)


def _ssd_decay(c, h):
    lm = lax.broadcasted_iota(jnp.int32, (128, 128), 0) >= lax.broadcasted_iota(jnp.int32, (128, 128), 1)
    return jnp.exp(jnp.where(lm, c["acs"][:, h:h + 1] - c["acsT"][h:h + 1, :], NEG_INF))


def _ssd_pair_operands(c, CB, h0):
    lane = lax.broadcasted_iota(jnp.int32, (128, 128), 1)
    L0, L1 = _ssd_decay(c, h0), _ssd_decay(c, h0 + 1)
    M = jnp.concatenate([CB * L0, CB * L1], axis=1).astype(BF16)
    xp = c["xdt"][:, h0 * 64:h0 * 64 + 128]
    BD = jnp.concatenate([jnp.where(lane < 64, xp, 0.0), jnp.where(lane >= 64, xp, 0.0)], axis=0).astype(BF16)
    return L0, L1, M, BD


def _ssd_y(c, xa, state_ref, dskipE):
    per_group, ys = [], []
    for g in range(SSM_G):
        gs = slice(g * 512, (g + 1) * 512)
        Bb = xa[:, SSM_W + g * 128:SSM_W + (g + 1) * 128].astype(BF16)
        Cb = xa[:, SSM_W + 256 + g * 128:SSM_W + 256 + (g + 1) * 128].astype(BF16)
        CB = _dot_nt(Cb, Bb)
        Sg = state_ref[gs, :]
        yoff = _dot_nt(Cb, Sg.astype(BF16)) * c["eE"][:, gs]
        ydiag, pairs = [], []
        for j in range(4):
            ops = _ssd_pair_operands(c, CB, g * 8 + 2 * j)
            pairs.append(ops)
            ydiag.append(_dot(ops[2], ops[3]))
        ys.append(jnp.concatenate(ydiag, axis=1) + yoff)
        per_group.append(dict(Bb=Bb, Cb=Cb, CB=CB, Sg=Sg, yoff=yoff, pairs=pairs))
    Y = jnp.concatenate(ys, axis=1) + c["X"] * dskipE
    return Y, per_group


def _ssd_specs(S, rev):
    nc = S // CHUNK
    cm = (lambda b, i: (b, nc - 1 - i)) if rev else (lambda b, i: (b, i))
    xa = pl.BlockSpec((None, CHUNK, CONV_CH), lambda b, i: cm(b, i) + (0,))
    z = pl.BlockSpec((None, CHUNK, SSM_W), lambda b, i: cm(b, i) + (OFF["z"] // SSM_W,))
    dt = pl.BlockSpec((None, CHUNK, 128), lambda b, i: cm(b, i) + (OFF["dt"] // 128,))
    tok = pl.BlockSpec((None, CHUNK, SSM_W), lambda b, i: cm(b, i) + (0,))
    st = pl.BlockSpec((None, None, SSM_W, 128), lambda b, i: cm(b, i) + (0, 0))
    return nc, xa, z, dt, tok, st


def _ssd_fwd(xact, P, bias, alog, dskipE, ng, name):
    B, S, _ = P.shape
    nc, xa_spec, z_spec, dt_spec, tok, st_spec = _ssd_specs(S, False)
    E, ET, tri, _ = _ssd_consts()

    def body(xa_ref, z_ref, dt_ref, bias_ref, alog_ref, dsk_ref, ng_ref, E_ref, ET_ref, tri_ref, o_ref, sp_ref, state):
        @pl.when(pl.program_id(1) == 0)
        def _():
            state[...] = jnp.zeros((SSM_W, 128), F32)

        sp_ref[...] = state[...]
        xa = xa_ref[...]
        c = _ssd_pre(xa, dt_ref[...], bias_ref[...], alog_ref[...], E_ref[...], ET_ref[...], tri_ref[...])
        Y, groups = _ssd_y(c, xa, state, dsk_ref[...])
        Z = (c["xdt"] * c["wE"]).astype(BF16)
        for g in range(SSM_G):
            gs = slice(g * 512, (g + 1) * 512)
            state[gs, :] = groups[g]["Sg"] * c["cdcol"][gs, :] + _dot_tn(Z[:, gs], groups[g]["Bb"])
        zv = z_ref[...]
        yz = Y * (zv * _sigmoid(zv))
        outs = []
        for g in range(SSM_G):
            yg = yz[:, g * 512:(g + 1) * 512]
            outs.append(yg * lax.rsqrt(jnp.mean(yg * yg, axis=-1, keepdims=True) + EPS))
        o_ref[...] = (jnp.concatenate(outs, axis=1) * ng_ref[...]).astype(BF16)

    return pl.pallas_call(
        body, name=name, grid=(B, nc),
        in_specs=[xa_spec, z_spec, dt_spec, _full((1, 128)), _full((1, 128)), _full((1, SSM_W)), _full((1, SSM_W)),
                  _full((128, SSM_W)), _full((SSM_W, 128)), _full((128, 128))],
        out_specs=[tok, st_spec],
        out_shape=[jax.ShapeDtypeStruct((B, S, SSM_W), BF16), jax.ShapeDtypeStruct((B, nc, SSM_W, 128), F32)],
        scratch_shapes=[pltpu.VMEM((SSM_W, 128), F32)],
        compiler_params=pltpu.CompilerParams(dimension_semantics=("arbitrary", "arbitrary")),
    )(xact, P, P, bias, alog, dskipE, ng, E, ET, tri)


def _ssd_bwd(xact, P, sprev, dcat, bias, alog, dskipE, ng, name):
    B, S, _ = P.shape
    nc, xa_spec, z_spec, dt_spec, tok, st_spec = _ssd_specs(S, True)
    do_spec = pl.BlockSpec((None, CHUNK, SSM_W), lambda b, i: (b, nc - 1 - i, 1))
    E, ET, tri, triT = _ssd_consts()
    dt_out = pl.BlockSpec((None, CHUNK, 128), lambda b, i: (b, nc - 1 - i, 0))

    def body(xa_ref, z_ref, dt_ref, sp_ref, do_ref, bias_ref, alog_ref, dsk_ref, ng_ref, E_ref, ET_ref, tri_ref, triT_ref,
             dxa_ref, dz_ref, ddt_ref, dbias_ref, dalog_ref, ddsk_ref, dng_ref, dstate):
        first = (pl.program_id(0) == 0) & (pl.program_id(1) == 0)

        @pl.when(first)
        def _():
            for ref in (dbias_ref, dalog_ref, ddsk_ref, dng_ref):
                ref[...] = jnp.zeros(ref.shape, F32)

        @pl.when(pl.program_id(1) == 0)
        def _():
            dstate[...] = jnp.zeros((SSM_W, 128), F32)

        xa, ETm = xa_ref[...], ET_ref[...]
        c = _ssd_pre(xa, dt_ref[...], bias_ref[...], alog_ref[...], E_ref[...], ETm, tri_ref[...])
        Y, groups = _ssd_y(c, xa, sp_ref, dsk_ref[...])
        X, xdt = c["X"], c["xdt"]
        zv = z_ref[...]
        sg = _sigmoid(zv)
        zs = zv * sg
        yz = Y * zs
        dout = do_ref[...]
        dyz = []
        for g in range(SSM_G):
            gs = slice(g * 512, (g + 1) * 512)
            yg = yz[:, gs]
            r = lax.rsqrt(jnp.mean(yg * yg, axis=-1, keepdims=True) + EPS)
            yn = yg * r
            dng_ref[:, gs] += jnp.sum(dout[:, gs] * yn, axis=0, keepdims=True)
            dyn = dout[:, gs] * ng_ref[:, gs]
            dyz.append(r * (dyn - yn * jnp.mean(dyn * yn, axis=-1, keepdims=True)))
        dyz = jnp.concatenate(dyz, axis=1)
        dz_ref[...] = (dyz * Y * (sg * (1.0 + zv * (1.0 - sg)))).astype(BF16)
        dY = dyz * zs
        ddsk_ref[...] += jnp.sum(dY * X, axis=0, keepdims=True)
        dX = dY * dsk_ref[...]
        lane = lax.broadcasted_iota(jnp.int32, (128, 128), 1)
        sub = lax.broadcasted_iota(jnp.int32, (128, 128), 0)
        colform = jnp.zeros((128, 128), F32)
        rowform = jnp.zeros((128, 128), F32)
        dxdt, gacsE, dBC = [], [], []
        for g in range(SSM_G):
            gs = slice(g * 512, (g + 1) * 512)
            G = groups[g]
            Bb, Cb, CB, Sg = G["Bb"], G["Cb"], G["CB"], G["Sg"]
            dYg = dY[:, gs]
            dQ = (dYg * c["eE"][:, gs]).astype(BF16)
            dSn = dstate[gs, :]
            dSnb = dSn.astype(BF16)
            cd = c["cdcol"][gs, :]
            dC = _dot(dQ, Sg.astype(BF16))
            dSprev = _dot_tn(dQ, Cb) + dSn * cd
            hcol = jnp.sum(_dot(E_ref[:, gs], dSn * Sg * cd, HI), axis=-1, keepdims=True)
            rowform = rowform + jnp.where(lane == 127, hcol, 0.0)
            Zg = xdt[:, gs] * c["wE"][:, gs]
            dZ = _dot_nt(Bb, dSnb)
            dB = _dot(Zg.astype(BF16), dSnb)
            U = dZ * Zg
            ga = dYg * G["yoff"] - U
            ga = ga + jnp.where(lax.broadcasted_iota(jnp.int32, (128, 512), 0) == 127, jnp.sum(U, axis=0, keepdims=True), 0.0)
            gacsE.append(ga)
            dxg = [None] * 4
            dCB = jnp.zeros((128, 128), F32)
            for j in range(4):
                h0 = g * 8 + 2 * j
                L0, L1, M, BD = G["pairs"][j]
                dYp = dYg[:, j * 128:(j + 1) * 128].astype(BF16)
                dM = _dot_nt(dYp, BD)
                dBD = _dot_tn(M, dYp)
                dxg[j] = jnp.where(lane < 64, dBD[:128], dBD[128:])
                for t, (h, L) in enumerate(((h0, L0), (h0 + 1, L1))):
                    dMh = dM[:, t * 128:(t + 1) * 128]
                    dCB = dCB + dMh * L
                    Gh = dMh * CB * L
                    colform = colform + jnp.where(lane == h, jnp.sum(Gh, axis=1, keepdims=True), 0.0)
                    rowform = rowform - jnp.where(sub == h, jnp.sum(Gh, axis=0, keepdims=True), 0.0)
            dCBb = dCB.astype(BF16)
            dC = dC + _dot(dCBb, Bb)
            dB = dB + _dot_tn(dCBb, Cb)
            dxdt.append(jnp.concatenate(dxg, axis=1) + dZ * c["wE"][:, gs])
            dBC.append((dB, dC))
            dstate[gs, :] = dSprev
        dxdt = jnp.concatenate(dxdt, axis=1)
        dX = dX + dxdt * c["dtE"]
        ddt = _dot(dxdt * X, ETm, HI)
        dacs = colform + rowform.T + _dot(jnp.concatenate(gacsE, axis=1), ETm, HI)
        dda = _dot(triT_ref[...], dacs, HI)
        ddt = ddt + dda * c["a"]
        dalog_ref[...] += jnp.sum(dda * c["dtp"], axis=0, keepdims=True) * c["a"]
        ddtraw = jnp.where(lane < SSM_H, ddt * _sigmoid(c["pre"]), 0.0)
        dbias_ref[...] += jnp.sum(ddtraw, axis=0, keepdims=True)
        ddt_ref[...] = ddtraw.astype(BF16)
        dxa_ref[...] = jnp.concatenate([dX, dBC[0][0], dBC[1][0], dBC[0][1], dBC[1][1]], axis=1)

    p128, p1k = _full((1, 128)), _full((1, SSM_W))
    return pl.pallas_call(
        body, name=name, grid=(B, nc),
        in_specs=[xa_spec, z_spec, dt_spec, st_spec, do_spec, p128, p128, p1k, p1k,
                  _full((128, SSM_W)), _full((SSM_W, 128)), _full((128, 128)), _full((128, 128))],
        out_specs=[xa_spec, tok, dt_out, p128, p128, p1k, p1k],
        out_shape=[jax.ShapeDtypeStruct((B, S, CONV_CH), F32), jax.ShapeDtypeStruct((B, S, SSM_W), BF16),
                   jax.ShapeDtypeStruct((B, S, 128), BF16), jax.ShapeDtypeStruct((1, 128), F32),
                   jax.ShapeDtypeStruct((1, 128), F32), jax.ShapeDtypeStruct((1, SSM_W), F32),
                   jax.ShapeDtypeStruct((1, SSM_W), F32)],
        scratch_shapes=[pltpu.VMEM((SSM_W, 128), F32)],
        compiler_params=pltpu.CompilerParams(dimension_semantics=("arbitrary", "arbitrary")),
    )(xact, P, P, sprev, dcat, bias, alog, dskipE, ng, E, ET, tri, triT)


def _adamw(w, parts, m, v, name, tr=512, row0=0, prev=None):
    Rtot, C = w.shape
    ns, R = parts.shape[0], parts.shape[1]
    tr = min(tr, R)
    assert R % tr == 0 and row0 % tr == 0
    off = row0 // tr
    c1 = 1.0 / (1.0 - ADAM_B1 ** ADAM_STEP)
    c2 = 1.0 / (1.0 - ADAM_B2 ** ADAM_STEP)

    def body(w_ref, p_ref, m_ref, v_ref, *rest):
        g_ref, d_ref, mo_ref, vo_ref = rest[-4:]
        g = p_ref[0].astype(F32)
        for s in range(1, ns):
            g = g + p_ref[s].astype(F32)
        mn = ADAM_B1 * m_ref[...] + (1.0 - ADAM_B1) * g
        vn = ADAM_B2 * v_ref[...] + (1.0 - ADAM_B2) * (g * g)
        g_ref[...] = g
        mo_ref[...] = mn
        vo_ref[...] = vn
        d_ref[...] = -ADAM_LR * ((mn * c1) / (jnp.sqrt(vn * c2) + ADAM_EPS) + ADAM_WD * w_ref[...])

    blk = pl.BlockSpec((tr, C), lambda i: (i + off, 0))
    extra = [] if prev is None else list(prev)
    return pl.pallas_call(
        body, name=name, grid=(R // tr,),
        in_specs=[blk, pl.BlockSpec((ns, tr, C), lambda i: (0, i, 0)), blk, blk] + [pl.BlockSpec(memory_space=pl.ANY)] * len(extra),
        out_specs=[blk] * 4, out_shape=[jax.ShapeDtypeStruct((Rtot, C), F32)] * 4,
        input_output_aliases={4 + k: k for k in range(len(extra))})(w, parts, m, v, *extra)


_SMALL = ("ada_b", "norm1_g", "gm_ln_g", "gm_ln_b", "gm_ws", "gm_bs", "gm_norm_g", "attn_sinks", "attn_norm_g", "conv_b",
          "dt_bias", "a_log", "d_skip", "ssm_norm_g", "norm2_g", "final_norm_g")


def _pack(arrs):
    flat = []
    for a in arrs:
        f = a.reshape(-1).astype(F32)
        flat.append(jnp.pad(f, (0, (-f.shape[0]) % 1024)))
    return jnp.concatenate(flat).reshape(-1, 128)


def _unpack(pack, like):
    out, o = [], 0
    flat = pack.reshape(-1)
    for a in like:
        n = int(np.prod(a.shape))
        out.append(flat[o:o + n].reshape(a.shape))
        o += n + (-n) % 1024
    return out


def kernel(x, c, ada_w, ada_b, norm1_g, w_in, gm_ln_g, gm_ln_b, gm_ws, gm_bs, gm_norm_g, attn_sinks, attn_norm_g, conv_w, conv_b, dt_bias, a_log, d_skip, ssm_norm_g, w_out, norm2_g, w_mlp1, w_mlp2, final_norm_g, loss_target, m_ada_w, m_ada_b, m_norm1_g, m_w_in, m_gm_ln_g, m_gm_ln_b, m_gm_ws, m_gm_bs, m_gm_norm_g, m_attn_sinks, m_attn_norm_g, m_conv_w, m_conv_b, m_dt_bias, m_a_log, m_d_skip, m_ssm_norm_g, m_w_out, m_norm2_g, m_w_mlp1, m_w_mlp2, m_final_norm_g, v_ada_w, v_ada_b, v_norm1_g, v_w_in, v_gm_ln_g, v_gm_ln_b, v_gm_ws, v_gm_bs, v_gm_norm_g, v_attn_sinks, v_attn_norm_g, v_conv_w, v_conv_b, v_dt_bias, v_a_log, v_d_skip, v_ssm_norm_g, v_w_out, v_norm2_g, v_w_mlp1, v_w_mlp2, v_final_norm_g):
    args = dict(locals())
    B, S, _ = x.shape
    T = B * S
    L = DEPTH
    me = 4 * lax.axis_index("x") + 2 * lax.axis_index("y") + lax.axis_index("c")

    shards = lambda l: [w_in[l].astype(BF16), w_out[l].astype(BF16), w_mlp1[l].astype(BF16), w_mlp2[l].astype(BF16)]
    gath = _gather2([c, conv_w] + shards(0), "ag_weights0")
    gw = [gath[2:]] + [_gather2(shards(l), f"ag_weights{l}") for l in range(1, L)]
    c_all = gath[0].reshape(NDEV * B, D)
    c_act = (c_all * jax.nn.sigmoid(c_all)).astype(BF16)
    nb_rows = c_act.shape[0]
    c_pad = jnp.pad(c_act, ((0, 128 - nb_rows), (0, 0)))
    adw = ada_w.astype(BF16)
    mod_part = jnp.stack([_mm(c_pad, adw[l], mode="nn", name=f"mod{l}", tn=768)[:nb_rows] for l in range(L)])
    mod_all = _exchange([mod_part], "ag_mod", False)[0]
    mod_mine = lax.dynamic_slice_in_dim(mod_all, me * B, B, axis=2)
    mod = jnp.transpose(mod_mine, (1, 2, 0, 3)).reshape(L, B, 6 * D) + ada_b[:, None, :]
    mods = [[mod[l][:, None, i * D:(i + 1) * D] for i in range(6)] for l in range(L)]

    win_g = [_to_work_cols(jnp.transpose(gw[l][0], (1, 0, 2)).reshape(D, IN_W)) for l in range(L)]
    wout_g = [gw[l][1].reshape(D, D) for l in range(L)]
    w1_g = [gw[l][2] for l in range(L)]
    w2_g = [gw[l][3].reshape(DFF, D) for l in range(L)]

    tril = jnp.tril(jnp.ones((128, 128), F32))
    row = lambda a: a.reshape(1, -1)
    pad128 = lambda a: jnp.pad(a.reshape(1, -1), ((0, 0), (0, 128 - a.shape[-1])))
    small = []
    for l in range(L):
        wt = gm_ws[l] * tril
        small.append(dict(
            lng=row(gm_ln_g[l]), lnb=row(gm_ln_b[l]), wt=wt.astype(BF16), wtT=jnp.swapaxes(wt, 1, 2).astype(BF16),
            bsx=jnp.repeat(gm_bs[l].T, 128, axis=1), gog=row(gm_norm_g[l]), sinks=attn_sinks[l], aog=row(attn_norm_g[l]),
            bias=pad128(dt_bias[l]), alog=pad128(a_log[l]), dskE=jnp.repeat(d_skip[l], SSM_HD).reshape(1, SSM_W),
            sng=row(ssm_norm_g[l]), cb=row(conv_b[l])))
    convw_all = jnp.transpose(gath[1], (1, 2, 0, 3)).reshape(L, 4, CONV_CH)
    convw8 = jnp.pad(convw_all, ((0, 0), (0, 4), (0, 0)))

    saved = []
    xl = x
    h = _norm_fwd(xl, row(norm1_g[0]), mods[0][1], mods[0][0], "norm1_f0")
    for l in range(L):
        sm = small[l]
        P = _mm(h.reshape(T, D), win_g[l], mode="nn", name=f"proj_in{l}", tn=1536).reshape(B, S, PW)
        out_a = _gmlp_fwd(P, sm["lng"], sm["lnb"], sm["wt"], sm["bsx"], sm["gog"], f"gmlp_f{l}")
        out_b = _attn_fwd(P, sm["sinks"], sm["aog"], f"attn_f{l}")
        xact = _conv_fwd(P, convw8[l], sm["cb"], f"conv_f{l}")
        out_c, sprev = _ssd_fwd(xact, P, sm["bias"], sm["alog"], sm["dskE"], sm["sng"], f"ssd_f{l}")
        cat = jnp.concatenate([out_a, out_b, out_c], axis=-1)
        mix = _mm(cat.reshape(T, D), wout_g[l], mode="nn", name=f"proj_out{l}").reshape(B, S, D)
        x_mid, h2 = _norm_fwd(xl, row(norm2_g[l]), mods[l][4], mods[l][3], f"norm2_f{l}", resid=(mix, mods[l][2]))
        a_act, r_act = _mm(h2.reshape(T, D), w1_g[l], mode="nn", name=f"mlp1_{l}", out_dtypes=(BF16, BF16), col_blocked_b=True,
                           epilogue=lambda acc: (acc, jnp.square(jnp.maximum(acc, 0.0))))
        m2 = _mm(r_act, w2_g[l], mode="nn", name=f"mlp2_{l}").reshape(B, S, D)
        saved.append(dict(x_in=xl, h=h, P=P, xact=xact, sprev=sprev, cat=cat, mix=mix, x_mid=x_mid, h2=h2, a=a_act, r=r_act, m2=m2))
        if l + 1 < L:
            xl, h = _norm_fwd(x_mid, row(norm1_g[l + 1]), mods[l + 1][1], mods[l + 1][0], f"norm1_f{l + 1}", resid=(m2, mods[l][5]))

    sv = saved[L - 1]
    nb = _norm_bwd(sv["x_mid"], row(final_norm_g), "final_b", tgt=loss_target, br=sv["m2"], gate=mods[L - 1][5], x_is_prev=True)
    loss_part, g_final = nb["loss"], nb["dg"]
    dmod, gsm, gconvw = [None] * L, [None] * L, [None] * L
    gW = dict(w_in=[None] * L, w_out=[None] * L, w_mlp1=[None] * L, w_mlp2=[None] * L)
    big = ("w_in", "w_out", "w_mlp1", "w_mlp2")
    big_res = dict.fromkeys(big)
    core = lax.axis_index("c").astype(jnp.int32).reshape(1)
    for l in reversed(range(L)):
        sv, sm = saved[l], small[l]
        dm2, dxo, dg2 = nb["dbr"].reshape(T, D), nb["dx"], nb["dgate"]
        da = _mm(dm2, w2_g[l], mode="nt", name=f"mlp2_dx{l}", out_dtypes=(BF16,), extras=(sv["a"],),
                 epilogue=lambda acc, a: (acc * (2.0 * jnp.maximum(a.astype(F32), 0.0)),))
        gW["w_mlp2"][l] = _mm(sv["r"], dm2, mode="tn", name=f"mlp2_dw{l}", out_dtypes=(BF16,), tk=1024).reshape(4, 2, DFF // NDEV, D)
        dh2 = _mm(da, w1_g[l], mode="nt", name=f"mlp1_dx{l}", col_blocked_b=True).reshape(B, S, D)
        gW["w_mlp1"][l] = _mm(sv["h2"].reshape(T, D), da, mode="tn", name=f"mlp1_dw{l}", out_dtypes=(BF16,), tk=1024,
                              col_blocked_out=True).reshape(4, 2, D, DFF // NDEV)
        nb2 = _norm_bwd(sv["x_mid"], row(norm2_g[l]), f"norm2_b{l}", sc=mods[l][4], dh=dh2, dres=dxo, br=sv["mix"], gate=mods[l][2])
        dmix = nb2["dbr"].reshape(T, D)
        dcat = _mm(dmix, wout_g[l], mode="nt", name=f"proj_out_dx{l}").reshape(B, S, D)
        gW["w_out"][l] = _mm(sv["cat"].reshape(T, D), dmix, mode="tn", name=f"proj_out_dw{l}", out_dtypes=(BF16,),
                             tk=1024).reshape(4, 2, D // NDEV, D)
        du, dv, dlng, dlnb, dws, dbsx, dgog = _gmlp_bwd(sv["P"], dcat, sm["lng"], sm["lnb"], sm["wt"], sm["wtT"], sm["bsx"],
                                                        sm["gog"], f"gmlp_b{l}")
        dq, dk, dvv, dsink, daog = _attn_bwd(sv["P"], dcat, sm["sinks"], sm["aog"], f"attn_b{l}")
        dxa, dz, ddt, dbias, dalog, ddsk, dsng = _ssd_bwd(sv["xact"], sv["P"], sv["sprev"], dcat, sm["bias"], sm["alog"],
                                                          sm["dskE"], sm["sng"], f"ssd_b{l}")
        dxbc, dcw, dcb = _conv_bwd(sv["P"], dxa, convw8[l], sm["cb"], f"conv_b{l}")
        dP = jnp.concatenate([dxbc, du, dz, dv, dq, dk, dvv, ddt, jnp.zeros((B, S, PW - OFF["dt"] - 128), BF16)],
                             axis=-1).reshape(T, PW)
        dh = _mm(dP, win_g[l], mode="nt", name=f"proj_in_dx{l}", tk=1536).reshape(B, S, D)
        dwin = _mm(sv["h"].reshape(T, D), dP, mode="tn", name=f"proj_in_dw{l}", out_dtypes=(BF16,), tn=1536, tk=1024)
        gW["w_in"][l] = jnp.transpose(_from_work_cols(dwin).reshape(D, NDEV, IN_W // NDEV), (1, 0, 2)).reshape(4, 2, D, IN_W // NDEV)
        nb = _norm_bwd(sv["x_in"], row(norm1_g[l]), f"norm1_b{l}", sc=mods[l][1], dh=dh, dres=nb2["dx"],
                       br=saved[l - 1]["m2"] if l > 0 else None, gate=mods[l - 1][5] if l > 0 else None)
        parts = [gW[n][l] for n in big]
        from_sib = _pair_exchange(parts, f"rs_pair{l}")
        chip_sums = [_pair_add(p, r, core, f"rs_add_{n}{l}") for n, p, r in zip(big, parts, from_sib)]
        for n, r_, tr in zip(big, _chip_exchange(chip_sums, f"rs_chips{l}"), (256, 256, 256, 128)):
            w = args[n]
            rows = w.shape[1]
            big_res[n] = _adamw(w.reshape(-1, w.shape[-1]), r_, args["m_" + n].reshape(-1, w.shape[-1]),
                                args["v_" + n].reshape(-1, w.shape[-1]), f"adamw_{n}{l}", tr=tr, row0=l * rows, prev=big_res[n])
        dmod[l] = jnp.concatenate([nb["dsh"], nb["dsc"], nb2["dgate"], nb2["dsh"], nb2["dsc"], dg2], axis=-1)
        gconvw[l] = dcw[:4]
        gsm[l] = dict(
            ada_b=jnp.sum(dmod[l], axis=(0, 1)), norm1_g=nb["dg"], gm_ln_g=dlng, gm_ln_b=dlnb, gm_ws=dws,
            gm_bs=dbsx.reshape(128, GM_H, 128).sum(-1).T, gm_norm_g=dgog, attn_sinks=dsink[:, 0], attn_norm_g=daog,
            conv_b=dcb, dt_bias=dbias[0, :SSM_H], a_log=dalog[0, :SSM_H], d_skip=ddsk.reshape(SSM_H, SSM_HD).sum(-1),
            ssm_norm_g=dsng, norm2_g=nb2["dg"])
    grad_x = nb["dx"]

    per_layer = [n for n in _SMALL if n != "final_norm_g"]
    g_small = [jnp.stack([gsm[l][n].reshape(args[n].shape[1:]) for l in range(L)]) for n in per_layer] + [g_final.reshape(D)]
    zc = jnp.zeros((L, 4, CONV_CH), F32)
    z1 = jnp.zeros((1, 128), F32)
    gpack = _pack([loss_part] + g_small + [jnp.stack(gconvw)])
    got = _exchange([jnp.stack(dmod).reshape(L, B, 6 * D), gpack], "ag_small", False)
    like = [z1] + [args[n] for n in _SMALL] + [zc]
    packs = [_pack([z1] + [args[p + n] for n in _SMALL] + [zc]) for p in ("", "m_", "v_")]
    sres = [_unpack(p, like) for p in _adamw(packs[0], got[1], packs[1], packs[2], "adamw_small", tr=gpack.shape[0])]
    res = {n: [r[1 + i] for r in sres] for i, n in enumerate(_SMALL)}
    loss = sres[0][0][0, 0]
    gcw = lax.dynamic_slice_in_dim(sres[0][-1], me * (CONV_CH // NDEV), CONV_CH // NDEV, axis=2)

    def update(name, parts, tr):
        w = args[name]
        r = _adamw(w.reshape(-1, w.shape[-1]), parts, args["m_" + name].reshape(-1, w.shape[-1]),
                   args["v_" + name].reshape(-1, w.shape[-1]), "adamw_" + name, tr=tr)
        res[name] = [a.reshape(w.shape) for a in r]

    update("conv_w", gcw.reshape(1, L * 4, CONV_CH // NDEV), L * 4)

    dmod_all = jnp.transpose(got[0], (1, 0, 2, 3)).reshape(L, NDEV * B, 6 * D)
    dm_mine = lax.dynamic_slice_in_dim(dmod_all, me * (6 * D // NDEV), 6 * D // NDEV, axis=2)
    dm_pad = jnp.pad(dm_mine, ((0, 0), (0, 128 - nb_rows), (0, 0))).astype(BF16)
    g_adaw = jnp.stack([_mm(c_pad, dm_pad[l], mode="tn", name=f"ada_dw{l}", tn=768) for l in range(L)])
    update("ada_w", g_adaw.reshape(1, L * D, 6 * D // NDEV), 256)

    for n in big:
        res[n] = [a.reshape(args[n].shape) for a in big_res[n]]

    names = ['ada_w', 'ada_b', 'norm1_g', 'w_in', 'gm_ln_g', 'gm_ln_b', 'gm_ws', 'gm_bs', 'gm_norm_g', 'attn_sinks',
             'attn_norm_g', 'conv_w', 'conv_b', 'dt_bias', 'a_log', 'd_skip', 'ssm_norm_g', 'w_out', 'norm2_g', 'w_mlp1',
             'w_mlp2', 'final_norm_g']
    return (loss, grad_x, *[res[n][0] for n in names], *[res[n][1] for n in names], *[res[n][2] for n in names],
            *[res[n][3] for n in names])
```

```python
import functools

import jax
import jax.numpy as jnp
import numpy as np
from jax import lax
from jax.experimental import pallas as pl
from jax.experimental.pallas import tpu as pltpu

F32, BF16 = jnp.float32, jnp.bfloat16
HI = lax.Precision.HIGHEST
MESH = pl.DeviceIdType.MESH
NDEV = 8

D = 2048
DEPTH = 2
CHUNK = 128
GM_W, GM_H = 512, 4
ATT_W, KV_W, ATT_H = 512, 128, 8
SSM_W, SSM_H, SSM_HD, SSM_G = 1024, 16, 64, 2
CONV_CH = 1536
IN_W = 4368
DFF = 8192
EPS = 1e-6
NEG_INF = -1e30
GELU_K = 0.7978845608028654
GELU_C = 0.044715

_ORIG = (("u", 512), ("v", 512), ("q", 512), ("k", 128), ("vv", 128), ("z", 1024), ("xbc", 1536), ("dt", 16))
OFF = dict(xbc=0, u=1536, z=2048, v=3072, q=3584, k=4096, vv=4224, dt=4352)
PW = 4608

ADAM_LR, ADAM_B1, ADAM_B2, ADAM_EPS, ADAM_WD, ADAM_STEP = 0.001, 0.9, 0.999, 1e-08, 0.01, 10


def _to_work_cols(w):
    parts, o = {}, 0
    for name, wd in _ORIG:
        parts[name] = w[..., o:o + wd]
        o += wd
    z = lambda n: jnp.zeros(w.shape[:-1] + (n,), w.dtype)
    return jnp.concatenate([parts["xbc"], parts["u"], parts["z"], parts["v"], parts["q"], parts["k"], parts["vv"],
                            parts["dt"], z(PW - OFF["dt"] - 16)], axis=-1)


def _from_work_cols(wp):
    return jnp.concatenate([wp[..., OFF[name]:OFF[name] + wd] for name, wd in _ORIG], axis=-1)


def _sigmoid(x):
    return 1.0 / (1.0 + jnp.exp(-x))


def _gelu(x):
    return 0.5 * x * (1.0 + jnp.tanh(GELU_K * (x + GELU_C * x * x * x)))


def _gelu_grad(x):
    t = jnp.tanh(GELU_K * (x + GELU_C * x * x * x))
    return 0.5 * (1.0 + t) + 0.5 * x * (1.0 - t * t) * GELU_K * (1.0 + 3.0 * GELU_C * x * x)


def _dot(a, b, prec=None):
    return jnp.dot(a, b, precision=prec, preferred_element_type=F32)


def _dot_nt(a, b, prec=None):
    return lax.dot_general(a, b, (((1,), (1,)), ((), ())), precision=prec, preferred_element_type=F32)


def _dot_tn(a, b, prec=None):
    return lax.dot_general(a, b, (((0,), (0,)), ((), ())), precision=prec, preferred_element_type=F32)


def _full(shape):
    return pl.BlockSpec(shape, lambda *_: (0,) * len(shape))


_HBM = pl.BlockSpec(memory_space=pltpu.HBM)


def _me():
    return lax.axis_index("x"), lax.axis_index("y"), lax.axis_index("c")


def _peer(k):
    x, y, c = _me()
    px = 1 - x if k & 4 else x
    py = 1 - y if k & 2 else y
    pc = 1 - c if k & 1 else c
    return (px, py, pc), 4 * px + 2 * py + pc


def _exchange(xs, name, scatter):
    n = len(xs)

    def body(*refs):
        ins, outs = refs[:n], refs[n:2 * n]
        send, recv, loc = refs[2 * n:]
        x, y, c = _me()
        me = 4 * x + 2 * y + c
        started = []
        for i in range(n):
            own = pltpu.make_async_copy(ins[i].at[me] if scatter else ins[i], outs[i].at[me], loc.at[i])
            own.start()
            started.append(own)
        for k in range(1, NDEV):
            dev, lin = _peer(k)
            for i in range(n):
                pltpu.make_async_remote_copy(
                    src_ref=ins[i].at[lin] if scatter else ins[i], dst_ref=outs[i].at[me],
                    send_sem=send.at[i, k - 1], recv_sem=recv.at[i, k - 1], device_id=dev, device_id_type=MESH).start()
        for k in range(1, NDEV):
            dev, lin = _peer(k)
            for i in range(n):
                pltpu.make_async_remote_copy(
                    src_ref=ins[i].at[lin] if scatter else ins[i], dst_ref=outs[i].at[lin],
                    send_sem=send.at[i, k - 1], recv_sem=recv.at[i, k - 1], device_id=dev, device_id_type=MESH).wait()
        for own in started:
            own.wait()

    out_shape = [jax.ShapeDtypeStruct(a.shape if scatter else (NDEV,) + a.shape, a.dtype) for a in xs]
    return pl.pallas_call(
        body, name=name, out_shape=out_shape, in_specs=[_HBM] * n, out_specs=[_HBM] * n,
        scratch_shapes=[pltpu.SemaphoreType.DMA((n, NDEV - 1)), pltpu.SemaphoreType.DMA((n, NDEV - 1)),
                        pltpu.SemaphoreType.DMA((n,))],
        compiler_params=pltpu.CompilerParams(has_side_effects=True),
    )(*xs)


def _chips():
    x, y, c = _me()
    return x, y, c, [(1 - x, y), (x, 1 - y), (1 - x, 1 - y)]


def _gather2(xs, name):
    n = len(xs)

    def body(*refs):
        ins, outs = refs[:n], refs[n:2 * n]
        send, recv, loc = refs[2 * n:]
        x, y, c, chips = _chips()
        me, sib = (x, y, c), (x, y, 1 - c)

        def cp(i, k, block, to, src=None):
            slot = outs[i].at[4 * block[0] + 2 * block[1] + block[2]]
            return pltpu.make_async_remote_copy(src_ref=slot if src is None else src, dst_ref=slot, send_sem=send.at[i, k],
                                                recv_sem=recv.at[i, k], device_id=to, device_id_type=MESH)

        sent = []
        for i in range(n):
            for j, chip in enumerate(chips):
                sent.append(cp(i, 1 + j, me, (*chip, c), src=ins[i]))
            sent.append(cp(i, 0, me, sib, src=ins[i]))
        for s in sent:
            s.start()
        own = [pltpu.make_async_copy(ins[i], outs[i].at[4 * x + 2 * y + c], loc.at[i]) for i in range(n)]
        for o in own:
            o.start()
        for j, chip in enumerate(chips):
            for i in range(n):
                cp(i, 1 + j, (*chip, c), me).wait_recv()
                fwd = cp(i, 4 + j, (*chip, c), sib)
                fwd.start()
                sent.append(fwd)
        for i in range(n):
            cp(i, 0, sib, me).wait_recv()
            for j, chip in enumerate(chips):
                cp(i, 4 + j, (*chip, 1 - c), me).wait_recv()
        for s in sent:
            s.wait_send()
        for o in own:
            o.wait()

    return pl.pallas_call(
        body, name=name, out_shape=[jax.ShapeDtypeStruct((NDEV,) + a.shape, a.dtype) for a in xs],
        in_specs=[_HBM] * n, out_specs=[_HBM] * n,
        scratch_shapes=[pltpu.SemaphoreType.DMA((n, 7)), pltpu.SemaphoreType.DMA((n, 7)), pltpu.SemaphoreType.DMA((n,))],
        compiler_params=pltpu.CompilerParams(has_side_effects=True),
    )(*xs)


def _pair_exchange(ps, name):
    n = len(ps)

    def body(*refs):
        ins, outs = refs[:n], refs[n:2 * n]
        send, recv = refs[2 * n:]
        x, y, c = _me()
        cps = [pltpu.make_async_remote_copy(src_ref=ins[i].at[ch, 1 - c], dst_ref=outs[i].at[ch], send_sem=send.at[i, ch],
                                            recv_sem=recv.at[i, ch], device_id=(x, y, 1 - c), device_id_type=MESH)
               for i in range(n) for ch in range(4)]
        for cp in cps:
            cp.start()
        for cp in cps:
            cp.wait()

    return pl.pallas_call(
        body, name=name, out_shape=[jax.ShapeDtypeStruct((4,) + a.shape[2:], a.dtype) for a in ps],
        in_specs=[_HBM] * n, out_specs=[_HBM] * n,
        scratch_shapes=[pltpu.SemaphoreType.DMA((n, 4)), pltpu.SemaphoreType.DMA((n, 4))],
        compiler_params=pltpu.CompilerParams(has_side_effects=True),
    )(*ps)


def _pair_add(p, r1, core, name, tr=256):
    _, _, R, C = p.shape
    tr = min(tr, R)

    def body(core_ref, p_ref, r_ref, o_ref, o2_ref):
        s = (p_ref[...].astype(F32) + r_ref[...].astype(F32)).astype(o_ref.dtype)
        o_ref[...] = s
        o2_ref[...] = s

    blk = pl.BlockSpec((None, tr, C), lambda ch, i, core_ref: (ch, i, 0))
    return pl.pallas_call(
        body, name=name, out_shape=[jax.ShapeDtypeStruct((4, R, C), p.dtype)] * 2,
        grid_spec=pltpu.PrefetchScalarGridSpec(
            num_scalar_prefetch=1, grid=(4, R // tr),
            in_specs=[pl.BlockSpec((None, None, tr, C), lambda ch, i, core_ref: (ch, core_ref[0], i, 0)), blk],
            out_specs=[blk, blk]),
    )(core, p, r1)


def _chip_exchange(ss, name):
    n = len(ss)

    def body(*refs):
        ins, outs = refs[:n], refs[n:2 * n]
        send, recv, loc = refs[2 * n:]
        x, y, c, chips = _chips()
        mine = 2 * x + y
        own = [pltpu.make_async_copy(ins[i].at[mine], outs[i].at[mine], loc.at[i]) for i in range(n)]
        cps = []
        for j, (px, py) in enumerate(chips):
            for i in range(n):
                cps.append(pltpu.make_async_remote_copy(
                    src_ref=ins[i].at[2 * px + py], dst_ref=outs[i].at[mine], send_sem=send.at[i, j], recv_sem=recv.at[i, j],
                    device_id=(px, py, c), device_id_type=MESH))
        for cp in cps + own:
            cp.start()
        for j, (px, py) in enumerate(chips):
            for i in range(n):
                pltpu.make_async_remote_copy(
                    src_ref=ins[i].at[2 * px + py], dst_ref=outs[i].at[2 * px + py], send_sem=send.at[i, j],
                    recv_sem=recv.at[i, j], device_id=(px, py, c), device_id_type=MESH).wait()
        for o in own:
            o.wait()

    return pl.pallas_call(
        body, name=name, out_shape=[jax.ShapeDtypeStruct(a.shape, a.dtype) for a in ss],
        in_specs=[_HBM] * n, out_specs=[_HBM] * n,
        scratch_shapes=[pltpu.SemaphoreType.DMA((n, 3)), pltpu.SemaphoreType.DMA((n, 3)), pltpu.SemaphoreType.DMA((n,))],
        compiler_params=pltpu.CompilerParams(has_side_effects=True),
    )(*ss)


_SEM = pl.BlockSpec(memory_space=pltpu.SEMAPHORE)
_ANY = pl.BlockSpec(memory_space=pl.ANY)
_DATAFLOW = pltpu.SideEffectType.DATAFLOW_SIDE_EFFECTING


def _hbm(a):
    return pltpu.with_memory_space_constraint(a, pltpu.HBM)


def _gather_targets():
    x, y, c, chips = _chips()
    return 4 * x + 2 * y + c, [(x, y, 1 - c)] + [(*chip, c) for chip in chips]


def _gather_start(v, order, name):
    def body(v_ref, land_ref, order_ref, *rest):
        sems, token = rest[:8], rest[10]
        me, targets = _gather_targets()
        for k, to in enumerate(targets):
            pltpu.make_async_remote_copy(src_ref=v_ref, dst_ref=land_ref.at[me], send_sem=sems[k], recv_sem=sems[4 + k],
                                         device_id=to, device_id_type=MESH).start()
        token[...] = jnp.zeros_like(token)

    land = lax.empty((NDEV,) + v.shape, v.dtype)
    outs = pl.pallas_call(
        body, name=name,
        out_shape=(pltpu.SemaphoreType.DMA(()),) * 8 + (pltpu.HBM(v.shape, v.dtype), pltpu.HBM(land.shape, land.dtype),
                                                        jax.ShapeDtypeStruct((8, 128), F32)),
        in_specs=(_HBM, _HBM, _ANY), out_specs=(_SEM,) * 8 + (_HBM, _HBM, pl.BlockSpec(memory_space=pltpu.VMEM)),
        input_output_aliases={0: 8, 1: 9}, compiler_params=pltpu.CompilerParams(has_side_effects=_DATAFLOW),
    )(_hbm(v), _hbm(land), order)
    return outs[:8], outs[8], outs[9], outs[10]


def _gather_wait(sems, v_thru, land_thru, after, name):
    def body(v_ref, land_ref, *rest):
        sems_ = rest[:8]
        me, targets = _gather_targets()
        for k, to in enumerate(targets):
            cp = pltpu.make_async_remote_copy(src_ref=v_ref, dst_ref=land_ref.at[me], send_sem=sems_[k], recv_sem=sems_[4 + k],
                                              device_id=to, device_id_type=MESH)
            cp.wait_send()
            cp.wait_recv()

    return pl.pallas_call(
        body, name=name, out_shape=(pltpu.HBM(v_thru.shape, v_thru.dtype), pltpu.HBM(land_thru.shape, land_thru.dtype)),
        in_specs=(_HBM, _HBM) + (_SEM,) * 8 + (_ANY,), out_specs=(_HBM, _HBM), input_output_aliases={0: 0, 1: 1},
        compiler_params=pltpu.CompilerParams(has_side_effects=_DATAFLOW),
    )(v_thru, land_thru, *sems, after)


def _gather_finish(vs, lands, name):
    n = len(vs)

    def body(*refs):
        v_refs, outs = refs[:n], refs[2 * n:3 * n]
        send, recv, loc = refs[3 * n:]
        x, y, c, chips = _chips()
        own = [pltpu.make_async_copy(v_refs[i], outs[i].at[4 * x + 2 * y + c], loc.at[i]) for i in range(n)]
        fwd = [pltpu.make_async_remote_copy(src_ref=outs[i].at[4 * px + 2 * py + c], dst_ref=outs[i].at[4 * px + 2 * py + c],
                                            send_sem=send.at[i, j], recv_sem=recv.at[i, j], device_id=(x, y, 1 - c),
                                            device_id_type=MESH)
               for i in range(n) for j, (px, py) in enumerate(chips)]
        for cp in fwd + own:
            cp.start()
        for i in range(n):
            for j, (px, py) in enumerate(chips):
                slot = outs[i].at[4 * px + 2 * py + 1 - c]
                pltpu.make_async_remote_copy(src_ref=slot, dst_ref=slot, send_sem=send.at[i, j], recv_sem=recv.at[i, j],
                                             device_id=(x, y, 1 - c), device_id_type=MESH).wait()
        for o in own:
            o.wait()

    return pl.pallas_call(
        body, name=name, out_shape=[jax.ShapeDtypeStruct(a.shape, a.dtype) for a in lands],
        in_specs=[_HBM] * (2 * n), out_specs=[_HBM] * n, input_output_aliases={n + i: i for i in range(n)},
        scratch_shapes=[pltpu.SemaphoreType.DMA((n, 3)), pltpu.SemaphoreType.DMA((n, 3)), pltpu.SemaphoreType.DMA((n,))],
        compiler_params=pltpu.CompilerParams(has_side_effects=True),
    )(*vs, *lands)


def _chip_targets():
    x, y, c, chips = _chips()
    return 2 * x + y, [((px, py, c), 2 * px + py) for px, py in chips]


def _chipsum_start(s, land, order, name):
    def body(s_ref, land_ref, order_ref, *rest):
        sems, token = rest[:6], rest[8]
        mine, targets = _chip_targets()
        for k, (to, ch) in enumerate(targets):
            pltpu.make_async_remote_copy(src_ref=s_ref.at[ch], dst_ref=land_ref.at[mine], send_sem=sems[k], recv_sem=sems[3 + k],
                                         device_id=to, device_id_type=MESH).start()
        token[...] = jnp.zeros_like(token)

    outs = pl.pallas_call(
        body, name=name,
        out_shape=(pltpu.SemaphoreType.DMA(()),) * 6 + (pltpu.HBM(s.shape, s.dtype), pltpu.HBM(land.shape, land.dtype),
                                                        jax.ShapeDtypeStruct((8, 128), F32)),
        in_specs=(_HBM, _HBM, _ANY), out_specs=(_SEM,) * 6 + (_HBM, _HBM, pl.BlockSpec(memory_space=pltpu.VMEM)),
        input_output_aliases={0: 6, 1: 7}, compiler_params=pltpu.CompilerParams(has_side_effects=_DATAFLOW),
    )(_hbm(s), _hbm(land), order)
    return outs[:6], outs[6], outs[7], outs[8]


def _chipsum_wait(sems, s_thru, land_thru, after, name):
    def body(s_ref, land_ref, *rest):
        sems_ = rest[:6]
        mine, targets = _chip_targets()
        for k, (to, ch) in enumerate(targets):
            cp = pltpu.make_async_remote_copy(src_ref=s_ref.at[ch], dst_ref=land_ref.at[ch], send_sem=sems_[k], recv_sem=sems_[3 + k],
                                              device_id=to, device_id_type=MESH)
            cp.wait_send()
            cp.wait_recv()

    return pl.pallas_call(
        body, name=name, out_shape=(pltpu.HBM(s_thru.shape, s_thru.dtype), pltpu.HBM(land_thru.shape, land_thru.dtype)),
        in_specs=(_HBM, _HBM) + (_SEM,) * 6 + (_ANY,), out_specs=(_HBM, _HBM), input_output_aliases={0: 0, 1: 1},
        compiler_params=pltpu.CompilerParams(has_side_effects=_DATAFLOW),
    )(s_thru, land_thru, *sems, after)[1]


def _mm(a, b, *, mode, name, out_dtypes=(F32,), epilogue=None, extras=(), tm=1024, tn=1024, tk=2048,
        col_blocked_b=False, col_blocked_out=False, order=None):
    CB = 1024
    if col_blocked_b:
        assert mode in ("nn", "nt") and b.shape[2] == CB
        (M, K), N = a.shape, (b.shape[0] * CB if mode == "nn" else b.shape[1])
        tn, tk = (CB, tk) if mode == "nn" else (tn, CB)
    elif mode == "nn":
        (M, K), N = a.shape, b.shape[1]
    elif mode == "nt":
        (M, K), N = a.shape, b.shape[0]
    else:
        (K, M), N = a.shape, b.shape[1]
    if col_blocked_out:
        assert len(out_dtypes) == 1 and N % CB == 0
        tn = CB
    tm, tn, tk = min(tm, M), min(tn, N), min(tk, K)
    assert M % tm == 0 and N % tn == 0 and K % tk == 0, (M, N, K, tm, tn, tk)
    nk = K // tk
    ne, no = len(extras), len(out_dtypes)
    dims = {"nn": (((1,), (0,)), ((), ())), "nt": (((1,), (1,)), ((), ())), "tn": (((0,), (0,)), ((), ()))}[mode]

    no_ = 0 if order is None else 1

    def body(a_ref, b_ref, *rest):
        rest = rest[no_:]
        ex, outs = rest[:ne], rest[ne:ne + no]

        def finish(acc):
            res = epilogue(acc, *[e[...] for e in ex]) if epilogue is not None else (acc,)
            for o, r in zip(outs, res):
                o[...] = r.astype(o.dtype)

        part = lax.dot_general(a_ref[...], b_ref[...], dims, preferred_element_type=F32)
        if nk == 1:
            finish(part)
        else:
            acc_ref = rest[-1]
            k = pl.program_id(2)

            @pl.when(k == 0)
            def _():
                acc_ref[...] = part

            @pl.when(k > 0)
            def _():
                acc_ref[...] += part

            @pl.when(k == nk - 1)
            def _():
                finish(acc_ref[...])

    a_spec = {"nn": pl.BlockSpec((tm, tk), lambda i, j, k: (i, k)), "nt": pl.BlockSpec((tm, tk), lambda i, j, k: (i, k)),
              "tn": pl.BlockSpec((tk, tm), lambda i, j, k: (k, i))}[mode]
    b_spec = {"nn": pl.BlockSpec((tk, tn), lambda i, j, k: (k, j)), "nt": pl.BlockSpec((tn, tk), lambda i, j, k: (j, k)),
              "tn": pl.BlockSpec((tk, tn), lambda i, j, k: (k, j))}[mode]
    if col_blocked_b:
        b_spec = (pl.BlockSpec((None, tk, CB), lambda i, j, k: (j, k, 0)) if mode == "nn"
                  else pl.BlockSpec((None, tn, CB), lambda i, j, k: (k, j, 0)))
    e_spec = pl.BlockSpec((tm, tn), lambda i, j, k: (i, j))
    o_spec, o_dims = e_spec, (M, N)
    if col_blocked_out:
        o_spec, o_dims = pl.BlockSpec((None, tm, CB), lambda i, j, k: (j, i, 0)), (N // CB, M, CB)
    outs = pl.pallas_call(
        body, name=name, grid=(M // tm, N // tn, nk),
        in_specs=[a_spec, b_spec] + [_ANY] * no_ + [e_spec] * ne, out_specs=[o_spec] * no,
        out_shape=[jax.ShapeDtypeStruct(o_dims, dt) for dt in out_dtypes],
        scratch_shapes=[pltpu.VMEM((tm, tn), F32)] if nk > 1 else [],
        compiler_params=pltpu.CompilerParams(dimension_semantics=("parallel", "parallel", "arbitrary")),
    )(a, b, *([] if order is None else [order]), *extras)
    return outs if no > 1 else outs[0]


def _norm_fwd(x, g, sc, sh, name, resid=None):
    B, S, Dm = x.shape
    ts = min(S, 256)
    tok = pl.BlockSpec((None, ts, Dm), lambda b, i: (b, i, 0))
    row = pl.BlockSpec((None, 1, Dm), lambda b, i: (b, 0, 0))
    par = pl.BlockSpec((1, Dm), lambda b, i: (0, 0))

    def body(*refs):
        if resid is not None:
            x_ref, br_ref, gt_ref, g_ref, sc_ref, sh_ref, xo_ref, h_ref = refs
            xv = x_ref[...] + gt_ref[...] * br_ref[...]
            xo_ref[...] = xv
        else:
            x_ref, g_ref, sc_ref, sh_ref, h_ref = refs
            xv = x_ref[...]
        r = lax.rsqrt(jnp.mean(xv * xv, axis=-1, keepdims=True) + EPS)
        h_ref[...] = ((xv * r * g_ref[...]) * (1.0 + sc_ref[...]) + sh_ref[...]).astype(BF16)

    h_shape = jax.ShapeDtypeStruct((B, S, Dm), BF16)
    if resid is not None:
        return pl.pallas_call(body, name=name, grid=(B, S // ts), in_specs=[tok, tok, row, par, row, row],
                              out_specs=[tok, tok], out_shape=[jax.ShapeDtypeStruct((B, S, Dm), F32), h_shape],
                              )(x, resid[0], resid[1], g, sc, sh)
    return pl.pallas_call(body, name=name, grid=(B, S // ts), in_specs=[tok, par, row, row], out_specs=tok,
                          out_shape=h_shape)(x, g, sc, sh)


def _norm_bwd(x, g, name, *, sc=None, dh=None, dres=None, tgt=None, br=None, gate=None, x_is_prev=False):
    B, S, Dm = x.shape
    ts = min(S, 256)
    final = tgt is not None
    has_br = br is not None
    tok = pl.BlockSpec((None, ts, Dm), lambda b, i: (b, i, 0))
    row = pl.BlockSpec((None, 1, Dm), lambda b, i: (b, 0, 0))
    par = pl.BlockSpec((1, Dm), lambda b, i: (0, 0))
    ins, in_specs = [x, g], [tok, par]
    if final:
        ins, in_specs = ins + [tgt], in_specs + [tok]
    else:
        ins, in_specs = ins + [sc, dh], in_specs + [row, tok]
    if dres is not None:
        ins, in_specs = ins + [dres], in_specs + [tok]
    if has_br:
        ins, in_specs = ins + [br, gate], in_specs + [tok, row]
    n_in = len(ins)
    out_shape = [jax.ShapeDtypeStruct((B, S, Dm), F32), jax.ShapeDtypeStruct((1, Dm), F32)]
    out_specs = [tok, par]
    if final:
        out_shape.append(jax.ShapeDtypeStruct((1, 128), F32))
        out_specs.append(pl.BlockSpec((1, 128), lambda b, i: (0, 0)))
    else:
        out_shape += [jax.ShapeDtypeStruct((B, 1, Dm), F32)] * 2
        out_specs += [row, row]
    if has_br:
        out_shape += [jax.ShapeDtypeStruct((B, S, Dm), BF16), jax.ShapeDtypeStruct((B, 1, Dm), F32)]
        out_specs += [tok, row]

    def body(*refs):
        it = iter(refs[:n_in])
        outs = iter(refs[n_in:])
        x_ref, g_ref = next(it), next(it)
        b, i = pl.program_id(0), pl.program_id(1)
        first, first_row = (b == 0) & (i == 0), i == 0
        xv, gv = x_ref[...], g_ref[...]
        if x_is_prev:
            xv = xv + refs[n_in - 1][...] * refs[n_in - 2][...]
        r = lax.rsqrt(jnp.mean(xv * xv, axis=-1, keepdims=True) + EPS)
        n = xv * r
        dx_ref, dg_ref = next(outs), next(outs)

        def acc(ref, val, init):
            @pl.when(init)
            def _():
                ref[...] = val

            @pl.when(jnp.logical_not(init))
            def _():
                ref[...] += val

        if final:
            t_ref = next(it)
            loss_ref = next(outs)
            e = n * gv - t_ref[...]
            acc(loss_ref, jnp.zeros((1, 128), F32) + 0.5 * jnp.sum(e * e) / Dm, first)
            dyg = e * (1.0 / Dm)
        else:
            sc_ref, dh_ref = next(it), next(it)
            dsc_ref, dsh_ref = next(outs), next(outs)
            dhv = dh_ref[...]
            acc(dsh_ref, jnp.sum(dhv, axis=0, keepdims=True), first_row)
            acc(dsc_ref, jnp.sum(dhv * (n * gv), axis=0, keepdims=True), first_row)
            dyg = dhv * (1.0 + sc_ref[...])
        acc(dg_ref, jnp.sum(dyg * n, axis=0, keepdims=True), first)
        dn = dyg * gv
        dx = r * (dn - n * jnp.mean(dn * n, axis=-1, keepdims=True))
        if dres is not None:
            dx = dx + next(it)[...]
        dx_ref[...] = dx
        if has_br:
            br_ref, gt_ref = next(it), next(it)
            dbr_ref, dgt_ref = next(outs), next(outs)
            dbr_ref[...] = (dx * gt_ref[...]).astype(BF16)
            acc(dgt_ref, jnp.sum(dx * br_ref[...], axis=0, keepdims=True), first_row)

    outs = pl.pallas_call(body, name=name, grid=(B, S // ts), in_specs=in_specs, out_specs=out_specs, out_shape=out_shape,
                          compiler_params=pltpu.CompilerParams(dimension_semantics=("arbitrary", "arbitrary")))(*ins)
    res = dict(dx=outs[0], dg=outs[1])
    if final:
        res["loss"] = outs[2]
    else:
        res["dsc"], res["dsh"] = outs[2], outs[3]
    if has_br:
        res["dbr"], res["dgate"] = outs[-2], outs[-1]
    return res


def _gm_heads(vg, lng, lnb):
    res = []
    for h in range(GM_H):
        sl = slice(h * 128, (h + 1) * 128)
        vh = vg[:, sl]
        xc = vh - jnp.mean(vh, axis=-1, keepdims=True)
        rstd = lax.rsqrt(jnp.mean(xc * xc, axis=-1, keepdims=True) + 1e-5)
        xhat = xc * rstd
        res.append((xhat, rstd, xhat * lng[:, sl] + lnb[:, sl]))
    return res


def _gm_gate(heads, wt_ref, bsx, nch):
    cols = []
    for h in range(GM_H):
        vn = heads[h][2].astype(BF16)
        rows = [_dot(wt_ref[h], vn[c * CHUNK:(c + 1) * CHUNK]) + bsx[:, h * 128:(h + 1) * 128] for c in range(nch)]
        cols.append(jnp.concatenate(rows, axis=0) if nch > 1 else rows[0])
    return jnp.concatenate(cols, axis=1)


def _gm_specs(S):
    tb = min(S, 512)
    u = pl.BlockSpec((None, tb, GM_W), lambda b, i: (b, i, OFF["u"] // GM_W))
    v = pl.BlockSpec((None, tb, GM_W), lambda b, i: (b, i, OFF["v"] // GM_W))
    tok = pl.BlockSpec((None, tb, GM_W), lambda b, i: (b, i, 0))
    return tb, u, v, tok


def _gmlp_fwd(P, lng, lnb, wt, bsx, og, name):
    B, S, _ = P.shape
    tb, u_spec, v_spec, tok = _gm_specs(S)
    nch = tb // CHUNK

    def body(u_ref, v_ref, lng_ref, lnb_ref, wt_ref, bsx_ref, og_ref, o_ref):
        heads = _gm_heads(_gelu(v_ref[...]), lng_ref[...], lnb_ref[...])
        y = _gelu(u_ref[...]) * _gm_gate(heads, wt_ref, bsx_ref[...], nch)
        r = lax.rsqrt(jnp.mean(y * y, axis=-1, keepdims=True) + EPS)
        o_ref[...] = (y * r * og_ref[...]).astype(BF16)

    return pl.pallas_call(
        body, name=name, grid=(B, S // tb),
        in_specs=[u_spec, v_spec, _full((1, GM_W)), _full((1, GM_W)), _full((GM_H, 128, 128)), _full((128, GM_W)), _full((1, GM_W))],
        out_specs=tok, out_shape=jax.ShapeDtypeStruct((B, S, GM_W), BF16))(P, P, lng, lnb, wt, bsx, og)


def _gmlp_bwd(P, dcat, lng, lnb, wt, wtT, bsx, og, name):
    B, S, _ = P.shape
    tb, u_spec, v_spec, tok = _gm_specs(S)
    nch = tb // CHUNK
    do_spec = pl.BlockSpec((None, tb, GM_W), lambda b, i: (b, i, 0))

    def body(u_ref, v_ref, do_ref, lng_ref, lnb_ref, wt_ref, wtT_ref, bsx_ref, og_ref,
             du_ref, dv_ref, dlng_ref, dlnb_ref, dws_ref, dbsx_ref, dog_ref):
        first = (pl.program_id(0) == 0) & (pl.program_id(1) == 0)

        @pl.when(first)
        def _():
            for ref in (dlng_ref, dlnb_ref, dws_ref, dbsx_ref, dog_ref):
                ref[...] = jnp.zeros(ref.shape, F32)

        u, v, lng = u_ref[...], v_ref[...], lng_ref[...]
        ug = _gelu(u)
        heads = _gm_heads(_gelu(v), lng, lnb_ref[...])
        gate = _gm_gate(heads, wt_ref, bsx_ref[...], nch)
        y = ug * gate
        r = lax.rsqrt(jnp.mean(y * y, axis=-1, keepdims=True) + EPS)
        yn = y * r
        dout = do_ref[...]
        dog_ref[...] += jnp.sum(dout * yn, axis=0, keepdims=True)
        dyn = dout * og_ref[...]
        dy = r * (dyn - yn * jnp.mean(dyn * yn, axis=-1, keepdims=True))
        du_ref[...] = (dy * gate * _gelu_grad(u)).astype(BF16)
        dgate = dy * ug
        tril = lax.broadcasted_iota(jnp.int32, (128, 128), 0) >= lax.broadcasted_iota(jnp.int32, (128, 128), 1)
        dvg = []
        for h in range(GM_H):
            sl = slice(h * 128, (h + 1) * 128)
            xhat, rstd, vn = heads[h]
            vnb = vn.astype(BF16)
            dgh = dgate[:, sl]
            dgb = dgh.astype(BF16)
            dbs = jnp.zeros((128, 128), F32)
            dw = jnp.zeros((128, 128), F32)
            dvn = []
            for c in range(nch):
                rs = slice(c * CHUNK, (c + 1) * CHUNK)
                dbs = dbs + dgh[rs]
                dw = dw + _dot_nt(dgb[rs], vnb[rs])
                dvn.append(_dot(wtT_ref[h], dgb[rs]))
            dvn = jnp.concatenate(dvn, axis=0) if nch > 1 else dvn[0]
            dbsx_ref[:, sl] += dbs
            dws_ref[h] += jnp.where(tril, dw, 0.0)
            dlng_ref[:, sl] += jnp.sum(dvn * xhat, axis=0, keepdims=True)
            dlnb_ref[:, sl] += jnp.sum(dvn, axis=0, keepdims=True)
            dxh = dvn * lng[:, sl]
            dvg.append(rstd * (dxh - jnp.mean(dxh, axis=-1, keepdims=True) - xhat * jnp.mean(dxh * xhat, axis=-1, keepdims=True)))
        dv_ref[...] = (jnp.concatenate(dvg, axis=1) * _gelu_grad(v)).astype(BF16)

    p512, w3 = _full((1, GM_W)), _full((GM_H, 128, 128))
    return pl.pallas_call(
        body, name=name, grid=(B, S // tb),
        in_specs=[u_spec, v_spec, do_spec, p512, p512, w3, w3, _full((128, GM_W)), p512],
        out_specs=[tok, tok, p512, p512, w3, _full((128, GM_W)), p512],
        out_shape=[jax.ShapeDtypeStruct((B, S, GM_W), BF16)] * 2 + [
            jax.ShapeDtypeStruct((1, GM_W), F32), jax.ShapeDtypeStruct((1, GM_W), F32),
            jax.ShapeDtypeStruct((GM_H, 128, 128), F32), jax.ShapeDtypeStruct((128, GM_W), F32),
            jax.ShapeDtypeStruct((1, GM_W), F32)],
        compiler_params=pltpu.CompilerParams(dimension_semantics=("arbitrary", "arbitrary")),
    )(P, P, dcat, lng, lnb, wt, wtT, bsx, og)


def _lane_half():
    return lax.broadcasted_iota(jnp.int32, (128, 128), 1) // 64


def _att_stack(x, kvh, dtype):
    half = _lane_half()
    rows = []
    for g in range(4):
        i = kvh * 4 + g
        pair = x[:, (i // 2) * 128:(i // 2 + 1) * 128]
        if i % 2 != kvh:
            pair = pltpu.roll(pair, 64, 1)
        rows.append(jnp.where(half == kvh, pair, 0.0))
    return jnp.concatenate(rows, axis=0).astype(dtype)


def _att_unstack(pairs, y, kvh):
    half = _lane_half()
    for g in range(4):
        i = kvh * 4 + g
        piece = y[g * 128:(g + 1) * 128]
        if i % 2 != kvh:
            piece = pltpu.roll(piece, 64, 1)
        pairs[i // 2] = jnp.where(half == i % 2, piece, pairs[i // 2])
    return pairs


def _att_probs(qb, k2, st, sink_ref, kvh):
    qm = _att_stack(qb, kvh, BF16)
    s = _dot_nt(qm, k2) * (64 ** -0.5)
    qi = lax.broadcasted_iota(jnp.int32, (512, 256), 0) % 128
    kj = lax.broadcasted_iota(jnp.int32, (512, 256), 1)
    diff = qi + 128 - kj
    valid = (diff >= 0) & (diff < 128) & (st + kj - 128 >= 0)
    s = jnp.where(valid, s, NEG_INF)
    grp = lax.broadcasted_iota(jnp.int32, (512, 1), 0) // 128
    sink = jnp.zeros((512, 1), F32)
    for g in range(4):
        sink = jnp.where(grp == g, sink_ref[kvh * 4 + g], sink)
    m = jnp.maximum(jnp.max(s, axis=-1, keepdims=True), sink)
    e = jnp.exp(s - m)
    esink = jnp.exp(sink - m)
    inv = 1.0 / (jnp.sum(e, axis=-1, keepdims=True) + esink)
    return qm, e * inv, esink * inv


def _att_specs(S):
    q = pl.BlockSpec((None, S, ATT_W), lambda b: (b, 0, OFF["q"] // ATT_W))
    k = pl.BlockSpec((None, S, KV_W), lambda b: (b, 0, OFF["k"] // KV_W))
    v = pl.BlockSpec((None, S, KV_W), lambda b: (b, 0, OFF["vv"] // KV_W))
    tok = pl.BlockSpec((None, S, ATT_W), lambda b: (b, 0, 0))
    kv = pl.BlockSpec((None, S, KV_W), lambda b: (b, 0, 0))
    return q, k, v, tok, kv


_SMEM = pl.BlockSpec(memory_space=pltpu.SMEM)


def _attn_fwd(P, sinks, og, name):
    B, S, _ = P.shape
    q_spec, k_spec, v_spec, tok, _ = _att_specs(S)

    def body(q_ref, k_ref, v_ref, sink_ref, og_ref, o_ref, kpad, vpad):
        kpad[0:128, :] = jnp.zeros((128, KV_W), BF16)
        vpad[0:128, :] = jnp.zeros((128, KV_W), BF16)
        kpad[128:, :] = k_ref[...].astype(BF16)
        vpad[128:, :] = v_ref[...].astype(BF16)

        def step(n, carry):
            st = pl.multiple_of(n * 128, 128)
            qb = q_ref[pl.ds(st, 128), :]
            k2, v2 = kpad[pl.ds(st, 256), :], vpad[pl.ds(st, 256), :]
            pairs = [jnp.zeros((128, 128), F32)] * 4
            for kvh in range(2):
                _, p, _ = _att_probs(qb, k2, st, sink_ref, kvh)
                pairs = _att_unstack(pairs, _dot(p.astype(BF16), v2), kvh)
            o = jnp.concatenate(pairs, axis=1)
            r = lax.rsqrt(jnp.mean(o * o, axis=-1, keepdims=True) + EPS)
            o_ref[pl.ds(st, 128), :] = (o * r * og_ref[...]).astype(BF16)
            return carry

        lax.fori_loop(0, S // 128, step, 0)

    return pl.pallas_call(
        body, name=name, grid=(B,), in_specs=[q_spec, k_spec, v_spec, _SMEM, _full((1, ATT_W))], out_specs=tok,
        out_shape=jax.ShapeDtypeStruct((B, S, ATT_W), BF16),
        scratch_shapes=[pltpu.VMEM((S + 128, KV_W), BF16)] * 2)(P, P, P, sinks, og)


def _attn_bwd(P, dcat, sinks, og, name):
    B, S, _ = P.shape
    q_spec, k_spec, v_spec, tok, kv = _att_specs(S)
    do_spec = pl.BlockSpec((None, S, ATT_W), lambda b: (b, 0, GM_W // ATT_W))

    def body(q_ref, k_ref, v_ref, do_ref, sink_ref, og_ref, dq_ref, dk_ref, dv_ref, dsink_ref, dog_ref,
             kpad, vpad, dkpad, dvpad):
        @pl.when(pl.program_id(0) == 0)
        def _():
            dsink_ref[...] = jnp.zeros((8, 128), F32)
            dog_ref[...] = jnp.zeros((1, ATT_W), F32)

        kpad[0:128, :] = jnp.zeros((128, KV_W), BF16)
        vpad[0:128, :] = jnp.zeros((128, KV_W), BF16)
        kpad[128:, :] = k_ref[...].astype(BF16)
        vpad[128:, :] = v_ref[...].astype(BF16)
        dkpad[...] = jnp.zeros((S + 128, KV_W), F32)
        dvpad[...] = jnp.zeros((S + 128, KV_W), F32)
        half = _lane_half()
        head_row = lax.broadcasted_iota(jnp.int32, (8, 128), 0)

        def step(n, carry):
            st = pl.multiple_of(n * 128, 128)
            qb = q_ref[pl.ds(st, 128), :]
            k2, v2 = kpad[pl.ds(st, 256), :], vpad[pl.ds(st, 256), :]
            saved, pairs = [], [jnp.zeros((128, 128), F32)] * 4
            for kvh in range(2):
                qm, p, psink = _att_probs(qb, k2, st, sink_ref, kvh)
                o = _dot(p.astype(BF16), v2)
                saved.append((qm, p, psink, o))
                pairs = _att_unstack(pairs, o, kvh)
            o = jnp.concatenate(pairs, axis=1)
            r = lax.rsqrt(jnp.mean(o * o, axis=-1, keepdims=True) + EPS)
            on = o * r
            dout = do_ref[pl.ds(st, 128), :]
            dog_ref[...] += jnp.sum(dout * on, axis=0, keepdims=True)
            dyn = dout * og_ref[...]
            do = r * (dyn - on * jnp.mean(dyn * on, axis=-1, keepdims=True))
            dq_pairs = [jnp.zeros((128, 128), F32)] * 4
            dsink = jnp.zeros((8, 128), F32)
            for kvh in range(2):
                qm, p, psink, og_ = saved[kvh]
                dog = _att_stack(do, kvh, F32)
                delta = jnp.sum(dog * jnp.where(jnp.concatenate([half] * 4, axis=0) == kvh, og_, 0.0), axis=-1, keepdims=True)
                dogb, pb = dog.astype(BF16), p.astype(BF16)
                dvpad[pl.ds(st, 256), :] += _dot_tn(pb, dogb)
                dp = _dot_nt(dogb, v2)
                ds = (p * (dp - delta) * (64 ** -0.5)).astype(BF16)
                sd = psink * delta
                for g in range(4):
                    dsink = dsink - jnp.where(head_row == kvh * 4 + g, jnp.sum(sd[g * 128:(g + 1) * 128]), 0.0)
                dq_pairs = _att_unstack(dq_pairs, _dot(ds, k2), kvh)
                dkpad[pl.ds(st, 256), :] += _dot_tn(ds, qm)
            dsink_ref[...] += dsink
            dq_ref[pl.ds(st, 128), :] = jnp.concatenate(dq_pairs, axis=1).astype(BF16)
            return carry

        lax.fori_loop(0, S // 128, step, 0)
        dk_ref[...] = dkpad[128:, :].astype(BF16)
        dv_ref[...] = dvpad[128:, :].astype(BF16)

    return pl.pallas_call(
        body, name=name, grid=(B,),
        in_specs=[q_spec, k_spec, v_spec, do_spec, _SMEM, _full((1, ATT_W))],
        out_specs=[tok, kv, kv, _full((8, 128)), _full((1, ATT_W))],
        out_shape=[jax.ShapeDtypeStruct((B, S, ATT_W), BF16), jax.ShapeDtypeStruct((B, S, KV_W), BF16),
                   jax.ShapeDtypeStruct((B, S, KV_W), BF16), jax.ShapeDtypeStruct((8, 128), F32),
                   jax.ShapeDtypeStruct((1, ATT_W), F32)],
        scratch_shapes=[pltpu.VMEM((S + 128, KV_W), BF16)] * 2 + [pltpu.VMEM((S + 128, KV_W), F32)] * 2,
        compiler_params=pltpu.CompilerParams(dimension_semantics=("arbitrary",)),
    )(P, P, P, dcat, sinks, og)


CONV_TC = 256


def _conv_pre(ext, w_ref, b_ref, S):
    acc = b_ref[...] + w_ref[3:4, :] * ext[pl.ds(8, S), :]
    for k in range(1, 4):
        acc = acc + w_ref[3 - k:4 - k, :] * ext[pl.ds(8 - k, S), :]
    return acc


def _conv_fwd(P, w8, b, name):
    B, S, _ = P.shape
    nj = CONV_CH // CONV_TC
    x_spec = pl.BlockSpec((None, S, CONV_TC), lambda b_, j: (b_, 0, OFF["xbc"] // CONV_TC + j))
    tok = pl.BlockSpec((None, S, CONV_TC), lambda b_, j: (b_, 0, j))

    def body(x_ref, w_ref, b_ref, o_ref, ext):
        ext[0:8, :] = jnp.zeros((8, CONV_TC), F32)
        ext[8:, :] = x_ref[...]
        pre = _conv_pre(ext, w_ref, b_ref, S)
        o_ref[...] = pre * _sigmoid(pre)

    return pl.pallas_call(
        body, name=name, grid=(B, nj),
        in_specs=[x_spec, pl.BlockSpec((8, CONV_TC), lambda b_, j: (0, j)), pl.BlockSpec((1, CONV_TC), lambda b_, j: (0, j))],
        out_specs=tok, out_shape=jax.ShapeDtypeStruct((B, S, CONV_CH), F32),
        scratch_shapes=[pltpu.VMEM((S + 8, CONV_TC), F32)])(P, w8, b)


def _conv_bwd(P, dact, w8, b, name):
    B, S, _ = P.shape
    nj = CONV_CH // CONV_TC
    x_spec = pl.BlockSpec((None, S, CONV_TC), lambda j, b_: (b_, 0, OFF["xbc"] // CONV_TC + j))
    tok = pl.BlockSpec((None, S, CONV_TC), lambda j, b_: (b_, 0, j))
    w_spec = pl.BlockSpec((8, CONV_TC), lambda j, b_: (0, j))
    b_spec = pl.BlockSpec((1, CONV_TC), lambda j, b_: (0, j))

    def body(x_ref, d_ref, w_ref, b_ref, dx_ref, dw_ref, db_ref, ext, extd):
        @pl.when(pl.program_id(1) == 0)
        def _():
            dw_ref[...] = jnp.zeros((8, CONV_TC), F32)
            db_ref[...] = jnp.zeros((1, CONV_TC), F32)

        ext[0:8, :] = jnp.zeros((8, CONV_TC), F32)
        ext[8:, :] = x_ref[...]
        pre = _conv_pre(ext, w_ref, b_ref, S)
        sg = _sigmoid(pre)
        dpre = d_ref[...] * (sg * (1.0 + pre * (1.0 - sg)))
        extd[0:8, :] = jnp.zeros((8, CONV_TC), F32)
        extd[pl.ds(8, S), :] = dpre
        extd[pl.ds(8 + S, 8), :] = jnp.zeros((8, CONV_TC), F32)
        dx = w_ref[3:4, :] * dpre
        for k in range(1, 4):
            dx = dx + w_ref[3 - k:4 - k, :] * extd[pl.ds(8 + k, S), :]
        dx_ref[...] = dx.astype(BF16)
        db_ref[...] += jnp.sum(dpre, axis=0, keepdims=True)
        sub = lax.broadcasted_iota(jnp.int32, (8, CONV_TC), 0)
        dw = jnp.zeros((8, CONV_TC), F32)
        for i in range(4):
            dw = dw + jnp.where(sub == i, jnp.sum(dpre * ext[pl.ds(5 + i, S), :], axis=0, keepdims=True), 0.0)
        dw_ref[...] += dw

    return pl.pallas_call(
        body, name=name, grid=(nj, B), in_specs=[x_spec, tok, w_spec, b_spec], out_specs=[tok, w_spec, b_spec],
        out_shape=[jax.ShapeDtypeStruct((B, S, CONV_CH), BF16), jax.ShapeDtypeStruct((8, CONV_CH), F32),
                   jax.ShapeDtypeStruct((1, CONV_CH), F32)],
        scratch_shapes=[pltpu.VMEM((S + 8, CONV_TC), F32), pltpu.VMEM((S + 16, CONV_TC), F32)],
        compiler_params=pltpu.CompilerParams(dimension_semantics=("arbitrary", "arbitrary")),
    )(P, dact, w8, b)


def _ssd_consts():
    hd = np.arange(SSM_W) // SSM_HD
    E = (np.arange(128)[:, None] == hd[None, :]).astype(np.float32)
    tri = (np.arange(128)[:, None] >= np.arange(128)[None, :]).astype(np.float32)
    return jnp.asarray(E), jnp.asarray(E.T), jnp.asarray(tri), jnp.asarray(tri.T)


def _ssd_pre(xa, dtraw, bias, alog, E, ET, tri):
    lane = lax.broadcasted_iota(jnp.int32, (128, 128), 1)
    pre = dtraw + bias
    dtp = jnp.where(lane < SSM_H, jnp.maximum(pre, 0.0) + jnp.log(1.0 + jnp.exp(-jnp.abs(pre))), 0.0)
    a = -jnp.exp(alog)
    acs = _dot(tri, dtp * a, HI)
    acsT = acs.T
    dtE, acsE = _dot(dtp, E, HI), _dot(acs, E, HI)
    cdcol = jnp.exp(_dot(ET, acsT, HI)[:, 127:128])
    X = xa[:, :SSM_W]
    xdt = X * dtE
    wE = jnp.exp(acsE[127:128, :] - acsE)
    eE = jnp.exp(acsE)
    return dict(pre=pre, dtp=dtp, a=a, acs=acs, acsT=acsT, dtE=dtE, acsE=acsE, cdcol=cdcol, X=X, xdt=xdt, wE=wE, eE=eE)


def _ssd_decay(c, h):
    lm = lax.broadcasted_iota(jnp.int32, (128, 128), 0) >= lax.broadcasted_iota(jnp.int32, (128, 128), 1)
    return jnp.exp(jnp.where(lm, c["acs"][:, h:h + 1] - c["acsT"][h:h + 1, :], NEG_INF))


def _ssd_pair_operands(c, CB, h0):
    lane = lax.broadcasted_iota(jnp.int32, (128, 128), 1)
    L0, L1 = _ssd_decay(c, h0), _ssd_decay(c, h0 + 1)
    M = jnp.concatenate([CB * L0, CB * L1], axis=1).astype(BF16)
    xp = c["xdt"][:, h0 * 64:h0 * 64 + 128]
    BD = jnp.concatenate([jnp.where(lane < 64, xp, 0.0), jnp.where(lane >= 64, xp, 0.0)], axis=0).astype(BF16)
    return L0, L1, M, BD


def _ssd_y(c, xa, state_ref, dskipE):
    per_group, ys = [], []
    for g in range(SSM_G):
        gs = slice(g * 512, (g + 1) * 512)
        Bb = xa[:, SSM_W + g * 128:SSM_W + (g + 1) * 128].astype(BF16)
        Cb = xa[:, SSM_W + 256 + g * 128:SSM_W + 256 + (g + 1) * 128].astype(BF16)
        CB = _dot_nt(Cb, Bb)
        Sg = state_ref[gs, :]
        yoff = _dot_nt(Cb, Sg.astype(BF16)) * c["eE"][:, gs]
        ydiag, pairs = [], []
        for j in range(4):
            ops = _ssd_pair_operands(c, CB, g * 8 + 2 * j)
            pairs.append(ops)
            ydiag.append(_dot(ops[2], ops[3]))
        ys.append(jnp.concatenate(ydiag, axis=1) + yoff)
        per_group.append(dict(Bb=Bb, Cb=Cb, CB=CB, Sg=Sg, yoff=yoff, pairs=pairs))
    Y = jnp.concatenate(ys, axis=1) + c["X"] * dskipE
    return Y, per_group


def _ssd_specs(S, rev):
    nc = S // CHUNK
    cm = (lambda b, i: (b, nc - 1 - i)) if rev else (lambda b, i: (b, i))
    xa = pl.BlockSpec((None, CHUNK, CONV_CH), lambda b, i: cm(b, i) + (0,))
    z = pl.BlockSpec((None, CHUNK, SSM_W), lambda b, i: cm(b, i) + (OFF["z"] // SSM_W,))
    dt = pl.BlockSpec((None, CHUNK, 128), lambda b, i: cm(b, i) + (OFF["dt"] // 128,))
    tok = pl.BlockSpec((None, CHUNK, SSM_W), lambda b, i: cm(b, i) + (0,))
    st = pl.BlockSpec((None, None, SSM_W, 128), lambda b, i: cm(b, i) + (0, 0))
    return nc, xa, z, dt, tok, st


def _ssd_fwd(xact, P, bias, alog, dskipE, ng, name):
    B, S, _ = P.shape
    nc, xa_spec, z_spec, dt_spec, tok, st_spec = _ssd_specs(S, False)
    E, ET, tri, _ = _ssd_consts()

    def body(xa_ref, z_ref, dt_ref, bias_ref, alog_ref, dsk_ref, ng_ref, E_ref, ET_ref, tri_ref, o_ref, sp_ref, state):
        @pl.when(pl.program_id(1) == 0)
        def _():
            state[...] = jnp.zeros((SSM_W, 128), F32)

        sp_ref[...] = state[...]
        xa = xa_ref[...]
        c = _ssd_pre(xa, dt_ref[...], bias_ref[...], alog_ref[...], E_ref[...], ET_ref[...], tri_ref[...])
        Y, groups = _ssd_y(c, xa, state, dsk_ref[...])
        Z = (c["xdt"] * c["wE"]).astype(BF16)
        for g in range(SSM_G):
            gs = slice(g * 512, (g + 1) * 512)
            state[gs, :] = groups[g]["Sg"] * c["cdcol"][gs, :] + _dot_tn(Z[:, gs], groups[g]["Bb"])
        zv = z_ref[...]
        yz = Y * (zv * _sigmoid(zv))
        outs = []
        for g in range(SSM_G):
            yg = yz[:, g * 512:(g + 1) * 512]
            outs.append(yg * lax.rsqrt(jnp.mean(yg * yg, axis=-1, keepdims=True) + EPS))
        o_ref[...] = (jnp.concatenate(outs, axis=1) * ng_ref[...]).astype(BF16)

    return pl.pallas_call(
        body, name=name, grid=(B, nc),
        in_specs=[xa_spec, z_spec, dt_spec, _full((1, 128)), _full((1, 128)), _full((1, SSM_W)), _full((1, SSM_W)),
                  _full((128, SSM_W)), _full((SSM_W, 128)), _full((128, 128))],
        out_specs=[tok, st_spec],
        out_shape=[jax.ShapeDtypeStruct((B, S, SSM_W), BF16), jax.ShapeDtypeStruct((B, nc, SSM_W, 128), F32)],
        scratch_shapes=[pltpu.VMEM((SSM_W, 128), F32)],
        compiler_params=pltpu.CompilerParams(dimension_semantics=("arbitrary", "arbitrary")),
    )(xact, P, P, bias, alog, dskipE, ng, E, ET, tri)


def _ssd_bwd(xact, P, sprev, dcat, bias, alog, dskipE, ng, name):
    B, S, _ = P.shape
    nc, xa_spec, z_spec, dt_spec, tok, st_spec = _ssd_specs(S, True)
    do_spec = pl.BlockSpec((None, CHUNK, SSM_W), lambda b, i: (b, nc - 1 - i, 1))
    E, ET, tri, triT = _ssd_consts()
    dt_out = pl.BlockSpec((None, CHUNK, 128), lambda b, i: (b, nc - 1 - i, 0))

    def body(xa_ref, z_ref, dt_ref, sp_ref, do_ref, bias_ref, alog_ref, dsk_ref, ng_ref, E_ref, ET_ref, tri_ref, triT_ref,
             dxa_ref, dz_ref, ddt_ref, dbias_ref, dalog_ref, ddsk_ref, dng_ref, dstate):
        first = (pl.program_id(0) == 0) & (pl.program_id(1) == 0)

        @pl.when(first)
        def _():
            for ref in (dbias_ref, dalog_ref, ddsk_ref, dng_ref):
                ref[...] = jnp.zeros(ref.shape, F32)

        @pl.when(pl.program_id(1) == 0)
        def _():
            dstate[...] = jnp.zeros((SSM_W, 128), F32)

        xa, ETm = xa_ref[...], ET_ref[...]
        c = _ssd_pre(xa, dt_ref[...], bias_ref[...], alog_ref[...], E_ref[...], ETm, tri_ref[...])
        Y, groups = _ssd_y(c, xa, sp_ref, dsk_ref[...])
        X, xdt = c["X"], c["xdt"]
        zv = z_ref[...]
        sg = _sigmoid(zv)
        zs = zv * sg
        yz = Y * zs
        dout = do_ref[...]
        dyz = []
        for g in range(SSM_G):
            gs = slice(g * 512, (g + 1) * 512)
            yg = yz[:, gs]
            r = lax.rsqrt(jnp.mean(yg * yg, axis=-1, keepdims=True) + EPS)
            yn = yg * r
            dng_ref[:, gs] += jnp.sum(dout[:, gs] * yn, axis=0, keepdims=True)
            dyn = dout[:, gs] * ng_ref[:, gs]
            dyz.append(r * (dyn - yn * jnp.mean(dyn * yn, axis=-1, keepdims=True)))
        dyz = jnp.concatenate(dyz, axis=1)
        dz_ref[...] = (dyz * Y * (sg * (1.0 + zv * (1.0 - sg)))).astype(BF16)
        dY = dyz * zs
        ddsk_ref[...] += jnp.sum(dY * X, axis=0, keepdims=True)
        dX = dY * dsk_ref[...]
        lane = lax.broadcasted_iota(jnp.int32, (128, 128), 1)
        sub = lax.broadcasted_iota(jnp.int32, (128, 128), 0)
        colform = jnp.zeros((128, 128), F32)
        rowform = jnp.zeros((128, 128), F32)
        dxdt, gacsE, dBC = [], [], []
        for g in range(SSM_G):
            gs = slice(g * 512, (g + 1) * 512)
            G = groups[g]
            Bb, Cb, CB, Sg = G["Bb"], G["Cb"], G["CB"], G["Sg"]
            dYg = dY[:, gs]
            dQ = (dYg * c["eE"][:, gs]).astype(BF16)
            dSn = dstate[gs, :]
            dSnb = dSn.astype(BF16)
            cd = c["cdcol"][gs, :]
            dC = _dot(dQ, Sg.astype(BF16))
            dSprev = _dot_tn(dQ, Cb) + dSn * cd
            hcol = jnp.sum(_dot(E_ref[:, gs], dSn * Sg * cd, HI), axis=-1, keepdims=True)
            rowform = rowform + jnp.where(lane == 127, hcol, 0.0)
            Zg = xdt[:, gs] * c["wE"][:, gs]
            dZ = _dot_nt(Bb, dSnb)
            dB = _dot(Zg.astype(BF16), dSnb)
            U = dZ * Zg
            ga = dYg * G["yoff"] - U
            ga = ga + jnp.where(lax.broadcasted_iota(jnp.int32, (128, 512), 0) == 127, jnp.sum(U, axis=0, keepdims=True), 0.0)
            gacsE.append(ga)
            dxg = [None] * 4
            dCB = jnp.zeros((128, 128), F32)
            for j in range(4):
                h0 = g * 8 + 2 * j
                L0, L1, M, BD = G["pairs"][j]
                dYp = dYg[:, j * 128:(j + 1) * 128].astype(BF16)
                dM = _dot_nt(dYp, BD)
                dBD = _dot_tn(M, dYp)
                dxg[j] = jnp.where(lane < 64, dBD[:128], dBD[128:])
                for t, (h, L) in enumerate(((h0, L0), (h0 + 1, L1))):
                    dMh = dM[:, t * 128:(t + 1) * 128]
                    dCB = dCB + dMh * L
                    Gh = dMh * CB * L
                    colform = colform + jnp.where(lane == h, jnp.sum(Gh, axis=1, keepdims=True), 0.0)
                    rowform = rowform - jnp.where(sub == h, jnp.sum(Gh, axis=0, keepdims=True), 0.0)
            dCBb = dCB.astype(BF16)
            dC = dC + _dot(dCBb, Bb)
            dB = dB + _dot_tn(dCBb, Cb)
            dxdt.append(jnp.concatenate(dxg, axis=1) + dZ * c["wE"][:, gs])
            dBC.append((dB, dC))
            dstate[gs, :] = dSprev
        dxdt = jnp.concatenate(dxdt, axis=1)
        dX = dX + dxdt * c["dtE"]
        ddt = _dot(dxdt * X, ETm, HI)
        dacs = colform + rowform.T + _dot(jnp.concatenate(gacsE, axis=1), ETm, HI)
        dda = _dot(triT_ref[...], dacs, HI)
        ddt = ddt + dda * c["a"]
        dalog_ref[...] += jnp.sum(dda * c["dtp"], axis=0, keepdims=True) * c["a"]
        ddtraw = jnp.where(lane < SSM_H, ddt * _sigmoid(c["pre"]), 0.0)
        dbias_ref[...] += jnp.sum(ddtraw, axis=0, keepdims=True)
        ddt_ref[...] = ddtraw.astype(BF16)
        dxa_ref[...] = jnp.concatenate([dX, dBC[0][0], dBC[1][0], dBC[0][1], dBC[1][1]], axis=1)

    p128, p1k = _full((1, 128)), _full((1, SSM_W))
    return pl.pallas_call(
        body, name=name, grid=(B, nc),
        in_specs=[xa_spec, z_spec, dt_spec, st_spec, do_spec, p128, p128, p1k, p1k,
                  _full((128, SSM_W)), _full((SSM_W, 128)), _full((128, 128)), _full((128, 128))],
        out_specs=[xa_spec, tok, dt_out, p128, p128, p1k, p1k],
        out_shape=[jax.ShapeDtypeStruct((B, S, CONV_CH), F32), jax.ShapeDtypeStruct((B, S, SSM_W), BF16),
                   jax.ShapeDtypeStruct((B, S, 128), BF16), jax.ShapeDtypeStruct((1, 128), F32),
                   jax.ShapeDtypeStruct((1, 128), F32), jax.ShapeDtypeStruct((1, SSM_W), F32),
                   jax.ShapeDtypeStruct((1, SSM_W), F32)],
        scratch_shapes=[pltpu.VMEM((SSM_W, 128), F32)],
        compiler_params=pltpu.CompilerParams(dimension_semantics=("arbitrary", "arbitrary")),
    )(xact, P, P, sprev, dcat, bias, alog, dskipE, ng, E, ET, tri, triT)


def _adamw(w, parts, m, v, name, tr=512, row0=0, prev=None):
    Rtot, C = w.shape
    ns, R = parts.shape[0], parts.shape[1]
    tr = min(tr, R)
    assert R % tr == 0 and row0 % tr == 0
    off = row0 // tr
    c1 = 1.0 / (1.0 - ADAM_B1 ** ADAM_STEP)
    c2 = 1.0 / (1.0 - ADAM_B2 ** ADAM_STEP)

    def body(w_ref, p_ref, m_ref, v_ref, *rest):
        g_ref, d_ref, mo_ref, vo_ref = rest[-4:]
        g = p_ref[0].astype(F32)
        for s in range(1, ns):
            g = g + p_ref[s].astype(F32)
        mn = ADAM_B1 * m_ref[...] + (1.0 - ADAM_B1) * g
        vn = ADAM_B2 * v_ref[...] + (1.0 - ADAM_B2) * (g * g)
        g_ref[...] = g
        mo_ref[...] = mn
        vo_ref[...] = vn
        d_ref[...] = -ADAM_LR * ((mn * c1) / (jnp.sqrt(vn * c2) + ADAM_EPS) + ADAM_WD * w_ref[...])

    blk = pl.BlockSpec((tr, C), lambda i: (i + off, 0))
    extra = [] if prev is None else list(prev)
    return pl.pallas_call(
        body, name=name, grid=(R // tr,),
        in_specs=[blk, pl.BlockSpec((ns, tr, C), lambda i: (0, i, 0)), blk, blk] + [pl.BlockSpec(memory_space=pl.ANY)] * len(extra),
        out_specs=[blk] * 4, out_shape=[jax.ShapeDtypeStruct((Rtot, C), F32)] * 4,
        input_output_aliases={4 + k: k for k in range(len(extra))})(w, parts, m, v, *extra)


_SMALL = ("ada_b", "norm1_g", "gm_ln_g", "gm_ln_b", "gm_ws", "gm_bs", "gm_norm_g", "attn_sinks", "attn_norm_g", "conv_b",
          "dt_bias", "a_log", "d_skip", "ssm_norm_g", "norm2_g", "final_norm_g")


def _pack(arrs):
    flat = []
    for a in arrs:
        f = a.reshape(-1).astype(F32)
        flat.append(jnp.pad(f, (0, (-f.shape[0]) % 1024)))
    return jnp.concatenate(flat).reshape(-1, 128)


def _unpack(pack, like):
    out, o = [], 0
    flat = pack.reshape(-1)
    for a in like:
        n = int(np.prod(a.shape))
        out.append(flat[o:o + n].reshape(a.shape))
        o += n + (-n) % 1024
    return out


def kernel(x, c, ada_w, ada_b, norm1_g, w_in, gm_ln_g, gm_ln_b, gm_ws, gm_bs, gm_norm_g, attn_sinks, attn_norm_g, conv_w, conv_b, dt_bias, a_log, d_skip, ssm_norm_g, w_out, norm2_g, w_mlp1, w_mlp2, final_norm_g, loss_target, m_ada_w, m_ada_b, m_norm1_g, m_w_in, m_gm_ln_g, m_gm_ln_b, m_gm_ws, m_gm_bs, m_gm_norm_g, m_attn_sinks, m_attn_norm_g, m_conv_w, m_conv_b, m_dt_bias, m_a_log, m_d_skip, m_ssm_norm_g, m_w_out, m_norm2_g, m_w_mlp1, m_w_mlp2, m_final_norm_g, v_ada_w, v_ada_b, v_norm1_g, v_w_in, v_gm_ln_g, v_gm_ln_b, v_gm_ws, v_gm_bs, v_gm_norm_g, v_attn_sinks, v_attn_norm_g, v_conv_w, v_conv_b, v_dt_bias, v_a_log, v_d_skip, v_ssm_norm_g, v_w_out, v_norm2_g, v_w_mlp1, v_w_mlp2, v_final_norm_g):
    args = dict(locals())
    B, S, _ = x.shape
    T = B * S
    L = DEPTH
    me = 4 * lax.axis_index("x") + 2 * lax.axis_index("y") + lax.axis_index("c")

    gath = _gather2([c, conv_w], "ag_c")
    big = ("w_in", "w_out", "w_mlp1", "w_mlp2")
    tok, pending = gath[0], {}
    for l in range(L):
        for n in big:
            sems, v_thru, land_thru, tok = _gather_start(args[n][l].astype(BF16), tok, f"ag_start_{n}{l}")
            pending[n, l] = (sems, v_thru, land_thru)

    def gathered(n, l, after):
        v_done, land = _gather_wait(*pending.pop((n, l)), after, f"ag_wait_{n}{l}")
        return _gather_finish([v_done], [land], f"ag_fin_{n}{l}")[0]

    c_all = gath[0].reshape(NDEV * B, D) + tok[0, 0]
    c_act = (c_all * jax.nn.sigmoid(c_all)).astype(BF16)
    nb_rows = c_act.shape[0]
    c_pad = jnp.pad(c_act, ((0, 128 - nb_rows), (0, 0)))
    adw = ada_w.astype(BF16)
    mod_part = jnp.stack([_mm(c_pad, adw[l], mode="nn", name=f"mod{l}", tn=768)[:nb_rows] for l in range(L)])
    mod_all = _exchange([mod_part], "ag_mod", False)[0]
    mod_mine = lax.dynamic_slice_in_dim(mod_all, me * B, B, axis=2)
    mod = jnp.transpose(mod_mine, (1, 2, 0, 3)).reshape(L, B, 6 * D) + ada_b[:, None, :]
    mods = [[mod[l][:, None, i * D:(i + 1) * D] for i in range(6)] for l in range(L)]

    win_g, wout_g, w1_g, w2_g = [None] * L, [None] * L, [None] * L, [None] * L

    tril = jnp.tril(jnp.ones((128, 128), F32))
    row = lambda a: a.reshape(1, -1)
    pad128 = lambda a: jnp.pad(a.reshape(1, -1), ((0, 0), (0, 128 - a.shape[-1])))
    small = []
    for l in range(L):
        wt = gm_ws[l] * tril
        small.append(dict(
            lng=row(gm_ln_g[l]), lnb=row(gm_ln_b[l]), wt=wt.astype(BF16), wtT=jnp.swapaxes(wt, 1, 2).astype(BF16),
            bsx=jnp.repeat(gm_bs[l].T, 128, axis=1), gog=row(gm_norm_g[l]), sinks=attn_sinks[l], aog=row(attn_norm_g[l]),
            bias=pad128(dt_bias[l]), alog=pad128(a_log[l]), dskE=jnp.repeat(d_skip[l], SSM_HD).reshape(1, SSM_W),
            sng=row(ssm_norm_g[l]), cb=row(conv_b[l])))
    convw_all = jnp.transpose(gath[1], (1, 2, 0, 3)).reshape(L, 4, CONV_CH)
    convw8 = jnp.pad(convw_all, ((0, 0), (0, 4), (0, 0)))

    saved = []
    xl = x
    h = _norm_fwd(xl, row(norm1_g[0]), mods[0][1], mods[0][0], "norm1_f0")
    for l in range(L):
        sm = small[l]
        win_g[l] = _to_work_cols(jnp.transpose(gathered("w_in", l, h), (1, 0, 2)).reshape(D, IN_W))
        P = _mm(h.reshape(T, D), win_g[l], mode="nn", name=f"proj_in{l}", tn=1536).reshape(B, S, PW)
        out_a = _gmlp_fwd(P, sm["lng"], sm["lnb"], sm["wt"], sm["bsx"], sm["gog"], f"gmlp_f{l}")
        out_b = _attn_fwd(P, sm["sinks"], sm["aog"], f"attn_f{l}")
        xact = _conv_fwd(P, convw8[l], sm["cb"], f"conv_f{l}")
        out_c, sprev = _ssd_fwd(xact, P, sm["bias"], sm["alog"], sm["dskE"], sm["sng"], f"ssd_f{l}")
        cat = jnp.concatenate([out_a, out_b, out_c], axis=-1)
        wout_g[l] = gathered("w_out", l, cat).reshape(D, D)
        mix = _mm(cat.reshape(T, D), wout_g[l], mode="nn", name=f"proj_out{l}").reshape(B, S, D)
        x_mid, h2 = _norm_fwd(xl, row(norm2_g[l]), mods[l][4], mods[l][3], f"norm2_f{l}", resid=(mix, mods[l][2]))
        w1_g[l] = gathered("w_mlp1", l, h2)
        a_act, r_act = _mm(h2.reshape(T, D), w1_g[l], mode="nn", name=f"mlp1_{l}", out_dtypes=(BF16, BF16), col_blocked_b=True,
                           epilogue=lambda acc: (acc, jnp.square(jnp.maximum(acc, 0.0))))
        w2_g[l] = gathered("w_mlp2", l, r_act).reshape(DFF, D)
        m2 = _mm(r_act, w2_g[l], mode="nn", name=f"mlp2_{l}").reshape(B, S, D)
        saved.append(dict(x_in=xl, h=h, P=P, xact=xact, sprev=sprev, cat=cat, mix=mix, x_mid=x_mid, h2=h2, a=a_act, r=r_act, m2=m2))
        if l + 1 < L:
            xl, h = _norm_fwd(x_mid, row(norm1_g[l + 1]), mods[l + 1][1], mods[l + 1][0], f"norm1_f{l + 1}", resid=(m2, mods[l][5]))

    sv = saved[L - 1]
    nb = _norm_bwd(sv["x_mid"], row(final_norm_g), "final_b", tgt=loss_target, br=sv["m2"], gate=mods[L - 1][5], x_is_prev=True)
    loss_part, g_final = nb["loss"], nb["dg"]
    dmod, gsm, gconvw = [None] * L, [None] * L, [None] * L
    core = lax.axis_index("c").astype(jnp.int32).reshape(1)
    reducing = []

    def reduce_start(n, l, p, order):
        from_sib = _pair_exchange([p], f"rs_pair_{n}{l}")[0]
        s, land = _pair_add(p, from_sib, core, f"rs_add_{n}{l}")
        sems, s_thru, land_thru, token = _chipsum_start(s, land, order, f"rs_start_{n}{l}")
        reducing.append((n, l, sems, s_thru, land_thru))
        return token

    for l in reversed(range(L)):
        sv, sm = saved[l], small[l]
        dm2, dxo, dg2 = nb["dbr"].reshape(T, D), nb["dx"], nb["dgate"]
        da = _mm(dm2, w2_g[l], mode="nt", name=f"mlp2_dx{l}", out_dtypes=(BF16,), extras=(sv["a"],),
                 epilogue=lambda acc, a: (acc * (2.0 * jnp.maximum(a.astype(F32), 0.0)),))
        dw2 = _mm(sv["r"], dm2, mode="tn", name=f"mlp2_dw{l}", out_dtypes=(BF16,), tk=1024).reshape(4, 2, DFF // NDEV, D)
        tok = reduce_start("w_mlp2", l, dw2, da)
        dh2 = _mm(da, w1_g[l], mode="nt", name=f"mlp1_dx{l}", col_blocked_b=True, order=tok).reshape(B, S, D)
        dw1 = _mm(sv["h2"].reshape(T, D), da, mode="tn", name=f"mlp1_dw{l}", out_dtypes=(BF16,), tk=1024,
                  col_blocked_out=True).reshape(4, 2, D, DFF // NDEV)
        tok = reduce_start("w_mlp1", l, dw1, dh2)
        nb2 = _norm_bwd(sv["x_mid"], row(norm2_g[l]) + tok[0, 0], f"norm2_b{l}", sc=mods[l][4], dh=dh2, dres=dxo, br=sv["mix"],
                        gate=mods[l][2])
        dmix = nb2["dbr"].reshape(T, D)
        dwo = _mm(sv["cat"].reshape(T, D), dmix, mode="tn", name=f"proj_out_dw{l}", out_dtypes=(BF16,),
                  tk=1024).reshape(4, 2, D // NDEV, D)
        tok = reduce_start("w_out", l, dwo, dmix)
        dcat = _mm(dmix, wout_g[l], mode="nt", name=f"proj_out_dx{l}", order=tok).reshape(B, S, D)
        du, dv, dlng, dlnb, dws, dbsx, dgog = _gmlp_bwd(sv["P"], dcat, sm["lng"], sm["lnb"], sm["wt"], sm["wtT"], sm["bsx"],
                                                        sm["gog"], f"gmlp_b{l}")
        dq, dk, dvv, dsink, daog = _attn_bwd(sv["P"], dcat, sm["sinks"], sm["aog"], f"attn_b{l}")
        dxa, dz, ddt, dbias, dalog, ddsk, dsng = _ssd_bwd(sv["xact"], sv["P"], sv["sprev"], dcat, sm["bias"], sm["alog"],
                                                          sm["dskE"], sm["sng"], f"ssd_b{l}")
        dxbc, dcw, dcb = _conv_bwd(sv["P"], dxa, convw8[l], sm["cb"], f"conv_b{l}")
        dP = jnp.concatenate([dxbc, du, dz, dv, dq, dk, dvv, ddt, jnp.zeros((B, S, PW - OFF["dt"] - 128), BF16)],
                             axis=-1).reshape(T, PW)
        dwin = _mm(sv["h"].reshape(T, D), dP, mode="tn", name=f"proj_in_dw{l}", out_dtypes=(BF16,), tn=1536, tk=1024)
        dwin = jnp.transpose(_from_work_cols(dwin).reshape(D, NDEV, IN_W // NDEV), (1, 0, 2)).reshape(4, 2, D, IN_W // NDEV)
        tok = reduce_start("w_in", l, dwin, dP)
        dh = _mm(dP, win_g[l], mode="nt", name=f"proj_in_dx{l}", tk=1536, order=tok).reshape(B, S, D)
        nb = _norm_bwd(sv["x_in"], row(norm1_g[l]), f"norm1_b{l}", sc=mods[l][1], dh=dh, dres=nb2["dx"],
                       br=saved[l - 1]["m2"] if l > 0 else None, gate=mods[l - 1][5] if l > 0 else None)
        dmod[l] = jnp.concatenate([nb["dsh"], nb["dsc"], nb2["dgate"], nb2["dsh"], nb2["dsc"], dg2], axis=-1)
        gconvw[l] = dcw[:4]
        gsm[l] = dict(
            ada_b=jnp.sum(dmod[l], axis=(0, 1)), norm1_g=nb["dg"], gm_ln_g=dlng, gm_ln_b=dlnb, gm_ws=dws,
            gm_bs=dbsx.reshape(128, GM_H, 128).sum(-1).T, gm_norm_g=dgog, attn_sinks=dsink[:, 0], attn_norm_g=daog,
            conv_b=dcb, dt_bias=dbias[0, :SSM_H], a_log=dalog[0, :SSM_H], d_skip=ddsk.reshape(SSM_H, SSM_HD).sum(-1),
            ssm_norm_g=dsng, norm2_g=nb2["dg"])
    grad_x = nb["dx"]

    per_layer = [n for n in _SMALL if n != "final_norm_g"]
    g_small = [jnp.stack([gsm[l][n].reshape(args[n].shape[1:]) for l in range(L)]) for n in per_layer] + [g_final.reshape(D)]
    zc = jnp.zeros((L, 4, CONV_CH), F32)
    z1 = jnp.zeros((1, 128), F32)
    gpack = _pack([loss_part] + g_small + [jnp.stack(gconvw)])
    got = _exchange([jnp.stack(dmod).reshape(L, B, 6 * D), gpack], "ag_small", False)
    like = [z1] + [args[n] for n in _SMALL] + [zc]
    packs = [_pack([z1] + [args[p + n] for n in _SMALL] + [zc]) for p in ("", "m_", "v_")]
    sres = [_unpack(p, like) for p in _adamw(packs[0], got[1], packs[1], packs[2], "adamw_small", tr=gpack.shape[0])]
    res = {n: [r[1 + i] for r in sres] for i, n in enumerate(_SMALL)}
    loss = sres[0][0][0, 0]
    gcw = lax.dynamic_slice_in_dim(sres[0][-1], me * (CONV_CH // NDEV), CONV_CH // NDEV, axis=2)

    def update(name, parts, tr):
        w = args[name]
        r = _adamw(w.reshape(-1, w.shape[-1]), parts, args["m_" + name].reshape(-1, w.shape[-1]),
                   args["v_" + name].reshape(-1, w.shape[-1]), "adamw_" + name, tr=tr)
        res[name] = [a.reshape(w.shape) for a in r]

    update("conv_w", gcw.reshape(1, L * 4, CONV_CH // NDEV), L * 4)

    dmod_all = jnp.transpose(got[0], (1, 0, 2, 3)).reshape(L, NDEV * B, 6 * D)
    dm_mine = lax.dynamic_slice_in_dim(dmod_all, me * (6 * D // NDEV), 6 * D // NDEV, axis=2)
    dm_pad = jnp.pad(dm_mine, ((0, 0), (0, 128 - nb_rows), (0, 0))).astype(BF16)
    g_adaw = jnp.stack([_mm(c_pad, dm_pad[l], mode="tn", name=f"ada_dw{l}", tn=768) for l in range(L)])
    update("ada_w", g_adaw.reshape(1, L * D, 6 * D // NDEV), 256)

    big_res, after = dict.fromkeys(big), res["ada_w"][0]
    tile_rows = dict(w_in=256, w_out=256, w_mlp1=256, w_mlp2=128)
    for n, l, sems, s_thru, land_thru in reducing:
        parts = _chipsum_wait(sems, s_thru, land_thru, after, f"rs_wait_{n}{l}")
        w = args[n]
        big_res[n] = _adamw(w.reshape(-1, w.shape[-1]), parts, args["m_" + n].reshape(-1, w.shape[-1]),
                            args["v_" + n].reshape(-1, w.shape[-1]), f"adamw_{n}{l}", tr=tile_rows[n], row0=l * w.shape[1],
                            prev=big_res[n])
        after = big_res[n][0]
    for n in big:
        res[n] = [a.reshape(args[n].shape) for a in big_res[n]]

    names = ['ada_w', 'ada_b', 'norm1_g', 'w_in', 'gm_ln_g', 'gm_ln_b', 'gm_ws', 'gm_bs', 'gm_norm_g', 'attn_sinks',
             'attn_norm_g', 'conv_w', 'conv_b', 'dt_bias', 'a_log', 'd_skip', 'ssm_norm_g', 'w_out', 'norm2_g', 'w_mlp1',
             'w_mlp2', 'final_norm_g']
    return (loss, grad_x, *[res[n][0] for n in names], *[res[n][1] for n in names], *[res[n][2] for n in names],
            *[res[n][3] for n in names])
```

```python
import functools

import jax
import jax.numpy as jnp
import numpy as np
from jax import lax
from jax.experimental import pallas as pl
from jax.experimental.pallas import tpu as pltpu

F32, BF16 = jnp.float32, jnp.bfloat16
HI = lax.Precision.HIGHEST
MESH = pl.DeviceIdType.MESH
NDEV = 8

D = 2048
DEPTH = 2
CHUNK = 128
GM_W, GM_H = 512, 4
ATT_W, KV_W, ATT_H = 512, 128, 8
SSM_W, SSM_H, SSM_HD, SSM_G = 1024, 16, 64, 2
CONV_CH = 1536
IN_W = 4368
DFF = 8192
EPS = 1e-6
NEG_INF = -1e30
GELU_K = 0.7978845608028654
GELU_C = 0.044715

_ORIG = (("u", 512), ("v", 512), ("q", 512), ("k", 128), ("vv", 128), ("z", 1024), ("xbc", 1536), ("dt", 16))
OFF = dict(xbc=0, u=1536, z=2048, v=3072, q=3584, k=4096, vv=4224, dt=4352)
PW = 4608

ADAM_LR, ADAM_B1, ADAM_B2, ADAM_EPS, ADAM_WD, ADAM_STEP = 0.001, 0.9, 0.999, 1e-08, 0.01, 10


def _to_work_cols(w):
    parts, o = {}, 0
    for name, wd in _ORIG:
        parts[name] = w[..., o:o + wd]
        o += wd
    z = lambda n: jnp.zeros(w.shape[:-1] + (n,), w.dtype)
    return jnp.concatenate([parts["xbc"], parts["u"], parts["z"], parts["v"], parts["q"], parts["k"], parts["vv"],
                            parts["dt"], z(PW - OFF["dt"] - 16)], axis=-1)


def _from_work_cols(wp):
    return jnp.concatenate([wp[..., OFF[name]:OFF[name] + wd] for name, wd in _ORIG], axis=-1)


def _sigmoid(x):
    return 1.0 / (1.0 + jnp.exp(-x))


def _gelu(x):
    return 0.5 * x * (1.0 + jnp.tanh(GELU_K * (x + GELU_C * x * x * x)))


def _gelu_grad(x):
    t = jnp.tanh(GELU_K * (x + GELU_C * x * x * x))
    return 0.5 * (1.0 + t) + 0.5 * x * (1.0 - t * t) * GELU_K * (1.0 + 3.0 * GELU_C * x * x)


def _dot(a, b, prec=None):
    return jnp.dot(a, b, precision=prec, preferred_element_type=F32)


def _dot_nt(a, b, prec=None):
    return lax.dot_general(a, b, (((1,), (1,)), ((), ())), precision=prec, preferred_element_type=F32)


def _dot_tn(a, b, prec=None):
    return lax.dot_general(a, b, (((0,), (0,)), ((), ())), precision=prec, preferred_element_type=F32)


def _full(shape):
    return pl.BlockSpec(shape, lambda *_: (0,) * len(shape))


_HBM = pl.BlockSpec(memory_space=pltpu.HBM)


def _me():
    return lax.axis_index("x"), lax.axis_index("y"), lax.axis_index("c")


def _peer(k):
    x, y, c = _me()
    px = 1 - x if k & 4 else x
    py = 1 - y if k & 2 else y
    pc = 1 - c if k & 1 else c
    return (px, py, pc), 4 * px + 2 * py + pc


def _exchange(xs, name, scatter):
    n = len(xs)

    def body(*refs):
        ins, outs = refs[:n], refs[n:2 * n]
        send, recv, loc = refs[2 * n:]
        x, y, c = _me()
        me = 4 * x + 2 * y + c
        started = []
        for i in range(n):
            own = pltpu.make_async_copy(ins[i].at[me] if scatter else ins[i], outs[i].at[me], loc.at[i])
            own.start()
            started.append(own)
        for k in range(1, NDEV):
            dev, lin = _peer(k)
            for i in range(n):
                pltpu.make_async_remote_copy(
                    src_ref=ins[i].at[lin] if scatter else ins[i], dst_ref=outs[i].at[me],
                    send_sem=send.at[i, k - 1], recv_sem=recv.at[i, k - 1], device_id=dev, device_id_type=MESH).start()
        for k in range(1, NDEV):
            dev, lin = _peer(k)
            for i in range(n):
                pltpu.make_async_remote_copy(
                    src_ref=ins[i].at[lin] if scatter else ins[i], dst_ref=outs[i].at[lin],
                    send_sem=send.at[i, k - 1], recv_sem=recv.at[i, k - 1], device_id=dev, device_id_type=MESH).wait()
        for own in started:
            own.wait()

    out_shape = [jax.ShapeDtypeStruct(a.shape if scatter else (NDEV,) + a.shape, a.dtype) for a in xs]
    return pl.pallas_call(
        body, name=name, out_shape=out_shape, in_specs=[_HBM] * n, out_specs=[_HBM] * n,
        scratch_shapes=[pltpu.SemaphoreType.DMA((n, NDEV - 1)), pltpu.SemaphoreType.DMA((n, NDEV - 1)),
                        pltpu.SemaphoreType.DMA((n,))],
        compiler_params=pltpu.CompilerParams(has_side_effects=True),
    )(*xs)


def _chips():
    x, y, c = _me()
    return x, y, c, [(1 - x, y), (x, 1 - y), (1 - x, 1 - y)]


def _gather2(xs, name):
    n = len(xs)

    def body(*refs):
        ins, outs = refs[:n], refs[n:2 * n]
        send, recv, loc = refs[2 * n:]
        x, y, c, chips = _chips()
        me, sib = (x, y, c), (x, y, 1 - c)

        def cp(i, k, block, to, src=None):
            slot = outs[i].at[4 * block[0] + 2 * block[1] + block[2]]
            return pltpu.make_async_remote_copy(src_ref=slot if src is None else src, dst_ref=slot, send_sem=send.at[i, k],
                                                recv_sem=recv.at[i, k], device_id=to, device_id_type=MESH)

        sent = []
        for i in range(n):
            for j, chip in enumerate(chips):
                sent.append(cp(i, 1 + j, me, (*chip, c), src=ins[i]))
            sent.append(cp(i, 0, me, sib, src=ins[i]))
        for s in sent:
            s.start()
        own = [pltpu.make_async_copy(ins[i], outs[i].at[4 * x + 2 * y + c], loc.at[i]) for i in range(n)]
        for o in own:
            o.start()
        for j, chip in enumerate(chips):
            for i in range(n):
                cp(i, 1 + j, (*chip, c), me).wait_recv()
                fwd = cp(i, 4 + j, (*chip, c), sib)
                fwd.start()
                sent.append(fwd)
        for i in range(n):
            cp(i, 0, sib, me).wait_recv()
            for j, chip in enumerate(chips):
                cp(i, 4 + j, (*chip, 1 - c), me).wait_recv()
        for s in sent:
            s.wait_send()
        for o in own:
            o.wait()

    return pl.pallas_call(
        body, name=name, out_shape=[jax.ShapeDtypeStruct((NDEV,) + a.shape, a.dtype) for a in xs],
        in_specs=[_HBM] * n, out_specs=[_HBM] * n,
        scratch_shapes=[pltpu.SemaphoreType.DMA((n, 7)), pltpu.SemaphoreType.DMA((n, 7)), pltpu.SemaphoreType.DMA((n,))],
        compiler_params=pltpu.CompilerParams(has_side_effects=True),
    )(*xs)


def _pair_exchange(ps, name):
    n = len(ps)

    def body(*refs):
        ins, outs = refs[:n], refs[n:2 * n]
        send, recv = refs[2 * n:]
        x, y, c = _me()
        cps = [pltpu.make_async_remote_copy(src_ref=ins[i].at[ch, 1 - c], dst_ref=outs[i].at[ch], send_sem=send.at[i, ch],
                                            recv_sem=recv.at[i, ch], device_id=(x, y, 1 - c), device_id_type=MESH)
               for i in range(n) for ch in range(4)]
        for cp in cps:
            cp.start()
        for cp in cps:
            cp.wait()

    return pl.pallas_call(
        body, name=name, out_shape=[jax.ShapeDtypeStruct((4,) + a.shape[2:], a.dtype) for a in ps],
        in_specs=[_HBM] * n, out_specs=[_HBM] * n,
        scratch_shapes=[pltpu.SemaphoreType.DMA((n, 4)), pltpu.SemaphoreType.DMA((n, 4))],
        compiler_params=pltpu.CompilerParams(has_side_effects=True),
    )(*ps)


def _pair_add(p, r1, core, name, tr=256):
    _, _, R, C = p.shape
    tr = min(tr, R)

    def body(core_ref, p_ref, r_ref, o_ref, o2_ref):
        s = (p_ref[...].astype(F32) + r_ref[...].astype(F32)).astype(o_ref.dtype)
        o_ref[...] = s
        o2_ref[...] = s

    blk = pl.BlockSpec((None, tr, C), lambda ch, i, core_ref: (ch, i, 0))
    return pl.pallas_call(
        body, name=name, out_shape=[jax.ShapeDtypeStruct((4, R, C), p.dtype)] * 2,
        grid_spec=pltpu.PrefetchScalarGridSpec(
            num_scalar_prefetch=1, grid=(4, R // tr),
            in_specs=[pl.BlockSpec((None, None, tr, C), lambda ch, i, core_ref: (ch, core_ref[0], i, 0)), blk],
            out_specs=[blk, blk]),
    )(core, p, r1)


def _chip_exchange(ss, name):
    n = len(ss)

    def body(*refs):
        ins, outs = refs[:n], refs[n:2 * n]
        send, recv, loc = refs[2 * n:]
        x, y, c, chips = _chips()
        mine = 2 * x + y
        own = [pltpu.make_async_copy(ins[i].at[mine], outs[i].at[mine], loc.at[i]) for i in range(n)]
        cps = []
        for j, (px, py) in enumerate(chips):
            for i in range(n):
                cps.append(pltpu.make_async_remote_copy(
                    src_ref=ins[i].at[2 * px + py], dst_ref=outs[i].at[mine], send_sem=send.at[i, j], recv_sem=recv.at[i, j],
                    device_id=(px, py, c), device_id_type=MESH))
        for cp in cps + own:
            cp.start()
        for j, (px, py) in enumerate(chips):
            for i in range(n):
                pltpu.make_async_remote_copy(
                    src_ref=ins[i].at[2 * px + py], dst_ref=outs[i].at[2 * px + py], send_sem=send.at[i, j],
                    recv_sem=recv.at[i, j], device_id=(px, py, c), device_id_type=MESH).wait()
        for o in own:
            o.wait()

    return pl.pallas_call(
        body, name=name, out_shape=[jax.ShapeDtypeStruct(a.shape, a.dtype) for a in ss],
        in_specs=[_HBM] * n, out_specs=[_HBM] * n,
        scratch_shapes=[pltpu.SemaphoreType.DMA((n, 3)), pltpu.SemaphoreType.DMA((n, 3)), pltpu.SemaphoreType.DMA((n,))],
        compiler_params=pltpu.CompilerParams(has_side_effects=True),
    )(*ss)


_SEM = pl.BlockSpec(memory_space=pltpu.SEMAPHORE)
_ANY = pl.BlockSpec(memory_space=pl.ANY)
_DATAFLOW = pltpu.SideEffectType.DATAFLOW_SIDE_EFFECTING


def _hbm(a):
    return pltpu.with_memory_space_constraint(a, pltpu.HBM)


def _gather_targets():
    x, y, c, chips = _chips()
    return 4 * x + 2 * y + c, [(x, y, 1 - c)] + [(*chip, c) for chip in chips]


def _gather_start(v, order, name):
    def body(v_ref, land_ref, order_ref, *rest):
        sems, token = rest[:8], rest[10]
        me, targets = _gather_targets()
        for k, to in enumerate(targets):
            pltpu.make_async_remote_copy(src_ref=v_ref, dst_ref=land_ref.at[me], send_sem=sems[k], recv_sem=sems[4 + k],
                                         device_id=to, device_id_type=MESH).start()
        token[...] = jnp.zeros_like(token)

    land = lax.empty((NDEV,) + v.shape, v.dtype)
    outs = pl.pallas_call(
        body, name=name,
        out_shape=(pltpu.SemaphoreType.DMA(()),) * 8 + (pltpu.HBM(v.shape, v.dtype), pltpu.HBM(land.shape, land.dtype),
                                                        jax.ShapeDtypeStruct((8, 128), F32)),
        in_specs=(_HBM, _HBM, _ANY), out_specs=(_SEM,) * 8 + (_HBM, _HBM, pl.BlockSpec(memory_space=pltpu.VMEM)),
        input_output_aliases={0: 8, 1: 9}, compiler_params=pltpu.CompilerParams(has_side_effects=_DATAFLOW),
    )(_hbm(v), _hbm(land), order)
    return outs[:8], outs[8], outs[9], outs[10]


def _gather_wait(sems, v_thru, land_thru, after, name):
    def body(v_ref, land_ref, *rest):
        sems_ = rest[:8]
        me, targets = _gather_targets()
        for k, to in enumerate(targets):
            cp = pltpu.make_async_remote_copy(src_ref=v_ref, dst_ref=land_ref.at[me], send_sem=sems_[k], recv_sem=sems_[4 + k],
                                              device_id=to, device_id_type=MESH)
            cp.wait_send()
            cp.wait_recv()

    return pl.pallas_call(
        body, name=name, out_shape=(pltpu.HBM(v_thru.shape, v_thru.dtype), pltpu.HBM(land_thru.shape, land_thru.dtype)),
        in_specs=(_HBM, _HBM) + (_SEM,) * 8 + (_ANY,), out_specs=(_HBM, _HBM), input_output_aliases={0: 0, 1: 1},
        compiler_params=pltpu.CompilerParams(has_side_effects=_DATAFLOW),
    )(v_thru, land_thru, *sems, after)


def _gather_finish(vs, lands, name):
    n = len(vs)

    def body(*refs):
        v_refs, outs = refs[:n], refs[2 * n:3 * n]
        send, recv, loc = refs[3 * n:]
        x, y, c, chips = _chips()
        own = [pltpu.make_async_copy(v_refs[i], outs[i].at[4 * x + 2 * y + c], loc.at[i]) for i in range(n)]
        fwd = [pltpu.make_async_remote_copy(src_ref=outs[i].at[4 * px + 2 * py + c], dst_ref=outs[i].at[4 * px + 2 * py + c],
                                            send_sem=send.at[i, j], recv_sem=recv.at[i, j], device_id=(x, y, 1 - c),
                                            device_id_type=MESH)
               for i in range(n) for j, (px, py) in enumerate(chips)]
        for cp in fwd + own:
            cp.start()
        for i in range(n):
            for j, (px, py) in enumerate(chips):
                slot = outs[i].at[4 * px + 2 * py + 1 - c]
                pltpu.make_async_remote_copy(src_ref=slot, dst_ref=slot, send_sem=send.at[i, j], recv_sem=recv.at[i, j],
                                             device_id=(x, y, 1 - c), device_id_type=MESH).wait()
        for o in own:
            o.wait()

    return pl.pallas_call(
        body, name=name, out_shape=[jax.ShapeDtypeStruct(a.shape, a.dtype) for a in lands],
        in_specs=[_HBM] * (2 * n), out_specs=[_HBM] * n, input_output_aliases={n + i: i for i in range(n)},
        scratch_shapes=[pltpu.SemaphoreType.DMA((n, 3)), pltpu.SemaphoreType.DMA((n, 3)), pltpu.SemaphoreType.DMA((n,))],
        compiler_params=pltpu.CompilerParams(has_side_effects=True),
    )(*vs, *lands)


def _chip_targets():
    x, y, c, chips = _chips()
    return 2 * x + y, [((px, py, c), 2 * px + py) for px, py in chips]


def _chipsum_start(s, land, order, name):
    def body(s_ref, land_ref, order_ref, *rest):
        sems, token = rest[:6], rest[8]
        mine, targets = _chip_targets()
        for k, (to, ch) in enumerate(targets):
            pltpu.make_async_remote_copy(src_ref=s_ref.at[ch], dst_ref=land_ref.at[mine], send_sem=sems[k], recv_sem=sems[3 + k],
                                         device_id=to, device_id_type=MESH).start()
        token[...] = jnp.zeros_like(token)

    outs = pl.pallas_call(
        body, name=name,
        out_shape=(pltpu.SemaphoreType.DMA(()),) * 6 + (pltpu.HBM(s.shape, s.dtype), pltpu.HBM(land.shape, land.dtype),
                                                        jax.ShapeDtypeStruct((8, 128), F32)),
        in_specs=(_HBM, _HBM, _ANY), out_specs=(_SEM,) * 6 + (_HBM, _HBM, pl.BlockSpec(memory_space=pltpu.VMEM)),
        input_output_aliases={0: 6, 1: 7}, compiler_params=pltpu.CompilerParams(has_side_effects=_DATAFLOW),
    )(_hbm(s), _hbm(land), order)
    return outs[:6], outs[6], outs[7], outs[8]


def _chipsum_wait(sems, s_thru, land_thru, after, name):
    def body(s_ref, land_ref, *rest):
        sems_ = rest[:6]
        mine, targets = _chip_targets()
        for k, (to, ch) in enumerate(targets):
            cp = pltpu.make_async_remote_copy(src_ref=s_ref.at[ch], dst_ref=land_ref.at[ch], send_sem=sems_[k], recv_sem=sems_[3 + k],
                                              device_id=to, device_id_type=MESH)
            cp.wait_send()
            cp.wait_recv()

    return pl.pallas_call(
        body, name=name, out_shape=(pltpu.HBM(s_thru.shape, s_thru.dtype), pltpu.HBM(land_thru.shape, land_thru.dtype)),
        in_specs=(_HBM, _HBM) + (_SEM,) * 6 + (_ANY,), out_specs=(_HBM, _HBM), input_output_aliases={0: 0, 1: 1},
        compiler_params=pltpu.CompilerParams(has_side_effects=_DATAFLOW),
    )(s_thru, land_thru, *sems, after)[1]


def _mm(a, b, *, mode, name, out_dtypes=(F32,), epilogue=None, extras=(), tm=1024, tn=1024, tk=2048,
        col_blocked_b=False, col_blocked_out=False, order=None):
    CB = 1024
    if col_blocked_b:
        assert mode in ("nn", "nt") and b.shape[2] == CB
        (M, K), N = a.shape, (b.shape[0] * CB if mode == "nn" else b.shape[1])
        tn, tk = (CB, tk) if mode == "nn" else (tn, CB)
    elif mode == "nn":
        (M, K), N = a.shape, b.shape[1]
    elif mode == "nt":
        (M, K), N = a.shape, b.shape[0]
    else:
        (K, M), N = a.shape, b.shape[1]
    if col_blocked_out:
        assert len(out_dtypes) == 1 and N % CB == 0
        tn = CB
    tm, tn, tk = min(tm, M), min(tn, N), min(tk, K)
    assert M % tm == 0 and N % tn == 0 and K % tk == 0, (M, N, K, tm, tn, tk)
    nk = K // tk
    ne, no = len(extras), len(out_dtypes)
    dims = {"nn": (((1,), (0,)), ((), ())), "nt": (((1,), (1,)), ((), ())), "tn": (((0,), (0,)), ((), ()))}[mode]

    no_ = 0 if order is None else 1

    def body(a_ref, b_ref, *rest):
        rest = rest[no_:]
        ex, outs = rest[:ne], rest[ne:ne + no]

        def finish(acc):
            res = epilogue(acc, *[e[...] for e in ex]) if epilogue is not None else (acc,)
            for o, r in zip(outs, res):
                o[...] = r.astype(o.dtype)

        part = lax.dot_general(a_ref[...], b_ref[...], dims, preferred_element_type=F32)
        if nk == 1:
            finish(part)
        else:
            acc_ref = rest[-1]
            k = pl.program_id(2)

            @pl.when(k == 0)
            def _():
                acc_ref[...] = part

            @pl.when(k > 0)
            def _():
                acc_ref[...] += part

            @pl.when(k == nk - 1)
            def _():
                finish(acc_ref[...])

    a_spec = {"nn": pl.BlockSpec((tm, tk), lambda i, j, k: (i, k)), "nt": pl.BlockSpec((tm, tk), lambda i, j, k: (i, k)),
              "tn": pl.BlockSpec((tk, tm), lambda i, j, k: (k, i))}[mode]
    b_spec = {"nn": pl.BlockSpec((tk, tn), lambda i, j, k: (k, j)), "nt": pl.BlockSpec((tn, tk), lambda i, j, k: (j, k)),
              "tn": pl.BlockSpec((tk, tn), lambda i, j, k: (k, j))}[mode]
    if col_blocked_b:
        b_spec = (pl.BlockSpec((None, tk, CB), lambda i, j, k: (j, k, 0)) if mode == "nn"
                  else pl.BlockSpec((None, tn, CB), lambda i, j, k: (k, j, 0)))
    e_spec = pl.BlockSpec((tm, tn), lambda i, j, k: (i, j))
    o_spec, o_dims = e_spec, (M, N)
    if col_blocked_out:
        o_spec, o_dims = pl.BlockSpec((None, tm, CB), lambda i, j, k: (j, i, 0)), (N // CB, M, CB)
    outs = pl.pallas_call(
        body, name=name, grid=(M // tm, N // tn, nk),
        in_specs=[a_spec, b_spec] + [_ANY] * no_ + [e_spec] * ne, out_specs=[o_spec] * no,
        out_shape=[jax.ShapeDtypeStruct(o_dims, dt) for dt in out_dtypes],
        scratch_shapes=[pltpu.VMEM((tm, tn), F32)] if nk > 1 else [],
        compiler_params=pltpu.CompilerParams(dimension_semantics=("parallel", "parallel", "arbitrary")),
    )(a, b, *([] if order is None else [order]), *extras)
    return outs if no > 1 else outs[0]


def _norm_fwd(x, g, sc, sh, name, resid=None):
    B, S, Dm = x.shape
    ts = min(S, 256)
    tok = pl.BlockSpec((None, ts, Dm), lambda b, i: (b, i, 0))
    row = pl.BlockSpec((None, 1, Dm), lambda b, i: (b, 0, 0))
    par = pl.BlockSpec((1, Dm), lambda b, i: (0, 0))

    def body(*refs):
        if resid is not None:
            x_ref, br_ref, gt_ref, g_ref, sc_ref, sh_ref, xo_ref, h_ref = refs
            xv = x_ref[...] + gt_ref[...] * br_ref[...]
            xo_ref[...] = xv
        else:
            x_ref, g_ref, sc_ref, sh_ref, h_ref = refs
            xv = x_ref[...]
        r = lax.rsqrt(jnp.mean(xv * xv, axis=-1, keepdims=True) + EPS)
        h_ref[...] = ((xv * r * g_ref[...]) * (1.0 + sc_ref[...]) + sh_ref[...]).astype(BF16)

    h_shape = jax.ShapeDtypeStruct((B, S, Dm), BF16)
    if resid is not None:
        return pl.pallas_call(body, name=name, grid=(B, S // ts), in_specs=[tok, tok, row, par, row, row],
                              out_specs=[tok, tok], out_shape=[jax.ShapeDtypeStruct((B, S, Dm), F32), h_shape],
                              )(x, resid[0], resid[1], g, sc, sh)
    return pl.pallas_call(body, name=name, grid=(B, S // ts), in_specs=[tok, par, row, row], out_specs=tok,
                          out_shape=h_shape)(x, g, sc, sh)


def _norm_bwd(x, g, name, *, sc=None, dh=None, dres=None, tgt=None, br=None, gate=None, x_is_prev=False):
    B, S, Dm = x.shape
    ts = min(S, 256)
    final = tgt is not None
    has_br = br is not None
    tok = pl.BlockSpec((None, ts, Dm), lambda b, i: (b, i, 0))
    row = pl.BlockSpec((None, 1, Dm), lambda b, i: (b, 0, 0))
    par = pl.BlockSpec((1, Dm), lambda b, i: (0, 0))
    ins, in_specs = [x, g], [tok, par]
    if final:
        ins, in_specs = ins + [tgt], in_specs + [tok]
    else:
        ins, in_specs = ins + [sc, dh], in_specs + [row, tok]
    if dres is not None:
        ins, in_specs = ins + [dres], in_specs + [tok]
    if has_br:
        ins, in_specs = ins + [br, gate], in_specs + [tok, row]
    n_in = len(ins)
    out_shape = [jax.ShapeDtypeStruct((B, S, Dm), F32), jax.ShapeDtypeStruct((1, Dm), F32)]
    out_specs = [tok, par]
    if final:
        out_shape.append(jax.ShapeDtypeStruct((1, 128), F32))
        out_specs.append(pl.BlockSpec((1, 128), lambda b, i: (0, 0)))
    else:
        out_shape += [jax.ShapeDtypeStruct((B, 1, Dm), F32)] * 2
        out_specs += [row, row]
    if has_br:
        out_shape += [jax.ShapeDtypeStruct((B, S, Dm), BF16), jax.ShapeDtypeStruct((B, 1, Dm), F32)]
        out_specs += [tok, row]

    def body(*refs):
        it = iter(refs[:n_in])
        outs = iter(refs[n_in:])
        x_ref, g_ref = next(it), next(it)
        b, i = pl.program_id(0), pl.program_id(1)
        first, first_row = (b == 0) & (i == 0), i == 0
        xv, gv = x_ref[...], g_ref[...]
        if x_is_prev:
            xv = xv + refs[n_in - 1][...] * refs[n_in - 2][...]
        r = lax.rsqrt(jnp.mean(xv * xv, axis=-1, keepdims=True) + EPS)
        n = xv * r
        dx_ref, dg_ref = next(outs), next(outs)

        def acc(ref, val, init):
            @pl.when(init)
            def _():
                ref[...] = val

            @pl.when(jnp.logical_not(init))
            def _():
                ref[...] += val

        if final:
            t_ref = next(it)
            loss_ref = next(outs)
            e = n * gv - t_ref[...]
            acc(loss_ref, jnp.zeros((1, 128), F32) + 0.5 * jnp.sum(e * e) / Dm, first)
            dyg = e * (1.0 / Dm)
        else:
            sc_ref, dh_ref = next(it), next(it)
            dsc_ref, dsh_ref = next(outs), next(outs)
            dhv = dh_ref[...]
            acc(dsh_ref, jnp.sum(dhv, axis=0, keepdims=True), first_row)
            acc(dsc_ref, jnp.sum(dhv * (n * gv), axis=0, keepdims=True), first_row)
            dyg = dhv * (1.0 + sc_ref[...])
        acc(dg_ref, jnp.sum(dyg * n, axis=0, keepdims=True), first)
        dn = dyg * gv
        dx = r * (dn - n * jnp.mean(dn * n, axis=-1, keepdims=True))
        if dres is not None:
            dx = dx + next(it)[...]
        dx_ref[...] = dx
        if has_br:
            br_ref, gt_ref = next(it), next(it)
            dbr_ref, dgt_ref = next(outs), next(outs)
            dbr_ref[...] = (dx * gt_ref[...]).astype(BF16)
            acc(dgt_ref, jnp.sum(dx * br_ref[...], axis=0, keepdims=True), first_row)

    outs = pl.pallas_call(body, name=name, grid=(B, S // ts), in_specs=in_specs, out_specs=out_specs, out_shape=out_shape,
                          compiler_params=pltpu.CompilerParams(dimension_semantics=("arbitrary", "arbitrary")))(*ins)
    res = dict(dx=outs[0], dg=outs[1])
    if final:
        res["loss"] = outs[2]
    else:
        res["dsc"], res["dsh"] = outs[2], outs[3]
    if has_br:
        res["dbr"], res["dgate"] = outs[-2], outs[-1]
    return res


def _gm_heads(vg, lng, lnb):
    res = []
    for h in range(GM_H):
        sl = slice(h * 128, (h + 1) * 128)
        vh = vg[:, sl]
        xc = vh - jnp.mean(vh, axis=-1, keepdims=True)
        rstd = lax.rsqrt(jnp.mean(xc * xc, axis=-1, keepdims=True) + 1e-5)
        xhat = xc * rstd
        res.append((xhat, rstd, xhat * lng[:, sl] + lnb[:, sl]))
    return res


def _gm_gate(heads, wt_ref, bsx, nch):
    cols = []
    for h in range(GM_H):
        vn = heads[h][2].astype(BF16)
        rows = [_dot(wt_ref[h], vn[c * CHUNK:(c + 1) * CHUNK]) + bsx[:, h * 128:(h + 1) * 128] for c in range(nch)]
        cols.append(jnp.concatenate(rows, axis=0) if nch > 1 else rows[0])
    return jnp.concatenate(cols, axis=1)


def _gm_specs(S):
    tb = min(S, 512)
    u = pl.BlockSpec((None, tb, GM_W), lambda b, i: (b, i, OFF["u"] // GM_W))
    v = pl.BlockSpec((None, tb, GM_W), lambda b, i: (b, i, OFF["v"] // GM_W))
    tok = pl.BlockSpec((None, tb, GM_W), lambda b, i: (b, i, 0))
    return tb, u, v, tok


def _gmlp_fwd(P, lng, lnb, wt, bsx, og, name):
    B, S, _ = P.shape
    tb, u_spec, v_spec, tok = _gm_specs(S)
    nch = tb // CHUNK

    def body(u_ref, v_ref, lng_ref, lnb_ref, wt_ref, bsx_ref, og_ref, o_ref):
        heads = _gm_heads(_gelu(v_ref[...]), lng_ref[...], lnb_ref[...])
        y = _gelu(u_ref[...]) * _gm_gate(heads, wt_ref, bsx_ref[...], nch)
        r = lax.rsqrt(jnp.mean(y * y, axis=-1, keepdims=True) + EPS)
        o_ref[...] = (y * r * og_ref[...]).astype(BF16)

    return pl.pallas_call(
        body, name=name, grid=(B, S // tb),
        in_specs=[u_spec, v_spec, _full((1, GM_W)), _full((1, GM_W)), _full((GM_H, 128, 128)), _full((128, GM_W)), _full((1, GM_W))],
        out_specs=tok, out_shape=jax.ShapeDtypeStruct((B, S, GM_W), BF16))(P, P, lng, lnb, wt, bsx, og)


def _gmlp_bwd(P, dcat, lng, lnb, wt, wtT, bsx, og, name):
    B, S, _ = P.shape
    tb, u_spec, v_spec, tok = _gm_specs(S)
    nch = tb // CHUNK
    do_spec = pl.BlockSpec((None, tb, GM_W), lambda b, i: (b, i, 0))

    def body(u_ref, v_ref, do_ref, lng_ref, lnb_ref, wt_ref, wtT_ref, bsx_ref, og_ref,
             du_ref, dv_ref, dlng_ref, dlnb_ref, dws_ref, dbsx_ref, dog_ref):
        first = (pl.program_id(0) == 0) & (pl.program_id(1) == 0)

        @pl.when(first)
        def _():
            for ref in (dlng_ref, dlnb_ref, dws_ref, dbsx_ref, dog_ref):
                ref[...] = jnp.zeros(ref.shape, F32)

        u, v, lng = u_ref[...], v_ref[...], lng_ref[...]
        ug = _gelu(u)
        heads = _gm_heads(_gelu(v), lng, lnb_ref[...])
        gate = _gm_gate(heads, wt_ref, bsx_ref[...], nch)
        y = ug * gate
        r = lax.rsqrt(jnp.mean(y * y, axis=-1, keepdims=True) + EPS)
        yn = y * r
        dout = do_ref[...]
        dog_ref[...] += jnp.sum(dout * yn, axis=0, keepdims=True)
        dyn = dout * og_ref[...]
        dy = r * (dyn - yn * jnp.mean(dyn * yn, axis=-1, keepdims=True))
        du_ref[...] = (dy * gate * _gelu_grad(u)).astype(BF16)
        dgate = dy * ug
        tril = lax.broadcasted_iota(jnp.int32, (128, 128), 0) >= lax.broadcasted_iota(jnp.int32, (128, 128), 1)
        dvg = []
        for h in range(GM_H):
            sl = slice(h * 128, (h + 1) * 128)
            xhat, rstd, vn = heads[h]
            vnb = vn.astype(BF16)
            dgh = dgate[:, sl]
            dgb = dgh.astype(BF16)
            dbs = jnp.zeros((128, 128), F32)
            dw = jnp.zeros((128, 128), F32)
            dvn = []
            for c in range(nch):
                rs = slice(c * CHUNK, (c + 1) * CHUNK)
                dbs = dbs + dgh[rs]
                dw = dw + _dot_nt(dgb[rs], vnb[rs])
                dvn.append(_dot(wtT_ref[h], dgb[rs]))
            dvn = jnp.concatenate(dvn, axis=0) if nch > 1 else dvn[0]
            dbsx_ref[:, sl] += dbs
            dws_ref[h] += jnp.where(tril, dw, 0.0)
            dlng_ref[:, sl] += jnp.sum(dvn * xhat, axis=0, keepdims=True)
            dlnb_ref[:, sl] += jnp.sum(dvn, axis=0, keepdims=True)
            dxh = dvn * lng[:, sl]
            dvg.append(rstd * (dxh - jnp.mean(dxh, axis=-1, keepdims=True) - xhat * jnp.mean(dxh * xhat, axis=-1, keepdims=True)))
        dv_ref[...] = (jnp.concatenate(dvg, axis=1) * _gelu_grad(v)).astype(BF16)

    p512, w3 = _full((1, GM_W)), _full((GM_H, 128, 128))
    return pl.pallas_call(
        body, name=name, grid=(B, S // tb),
        in_specs=[u_spec, v_spec, do_spec, p512, p512, w3, w3, _full((128, GM_W)), p512],
        out_specs=[tok, tok, p512, p512, w3, _full((128, GM_W)), p512],
        out_shape=[jax.ShapeDtypeStruct((B, S, GM_W), BF16)] * 2 + [
            jax.ShapeDtypeStruct((1, GM_W), F32), jax.ShapeDtypeStruct((1, GM_W), F32),
            jax.ShapeDtypeStruct((GM_H, 128, 128), F32), jax.ShapeDtypeStruct((128, GM_W), F32),
            jax.ShapeDtypeStruct((1, GM_W), F32)],
        compiler_params=pltpu.CompilerParams(dimension_semantics=("arbitrary", "arbitrary")),
    )(P, P, dcat, lng, lnb, wt, wtT, bsx, og)


def _lane_half():
    return lax.broadcasted_iota(jnp.int32, (128, 128), 1) // 64


def _att_stack(x, kvh, dtype):
    half = _lane_half()
    rows = []
    for g in range(4):
        i = kvh * 4 + g
        pair = x[:, (i // 2) * 128:(i // 2 + 1) * 128]
        if i % 2 != kvh:
            pair = pltpu.roll(pair, 64, 1)
        rows.append(jnp.where(half == kvh, pair, 0.0))
    return jnp.concatenate(rows, axis=0).astype(dtype)


def _att_unstack(pairs, y, kvh):
    half = _lane_half()
    for g in range(4):
        i = kvh * 4 + g
        piece = y[g * 128:(g + 1) * 128]
        if i % 2 != kvh:
            piece = pltpu.roll(piece, 64, 1)
        pairs[i // 2] = jnp.where(half == i % 2, piece, pairs[i // 2])
    return pairs


def _att_probs(qb, k2, st, sink_ref, kvh):
    qm = _att_stack(qb, kvh, BF16)
    s = _dot_nt(qm, k2) * (64 ** -0.5)
    qi = lax.broadcasted_iota(jnp.int32, (512, 256), 0) % 128
    kj = lax.broadcasted_iota(jnp.int32, (512, 256), 1)
    diff = qi + 128 - kj
    valid = (diff >= 0) & (diff < 128) & (st + kj - 128 >= 0)
    s = jnp.where(valid, s, NEG_INF)
    grp = lax.broadcasted_iota(jnp.int32, (512, 1), 0) // 128
    sink = jnp.zeros((512, 1), F32)
    for g in range(4):
        sink = jnp.where(grp == g, sink_ref[kvh * 4 + g], sink)
    m = jnp.maximum(jnp.max(s, axis=-1, keepdims=True), sink)
    e = jnp.exp(s - m)
    esink = jnp.exp(sink - m)
    inv = 1.0 / (jnp.sum(e, axis=-1, keepdims=True) + esink)
    return qm, e * inv, esink * inv


def _att_specs(S):
    q = pl.BlockSpec((None, S, ATT_W), lambda b: (b, 0, OFF["q"] // ATT_W))
    k = pl.BlockSpec((None, S, KV_W), lambda b: (b, 0, OFF["k"] // KV_W))
    v = pl.BlockSpec((None, S, KV_W), lambda b: (b, 0, OFF["vv"] // KV_W))
    tok = pl.BlockSpec((None, S, ATT_W), lambda b: (b, 0, 0))
    kv = pl.BlockSpec((None, S, KV_W), lambda b: (b, 0, 0))
    return q, k, v, tok, kv


_SMEM = pl.BlockSpec(memory_space=pltpu.SMEM)


def _attn_fwd(P, sinks, og, name):
    B, S, _ = P.shape
    q_spec, k_spec, v_spec, tok, _ = _att_specs(S)

    def body(q_ref, k_ref, v_ref, sink_ref, og_ref, o_ref, kpad, vpad):
        kpad[0:128, :] = jnp.zeros((128, KV_W), BF16)
        vpad[0:128, :] = jnp.zeros((128, KV_W), BF16)
        kpad[128:, :] = k_ref[...].astype(BF16)
        vpad[128:, :] = v_ref[...].astype(BF16)

        def step(n, carry):
            st = pl.multiple_of(n * 128, 128)
            qb = q_ref[pl.ds(st, 128), :]
            k2, v2 = kpad[pl.ds(st, 256), :], vpad[pl.ds(st, 256), :]
            pairs = [jnp.zeros((128, 128), F32)] * 4
            for kvh in range(2):
                _, p, _ = _att_probs(qb, k2, st, sink_ref, kvh)
                pairs = _att_unstack(pairs, _dot(p.astype(BF16), v2), kvh)
            o = jnp.concatenate(pairs, axis=1)
            r = lax.rsqrt(jnp.mean(o * o, axis=-1, keepdims=True) + EPS)
            o_ref[pl.ds(st, 128), :] = (o * r * og_ref[...]).astype(BF16)
            return carry

        lax.fori_loop(0, S // 128, step, 0)

    return pl.pallas_call(
        body, name=name, grid=(B,), in_specs=[q_spec, k_spec, v_spec, _SMEM, _full((1, ATT_W))], out_specs=tok,
        out_shape=jax.ShapeDtypeStruct((B, S, ATT_W), BF16),
        scratch_shapes=[pltpu.VMEM((S + 128, KV_W), BF16)] * 2)(P, P, P, sinks, og)


def _attn_bwd(P, dcat, sinks, og, name):
    B, S, _ = P.shape
    q_spec, k_spec, v_spec, tok, kv = _att_specs(S)
    do_spec = pl.BlockSpec((None, S, ATT_W), lambda b: (b, 0, GM_W // ATT_W))

    def body(q_ref, k_ref, v_ref, do_ref, sink_ref, og_ref, dq_ref, dk_ref, dv_ref, dsink_ref, dog_ref,
             kpad, vpad, dkpad, dvpad):
        @pl.when(pl.program_id(0) == 0)
        def _():
            dsink_ref[...] = jnp.zeros((8, 128), F32)
            dog_ref[...] = jnp.zeros((1, ATT_W), F32)

        kpad[0:128, :] = jnp.zeros((128, KV_W), BF16)
        vpad[0:128, :] = jnp.zeros((128, KV_W), BF16)
        kpad[128:, :] = k_ref[...].astype(BF16)
        vpad[128:, :] = v_ref[...].astype(BF16)
        dkpad[...] = jnp.zeros((S + 128, KV_W), F32)
        dvpad[...] = jnp.zeros((S + 128, KV_W), F32)
        half = _lane_half()
        head_row = lax.broadcasted_iota(jnp.int32, (8, 128), 0)

        def step(n, carry):
            st = pl.multiple_of(n * 128, 128)
            qb = q_ref[pl.ds(st, 128), :]
            k2, v2 = kpad[pl.ds(st, 256), :], vpad[pl.ds(st, 256), :]
            saved, pairs = [], [jnp.zeros((128, 128), F32)] * 4
            for kvh in range(2):
                qm, p, psink = _att_probs(qb, k2, st, sink_ref, kvh)
                o = _dot(p.astype(BF16), v2)
                saved.append((qm, p, psink, o))
                pairs = _att_unstack(pairs, o, kvh)
            o = jnp.concatenate(pairs, axis=1)
            r = lax.rsqrt(jnp.mean(o * o, axis=-1, keepdims=True) + EPS)
            on = o * r
            dout = do_ref[pl.ds(st, 128), :]
            dog_ref[...] += jnp.sum(dout * on, axis=0, keepdims=True)
            dyn = dout * og_ref[...]
            do = r * (dyn - on * jnp.mean(dyn * on, axis=-1, keepdims=True))
            dq_pairs = [jnp.zeros((128, 128), F32)] * 4
            dsink = jnp.zeros((8, 128), F32)
            for kvh in range(2):
                qm, p, psink, og_ = saved[kvh]
                dog = _att_stack(do, kvh, F32)
                delta = jnp.sum(dog * jnp.where(jnp.concatenate([half] * 4, axis=0) == kvh, og_, 0.0), axis=-1, keepdims=True)
                dogb, pb = dog.astype(BF16), p.astype(BF16)
                dvpad[pl.ds(st, 256), :] += _dot_tn(pb, dogb)
                dp = _dot_nt(dogb, v2)
                ds = (p * (dp - delta) * (64 ** -0.5)).astype(BF16)
                sd = psink * delta
                for g in range(4):
                    dsink = dsink - jnp.where(head_row == kvh * 4 + g, jnp.sum(sd[g * 128:(g + 1) * 128]), 0.0)
                dq_pairs = _att_unstack(dq_pairs, _dot(ds, k2), kvh)
                dkpad[pl.ds(st, 256), :] += _dot_tn(ds, qm)
            dsink_ref[...] += dsink
            dq_ref[pl.ds(st, 128), :] = jnp.concatenate(dq_pairs, axis=1).astype(BF16)
            return carry

        lax.fori_loop(0, S // 128, step, 0)
        dk_ref[...] = dkpad[128:, :].astype(BF16)
        dv_ref[...] = dvpad[128:, :].astype(BF16)

    return pl.pallas_call(
        body, name=name, grid=(B,),
        in_specs=[q_spec, k_spec, v_spec, do_spec, _SMEM, _full((1, ATT_W))],
        out_specs=[tok, kv, kv, _full((8, 128)), _full((1, ATT_W))],
        out_shape=[jax.ShapeDtypeStruct((B, S, ATT_W), BF16), jax.ShapeDtypeStruct((B, S, KV_W), BF16),
                   jax.ShapeDtypeStruct((B, S, KV_W), BF16), jax.ShapeDtypeStruct((8, 128), F32),
                   jax.ShapeDtypeStruct((1, ATT_W), F32)],
        scratch_shapes=[pltpu.VMEM((S + 128, KV_W), BF16)] * 2 + [pltpu.VMEM((S + 128, KV_W), F32)] * 2,
        compiler_params=pltpu.CompilerParams(dimension_semantics=("arbitrary",)),
    )(P, P, P, dcat, sinks, og)


CONV_TC = 256


def _conv_pre(ext, w_ref, b_ref, S):
    acc = b_ref[...] + w_ref[3:4, :] * ext[pl.ds(8, S), :]
    for k in range(1, 4):
        acc = acc + w_ref[3 - k:4 - k, :] * ext[pl.ds(8 - k, S), :]
    return acc


def _conv_fwd(P, w8, b, name):
    B, S, _ = P.shape
    nj = CONV_CH // CONV_TC
    x_spec = pl.BlockSpec((None, S, CONV_TC), lambda b_, j: (b_, 0, OFF["xbc"] // CONV_TC + j))
    tok = pl.BlockSpec((None, S, CONV_TC), lambda b_, j: (b_, 0, j))

    def body(x_ref, w_ref, b_ref, o_ref, ext):
        ext[0:8, :] = jnp.zeros((8, CONV_TC), F32)
        ext[8:, :] = x_ref[...]
        pre = _conv_pre(ext, w_ref, b_ref, S)
        o_ref[...] = pre * _sigmoid(pre)

    return pl.pallas_call(
        body, name=name, grid=(B, nj),
        in_specs=[x_spec, pl.BlockSpec((8, CONV_TC), lambda b_, j: (0, j)), pl.BlockSpec((1, CONV_TC), lambda b_, j: (0, j))],
        out_specs=tok, out_shape=jax.ShapeDtypeStruct((B, S, CONV_CH), F32),
        scratch_shapes=[pltpu.VMEM((S + 8, CONV_TC), F32)])(P, w8, b)


def _conv_bwd(P, dact, w8, b, name):
    B, S, _ = P.shape
    nj = CONV_CH // CONV_TC
    x_spec = pl.BlockSpec((None, S, CONV_TC), lambda j, b_: (b_, 0, OFF["xbc"] // CONV_TC + j))
    tok = pl.BlockSpec((None, S, CONV_TC), lambda j, b_: (b_, 0, j))
    w_spec = pl.BlockSpec((8, CONV_TC), lambda j, b_: (0, j))
    b_spec = pl.BlockSpec((1, CONV_TC), lambda j, b_: (0, j))

    def body(x_ref, d_ref, w_ref, b_ref, dx_ref, dw_ref, db_ref, ext, extd):
        @pl.when(pl.program_id(1) == 0)
        def _():
            dw_ref[...] = jnp.zeros((8, CONV_TC), F32)
            db_ref[...] = jnp.zeros((1, CONV_TC), F32)

        ext[0:8, :] = jnp.zeros((8, CONV_TC), F32)
        ext[8:, :] = x_ref[...]
        pre = _conv_pre(ext, w_ref, b_ref, S)
        sg = _sigmoid(pre)
        dpre = d_ref[...] * (sg * (1.0 + pre * (1.0 - sg)))
        extd[0:8, :] = jnp.zeros((8, CONV_TC), F32)
        extd[pl.ds(8, S), :] = dpre
        extd[pl.ds(8 + S, 8), :] = jnp.zeros((8, CONV_TC), F32)
        dx = w_ref[3:4, :] * dpre
        for k in range(1, 4):
            dx = dx + w_ref[3 - k:4 - k, :] * extd[pl.ds(8 + k, S), :]
        dx_ref[...] = dx.astype(BF16)
        db_ref[...] += jnp.sum(dpre, axis=0, keepdims=True)
        sub = lax.broadcasted_iota(jnp.int32, (8, CONV_TC), 0)
        dw = jnp.zeros((8, CONV_TC), F32)
        for i in range(4):
            dw = dw + jnp.where(sub == i, jnp.sum(dpre * ext[pl.ds(5 + i, S), :], axis=0, keepdims=True), 0.0)
        dw_ref[...] += dw

    return pl.pallas_call(
        body, name=name, grid=(nj, B), in_specs=[x_spec, tok, w_spec, b_spec], out_specs=[tok, w_spec, b_spec],
        out_shape=[jax.ShapeDtypeStruct((B, S, CONV_CH), BF16), jax.ShapeDtypeStruct((8, CONV_CH), F32),
                   jax.ShapeDtypeStruct((1, CONV_CH), F32)],
        scratch_shapes=[pltpu.VMEM((S + 8, CONV_TC), F32), pltpu.VMEM((S + 16, CONV_TC), F32)],
        compiler_params=pltpu.CompilerParams(dimension_semantics=("arbitrary", "arbitrary")),
    )(P, dact, w8, b)


def _ssd_consts():
    hd = np.arange(SSM_W) // SSM_HD
    E = (np.arange(128)[:, None] == hd[None, :]).astype(np.float32)
    tri = (np.arange(128)[:, None] >= np.arange(128)[None, :]).astype(np.float32)
    return jnp.asarray(E), jnp.asarray(E.T), jnp.asarray(tri), jnp.asarray(tri.T)


def _ssd_pre(xa, dtraw, bias, alog, E, ET, tri):
    lane = lax.broadcasted_iota(jnp.int32, (128, 128), 1)
    pre = dtraw + bias
    dtp = jnp.where(lane < SSM_H, jnp.maximum(pre, 0.0) + jnp.log(1.0 + jnp.exp(-jnp.abs(pre))), 0.0)
    a = -jnp.exp(alog)
    acs = _dot(tri, dtp * a, HI)
    acsT = acs.T
    dtE, acsE = _dot(dtp, E, HI), _dot(acs, E, HI)
    cdcol = jnp.exp(_dot(ET, acsT, HI)[:, 127:128])
    X = xa[:, :SSM_W]
    xdt = X * dtE
    wE = jnp.exp(acsE[127:128, :] - acsE)
    eE = jnp.exp(acsE)
    return dict(pre=pre, dtp=dtp, a=a, acs=acs, acsT=acsT, dtE=dtE, acsE=acsE, cdcol=cdcol, X=X, xdt=xdt, wE=wE, eE=eE)


def _ssd_decay(c, h):
    lm = lax.broadcasted_iota(jnp.int32, (128, 128), 0) >= lax.broadcasted_iota(jnp.int32, (128, 128), 1)
    return jnp.exp(jnp.where(lm, c["acs"][:, h:h + 1] - c["acsT"][h:h + 1, :], NEG_INF))


def _ssd_pair_operands(c, CB, h0):
    lane = lax.broadcasted_iota(jnp.int32, (128, 128), 1)
    L0, L1 = _ssd_decay(c, h0), _ssd_decay(c, h0 + 1)
    M = jnp.concatenate([CB * L0, CB * L1], axis=1).astype(BF16)
    xp = c["xdt"][:, h0 * 64:h0 * 64 + 128]
    BD = jnp.concatenate([jnp.where(lane < 64, xp, 0.0), jnp.where(lane >= 64, xp, 0.0)], axis=0).astype(BF16)
    return L0, L1, M, BD


def _ssd_y(c, xa, state_ref, dskipE):
    per_group, ys = [], []
    for g in range(SSM_G):
        gs = slice(g * 512, (g + 1) * 512)
        Bb = xa[:, SSM_W + g * 128:SSM_W + (g + 1) * 128].astype(BF16)
        Cb = xa[:, SSM_W + 256 + g * 128:SSM_W + 256 + (g + 1) * 128].astype(BF16)
        CB = _dot_nt(Cb, Bb)
        Sg = state_ref[gs, :]
        yoff = _dot_nt(Cb, Sg.astype(BF16)) * c["eE"][:, gs]
        ydiag, pairs = [], []
        for j in range(4):
            ops = _ssd_pair_operands(c, CB, g * 8 + 2 * j)
            pairs.append(ops)
            ydiag.append(_dot(ops[2], ops[3]))
        ys.append(jnp.concatenate(ydiag, axis=1) + yoff)
        per_group.append(dict(Bb=Bb, Cb=Cb, CB=CB, Sg=Sg, yoff=yoff, pairs=pairs))
    Y = jnp.concatenate(ys, axis=1) + c["X"] * dskipE
    return Y, per_group


def _ssd_specs(S, rev):
    nc = S // CHUNK
    cm = (lambda b, i: (b, nc - 1 - i)) if rev else (lambda b, i: (b, i))
    xa = pl.BlockSpec((None, CHUNK, CONV_CH), lambda b, i: cm(b, i) + (0,))
    z = pl.BlockSpec((None, CHUNK, SSM_W), lambda b, i: cm(b, i) + (OFF["z"] // SSM_W,))
    dt = pl.BlockSpec((None, CHUNK, 128), lambda b, i: cm(b, i) + (OFF["dt"] // 128,))
    tok = pl.BlockSpec((None, CHUNK, SSM_W), lambda b, i: cm(b, i) + (0,))
    st = pl.BlockSpec((None, None, SSM_W, 128), lambda b, i: cm(b, i) + (0, 0))
    return nc, xa, z, dt, tok, st


def _ssd_fwd(xact, P, bias, alog, dskipE, ng, name):
    B, S, _ = P.shape
    nc, xa_spec, z_spec, dt_spec, tok, st_spec = _ssd_specs(S, False)
    E, ET, tri, _ = _ssd_consts()

    def body(xa_ref, z_ref, dt_ref, bias_ref, alog_ref, dsk_ref, ng_ref, E_ref, ET_ref, tri_ref, o_ref, sp_ref, state):
        @pl.when(pl.program_id(1) == 0)
        def _():
            state[...] = jnp.zeros((SSM_W, 128), F32)

        sp_ref[...] = state[...]
        xa = xa_ref[...]
        c = _ssd_pre(xa, dt_ref[...], bias_ref[...], alog_ref[...], E_ref[...], ET_ref[...], tri_ref[...])
        Y, groups = _ssd_y(c, xa, state, dsk_ref[...])
        Z = (c["xdt"] * c["wE"]).astype(BF16)
        for g in range(SSM_G):
            gs = slice(g * 512, (g + 1) * 512)
            state[gs, :] = groups[g]["Sg"] * c["cdcol"][gs, :] + _dot_tn(Z[:, gs], groups[g]["Bb"])
        zv = z_ref[...]
        yz = Y * (zv * _sigmoid(zv))
        outs = []
        for g in range(SSM_G):
            yg = yz[:, g * 512:(g + 1) * 512]
            outs.append(yg * lax.rsqrt(jnp.mean(yg * yg, axis=-1, keepdims=True) + EPS))
        o_ref[...] = (jnp.concatenate(outs, axis=1) * ng_ref[...]).astype(BF16)

    return pl.pallas_call(
        body, name=name, grid=(B, nc),
        in_specs=[xa_spec, z_spec, dt_spec, _full((1, 128)), _full((1, 128)), _full((1, SSM_W)), _full((1, SSM_W)),
                  _full((128, SSM_W)), _full((SSM_W, 128)), _full((128, 128))],
        out_specs=[tok, st_spec],
        out_shape=[jax.ShapeDtypeStruct((B, S, SSM_W), BF16), jax.ShapeDtypeStruct((B, nc, SSM_W, 128), F32)],
        scratch_shapes=[pltpu.VMEM((SSM_W, 128), F32)],
        compiler_params=pltpu.CompilerParams(dimension_semantics=("arbitrary", "arbitrary")),
    )(xact, P, P, bias, alog, dskipE, ng, E, ET, tri)


def _ssd_bwd(xact, P, sprev, dcat, bias, alog, dskipE, ng, name):
    B, S, _ = P.shape
    nc, xa_spec, z_spec, dt_spec, tok, st_spec = _ssd_specs(S, True)
    do_spec = pl.BlockSpec((None, CHUNK, SSM_W), lambda b, i: (b, nc - 1 - i, 1))
    E, ET, tri, triT = _ssd_consts()
    dt_out = pl.BlockSpec((None, CHUNK, 128), lambda b, i: (b, nc - 1 - i, 0))

    def body(xa_ref, z_ref, dt_ref, sp_ref, do_ref, bias_ref, alog_ref, dsk_ref, ng_ref, E_ref, ET_ref, tri_ref, triT_ref,
             dxa_ref, dz_ref, ddt_ref, dbias_ref, dalog_ref, ddsk_ref, dng_ref, dstate):
        first = (pl.program_id(0) == 0) & (pl.program_id(1) == 0)

        @pl.when(first)
        def _():
            for ref in (dbias_ref, dalog_ref, ddsk_ref, dng_ref):
                ref[...] = jnp.zeros(ref.shape, F32)

        @pl.when(pl.program_id(1) == 0)
        def _():
            dstate[...] = jnp.zeros((SSM_W, 128), F32)

        xa, ETm = xa_ref[...], ET_ref[...]
        c = _ssd_pre(xa, dt_ref[...], bias_ref[...], alog_ref[...], E_ref[...], ETm, tri_ref[...])
        Y, groups = _ssd_y(c, xa, sp_ref, dsk_ref[...])
        X, xdt = c["X"], c["xdt"]
        zv = z_ref[...]
        sg = _sigmoid(zv)
        zs = zv * sg
        yz = Y * zs
        dout = do_ref[...]
        dyz = []
        for g in range(SSM_G):
            gs = slice(g * 512, (g + 1) * 512)
            yg = yz[:, gs]
            r = lax.rsqrt(jnp.mean(yg * yg, axis=-1, keepdims=True) + EPS)
            yn = yg * r
            dng_ref[:, gs] += jnp.sum(dout[:, gs] * yn, axis=0, keepdims=True)
            dyn = dout[:, gs] * ng_ref[:, gs]
            dyz.append(r * (dyn - yn * jnp.mean(dyn * yn, axis=-1, keepdims=True)))
        dyz = jnp.concatenate(dyz, axis=1)
        dz_ref[...] = (dyz * Y * (sg * (1.0 + zv * (1.0 - sg)))).astype(BF16)
        dY = dyz * zs
        ddsk_ref[...] += jnp.sum(dY * X, axis=0, keepdims=True)
        dX = dY * dsk_ref[...]
        lane = lax.broadcasted_iota(jnp.int32, (128, 128), 1)
        sub = lax.broadcasted_iota(jnp.int32, (128, 128), 0)
        colform = jnp.zeros((128, 128), F32)
        rowform = jnp.zeros((128, 128), F32)
        dxdt, gacsE, dBC = [], [], []
        for g in range(SSM_G):
            gs = slice(g * 512, (g + 1) * 512)
            G = groups[g]
            Bb, Cb, CB, Sg = G["Bb"], G["Cb"], G["CB"], G["Sg"]
            dYg = dY[:, gs]
            dQ = (dYg * c["eE"][:, gs]).astype(BF16)
            dSn = dstate[gs, :]
            dSnb = dSn.astype(BF16)
            cd = c["cdcol"][gs, :]
            dC = _dot(dQ, Sg.astype(BF16))
            dSprev = _dot_tn(dQ, Cb) + dSn * cd
            hcol = jnp.sum(_dot(E_ref[:, gs], dSn * Sg * cd, HI), axis=-1, keepdims=True)
            rowform = rowform + jnp.where(lane == 127, hcol, 0.0)
            Zg = xdt[:, gs] * c["wE"][:, gs]
            dZ = _dot_nt(Bb, dSnb)
            dB = _dot(Zg.astype(BF16), dSnb)
            U = dZ * Zg
            ga = dYg * G["yoff"] - U
            ga = ga + jnp.where(lax.broadcasted_iota(jnp.int32, (128, 512), 0) == 127, jnp.sum(U, axis=0, keepdims=True), 0.0)
            gacsE.append(ga)
            dxg = [None] * 4
            dCB = jnp.zeros((128, 128), F32)
            for j in range(4):
                h0 = g * 8 + 2 * j
                L0, L1, M, BD = G["pairs"][j]
                dYp = dYg[:, j * 128:(j + 1) * 128].astype(BF16)
                dM = _dot_nt(dYp, BD)
                dBD = _dot_tn(M, dYp)
                dxg[j] = jnp.where(lane < 64, dBD[:128], dBD[128:])
                for t, (h, L) in enumerate(((h0, L0), (h0 + 1, L1))):
                    dMh = dM[:, t * 128:(t + 1) * 128]
                    dCB = dCB + dMh * L
                    Gh = dMh * CB * L
                    colform = colform + jnp.where(lane == h, jnp.sum(Gh, axis=1, keepdims=True), 0.0)
                    rowform = rowform - jnp.where(sub == h, jnp.sum(Gh, axis=0, keepdims=True), 0.0)
            dCBb = dCB.astype(BF16)
            dC = dC + _dot(dCBb, Bb)
            dB = dB + _dot_tn(dCBb, Cb)
            dxdt.append(jnp.concatenate(dxg, axis=1) + dZ * c["wE"][:, gs])
            dBC.append((dB, dC))
            dstate[gs, :] = dSprev
        dxdt = jnp.concatenate(dxdt, axis=1)
        dX = dX + dxdt * c["dtE"]
        ddt = _dot(dxdt * X, ETm, HI)
        dacs = colform + rowform.T + _dot(jnp.concatenate(gacsE, axis=1), ETm, HI)
        dda = _dot(triT_ref[...], dacs, HI)
        ddt = ddt + dda * c["a"]
        dalog_ref[...] += jnp.sum(dda * c["dtp"], axis=0, keepdims=True) * c["a"]
        ddtraw = jnp.where(lane < SSM_H, ddt * _sigmoid(c["pre"]), 0.0)
        dbias_ref[...] += jnp.sum(ddtraw, axis=0, keepdims=True)
        ddt_ref[...] = ddtraw.astype(BF16)
        dxa_ref[...] = jnp.concatenate([dX, dBC[0][0], dBC[1][0], dBC[0][1], dBC[1][1]], axis=1)

    p128, p1k = _full((1, 128)), _full((1, SSM_W))
    return pl.pallas_call(
        body, name=name, grid=(B, nc),
        in_specs=[xa_spec, z_spec, dt_spec, st_spec, do_spec, p128, p128, p1k, p1k,
                  _full((128, SSM_W)), _full((SSM_W, 128)), _full((128, 128)), _full((128, 128))],
        out_specs=[xa_spec, tok, dt_out, p128, p128, p1k, p1k],
        out_shape=[jax.ShapeDtypeStruct((B, S, CONV_CH), F32), jax.ShapeDtypeStruct((B, S, SSM_W), BF16),
                   jax.ShapeDtypeStruct((B, S, 128), BF16), jax.ShapeDtypeStruct((1, 128), F32),
                   jax.ShapeDtypeStruct((1, 128), F32), jax.ShapeDtypeStruct((1, SSM_W), F32),
                   jax.ShapeDtypeStruct((1, SSM_W), F32)],
        scratch_shapes=[pltpu.VMEM((SSM_W, 128), F32)],
        compiler_params=pltpu.CompilerParams(dimension_semantics=("arbitrary", "arbitrary")),
    )(xact, P, P, sprev, dcat, bias, alog, dskipE, ng, E, ET, tri, triT)


def _adamw(w, parts, m, v, name, tr=512, row0=0, prev=None):
    Rtot, C = w.shape
    ns, R = parts.shape[0], parts.shape[1]
    tr = min(tr, R)
    assert R % tr == 0 and row0 % tr == 0
    off = row0 // tr
    c1 = 1.0 / (1.0 - ADAM_B1 ** ADAM_STEP)
    c2 = 1.0 / (1.0 - ADAM_B2 ** ADAM_STEP)

    def body(w_ref, p_ref, m_ref, v_ref, *rest):
        g_ref, d_ref, mo_ref, vo_ref = rest[-4:]
        g = p_ref[0].astype(F32)
        for s in range(1, ns):
            g = g + p_ref[s].astype(F32)
        mn = ADAM_B1 * m_ref[...] + (1.0 - ADAM_B1) * g
        vn = ADAM_B2 * v_ref[...] + (1.0 - ADAM_B2) * (g * g)
        g_ref[...] = g
        mo_ref[...] = mn
        vo_ref[...] = vn
        d_ref[...] = -ADAM_LR * ((mn * c1) / (jnp.sqrt(vn * c2) + ADAM_EPS) + ADAM_WD * w_ref[...])

    blk = pl.BlockSpec((tr, C), lambda i: (i + off, 0))
    extra = [] if prev is None else list(prev)
    return pl.pallas_call(
        body, name=name, grid=(R // tr,),
        in_specs=[blk, pl.BlockSpec((ns, tr, C), lambda i: (0, i, 0)), blk, blk] + [pl.BlockSpec(memory_space=pl.ANY)] * len(extra),
        out_specs=[blk] * 4, out_shape=[jax.ShapeDtypeStruct((Rtot, C), F32)] * 4,
        input_output_aliases={4 + k: k for k in range(len(extra))})(w, parts, m, v, *extra)


_SMALL = ("ada_b", "norm1_g", "gm_ln_g", "gm_ln_b", "gm_ws", "gm_bs", "gm_norm_g", "attn_sinks", "attn_norm_g", "conv_b",
          "dt_bias", "a_log", "d_skip", "ssm_norm_g", "norm2_g", "final_norm_g")


def _pack(arrs):
    flat = []
    for a in arrs:
        f = a.reshape(-1).astype(F32)
        flat.append(jnp.pad(f, (0, (-f.shape[0]) % 1024)))
    return jnp.concatenate(flat).reshape(-1, 128)


def _unpack(pack, like):
    out, o = [], 0
    flat = pack.reshape(-1)
    for a in like:
        n = int(np.prod(a.shape))
        out.append(flat[o:o + n].reshape(a.shape))
        o += n + (-n) % 1024
    return out


def kernel(x, c, ada_w, ada_b, norm1_g, w_in, gm_ln_g, gm_ln_b, gm_ws, gm_bs, gm_norm_g, attn_sinks, attn_norm_g, conv_w, conv_b, dt_bias, a_log, d_skip, ssm_norm_g, w_out, norm2_g, w_mlp1, w_mlp2, final_norm_g, loss_target, m_ada_w, m_ada_b, m_norm1_g, m_w_in, m_gm_ln_g, m_gm_ln_b, m_gm_ws, m_gm_bs, m_gm_norm_g, m_attn_sinks, m_attn_norm_g, m_conv_w, m_conv_b, m_dt_bias, m_a_log, m_d_skip, m_ssm_norm_g, m_w_out, m_norm2_g, m_w_mlp1, m_w_mlp2, m_final_norm_g, v_ada_w, v_ada_b, v_norm1_g, v_w_in, v_gm_ln_g, v_gm_ln_b, v_gm_ws, v_gm_bs, v_gm_norm_g, v_attn_sinks, v_attn_norm_g, v_conv_w, v_conv_b, v_dt_bias, v_a_log, v_d_skip, v_ssm_norm_g, v_w_out, v_norm2_g, v_w_mlp1, v_w_mlp2, v_final_norm_g):
    args = dict(locals())
    B, S, _ = x.shape
    T = B * S
    L = DEPTH
    me = 4 * lax.axis_index("x") + 2 * lax.axis_index("y") + lax.axis_index("c")

    gath = _gather2([c, conv_w], "ag_c")
    big = ("w_in", "w_out", "w_mlp1", "w_mlp2")
    chain = [(n, l) for l in range(L) for n in ("w_in", "w_mlp1", "w_out", "w_mlp2")]
    inflight = {}

    def start_next(order):
        if not chain:
            return jnp.zeros((8, 128), F32)
        n, l = chain.pop(0)
        sems, v_thru, land_thru, token = _gather_start(args[n][l].astype(BF16), order, f"ag_start_{n}{l}")
        inflight[n, l] = (sems, v_thru, land_thru)
        return token

    def gathered(n, l, after):
        v_done, land = _gather_wait(*inflight.pop((n, l)), after, f"ag_wait_{n}{l}")
        return _gather_finish([v_done], [land], f"ag_fin_{n}{l}")[0]

    c_all = gath[0].reshape(NDEV * B, D)
    c_act = (c_all * jax.nn.sigmoid(c_all)).astype(BF16)
    nb_rows = c_act.shape[0]
    c_pad = jnp.pad(c_act, ((0, 128 - nb_rows), (0, 0)))
    adw = ada_w.astype(BF16)
    mod_part = jnp.stack([_mm(c_pad, adw[l], mode="nn", name=f"mod{l}", tn=768)[:nb_rows] for l in range(L)])
    mod_all = _exchange([mod_part], "ag_mod", False)[0]
    mod_mine = lax.dynamic_slice_in_dim(mod_all, me * B, B, axis=2)
    mod = jnp.transpose(mod_mine, (1, 2, 0, 3)).reshape(L, B, 6 * D) + ada_b[:, None, :]
    mods = [[mod[l][:, None, i * D:(i + 1) * D] for i in range(6)] for l in range(L)]

    win_g, wout_g, w1_g, w2_g = [None] * L, [None] * L, [None] * L, [None] * L

    tril = jnp.tril(jnp.ones((128, 128), F32))
    row = lambda a: a.reshape(1, -1)
    pad128 = lambda a: jnp.pad(a.reshape(1, -1), ((0, 0), (0, 128 - a.shape[-1])))
    small = []
    for l in range(L):
        wt = gm_ws[l] * tril
        small.append(dict(
            lng=row(gm_ln_g[l]), lnb=row(gm_ln_b[l]), wt=wt.astype(BF16), wtT=jnp.swapaxes(wt, 1, 2).astype(BF16),
            bsx=jnp.repeat(gm_bs[l].T, 128, axis=1), gog=row(gm_norm_g[l]), sinks=attn_sinks[l], aog=row(attn_norm_g[l]),
            bias=pad128(dt_bias[l]), alog=pad128(a_log[l]), dskE=jnp.repeat(d_skip[l], SSM_HD).reshape(1, SSM_W),
            sng=row(ssm_norm_g[l]), cb=row(conv_b[l])))
    convw_all = jnp.transpose(gath[1], (1, 2, 0, 3)).reshape(L, 4, CONV_CH)
    convw8 = jnp.pad(convw_all, ((0, 0), (0, 4), (0, 0)))

    saved = []
    xl = x
    tok = start_next(mod)
    h = _norm_fwd(xl, row(norm1_g[0]) + tok[0, 0], mods[0][1], mods[0][0], "norm1_f0")
    for l in range(L):
        sm = small[l]
        g_in = gathered("w_in", l, h)
        tok = start_next(g_in)
        win_g[l] = _to_work_cols(jnp.transpose(g_in, (1, 0, 2)).reshape(D, IN_W))
        P = _mm(h.reshape(T, D), win_g[l], mode="nn", name=f"proj_in{l}", tn=1536, order=tok).reshape(B, S, PW)
        out_a = _gmlp_fwd(P, sm["lng"], sm["lnb"], sm["wt"], sm["bsx"], sm["gog"], f"gmlp_f{l}")
        out_b = _attn_fwd(P, sm["sinks"], sm["aog"], f"attn_f{l}")
        xact = _conv_fwd(P, convw8[l], sm["cb"], f"conv_f{l}")
        w1_g[l] = gathered("w_mlp1", l, xact)
        tok = start_next(w1_g[l])
        out_c, sprev = _ssd_fwd(xact, P, sm["bias"], sm["alog"], sm["dskE"], sm["sng"] + tok[0:1, 0:1], f"ssd_f{l}")
        cat = jnp.concatenate([out_a, out_b, out_c], axis=-1)
        g_out = gathered("w_out", l, cat)
        tok = start_next(g_out)
        wout_g[l] = g_out.reshape(D, D)
        mix = _mm(cat.reshape(T, D), wout_g[l], mode="nn", name=f"proj_out{l}", order=tok).reshape(B, S, D)
        x_mid, h2 = _norm_fwd(xl, row(norm2_g[l]), mods[l][4], mods[l][3], f"norm2_f{l}", resid=(mix, mods[l][2]))
        a_act, r_act = _mm(h2.reshape(T, D), w1_g[l], mode="nn", name=f"mlp1_{l}", out_dtypes=(BF16, BF16), col_blocked_b=True,
                           epilogue=lambda acc: (acc, jnp.square(jnp.maximum(acc, 0.0))))
        g_2 = gathered("w_mlp2", l, r_act)
        tok = start_next(g_2)
        w2_g[l] = g_2.reshape(DFF, D)
        m2 = _mm(r_act, w2_g[l], mode="nn", name=f"mlp2_{l}", order=tok).reshape(B, S, D)
        saved.append(dict(x_in=xl, h=h, P=P, xact=xact, sprev=sprev, cat=cat, mix=mix, x_mid=x_mid, h2=h2, a=a_act, r=r_act, m2=m2))
        if l + 1 < L:
            xl, h = _norm_fwd(x_mid, row(norm1_g[l + 1]), mods[l + 1][1], mods[l + 1][0], f"norm1_f{l + 1}", resid=(m2, mods[l][5]))

    sv = saved[L - 1]
    nb = _norm_bwd(sv["x_mid"], row(final_norm_g), "final_b", tgt=loss_target, br=sv["m2"], gate=mods[L - 1][5], x_is_prev=True)
    loss_part, g_final = nb["loss"], nb["dg"]
    dmod, gsm, gconvw = [None] * L, [None] * L, [None] * L
    core = lax.axis_index("c").astype(jnp.int32).reshape(1)
    reducing = []

    def reduce_start(n, l, p, order):
        from_sib = _pair_exchange([p], f"rs_pair_{n}{l}")[0]
        s, land = _pair_add(p, from_sib, core, f"rs_add_{n}{l}")
        sems, s_thru, land_thru, token = _chipsum_start(s, land, order, f"rs_start_{n}{l}")
        reducing.append((n, l, sems, s_thru, land_thru))
        return token

    for l in reversed(range(L)):
        sv, sm = saved[l], small[l]
        dm2, dxo, dg2 = nb["dbr"].reshape(T, D), nb["dx"], nb["dgate"]
        da = _mm(dm2, w2_g[l], mode="nt", name=f"mlp2_dx{l}", out_dtypes=(BF16,), extras=(sv["a"],),
                 epilogue=lambda acc, a: (acc * (2.0 * jnp.maximum(a.astype(F32), 0.0)),))
        dw2 = _mm(sv["r"], dm2, mode="tn", name=f"mlp2_dw{l}", out_dtypes=(BF16,), tk=1024).reshape(4, 2, DFF // NDEV, D)
        tok = reduce_start("w_mlp2", l, dw2, da)
        dh2 = _mm(da, w1_g[l], mode="nt", name=f"mlp1_dx{l}", col_blocked_b=True, order=tok).reshape(B, S, D)
        dw1 = _mm(sv["h2"].reshape(T, D), da, mode="tn", name=f"mlp1_dw{l}", out_dtypes=(BF16,), tk=1024,
                  col_blocked_out=True).reshape(4, 2, D, DFF // NDEV)
        tok = reduce_start("w_mlp1", l, dw1, dh2)
        nb2 = _norm_bwd(sv["x_mid"], row(norm2_g[l]) + tok[0, 0], f"norm2_b{l}", sc=mods[l][4], dh=dh2, dres=dxo, br=sv["mix"],
                        gate=mods[l][2])
        dmix = nb2["dbr"].reshape(T, D)
        dcat = _mm(dmix, wout_g[l], mode="nt", name=f"proj_out_dx{l}").reshape(B, S, D)
        du, dv, dlng, dlnb, dws, dbsx, dgog = _gmlp_bwd(sv["P"], dcat, sm["lng"], sm["lnb"], sm["wt"], sm["wtT"], sm["bsx"],
                                                        sm["gog"], f"gmlp_b{l}")
        dq, dk, dvv, dsink, daog = _attn_bwd(sv["P"], dcat, sm["sinks"], sm["aog"], f"attn_b{l}")
        dwo = _mm(sv["cat"].reshape(T, D), dmix, mode="tn", name=f"proj_out_dw{l}", out_dtypes=(BF16,), tk=1024,
                  order=dq).reshape(4, 2, D // NDEV, D)
        tok = reduce_start("w_out", l, dwo, dmix)
        dxa, dz, ddt, dbias, dalog, ddsk, dsng = _ssd_bwd(sv["xact"], sv["P"], sv["sprev"], dcat, sm["bias"], sm["alog"],
                                                          sm["dskE"], sm["sng"] + tok[0:1, 0:1], f"ssd_b{l}")
        dxbc, dcw, dcb = _conv_bwd(sv["P"], dxa, convw8[l], sm["cb"], f"conv_b{l}")
        dP = jnp.concatenate([dxbc, du, dz, dv, dq, dk, dvv, ddt, jnp.zeros((B, S, PW - OFF["dt"] - 128), BF16)],
                             axis=-1).reshape(T, PW)
        dwin = _mm(sv["h"].reshape(T, D), dP, mode="tn", name=f"proj_in_dw{l}", out_dtypes=(BF16,), tn=1536, tk=1024)
        dwin = jnp.transpose(_from_work_cols(dwin).reshape(D, NDEV, IN_W // NDEV), (1, 0, 2)).reshape(4, 2, D, IN_W // NDEV)
        tok = reduce_start("w_in", l, dwin, dP)
        dh = _mm(dP, win_g[l], mode="nt", name=f"proj_in_dx{l}", tk=1536, order=tok).reshape(B, S, D)
        nb = _norm_bwd(sv["x_in"], row(norm1_g[l]), f"norm1_b{l}", sc=mods[l][1], dh=dh, dres=nb2["dx"],
                       br=saved[l - 1]["m2"] if l > 0 else None, gate=mods[l - 1][5] if l > 0 else None)
        dmod[l] = jnp.concatenate([nb["dsh"], nb["dsc"], nb2["dgate"], nb2["dsh"], nb2["dsc"], dg2], axis=-1)
        gconvw[l] = dcw[:4]
        gsm[l] = dict(
            ada_b=jnp.sum(dmod[l], axis=(0, 1)), norm1_g=nb["dg"], gm_ln_g=dlng, gm_ln_b=dlnb, gm_ws=dws,
            gm_bs=dbsx.reshape(128, GM_H, 128).sum(-1).T, gm_norm_g=dgog, attn_sinks=dsink[:, 0], attn_norm_g=daog,
            conv_b=dcb, dt_bias=dbias[0, :SSM_H], a_log=dalog[0, :SSM_H], d_skip=ddsk.reshape(SSM_H, SSM_HD).sum(-1),
            ssm_norm_g=dsng, norm2_g=nb2["dg"])
    grad_x = nb["dx"]

    per_layer = [n for n in _SMALL if n != "final_norm_g"]
    g_small = [jnp.stack([gsm[l][n].reshape(args[n].shape[1:]) for l in range(L)]) for n in per_layer] + [g_final.reshape(D)]
    zc = jnp.zeros((L, 4, CONV_CH), F32)
    z1 = jnp.zeros((1, 128), F32)
    gpack = _pack([loss_part] + g_small + [jnp.stack(gconvw)])
    got = _exchange([jnp.stack(dmod).reshape(L, B, 6 * D), gpack], "ag_small", False)
    like = [z1] + [args[n] for n in _SMALL] + [zc]
    packs = [_pack([z1] + [args[p + n] for n in _SMALL] + [zc]) for p in ("", "m_", "v_")]
    sres = [_unpack(p, like) for p in _adamw(packs[0], got[1], packs[1], packs[2], "adamw_small", tr=gpack.shape[0])]
    res = {n: [r[1 + i] for r in sres] for i, n in enumerate(_SMALL)}
    loss = sres[0][0][0, 0]
    gcw = lax.dynamic_slice_in_dim(sres[0][-1], me * (CONV_CH // NDEV), CONV_CH // NDEV, axis=2)

    def update(name, parts, tr):
        w = args[name]
        r = _adamw(w.reshape(-1, w.shape[-1]), parts, args["m_" + name].reshape(-1, w.shape[-1]),
                   args["v_" + name].reshape(-1, w.shape[-1]), "adamw_" + name, tr=tr)
        res[name] = [a.reshape(w.shape) for a in r]

    update("conv_w", gcw.reshape(1, L * 4, CONV_CH // NDEV), L * 4)

    dmod_all = jnp.transpose(got[0], (1, 0, 2, 3)).reshape(L, NDEV * B, 6 * D)
    dm_mine = lax.dynamic_slice_in_dim(dmod_all, me * (6 * D // NDEV), 6 * D // NDEV, axis=2)
    dm_pad = jnp.pad(dm_mine, ((0, 0), (0, 128 - nb_rows), (0, 0))).astype(BF16)
    g_adaw = jnp.stack([_mm(c_pad, dm_pad[l], mode="tn", name=f"ada_dw{l}", tn=768) for l in range(L)])
    update("ada_w", g_adaw.reshape(1, L * D, 6 * D // NDEV), 256)

    big_res, after = dict.fromkeys(big), res["ada_w"][0]
    tile_rows = dict(w_in=256, w_out=256, w_mlp1=256, w_mlp2=128)
    for n, l, sems, s_thru, land_thru in reducing:
        parts = _chipsum_wait(sems, s_thru, land_thru, after, f"rs_wait_{n}{l}")
        w = args[n]
        big_res[n] = _adamw(w.reshape(-1, w.shape[-1]), parts, args["m_" + n].reshape(-1, w.shape[-1]),
                            args["v_" + n].reshape(-1, w.shape[-1]), f"adamw_{n}{l}", tr=tile_rows[n], row0=l * w.shape[1],
                            prev=big_res[n])
        after = big_res[n][0]
    for n in big:
        res[n] = [a.reshape(args[n].shape) for a in big_res[n]]

    names = ['ada_w', 'ada_b', 'norm1_g', 'w_in', 'gm_ln_g', 'gm_ln_b', 'gm_ws', 'gm_bs', 'gm_norm_g', 'attn_sinks',
             'attn_norm_g', 'conv_w', 'conv_b', 'dt_bias', 'a_log', 'd_skip', 'ssm_norm_g', 'w_out', 'norm2_g', 'w_mlp1',
             'w_mlp2', 'final_norm_g']
    return (loss, grad_x, *[res[n][0] for n in names], *[res[n][1] for n in names], *[res[n][2] for n in names],
            *[res[n][3] for n in names])
```

```python
import functools

import jax
import jax.numpy as jnp
import numpy as np
from jax import lax
from jax.experimental import pallas as pl
from jax.experimental.pallas import tpu as pltpu

F32, BF16 = jnp.float32, jnp.bfloat16
HI = lax.Precision.HIGHEST
MESH = pl.DeviceIdType.MESH
NDEV = 8

D = 2048
DEPTH = 2
CHUNK = 128
GM_W, GM_H = 512, 4
ATT_W, KV_W, ATT_H = 512, 128, 8
SSM_W, SSM_H, SSM_HD, SSM_G = 1024, 16, 64, 2
CONV_CH = 1536
IN_W = 4368
DFF = 8192
EPS = 1e-6
NEG_INF = -1e30
GELU_K = 0.7978845608028654
GELU_C = 0.044715

_ORIG = (("u", 512), ("v", 512), ("q", 512), ("k", 128), ("vv", 128), ("z", 1024), ("xbc", 1536), ("dt", 16))
OFF = dict(xbc=0, u=1536, z=2048, v=3072, q=3584, k=4096, vv=4224, dt=4352)
PW = 4608

ADAM_LR, ADAM_B1, ADAM_B2, ADAM_EPS, ADAM_WD, ADAM_STEP = 0.001, 0.9, 0.999, 1e-08, 0.01, 10


def _to_work_cols(w):
    parts, o = {}, 0
    for name, wd in _ORIG:
        parts[name] = w[..., o:o + wd]
        o += wd
    z = lambda n: jnp.zeros(w.shape[:-1] + (n,), w.dtype)
    return jnp.concatenate([parts["xbc"], parts["u"], parts["z"], parts["v"], parts["q"], parts["k"], parts["vv"],
                            parts["dt"], z(PW - OFF["dt"] - 16)], axis=-1)


def _from_work_cols(wp):
    return jnp.concatenate([wp[..., OFF[name]:OFF[name] + wd] for name, wd in _ORIG], axis=-1)


def _sigmoid(x):
    return 1.0 / (1.0 + jnp.exp(-x))


def _gelu(x):
    return 0.5 * x * (1.0 + jnp.tanh(GELU_K * (x + GELU_C * x * x * x)))


def _gelu_grad(x):
    t = jnp.tanh(GELU_K * (x + GELU_C * x * x * x))
    return 0.5 * (1.0 + t) + 0.5 * x * (1.0 - t * t) * GELU_K * (1.0 + 3.0 * GELU_C * x * x)


def _dot(a, b, prec=None):
    return jnp.dot(a, b, precision=prec, preferred_element_type=F32)


def _dot_nt(a, b, prec=None):
    return lax.dot_general(a, b, (((1,), (1,)), ((), ())), precision=prec, preferred_element_type=F32)


def _dot_tn(a, b, prec=None):
    return lax.dot_general(a, b, (((0,), (0,)), ((), ())), precision=prec, preferred_element_type=F32)


def _full(shape):
    return pl.BlockSpec(shape, lambda *_: (0,) * len(shape))


_HBM = pl.BlockSpec(memory_space=pltpu.HBM)


def _me():
    return lax.axis_index("x"), lax.axis_index("y"), lax.axis_index("c")


def _peer(k):
    x, y, c = _me()
    px = 1 - x if k & 4 else x
    py = 1 - y if k & 2 else y
    pc = 1 - c if k & 1 else c
    return (px, py, pc), 4 * px + 2 * py + pc


def _exchange(xs, name, scatter):
    n = len(xs)

    def body(*refs):
        ins, outs = refs[:n], refs[n:2 * n]
        send, recv, loc = refs[2 * n:]
        x, y, c = _me()
        me = 4 * x + 2 * y + c
        started = []
        for i in range(n):
            own = pltpu.make_async_copy(ins[i].at[me] if scatter else ins[i], outs[i].at[me], loc.at[i])
            own.start()
            started.append(own)
        for k in range(1, NDEV):
            dev, lin = _peer(k)
            for i in range(n):
                pltpu.make_async_remote_copy(
                    src_ref=ins[i].at[lin] if scatter else ins[i], dst_ref=outs[i].at[me],
                    send_sem=send.at[i, k - 1], recv_sem=recv.at[i, k - 1], device_id=dev, device_id_type=MESH).start()
        for k in range(1, NDEV):
            dev, lin = _peer(k)
            for i in range(n):
                pltpu.make_async_remote_copy(
                    src_ref=ins[i].at[lin] if scatter else ins[i], dst_ref=outs[i].at[lin],
                    send_sem=send.at[i, k - 1], recv_sem=recv.at[i, k - 1], device_id=dev, device_id_type=MESH).wait()
        for own in started:
            own.wait()

    out_shape = [jax.ShapeDtypeStruct(a.shape if scatter else (NDEV,) + a.shape, a.dtype) for a in xs]
    return pl.pallas_call(
        body, name=name, out_shape=out_shape, in_specs=[_HBM] * n, out_specs=[_HBM] * n,
        scratch_shapes=[pltpu.SemaphoreType.DMA((n, NDEV - 1)), pltpu.SemaphoreType.DMA((n, NDEV - 1)),
                        pltpu.SemaphoreType.DMA((n,))],
        compiler_params=pltpu.CompilerParams(has_side_effects=True),
    )(*xs)


def _chips():
    x, y, c = _me()
    return x, y, c, [(1 - x, y), (x, 1 - y), (1 - x, 1 - y)]


def _gather2(xs, name):
    n = len(xs)

    def body(*refs):
        ins, outs = refs[:n], refs[n:2 * n]
        send, recv, loc = refs[2 * n:]
        x, y, c, chips = _chips()
        me, sib = (x, y, c), (x, y, 1 - c)

        def cp(i, k, block, to, src=None):
            slot = outs[i].at[4 * block[0] + 2 * block[1] + block[2]]
            return pltpu.make_async_remote_copy(src_ref=slot if src is None else src, dst_ref=slot, send_sem=send.at[i, k],
                                                recv_sem=recv.at[i, k], device_id=to, device_id_type=MESH)

        sent = []
        for i in range(n):
            for j, chip in enumerate(chips):
                sent.append(cp(i, 1 + j, me, (*chip, c), src=ins[i]))
            sent.append(cp(i, 0, me, sib, src=ins[i]))
        for s in sent:
            s.start()
        own = [pltpu.make_async_copy(ins[i], outs[i].at[4 * x + 2 * y + c], loc.at[i]) for i in range(n)]
        for o in own:
            o.start()
        for j, chip in enumerate(chips):
            for i in range(n):
                cp(i, 1 + j, (*chip, c), me).wait_recv()
                fwd = cp(i, 4 + j, (*chip, c), sib)
                fwd.start()
                sent.append(fwd)
        for i in range(n):
            cp(i, 0, sib, me).wait_recv()
            for j, chip in enumerate(chips):
                cp(i, 4 + j, (*chip, 1 - c), me).wait_recv()
        for s in sent:
            s.wait_send()
        for o in own:
            o.wait()

    return pl.pallas_call(
        body, name=name, out_shape=[jax.ShapeDtypeStruct((NDEV,) + a.shape, a.dtype) for a in xs],
        in_specs=[_HBM] * n, out_specs=[_HBM] * n,
        scratch_shapes=[pltpu.SemaphoreType.DMA((n, 7)), pltpu.SemaphoreType.DMA((n, 7)), pltpu.SemaphoreType.DMA((n,))],
        compiler_params=pltpu.CompilerParams(has_side_effects=True),
    )(*xs)


def _pair_exchange(ps, name):
    n = len(ps)

    def body(*refs):
        ins, outs = refs[:n], refs[n:2 * n]
        send, recv = refs[2 * n:]
        x, y, c = _me()
        cps = [pltpu.make_async_remote_copy(src_ref=ins[i].at[ch, 1 - c], dst_ref=outs[i].at[ch], send_sem=send.at[i, ch],
                                            recv_sem=recv.at[i, ch], device_id=(x, y, 1 - c), device_id_type=MESH)
               for i in range(n) for ch in range(4)]
        for cp in cps:
            cp.start()
        for cp in cps:
            cp.wait()

    return pl.pallas_call(
        body, name=name, out_shape=[jax.ShapeDtypeStruct((4,) + a.shape[2:], a.dtype) for a in ps],
        in_specs=[_HBM] * n, out_specs=[_HBM] * n,
        scratch_shapes=[pltpu.SemaphoreType.DMA((n, 4)), pltpu.SemaphoreType.DMA((n, 4))],
        compiler_params=pltpu.CompilerParams(has_side_effects=True),
    )(*ps)


def _pair_add(p, r1, core, name, tr=256):
    _, _, R, C = p.shape
    tr = min(tr, R)

    def body(core_ref, p_ref, r_ref, o_ref, o2_ref):
        s = (p_ref[...].astype(F32) + r_ref[...].astype(F32)).astype(o_ref.dtype)
        o_ref[...] = s
        o2_ref[...] = s

    blk = pl.BlockSpec((None, tr, C), lambda ch, i, core_ref: (ch, i, 0))
    return pl.pallas_call(
        body, name=name, out_shape=[jax.ShapeDtypeStruct((4, R, C), p.dtype)] * 2,
        grid_spec=pltpu.PrefetchScalarGridSpec(
            num_scalar_prefetch=1, grid=(4, R // tr),
            in_specs=[pl.BlockSpec((None, None, tr, C), lambda ch, i, core_ref: (ch, core_ref[0], i, 0)), blk],
            out_specs=[blk, blk]),
    )(core, p, r1)


def _chip_exchange(ss, name):
    n = len(ss)

    def body(*refs):
        ins, outs = refs[:n], refs[n:2 * n]
        send, recv, loc = refs[2 * n:]
        x, y, c, chips = _chips()
        mine = 2 * x + y
        own = [pltpu.make_async_copy(ins[i].at[mine], outs[i].at[mine], loc.at[i]) for i in range(n)]
        cps = []
        for j, (px, py) in enumerate(chips):
            for i in range(n):
                cps.append(pltpu.make_async_remote_copy(
                    src_ref=ins[i].at[2 * px + py], dst_ref=outs[i].at[mine], send_sem=send.at[i, j], recv_sem=recv.at[i, j],
                    device_id=(px, py, c), device_id_type=MESH))
        for cp in cps + own:
            cp.start()
        for j, (px, py) in enumerate(chips):
            for i in range(n):
                pltpu.make_async_remote_copy(
                    src_ref=ins[i].at[2 * px + py], dst_ref=outs[i].at[2 * px + py], send_sem=send.at[i, j],
                    recv_sem=recv.at[i, j], device_id=(px, py, c), device_id_type=MESH).wait()
        for o in own:
            o.wait()

    return pl.pallas_call(
        body, name=name, out_shape=[jax.ShapeDtypeStruct(a.shape, a.dtype) for a in ss],
        in_specs=[_HBM] * n, out_specs=[_HBM] * n,
        scratch_shapes=[pltpu.SemaphoreType.DMA((n, 3)), pltpu.SemaphoreType.DMA((n, 3)), pltpu.SemaphoreType.DMA((n,))],
        compiler_params=pltpu.CompilerParams(has_side_effects=True),
    )(*ss)


_SEM = pl.BlockSpec(memory_space=pltpu.SEMAPHORE)
_ANY = pl.BlockSpec(memory_space=pl.ANY)
_DATAFLOW = pltpu.SideEffectType.DATAFLOW_SIDE_EFFECTING


def _hbm(a):
    return pltpu.with_memory_space_constraint(a, pltpu.HBM)


def _gather_targets():
    x, y, c, chips = _chips()
    return 4 * x + 2 * y + c, [(x, y, 1 - c)] + [(*chip, c) for chip in chips]


def _landing_zone(v, me, name, tr=512):
    R, C = v.shape
    tr = min(tr, R)

    def body(me_ref, v_ref, o_ref):
        o_ref[...] = v_ref[...]

    return pl.pallas_call(
        body, name=name, out_shape=jax.ShapeDtypeStruct((NDEV, R, C), v.dtype),
        grid_spec=pltpu.PrefetchScalarGridSpec(
            num_scalar_prefetch=1, grid=(R // tr,), in_specs=[pl.BlockSpec((tr, C), lambda i, me_ref: (i, 0))],
            out_specs=pl.BlockSpec((None, tr, C), lambda i, me_ref: (me_ref[0], i, 0))),
    )(me, v)


def _gather_start(v, land, order, name):
    def body(v_ref, land_ref, order_ref, *rest):
        sems, token = rest[:8], rest[10]
        me, targets = _gather_targets()
        for k, to in enumerate(targets):
            pltpu.make_async_remote_copy(src_ref=v_ref, dst_ref=land_ref.at[me], send_sem=sems[k], recv_sem=sems[4 + k],
                                         device_id=to, device_id_type=MESH).start()
        token[...] = jnp.zeros_like(token)

    outs = pl.pallas_call(
        body, name=name,
        out_shape=(pltpu.SemaphoreType.DMA(()),) * 8 + (pltpu.HBM(v.shape, v.dtype), pltpu.HBM(land.shape, land.dtype),
                                                        jax.ShapeDtypeStruct((8, 128), F32)),
        in_specs=(_HBM, _HBM, _ANY), out_specs=(_SEM,) * 8 + (_HBM, _HBM, pl.BlockSpec(memory_space=pltpu.VMEM)),
        input_output_aliases={0: 8, 1: 9}, compiler_params=pltpu.CompilerParams(has_side_effects=_DATAFLOW),
    )(_hbm(v), _hbm(land), order)
    return outs[:8], outs[8], outs[9], outs[10]


def _gather_wait(sems, v_thru, land_thru, after, name):
    def body(v_ref, land_ref, *rest):
        sems_ = rest[:8]
        me, targets = _gather_targets()
        for k, to in enumerate(targets):
            cp = pltpu.make_async_remote_copy(src_ref=v_ref, dst_ref=land_ref.at[me], send_sem=sems_[k], recv_sem=sems_[4 + k],
                                              device_id=to, device_id_type=MESH)
            cp.wait_send()
            cp.wait_recv()

    return pl.pallas_call(
        body, name=name, out_shape=(pltpu.HBM(v_thru.shape, v_thru.dtype), pltpu.HBM(land_thru.shape, land_thru.dtype)),
        in_specs=(_HBM, _HBM) + (_SEM,) * 8 + (_ANY,), out_specs=(_HBM, _HBM), input_output_aliases={0: 0, 1: 1},
        compiler_params=pltpu.CompilerParams(has_side_effects=_DATAFLOW),
    )(v_thru, land_thru, *sems, after)


def _gather_finish(land, name):
    def body(land_ref, out, send, recv):
        x, y, c, chips = _chips()
        fwd = [pltpu.make_async_remote_copy(src_ref=out.at[4 * px + 2 * py + c], dst_ref=out.at[4 * px + 2 * py + c],
                                            send_sem=send.at[j], recv_sem=recv.at[j], device_id=(x, y, 1 - c), device_id_type=MESH)
               for j, (px, py) in enumerate(chips)]
        for cp in fwd:
            cp.start()
        for j, (px, py) in enumerate(chips):
            slot = out.at[4 * px + 2 * py + 1 - c]
            pltpu.make_async_remote_copy(src_ref=slot, dst_ref=slot, send_sem=send.at[j], recv_sem=recv.at[j],
                                         device_id=(x, y, 1 - c), device_id_type=MESH).wait()

    return pl.pallas_call(
        body, name=name, out_shape=jax.ShapeDtypeStruct(land.shape, land.dtype),
        in_specs=[_HBM], out_specs=_HBM, input_output_aliases={0: 0},
        scratch_shapes=[pltpu.SemaphoreType.DMA((3,)), pltpu.SemaphoreType.DMA((3,))],
        compiler_params=pltpu.CompilerParams(has_side_effects=True),
    )(land)


def _chip_targets():
    x, y, c, chips = _chips()
    return 2 * x + y, [((px, py, c), 2 * px + py) for px, py in chips]


def _chipsum_start(s, land, order, name):
    def body(s_ref, land_ref, order_ref, *rest):
        sems, token = rest[:6], rest[8]
        mine, targets = _chip_targets()
        for k, (to, ch) in enumerate(targets):
            pltpu.make_async_remote_copy(src_ref=s_ref.at[ch], dst_ref=land_ref.at[mine], send_sem=sems[k], recv_sem=sems[3 + k],
                                         device_id=to, device_id_type=MESH).start()
        token[...] = jnp.zeros_like(token)

    outs = pl.pallas_call(
        body, name=name,
        out_shape=(pltpu.SemaphoreType.DMA(()),) * 6 + (pltpu.HBM(s.shape, s.dtype), pltpu.HBM(land.shape, land.dtype),
                                                        jax.ShapeDtypeStruct((8, 128), F32)),
        in_specs=(_HBM, _HBM, _ANY), out_specs=(_SEM,) * 6 + (_HBM, _HBM, pl.BlockSpec(memory_space=pltpu.VMEM)),
        input_output_aliases={0: 6, 1: 7}, compiler_params=pltpu.CompilerParams(has_side_effects=_DATAFLOW),
    )(_hbm(s), _hbm(land), order)
    return outs[:6], outs[6], outs[7], outs[8]


def _chipsum_wait(sems, s_thru, land_thru, after, name):
    def body(s_ref, land_ref, *rest):
        sems_ = rest[:6]
        mine, targets = _chip_targets()
        for k, (to, ch) in enumerate(targets):
            cp = pltpu.make_async_remote_copy(src_ref=s_ref.at[ch], dst_ref=land_ref.at[ch], send_sem=sems_[k], recv_sem=sems_[3 + k],
                                              device_id=to, device_id_type=MESH)
            cp.wait_send()
            cp.wait_recv()

    return pl.pallas_call(
        body, name=name, out_shape=(pltpu.HBM(s_thru.shape, s_thru.dtype), pltpu.HBM(land_thru.shape, land_thru.dtype)),
        in_specs=(_HBM, _HBM) + (_SEM,) * 6 + (_ANY,), out_specs=(_HBM, _HBM), input_output_aliases={0: 0, 1: 1},
        compiler_params=pltpu.CompilerParams(has_side_effects=_DATAFLOW),
    )(s_thru, land_thru, *sems, after)[1]


def _mm(a, b, *, mode, name, out_dtypes=(F32,), epilogue=None, extras=(), tm=1024, tn=1024, tk=2048,
        col_blocked_b=False, col_blocked_out=False, order=None):
    CB = 1024
    if col_blocked_b:
        assert mode in ("nn", "nt") and b.shape[2] == CB
        (M, K), N = a.shape, (b.shape[0] * CB if mode == "nn" else b.shape[1])
        assert mode == "nn" or tk % CB == 0
        tn = CB if mode == "nn" else tn
    elif mode == "nn":
        (M, K), N = a.shape, b.shape[1]
    elif mode == "nt":
        (M, K), N = a.shape, b.shape[0]
    else:
        (K, M), N = a.shape, b.shape[1]
    if col_blocked_out:
        assert len(out_dtypes) == 1 and N % CB == 0
        tn = CB
    tm, tn, tk = min(tm, M), min(tn, N), min(tk, K)
    assert M % tm == 0 and N % tn == 0 and K % tk == 0, (M, N, K, tm, tn, tk)
    nk = K // tk
    ne, no = len(extras), len(out_dtypes)
    dims = {"nn": (((1,), (0,)), ((), ())), "nt": (((1,), (1,)), ((), ())), "tn": (((0,), (0,)), ((), ()))}[mode]

    no_ = 0 if order is None else 1

    def body(a_ref, b_ref, *rest):
        rest = rest[no_:]
        ex, outs = rest[:ne], rest[ne:ne + no]

        def finish(acc):
            res = epilogue(acc, *[e[...] for e in ex]) if epilogue is not None else (acc,)
            for o, r in zip(outs, res):
                o[...] = r.astype(o.dtype)

        if col_blocked_b and mode == "nt":
            part = sum(lax.dot_general(a_ref[:, q * CB:(q + 1) * CB], b_ref[q], dims, preferred_element_type=F32)
                       for q in range(tk // CB))
        else:
            part = lax.dot_general(a_ref[...], b_ref[...], dims, preferred_element_type=F32)
        if nk == 1:
            finish(part)
        else:
            acc_ref = rest[-1]
            k = pl.program_id(2)

            @pl.when(k == 0)
            def _():
                acc_ref[...] = part

            @pl.when(k > 0)
            def _():
                acc_ref[...] += part

            @pl.when(k == nk - 1)
            def _():
                finish(acc_ref[...])

    a_spec = {"nn": pl.BlockSpec((tm, tk), lambda i, j, k: (i, k)), "nt": pl.BlockSpec((tm, tk), lambda i, j, k: (i, k)),
              "tn": pl.BlockSpec((tk, tm), lambda i, j, k: (k, i))}[mode]
    b_spec = {"nn": pl.BlockSpec((tk, tn), lambda i, j, k: (k, j)), "nt": pl.BlockSpec((tn, tk), lambda i, j, k: (j, k)),
              "tn": pl.BlockSpec((tk, tn), lambda i, j, k: (k, j))}[mode]
    if col_blocked_b:
        b_spec = (pl.BlockSpec((None, tk, CB), lambda i, j, k: (j, k, 0)) if mode == "nn"
                  else pl.BlockSpec((tk // CB, tn, CB), lambda i, j, k: (k, j, 0)))
    e_spec = pl.BlockSpec((tm, tn), lambda i, j, k: (i, j))
    o_spec, o_dims = e_spec, (M, N)
    if col_blocked_out:
        o_spec, o_dims = pl.BlockSpec((None, tm, CB), lambda i, j, k: (j, i, 0)), (N // CB, M, CB)
    outs = pl.pallas_call(
        body, name=name, grid=(M // tm, N // tn, nk),
        in_specs=[a_spec, b_spec] + [_ANY] * no_ + [e_spec] * ne, out_specs=[o_spec] * no,
        out_shape=[jax.ShapeDtypeStruct(o_dims, dt) for dt in out_dtypes],
        scratch_shapes=[pltpu.VMEM((tm, tn), F32)] if nk > 1 else [],
        compiler_params=pltpu.CompilerParams(dimension_semantics=("parallel", "parallel", "arbitrary")),
    )(a, b, *([] if order is None else [order]), *extras)
    return outs if no > 1 else outs[0]


def _norm_fwd(x, g, sc, sh, name, resid=None):
    B, S, Dm = x.shape
    ts = min(S, 256)
    tok = pl.BlockSpec((None, ts, Dm), lambda b, i: (b, i, 0))
    row = pl.BlockSpec((None, 1, Dm), lambda b, i: (b, 0, 0))
    par = pl.BlockSpec((1, Dm), lambda b, i: (0, 0))

    def body(*refs):
        if resid is not None:
            x_ref, br_ref, gt_ref, g_ref, sc_ref, sh_ref, xo_ref, h_ref = refs
            xv = x_ref[...] + gt_ref[...] * br_ref[...]
            xo_ref[...] = xv
        else:
            x_ref, g_ref, sc_ref, sh_ref, h_ref = refs
            xv = x_ref[...]
        r = lax.rsqrt(jnp.mean(xv * xv, axis=-1, keepdims=True) + EPS)
        h_ref[...] = ((xv * r * g_ref[...]) * (1.0 + sc_ref[...]) + sh_ref[...]).astype(BF16)

    h_shape = jax.ShapeDtypeStruct((B, S, Dm), BF16)
    if resid is not None:
        return pl.pallas_call(body, name=name, grid=(B, S // ts), in_specs=[tok, tok, row, par, row, row],
                              out_specs=[tok, tok], out_shape=[jax.ShapeDtypeStruct((B, S, Dm), F32), h_shape],
                              )(x, resid[0], resid[1], g, sc, sh)
    return pl.pallas_call(body, name=name, grid=(B, S // ts), in_specs=[tok, par, row, row], out_specs=tok,
                          out_shape=h_shape)(x, g, sc, sh)


def _norm_bwd(x, g, name, *, sc=None, dh=None, dres=None, tgt=None, br=None, gate=None, x_is_prev=False):
    B, S, Dm = x.shape
    ts = min(S, 256)
    final = tgt is not None
    has_br = br is not None
    tok = pl.BlockSpec((None, ts, Dm), lambda b, i: (b, i, 0))
    row = pl.BlockSpec((None, 1, Dm), lambda b, i: (b, 0, 0))
    par = pl.BlockSpec((1, Dm), lambda b, i: (0, 0))
    ins, in_specs = [x, g], [tok, par]
    if final:
        ins, in_specs = ins + [tgt], in_specs + [tok]
    else:
        ins, in_specs = ins + [sc, dh], in_specs + [row, tok]
    if dres is not None:
        ins, in_specs = ins + [dres], in_specs + [tok]
    if has_br:
        ins, in_specs = ins + [br, gate], in_specs + [tok, row]
    n_in = len(ins)
    out_shape = [jax.ShapeDtypeStruct((B, S, Dm), F32), jax.ShapeDtypeStruct((1, Dm), F32)]
    out_specs = [tok, par]
    if final:
        out_shape.append(jax.ShapeDtypeStruct((1, 128), F32))
        out_specs.append(pl.BlockSpec((1, 128), lambda b, i: (0, 0)))
    else:
        out_shape += [jax.ShapeDtypeStruct((B, 1, Dm), F32)] * 2
        out_specs += [row, row]
    if has_br:
        out_shape += [jax.ShapeDtypeStruct((B, S, Dm), BF16), jax.ShapeDtypeStruct((B, 1, Dm), F32)]
        out_specs += [tok, row]

    def body(*refs):
        it = iter(refs[:n_in])
        outs = iter(refs[n_in:])
        x_ref, g_ref = next(it), next(it)
        b, i = pl.program_id(0), pl.program_id(1)
        first, first_row = (b == 0) & (i == 0), i == 0
        xv, gv = x_ref[...], g_ref[...]
        if x_is_prev:
            xv = xv + refs[n_in - 1][...] * refs[n_in - 2][...]
        r = lax.rsqrt(jnp.mean(xv * xv, axis=-1, keepdims=True) + EPS)
        n = xv * r
        dx_ref, dg_ref = next(outs), next(outs)

        def acc(ref, val, init):
            @pl.when(init)
            def _():
                ref[...] = val

            @pl.when(jnp.logical_not(init))
            def _():
                ref[...] += val

        if final:
            t_ref = next(it)
            loss_ref = next(outs)
            e = n * gv - t_ref[...]
            acc(loss_ref, jnp.zeros((1, 128), F32) + 0.5 * jnp.sum(e * e) / Dm, first)
            dyg = e * (1.0 / Dm)
        else:
            sc_ref, dh_ref = next(it), next(it)
            dsc_ref, dsh_ref = next(outs), next(outs)
            dhv = dh_ref[...]
            acc(dsh_ref, jnp.sum(dhv, axis=0, keepdims=True), first_row)
            acc(dsc_ref, jnp.sum(dhv * (n * gv), axis=0, keepdims=True), first_row)
            dyg = dhv * (1.0 + sc_ref[...])
        acc(dg_ref, jnp.sum(dyg * n, axis=0, keepdims=True), first)
        dn = dyg * gv
        dx = r * (dn - n * jnp.mean(dn * n, axis=-1, keepdims=True))
        if dres is not None:
            dx = dx + next(it)[...]
        dx_ref[...] = dx
        if has_br:
            br_ref, gt_ref = next(it), next(it)
            dbr_ref, dgt_ref = next(outs), next(outs)
            dbr_ref[...] = (dx * gt_ref[...]).astype(BF16)
            acc(dgt_ref, jnp.sum(dx * br_ref[...], axis=0, keepdims=True), first_row)

    outs = pl.pallas_call(body, name=name, grid=(B, S // ts), in_specs=in_specs, out_specs=out_specs, out_shape=out_shape,
                          compiler_params=pltpu.CompilerParams(dimension_semantics=("arbitrary", "arbitrary")))(*ins)
    res = dict(dx=outs[0], dg=outs[1])
    if final:
        res["loss"] = outs[2]
    else:
        res["dsc"], res["dsh"] = outs[2], outs[3]
    if has_br:
        res["dbr"], res["dgate"] = outs[-2], outs[-1]
    return res


def _gm_heads(vg, lng, lnb):
    res = []
    for h in range(GM_H):
        sl = slice(h * 128, (h + 1) * 128)
        vh = vg[:, sl]
        xc = vh - jnp.mean(vh, axis=-1, keepdims=True)
        rstd = lax.rsqrt(jnp.mean(xc * xc, axis=-1, keepdims=True) + 1e-5)
        xhat = xc * rstd
        res.append((xhat, rstd, xhat * lng[:, sl] + lnb[:, sl]))
    return res


def _gm_gate(heads, wt_ref, bsx, nch):
    cols = []
    for h in range(GM_H):
        vn = heads[h][2].astype(BF16)
        rows = [_dot(wt_ref[h], vn[c * CHUNK:(c + 1) * CHUNK]) + bsx[:, h * 128:(h + 1) * 128] for c in range(nch)]
        cols.append(jnp.concatenate(rows, axis=0) if nch > 1 else rows[0])
    return jnp.concatenate(cols, axis=1)


def _gm_specs(S):
    tb = min(S, 512)
    u = pl.BlockSpec((None, tb, GM_W), lambda b, i: (b, i, OFF["u"] // GM_W))
    v = pl.BlockSpec((None, tb, GM_W), lambda b, i: (b, i, OFF["v"] // GM_W))
    tok = pl.BlockSpec((None, tb, GM_W), lambda b, i: (b, i, 0))
    return tb, u, v, tok


def _gmlp_fwd(P, lng, lnb, wt, bsx, og, name):
    B, S, _ = P.shape
    tb, u_spec, v_spec, tok = _gm_specs(S)
    nch = tb // CHUNK

    def body(u_ref, v_ref, lng_ref, lnb_ref, wt_ref, bsx_ref, og_ref, o_ref):
        heads = _gm_heads(_gelu(v_ref[...]), lng_ref[...], lnb_ref[...])
        y = _gelu(u_ref[...]) * _gm_gate(heads, wt_ref, bsx_ref[...], nch)
        r = lax.rsqrt(jnp.mean(y * y, axis=-1, keepdims=True) + EPS)
        o_ref[...] = (y * r * og_ref[...]).astype(BF16)

    return pl.pallas_call(
        body, name=name, grid=(B, S // tb),
        in_specs=[u_spec, v_spec, _full((1, GM_W)), _full((1, GM_W)), _full((GM_H, 128, 128)), _full((128, GM_W)), _full((1, GM_W))],
        out_specs=tok, out_shape=jax.ShapeDtypeStruct((B, S, GM_W), BF16))(P, P, lng, lnb, wt, bsx, og)


def _gmlp_bwd(P, dcat, lng, lnb, wt, wtT, bsx, og, name):
    B, S, _ = P.shape
    tb, u_spec, v_spec, tok = _gm_specs(S)
    nch = tb // CHUNK
    do_spec = pl.BlockSpec((None, tb, GM_W), lambda b, i: (b, i, 0))

    def body(u_ref, v_ref, do_ref, lng_ref, lnb_ref, wt_ref, wtT_ref, bsx_ref, og_ref,
             du_ref, dv_ref, dlng_ref, dlnb_ref, dws_ref, dbsx_ref, dog_ref):
        first = (pl.program_id(0) == 0) & (pl.program_id(1) == 0)

        @pl.when(first)
        def _():
            for ref in (dlng_ref, dlnb_ref, dws_ref, dbsx_ref, dog_ref):
                ref[...] = jnp.zeros(ref.shape, F32)

        u, v, lng = u_ref[...], v_ref[...], lng_ref[...]
        ug = _gelu(u)
        heads = _gm_heads(_gelu(v), lng, lnb_ref[...])
        gate = _gm_gate(heads, wt_ref, bsx_ref[...], nch)
        y = ug * gate
        r = lax.rsqrt(jnp.mean(y * y, axis=-1, keepdims=True) + EPS)
        yn = y * r
        dout = do_ref[...]
        dog_ref[...] += jnp.sum(dout * yn, axis=0, keepdims=True)
        dyn = dout * og_ref[...]
        dy = r * (dyn - yn * jnp.mean(dyn * yn, axis=-1, keepdims=True))
        du_ref[...] = (dy * gate * _gelu_grad(u)).astype(BF16)
        dgate = dy * ug
        tril = lax.broadcasted_iota(jnp.int32, (128, 128), 0) >= lax.broadcasted_iota(jnp.int32, (128, 128), 1)
        dvg = []
        for h in range(GM_H):
            sl = slice(h * 128, (h + 1) * 128)
            xhat, rstd, vn = heads[h]
            vnb = vn.astype(BF16)
            dgh = dgate[:, sl]
            dgb = dgh.astype(BF16)
            dbs = jnp.zeros((128, 128), F32)
            dw = jnp.zeros((128, 128), F32)
            dvn = []
            for c in range(nch):
                rs = slice(c * CHUNK, (c + 1) * CHUNK)
                dbs = dbs + dgh[rs]
                dw = dw + _dot_nt(dgb[rs], vnb[rs])
                dvn.append(_dot(wtT_ref[h], dgb[rs]))
            dvn = jnp.concatenate(dvn, axis=0) if nch > 1 else dvn[0]
            dbsx_ref[:, sl] += dbs
            dws_ref[h] += jnp.where(tril, dw, 0.0)
            dlng_ref[:, sl] += jnp.sum(dvn * xhat, axis=0, keepdims=True)
            dlnb_ref[:, sl] += jnp.sum(dvn, axis=0, keepdims=True)
            dxh = dvn * lng[:, sl]
            dvg.append(rstd * (dxh - jnp.mean(dxh, axis=-1, keepdims=True) - xhat * jnp.mean(dxh * xhat, axis=-1, keepdims=True)))
        dv_ref[...] = (jnp.concatenate(dvg, axis=1) * _gelu_grad(v)).astype(BF16)

    p512, w3 = _full((1, GM_W)), _full((GM_H, 128, 128))
    return pl.pallas_call(
        body, name=name, grid=(B, S // tb),
        in_specs=[u_spec, v_spec, do_spec, p512, p512, w3, w3, _full((128, GM_W)), p512],
        out_specs=[tok, tok, p512, p512, w3, _full((128, GM_W)), p512],
        out_shape=[jax.ShapeDtypeStruct((B, S, GM_W), BF16)] * 2 + [
            jax.ShapeDtypeStruct((1, GM_W), F32), jax.ShapeDtypeStruct((1, GM_W), F32),
            jax.ShapeDtypeStruct((GM_H, 128, 128), F32), jax.ShapeDtypeStruct((128, GM_W), F32),
            jax.ShapeDtypeStruct((1, GM_W), F32)],
        compiler_params=pltpu.CompilerParams(dimension_semantics=("arbitrary", "arbitrary")),
    )(P, P, dcat, lng, lnb, wt, wtT, bsx, og)


def _lane_half():
    return lax.broadcasted_iota(jnp.int32, (128, 128), 1) // 64


def _att_stack(x, kvh, dtype):
    half = _lane_half()
    rows = []
    for g in range(4):
        i = kvh * 4 + g
        pair = x[:, (i // 2) * 128:(i // 2 + 1) * 128]
        if i % 2 != kvh:
            pair = pltpu.roll(pair, 64, 1)
        rows.append(jnp.where(half == kvh, pair, 0.0))
    return jnp.concatenate(rows, axis=0).astype(dtype)


def _att_unstack(pairs, y, kvh):
    half = _lane_half()
    for g in range(4):
        i = kvh * 4 + g
        piece = y[g * 128:(g + 1) * 128]
        if i % 2 != kvh:
            piece = pltpu.roll(piece, 64, 1)
        pairs[i // 2] = jnp.where(half == i % 2, piece, pairs[i // 2])
    return pairs


def _att_probs(qb, k2, st, sink_ref, kvh):
    qm = _att_stack(qb, kvh, BF16)
    s = _dot_nt(qm, k2) * (64 ** -0.5)
    qi = lax.broadcasted_iota(jnp.int32, (512, 256), 0) % 128
    kj = lax.broadcasted_iota(jnp.int32, (512, 256), 1)
    diff = qi + 128 - kj
    valid = (diff >= 0) & (diff < 128) & (st + kj - 128 >= 0)
    s = jnp.where(valid, s, NEG_INF)
    grp = lax.broadcasted_iota(jnp.int32, (512, 1), 0) // 128
    sink = jnp.zeros((512, 1), F32)
    for g in range(4):
        sink = jnp.where(grp == g, sink_ref[kvh * 4 + g], sink)
    m = jnp.maximum(jnp.max(s, axis=-1, keepdims=True), sink)
    e = jnp.exp(s - m)
    esink = jnp.exp(sink - m)
    inv = 1.0 / (jnp.sum(e, axis=-1, keepdims=True) + esink)
    return qm, e * inv, esink * inv


def _att_specs(S):
    q = pl.BlockSpec((None, S, ATT_W), lambda b: (b, 0, OFF["q"] // ATT_W))
    k = pl.BlockSpec((None, S, KV_W), lambda b: (b, 0, OFF["k"] // KV_W))
    v = pl.BlockSpec((None, S, KV_W), lambda b: (b, 0, OFF["vv"] // KV_W))
    tok = pl.BlockSpec((None, S, ATT_W), lambda b: (b, 0, 0))
    kv = pl.BlockSpec((None, S, KV_W), lambda b: (b, 0, 0))
    return q, k, v, tok, kv


_SMEM = pl.BlockSpec(memory_space=pltpu.SMEM)


def _attn_fwd(P, sinks, og, name):
    B, S, _ = P.shape
    q_spec, k_spec, v_spec, tok, _ = _att_specs(S)

    def body(q_ref, k_ref, v_ref, sink_ref, og_ref, o_ref, kpad, vpad):
        kpad[0:128, :] = jnp.zeros((128, KV_W), BF16)
        vpad[0:128, :] = jnp.zeros((128, KV_W), BF16)
        kpad[128:, :] = k_ref[...].astype(BF16)
        vpad[128:, :] = v_ref[...].astype(BF16)

        def step(n, carry):
            st = pl.multiple_of(n * 128, 128)
            qb = q_ref[pl.ds(st, 128), :]
            k2, v2 = kpad[pl.ds(st, 256), :], vpad[pl.ds(st, 256), :]
            pairs = [jnp.zeros((128, 128), F32)] * 4
            for kvh in range(2):
                _, p, _ = _att_probs(qb, k2, st, sink_ref, kvh)
                pairs = _att_unstack(pairs, _dot(p.astype(BF16), v2), kvh)
            o = jnp.concatenate(pairs, axis=1)
            r = lax.rsqrt(jnp.mean(o * o, axis=-1, keepdims=True) + EPS)
            o_ref[pl.ds(st, 128), :] = (o * r * og_ref[...]).astype(BF16)
            return carry

        lax.fori_loop(0, S // 128, step, 0)

    return pl.pallas_call(
        body, name=name, grid=(B,), in_specs=[q_spec, k_spec, v_spec, _SMEM, _full((1, ATT_W))], out_specs=tok,
        out_shape=jax.ShapeDtypeStruct((B, S, ATT_W), BF16),
        scratch_shapes=[pltpu.VMEM((S + 128, KV_W), BF16)] * 2)(P, P, P, sinks, og)


def _attn_bwd(P, dcat, sinks, og, name):
    B, S, _ = P.shape
    q_spec, k_spec, v_spec, tok, kv = _att_specs(S)
    do_spec = pl.BlockSpec((None, S, ATT_W), lambda b: (b, 0, GM_W // ATT_W))

    def body(q_ref, k_ref, v_ref, do_ref, sink_ref, og_ref, dq_ref, dk_ref, dv_ref, dsink_ref, dog_ref,
             kpad, vpad, dkpad, dvpad):
        @pl.when(pl.program_id(0) == 0)
        def _():
            dsink_ref[...] = jnp.zeros((8, 128), F32)
            dog_ref[...] = jnp.zeros((1, ATT_W), F32)

        kpad[0:128, :] = jnp.zeros((128, KV_W), BF16)
        vpad[0:128, :] = jnp.zeros((128, KV_W), BF16)
        kpad[128:, :] = k_ref[...].astype(BF16)
        vpad[128:, :] = v_ref[...].astype(BF16)
        dkpad[...] = jnp.zeros((S + 128, KV_W), F32)
        dvpad[...] = jnp.zeros((S + 128, KV_W), F32)
        half = _lane_half()
        head_row = lax.broadcasted_iota(jnp.int32, (8, 128), 0)

        def step(n, carry):
            st = pl.multiple_of(n * 128, 128)
            qb = q_ref[pl.ds(st, 128), :]
            k2, v2 = kpad[pl.ds(st, 256), :], vpad[pl.ds(st, 256), :]
            saved, pairs = [], [jnp.zeros((128, 128), F32)] * 4
            for kvh in range(2):
                qm, p, psink = _att_probs(qb, k2, st, sink_ref, kvh)
                o = _dot(p.astype(BF16), v2)
                saved.append((qm, p, psink, o))
                pairs = _att_unstack(pairs, o, kvh)
            o = jnp.concatenate(pairs, axis=1)
            r = lax.rsqrt(jnp.mean(o * o, axis=-1, keepdims=True) + EPS)
            on = o * r
            dout = do_ref[pl.ds(st, 128), :]
            dog_ref[...] += jnp.sum(dout * on, axis=0, keepdims=True)
            dyn = dout * og_ref[...]
            do = r * (dyn - on * jnp.mean(dyn * on, axis=-1, keepdims=True))
            dq_pairs = [jnp.zeros((128, 128), F32)] * 4
            dsink = jnp.zeros((8, 128), F32)
            for kvh in range(2):
                qm, p, psink, og_ = saved[kvh]
                dog = _att_stack(do, kvh, F32)
                delta = jnp.sum(dog * jnp.where(jnp.concatenate([half] * 4, axis=0) == kvh, og_, 0.0), axis=-1, keepdims=True)
                dogb, pb = dog.astype(BF16), p.astype(BF16)
                dvpad[pl.ds(st, 256), :] += _dot_tn(pb, dogb)
                dp = _dot_nt(dogb, v2)
                ds = (p * (dp - delta) * (64 ** -0.5)).astype(BF16)
                sd = psink * delta
                for g in range(4):
                    dsink = dsink - jnp.where(head_row == kvh * 4 + g, jnp.sum(sd[g * 128:(g + 1) * 128]), 0.0)
                dq_pairs = _att_unstack(dq_pairs, _dot(ds, k2), kvh)
                dkpad[pl.ds(st, 256), :] += _dot_tn(ds, qm)
            dsink_ref[...] += dsink
            dq_ref[pl.ds(st, 128), :] = jnp.concatenate(dq_pairs, axis=1).astype(BF16)
            return carry

        lax.fori_loop(0, S // 128, step, 0)
        dk_ref[...] = dkpad[128:, :].astype(BF16)
        dv_ref[...] = dvpad[128:, :].astype(BF16)

    return pl.pallas_call(
        body, name=name, grid=(B,),
        in_specs=[q_spec, k_spec, v_spec, do_spec, _SMEM, _full((1, ATT_W))],
        out_specs=[tok, kv, kv, _full((8, 128)), _full((1, ATT_W))],
        out_shape=[jax.ShapeDtypeStruct((B, S, ATT_W), BF16), jax.ShapeDtypeStruct((B, S, KV_W), BF16),
                   jax.ShapeDtypeStruct((B, S, KV_W), BF16), jax.ShapeDtypeStruct((8, 128), F32),
                   jax.ShapeDtypeStruct((1, ATT_W), F32)],
        scratch_shapes=[pltpu.VMEM((S + 128, KV_W), BF16)] * 2 + [pltpu.VMEM((S + 128, KV_W), F32)] * 2,
        compiler_params=pltpu.CompilerParams(dimension_semantics=("arbitrary",)),
    )(P, P, P, dcat, sinks, og)


CONV_TC = 256


def _conv_pre(ext, w_ref, b_ref, S):
    acc = b_ref[...] + w_ref[3:4, :] * ext[pl.ds(8, S), :]
    for k in range(1, 4):
        acc = acc + w_ref[3 - k:4 - k, :] * ext[pl.ds(8 - k, S), :]
    return acc


def _conv_fwd(P, w8, b, name):
    B, S, _ = P.shape
    nj = CONV_CH // CONV_TC
    x_spec = pl.BlockSpec((None, S, CONV_TC), lambda b_, j: (b_, 0, OFF["xbc"] // CONV_TC + j))
    tok = pl.BlockSpec((None, S, CONV_TC), lambda b_, j: (b_, 0, j))

    def body(x_ref, w_ref, b_ref, o_ref, ext):
        ext[0:8, :] = jnp.zeros((8, CONV_TC), F32)
        ext[8:, :] = x_ref[...]
        pre = _conv_pre(ext, w_ref, b_ref, S)
        o_ref[...] = pre * _sigmoid(pre)

    return pl.pallas_call(
        body, name=name, grid=(B, nj),
        in_specs=[x_spec, pl.BlockSpec((8, CONV_TC), lambda b_, j: (0, j)), pl.BlockSpec((1, CONV_TC), lambda b_, j: (0, j))],
        out_specs=tok, out_shape=jax.ShapeDtypeStruct((B, S, CONV_CH), F32),
        scratch_shapes=[pltpu.VMEM((S + 8, CONV_TC), F32)])(P, w8, b)


def _conv_bwd(P, dact, w8, b, name):
    B, S, _ = P.shape
    nj = CONV_CH // CONV_TC
    x_spec = pl.BlockSpec((None, S, CONV_TC), lambda j, b_: (b_, 0, OFF["xbc"] // CONV_TC + j))
    tok = pl.BlockSpec((None, S, CONV_TC), lambda j, b_: (b_, 0, j))
    w_spec = pl.BlockSpec((8, CONV_TC), lambda j, b_: (0, j))
    b_spec = pl.BlockSpec((1, CONV_TC), lambda j, b_: (0, j))

    def body(x_ref, d_ref, w_ref, b_ref, dx_ref, dw_ref, db_ref, ext, extd):
        @pl.when(pl.program_id(1) == 0)
        def _():
            dw_ref[...] = jnp.zeros((8, CONV_TC), F32)
            db_ref[...] = jnp.zeros((1, CONV_TC), F32)

        ext[0:8, :] = jnp.zeros((8, CONV_TC), F32)
        ext[8:, :] = x_ref[...]
        pre = _conv_pre(ext, w_ref, b_ref, S)
        sg = _sigmoid(pre)
        dpre = d_ref[...] * (sg * (1.0 + pre * (1.0 - sg)))
        extd[0:8, :] = jnp.zeros((8, CONV_TC), F32)
        extd[pl.ds(8, S), :] = dpre
        extd[pl.ds(8 + S, 8), :] = jnp.zeros((8, CONV_TC), F32)
        dx = w_ref[3:4, :] * dpre
        for k in range(1, 4):
            dx = dx + w_ref[3 - k:4 - k, :] * extd[pl.ds(8 + k, S), :]
        dx_ref[...] = dx.astype(BF16)
        db_ref[...] += jnp.sum(dpre, axis=0, keepdims=True)
        sub = lax.broadcasted_iota(jnp.int32, (8, CONV_TC), 0)
        dw = jnp.zeros((8, CONV_TC), F32)
        for i in range(4):
            dw = dw + jnp.where(sub == i, jnp.sum(dpre * ext[pl.ds(5 + i, S), :], axis=0, keepdims=True), 0.0)
        dw_ref[...] += dw

    return pl.pallas_call(
        body, name=name, grid=(nj, B), in_specs=[x_spec, tok, w_spec, b_spec], out_specs=[tok, w_spec, b_spec],
        out_shape=[jax.ShapeDtypeStruct((B, S, CONV_CH), BF16), jax.ShapeDtypeStruct((8, CONV_CH), F32),
                   jax.ShapeDtypeStruct((1, CONV_CH), F32)],
        scratch_shapes=[pltpu.VMEM((S + 8, CONV_TC), F32), pltpu.VMEM((S + 16, CONV_TC), F32)],
        compiler_params=pltpu.CompilerParams(dimension_semantics=("arbitrary", "arbitrary")),
    )(P, dact, w8, b)


def _ssd_consts():
    hd = np.arange(SSM_W) // SSM_HD
    E = (np.arange(128)[:, None] == hd[None, :]).astype(np.float32)
    tri = (np.arange(128)[:, None] >= np.arange(128)[None, :]).astype(np.float32)
    return jnp.asarray(E), jnp.asarray(E.T), jnp.asarray(tri), jnp.asarray(tri.T)


def _ssd_pre(xa, dtraw, bias, alog, E, ET, tri):
    lane = lax.broadcasted_iota(jnp.int32, (128, 128), 1)
    pre = dtraw + bias
    dtp = jnp.where(lane < SSM_H, jnp.maximum(pre, 0.0) + jnp.log(1.0 + jnp.exp(-jnp.abs(pre))), 0.0)
    a = -jnp.exp(alog)
    acs = _dot(tri, dtp * a, HI)
    acsT = acs.T
    dtE, acsE = _dot(dtp, E, HI), _dot(acs, E, HI)
    cdcol = jnp.exp(_dot(ET, acsT, HI)[:, 127:128])
    X = xa[:, :SSM_W]
    xdt = X * dtE
    wE = jnp.exp(acsE[127:128, :] - acsE)
    eE = jnp.exp(acsE)
    return dict(pre=pre, dtp=dtp, a=a, acs=acs, acsT=acsT, dtE=dtE, acsE=acsE, cdcol=cdcol, X=X, xdt=xdt, wE=wE, eE=eE)


def _ssd_decay(c, h):
    lm = lax.broadcasted_iota(jnp.int32, (128, 128), 0) >= lax.broadcasted_iota(jnp.int32, (128, 128), 1)
    return jnp.exp(jnp.where(lm, c["acs"][:, h:h + 1] - c["acsT"][h:h + 1, :], NEG_INF))


def _ssd_pair_operands(c, CB, h0):
    lane = lax.broadcasted_iota(jnp.int32, (128, 128), 1)
    L0, L1 = _ssd_decay(c, h0), _ssd_decay(c, h0 + 1)
    M = jnp.concatenate([CB * L0, CB * L1], axis=1).astype(BF16)
    xp = c["xdt"][:, h0 * 64:h0 * 64 + 128]
    BD = jnp.concatenate([jnp.where(lane < 64, xp, 0.0), jnp.where(lane >= 64, xp, 0.0)], axis=0).astype(BF16)
    return L0, L1, M, BD


def _ssd_y(c, xa, state_ref, dskipE):
    per_group, ys = [], []
    for g in range(SSM_G):
        gs = slice(g * 512, (g + 1) * 512)
        Bb = xa[:, SSM_W + g * 128:SSM_W + (g + 1) * 128].astype(BF16)
        Cb = xa[:, SSM_W + 256 + g * 128:SSM_W + 256 + (g + 1) * 128].astype(BF16)
        CB = _dot_nt(Cb, Bb)
        Sg = state_ref[gs, :]
        yoff = _dot_nt(Cb, Sg.astype(BF16)) * c["eE"][:, gs]
        ydiag, pairs = [], []
        for j in range(4):
            ops = _ssd_pair_operands(c, CB, g * 8 + 2 * j)
            pairs.append(ops)
            ydiag.append(_dot(ops[2], ops[3]))
        ys.append(jnp.concatenate(ydiag, axis=1) + yoff)
        per_group.append(dict(Bb=Bb, Cb=Cb, CB=CB, Sg=Sg, yoff=yoff, pairs=pairs))
    Y = jnp.concatenate(ys, axis=1) + c["X"] * dskipE
    return Y, per_group


def _ssd_specs(S, rev):
    nc = S // CHUNK
    cm = (lambda b, i: (b, nc - 1 - i)) if rev else (lambda b, i: (b, i))
    xa = pl.BlockSpec((None, CHUNK, CONV_CH), lambda b, i: cm(b, i) + (0,))
    z = pl.BlockSpec((None, CHUNK, SSM_W), lambda b, i: cm(b, i) + (OFF["z"] // SSM_W,))
    dt = pl.BlockSpec((None, CHUNK, 128), lambda b, i: cm(b, i) + (OFF["dt"] // 128,))
    tok = pl.BlockSpec((None, CHUNK, SSM_W), lambda b, i: cm(b, i) + (0,))
    st = pl.BlockSpec((None, None, SSM_W, 128), lambda b, i: cm(b, i) + (0, 0))
    return nc, xa, z, dt, tok, st


def _ssd_fwd(xact, P, bias, alog, dskipE, ng, name):
    B, S, _ = P.shape
    nc, xa_spec, z_spec, dt_spec, tok, st_spec = _ssd_specs(S, False)
    E, ET, tri, _ = _ssd_consts()

    def body(xa_ref, z_ref, dt_ref, bias_ref, alog_ref, dsk_ref, ng_ref, E_ref, ET_ref, tri_ref, o_ref, sp_ref, state):
        @pl.when(pl.program_id(1) == 0)
        def _():
            state[...] = jnp.zeros((SSM_W, 128), F32)

        sp_ref[...] = state[...]
        xa = xa_ref[...]
        c = _ssd_pre(xa, dt_ref[...], bias_ref[...], alog_ref[...], E_ref[...], ET_ref[...], tri_ref[...])
        Y, groups = _ssd_y(c, xa, state, dsk_ref[...])
        Z = (c["xdt"] * c["wE"]).astype(BF16)
        for g in range(SSM_G):
            gs = slice(g * 512, (g + 1) * 512)
            state[gs, :] = groups[g]["Sg"] * c["cdcol"][gs, :] + _dot_tn(Z[:, gs], groups[g]["Bb"])
        zv = z_ref[...]
        yz = Y * (zv * _sigmoid(zv))
        outs = []
        for g in range(SSM_G):
            yg = yz[:, g * 512:(g + 1) * 512]
            outs.append(yg * lax.rsqrt(jnp.mean(yg * yg, axis=-1, keepdims=True) + EPS))
        o_ref[...] = (jnp.concatenate(outs, axis=1) * ng_ref[...]).astype(BF16)

    return pl.pallas_call(
        body, name=name, grid=(B, nc),
        in_specs=[xa_spec, z_spec, dt_spec, _full((1, 128)), _full((1, 128)), _full((1, SSM_W)), _full((1, SSM_W)),
                  _full((128, SSM_W)), _full((SSM_W, 128)), _full((128, 128))],
        out_specs=[tok, st_spec],
        out_shape=[jax.ShapeDtypeStruct((B, S, SSM_W), BF16), jax.ShapeDtypeStruct((B, nc, SSM_W, 128), F32)],
        scratch_shapes=[pltpu.VMEM((SSM_W, 128), F32)],
        compiler_params=pltpu.CompilerParams(dimension_semantics=("arbitrary", "arbitrary")),
    )(xact, P, P, bias, alog, dskipE, ng, E, ET, tri)


def _ssd_bwd(xact, P, sprev, dcat, bias, alog, dskipE, ng, name):
    B, S, _ = P.shape
    nc, xa_spec, z_spec, dt_spec, tok, st_spec = _ssd_specs(S, True)
    do_spec = pl.BlockSpec((None, CHUNK, SSM_W), lambda b, i: (b, nc - 1 - i, 1))
    E, ET, tri, triT = _ssd_consts()
    dt_out = pl.BlockSpec((None, CHUNK, 128), lambda b, i: (b, nc - 1 - i, 0))

    def body(xa_ref, z_ref, dt_ref, sp_ref, do_ref, bias_ref, alog_ref, dsk_ref, ng_ref, E_ref, ET_ref, tri_ref, triT_ref,
             dxa_ref, dz_ref, ddt_ref, dbias_ref, dalog_ref, ddsk_ref, dng_ref, dstate):
        first = (pl.program_id(0) == 0) & (pl.program_id(1) == 0)

        @pl.when(first)
        def _():
            for ref in (dbias_ref, dalog_ref, ddsk_ref, dng_ref):
                ref[...] = jnp.zeros(ref.shape, F32)

        @pl.when(pl.program_id(1) == 0)
        def _():
            dstate[...] = jnp.zeros((SSM_W, 128), F32)

        xa, ETm = xa_ref[...], ET_ref[...]
        c = _ssd_pre(xa, dt_ref[...], bias_ref[...], alog_ref[...], E_ref[...], ETm, tri_ref[...])
        Y, groups = _ssd_y(c, xa, sp_ref, dsk_ref[...])
        X, xdt = c["X"], c["xdt"]
        zv = z_ref[...]
        sg = _sigmoid(zv)
        zs = zv * sg
        yz = Y * zs
        dout = do_ref[...]
        dyz = []
        for g in range(SSM_G):
            gs = slice(g * 512, (g + 1) * 512)
            yg = yz[:, gs]
            r = lax.rsqrt(jnp.mean(yg * yg, axis=-1, keepdims=True) + EPS)
            yn = yg * r
            dng_ref[:, gs] += jnp.sum(dout[:, gs] * yn, axis=0, keepdims=True)
            dyn = dout[:, gs] * ng_ref[:, gs]
            dyz.append(r * (dyn - yn * jnp.mean(dyn * yn, axis=-1, keepdims=True)))
        dyz = jnp.concatenate(dyz, axis=1)
        dz_ref[...] = (dyz * Y * (sg * (1.0 + zv * (1.0 - sg)))).astype(BF16)
        dY = dyz * zs
        ddsk_ref[...] += jnp.sum(dY * X, axis=0, keepdims=True)
        dX = dY * dsk_ref[...]
        lane = lax.broadcasted_iota(jnp.int32, (128, 128), 1)
        sub = lax.broadcasted_iota(jnp.int32, (128, 128), 0)
        colform = jnp.zeros((128, 128), F32)
        rowform = jnp.zeros((128, 128), F32)
        dxdt, gacsE, dBC = [], [], []
        for g in range(SSM_G):
            gs = slice(g * 512, (g + 1) * 512)
            G = groups[g]
            Bb, Cb, CB, Sg = G["Bb"], G["Cb"], G["CB"], G["Sg"]
            dYg = dY[:, gs]
            dQ = (dYg * c["eE"][:, gs]).astype(BF16)
            dSn = dstate[gs, :]
            dSnb = dSn.astype(BF16)
            cd = c["cdcol"][gs, :]
            dC = _dot(dQ, Sg.astype(BF16))
            dSprev = _dot_tn(dQ, Cb) + dSn * cd
            hcol = jnp.sum(_dot(E_ref[:, gs], dSn * Sg * cd, HI), axis=-1, keepdims=True)
            rowform = rowform + jnp.where(lane == 127, hcol, 0.0)
            Zg = xdt[:, gs] * c["wE"][:, gs]
            dZ = _dot_nt(Bb, dSnb)
            dB = _dot(Zg.astype(BF16), dSnb)
            U = dZ * Zg
            ga = dYg * G["yoff"] - U
            ga = ga + jnp.where(lax.broadcasted_iota(jnp.int32, (128, 512), 0) == 127, jnp.sum(U, axis=0, keepdims=True), 0.0)
            gacsE.append(ga)
            dxg = [None] * 4
            dCB = jnp.zeros((128, 128), F32)
            for j in range(4):
                h0 = g * 8 + 2 * j
                L0, L1, M, BD = G["pairs"][j]
                dYp = dYg[:, j * 128:(j + 1) * 128].astype(BF16)
                dM = _dot_nt(dYp, BD)
                dBD = _dot_tn(M, dYp)
                dxg[j] = jnp.where(lane < 64, dBD[:128], dBD[128:])
                for t, (h, L) in enumerate(((h0, L0), (h0 + 1, L1))):
                    dMh = dM[:, t * 128:(t + 1) * 128]
                    dCB = dCB + dMh * L
                    Gh = dMh * CB * L
                    colform = colform + jnp.where(lane == h, jnp.sum(Gh, axis=1, keepdims=True), 0.0)
                    rowform = rowform - jnp.where(sub == h, jnp.sum(Gh, axis=0, keepdims=True), 0.0)
            dCBb = dCB.astype(BF16)
            dC = dC + _dot(dCBb, Bb)
            dB = dB + _dot_tn(dCBb, Cb)
            dxdt.append(jnp.concatenate(dxg, axis=1) + dZ * c["wE"][:, gs])
            dBC.append((dB, dC))
            dstate[gs, :] = dSprev
        dxdt = jnp.concatenate(dxdt, axis=1)
        dX = dX + dxdt * c["dtE"]
        ddt = _dot(dxdt * X, ETm, HI)
        dacs = colform + rowform.T + _dot(jnp.concatenate(gacsE, axis=1), ETm, HI)
        dda = _dot(triT_ref[...], dacs, HI)
        ddt = ddt + dda * c["a"]
        dalog_ref[...] += jnp.sum(dda * c["dtp"], axis=0, keepdims=True) * c["a"]
        ddtraw = jnp.where(lane < SSM_H, ddt * _sigmoid(c["pre"]), 0.0)
        dbias_ref[...] += jnp.sum(ddtraw, axis=0, keepdims=True)
        ddt_ref[...] = ddtraw.astype(BF16)
        dxa_ref[...] = jnp.concatenate([dX, dBC[0][0], dBC[1][0], dBC[0][1], dBC[1][1]], axis=1)

    p128, p1k = _full((1, 128)), _full((1, SSM_W))
    return pl.pallas_call(
        body, name=name, grid=(B, nc),
        in_specs=[xa_spec, z_spec, dt_spec, st_spec, do_spec, p128, p128, p1k, p1k,
                  _full((128, SSM_W)), _full((SSM_W, 128)), _full((128, 128)), _full((128, 128))],
        out_specs=[xa_spec, tok, dt_out, p128, p128, p1k, p1k],
        out_shape=[jax.ShapeDtypeStruct((B, S, CONV_CH), F32), jax.ShapeDtypeStruct((B, S, SSM_W), BF16),
                   jax.ShapeDtypeStruct((B, S, 128), BF16), jax.ShapeDtypeStruct((1, 128), F32),
                   jax.ShapeDtypeStruct((1, 128), F32), jax.ShapeDtypeStruct((1, SSM_W), F32),
                   jax.ShapeDtypeStruct((1, SSM_W), F32)],
        scratch_shapes=[pltpu.VMEM((SSM_W, 128), F32)],
        compiler_params=pltpu.CompilerParams(dimension_semantics=("arbitrary", "arbitrary")),
    )(xact, P, P, sprev, dcat, bias, alog, dskipE, ng, E, ET, tri, triT)


def _adamw(w, parts, m, v, name, tr=512, row0=0, prev=None):
    Rtot, C = w.shape
    ns, R = parts.shape[0], parts.shape[1]
    tr = min(tr, R)
    assert R % tr == 0 and row0 % tr == 0
    off = row0 // tr
    c1 = 1.0 / (1.0 - ADAM_B1 ** ADAM_STEP)
    c2 = 1.0 / (1.0 - ADAM_B2 ** ADAM_STEP)

    def body(w_ref, p_ref, m_ref, v_ref, *rest):
        g_ref, d_ref, mo_ref, vo_ref = rest[-4:]
        g = p_ref[0].astype(F32)
        for s in range(1, ns):
            g = g + p_ref[s].astype(F32)
        mn = ADAM_B1 * m_ref[...] + (1.0 - ADAM_B1) * g
        vn = ADAM_B2 * v_ref[...] + (1.0 - ADAM_B2) * (g * g)
        g_ref[...] = g
        mo_ref[...] = mn
        vo_ref[...] = vn
        d_ref[...] = -ADAM_LR * ((mn * c1) / (jnp.sqrt(vn * c2) + ADAM_EPS) + ADAM_WD * w_ref[...])

    blk = pl.BlockSpec((tr, C), lambda i: (i + off, 0))
    extra = [] if prev is None else list(prev)
    return pl.pallas_call(
        body, name=name, grid=(R // tr,),
        in_specs=[blk, pl.BlockSpec((ns, tr, C), lambda i: (0, i, 0)), blk, blk] + [pl.BlockSpec(memory_space=pl.ANY)] * len(extra),
        out_specs=[blk] * 4, out_shape=[jax.ShapeDtypeStruct((Rtot, C), F32)] * 4,
        input_output_aliases={4 + k: k for k in range(len(extra))})(w, parts, m, v, *extra)


_SMALL = ("ada_b", "norm1_g", "gm_ln_g", "gm_ln_b", "gm_ws", "gm_bs", "gm_norm_g", "attn_sinks", "attn_norm_g", "conv_b",
          "dt_bias", "a_log", "d_skip", "ssm_norm_g", "norm2_g", "final_norm_g")


def _pack(arrs):
    flat = []
    for a in arrs:
        f = a.reshape(-1).astype(F32)
        flat.append(jnp.pad(f, (0, (-f.shape[0]) % 1024)))
    return jnp.concatenate(flat).reshape(-1, 128)


def _unpack(pack, like):
    out, o = [], 0
    flat = pack.reshape(-1)
    for a in like:
        n = int(np.prod(a.shape))
        out.append(flat[o:o + n].reshape(a.shape))
        o += n + (-n) % 1024
    return out


def kernel(x, c, ada_w, ada_b, norm1_g, w_in, gm_ln_g, gm_ln_b, gm_ws, gm_bs, gm_norm_g, attn_sinks, attn_norm_g, conv_w, conv_b, dt_bias, a_log, d_skip, ssm_norm_g, w_out, norm2_g, w_mlp1, w_mlp2, final_norm_g, loss_target, m_ada_w, m_ada_b, m_norm1_g, m_w_in, m_gm_ln_g, m_gm_ln_b, m_gm_ws, m_gm_bs, m_gm_norm_g, m_attn_sinks, m_attn_norm_g, m_conv_w, m_conv_b, m_dt_bias, m_a_log, m_d_skip, m_ssm_norm_g, m_w_out, m_norm2_g, m_w_mlp1, m_w_mlp2, m_final_norm_g, v_ada_w, v_ada_b, v_norm1_g, v_w_in, v_gm_ln_g, v_gm_ln_b, v_gm_ws, v_gm_bs, v_gm_norm_g, v_attn_sinks, v_attn_norm_g, v_conv_w, v_conv_b, v_dt_bias, v_a_log, v_d_skip, v_ssm_norm_g, v_w_out, v_norm2_g, v_w_mlp1, v_w_mlp2, v_final_norm_g):
    args = dict(locals())
    B, S, _ = x.shape
    T = B * S
    L = DEPTH
    me = 4 * lax.axis_index("x") + 2 * lax.axis_index("y") + lax.axis_index("c")

    gath = _gather2([c, conv_w], "ag_c")
    big = ("w_in", "w_out", "w_mlp1", "w_mlp2")
    chain = [(n, l) for l in range(L) for n in ("w_in", "w_mlp1", "w_out", "w_mlp2")]
    inflight = {}

    def start_next(order):
        if not chain:
            return jnp.zeros((8, 128), F32)
        n, l = chain.pop(0)
        sems, v_thru, land_thru, token = _gather_start(shard[n, l], zone[n, l], order, f"ag_start_{n}{l}")
        inflight[n, l] = (sems, v_thru, land_thru)
        return token

    def gathered(n, l, after):
        _, land = _gather_wait(*inflight.pop((n, l)), after, f"ag_wait_{n}{l}")
        return _gather_finish(land, f"ag_fin_{n}{l}")

    me1 = me.astype(jnp.int32).reshape(1)
    shard = {(n, l): args[n][l].astype(BF16) for n, l in chain}
    zone = {k: _landing_zone(v, me1, f"ag_zone_{k[0]}{k[1]}") for k, v in shard.items()}

    c_all = gath[0].reshape(NDEV * B, D)
    c_act = (c_all * jax.nn.sigmoid(c_all)).astype(BF16)
    nb_rows = c_act.shape[0]
    c_pad = jnp.pad(c_act, ((0, 128 - nb_rows), (0, 0)))
    adw = ada_w.astype(BF16)
    mod_part = jnp.stack([_mm(c_pad, adw[l], mode="nn", name=f"mod{l}", tn=768)[:nb_rows] for l in range(L)])
    mod_all = _exchange([mod_part], "ag_mod", False)[0]
    mod_mine = lax.dynamic_slice_in_dim(mod_all, me * B, B, axis=2)
    mod = jnp.transpose(mod_mine, (1, 2, 0, 3)).reshape(L, B, 6 * D) + ada_b[:, None, :]
    mods = [[mod[l][:, None, i * D:(i + 1) * D] for i in range(6)] for l in range(L)]

    win_g, wout_g, w1_g, w2_g = [None] * L, [None] * L, [None] * L, [None] * L

    tril = jnp.tril(jnp.ones((128, 128), F32))
    row = lambda a: a.reshape(1, -1)
    pad128 = lambda a: jnp.pad(a.reshape(1, -1), ((0, 0), (0, 128 - a.shape[-1])))
    small = []
    for l in range(L):
        wt = gm_ws[l] * tril
        small.append(dict(
            lng=row(gm_ln_g[l]), lnb=row(gm_ln_b[l]), wt=wt.astype(BF16), wtT=jnp.swapaxes(wt, 1, 2).astype(BF16),
            bsx=jnp.repeat(gm_bs[l].T, 128, axis=1), gog=row(gm_norm_g[l]), sinks=attn_sinks[l], aog=row(attn_norm_g[l]),
            bias=pad128(dt_bias[l]), alog=pad128(a_log[l]), dskE=jnp.repeat(d_skip[l], SSM_HD).reshape(1, SSM_W),
            sng=row(ssm_norm_g[l]), cb=row(conv_b[l])))
    convw_all = jnp.transpose(gath[1], (1, 2, 0, 3)).reshape(L, 4, CONV_CH)
    convw8 = jnp.pad(convw_all, ((0, 0), (0, 4), (0, 0)))

    saved = []
    xl = x
    tok = start_next(mod)
    h = _norm_fwd(xl, row(norm1_g[0]) + tok[0, 0], mods[0][1], mods[0][0], "norm1_f0")
    for l in range(L):
        sm = small[l]
        g_in = gathered("w_in", l, h)
        tok = start_next(g_in)
        win_g[l] = _to_work_cols(jnp.transpose(g_in, (1, 0, 2)).reshape(D, IN_W))
        P = _mm(h.reshape(T, D), win_g[l], mode="nn", name=f"proj_in{l}", tn=1536, order=tok).reshape(B, S, PW)
        out_a = _gmlp_fwd(P, sm["lng"], sm["lnb"], sm["wt"], sm["bsx"], sm["gog"], f"gmlp_f{l}")
        out_b = _attn_fwd(P, sm["sinks"], sm["aog"], f"attn_f{l}")
        xact = _conv_fwd(P, convw8[l], sm["cb"], f"conv_f{l}")
        w1_g[l] = gathered("w_mlp1", l, xact)
        tok = start_next(w1_g[l])
        out_c, sprev = _ssd_fwd(xact, P, sm["bias"], sm["alog"], sm["dskE"], sm["sng"] + tok[0:1, 0:1], f"ssd_f{l}")
        cat = jnp.concatenate([out_a, out_b, out_c], axis=-1)
        g_out = gathered("w_out", l, cat)
        tok = start_next(g_out)
        wout_g[l] = g_out.reshape(D, D)
        mix = _mm(cat.reshape(T, D), wout_g[l], mode="nn", name=f"proj_out{l}", order=tok).reshape(B, S, D)
        x_mid, h2 = _norm_fwd(xl, row(norm2_g[l]), mods[l][4], mods[l][3], f"norm2_f{l}", resid=(mix, mods[l][2]))
        a_act, r_act = _mm(h2.reshape(T, D), w1_g[l], mode="nn", name=f"mlp1_{l}", out_dtypes=(BF16, BF16), col_blocked_b=True,
                           epilogue=lambda acc: (acc, jnp.square(jnp.maximum(acc, 0.0))))
        g_2 = gathered("w_mlp2", l, r_act)
        tok = start_next(g_2)
        w2_g[l] = g_2.reshape(DFF, D)
        m2 = _mm(r_act, w2_g[l], mode="nn", name=f"mlp2_{l}", order=tok, tk=4096).reshape(B, S, D)
        saved.append(dict(x_in=xl, h=h, P=P, xact=xact, sprev=sprev, cat=cat, mix=mix, x_mid=x_mid, h2=h2, a=a_act, r=r_act, m2=m2))
        if l + 1 < L:
            xl, h = _norm_fwd(x_mid, row(norm1_g[l + 1]), mods[l + 1][1], mods[l + 1][0], f"norm1_f{l + 1}", resid=(m2, mods[l][5]))

    sv = saved[L - 1]
    nb = _norm_bwd(sv["x_mid"], row(final_norm_g), "final_b", tgt=loss_target, br=sv["m2"], gate=mods[L - 1][5], x_is_prev=True)
    loss_part, g_final = nb["loss"], nb["dg"]
    dmod, gsm, gconvw = [None] * L, [None] * L, [None] * L
    core = lax.axis_index("c").astype(jnp.int32).reshape(1)
    reducing = []

    def reduce_start(n, l, p, order):
        from_sib = _pair_exchange([p], f"rs_pair_{n}{l}")[0]
        s, land = _pair_add(p, from_sib, core, f"rs_add_{n}{l}")
        sems, s_thru, land_thru, token = _chipsum_start(s, land, order, f"rs_start_{n}{l}")
        reducing.append((n, l, sems, s_thru, land_thru))
        return token

    for l in reversed(range(L)):
        sv, sm = saved[l], small[l]
        dm2, dxo, dg2 = nb["dbr"].reshape(T, D), nb["dx"], nb["dgate"]
        da = _mm(dm2, w2_g[l], mode="nt", name=f"mlp2_dx{l}", out_dtypes=(BF16,), extras=(sv["a"],),
                 epilogue=lambda acc, a: (acc * (2.0 * jnp.maximum(a.astype(F32), 0.0)),))
        dw2 = _mm(sv["r"], dm2, mode="tn", name=f"mlp2_dw{l}", out_dtypes=(BF16,), tk=2048).reshape(4, 2, DFF // NDEV, D)
        tok = reduce_start("w_mlp2", l, dw2, da)
        dh2 = _mm(da, w1_g[l], mode="nt", name=f"mlp1_dx{l}", col_blocked_b=True, order=tok).reshape(B, S, D)
        dw1 = _mm(sv["h2"].reshape(T, D), da, mode="tn", name=f"mlp1_dw{l}", out_dtypes=(BF16,), tk=2048,
                  col_blocked_out=True).reshape(4, 2, D, DFF // NDEV)
        tok = reduce_start("w_mlp1", l, dw1, dh2)
        nb2 = _norm_bwd(sv["x_mid"], row(norm2_g[l]) + tok[0, 0], f"norm2_b{l}", sc=mods[l][4], dh=dh2, dres=dxo, br=sv["mix"],
                        gate=mods[l][2])
        dmix = nb2["dbr"].reshape(T, D)
        dcat = _mm(dmix, wout_g[l], mode="nt", name=f"proj_out_dx{l}").reshape(B, S, D)
        du, dv, dlng, dlnb, dws, dbsx, dgog = _gmlp_bwd(sv["P"], dcat, sm["lng"], sm["lnb"], sm["wt"], sm["wtT"], sm["bsx"],
                                                        sm["gog"], f"gmlp_b{l}")
        dq, dk, dvv, dsink, daog = _attn_bwd(sv["P"], dcat, sm["sinks"], sm["aog"], f"attn_b{l}")
        dwo = _mm(sv["cat"].reshape(T, D), dmix, mode="tn", name=f"proj_out_dw{l}", out_dtypes=(BF16,), tk=2048,
                  order=dq).reshape(4, 2, D // NDEV, D)
        tok = reduce_start("w_out", l, dwo, dmix)
        dxa, dz, ddt, dbias, dalog, ddsk, dsng = _ssd_bwd(sv["xact"], sv["P"], sv["sprev"], dcat, sm["bias"], sm["alog"],
                                                          sm["dskE"], sm["sng"] + tok[0:1, 0:1], f"ssd_b{l}")
        dxbc, dcw, dcb = _conv_bwd(sv["P"], dxa, convw8[l], sm["cb"], f"conv_b{l}")
        dP = jnp.concatenate([dxbc, du, dz, dv, dq, dk, dvv, ddt, jnp.zeros((B, S, PW - OFF["dt"] - 128), BF16)],
                             axis=-1).reshape(T, PW)
        dwin = _mm(sv["h"].reshape(T, D), dP, mode="tn", name=f"proj_in_dw{l}", out_dtypes=(BF16,), tn=1536, tk=2048)
        dwin = jnp.transpose(_from_work_cols(dwin).reshape(D, NDEV, IN_W // NDEV), (1, 0, 2)).reshape(4, 2, D, IN_W // NDEV)
        tok = reduce_start("w_in", l, dwin, dP)
        dh = _mm(dP, win_g[l], mode="nt", name=f"proj_in_dx{l}", tk=2304, order=tok).reshape(B, S, D)
        nb = _norm_bwd(sv["x_in"], row(norm1_g[l]), f"norm1_b{l}", sc=mods[l][1], dh=dh, dres=nb2["dx"],
                       br=saved[l - 1]["m2"] if l > 0 else None, gate=mods[l - 1][5] if l > 0 else None)
        dmod[l] = jnp.concatenate([nb["dsh"], nb["dsc"], nb2["dgate"], nb2["dsh"], nb2["dsc"], dg2], axis=-1)
        gconvw[l] = dcw[:4]
        gsm[l] = dict(
            ada_b=jnp.sum(dmod[l], axis=(0, 1)), norm1_g=nb["dg"], gm_ln_g=dlng, gm_ln_b=dlnb, gm_ws=dws,
            gm_bs=dbsx.reshape(128, GM_H, 128).sum(-1).T, gm_norm_g=dgog, attn_sinks=dsink[:, 0], attn_norm_g=daog,
            conv_b=dcb, dt_bias=dbias[0, :SSM_H], a_log=dalog[0, :SSM_H], d_skip=ddsk.reshape(SSM_H, SSM_HD).sum(-1),
            ssm_norm_g=dsng, norm2_g=nb2["dg"])
    grad_x = nb["dx"]

    per_layer = [n for n in _SMALL if n != "final_norm_g"]
    g_small = [jnp.stack([gsm[l][n].reshape(args[n].shape[1:]) for l in range(L)]) for n in per_layer] + [g_final.reshape(D)]
    zc = jnp.zeros((L, 4, CONV_CH), F32)
    z1 = jnp.zeros((1, 128), F32)
    gpack = _pack([loss_part] + g_small + [jnp.stack(gconvw)])
    got = _exchange([jnp.stack(dmod).reshape(L, B, 6 * D), gpack], "ag_small", False)
    like = [z1] + [args[n] for n in _SMALL] + [zc]
    packs = [_pack([z1] + [args[p + n] for n in _SMALL] + [zc]) for p in ("", "m_", "v_")]
    sres = [_unpack(p, like) for p in _adamw(packs[0], got[1], packs[1], packs[2], "adamw_small", tr=gpack.shape[0])]
    res = {n: [r[1 + i] for r in sres] for i, n in enumerate(_SMALL)}
    loss = sres[0][0][0, 0]
    gcw = lax.dynamic_slice_in_dim(sres[0][-1], me * (CONV_CH // NDEV), CONV_CH // NDEV, axis=2)

    def update(name, parts, tr):
        w = args[name]
        r = _adamw(w.reshape(-1, w.shape[-1]), parts, args["m_" + name].reshape(-1, w.shape[-1]),
                   args["v_" + name].reshape(-1, w.shape[-1]), "adamw_" + name, tr=tr)
        res[name] = [a.reshape(w.shape) for a in r]

    update("conv_w", gcw.reshape(1, L * 4, CONV_CH // NDEV), L * 4)

    dmod_all = jnp.transpose(got[0], (1, 0, 2, 3)).reshape(L, NDEV * B, 6 * D)
    dm_mine = lax.dynamic_slice_in_dim(dmod_all, me * (6 * D // NDEV), 6 * D // NDEV, axis=2)
    dm_pad = jnp.pad(dm_mine, ((0, 0), (0, 128 - nb_rows), (0, 0))).astype(BF16)
    g_adaw = jnp.stack([_mm(c_pad, dm_pad[l], mode="tn", name=f"ada_dw{l}", tn=768) for l in range(L)])
    update("ada_w", g_adaw.reshape(1, L * D, 6 * D // NDEV), 256)

    big_res, after = dict.fromkeys(big), res["ada_w"][0]
    tile_rows = dict(w_in=256, w_out=256, w_mlp1=256, w_mlp2=128)
    for n, l, sems, s_thru, land_thru in reducing:
        parts = _chipsum_wait(sems, s_thru, land_thru, after, f"rs_wait_{n}{l}")
        w = args[n]
        big_res[n] = _adamw(w.reshape(-1, w.shape[-1]), parts, args["m_" + n].reshape(-1, w.shape[-1]),
                            args["v_" + n].reshape(-1, w.shape[-1]), f"adamw_{n}{l}", tr=tile_rows[n], row0=l * w.shape[1],
                            prev=big_res[n])
        after = big_res[n][0]
    for n in big:
        res[n] = [a.reshape(args[n].shape) for a in big_res[n]]

    names = ['ada_w', 'ada_b', 'norm1_g', 'w_in', 'gm_ln_g', 'gm_ln_b', 'gm_ws', 'gm_bs', 'gm_norm_g', 'attn_sinks',
             'attn_norm_g', 'conv_w', 'conv_b', 'dt_bias', 'a_log', 'd_skip', 'ssm_norm_g', 'w_out', 'norm2_g', 'w_mlp1',
             'w_mlp2', 'final_norm_g']
    return (loss, grad_x, *[res[n][0] for n in names], *[res[n][1] for n in names], *[res[n][2] for n in names],
            *[res[n][3] for n in names])
```

```python
import functools

import jax
import jax.numpy as jnp
import numpy as np
from jax import lax
from jax.experimental import pallas as pl
from jax.experimental.pallas import tpu as pltpu

F32, BF16 = jnp.float32, jnp.bfloat16
HI = lax.Precision.HIGHEST
MESH = pl.DeviceIdType.MESH
NDEV = 8

D = 2048
DEPTH = 2
CHUNK = 128
GM_W, GM_H = 512, 4
ATT_W, KV_W, ATT_H = 512, 128, 8
SSM_W, SSM_H, SSM_HD, SSM_G = 1024, 16, 64, 2
CONV_CH = 1536
IN_W = 4368
DFF = 8192
EPS = 1e-6
NEG_INF = -1e30
GELU_K = 0.7978845608028654
GELU_C = 0.044715

_ORIG = (("u", 512), ("v", 512), ("q", 512), ("k", 128), ("vv", 128), ("z", 1024), ("xbc", 1536), ("dt", 16))
OFF = dict(u=0, v=512, q=1024, k=1536, vv=1664, z=1792, xbc=2816, dt=4352)
PW = 4608

ADAM_LR, ADAM_B1, ADAM_B2, ADAM_EPS, ADAM_WD, ADAM_STEP = 0.001, 0.9, 0.999, 1e-08, 0.01, 10


def _to_work_cols(w):
    return jnp.pad(w, [(0, 0)] * (w.ndim - 1) + [(0, PW - IN_W)])


def _from_work_cols(wp):
    return wp[..., :IN_W]


def _sigmoid(x):
    return 1.0 / (1.0 + jnp.exp(-x))


def _gelu(x):
    return 0.5 * x * (1.0 + jnp.tanh(GELU_K * (x + GELU_C * x * x * x)))


def _gelu_grad(x):
    t = jnp.tanh(GELU_K * (x + GELU_C * x * x * x))
    return 0.5 * (1.0 + t) + 0.5 * x * (1.0 - t * t) * GELU_K * (1.0 + 3.0 * GELU_C * x * x)


def _dot(a, b, prec=None):
    return jnp.dot(a, b, precision=prec, preferred_element_type=F32)


def _dot_nt(a, b, prec=None):
    return lax.dot_general(a, b, (((1,), (1,)), ((), ())), precision=prec, preferred_element_type=F32)


def _dot_tn(a, b, prec=None):
    return lax.dot_general(a, b, (((0,), (0,)), ((), ())), precision=prec, preferred_element_type=F32)


def _full(shape):
    return pl.BlockSpec(shape, lambda *_: (0,) * len(shape))


_HBM = pl.BlockSpec(memory_space=pltpu.HBM)


def _me():
    return lax.axis_index("x"), lax.axis_index("y"), lax.axis_index("c")


def _peer(k):
    x, y, c = _me()
    px = 1 - x if k & 4 else x
    py = 1 - y if k & 2 else y
    pc = 1 - c if k & 1 else c
    return (px, py, pc), 4 * px + 2 * py + pc


def _gather_small(xs, name, order=None):
    n = len(xs)

    def body(*refs):
        ins, outs = refs[:n], refs[-n - 3:-3]
        send, recv, loc = refs[-3:]
        x, y, c = _me()
        me = 4 * x + 2 * y + c
        started = []
        for i in range(n):
            own = pltpu.make_async_copy(ins[i], outs[i].at[me], loc.at[i])
            own.start()
            started.append(own)
        for k in range(1, NDEV):
            dev, lin = _peer(k)
            for i in range(n):
                pltpu.make_async_remote_copy(
                    src_ref=ins[i], dst_ref=outs[i].at[me],
                    send_sem=send.at[i, k - 1], recv_sem=recv.at[i, k - 1], device_id=dev, device_id_type=MESH).start()
        for k in range(1, NDEV):
            dev, lin = _peer(k)
            for i in range(n):
                pltpu.make_async_remote_copy(
                    src_ref=ins[i], dst_ref=outs[i].at[lin],
                    send_sem=send.at[i, k - 1], recv_sem=recv.at[i, k - 1], device_id=dev, device_id_type=MESH).wait()
        for own in started:
            own.wait()

    extra = [] if order is None else [order]
    return pl.pallas_call(
        body, name=name, out_shape=[jax.ShapeDtypeStruct((NDEV,) + a.shape, a.dtype) for a in xs],
        in_specs=[_HBM] * n + [pl.BlockSpec(memory_space=pl.ANY)] * len(extra), out_specs=[_HBM] * n,
        scratch_shapes=[pltpu.SemaphoreType.DMA((n, NDEV - 1)), pltpu.SemaphoreType.DMA((n, NDEV - 1)),
                        pltpu.SemaphoreType.DMA((n,))],
        compiler_params=pltpu.CompilerParams(has_side_effects=True),
    )(*xs, *extra)


def _chips():
    x, y, c = _me()
    return x, y, c, [(1 - x, y), (x, 1 - y), (1 - x, 1 - y)]


def _gather2(xs, name):
    n = len(xs)

    def body(*refs):
        ins, outs = refs[:n], refs[n:2 * n]
        send, recv, loc = refs[2 * n:]
        x, y, c, chips = _chips()
        me, sib = (x, y, c), (x, y, 1 - c)

        def cp(i, k, block, to, src=None):
            slot = outs[i].at[4 * block[0] + 2 * block[1] + block[2]]
            return pltpu.make_async_remote_copy(src_ref=slot if src is None else src, dst_ref=slot, send_sem=send.at[i, k],
                                                recv_sem=recv.at[i, k], device_id=to, device_id_type=MESH)

        sent = []
        for i in range(n):
            for j, chip in enumerate(chips):
                sent.append(cp(i, 1 + j, me, (*chip, c), src=ins[i]))
            sent.append(cp(i, 0, me, sib, src=ins[i]))
        for s in sent:
            s.start()
        own = [pltpu.make_async_copy(ins[i], outs[i].at[4 * x + 2 * y + c], loc.at[i]) for i in range(n)]
        for o in own:
            o.start()
        for j, chip in enumerate(chips):
            for i in range(n):
                cp(i, 1 + j, (*chip, c), me).wait_recv()
                fwd = cp(i, 4 + j, (*chip, c), sib)
                fwd.start()
                sent.append(fwd)
        for i in range(n):
            cp(i, 0, sib, me).wait_recv()
            for j, chip in enumerate(chips):
                cp(i, 4 + j, (*chip, 1 - c), me).wait_recv()
        for s in sent:
            s.wait_send()
        for o in own:
            o.wait()

    return pl.pallas_call(
        body, name=name, out_shape=[jax.ShapeDtypeStruct((NDEV,) + a.shape, a.dtype) for a in xs],
        in_specs=[_HBM] * n, out_specs=[_HBM] * n,
        scratch_shapes=[pltpu.SemaphoreType.DMA((n, 7)), pltpu.SemaphoreType.DMA((n, 7)), pltpu.SemaphoreType.DMA((n,))],
        compiler_params=pltpu.CompilerParams(has_side_effects=True),
    )(*xs)


def _pair_exchange(ps, name):
    n = len(ps)

    def body(*refs):
        ins, outs = refs[:n], refs[n:2 * n]
        send, recv = refs[2 * n:]
        x, y, c = _me()
        cps = [pltpu.make_async_remote_copy(src_ref=ins[i].at[ch, 1 - c], dst_ref=outs[i].at[ch], send_sem=send.at[i, ch],
                                            recv_sem=recv.at[i, ch], device_id=(x, y, 1 - c), device_id_type=MESH)
               for i in range(n) for ch in range(4)]
        for cp in cps:
            cp.start()
        for cp in cps:
            cp.wait()

    return pl.pallas_call(
        body, name=name, out_shape=[jax.ShapeDtypeStruct((4,) + a.shape[2:], a.dtype) for a in ps],
        in_specs=[_HBM] * n, out_specs=[_HBM] * n,
        scratch_shapes=[pltpu.SemaphoreType.DMA((n, 4)), pltpu.SemaphoreType.DMA((n, 4))],
        compiler_params=pltpu.CompilerParams(has_side_effects=True),
    )(*ps)


def _pair_add(p, r1, core, name, tr=256):
    _, _, R, C = p.shape
    tr = min(tr, R)

    def body(core_ref, p_ref, r_ref, o_ref, o2_ref):
        s = (p_ref[...].astype(F32) + r_ref[...].astype(F32)).astype(o_ref.dtype)
        o_ref[...] = s
        o2_ref[...] = s

    blk = pl.BlockSpec((None, tr, C), lambda ch, i, core_ref: (ch, i, 0))
    return pl.pallas_call(
        body, name=name, out_shape=[jax.ShapeDtypeStruct((4, R, C), p.dtype)] * 2,
        grid_spec=pltpu.PrefetchScalarGridSpec(
            num_scalar_prefetch=1, grid=(4, R // tr),
            in_specs=[pl.BlockSpec((None, None, tr, C), lambda ch, i, core_ref: (ch, core_ref[0], i, 0)), blk],
            out_specs=[blk, blk]),
    )(core, p, r1)


_SEM = pl.BlockSpec(memory_space=pltpu.SEMAPHORE)
_ANY = pl.BlockSpec(memory_space=pl.ANY)
_DATAFLOW = pltpu.SideEffectType.DATAFLOW_SIDE_EFFECTING


def _hbm(a):
    return pltpu.with_memory_space_constraint(a, pltpu.HBM)


def _gather_targets():
    x, y, c, chips = _chips()
    return 4 * x + 2 * y + c, [(x, y, 1 - c)] + [(*chip, c) for chip in chips]


def _landing_zone(v, me, name, tr=512):
    R, C = v.shape
    tr = min(tr, R)

    def body(me_ref, v_ref, o_ref):
        o_ref[...] = v_ref[...]

    return pl.pallas_call(
        body, name=name, out_shape=jax.ShapeDtypeStruct((NDEV, R, C), v.dtype),
        grid_spec=pltpu.PrefetchScalarGridSpec(
            num_scalar_prefetch=1, grid=(R // tr,), in_specs=[pl.BlockSpec((tr, C), lambda i, me_ref: (i, 0))],
            out_specs=pl.BlockSpec((None, tr, C), lambda i, me_ref: (me_ref[0], i, 0))),
    )(me, v)


def _gather_start(v, land, order, name):
    def body(v_ref, land_ref, order_ref, *rest):
        sems, token = rest[:8], rest[10]
        me, targets = _gather_targets()
        for k, to in enumerate(targets):
            pltpu.make_async_remote_copy(src_ref=v_ref, dst_ref=land_ref.at[me], send_sem=sems[k], recv_sem=sems[4 + k],
                                         device_id=to, device_id_type=MESH).start()
        token[...] = jnp.zeros_like(token)

    outs = pl.pallas_call(
        body, name=name,
        out_shape=(pltpu.SemaphoreType.DMA(()),) * 8 + (pltpu.HBM(v.shape, v.dtype), pltpu.HBM(land.shape, land.dtype),
                                                        jax.ShapeDtypeStruct((8, 128), F32)),
        in_specs=(_HBM, _HBM, _ANY), out_specs=(_SEM,) * 8 + (_HBM, _HBM, pl.BlockSpec(memory_space=pltpu.VMEM)),
        input_output_aliases={0: 8, 1: 9}, compiler_params=pltpu.CompilerParams(has_side_effects=_DATAFLOW),
    )(_hbm(v), _hbm(land), order)
    return outs[:8], outs[8], outs[9], outs[10]


def _gather_wait(sems, v_thru, land_thru, after, name):
    def body(v_ref, land_ref, *rest):
        sems_ = rest[:8]
        me, targets = _gather_targets()
        for k, to in enumerate(targets):
            cp = pltpu.make_async_remote_copy(src_ref=v_ref, dst_ref=land_ref.at[me], send_sem=sems_[k], recv_sem=sems_[4 + k],
                                              device_id=to, device_id_type=MESH)
            cp.wait_send()
            cp.wait_recv()

    return pl.pallas_call(
        body, name=name, out_shape=(pltpu.HBM(v_thru.shape, v_thru.dtype), pltpu.HBM(land_thru.shape, land_thru.dtype)),
        in_specs=(_HBM, _HBM) + (_SEM,) * 8 + (_ANY,), out_specs=(_HBM, _HBM), input_output_aliases={0: 0, 1: 1},
        compiler_params=pltpu.CompilerParams(has_side_effects=_DATAFLOW),
    )(v_thru, land_thru, *sems, after)


def _gather_finish(land, name):
    def body(land_ref, out, send, recv):
        x, y, c, chips = _chips()
        fwd = [pltpu.make_async_remote_copy(src_ref=out.at[4 * px + 2 * py + c], dst_ref=out.at[4 * px + 2 * py + c],
                                            send_sem=send.at[j], recv_sem=recv.at[j], device_id=(x, y, 1 - c), device_id_type=MESH)
               for j, (px, py) in enumerate(chips)]
        for cp in fwd:
            cp.start()
        for j, (px, py) in enumerate(chips):
            slot = out.at[4 * px + 2 * py + 1 - c]
            pltpu.make_async_remote_copy(src_ref=slot, dst_ref=slot, send_sem=send.at[j], recv_sem=recv.at[j],
                                         device_id=(x, y, 1 - c), device_id_type=MESH).wait()

    return pl.pallas_call(
        body, name=name, out_shape=jax.ShapeDtypeStruct(land.shape, land.dtype),
        in_specs=[_HBM], out_specs=_HBM, input_output_aliases={0: 0},
        scratch_shapes=[pltpu.SemaphoreType.DMA((3,)), pltpu.SemaphoreType.DMA((3,))],
        compiler_params=pltpu.CompilerParams(has_side_effects=True),
    )(land)


def _chip_targets():
    x, y, c, chips = _chips()
    return 2 * x + y, [((px, py, c), 2 * px + py) for px, py in chips]


def _chipsum_start(s, land, order, name):
    def body(s_ref, land_ref, order_ref, *rest):
        sems, token = rest[:6], rest[8]
        mine, targets = _chip_targets()
        for k, (to, ch) in enumerate(targets):
            pltpu.make_async_remote_copy(src_ref=s_ref.at[ch], dst_ref=land_ref.at[mine], send_sem=sems[k], recv_sem=sems[3 + k],
                                         device_id=to, device_id_type=MESH).start()
        token[...] = jnp.zeros_like(token)

    outs = pl.pallas_call(
        body, name=name,
        out_shape=(pltpu.SemaphoreType.DMA(()),) * 6 + (pltpu.HBM(s.shape, s.dtype), pltpu.HBM(land.shape, land.dtype),
                                                        jax.ShapeDtypeStruct((8, 128), F32)),
        in_specs=(_HBM, _HBM, _ANY), out_specs=(_SEM,) * 6 + (_HBM, _HBM, pl.BlockSpec(memory_space=pltpu.VMEM)),
        input_output_aliases={0: 6, 1: 7}, compiler_params=pltpu.CompilerParams(has_side_effects=_DATAFLOW),
    )(_hbm(s), _hbm(land), order)
    return outs[:6], outs[6], outs[7], outs[8]


def _chipsum_wait(sems, s_thru, land_thru, after, name):
    def body(s_ref, land_ref, *rest):
        sems_ = rest[:6]
        mine, targets = _chip_targets()
        for k, (to, ch) in enumerate(targets):
            cp = pltpu.make_async_remote_copy(src_ref=s_ref.at[ch], dst_ref=land_ref.at[ch], send_sem=sems_[k], recv_sem=sems_[3 + k],
                                              device_id=to, device_id_type=MESH)
            cp.wait_send()
            cp.wait_recv()

    return pl.pallas_call(
        body, name=name, out_shape=(pltpu.HBM(s_thru.shape, s_thru.dtype), pltpu.HBM(land_thru.shape, land_thru.dtype)),
        in_specs=(_HBM, _HBM) + (_SEM,) * 6 + (_ANY,), out_specs=(_HBM, _HBM), input_output_aliases={0: 0, 1: 1},
        compiler_params=pltpu.CompilerParams(has_side_effects=_DATAFLOW),
    )(s_thru, land_thru, *sems, after)[1]


def _mm(a, b, *, mode, name, out_dtypes=(F32,), epilogue=None, extras=(), tm=1024, tn=1024, tk=2048,
        col_blocked_b=False, col_blocked_out=False, order=None):
    CB = 1024
    if col_blocked_b:
        assert mode in ("nn", "nt") and b.shape[2] == CB
        (M, K), N = a.shape, (b.shape[0] * CB if mode == "nn" else b.shape[1])
        assert mode == "nn" or tk % CB == 0
        tn = CB if mode == "nn" else tn
    elif mode == "nn":
        (M, K), N = a.shape, b.shape[1]
    elif mode == "nt":
        (M, K), N = a.shape, b.shape[0]
    else:
        (K, M), N = a.shape, b.shape[1]
    if col_blocked_out:
        assert len(out_dtypes) == 1 and N % CB == 0
        tn = CB
    tm, tn, tk = min(tm, M), min(tn, N), min(tk, K)
    assert M % tm == 0 and N % tn == 0 and K % tk == 0, (M, N, K, tm, tn, tk)
    nk = K // tk
    ne, no = len(extras), len(out_dtypes)
    dims = {"nn": (((1,), (0,)), ((), ())), "nt": (((1,), (1,)), ((), ())), "tn": (((0,), (0,)), ((), ()))}[mode]

    no_ = 0 if order is None else 1

    def body(a_ref, b_ref, *rest):
        rest = rest[no_:]
        ex, outs = rest[:ne], rest[ne:ne + no]

        def finish(acc):
            res = epilogue(acc, *[e[...] for e in ex]) if epilogue is not None else (acc,)
            for o, r in zip(outs, res):
                o[...] = r.astype(o.dtype)

        if col_blocked_b and mode == "nt":
            part = sum(lax.dot_general(a_ref[:, q * CB:(q + 1) * CB], b_ref[q], dims, preferred_element_type=F32)
                       for q in range(tk // CB))
        else:
            part = lax.dot_general(a_ref[...], b_ref[...], dims, preferred_element_type=F32)
        if nk == 1:
            finish(part)
        else:
            acc_ref = rest[-1]
            k = pl.program_id(2)

            @pl.when(k == 0)
            def _():
                acc_ref[...] = part

            @pl.when(k > 0)
            def _():
                acc_ref[...] += part

            @pl.when(k == nk - 1)
            def _():
                finish(acc_ref[...])

    a_spec = {"nn": pl.BlockSpec((tm, tk), lambda i, j, k: (i, k)), "nt": pl.BlockSpec((tm, tk), lambda i, j, k: (i, k)),
              "tn": pl.BlockSpec((tk, tm), lambda i, j, k: (k, i))}[mode]
    b_spec = {"nn": pl.BlockSpec((tk, tn), lambda i, j, k: (k, j)), "nt": pl.BlockSpec((tn, tk), lambda i, j, k: (j, k)),
              "tn": pl.BlockSpec((tk, tn), lambda i, j, k: (k, j))}[mode]
    if col_blocked_b:
        b_spec = (pl.BlockSpec((None, tk, CB), lambda i, j, k: (j, k, 0)) if mode == "nn"
                  else pl.BlockSpec((tk // CB, tn, CB), lambda i, j, k: (k, j, 0)))
    e_spec = pl.BlockSpec((tm, tn), lambda i, j, k: (i, j))
    o_spec, o_dims = e_spec, (M, N)
    if col_blocked_out:
        o_spec, o_dims = pl.BlockSpec((None, tm, CB), lambda i, j, k: (j, i, 0)), (N // CB, M, CB)
    outs = pl.pallas_call(
        body, name=name, grid=(M // tm, N // tn, nk),
        in_specs=[a_spec, b_spec] + [_ANY] * no_ + [e_spec] * ne, out_specs=[o_spec] * no,
        out_shape=[jax.ShapeDtypeStruct(o_dims, dt) for dt in out_dtypes],
        scratch_shapes=[pltpu.VMEM((tm, tn), F32)] if nk > 1 else [],
        compiler_params=pltpu.CompilerParams(dimension_semantics=("parallel", "parallel", "arbitrary")),
    )(a, b, *([] if order is None else [order]), *extras)
    return outs if no > 1 else outs[0]


def _norm_fwd(x, g, sc, sh, name, resid=None):
    B, S, Dm = x.shape
    ts = min(S, 256)
    tok = pl.BlockSpec((None, ts, Dm), lambda b, i: (b, i, 0))
    row = pl.BlockSpec((None, 1, Dm), lambda b, i: (b, 0, 0))
    par = pl.BlockSpec((1, Dm), lambda b, i: (0, 0))

    def body(*refs):
        if resid is not None:
            x_ref, br_ref, gt_ref, g_ref, sc_ref, sh_ref, xo_ref, h_ref = refs
            xv = x_ref[...] + gt_ref[...] * br_ref[...]
            xo_ref[...] = xv
        else:
            x_ref, g_ref, sc_ref, sh_ref, h_ref = refs
            xv = x_ref[...]
        r = lax.rsqrt(jnp.mean(xv * xv, axis=-1, keepdims=True) + EPS)
        h_ref[...] = ((xv * r * g_ref[...]) * (1.0 + sc_ref[...]) + sh_ref[...]).astype(BF16)

    h_shape = jax.ShapeDtypeStruct((B, S, Dm), BF16)
    if resid is not None:
        return pl.pallas_call(body, name=name, grid=(B, S // ts), in_specs=[tok, tok, row, par, row, row],
                              out_specs=[tok, tok], out_shape=[jax.ShapeDtypeStruct((B, S, Dm), F32), h_shape],
                              )(x, resid[0], resid[1], g, sc, sh)
    return pl.pallas_call(body, name=name, grid=(B, S // ts), in_specs=[tok, par, row, row], out_specs=tok,
                          out_shape=h_shape)(x, g, sc, sh)


def _norm_bwd(x, g, name, *, sc=None, dh=None, dres=None, tgt=None, br=None, gate=None, x_is_prev=False):
    B, S, Dm = x.shape
    ts = min(S, 256)
    final = tgt is not None
    has_br = br is not None
    tok = pl.BlockSpec((None, ts, Dm), lambda b, i: (b, i, 0))
    row = pl.BlockSpec((None, 1, Dm), lambda b, i: (b, 0, 0))
    par = pl.BlockSpec((1, Dm), lambda b, i: (0, 0))
    ins, in_specs = [x, g], [tok, par]
    if final:
        ins, in_specs = ins + [tgt], in_specs + [tok]
    else:
        ins, in_specs = ins + [sc, dh], in_specs + [row, tok]
    if dres is not None:
        ins, in_specs = ins + [dres], in_specs + [tok]
    if has_br:
        ins, in_specs = ins + [br, gate], in_specs + [tok, row]
    n_in = len(ins)
    out_shape = [jax.ShapeDtypeStruct((B, S, Dm), F32), jax.ShapeDtypeStruct((1, Dm), F32)]
    out_specs = [tok, par]
    if final:
        out_shape.append(jax.ShapeDtypeStruct((1, 128), F32))
        out_specs.append(pl.BlockSpec((1, 128), lambda b, i: (0, 0)))
    else:
        out_shape += [jax.ShapeDtypeStruct((B, 1, Dm), F32)] * 2
        out_specs += [row, row]
    if has_br:
        out_shape += [jax.ShapeDtypeStruct((B, S, Dm), BF16), jax.ShapeDtypeStruct((B, 1, Dm), F32)]
        out_specs += [tok, row]

    def body(*refs):
        it = iter(refs[:n_in])
        outs = iter(refs[n_in:])
        x_ref, g_ref = next(it), next(it)
        b, i = pl.program_id(0), pl.program_id(1)
        first, first_row = (b == 0) & (i == 0), i == 0
        xv, gv = x_ref[...], g_ref[...]
        if x_is_prev:
            xv = xv + refs[n_in - 1][...] * refs[n_in - 2][...]
        r = lax.rsqrt(jnp.mean(xv * xv, axis=-1, keepdims=True) + EPS)
        n = xv * r
        dx_ref, dg_ref = next(outs), next(outs)

        def acc(ref, val, init):
            @pl.when(init)
            def _():
                ref[...] = val

            @pl.when(jnp.logical_not(init))
            def _():
                ref[...] += val

        if final:
            t_ref = next(it)
            loss_ref = next(outs)
            e = n * gv - t_ref[...]
            acc(loss_ref, jnp.zeros((1, 128), F32) + 0.5 * jnp.sum(e * e) / Dm, first)
            dyg = e * (1.0 / Dm)
        else:
            sc_ref, dh_ref = next(it), next(it)
            dsc_ref, dsh_ref = next(outs), next(outs)
            dhv = dh_ref[...]
            acc(dsh_ref, jnp.sum(dhv, axis=0, keepdims=True), first_row)
            acc(dsc_ref, jnp.sum(dhv * (n * gv), axis=0, keepdims=True), first_row)
            dyg = dhv * (1.0 + sc_ref[...])
        acc(dg_ref, jnp.sum(dyg * n, axis=0, keepdims=True), first)
        dn = dyg * gv
        dx = r * (dn - n * jnp.mean(dn * n, axis=-1, keepdims=True))
        if dres is not None:
            dx = dx + next(it)[...]
        dx_ref[...] = dx
        if has_br:
            br_ref, gt_ref = next(it), next(it)
            dbr_ref, dgt_ref = next(outs), next(outs)
            dbr_ref[...] = (dx * gt_ref[...]).astype(BF16)
            acc(dgt_ref, jnp.sum(dx * br_ref[...], axis=0, keepdims=True), first_row)

    outs = pl.pallas_call(body, name=name, grid=(B, S // ts), in_specs=in_specs, out_specs=out_specs, out_shape=out_shape,
                          compiler_params=pltpu.CompilerParams(dimension_semantics=("arbitrary", "arbitrary")))(*ins)
    res = dict(dx=outs[0], dg=outs[1])
    if final:
        res["loss"] = outs[2]
    else:
        res["dsc"], res["dsh"] = outs[2], outs[3]
    if has_br:
        res["dbr"], res["dgate"] = outs[-2], outs[-1]
    return res


def _gm_heads(vg, lng, lnb):
    res = []
    for h in range(GM_H):
        sl = slice(h * 128, (h + 1) * 128)
        vh = vg[:, sl]
        xc = vh - jnp.mean(vh, axis=-1, keepdims=True)
        rstd = lax.rsqrt(jnp.mean(xc * xc, axis=-1, keepdims=True) + 1e-5)
        xhat = xc * rstd
        res.append((xhat, rstd, xhat * lng[:, sl] + lnb[:, sl]))
    return res


def _gm_gate(heads, wt_ref, bsx, nch):
    cols = []
    for h in range(GM_H):
        vn = heads[h][2].astype(BF16)
        rows = [_dot(wt_ref[h], vn[c * CHUNK:(c + 1) * CHUNK]) + bsx[:, h * 128:(h + 1) * 128] for c in range(nch)]
        cols.append(jnp.concatenate(rows, axis=0) if nch > 1 else rows[0])
    return jnp.concatenate(cols, axis=1)


def _gm_specs(S):
    tb = min(S, 512)
    u = pl.BlockSpec((None, tb, GM_W), lambda b, i: (b, i, OFF["u"] // GM_W))
    v = pl.BlockSpec((None, tb, GM_W), lambda b, i: (b, i, OFF["v"] // GM_W))
    tok = pl.BlockSpec((None, tb, GM_W), lambda b, i: (b, i, 0))
    return tb, u, v, tok


def _gmlp_fwd(P, lng, lnb, wt, bsx, og, name):
    B, S, _ = P.shape
    tb, u_spec, v_spec, tok = _gm_specs(S)
    nch = tb // CHUNK

    def body(u_ref, v_ref, lng_ref, lnb_ref, wt_ref, bsx_ref, og_ref, o_ref):
        heads = _gm_heads(_gelu(v_ref[...]), lng_ref[...], lnb_ref[...])
        y = _gelu(u_ref[...]) * _gm_gate(heads, wt_ref, bsx_ref[...], nch)
        r = lax.rsqrt(jnp.mean(y * y, axis=-1, keepdims=True) + EPS)
        o_ref[...] = (y * r * og_ref[...]).astype(BF16)

    return pl.pallas_call(
        body, name=name, grid=(B, S // tb),
        in_specs=[u_spec, v_spec, _full((1, GM_W)), _full((1, GM_W)), _full((GM_H, 128, 128)), _full((128, GM_W)), _full((1, GM_W))],
        out_specs=tok, out_shape=jax.ShapeDtypeStruct((B, S, GM_W), BF16))(P, P, lng, lnb, wt, bsx, og)


def _gmlp_bwd(P, dcat, lng, lnb, wt, wtT, bsx, og, name):
    B, S, _ = P.shape
    tb, u_spec, v_spec, tok = _gm_specs(S)
    nch = tb // CHUNK
    do_spec = pl.BlockSpec((None, tb, GM_W), lambda b, i: (b, i, 0))

    def body(u_ref, v_ref, do_ref, lng_ref, lnb_ref, wt_ref, wtT_ref, bsx_ref, og_ref,
             du_ref, dv_ref, dlng_ref, dlnb_ref, dws_ref, dbsx_ref, dog_ref):
        first = (pl.program_id(0) == 0) & (pl.program_id(1) == 0)

        @pl.when(first)
        def _():
            for ref in (dlng_ref, dlnb_ref, dws_ref, dbsx_ref, dog_ref):
                ref[...] = jnp.zeros(ref.shape, F32)

        u, v, lng = u_ref[...], v_ref[...], lng_ref[...]
        ug = _gelu(u)
        heads = _gm_heads(_gelu(v), lng, lnb_ref[...])
        gate = _gm_gate(heads, wt_ref, bsx_ref[...], nch)
        y = ug * gate
        r = lax.rsqrt(jnp.mean(y * y, axis=-1, keepdims=True) + EPS)
        yn = y * r
        dout = do_ref[...]
        dog_ref[...] += jnp.sum(dout * yn, axis=0, keepdims=True)
        dyn = dout * og_ref[...]
        dy = r * (dyn - yn * jnp.mean(dyn * yn, axis=-1, keepdims=True))
        du_ref[...] = (dy * gate * _gelu_grad(u)).astype(BF16)
        dgate = dy * ug
        tril = lax.broadcasted_iota(jnp.int32, (128, 128), 0) >= lax.broadcasted_iota(jnp.int32, (128, 128), 1)
        dvg = []
        for h in range(GM_H):
            sl = slice(h * 128, (h + 1) * 128)
            xhat, rstd, vn = heads[h]
            vnb = vn.astype(BF16)
            dgh = dgate[:, sl]
            dgb = dgh.astype(BF16)
            dbs = jnp.zeros((128, 128), F32)
            dw = jnp.zeros((128, 128), F32)
            dvn = []
            for c in range(nch):
                rs = slice(c * CHUNK, (c + 1) * CHUNK)
                dbs = dbs + dgh[rs]
                dw = dw + _dot_nt(dgb[rs], vnb[rs])
                dvn.append(_dot(wtT_ref[h], dgb[rs]))
            dvn = jnp.concatenate(dvn, axis=0) if nch > 1 else dvn[0]
            dbsx_ref[:, sl] += dbs
            dws_ref[h] += jnp.where(tril, dw, 0.0)
            dlng_ref[:, sl] += jnp.sum(dvn * xhat, axis=0, keepdims=True)
            dlnb_ref[:, sl] += jnp.sum(dvn, axis=0, keepdims=True)
            dxh = dvn * lng[:, sl]
            dvg.append(rstd * (dxh - jnp.mean(dxh, axis=-1, keepdims=True) - xhat * jnp.mean(dxh * xhat, axis=-1, keepdims=True)))
        dv_ref[...] = (jnp.concatenate(dvg, axis=1) * _gelu_grad(v)).astype(BF16)

    p512, w3 = _full((1, GM_W)), _full((GM_H, 128, 128))
    return pl.pallas_call(
        body, name=name, grid=(B, S // tb),
        in_specs=[u_spec, v_spec, do_spec, p512, p512, w3, w3, _full((128, GM_W)), p512],
        out_specs=[tok, tok, p512, p512, w3, _full((128, GM_W)), p512],
        out_shape=[jax.ShapeDtypeStruct((B, S, GM_W), BF16)] * 2 + [
            jax.ShapeDtypeStruct((1, GM_W), F32), jax.ShapeDtypeStruct((1, GM_W), F32),
            jax.ShapeDtypeStruct((GM_H, 128, 128), F32), jax.ShapeDtypeStruct((128, GM_W), F32),
            jax.ShapeDtypeStruct((1, GM_W), F32)],
        compiler_params=pltpu.CompilerParams(dimension_semantics=("arbitrary", "arbitrary")),
    )(P, P, dcat, lng, lnb, wt, wtT, bsx, og)


def _lane_half():
    return lax.broadcasted_iota(jnp.int32, (128, 128), 1) // 64


def _att_stack(x, kvh, dtype):
    half = _lane_half()
    rows = []
    for g in range(4):
        i = kvh * 4 + g
        pair = x[:, (i // 2) * 128:(i // 2 + 1) * 128]
        if i % 2 != kvh:
            pair = pltpu.roll(pair, 64, 1)
        rows.append(jnp.where(half == kvh, pair, 0.0))
    return jnp.concatenate(rows, axis=0).astype(dtype)


def _att_unstack(pairs, y, kvh):
    half = _lane_half()
    for g in range(4):
        i = kvh * 4 + g
        piece = y[g * 128:(g + 1) * 128]
        if i % 2 != kvh:
            piece = pltpu.roll(piece, 64, 1)
        pairs[i // 2] = jnp.where(half == i % 2, piece, pairs[i // 2])
    return pairs


def _att_probs(qb, k2, st, sink_ref, kvh):
    qm = _att_stack(qb, kvh, BF16)
    s = _dot_nt(qm, k2) * (64 ** -0.5)
    qi = lax.broadcasted_iota(jnp.int32, (512, 256), 0) % 128
    kj = lax.broadcasted_iota(jnp.int32, (512, 256), 1)
    diff = qi + 128 - kj
    valid = (diff >= 0) & (diff < 128) & (st + kj - 128 >= 0)
    s = jnp.where(valid, s, NEG_INF)
    grp = lax.broadcasted_iota(jnp.int32, (512, 1), 0) // 128
    sink = jnp.zeros((512, 1), F32)
    for g in range(4):
        sink = jnp.where(grp == g, sink_ref[kvh * 4 + g], sink)
    m = jnp.maximum(jnp.max(s, axis=-1, keepdims=True), sink)
    e = jnp.exp(s - m)
    esink = jnp.exp(sink - m)
    inv = 1.0 / (jnp.sum(e, axis=-1, keepdims=True) + esink)
    return qm, e * inv, esink * inv


def _att_specs(S):
    q = pl.BlockSpec((None, S, ATT_W), lambda b: (b, 0, OFF["q"] // ATT_W))
    k = pl.BlockSpec((None, S, KV_W), lambda b: (b, 0, OFF["k"] // KV_W))
    v = pl.BlockSpec((None, S, KV_W), lambda b: (b, 0, OFF["vv"] // KV_W))
    tok = pl.BlockSpec((None, S, ATT_W), lambda b: (b, 0, 0))
    kv = pl.BlockSpec((None, S, KV_W), lambda b: (b, 0, 0))
    return q, k, v, tok, kv


_SMEM = pl.BlockSpec(memory_space=pltpu.SMEM)


def _attn_fwd(P, sinks, og, name):
    B, S, _ = P.shape
    q_spec, k_spec, v_spec, tok, _ = _att_specs(S)

    def body(q_ref, k_ref, v_ref, sink_ref, og_ref, o_ref, kpad, vpad):
        kpad[0:128, :] = jnp.zeros((128, KV_W), BF16)
        vpad[0:128, :] = jnp.zeros((128, KV_W), BF16)
        kpad[128:, :] = k_ref[...].astype(BF16)
        vpad[128:, :] = v_ref[...].astype(BF16)

        def step(n, carry):
            st = pl.multiple_of(n * 128, 128)
            qb = q_ref[pl.ds(st, 128), :]
            k2, v2 = kpad[pl.ds(st, 256), :], vpad[pl.ds(st, 256), :]
            pairs = [jnp.zeros((128, 128), F32)] * 4
            for kvh in range(2):
                _, p, _ = _att_probs(qb, k2, st, sink_ref, kvh)
                pairs = _att_unstack(pairs, _dot(p.astype(BF16), v2), kvh)
            o = jnp.concatenate(pairs, axis=1)
            r = lax.rsqrt(jnp.mean(o * o, axis=-1, keepdims=True) + EPS)
            o_ref[pl.ds(st, 128), :] = (o * r * og_ref[...]).astype(BF16)
            return carry

        lax.fori_loop(0, S // 128, step, 0)

    return pl.pallas_call(
        body, name=name, grid=(B,), in_specs=[q_spec, k_spec, v_spec, _SMEM, _full((1, ATT_W))], out_specs=tok,
        out_shape=jax.ShapeDtypeStruct((B, S, ATT_W), BF16),
        scratch_shapes=[pltpu.VMEM((S + 128, KV_W), BF16)] * 2)(P, P, P, sinks, og)


def _attn_bwd(P, dcat, sinks, og, name):
    B, S, _ = P.shape
    q_spec, k_spec, v_spec, tok, kv = _att_specs(S)
    do_spec = pl.BlockSpec((None, S, ATT_W), lambda b: (b, 0, GM_W // ATT_W))

    def body(q_ref, k_ref, v_ref, do_ref, sink_ref, og_ref, dq_ref, dk_ref, dv_ref, dsink_ref, dog_ref,
             kpad, vpad, dkpad, dvpad):
        @pl.when(pl.program_id(0) == 0)
        def _():
            dsink_ref[...] = jnp.zeros((8, 128), F32)
            dog_ref[...] = jnp.zeros((1, ATT_W), F32)

        kpad[0:128, :] = jnp.zeros((128, KV_W), BF16)
        vpad[0:128, :] = jnp.zeros((128, KV_W), BF16)
        kpad[128:, :] = k_ref[...].astype(BF16)
        vpad[128:, :] = v_ref[...].astype(BF16)
        dkpad[...] = jnp.zeros((S + 128, KV_W), F32)
        dvpad[...] = jnp.zeros((S + 128, KV_W), F32)
        half = _lane_half()
        head_row = lax.broadcasted_iota(jnp.int32, (8, 128), 0)

        def step(n, carry):
            st = pl.multiple_of(n * 128, 128)
            qb = q_ref[pl.ds(st, 128), :]
            k2, v2 = kpad[pl.ds(st, 256), :], vpad[pl.ds(st, 256), :]
            saved, pairs = [], [jnp.zeros((128, 128), F32)] * 4
            for kvh in range(2):
                qm, p, psink = _att_probs(qb, k2, st, sink_ref, kvh)
                o = _dot(p.astype(BF16), v2)
                saved.append((qm, p, psink, o))
                pairs = _att_unstack(pairs, o, kvh)
            o = jnp.concatenate(pairs, axis=1)
            r = lax.rsqrt(jnp.mean(o * o, axis=-1, keepdims=True) + EPS)
            on = o * r
            dout = do_ref[pl.ds(st, 128), :]
            dog_ref[...] += jnp.sum(dout * on, axis=0, keepdims=True)
            dyn = dout * og_ref[...]
            do = r * (dyn - on * jnp.mean(dyn * on, axis=-1, keepdims=True))
            dq_pairs = [jnp.zeros((128, 128), F32)] * 4
            dsink = jnp.zeros((8, 128), F32)
            for kvh in range(2):
                qm, p, psink, og_ = saved[kvh]
                dog = _att_stack(do, kvh, F32)
                delta = jnp.sum(dog * jnp.where(jnp.concatenate([half] * 4, axis=0) == kvh, og_, 0.0), axis=-1, keepdims=True)
                dogb, pb = dog.astype(BF16), p.astype(BF16)
                dvpad[pl.ds(st, 256), :] += _dot_tn(pb, dogb)
                dp = _dot_nt(dogb, v2)
                ds = (p * (dp - delta) * (64 ** -0.5)).astype(BF16)
                sd = psink * delta
                for g in range(4):
                    dsink = dsink - jnp.where(head_row == kvh * 4 + g, jnp.sum(sd[g * 128:(g + 1) * 128]), 0.0)
                dq_pairs = _att_unstack(dq_pairs, _dot(ds, k2), kvh)
                dkpad[pl.ds(st, 256), :] += _dot_tn(ds, qm)
            dsink_ref[...] += dsink
            dq_ref[pl.ds(st, 128), :] = jnp.concatenate(dq_pairs, axis=1).astype(BF16)
            return carry

        lax.fori_loop(0, S // 128, step, 0)
        dk_ref[...] = dkpad[128:, :].astype(BF16)
        dv_ref[...] = dvpad[128:, :].astype(BF16)

    return pl.pallas_call(
        body, name=name, grid=(B,),
        in_specs=[q_spec, k_spec, v_spec, do_spec, _SMEM, _full((1, ATT_W))],
        out_specs=[tok, kv, kv, _full((8, 128)), _full((1, ATT_W))],
        out_shape=[jax.ShapeDtypeStruct((B, S, ATT_W), BF16), jax.ShapeDtypeStruct((B, S, KV_W), BF16),
                   jax.ShapeDtypeStruct((B, S, KV_W), BF16), jax.ShapeDtypeStruct((8, 128), F32),
                   jax.ShapeDtypeStruct((1, ATT_W), F32)],
        scratch_shapes=[pltpu.VMEM((S + 128, KV_W), BF16)] * 2 + [pltpu.VMEM((S + 128, KV_W), F32)] * 2,
        compiler_params=pltpu.CompilerParams(dimension_semantics=("arbitrary",)),
    )(P, P, P, dcat, sinks, og)


CONV_TC = 256
CONV_RC = 64


def _conv_taps(ext, r0):
    return [ext[pl.ds(r0 + 8 - k, CONV_RC), :] for k in range(4)]


def _conv_pre(taps, w_ref, b_ref):
    acc = b_ref[...] + w_ref[3:4, :] * taps[0]
    for k in range(1, 4):
        acc = acc + w_ref[3 - k:4 - k, :] * taps[k]
    return acc


def _conv_fwd(P, w8, b, name):
    B, S, _ = P.shape
    nj = CONV_CH // CONV_TC
    x_spec = pl.BlockSpec((None, S, CONV_TC), lambda b_, j: (b_, 0, OFF["xbc"] // CONV_TC + j))
    tok = pl.BlockSpec((None, S, CONV_TC), lambda b_, j: (b_, 0, j))

    def body(x_ref, w_ref, b_ref, o_ref, ext):
        ext[0:8, :] = jnp.zeros((8, CONV_TC), F32)
        ext[8:, :] = x_ref[...]
        for r0 in range(0, S, CONV_RC):
            pre = _conv_pre(_conv_taps(ext, r0), w_ref, b_ref)
            o_ref[pl.ds(r0, CONV_RC), :] = pre * _sigmoid(pre)

    return pl.pallas_call(
        body, name=name, grid=(B, nj),
        in_specs=[x_spec, pl.BlockSpec((8, CONV_TC), lambda b_, j: (0, j)), pl.BlockSpec((1, CONV_TC), lambda b_, j: (0, j))],
        out_specs=tok, out_shape=jax.ShapeDtypeStruct((B, S, CONV_CH), F32),
        scratch_shapes=[pltpu.VMEM((S + 8, CONV_TC), F32)])(P, w8, b)


def _conv_bwd(P, dact, w8, b, name):
    B, S, _ = P.shape
    nj = CONV_CH // CONV_TC
    x_spec = pl.BlockSpec((None, S, CONV_TC), lambda j, b_: (b_, 0, OFF["xbc"] // CONV_TC + j))
    tok = pl.BlockSpec((None, S, CONV_TC), lambda j, b_: (b_, 0, j))
    w_spec = pl.BlockSpec((8, CONV_TC), lambda j, b_: (0, j))
    b_spec = pl.BlockSpec((1, CONV_TC), lambda j, b_: (0, j))

    def body(x_ref, d_ref, w_ref, b_ref, dx_ref, dw_ref, db_ref, ext, extd):
        @pl.when(pl.program_id(1) == 0)
        def _():
            dw_ref[...] = jnp.zeros((8, CONV_TC), F32)
            db_ref[...] = jnp.zeros((1, CONV_TC), F32)

        ext[0:8, :] = jnp.zeros((8, CONV_TC), F32)
        ext[8:, :] = x_ref[...]
        extd[pl.ds(8 + S, 8), :] = jnp.zeros((8, CONV_TC), F32)
        db = jnp.zeros((1, CONV_TC), F32)
        dws = [jnp.zeros((1, CONV_TC), F32)] * 4
        for r0 in range(0, S, CONV_RC):
            taps = _conv_taps(ext, r0)
            pre = _conv_pre(taps, w_ref, b_ref)
            sg = _sigmoid(pre)
            dpre = d_ref[pl.ds(r0, CONV_RC), :] * (sg * (1.0 + pre * (1.0 - sg)))
            extd[pl.ds(8 + r0, CONV_RC), :] = dpre
            db = db + jnp.sum(dpre, axis=0, keepdims=True)
            dws = [dws[i] + jnp.sum(dpre * taps[3 - i], axis=0, keepdims=True) for i in range(4)]
        for r0 in range(0, S, CONV_RC):
            dx = w_ref[3:4, :] * extd[pl.ds(8 + r0, CONV_RC), :]
            for k in range(1, 4):
                dx = dx + w_ref[3 - k:4 - k, :] * extd[pl.ds(8 + r0 + k, CONV_RC), :]
            dx_ref[pl.ds(r0, CONV_RC), :] = dx.astype(BF16)
        db_ref[...] += db
        sub = lax.broadcasted_iota(jnp.int32, (8, CONV_TC), 0)
        dw_ref[...] += sum(jnp.where(sub == i, dws[i], 0.0) for i in range(4))

    return pl.pallas_call(
        body, name=name, grid=(nj, B), in_specs=[x_spec, tok, w_spec, b_spec], out_specs=[tok, w_spec, b_spec],
        out_shape=[jax.ShapeDtypeStruct((B, S, CONV_CH), BF16), jax.ShapeDtypeStruct((8, CONV_CH), F32),
                   jax.ShapeDtypeStruct((1, CONV_CH), F32)],
        scratch_shapes=[pltpu.VMEM((S + 8, CONV_TC), F32), pltpu.VMEM((S + 16, CONV_TC), F32)],
        compiler_params=pltpu.CompilerParams(dimension_semantics=("arbitrary", "arbitrary")),
    )(P, dact, w8, b)


def _ssd_consts():
    hd = np.arange(SSM_W) // SSM_HD
    E = (np.arange(128)[:, None] == hd[None, :]).astype(np.float32)
    tri = (np.arange(128)[:, None] >= np.arange(128)[None, :]).astype(np.float32)
    return jnp.asarray(E, BF16), jnp.asarray(E.T, BF16), jnp.asarray(tri, BF16), jnp.asarray(tri.T, BF16)


def _pieces(x, n):
    out, r = [], x
    for _ in range(n):
        p = r.astype(BF16)
        out.append(p)
        r = r - p.astype(F32)
    return out


def _dot01(x, m01, n):
    return sum(_dot(p, m01) for p in _pieces(x, n))


def _dot01_left(m01, x, n):
    return sum(_dot(m01, p) for p in _pieces(x, n))


def _ssd_pre(xa, dtraw, bias, alog, E, tri):
    lane = lax.broadcasted_iota(jnp.int32, (128, 128), 1)
    pre = dtraw + bias
    dtp = jnp.where(lane < SSM_H, jnp.maximum(pre, 0.0) + jnp.log(1.0 + jnp.exp(-jnp.abs(pre))), 0.0)
    a = -jnp.exp(alog)
    acs = _dot01_left(tri, dtp * a, 3)
    acsT = acs.T
    dtE, acsE = _dot01(dtp, E, 2), _dot01(acs, E, 3)
    X = xa[:, :SSM_W]
    xdt = X * dtE
    wE = jnp.exp(acsE[127:128, :] - acsE)
    eE = jnp.exp(acsE)
    cdE = eE[127:128, :]
    return dict(pre=pre, dtp=dtp, a=a, acs=acs, acsT=acsT, dtE=dtE, acsE=acsE, cdE=cdE, X=X, xdt=xdt, wE=wE, eE=eE)


def _ssd_decay(c, h):
    lm = lax.broadcasted_iota(jnp.int32, (128, 128), 0) >= lax.broadcasted_iota(jnp.int32, (128, 128), 1)
    return jnp.exp(jnp.where(lm, c["acs"][:, h:h + 1] - c["acsT"][h:h + 1, :], NEG_INF))


def _ssd_pair_operands(c, CB, h0):
    lane = lax.broadcasted_iota(jnp.int32, (128, 128), 1)
    L0, L1 = _ssd_decay(c, h0), _ssd_decay(c, h0 + 1)
    M = jnp.concatenate([CB * L0, CB * L1], axis=1).astype(BF16)
    xp = c["xdt"][:, h0 * 64:h0 * 64 + 128]
    BD = jnp.concatenate([jnp.where(lane < 64, xp, 0.0), jnp.where(lane >= 64, xp, 0.0)], axis=0).astype(BF16)
    return L0, L1, M, BD


def _ssd_y(c, xa, state_ref, dskipE):
    per_group, ys = [], []
    for g in range(SSM_G):
        gs = slice(g * 512, (g + 1) * 512)
        Bb = xa[:, SSM_W + g * 128:SSM_W + (g + 1) * 128].astype(BF16)
        Cb = xa[:, SSM_W + 256 + g * 128:SSM_W + 256 + (g + 1) * 128].astype(BF16)
        CB = _dot_nt(Cb, Bb)
        Sg = state_ref[:, gs]
        yoff = _dot(Cb, Sg.astype(BF16)) * c["eE"][:, gs]
        ydiag, pairs = [], []
        for j in range(4):
            ops = _ssd_pair_operands(c, CB, g * 8 + 2 * j)
            pairs.append(ops)
            ydiag.append(_dot(ops[2], ops[3]))
        ys.append(jnp.concatenate(ydiag, axis=1) + yoff)
        per_group.append(dict(Bb=Bb, Cb=Cb, CB=CB, Sg=Sg, yoff=yoff, pairs=pairs))
    Y = jnp.concatenate(ys, axis=1) + c["X"] * dskipE
    return Y, per_group


def _ssd_specs(S, rev):
    nc = S // CHUNK
    cm = (lambda b, i: (b, nc - 1 - i)) if rev else (lambda b, i: (b, i))
    xa = pl.BlockSpec((None, CHUNK, CONV_CH), lambda b, i: cm(b, i) + (0,))
    z = [pl.BlockSpec((None, CHUNK, 256), lambda b, i, q=q: cm(b, i) + (OFF["z"] // 256 + q,)) for q in range(4)]
    dt = pl.BlockSpec((None, CHUNK, 128), lambda b, i: cm(b, i) + (OFF["dt"] // 128,))
    tok = pl.BlockSpec((None, CHUNK, SSM_W), lambda b, i: cm(b, i) + (0,))
    st = pl.BlockSpec((None, None, 128, SSM_W), lambda b, i: cm(b, i) + (0, 0))
    return nc, xa, z, dt, tok, st


def _ssd_fwd(xact, P, bias, alog, dskipE, ng, name):
    B, S, _ = P.shape
    nc, xa_spec, z_specs, dt_spec, tok, st_spec = _ssd_specs(S, False)
    E, _, tri, _ = _ssd_consts()

    def body(xa_ref, z0, z1, z2, z3, dt_ref, bias_ref, alog_ref, dsk_ref, ng_ref, E_ref, tri_ref, o_ref, sp_ref, state):
        @pl.when(pl.program_id(1) == 0)
        def _():
            state[...] = jnp.zeros((128, SSM_W), F32)

        sp_ref[...] = state[...]
        xa = xa_ref[...]
        c = _ssd_pre(xa, dt_ref[...], bias_ref[...], alog_ref[...], E_ref[...], tri_ref[...])
        Y, groups = _ssd_y(c, xa, state, dsk_ref[...])
        Z = (c["xdt"] * c["wE"]).astype(BF16)
        for g in range(SSM_G):
            gs = slice(g * 512, (g + 1) * 512)
            state[:, gs] = groups[g]["Sg"] * c["cdE"][:, gs] + _dot_tn(groups[g]["Bb"], Z[:, gs])
        zv = jnp.concatenate([z0[...], z1[...], z2[...], z3[...]], axis=1)
        yz = Y * (zv * _sigmoid(zv))
        outs = []
        for g in range(SSM_G):
            yg = yz[:, g * 512:(g + 1) * 512]
            outs.append(yg * lax.rsqrt(jnp.mean(yg * yg, axis=-1, keepdims=True) + EPS))
        o_ref[...] = (jnp.concatenate(outs, axis=1) * ng_ref[...]).astype(BF16)

    return pl.pallas_call(
        body, name=name, grid=(B, nc),
        in_specs=[xa_spec] + z_specs + [dt_spec, _full((1, 128)), _full((1, 128)), _full((1, SSM_W)), _full((1, SSM_W)),
                                        _full((128, SSM_W)), _full((128, 128))],
        out_specs=[tok, st_spec],
        out_shape=[jax.ShapeDtypeStruct((B, S, SSM_W), BF16), jax.ShapeDtypeStruct((B, nc, 128, SSM_W), F32)],
        scratch_shapes=[pltpu.VMEM((128, SSM_W), F32)],
        compiler_params=pltpu.CompilerParams(dimension_semantics=("arbitrary", "arbitrary")),
    )(xact, P, P, P, P, P, bias, alog, dskipE, ng, E, tri)


def _ssd_bwd(xact, P, sprev, dcat, bias, alog, dskipE, ng, name):
    B, S, _ = P.shape
    nc, xa_spec, z_specs, dt_spec, tok, st_spec = _ssd_specs(S, True)
    do_spec = pl.BlockSpec((None, CHUNK, SSM_W), lambda b, i: (b, nc - 1 - i, 1))
    E, ET, tri, triT = _ssd_consts()
    dt_out = pl.BlockSpec((None, CHUNK, 128), lambda b, i: (b, nc - 1 - i, 0))

    def body(xa_ref, z0, z1, z2, z3, dt_ref, sp_ref, do_ref, bias_ref, alog_ref, dsk_ref, ng_ref, E_ref, ET_ref, tri_ref,
             triT_ref, dxa_ref, dz_ref, ddt_ref, dbias_ref, dalog_ref, ddsk_ref, dng_ref, dstate):
        first = (pl.program_id(0) == 0) & (pl.program_id(1) == 0)

        @pl.when(first)
        def _():
            for ref in (dbias_ref, dalog_ref, ddsk_ref, dng_ref):
                ref[...] = jnp.zeros(ref.shape, F32)

        @pl.when(pl.program_id(1) == 0)
        def _():
            dstate[...] = jnp.zeros((128, SSM_W), F32)

        xa, ETm = xa_ref[...], ET_ref[...]
        c = _ssd_pre(xa, dt_ref[...], bias_ref[...], alog_ref[...], E_ref[...], tri_ref[...])
        Y, groups = _ssd_y(c, xa, sp_ref, dsk_ref[...])
        X, xdt = c["X"], c["xdt"]
        zv = jnp.concatenate([z0[...], z1[...], z2[...], z3[...]], axis=1)
        sg = _sigmoid(zv)
        zs = zv * sg
        yz = Y * zs
        dout = do_ref[...]
        dyz = []
        for g in range(SSM_G):
            gs = slice(g * 512, (g + 1) * 512)
            yg = yz[:, gs]
            r = lax.rsqrt(jnp.mean(yg * yg, axis=-1, keepdims=True) + EPS)
            yn = yg * r
            dng_ref[:, gs] += jnp.sum(dout[:, gs] * yn, axis=0, keepdims=True)
            dyn = dout[:, gs] * ng_ref[:, gs]
            dyz.append(r * (dyn - yn * jnp.mean(dyn * yn, axis=-1, keepdims=True)))
        dyz = jnp.concatenate(dyz, axis=1)
        dz_ref[...] = (dyz * Y * (sg * (1.0 + zv * (1.0 - sg)))).astype(BF16)
        dY = dyz * zs
        ddsk_ref[...] += jnp.sum(dY * X, axis=0, keepdims=True)
        dX = dY * dsk_ref[...]
        lane = lax.broadcasted_iota(jnp.int32, (128, 128), 1)
        sub = lax.broadcasted_iota(jnp.int32, (128, 128), 0)
        colform = jnp.zeros((128, 128), F32)
        rowform = jnp.zeros((128, 128), F32)
        dxdt, gacsE, dBC = [], [], []
        for g in range(SSM_G):
            gs = slice(g * 512, (g + 1) * 512)
            G = groups[g]
            Bb, Cb, CB, Sg = G["Bb"], G["Cb"], G["CB"], G["Sg"]
            dYg = dY[:, gs]
            dQ = (dYg * c["eE"][:, gs]).astype(BF16)
            dSn = dstate[:, gs]
            dSnb = dSn.astype(BF16)
            cd = c["cdE"][:, gs]
            dC = _dot_nt(dQ, Sg.astype(BF16))
            dSprev = _dot_tn(Cb, dQ) + dSn * cd
            t1 = jnp.broadcast_to(jnp.sum(dSn * Sg * cd, axis=0, keepdims=True), (8, 512))
            colform = colform + jnp.where(sub == 127, _dot01(t1, ETm[gs, :], 2)[0:1, :], 0.0)
            Zg = xdt[:, gs] * c["wE"][:, gs]
            dZ = _dot(Bb, dSnb)
            dB = _dot_nt(Zg.astype(BF16), dSnb)
            U = dZ * Zg
            ga = dYg * G["yoff"] - U
            ga = ga + jnp.where(lax.broadcasted_iota(jnp.int32, (128, 512), 0) == 127, jnp.sum(U, axis=0, keepdims=True), 0.0)
            gacsE.append(ga)
            dxg = [None] * 4
            dCB = jnp.zeros((128, 128), F32)
            for j in range(4):
                h0 = g * 8 + 2 * j
                L0, L1, M, BD = G["pairs"][j]
                dYp = dYg[:, j * 128:(j + 1) * 128].astype(BF16)
                dM = _dot_nt(dYp, BD)
                dBD = _dot_tn(M, dYp)
                dxg[j] = jnp.where(lane < 64, dBD[:128], dBD[128:])
                for t, (h, L) in enumerate(((h0, L0), (h0 + 1, L1))):
                    dMh = dM[:, t * 128:(t + 1) * 128]
                    dCB = dCB + dMh * L
                    Gh = dMh * CB * L
                    colform = colform + jnp.where(lane == h, jnp.sum(Gh, axis=1, keepdims=True), 0.0)
                    rowform = rowform - jnp.where(sub == h, jnp.sum(Gh, axis=0, keepdims=True), 0.0)
            dCBb = dCB.astype(BF16)
            dC = dC + _dot(dCBb, Bb)
            dB = dB + _dot_tn(dCBb, Cb)
            dxdt.append(jnp.concatenate(dxg, axis=1) + dZ * c["wE"][:, gs])
            dBC.append((dB, dC))
            dstate[:, gs] = dSprev
        dxdt = jnp.concatenate(dxdt, axis=1)
        dX = dX + dxdt * c["dtE"]
        ddt = _dot01(dxdt * X, ETm, 2)
        dacs = colform + rowform.T + _dot01(jnp.concatenate(gacsE, axis=1), ETm, 2)
        dda = _dot01_left(triT_ref[...], dacs, 2)
        ddt = ddt + dda * c["a"]
        dalog_ref[...] += jnp.sum(dda * c["dtp"], axis=0, keepdims=True) * c["a"]
        ddtraw = jnp.where(lane < SSM_H, ddt * _sigmoid(c["pre"]), 0.0)
        dbias_ref[...] += jnp.sum(ddtraw, axis=0, keepdims=True)
        ddt_ref[...] = ddtraw.astype(BF16)
        dxa_ref[...] = jnp.concatenate([dX, dBC[0][0], dBC[1][0], dBC[0][1], dBC[1][1]], axis=1)

    p128, p1k = _full((1, 128)), _full((1, SSM_W))
    return pl.pallas_call(
        body, name=name, grid=(B, nc),
        in_specs=[xa_spec] + z_specs + [dt_spec, st_spec, do_spec, p128, p128, p1k, p1k,
                                        _full((128, SSM_W)), _full((SSM_W, 128)), _full((128, 128)), _full((128, 128))],
        out_specs=[xa_spec, tok, dt_out, p128, p128, p1k, p1k],
        out_shape=[jax.ShapeDtypeStruct((B, S, CONV_CH), F32), jax.ShapeDtypeStruct((B, S, SSM_W), BF16),
                   jax.ShapeDtypeStruct((B, S, 128), BF16), jax.ShapeDtypeStruct((1, 128), F32),
                   jax.ShapeDtypeStruct((1, 128), F32), jax.ShapeDtypeStruct((1, SSM_W), F32),
                   jax.ShapeDtypeStruct((1, SSM_W), F32)],
        scratch_shapes=[pltpu.VMEM((128, SSM_W), F32)],
        compiler_params=pltpu.CompilerParams(dimension_semantics=("arbitrary", "arbitrary")),
    )(xact, P, P, P, P, P, sprev, dcat, bias, alog, dskipE, ng, E, ET, tri, triT)


def _adamw(w, parts, m, v, name, tr=512, row0=0, prev=None):
    Rtot, C = w.shape
    ns, R = parts.shape[0], parts.shape[1]
    tr = min(tr, R)
    assert R % tr == 0 and row0 % tr == 0
    off = row0 // tr
    c1 = 1.0 / (1.0 - ADAM_B1 ** ADAM_STEP)
    c2 = 1.0 / (1.0 - ADAM_B2 ** ADAM_STEP)

    def body(w_ref, p_ref, m_ref, v_ref, *rest):
        g_ref, d_ref, mo_ref, vo_ref = rest[-4:]
        g = p_ref[0].astype(F32)
        for s in range(1, ns):
            g = g + p_ref[s].astype(F32)
        mn = ADAM_B1 * m_ref[...] + (1.0 - ADAM_B1) * g
        vn = ADAM_B2 * v_ref[...] + (1.0 - ADAM_B2) * (g * g)
        g_ref[...] = g
        mo_ref[...] = mn
        vo_ref[...] = vn
        d_ref[...] = -ADAM_LR * ((mn * c1) / (jnp.sqrt(vn * c2) + ADAM_EPS) + ADAM_WD * w_ref[...])

    blk = pl.BlockSpec((tr, C), lambda i: (i + off, 0))
    extra = [] if prev is None else list(prev)
    return pl.pallas_call(
        body, name=name, grid=(R // tr,),
        in_specs=[blk, pl.BlockSpec((ns, tr, C), lambda i: (0, i, 0)), blk, blk] + [pl.BlockSpec(memory_space=pl.ANY)] * len(extra),
        out_specs=[blk] * 4, out_shape=[jax.ShapeDtypeStruct((Rtot, C), F32)] * 4,
        input_output_aliases={4 + k: k for k in range(len(extra))})(w, parts, m, v, *extra)


_SMALL = ("ada_b", "norm1_g", "gm_ln_g", "gm_ln_b", "gm_ws", "gm_bs", "gm_norm_g", "attn_sinks", "attn_norm_g", "conv_b",
          "dt_bias", "a_log", "d_skip", "ssm_norm_g", "norm2_g", "final_norm_g")


def _pack(arrs):
    flat = []
    for a in arrs:
        f = a.reshape(-1).astype(F32)
        flat.append(jnp.pad(f, (0, (-f.shape[0]) % 1024)))
    return jnp.concatenate(flat).reshape(-1, 128)


def _unpack(pack, like):
    out, o = [], 0
    flat = pack.reshape(-1)
    for a in like:
        n = int(np.prod(a.shape))
        out.append(flat[o:o + n].reshape(a.shape))
        o += n + (-n) % 1024
    return out


def kernel(x, c, ada_w, ada_b, norm1_g, w_in, gm_ln_g, gm_ln_b, gm_ws, gm_bs, gm_norm_g, attn_sinks, attn_norm_g, conv_w, conv_b, dt_bias, a_log, d_skip, ssm_norm_g, w_out, norm2_g, w_mlp1, w_mlp2, final_norm_g, loss_target, m_ada_w, m_ada_b, m_norm1_g, m_w_in, m_gm_ln_g, m_gm_ln_b, m_gm_ws, m_gm_bs, m_gm_norm_g, m_attn_sinks, m_attn_norm_g, m_conv_w, m_conv_b, m_dt_bias, m_a_log, m_d_skip, m_ssm_norm_g, m_w_out, m_norm2_g, m_w_mlp1, m_w_mlp2, m_final_norm_g, v_ada_w, v_ada_b, v_norm1_g, v_w_in, v_gm_ln_g, v_gm_ln_b, v_gm_ws, v_gm_bs, v_gm_norm_g, v_attn_sinks, v_attn_norm_g, v_conv_w, v_conv_b, v_dt_bias, v_a_log, v_d_skip, v_ssm_norm_g, v_w_out, v_norm2_g, v_w_mlp1, v_w_mlp2, v_final_norm_g):
    args = dict(locals())
    B, S, _ = x.shape
    T = B * S
    L = DEPTH
    me = 4 * lax.axis_index("x") + 2 * lax.axis_index("y") + lax.axis_index("c")

    gath = _gather2([c, conv_w], "ag_c")
    big = ("w_in", "w_out", "w_mlp1", "w_mlp2")
    chain = [(n, l) for l in range(L) for n in ("w_in", "w_mlp1", "w_out", "w_mlp2")]
    inflight = {}

    def start_next(order):
        if not chain:
            return jnp.zeros((8, 128), F32)
        n, l = chain.pop(0)
        sems, v_thru, land_thru, token = _gather_start(shard[n, l], zone[n, l], order, f"ag_start_{n}{l}")
        inflight[n, l] = (sems, v_thru, land_thru)
        return token

    def gathered(n, l, after):
        _, land = _gather_wait(*inflight.pop((n, l)), after, f"ag_wait_{n}{l}")
        return _gather_finish(land, f"ag_fin_{n}{l}")

    me1 = me.astype(jnp.int32).reshape(1)
    shard = {(n, l): args[n][l].astype(BF16) for n, l in chain}
    zone = {k: _landing_zone(v, me1, f"ag_zone_{k[0]}{k[1]}") for k, v in shard.items()}

    c_all = gath[0].reshape(NDEV * B, D)
    c_act = (c_all * jax.nn.sigmoid(c_all)).astype(BF16)
    nb_rows = c_act.shape[0]
    c_pad = jnp.pad(c_act, ((0, 128 - nb_rows), (0, 0)))
    adw = ada_w.astype(BF16)
    mod_part = jnp.stack([_mm(c_pad, adw[l], mode="nn", name=f"mod{l}", tn=768)[:nb_rows] for l in range(L)])
    mod_all = _gather_small([mod_part], "ag_mod")[0]
    mod_mine = lax.dynamic_slice_in_dim(mod_all, me * B, B, axis=2)
    mod = jnp.transpose(mod_mine, (1, 2, 0, 3)).reshape(L, B, 6 * D) + ada_b[:, None, :]
    mods = [[mod[l][:, None, i * D:(i + 1) * D] for i in range(6)] for l in range(L)]

    win_g, wout_g, w1_g, w2_g = [None] * L, [None] * L, [None] * L, [None] * L

    tril = jnp.tril(jnp.ones((128, 128), F32))
    row = lambda a: a.reshape(1, -1)
    pad128 = lambda a: jnp.pad(a.reshape(1, -1), ((0, 0), (0, 128 - a.shape[-1])))
    small = []
    for l in range(L):
        wt = gm_ws[l] * tril
        small.append(dict(
            lng=row(gm_ln_g[l]), lnb=row(gm_ln_b[l]), wt=wt.astype(BF16), wtT=jnp.swapaxes(wt, 1, 2).astype(BF16),
            bsx=jnp.repeat(gm_bs[l].T, 128, axis=1), gog=row(gm_norm_g[l]), sinks=attn_sinks[l], aog=row(attn_norm_g[l]),
            bias=pad128(dt_bias[l]), alog=pad128(a_log[l]), dskE=jnp.repeat(d_skip[l], SSM_HD).reshape(1, SSM_W),
            sng=row(ssm_norm_g[l]), cb=row(conv_b[l])))
    convw_all = jnp.transpose(gath[1], (1, 2, 0, 3)).reshape(L, 4, CONV_CH)
    convw8 = jnp.pad(convw_all, ((0, 0), (0, 4), (0, 0)))

    saved = []
    xl = x
    tok = start_next(mod)
    h = _norm_fwd(xl, row(norm1_g[0]) + tok[0, 0], mods[0][1], mods[0][0], "norm1_f0")
    for l in range(L):
        sm = small[l]
        g_in = gathered("w_in", l, h)
        tok = start_next(g_in)
        win_g[l] = _to_work_cols(jnp.transpose(g_in, (1, 0, 2)).reshape(D, IN_W))
        P = _mm(h.reshape(T, D), win_g[l], mode="nn", name=f"proj_in{l}", tn=1536, order=tok).reshape(B, S, PW)
        out_a = _gmlp_fwd(P, sm["lng"], sm["lnb"], sm["wt"], sm["bsx"], sm["gog"], f"gmlp_f{l}")
        out_b = _attn_fwd(P, sm["sinks"], sm["aog"], f"attn_f{l}")
        xact = _conv_fwd(P, convw8[l], sm["cb"], f"conv_f{l}")
        w1_g[l] = gathered("w_mlp1", l, xact)
        tok = start_next(w1_g[l])
        out_c, sprev = _ssd_fwd(xact, P, sm["bias"], sm["alog"], sm["dskE"], sm["sng"] + tok[0:1, 0:1], f"ssd_f{l}")
        cat = jnp.concatenate([out_a, out_b, out_c], axis=-1)
        g_out = gathered("w_out", l, cat)
        tok = start_next(g_out)
        wout_g[l] = g_out.reshape(D, D)
        mix = _mm(cat.reshape(T, D), wout_g[l], mode="nn", name=f"proj_out{l}", order=tok).reshape(B, S, D)
        x_mid, h2 = _norm_fwd(xl, row(norm2_g[l]), mods[l][4], mods[l][3], f"norm2_f{l}", resid=(mix, mods[l][2]))
        a_act, r_act = _mm(h2.reshape(T, D), w1_g[l], mode="nn", name=f"mlp1_{l}", out_dtypes=(BF16, BF16), col_blocked_b=True,
                           epilogue=lambda acc: (acc, jnp.square(jnp.maximum(acc, 0.0))))
        g_2 = gathered("w_mlp2", l, r_act)
        tok = start_next(g_2)
        w2_g[l] = g_2.reshape(DFF, D)
        m2 = _mm(r_act, w2_g[l], mode="nn", name=f"mlp2_{l}", order=tok, tk=4096).reshape(B, S, D)
        saved.append(dict(x_in=xl, h=h, P=P, xact=xact, sprev=sprev, cat=cat, mix=mix, x_mid=x_mid, h2=h2, a=a_act, r=r_act, m2=m2))
        if l + 1 < L:
            xl, h = _norm_fwd(x_mid, row(norm1_g[l + 1]), mods[l + 1][1], mods[l + 1][0], f"norm1_f{l + 1}", resid=(m2, mods[l][5]))

    sv = saved[L - 1]
    nb = _norm_bwd(sv["x_mid"], row(final_norm_g), "final_b", tgt=loss_target, br=sv["m2"], gate=mods[L - 1][5], x_is_prev=True)
    loss_part, g_final = nb["loss"], nb["dg"]
    dmod, gsm, gconvw = [None] * L, [None] * L, [None] * L
    core = lax.axis_index("c").astype(jnp.int32).reshape(1)
    reducing = []

    def reduce_start(n, l, p, order):
        from_sib = _pair_exchange([p], f"rs_pair_{n}{l}")[0]
        s, land = _pair_add(p, from_sib, core, f"rs_add_{n}{l}")
        sems, s_thru, land_thru, token = _chipsum_start(s, land, order, f"rs_start_{n}{l}")
        reducing.append((n, l, sems, s_thru, land_thru))
        return token

    for l in reversed(range(L)):
        sv, sm = saved[l], small[l]
        dm2, dxo, dg2 = nb["dbr"].reshape(T, D), nb["dx"], nb["dgate"]
        da = _mm(dm2, w2_g[l], mode="nt", name=f"mlp2_dx{l}", out_dtypes=(BF16,), extras=(sv["a"],),
                 epilogue=lambda acc, a: (acc * (2.0 * jnp.maximum(a.astype(F32), 0.0)),))
        dw2 = _mm(sv["r"], dm2, mode="tn", name=f"mlp2_dw{l}", out_dtypes=(BF16,), tk=2048).reshape(4, 2, DFF // NDEV, D)
        tok = reduce_start("w_mlp2", l, dw2, da)
        dh2 = _mm(da, w1_g[l], mode="nt", name=f"mlp1_dx{l}", col_blocked_b=True, order=tok).reshape(B, S, D)
        dw1 = _mm(sv["h2"].reshape(T, D), da, mode="tn", name=f"mlp1_dw{l}", out_dtypes=(BF16,), tk=2048,
                  col_blocked_out=True).reshape(4, 2, D, DFF // NDEV)
        tok = reduce_start("w_mlp1", l, dw1, dh2)
        nb2 = _norm_bwd(sv["x_mid"], row(norm2_g[l]) + tok[0, 0], f"norm2_b{l}", sc=mods[l][4], dh=dh2, dres=dxo, br=sv["mix"],
                        gate=mods[l][2])
        dmix = nb2["dbr"].reshape(T, D)
        dcat = _mm(dmix, wout_g[l], mode="nt", name=f"proj_out_dx{l}").reshape(B, S, D)
        du, dv, dlng, dlnb, dws, dbsx, dgog = _gmlp_bwd(sv["P"], dcat, sm["lng"], sm["lnb"], sm["wt"], sm["wtT"], sm["bsx"],
                                                        sm["gog"], f"gmlp_b{l}")
        dq, dk, dvv, dsink, daog = _attn_bwd(sv["P"], dcat, sm["sinks"], sm["aog"], f"attn_b{l}")
        dwo = _mm(sv["cat"].reshape(T, D), dmix, mode="tn", name=f"proj_out_dw{l}", out_dtypes=(BF16,), tk=2048,
                  order=dq).reshape(4, 2, D // NDEV, D)
        tok = reduce_start("w_out", l, dwo, dmix)
        dxa, dz, ddt, dbias, dalog, ddsk, dsng = _ssd_bwd(sv["xact"], sv["P"], sv["sprev"], dcat, sm["bias"], sm["alog"],
                                                          sm["dskE"], sm["sng"] + tok[0:1, 0:1], f"ssd_b{l}")
        dxbc, dcw, dcb = _conv_bwd(sv["P"], dxa, convw8[l], sm["cb"], f"conv_b{l}")
        dP = jnp.concatenate([du, dv, dq, dk, dvv, dz, dxbc, ddt, jnp.zeros((B, S, PW - OFF["dt"] - 128), BF16)],
                             axis=-1).reshape(T, PW)
        dwin = _mm(sv["h"].reshape(T, D), dP, mode="tn", name=f"proj_in_dw{l}", out_dtypes=(BF16,), tn=1536, tk=2048)
        dwin = jnp.transpose(_from_work_cols(dwin).reshape(D, NDEV, IN_W // NDEV), (1, 0, 2)).reshape(4, 2, D, IN_W // NDEV)
        tok = reduce_start("w_in", l, dwin, dP)
        dh = _mm(dP, win_g[l], mode="nt", name=f"proj_in_dx{l}", tk=2304, order=tok).reshape(B, S, D)
        nb = _norm_bwd(sv["x_in"], row(norm1_g[l]), f"norm1_b{l}", sc=mods[l][1], dh=dh, dres=nb2["dx"],
                       br=saved[l - 1]["m2"] if l > 0 else None, gate=mods[l - 1][5] if l > 0 else None)
        dmod[l] = jnp.concatenate([nb["dsh"], nb["dsc"], nb2["dgate"], nb2["dsh"], nb2["dsc"], dg2], axis=-1)
        gconvw[l] = dcw[:4]
        gsm[l] = dict(
            ada_b=jnp.sum(dmod[l], axis=(0, 1)), norm1_g=nb["dg"], gm_ln_g=dlng, gm_ln_b=dlnb, gm_ws=dws,
            gm_bs=dbsx.reshape(128, GM_H, 128).sum(-1).T, gm_norm_g=dgog, attn_sinks=dsink[:, 0], attn_norm_g=daog,
            conv_b=dcb, dt_bias=dbias[0, :SSM_H], a_log=dalog[0, :SSM_H], d_skip=ddsk.reshape(SSM_H, SSM_HD).sum(-1),
            ssm_norm_g=dsng, norm2_g=nb2["dg"])
    grad_x = nb["dx"]

    big_res, after = dict.fromkeys(big), grad_x
    tile_rows = dict(w_in=256, w_out=256, w_mlp1=256, w_mlp2=128)

    def finish_reduce(n, l, sems, s_thru, land_thru, after):
        parts = _chipsum_wait(sems, s_thru, land_thru, after, f"rs_wait_{n}{l}")
        w = args[n]
        big_res[n] = _adamw(w.reshape(-1, w.shape[-1]), parts, args["m_" + n].reshape(-1, w.shape[-1]),
                            args["v_" + n].reshape(-1, w.shape[-1]), f"adamw_{n}{l}", tr=tile_rows[n], row0=l * w.shape[1],
                            prev=big_res[n])
        return big_res[n][0]

    for item in reducing[:-1]:
        after = finish_reduce(*item, after)

    per_layer = [n for n in _SMALL if n != "final_norm_g"]
    g_small = [jnp.stack([gsm[l][n].reshape(args[n].shape[1:]) for l in range(L)]) for n in per_layer] + [g_final.reshape(D)]
    zc = jnp.zeros((L, 4, CONV_CH), F32)
    z1 = jnp.zeros((1, 128), F32)
    gpack = _pack([loss_part] + g_small + [jnp.stack(gconvw)])
    got = _gather_small([jnp.stack(dmod).reshape(L, B, 6 * D), gpack], "ag_small", order=after)
    like = [z1] + [args[n] for n in _SMALL] + [zc]
    packs = [_pack([z1] + [args[p + n] for n in _SMALL] + [zc]) for p in ("", "m_", "v_")]
    sres = [_unpack(p, like) for p in _adamw(packs[0], got[1], packs[1], packs[2], "adamw_small", tr=gpack.shape[0])]
    res = {n: [r[1 + i] for r in sres] for i, n in enumerate(_SMALL)}
    loss = sres[0][0][0, 0]
    gcw = lax.dynamic_slice_in_dim(sres[0][-1], me * (CONV_CH // NDEV), CONV_CH // NDEV, axis=2)

    def update(name, parts, tr):
        w = args[name]
        r = _adamw(w.reshape(-1, w.shape[-1]), parts, args["m_" + name].reshape(-1, w.shape[-1]),
                   args["v_" + name].reshape(-1, w.shape[-1]), "adamw_" + name, tr=tr)
        res[name] = [a.reshape(w.shape) for a in r]

    update("conv_w", gcw.reshape(1, L * 4, CONV_CH // NDEV), L * 4)

    dmod_all = jnp.transpose(got[0], (1, 0, 2, 3)).reshape(L, NDEV * B, 6 * D)
    dm_mine = lax.dynamic_slice_in_dim(dmod_all, me * (6 * D // NDEV), 6 * D // NDEV, axis=2)
    dm_pad = jnp.pad(dm_mine, ((0, 0), (0, 128 - nb_rows), (0, 0))).astype(BF16)
    g_adaw = jnp.stack([_mm(c_pad, dm_pad[l], mode="tn", name=f"ada_dw{l}", tn=768) for l in range(L)])
    update("ada_w", g_adaw.reshape(1, L * D, 6 * D // NDEV), 256)

    finish_reduce(*reducing[-1], res["ada_w"][0])
    for n in big:
        res[n] = [a.reshape(args[n].shape) for a in big_res[n]]

    names = ['ada_w', 'ada_b', 'norm1_g', 'w_in', 'gm_ln_g', 'gm_ln_b', 'gm_ws', 'gm_bs', 'gm_norm_g', 'attn_sinks',
             'attn_norm_g', 'conv_w', 'conv_b', 'dt_bias', 'a_log', 'd_skip', 'ssm_norm_g', 'w_out', 'norm2_g', 'w_mlp1',
             'w_mlp2', 'final_norm_g']
    return (loss, grad_x, *[res[n][0] for n in names], *[res[n][1] for n in names], *[res[n][2] for n in names],
            *[res[n][3] for n in names])
```

```python
import functools

import jax
import jax.numpy as jnp
import numpy as np
from jax import lax
from jax.experimental import pallas as pl
from jax.experimental.pallas import tpu as pltpu

F32, BF16 = jnp.float32, jnp.bfloat16
HI = lax.Precision.HIGHEST
MESH = pl.DeviceIdType.MESH
NDEV = 8

D = 2048
DEPTH = 2
CHUNK = 128
GM_W, GM_H = 512, 4
ATT_W, KV_W, ATT_H = 512, 128, 8
SSM_W, SSM_H, SSM_HD, SSM_G = 1024, 16, 64, 2
CONV_CH = 1536
IN_W = 4368
DFF = 8192
EPS = 1e-6
NEG_INF = -1e30
GELU_K = 0.7978845608028654
GELU_C = 0.044715

_ORIG = (("u", 512), ("v", 512), ("q", 512), ("k", 128), ("vv", 128), ("z", 1024), ("xbc", 1536), ("dt", 16))
OFF = dict(u=0, v=512, q=1024, k=1536, vv=1664, z=1792, xbc=2816, dt=4352)
PW = 4608

ADAM_LR, ADAM_B1, ADAM_B2, ADAM_EPS, ADAM_WD, ADAM_STEP = 0.001, 0.9, 0.999, 1e-08, 0.01, 10


def _to_work_cols(w):
    return jnp.pad(w, [(0, 0)] * (w.ndim - 1) + [(0, PW - IN_W)])


def _from_work_cols(wp):
    return wp[..., :IN_W]


def _sigmoid(x):
    return 1.0 / (1.0 + jnp.exp(-x))


def _gelu(x):
    return 0.5 * x * (1.0 + jnp.tanh(GELU_K * (x + GELU_C * x * x * x)))


def _gelu_grad(x):
    t = jnp.tanh(GELU_K * (x + GELU_C * x * x * x))
    return 0.5 * (1.0 + t) + 0.5 * x * (1.0 - t * t) * GELU_K * (1.0 + 3.0 * GELU_C * x * x)


def _dot(a, b, prec=None):
    return jnp.dot(a, b, precision=prec, preferred_element_type=F32)


def _dot_nt(a, b, prec=None):
    return lax.dot_general(a, b, (((1,), (1,)), ((), ())), precision=prec, preferred_element_type=F32)


def _dot_tn(a, b, prec=None):
    return lax.dot_general(a, b, (((0,), (0,)), ((), ())), precision=prec, preferred_element_type=F32)


def _full(shape):
    return pl.BlockSpec(shape, lambda *_: (0,) * len(shape))


_HBM = pl.BlockSpec(memory_space=pltpu.HBM)


def _me():
    return lax.axis_index("x"), lax.axis_index("y"), lax.axis_index("c")


def _peer(k):
    x, y, c = _me()
    px = 1 - x if k & 4 else x
    py = 1 - y if k & 2 else y
    pc = 1 - c if k & 1 else c
    return (px, py, pc), 4 * px + 2 * py + pc


def _gather_small(xs, name, order=None):
    n = len(xs)

    def body(*refs):
        ins, outs = refs[:n], refs[-n - 3:-3]
        send, recv, loc = refs[-3:]
        x, y, c = _me()
        me = 4 * x + 2 * y + c
        started = []
        for i in range(n):
            own = pltpu.make_async_copy(ins[i], outs[i].at[me], loc.at[i])
            own.start()
            started.append(own)
        for k in range(1, NDEV):
            dev, lin = _peer(k)
            for i in range(n):
                pltpu.make_async_remote_copy(
                    src_ref=ins[i], dst_ref=outs[i].at[me],
                    send_sem=send.at[i, k - 1], recv_sem=recv.at[i, k - 1], device_id=dev, device_id_type=MESH).start()
        for k in range(1, NDEV):
            dev, lin = _peer(k)
            for i in range(n):
                pltpu.make_async_remote_copy(
                    src_ref=ins[i], dst_ref=outs[i].at[lin],
                    send_sem=send.at[i, k - 1], recv_sem=recv.at[i, k - 1], device_id=dev, device_id_type=MESH).wait()
        for own in started:
            own.wait()

    extra = [] if order is None else [order]
    return pl.pallas_call(
        body, name=name, out_shape=[jax.ShapeDtypeStruct((NDEV,) + a.shape, a.dtype) for a in xs],
        in_specs=[_HBM] * n + [pl.BlockSpec(memory_space=pl.ANY)] * len(extra), out_specs=[_HBM] * n,
        scratch_shapes=[pltpu.SemaphoreType.DMA((n, NDEV - 1)), pltpu.SemaphoreType.DMA((n, NDEV - 1)),
                        pltpu.SemaphoreType.DMA((n,))],
        compiler_params=pltpu.CompilerParams(has_side_effects=True),
    )(*xs, *extra)


def _chips():
    x, y, c = _me()
    return x, y, c, [(1 - x, y), (x, 1 - y), (1 - x, 1 - y)]


def _gather2(xs, name, order=None):
    n = len(xs)
    extra = [] if order is None else [order]

    def body(*refs):
        ins, outs = refs[:n], refs[-n - 3:-3]
        send, recv, loc = refs[-3:]
        x, y, c, chips = _chips()
        me, sib = (x, y, c), (x, y, 1 - c)

        def cp(i, k, block, to, src=None):
            slot = outs[i].at[4 * block[0] + 2 * block[1] + block[2]]
            return pltpu.make_async_remote_copy(src_ref=slot if src is None else src, dst_ref=slot, send_sem=send.at[i, k],
                                                recv_sem=recv.at[i, k], device_id=to, device_id_type=MESH)

        sent = []
        for i in range(n):
            for j, chip in enumerate(chips):
                sent.append(cp(i, 1 + j, me, (*chip, c), src=ins[i]))
            sent.append(cp(i, 0, me, sib, src=ins[i]))
        for s in sent:
            s.start()
        own = [pltpu.make_async_copy(ins[i], outs[i].at[4 * x + 2 * y + c], loc.at[i]) for i in range(n)]
        for o in own:
            o.start()
        for j, chip in enumerate(chips):
            for i in range(n):
                cp(i, 1 + j, (*chip, c), me).wait_recv()
                fwd = cp(i, 4 + j, (*chip, c), sib)
                fwd.start()
                sent.append(fwd)
        for i in range(n):
            cp(i, 0, sib, me).wait_recv()
            for j, chip in enumerate(chips):
                cp(i, 4 + j, (*chip, 1 - c), me).wait_recv()
        for s in sent:
            s.wait_send()
        for o in own:
            o.wait()

    return pl.pallas_call(
        body, name=name, out_shape=[jax.ShapeDtypeStruct((NDEV,) + a.shape, a.dtype) for a in xs],
        in_specs=[_HBM] * n + [pl.BlockSpec(memory_space=pl.ANY)] * len(extra), out_specs=[_HBM] * n,
        scratch_shapes=[pltpu.SemaphoreType.DMA((n, 7)), pltpu.SemaphoreType.DMA((n, 7)), pltpu.SemaphoreType.DMA((n,))],
        compiler_params=pltpu.CompilerParams(has_side_effects=True),
    )(*xs, *extra)


def _pair_exchange(ps, name):
    n = len(ps)

    def body(*refs):
        ins, outs = refs[:n], refs[n:2 * n]
        send, recv = refs[2 * n:]
        x, y, c = _me()
        cps = [pltpu.make_async_remote_copy(src_ref=ins[i].at[ch, 1 - c], dst_ref=outs[i].at[ch], send_sem=send.at[i, ch],
                                            recv_sem=recv.at[i, ch], device_id=(x, y, 1 - c), device_id_type=MESH)
               for i in range(n) for ch in range(4)]
        for cp in cps:
            cp.start()
        for cp in cps:
            cp.wait()

    return pl.pallas_call(
        body, name=name, out_shape=[jax.ShapeDtypeStruct((4,) + a.shape[2:], a.dtype) for a in ps],
        in_specs=[_HBM] * n, out_specs=[_HBM] * n,
        scratch_shapes=[pltpu.SemaphoreType.DMA((n, 4)), pltpu.SemaphoreType.DMA((n, 4))],
        compiler_params=pltpu.CompilerParams(has_side_effects=True),
    )(*ps)


def _pair_add(p, r1, core, name, tr=256):
    _, _, R, C = p.shape
    tr = min(tr, R)

    def body(core_ref, p_ref, r_ref, o_ref, o2_ref):
        s = (p_ref[...].astype(F32) + r_ref[...].astype(F32)).astype(o_ref.dtype)
        o_ref[...] = s
        o2_ref[...] = s

    blk = pl.BlockSpec((None, tr, C), lambda ch, i, core_ref: (ch, i, 0))
    return pl.pallas_call(
        body, name=name, out_shape=[jax.ShapeDtypeStruct((4, R, C), p.dtype)] * 2,
        grid_spec=pltpu.PrefetchScalarGridSpec(
            num_scalar_prefetch=1, grid=(4, R // tr),
            in_specs=[pl.BlockSpec((None, None, tr, C), lambda ch, i, core_ref: (ch, core_ref[0], i, 0)), blk],
            out_specs=[blk, blk]),
    )(core, p, r1)


_SEM = pl.BlockSpec(memory_space=pltpu.SEMAPHORE)
_ANY = pl.BlockSpec(memory_space=pl.ANY)
_DATAFLOW = pltpu.SideEffectType.DATAFLOW_SIDE_EFFECTING


def _hbm(a):
    return pltpu.with_memory_space_constraint(a, pltpu.HBM)


def _gather_targets():
    x, y, c, chips = _chips()
    return 4 * x + 2 * y + c, [(x, y, 1 - c)] + [(*chip, c) for chip in chips]


def _landing_zone(v, me, name, tr=512):
    R, C = v.shape
    tr = min(tr, R)

    def body(me_ref, v_ref, o_ref):
        o_ref[...] = v_ref[...]

    return pl.pallas_call(
        body, name=name, out_shape=jax.ShapeDtypeStruct((NDEV, R, C), v.dtype),
        grid_spec=pltpu.PrefetchScalarGridSpec(
            num_scalar_prefetch=1, grid=(R // tr,), in_specs=[pl.BlockSpec((tr, C), lambda i, me_ref: (i, 0))],
            out_specs=pl.BlockSpec((None, tr, C), lambda i, me_ref: (me_ref[0], i, 0))),
    )(me, v)


def _gather_start(v, land, order, name):
    def body(v_ref, land_ref, order_ref, *rest):
        sems, token = rest[:8], rest[10]
        me, targets = _gather_targets()
        for k, to in enumerate(targets):
            pltpu.make_async_remote_copy(src_ref=v_ref, dst_ref=land_ref.at[me], send_sem=sems[k], recv_sem=sems[4 + k],
                                         device_id=to, device_id_type=MESH).start()
        token[...] = jnp.zeros_like(token)

    outs = pl.pallas_call(
        body, name=name,
        out_shape=(pltpu.SemaphoreType.DMA(()),) * 8 + (pltpu.HBM(v.shape, v.dtype), pltpu.HBM(land.shape, land.dtype),
                                                        jax.ShapeDtypeStruct((8, 128), F32)),
        in_specs=(_HBM, _HBM, _ANY), out_specs=(_SEM,) * 8 + (_HBM, _HBM, pl.BlockSpec(memory_space=pltpu.VMEM)),
        input_output_aliases={0: 8, 1: 9}, compiler_params=pltpu.CompilerParams(has_side_effects=_DATAFLOW),
    )(_hbm(v), _hbm(land), order)
    return outs[:8], outs[8], outs[9], outs[10]


def _gather_wait(sems, v_thru, land_thru, after, name):
    def body(v_ref, land_ref, *rest):
        sems_ = rest[:8]
        me, targets = _gather_targets()
        for k, to in enumerate(targets):
            cp = pltpu.make_async_remote_copy(src_ref=v_ref, dst_ref=land_ref.at[me], send_sem=sems_[k], recv_sem=sems_[4 + k],
                                              device_id=to, device_id_type=MESH)
            cp.wait_send()
            cp.wait_recv()

    return pl.pallas_call(
        body, name=name, out_shape=(pltpu.HBM(v_thru.shape, v_thru.dtype), pltpu.HBM(land_thru.shape, land_thru.dtype)),
        in_specs=(_HBM, _HBM) + (_SEM,) * 8 + (_ANY,), out_specs=(_HBM, _HBM), input_output_aliases={0: 0, 1: 1},
        compiler_params=pltpu.CompilerParams(has_side_effects=_DATAFLOW),
    )(v_thru, land_thru, *sems, after)


def _gather_finish(land, name):
    def body(land_ref, out, send, recv):
        x, y, c, chips = _chips()
        fwd = [pltpu.make_async_remote_copy(src_ref=out.at[4 * px + 2 * py + c], dst_ref=out.at[4 * px + 2 * py + c],
                                            send_sem=send.at[j], recv_sem=recv.at[j], device_id=(x, y, 1 - c), device_id_type=MESH)
               for j, (px, py) in enumerate(chips)]
        for cp in fwd:
            cp.start()
        for j, (px, py) in enumerate(chips):
            slot = out.at[4 * px + 2 * py + 1 - c]
            pltpu.make_async_remote_copy(src_ref=slot, dst_ref=slot, send_sem=send.at[j], recv_sem=recv.at[j],
                                         device_id=(x, y, 1 - c), device_id_type=MESH).wait()

    return pl.pallas_call(
        body, name=name, out_shape=jax.ShapeDtypeStruct(land.shape, land.dtype),
        in_specs=[_HBM], out_specs=_HBM, input_output_aliases={0: 0},
        scratch_shapes=[pltpu.SemaphoreType.DMA((3,)), pltpu.SemaphoreType.DMA((3,))],
        compiler_params=pltpu.CompilerParams(has_side_effects=True),
    )(land)


def _chip_targets():
    x, y, c, chips = _chips()
    return 2 * x + y, [((px, py, c), 2 * px + py) for px, py in chips]


def _chipsum_start(s, land, order, name):
    def body(s_ref, land_ref, order_ref, *rest):
        sems, token = rest[:6], rest[8]
        mine, targets = _chip_targets()
        for k, (to, ch) in enumerate(targets):
            pltpu.make_async_remote_copy(src_ref=s_ref.at[ch], dst_ref=land_ref.at[mine], send_sem=sems[k], recv_sem=sems[3 + k],
                                         device_id=to, device_id_type=MESH).start()
        token[...] = jnp.zeros_like(token)

    outs = pl.pallas_call(
        body, name=name,
        out_shape=(pltpu.SemaphoreType.DMA(()),) * 6 + (pltpu.HBM(s.shape, s.dtype), pltpu.HBM(land.shape, land.dtype),
                                                        jax.ShapeDtypeStruct((8, 128), F32)),
        in_specs=(_HBM, _HBM, _ANY), out_specs=(_SEM,) * 6 + (_HBM, _HBM, pl.BlockSpec(memory_space=pltpu.VMEM)),
        input_output_aliases={0: 6, 1: 7}, compiler_params=pltpu.CompilerParams(has_side_effects=_DATAFLOW),
    )(_hbm(s), _hbm(land), order)
    return outs[:6], outs[6], outs[7], outs[8]


def _chipsum_wait(sems, s_thru, land_thru, after, name):
    def body(s_ref, land_ref, *rest):
        sems_ = rest[:6]
        mine, targets = _chip_targets()
        for k, (to, ch) in enumerate(targets):
            cp = pltpu.make_async_remote_copy(src_ref=s_ref.at[ch], dst_ref=land_ref.at[ch], send_sem=sems_[k], recv_sem=sems_[3 + k],
                                              device_id=to, device_id_type=MESH)
            cp.wait_send()
            cp.wait_recv()

    return pl.pallas_call(
        body, name=name, out_shape=(pltpu.HBM(s_thru.shape, s_thru.dtype), pltpu.HBM(land_thru.shape, land_thru.dtype)),
        in_specs=(_HBM, _HBM) + (_SEM,) * 6 + (_ANY,), out_specs=(_HBM, _HBM), input_output_aliases={0: 0, 1: 1},
        compiler_params=pltpu.CompilerParams(has_side_effects=_DATAFLOW),
    )(s_thru, land_thru, *sems, after)[1]


def _mm(a, b, *, mode, name, out_dtypes=(F32,), epilogue=None, extras=(), tm=1024, tn=1024, tk=2048,
        col_blocked_b=False, col_blocked_out=False, order=None):
    CB = 1024
    if col_blocked_b:
        assert mode in ("nn", "nt") and b.shape[2] == CB
        (M, K), N = a.shape, (b.shape[0] * CB if mode == "nn" else b.shape[1])
        assert mode == "nn" or tk % CB == 0
        tn = CB if mode == "nn" else tn
    elif mode == "nn":
        (M, K), N = a.shape, b.shape[1]
    elif mode == "nt":
        (M, K), N = a.shape, b.shape[0]
    else:
        (K, M), N = a.shape, b.shape[1]
    if col_blocked_out:
        assert len(out_dtypes) == 1 and N % CB == 0
        tn = CB
    tm, tn, tk = min(tm, M), min(tn, N), min(tk, K)
    assert M % tm == 0 and N % tn == 0 and K % tk == 0, (M, N, K, tm, tn, tk)
    nk = K // tk
    ne, no = len(extras), len(out_dtypes)
    dims = {"nn": (((1,), (0,)), ((), ())), "nt": (((1,), (1,)), ((), ())), "tn": (((0,), (0,)), ((), ()))}[mode]

    no_ = 0 if order is None else 1

    def body(a_ref, b_ref, *rest):
        rest = rest[no_:]
        ex, outs = rest[:ne], rest[ne:ne + no]

        def finish(acc):
            res = epilogue(acc, *[e[...] for e in ex]) if epilogue is not None else (acc,)
            for o, r in zip(outs, res):
                o[...] = r.astype(o.dtype)

        if col_blocked_b and mode == "nt":
            part = sum(lax.dot_general(a_ref[:, q * CB:(q + 1) * CB], b_ref[q], dims, preferred_element_type=F32)
                       for q in range(tk // CB))
        else:
            part = lax.dot_general(a_ref[...], b_ref[...], dims, preferred_element_type=F32)
        if nk == 1:
            finish(part)
        else:
            acc_ref = rest[-1]
            k = pl.program_id(2)

            @pl.when(k == 0)
            def _():
                acc_ref[...] = part

            @pl.when(k > 0)
            def _():
                acc_ref[...] += part

            @pl.when(k == nk - 1)
            def _():
                finish(acc_ref[...])

    a_spec = {"nn": pl.BlockSpec((tm, tk), lambda i, j, k: (i, k)), "nt": pl.BlockSpec((tm, tk), lambda i, j, k: (i, k)),
              "tn": pl.BlockSpec((tk, tm), lambda i, j, k: (k, i))}[mode]
    b_spec = {"nn": pl.BlockSpec((tk, tn), lambda i, j, k: (k, j)), "nt": pl.BlockSpec((tn, tk), lambda i, j, k: (j, k)),
              "tn": pl.BlockSpec((tk, tn), lambda i, j, k: (k, j))}[mode]
    if col_blocked_b:
        b_spec = (pl.BlockSpec((None, tk, CB), lambda i, j, k: (j, k, 0)) if mode == "nn"
                  else pl.BlockSpec((tk // CB, tn, CB), lambda i, j, k: (k, j, 0)))
    e_spec = pl.BlockSpec((tm, tn), lambda i, j, k: (i, j))
    o_spec, o_dims = e_spec, (M, N)
    if col_blocked_out:
        o_spec, o_dims = pl.BlockSpec((None, tm, CB), lambda i, j, k: (j, i, 0)), (N // CB, M, CB)
    outs = pl.pallas_call(
        body, name=name, grid=(M // tm, N // tn, nk),
        in_specs=[a_spec, b_spec] + [_ANY] * no_ + [e_spec] * ne, out_specs=[o_spec] * no,
        out_shape=[jax.ShapeDtypeStruct(o_dims, dt) for dt in out_dtypes],
        scratch_shapes=[pltpu.VMEM((tm, tn), F32)] if nk > 1 else [],
        compiler_params=pltpu.CompilerParams(dimension_semantics=("parallel", "parallel", "arbitrary")),
    )(a, b, *([] if order is None else [order]), *extras)
    return outs if no > 1 else outs[0]


def _norm_fwd(x, g, sc, sh, name, resid=None):
    B, S, Dm = x.shape
    ts = min(S, 256)
    tok = pl.BlockSpec((None, ts, Dm), lambda b, i: (b, i, 0))
    row = pl.BlockSpec((None, 1, Dm), lambda b, i: (b, 0, 0))
    par = pl.BlockSpec((1, Dm), lambda b, i: (0, 0))

    def body(*refs):
        if resid is not None:
            x_ref, br_ref, gt_ref, g_ref, sc_ref, sh_ref, xo_ref, h_ref = refs
            xv = x_ref[...] + gt_ref[...] * br_ref[...]
            xo_ref[...] = xv
        else:
            x_ref, g_ref, sc_ref, sh_ref, h_ref = refs
            xv = x_ref[...]
        r = lax.rsqrt(jnp.mean(xv * xv, axis=-1, keepdims=True) + EPS)
        h_ref[...] = ((xv * r * g_ref[...]) * (1.0 + sc_ref[...]) + sh_ref[...]).astype(BF16)

    h_shape = jax.ShapeDtypeStruct((B, S, Dm), BF16)
    if resid is not None:
        return pl.pallas_call(body, name=name, grid=(B, S // ts), in_specs=[tok, tok, row, par, row, row],
                              out_specs=[tok, tok], out_shape=[jax.ShapeDtypeStruct((B, S, Dm), F32), h_shape],
                              )(x, resid[0], resid[1], g, sc, sh)
    return pl.pallas_call(body, name=name, grid=(B, S // ts), in_specs=[tok, par, row, row], out_specs=tok,
                          out_shape=h_shape)(x, g, sc, sh)


def _norm_bwd(x, g, name, *, sc=None, dh=None, dres=None, tgt=None, br=None, gate=None, x_is_prev=False):
    B, S, Dm = x.shape
    ts = min(S, 256)
    final = tgt is not None
    has_br = br is not None
    tok = pl.BlockSpec((None, ts, Dm), lambda b, i: (b, i, 0))
    row = pl.BlockSpec((None, 1, Dm), lambda b, i: (b, 0, 0))
    par = pl.BlockSpec((1, Dm), lambda b, i: (0, 0))
    ins, in_specs = [x, g], [tok, par]
    if final:
        ins, in_specs = ins + [tgt], in_specs + [tok]
    else:
        ins, in_specs = ins + [sc, dh], in_specs + [row, tok]
    if dres is not None:
        ins, in_specs = ins + [dres], in_specs + [tok]
    if has_br:
        ins, in_specs = ins + [br, gate], in_specs + [tok, row]
    n_in = len(ins)
    out_shape = [jax.ShapeDtypeStruct((B, S, Dm), F32), jax.ShapeDtypeStruct((1, Dm), F32)]
    out_specs = [tok, par]
    if final:
        out_shape.append(jax.ShapeDtypeStruct((1, 128), F32))
        out_specs.append(pl.BlockSpec((1, 128), lambda b, i: (0, 0)))
    else:
        out_shape += [jax.ShapeDtypeStruct((B, 1, Dm), F32)] * 2
        out_specs += [row, row]
    if has_br:
        out_shape += [jax.ShapeDtypeStruct((B, S, Dm), BF16), jax.ShapeDtypeStruct((B, 1, Dm), F32)]
        out_specs += [tok, row]

    def body(*refs):
        it = iter(refs[:n_in])
        outs = iter(refs[n_in:])
        x_ref, g_ref = next(it), next(it)
        b, i = pl.program_id(0), pl.program_id(1)
        first, first_row = (b == 0) & (i == 0), i == 0
        xv, gv = x_ref[...], g_ref[...]
        if x_is_prev:
            xv = xv + refs[n_in - 1][...] * refs[n_in - 2][...]
        r = lax.rsqrt(jnp.mean(xv * xv, axis=-1, keepdims=True) + EPS)
        n = xv * r
        dx_ref, dg_ref = next(outs), next(outs)

        def acc(ref, val, init):
            @pl.when(init)
            def _():
                ref[...] = val

            @pl.when(jnp.logical_not(init))
            def _():
                ref[...] += val

        if final:
            t_ref = next(it)
            loss_ref = next(outs)
            e = n * gv - t_ref[...]
            acc(loss_ref, jnp.zeros((1, 128), F32) + 0.5 * jnp.sum(e * e) / Dm, first)
            dyg = e * (1.0 / Dm)
        else:
            sc_ref, dh_ref = next(it), next(it)
            dsc_ref, dsh_ref = next(outs), next(outs)
            dhv = dh_ref[...]
            acc(dsh_ref, jnp.sum(dhv, axis=0, keepdims=True), first_row)
            acc(dsc_ref, jnp.sum(dhv * (n * gv), axis=0, keepdims=True), first_row)
            dyg = dhv * (1.0 + sc_ref[...])
        acc(dg_ref, jnp.sum(dyg * n, axis=0, keepdims=True), first)
        dn = dyg * gv
        dx = r * (dn - n * jnp.mean(dn * n, axis=-1, keepdims=True))
        if dres is not None:
            dx = dx + next(it)[...]
        dx_ref[...] = dx
        if has_br:
            br_ref, gt_ref = next(it), next(it)
            dbr_ref, dgt_ref = next(outs), next(outs)
            dbr_ref[...] = (dx * gt_ref[...]).astype(BF16)
            acc(dgt_ref, jnp.sum(dx * br_ref[...], axis=0, keepdims=True), first_row)

    outs = pl.pallas_call(body, name=name, grid=(B, S // ts), in_specs=in_specs, out_specs=out_specs, out_shape=out_shape,
                          compiler_params=pltpu.CompilerParams(dimension_semantics=("arbitrary", "arbitrary")))(*ins)
    res = dict(dx=outs[0], dg=outs[1])
    if final:
        res["loss"] = outs[2]
    else:
        res["dsc"], res["dsh"] = outs[2], outs[3]
    if has_br:
        res["dbr"], res["dgate"] = outs[-2], outs[-1]
    return res


def _gm_heads(vg, lng, lnb):
    res = []
    for h in range(GM_H):
        sl = slice(h * 128, (h + 1) * 128)
        vh = vg[:, sl]
        xc = vh - jnp.mean(vh, axis=-1, keepdims=True)
        rstd = lax.rsqrt(jnp.mean(xc * xc, axis=-1, keepdims=True) + 1e-5)
        xhat = xc * rstd
        res.append((xhat, rstd, xhat * lng[:, sl] + lnb[:, sl]))
    return res


def _gm_gate(heads, wt_ref, bsx, nch):
    cols = []
    for h in range(GM_H):
        vn = heads[h][2].astype(BF16)
        rows = [_dot(wt_ref[h], vn[c * CHUNK:(c + 1) * CHUNK]) + bsx[:, h * 128:(h + 1) * 128] for c in range(nch)]
        cols.append(jnp.concatenate(rows, axis=0) if nch > 1 else rows[0])
    return jnp.concatenate(cols, axis=1)


def _gm_specs(S):
    tb = min(S, 512)
    u = pl.BlockSpec((None, tb, GM_W), lambda b, i: (b, i, OFF["u"] // GM_W))
    v = pl.BlockSpec((None, tb, GM_W), lambda b, i: (b, i, OFF["v"] // GM_W))
    tok = pl.BlockSpec((None, tb, GM_W), lambda b, i: (b, i, 0))
    return tb, u, v, tok


def _gmlp_fwd(P, lng, lnb, wt, bsx, og, name):
    B, S, _ = P.shape
    tb, u_spec, v_spec, tok = _gm_specs(S)
    nch = tb // CHUNK

    def body(u_ref, v_ref, lng_ref, lnb_ref, wt_ref, bsx_ref, og_ref, o_ref):
        heads = _gm_heads(_gelu(v_ref[...]), lng_ref[...], lnb_ref[...])
        y = _gelu(u_ref[...]) * _gm_gate(heads, wt_ref, bsx_ref[...], nch)
        r = lax.rsqrt(jnp.mean(y * y, axis=-1, keepdims=True) + EPS)
        o_ref[...] = (y * r * og_ref[...]).astype(BF16)

    return pl.pallas_call(
        body, name=name, grid=(B, S // tb),
        in_specs=[u_spec, v_spec, _full((1, GM_W)), _full((1, GM_W)), _full((GM_H, 128, 128)), _full((128, GM_W)), _full((1, GM_W))],
        out_specs=tok, out_shape=jax.ShapeDtypeStruct((B, S, GM_W), BF16))(P, P, lng, lnb, wt, bsx, og)


def _gmlp_bwd(P, dcat, lng, lnb, wt, wtT, bsx, og, name):
    B, S, _ = P.shape
    tb, u_spec, v_spec, tok = _gm_specs(S)
    nch = tb // CHUNK
    do_spec = pl.BlockSpec((None, tb, GM_W), lambda b, i: (b, i, 0))

    def body(u_ref, v_ref, do_ref, lng_ref, lnb_ref, wt_ref, wtT_ref, bsx_ref, og_ref,
             du_ref, dv_ref, dlng_ref, dlnb_ref, dws_ref, dbsx_ref, dog_ref):
        first = (pl.program_id(0) == 0) & (pl.program_id(1) == 0)

        @pl.when(first)
        def _():
            for ref in (dlng_ref, dlnb_ref, dws_ref, dbsx_ref, dog_ref):
                ref[...] = jnp.zeros(ref.shape, F32)

        u, v, lng = u_ref[...], v_ref[...], lng_ref[...]
        ug = _gelu(u)
        heads = _gm_heads(_gelu(v), lng, lnb_ref[...])
        gate = _gm_gate(heads, wt_ref, bsx_ref[...], nch)
        y = ug * gate
        r = lax.rsqrt(jnp.mean(y * y, axis=-1, keepdims=True) + EPS)
        yn = y * r
        dout = do_ref[...]
        dog_ref[...] += jnp.sum(dout * yn, axis=0, keepdims=True)
        dyn = dout * og_ref[...]
        dy = r * (dyn - yn * jnp.mean(dyn * yn, axis=-1, keepdims=True))
        du_ref[...] = (dy * gate * _gelu_grad(u)).astype(BF16)
        dgate = dy * ug
        tril = lax.broadcasted_iota(jnp.int32, (128, 128), 0) >= lax.broadcasted_iota(jnp.int32, (128, 128), 1)
        dvg = []
        for h in range(GM_H):
            sl = slice(h * 128, (h + 1) * 128)
            xhat, rstd, vn = heads[h]
            vnb = vn.astype(BF16)
            dgh = dgate[:, sl]
            dgb = dgh.astype(BF16)
            dbs = jnp.zeros((128, 128), F32)
            dw = jnp.zeros((128, 128), F32)
            dvn = []
            for c in range(nch):
                rs = slice(c * CHUNK, (c + 1) * CHUNK)
                dbs = dbs + dgh[rs]
                dw = dw + _dot_nt(dgb[rs], vnb[rs])
                dvn.append(_dot(wtT_ref[h], dgb[rs]))
            dvn = jnp.concatenate(dvn, axis=0) if nch > 1 else dvn[0]
            dbsx_ref[:, sl] += dbs
            dws_ref[h] += jnp.where(tril, dw, 0.0)
            dlng_ref[:, sl] += jnp.sum(dvn * xhat, axis=0, keepdims=True)
            dlnb_ref[:, sl] += jnp.sum(dvn, axis=0, keepdims=True)
            dxh = dvn * lng[:, sl]
            dvg.append(rstd * (dxh - jnp.mean(dxh, axis=-1, keepdims=True) - xhat * jnp.mean(dxh * xhat, axis=-1, keepdims=True)))
        dv_ref[...] = (jnp.concatenate(dvg, axis=1) * _gelu_grad(v)).astype(BF16)

    p512, w3 = _full((1, GM_W)), _full((GM_H, 128, 128))
    return pl.pallas_call(
        body, name=name, grid=(B, S // tb),
        in_specs=[u_spec, v_spec, do_spec, p512, p512, w3, w3, _full((128, GM_W)), p512],
        out_specs=[tok, tok, p512, p512, w3, _full((128, GM_W)), p512],
        out_shape=[jax.ShapeDtypeStruct((B, S, GM_W), BF16)] * 2 + [
            jax.ShapeDtypeStruct((1, GM_W), F32), jax.ShapeDtypeStruct((1, GM_W), F32),
            jax.ShapeDtypeStruct((GM_H, 128, 128), F32), jax.ShapeDtypeStruct((128, GM_W), F32),
            jax.ShapeDtypeStruct((1, GM_W), F32)],
        compiler_params=pltpu.CompilerParams(dimension_semantics=("arbitrary", "arbitrary")),
    )(P, P, dcat, lng, lnb, wt, wtT, bsx, og)


def _lane_half():
    return lax.broadcasted_iota(jnp.int32, (128, 128), 1) // 64


def _att_stack(x, kvh, dtype):
    half = _lane_half()
    rows = []
    for g in range(4):
        i = kvh * 4 + g
        pair = x[:, (i // 2) * 128:(i // 2 + 1) * 128]
        if i % 2 != kvh:
            pair = pltpu.roll(pair, 64, 1)
        rows.append(jnp.where(half == kvh, pair, 0.0))
    return jnp.concatenate(rows, axis=0).astype(dtype)


def _att_unstack(pairs, y, kvh):
    half = _lane_half()
    for g in range(4):
        i = kvh * 4 + g
        piece = y[g * 128:(g + 1) * 128]
        if i % 2 != kvh:
            piece = pltpu.roll(piece, 64, 1)
        pairs[i // 2] = jnp.where(half == i % 2, piece, pairs[i // 2])
    return pairs


def _att_probs(qb, k2, st, sink_ref, kvh):
    qm = _att_stack(qb, kvh, BF16)
    s = _dot_nt(qm, k2) * (64 ** -0.5)
    qi = lax.broadcasted_iota(jnp.int32, (512, 256), 0) % 128
    kj = lax.broadcasted_iota(jnp.int32, (512, 256), 1)
    diff = qi + 128 - kj
    valid = (diff >= 0) & (diff < 128) & (st + kj - 128 >= 0)
    s = jnp.where(valid, s, NEG_INF)
    grp = lax.broadcasted_iota(jnp.int32, (512, 1), 0) // 128
    sink = jnp.zeros((512, 1), F32)
    for g in range(4):
        sink = jnp.where(grp == g, sink_ref[kvh * 4 + g], sink)
    m = jnp.maximum(jnp.max(s, axis=-1, keepdims=True), sink)
    e = jnp.exp(s - m)
    esink = jnp.exp(sink - m)
    inv = 1.0 / (jnp.sum(e, axis=-1, keepdims=True) + esink)
    return qm, e * inv, esink * inv


def _att_specs(S):
    q = pl.BlockSpec((None, S, ATT_W), lambda b: (b, 0, OFF["q"] // ATT_W))
    k = pl.BlockSpec((None, S, KV_W), lambda b: (b, 0, OFF["k"] // KV_W))
    v = pl.BlockSpec((None, S, KV_W), lambda b: (b, 0, OFF["vv"] // KV_W))
    tok = pl.BlockSpec((None, S, ATT_W), lambda b: (b, 0, 0))
    kv = pl.BlockSpec((None, S, KV_W), lambda b: (b, 0, 0))
    return q, k, v, tok, kv


_SMEM = pl.BlockSpec(memory_space=pltpu.SMEM)


def _attn_fwd(P, sinks, og, name):
    B, S, _ = P.shape
    q_spec, k_spec, v_spec, tok, _ = _att_specs(S)

    def body(q_ref, k_ref, v_ref, sink_ref, og_ref, o_ref, kpad, vpad):
        kpad[0:128, :] = jnp.zeros((128, KV_W), BF16)
        vpad[0:128, :] = jnp.zeros((128, KV_W), BF16)
        kpad[128:, :] = k_ref[...].astype(BF16)
        vpad[128:, :] = v_ref[...].astype(BF16)

        def step(n, carry):
            st = pl.multiple_of(n * 128, 128)
            qb = q_ref[pl.ds(st, 128), :]
            k2, v2 = kpad[pl.ds(st, 256), :], vpad[pl.ds(st, 256), :]
            pairs = [jnp.zeros((128, 128), F32)] * 4
            for kvh in range(2):
                _, p, _ = _att_probs(qb, k2, st, sink_ref, kvh)
                pairs = _att_unstack(pairs, _dot(p.astype(BF16), v2), kvh)
            o = jnp.concatenate(pairs, axis=1)
            r = lax.rsqrt(jnp.mean(o * o, axis=-1, keepdims=True) + EPS)
            o_ref[pl.ds(st, 128), :] = (o * r * og_ref[...]).astype(BF16)
            return carry

        lax.fori_loop(0, S // 128, step, 0)

    return pl.pallas_call(
        body, name=name, grid=(B,), in_specs=[q_spec, k_spec, v_spec, _SMEM, _full((1, ATT_W))], out_specs=tok,
        out_shape=jax.ShapeDtypeStruct((B, S, ATT_W), BF16),
        scratch_shapes=[pltpu.VMEM((S + 128, KV_W), BF16)] * 2)(P, P, P, sinks, og)


def _attn_bwd(P, dcat, sinks, og, name):
    B, S, _ = P.shape
    q_spec, k_spec, v_spec, tok, kv = _att_specs(S)
    do_spec = pl.BlockSpec((None, S, ATT_W), lambda b: (b, 0, GM_W // ATT_W))

    def body(q_ref, k_ref, v_ref, do_ref, sink_ref, og_ref, dq_ref, dk_ref, dv_ref, dsink_ref, dog_ref,
             kpad, vpad, dkpad, dvpad):
        @pl.when(pl.program_id(0) == 0)
        def _():
            dsink_ref[...] = jnp.zeros((8, 128), F32)
            dog_ref[...] = jnp.zeros((1, ATT_W), F32)

        kpad[0:128, :] = jnp.zeros((128, KV_W), BF16)
        vpad[0:128, :] = jnp.zeros((128, KV_W), BF16)
        kpad[128:, :] = k_ref[...].astype(BF16)
        vpad[128:, :] = v_ref[...].astype(BF16)
        dkpad[...] = jnp.zeros((S + 128, KV_W), F32)
        dvpad[...] = jnp.zeros((S + 128, KV_W), F32)
        half = _lane_half()
        head_row = lax.broadcasted_iota(jnp.int32, (8, 128), 0)

        def step(n, carry):
            st = pl.multiple_of(n * 128, 128)
            qb = q_ref[pl.ds(st, 128), :]
            k2, v2 = kpad[pl.ds(st, 256), :], vpad[pl.ds(st, 256), :]
            saved, pairs = [], [jnp.zeros((128, 128), F32)] * 4
            for kvh in range(2):
                qm, p, psink = _att_probs(qb, k2, st, sink_ref, kvh)
                o = _dot(p.astype(BF16), v2)
                saved.append((qm, p, psink, o))
                pairs = _att_unstack(pairs, o, kvh)
            o = jnp.concatenate(pairs, axis=1)
            r = lax.rsqrt(jnp.mean(o * o, axis=-1, keepdims=True) + EPS)
            on = o * r
            dout = do_ref[pl.ds(st, 128), :]
            dog_ref[...] += jnp.sum(dout * on, axis=0, keepdims=True)
            dyn = dout * og_ref[...]
            do = r * (dyn - on * jnp.mean(dyn * on, axis=-1, keepdims=True))
            dq_pairs = [jnp.zeros((128, 128), F32)] * 4
            dsink = jnp.zeros((8, 128), F32)
            for kvh in range(2):
                qm, p, psink, og_ = saved[kvh]
                dog = _att_stack(do, kvh, F32)
                delta = jnp.sum(dog * jnp.where(jnp.concatenate([half] * 4, axis=0) == kvh, og_, 0.0), axis=-1, keepdims=True)
                dogb, pb = dog.astype(BF16), p.astype(BF16)
                dvpad[pl.ds(st, 256), :] += _dot_tn(pb, dogb)
                dp = _dot_nt(dogb, v2)
                ds = (p * (dp - delta) * (64 ** -0.5)).astype(BF16)
                sd = psink * delta
                for g in range(4):
                    dsink = dsink - jnp.where(head_row == kvh * 4 + g, jnp.sum(sd[g * 128:(g + 1) * 128]), 0.0)
                dq_pairs = _att_unstack(dq_pairs, _dot(ds, k2), kvh)
                dkpad[pl.ds(st, 256), :] += _dot_tn(ds, qm)
            dsink_ref[...] += dsink
            dq_ref[pl.ds(st, 128), :] = jnp.concatenate(dq_pairs, axis=1).astype(BF16)
            return carry

        lax.fori_loop(0, S // 128, step, 0)
        dk_ref[...] = dkpad[128:, :].astype(BF16)
        dv_ref[...] = dvpad[128:, :].astype(BF16)

    return pl.pallas_call(
        body, name=name, grid=(B,),
        in_specs=[q_spec, k_spec, v_spec, do_spec, _SMEM, _full((1, ATT_W))],
        out_specs=[tok, kv, kv, _full((8, 128)), _full((1, ATT_W))],
        out_shape=[jax.ShapeDtypeStruct((B, S, ATT_W), BF16), jax.ShapeDtypeStruct((B, S, KV_W), BF16),
                   jax.ShapeDtypeStruct((B, S, KV_W), BF16), jax.ShapeDtypeStruct((8, 128), F32),
                   jax.ShapeDtypeStruct((1, ATT_W), F32)],
        scratch_shapes=[pltpu.VMEM((S + 128, KV_W), BF16)] * 2 + [pltpu.VMEM((S + 128, KV_W), F32)] * 2,
        compiler_params=pltpu.CompilerParams(dimension_semantics=("arbitrary",)),
    )(P, P, P, dcat, sinks, og)


CONV_TC = 256
CONV_RC = 64


def _conv_taps(ext, r0):
    return [ext[pl.ds(r0 + 8 - k, CONV_RC), :] for k in range(4)]


def _conv_pre(taps, w_ref, b_ref):
    acc = b_ref[...] + w_ref[3:4, :] * taps[0]
    for k in range(1, 4):
        acc = acc + w_ref[3 - k:4 - k, :] * taps[k]
    return acc


def _conv_fwd(P, w8, b, name):
    B, S, _ = P.shape
    nj = CONV_CH // CONV_TC
    x_spec = pl.BlockSpec((None, S, CONV_TC), lambda b_, j: (b_, 0, OFF["xbc"] // CONV_TC + j))
    tok = pl.BlockSpec((None, S, CONV_TC), lambda b_, j: (b_, 0, j))

    def body(x_ref, w_ref, b_ref, o_ref, ext):
        ext[0:8, :] = jnp.zeros((8, CONV_TC), F32)
        ext[8:, :] = x_ref[...]
        for r0 in range(0, S, CONV_RC):
            pre = _conv_pre(_conv_taps(ext, r0), w_ref, b_ref)
            o_ref[pl.ds(r0, CONV_RC), :] = pre * _sigmoid(pre)

    return pl.pallas_call(
        body, name=name, grid=(B, nj),
        in_specs=[x_spec, pl.BlockSpec((8, CONV_TC), lambda b_, j: (0, j)), pl.BlockSpec((1, CONV_TC), lambda b_, j: (0, j))],
        out_specs=tok, out_shape=jax.ShapeDtypeStruct((B, S, CONV_CH), F32),
        scratch_shapes=[pltpu.VMEM((S + 8, CONV_TC), F32)])(P, w8, b)


def _conv_bwd(P, dact, w8, b, name):
    B, S, _ = P.shape
    nj = CONV_CH // CONV_TC
    x_spec = pl.BlockSpec((None, S, CONV_TC), lambda j, b_: (b_, 0, OFF["xbc"] // CONV_TC + j))
    tok = pl.BlockSpec((None, S, CONV_TC), lambda j, b_: (b_, 0, j))
    w_spec = pl.BlockSpec((8, CONV_TC), lambda j, b_: (0, j))
    b_spec = pl.BlockSpec((1, CONV_TC), lambda j, b_: (0, j))

    def body(x_ref, d_ref, w_ref, b_ref, dx_ref, dw_ref, db_ref, ext, extd):
        @pl.when(pl.program_id(1) == 0)
        def _():
            dw_ref[...] = jnp.zeros((8, CONV_TC), F32)
            db_ref[...] = jnp.zeros((1, CONV_TC), F32)

        ext[0:8, :] = jnp.zeros((8, CONV_TC), F32)
        ext[8:, :] = x_ref[...]
        extd[pl.ds(8 + S, 8), :] = jnp.zeros((8, CONV_TC), F32)
        db = jnp.zeros((1, CONV_TC), F32)
        dws = [jnp.zeros((1, CONV_TC), F32)] * 4
        for r0 in range(0, S, CONV_RC):
            taps = _conv_taps(ext, r0)
            pre = _conv_pre(taps, w_ref, b_ref)
            sg = _sigmoid(pre)
            dpre = d_ref[pl.ds(r0, CONV_RC), :] * (sg * (1.0 + pre * (1.0 - sg)))
            extd[pl.ds(8 + r0, CONV_RC), :] = dpre
            db = db + jnp.sum(dpre, axis=0, keepdims=True)
            dws = [dws[i] + jnp.sum(dpre * taps[3 - i], axis=0, keepdims=True) for i in range(4)]
        for r0 in range(0, S, CONV_RC):
            dx = w_ref[3:4, :] * extd[pl.ds(8 + r0, CONV_RC), :]
            for k in range(1, 4):
                dx = dx + w_ref[3 - k:4 - k, :] * extd[pl.ds(8 + r0 + k, CONV_RC), :]
            dx_ref[pl.ds(r0, CONV_RC), :] = dx.astype(BF16)
        db_ref[...] += db
        sub = lax.broadcasted_iota(jnp.int32, (8, CONV_TC), 0)
        dw_ref[...] += sum(jnp.where(sub == i, dws[i], 0.0) for i in range(4))

    return pl.pallas_call(
        body, name=name, grid=(nj, B), in_specs=[x_spec, tok, w_spec, b_spec], out_specs=[tok, w_spec, b_spec],
        out_shape=[jax.ShapeDtypeStruct((B, S, CONV_CH), BF16), jax.ShapeDtypeStruct((8, CONV_CH), F32),
                   jax.ShapeDtypeStruct((1, CONV_CH), F32)],
        scratch_shapes=[pltpu.VMEM((S + 8, CONV_TC), F32), pltpu.VMEM((S + 16, CONV_TC), F32)],
        compiler_params=pltpu.CompilerParams(dimension_semantics=("arbitrary", "arbitrary")),
    )(P, dact, w8, b)


def _ssd_consts():
    hd = np.arange(SSM_W) // SSM_HD
    E = (np.arange(128)[:, None] == hd[None, :]).astype(np.float32)
    tri = (np.arange(128)[:, None] >= np.arange(128)[None, :]).astype(np.float32)
    return jnp.asarray(E, BF16), jnp.asarray(E.T, BF16), jnp.asarray(tri, BF16), jnp.asarray(tri.T, BF16)


def _pieces(x, n):
    out, r = [], x
    for _ in range(n):
        p = r.astype(BF16)
        out.append(p)
        r = r - p.astype(F32)
    return out


def _dot01(x, m01, n):
    return sum(_dot(p, m01) for p in _pieces(x, n))


def _dot01_left(m01, x, n):
    return sum(_dot(m01, p) for p in _pieces(x, n))


def _ssd_pre(xa, dtraw, bias, alog, E, tri):
    lane = lax.broadcasted_iota(jnp.int32, (128, 128), 1)
    pre = dtraw + bias
    dtp = jnp.where(lane < SSM_H, jnp.maximum(pre, 0.0) + jnp.log(1.0 + jnp.exp(-jnp.abs(pre))), 0.0)
    a = -jnp.exp(alog)
    acs = _dot01_left(tri, dtp * a, 3)
    acsT = acs.T
    dtE, acsE = _dot01(dtp, E, 2), _dot01(acs, E, 3)
    X = xa[:, :SSM_W]
    xdt = X * dtE
    wE = jnp.exp(acsE[127:128, :] - acsE)
    eE = jnp.exp(acsE)
    cdE = eE[127:128, :]
    return dict(pre=pre, dtp=dtp, a=a, acs=acs, acsT=acsT, dtE=dtE, acsE=acsE, cdE=cdE, X=X, xdt=xdt, wE=wE, eE=eE)


def _ssd_decay(c, h):
    lm = lax.broadcasted_iota(jnp.int32, (128, 128), 0) >= lax.broadcasted_iota(jnp.int32, (128, 128), 1)
    return jnp.exp(jnp.where(lm, c["acs"][:, h:h + 1] - c["acsT"][h:h + 1, :], NEG_INF))


def _ssd_pair_operands(c, CB, h0):
    lane = lax.broadcasted_iota(jnp.int32, (128, 128), 1)
    L0, L1 = _ssd_decay(c, h0), _ssd_decay(c, h0 + 1)
    M = jnp.concatenate([CB * L0, CB * L1], axis=1).astype(BF16)
    xp = c["xdt"][:, h0 * 64:h0 * 64 + 128]
    BD = jnp.concatenate([jnp.where(lane < 64, xp, 0.0), jnp.where(lane >= 64, xp, 0.0)], axis=0).astype(BF16)
    return L0, L1, M, BD


def _ssd_y(c, xa, state_ref, dskipE):
    per_group, ys = [], []
    for g in range(SSM_G):
        gs = slice(g * 512, (g + 1) * 512)
        Bb = xa[:, SSM_W + g * 128:SSM_W + (g + 1) * 128].astype(BF16)
        Cb = xa[:, SSM_W + 256 + g * 128:SSM_W + 256 + (g + 1) * 128].astype(BF16)
        CB = _dot_nt(Cb, Bb)
        Sg = state_ref[:, gs]
        yoff = _dot(Cb, Sg.astype(BF16)) * c["eE"][:, gs]
        ydiag, pairs = [], []
        for j in range(4):
            ops = _ssd_pair_operands(c, CB, g * 8 + 2 * j)
            pairs.append(ops)
            ydiag.append(_dot(ops[2], ops[3]))
        ys.append(jnp.concatenate(ydiag, axis=1) + yoff)
        per_group.append(dict(Bb=Bb, Cb=Cb, CB=CB, Sg=Sg, yoff=yoff, pairs=pairs))
    Y = jnp.concatenate(ys, axis=1) + c["X"] * dskipE
    return Y, per_group


def _ssd_specs(S, rev):
    nc = S // CHUNK
    cm = (lambda b, i: (b, nc - 1 - i)) if rev else (lambda b, i: (b, i))
    xa = pl.BlockSpec((None, CHUNK, CONV_CH), lambda b, i: cm(b, i) + (0,))
    z = [pl.BlockSpec((None, CHUNK, 256), lambda b, i, q=q: cm(b, i) + (OFF["z"] // 256 + q,)) for q in range(4)]
    dt = pl.BlockSpec((None, CHUNK, 128), lambda b, i: cm(b, i) + (OFF["dt"] // 128,))
    tok = pl.BlockSpec((None, CHUNK, SSM_W), lambda b, i: cm(b, i) + (0,))
    st = pl.BlockSpec((None, None, 128, SSM_W), lambda b, i: cm(b, i) + (0, 0))
    return nc, xa, z, dt, tok, st


def _ssd_fwd(xact, P, bias, alog, dskipE, ng, name):
    B, S, _ = P.shape
    nc, xa_spec, z_specs, dt_spec, tok, st_spec = _ssd_specs(S, False)
    E, _, tri, _ = _ssd_consts()

    def body(xa_ref, z0, z1, z2, z3, dt_ref, bias_ref, alog_ref, dsk_ref, ng_ref, E_ref, tri_ref, o_ref, sp_ref, state):
        @pl.when(pl.program_id(1) == 0)
        def _():
            state[...] = jnp.zeros((128, SSM_W), F32)

        sp_ref[...] = state[...]
        xa = xa_ref[...]
        c = _ssd_pre(xa, dt_ref[...], bias_ref[...], alog_ref[...], E_ref[...], tri_ref[...])
        Y, groups = _ssd_y(c, xa, state, dsk_ref[...])
        Z = (c["xdt"] * c["wE"]).astype(BF16)
        for g in range(SSM_G):
            gs = slice(g * 512, (g + 1) * 512)
            state[:, gs] = groups[g]["Sg"] * c["cdE"][:, gs] + _dot_tn(groups[g]["Bb"], Z[:, gs])
        zv = jnp.concatenate([z0[...], z1[...], z2[...], z3[...]], axis=1)
        yz = Y * (zv * _sigmoid(zv))
        outs = []
        for g in range(SSM_G):
            yg = yz[:, g * 512:(g + 1) * 512]
            outs.append(yg * lax.rsqrt(jnp.mean(yg * yg, axis=-1, keepdims=True) + EPS))
        o_ref[...] = (jnp.concatenate(outs, axis=1) * ng_ref[...]).astype(BF16)

    return pl.pallas_call(
        body, name=name, grid=(B, nc),
        in_specs=[xa_spec] + z_specs + [dt_spec, _full((1, 128)), _full((1, 128)), _full((1, SSM_W)), _full((1, SSM_W)),
                                        _full((128, SSM_W)), _full((128, 128))],
        out_specs=[tok, st_spec],
        out_shape=[jax.ShapeDtypeStruct((B, S, SSM_W), BF16), jax.ShapeDtypeStruct((B, nc, 128, SSM_W), F32)],
        scratch_shapes=[pltpu.VMEM((128, SSM_W), F32)],
        compiler_params=pltpu.CompilerParams(dimension_semantics=("arbitrary", "arbitrary")),
    )(xact, P, P, P, P, P, bias, alog, dskipE, ng, E, tri)


def _ssd_bwd(xact, P, sprev, dcat, bias, alog, dskipE, ng, name):
    B, S, _ = P.shape
    nc, xa_spec, z_specs, dt_spec, tok, st_spec = _ssd_specs(S, True)
    do_spec = pl.BlockSpec((None, CHUNK, SSM_W), lambda b, i: (b, nc - 1 - i, 1))
    E, ET, tri, triT = _ssd_consts()
    dt_out = pl.BlockSpec((None, CHUNK, 128), lambda b, i: (b, nc - 1 - i, 0))

    def body(xa_ref, z0, z1, z2, z3, dt_ref, sp_ref, do_ref, bias_ref, alog_ref, dsk_ref, ng_ref, E_ref, ET_ref, tri_ref,
             triT_ref, dxa_ref, dz_ref, ddt_ref, dbias_ref, dalog_ref, ddsk_ref, dng_ref, dstate):
        first = (pl.program_id(0) == 0) & (pl.program_id(1) == 0)

        @pl.when(first)
        def _():
            for ref in (dbias_ref, dalog_ref, ddsk_ref, dng_ref):
                ref[...] = jnp.zeros(ref.shape, F32)

        @pl.when(pl.program_id(1) == 0)
        def _():
            dstate[...] = jnp.zeros((128, SSM_W), F32)

        xa, ETm = xa_ref[...], ET_ref[...]
        c = _ssd_pre(xa, dt_ref[...], bias_ref[...], alog_ref[...], E_ref[...], tri_ref[...])
        Y, groups = _ssd_y(c, xa, sp_ref, dsk_ref[...])
        X, xdt = c["X"], c["xdt"]
        zv = jnp.concatenate([z0[...], z1[...], z2[...], z3[...]], axis=1)
        sg = _sigmoid(zv)
        zs = zv * sg
        yz = Y * zs
        dout = do_ref[...]
        dyz = []
        for g in range(SSM_G):
            gs = slice(g * 512, (g + 1) * 512)
            yg = yz[:, gs]
            r = lax.rsqrt(jnp.mean(yg * yg, axis=-1, keepdims=True) + EPS)
            yn = yg * r
            dng_ref[:, gs] += jnp.sum(dout[:, gs] * yn, axis=0, keepdims=True)
            dyn = dout[:, gs] * ng_ref[:, gs]
            dyz.append(r * (dyn - yn * jnp.mean(dyn * yn, axis=-1, keepdims=True)))
        dyz = jnp.concatenate(dyz, axis=1)
        dz_ref[...] = (dyz * Y * (sg * (1.0 + zv * (1.0 - sg)))).astype(BF16)
        dY = dyz * zs
        ddsk_ref[...] += jnp.sum(dY * X, axis=0, keepdims=True)
        dX = dY * dsk_ref[...]
        lane = lax.broadcasted_iota(jnp.int32, (128, 128), 1)
        sub = lax.broadcasted_iota(jnp.int32, (128, 128), 0)
        colform = jnp.zeros((128, 128), F32)
        rowform = jnp.zeros((128, 128), F32)
        dxdt, gacsE, dBC = [], [], []
        for g in range(SSM_G):
            gs = slice(g * 512, (g + 1) * 512)
            G = groups[g]
            Bb, Cb, CB, Sg = G["Bb"], G["Cb"], G["CB"], G["Sg"]
            dYg = dY[:, gs]
            dQ = (dYg * c["eE"][:, gs]).astype(BF16)
            dSn = dstate[:, gs]
            dSnb = dSn.astype(BF16)
            cd = c["cdE"][:, gs]
            dC = _dot_nt(dQ, Sg.astype(BF16))
            dSprev = _dot_tn(Cb, dQ) + dSn * cd
            t1 = jnp.broadcast_to(jnp.sum(dSn * Sg * cd, axis=0, keepdims=True), (8, 512))
            colform = colform + jnp.where(sub == 127, _dot01(t1, ETm[gs, :], 2)[0:1, :], 0.0)
            Zg = xdt[:, gs] * c["wE"][:, gs]
            dZ = _dot(Bb, dSnb)
            dB = _dot_nt(Zg.astype(BF16), dSnb)
            U = dZ * Zg
            ga = dYg * G["yoff"] - U
            ga = ga + jnp.where(lax.broadcasted_iota(jnp.int32, (128, 512), 0) == 127, jnp.sum(U, axis=0, keepdims=True), 0.0)
            gacsE.append(ga)
            dxg = [None] * 4
            dCB = jnp.zeros((128, 128), F32)
            for j in range(4):
                h0 = g * 8 + 2 * j
                L0, L1, M, BD = G["pairs"][j]
                dYp = dYg[:, j * 128:(j + 1) * 128].astype(BF16)
                dM = _dot_nt(dYp, BD)
                dBD = _dot_tn(M, dYp)
                dxg[j] = jnp.where(lane < 64, dBD[:128], dBD[128:])
                for t, (h, L) in enumerate(((h0, L0), (h0 + 1, L1))):
                    dMh = dM[:, t * 128:(t + 1) * 128]
                    dCB = dCB + dMh * L
                    Gh = dMh * CB * L
                    colform = colform + jnp.where(lane == h, jnp.sum(Gh, axis=1, keepdims=True), 0.0)
                    rowform = rowform - jnp.where(sub == h, jnp.sum(Gh, axis=0, keepdims=True), 0.0)
            dCBb = dCB.astype(BF16)
            dC = dC + _dot(dCBb, Bb)
            dB = dB + _dot_tn(dCBb, Cb)
            dxdt.append(jnp.concatenate(dxg, axis=1) + dZ * c["wE"][:, gs])
            dBC.append((dB, dC))
            dstate[:, gs] = dSprev
        dxdt = jnp.concatenate(dxdt, axis=1)
        dX = dX + dxdt * c["dtE"]
        ddt = _dot01(dxdt * X, ETm, 2)
        dacs = colform + rowform.T + _dot01(jnp.concatenate(gacsE, axis=1), ETm, 2)
        dda = _dot01_left(triT_ref[...], dacs, 2)
        ddt = ddt + dda * c["a"]
        dalog_ref[...] += jnp.sum(dda * c["dtp"], axis=0, keepdims=True) * c["a"]
        ddtraw = jnp.where(lane < SSM_H, ddt * _sigmoid(c["pre"]), 0.0)
        dbias_ref[...] += jnp.sum(ddtraw, axis=0, keepdims=True)
        ddt_ref[...] = ddtraw.astype(BF16)
        dxa_ref[...] = jnp.concatenate([dX, dBC[0][0], dBC[1][0], dBC[0][1], dBC[1][1]], axis=1)

    p128, p1k = _full((1, 128)), _full((1, SSM_W))
    return pl.pallas_call(
        body, name=name, grid=(B, nc),
        in_specs=[xa_spec] + z_specs + [dt_spec, st_spec, do_spec, p128, p128, p1k, p1k,
                                        _full((128, SSM_W)), _full((SSM_W, 128)), _full((128, 128)), _full((128, 128))],
        out_specs=[xa_spec, tok, dt_out, p128, p128, p1k, p1k],
        out_shape=[jax.ShapeDtypeStruct((B, S, CONV_CH), F32), jax.ShapeDtypeStruct((B, S, SSM_W), BF16),
                   jax.ShapeDtypeStruct((B, S, 128), BF16), jax.ShapeDtypeStruct((1, 128), F32),
                   jax.ShapeDtypeStruct((1, 128), F32), jax.ShapeDtypeStruct((1, SSM_W), F32),
                   jax.ShapeDtypeStruct((1, SSM_W), F32)],
        scratch_shapes=[pltpu.VMEM((128, SSM_W), F32)],
        compiler_params=pltpu.CompilerParams(dimension_semantics=("arbitrary", "arbitrary")),
    )(xact, P, P, P, P, P, sprev, dcat, bias, alog, dskipE, ng, E, ET, tri, triT)


def _adamw(w, parts, m, v, name, tr=512, row0=0, prev=None):
    Rtot, C = w.shape
    ns, R = parts.shape[0], parts.shape[1]
    tr = min(tr, R)
    assert R % tr == 0 and row0 % tr == 0
    off = row0 // tr
    c1 = 1.0 / (1.0 - ADAM_B1 ** ADAM_STEP)
    c2 = 1.0 / (1.0 - ADAM_B2 ** ADAM_STEP)

    def body(w_ref, p_ref, m_ref, v_ref, *rest):
        g_ref, d_ref, mo_ref, vo_ref = rest[-4:]
        g = p_ref[0].astype(F32)
        for s in range(1, ns):
            g = g + p_ref[s].astype(F32)
        mn = ADAM_B1 * m_ref[...] + (1.0 - ADAM_B1) * g
        vn = ADAM_B2 * v_ref[...] + (1.0 - ADAM_B2) * (g * g)
        g_ref[...] = g
        mo_ref[...] = mn
        vo_ref[...] = vn
        d_ref[...] = -ADAM_LR * ((mn * c1) / (jnp.sqrt(vn * c2) + ADAM_EPS) + ADAM_WD * w_ref[...])

    blk = pl.BlockSpec((tr, C), lambda i: (i + off, 0))
    extra = [] if prev is None else list(prev)
    return pl.pallas_call(
        body, name=name, grid=(R // tr,),
        in_specs=[blk, pl.BlockSpec((ns, tr, C), lambda i: (0, i, 0)), blk, blk] + [pl.BlockSpec(memory_space=pl.ANY)] * len(extra),
        out_specs=[blk] * 4, out_shape=[jax.ShapeDtypeStruct((Rtot, C), F32)] * 4,
        input_output_aliases={4 + k: k for k in range(len(extra))})(w, parts, m, v, *extra)


_SMALL = ("ada_b", "norm1_g", "gm_ln_g", "gm_ln_b", "gm_ws", "gm_bs", "gm_norm_g", "attn_sinks", "attn_norm_g", "conv_b",
          "dt_bias", "a_log", "d_skip", "ssm_norm_g", "norm2_g", "final_norm_g")


def _pack(arrs):
    flat = []
    for a in arrs:
        f = a.reshape(-1).astype(F32)
        flat.append(jnp.pad(f, (0, (-f.shape[0]) % 1024)))
    return jnp.concatenate(flat).reshape(-1, 128)


def _unpack(pack, like):
    out, o = [], 0
    flat = pack.reshape(-1)
    for a in like:
        n = int(np.prod(a.shape))
        out.append(flat[o:o + n].reshape(a.shape))
        o += n + (-n) % 1024
    return out


def kernel(x, c, ada_w, ada_b, norm1_g, w_in, gm_ln_g, gm_ln_b, gm_ws, gm_bs, gm_norm_g, attn_sinks, attn_norm_g, conv_w, conv_b, dt_bias, a_log, d_skip, ssm_norm_g, w_out, norm2_g, w_mlp1, w_mlp2, final_norm_g, loss_target, m_ada_w, m_ada_b, m_norm1_g, m_w_in, m_gm_ln_g, m_gm_ln_b, m_gm_ws, m_gm_bs, m_gm_norm_g, m_attn_sinks, m_attn_norm_g, m_conv_w, m_conv_b, m_dt_bias, m_a_log, m_d_skip, m_ssm_norm_g, m_w_out, m_norm2_g, m_w_mlp1, m_w_mlp2, m_final_norm_g, v_ada_w, v_ada_b, v_norm1_g, v_w_in, v_gm_ln_g, v_gm_ln_b, v_gm_ws, v_gm_bs, v_gm_norm_g, v_attn_sinks, v_attn_norm_g, v_conv_w, v_conv_b, v_dt_bias, v_a_log, v_d_skip, v_ssm_norm_g, v_w_out, v_norm2_g, v_w_mlp1, v_w_mlp2, v_final_norm_g):
    args = dict(locals())
    B, S, _ = x.shape
    T = B * S
    L = DEPTH
    me = 4 * lax.axis_index("x") + 2 * lax.axis_index("y") + lax.axis_index("c")

    gath = _gather2([c, conv_w], "ag_c")
    big = ("w_in", "w_out", "w_mlp1", "w_mlp2")
    chain = [(n, l) for l in range(L) for n in ("w_in", "w_mlp1", "w_out", "w_mlp2")]
    inflight = {}

    def start_next(order):
        if not chain:
            return jnp.zeros((8, 128), F32)
        n, l = chain.pop(0)
        sems, v_thru, land_thru, token = _gather_start(shard[n, l], zone[n, l], order, f"ag_start_{n}{l}")
        inflight[n, l] = (sems, v_thru, land_thru)
        return token

    def gathered(n, l, after):
        _, land = _gather_wait(*inflight.pop((n, l)), after, f"ag_wait_{n}{l}")
        return _gather_finish(land, f"ag_fin_{n}{l}")

    me1 = me.astype(jnp.int32).reshape(1)
    shard = {(n, l): args[n][l].astype(BF16) for n, l in chain}
    zone = {k: _landing_zone(v, me1, f"ag_zone_{k[0]}{k[1]}") for k, v in shard.items()}

    c_all = gath[0].reshape(NDEV * B, D)
    c_act = (c_all * jax.nn.sigmoid(c_all)).astype(BF16)
    nb_rows = c_act.shape[0]
    c_pad = jnp.pad(c_act, ((0, 128 - nb_rows), (0, 0)))
    adw = ada_w.astype(BF16)
    mod_part = jnp.stack([_mm(c_pad, adw[l], mode="nn", name=f"mod{l}", tn=768)[:nb_rows] for l in range(L)])
    mod_all = _gather_small([mod_part], "ag_mod")[0]
    mod_mine = lax.dynamic_slice_in_dim(mod_all, me * B, B, axis=2)
    mod = jnp.transpose(mod_mine, (1, 2, 0, 3)).reshape(L, B, 6 * D) + ada_b[:, None, :]
    mods = [[mod[l][:, None, i * D:(i + 1) * D] for i in range(6)] for l in range(L)]

    win_g, wout_g, w1_g, w2_g = [None] * L, [None] * L, [None] * L, [None] * L

    tril = jnp.tril(jnp.ones((128, 128), F32))
    row = lambda a: a.reshape(1, -1)
    pad128 = lambda a: jnp.pad(a.reshape(1, -1), ((0, 0), (0, 128 - a.shape[-1])))
    small = []
    for l in range(L):
        wt = gm_ws[l] * tril
        small.append(dict(
            lng=row(gm_ln_g[l]), lnb=row(gm_ln_b[l]), wt=wt.astype(BF16), wtT=jnp.swapaxes(wt, 1, 2).astype(BF16),
            bsx=jnp.repeat(gm_bs[l].T, 128, axis=1), gog=row(gm_norm_g[l]), sinks=attn_sinks[l], aog=row(attn_norm_g[l]),
            bias=pad128(dt_bias[l]), alog=pad128(a_log[l]), dskE=jnp.repeat(d_skip[l], SSM_HD).reshape(1, SSM_W),
            sng=row(ssm_norm_g[l]), cb=row(conv_b[l])))
    convw_all = jnp.transpose(gath[1], (1, 2, 0, 3)).reshape(L, 4, CONV_CH)
    convw8 = jnp.pad(convw_all, ((0, 0), (0, 4), (0, 0)))

    saved = []
    xl = x
    tok = start_next(mod)
    h = _norm_fwd(xl, row(norm1_g[0]) + tok[0, 0], mods[0][1], mods[0][0], "norm1_f0")
    for l in range(L):
        sm = small[l]
        g_in = gathered("w_in", l, h)
        tok = start_next(g_in)
        win_g[l] = _to_work_cols(jnp.transpose(g_in, (1, 0, 2)).reshape(D, IN_W))
        P = _mm(h.reshape(T, D), win_g[l], mode="nn", name=f"proj_in{l}", tn=1536, order=tok).reshape(B, S, PW)
        out_a = _gmlp_fwd(P, sm["lng"], sm["lnb"], sm["wt"], sm["bsx"], sm["gog"], f"gmlp_f{l}")
        out_b = _attn_fwd(P, sm["sinks"], sm["aog"], f"attn_f{l}")
        xact = _conv_fwd(P, convw8[l], sm["cb"], f"conv_f{l}")
        w1_g[l] = gathered("w_mlp1", l, xact)
        tok = start_next(w1_g[l])
        out_c, sprev = _ssd_fwd(xact, P, sm["bias"], sm["alog"], sm["dskE"], sm["sng"] + tok[0:1, 0:1], f"ssd_f{l}")
        cat = jnp.concatenate([out_a, out_b, out_c], axis=-1)
        g_out = gathered("w_out", l, cat)
        tok = start_next(g_out)
        wout_g[l] = g_out.reshape(D, D)
        mix = _mm(cat.reshape(T, D), wout_g[l], mode="nn", name=f"proj_out{l}", order=tok).reshape(B, S, D)
        x_mid, h2 = _norm_fwd(xl, row(norm2_g[l]), mods[l][4], mods[l][3], f"norm2_f{l}", resid=(mix, mods[l][2]))
        a_act, r_act = _mm(h2.reshape(T, D), w1_g[l], mode="nn", name=f"mlp1_{l}", out_dtypes=(BF16, BF16), col_blocked_b=True,
                           epilogue=lambda acc: (acc, jnp.square(jnp.maximum(acc, 0.0))), tm=1024 * (l + 1))
        g_2 = gathered("w_mlp2", l, r_act)
        tok = start_next(g_2)
        w2_g[l] = g_2.reshape(DFF, D)
        m2 = _mm(r_act, w2_g[l], mode="nn", name=f"mlp2_{l}", order=tok, tk=4096).reshape(B, S, D)
        saved.append(dict(x_in=xl, h=h, P=P, xact=xact, sprev=sprev, cat=cat, mix=mix, x_mid=x_mid, h2=h2, a=a_act, r=r_act, m2=m2))
        if l + 1 < L:
            xl, h = _norm_fwd(x_mid, row(norm1_g[l + 1]), mods[l + 1][1], mods[l + 1][0], f"norm1_f{l + 1}", resid=(m2, mods[l][5]))

    sv = saved[L - 1]
    nb = _norm_bwd(sv["x_mid"], row(final_norm_g), "final_b", tgt=loss_target, br=sv["m2"], gate=mods[L - 1][5], x_is_prev=True)
    loss_part, g_final = nb["loss"], nb["dg"]
    dmod, gsm, gconvw = [None] * L, [None] * L, [None] * L
    core = lax.axis_index("c").astype(jnp.int32).reshape(1)
    reducing = []

    def reduce_start(n, l, p, order):
        from_sib = _pair_exchange([p], f"rs_pair_{n}{l}")[0]
        s, land = _pair_add(p, from_sib, core, f"rs_add_{n}{l}")
        sems, s_thru, land_thru, token = _chipsum_start(s, land, order, f"rs_start_{n}{l}")
        reducing.append((n, l, sems, s_thru, land_thru))
        return token

    for l in reversed(range(L)):
        sv, sm = saved[l], small[l]
        dm2, dxo, dg2 = nb["dbr"].reshape(T, D), nb["dx"], nb["dgate"]
        da = _mm(dm2, w2_g[l], mode="nt", name=f"mlp2_dx{l}", out_dtypes=(BF16,), extras=(sv["a"],),
                 epilogue=lambda acc, a: (acc * (2.0 * jnp.maximum(a.astype(F32), 0.0)),), tm=1024 * (l + 1))
        dw2 = _mm(sv["r"], dm2, mode="tn", name=f"mlp2_dw{l}", out_dtypes=(BF16,), tk=2048,
                  tn=1024 * (l + 1)).reshape(4, 2, DFF // NDEV, D)
        tok = reduce_start("w_mlp2", l, dw2, da)
        dh2 = _mm(da, w1_g[l], mode="nt", name=f"mlp1_dx{l}", col_blocked_b=True, order=tok,
                  **(dict(tm=512, tn=2048) if l else {})).reshape(B, S, D)
        dw1 = _mm(sv["h2"].reshape(T, D), da, mode="tn", name=f"mlp1_dw{l}", out_dtypes=(BF16,), tk=2048,
                  col_blocked_out=True, tm=1024 * (l + 1)).reshape(4, 2, D, DFF // NDEV)
        tok = reduce_start("w_mlp1", l, dw1, dh2)
        nb2 = _norm_bwd(sv["x_mid"], row(norm2_g[l]) + tok[0, 0], f"norm2_b{l}", sc=mods[l][4], dh=dh2, dres=dxo, br=sv["mix"],
                        gate=mods[l][2])
        dmix = nb2["dbr"].reshape(T, D)
        dcat = _mm(dmix, wout_g[l], mode="nt", name=f"proj_out_dx{l}").reshape(B, S, D)
        du, dv, dlng, dlnb, dws, dbsx, dgog = _gmlp_bwd(sv["P"], dcat, sm["lng"], sm["lnb"], sm["wt"], sm["wtT"], sm["bsx"],
                                                        sm["gog"], f"gmlp_b{l}")
        dq, dk, dvv, dsink, daog = _attn_bwd(sv["P"], dcat, sm["sinks"], sm["aog"], f"attn_b{l}")
        dwo = _mm(sv["cat"].reshape(T, D), dmix, mode="tn", name=f"proj_out_dw{l}", out_dtypes=(BF16,), tk=2048,
                  order=dq).reshape(4, 2, D // NDEV, D)
        tok = reduce_start("w_out", l, dwo, dmix)
        dxa, dz, ddt, dbias, dalog, ddsk, dsng = _ssd_bwd(sv["xact"], sv["P"], sv["sprev"], dcat, sm["bias"], sm["alog"],
                                                          sm["dskE"], sm["sng"] + tok[0:1, 0:1], f"ssd_b{l}")
        dxbc, dcw, dcb = _conv_bwd(sv["P"], dxa, convw8[l], sm["cb"], f"conv_b{l}")
        dP = jnp.concatenate([du, dv, dq, dk, dvv, dz, dxbc, ddt, jnp.zeros((B, S, PW - OFF["dt"] - 128), BF16)],
                             axis=-1).reshape(T, PW)
        dwin = _mm(sv["h"].reshape(T, D), dP, mode="tn", name=f"proj_in_dw{l}", out_dtypes=(BF16,), tn=1536, tk=2048)
        dwin = jnp.transpose(_from_work_cols(dwin).reshape(D, NDEV, IN_W // NDEV), (1, 0, 2)).reshape(4, 2, D, IN_W // NDEV)
        tok = reduce_start("w_in", l, dwin, dP)
        dh = _mm(dP, win_g[l], mode="nt", name=f"proj_in_dx{l}", tk=2304, order=tok).reshape(B, S, D)
        nb = _norm_bwd(sv["x_in"], row(norm1_g[l]), f"norm1_b{l}", sc=mods[l][1], dh=dh, dres=nb2["dx"],
                       br=saved[l - 1]["m2"] if l > 0 else None, gate=mods[l - 1][5] if l > 0 else None)
        dmod[l] = jnp.concatenate([nb["dsh"], nb["dsc"], nb2["dgate"], nb2["dsh"], nb2["dsc"], dg2], axis=-1)
        gconvw[l] = dcw[:4]
        gsm[l] = dict(
            ada_b=jnp.sum(dmod[l], axis=(0, 1)), norm1_g=nb["dg"], gm_ln_g=dlng, gm_ln_b=dlnb, gm_ws=dws,
            gm_bs=dbsx.reshape(128, GM_H, 128).sum(-1).T, gm_norm_g=dgog, attn_sinks=dsink[:, 0], attn_norm_g=daog,
            conv_b=dcb, dt_bias=dbias[0, :SSM_H], a_log=dalog[0, :SSM_H], d_skip=ddsk.reshape(SSM_H, SSM_HD).sum(-1),
            ssm_norm_g=dsng, norm2_g=nb2["dg"])
    grad_x = nb["dx"]

    big_res, after = dict.fromkeys(big), grad_x
    tile_rows = dict(w_in=256, w_out=256, w_mlp1=256, w_mlp2=128)

    def finish_reduce(n, l, sems, s_thru, land_thru, after):
        parts = _chipsum_wait(sems, s_thru, land_thru, after, f"rs_wait_{n}{l}")
        w = args[n]
        big_res[n] = _adamw(w.reshape(-1, w.shape[-1]), parts, args["m_" + n].reshape(-1, w.shape[-1]),
                            args["v_" + n].reshape(-1, w.shape[-1]), f"adamw_{n}{l}", tr=tile_rows[n], row0=l * w.shape[1],
                            prev=big_res[n])
        return big_res[n][0]

    for item in reducing[:-1]:
        after = finish_reduce(*item, after)

    per_layer = [n for n in _SMALL if n != "final_norm_g"]
    g_small = [jnp.stack([gsm[l][n].reshape(args[n].shape[1:]) for l in range(L)]) for n in per_layer] + [g_final.reshape(D)]
    zc = jnp.zeros((L, 4, CONV_CH), F32)
    z1 = jnp.zeros((1, 128), F32)
    gpack = _pack([loss_part] + g_small + [jnp.stack(gconvw)])
    got = _gather2([jnp.stack(dmod).reshape(L, B, 6 * D), gpack], "ag_small", order=after)
    like = [z1] + [args[n] for n in _SMALL] + [zc]
    packs = [_pack([z1] + [args[p + n] for n in _SMALL] + [zc]) for p in ("", "m_", "v_")]
    sres = [_unpack(p, like) for p in _adamw(packs[0], got[1], packs[1], packs[2], "adamw_small", tr=gpack.shape[0])]
    res = {n: [r[1 + i] for r in sres] for i, n in enumerate(_SMALL)}
    loss = sres[0][0][0, 0]
    gcw = lax.dynamic_slice_in_dim(sres[0][-1], me * (CONV_CH // NDEV), CONV_CH // NDEV, axis=2)

    def update(name, parts, tr):
        w = args[name]
        r = _adamw(w.reshape(-1, w.shape[-1]), parts, args["m_" + name].reshape(-1, w.shape[-1]),
                   args["v_" + name].reshape(-1, w.shape[-1]), "adamw_" + name, tr=tr)
        res[name] = [a.reshape(w.shape) for a in r]

    update("conv_w", gcw.reshape(1, L * 4, CONV_CH // NDEV), L * 4)

    dmod_all = jnp.transpose(got[0], (1, 0, 2, 3)).reshape(L, NDEV * B, 6 * D)
    dm_mine = lax.dynamic_slice_in_dim(dmod_all, me * (6 * D // NDEV), 6 * D // NDEV, axis=2)
    dm_pad = jnp.pad(dm_mine, ((0, 0), (0, 128 - nb_rows), (0, 0))).astype(BF16)
    g_adaw = jnp.stack([_mm(c_pad, dm_pad[l], mode="tn", name=f"ada_dw{l}", tn=768) for l in range(L)])
    update("ada_w", g_adaw.reshape(1, L * D, 6 * D // NDEV), 256)

    finish_reduce(*reducing[-1], res["ada_w"][0])
    for n in big:
        res[n] = [a.reshape(args[n].shape) for a in big_res[n]]

    names = ['ada_w', 'ada_b', 'norm1_g', 'w_in', 'gm_ln_g', 'gm_ln_b', 'gm_ws', 'gm_bs', 'gm_norm_g', 'attn_sinks',
             'attn_norm_g', 'conv_w', 'conv_b', 'dt_bias', 'a_log', 'd_skip', 'ssm_norm_g', 'w_out', 'norm2_g', 'w_mlp1',
             'w_mlp2', 'final_norm_g']
    return (loss, grad_x, *[res[n][0] for n in names], *[res[n][1] for n in names], *[res[n][2] for n in names],
            *[res[n][3] for n in names])
```

```python
import functools

import jax
import jax.numpy as jnp
import numpy as np
from jax import lax
from jax.experimental import pallas as pl
from jax.experimental.pallas import tpu as pltpu

F32, BF16 = jnp.float32, jnp.bfloat16
HI = lax.Precision.HIGHEST
MESH = pl.DeviceIdType.MESH
NDEV = 8

D = 2048
DEPTH = 2
CHUNK = 128
GM_W, GM_H = 512, 4
ATT_W, KV_W, ATT_H = 512, 128, 8
SSM_W, SSM_H, SSM_HD, SSM_G = 1024, 16, 64, 2
CONV_CH = 1536
IN_W = 4368
DFF = 8192
EPS = 1e-6
NEG_INF = -1e30
GELU_K = 0.7978845608028654
GELU_C = 0.044715

_ORIG = (("u", 512), ("v", 512), ("q", 512), ("k", 128), ("vv", 128), ("z", 1024), ("xbc", 1536), ("dt", 16))
OFF = dict(u=0, v=512, q=1024, k=1536, vv=1664, z=1792, xbc=2816, dt=4352)
PW = 4608

ADAM_LR, ADAM_B1, ADAM_B2, ADAM_EPS, ADAM_WD, ADAM_STEP = 0.001, 0.9, 0.999, 1e-08, 0.01, 10


def _to_work_cols(w):
    return jnp.pad(w, [(0, 0)] * (w.ndim - 1) + [(0, PW - IN_W)])


def _from_work_cols(wp):
    return wp[..., :IN_W]


def _sigmoid(x):
    return 1.0 / (1.0 + jnp.exp(-x))


def _gelu(x):
    return 0.5 * x * (1.0 + jnp.tanh(GELU_K * (x + GELU_C * x * x * x)))


def _gelu_grad(x):
    t = jnp.tanh(GELU_K * (x + GELU_C * x * x * x))
    return 0.5 * (1.0 + t) + 0.5 * x * (1.0 - t * t) * GELU_K * (1.0 + 3.0 * GELU_C * x * x)


def _dot(a, b, prec=None):
    return jnp.dot(a, b, precision=prec, preferred_element_type=F32)


def _dot_nt(a, b, prec=None):
    return lax.dot_general(a, b, (((1,), (1,)), ((), ())), precision=prec, preferred_element_type=F32)


def _dot_tn(a, b, prec=None):
    return lax.dot_general(a, b, (((0,), (0,)), ((), ())), precision=prec, preferred_element_type=F32)


def _full(shape):
    return pl.BlockSpec(shape, lambda *_: (0,) * len(shape))


_HBM = pl.BlockSpec(memory_space=pltpu.HBM)


def _me():
    return lax.axis_index("x"), lax.axis_index("y"), lax.axis_index("c")


def _peer(k):
    x, y, c = _me()
    px = 1 - x if k & 4 else x
    py = 1 - y if k & 2 else y
    pc = 1 - c if k & 1 else c
    return (px, py, pc), 4 * px + 2 * py + pc


def _gather_small(xs, name, order=None):
    n = len(xs)

    def body(*refs):
        ins, outs = refs[:n], refs[-n - 3:-3]
        send, recv, loc = refs[-3:]
        x, y, c = _me()
        me = 4 * x + 2 * y + c
        started = []
        for i in range(n):
            own = pltpu.make_async_copy(ins[i], outs[i].at[me], loc.at[i])
            own.start()
            started.append(own)
        for k in range(1, NDEV):
            dev, lin = _peer(k)
            for i in range(n):
                pltpu.make_async_remote_copy(
                    src_ref=ins[i], dst_ref=outs[i].at[me],
                    send_sem=send.at[i, k - 1], recv_sem=recv.at[i, k - 1], device_id=dev, device_id_type=MESH).start()
        for k in range(1, NDEV):
            dev, lin = _peer(k)
            for i in range(n):
                pltpu.make_async_remote_copy(
                    src_ref=ins[i], dst_ref=outs[i].at[lin],
                    send_sem=send.at[i, k - 1], recv_sem=recv.at[i, k - 1], device_id=dev, device_id_type=MESH).wait()
        for own in started:
            own.wait()

    extra = [] if order is None else [order]
    return pl.pallas_call(
        body, name=name, out_shape=[jax.ShapeDtypeStruct((NDEV,) + a.shape, a.dtype) for a in xs],
        in_specs=[_HBM] * n + [pl.BlockSpec(memory_space=pl.ANY)] * len(extra), out_specs=[_HBM] * n,
        scratch_shapes=[pltpu.SemaphoreType.DMA((n, NDEV - 1)), pltpu.SemaphoreType.DMA((n, NDEV - 1)),
                        pltpu.SemaphoreType.DMA((n,))],
        compiler_params=pltpu.CompilerParams(has_side_effects=True),
    )(*xs, *extra)


def _chips():
    x, y, c = _me()
    return x, y, c, [(1 - x, y), (x, 1 - y), (1 - x, 1 - y)]


def _gather2(xs, name, order=None):
    n = len(xs)
    extra = [] if order is None else [order]

    def body(*refs):
        ins, outs = refs[:n], refs[-n - 3:-3]
        send, recv, loc = refs[-3:]
        x, y, c, chips = _chips()
        me, sib = (x, y, c), (x, y, 1 - c)

        def cp(i, k, block, to, src=None):
            slot = outs[i].at[4 * block[0] + 2 * block[1] + block[2]]
            return pltpu.make_async_remote_copy(src_ref=slot if src is None else src, dst_ref=slot, send_sem=send.at[i, k],
                                                recv_sem=recv.at[i, k], device_id=to, device_id_type=MESH)

        sent = []
        for i in range(n):
            for j, chip in enumerate(chips):
                sent.append(cp(i, 1 + j, me, (*chip, c), src=ins[i]))
            sent.append(cp(i, 0, me, sib, src=ins[i]))
        for s in sent:
            s.start()
        own = [pltpu.make_async_copy(ins[i], outs[i].at[4 * x + 2 * y + c], loc.at[i]) for i in range(n)]
        for o in own:
            o.start()
        for j, chip in enumerate(chips):
            for i in range(n):
                cp(i, 1 + j, (*chip, c), me).wait_recv()
                fwd = cp(i, 4 + j, (*chip, c), sib)
                fwd.start()
                sent.append(fwd)
        for i in range(n):
            cp(i, 0, sib, me).wait_recv()
            for j, chip in enumerate(chips):
                cp(i, 4 + j, (*chip, 1 - c), me).wait_recv()
        for s in sent:
            s.wait_send()
        for o in own:
            o.wait()

    return pl.pallas_call(
        body, name=name, out_shape=[jax.ShapeDtypeStruct((NDEV,) + a.shape, a.dtype) for a in xs],
        in_specs=[_HBM] * n + [pl.BlockSpec(memory_space=pl.ANY)] * len(extra), out_specs=[_HBM] * n,
        scratch_shapes=[pltpu.SemaphoreType.DMA((n, 7)), pltpu.SemaphoreType.DMA((n, 7)), pltpu.SemaphoreType.DMA((n,))],
        compiler_params=pltpu.CompilerParams(has_side_effects=True),
    )(*xs, *extra)


def _pair_exchange(ps, name):
    n = len(ps)

    def body(*refs):
        ins, outs = refs[:n], refs[n:2 * n]
        send, recv = refs[2 * n:]
        x, y, c = _me()
        cps = [pltpu.make_async_remote_copy(src_ref=ins[i].at[ch, 1 - c], dst_ref=outs[i].at[ch], send_sem=send.at[i, ch],
                                            recv_sem=recv.at[i, ch], device_id=(x, y, 1 - c), device_id_type=MESH)
               for i in range(n) for ch in range(4)]
        for cp in cps:
            cp.start()
        for cp in cps:
            cp.wait()

    return pl.pallas_call(
        body, name=name, out_shape=[jax.ShapeDtypeStruct((4,) + a.shape[2:], a.dtype) for a in ps],
        in_specs=[_HBM] * n, out_specs=[_HBM] * n,
        scratch_shapes=[pltpu.SemaphoreType.DMA((n, 4)), pltpu.SemaphoreType.DMA((n, 4))],
        compiler_params=pltpu.CompilerParams(has_side_effects=True),
    )(*ps)


def _pair_add(p, r1, core, name, tr=256):
    _, _, R, C = p.shape
    tr = min(tr, R)

    def body(core_ref, p_ref, r_ref, o_ref, o2_ref):
        s = (p_ref[...].astype(F32) + r_ref[...].astype(F32)).astype(o_ref.dtype)
        o_ref[...] = s
        o2_ref[...] = s

    blk = pl.BlockSpec((None, tr, C), lambda ch, i, core_ref: (ch, i, 0))
    return pl.pallas_call(
        body, name=name, out_shape=[jax.ShapeDtypeStruct((4, R, C), p.dtype)] * 2,
        grid_spec=pltpu.PrefetchScalarGridSpec(
            num_scalar_prefetch=1, grid=(4, R // tr),
            in_specs=[pl.BlockSpec((None, None, tr, C), lambda ch, i, core_ref: (ch, core_ref[0], i, 0)), blk],
            out_specs=[blk, blk]),
    )(core, p, r1)


_SEM = pl.BlockSpec(memory_space=pltpu.SEMAPHORE)
_ANY = pl.BlockSpec(memory_space=pl.ANY)
_DATAFLOW = pltpu.SideEffectType.DATAFLOW_SIDE_EFFECTING


def _hbm(a):
    return pltpu.with_memory_space_constraint(a, pltpu.HBM)


def _gather_targets():
    x, y, c, chips = _chips()
    return 4 * x + 2 * y + c, [(x, y, 1 - c)] + [(*chip, c) for chip in chips]


def _landing_zone(v, me, name, tr=512):
    R, C = v.shape
    tr = min(tr, R)

    def body(me_ref, v_ref, o_ref):
        o_ref[...] = v_ref[...]

    return pl.pallas_call(
        body, name=name, out_shape=jax.ShapeDtypeStruct((NDEV, R, C), v.dtype),
        grid_spec=pltpu.PrefetchScalarGridSpec(
            num_scalar_prefetch=1, grid=(R // tr,), in_specs=[pl.BlockSpec((tr, C), lambda i, me_ref: (i, 0))],
            out_specs=pl.BlockSpec((None, tr, C), lambda i, me_ref: (me_ref[0], i, 0))),
    )(me, v)


def _gather_start(v, land, order, name):
    def body(v_ref, land_ref, order_ref, *rest):
        sems, token = rest[:8], rest[10]
        me, targets = _gather_targets()
        for k, to in enumerate(targets):
            pltpu.make_async_remote_copy(src_ref=v_ref, dst_ref=land_ref.at[me], send_sem=sems[k], recv_sem=sems[4 + k],
                                         device_id=to, device_id_type=MESH).start()
        token[...] = jnp.zeros_like(token)

    outs = pl.pallas_call(
        body, name=name,
        out_shape=(pltpu.SemaphoreType.DMA(()),) * 8 + (pltpu.HBM(v.shape, v.dtype), pltpu.HBM(land.shape, land.dtype),
                                                        jax.ShapeDtypeStruct((8, 128), F32)),
        in_specs=(_HBM, _HBM, _ANY), out_specs=(_SEM,) * 8 + (_HBM, _HBM, pl.BlockSpec(memory_space=pltpu.VMEM)),
        input_output_aliases={0: 8, 1: 9}, compiler_params=pltpu.CompilerParams(has_side_effects=_DATAFLOW),
    )(_hbm(v), _hbm(land), order)
    return outs[:8], outs[8], outs[9], outs[10]


def _gather_wait(sems, v_thru, land_thru, after, name):
    def body(v_ref, land_ref, *rest):
        sems_ = rest[:8]
        me, targets = _gather_targets()
        for k, to in enumerate(targets):
            cp = pltpu.make_async_remote_copy(src_ref=v_ref, dst_ref=land_ref.at[me], send_sem=sems_[k], recv_sem=sems_[4 + k],
                                              device_id=to, device_id_type=MESH)
            cp.wait_send()
            cp.wait_recv()

    return pl.pallas_call(
        body, name=name, out_shape=(pltpu.HBM(v_thru.shape, v_thru.dtype), pltpu.HBM(land_thru.shape, land_thru.dtype)),
        in_specs=(_HBM, _HBM) + (_SEM,) * 8 + (_ANY,), out_specs=(_HBM, _HBM), input_output_aliases={0: 0, 1: 1},
        compiler_params=pltpu.CompilerParams(has_side_effects=_DATAFLOW),
    )(v_thru, land_thru, *sems, after)


def _gather_finish(land, name):
    def body(land_ref, out, send, recv):
        x, y, c, chips = _chips()
        fwd = [pltpu.make_async_remote_copy(src_ref=out.at[4 * px + 2 * py + c], dst_ref=out.at[4 * px + 2 * py + c],
                                            send_sem=send.at[j], recv_sem=recv.at[j], device_id=(x, y, 1 - c), device_id_type=MESH)
               for j, (px, py) in enumerate(chips)]
        for cp in fwd:
            cp.start()
        for j, (px, py) in enumerate(chips):
            slot = out.at[4 * px + 2 * py + 1 - c]
            pltpu.make_async_remote_copy(src_ref=slot, dst_ref=slot, send_sem=send.at[j], recv_sem=recv.at[j],
                                         device_id=(x, y, 1 - c), device_id_type=MESH).wait()

    return pl.pallas_call(
        body, name=name, out_shape=jax.ShapeDtypeStruct(land.shape, land.dtype),
        in_specs=[_HBM], out_specs=_HBM, input_output_aliases={0: 0},
        scratch_shapes=[pltpu.SemaphoreType.DMA((3,)), pltpu.SemaphoreType.DMA((3,))],
        compiler_params=pltpu.CompilerParams(has_side_effects=True),
    )(land)


def _chip_targets():
    x, y, c, chips = _chips()
    return 2 * x + y, [((px, py, c), 2 * px + py) for px, py in chips]


def _chipsum_start(s, land, order, name):
    def body(s_ref, land_ref, order_ref, *rest):
        sems, token = rest[:6], rest[8]
        mine, targets = _chip_targets()
        for k, (to, ch) in enumerate(targets):
            pltpu.make_async_remote_copy(src_ref=s_ref.at[ch], dst_ref=land_ref.at[mine], send_sem=sems[k], recv_sem=sems[3 + k],
                                         device_id=to, device_id_type=MESH).start()
        token[...] = jnp.zeros_like(token)

    outs = pl.pallas_call(
        body, name=name,
        out_shape=(pltpu.SemaphoreType.DMA(()),) * 6 + (pltpu.HBM(s.shape, s.dtype), pltpu.HBM(land.shape, land.dtype),
                                                        jax.ShapeDtypeStruct((8, 128), F32)),
        in_specs=(_HBM, _HBM, _ANY), out_specs=(_SEM,) * 6 + (_HBM, _HBM, pl.BlockSpec(memory_space=pltpu.VMEM)),
        input_output_aliases={0: 6, 1: 7}, compiler_params=pltpu.CompilerParams(has_side_effects=_DATAFLOW),
    )(_hbm(s), _hbm(land), order)
    return outs[:6], outs[6], outs[7], outs[8]


def _chipsum_wait(sems, s_thru, land_thru, after, name):
    def body(s_ref, land_ref, *rest):
        sems_ = rest[:6]
        mine, targets = _chip_targets()
        for k, (to, ch) in enumerate(targets):
            cp = pltpu.make_async_remote_copy(src_ref=s_ref.at[ch], dst_ref=land_ref.at[ch], send_sem=sems_[k], recv_sem=sems_[3 + k],
                                              device_id=to, device_id_type=MESH)
            cp.wait_send()
            cp.wait_recv()

    return pl.pallas_call(
        body, name=name, out_shape=(pltpu.HBM(s_thru.shape, s_thru.dtype), pltpu.HBM(land_thru.shape, land_thru.dtype)),
        in_specs=(_HBM, _HBM) + (_SEM,) * 6 + (_ANY,), out_specs=(_HBM, _HBM), input_output_aliases={0: 0, 1: 1},
        compiler_params=pltpu.CompilerParams(has_side_effects=_DATAFLOW),
    )(s_thru, land_thru, *sems, after)[1]


def _sibling_start(p, order, name):
    def body(p_ref, land_ref, order_ref, send_sem, recv_sem, p_thru, land_thru, token):
        x, y, c = _me()
        pltpu.make_async_remote_copy(src_ref=p_ref, dst_ref=land_ref, send_sem=send_sem, recv_sem=recv_sem,
                                     device_id=(x, y, 1 - c), device_id_type=MESH).start()
        token[...] = jnp.zeros_like(token)

    land = lax.empty(p.shape, p.dtype)
    outs = pl.pallas_call(
        body, name=name,
        out_shape=(pltpu.SemaphoreType.DMA(()),) * 2 + (pltpu.HBM(p.shape, p.dtype), pltpu.HBM(p.shape, p.dtype),
                                                        jax.ShapeDtypeStruct((8, 128), F32)),
        in_specs=(_HBM, _HBM, _ANY), out_specs=(_SEM,) * 2 + (_HBM, _HBM, pl.BlockSpec(memory_space=pltpu.VMEM)),
        input_output_aliases={0: 2, 1: 3}, compiler_params=pltpu.CompilerParams(has_side_effects=_DATAFLOW),
    )(_hbm(p), _hbm(land), order)
    return outs[:2], outs[2], outs[3], outs[4]


def _sibling_wait(sems, p_thru, land_thru, after, name):
    def body(p_ref, land_ref, send_sem, recv_sem, after_ref, p_dead, got_ref):
        x, y, c = _me()
        cp = pltpu.make_async_remote_copy(src_ref=p_ref, dst_ref=land_ref, send_sem=send_sem, recv_sem=recv_sem,
                                          device_id=(x, y, 1 - c), device_id_type=MESH)
        cp.wait_send()
        cp.wait_recv()

    return pl.pallas_call(
        body, name=name, out_shape=(pltpu.HBM(p_thru.shape, p_thru.dtype), pltpu.HBM(land_thru.shape, land_thru.dtype)),
        in_specs=(_HBM, _HBM, _SEM, _SEM, _ANY), out_specs=(_HBM, _HBM), input_output_aliases={0: 0, 1: 1},
        compiler_params=pltpu.CompilerParams(has_side_effects=_DATAFLOW),
    )(p_thru, land_thru, *sems, after)[1]


def _mm(a, b, *, mode, name, out_dtypes=(F32,), epilogue=None, extras=(), tm=1024, tn=1024, tk=2048,
        col_blocked_b=False, col_blocked_out=False, order=None):
    CB = 1024
    if col_blocked_b:
        assert mode in ("nn", "nt") and b.shape[2] == CB
        (M, K), N = a.shape, (b.shape[0] * CB if mode == "nn" else b.shape[1])
        assert mode == "nn" or tk % CB == 0
        tn = CB if mode == "nn" else tn
    elif mode == "nn":
        (M, K), N = a.shape, b.shape[1]
    elif mode == "nt":
        (M, K), N = a.shape, b.shape[0]
    else:
        (K, M), N = a.shape, b.shape[1]
    if col_blocked_out:
        assert len(out_dtypes) == 1 and N % CB == 0
        tn = CB
    tm, tn, tk = min(tm, M), min(tn, N), min(tk, K)
    assert M % tm == 0 and N % tn == 0 and K % tk == 0, (M, N, K, tm, tn, tk)
    nk = K // tk
    ne, no = len(extras), len(out_dtypes)
    dims = {"nn": (((1,), (0,)), ((), ())), "nt": (((1,), (1,)), ((), ())), "tn": (((0,), (0,)), ((), ()))}[mode]

    no_ = 0 if order is None else 1

    def body(a_ref, b_ref, *rest):
        rest = rest[no_:]
        ex, outs = rest[:ne], rest[ne:ne + no]

        def finish(acc):
            res = epilogue(acc, *[e[...] for e in ex]) if epilogue is not None else (acc,)
            for o, r in zip(outs, res):
                o[...] = r.astype(o.dtype)

        if col_blocked_b and mode == "nt":
            part = sum(lax.dot_general(a_ref[:, q * CB:(q + 1) * CB], b_ref[q], dims, preferred_element_type=F32)
                       for q in range(tk // CB))
        else:
            part = lax.dot_general(a_ref[...], b_ref[...], dims, preferred_element_type=F32)
        if nk == 1:
            finish(part)
        else:
            acc_ref = rest[-1]
            k = pl.program_id(2)

            @pl.when(k == 0)
            def _():
                acc_ref[...] = part

            @pl.when(k > 0)
            def _():
                acc_ref[...] += part

            @pl.when(k == nk - 1)
            def _():
                finish(acc_ref[...])

    a_spec = {"nn": pl.BlockSpec((tm, tk), lambda i, j, k: (i, k)), "nt": pl.BlockSpec((tm, tk), lambda i, j, k: (i, k)),
              "tn": pl.BlockSpec((tk, tm), lambda i, j, k: (k, i))}[mode]
    b_spec = {"nn": pl.BlockSpec((tk, tn), lambda i, j, k: (k, j)), "nt": pl.BlockSpec((tn, tk), lambda i, j, k: (j, k)),
              "tn": pl.BlockSpec((tk, tn), lambda i, j, k: (k, j))}[mode]
    if col_blocked_b:
        b_spec = (pl.BlockSpec((None, tk, CB), lambda i, j, k: (j, k, 0)) if mode == "nn"
                  else pl.BlockSpec((tk // CB, tn, CB), lambda i, j, k: (k, j, 0)))
    e_spec = pl.BlockSpec((tm, tn), lambda i, j, k: (i, j))
    o_spec, o_dims = e_spec, (M, N)
    if col_blocked_out:
        o_spec, o_dims = pl.BlockSpec((None, tm, CB), lambda i, j, k: (j, i, 0)), (N // CB, M, CB)
    outs = pl.pallas_call(
        body, name=name, grid=(M // tm, N // tn, nk),
        in_specs=[a_spec, b_spec] + [_ANY] * no_ + [e_spec] * ne, out_specs=[o_spec] * no,
        out_shape=[jax.ShapeDtypeStruct(o_dims, dt) for dt in out_dtypes],
        scratch_shapes=[pltpu.VMEM((tm, tn), F32)] if nk > 1 else [],
        compiler_params=pltpu.CompilerParams(dimension_semantics=("parallel", "parallel", "arbitrary")),
    )(a, b, *([] if order is None else [order]), *extras)
    return outs if no > 1 else outs[0]


def _dw_half(a, b, side, *, axis, name, add=None, order=None, tile=1024, tk=2048):
    (K, M), N = a.shape, b.shape[1]
    tk = min(tk, K)
    nk = K // tk
    if axis == "m":
        tm, tn = tile, min(N, 1024)
        grid, o_dims = (4, N // tn, nk), (4, tile, N)
        a_spec = pl.BlockSpec((tk, tm), lambda q, j, k, s: (k, 2 * q + s[0]))
        b_spec = pl.BlockSpec((tk, tn), lambda q, j, k, s: (k, j))
        o_spec = pl.BlockSpec((None, tm, tn), lambda q, j, k, s: (q, 0, j))
    else:
        tm, tn = min(M, 1024), tile
        grid, o_dims = (M // tm, 4, nk), (4, M, tile)
        a_spec = pl.BlockSpec((tk, tm), lambda i, q, k, s: (k, i))
        b_spec = pl.BlockSpec((tk, tn), lambda i, q, k, s: (k, 2 * q + s[0]))
        o_spec = pl.BlockSpec((None, tm, tn), lambda i, q, k, s: (q, i, 0))
    n_order, n_add = int(order is not None), int(add is not None)
    n_out = 1 + n_add

    def body(s_ref, a_ref, b_ref, *rest):
        rest = rest[n_order:]
        outs, acc_ref = rest[n_add:n_add + n_out], rest[-1]
        k = pl.program_id(2)
        part = _dot_tn(a_ref[...], b_ref[...])

        @pl.when(k == 0)
        def _():
            acc_ref[...] = part

        @pl.when(k > 0)
        def _():
            acc_ref[...] += part

        @pl.when(k == nk - 1)
        def _():
            res = acc_ref[...] + rest[0][...].astype(F32) if n_add else acc_ref[...]
            for o in outs:
                o[...] = res.astype(o.dtype)

    outs = pl.pallas_call(
        body, name=name, out_shape=[jax.ShapeDtypeStruct(o_dims, BF16)] * n_out,
        grid_spec=pltpu.PrefetchScalarGridSpec(
            num_scalar_prefetch=1, grid=grid, in_specs=[a_spec, b_spec] + [_ANY] * n_order + [o_spec] * n_add,
            out_specs=[o_spec] * n_out, scratch_shapes=[pltpu.VMEM((tm, tn), F32)]),
        compiler_params=pltpu.CompilerParams(dimension_semantics=("arbitrary", "arbitrary", "arbitrary")),
    )(side, a, b, *([order] if n_order else []), *([add] if n_add else []))
    return outs if n_add else outs[0]


def _norm_fwd(x, g, sc, sh, name, resid=None):
    B, S, Dm = x.shape
    ts = min(S, 256)
    tok = pl.BlockSpec((None, ts, Dm), lambda b, i: (b, i, 0))
    row = pl.BlockSpec((None, 1, Dm), lambda b, i: (b, 0, 0))
    par = pl.BlockSpec((1, Dm), lambda b, i: (0, 0))

    def body(*refs):
        if resid is not None:
            x_ref, br_ref, gt_ref, g_ref, sc_ref, sh_ref, xo_ref, h_ref = refs
            xv = x_ref[...] + gt_ref[...] * br_ref[...]
            xo_ref[...] = xv
        else:
            x_ref, g_ref, sc_ref, sh_ref, h_ref = refs
            xv = x_ref[...]
        r = lax.rsqrt(jnp.mean(xv * xv, axis=-1, keepdims=True) + EPS)
        h_ref[...] = ((xv * r * g_ref[...]) * (1.0 + sc_ref[...]) + sh_ref[...]).astype(BF16)

    h_shape = jax.ShapeDtypeStruct((B, S, Dm), BF16)
    if resid is not None:
        return pl.pallas_call(body, name=name, grid=(B, S // ts), in_specs=[tok, tok, row, par, row, row],
                              out_specs=[tok, tok], out_shape=[jax.ShapeDtypeStruct((B, S, Dm), F32), h_shape],
                              )(x, resid[0], resid[1], g, sc, sh)
    return pl.pallas_call(body, name=name, grid=(B, S // ts), in_specs=[tok, par, row, row], out_specs=tok,
                          out_shape=h_shape)(x, g, sc, sh)


def _norm_bwd(x, g, name, *, sc=None, dh=None, dres=None, tgt=None, br=None, gate=None, x_is_prev=False):
    B, S, Dm = x.shape
    ts = min(S, 256)
    final = tgt is not None
    has_br = br is not None
    tok = pl.BlockSpec((None, ts, Dm), lambda b, i: (b, i, 0))
    row = pl.BlockSpec((None, 1, Dm), lambda b, i: (b, 0, 0))
    par = pl.BlockSpec((1, Dm), lambda b, i: (0, 0))
    ins, in_specs = [x, g], [tok, par]
    if final:
        ins, in_specs = ins + [tgt], in_specs + [tok]
    else:
        ins, in_specs = ins + [sc, dh], in_specs + [row, tok]
    if dres is not None:
        ins, in_specs = ins + [dres], in_specs + [tok]
    if has_br:
        ins, in_specs = ins + [br, gate], in_specs + [tok, row]
    n_in = len(ins)
    out_shape = [jax.ShapeDtypeStruct((B, S, Dm), F32), jax.ShapeDtypeStruct((1, Dm), F32)]
    out_specs = [tok, par]
    if final:
        out_shape.append(jax.ShapeDtypeStruct((1, 128), F32))
        out_specs.append(pl.BlockSpec((1, 128), lambda b, i: (0, 0)))
    else:
        out_shape += [jax.ShapeDtypeStruct((B, 1, Dm), F32)] * 2
        out_specs += [row, row]
    if has_br:
        out_shape += [jax.ShapeDtypeStruct((B, S, Dm), BF16), jax.ShapeDtypeStruct((B, 1, Dm), F32)]
        out_specs += [tok, row]

    def body(*refs):
        it = iter(refs[:n_in])
        outs = iter(refs[n_in:])
        x_ref, g_ref = next(it), next(it)
        b, i = pl.program_id(0), pl.program_id(1)
        first, first_row = (b == 0) & (i == 0), i == 0
        xv, gv = x_ref[...], g_ref[...]
        if x_is_prev:
            xv = xv + refs[n_in - 1][...] * refs[n_in - 2][...]
        r = lax.rsqrt(jnp.mean(xv * xv, axis=-1, keepdims=True) + EPS)
        n = xv * r
        dx_ref, dg_ref = next(outs), next(outs)

        def acc(ref, val, init):
            @pl.when(init)
            def _():
                ref[...] = val

            @pl.when(jnp.logical_not(init))
            def _():
                ref[...] += val

        if final:
            t_ref = next(it)
            loss_ref = next(outs)
            e = n * gv - t_ref[...]
            acc(loss_ref, jnp.zeros((1, 128), F32) + 0.5 * jnp.sum(e * e) / Dm, first)
            dyg = e * (1.0 / Dm)
        else:
            sc_ref, dh_ref = next(it), next(it)
            dsc_ref, dsh_ref = next(outs), next(outs)
            dhv = dh_ref[...]
            acc(dsh_ref, jnp.sum(dhv, axis=0, keepdims=True), first_row)
            acc(dsc_ref, jnp.sum(dhv * (n * gv), axis=0, keepdims=True), first_row)
            dyg = dhv * (1.0 + sc_ref[...])
        acc(dg_ref, jnp.sum(dyg * n, axis=0, keepdims=True), first)
        dn = dyg * gv
        dx = r * (dn - n * jnp.mean(dn * n, axis=-1, keepdims=True))
        if dres is not None:
            dx = dx + next(it)[...]
        dx_ref[...] = dx
        if has_br:
            br_ref, gt_ref = next(it), next(it)
            dbr_ref, dgt_ref = next(outs), next(outs)
            dbr_ref[...] = (dx * gt_ref[...]).astype(BF16)
            acc(dgt_ref, jnp.sum(dx * br_ref[...], axis=0, keepdims=True), first_row)

    outs = pl.pallas_call(body, name=name, grid=(B, S // ts), in_specs=in_specs, out_specs=out_specs, out_shape=out_shape,
                          compiler_params=pltpu.CompilerParams(dimension_semantics=("arbitrary", "arbitrary")))(*ins)
    res = dict(dx=outs[0], dg=outs[1])
    if final:
        res["loss"] = outs[2]
    else:
        res["dsc"], res["dsh"] = outs[2], outs[3]
    if has_br:
        res["dbr"], res["dgate"] = outs[-2], outs[-1]
    return res


def _gm_heads(vg, lng, lnb):
    res = []
    for h in range(GM_H):
        sl = slice(h * 128, (h + 1) * 128)
        vh = vg[:, sl]
        xc = vh - jnp.mean(vh, axis=-1, keepdims=True)
        rstd = lax.rsqrt(jnp.mean(xc * xc, axis=-1, keepdims=True) + 1e-5)
        xhat = xc * rstd
        res.append((xhat, rstd, xhat * lng[:, sl] + lnb[:, sl]))
    return res


def _gm_gate(heads, wt_ref, bsx, nch):
    cols = []
    for h in range(GM_H):
        vn = heads[h][2].astype(BF16)
        rows = [_dot(wt_ref[h], vn[c * CHUNK:(c + 1) * CHUNK]) + bsx[:, h * 128:(h + 1) * 128] for c in range(nch)]
        cols.append(jnp.concatenate(rows, axis=0) if nch > 1 else rows[0])
    return jnp.concatenate(cols, axis=1)


def _gm_specs(S):
    tb = min(S, 512)
    u = pl.BlockSpec((None, tb, GM_W), lambda b, i: (b, i, OFF["u"] // GM_W))
    v = pl.BlockSpec((None, tb, GM_W), lambda b, i: (b, i, OFF["v"] // GM_W))
    tok = pl.BlockSpec((None, tb, GM_W), lambda b, i: (b, i, 0))
    return tb, u, v, tok


def _gmlp_fwd(P, lng, lnb, wt, bsx, og, name):
    B, S, _ = P.shape
    tb, u_spec, v_spec, tok = _gm_specs(S)
    nch = tb // CHUNK

    def body(u_ref, v_ref, lng_ref, lnb_ref, wt_ref, bsx_ref, og_ref, o_ref):
        heads = _gm_heads(_gelu(v_ref[...]), lng_ref[...], lnb_ref[...])
        y = _gelu(u_ref[...]) * _gm_gate(heads, wt_ref, bsx_ref[...], nch)
        r = lax.rsqrt(jnp.mean(y * y, axis=-1, keepdims=True) + EPS)
        o_ref[...] = (y * r * og_ref[...]).astype(BF16)

    return pl.pallas_call(
        body, name=name, grid=(B, S // tb),
        in_specs=[u_spec, v_spec, _full((1, GM_W)), _full((1, GM_W)), _full((GM_H, 128, 128)), _full((128, GM_W)), _full((1, GM_W))],
        out_specs=tok, out_shape=jax.ShapeDtypeStruct((B, S, GM_W), BF16))(P, P, lng, lnb, wt, bsx, og)


def _gmlp_bwd(P, dcat, lng, lnb, wt, wtT, bsx, og, name):
    B, S, _ = P.shape
    tb, u_spec, v_spec, tok = _gm_specs(S)
    nch = tb // CHUNK
    do_spec = pl.BlockSpec((None, tb, GM_W), lambda b, i: (b, i, 0))

    def body(u_ref, v_ref, do_ref, lng_ref, lnb_ref, wt_ref, wtT_ref, bsx_ref, og_ref,
             du_ref, dv_ref, dlng_ref, dlnb_ref, dws_ref, dbsx_ref, dog_ref):
        first = (pl.program_id(0) == 0) & (pl.program_id(1) == 0)

        @pl.when(first)
        def _():
            for ref in (dlng_ref, dlnb_ref, dws_ref, dbsx_ref, dog_ref):
                ref[...] = jnp.zeros(ref.shape, F32)

        u, v, lng = u_ref[...], v_ref[...], lng_ref[...]
        ug = _gelu(u)
        heads = _gm_heads(_gelu(v), lng, lnb_ref[...])
        gate = _gm_gate(heads, wt_ref, bsx_ref[...], nch)
        y = ug * gate
        r = lax.rsqrt(jnp.mean(y * y, axis=-1, keepdims=True) + EPS)
        yn = y * r
        dout = do_ref[...]
        dog_ref[...] += jnp.sum(dout * yn, axis=0, keepdims=True)
        dyn = dout * og_ref[...]
        dy = r * (dyn - yn * jnp.mean(dyn * yn, axis=-1, keepdims=True))
        du_ref[...] = (dy * gate * _gelu_grad(u)).astype(BF16)
        dgate = dy * ug
        tril = lax.broadcasted_iota(jnp.int32, (128, 128), 0) >= lax.broadcasted_iota(jnp.int32, (128, 128), 1)
        dvg = []
        for h in range(GM_H):
            sl = slice(h * 128, (h + 1) * 128)
            xhat, rstd, vn = heads[h]
            vnb = vn.astype(BF16)
            dgh = dgate[:, sl]
            dgb = dgh.astype(BF16)
            dbs = jnp.zeros((128, 128), F32)
            dw = jnp.zeros((128, 128), F32)
            dvn = []
            for c in range(nch):
                rs = slice(c * CHUNK, (c + 1) * CHUNK)
                dbs = dbs + dgh[rs]
                dw = dw + _dot_nt(dgb[rs], vnb[rs])
                dvn.append(_dot(wtT_ref[h], dgb[rs]))
            dvn = jnp.concatenate(dvn, axis=0) if nch > 1 else dvn[0]
            dbsx_ref[:, sl] += dbs
            dws_ref[h] += jnp.where(tril, dw, 0.0)
            dlng_ref[:, sl] += jnp.sum(dvn * xhat, axis=0, keepdims=True)
            dlnb_ref[:, sl] += jnp.sum(dvn, axis=0, keepdims=True)
            dxh = dvn * lng[:, sl]
            dvg.append(rstd * (dxh - jnp.mean(dxh, axis=-1, keepdims=True) - xhat * jnp.mean(dxh * xhat, axis=-1, keepdims=True)))
        dv_ref[...] = (jnp.concatenate(dvg, axis=1) * _gelu_grad(v)).astype(BF16)

    p512, w3 = _full((1, GM_W)), _full((GM_H, 128, 128))
    return pl.pallas_call(
        body, name=name, grid=(B, S // tb),
        in_specs=[u_spec, v_spec, do_spec, p512, p512, w3, w3, _full((128, GM_W)), p512],
        out_specs=[tok, tok, p512, p512, w3, _full((128, GM_W)), p512],
        out_shape=[jax.ShapeDtypeStruct((B, S, GM_W), BF16)] * 2 + [
            jax.ShapeDtypeStruct((1, GM_W), F32), jax.ShapeDtypeStruct((1, GM_W), F32),
            jax.ShapeDtypeStruct((GM_H, 128, 128), F32), jax.ShapeDtypeStruct((128, GM_W), F32),
            jax.ShapeDtypeStruct((1, GM_W), F32)],
        compiler_params=pltpu.CompilerParams(dimension_semantics=("arbitrary", "arbitrary")),
    )(P, P, dcat, lng, lnb, wt, wtT, bsx, og)


def _lane_half():
    return lax.broadcasted_iota(jnp.int32, (128, 128), 1) // 64


def _att_stack(x, kvh, dtype):
    half = _lane_half()
    rows = []
    for g in range(4):
        i = kvh * 4 + g
        pair = x[:, (i // 2) * 128:(i // 2 + 1) * 128]
        if i % 2 != kvh:
            pair = pltpu.roll(pair, 64, 1)
        rows.append(jnp.where(half == kvh, pair, 0.0))
    return jnp.concatenate(rows, axis=0).astype(dtype)


def _att_unstack(pairs, y, kvh):
    half = _lane_half()
    for g in range(4):
        i = kvh * 4 + g
        piece = y[g * 128:(g + 1) * 128]
        if i % 2 != kvh:
            piece = pltpu.roll(piece, 64, 1)
        pairs[i // 2] = jnp.where(half == i % 2, piece, pairs[i // 2])
    return pairs


def _att_probs(qb, k2, st, sink_ref, kvh):
    qm = _att_stack(qb, kvh, BF16)
    s = _dot_nt(qm, k2) * (64 ** -0.5)
    qi = lax.broadcasted_iota(jnp.int32, (512, 256), 0) % 128
    kj = lax.broadcasted_iota(jnp.int32, (512, 256), 1)
    diff = qi + 128 - kj
    valid = (diff >= 0) & (diff < 128) & (st + kj - 128 >= 0)
    s = jnp.where(valid, s, NEG_INF)
    grp = lax.broadcasted_iota(jnp.int32, (512, 1), 0) // 128
    sink = jnp.zeros((512, 1), F32)
    for g in range(4):
        sink = jnp.where(grp == g, sink_ref[kvh * 4 + g], sink)
    m = jnp.maximum(jnp.max(s, axis=-1, keepdims=True), sink)
    e = jnp.exp(s - m)
    esink = jnp.exp(sink - m)
    inv = 1.0 / (jnp.sum(e, axis=-1, keepdims=True) + esink)
    return qm, e * inv, esink * inv


def _att_specs(S):
    q = pl.BlockSpec((None, S, ATT_W), lambda b: (b, 0, OFF["q"] // ATT_W))
    k = pl.BlockSpec((None, S, KV_W), lambda b: (b, 0, OFF["k"] // KV_W))
    v = pl.BlockSpec((None, S, KV_W), lambda b: (b, 0, OFF["vv"] // KV_W))
    tok = pl.BlockSpec((None, S, ATT_W), lambda b: (b, 0, 0))
    kv = pl.BlockSpec((None, S, KV_W), lambda b: (b, 0, 0))
    return q, k, v, tok, kv


_SMEM = pl.BlockSpec(memory_space=pltpu.SMEM)


def _attn_fwd(P, sinks, og, name):
    B, S, _ = P.shape
    q_spec, k_spec, v_spec, tok, _ = _att_specs(S)

    def body(q_ref, k_ref, v_ref, sink_ref, og_ref, o_ref, kpad, vpad):
        kpad[0:128, :] = jnp.zeros((128, KV_W), BF16)
        vpad[0:128, :] = jnp.zeros((128, KV_W), BF16)
        kpad[128:, :] = k_ref[...].astype(BF16)
        vpad[128:, :] = v_ref[...].astype(BF16)

        def step(n, carry):
            st = pl.multiple_of(n * 128, 128)
            qb = q_ref[pl.ds(st, 128), :]
            k2, v2 = kpad[pl.ds(st, 256), :], vpad[pl.ds(st, 256), :]
            pairs = [jnp.zeros((128, 128), F32)] * 4
            for kvh in range(2):
                _, p, _ = _att_probs(qb, k2, st, sink_ref, kvh)
                pairs = _att_unstack(pairs, _dot(p.astype(BF16), v2), kvh)
            o = jnp.concatenate(pairs, axis=1)
            r = lax.rsqrt(jnp.mean(o * o, axis=-1, keepdims=True) + EPS)
            o_ref[pl.ds(st, 128), :] = (o * r * og_ref[...]).astype(BF16)
            return carry

        lax.fori_loop(0, S // 128, step, 0)

    return pl.pallas_call(
        body, name=name, grid=(B,), in_specs=[q_spec, k_spec, v_spec, _SMEM, _full((1, ATT_W))], out_specs=tok,
        out_shape=jax.ShapeDtypeStruct((B, S, ATT_W), BF16),
        scratch_shapes=[pltpu.VMEM((S + 128, KV_W), BF16)] * 2)(P, P, P, sinks, og)


def _attn_bwd(P, dcat, sinks, og, name):
    B, S, _ = P.shape
    q_spec, k_spec, v_spec, tok, kv = _att_specs(S)
    do_spec = pl.BlockSpec((None, S, ATT_W), lambda b: (b, 0, GM_W // ATT_W))

    def body(q_ref, k_ref, v_ref, do_ref, sink_ref, og_ref, dq_ref, dk_ref, dv_ref, dsink_ref, dog_ref,
             kpad, vpad, dkpad, dvpad):
        @pl.when(pl.program_id(0) == 0)
        def _():
            dsink_ref[...] = jnp.zeros((8, 128), F32)
            dog_ref[...] = jnp.zeros((1, ATT_W), F32)

        kpad[0:128, :] = jnp.zeros((128, KV_W), BF16)
        vpad[0:128, :] = jnp.zeros((128, KV_W), BF16)
        kpad[128:, :] = k_ref[...].astype(BF16)
        vpad[128:, :] = v_ref[...].astype(BF16)
        dkpad[...] = jnp.zeros((S + 128, KV_W), F32)
        dvpad[...] = jnp.zeros((S + 128, KV_W), F32)
        half = _lane_half()
        head_row = lax.broadcasted_iota(jnp.int32, (8, 128), 0)

        def step(n, carry):
            st = pl.multiple_of(n * 128, 128)
            qb = q_ref[pl.ds(st, 128), :]
            k2, v2 = kpad[pl.ds(st, 256), :], vpad[pl.ds(st, 256), :]
            saved, pairs = [], [jnp.zeros((128, 128), F32)] * 4
            for kvh in range(2):
                qm, p, psink = _att_probs(qb, k2, st, sink_ref, kvh)
                o = _dot(p.astype(BF16), v2)
                saved.append((qm, p, psink, o))
                pairs = _att_unstack(pairs, o, kvh)
            o = jnp.concatenate(pairs, axis=1)
            r = lax.rsqrt(jnp.mean(o * o, axis=-1, keepdims=True) + EPS)
            on = o * r
            dout = do_ref[pl.ds(st, 128), :]
            dog_ref[...] += jnp.sum(dout * on, axis=0, keepdims=True)
            dyn = dout * og_ref[...]
            do = r * (dyn - on * jnp.mean(dyn * on, axis=-1, keepdims=True))
            dq_pairs = [jnp.zeros((128, 128), F32)] * 4
            dsink = jnp.zeros((8, 128), F32)
            for kvh in range(2):
                qm, p, psink, og_ = saved[kvh]
                dog = _att_stack(do, kvh, F32)
                delta = jnp.sum(dog * jnp.where(jnp.concatenate([half] * 4, axis=0) == kvh, og_, 0.0), axis=-1, keepdims=True)
                dogb, pb = dog.astype(BF16), p.astype(BF16)
                dvpad[pl.ds(st, 256), :] += _dot_tn(pb, dogb)
                dp = _dot_nt(dogb, v2)
                ds = (p * (dp - delta) * (64 ** -0.5)).astype(BF16)
                sd = psink * delta
                for g in range(4):
                    dsink = dsink - jnp.where(head_row == kvh * 4 + g, jnp.sum(sd[g * 128:(g + 1) * 128]), 0.0)
                dq_pairs = _att_unstack(dq_pairs, _dot(ds, k2), kvh)
                dkpad[pl.ds(st, 256), :] += _dot_tn(ds, qm)
            dsink_ref[...] += dsink
            dq_ref[pl.ds(st, 128), :] = jnp.concatenate(dq_pairs, axis=1).astype(BF16)
            return carry

        lax.fori_loop(0, S // 128, step, 0)
        dk_ref[...] = dkpad[128:, :].astype(BF16)
        dv_ref[...] = dvpad[128:, :].astype(BF16)

    return pl.pallas_call(
        body, name=name, grid=(B,),
        in_specs=[q_spec, k_spec, v_spec, do_spec, _SMEM, _full((1, ATT_W))],
        out_specs=[tok, kv, kv, _full((8, 128)), _full((1, ATT_W))],
        out_shape=[jax.ShapeDtypeStruct((B, S, ATT_W), BF16), jax.ShapeDtypeStruct((B, S, KV_W), BF16),
                   jax.ShapeDtypeStruct((B, S, KV_W), BF16), jax.ShapeDtypeStruct((8, 128), F32),
                   jax.ShapeDtypeStruct((1, ATT_W), F32)],
        scratch_shapes=[pltpu.VMEM((S + 128, KV_W), BF16)] * 2 + [pltpu.VMEM((S + 128, KV_W), F32)] * 2,
        compiler_params=pltpu.CompilerParams(dimension_semantics=("arbitrary",)),
    )(P, P, P, dcat, sinks, og)


CONV_TC = 256
CONV_RC = 64


def _conv_taps(ext, r0):
    return [ext[pl.ds(r0 + 8 - k, CONV_RC), :] for k in range(4)]


def _conv_pre(taps, w_ref, b_ref):
    acc = b_ref[...] + w_ref[3:4, :] * taps[0]
    for k in range(1, 4):
        acc = acc + w_ref[3 - k:4 - k, :] * taps[k]
    return acc


def _conv_fwd(P, w8, b, name):
    B, S, _ = P.shape
    nj = CONV_CH // CONV_TC
    x_spec = pl.BlockSpec((None, S, CONV_TC), lambda b_, j: (b_, 0, OFF["xbc"] // CONV_TC + j))
    tok = pl.BlockSpec((None, S, CONV_TC), lambda b_, j: (b_, 0, j))

    def body(x_ref, w_ref, b_ref, o_ref, ext):
        ext[0:8, :] = jnp.zeros((8, CONV_TC), F32)
        ext[8:, :] = x_ref[...]
        for r0 in range(0, S, CONV_RC):
            pre = _conv_pre(_conv_taps(ext, r0), w_ref, b_ref)
            o_ref[pl.ds(r0, CONV_RC), :] = pre * _sigmoid(pre)

    return pl.pallas_call(
        body, name=name, grid=(B, nj),
        in_specs=[x_spec, pl.BlockSpec((8, CONV_TC), lambda b_, j: (0, j)), pl.BlockSpec((1, CONV_TC), lambda b_, j: (0, j))],
        out_specs=tok, out_shape=jax.ShapeDtypeStruct((B, S, CONV_CH), F32),
        scratch_shapes=[pltpu.VMEM((S + 8, CONV_TC), F32)])(P, w8, b)


def _conv_bwd(P, dact, w8, b, name):
    B, S, _ = P.shape
    nj = CONV_CH // CONV_TC
    x_spec = pl.BlockSpec((None, S, CONV_TC), lambda j, b_: (b_, 0, OFF["xbc"] // CONV_TC + j))
    tok = pl.BlockSpec((None, S, CONV_TC), lambda j, b_: (b_, 0, j))
    w_spec = pl.BlockSpec((8, CONV_TC), lambda j, b_: (0, j))
    b_spec = pl.BlockSpec((1, CONV_TC), lambda j, b_: (0, j))

    def body(x_ref, d_ref, w_ref, b_ref, dx_ref, dw_ref, db_ref, ext, extd):
        @pl.when(pl.program_id(1) == 0)
        def _():
            dw_ref[...] = jnp.zeros((8, CONV_TC), F32)
            db_ref[...] = jnp.zeros((1, CONV_TC), F32)

        ext[0:8, :] = jnp.zeros((8, CONV_TC), F32)
        ext[8:, :] = x_ref[...]
        extd[pl.ds(8 + S, 8), :] = jnp.zeros((8, CONV_TC), F32)
        db = jnp.zeros((1, CONV_TC), F32)
        dws = [jnp.zeros((1, CONV_TC), F32)] * 4
        for r0 in range(0, S, CONV_RC):
            taps = _conv_taps(ext, r0)
            pre = _conv_pre(taps, w_ref, b_ref)
            sg = _sigmoid(pre)
            dpre = d_ref[pl.ds(r0, CONV_RC), :] * (sg * (1.0 + pre * (1.0 - sg)))
            extd[pl.ds(8 + r0, CONV_RC), :] = dpre
            db = db + jnp.sum(dpre, axis=0, keepdims=True)
            dws = [dws[i] + jnp.sum(dpre * taps[3 - i], axis=0, keepdims=True) for i in range(4)]
        for r0 in range(0, S, CONV_RC):
            dx = w_ref[3:4, :] * extd[pl.ds(8 + r0, CONV_RC), :]
            for k in range(1, 4):
                dx = dx + w_ref[3 - k:4 - k, :] * extd[pl.ds(8 + r0 + k, CONV_RC), :]
            dx_ref[pl.ds(r0, CONV_RC), :] = dx.astype(BF16)
        db_ref[...] += db
        sub = lax.broadcasted_iota(jnp.int32, (8, CONV_TC), 0)
        dw_ref[...] += sum(jnp.where(sub == i, dws[i], 0.0) for i in range(4))

    return pl.pallas_call(
        body, name=name, grid=(nj, B), in_specs=[x_spec, tok, w_spec, b_spec], out_specs=[tok, w_spec, b_spec],
        out_shape=[jax.ShapeDtypeStruct((B, S, CONV_CH), BF16), jax.ShapeDtypeStruct((8, CONV_CH), F32),
                   jax.ShapeDtypeStruct((1, CONV_CH), F32)],
        scratch_shapes=[pltpu.VMEM((S + 8, CONV_TC), F32), pltpu.VMEM((S + 16, CONV_TC), F32)],
        compiler_params=pltpu.CompilerParams(dimension_semantics=("arbitrary", "arbitrary")),
    )(P, dact, w8, b)


def _ssd_consts():
    hd = np.arange(SSM_W) // SSM_HD
    E = (np.arange(128)[:, None] == hd[None, :]).astype(np.float32)
    tri = (np.arange(128)[:, None] >= np.arange(128)[None, :]).astype(np.float32)
    return jnp.asarray(E, BF16), jnp.asarray(E.T, BF16), jnp.asarray(tri, BF16), jnp.asarray(tri.T, BF16)


def _pieces(x, n):
    out, r = [], x
    for _ in range(n):
        p = r.astype(BF16)
        out.append(p)
        r = r - p.astype(F32)
    return out


def _dot01(x, m01, n):
    return sum(_dot(p, m01) for p in _pieces(x, n))


def _dot01_left(m01, x, n):
    return sum(_dot(m01, p) for p in _pieces(x, n))


def _ssd_pre(xa, dtraw, bias, alog, E, tri):
    lane = lax.broadcasted_iota(jnp.int32, (128, 128), 1)
    pre = dtraw + bias
    dtp = jnp.where(lane < SSM_H, jnp.maximum(pre, 0.0) + jnp.log(1.0 + jnp.exp(-jnp.abs(pre))), 0.0)
    a = -jnp.exp(alog)
    acs = _dot01_left(tri, dtp * a, 3)
    acsT = acs.T
    dtE, acsE = _dot01(dtp, E, 2), _dot01(acs, E, 3)
    X = xa[:, :SSM_W]
    xdt = X * dtE
    wE = jnp.exp(acsE[127:128, :] - acsE)
    eE = jnp.exp(acsE)
    cdE = eE[127:128, :]
    return dict(pre=pre, dtp=dtp, a=a, acs=acs, acsT=acsT, dtE=dtE, acsE=acsE, cdE=cdE, X=X, xdt=xdt, wE=wE, eE=eE)


def _ssd_decay(c, h):
    lm = lax.broadcasted_iota(jnp.int32, (128, 128), 0) >= lax.broadcasted_iota(jnp.int32, (128, 128), 1)
    return jnp.exp(jnp.where(lm, c["acs"][:, h:h + 1] - c["acsT"][h:h + 1, :], NEG_INF))


def _ssd_pair_operands(c, CB, h0):
    lane = lax.broadcasted_iota(jnp.int32, (128, 128), 1)
    L0, L1 = _ssd_decay(c, h0), _ssd_decay(c, h0 + 1)
    M = jnp.concatenate([CB * L0, CB * L1], axis=1).astype(BF16)
    xp = c["xdt"][:, h0 * 64:h0 * 64 + 128]
    BD = jnp.concatenate([jnp.where(lane < 64, xp, 0.0), jnp.where(lane >= 64, xp, 0.0)], axis=0).astype(BF16)
    return L0, L1, M, BD


def _ssd_y(c, xa, state_ref, dskipE):
    per_group, ys = [], []
    for g in range(SSM_G):
        gs = slice(g * 512, (g + 1) * 512)
        Bb = xa[:, SSM_W + g * 128:SSM_W + (g + 1) * 128].astype(BF16)
        Cb = xa[:, SSM_W + 256 + g * 128:SSM_W + 256 + (g + 1) * 128].astype(BF16)
        CB = _dot_nt(Cb, Bb)
        Sg = state_ref[:, gs]
        yoff = _dot(Cb, Sg.astype(BF16)) * c["eE"][:, gs]
        ydiag, pairs = [], []
        for j in range(4):
            ops = _ssd_pair_operands(c, CB, g * 8 + 2 * j)
            pairs.append(ops)
            ydiag.append(_dot(ops[2], ops[3]))
        ys.append(jnp.concatenate(ydiag, axis=1) + yoff)
        per_group.append(dict(Bb=Bb, Cb=Cb, CB=CB, Sg=Sg, yoff=yoff, pairs=pairs))
    Y = jnp.concatenate(ys, axis=1) + c["X"] * dskipE
    return Y, per_group


def _ssd_specs(S, rev):
    nc = S // CHUNK
    cm = (lambda b, i: (b, nc - 1 - i)) if rev else (lambda b, i: (b, i))
    xa = pl.BlockSpec((None, CHUNK, CONV_CH), lambda b, i: cm(b, i) + (0,))
    z = [pl.BlockSpec((None, CHUNK, 256), lambda b, i, q=q: cm(b, i) + (OFF["z"] // 256 + q,)) for q in range(4)]
    dt = pl.BlockSpec((None, CHUNK, 128), lambda b, i: cm(b, i) + (OFF["dt"] // 128,))
    tok = pl.BlockSpec((None, CHUNK, SSM_W), lambda b, i: cm(b, i) + (0,))
    st = pl.BlockSpec((None, None, 128, SSM_W), lambda b, i: cm(b, i) + (0, 0))
    return nc, xa, z, dt, tok, st


def _ssd_fwd(xact, P, bias, alog, dskipE, ng, name):
    B, S, _ = P.shape
    nc, xa_spec, z_specs, dt_spec, tok, st_spec = _ssd_specs(S, False)
    E, _, tri, _ = _ssd_consts()

    def body(xa_ref, z0, z1, z2, z3, dt_ref, bias_ref, alog_ref, dsk_ref, ng_ref, E_ref, tri_ref, o_ref, sp_ref, state):
        @pl.when(pl.program_id(1) == 0)
        def _():
            state[...] = jnp.zeros((128, SSM_W), F32)

        sp_ref[...] = state[...]
        xa = xa_ref[...]
        c = _ssd_pre(xa, dt_ref[...], bias_ref[...], alog_ref[...], E_ref[...], tri_ref[...])
        Y, groups = _ssd_y(c, xa, state, dsk_ref[...])
        Z = (c["xdt"] * c["wE"]).astype(BF16)
        for g in range(SSM_G):
            gs = slice(g * 512, (g + 1) * 512)
            state[:, gs] = groups[g]["Sg"] * c["cdE"][:, gs] + _dot_tn(groups[g]["Bb"], Z[:, gs])
        zv = jnp.concatenate([z0[...], z1[...], z2[...], z3[...]], axis=1)
        yz = Y * (zv * _sigmoid(zv))
        outs = []
        for g in range(SSM_G):
            yg = yz[:, g * 512:(g + 1) * 512]
            outs.append(yg * lax.rsqrt(jnp.mean(yg * yg, axis=-1, keepdims=True) + EPS))
        o_ref[...] = (jnp.concatenate(outs, axis=1) * ng_ref[...]).astype(BF16)

    return pl.pallas_call(
        body, name=name, grid=(B, nc),
        in_specs=[xa_spec] + z_specs + [dt_spec, _full((1, 128)), _full((1, 128)), _full((1, SSM_W)), _full((1, SSM_W)),
                                        _full((128, SSM_W)), _full((128, 128))],
        out_specs=[tok, st_spec],
        out_shape=[jax.ShapeDtypeStruct((B, S, SSM_W), BF16), jax.ShapeDtypeStruct((B, nc, 128, SSM_W), F32)],
        scratch_shapes=[pltpu.VMEM((128, SSM_W), F32)],
        compiler_params=pltpu.CompilerParams(dimension_semantics=("arbitrary", "arbitrary")),
    )(xact, P, P, P, P, P, bias, alog, dskipE, ng, E, tri)


def _ssd_bwd(xact, P, sprev, dcat, bias, alog, dskipE, ng, name):
    B, S, _ = P.shape
    nc, xa_spec, z_specs, dt_spec, tok, st_spec = _ssd_specs(S, True)
    do_spec = pl.BlockSpec((None, CHUNK, SSM_W), lambda b, i: (b, nc - 1 - i, 1))
    E, ET, tri, triT = _ssd_consts()
    dt_out = pl.BlockSpec((None, CHUNK, 128), lambda b, i: (b, nc - 1 - i, 0))

    def body(xa_ref, z0, z1, z2, z3, dt_ref, sp_ref, do_ref, bias_ref, alog_ref, dsk_ref, ng_ref, E_ref, ET_ref, tri_ref,
             triT_ref, dxa_ref, dz_ref, ddt_ref, dbias_ref, dalog_ref, ddsk_ref, dng_ref, dstate):
        first = (pl.program_id(0) == 0) & (pl.program_id(1) == 0)

        @pl.when(first)
        def _():
            for ref in (dbias_ref, dalog_ref, ddsk_ref, dng_ref):
                ref[...] = jnp.zeros(ref.shape, F32)

        @pl.when(pl.program_id(1) == 0)
        def _():
            dstate[...] = jnp.zeros((128, SSM_W), F32)

        xa, ETm = xa_ref[...], ET_ref[...]
        c = _ssd_pre(xa, dt_ref[...], bias_ref[...], alog_ref[...], E_ref[...], tri_ref[...])
        Y, groups = _ssd_y(c, xa, sp_ref, dsk_ref[...])
        X, xdt = c["X"], c["xdt"]
        zv = jnp.concatenate([z0[...], z1[...], z2[...], z3[...]], axis=1)
        sg = _sigmoid(zv)
        zs = zv * sg
        yz = Y * zs
        dout = do_ref[...]
        dyz = []
        for g in range(SSM_G):
            gs = slice(g * 512, (g + 1) * 512)
            yg = yz[:, gs]
            r = lax.rsqrt(jnp.mean(yg * yg, axis=-1, keepdims=True) + EPS)
            yn = yg * r
            dng_ref[:, gs] += jnp.sum(dout[:, gs] * yn, axis=0, keepdims=True)
            dyn = dout[:, gs] * ng_ref[:, gs]
            dyz.append(r * (dyn - yn * jnp.mean(dyn * yn, axis=-1, keepdims=True)))
        dyz = jnp.concatenate(dyz, axis=1)
        dz_ref[...] = (dyz * Y * (sg * (1.0 + zv * (1.0 - sg)))).astype(BF16)
        dY = dyz * zs
        ddsk_ref[...] += jnp.sum(dY * X, axis=0, keepdims=True)
        dX = dY * dsk_ref[...]
        lane = lax.broadcasted_iota(jnp.int32, (128, 128), 1)
        sub = lax.broadcasted_iota(jnp.int32, (128, 128), 0)
        colform = jnp.zeros((128, 128), F32)
        rowform = jnp.zeros((128, 128), F32)
        dxdt, gacsE, dBC = [], [], []
        for g in range(SSM_G):
            gs = slice(g * 512, (g + 1) * 512)
            G = groups[g]
            Bb, Cb, CB, Sg = G["Bb"], G["Cb"], G["CB"], G["Sg"]
            dYg = dY[:, gs]
            dQ = (dYg * c["eE"][:, gs]).astype(BF16)
            dSn = dstate[:, gs]
            dSnb = dSn.astype(BF16)
            cd = c["cdE"][:, gs]
            dC = _dot_nt(dQ, Sg.astype(BF16))
            dSprev = _dot_tn(Cb, dQ) + dSn * cd
            t1 = jnp.broadcast_to(jnp.sum(dSn * Sg * cd, axis=0, keepdims=True), (8, 512))
            colform = colform + jnp.where(sub == 127, _dot01(t1, ETm[gs, :], 2)[0:1, :], 0.0)
            Zg = xdt[:, gs] * c["wE"][:, gs]
            dZ = _dot(Bb, dSnb)
            dB = _dot_nt(Zg.astype(BF16), dSnb)
            U = dZ * Zg
            ga = dYg * G["yoff"] - U
            ga = ga + jnp.where(lax.broadcasted_iota(jnp.int32, (128, 512), 0) == 127, jnp.sum(U, axis=0, keepdims=True), 0.0)
            gacsE.append(ga)
            dxg = [None] * 4
            dCB = jnp.zeros((128, 128), F32)
            for j in range(4):
                h0 = g * 8 + 2 * j
                L0, L1, M, BD = G["pairs"][j]
                dYp = dYg[:, j * 128:(j + 1) * 128].astype(BF16)
                dM = _dot_nt(dYp, BD)
                dBD = _dot_tn(M, dYp)
                dxg[j] = jnp.where(lane < 64, dBD[:128], dBD[128:])
                for t, (h, L) in enumerate(((h0, L0), (h0 + 1, L1))):
                    dMh = dM[:, t * 128:(t + 1) * 128]
                    dCB = dCB + dMh * L
                    Gh = dMh * CB * L
                    colform = colform + jnp.where(lane == h, jnp.sum(Gh, axis=1, keepdims=True), 0.0)
                    rowform = rowform - jnp.where(sub == h, jnp.sum(Gh, axis=0, keepdims=True), 0.0)
            dCBb = dCB.astype(BF16)
            dC = dC + _dot(dCBb, Bb)
            dB = dB + _dot_tn(dCBb, Cb)
            dxdt.append(jnp.concatenate(dxg, axis=1) + dZ * c["wE"][:, gs])
            dBC.append((dB, dC))
            dstate[:, gs] = dSprev
        dxdt = jnp.concatenate(dxdt, axis=1)
        dX = dX + dxdt * c["dtE"]
        ddt = _dot01(dxdt * X, ETm, 2)
        dacs = colform + rowform.T + _dot01(jnp.concatenate(gacsE, axis=1), ETm, 2)
        dda = _dot01_left(triT_ref[...], dacs, 2)
        ddt = ddt + dda * c["a"]
        dalog_ref[...] += jnp.sum(dda * c["dtp"], axis=0, keepdims=True) * c["a"]
        ddtraw = jnp.where(lane < SSM_H, ddt * _sigmoid(c["pre"]), 0.0)
        dbias_ref[...] += jnp.sum(ddtraw, axis=0, keepdims=True)
        ddt_ref[...] = ddtraw.astype(BF16)
        dxa_ref[...] = jnp.concatenate([dX, dBC[0][0], dBC[1][0], dBC[0][1], dBC[1][1]], axis=1)

    p128, p1k = _full((1, 128)), _full((1, SSM_W))
    return pl.pallas_call(
        body, name=name, grid=(B, nc),
        in_specs=[xa_spec] + z_specs + [dt_spec, st_spec, do_spec, p128, p128, p1k, p1k,
                                        _full((128, SSM_W)), _full((SSM_W, 128)), _full((128, 128)), _full((128, 128))],
        out_specs=[xa_spec, tok, dt_out, p128, p128, p1k, p1k],
        out_shape=[jax.ShapeDtypeStruct((B, S, CONV_CH), F32), jax.ShapeDtypeStruct((B, S, SSM_W), BF16),
                   jax.ShapeDtypeStruct((B, S, 128), BF16), jax.ShapeDtypeStruct((1, 128), F32),
                   jax.ShapeDtypeStruct((1, 128), F32), jax.ShapeDtypeStruct((1, SSM_W), F32),
                   jax.ShapeDtypeStruct((1, SSM_W), F32)],
        scratch_shapes=[pltpu.VMEM((128, SSM_W), F32)],
        compiler_params=pltpu.CompilerParams(dimension_semantics=("arbitrary", "arbitrary")),
    )(xact, P, P, P, P, P, sprev, dcat, bias, alog, dskipE, ng, E, ET, tri, triT)


def _adamw(w, parts, m, v, name, tr=512, row0=0, prev=None):
    Rtot, C = w.shape
    ns, R = parts.shape[0], parts.shape[1]
    tr = min(tr, R)
    assert R % tr == 0 and row0 % tr == 0
    off = row0 // tr
    c1 = 1.0 / (1.0 - ADAM_B1 ** ADAM_STEP)
    c2 = 1.0 / (1.0 - ADAM_B2 ** ADAM_STEP)

    def body(w_ref, p_ref, m_ref, v_ref, *rest):
        g_ref, d_ref, mo_ref, vo_ref = rest[-4:]
        g = p_ref[0].astype(F32)
        for s in range(1, ns):
            g = g + p_ref[s].astype(F32)
        mn = ADAM_B1 * m_ref[...] + (1.0 - ADAM_B1) * g
        vn = ADAM_B2 * v_ref[...] + (1.0 - ADAM_B2) * (g * g)
        g_ref[...] = g
        mo_ref[...] = mn
        vo_ref[...] = vn
        d_ref[...] = -ADAM_LR * ((mn * c1) / (jnp.sqrt(vn * c2) + ADAM_EPS) + ADAM_WD * w_ref[...])

    blk = pl.BlockSpec((tr, C), lambda i: (i + off, 0))
    extra = [] if prev is None else list(prev)
    return pl.pallas_call(
        body, name=name, grid=(R // tr,),
        in_specs=[blk, pl.BlockSpec((ns, tr, C), lambda i: (0, i, 0)), blk, blk] + [pl.BlockSpec(memory_space=pl.ANY)] * len(extra),
        out_specs=[blk] * 4, out_shape=[jax.ShapeDtypeStruct((Rtot, C), F32)] * 4,
        input_output_aliases={4 + k: k for k in range(len(extra))})(w, parts, m, v, *extra)


_SMALL = ("ada_b", "norm1_g", "gm_ln_g", "gm_ln_b", "gm_ws", "gm_bs", "gm_norm_g", "attn_sinks", "attn_norm_g", "conv_b",
          "dt_bias", "a_log", "d_skip", "ssm_norm_g", "norm2_g", "final_norm_g")


def _pack(arrs):
    flat = []
    for a in arrs:
        f = a.reshape(-1).astype(F32)
        flat.append(jnp.pad(f, (0, (-f.shape[0]) % 1024)))
    return jnp.concatenate(flat).reshape(-1, 128)


def _unpack(pack, like):
    out, o = [], 0
    flat = pack.reshape(-1)
    for a in like:
        n = int(np.prod(a.shape))
        out.append(flat[o:o + n].reshape(a.shape))
        o += n + (-n) % 1024
    return out


def kernel(x, c, ada_w, ada_b, norm1_g, w_in, gm_ln_g, gm_ln_b, gm_ws, gm_bs, gm_norm_g, attn_sinks, attn_norm_g, conv_w, conv_b, dt_bias, a_log, d_skip, ssm_norm_g, w_out, norm2_g, w_mlp1, w_mlp2, final_norm_g, loss_target, m_ada_w, m_ada_b, m_norm1_g, m_w_in, m_gm_ln_g, m_gm_ln_b, m_gm_ws, m_gm_bs, m_gm_norm_g, m_attn_sinks, m_attn_norm_g, m_conv_w, m_conv_b, m_dt_bias, m_a_log, m_d_skip, m_ssm_norm_g, m_w_out, m_norm2_g, m_w_mlp1, m_w_mlp2, m_final_norm_g, v_ada_w, v_ada_b, v_norm1_g, v_w_in, v_gm_ln_g, v_gm_ln_b, v_gm_ws, v_gm_bs, v_gm_norm_g, v_attn_sinks, v_attn_norm_g, v_conv_w, v_conv_b, v_dt_bias, v_a_log, v_d_skip, v_ssm_norm_g, v_w_out, v_norm2_g, v_w_mlp1, v_w_mlp2, v_final_norm_g):
    args = dict(locals())
    B, S, _ = x.shape
    T = B * S
    L = DEPTH
    me = 4 * lax.axis_index("x") + 2 * lax.axis_index("y") + lax.axis_index("c")

    gath = _gather2([c, conv_w], "ag_c")
    big = ("w_in", "w_out", "w_mlp1", "w_mlp2")
    chain = [(n, l) for l in range(L) for n in ("w_in", "w_mlp1", "w_out", "w_mlp2")]
    inflight = {}

    def start_next(order):
        if not chain:
            return jnp.zeros((8, 128), F32)
        n, l = chain.pop(0)
        sems, v_thru, land_thru, token = _gather_start(shard[n, l], zone[n, l], order, f"ag_start_{n}{l}")
        inflight[n, l] = (sems, v_thru, land_thru)
        return token

    def gathered(n, l, after):
        _, land = _gather_wait(*inflight.pop((n, l)), after, f"ag_wait_{n}{l}")
        return _gather_finish(land, f"ag_fin_{n}{l}")

    me1 = me.astype(jnp.int32).reshape(1)
    shard = {(n, l): args[n][l].astype(BF16) for n, l in chain}
    zone = {k: _landing_zone(v, me1, f"ag_zone_{k[0]}{k[1]}") for k, v in shard.items()}

    c_all = gath[0].reshape(NDEV * B, D)
    c_act = (c_all * jax.nn.sigmoid(c_all)).astype(BF16)
    nb_rows = c_act.shape[0]
    c_pad = jnp.pad(c_act, ((0, 128 - nb_rows), (0, 0)))
    adw = ada_w.astype(BF16)
    mod_part = jnp.stack([_mm(c_pad, adw[l], mode="nn", name=f"mod{l}", tn=768)[:nb_rows] for l in range(L)])
    mod_all = _gather_small([mod_part], "ag_mod")[0]
    mod_mine = lax.dynamic_slice_in_dim(mod_all, me * B, B, axis=2)
    mod = jnp.transpose(mod_mine, (1, 2, 0, 3)).reshape(L, B, 6 * D) + ada_b[:, None, :]
    mods = [[mod[l][:, None, i * D:(i + 1) * D] for i in range(6)] for l in range(L)]

    win_g, wout_g, w1_g, w2_g = [None] * L, [None] * L, [None] * L, [None] * L

    tril = jnp.tril(jnp.ones((128, 128), F32))
    row = lambda a: a.reshape(1, -1)
    pad128 = lambda a: jnp.pad(a.reshape(1, -1), ((0, 0), (0, 128 - a.shape[-1])))
    small = []
    for l in range(L):
        wt = gm_ws[l] * tril
        small.append(dict(
            lng=row(gm_ln_g[l]), lnb=row(gm_ln_b[l]), wt=wt.astype(BF16), wtT=jnp.swapaxes(wt, 1, 2).astype(BF16),
            bsx=jnp.repeat(gm_bs[l].T, 128, axis=1), gog=row(gm_norm_g[l]), sinks=attn_sinks[l], aog=row(attn_norm_g[l]),
            bias=pad128(dt_bias[l]), alog=pad128(a_log[l]), dskE=jnp.repeat(d_skip[l], SSM_HD).reshape(1, SSM_W),
            sng=row(ssm_norm_g[l]), cb=row(conv_b[l])))
    convw_all = jnp.transpose(gath[1], (1, 2, 0, 3)).reshape(L, 4, CONV_CH)
    convw8 = jnp.pad(convw_all, ((0, 0), (0, 4), (0, 0)))

    saved = []
    xl = x
    tok = start_next(mod)
    h = _norm_fwd(xl, row(norm1_g[0]) + tok[0, 0], mods[0][1], mods[0][0], "norm1_f0")
    for l in range(L):
        sm = small[l]
        g_in = gathered("w_in", l, h)
        tok = start_next(g_in)
        win_g[l] = _to_work_cols(jnp.transpose(g_in, (1, 0, 2)).reshape(D, IN_W))
        P = _mm(h.reshape(T, D), win_g[l], mode="nn", name=f"proj_in{l}", tn=1536, order=tok).reshape(B, S, PW)
        out_a = _gmlp_fwd(P, sm["lng"], sm["lnb"], sm["wt"], sm["bsx"], sm["gog"], f"gmlp_f{l}")
        out_b = _attn_fwd(P, sm["sinks"], sm["aog"], f"attn_f{l}")
        xact = _conv_fwd(P, convw8[l], sm["cb"], f"conv_f{l}")
        w1_g[l] = gathered("w_mlp1", l, xact)
        tok = start_next(w1_g[l])
        out_c, sprev = _ssd_fwd(xact, P, sm["bias"], sm["alog"], sm["dskE"], sm["sng"] + tok[0:1, 0:1], f"ssd_f{l}")
        cat = jnp.concatenate([out_a, out_b, out_c], axis=-1)
        g_out = gathered("w_out", l, cat)
        tok = start_next(g_out)
        wout_g[l] = g_out.reshape(D, D)
        mix = _mm(cat.reshape(T, D), wout_g[l], mode="nn", name=f"proj_out{l}", order=tok).reshape(B, S, D)
        x_mid, h2 = _norm_fwd(xl, row(norm2_g[l]), mods[l][4], mods[l][3], f"norm2_f{l}", resid=(mix, mods[l][2]))
        a_act, r_act = _mm(h2.reshape(T, D), w1_g[l], mode="nn", name=f"mlp1_{l}", out_dtypes=(BF16, BF16), col_blocked_b=True,
                           epilogue=lambda acc: (acc, jnp.square(jnp.maximum(acc, 0.0))))
        g_2 = gathered("w_mlp2", l, r_act)
        tok = start_next(g_2)
        w2_g[l] = g_2.reshape(DFF, D)
        m2 = _mm(r_act, w2_g[l], mode="nn", name=f"mlp2_{l}", order=tok, tk=4096).reshape(B, S, D)
        saved.append(dict(x_in=xl, h=h, P=P, xact=xact, sprev=sprev, cat=cat, mix=mix, x_mid=x_mid, h2=h2, a=a_act, r=r_act, m2=m2))
        if l + 1 < L:
            xl, h = _norm_fwd(x_mid, row(norm1_g[l + 1]), mods[l + 1][1], mods[l + 1][0], f"norm1_f{l + 1}", resid=(m2, mods[l][5]))

    sv = saved[L - 1]
    nb = _norm_bwd(sv["x_mid"], row(final_norm_g), "final_b", tgt=loss_target, br=sv["m2"], gate=mods[L - 1][5], x_is_prev=True)
    loss_part, g_final = nb["loss"], nb["dg"]
    dmod, gsm, gconvw = [None] * L, [None] * L, [None] * L
    core = lax.axis_index("c").astype(jnp.int32).reshape(1)
    reducing = []

    def reduce_start(n, l, p, order):
        from_sib = _pair_exchange([p], f"rs_pair_{n}{l}")[0]
        s, land = _pair_add(p, from_sib, core, f"rs_add_{n}{l}")
        return reduce_exchange(n, l, s, land, order)

    def reduce_exchange(n, l, s, land, order):
        sems, s_thru, land_thru, token = _chipsum_start(s, land, order, f"rs_start_{n}{l}")
        reducing.append((n, l, sems, s_thru, land_thru))
        return token

    other = 1 - core

    for l in reversed(range(L)):
        sv, sm = saved[l], small[l]
        dm2, dxo, dg2 = nb["dbr"].reshape(T, D), nb["dx"], nb["dgate"]
        da = _mm(dm2, w2_g[l], mode="nt", name=f"mlp2_dx{l}", out_dtypes=(BF16,), extras=(sv["a"],),
                 epilogue=lambda acc, a: (acc * (2.0 * jnp.maximum(a.astype(F32), 0.0)),))
        h2f = sv["h2"].reshape(T, D)
        sent2 = _sibling_start(_dw_half(sv["r"], dm2, other, axis="m", name=f"mlp2_dw_sib{l}"), da, f"rs_sib_start_w_mlp2{l}")
        dh2 = _mm(da, w1_g[l], mode="nt", name=f"mlp1_dx{l}", col_blocked_b=True, order=sent2[3]).reshape(B, S, D)
        from_sib = _sibling_wait(*sent2[:3], dh2, f"rs_sib_wait_w_mlp2{l}")
        sent1 = _sibling_start(_dw_half(h2f, da, other, axis="n", name=f"mlp1_dw_sib{l}", order=from_sib), da,
                               f"rs_sib_start_w_mlp1{l}")
        s2, land2 = _dw_half(sv["r"], dm2, core, axis="m", name=f"mlp2_dw_own{l}", add=from_sib, order=sent1[3])
        tok = reduce_exchange("w_mlp2", l, s2, land2, da)
        nb2 = _norm_bwd(sv["x_mid"], row(norm2_g[l]) + tok[0, 0], f"norm2_b{l}", sc=mods[l][4], dh=dh2, dres=dxo, br=sv["mix"],
                        gate=mods[l][2])
        dmix = nb2["dbr"].reshape(T, D)
        from_sib = _sibling_wait(*sent1[:3], dmix, f"rs_sib_wait_w_mlp1{l}")
        s1, land1 = _dw_half(h2f, da, core, axis="n", name=f"mlp1_dw_own{l}", add=from_sib)
        tok = reduce_exchange("w_mlp1", l, s1, land1, dmix)
        dcat = _mm(dmix, wout_g[l], mode="nt", name=f"proj_out_dx{l}", order=tok).reshape(B, S, D)
        du, dv, dlng, dlnb, dws, dbsx, dgog = _gmlp_bwd(sv["P"], dcat, sm["lng"], sm["lnb"], sm["wt"], sm["wtT"], sm["bsx"],
                                                        sm["gog"], f"gmlp_b{l}")
        dq, dk, dvv, dsink, daog = _attn_bwd(sv["P"], dcat, sm["sinks"], sm["aog"], f"attn_b{l}")
        dwo = _mm(sv["cat"].reshape(T, D), dmix, mode="tn", name=f"proj_out_dw{l}", out_dtypes=(BF16,), tk=2048,
                  order=dq).reshape(4, 2, D // NDEV, D)
        tok = reduce_start("w_out", l, dwo, dmix)
        dxa, dz, ddt, dbias, dalog, ddsk, dsng = _ssd_bwd(sv["xact"], sv["P"], sv["sprev"], dcat, sm["bias"], sm["alog"],
                                                          sm["dskE"], sm["sng"] + tok[0:1, 0:1], f"ssd_b{l}")
        dxbc, dcw, dcb = _conv_bwd(sv["P"], dxa, convw8[l], sm["cb"], f"conv_b{l}")
        dP = jnp.concatenate([du, dv, dq, dk, dvv, dz, dxbc, ddt, jnp.zeros((B, S, PW - OFF["dt"] - 128), BF16)],
                             axis=-1).reshape(T, PW)
        dwin = _mm(sv["h"].reshape(T, D), dP, mode="tn", name=f"proj_in_dw{l}", out_dtypes=(BF16,), tn=1536, tk=2048)
        dwin = jnp.transpose(_from_work_cols(dwin).reshape(D, NDEV, IN_W // NDEV), (1, 0, 2)).reshape(4, 2, D, IN_W // NDEV)
        tok = reduce_start("w_in", l, dwin, dP)
        dh = _mm(dP, win_g[l], mode="nt", name=f"proj_in_dx{l}", tk=2304, order=tok).reshape(B, S, D)
        nb = _norm_bwd(sv["x_in"], row(norm1_g[l]), f"norm1_b{l}", sc=mods[l][1], dh=dh, dres=nb2["dx"],
                       br=saved[l - 1]["m2"] if l > 0 else None, gate=mods[l - 1][5] if l > 0 else None)
        dmod[l] = jnp.concatenate([nb["dsh"], nb["dsc"], nb2["dgate"], nb2["dsh"], nb2["dsc"], dg2], axis=-1)
        gconvw[l] = dcw[:4]
        gsm[l] = dict(
            ada_b=jnp.sum(dmod[l], axis=(0, 1)), norm1_g=nb["dg"], gm_ln_g=dlng, gm_ln_b=dlnb, gm_ws=dws,
            gm_bs=dbsx.reshape(128, GM_H, 128).sum(-1).T, gm_norm_g=dgog, attn_sinks=dsink[:, 0], attn_norm_g=daog,
            conv_b=dcb, dt_bias=dbias[0, :SSM_H], a_log=dalog[0, :SSM_H], d_skip=ddsk.reshape(SSM_H, SSM_HD).sum(-1),
            ssm_norm_g=dsng, norm2_g=nb2["dg"])
    grad_x = nb["dx"]

    big_res, after = dict.fromkeys(big), grad_x
    tile_rows = dict(w_in=256, w_out=256, w_mlp1=256, w_mlp2=128)

    def finish_reduce(n, l, sems, s_thru, land_thru, after):
        parts = _chipsum_wait(sems, s_thru, land_thru, after, f"rs_wait_{n}{l}")
        w = args[n]
        big_res[n] = _adamw(w.reshape(-1, w.shape[-1]), parts, args["m_" + n].reshape(-1, w.shape[-1]),
                            args["v_" + n].reshape(-1, w.shape[-1]), f"adamw_{n}{l}", tr=tile_rows[n], row0=l * w.shape[1],
                            prev=big_res[n])
        return big_res[n][0]

    for item in reducing[:-1]:
        after = finish_reduce(*item, after)

    per_layer = [n for n in _SMALL if n != "final_norm_g"]
    g_small = [jnp.stack([gsm[l][n].reshape(args[n].shape[1:]) for l in range(L)]) for n in per_layer] + [g_final.reshape(D)]
    zc = jnp.zeros((L, 4, CONV_CH), F32)
    z1 = jnp.zeros((1, 128), F32)
    gpack = _pack([loss_part] + g_small + [jnp.stack(gconvw)])
    got = _gather2([jnp.stack(dmod).reshape(L, B, 6 * D), gpack], "ag_small", order=after)
    like = [z1] + [args[n] for n in _SMALL] + [zc]
    packs = [_pack([z1] + [args[p + n] for n in _SMALL] + [zc]) for p in ("", "m_", "v_")]
    sres = [_unpack(p, like) for p in _adamw(packs[0], got[1], packs[1], packs[2], "adamw_small", tr=gpack.shape[0])]
    res = {n: [r[1 + i] for r in sres] for i, n in enumerate(_SMALL)}
    loss = sres[0][0][0, 0]
    gcw = lax.dynamic_slice_in_dim(sres[0][-1], me * (CONV_CH // NDEV), CONV_CH // NDEV, axis=2)

    def update(name, parts, tr):
        w = args[name]
        r = _adamw(w.reshape(-1, w.shape[-1]), parts, args["m_" + name].reshape(-1, w.shape[-1]),
                   args["v_" + name].reshape(-1, w.shape[-1]), "adamw_" + name, tr=tr)
        res[name] = [a.reshape(w.shape) for a in r]

    update("conv_w", gcw.reshape(1, L * 4, CONV_CH // NDEV), L * 4)

    dmod_all = jnp.transpose(got[0], (1, 0, 2, 3)).reshape(L, NDEV * B, 6 * D)
    dm_mine = lax.dynamic_slice_in_dim(dmod_all, me * (6 * D // NDEV), 6 * D // NDEV, axis=2)
    dm_pad = jnp.pad(dm_mine, ((0, 0), (0, 128 - nb_rows), (0, 0))).astype(BF16)
    g_adaw = jnp.stack([_mm(c_pad, dm_pad[l], mode="tn", name=f"ada_dw{l}", tn=768) for l in range(L)])
    update("ada_w", g_adaw.reshape(1, L * D, 6 * D // NDEV), 256)

    finish_reduce(*reducing[-1], res["ada_w"][0])
    for n in big:
        res[n] = [a.reshape(args[n].shape) for a in big_res[n]]

    names = ['ada_w', 'ada_b', 'norm1_g', 'w_in', 'gm_ln_g', 'gm_ln_b', 'gm_ws', 'gm_bs', 'gm_norm_g', 'attn_sinks',
             'attn_norm_g', 'conv_w', 'conv_b', 'dt_bias', 'a_log', 'd_skip', 'ssm_norm_g', 'w_out', 'norm2_g', 'w_mlp1',
             'w_mlp2', 'final_norm_g']
    return (loss, grad_x, *[res[n][0] for n in names], *[res[n][1] for n in names], *[res[n][2] for n in names],
            *[res[n][3] for n in names])
```

```python
import functools

import jax
import jax.numpy as jnp
import numpy as np
from jax import lax
from jax.experimental import pallas as pl
from jax.experimental.pallas import tpu as pltpu

F32, BF16 = jnp.float32, jnp.bfloat16
HI = lax.Precision.HIGHEST
MESH = pl.DeviceIdType.MESH
NDEV = 8

D = 2048
DEPTH = 2
CHUNK = 128
GM_W, GM_H = 512, 4
ATT_W, KV_W, ATT_H = 512, 128, 8
SSM_W, SSM_H, SSM_HD, SSM_G = 1024, 16, 64, 2
CONV_CH = 1536
IN_W = 4368
DFF = 8192
EPS = 1e-6
NEG_INF = -1e30
GELU_K = 0.7978845608028654
GELU_C = 0.044715

_ORIG = (("u", 512), ("v", 512), ("q", 512), ("k", 128), ("vv", 128), ("z", 1024), ("xbc", 1536), ("dt", 16))
OFF = dict(u=0, v=512, q=1024, k=1536, vv=1664, z=1792, xbc=2816, dt=4352)
PW = 4608

ADAM_LR, ADAM_B1, ADAM_B2, ADAM_EPS, ADAM_WD, ADAM_STEP = 0.001, 0.9, 0.999, 1e-08, 0.01, 10


def _to_work_cols(w):
    return jnp.pad(w, [(0, 0)] * (w.ndim - 1) + [(0, PW - IN_W)])


def _from_work_cols(wp):
    return wp[..., :IN_W]


def _sigmoid(x):
    return 1.0 / (1.0 + jnp.exp(-x))


def _gelu(x):
    return 0.5 * x * (1.0 + jnp.tanh(GELU_K * (x + GELU_C * x * x * x)))


def _gelu_grad(x):
    t = jnp.tanh(GELU_K * (x + GELU_C * x * x * x))
    return 0.5 * (1.0 + t) + 0.5 * x * (1.0 - t * t) * GELU_K * (1.0 + 3.0 * GELU_C * x * x)


def _dot(a, b, prec=None):
    return jnp.dot(a, b, precision=prec, preferred_element_type=F32)


def _dot_nt(a, b, prec=None):
    return lax.dot_general(a, b, (((1,), (1,)), ((), ())), precision=prec, preferred_element_type=F32)


def _dot_tn(a, b, prec=None):
    return lax.dot_general(a, b, (((0,), (0,)), ((), ())), precision=prec, preferred_element_type=F32)


def _full(shape):
    return pl.BlockSpec(shape, lambda *_: (0,) * len(shape))


_HBM = pl.BlockSpec(memory_space=pltpu.HBM)


def _me():
    return lax.axis_index("x"), lax.axis_index("y"), lax.axis_index("c")


def _peer(k):
    x, y, c = _me()
    px = 1 - x if k & 4 else x
    py = 1 - y if k & 2 else y
    pc = 1 - c if k & 1 else c
    return (px, py, pc), 4 * px + 2 * py + pc


def _gather_small(xs, name, order=()):
    n = len(xs)

    def body(*refs):
        ins, outs = refs[:n], refs[-n - 3:-3]
        send, recv, loc = refs[-3:]
        x, y, c = _me()
        me = 4 * x + 2 * y + c
        started = []
        for i in range(n):
            own = pltpu.make_async_copy(ins[i], outs[i].at[me], loc.at[i])
            own.start()
            started.append(own)
        for k in range(1, NDEV):
            dev, lin = _peer(k)
            for i in range(n):
                pltpu.make_async_remote_copy(
                    src_ref=ins[i], dst_ref=outs[i].at[me],
                    send_sem=send.at[i, k - 1], recv_sem=recv.at[i, k - 1], device_id=dev, device_id_type=MESH).start()
        for k in range(1, NDEV):
            dev, lin = _peer(k)
            for i in range(n):
                pltpu.make_async_remote_copy(
                    src_ref=ins[i], dst_ref=outs[i].at[lin],
                    send_sem=send.at[i, k - 1], recv_sem=recv.at[i, k - 1], device_id=dev, device_id_type=MESH).wait()
        for own in started:
            own.wait()

    extra = list(order)
    return pl.pallas_call(
        body, name=name, out_shape=[jax.ShapeDtypeStruct((NDEV,) + a.shape, a.dtype) for a in xs],
        in_specs=[_HBM] * n + [pl.BlockSpec(memory_space=pl.ANY)] * len(extra), out_specs=[_HBM] * n,
        scratch_shapes=[pltpu.SemaphoreType.DMA((n, NDEV - 1)), pltpu.SemaphoreType.DMA((n, NDEV - 1)),
                        pltpu.SemaphoreType.DMA((n,))],
        compiler_params=pltpu.CompilerParams(has_side_effects=True),
    )(*xs, *extra)


def _chips():
    x, y, c = _me()
    return x, y, c, [(1 - x, y), (x, 1 - y), (1 - x, 1 - y)]


def _gather2(xs, name, order=None):
    n = len(xs)
    extra = [] if order is None else [order]

    def body(*refs):
        ins, outs = refs[:n], refs[-n - 3:-3]
        send, recv, loc = refs[-3:]
        x, y, c, chips = _chips()
        me, sib = (x, y, c), (x, y, 1 - c)

        def cp(i, k, block, to, src=None):
            slot = outs[i].at[4 * block[0] + 2 * block[1] + block[2]]
            return pltpu.make_async_remote_copy(src_ref=slot if src is None else src, dst_ref=slot, send_sem=send.at[i, k],
                                                recv_sem=recv.at[i, k], device_id=to, device_id_type=MESH)

        sent = []
        for i in range(n):
            for j, chip in enumerate(chips):
                sent.append(cp(i, 1 + j, me, (*chip, c), src=ins[i]))
            sent.append(cp(i, 0, me, sib, src=ins[i]))
        for s in sent:
            s.start()
        own = [pltpu.make_async_copy(ins[i], outs[i].at[4 * x + 2 * y + c], loc.at[i]) for i in range(n)]
        for o in own:
            o.start()
        for j, chip in enumerate(chips):
            for i in range(n):
                cp(i, 1 + j, (*chip, c), me).wait_recv()
                fwd = cp(i, 4 + j, (*chip, c), sib)
                fwd.start()
                sent.append(fwd)
        for i in range(n):
            cp(i, 0, sib, me).wait_recv()
            for j, chip in enumerate(chips):
                cp(i, 4 + j, (*chip, 1 - c), me).wait_recv()
        for s in sent:
            s.wait_send()
        for o in own:
            o.wait()

    return pl.pallas_call(
        body, name=name, out_shape=[jax.ShapeDtypeStruct((NDEV,) + a.shape, a.dtype) for a in xs],
        in_specs=[_HBM] * n + [pl.BlockSpec(memory_space=pl.ANY)] * len(extra), out_specs=[_HBM] * n,
        scratch_shapes=[pltpu.SemaphoreType.DMA((n, 7)), pltpu.SemaphoreType.DMA((n, 7)), pltpu.SemaphoreType.DMA((n,))],
        compiler_params=pltpu.CompilerParams(has_side_effects=True),
    )(*xs, *extra)


def _pair_add(p, r1, core, name, tr=256):
    _, _, R, C = p.shape
    tr = min(tr, R)

    def body(core_ref, p_ref, r_ref, o_ref, o2_ref):
        s = (p_ref[...].astype(F32) + r_ref[...].astype(F32)).astype(o_ref.dtype)
        o_ref[...] = s
        o2_ref[...] = s

    blk = pl.BlockSpec((None, tr, C), lambda ch, i, core_ref: (ch, i, 0))
    return pl.pallas_call(
        body, name=name, out_shape=[jax.ShapeDtypeStruct((4, R, C), p.dtype)] * 2,
        grid_spec=pltpu.PrefetchScalarGridSpec(
            num_scalar_prefetch=1, grid=(4, R // tr),
            in_specs=[pl.BlockSpec((None, None, tr, C), lambda ch, i, core_ref: (ch, core_ref[0], i, 0)), blk],
            out_specs=[blk, blk]),
    )(core, p, r1)


_SEM = pl.BlockSpec(memory_space=pltpu.SEMAPHORE)
_ANY = pl.BlockSpec(memory_space=pl.ANY)
_DATAFLOW = pltpu.SideEffectType.DATAFLOW_SIDE_EFFECTING


def _hbm(a):
    return pltpu.with_memory_space_constraint(a, pltpu.HBM)


def _gather_targets():
    x, y, c, chips = _chips()
    return 4 * x + 2 * y + c, [(x, y, 1 - c)] + [(*chip, c) for chip in chips]


def _landing_zone(w, me, name, tr=512):
    R, C = w.shape
    tr = min(tr, R)

    def body(me_ref, w_ref, o_ref):
        o_ref[...] = w_ref[...].astype(BF16)

    return pl.pallas_call(
        body, name=name, out_shape=jax.ShapeDtypeStruct((NDEV, R, C), BF16),
        grid_spec=pltpu.PrefetchScalarGridSpec(
            num_scalar_prefetch=1, grid=(R // tr,), in_specs=[pl.BlockSpec((tr, C), lambda i, me_ref: (i, 0))],
            out_specs=pl.BlockSpec((None, tr, C), lambda i, me_ref: (me_ref[0], i, 0))),
    )(me, w)


def _gather_start(land, order, name):
    def body(land_ref, order_ref, *rest):
        sems, token = rest[:8], rest[9]
        me, targets = _gather_targets()
        for k, to in enumerate(targets):
            pltpu.make_async_remote_copy(src_ref=land_ref.at[me], dst_ref=land_ref.at[me], send_sem=sems[k],
                                         recv_sem=sems[4 + k], device_id=to, device_id_type=MESH).start()
        token[...] = jnp.zeros_like(token)

    outs = pl.pallas_call(
        body, name=name,
        out_shape=(pltpu.SemaphoreType.DMA(()),) * 8 + (pltpu.HBM(land.shape, land.dtype), jax.ShapeDtypeStruct((8, 128), F32)),
        in_specs=(_HBM, _ANY), out_specs=(_SEM,) * 8 + (_HBM, pl.BlockSpec(memory_space=pltpu.VMEM)),
        input_output_aliases={0: 8}, compiler_params=pltpu.CompilerParams(has_side_effects=_DATAFLOW),
    )(_hbm(land), order)
    return outs[:8], outs[8], outs[9]


def _gather_wait(sems, land_thru, after, name):
    def body(land_ref, *rest):
        sems_ = rest[:8]
        me, targets = _gather_targets()
        for k, to in enumerate(targets):
            cp = pltpu.make_async_remote_copy(src_ref=land_ref.at[me], dst_ref=land_ref.at[me], send_sem=sems_[k],
                                              recv_sem=sems_[4 + k], device_id=to, device_id_type=MESH)
            cp.wait_send()
            cp.wait_recv()

    return pl.pallas_call(
        body, name=name, out_shape=pltpu.HBM(land_thru.shape, land_thru.dtype),
        in_specs=(_HBM,) + (_SEM,) * 8 + (_ANY,), out_specs=_HBM, input_output_aliases={0: 0},
        compiler_params=pltpu.CompilerParams(has_side_effects=_DATAFLOW),
    )(land_thru, *sems, after)


def _gather_finish(land, name):
    def body(land_ref, out, send, recv):
        x, y, c, chips = _chips()
        fwd = [pltpu.make_async_remote_copy(src_ref=out.at[4 * px + 2 * py + c], dst_ref=out.at[4 * px + 2 * py + c],
                                            send_sem=send.at[j], recv_sem=recv.at[j], device_id=(x, y, 1 - c), device_id_type=MESH)
               for j, (px, py) in enumerate(chips)]
        for cp in fwd:
            cp.start()
        for j, (px, py) in enumerate(chips):
            slot = out.at[4 * px + 2 * py + 1 - c]
            pltpu.make_async_remote_copy(src_ref=slot, dst_ref=slot, send_sem=send.at[j], recv_sem=recv.at[j],
                                         device_id=(x, y, 1 - c), device_id_type=MESH).wait()

    return pl.pallas_call(
        body, name=name, out_shape=jax.ShapeDtypeStruct(land.shape, land.dtype),
        in_specs=[_HBM], out_specs=_HBM, input_output_aliases={0: 0},
        scratch_shapes=[pltpu.SemaphoreType.DMA((3,)), pltpu.SemaphoreType.DMA((3,))],
        compiler_params=pltpu.CompilerParams(has_side_effects=True),
    )(land)


def _chip_targets():
    x, y, c, chips = _chips()
    return 2 * x + y, [((px, py, c), 2 * px + py) for px, py in chips]


def _chipsum_start(s, land, order, name):
    def body(s_ref, land_ref, order_ref, *rest):
        sems, token = rest[:6], rest[8]
        mine, targets = _chip_targets()
        for k, (to, ch) in enumerate(targets):
            pltpu.make_async_remote_copy(src_ref=s_ref.at[ch], dst_ref=land_ref.at[mine], send_sem=sems[k], recv_sem=sems[3 + k],
                                         device_id=to, device_id_type=MESH).start()
        token[...] = jnp.zeros_like(token)

    outs = pl.pallas_call(
        body, name=name,
        out_shape=(pltpu.SemaphoreType.DMA(()),) * 6 + (pltpu.HBM(s.shape, s.dtype), pltpu.HBM(land.shape, land.dtype),
                                                        jax.ShapeDtypeStruct((8, 128), F32)),
        in_specs=(_HBM, _HBM, _ANY), out_specs=(_SEM,) * 6 + (_HBM, _HBM, pl.BlockSpec(memory_space=pltpu.VMEM)),
        input_output_aliases={0: 6, 1: 7}, compiler_params=pltpu.CompilerParams(has_side_effects=_DATAFLOW),
    )(_hbm(s), _hbm(land), order)
    return outs[:6], outs[6], outs[7], outs[8]


def _chipsum_wait(sems, s_thru, land_thru, after, name):
    def body(s_ref, land_ref, *rest):
        sems_ = rest[:6]
        mine, targets = _chip_targets()
        for k, (to, ch) in enumerate(targets):
            cp = pltpu.make_async_remote_copy(src_ref=s_ref.at[ch], dst_ref=land_ref.at[ch], send_sem=sems_[k], recv_sem=sems_[3 + k],
                                              device_id=to, device_id_type=MESH)
            cp.wait_send()
            cp.wait_recv()

    return pl.pallas_call(
        body, name=name, out_shape=(pltpu.HBM(s_thru.shape, s_thru.dtype), pltpu.HBM(land_thru.shape, land_thru.dtype)),
        in_specs=(_HBM, _HBM) + (_SEM,) * 6 + (_ANY,), out_specs=(_HBM, _HBM), input_output_aliases={0: 0, 1: 1},
        compiler_params=pltpu.CompilerParams(has_side_effects=_DATAFLOW),
    )(s_thru, land_thru, *sems, after)[1]


def _pair_start(p, order, name):
    def body(p_ref, land_ref, order_ref, *rest):
        sems, token = rest[:8], rest[10]
        x, y, c = _me()
        for ch in range(4):
            pltpu.make_async_remote_copy(src_ref=p_ref.at[ch, 1 - c], dst_ref=land_ref.at[ch], send_sem=sems[ch],
                                         recv_sem=sems[4 + ch], device_id=(x, y, 1 - c), device_id_type=MESH).start()
        token[...] = jnp.zeros_like(token)

    land = lax.empty((4,) + p.shape[2:], p.dtype)
    outs = pl.pallas_call(
        body, name=name,
        out_shape=(pltpu.SemaphoreType.DMA(()),) * 8 + (pltpu.HBM(p.shape, p.dtype), pltpu.HBM(land.shape, land.dtype),
                                                        jax.ShapeDtypeStruct((8, 128), F32)),
        in_specs=(_HBM, _HBM, _ANY), out_specs=(_SEM,) * 8 + (_HBM, _HBM, pl.BlockSpec(memory_space=pltpu.VMEM)),
        input_output_aliases={0: 8, 1: 9}, compiler_params=pltpu.CompilerParams(has_side_effects=_DATAFLOW),
    )(_hbm(p), _hbm(land), order)
    return outs[:8], outs[8], outs[9], outs[10]


def _pair_wait(sems, p_thru, land_thru, after, name):
    def body(p_ref, land_ref, *rest):
        sems_ = rest[:8]
        x, y, c = _me()
        for ch in range(4):
            cp = pltpu.make_async_remote_copy(src_ref=p_ref.at[ch, 1 - c], dst_ref=land_ref.at[ch], send_sem=sems_[ch],
                                              recv_sem=sems_[4 + ch], device_id=(x, y, 1 - c), device_id_type=MESH)
            cp.wait_send()
            cp.wait_recv()

    return pl.pallas_call(
        body, name=name, out_shape=(pltpu.HBM(p_thru.shape, p_thru.dtype), pltpu.HBM(land_thru.shape, land_thru.dtype)),
        in_specs=(_HBM, _HBM) + (_SEM,) * 8 + (_ANY,), out_specs=(_HBM, _HBM), input_output_aliases={0: 0, 1: 1},
        compiler_params=pltpu.CompilerParams(has_side_effects=_DATAFLOW),
    )(p_thru, land_thru, *sems, after)


def _sibling_start(p, order, name):
    def body(p_ref, land_ref, order_ref, send_sem, recv_sem, p_thru, land_thru, token):
        x, y, c = _me()
        pltpu.make_async_remote_copy(src_ref=p_ref, dst_ref=land_ref, send_sem=send_sem, recv_sem=recv_sem,
                                     device_id=(x, y, 1 - c), device_id_type=MESH).start()
        token[...] = jnp.zeros_like(token)

    land = lax.empty(p.shape, p.dtype)
    outs = pl.pallas_call(
        body, name=name,
        out_shape=(pltpu.SemaphoreType.DMA(()),) * 2 + (pltpu.HBM(p.shape, p.dtype), pltpu.HBM(p.shape, p.dtype),
                                                        jax.ShapeDtypeStruct((8, 128), F32)),
        in_specs=(_HBM, _HBM, _ANY), out_specs=(_SEM,) * 2 + (_HBM, _HBM, pl.BlockSpec(memory_space=pltpu.VMEM)),
        input_output_aliases={0: 2, 1: 3}, compiler_params=pltpu.CompilerParams(has_side_effects=_DATAFLOW),
    )(_hbm(p), _hbm(land), order)
    return outs[:2], outs[2], outs[3], outs[4]


def _sibling_wait(sems, p_thru, land_thru, after, name):
    def body(p_ref, land_ref, send_sem, recv_sem, after_ref, p_dead, got_ref):
        x, y, c = _me()
        cp = pltpu.make_async_remote_copy(src_ref=p_ref, dst_ref=land_ref, send_sem=send_sem, recv_sem=recv_sem,
                                          device_id=(x, y, 1 - c), device_id_type=MESH)
        cp.wait_send()
        cp.wait_recv()

    return pl.pallas_call(
        body, name=name, out_shape=(pltpu.HBM(p_thru.shape, p_thru.dtype), pltpu.HBM(land_thru.shape, land_thru.dtype)),
        in_specs=(_HBM, _HBM, _SEM, _SEM, _ANY), out_specs=(_HBM, _HBM), input_output_aliases={0: 0, 1: 1},
        compiler_params=pltpu.CompilerParams(has_side_effects=_DATAFLOW),
    )(p_thru, land_thru, *sems, after)[1]


def _mm(a, b, *, mode, name, out_dtypes=(F32,), epilogue=None, extras=(), tm=1024, tn=1024, tk=2048,
        col_blocked_b=False, col_blocked_out=False, order=None):
    CB = 1024
    if col_blocked_b:
        assert mode in ("nn", "nt") and b.shape[2] == CB
        (M, K), N = a.shape, (b.shape[0] * CB if mode == "nn" else b.shape[1])
        assert mode == "nn" or tk % CB == 0
        tn = CB if mode == "nn" else tn
    elif mode == "nn":
        (M, K), N = a.shape, b.shape[1]
    elif mode == "nt":
        (M, K), N = a.shape, b.shape[0]
    else:
        (K, M), N = a.shape, b.shape[1]
    if col_blocked_out:
        assert len(out_dtypes) == 1 and N % CB == 0
        tn = CB
    tm, tn, tk = min(tm, M), min(tn, N), min(tk, K)
    assert M % tm == 0 and N % tn == 0 and K % tk == 0, (M, N, K, tm, tn, tk)
    nk = K // tk
    ne, no = len(extras), len(out_dtypes)
    dims = {"nn": (((1,), (0,)), ((), ())), "nt": (((1,), (1,)), ((), ())), "tn": (((0,), (0,)), ((), ()))}[mode]

    no_ = 0 if order is None else 1

    def body(a_ref, b_ref, *rest):
        rest = rest[no_:]
        ex, outs = rest[:ne], rest[ne:ne + no]

        def finish(acc):
            res = epilogue(acc, *[e[...] for e in ex]) if epilogue is not None else (acc,)
            for o, r in zip(outs, res):
                o[...] = r.astype(o.dtype)

        if col_blocked_b and mode == "nt":
            part = sum(lax.dot_general(a_ref[:, q * CB:(q + 1) * CB], b_ref[q], dims, preferred_element_type=F32)
                       for q in range(tk // CB))
        else:
            part = lax.dot_general(a_ref[...], b_ref[...], dims, preferred_element_type=F32)
        if nk == 1:
            finish(part)
        else:
            acc_ref = rest[-1]
            k = pl.program_id(2)

            @pl.when(k == 0)
            def _():
                acc_ref[...] = part

            @pl.when(k > 0)
            def _():
                acc_ref[...] += part

            @pl.when(k == nk - 1)
            def _():
                finish(acc_ref[...])

    a_spec = {"nn": pl.BlockSpec((tm, tk), lambda i, j, k: (i, k)), "nt": pl.BlockSpec((tm, tk), lambda i, j, k: (i, k)),
              "tn": pl.BlockSpec((tk, tm), lambda i, j, k: (k, i))}[mode]
    b_spec = {"nn": pl.BlockSpec((tk, tn), lambda i, j, k: (k, j)), "nt": pl.BlockSpec((tn, tk), lambda i, j, k: (j, k)),
              "tn": pl.BlockSpec((tk, tn), lambda i, j, k: (k, j))}[mode]
    if col_blocked_b:
        b_spec = (pl.BlockSpec((None, tk, CB), lambda i, j, k: (j, k, 0)) if mode == "nn"
                  else pl.BlockSpec((tk // CB, tn, CB), lambda i, j, k: (k, j, 0)))
    e_spec = pl.BlockSpec((tm, tn), lambda i, j, k: (i, j))
    o_spec, o_dims = e_spec, (M, N)
    if col_blocked_out:
        o_spec, o_dims = pl.BlockSpec((None, tm, CB), lambda i, j, k: (j, i, 0)), (N // CB, M, CB)
    outs = pl.pallas_call(
        body, name=name, grid=(M // tm, N // tn, nk),
        in_specs=[a_spec, b_spec] + [_ANY] * no_ + [e_spec] * ne, out_specs=[o_spec] * no,
        out_shape=[jax.ShapeDtypeStruct(o_dims, dt) for dt in out_dtypes],
        scratch_shapes=[pltpu.VMEM((tm, tn), F32)] if nk > 1 else [],
        compiler_params=pltpu.CompilerParams(dimension_semantics=("parallel", "parallel", "arbitrary")),
    )(a, b, *([] if order is None else [order]), *extras)
    return outs if no > 1 else outs[0]


def _dw_half(a, b, side, *, axis, name, add=None, order=None, tile=1024, tk=2048):
    (K, M), N = a.shape, b.shape[1]
    tk = min(tk, K)
    nk = K // tk
    if axis == "m":
        tm, tn = tile, min(N, 1024)
        grid, o_dims = (4, N // tn, nk), (4, tile, N)
        a_spec = pl.BlockSpec((tk, tm), lambda q, j, k, s: (k, 2 * q + s[0]))
        b_spec = pl.BlockSpec((tk, tn), lambda q, j, k, s: (k, j))
        o_spec = pl.BlockSpec((None, tm, tn), lambda q, j, k, s: (q, 0, j))
    else:
        tm, tn = min(M, 1024), tile
        grid, o_dims = (M // tm, 4, nk), (4, M, tile)
        a_spec = pl.BlockSpec((tk, tm), lambda i, q, k, s: (k, i))
        b_spec = pl.BlockSpec((tk, tn), lambda i, q, k, s: (k, 2 * q + s[0]))
        o_spec = pl.BlockSpec((None, tm, tn), lambda i, q, k, s: (q, i, 0))
    n_order, n_add = int(order is not None), int(add is not None)
    n_out = 1 + n_add

    def body(s_ref, a_ref, b_ref, *rest):
        rest = rest[n_order:]
        outs, acc_ref = rest[n_add:n_add + n_out], rest[-1]
        k = pl.program_id(2)
        part = _dot_tn(a_ref[...], b_ref[...])

        @pl.when(k == 0)
        def _():
            acc_ref[...] = part

        @pl.when(k > 0)
        def _():
            acc_ref[...] += part

        @pl.when(k == nk - 1)
        def _():
            res = acc_ref[...] + rest[0][...].astype(F32) if n_add else acc_ref[...]
            for o in outs:
                o[...] = res.astype(o.dtype)

    outs = pl.pallas_call(
        body, name=name, out_shape=[jax.ShapeDtypeStruct(o_dims, BF16)] * n_out,
        grid_spec=pltpu.PrefetchScalarGridSpec(
            num_scalar_prefetch=1, grid=grid, in_specs=[a_spec, b_spec] + [_ANY] * n_order + [o_spec] * n_add,
            out_specs=[o_spec] * n_out, scratch_shapes=[pltpu.VMEM((tm, tn), F32)]),
        compiler_params=pltpu.CompilerParams(dimension_semantics=("arbitrary", "arbitrary", "arbitrary")),
    )(side, a, b, *([order] if n_order else []), *([add] if n_add else []))
    return outs if n_add else outs[0]


def _norm_fwd(x, g, sc, sh, name, resid=None):
    B, S, Dm = x.shape
    ts = min(S, 256)
    tok = pl.BlockSpec((None, ts, Dm), lambda b, i: (b, i, 0))
    row = pl.BlockSpec((None, 1, Dm), lambda b, i: (b, 0, 0))
    par = pl.BlockSpec((1, Dm), lambda b, i: (0, 0))

    def body(*refs):
        if resid is not None:
            x_ref, br_ref, gt_ref, g_ref, sc_ref, sh_ref, xo_ref, h_ref = refs
            xv = x_ref[...] + gt_ref[...] * br_ref[...]
            xo_ref[...] = xv
        else:
            x_ref, g_ref, sc_ref, sh_ref, h_ref = refs
            xv = x_ref[...]
        r = lax.rsqrt(jnp.mean(xv * xv, axis=-1, keepdims=True) + EPS)
        h_ref[...] = ((xv * r * g_ref[...]) * (1.0 + sc_ref[...]) + sh_ref[...]).astype(BF16)

    h_shape = jax.ShapeDtypeStruct((B, S, Dm), BF16)
    if resid is not None:
        return pl.pallas_call(body, name=name, grid=(B, S // ts), in_specs=[tok, tok, row, par, row, row],
                              out_specs=[tok, tok], out_shape=[jax.ShapeDtypeStruct((B, S, Dm), F32), h_shape],
                              )(x, resid[0], resid[1], g, sc, sh)
    return pl.pallas_call(body, name=name, grid=(B, S // ts), in_specs=[tok, par, row, row], out_specs=tok,
                          out_shape=h_shape)(x, g, sc, sh)


def _norm_bwd(x, g, name, *, sc=None, dh=None, dres=None, tgt=None, br=None, gate=None, x_is_prev=False):
    B, S, Dm = x.shape
    ts = min(S, 256)
    final = tgt is not None
    has_br = br is not None
    tok = pl.BlockSpec((None, ts, Dm), lambda b, i: (b, i, 0))
    row = pl.BlockSpec((None, 1, Dm), lambda b, i: (b, 0, 0))
    par = pl.BlockSpec((1, Dm), lambda b, i: (0, 0))
    ins, in_specs = [x, g], [tok, par]
    if final:
        ins, in_specs = ins + [tgt], in_specs + [tok]
    else:
        ins, in_specs = ins + [sc, dh], in_specs + [row, tok]
    if dres is not None:
        ins, in_specs = ins + [dres], in_specs + [tok]
    if has_br:
        ins, in_specs = ins + [br, gate], in_specs + [tok, row]
    n_in = len(ins)
    out_shape = [jax.ShapeDtypeStruct((B, S, Dm), F32), jax.ShapeDtypeStruct((1, Dm), F32)]
    out_specs = [tok, par]
    if final:
        out_shape.append(jax.ShapeDtypeStruct((1, 128), F32))
        out_specs.append(pl.BlockSpec((1, 128), lambda b, i: (0, 0)))
    else:
        out_shape += [jax.ShapeDtypeStruct((B, 1, Dm), F32)] * 2
        out_specs += [row, row]
    if has_br:
        out_shape += [jax.ShapeDtypeStruct((B, S, Dm), BF16), jax.ShapeDtypeStruct((B, 1, Dm), F32)]
        out_specs += [tok, row]

    def body(*refs):
        it = iter(refs[:n_in])
        outs = iter(refs[n_in:])
        x_ref, g_ref = next(it), next(it)
        b, i = pl.program_id(0), pl.program_id(1)
        first, first_row = (b == 0) & (i == 0), i == 0
        xv, gv = x_ref[...], g_ref[...]
        if x_is_prev:
            xv = xv + refs[n_in - 1][...] * refs[n_in - 2][...]
        r = lax.rsqrt(jnp.mean(xv * xv, axis=-1, keepdims=True) + EPS)
        n = xv * r
        dx_ref, dg_ref = next(outs), next(outs)

        def acc(ref, val, init):
            @pl.when(init)
            def _():
                ref[...] = val

            @pl.when(jnp.logical_not(init))
            def _():
                ref[...] += val

        if final:
            t_ref = next(it)
            loss_ref = next(outs)
            e = n * gv - t_ref[...]
            acc(loss_ref, jnp.zeros((1, 128), F32) + 0.5 * jnp.sum(e * e) / Dm, first)
            dyg = e * (1.0 / Dm)
        else:
            sc_ref, dh_ref = next(it), next(it)
            dsc_ref, dsh_ref = next(outs), next(outs)
            dhv = dh_ref[...]
            acc(dsh_ref, jnp.sum(dhv, axis=0, keepdims=True), first_row)
            acc(dsc_ref, jnp.sum(dhv * (n * gv), axis=0, keepdims=True), first_row)
            dyg = dhv * (1.0 + sc_ref[...])
        acc(dg_ref, jnp.sum(dyg * n, axis=0, keepdims=True), first)
        dn = dyg * gv
        dx = r * (dn - n * jnp.mean(dn * n, axis=-1, keepdims=True))
        if dres is not None:
            dx = dx + next(it)[...]
        dx_ref[...] = dx
        if has_br:
            br_ref, gt_ref = next(it), next(it)
            dbr_ref, dgt_ref = next(outs), next(outs)
            dbr_ref[...] = (dx * gt_ref[...]).astype(BF16)
            acc(dgt_ref, jnp.sum(dx * br_ref[...], axis=0, keepdims=True), first_row)

    outs = pl.pallas_call(body, name=name, grid=(B, S // ts), in_specs=in_specs, out_specs=out_specs, out_shape=out_shape,
                          compiler_params=pltpu.CompilerParams(dimension_semantics=("arbitrary", "arbitrary")))(*ins)
    res = dict(dx=outs[0], dg=outs[1])
    if final:
        res["loss"] = outs[2]
    else:
        res["dsc"], res["dsh"] = outs[2], outs[3]
    if has_br:
        res["dbr"], res["dgate"] = outs[-2], outs[-1]
    return res


def _gm_heads(vg, lng, lnb):
    res = []
    for h in range(GM_H):
        sl = slice(h * 128, (h + 1) * 128)
        vh = vg[:, sl]
        xc = vh - jnp.mean(vh, axis=-1, keepdims=True)
        rstd = lax.rsqrt(jnp.mean(xc * xc, axis=-1, keepdims=True) + 1e-5)
        xhat = xc * rstd
        res.append((xhat, rstd, xhat * lng[:, sl] + lnb[:, sl]))
    return res


def _gm_gate(heads, wt_ref, bsx, nch):
    cols = []
    for h in range(GM_H):
        vn = heads[h][2].astype(BF16)
        rows = [_dot(wt_ref[h], vn[c * CHUNK:(c + 1) * CHUNK]) + bsx[:, h * 128:(h + 1) * 128] for c in range(nch)]
        cols.append(jnp.concatenate(rows, axis=0) if nch > 1 else rows[0])
    return jnp.concatenate(cols, axis=1)


def _gm_specs(S):
    tb = min(S, 512)
    u = pl.BlockSpec((None, tb, GM_W), lambda b, i: (b, i, OFF["u"] // GM_W))
    v = pl.BlockSpec((None, tb, GM_W), lambda b, i: (b, i, OFF["v"] // GM_W))
    tok = pl.BlockSpec((None, tb, GM_W), lambda b, i: (b, i, 0))
    return tb, u, v, tok


def _gmlp_fwd(P, lng, lnb, wt, bsx, og, name):
    B, S, _ = P.shape
    tb, u_spec, v_spec, tok = _gm_specs(S)
    nch = tb // CHUNK

    def body(u_ref, v_ref, lng_ref, lnb_ref, wt_ref, bsx_ref, og_ref, o_ref):
        heads = _gm_heads(_gelu(v_ref[...]), lng_ref[...], lnb_ref[...])
        y = _gelu(u_ref[...]) * _gm_gate(heads, wt_ref, bsx_ref[...], nch)
        r = lax.rsqrt(jnp.mean(y * y, axis=-1, keepdims=True) + EPS)
        o_ref[...] = (y * r * og_ref[...]).astype(BF16)

    return pl.pallas_call(
        body, name=name, grid=(B, S // tb),
        in_specs=[u_spec, v_spec, _full((1, GM_W)), _full((1, GM_W)), _full((GM_H, 128, 128)), _full((128, GM_W)), _full((1, GM_W))],
        out_specs=tok, out_shape=jax.ShapeDtypeStruct((B, S, GM_W + ATT_W + SSM_W), BF16))(P, P, lng, lnb, wt, bsx, og)


def _gmlp_bwd(P, dcat, lng, lnb, wt, wtT, bsx, og, name):
    B, S, _ = P.shape
    tb, u_spec, v_spec, tok = _gm_specs(S)
    nch = tb // CHUNK
    do_spec = pl.BlockSpec((None, tb, GM_W), lambda b, i: (b, i, 0))

    def body(u_ref, v_ref, do_ref, lng_ref, lnb_ref, wt_ref, wtT_ref, bsx_ref, og_ref,
             du_ref, dv_ref, dlng_ref, dlnb_ref, dws_ref, dbsx_ref, dog_ref):
        first = (pl.program_id(0) == 0) & (pl.program_id(1) == 0)

        @pl.when(first)
        def _():
            for ref in (dlng_ref, dlnb_ref, dws_ref, dbsx_ref, dog_ref):
                ref[...] = jnp.zeros(ref.shape, F32)

        u, v, lng = u_ref[...], v_ref[...], lng_ref[...]
        ug = _gelu(u)
        heads = _gm_heads(_gelu(v), lng, lnb_ref[...])
        gate = _gm_gate(heads, wt_ref, bsx_ref[...], nch)
        y = ug * gate
        r = lax.rsqrt(jnp.mean(y * y, axis=-1, keepdims=True) + EPS)
        yn = y * r
        dout = do_ref[...]
        dog_ref[...] += jnp.sum(dout * yn, axis=0, keepdims=True)
        dyn = dout * og_ref[...]
        dy = r * (dyn - yn * jnp.mean(dyn * yn, axis=-1, keepdims=True))
        du_ref[...] = (dy * gate * _gelu_grad(u)).astype(BF16)
        dgate = dy * ug
        tril = lax.broadcasted_iota(jnp.int32, (128, 128), 0) >= lax.broadcasted_iota(jnp.int32, (128, 128), 1)
        dvg = []
        for h in range(GM_H):
            sl = slice(h * 128, (h + 1) * 128)
            xhat, rstd, vn = heads[h]
            vnb = vn.astype(BF16)
            dgh = dgate[:, sl]
            dgb = dgh.astype(BF16)
            dbs = jnp.zeros((128, 128), F32)
            dw = jnp.zeros((128, 128), F32)
            dvn = []
            for c in range(nch):
                rs = slice(c * CHUNK, (c + 1) * CHUNK)
                dbs = dbs + dgh[rs]
                dw = dw + _dot_nt(dgb[rs], vnb[rs])
                dvn.append(_dot(wtT_ref[h], dgb[rs]))
            dvn = jnp.concatenate(dvn, axis=0) if nch > 1 else dvn[0]
            dbsx_ref[:, sl] += dbs
            dws_ref[h] += jnp.where(tril, dw, 0.0)
            dlng_ref[:, sl] += jnp.sum(dvn * xhat, axis=0, keepdims=True)
            dlnb_ref[:, sl] += jnp.sum(dvn, axis=0, keepdims=True)
            dxh = dvn * lng[:, sl]
            dvg.append(rstd * (dxh - jnp.mean(dxh, axis=-1, keepdims=True) - xhat * jnp.mean(dxh * xhat, axis=-1, keepdims=True)))
        dv_ref[...] = (jnp.concatenate(dvg, axis=1) * _gelu_grad(v)).astype(BF16)

    p512, w3 = _full((1, GM_W)), _full((GM_H, 128, 128))
    return pl.pallas_call(
        body, name=name, grid=(B, S // tb),
        in_specs=[u_spec, v_spec, do_spec, p512, p512, w3, w3, _full((128, GM_W)), p512],
        out_specs=[tok, tok, p512, p512, w3, _full((128, GM_W)), p512],
        out_shape=[jax.ShapeDtypeStruct((B, S, GM_W), BF16)] * 2 + [
            jax.ShapeDtypeStruct((1, GM_W), F32), jax.ShapeDtypeStruct((1, GM_W), F32),
            jax.ShapeDtypeStruct((GM_H, 128, 128), F32), jax.ShapeDtypeStruct((128, GM_W), F32),
            jax.ShapeDtypeStruct((1, GM_W), F32)],
        compiler_params=pltpu.CompilerParams(dimension_semantics=("arbitrary", "arbitrary")),
    )(P, P, dcat, lng, lnb, wt, wtT, bsx, og)


def _lane_half():
    return lax.broadcasted_iota(jnp.int32, (128, 128), 1) // 64


def _att_stack(x, kvh, dtype):
    half = _lane_half()
    rows = []
    for g in range(4):
        i = kvh * 4 + g
        pair = x[:, (i // 2) * 128:(i // 2 + 1) * 128]
        if i % 2 != kvh:
            pair = pltpu.roll(pair, 64, 1)
        rows.append(jnp.where(half == kvh, pair, 0.0))
    return jnp.concatenate(rows, axis=0).astype(dtype)


def _att_unstack(pairs, y, kvh):
    half = _lane_half()
    for g in range(4):
        i = kvh * 4 + g
        piece = y[g * 128:(g + 1) * 128]
        if i % 2 != kvh:
            piece = pltpu.roll(piece, 64, 1)
        pairs[i // 2] = jnp.where(half == i % 2, piece, pairs[i // 2])
    return pairs


def _att_probs(qb, k2, st, sink_ref, kvh):
    qm = _att_stack(qb, kvh, BF16)
    s = _dot_nt(qm, k2) * (64 ** -0.5)
    qi = lax.broadcasted_iota(jnp.int32, (512, 256), 0) % 128
    kj = lax.broadcasted_iota(jnp.int32, (512, 256), 1)
    diff = qi + 128 - kj
    valid = (diff >= 0) & (diff < 128) & (st + kj - 128 >= 0)
    s = jnp.where(valid, s, NEG_INF)
    grp = lax.broadcasted_iota(jnp.int32, (512, 1), 0) // 128
    sink = jnp.zeros((512, 1), F32)
    for g in range(4):
        sink = jnp.where(grp == g, sink_ref[kvh * 4 + g], sink)
    m = jnp.maximum(jnp.max(s, axis=-1, keepdims=True), sink)
    e = jnp.exp(s - m)
    esink = jnp.exp(sink - m)
    inv = 1.0 / (jnp.sum(e, axis=-1, keepdims=True) + esink)
    return qm, e * inv, esink * inv


def _att_specs(S):
    q = pl.BlockSpec((None, S, ATT_W), lambda b: (b, 0, OFF["q"] // ATT_W))
    k = pl.BlockSpec((None, S, KV_W), lambda b: (b, 0, OFF["k"] // KV_W))
    v = pl.BlockSpec((None, S, KV_W), lambda b: (b, 0, OFF["vv"] // KV_W))
    tok = pl.BlockSpec((None, S, ATT_W), lambda b: (b, 0, 0))
    kv = pl.BlockSpec((None, S, KV_W), lambda b: (b, 0, 0))
    return q, k, v, tok, kv


_SMEM = pl.BlockSpec(memory_space=pltpu.SMEM)


def _attn_fwd(P, sinks, og, cat, name):
    B, S, _ = P.shape
    q_spec, k_spec, v_spec, _, _ = _att_specs(S)
    tok = pl.BlockSpec((None, S, ATT_W), lambda b: (b, 0, GM_W // ATT_W))

    def body(q_ref, k_ref, v_ref, sink_ref, og_ref, cat_ref, o_ref, kpad, vpad):
        kpad[0:128, :] = jnp.zeros((128, KV_W), BF16)
        vpad[0:128, :] = jnp.zeros((128, KV_W), BF16)
        kpad[128:, :] = k_ref[...].astype(BF16)
        vpad[128:, :] = v_ref[...].astype(BF16)

        def step(n, carry):
            st = pl.multiple_of(n * 128, 128)
            qb = q_ref[pl.ds(st, 128), :]
            k2, v2 = kpad[pl.ds(st, 256), :], vpad[pl.ds(st, 256), :]
            pairs = [jnp.zeros((128, 128), F32)] * 4
            for kvh in range(2):
                _, p, _ = _att_probs(qb, k2, st, sink_ref, kvh)
                pairs = _att_unstack(pairs, _dot(p.astype(BF16), v2), kvh)
            o = jnp.concatenate(pairs, axis=1)
            r = lax.rsqrt(jnp.mean(o * o, axis=-1, keepdims=True) + EPS)
            o_ref[pl.ds(st, 128), :] = (o * r * og_ref[...]).astype(BF16)
            return carry

        lax.fori_loop(0, S // 128, step, 0)

    return pl.pallas_call(
        body, name=name, grid=(B,), in_specs=[q_spec, k_spec, v_spec, _SMEM, _full((1, ATT_W)), _ANY], out_specs=tok,
        out_shape=jax.ShapeDtypeStruct(cat.shape, BF16), input_output_aliases={5: 0},
        scratch_shapes=[pltpu.VMEM((S + 128, KV_W), BF16)] * 2)(P, P, P, sinks, og, cat)


def _attn_bwd(P, dcat, sinks, og, name):
    B, S, _ = P.shape
    q_spec, k_spec, v_spec, tok, kv = _att_specs(S)
    do_spec = pl.BlockSpec((None, S, ATT_W), lambda b: (b, 0, GM_W // ATT_W))

    def body(q_ref, k_ref, v_ref, do_ref, sink_ref, og_ref, dq_ref, dk_ref, dv_ref, dsink_ref, dog_ref,
             kpad, vpad, dkpad, dvpad):
        @pl.when(pl.program_id(0) == 0)
        def _():
            dsink_ref[...] = jnp.zeros((8, 128), F32)
            dog_ref[...] = jnp.zeros((1, ATT_W), F32)

        kpad[0:128, :] = jnp.zeros((128, KV_W), BF16)
        vpad[0:128, :] = jnp.zeros((128, KV_W), BF16)
        kpad[128:, :] = k_ref[...].astype(BF16)
        vpad[128:, :] = v_ref[...].astype(BF16)
        dkpad[...] = jnp.zeros((S + 128, KV_W), F32)
        dvpad[...] = jnp.zeros((S + 128, KV_W), F32)
        half = _lane_half()
        head_row = lax.broadcasted_iota(jnp.int32, (8, 128), 0)

        def step(n, carry):
            st = pl.multiple_of(n * 128, 128)
            qb = q_ref[pl.ds(st, 128), :]
            k2, v2 = kpad[pl.ds(st, 256), :], vpad[pl.ds(st, 256), :]
            saved, pairs = [], [jnp.zeros((128, 128), F32)] * 4
            for kvh in range(2):
                qm, p, psink = _att_probs(qb, k2, st, sink_ref, kvh)
                o = _dot(p.astype(BF16), v2)
                saved.append((qm, p, psink, o))
                pairs = _att_unstack(pairs, o, kvh)
            o = jnp.concatenate(pairs, axis=1)
            r = lax.rsqrt(jnp.mean(o * o, axis=-1, keepdims=True) + EPS)
            on = o * r
            dout = do_ref[pl.ds(st, 128), :]
            dog_ref[...] += jnp.sum(dout * on, axis=0, keepdims=True)
            dyn = dout * og_ref[...]
            do = r * (dyn - on * jnp.mean(dyn * on, axis=-1, keepdims=True))
            dq_pairs = [jnp.zeros((128, 128), F32)] * 4
            dsink = jnp.zeros((8, 128), F32)
            for kvh in range(2):
                qm, p, psink, og_ = saved[kvh]
                dog = _att_stack(do, kvh, F32)
                delta = jnp.sum(dog * jnp.where(jnp.concatenate([half] * 4, axis=0) == kvh, og_, 0.0), axis=-1, keepdims=True)
                dogb, pb = dog.astype(BF16), p.astype(BF16)
                dvpad[pl.ds(st, 256), :] += _dot_tn(pb, dogb)
                dp = _dot_nt(dogb, v2)
                ds = (p * (dp - delta) * (64 ** -0.5)).astype(BF16)
                sd = psink * delta
                for g in range(4):
                    dsink = dsink - jnp.where(head_row == kvh * 4 + g, jnp.sum(sd[g * 128:(g + 1) * 128]), 0.0)
                dq_pairs = _att_unstack(dq_pairs, _dot(ds, k2), kvh)
                dkpad[pl.ds(st, 256), :] += _dot_tn(ds, qm)
            dsink_ref[...] += dsink
            dq_ref[pl.ds(st, 128), :] = jnp.concatenate(dq_pairs, axis=1).astype(BF16)
            return carry

        lax.fori_loop(0, S // 128, step, 0)
        dk_ref[...] = dkpad[128:, :].astype(BF16)
        dv_ref[...] = dvpad[128:, :].astype(BF16)

    return pl.pallas_call(
        body, name=name, grid=(B,),
        in_specs=[q_spec, k_spec, v_spec, do_spec, _SMEM, _full((1, ATT_W))],
        out_specs=[tok, kv, kv, _full((8, 128)), _full((1, ATT_W))],
        out_shape=[jax.ShapeDtypeStruct((B, S, ATT_W), BF16), jax.ShapeDtypeStruct((B, S, KV_W), BF16),
                   jax.ShapeDtypeStruct((B, S, KV_W), BF16), jax.ShapeDtypeStruct((8, 128), F32),
                   jax.ShapeDtypeStruct((1, ATT_W), F32)],
        scratch_shapes=[pltpu.VMEM((S + 128, KV_W), BF16)] * 2 + [pltpu.VMEM((S + 128, KV_W), F32)] * 2,
        compiler_params=pltpu.CompilerParams(dimension_semantics=("arbitrary",)),
    )(P, P, P, dcat, sinks, og)


CONV_TC = 256
CONV_RC = 64


def _conv_taps(ext, r0):
    return [ext[pl.ds(r0 + 8 - k, CONV_RC), :] for k in range(4)]


def _conv_pre(taps, w_ref, b_ref):
    acc = b_ref[...] + w_ref[3:4, :] * taps[0]
    for k in range(1, 4):
        acc = acc + w_ref[3 - k:4 - k, :] * taps[k]
    return acc


def _conv_fwd(P, w8, b, name):
    B, S, _ = P.shape
    nj = CONV_CH // CONV_TC
    x_spec = pl.BlockSpec((None, S, CONV_TC), lambda b_, j: (b_, 0, OFF["xbc"] // CONV_TC + j))
    tok = pl.BlockSpec((None, S, CONV_TC), lambda b_, j: (b_, 0, j))

    def body(x_ref, w_ref, b_ref, o_ref, ext):
        ext[0:8, :] = jnp.zeros((8, CONV_TC), F32)
        ext[8:, :] = x_ref[...]
        for r0 in range(0, S, CONV_RC):
            pre = _conv_pre(_conv_taps(ext, r0), w_ref, b_ref)
            o_ref[pl.ds(r0, CONV_RC), :] = pre * _sigmoid(pre)

    return pl.pallas_call(
        body, name=name, grid=(B, nj),
        in_specs=[x_spec, pl.BlockSpec((8, CONV_TC), lambda b_, j: (0, j)), pl.BlockSpec((1, CONV_TC), lambda b_, j: (0, j))],
        out_specs=tok, out_shape=jax.ShapeDtypeStruct((B, S, CONV_CH), F32),
        scratch_shapes=[pltpu.VMEM((S + 8, CONV_TC), F32)])(P, w8, b)


def _conv_bwd(P, dact, w8, b, name):
    B, S, _ = P.shape
    nj = CONV_CH // CONV_TC
    x_spec = pl.BlockSpec((None, S, CONV_TC), lambda j, b_: (b_, 0, OFF["xbc"] // CONV_TC + j))
    tok = pl.BlockSpec((None, S, CONV_TC), lambda j, b_: (b_, 0, j))
    w_spec = pl.BlockSpec((8, CONV_TC), lambda j, b_: (0, j))
    b_spec = pl.BlockSpec((1, CONV_TC), lambda j, b_: (0, j))

    def body(x_ref, d_ref, w_ref, b_ref, dx_ref, dw_ref, db_ref, ext, extd):
        @pl.when(pl.program_id(1) == 0)
        def _():
            dw_ref[...] = jnp.zeros((8, CONV_TC), F32)
            db_ref[...] = jnp.zeros((1, CONV_TC), F32)

        ext[0:8, :] = jnp.zeros((8, CONV_TC), F32)
        ext[8:, :] = x_ref[...]
        extd[pl.ds(8 + S, 8), :] = jnp.zeros((8, CONV_TC), F32)
        db = jnp.zeros((1, CONV_TC), F32)
        dws = [jnp.zeros((1, CONV_TC), F32)] * 4
        for r0 in range(0, S, CONV_RC):
            taps = _conv_taps(ext, r0)
            pre = _conv_pre(taps, w_ref, b_ref)
            sg = _sigmoid(pre)
            dpre = d_ref[pl.ds(r0, CONV_RC), :] * (sg * (1.0 + pre * (1.0 - sg)))
            extd[pl.ds(8 + r0, CONV_RC), :] = dpre
            db = db + jnp.sum(dpre, axis=0, keepdims=True)
            dws = [dws[i] + jnp.sum(dpre * taps[3 - i], axis=0, keepdims=True) for i in range(4)]
        for r0 in range(0, S, CONV_RC):
            dx = w_ref[3:4, :] * extd[pl.ds(8 + r0, CONV_RC), :]
            for k in range(1, 4):
                dx = dx + w_ref[3 - k:4 - k, :] * extd[pl.ds(8 + r0 + k, CONV_RC), :]
            dx_ref[pl.ds(r0, CONV_RC), :] = dx.astype(BF16)
        db_ref[...] += db
        sub = lax.broadcasted_iota(jnp.int32, (8, CONV_TC), 0)
        dw_ref[...] += sum(jnp.where(sub == i, dws[i], 0.0) for i in range(4))

    return pl.pallas_call(
        body, name=name, grid=(nj, B), in_specs=[x_spec, tok, w_spec, b_spec], out_specs=[tok, w_spec, b_spec],
        out_shape=[jax.ShapeDtypeStruct((B, S, CONV_CH), BF16), jax.ShapeDtypeStruct((8, CONV_CH), F32),
                   jax.ShapeDtypeStruct((1, CONV_CH), F32)],
        scratch_shapes=[pltpu.VMEM((S + 8, CONV_TC), F32), pltpu.VMEM((S + 16, CONV_TC), F32)],
        compiler_params=pltpu.CompilerParams(dimension_semantics=("arbitrary", "arbitrary")),
    )(P, dact, w8, b)


def _ssd_consts():
    hd = np.arange(SSM_W) // SSM_HD
    E = (np.arange(128)[:, None] == hd[None, :]).astype(np.float32)
    tri = (np.arange(128)[:, None] >= np.arange(128)[None, :]).astype(np.float32)
    return jnp.asarray(E, BF16), jnp.asarray(E.T, BF16), jnp.asarray(tri, BF16), jnp.asarray(tri.T, BF16)


def _pieces(x, n):
    out, r = [], x
    for _ in range(n):
        p = r.astype(BF16)
        out.append(p)
        r = r - p.astype(F32)
    return out


def _dot01(x, m01, n):
    return sum(_dot(p, m01) for p in _pieces(x, n))


def _dot01_left(m01, x, n):
    return sum(_dot(m01, p) for p in _pieces(x, n))


def _ssd_pre(xa, dtraw, bias, alog, E, tri):
    lane = lax.broadcasted_iota(jnp.int32, (128, 128), 1)
    pre = dtraw + bias
    dtp = jnp.where(lane < SSM_H, jnp.maximum(pre, 0.0) + jnp.log(1.0 + jnp.exp(-jnp.abs(pre))), 0.0)
    a = -jnp.exp(alog)
    acs = _dot01_left(tri, dtp * a, 3)
    acsT = acs.T
    dtE, acsE = _dot01(dtp, E, 2), _dot01(acs, E, 3)
    X = xa[:, :SSM_W]
    xdt = X * dtE
    wE = jnp.exp(acsE[127:128, :] - acsE)
    eE = jnp.exp(acsE)
    cdE = eE[127:128, :]
    return dict(pre=pre, dtp=dtp, a=a, acs=acs, acsT=acsT, dtE=dtE, acsE=acsE, cdE=cdE, X=X, xdt=xdt, wE=wE, eE=eE)


def _ssd_decay(c, h):
    lm = lax.broadcasted_iota(jnp.int32, (128, 128), 0) >= lax.broadcasted_iota(jnp.int32, (128, 128), 1)
    return jnp.exp(jnp.where(lm, c["acs"][:, h:h + 1] - c["acsT"][h:h + 1, :], NEG_INF))


def _ssd_pair_operands(c, CB, h0):
    lane = lax.broadcasted_iota(jnp.int32, (128, 128), 1)
    L0, L1 = _ssd_decay(c, h0), _ssd_decay(c, h0 + 1)
    M = jnp.concatenate([CB * L0, CB * L1], axis=1).astype(BF16)
    xp = c["xdt"][:, h0 * 64:h0 * 64 + 128]
    BD = jnp.concatenate([jnp.where(lane < 64, xp, 0.0), jnp.where(lane >= 64, xp, 0.0)], axis=0).astype(BF16)
    return L0, L1, M, BD


def _ssd_y(c, xa, state_ref, dskipE):
    per_group, ys = [], []
    for g in range(SSM_G):
        gs = slice(g * 512, (g + 1) * 512)
        Bb = xa[:, SSM_W + g * 128:SSM_W + (g + 1) * 128].astype(BF16)
        Cb = xa[:, SSM_W + 256 + g * 128:SSM_W + 256 + (g + 1) * 128].astype(BF16)
        CB = _dot_nt(Cb, Bb)
        Sg = state_ref[:, gs]
        yoff = _dot(Cb, Sg.astype(BF16)) * c["eE"][:, gs]
        ydiag, pairs = [], []
        for j in range(4):
            ops = _ssd_pair_operands(c, CB, g * 8 + 2 * j)
            pairs.append(ops)
            ydiag.append(_dot(ops[2], ops[3]))
        ys.append(jnp.concatenate(ydiag, axis=1) + yoff)
        per_group.append(dict(Bb=Bb, Cb=Cb, CB=CB, Sg=Sg, yoff=yoff, pairs=pairs))
    Y = jnp.concatenate(ys, axis=1) + c["X"] * dskipE
    return Y, per_group


def _ssd_specs(S, rev):
    nc = S // CHUNK
    cm = (lambda b, i: (b, nc - 1 - i)) if rev else (lambda b, i: (b, i))
    xa = pl.BlockSpec((None, CHUNK, CONV_CH), lambda b, i: cm(b, i) + (0,))
    z = [pl.BlockSpec((None, CHUNK, 256), lambda b, i, q=q: cm(b, i) + (OFF["z"] // 256 + q,)) for q in range(4)]
    dt = pl.BlockSpec((None, CHUNK, 128), lambda b, i: cm(b, i) + (OFF["dt"] // 128,))
    tok = pl.BlockSpec((None, CHUNK, SSM_W), lambda b, i: cm(b, i) + (0,))
    st = pl.BlockSpec((None, None, 128, SSM_W), lambda b, i: cm(b, i) + (0, 0))
    return nc, xa, z, dt, tok, st


def _ssd_fwd(xact, P, bias, alog, dskipE, ng, cat, name):
    B, S, _ = P.shape
    nc, xa_spec, z_specs, dt_spec, _, st_spec = _ssd_specs(S, False)
    tok = pl.BlockSpec((None, CHUNK, SSM_W), lambda b, i: (b, i, 1))
    E, _, tri, _ = _ssd_consts()

    def body(xa_ref, z0, z1, z2, z3, dt_ref, bias_ref, alog_ref, dsk_ref, ng_ref, E_ref, tri_ref, cat_ref, o_ref, sp_ref, state):
        @pl.when(pl.program_id(1) == 0)
        def _():
            state[...] = jnp.zeros((128, SSM_W), F32)

        sp_ref[...] = state[...]
        xa = xa_ref[...]
        c = _ssd_pre(xa, dt_ref[...], bias_ref[...], alog_ref[...], E_ref[...], tri_ref[...])
        Y, groups = _ssd_y(c, xa, state, dsk_ref[...])
        Z = (c["xdt"] * c["wE"]).astype(BF16)
        for g in range(SSM_G):
            gs = slice(g * 512, (g + 1) * 512)
            state[:, gs] = groups[g]["Sg"] * c["cdE"][:, gs] + _dot_tn(groups[g]["Bb"], Z[:, gs])
        zv = jnp.concatenate([z0[...], z1[...], z2[...], z3[...]], axis=1)
        yz = Y * (zv * _sigmoid(zv))
        outs = []
        for g in range(SSM_G):
            yg = yz[:, g * 512:(g + 1) * 512]
            outs.append(yg * lax.rsqrt(jnp.mean(yg * yg, axis=-1, keepdims=True) + EPS))
        o_ref[...] = (jnp.concatenate(outs, axis=1) * ng_ref[...]).astype(BF16)

    return pl.pallas_call(
        body, name=name, grid=(B, nc),
        in_specs=[xa_spec] + z_specs + [dt_spec, _full((1, 128)), _full((1, 128)), _full((1, SSM_W)), _full((1, SSM_W)),
                                        _full((128, SSM_W)), _full((128, 128)), _ANY],
        out_specs=[tok, st_spec],
        out_shape=[jax.ShapeDtypeStruct(cat.shape, BF16), jax.ShapeDtypeStruct((B, nc, 128, SSM_W), F32)],
        scratch_shapes=[pltpu.VMEM((128, SSM_W), F32)], input_output_aliases={12: 0},
        compiler_params=pltpu.CompilerParams(dimension_semantics=("arbitrary", "arbitrary")),
    )(xact, P, P, P, P, P, bias, alog, dskipE, ng, E, tri, cat)


def _ssd_bwd(xact, P, sprev, dcat, bias, alog, dskipE, ng, name):
    B, S, _ = P.shape
    nc, xa_spec, z_specs, dt_spec, tok, st_spec = _ssd_specs(S, True)
    do_spec = pl.BlockSpec((None, CHUNK, SSM_W), lambda b, i: (b, nc - 1 - i, 1))
    E, ET, tri, triT = _ssd_consts()
    dt_out = pl.BlockSpec((None, CHUNK, 128), lambda b, i: (b, nc - 1 - i, 0))

    def body(xa_ref, z0, z1, z2, z3, dt_ref, sp_ref, do_ref, bias_ref, alog_ref, dsk_ref, ng_ref, E_ref, ET_ref, tri_ref,
             triT_ref, dxa_ref, dz_ref, ddt_ref, dbias_ref, dalog_ref, ddsk_ref, dng_ref, dstate):
        first = (pl.program_id(0) == 0) & (pl.program_id(1) == 0)

        @pl.when(first)
        def _():
            for ref in (dbias_ref, dalog_ref, ddsk_ref, dng_ref):
                ref[...] = jnp.zeros(ref.shape, F32)

        @pl.when(pl.program_id(1) == 0)
        def _():
            dstate[...] = jnp.zeros((128, SSM_W), F32)

        xa, ETm = xa_ref[...], ET_ref[...]
        c = _ssd_pre(xa, dt_ref[...], bias_ref[...], alog_ref[...], E_ref[...], tri_ref[...])
        Y, groups = _ssd_y(c, xa, sp_ref, dsk_ref[...])
        X, xdt = c["X"], c["xdt"]
        zv = jnp.concatenate([z0[...], z1[...], z2[...], z3[...]], axis=1)
        sg = _sigmoid(zv)
        zs = zv * sg
        yz = Y * zs
        dout = do_ref[...]
        dyz = []
        for g in range(SSM_G):
            gs = slice(g * 512, (g + 1) * 512)
            yg = yz[:, gs]
            r = lax.rsqrt(jnp.mean(yg * yg, axis=-1, keepdims=True) + EPS)
            yn = yg * r
            dng_ref[:, gs] += jnp.sum(dout[:, gs] * yn, axis=0, keepdims=True)
            dyn = dout[:, gs] * ng_ref[:, gs]
            dyz.append(r * (dyn - yn * jnp.mean(dyn * yn, axis=-1, keepdims=True)))
        dyz = jnp.concatenate(dyz, axis=1)
        dz_ref[...] = (dyz * Y * (sg * (1.0 + zv * (1.0 - sg)))).astype(BF16)
        dY = dyz * zs
        ddsk_ref[...] += jnp.sum(dY * X, axis=0, keepdims=True)
        dX = dY * dsk_ref[...]
        lane = lax.broadcasted_iota(jnp.int32, (128, 128), 1)
        sub = lax.broadcasted_iota(jnp.int32, (128, 128), 0)
        colform = jnp.zeros((128, 128), F32)
        rowform = jnp.zeros((128, 128), F32)
        dxdt, gacsE, dBC = [], [], []
        for g in range(SSM_G):
            gs = slice(g * 512, (g + 1) * 512)
            G = groups[g]
            Bb, Cb, CB, Sg = G["Bb"], G["Cb"], G["CB"], G["Sg"]
            dYg = dY[:, gs]
            dQ = (dYg * c["eE"][:, gs]).astype(BF16)
            dSn = dstate[:, gs]
            dSnb = dSn.astype(BF16)
            cd = c["cdE"][:, gs]
            dC = _dot_nt(dQ, Sg.astype(BF16))
            dSprev = _dot_tn(Cb, dQ) + dSn * cd
            t1 = jnp.broadcast_to(jnp.sum(dSn * Sg * cd, axis=0, keepdims=True), (8, 512))
            colform = colform + jnp.where(sub == 127, _dot01(t1, ETm[gs, :], 2)[0:1, :], 0.0)
            Zg = xdt[:, gs] * c["wE"][:, gs]
            dZ = _dot(Bb, dSnb)
            dB = _dot_nt(Zg.astype(BF16), dSnb)
            U = dZ * Zg
            ga = dYg * G["yoff"] - U
            ga = ga + jnp.where(lax.broadcasted_iota(jnp.int32, (128, 512), 0) == 127, jnp.sum(U, axis=0, keepdims=True), 0.0)
            gacsE.append(ga)
            dxg = [None] * 4
            dCB = jnp.zeros((128, 128), F32)
            for j in range(4):
                h0 = g * 8 + 2 * j
                L0, L1, M, BD = G["pairs"][j]
                dYp = dYg[:, j * 128:(j + 1) * 128].astype(BF16)
                dM = _dot_nt(dYp, BD)
                dBD = _dot_tn(M, dYp)
                dxg[j] = jnp.where(lane < 64, dBD[:128], dBD[128:])
                for t, (h, L) in enumerate(((h0, L0), (h0 + 1, L1))):
                    dMh = dM[:, t * 128:(t + 1) * 128]
                    dCB = dCB + dMh * L
                    Gh = dMh * CB * L
                    colform = colform + jnp.where(lane == h, jnp.sum(Gh, axis=1, keepdims=True), 0.0)
                    rowform = rowform - jnp.where(sub == h, jnp.sum(Gh, axis=0, keepdims=True), 0.0)
            dCBb = dCB.astype(BF16)
            dC = dC + _dot(dCBb, Bb)
            dB = dB + _dot_tn(dCBb, Cb)
            dxdt.append(jnp.concatenate(dxg, axis=1) + dZ * c["wE"][:, gs])
            dBC.append((dB, dC))
            dstate[:, gs] = dSprev
        dxdt = jnp.concatenate(dxdt, axis=1)
        dX = dX + dxdt * c["dtE"]
        ddt = _dot01(dxdt * X, ETm, 2)
        dacs = colform + rowform.T + _dot01(jnp.concatenate(gacsE, axis=1), ETm, 2)
        dda = _dot01_left(triT_ref[...], dacs, 2)
        ddt = ddt + dda * c["a"]
        dalog_ref[...] += jnp.sum(dda * c["dtp"], axis=0, keepdims=True) * c["a"]
        ddtraw = jnp.where(lane < SSM_H, ddt * _sigmoid(c["pre"]), 0.0)
        dbias_ref[...] += jnp.sum(ddtraw, axis=0, keepdims=True)
        ddt_ref[...] = ddtraw.astype(BF16)
        dxa_ref[...] = jnp.concatenate([dX, dBC[0][0], dBC[1][0], dBC[0][1], dBC[1][1]], axis=1)

    p128, p1k = _full((1, 128)), _full((1, SSM_W))
    return pl.pallas_call(
        body, name=name, grid=(B, nc),
        in_specs=[xa_spec] + z_specs + [dt_spec, st_spec, do_spec, p128, p128, p1k, p1k,
                                        _full((128, SSM_W)), _full((SSM_W, 128)), _full((128, 128)), _full((128, 128))],
        out_specs=[xa_spec, tok, dt_out, p128, p128, p1k, p1k],
        out_shape=[jax.ShapeDtypeStruct((B, S, CONV_CH), F32), jax.ShapeDtypeStruct((B, S, SSM_W), BF16),
                   jax.ShapeDtypeStruct((B, S, 128), BF16), jax.ShapeDtypeStruct((1, 128), F32),
                   jax.ShapeDtypeStruct((1, 128), F32), jax.ShapeDtypeStruct((1, SSM_W), F32),
                   jax.ShapeDtypeStruct((1, SSM_W), F32)],
        scratch_shapes=[pltpu.VMEM((128, SSM_W), F32)],
        compiler_params=pltpu.CompilerParams(dimension_semantics=("arbitrary", "arbitrary")),
    )(xact, P, P, P, P, P, sprev, dcat, bias, alog, dskipE, ng, E, ET, tri, triT)


def _adamw(w, parts, m, v, name, tr=512, row0=0, prev=None):
    Rtot, C = w.shape
    ns, R = parts.shape[0], parts.shape[1]
    tr = min(tr, R)
    assert R % tr == 0 and row0 % tr == 0
    off = row0 // tr
    c1 = 1.0 / (1.0 - ADAM_B1 ** ADAM_STEP)
    c2 = 1.0 / (1.0 - ADAM_B2 ** ADAM_STEP)

    def body(w_ref, p_ref, m_ref, v_ref, *rest):
        g_ref, d_ref, mo_ref, vo_ref = rest[-4:]
        g = p_ref[0].astype(F32)
        for s in range(1, ns):
            g = g + p_ref[s].astype(F32)
        mn = ADAM_B1 * m_ref[...] + (1.0 - ADAM_B1) * g
        vn = ADAM_B2 * v_ref[...] + (1.0 - ADAM_B2) * (g * g)
        g_ref[...] = g
        mo_ref[...] = mn
        vo_ref[...] = vn
        d_ref[...] = -ADAM_LR * ((mn * c1) / (jnp.sqrt(vn * c2) + ADAM_EPS) + ADAM_WD * w_ref[...])

    blk = pl.BlockSpec((tr, C), lambda i: (i + off, 0))
    extra = [] if prev is None else list(prev)
    return pl.pallas_call(
        body, name=name, grid=(R // tr,),
        in_specs=[blk, pl.BlockSpec((ns, tr, C), lambda i: (0, i, 0)), blk, blk] + [pl.BlockSpec(memory_space=pl.ANY)] * len(extra),
        out_specs=[blk] * 4, out_shape=[jax.ShapeDtypeStruct((Rtot, C), F32)] * 4,
        input_output_aliases={4 + k: k for k in range(len(extra))})(w, parts, m, v, *extra)


_SMALL = ("ada_b", "norm1_g", "gm_ln_g", "gm_ln_b", "gm_ws", "gm_bs", "gm_norm_g", "attn_sinks", "attn_norm_g", "conv_b",
          "dt_bias", "a_log", "d_skip", "ssm_norm_g", "norm2_g", "final_norm_g")


def _pack(arrs):
    flat = []
    for a in arrs:
        f = a.reshape(-1).astype(F32)
        flat.append(jnp.pad(f, (0, (-f.shape[0]) % 1024)))
    return jnp.concatenate(flat).reshape(-1, 128)


def _unpack(pack, like):
    out, o = [], 0
    flat = pack.reshape(-1)
    for a in like:
        n = int(np.prod(a.shape))
        out.append(flat[o:o + n].reshape(a.shape))
        o += n + (-n) % 1024
    return out


def kernel(x, c, ada_w, ada_b, norm1_g, w_in, gm_ln_g, gm_ln_b, gm_ws, gm_bs, gm_norm_g, attn_sinks, attn_norm_g, conv_w, conv_b, dt_bias, a_log, d_skip, ssm_norm_g, w_out, norm2_g, w_mlp1, w_mlp2, final_norm_g, loss_target, m_ada_w, m_ada_b, m_norm1_g, m_w_in, m_gm_ln_g, m_gm_ln_b, m_gm_ws, m_gm_bs, m_gm_norm_g, m_attn_sinks, m_attn_norm_g, m_conv_w, m_conv_b, m_dt_bias, m_a_log, m_d_skip, m_ssm_norm_g, m_w_out, m_norm2_g, m_w_mlp1, m_w_mlp2, m_final_norm_g, v_ada_w, v_ada_b, v_norm1_g, v_w_in, v_gm_ln_g, v_gm_ln_b, v_gm_ws, v_gm_bs, v_gm_norm_g, v_attn_sinks, v_attn_norm_g, v_conv_w, v_conv_b, v_dt_bias, v_a_log, v_d_skip, v_ssm_norm_g, v_w_out, v_norm2_g, v_w_mlp1, v_w_mlp2, v_final_norm_g):
    args = dict(locals())
    B, S, _ = x.shape
    T = B * S
    L = DEPTH
    me = 4 * lax.axis_index("x") + 2 * lax.axis_index("y") + lax.axis_index("c")

    gath = _gather2([c, conv_w], "ag_c")
    big = ("w_in", "w_out", "w_mlp1", "w_mlp2")
    chain = [(n, l) for l in range(L) for n in ("w_in", "w_mlp1", "w_out", "w_mlp2")]
    inflight = {}

    def start_next(order):
        if not chain:
            return jnp.zeros((8, 128), F32)
        n, l = chain.pop(0)
        sems, land_thru, token = _gather_start(zone[n, l], order, f"ag_start_{n}{l}")
        inflight[n, l] = (sems, land_thru)
        return token

    def gathered(n, l, after):
        land = _gather_wait(*inflight.pop((n, l)), after, f"ag_wait_{n}{l}")
        return _gather_finish(land, f"ag_fin_{n}{l}")

    me1 = me.astype(jnp.int32).reshape(1)
    zone = {(n, l): _landing_zone(args[n][l], me1, f"ag_zone_{n}{l}") for n, l in chain}
    later_zones = [zone[k] for k in chain[1:]]

    tok = start_next(gath[0])
    c_all = gath[0].reshape(NDEV * B, D) + tok[0, 0]
    c_act = (c_all * jax.nn.sigmoid(c_all)).astype(BF16)
    nb_rows = c_act.shape[0]
    c_pad = jnp.pad(c_act, ((0, 128 - nb_rows), (0, 0)))
    adw = ada_w.astype(BF16)
    mod_part = jnp.stack([_mm(c_pad, adw[l], mode="nn", name=f"mod{l}", tn=768)[:nb_rows] for l in range(L)])
    mod_all = _gather_small([mod_part], "ag_mod", order=later_zones)[0]
    mod_mine = lax.dynamic_slice_in_dim(mod_all, me * B, B, axis=2)
    mod = jnp.transpose(mod_mine, (1, 2, 0, 3)).reshape(L, B, 6 * D) + ada_b[:, None, :]
    mods = [[mod[l][:, None, i * D:(i + 1) * D] for i in range(6)] for l in range(L)]

    win_g, wout_g, w1_g, w2_g = [None] * L, [None] * L, [None] * L, [None] * L

    tril = jnp.tril(jnp.ones((128, 128), F32))
    row = lambda a: a.reshape(1, -1)
    pad128 = lambda a: jnp.pad(a.reshape(1, -1), ((0, 0), (0, 128 - a.shape[-1])))
    small = []
    for l in range(L):
        wt = gm_ws[l] * tril
        small.append(dict(
            lng=row(gm_ln_g[l]), lnb=row(gm_ln_b[l]), wt=wt.astype(BF16), wtT=jnp.swapaxes(wt, 1, 2).astype(BF16),
            bsx=jnp.repeat(gm_bs[l].T, 128, axis=1), gog=row(gm_norm_g[l]), sinks=attn_sinks[l], aog=row(attn_norm_g[l]),
            bias=pad128(dt_bias[l]), alog=pad128(a_log[l]), dskE=jnp.repeat(d_skip[l], SSM_HD).reshape(1, SSM_W),
            sng=row(ssm_norm_g[l]), cb=row(conv_b[l])))
    convw_all = jnp.transpose(gath[1], (1, 2, 0, 3)).reshape(L, 4, CONV_CH)
    convw8 = jnp.pad(convw_all, ((0, 0), (0, 4), (0, 0)))

    saved = []
    xl = x
    h = _norm_fwd(xl, row(norm1_g[0]), mods[0][1], mods[0][0], "norm1_f0")
    for l in range(L):
        sm = small[l]
        g_in = gathered("w_in", l, h)
        tok = start_next(g_in)
        win_g[l] = _to_work_cols(jnp.transpose(g_in, (1, 0, 2)).reshape(D, IN_W))
        P = _mm(h.reshape(T, D), win_g[l], mode="nn", name=f"proj_in{l}", tn=1536, order=tok).reshape(B, S, PW)
        cat = _gmlp_fwd(P, sm["lng"], sm["lnb"], sm["wt"], sm["bsx"], sm["gog"], f"gmlp_f{l}")
        cat = _attn_fwd(P, sm["sinks"], sm["aog"], cat, f"attn_f{l}")
        xact = _conv_fwd(P, convw8[l], sm["cb"], f"conv_f{l}")
        w1_g[l] = gathered("w_mlp1", l, xact)
        tok = start_next(w1_g[l])
        cat, sprev = _ssd_fwd(xact, P, sm["bias"], sm["alog"], sm["dskE"], sm["sng"] + tok[0:1, 0:1], cat, f"ssd_f{l}")
        g_out = gathered("w_out", l, cat)
        tok = start_next(g_out)
        wout_g[l] = g_out.reshape(D, D)
        mix = _mm(cat.reshape(T, D), wout_g[l], mode="nn", name=f"proj_out{l}", order=tok).reshape(B, S, D)
        x_mid, h2 = _norm_fwd(xl, row(norm2_g[l]), mods[l][4], mods[l][3], f"norm2_f{l}", resid=(mix, mods[l][2]))
        a_act, r_act = _mm(h2.reshape(T, D), w1_g[l], mode="nn", name=f"mlp1_{l}", out_dtypes=(BF16, BF16), col_blocked_b=True,
                           epilogue=lambda acc: (acc, jnp.square(jnp.maximum(acc, 0.0))))
        g_2 = gathered("w_mlp2", l, r_act)
        tok = start_next(g_2)
        w2_g[l] = g_2.reshape(DFF, D)
        m2 = _mm(r_act, w2_g[l], mode="nn", name=f"mlp2_{l}", order=tok, tk=4096).reshape(B, S, D)
        saved.append(dict(x_in=xl, h=h, P=P, xact=xact, sprev=sprev, cat=cat, mix=mix, x_mid=x_mid, h2=h2, a=a_act, r=r_act, m2=m2))
        if l + 1 < L:
            xl, h = _norm_fwd(x_mid, row(norm1_g[l + 1]), mods[l + 1][1], mods[l + 1][0], f"norm1_f{l + 1}", resid=(m2, mods[l][5]))

    sv = saved[L - 1]
    nb = _norm_bwd(sv["x_mid"], row(final_norm_g), "final_b", tgt=loss_target, br=sv["m2"], gate=mods[L - 1][5], x_is_prev=True)
    loss_part, g_final = nb["loss"], nb["dg"]
    dmod, gsm, gconvw = [None] * L, [None] * L, [None] * L
    core = lax.axis_index("c").astype(jnp.int32).reshape(1)
    reducing = []

    def reduce_start(n, l, sent, after):
        p, from_sib = _pair_wait(*sent[:3], after, f"rs_pair_wait_{n}{l}")
        s, land = _pair_add(p, from_sib, core, f"rs_add_{n}{l}")
        return reduce_exchange(n, l, s, land, after)

    def reduce_exchange(n, l, s, land, order):
        sems, s_thru, land_thru, token = _chipsum_start(s, land, order, f"rs_start_{n}{l}")
        reducing.append((n, l, sems, s_thru, land_thru))
        return token

    other = 1 - core

    for l in reversed(range(L)):
        sv, sm = saved[l], small[l]
        dm2, dxo, dg2 = nb["dbr"].reshape(T, D), nb["dx"], nb["dgate"]
        da = _mm(dm2, w2_g[l], mode="nt", name=f"mlp2_dx{l}", out_dtypes=(BF16,), extras=(sv["a"],),
                 epilogue=lambda acc, a: (acc * (2.0 * jnp.maximum(a.astype(F32), 0.0)),))
        h2f = sv["h2"].reshape(T, D)
        sent2 = _sibling_start(_dw_half(sv["r"], dm2, other, axis="m", name=f"mlp2_dw_sib{l}"), da, f"rs_sib_start_w_mlp2{l}")
        dh2 = _mm(da, w1_g[l], mode="nt", name=f"mlp1_dx{l}", col_blocked_b=True, order=sent2[3]).reshape(B, S, D)
        from_sib = _sibling_wait(*sent2[:3], dh2, f"rs_sib_wait_w_mlp2{l}")
        sent1 = _sibling_start(_dw_half(h2f, da, other, axis="n", name=f"mlp1_dw_sib{l}", order=from_sib), da,
                               f"rs_sib_start_w_mlp1{l}")
        s2, land2 = _dw_half(sv["r"], dm2, core, axis="m", name=f"mlp2_dw_own{l}", add=from_sib, order=sent1[3])
        tok = reduce_exchange("w_mlp2", l, s2, land2, da)
        nb2 = _norm_bwd(sv["x_mid"], row(norm2_g[l]) + tok[0, 0], f"norm2_b{l}", sc=mods[l][4], dh=dh2, dres=dxo, br=sv["mix"],
                        gate=mods[l][2])
        dmix = nb2["dbr"].reshape(T, D)
        from_sib = _sibling_wait(*sent1[:3], dmix, f"rs_sib_wait_w_mlp1{l}")
        s1, land1 = _dw_half(h2f, da, core, axis="n", name=f"mlp1_dw_own{l}", add=from_sib)
        tok = reduce_exchange("w_mlp1", l, s1, land1, dmix)
        dcat = _mm(dmix, wout_g[l], mode="nt", name=f"proj_out_dx{l}", order=tok).reshape(B, S, D)
        du, dv, dlng, dlnb, dws, dbsx, dgog = _gmlp_bwd(sv["P"], dcat, sm["lng"], sm["lnb"], sm["wt"], sm["wtT"], sm["bsx"],
                                                        sm["gog"], f"gmlp_b{l}")
        dq, dk, dvv, dsink, daog = _attn_bwd(sv["P"], dcat, sm["sinks"], sm["aog"], f"attn_b{l}")
        dwo = _mm(sv["cat"].reshape(T, D), dmix, mode="tn", name=f"proj_out_dw{l}", out_dtypes=(BF16,), tk=2048,
                  order=dq).reshape(4, 2, D // NDEV, D)
        sent = _pair_start(dwo, dmix, f"rs_pair_start_w_out{l}")
        dxa, dz, ddt, dbias, dalog, ddsk, dsng = _ssd_bwd(sv["xact"], sv["P"], sv["sprev"], dcat, sm["bias"], sm["alog"],
                                                          sm["dskE"], sm["sng"] + sent[3][0:1, 0:1], f"ssd_b{l}")
        tok = reduce_start("w_out", l, sent, dxa)
        dxbc, dcw, dcb = _conv_bwd(sv["P"], dxa, convw8[l], sm["cb"] + tok[0:1, 0:1], f"conv_b{l}")
        dP = jnp.concatenate([du, dv, dq, dk, dvv, dz, dxbc, ddt, jnp.zeros((B, S, PW - OFF["dt"] - 128), BF16)],
                             axis=-1).reshape(T, PW)
        dwin = _mm(sv["h"].reshape(T, D), dP, mode="tn", name=f"proj_in_dw{l}", out_dtypes=(BF16,), tn=1536, tk=2048)
        dwin = jnp.transpose(_from_work_cols(dwin).reshape(D, NDEV, IN_W // NDEV), (1, 0, 2)).reshape(4, 2, D, IN_W // NDEV)
        sent = _pair_start(dwin, dP, f"rs_pair_start_w_in{l}")
        dh = _mm(dP, win_g[l], mode="nt", name=f"proj_in_dx{l}", tk=2304, order=sent[3]).reshape(B, S, D)
        tok = reduce_start("w_in", l, sent, dh)
        nb = _norm_bwd(sv["x_in"], row(norm1_g[l]) + tok[0, 0], f"norm1_b{l}", sc=mods[l][1], dh=dh, dres=nb2["dx"],
                       br=saved[l - 1]["m2"] if l > 0 else None, gate=mods[l - 1][5] if l > 0 else None)
        dmod[l] = jnp.concatenate([nb["dsh"], nb["dsc"], nb2["dgate"], nb2["dsh"], nb2["dsc"], dg2], axis=-1)
        gconvw[l] = dcw[:4]
        gsm[l] = dict(
            ada_b=jnp.sum(dmod[l], axis=(0, 1)), norm1_g=nb["dg"], gm_ln_g=dlng, gm_ln_b=dlnb, gm_ws=dws,
            gm_bs=dbsx.reshape(128, GM_H, 128).sum(-1).T, gm_norm_g=dgog, attn_sinks=dsink[:, 0], attn_norm_g=daog,
            conv_b=dcb, dt_bias=dbias[0, :SSM_H], a_log=dalog[0, :SSM_H], d_skip=ddsk.reshape(SSM_H, SSM_HD).sum(-1),
            ssm_norm_g=dsng, norm2_g=nb2["dg"])
    grad_x = nb["dx"]

    big_res, after = dict.fromkeys(big), grad_x
    tile_rows = dict(w_in=256, w_out=256, w_mlp1=256, w_mlp2=128)

    def finish_reduce(n, l, sems, s_thru, land_thru, after):
        parts = _chipsum_wait(sems, s_thru, land_thru, after, f"rs_wait_{n}{l}")
        w = args[n]
        big_res[n] = _adamw(w.reshape(-1, w.shape[-1]), parts, args["m_" + n].reshape(-1, w.shape[-1]),
                            args["v_" + n].reshape(-1, w.shape[-1]), f"adamw_{n}{l}", tr=tile_rows[n], row0=l * w.shape[1],
                            prev=big_res[n])
        return big_res[n][0]

    for item in reducing[:-1]:
        after = finish_reduce(*item, after)

    per_layer = [n for n in _SMALL if n != "final_norm_g"]
    g_small = [jnp.stack([gsm[l][n].reshape(args[n].shape[1:]) for l in range(L)]) for n in per_layer] + [g_final.reshape(D)]
    zc = jnp.zeros((L, 4, CONV_CH), F32)
    z1 = jnp.zeros((1, 128), F32)
    gpack = _pack([loss_part] + g_small + [jnp.stack(gconvw)])
    got = _gather2([jnp.stack(dmod).reshape(L, B, 6 * D), gpack], "ag_small", order=after)
    like = [z1] + [args[n] for n in _SMALL] + [zc]
    packs = [_pack([z1] + [args[p + n] for n in _SMALL] + [zc]) for p in ("", "m_", "v_")]
    sres = [_unpack(p, like) for p in _adamw(packs[0], got[1], packs[1], packs[2], "adamw_small", tr=gpack.shape[0])]
    res = {n: [r[1 + i] for r in sres] for i, n in enumerate(_SMALL)}
    loss = sres[0][0][0, 0]
    gcw = lax.dynamic_slice_in_dim(sres[0][-1], me * (CONV_CH // NDEV), CONV_CH // NDEV, axis=2)

    def update(name, parts, tr):
        w = args[name]
        r = _adamw(w.reshape(-1, w.shape[-1]), parts, args["m_" + name].reshape(-1, w.shape[-1]),
                   args["v_" + name].reshape(-1, w.shape[-1]), "adamw_" + name, tr=tr)
        res[name] = [a.reshape(w.shape) for a in r]

    update("conv_w", gcw.reshape(1, L * 4, CONV_CH // NDEV), L * 4)

    dmod_all = jnp.transpose(got[0], (1, 0, 2, 3)).reshape(L, NDEV * B, 6 * D)
    dm_mine = lax.dynamic_slice_in_dim(dmod_all, me * (6 * D // NDEV), 6 * D // NDEV, axis=2)
    dm_pad = jnp.pad(dm_mine, ((0, 0), (0, 128 - nb_rows), (0, 0))).astype(BF16)
    g_adaw = jnp.stack([_mm(c_pad, dm_pad[l], mode="tn", name=f"ada_dw{l}", tn=768) for l in range(L)])
    update("ada_w", g_adaw.reshape(1, L * D, 6 * D // NDEV), 256)

    finish_reduce(*reducing[-1], res["ada_w"][0])
    for n in big:
        res[n] = [a.reshape(args[n].shape) for a in big_res[n]]

    names = ['ada_w', 'ada_b', 'norm1_g', 'w_in', 'gm_ln_g', 'gm_ln_b', 'gm_ws', 'gm_bs', 'gm_norm_g', 'attn_sinks',
             'attn_norm_g', 'conv_w', 'conv_b', 'dt_bias', 'a_log', 'd_skip', 'ssm_norm_g', 'w_out', 'norm2_g', 'w_mlp1',
             'w_mlp2', 'final_norm_g']
    return (loss, grad_x, *[res[n][0] for n in names], *[res[n][1] for n in names], *[res[n][2] for n in names],
            *[res[n][3] for n in names])
```

```python
import functools

import jax
import jax.numpy as jnp
import numpy as np
from jax import lax
from jax.experimental import pallas as pl
from jax.experimental.pallas import tpu as pltpu

F32, BF16 = jnp.float32, jnp.bfloat16
HI = lax.Precision.HIGHEST
MESH = pl.DeviceIdType.MESH
NDEV = 8

D = 2048
DEPTH = 2
CHUNK = 128
GM_W, GM_H = 512, 4
ATT_W, KV_W, ATT_H = 512, 128, 8
SSM_W, SSM_H, SSM_HD, SSM_G = 1024, 16, 64, 2
CONV_CH = 1536
IN_W = 4368
DFF = 8192
EPS = 1e-6
NEG_INF = -1e30
GELU_K = 0.7978845608028654
GELU_C = 0.044715

_ORIG = (("u", 512), ("v", 512), ("q", 512), ("k", 128), ("vv", 128), ("z", 1024), ("xbc", 1536), ("dt", 16))
OFF = dict(u=0, v=512, q=1024, k=1536, vv=1664, z=1792, xbc=2816, dt=4352)
PW = 4608

ADAM_LR, ADAM_B1, ADAM_B2, ADAM_EPS, ADAM_WD, ADAM_STEP = 0.001, 0.9, 0.999, 1e-08, 0.01, 10


def _shards_to_cols(g, name, tr=256):
    n, R, C = g.shape

    def body(g_ref, o_ref):
        o_ref[...] = jnp.concatenate([g_ref[s] for s in range(n)] + [jnp.zeros((tr, PW - n * C), g.dtype)], axis=1)

    return pl.pallas_call(body, name=name, grid=(R // tr,), in_specs=[pl.BlockSpec((n, tr, C), lambda i: (0, i, 0))],
                          out_specs=pl.BlockSpec((tr, PW), lambda i: (i, 0)), out_shape=jax.ShapeDtypeStruct((R, PW), g.dtype))(g)


def _cols_to_shards(w, name, tr=256):
    R, C = w.shape[0], IN_W // NDEV

    def body(w_ref, o_ref):
        x = w_ref[...]
        for s in range(NDEV):
            o_ref[s] = x[:, C * s:C * (s + 1)]

    return pl.pallas_call(body, name=name, grid=(R // tr,), in_specs=[pl.BlockSpec((tr, PW), lambda i: (i, 0))],
                          out_specs=pl.BlockSpec((NDEV, tr, C), lambda i: (0, i, 0)),
                          out_shape=jax.ShapeDtypeStruct((NDEV, R, C), w.dtype))(w)


def _sigmoid(x):
    return 1.0 / (1.0 + jnp.exp(-x))


def _gelu(x):
    return 0.5 * x * (1.0 + jnp.tanh(GELU_K * (x + GELU_C * x * x * x)))


def _gelu_grad(x):
    t = jnp.tanh(GELU_K * (x + GELU_C * x * x * x))
    return 0.5 * (1.0 + t) + 0.5 * x * (1.0 - t * t) * GELU_K * (1.0 + 3.0 * GELU_C * x * x)


def _dot(a, b, prec=None):
    return jnp.dot(a, b, precision=prec, preferred_element_type=F32)


def _dot_nt(a, b, prec=None):
    return lax.dot_general(a, b, (((1,), (1,)), ((), ())), precision=prec, preferred_element_type=F32)


def _dot_tn(a, b, prec=None):
    return lax.dot_general(a, b, (((0,), (0,)), ((), ())), precision=prec, preferred_element_type=F32)


def _full(shape):
    return pl.BlockSpec(shape, lambda *_: (0,) * len(shape))


_HBM = pl.BlockSpec(memory_space=pltpu.HBM)


def _me():
    return lax.axis_index("x"), lax.axis_index("y"), lax.axis_index("c")


def _peer(k):
    x, y, c = _me()
    px = 1 - x if k & 4 else x
    py = 1 - y if k & 2 else y
    pc = 1 - c if k & 1 else c
    return (px, py, pc), 4 * px + 2 * py + pc


def _gather_small(xs, name, order=()):
    n = len(xs)

    def body(*refs):
        ins, outs = refs[:n], refs[-n - 3:-3]
        send, recv, loc = refs[-3:]
        x, y, c = _me()
        me = 4 * x + 2 * y + c
        started = []
        for i in range(n):
            own = pltpu.make_async_copy(ins[i], outs[i].at[me], loc.at[i])
            own.start()
            started.append(own)
        for k in range(1, NDEV):
            dev, lin = _peer(k)
            for i in range(n):
                pltpu.make_async_remote_copy(
                    src_ref=ins[i], dst_ref=outs[i].at[me],
                    send_sem=send.at[i, k - 1], recv_sem=recv.at[i, k - 1], device_id=dev, device_id_type=MESH).start()
        for k in range(1, NDEV):
            dev, lin = _peer(k)
            for i in range(n):
                pltpu.make_async_remote_copy(
                    src_ref=ins[i], dst_ref=outs[i].at[lin],
                    send_sem=send.at[i, k - 1], recv_sem=recv.at[i, k - 1], device_id=dev, device_id_type=MESH).wait()
        for own in started:
            own.wait()

    extra = list(order)
    return pl.pallas_call(
        body, name=name, out_shape=[jax.ShapeDtypeStruct((NDEV,) + a.shape, a.dtype) for a in xs],
        in_specs=[_HBM] * n + [pl.BlockSpec(memory_space=pl.ANY)] * len(extra), out_specs=[_HBM] * n,
        scratch_shapes=[pltpu.SemaphoreType.DMA((n, NDEV - 1)), pltpu.SemaphoreType.DMA((n, NDEV - 1)),
                        pltpu.SemaphoreType.DMA((n,))],
        compiler_params=pltpu.CompilerParams(has_side_effects=True),
    )(*xs, *extra)


def _chips():
    x, y, c = _me()
    return x, y, c, [(1 - x, y), (x, 1 - y), (1 - x, 1 - y)]


def _gather2(xs, name, order=None):
    n = len(xs)
    extra = [] if order is None else [order]

    def body(*refs):
        ins, outs = refs[:n], refs[-n - 3:-3]
        send, recv, loc = refs[-3:]
        x, y, c, chips = _chips()
        me, sib = (x, y, c), (x, y, 1 - c)

        def cp(i, k, block, to, src=None):
            slot = outs[i].at[4 * block[0] + 2 * block[1] + block[2]]
            return pltpu.make_async_remote_copy(src_ref=slot if src is None else src, dst_ref=slot, send_sem=send.at[i, k],
                                                recv_sem=recv.at[i, k], device_id=to, device_id_type=MESH)

        sent = []
        for i in range(n):
            for j, chip in enumerate(chips):
                sent.append(cp(i, 1 + j, me, (*chip, c), src=ins[i]))
            sent.append(cp(i, 0, me, sib, src=ins[i]))
        for s in sent:
            s.start()
        own = [pltpu.make_async_copy(ins[i], outs[i].at[4 * x + 2 * y + c], loc.at[i]) for i in range(n)]
        for o in own:
            o.start()
        for j, chip in enumerate(chips):
            for i in range(n):
                cp(i, 1 + j, (*chip, c), me).wait_recv()
                fwd = cp(i, 4 + j, (*chip, c), sib)
                fwd.start()
                sent.append(fwd)
        for i in range(n):
            cp(i, 0, sib, me).wait_recv()
            for j, chip in enumerate(chips):
                cp(i, 4 + j, (*chip, 1 - c), me).wait_recv()
        for s in sent:
            s.wait_send()
        for o in own:
            o.wait()

    return pl.pallas_call(
        body, name=name, out_shape=[jax.ShapeDtypeStruct((NDEV,) + a.shape, a.dtype) for a in xs],
        in_specs=[_HBM] * n + [pl.BlockSpec(memory_space=pl.ANY)] * len(extra), out_specs=[_HBM] * n,
        scratch_shapes=[pltpu.SemaphoreType.DMA((n, 7)), pltpu.SemaphoreType.DMA((n, 7)), pltpu.SemaphoreType.DMA((n,))],
        compiler_params=pltpu.CompilerParams(has_side_effects=True),
    )(*xs, *extra)


def _pair_add(p, r1, core, name, tr=256):
    _, _, R, C = p.shape
    tr = min(tr, R)

    def body(core_ref, p_ref, r_ref, o_ref, o2_ref):
        s = (p_ref[...].astype(F32) + r_ref[...].astype(F32)).astype(o_ref.dtype)
        o_ref[...] = s
        o2_ref[...] = s

    blk = pl.BlockSpec((None, tr, C), lambda ch, i, core_ref: (ch, i, 0))
    return pl.pallas_call(
        body, name=name, out_shape=[jax.ShapeDtypeStruct((4, R, C), p.dtype)] * 2,
        grid_spec=pltpu.PrefetchScalarGridSpec(
            num_scalar_prefetch=1, grid=(4, R // tr),
            in_specs=[pl.BlockSpec((None, None, tr, C), lambda ch, i, core_ref: (ch, core_ref[0], i, 0)), blk],
            out_specs=[blk, blk]),
    )(core, p, r1)


_SEM = pl.BlockSpec(memory_space=pltpu.SEMAPHORE)
_ANY = pl.BlockSpec(memory_space=pl.ANY)
_DATAFLOW = pltpu.SideEffectType.DATAFLOW_SIDE_EFFECTING


def _hbm(a):
    return pltpu.with_memory_space_constraint(a, pltpu.HBM)


def _gather_targets():
    x, y, c, chips = _chips()
    return 4 * x + 2 * y + c, [(x, y, 1 - c)] + [(*chip, c) for chip in chips]


def _landing_zone(w, l, me, name, tr=512):
    _, R, C = w.shape
    tr = min(tr, R)

    def body(me_ref, w_ref, o_ref):
        o_ref[...] = w_ref[...].astype(BF16)

    return pl.pallas_call(
        body, name=name, out_shape=jax.ShapeDtypeStruct((NDEV, R, C), BF16),
        grid_spec=pltpu.PrefetchScalarGridSpec(
            num_scalar_prefetch=1, grid=(R // tr,), in_specs=[pl.BlockSpec((None, tr, C), lambda i, me_ref: (l, i, 0))],
            out_specs=pl.BlockSpec((None, tr, C), lambda i, me_ref: (me_ref[0], i, 0))),
    )(me, w)


def _gather_start(land, order, name):
    def body(land_ref, order_ref, *rest):
        sems, token = rest[:8], rest[9]
        me, targets = _gather_targets()
        for k, to in enumerate(targets):
            pltpu.make_async_remote_copy(src_ref=land_ref.at[me], dst_ref=land_ref.at[me], send_sem=sems[k],
                                         recv_sem=sems[4 + k], device_id=to, device_id_type=MESH).start()
        token[...] = jnp.zeros_like(token)

    outs = pl.pallas_call(
        body, name=name,
        out_shape=(pltpu.SemaphoreType.DMA(()),) * 8 + (pltpu.HBM(land.shape, land.dtype), jax.ShapeDtypeStruct((8, 128), F32)),
        in_specs=(_HBM, _ANY), out_specs=(_SEM,) * 8 + (_HBM, pl.BlockSpec(memory_space=pltpu.VMEM)),
        input_output_aliases={0: 8}, compiler_params=pltpu.CompilerParams(has_side_effects=_DATAFLOW),
    )(_hbm(land), order)
    return outs[:8], outs[8], outs[9]


def _gather_wait(sems, land_thru, after, name):
    def body(land_ref, *rest):
        sems_ = rest[:8]
        me, targets = _gather_targets()
        for k, to in enumerate(targets):
            cp = pltpu.make_async_remote_copy(src_ref=land_ref.at[me], dst_ref=land_ref.at[me], send_sem=sems_[k],
                                              recv_sem=sems_[4 + k], device_id=to, device_id_type=MESH)
            cp.wait_send()
            cp.wait_recv()

    return pl.pallas_call(
        body, name=name, out_shape=pltpu.HBM(land_thru.shape, land_thru.dtype),
        in_specs=(_HBM,) + (_SEM,) * 8 + (_ANY,), out_specs=_HBM, input_output_aliases={0: 0},
        compiler_params=pltpu.CompilerParams(has_side_effects=_DATAFLOW),
    )(land_thru, *sems, after)


def _gather_finish(land, name):
    def body(land_ref, out, send, recv):
        x, y, c, chips = _chips()
        fwd = [pltpu.make_async_remote_copy(src_ref=out.at[4 * px + 2 * py + c], dst_ref=out.at[4 * px + 2 * py + c],
                                            send_sem=send.at[j], recv_sem=recv.at[j], device_id=(x, y, 1 - c), device_id_type=MESH)
               for j, (px, py) in enumerate(chips)]
        for cp in fwd:
            cp.start()
        for j, (px, py) in enumerate(chips):
            slot = out.at[4 * px + 2 * py + 1 - c]
            pltpu.make_async_remote_copy(src_ref=slot, dst_ref=slot, send_sem=send.at[j], recv_sem=recv.at[j],
                                         device_id=(x, y, 1 - c), device_id_type=MESH).wait()

    return pl.pallas_call(
        body, name=name, out_shape=jax.ShapeDtypeStruct(land.shape, land.dtype),
        in_specs=[_HBM], out_specs=_HBM, input_output_aliases={0: 0},
        scratch_shapes=[pltpu.SemaphoreType.DMA((3,)), pltpu.SemaphoreType.DMA((3,))],
        compiler_params=pltpu.CompilerParams(has_side_effects=True),
    )(land)


def _chip_targets():
    x, y, c, chips = _chips()
    return 2 * x + y, [((px, py, c), 2 * px + py) for px, py in chips]


def _chipsum_start(s, land, order, name):
    def body(s_ref, land_ref, order_ref, *rest):
        sems, token = rest[:6], rest[8]
        mine, targets = _chip_targets()
        for k, (to, ch) in enumerate(targets):
            pltpu.make_async_remote_copy(src_ref=s_ref.at[ch], dst_ref=land_ref.at[mine], send_sem=sems[k], recv_sem=sems[3 + k],
                                         device_id=to, device_id_type=MESH).start()
        token[...] = jnp.zeros_like(token)

    outs = pl.pallas_call(
        body, name=name,
        out_shape=(pltpu.SemaphoreType.DMA(()),) * 6 + (pltpu.HBM(s.shape, s.dtype), pltpu.HBM(land.shape, land.dtype),
                                                        jax.ShapeDtypeStruct((8, 128), F32)),
        in_specs=(_HBM, _HBM, _ANY), out_specs=(_SEM,) * 6 + (_HBM, _HBM, pl.BlockSpec(memory_space=pltpu.VMEM)),
        input_output_aliases={0: 6, 1: 7}, compiler_params=pltpu.CompilerParams(has_side_effects=_DATAFLOW),
    )(_hbm(s), _hbm(land), order)
    return outs[:6], outs[6], outs[7], outs[8]


def _chipsum_wait(sems, s_thru, land_thru, after, name):
    def body(s_ref, land_ref, *rest):
        sems_ = rest[:6]
        mine, targets = _chip_targets()
        for k, (to, ch) in enumerate(targets):
            cp = pltpu.make_async_remote_copy(src_ref=s_ref.at[ch], dst_ref=land_ref.at[ch], send_sem=sems_[k], recv_sem=sems_[3 + k],
                                              device_id=to, device_id_type=MESH)
            cp.wait_send()
            cp.wait_recv()

    return pl.pallas_call(
        body, name=name, out_shape=(pltpu.HBM(s_thru.shape, s_thru.dtype), pltpu.HBM(land_thru.shape, land_thru.dtype)),
        in_specs=(_HBM, _HBM) + (_SEM,) * 6 + (_ANY,), out_specs=(_HBM, _HBM), input_output_aliases={0: 0, 1: 1},
        compiler_params=pltpu.CompilerParams(has_side_effects=_DATAFLOW),
    )(s_thru, land_thru, *sems, after)[1]


def _pair_start(p, order, name):
    def body(p_ref, land_ref, order_ref, *rest):
        sems, token = rest[:8], rest[10]
        x, y, c = _me()
        for ch in range(4):
            pltpu.make_async_remote_copy(src_ref=p_ref.at[ch, 1 - c], dst_ref=land_ref.at[ch], send_sem=sems[ch],
                                         recv_sem=sems[4 + ch], device_id=(x, y, 1 - c), device_id_type=MESH).start()
        token[...] = jnp.zeros_like(token)

    land = lax.empty((4,) + p.shape[2:], p.dtype)
    outs = pl.pallas_call(
        body, name=name,
        out_shape=(pltpu.SemaphoreType.DMA(()),) * 8 + (pltpu.HBM(p.shape, p.dtype), pltpu.HBM(land.shape, land.dtype),
                                                        jax.ShapeDtypeStruct((8, 128), F32)),
        in_specs=(_HBM, _HBM, _ANY), out_specs=(_SEM,) * 8 + (_HBM, _HBM, pl.BlockSpec(memory_space=pltpu.VMEM)),
        input_output_aliases={0: 8, 1: 9}, compiler_params=pltpu.CompilerParams(has_side_effects=_DATAFLOW),
    )(_hbm(p), _hbm(land), order)
    return outs[:8], outs[8], outs[9], outs[10]


def _pair_wait(sems, p_thru, land_thru, after, name):
    def body(p_ref, land_ref, *rest):
        sems_ = rest[:8]
        x, y, c = _me()
        for ch in range(4):
            cp = pltpu.make_async_remote_copy(src_ref=p_ref.at[ch, 1 - c], dst_ref=land_ref.at[ch], send_sem=sems_[ch],
                                              recv_sem=sems_[4 + ch], device_id=(x, y, 1 - c), device_id_type=MESH)
            cp.wait_send()
            cp.wait_recv()

    return pl.pallas_call(
        body, name=name, out_shape=(pltpu.HBM(p_thru.shape, p_thru.dtype), pltpu.HBM(land_thru.shape, land_thru.dtype)),
        in_specs=(_HBM, _HBM) + (_SEM,) * 8 + (_ANY,), out_specs=(_HBM, _HBM), input_output_aliases={0: 0, 1: 1},
        compiler_params=pltpu.CompilerParams(has_side_effects=_DATAFLOW),
    )(p_thru, land_thru, *sems, after)


def _sibling_start(p, order, name):
    def body(p_ref, land_ref, order_ref, send_sem, recv_sem, p_thru, land_thru, token):
        x, y, c = _me()
        pltpu.make_async_remote_copy(src_ref=p_ref, dst_ref=land_ref, send_sem=send_sem, recv_sem=recv_sem,
                                     device_id=(x, y, 1 - c), device_id_type=MESH).start()
        token[...] = jnp.zeros_like(token)

    land = lax.empty(p.shape, p.dtype)
    outs = pl.pallas_call(
        body, name=name,
        out_shape=(pltpu.SemaphoreType.DMA(()),) * 2 + (pltpu.HBM(p.shape, p.dtype), pltpu.HBM(p.shape, p.dtype),
                                                        jax.ShapeDtypeStruct((8, 128), F32)),
        in_specs=(_HBM, _HBM, _ANY), out_specs=(_SEM,) * 2 + (_HBM, _HBM, pl.BlockSpec(memory_space=pltpu.VMEM)),
        input_output_aliases={0: 2, 1: 3}, compiler_params=pltpu.CompilerParams(has_side_effects=_DATAFLOW),
    )(_hbm(p), _hbm(land), order)
    return outs[:2], outs[2], outs[3], outs[4]


def _sibling_wait(sems, p_thru, land_thru, after, name):
    def body(p_ref, land_ref, send_sem, recv_sem, after_ref, p_dead, got_ref):
        x, y, c = _me()
        cp = pltpu.make_async_remote_copy(src_ref=p_ref, dst_ref=land_ref, send_sem=send_sem, recv_sem=recv_sem,
                                          device_id=(x, y, 1 - c), device_id_type=MESH)
        cp.wait_send()
        cp.wait_recv()

    return pl.pallas_call(
        body, name=name, out_shape=(pltpu.HBM(p_thru.shape, p_thru.dtype), pltpu.HBM(land_thru.shape, land_thru.dtype)),
        in_specs=(_HBM, _HBM, _SEM, _SEM, _ANY), out_specs=(_HBM, _HBM), input_output_aliases={0: 0, 1: 1},
        compiler_params=pltpu.CompilerParams(has_side_effects=_DATAFLOW),
    )(p_thru, land_thru, *sems, after)[1]


def _mm(a, b, *, mode, name, out_dtypes=(F32,), epilogue=None, extras=(), tm=1024, tn=1024, tk=2048,
        col_blocked_b=False, col_blocked_out=False, order=None):
    CB = 1024
    if col_blocked_b:
        assert mode in ("nn", "nt") and b.shape[2] == CB
        (M, K), N = a.shape, (b.shape[0] * CB if mode == "nn" else b.shape[1])
        assert mode == "nn" or tk % CB == 0
        tn = CB if mode == "nn" else tn
    elif mode == "nn":
        (M, K), N = a.shape, b.shape[1]
    elif mode == "nt":
        (M, K), N = a.shape, b.shape[0]
    else:
        (K, M), N = a.shape, b.shape[1]
    if col_blocked_out:
        assert len(out_dtypes) == 1 and N % CB == 0
        tn = CB
    tm, tn, tk = min(tm, M), min(tn, N), min(tk, K)
    assert M % tm == 0 and N % tn == 0 and K % tk == 0, (M, N, K, tm, tn, tk)
    nk = K // tk
    ne, no = len(extras), len(out_dtypes)
    dims = {"nn": (((1,), (0,)), ((), ())), "nt": (((1,), (1,)), ((), ())), "tn": (((0,), (0,)), ((), ()))}[mode]

    no_ = 0 if order is None else 1

    def body(a_ref, b_ref, *rest):
        rest = rest[no_:]
        ex, outs = rest[:ne], rest[ne:ne + no]

        def finish(acc):
            res = epilogue(acc, *[e[...] for e in ex]) if epilogue is not None else (acc,)
            for o, r in zip(outs, res):
                o[...] = r.astype(o.dtype)

        if col_blocked_b and mode == "nt":
            part = sum(lax.dot_general(a_ref[:, q * CB:(q + 1) * CB], b_ref[q], dims, preferred_element_type=F32)
                       for q in range(tk // CB))
        else:
            part = lax.dot_general(a_ref[...], b_ref[...], dims, preferred_element_type=F32)
        if nk == 1:
            finish(part)
        else:
            acc_ref = rest[-1]
            k = pl.program_id(2)

            @pl.when(k == 0)
            def _():
                acc_ref[...] = part

            @pl.when(k > 0)
            def _():
                acc_ref[...] += part

            @pl.when(k == nk - 1)
            def _():
                finish(acc_ref[...])

    a_spec = {"nn": pl.BlockSpec((tm, tk), lambda i, j, k: (i, k)), "nt": pl.BlockSpec((tm, tk), lambda i, j, k: (i, k)),
              "tn": pl.BlockSpec((tk, tm), lambda i, j, k: (k, i))}[mode]
    b_spec = {"nn": pl.BlockSpec((tk, tn), lambda i, j, k: (k, j)), "nt": pl.BlockSpec((tn, tk), lambda i, j, k: (j, k)),
              "tn": pl.BlockSpec((tk, tn), lambda i, j, k: (k, j))}[mode]
    if col_blocked_b:
        b_spec = (pl.BlockSpec((None, tk, CB), lambda i, j, k: (j, k, 0)) if mode == "nn"
                  else pl.BlockSpec((tk // CB, tn, CB), lambda i, j, k: (k, j, 0)))
    e_spec = pl.BlockSpec((tm, tn), lambda i, j, k: (i, j))
    o_spec, o_dims = e_spec, (M, N)
    if col_blocked_out:
        o_spec, o_dims = pl.BlockSpec((None, tm, CB), lambda i, j, k: (j, i, 0)), (N // CB, M, CB)
    outs = pl.pallas_call(
        body, name=name, grid=(M // tm, N // tn, nk),
        in_specs=[a_spec, b_spec] + [_ANY] * no_ + [e_spec] * ne, out_specs=[o_spec] * no,
        out_shape=[jax.ShapeDtypeStruct(o_dims, dt) for dt in out_dtypes],
        scratch_shapes=[pltpu.VMEM((tm, tn), F32)] if nk > 1 else [],
        compiler_params=pltpu.CompilerParams(dimension_semantics=("parallel", "parallel", "arbitrary")),
    )(a, b, *([] if order is None else [order]), *extras)
    return outs if no > 1 else outs[0]


def _dw_half(a, b, side, *, axis, name, add=None, order=None, tile=1024, tk=2048):
    (K, M), N = a.shape, b.shape[1]
    tk = min(tk, K)
    nk = K // tk
    if axis == "m":
        tm, tn = tile, min(N, 1024)
        grid, o_dims = (4, N // tn, nk), (4, tile, N)
        a_spec = pl.BlockSpec((tk, tm), lambda q, j, k, s: (k, 2 * q + s[0]))
        b_spec = pl.BlockSpec((tk, tn), lambda q, j, k, s: (k, j))
        o_spec = pl.BlockSpec((None, tm, tn), lambda q, j, k, s: (q, 0, j))
    else:
        tm, tn = min(M, 1024), tile
        grid, o_dims = (M // tm, 4, nk), (4, M, tile)
        a_spec = pl.BlockSpec((tk, tm), lambda i, q, k, s: (k, i))
        b_spec = pl.BlockSpec((tk, tn), lambda i, q, k, s: (k, 2 * q + s[0]))
        o_spec = pl.BlockSpec((None, tm, tn), lambda i, q, k, s: (q, i, 0))
    n_order, n_add = int(order is not None), int(add is not None)
    n_out = 1 + n_add

    def body(s_ref, a_ref, b_ref, *rest):
        rest = rest[n_order:]
        outs, acc_ref = rest[n_add:n_add + n_out], rest[-1]
        k = pl.program_id(2)
        part = _dot_tn(a_ref[...], b_ref[...])

        @pl.when(k == 0)
        def _():
            acc_ref[...] = part

        @pl.when(k > 0)
        def _():
            acc_ref[...] += part

        @pl.when(k == nk - 1)
        def _():
            res = acc_ref[...] + rest[0][...].astype(F32) if n_add else acc_ref[...]
            for o in outs:
                o[...] = res.astype(o.dtype)

    outs = pl.pallas_call(
        body, name=name, out_shape=[jax.ShapeDtypeStruct(o_dims, BF16)] * n_out,
        grid_spec=pltpu.PrefetchScalarGridSpec(
            num_scalar_prefetch=1, grid=grid, in_specs=[a_spec, b_spec] + [_ANY] * n_order + [o_spec] * n_add,
            out_specs=[o_spec] * n_out, scratch_shapes=[pltpu.VMEM((tm, tn), F32)]),
        compiler_params=pltpu.CompilerParams(dimension_semantics=("arbitrary", "arbitrary", "arbitrary")),
    )(side, a, b, *([order] if n_order else []), *([add] if n_add else []))
    return outs if n_add else outs[0]


def _norm_fwd(x, g, sc, sh, name, resid=None):
    B, S, Dm = x.shape
    ts = min(S, 256)
    tok = pl.BlockSpec((None, ts, Dm), lambda b, i: (b, i, 0))
    row = pl.BlockSpec((None, 1, Dm), lambda b, i: (b, 0, 0))
    par = pl.BlockSpec((1, Dm), lambda b, i: (0, 0))

    def body(*refs):
        if resid is not None:
            x_ref, br_ref, gt_ref, g_ref, sc_ref, sh_ref, xo_ref, h_ref = refs
            xv = x_ref[...] + gt_ref[...] * br_ref[...]
            xo_ref[...] = xv
        else:
            x_ref, g_ref, sc_ref, sh_ref, h_ref = refs
            xv = x_ref[...]
        r = lax.rsqrt(jnp.mean(xv * xv, axis=-1, keepdims=True) + EPS)
        h_ref[...] = ((xv * r * g_ref[...]) * (1.0 + sc_ref[...]) + sh_ref[...]).astype(BF16)

    h_shape = jax.ShapeDtypeStruct((B, S, Dm), BF16)
    if resid is not None:
        return pl.pallas_call(body, name=name, grid=(B, S // ts), in_specs=[tok, tok, row, par, row, row],
                              out_specs=[tok, tok], out_shape=[jax.ShapeDtypeStruct((B, S, Dm), F32), h_shape],
                              )(x, resid[0], resid[1], g, sc, sh)
    return pl.pallas_call(body, name=name, grid=(B, S // ts), in_specs=[tok, par, row, row], out_specs=tok,
                          out_shape=h_shape)(x, g, sc, sh)


def _norm_bwd(x, g, name, *, sc=None, dh=None, dres=None, tgt=None, br=None, gate=None, x_is_prev=False):
    B, S, Dm = x.shape
    ts = min(S, 256)
    final = tgt is not None
    has_br = br is not None
    tok = pl.BlockSpec((None, ts, Dm), lambda b, i: (b, i, 0))
    row = pl.BlockSpec((None, 1, Dm), lambda b, i: (b, 0, 0))
    par = pl.BlockSpec((1, Dm), lambda b, i: (0, 0))
    ins, in_specs = [x, g], [tok, par]
    if final:
        ins, in_specs = ins + [tgt], in_specs + [tok]
    else:
        ins, in_specs = ins + [sc, dh], in_specs + [row, tok]
    if dres is not None:
        ins, in_specs = ins + [dres], in_specs + [tok]
    if has_br:
        ins, in_specs = ins + [br, gate], in_specs + [tok, row]
    n_in = len(ins)
    out_shape = [jax.ShapeDtypeStruct((B, S, Dm), F32), jax.ShapeDtypeStruct((1, Dm), F32)]
    out_specs = [tok, par]
    if final:
        out_shape.append(jax.ShapeDtypeStruct((1, 128), F32))
        out_specs.append(pl.BlockSpec((1, 128), lambda b, i: (0, 0)))
    else:
        out_shape += [jax.ShapeDtypeStruct((B, 1, Dm), F32)] * 2
        out_specs += [row, row]
    if has_br:
        out_shape += [jax.ShapeDtypeStruct((B, S, Dm), BF16), jax.ShapeDtypeStruct((B, 1, Dm), F32)]
        out_specs += [tok, row]

    def body(*refs):
        it = iter(refs[:n_in])
        outs = iter(refs[n_in:])
        x_ref, g_ref = next(it), next(it)
        b, i = pl.program_id(0), pl.program_id(1)
        first, first_row = (b == 0) & (i == 0), i == 0
        xv, gv = x_ref[...], g_ref[...]
        if x_is_prev:
            xv = xv + refs[n_in - 1][...] * refs[n_in - 2][...]
        r = lax.rsqrt(jnp.mean(xv * xv, axis=-1, keepdims=True) + EPS)
        n = xv * r
        dx_ref, dg_ref = next(outs), next(outs)

        def acc(ref, val, init):
            @pl.when(init)
            def _():
                ref[...] = val

            @pl.when(jnp.logical_not(init))
            def _():
                ref[...] += val

        if final:
            t_ref = next(it)
            loss_ref = next(outs)
            e = n * gv - t_ref[...]
            acc(loss_ref, jnp.zeros((1, 128), F32) + 0.5 * jnp.sum(e * e) / Dm, first)
            dyg = e * (1.0 / Dm)
        else:
            sc_ref, dh_ref = next(it), next(it)
            dsc_ref, dsh_ref = next(outs), next(outs)
            dhv = dh_ref[...]
            acc(dsh_ref, jnp.sum(dhv, axis=0, keepdims=True), first_row)
            acc(dsc_ref, jnp.sum(dhv * (n * gv), axis=0, keepdims=True), first_row)
            dyg = dhv * (1.0 + sc_ref[...])
        acc(dg_ref, jnp.sum(dyg * n, axis=0, keepdims=True), first)
        dn = dyg * gv
        dx = r * (dn - n * jnp.mean(dn * n, axis=-1, keepdims=True))
        if dres is not None:
            dx = dx + next(it)[...]
        dx_ref[...] = dx
        if has_br:
            br_ref, gt_ref = next(it), next(it)
            dbr_ref, dgt_ref = next(outs), next(outs)
            dbr_ref[...] = (dx * gt_ref[...]).astype(BF16)
            acc(dgt_ref, jnp.sum(dx * br_ref[...], axis=0, keepdims=True), first_row)

    outs = pl.pallas_call(body, name=name, grid=(B, S // ts), in_specs=in_specs, out_specs=out_specs, out_shape=out_shape,
                          compiler_params=pltpu.CompilerParams(dimension_semantics=("arbitrary", "arbitrary")))(*ins)
    res = dict(dx=outs[0], dg=outs[1])
    if final:
        res["loss"] = outs[2]
    else:
        res["dsc"], res["dsh"] = outs[2], outs[3]
    if has_br:
        res["dbr"], res["dgate"] = outs[-2], outs[-1]
    return res


def _gm_heads(vg, lng, lnb):
    res = []
    for h in range(GM_H):
        sl = slice(h * 128, (h + 1) * 128)
        vh = vg[:, sl]
        xc = vh - jnp.mean(vh, axis=-1, keepdims=True)
        rstd = lax.rsqrt(jnp.mean(xc * xc, axis=-1, keepdims=True) + 1e-5)
        xhat = xc * rstd
        res.append((xhat, rstd, xhat * lng[:, sl] + lnb[:, sl]))
    return res


def _gm_gate(heads, wt_ref, bsx, nch):
    cols = []
    for h in range(GM_H):
        vn = heads[h][2].astype(BF16)
        rows = [_dot(wt_ref[h], vn[c * CHUNK:(c + 1) * CHUNK]) + bsx[:, h * 128:(h + 1) * 128] for c in range(nch)]
        cols.append(jnp.concatenate(rows, axis=0) if nch > 1 else rows[0])
    return jnp.concatenate(cols, axis=1)


def _gm_specs(S):
    tb = min(S, 512)
    u = pl.BlockSpec((None, tb, GM_W), lambda b, i: (b, i, OFF["u"] // GM_W))
    v = pl.BlockSpec((None, tb, GM_W), lambda b, i: (b, i, OFF["v"] // GM_W))
    tok = pl.BlockSpec((None, tb, GM_W), lambda b, i: (b, i, 0))
    return tb, u, v, tok


def _gmlp_fwd(P, lng, lnb, wt, bsx, og, name):
    B, S, _ = P.shape
    tb, u_spec, v_spec, tok = _gm_specs(S)
    nch = tb // CHUNK

    def body(u_ref, v_ref, lng_ref, lnb_ref, wt_ref, bsx_ref, og_ref, o_ref):
        heads = _gm_heads(_gelu(v_ref[...]), lng_ref[...], lnb_ref[...])
        y = _gelu(u_ref[...]) * _gm_gate(heads, wt_ref, bsx_ref[...], nch)
        r = lax.rsqrt(jnp.mean(y * y, axis=-1, keepdims=True) + EPS)
        o_ref[...] = (y * r * og_ref[...]).astype(BF16)

    return pl.pallas_call(
        body, name=name, grid=(B, S // tb),
        in_specs=[u_spec, v_spec, _full((1, GM_W)), _full((1, GM_W)), _full((GM_H, 128, 128)), _full((128, GM_W)), _full((1, GM_W))],
        out_specs=tok, out_shape=jax.ShapeDtypeStruct((B, S, GM_W + ATT_W + SSM_W), BF16))(P, P, lng, lnb, wt, bsx, og)


def _gmlp_bwd(P, dcat, lng, lnb, wt, wtT, bsx, og, name):
    B, S, _ = P.shape
    tb, u_spec, v_spec, tok = _gm_specs(S)
    nch = tb // CHUNK
    do_spec = pl.BlockSpec((None, tb, GM_W), lambda b, i: (b, i, 0))

    def body(u_ref, v_ref, do_ref, lng_ref, lnb_ref, wt_ref, wtT_ref, bsx_ref, og_ref,
             du_ref, dv_ref, dlng_ref, dlnb_ref, dws_ref, dbsx_ref, dog_ref):
        first = (pl.program_id(0) == 0) & (pl.program_id(1) == 0)

        @pl.when(first)
        def _():
            for ref in (dlng_ref, dlnb_ref, dws_ref, dbsx_ref, dog_ref):
                ref[...] = jnp.zeros(ref.shape, F32)

        u, v, lng = u_ref[...], v_ref[...], lng_ref[...]
        ug = _gelu(u)
        heads = _gm_heads(_gelu(v), lng, lnb_ref[...])
        gate = _gm_gate(heads, wt_ref, bsx_ref[...], nch)
        y = ug * gate
        r = lax.rsqrt(jnp.mean(y * y, axis=-1, keepdims=True) + EPS)
        yn = y * r
        dout = do_ref[...]
        dog_ref[...] += jnp.sum(dout * yn, axis=0, keepdims=True)
        dyn = dout * og_ref[...]
        dy = r * (dyn - yn * jnp.mean(dyn * yn, axis=-1, keepdims=True))
        du_ref[...] = (dy * gate * _gelu_grad(u)).astype(BF16)
        dgate = dy * ug
        tril = lax.broadcasted_iota(jnp.int32, (128, 128), 0) >= lax.broadcasted_iota(jnp.int32, (128, 128), 1)
        dvg = []
        for h in range(GM_H):
            sl = slice(h * 128, (h + 1) * 128)
            xhat, rstd, vn = heads[h]
            vnb = vn.astype(BF16)
            dgh = dgate[:, sl]
            dgb = dgh.astype(BF16)
            dbs = jnp.zeros((128, 128), F32)
            dw = jnp.zeros((128, 128), F32)
            dvn = []
            for c in range(nch):
                rs = slice(c * CHUNK, (c + 1) * CHUNK)
                dbs = dbs + dgh[rs]
                dw = dw + _dot_nt(dgb[rs], vnb[rs])
                dvn.append(_dot(wtT_ref[h], dgb[rs]))
            dvn = jnp.concatenate(dvn, axis=0) if nch > 1 else dvn[0]
            dbsx_ref[:, sl] += dbs
            dws_ref[h] += jnp.where(tril, dw, 0.0)
            dlng_ref[:, sl] += jnp.sum(dvn * xhat, axis=0, keepdims=True)
            dlnb_ref[:, sl] += jnp.sum(dvn, axis=0, keepdims=True)
            dxh = dvn * lng[:, sl]
            dvg.append(rstd * (dxh - jnp.mean(dxh, axis=-1, keepdims=True) - xhat * jnp.mean(dxh * xhat, axis=-1, keepdims=True)))
        dv_ref[...] = (jnp.concatenate(dvg, axis=1) * _gelu_grad(v)).astype(BF16)

    p512, w3 = _full((1, GM_W)), _full((GM_H, 128, 128))
    return pl.pallas_call(
        body, name=name, grid=(B, S // tb),
        in_specs=[u_spec, v_spec, do_spec, p512, p512, w3, w3, _full((128, GM_W)), p512],
        out_specs=[tok, tok, p512, p512, w3, _full((128, GM_W)), p512],
        out_shape=[jax.ShapeDtypeStruct((B, S, GM_W), BF16)] * 2 + [
            jax.ShapeDtypeStruct((1, GM_W), F32), jax.ShapeDtypeStruct((1, GM_W), F32),
            jax.ShapeDtypeStruct((GM_H, 128, 128), F32), jax.ShapeDtypeStruct((128, GM_W), F32),
            jax.ShapeDtypeStruct((1, GM_W), F32)],
        compiler_params=pltpu.CompilerParams(dimension_semantics=("arbitrary", "arbitrary")),
    )(P, P, dcat, lng, lnb, wt, wtT, bsx, og)


def _lane_half():
    return lax.broadcasted_iota(jnp.int32, (128, 128), 1) // 64


def _att_stack(x, kvh, dtype):
    half = _lane_half()
    rows = []
    for g in range(4):
        i = kvh * 4 + g
        pair = x[:, (i // 2) * 128:(i // 2 + 1) * 128]
        if i % 2 != kvh:
            pair = pltpu.roll(pair, 64, 1)
        rows.append(jnp.where(half == kvh, pair, 0.0))
    return jnp.concatenate(rows, axis=0).astype(dtype)


def _att_unstack(pairs, y, kvh):
    half = _lane_half()
    for g in range(4):
        i = kvh * 4 + g
        piece = y[g * 128:(g + 1) * 128]
        if i % 2 != kvh:
            piece = pltpu.roll(piece, 64, 1)
        pairs[i // 2] = jnp.where(half == i % 2, piece, pairs[i // 2])
    return pairs


def _att_probs(qb, k2, st, sink_ref, kvh):
    qm = _att_stack(qb, kvh, BF16)
    s = _dot_nt(qm, k2) * (64 ** -0.5)
    qi = lax.broadcasted_iota(jnp.int32, (512, 256), 0) % 128
    kj = lax.broadcasted_iota(jnp.int32, (512, 256), 1)
    diff = qi + 128 - kj
    valid = (diff >= 0) & (diff < 128) & (st + kj - 128 >= 0)
    s = jnp.where(valid, s, NEG_INF)
    grp = lax.broadcasted_iota(jnp.int32, (512, 1), 0) // 128
    sink = jnp.zeros((512, 1), F32)
    for g in range(4):
        sink = jnp.where(grp == g, sink_ref[kvh * 4 + g], sink)
    m = jnp.maximum(jnp.max(s, axis=-1, keepdims=True), sink)
    e = jnp.exp(s - m)
    esink = jnp.exp(sink - m)
    inv = 1.0 / (jnp.sum(e, axis=-1, keepdims=True) + esink)
    return qm, e * inv, esink * inv


def _att_specs(S):
    q = pl.BlockSpec((None, S, ATT_W), lambda b: (b, 0, OFF["q"] // ATT_W))
    k = pl.BlockSpec((None, S, KV_W), lambda b: (b, 0, OFF["k"] // KV_W))
    v = pl.BlockSpec((None, S, KV_W), lambda b: (b, 0, OFF["vv"] // KV_W))
    tok = pl.BlockSpec((None, S, ATT_W), lambda b: (b, 0, 0))
    kv = pl.BlockSpec((None, S, KV_W), lambda b: (b, 0, 0))
    return q, k, v, tok, kv


_SMEM = pl.BlockSpec(memory_space=pltpu.SMEM)


def _attn_fwd(P, sinks, og, cat, name):
    B, S, _ = P.shape
    q_spec, k_spec, v_spec, _, _ = _att_specs(S)
    tok = pl.BlockSpec((None, S, ATT_W), lambda b: (b, 0, GM_W // ATT_W))

    def body(q_ref, k_ref, v_ref, sink_ref, og_ref, cat_ref, o_ref, kpad, vpad):
        kpad[0:128, :] = jnp.zeros((128, KV_W), BF16)
        vpad[0:128, :] = jnp.zeros((128, KV_W), BF16)
        kpad[128:, :] = k_ref[...].astype(BF16)
        vpad[128:, :] = v_ref[...].astype(BF16)

        def step(n, carry):
            st = pl.multiple_of(n * 128, 128)
            qb = q_ref[pl.ds(st, 128), :]
            k2, v2 = kpad[pl.ds(st, 256), :], vpad[pl.ds(st, 256), :]
            pairs = [jnp.zeros((128, 128), F32)] * 4
            for kvh in range(2):
                _, p, _ = _att_probs(qb, k2, st, sink_ref, kvh)
                pairs = _att_unstack(pairs, _dot(p.astype(BF16), v2), kvh)
            o = jnp.concatenate(pairs, axis=1)
            r = lax.rsqrt(jnp.mean(o * o, axis=-1, keepdims=True) + EPS)
            o_ref[pl.ds(st, 128), :] = (o * r * og_ref[...]).astype(BF16)
            return carry

        lax.fori_loop(0, S // 128, step, 0)

    return pl.pallas_call(
        body, name=name, grid=(B,), in_specs=[q_spec, k_spec, v_spec, _SMEM, _full((1, ATT_W)), _ANY], out_specs=tok,
        out_shape=jax.ShapeDtypeStruct(cat.shape, BF16), input_output_aliases={5: 0},
        scratch_shapes=[pltpu.VMEM((S + 128, KV_W), BF16)] * 2)(P, P, P, sinks, og, cat)


def _attn_bwd(P, dcat, sinks, og, name):
    B, S, _ = P.shape
    q_spec, k_spec, v_spec, tok, kv = _att_specs(S)
    do_spec = pl.BlockSpec((None, S, ATT_W), lambda b: (b, 0, GM_W // ATT_W))

    def body(q_ref, k_ref, v_ref, do_ref, sink_ref, og_ref, dq_ref, dk_ref, dv_ref, dsink_ref, dog_ref,
             kpad, vpad, dkpad, dvpad):
        @pl.when(pl.program_id(0) == 0)
        def _():
            dsink_ref[...] = jnp.zeros((8, 128), F32)
            dog_ref[...] = jnp.zeros((1, ATT_W), F32)

        kpad[0:128, :] = jnp.zeros((128, KV_W), BF16)
        vpad[0:128, :] = jnp.zeros((128, KV_W), BF16)
        kpad[128:, :] = k_ref[...].astype(BF16)
        vpad[128:, :] = v_ref[...].astype(BF16)
        dkpad[...] = jnp.zeros((S + 128, KV_W), F32)
        dvpad[...] = jnp.zeros((S + 128, KV_W), F32)
        half = _lane_half()
        head_row = lax.broadcasted_iota(jnp.int32, (8, 128), 0)

        def step(n, carry):
            st = pl.multiple_of(n * 128, 128)
            qb = q_ref[pl.ds(st, 128), :]
            k2, v2 = kpad[pl.ds(st, 256), :], vpad[pl.ds(st, 256), :]
            saved, pairs = [], [jnp.zeros((128, 128), F32)] * 4
            for kvh in range(2):
                qm, p, psink = _att_probs(qb, k2, st, sink_ref, kvh)
                o = _dot(p.astype(BF16), v2)
                saved.append((qm, p, psink, o))
                pairs = _att_unstack(pairs, o, kvh)
            o = jnp.concatenate(pairs, axis=1)
            r = lax.rsqrt(jnp.mean(o * o, axis=-1, keepdims=True) + EPS)
            on = o * r
            dout = do_ref[pl.ds(st, 128), :]
            dog_ref[...] += jnp.sum(dout * on, axis=0, keepdims=True)
            dyn = dout * og_ref[...]
            do = r * (dyn - on * jnp.mean(dyn * on, axis=-1, keepdims=True))
            dq_pairs = [jnp.zeros((128, 128), F32)] * 4
            dsink = jnp.zeros((8, 128), F32)
            for kvh in range(2):
                qm, p, psink, og_ = saved[kvh]
                dog = _att_stack(do, kvh, F32)
                delta = jnp.sum(dog * jnp.where(jnp.concatenate([half] * 4, axis=0) == kvh, og_, 0.0), axis=-1, keepdims=True)
                dogb, pb = dog.astype(BF16), p.astype(BF16)
                dvpad[pl.ds(st, 256), :] += _dot_tn(pb, dogb)
                dp = _dot_nt(dogb, v2)
                ds = (p * (dp - delta) * (64 ** -0.5)).astype(BF16)
                sd = psink * delta
                for g in range(4):
                    dsink = dsink - jnp.where(head_row == kvh * 4 + g, jnp.sum(sd[g * 128:(g + 1) * 128]), 0.0)
                dq_pairs = _att_unstack(dq_pairs, _dot(ds, k2), kvh)
                dkpad[pl.ds(st, 256), :] += _dot_tn(ds, qm)
            dsink_ref[...] += dsink
            dq_ref[pl.ds(st, 128), :] = jnp.concatenate(dq_pairs, axis=1).astype(BF16)
            return carry

        lax.fori_loop(0, S // 128, step, 0)
        dk_ref[...] = dkpad[128:, :].astype(BF16)
        dv_ref[...] = dvpad[128:, :].astype(BF16)

    return pl.pallas_call(
        body, name=name, grid=(B,),
        in_specs=[q_spec, k_spec, v_spec, do_spec, _SMEM, _full((1, ATT_W))],
        out_specs=[tok, kv, kv, _full((8, 128)), _full((1, ATT_W))],
        out_shape=[jax.ShapeDtypeStruct((B, S, ATT_W), BF16), jax.ShapeDtypeStruct((B, S, KV_W), BF16),
                   jax.ShapeDtypeStruct((B, S, KV_W), BF16), jax.ShapeDtypeStruct((8, 128), F32),
                   jax.ShapeDtypeStruct((1, ATT_W), F32)],
        scratch_shapes=[pltpu.VMEM((S + 128, KV_W), BF16)] * 2 + [pltpu.VMEM((S + 128, KV_W), F32)] * 2,
        compiler_params=pltpu.CompilerParams(dimension_semantics=("arbitrary",)),
    )(P, P, P, dcat, sinks, og)


CONV_TC = 256
CONV_RC = 64


def _conv_taps(ext, r0):
    return [ext[pl.ds(r0 + 8 - k, CONV_RC), :] for k in range(4)]


def _conv_pre(taps, w_ref, b_ref):
    acc = b_ref[...] + w_ref[3:4, :] * taps[0]
    for k in range(1, 4):
        acc = acc + w_ref[3 - k:4 - k, :] * taps[k]
    return acc


def _conv_fwd(P, w8, b, name):
    B, S, _ = P.shape
    nj = CONV_CH // CONV_TC
    x_spec = pl.BlockSpec((None, S, CONV_TC), lambda b_, j: (b_, 0, OFF["xbc"] // CONV_TC + j))
    tok = pl.BlockSpec((None, S, CONV_TC), lambda b_, j: (b_, 0, j))

    def body(x_ref, w_ref, b_ref, o_ref, ext):
        ext[0:8, :] = jnp.zeros((8, CONV_TC), F32)
        ext[8:, :] = x_ref[...]
        for r0 in range(0, S, CONV_RC):
            pre = _conv_pre(_conv_taps(ext, r0), w_ref, b_ref)
            o_ref[pl.ds(r0, CONV_RC), :] = pre * _sigmoid(pre)

    return pl.pallas_call(
        body, name=name, grid=(B, nj),
        in_specs=[x_spec, pl.BlockSpec((8, CONV_TC), lambda b_, j: (0, j)), pl.BlockSpec((1, CONV_TC), lambda b_, j: (0, j))],
        out_specs=tok, out_shape=jax.ShapeDtypeStruct((B, S, CONV_CH), F32),
        scratch_shapes=[pltpu.VMEM((S + 8, CONV_TC), F32)])(P, w8, b)


def _conv_bwd(P, dact, w8, b, name):
    B, S, _ = P.shape
    nj = CONV_CH // CONV_TC
    x_spec = pl.BlockSpec((None, S, CONV_TC), lambda j, b_: (b_, 0, OFF["xbc"] // CONV_TC + j))
    tok = pl.BlockSpec((None, S, CONV_TC), lambda j, b_: (b_, 0, j))
    w_spec = pl.BlockSpec((8, CONV_TC), lambda j, b_: (0, j))
    b_spec = pl.BlockSpec((1, CONV_TC), lambda j, b_: (0, j))

    def body(x_ref, d_ref, w_ref, b_ref, dx_ref, dw_ref, db_ref, ext, extd):
        @pl.when(pl.program_id(1) == 0)
        def _():
            dw_ref[...] = jnp.zeros((8, CONV_TC), F32)
            db_ref[...] = jnp.zeros((1, CONV_TC), F32)

        ext[0:8, :] = jnp.zeros((8, CONV_TC), F32)
        ext[8:, :] = x_ref[...]
        extd[pl.ds(8 + S, 8), :] = jnp.zeros((8, CONV_TC), F32)
        db = jnp.zeros((1, CONV_TC), F32)
        dws = [jnp.zeros((1, CONV_TC), F32)] * 4
        for r0 in range(0, S, CONV_RC):
            taps = _conv_taps(ext, r0)
            pre = _conv_pre(taps, w_ref, b_ref)
            sg = _sigmoid(pre)
            dpre = d_ref[pl.ds(r0, CONV_RC), :] * (sg * (1.0 + pre * (1.0 - sg)))
            extd[pl.ds(8 + r0, CONV_RC), :] = dpre
            db = db + jnp.sum(dpre, axis=0, keepdims=True)
            dws = [dws[i] + jnp.sum(dpre * taps[3 - i], axis=0, keepdims=True) for i in range(4)]
        for r0 in range(0, S, CONV_RC):
            dx = w_ref[3:4, :] * extd[pl.ds(8 + r0, CONV_RC), :]
            for k in range(1, 4):
                dx = dx + w_ref[3 - k:4 - k, :] * extd[pl.ds(8 + r0 + k, CONV_RC), :]
            dx_ref[pl.ds(r0, CONV_RC), :] = dx.astype(BF16)
        db_ref[...] += db
        sub = lax.broadcasted_iota(jnp.int32, (8, CONV_TC), 0)
        dw_ref[...] += sum(jnp.where(sub == i, dws[i], 0.0) for i in range(4))

    return pl.pallas_call(
        body, name=name, grid=(nj, B), in_specs=[x_spec, tok, w_spec, b_spec], out_specs=[tok, w_spec, b_spec],
        out_shape=[jax.ShapeDtypeStruct((B, S, CONV_CH), BF16), jax.ShapeDtypeStruct((8, CONV_CH), F32),
                   jax.ShapeDtypeStruct((1, CONV_CH), F32)],
        scratch_shapes=[pltpu.VMEM((S + 8, CONV_TC), F32), pltpu.VMEM((S + 16, CONV_TC), F32)],
        compiler_params=pltpu.CompilerParams(dimension_semantics=("arbitrary", "arbitrary")),
    )(P, dact, w8, b)


def _ssd_consts():
    hd = np.arange(SSM_W) // SSM_HD
    E = (np.arange(128)[:, None] == hd[None, :]).astype(np.float32)
    tri = (np.arange(128)[:, None] >= np.arange(128)[None, :]).astype(np.float32)
    return jnp.asarray(E, BF16), jnp.asarray(E.T, BF16), jnp.asarray(tri, BF16), jnp.asarray(tri.T, BF16)


def _pieces(x, n):
    out, r = [], x
    for _ in range(n):
        p = r.astype(BF16)
        out.append(p)
        r = r - p.astype(F32)
    return out


def _dot01(x, m01, n):
    return sum(_dot(p, m01) for p in _pieces(x, n))


def _dot01_left(m01, x, n):
    return sum(_dot(m01, p) for p in _pieces(x, n))


def _ssd_pre(xa, dtraw, bias, alog, E, tri):
    lane = lax.broadcasted_iota(jnp.int32, (128, 128), 1)
    pre = dtraw + bias
    dtp = jnp.where(lane < SSM_H, jnp.maximum(pre, 0.0) + jnp.log(1.0 + jnp.exp(-jnp.abs(pre))), 0.0)
    a = -jnp.exp(alog)
    acs = _dot01_left(tri, dtp * a, 3)
    acsT = acs.T
    dtE, acsE = _dot01(dtp, E, 2), _dot01(acs, E, 3)
    X = xa[:, :SSM_W]
    xdt = X * dtE
    wE = jnp.exp(acsE[127:128, :] - acsE)
    eE = jnp.exp(acsE)
    cdE = eE[127:128, :]
    return dict(pre=pre, dtp=dtp, a=a, acs=acs, acsT=acsT, dtE=dtE, acsE=acsE, cdE=cdE, X=X, xdt=xdt, wE=wE, eE=eE)


def _ssd_decay(c, h):
    lm = lax.broadcasted_iota(jnp.int32, (128, 128), 0) >= lax.broadcasted_iota(jnp.int32, (128, 128), 1)
    return jnp.exp(jnp.where(lm, c["acs"][:, h:h + 1] - c["acsT"][h:h + 1, :], NEG_INF))


def _ssd_pair_operands(c, CB, h0):
    lane = lax.broadcasted_iota(jnp.int32, (128, 128), 1)
    L0, L1 = _ssd_decay(c, h0), _ssd_decay(c, h0 + 1)
    M = jnp.concatenate([CB * L0, CB * L1], axis=1).astype(BF16)
    xp = c["xdt"][:, h0 * 64:h0 * 64 + 128]
    BD = jnp.concatenate([jnp.where(lane < 64, xp, 0.0), jnp.where(lane >= 64, xp, 0.0)], axis=0).astype(BF16)
    return L0, L1, M, BD


def _ssd_y(c, xa, state_ref, dskipE):
    per_group, ys = [], []
    for g in range(SSM_G):
        gs = slice(g * 512, (g + 1) * 512)
        Bb = xa[:, SSM_W + g * 128:SSM_W + (g + 1) * 128].astype(BF16)
        Cb = xa[:, SSM_W + 256 + g * 128:SSM_W + 256 + (g + 1) * 128].astype(BF16)
        CB = _dot_nt(Cb, Bb)
        Sg = state_ref[:, gs]
        yoff = _dot(Cb, Sg.astype(BF16)) * c["eE"][:, gs]
        ydiag, pairs = [], []
        for j in range(4):
            ops = _ssd_pair_operands(c, CB, g * 8 + 2 * j)
            pairs.append(ops)
            ydiag.append(_dot(ops[2], ops[3]))
        ys.append(jnp.concatenate(ydiag, axis=1) + yoff)
        per_group.append(dict(Bb=Bb, Cb=Cb, CB=CB, Sg=Sg, yoff=yoff, pairs=pairs))
    Y = jnp.concatenate(ys, axis=1) + c["X"] * dskipE
    return Y, per_group


def _ssd_specs(S, rev):
    nc = S // CHUNK
    cm = (lambda b, i: (b, nc - 1 - i)) if rev else (lambda b, i: (b, i))
    xa = pl.BlockSpec((None, CHUNK, CONV_CH), lambda b, i: cm(b, i) + (0,))
    z = [pl.BlockSpec((None, CHUNK, 256), lambda b, i, q=q: cm(b, i) + (OFF["z"] // 256 + q,)) for q in range(4)]
    dt = pl.BlockSpec((None, CHUNK, 128), lambda b, i: cm(b, i) + (OFF["dt"] // 128,))
    tok = pl.BlockSpec((None, CHUNK, SSM_W), lambda b, i: cm(b, i) + (0,))
    st = pl.BlockSpec((None, None, 128, SSM_W), lambda b, i: cm(b, i) + (0, 0))
    return nc, xa, z, dt, tok, st


def _ssd_fwd(xact, P, bias, alog, dskipE, ng, cat, name):
    B, S, _ = P.shape
    nc, xa_spec, z_specs, dt_spec, _, st_spec = _ssd_specs(S, False)
    tok = pl.BlockSpec((None, CHUNK, SSM_W), lambda b, i: (b, i, 1))
    E, _, tri, _ = _ssd_consts()

    def body(xa_ref, z0, z1, z2, z3, dt_ref, bias_ref, alog_ref, dsk_ref, ng_ref, E_ref, tri_ref, cat_ref, o_ref, sp_ref, state):
        @pl.when(pl.program_id(1) == 0)
        def _():
            state[...] = jnp.zeros((128, SSM_W), F32)

        sp_ref[...] = state[...]
        xa = xa_ref[...]
        c = _ssd_pre(xa, dt_ref[...], bias_ref[...], alog_ref[...], E_ref[...], tri_ref[...])
        Y, groups = _ssd_y(c, xa, state, dsk_ref[...])
        Z = (c["xdt"] * c["wE"]).astype(BF16)
        for g in range(SSM_G):
            gs = slice(g * 512, (g + 1) * 512)
            state[:, gs] = groups[g]["Sg"] * c["cdE"][:, gs] + _dot_tn(groups[g]["Bb"], Z[:, gs])
        zv = jnp.concatenate([z0[...], z1[...], z2[...], z3[...]], axis=1)
        yz = Y * (zv * _sigmoid(zv))
        outs = []
        for g in range(SSM_G):
            yg = yz[:, g * 512:(g + 1) * 512]
            outs.append(yg * lax.rsqrt(jnp.mean(yg * yg, axis=-1, keepdims=True) + EPS))
        o_ref[...] = (jnp.concatenate(outs, axis=1) * ng_ref[...]).astype(BF16)

    return pl.pallas_call(
        body, name=name, grid=(B, nc),
        in_specs=[xa_spec] + z_specs + [dt_spec, _full((1, 128)), _full((1, 128)), _full((1, SSM_W)), _full((1, SSM_W)),
                                        _full((128, SSM_W)), _full((128, 128)), _ANY],
        out_specs=[tok, st_spec],
        out_shape=[jax.ShapeDtypeStruct(cat.shape, BF16), jax.ShapeDtypeStruct((B, nc, 128, SSM_W), F32)],
        scratch_shapes=[pltpu.VMEM((128, SSM_W), F32)], input_output_aliases={12: 0},
        compiler_params=pltpu.CompilerParams(dimension_semantics=("arbitrary", "arbitrary")),
    )(xact, P, P, P, P, P, bias, alog, dskipE, ng, E, tri, cat)


def _ssd_bwd(xact, P, sprev, dcat, bias, alog, dskipE, ng, name):
    B, S, _ = P.shape
    nc, xa_spec, z_specs, dt_spec, tok, st_spec = _ssd_specs(S, True)
    do_spec = pl.BlockSpec((None, CHUNK, SSM_W), lambda b, i: (b, nc - 1 - i, 1))
    E, ET, tri, triT = _ssd_consts()
    dt_out = pl.BlockSpec((None, CHUNK, 128), lambda b, i: (b, nc - 1 - i, 0))

    def body(xa_ref, z0, z1, z2, z3, dt_ref, sp_ref, do_ref, bias_ref, alog_ref, dsk_ref, ng_ref, E_ref, ET_ref, tri_ref,
             triT_ref, dxa_ref, dz_ref, ddt_ref, dbias_ref, dalog_ref, ddsk_ref, dng_ref, dstate):
        first = (pl.program_id(0) == 0) & (pl.program_id(1) == 0)

        @pl.when(first)
        def _():
            for ref in (dbias_ref, dalog_ref, ddsk_ref, dng_ref):
                ref[...] = jnp.zeros(ref.shape, F32)

        @pl.when(pl.program_id(1) == 0)
        def _():
            dstate[...] = jnp.zeros((128, SSM_W), F32)

        xa, ETm = xa_ref[...], ET_ref[...]
        c = _ssd_pre(xa, dt_ref[...], bias_ref[...], alog_ref[...], E_ref[...], tri_ref[...])
        Y, groups = _ssd_y(c, xa, sp_ref, dsk_ref[...])
        X, xdt = c["X"], c["xdt"]
        zv = jnp.concatenate([z0[...], z1[...], z2[...], z3[...]], axis=1)
        sg = _sigmoid(zv)
        zs = zv * sg
        yz = Y * zs
        dout = do_ref[...]
        dyz = []
        for g in range(SSM_G):
            gs = slice(g * 512, (g + 1) * 512)
            yg = yz[:, gs]
            r = lax.rsqrt(jnp.mean(yg * yg, axis=-1, keepdims=True) + EPS)
            yn = yg * r
            dng_ref[:, gs] += jnp.sum(dout[:, gs] * yn, axis=0, keepdims=True)
            dyn = dout[:, gs] * ng_ref[:, gs]
            dyz.append(r * (dyn - yn * jnp.mean(dyn * yn, axis=-1, keepdims=True)))
        dyz = jnp.concatenate(dyz, axis=1)
        dz_ref[...] = (dyz * Y * (sg * (1.0 + zv * (1.0 - sg)))).astype(BF16)
        dY = dyz * zs
        ddsk_ref[...] += jnp.sum(dY * X, axis=0, keepdims=True)
        dX = dY * dsk_ref[...]
        lane = lax.broadcasted_iota(jnp.int32, (128, 128), 1)
        sub = lax.broadcasted_iota(jnp.int32, (128, 128), 0)
        colform = jnp.zeros((128, 128), F32)
        rowform = jnp.zeros((128, 128), F32)
        dxdt, gacsE, dBC = [], [], []
        for g in range(SSM_G):
            gs = slice(g * 512, (g + 1) * 512)
            G = groups[g]
            Bb, Cb, CB, Sg = G["Bb"], G["Cb"], G["CB"], G["Sg"]
            dYg = dY[:, gs]
            dQ = (dYg * c["eE"][:, gs]).astype(BF16)
            dSn = dstate[:, gs]
            dSnb = dSn.astype(BF16)
            cd = c["cdE"][:, gs]
            dC = _dot_nt(dQ, Sg.astype(BF16))
            dSprev = _dot_tn(Cb, dQ) + dSn * cd
            t1 = jnp.broadcast_to(jnp.sum(dSn * Sg * cd, axis=0, keepdims=True), (8, 512))
            colform = colform + jnp.where(sub == 127, _dot01(t1, ETm[gs, :], 2)[0:1, :], 0.0)
            Zg = xdt[:, gs] * c["wE"][:, gs]
            dZ = _dot(Bb, dSnb)
            dB = _dot_nt(Zg.astype(BF16), dSnb)
            U = dZ * Zg
            ga = dYg * G["yoff"] - U
            ga = ga + jnp.where(lax.broadcasted_iota(jnp.int32, (128, 512), 0) == 127, jnp.sum(U, axis=0, keepdims=True), 0.0)
            gacsE.append(ga)
            dxg = [None] * 4
            dCB = jnp.zeros((128, 128), F32)
            for j in range(4):
                h0 = g * 8 + 2 * j
                L0, L1, M, BD = G["pairs"][j]
                dYp = dYg[:, j * 128:(j + 1) * 128].astype(BF16)
                dM = _dot_nt(dYp, BD)
                dBD = _dot_tn(M, dYp)
                dxg[j] = jnp.where(lane < 64, dBD[:128], dBD[128:])
                for t, (h, L) in enumerate(((h0, L0), (h0 + 1, L1))):
                    dMh = dM[:, t * 128:(t + 1) * 128]
                    dCB = dCB + dMh * L
                    Gh = dMh * CB * L
                    colform = colform + jnp.where(lane == h, jnp.sum(Gh, axis=1, keepdims=True), 0.0)
                    rowform = rowform - jnp.where(sub == h, jnp.sum(Gh, axis=0, keepdims=True), 0.0)
            dCBb = dCB.astype(BF16)
            dC = dC + _dot(dCBb, Bb)
            dB = dB + _dot_tn(dCBb, Cb)
            dxdt.append(jnp.concatenate(dxg, axis=1) + dZ * c["wE"][:, gs])
            dBC.append((dB, dC))
            dstate[:, gs] = dSprev
        dxdt = jnp.concatenate(dxdt, axis=1)
        dX = dX + dxdt * c["dtE"]
        ddt = _dot01(dxdt * X, ETm, 2)
        dacs = colform + rowform.T + _dot01(jnp.concatenate(gacsE, axis=1), ETm, 2)
        dda = _dot01_left(triT_ref[...], dacs, 2)
        ddt = ddt + dda * c["a"]
        dalog_ref[...] += jnp.sum(dda * c["dtp"], axis=0, keepdims=True) * c["a"]
        ddtraw = jnp.where(lane < SSM_H, ddt * _sigmoid(c["pre"]), 0.0)
        dbias_ref[...] += jnp.sum(ddtraw, axis=0, keepdims=True)
        ddt_ref[...] = ddtraw.astype(BF16)
        dxa_ref[...] = jnp.concatenate([dX, dBC[0][0], dBC[1][0], dBC[0][1], dBC[1][1]], axis=1)

    p128, p1k = _full((1, 128)), _full((1, SSM_W))
    return pl.pallas_call(
        body, name=name, grid=(B, nc),
        in_specs=[xa_spec] + z_specs + [dt_spec, st_spec, do_spec, p128, p128, p1k, p1k,
                                        _full((128, SSM_W)), _full((SSM_W, 128)), _full((128, 128)), _full((128, 128))],
        out_specs=[xa_spec, tok, dt_out, p128, p128, p1k, p1k],
        out_shape=[jax.ShapeDtypeStruct((B, S, CONV_CH), F32), jax.ShapeDtypeStruct((B, S, SSM_W), BF16),
                   jax.ShapeDtypeStruct((B, S, 128), BF16), jax.ShapeDtypeStruct((1, 128), F32),
                   jax.ShapeDtypeStruct((1, 128), F32), jax.ShapeDtypeStruct((1, SSM_W), F32),
                   jax.ShapeDtypeStruct((1, SSM_W), F32)],
        scratch_shapes=[pltpu.VMEM((128, SSM_W), F32)],
        compiler_params=pltpu.CompilerParams(dimension_semantics=("arbitrary", "arbitrary")),
    )(xact, P, P, P, P, P, sprev, dcat, bias, alog, dskipE, ng, E, ET, tri, triT)


def _adamw(w, parts, m, v, name, tr=512, row0=0, prev=None):
    Rtot, C = w.shape
    ns, R = parts.shape[0], parts.shape[1]
    tr = min(tr, R)
    assert R % tr == 0 and row0 % tr == 0
    off = row0 // tr
    c1 = 1.0 / (1.0 - ADAM_B1 ** ADAM_STEP)
    c2 = 1.0 / (1.0 - ADAM_B2 ** ADAM_STEP)

    def body(w_ref, p_ref, m_ref, v_ref, *rest):
        g_ref, d_ref, mo_ref, vo_ref = rest[-4:]
        g = p_ref[0].astype(F32)
        for s in range(1, ns):
            g = g + p_ref[s].astype(F32)
        mn = ADAM_B1 * m_ref[...] + (1.0 - ADAM_B1) * g
        vn = ADAM_B2 * v_ref[...] + (1.0 - ADAM_B2) * (g * g)
        g_ref[...] = g
        mo_ref[...] = mn
        vo_ref[...] = vn
        d_ref[...] = -ADAM_LR * ((mn * c1) / (jnp.sqrt(vn * c2) + ADAM_EPS) + ADAM_WD * w_ref[...])

    blk = pl.BlockSpec((tr, C), lambda i: (i + off, 0))
    extra = [] if prev is None else list(prev)
    return pl.pallas_call(
        body, name=name, grid=(R // tr,),
        in_specs=[blk, pl.BlockSpec((ns, tr, C), lambda i: (0, i, 0)), blk, blk] + [pl.BlockSpec(memory_space=pl.ANY)] * len(extra),
        out_specs=[blk] * 4, out_shape=[jax.ShapeDtypeStruct((Rtot, C), F32)] * 4,
        input_output_aliases={4 + k: k for k in range(len(extra))})(w, parts, m, v, *extra)


_SMALL = ("ada_b", "norm1_g", "gm_ln_g", "gm_ln_b", "gm_ws", "gm_bs", "gm_norm_g", "attn_sinks", "attn_norm_g", "conv_b",
          "dt_bias", "a_log", "d_skip", "ssm_norm_g", "norm2_g", "final_norm_g")


def _pack(arrs):
    flat = []
    for a in arrs:
        f = a.reshape(-1).astype(F32)
        flat.append(jnp.pad(f, (0, (-f.shape[0]) % 1024)))
    return jnp.concatenate(flat).reshape(-1, 128)


def _unpack(pack, like):
    out, o = [], 0
    flat = pack.reshape(-1)
    for a in like:
        n = int(np.prod(a.shape))
        out.append(flat[o:o + n].reshape(a.shape))
        o += n + (-n) % 1024
    return out


def kernel(x, c, ada_w, ada_b, norm1_g, w_in, gm_ln_g, gm_ln_b, gm_ws, gm_bs, gm_norm_g, attn_sinks, attn_norm_g, conv_w, conv_b, dt_bias, a_log, d_skip, ssm_norm_g, w_out, norm2_g, w_mlp1, w_mlp2, final_norm_g, loss_target, m_ada_w, m_ada_b, m_norm1_g, m_w_in, m_gm_ln_g, m_gm_ln_b, m_gm_ws, m_gm_bs, m_gm_norm_g, m_attn_sinks, m_attn_norm_g, m_conv_w, m_conv_b, m_dt_bias, m_a_log, m_d_skip, m_ssm_norm_g, m_w_out, m_norm2_g, m_w_mlp1, m_w_mlp2, m_final_norm_g, v_ada_w, v_ada_b, v_norm1_g, v_w_in, v_gm_ln_g, v_gm_ln_b, v_gm_ws, v_gm_bs, v_gm_norm_g, v_attn_sinks, v_attn_norm_g, v_conv_w, v_conv_b, v_dt_bias, v_a_log, v_d_skip, v_ssm_norm_g, v_w_out, v_norm2_g, v_w_mlp1, v_w_mlp2, v_final_norm_g):
    args = dict(locals())
    B, S, _ = x.shape
    T = B * S
    L = DEPTH
    me = 4 * lax.axis_index("x") + 2 * lax.axis_index("y") + lax.axis_index("c")

    gath = _gather2([c, conv_w], "ag_c")
    big = ("w_in", "w_out", "w_mlp1", "w_mlp2")
    chain = [(n, l) for l in range(L) for n in ("w_in", "w_mlp1", "w_out", "w_mlp2")]
    inflight = {}

    def start_next(order):
        if not chain:
            return jnp.zeros((8, 128), F32)
        n, l = chain.pop(0)
        sems, land_thru, token = _gather_start(zone[n, l], order, f"ag_start_{n}{l}")
        inflight[n, l] = (sems, land_thru)
        return token

    def gathered(n, l, after):
        land = _gather_wait(*inflight.pop((n, l)), after, f"ag_wait_{n}{l}")
        return _gather_finish(land, f"ag_fin_{n}{l}")

    me1 = me.astype(jnp.int32).reshape(1)
    zone = {(n, l): _landing_zone(args[n], l, me1, f"ag_zone_{n}{l}") for n, l in chain}
    later_zones = [zone[k] for k in chain[1:]]

    tok = start_next(gath[0])
    c_all = gath[0].reshape(NDEV * B, D) + tok[0, 0]
    c_act = (c_all * jax.nn.sigmoid(c_all)).astype(BF16)
    nb_rows = c_act.shape[0]
    c_pad = jnp.pad(c_act, ((0, 128 - nb_rows), (0, 0)))
    adw = ada_w.astype(BF16)
    mod_part = jnp.stack([_mm(c_pad, adw[l], mode="nn", name=f"mod{l}", tn=768)[:nb_rows] for l in range(L)])
    mod_all = _gather_small([mod_part], "ag_mod", order=later_zones)[0]
    mod_mine = lax.dynamic_slice_in_dim(mod_all, me * B, B, axis=2)
    mod = jnp.transpose(mod_mine, (1, 2, 0, 3)).reshape(L, B, 6 * D) + ada_b[:, None, :]
    mods = [[mod[l][:, None, i * D:(i + 1) * D] for i in range(6)] for l in range(L)]

    win_g, wout_g, w1_g, w2_g = [None] * L, [None] * L, [None] * L, [None] * L

    tril = jnp.tril(jnp.ones((128, 128), F32))
    row = lambda a: a.reshape(1, -1)
    pad128 = lambda a: jnp.pad(a.reshape(1, -1), ((0, 0), (0, 128 - a.shape[-1])))
    small = []
    for l in range(L):
        wt = gm_ws[l] * tril
        small.append(dict(
            lng=row(gm_ln_g[l]), lnb=row(gm_ln_b[l]), wt=wt.astype(BF16), wtT=jnp.swapaxes(wt, 1, 2).astype(BF16),
            bsx=jnp.repeat(gm_bs[l].T, 128, axis=1), gog=row(gm_norm_g[l]), sinks=attn_sinks[l], aog=row(attn_norm_g[l]),
            bias=pad128(dt_bias[l]), alog=pad128(a_log[l]), dskE=jnp.repeat(d_skip[l], SSM_HD).reshape(1, SSM_W),
            sng=row(ssm_norm_g[l]), cb=row(conv_b[l])))
    convw_all = jnp.transpose(gath[1], (1, 2, 0, 3)).reshape(L, 4, CONV_CH)
    convw8 = jnp.pad(convw_all, ((0, 0), (0, 4), (0, 0)))

    saved = []
    xl = x
    h = _norm_fwd(xl, row(norm1_g[0]), mods[0][1], mods[0][0], "norm1_f0")
    for l in range(L):
        sm = small[l]
        g_in = gathered("w_in", l, h)
        tok = start_next(g_in)
        win_g[l] = _shards_to_cols(g_in, f"w_in_cols{l}")
        P = _mm(h.reshape(T, D), win_g[l], mode="nn", name=f"proj_in{l}", tn=1536, order=tok).reshape(B, S, PW)
        cat = _gmlp_fwd(P, sm["lng"], sm["lnb"], sm["wt"], sm["bsx"], sm["gog"], f"gmlp_f{l}")
        cat = _attn_fwd(P, sm["sinks"], sm["aog"], cat, f"attn_f{l}")
        xact = _conv_fwd(P, convw8[l], sm["cb"], f"conv_f{l}")
        w1_g[l] = gathered("w_mlp1", l, xact)
        tok = start_next(w1_g[l])
        cat, sprev = _ssd_fwd(xact, P, sm["bias"], sm["alog"], sm["dskE"], sm["sng"] + tok[0:1, 0:1], cat, f"ssd_f{l}")
        g_out = gathered("w_out", l, cat)
        tok = start_next(g_out)
        wout_g[l] = g_out.reshape(D, D)
        mix = _mm(cat.reshape(T, D), wout_g[l], mode="nn", name=f"proj_out{l}", order=tok).reshape(B, S, D)
        x_mid, h2 = _norm_fwd(xl, row(norm2_g[l]), mods[l][4], mods[l][3], f"norm2_f{l}", resid=(mix, mods[l][2]))
        a_act, r_act = _mm(h2.reshape(T, D), w1_g[l], mode="nn", name=f"mlp1_{l}", out_dtypes=(BF16, BF16), col_blocked_b=True,
                           epilogue=lambda acc: (acc, jnp.square(jnp.maximum(acc, 0.0))))
        g_2 = gathered("w_mlp2", l, r_act)
        tok = start_next(g_2)
        w2_g[l] = g_2.reshape(DFF, D)
        m2 = _mm(r_act, w2_g[l], mode="nn", name=f"mlp2_{l}", order=tok, tk=4096).reshape(B, S, D)
        saved.append(dict(x_in=xl, h=h, P=P, xact=xact, sprev=sprev, cat=cat, mix=mix, x_mid=x_mid, h2=h2, a=a_act, r=r_act, m2=m2))
        if l + 1 < L:
            xl, h = _norm_fwd(x_mid, row(norm1_g[l + 1]), mods[l + 1][1], mods[l + 1][0], f"norm1_f{l + 1}", resid=(m2, mods[l][5]))

    sv = saved[L - 1]
    nb = _norm_bwd(sv["x_mid"], row(final_norm_g), "final_b", tgt=loss_target, br=sv["m2"], gate=mods[L - 1][5], x_is_prev=True)
    loss_part, g_final = nb["loss"], nb["dg"]
    dmod, gsm, gconvw = [None] * L, [None] * L, [None] * L
    core = lax.axis_index("c").astype(jnp.int32).reshape(1)
    reducing = []

    def reduce_start(n, l, sent, after):
        p, from_sib = _pair_wait(*sent[:3], after, f"rs_pair_wait_{n}{l}")
        s, land = _pair_add(p, from_sib, core, f"rs_add_{n}{l}")
        return reduce_exchange(n, l, s, land, after)

    def reduce_exchange(n, l, s, land, order):
        sems, s_thru, land_thru, token = _chipsum_start(s, land, order, f"rs_start_{n}{l}")
        reducing.append((n, l, sems, s_thru, land_thru))
        return token

    other = 1 - core

    for l in reversed(range(L)):
        sv, sm = saved[l], small[l]
        dm2, dxo, dg2 = nb["dbr"].reshape(T, D), nb["dx"], nb["dgate"]
        da = _mm(dm2, w2_g[l], mode="nt", name=f"mlp2_dx{l}", out_dtypes=(BF16,), extras=(sv["a"],),
                 epilogue=lambda acc, a: (acc * (2.0 * jnp.maximum(a.astype(F32), 0.0)),))
        h2f = sv["h2"].reshape(T, D)
        sent2 = _sibling_start(_dw_half(sv["r"], dm2, other, axis="m", name=f"mlp2_dw_sib{l}"), da, f"rs_sib_start_w_mlp2{l}")
        dh2 = _mm(da, w1_g[l], mode="nt", name=f"mlp1_dx{l}", col_blocked_b=True, order=sent2[3]).reshape(B, S, D)
        from_sib = _sibling_wait(*sent2[:3], dh2, f"rs_sib_wait_w_mlp2{l}")
        sent1 = _sibling_start(_dw_half(h2f, da, other, axis="n", name=f"mlp1_dw_sib{l}", order=from_sib), da,
                               f"rs_sib_start_w_mlp1{l}")
        s2, land2 = _dw_half(sv["r"], dm2, core, axis="m", name=f"mlp2_dw_own{l}", add=from_sib, order=sent1[3])
        tok = reduce_exchange("w_mlp2", l, s2, land2, da)
        nb2 = _norm_bwd(sv["x_mid"], row(norm2_g[l]) + tok[0, 0], f"norm2_b{l}", sc=mods[l][4], dh=dh2, dres=dxo, br=sv["mix"],
                        gate=mods[l][2])
        dmix = nb2["dbr"].reshape(T, D)
        from_sib = _sibling_wait(*sent1[:3], dmix, f"rs_sib_wait_w_mlp1{l}")
        s1, land1 = _dw_half(h2f, da, core, axis="n", name=f"mlp1_dw_own{l}", add=from_sib)
        tok = reduce_exchange("w_mlp1", l, s1, land1, dmix)
        dcat = _mm(dmix, wout_g[l], mode="nt", name=f"proj_out_dx{l}", order=tok).reshape(B, S, D)
        du, dv, dlng, dlnb, dws, dbsx, dgog = _gmlp_bwd(sv["P"], dcat, sm["lng"], sm["lnb"], sm["wt"], sm["wtT"], sm["bsx"],
                                                        sm["gog"], f"gmlp_b{l}")
        dq, dk, dvv, dsink, daog = _attn_bwd(sv["P"], dcat, sm["sinks"], sm["aog"], f"attn_b{l}")
        dwo = _mm(sv["cat"].reshape(T, D), dmix, mode="tn", name=f"proj_out_dw{l}", out_dtypes=(BF16,), tk=2048,
                  order=dq).reshape(4, 2, D // NDEV, D)
        sent = _pair_start(dwo, dmix, f"rs_pair_start_w_out{l}")
        dxa, dz, ddt, dbias, dalog, ddsk, dsng = _ssd_bwd(sv["xact"], sv["P"], sv["sprev"], dcat, sm["bias"], sm["alog"],
                                                          sm["dskE"], sm["sng"] + sent[3][0:1, 0:1], f"ssd_b{l}")
        tok = reduce_start("w_out", l, sent, dxa)
        dxbc, dcw, dcb = _conv_bwd(sv["P"], dxa, convw8[l], sm["cb"] + tok[0:1, 0:1], f"conv_b{l}")
        dP = jnp.concatenate([du, dv, dq, dk, dvv, dz, dxbc, ddt, jnp.zeros((B, S, PW - OFF["dt"] - 128), BF16)],
                             axis=-1).reshape(T, PW)
        dwin = _mm(sv["h"].reshape(T, D), dP, mode="tn", name=f"proj_in_dw{l}", out_dtypes=(BF16,), tn=1536, tk=2048)
        dwin = _cols_to_shards(dwin, f"w_in_dshards{l}").reshape(4, 2, D, IN_W // NDEV)
        sent = _pair_start(dwin, dP, f"rs_pair_start_w_in{l}")
        dh = _mm(dP, win_g[l], mode="nt", name=f"proj_in_dx{l}", tk=2304, order=sent[3]).reshape(B, S, D)
        tok = reduce_start("w_in", l, sent, dh)
        nb = _norm_bwd(sv["x_in"], row(norm1_g[l]) + tok[0, 0], f"norm1_b{l}", sc=mods[l][1], dh=dh, dres=nb2["dx"],
                       br=saved[l - 1]["m2"] if l > 0 else None, gate=mods[l - 1][5] if l > 0 else None)
        dmod[l] = jnp.concatenate([nb["dsh"], nb["dsc"], nb2["dgate"], nb2["dsh"], nb2["dsc"], dg2], axis=-1)
        gconvw[l] = dcw[:4]
        gsm[l] = dict(
            ada_b=jnp.sum(dmod[l], axis=(0, 1)), norm1_g=nb["dg"], gm_ln_g=dlng, gm_ln_b=dlnb, gm_ws=dws,
            gm_bs=dbsx.reshape(128, GM_H, 128).sum(-1).T, gm_norm_g=dgog, attn_sinks=dsink[:, 0], attn_norm_g=daog,
            conv_b=dcb, dt_bias=dbias[0, :SSM_H], a_log=dalog[0, :SSM_H], d_skip=ddsk.reshape(SSM_H, SSM_HD).sum(-1),
            ssm_norm_g=dsng, norm2_g=nb2["dg"])
    grad_x = nb["dx"]

    big_res, after = dict.fromkeys(big), grad_x
    tile_rows = dict(w_in=256, w_out=256, w_mlp1=256, w_mlp2=128)

    def finish_reduce(n, l, sems, s_thru, land_thru, after):
        parts = _chipsum_wait(sems, s_thru, land_thru, after, f"rs_wait_{n}{l}")
        w = args[n]
        big_res[n] = _adamw(w.reshape(-1, w.shape[-1]), parts, args["m_" + n].reshape(-1, w.shape[-1]),
                            args["v_" + n].reshape(-1, w.shape[-1]), f"adamw_{n}{l}", tr=tile_rows[n], row0=l * w.shape[1],
                            prev=big_res[n])
        return big_res[n][0]

    for item in reducing[:-1]:
        after = finish_reduce(*item, after)

    per_layer = [n for n in _SMALL if n != "final_norm_g"]
    g_small = [jnp.stack([gsm[l][n].reshape(args[n].shape[1:]) for l in range(L)]) for n in per_layer] + [g_final.reshape(D)]
    zc = jnp.zeros((L, 4, CONV_CH), F32)
    z1 = jnp.zeros((1, 128), F32)
    gpack = _pack([loss_part] + g_small + [jnp.stack(gconvw)])
    got = _gather2([jnp.stack(dmod).reshape(L, B, 6 * D), gpack], "ag_small", order=after)
    like = [z1] + [args[n] for n in _SMALL] + [zc]
    packs = [_pack([z1] + [args[p + n] for n in _SMALL] + [zc]) for p in ("", "m_", "v_")]
    sres = [_unpack(p, like) for p in _adamw(packs[0], got[1], packs[1], packs[2], "adamw_small", tr=gpack.shape[0])]
    res = {n: [r[1 + i] for r in sres] for i, n in enumerate(_SMALL)}
    loss = sres[0][0][0, 0]
    gcw = lax.dynamic_slice_in_dim(sres[0][-1], me * (CONV_CH // NDEV), CONV_CH // NDEV, axis=2)

    def update(name, parts, tr):
        w = args[name]
        r = _adamw(w.reshape(-1, w.shape[-1]), parts, args["m_" + name].reshape(-1, w.shape[-1]),
                   args["v_" + name].reshape(-1, w.shape[-1]), "adamw_" + name, tr=tr)
        res[name] = [a.reshape(w.shape) for a in r]

    update("conv_w", gcw.reshape(1, L * 4, CONV_CH // NDEV), L * 4)

    dmod_all = jnp.transpose(got[0], (1, 0, 2, 3)).reshape(L, NDEV * B, 6 * D)
    dm_mine = lax.dynamic_slice_in_dim(dmod_all, me * (6 * D // NDEV), 6 * D // NDEV, axis=2)
    dm_pad = jnp.pad(dm_mine, ((0, 0), (0, 128 - nb_rows), (0, 0))).astype(BF16)
    g_adaw = jnp.stack([_mm(c_pad, dm_pad[l], mode="tn", name=f"ada_dw{l}", tn=768) for l in range(L)])
    update("ada_w", g_adaw.reshape(1, L * D, 6 * D // NDEV), 256)

    finish_reduce(*reducing[-1], res["ada_w"][0])
    for n in big:
        res[n] = [a.reshape(args[n].shape) for a in big_res[n]]

    names = ['ada_w', 'ada_b', 'norm1_g', 'w_in', 'gm_ln_g', 'gm_ln_b', 'gm_ws', 'gm_bs', 'gm_norm_g', 'attn_sinks',
             'attn_norm_g', 'conv_w', 'conv_b', 'dt_bias', 'a_log', 'd_skip', 'ssm_norm_g', 'w_out', 'norm2_g', 'w_mlp1',
             'w_mlp2', 'final_norm_g']
    return (loss, grad_x, *[res[n][0] for n in names], *[res[n][1] for n in names], *[res[n][2] for n in names],
            *[res[n][3] for n in names])
```

```python
import functools

import jax
import jax.numpy as jnp
import numpy as np
from jax import lax
from jax.experimental import pallas as pl
from jax.experimental.pallas import tpu as pltpu

F32, BF16 = jnp.float32, jnp.bfloat16
HI = lax.Precision.HIGHEST
MESH = pl.DeviceIdType.MESH
NDEV = 8

D = 2048
DEPTH = 2
CHUNK = 128
GM_W, GM_H = 512, 4
ATT_W, KV_W, ATT_H = 512, 128, 8
SSM_W, SSM_H, SSM_HD, SSM_G = 1024, 16, 64, 2
CONV_CH = 1536
IN_W = 4368
DFF = 8192
EPS = 1e-6
NEG_INF = -1e30
GELU_K = 0.7978845608028654
GELU_C = 0.044715

_ORIG = (("u", 512), ("v", 512), ("q", 512), ("k", 128), ("vv", 128), ("z", 1024), ("xbc", 1536), ("dt", 16))
OFF = dict(u=0, v=512, q=1024, k=1536, vv=1664, z=1792, xbc=2816, dt=4352)
PW = 4608

ADAM_LR, ADAM_B1, ADAM_B2, ADAM_EPS, ADAM_WD, ADAM_STEP = 0.001, 0.9, 0.999, 1e-08, 0.01, 10


def _shards_to_cols(g, name, tr=256):
    n, R, C = g.shape

    def body(g_ref, o_ref):
        o_ref[...] = jnp.concatenate([g_ref[s] for s in range(n)] + [jnp.zeros((tr, PW - n * C), g.dtype)], axis=1)

    return pl.pallas_call(body, name=name, grid=(R // tr,), in_specs=[pl.BlockSpec((n, tr, C), lambda i: (0, i, 0))],
                          out_specs=pl.BlockSpec((tr, PW), lambda i: (i, 0)), out_shape=jax.ShapeDtypeStruct((R, PW), g.dtype))(g)


def _cols_to_shards(w, name, tr=256):
    R, C = w.shape[0], IN_W // NDEV

    def body(w_ref, o_ref):
        x = w_ref[...]
        for s in range(NDEV):
            o_ref[s] = x[:, C * s:C * (s + 1)]

    return pl.pallas_call(body, name=name, grid=(R // tr,), in_specs=[pl.BlockSpec((tr, PW), lambda i: (i, 0))],
                          out_specs=pl.BlockSpec((NDEV, tr, C), lambda i: (0, i, 0)),
                          out_shape=jax.ShapeDtypeStruct((NDEV, R, C), w.dtype))(w)


def _sigmoid(x):
    return 1.0 / (1.0 + jnp.exp(-x))


def _gelu(x):
    return 0.5 * x * (1.0 + jnp.tanh(GELU_K * (x + GELU_C * x * x * x)))


def _gelu_grad(x):
    t = jnp.tanh(GELU_K * (x + GELU_C * x * x * x))
    return 0.5 * (1.0 + t) + 0.5 * x * (1.0 - t * t) * GELU_K * (1.0 + 3.0 * GELU_C * x * x)


def _dot(a, b, prec=None):
    return jnp.dot(a, b, precision=prec, preferred_element_type=F32)


def _dot_nt(a, b, prec=None):
    return lax.dot_general(a, b, (((1,), (1,)), ((), ())), precision=prec, preferred_element_type=F32)


def _dot_tn(a, b, prec=None):
    return lax.dot_general(a, b, (((0,), (0,)), ((), ())), precision=prec, preferred_element_type=F32)


def _full(shape):
    return pl.BlockSpec(shape, lambda *_: (0,) * len(shape))


_HBM = pl.BlockSpec(memory_space=pltpu.HBM)


def _me():
    return lax.axis_index("x"), lax.axis_index("y"), lax.axis_index("c")


def _peer(k):
    x, y, c = _me()
    px = 1 - x if k & 4 else x
    py = 1 - y if k & 2 else y
    pc = 1 - c if k & 1 else c
    return (px, py, pc), 4 * px + 2 * py + pc


def _gather_small(xs, name, order=()):
    n = len(xs)

    def body(*refs):
        ins, outs = refs[:n], refs[-n - 3:-3]
        send, recv, loc = refs[-3:]
        x, y, c = _me()
        me = 4 * x + 2 * y + c
        started = []
        for i in range(n):
            own = pltpu.make_async_copy(ins[i], outs[i].at[me], loc.at[i])
            own.start()
            started.append(own)
        for k in range(1, NDEV):
            dev, lin = _peer(k)
            for i in range(n):
                pltpu.make_async_remote_copy(
                    src_ref=ins[i], dst_ref=outs[i].at[me],
                    send_sem=send.at[i, k - 1], recv_sem=recv.at[i, k - 1], device_id=dev, device_id_type=MESH).start()
        for k in range(1, NDEV):
            dev, lin = _peer(k)
            for i in range(n):
                pltpu.make_async_remote_copy(
                    src_ref=ins[i], dst_ref=outs[i].at[lin],
                    send_sem=send.at[i, k - 1], recv_sem=recv.at[i, k - 1], device_id=dev, device_id_type=MESH).wait()
        for own in started:
            own.wait()

    extra = list(order)
    return pl.pallas_call(
        body, name=name, out_shape=[jax.ShapeDtypeStruct((NDEV,) + a.shape, a.dtype) for a in xs],
        in_specs=[_HBM] * n + [pl.BlockSpec(memory_space=pl.ANY)] * len(extra), out_specs=[_HBM] * n,
        scratch_shapes=[pltpu.SemaphoreType.DMA((n, NDEV - 1)), pltpu.SemaphoreType.DMA((n, NDEV - 1)),
                        pltpu.SemaphoreType.DMA((n,))],
        compiler_params=pltpu.CompilerParams(has_side_effects=True),
    )(*xs, *extra)


def _chips():
    x, y, c = _me()
    return x, y, c, [(1 - x, y), (x, 1 - y), (1 - x, 1 - y)]


def _gather2(xs, name, order=None):
    n = len(xs)
    extra = [] if order is None else [order]

    def body(*refs):
        ins, outs = refs[:n], refs[-n - 3:-3]
        send, recv, loc = refs[-3:]
        x, y, c, chips = _chips()
        me, sib = (x, y, c), (x, y, 1 - c)

        def cp(i, k, block, to, src=None):
            slot = outs[i].at[4 * block[0] + 2 * block[1] + block[2]]
            return pltpu.make_async_remote_copy(src_ref=slot if src is None else src, dst_ref=slot, send_sem=send.at[i, k],
                                                recv_sem=recv.at[i, k], device_id=to, device_id_type=MESH)

        sent = []
        for i in range(n):
            for j, chip in enumerate(chips):
                sent.append(cp(i, 1 + j, me, (*chip, c), src=ins[i]))
            sent.append(cp(i, 0, me, sib, src=ins[i]))
        for s in sent:
            s.start()
        own = [pltpu.make_async_copy(ins[i], outs[i].at[4 * x + 2 * y + c], loc.at[i]) for i in range(n)]
        for o in own:
            o.start()
        for j, chip in enumerate(chips):
            for i in range(n):
                cp(i, 1 + j, (*chip, c), me).wait_recv()
                fwd = cp(i, 4 + j, (*chip, c), sib)
                fwd.start()
                sent.append(fwd)
        for i in range(n):
            cp(i, 0, sib, me).wait_recv()
            for j, chip in enumerate(chips):
                cp(i, 4 + j, (*chip, 1 - c), me).wait_recv()
        for s in sent:
            s.wait_send()
        for o in own:
            o.wait()

    return pl.pallas_call(
        body, name=name, out_shape=[jax.ShapeDtypeStruct((NDEV,) + a.shape, a.dtype) for a in xs],
        in_specs=[_HBM] * n + [pl.BlockSpec(memory_space=pl.ANY)] * len(extra), out_specs=[_HBM] * n,
        scratch_shapes=[pltpu.SemaphoreType.DMA((n, 7)), pltpu.SemaphoreType.DMA((n, 7)), pltpu.SemaphoreType.DMA((n,))],
        compiler_params=pltpu.CompilerParams(has_side_effects=True),
    )(*xs, *extra)


def _pair_add(p, r1, core, name, tr=256):
    _, _, R, C = p.shape
    tr = min(tr, R)

    def body(core_ref, p_ref, r_ref, o_ref, o2_ref):
        s = (p_ref[...].astype(F32) + r_ref[...].astype(F32)).astype(o_ref.dtype)
        o_ref[...] = s
        o2_ref[...] = s

    blk = pl.BlockSpec((None, tr, C), lambda ch, i, core_ref: (ch, i, 0))
    return pl.pallas_call(
        body, name=name, out_shape=[jax.ShapeDtypeStruct((4, R, C), p.dtype)] * 2,
        grid_spec=pltpu.PrefetchScalarGridSpec(
            num_scalar_prefetch=1, grid=(4, R // tr),
            in_specs=[pl.BlockSpec((None, None, tr, C), lambda ch, i, core_ref: (ch, core_ref[0], i, 0)), blk],
            out_specs=[blk, blk]),
    )(core, p, r1)


_SEM = pl.BlockSpec(memory_space=pltpu.SEMAPHORE)
_ANY = pl.BlockSpec(memory_space=pl.ANY)
_DATAFLOW = pltpu.SideEffectType.DATAFLOW_SIDE_EFFECTING


def _hbm(a):
    return pltpu.with_memory_space_constraint(a, pltpu.HBM)


def _gather_targets():
    x, y, c, chips = _chips()
    return 4 * x + 2 * y + c, [(x, y, 1 - c)] + [(*chip, c) for chip in chips]


def _landing_zone(w, l, me, name, tr=512):
    _, R, C = w.shape
    tr = min(tr, R)

    def body(me_ref, w_ref, o_ref):
        o_ref[...] = w_ref[...].astype(BF16)

    return pl.pallas_call(
        body, name=name, out_shape=jax.ShapeDtypeStruct((NDEV, R, C), BF16),
        grid_spec=pltpu.PrefetchScalarGridSpec(
            num_scalar_prefetch=1, grid=(R // tr,), in_specs=[pl.BlockSpec((None, tr, C), lambda i, me_ref: (l, i, 0))],
            out_specs=pl.BlockSpec((None, tr, C), lambda i, me_ref: (me_ref[0], i, 0))),
    )(me, w)


def _gather_start(land, order, name):
    def body(land_ref, order_ref, *rest):
        sems, token = rest[:8], rest[9]
        me, targets = _gather_targets()
        for k, to in enumerate(targets):
            pltpu.make_async_remote_copy(src_ref=land_ref.at[me], dst_ref=land_ref.at[me], send_sem=sems[k],
                                         recv_sem=sems[4 + k], device_id=to, device_id_type=MESH).start()
        token[...] = jnp.zeros_like(token)

    outs = pl.pallas_call(
        body, name=name,
        out_shape=(pltpu.SemaphoreType.DMA(()),) * 8 + (pltpu.HBM(land.shape, land.dtype), jax.ShapeDtypeStruct((8, 128), F32)),
        in_specs=(_HBM, _ANY), out_specs=(_SEM,) * 8 + (_HBM, pl.BlockSpec(memory_space=pltpu.VMEM)),
        input_output_aliases={0: 8}, compiler_params=pltpu.CompilerParams(has_side_effects=_DATAFLOW),
    )(_hbm(land), order)
    return outs[:8], outs[8], outs[9]


def _gather_wait(sems, land_thru, after, name):
    def body(land_ref, *rest):
        sems_ = rest[:8]
        me, targets = _gather_targets()
        for k, to in enumerate(targets):
            cp = pltpu.make_async_remote_copy(src_ref=land_ref.at[me], dst_ref=land_ref.at[me], send_sem=sems_[k],
                                              recv_sem=sems_[4 + k], device_id=to, device_id_type=MESH)
            cp.wait_send()
            cp.wait_recv()

    return pl.pallas_call(
        body, name=name, out_shape=pltpu.HBM(land_thru.shape, land_thru.dtype),
        in_specs=(_HBM,) + (_SEM,) * 8 + (_ANY,), out_specs=_HBM, input_output_aliases={0: 0},
        compiler_params=pltpu.CompilerParams(has_side_effects=_DATAFLOW),
    )(land_thru, *sems, after)


def _gather_finish(land, name):
    def body(land_ref, out, send, recv):
        x, y, c, chips = _chips()
        fwd = [pltpu.make_async_remote_copy(src_ref=out.at[4 * px + 2 * py + c], dst_ref=out.at[4 * px + 2 * py + c],
                                            send_sem=send.at[j], recv_sem=recv.at[j], device_id=(x, y, 1 - c), device_id_type=MESH)
               for j, (px, py) in enumerate(chips)]
        for cp in fwd:
            cp.start()
        for j, (px, py) in enumerate(chips):
            slot = out.at[4 * px + 2 * py + 1 - c]
            pltpu.make_async_remote_copy(src_ref=slot, dst_ref=slot, send_sem=send.at[j], recv_sem=recv.at[j],
                                         device_id=(x, y, 1 - c), device_id_type=MESH).wait()

    return pl.pallas_call(
        body, name=name, out_shape=jax.ShapeDtypeStruct(land.shape, land.dtype),
        in_specs=[_HBM], out_specs=_HBM, input_output_aliases={0: 0},
        scratch_shapes=[pltpu.SemaphoreType.DMA((3,)), pltpu.SemaphoreType.DMA((3,))],
        compiler_params=pltpu.CompilerParams(has_side_effects=True),
    )(land)


def _chip_targets():
    x, y, c, chips = _chips()
    return 2 * x + y, [((px, py, c), 2 * px + py) for px, py in chips]


def _chipsum_start(s, land, order, name):
    def body(s_ref, land_ref, order_ref, *rest):
        sems, token = rest[:6], rest[8]
        mine, targets = _chip_targets()
        for k, (to, ch) in enumerate(targets):
            pltpu.make_async_remote_copy(src_ref=s_ref.at[ch], dst_ref=land_ref.at[mine], send_sem=sems[k], recv_sem=sems[3 + k],
                                         device_id=to, device_id_type=MESH).start()
        token[...] = jnp.zeros_like(token)

    outs = pl.pallas_call(
        body, name=name,
        out_shape=(pltpu.SemaphoreType.DMA(()),) * 6 + (pltpu.HBM(s.shape, s.dtype), pltpu.HBM(land.shape, land.dtype),
                                                        jax.ShapeDtypeStruct((8, 128), F32)),
        in_specs=(_HBM, _HBM, _ANY), out_specs=(_SEM,) * 6 + (_HBM, _HBM, pl.BlockSpec(memory_space=pltpu.VMEM)),
        input_output_aliases={0: 6, 1: 7}, compiler_params=pltpu.CompilerParams(has_side_effects=_DATAFLOW),
    )(_hbm(s), _hbm(land), order)
    return outs[:6], outs[6], outs[7], outs[8]


def _chipsum_wait(sems, s_thru, land_thru, after, name):
    def body(s_ref, land_ref, *rest):
        sems_ = rest[:6]
        mine, targets = _chip_targets()
        for k, (to, ch) in enumerate(targets):
            cp = pltpu.make_async_remote_copy(src_ref=s_ref.at[ch], dst_ref=land_ref.at[ch], send_sem=sems_[k], recv_sem=sems_[3 + k],
                                              device_id=to, device_id_type=MESH)
            cp.wait_send()
            cp.wait_recv()

    return pl.pallas_call(
        body, name=name, out_shape=(pltpu.HBM(s_thru.shape, s_thru.dtype), pltpu.HBM(land_thru.shape, land_thru.dtype)),
        in_specs=(_HBM, _HBM) + (_SEM,) * 6 + (_ANY,), out_specs=(_HBM, _HBM), input_output_aliases={0: 0, 1: 1},
        compiler_params=pltpu.CompilerParams(has_side_effects=_DATAFLOW),
    )(s_thru, land_thru, *sems, after)[1]


def _pair_start(p, order, name):
    def body(p_ref, land_ref, order_ref, *rest):
        sems, token = rest[:8], rest[10]
        x, y, c = _me()
        for ch in range(4):
            pltpu.make_async_remote_copy(src_ref=p_ref.at[ch, 1 - c], dst_ref=land_ref.at[ch], send_sem=sems[ch],
                                         recv_sem=sems[4 + ch], device_id=(x, y, 1 - c), device_id_type=MESH).start()
        token[...] = jnp.zeros_like(token)

    land = lax.empty((4,) + p.shape[2:], p.dtype)
    outs = pl.pallas_call(
        body, name=name,
        out_shape=(pltpu.SemaphoreType.DMA(()),) * 8 + (pltpu.HBM(p.shape, p.dtype), pltpu.HBM(land.shape, land.dtype),
                                                        jax.ShapeDtypeStruct((8, 128), F32)),
        in_specs=(_HBM, _HBM, _ANY), out_specs=(_SEM,) * 8 + (_HBM, _HBM, pl.BlockSpec(memory_space=pltpu.VMEM)),
        input_output_aliases={0: 8, 1: 9}, compiler_params=pltpu.CompilerParams(has_side_effects=_DATAFLOW),
    )(_hbm(p), _hbm(land), order)
    return outs[:8], outs[8], outs[9], outs[10]


def _pair_wait(sems, p_thru, land_thru, after, name):
    def body(p_ref, land_ref, *rest):
        sems_ = rest[:8]
        x, y, c = _me()
        for ch in range(4):
            cp = pltpu.make_async_remote_copy(src_ref=p_ref.at[ch, 1 - c], dst_ref=land_ref.at[ch], send_sem=sems_[ch],
                                              recv_sem=sems_[4 + ch], device_id=(x, y, 1 - c), device_id_type=MESH)
            cp.wait_send()
            cp.wait_recv()

    return pl.pallas_call(
        body, name=name, out_shape=(pltpu.HBM(p_thru.shape, p_thru.dtype), pltpu.HBM(land_thru.shape, land_thru.dtype)),
        in_specs=(_HBM, _HBM) + (_SEM,) * 8 + (_ANY,), out_specs=(_HBM, _HBM), input_output_aliases={0: 0, 1: 1},
        compiler_params=pltpu.CompilerParams(has_side_effects=_DATAFLOW),
    )(p_thru, land_thru, *sems, after)


def _sibling_start(p, order, name):
    def body(p_ref, land_ref, order_ref, send_sem, recv_sem, p_thru, land_thru, token):
        x, y, c = _me()
        pltpu.make_async_remote_copy(src_ref=p_ref, dst_ref=land_ref, send_sem=send_sem, recv_sem=recv_sem,
                                     device_id=(x, y, 1 - c), device_id_type=MESH).start()
        token[...] = jnp.zeros_like(token)

    land = lax.empty(p.shape, p.dtype)
    outs = pl.pallas_call(
        body, name=name,
        out_shape=(pltpu.SemaphoreType.DMA(()),) * 2 + (pltpu.HBM(p.shape, p.dtype), pltpu.HBM(p.shape, p.dtype),
                                                        jax.ShapeDtypeStruct((8, 128), F32)),
        in_specs=(_HBM, _HBM, _ANY), out_specs=(_SEM,) * 2 + (_HBM, _HBM, pl.BlockSpec(memory_space=pltpu.VMEM)),
        input_output_aliases={0: 2, 1: 3}, compiler_params=pltpu.CompilerParams(has_side_effects=_DATAFLOW),
    )(_hbm(p), _hbm(land), order)
    return outs[:2], outs[2], outs[3], outs[4]


def _sibling_wait(sems, p_thru, land_thru, after, name):
    def body(p_ref, land_ref, send_sem, recv_sem, after_ref, p_dead, got_ref):
        x, y, c = _me()
        cp = pltpu.make_async_remote_copy(src_ref=p_ref, dst_ref=land_ref, send_sem=send_sem, recv_sem=recv_sem,
                                          device_id=(x, y, 1 - c), device_id_type=MESH)
        cp.wait_send()
        cp.wait_recv()

    return pl.pallas_call(
        body, name=name, out_shape=(pltpu.HBM(p_thru.shape, p_thru.dtype), pltpu.HBM(land_thru.shape, land_thru.dtype)),
        in_specs=(_HBM, _HBM, _SEM, _SEM, _ANY), out_specs=(_HBM, _HBM), input_output_aliases={0: 0, 1: 1},
        compiler_params=pltpu.CompilerParams(has_side_effects=_DATAFLOW),
    )(p_thru, land_thru, *sems, after)[1]


def _mm(a, b, *, mode, name, out_dtypes=(F32,), epilogue=None, extras=(), tm=1024, tn=1024, tk=2048,
        col_blocked_b=False, col_blocked_out=False, order=None):
    CB = 1024
    if col_blocked_b:
        assert mode in ("nn", "nt") and b.shape[2] == CB
        (M, K), N = a.shape, (b.shape[0] * CB if mode == "nn" else b.shape[1])
        assert mode == "nn" or tk % CB == 0
        tn = CB if mode == "nn" else tn
    elif mode == "nn":
        (M, K), N = a.shape, b.shape[1]
    elif mode == "nt":
        (M, K), N = a.shape, b.shape[0]
    else:
        (K, M), N = a.shape, b.shape[1]
    if col_blocked_out:
        assert len(out_dtypes) == 1 and N % CB == 0
        tn = CB
    tm, tn, tk = min(tm, M), min(tn, N), min(tk, K)
    assert M % tm == 0 and N % tn == 0 and K % tk == 0, (M, N, K, tm, tn, tk)
    nk = K // tk
    ne, no = len(extras), len(out_dtypes)
    dims = {"nn": (((1,), (0,)), ((), ())), "nt": (((1,), (1,)), ((), ())), "tn": (((0,), (0,)), ((), ()))}[mode]

    no_ = 0 if order is None else 1

    def body(a_ref, b_ref, *rest):
        rest = rest[no_:]
        ex, outs = rest[:ne], rest[ne:ne + no]

        def finish(acc):
            res = epilogue(acc, *[e[...] for e in ex]) if epilogue is not None else (acc,)
            for o, r in zip(outs, res):
                o[...] = r.astype(o.dtype)

        if col_blocked_b and mode == "nt":
            part = sum(lax.dot_general(a_ref[:, q * CB:(q + 1) * CB], b_ref[q], dims, preferred_element_type=F32)
                       for q in range(tk // CB))
        else:
            part = lax.dot_general(a_ref[...], b_ref[...], dims, preferred_element_type=F32)
        if nk == 1:
            finish(part)
        else:
            acc_ref = rest[-1]
            k = pl.program_id(2)

            @pl.when(k == 0)
            def _():
                acc_ref[...] = part

            @pl.when(k > 0)
            def _():
                acc_ref[...] += part

            @pl.when(k == nk - 1)
            def _():
                finish(acc_ref[...])

    a_spec = {"nn": pl.BlockSpec((tm, tk), lambda i, j, k: (i, k)), "nt": pl.BlockSpec((tm, tk), lambda i, j, k: (i, k)),
              "tn": pl.BlockSpec((tk, tm), lambda i, j, k: (k, i))}[mode]
    b_spec = {"nn": pl.BlockSpec((tk, tn), lambda i, j, k: (k, j)), "nt": pl.BlockSpec((tn, tk), lambda i, j, k: (j, k)),
              "tn": pl.BlockSpec((tk, tn), lambda i, j, k: (k, j))}[mode]
    if col_blocked_b:
        b_spec = (pl.BlockSpec((None, tk, CB), lambda i, j, k: (j, k, 0)) if mode == "nn"
                  else pl.BlockSpec((tk // CB, tn, CB), lambda i, j, k: (k, j, 0)))
    e_spec = pl.BlockSpec((tm, tn), lambda i, j, k: (i, j))
    o_spec, o_dims = e_spec, (M, N)
    if col_blocked_out:
        o_spec, o_dims = pl.BlockSpec((None, tm, CB), lambda i, j, k: (j, i, 0)), (N // CB, M, CB)
    outs = pl.pallas_call(
        body, name=name, grid=(M // tm, N // tn, nk),
        in_specs=[a_spec, b_spec] + [_ANY] * no_ + [e_spec] * ne, out_specs=[o_spec] * no,
        out_shape=[jax.ShapeDtypeStruct(o_dims, dt) for dt in out_dtypes],
        scratch_shapes=[pltpu.VMEM((tm, tn), F32)] if nk > 1 else [],
        compiler_params=pltpu.CompilerParams(dimension_semantics=("parallel", "parallel", "arbitrary")),
    )(a, b, *([] if order is None else [order]), *extras)
    return outs if no > 1 else outs[0]


def _dw_half(a, b, side, *, axis, name, add=None, order=None, tile=1024, tk=2048):
    (K, M), N = a.shape, b.shape[1]
    tk = min(tk, K)
    nk = K // tk
    if axis == "m":
        tm, tn = tile, min(N, 1024)
        grid, o_dims = (4, N // tn, nk), (4, tile, N)
        a_spec = pl.BlockSpec((tk, tm), lambda q, j, k, s: (k, 2 * q + s[0]))
        b_spec = pl.BlockSpec((tk, tn), lambda q, j, k, s: (k, j))
        o_spec = pl.BlockSpec((None, tm, tn), lambda q, j, k, s: (q, 0, j))
    else:
        tm, tn = min(M, 1024), tile
        grid, o_dims = (M // tm, 4, nk), (4, M, tile)
        a_spec = pl.BlockSpec((tk, tm), lambda i, q, k, s: (k, i))
        b_spec = pl.BlockSpec((tk, tn), lambda i, q, k, s: (k, 2 * q + s[0]))
        o_spec = pl.BlockSpec((None, tm, tn), lambda i, q, k, s: (q, i, 0))
    n_order, n_add = int(order is not None), int(add is not None)
    n_out = 1 + n_add

    def body(s_ref, a_ref, b_ref, *rest):
        rest = rest[n_order:]
        outs, acc_ref = rest[n_add:n_add + n_out], rest[-1]
        k = pl.program_id(2)
        part = _dot_tn(a_ref[...], b_ref[...])

        @pl.when(k == 0)
        def _():
            acc_ref[...] = part

        @pl.when(k > 0)
        def _():
            acc_ref[...] += part

        @pl.when(k == nk - 1)
        def _():
            res = acc_ref[...] + rest[0][...].astype(F32) if n_add else acc_ref[...]
            for o in outs:
                o[...] = res.astype(o.dtype)

    outs = pl.pallas_call(
        body, name=name, out_shape=[jax.ShapeDtypeStruct(o_dims, BF16)] * n_out,
        grid_spec=pltpu.PrefetchScalarGridSpec(
            num_scalar_prefetch=1, grid=grid, in_specs=[a_spec, b_spec] + [_ANY] * n_order + [o_spec] * n_add,
            out_specs=[o_spec] * n_out, scratch_shapes=[pltpu.VMEM((tm, tn), F32)]),
        compiler_params=pltpu.CompilerParams(dimension_semantics=("arbitrary", "arbitrary", "arbitrary")),
    )(side, a, b, *([order] if n_order else []), *([add] if n_add else []))
    return outs if n_add else outs[0]


def _norm_fwd(x, g, sc, sh, name, resid=None):
    B, S, Dm = x.shape
    ts = min(S, 256)
    tok = pl.BlockSpec((None, ts, Dm), lambda b, i: (b, i, 0))
    row = pl.BlockSpec((None, 1, Dm), lambda b, i: (b, 0, 0))
    par = pl.BlockSpec((1, Dm), lambda b, i: (0, 0))

    def body(*refs):
        if resid is not None:
            x_ref, br_ref, gt_ref, g_ref, sc_ref, sh_ref, xo_ref, h_ref = refs
            xv = x_ref[...] + gt_ref[...] * br_ref[...]
            xo_ref[...] = xv
        else:
            x_ref, g_ref, sc_ref, sh_ref, h_ref = refs
            xv = x_ref[...]
        r = lax.rsqrt(jnp.mean(xv * xv, axis=-1, keepdims=True) + EPS)
        h_ref[...] = ((xv * r * g_ref[...]) * (1.0 + sc_ref[...]) + sh_ref[...]).astype(BF16)

    h_shape = jax.ShapeDtypeStruct((B, S, Dm), BF16)
    if resid is not None:
        return pl.pallas_call(body, name=name, grid=(B, S // ts), in_specs=[tok, tok, row, par, row, row],
                              out_specs=[tok, tok], out_shape=[jax.ShapeDtypeStruct((B, S, Dm), F32), h_shape],
                              )(x, resid[0], resid[1], g, sc, sh)
    return pl.pallas_call(body, name=name, grid=(B, S // ts), in_specs=[tok, par, row, row], out_specs=tok,
                          out_shape=h_shape)(x, g, sc, sh)


def _norm_bwd(x, g, name, *, sc=None, dh=None, dres=None, tgt=None, br=None, gate=None, x_is_prev=False):
    B, S, Dm = x.shape
    ts = min(S, 256)
    final = tgt is not None
    has_br = br is not None
    tok = pl.BlockSpec((None, ts, Dm), lambda b, i: (b, i, 0))
    row = pl.BlockSpec((None, 1, Dm), lambda b, i: (b, 0, 0))
    par = pl.BlockSpec((1, Dm), lambda b, i: (0, 0))
    ins, in_specs = [x, g], [tok, par]
    if final:
        ins, in_specs = ins + [tgt], in_specs + [tok]
    else:
        ins, in_specs = ins + [sc, dh], in_specs + [row, tok]
    if dres is not None:
        ins, in_specs = ins + [dres], in_specs + [tok]
    if has_br:
        ins, in_specs = ins + [br, gate], in_specs + [tok, row]
    n_in = len(ins)
    out_shape = [jax.ShapeDtypeStruct((B, S, Dm), F32), jax.ShapeDtypeStruct((1, Dm), F32)]
    out_specs = [tok, par]
    if final:
        out_shape.append(jax.ShapeDtypeStruct((1, 128), F32))
        out_specs.append(pl.BlockSpec((1, 128), lambda b, i: (0, 0)))
    else:
        out_shape += [jax.ShapeDtypeStruct((B, 1, Dm), F32)] * 2
        out_specs += [row, row]
    if has_br:
        out_shape += [jax.ShapeDtypeStruct((B, S, Dm), BF16), jax.ShapeDtypeStruct((B, 1, Dm), F32)]
        out_specs += [tok, row]

    def body(*refs):
        it = iter(refs[:n_in])
        outs = iter(refs[n_in:])
        x_ref, g_ref = next(it), next(it)
        b, i = pl.program_id(0), pl.program_id(1)
        first, first_row = (b == 0) & (i == 0), i == 0
        xv, gv = x_ref[...], g_ref[...]
        if x_is_prev:
            xv = xv + refs[n_in - 1][...] * refs[n_in - 2][...]
        r = lax.rsqrt(jnp.mean(xv * xv, axis=-1, keepdims=True) + EPS)
        n = xv * r
        dx_ref, dg_ref = next(outs), next(outs)

        def acc(ref, val, init):
            @pl.when(init)
            def _():
                ref[...] = val

            @pl.when(jnp.logical_not(init))
            def _():
                ref[...] += val

        if final:
            t_ref = next(it)
            loss_ref = next(outs)
            e = n * gv - t_ref[...]
            acc(loss_ref, jnp.zeros((1, 128), F32) + 0.5 * jnp.sum(e * e) / Dm, first)
            dyg = e * (1.0 / Dm)
        else:
            sc_ref, dh_ref = next(it), next(it)
            dsc_ref, dsh_ref = next(outs), next(outs)
            dhv = dh_ref[...]
            acc(dsh_ref, jnp.sum(dhv, axis=0, keepdims=True), first_row)
            acc(dsc_ref, jnp.sum(dhv * (n * gv), axis=0, keepdims=True), first_row)
            dyg = dhv * (1.0 + sc_ref[...])
        acc(dg_ref, jnp.sum(dyg * n, axis=0, keepdims=True), first)
        dn = dyg * gv
        dx = r * (dn - n * jnp.mean(dn * n, axis=-1, keepdims=True))
        if dres is not None:
            dx = dx + next(it)[...]
        dx_ref[...] = dx
        if has_br:
            br_ref, gt_ref = next(it), next(it)
            dbr_ref, dgt_ref = next(outs), next(outs)
            dbr_ref[...] = (dx * gt_ref[...]).astype(BF16)
            acc(dgt_ref, jnp.sum(dx * br_ref[...], axis=0, keepdims=True), first_row)

    outs = pl.pallas_call(body, name=name, grid=(B, S // ts), in_specs=in_specs, out_specs=out_specs, out_shape=out_shape,
                          compiler_params=pltpu.CompilerParams(dimension_semantics=("arbitrary", "arbitrary")))(*ins)
    res = dict(dx=outs[0], dg=outs[1])
    if final:
        res["loss"] = outs[2]
    else:
        res["dsc"], res["dsh"] = outs[2], outs[3]
    if has_br:
        res["dbr"], res["dgate"] = outs[-2], outs[-1]
    return res


def _gm_heads(vg, lng, lnb):
    res = []
    for h in range(GM_H):
        sl = slice(h * 128, (h + 1) * 128)
        vh = vg[:, sl]
        xc = vh - jnp.mean(vh, axis=-1, keepdims=True)
        rstd = lax.rsqrt(jnp.mean(xc * xc, axis=-1, keepdims=True) + 1e-5)
        xhat = xc * rstd
        res.append((xhat, rstd, xhat * lng[:, sl] + lnb[:, sl]))
    return res


def _gm_gate(heads, wt_ref, bsx, nch):
    cols = []
    for h in range(GM_H):
        vn = heads[h][2].astype(BF16)
        rows = [_dot(wt_ref[h], vn[c * CHUNK:(c + 1) * CHUNK]) + bsx[:, h * 128:(h + 1) * 128] for c in range(nch)]
        cols.append(jnp.concatenate(rows, axis=0) if nch > 1 else rows[0])
    return jnp.concatenate(cols, axis=1)


def _gm_specs(S):
    tb = min(S, 512)
    u = pl.BlockSpec((None, tb, GM_W), lambda b, i: (b, i, OFF["u"] // GM_W))
    v = pl.BlockSpec((None, tb, GM_W), lambda b, i: (b, i, OFF["v"] // GM_W))
    tok = pl.BlockSpec((None, tb, GM_W), lambda b, i: (b, i, 0))
    return tb, u, v, tok


def _gmlp_fwd(P, lng, lnb, wt, bsx, og, name):
    B, S, _ = P.shape
    tb, u_spec, v_spec, tok = _gm_specs(S)
    nch = tb // CHUNK

    def body(u_ref, v_ref, lng_ref, lnb_ref, wt_ref, bsx_ref, og_ref, o_ref):
        heads = _gm_heads(_gelu(v_ref[...]), lng_ref[...], lnb_ref[...])
        y = _gelu(u_ref[...]) * _gm_gate(heads, wt_ref, bsx_ref[...], nch)
        r = lax.rsqrt(jnp.mean(y * y, axis=-1, keepdims=True) + EPS)
        o_ref[...] = (y * r * og_ref[...]).astype(BF16)

    return pl.pallas_call(
        body, name=name, grid=(B, S // tb),
        in_specs=[u_spec, v_spec, _full((1, GM_W)), _full((1, GM_W)), _full((GM_H, 128, 128)), _full((128, GM_W)), _full((1, GM_W))],
        out_specs=tok, out_shape=jax.ShapeDtypeStruct((B, S, GM_W + ATT_W + SSM_W), BF16))(P, P, lng, lnb, wt, bsx, og)


def _gmlp_bwd(P, dcat, lng, lnb, wt, wtT, bsx, og, name):
    B, S, _ = P.shape
    tb, u_spec, v_spec, tok = _gm_specs(S)
    nch = tb // CHUNK
    do_spec = pl.BlockSpec((None, tb, GM_W), lambda b, i: (b, i, 0))

    def body(u_ref, v_ref, do_ref, lng_ref, lnb_ref, wt_ref, wtT_ref, bsx_ref, og_ref,
             du_ref, dv_ref, dlng_ref, dlnb_ref, dws_ref, dbsx_ref, dog_ref):
        first = (pl.program_id(0) == 0) & (pl.program_id(1) == 0)

        @pl.when(first)
        def _():
            for ref in (dlng_ref, dlnb_ref, dws_ref, dbsx_ref, dog_ref):
                ref[...] = jnp.zeros(ref.shape, F32)

        u, v, lng = u_ref[...], v_ref[...], lng_ref[...]
        ug = _gelu(u)
        heads = _gm_heads(_gelu(v), lng, lnb_ref[...])
        gate = _gm_gate(heads, wt_ref, bsx_ref[...], nch)
        y = ug * gate
        r = lax.rsqrt(jnp.mean(y * y, axis=-1, keepdims=True) + EPS)
        yn = y * r
        dout = do_ref[...]
        dog_ref[...] += jnp.sum(dout * yn, axis=0, keepdims=True)
        dyn = dout * og_ref[...]
        dy = r * (dyn - yn * jnp.mean(dyn * yn, axis=-1, keepdims=True))
        du_ref[...] = (dy * gate * _gelu_grad(u)).astype(BF16)
        dgate = dy * ug
        tril = lax.broadcasted_iota(jnp.int32, (128, 128), 0) >= lax.broadcasted_iota(jnp.int32, (128, 128), 1)
        dvg = []
        for h in range(GM_H):
            sl = slice(h * 128, (h + 1) * 128)
            xhat, rstd, vn = heads[h]
            vnb = vn.astype(BF16)
            dgh = dgate[:, sl]
            dgb = dgh.astype(BF16)
            dbs = jnp.zeros((128, 128), F32)
            dw = jnp.zeros((128, 128), F32)
            dvn = []
            for c in range(nch):
                rs = slice(c * CHUNK, (c + 1) * CHUNK)
                dbs = dbs + dgh[rs]
                dw = dw + _dot_nt(dgb[rs], vnb[rs])
                dvn.append(_dot(wtT_ref[h], dgb[rs]))
            dvn = jnp.concatenate(dvn, axis=0) if nch > 1 else dvn[0]
            dbsx_ref[:, sl] += dbs
            dws_ref[h] += jnp.where(tril, dw, 0.0)
            dlng_ref[:, sl] += jnp.sum(dvn * xhat, axis=0, keepdims=True)
            dlnb_ref[:, sl] += jnp.sum(dvn, axis=0, keepdims=True)
            dxh = dvn * lng[:, sl]
            dvg.append(rstd * (dxh - jnp.mean(dxh, axis=-1, keepdims=True) - xhat * jnp.mean(dxh * xhat, axis=-1, keepdims=True)))
        dv_ref[...] = (jnp.concatenate(dvg, axis=1) * _gelu_grad(v)).astype(BF16)

    p512, w3 = _full((1, GM_W)), _full((GM_H, 128, 128))
    return pl.pallas_call(
        body, name=name, grid=(B, S // tb),
        in_specs=[u_spec, v_spec, do_spec, p512, p512, w3, w3, _full((128, GM_W)), p512],
        out_specs=[tok, tok, p512, p512, w3, _full((128, GM_W)), p512],
        out_shape=[jax.ShapeDtypeStruct((B, S, GM_W), BF16)] * 2 + [
            jax.ShapeDtypeStruct((1, GM_W), F32), jax.ShapeDtypeStruct((1, GM_W), F32),
            jax.ShapeDtypeStruct((GM_H, 128, 128), F32), jax.ShapeDtypeStruct((128, GM_W), F32),
            jax.ShapeDtypeStruct((1, GM_W), F32)],
        compiler_params=pltpu.CompilerParams(dimension_semantics=("arbitrary", "arbitrary")),
    )(P, P, dcat, lng, lnb, wt, wtT, bsx, og)


def _lane_half():
    return lax.broadcasted_iota(jnp.int32, (128, 128), 1) // 64


def _att_stack(x, kvh, dtype):
    half = _lane_half()
    rows = []
    for g in range(4):
        i = kvh * 4 + g
        pair = x[:, (i // 2) * 128:(i // 2 + 1) * 128]
        if i % 2 != kvh:
            pair = pltpu.roll(pair, 64, 1)
        rows.append(jnp.where(half == kvh, pair, 0.0))
    return jnp.concatenate(rows, axis=0).astype(dtype)


def _att_unstack(pairs, y, kvh):
    half = _lane_half()
    for g in range(4):
        i = kvh * 4 + g
        piece = y[g * 128:(g + 1) * 128]
        if i % 2 != kvh:
            piece = pltpu.roll(piece, 64, 1)
        pairs[i // 2] = jnp.where(half == i % 2, piece, pairs[i // 2])
    return pairs


def _att_probs(qb, k2, st, sink_ref, kvh):
    qm = _att_stack(qb, kvh, BF16)
    s = _dot_nt(qm, k2) * (64 ** -0.5)
    qi = lax.broadcasted_iota(jnp.int32, (512, 256), 0) % 128
    kj = lax.broadcasted_iota(jnp.int32, (512, 256), 1)
    diff = qi + 128 - kj
    valid = (diff >= 0) & (diff < 128) & (st + kj - 128 >= 0)
    s = jnp.where(valid, s, NEG_INF)
    grp = lax.broadcasted_iota(jnp.int32, (512, 1), 0) // 128
    sink = jnp.zeros((512, 1), F32)
    for g in range(4):
        sink = jnp.where(grp == g, sink_ref[kvh * 4 + g], sink)
    m = jnp.maximum(jnp.max(s, axis=-1, keepdims=True), sink)
    e = jnp.exp(s - m)
    esink = jnp.exp(sink - m)
    inv = 1.0 / (jnp.sum(e, axis=-1, keepdims=True) + esink)
    return qm, e * inv, esink * inv


def _att_specs(S):
    q = pl.BlockSpec((None, S, ATT_W), lambda b: (b, 0, OFF["q"] // ATT_W))
    k = pl.BlockSpec((None, S, KV_W), lambda b: (b, 0, OFF["k"] // KV_W))
    v = pl.BlockSpec((None, S, KV_W), lambda b: (b, 0, OFF["vv"] // KV_W))
    tok = pl.BlockSpec((None, S, ATT_W), lambda b: (b, 0, 0))
    kv = pl.BlockSpec((None, S, KV_W), lambda b: (b, 0, 0))
    return q, k, v, tok, kv


_SMEM = pl.BlockSpec(memory_space=pltpu.SMEM)


def _attn_fwd(P, sinks, og, cat, name):
    B, S, _ = P.shape
    q_spec, k_spec, v_spec, _, _ = _att_specs(S)
    tok = pl.BlockSpec((None, S, ATT_W), lambda b: (b, 0, GM_W // ATT_W))

    def body(q_ref, k_ref, v_ref, sink_ref, og_ref, cat_ref, o_ref, kpad, vpad):
        kpad[0:128, :] = jnp.zeros((128, KV_W), BF16)
        vpad[0:128, :] = jnp.zeros((128, KV_W), BF16)
        kpad[128:, :] = k_ref[...].astype(BF16)
        vpad[128:, :] = v_ref[...].astype(BF16)

        def step(n, carry):
            st = pl.multiple_of(n * 128, 128)
            qb = q_ref[pl.ds(st, 128), :]
            k2, v2 = kpad[pl.ds(st, 256), :], vpad[pl.ds(st, 256), :]
            pairs = [jnp.zeros((128, 128), F32)] * 4
            for kvh in range(2):
                _, p, _ = _att_probs(qb, k2, st, sink_ref, kvh)
                pairs = _att_unstack(pairs, _dot(p.astype(BF16), v2), kvh)
            o = jnp.concatenate(pairs, axis=1)
            r = lax.rsqrt(jnp.mean(o * o, axis=-1, keepdims=True) + EPS)
            o_ref[pl.ds(st, 128), :] = (o * r * og_ref[...]).astype(BF16)
            return carry

        lax.fori_loop(0, S // 128, step, 0)

    return pl.pallas_call(
        body, name=name, grid=(B,), in_specs=[q_spec, k_spec, v_spec, _SMEM, _full((1, ATT_W)), _ANY], out_specs=tok,
        out_shape=jax.ShapeDtypeStruct(cat.shape, BF16), input_output_aliases={5: 0},
        scratch_shapes=[pltpu.VMEM((S + 128, KV_W), BF16)] * 2)(P, P, P, sinks, og, cat)


def _attn_bwd(P, dcat, sinks, og, name):
    B, S, _ = P.shape
    q_spec, k_spec, v_spec, tok, kv = _att_specs(S)
    do_spec = pl.BlockSpec((None, S, ATT_W), lambda b: (b, 0, GM_W // ATT_W))

    def body(q_ref, k_ref, v_ref, do_ref, sink_ref, og_ref, dq_ref, dk_ref, dv_ref, dsink_ref, dog_ref,
             kpad, vpad, dkpad, dvpad):
        @pl.when(pl.program_id(0) == 0)
        def _():
            dsink_ref[...] = jnp.zeros((8, 128), F32)
            dog_ref[...] = jnp.zeros((1, ATT_W), F32)

        kpad[0:128, :] = jnp.zeros((128, KV_W), BF16)
        vpad[0:128, :] = jnp.zeros((128, KV_W), BF16)
        kpad[128:, :] = k_ref[...].astype(BF16)
        vpad[128:, :] = v_ref[...].astype(BF16)
        dkpad[...] = jnp.zeros((S + 128, KV_W), F32)
        dvpad[...] = jnp.zeros((S + 128, KV_W), F32)
        half = _lane_half()
        head_row = lax.broadcasted_iota(jnp.int32, (8, 128), 0)

        def step(n, carry):
            st = pl.multiple_of(n * 128, 128)
            qb = q_ref[pl.ds(st, 128), :]
            k2, v2 = kpad[pl.ds(st, 256), :], vpad[pl.ds(st, 256), :]
            saved, pairs = [], [jnp.zeros((128, 128), F32)] * 4
            for kvh in range(2):
                qm, p, psink = _att_probs(qb, k2, st, sink_ref, kvh)
                o = _dot(p.astype(BF16), v2)
                saved.append((qm, p, psink, o))
                pairs = _att_unstack(pairs, o, kvh)
            o = jnp.concatenate(pairs, axis=1)
            r = lax.rsqrt(jnp.mean(o * o, axis=-1, keepdims=True) + EPS)
            on = o * r
            dout = do_ref[pl.ds(st, 128), :]
            dog_ref[...] += jnp.sum(dout * on, axis=0, keepdims=True)
            dyn = dout * og_ref[...]
            do = r * (dyn - on * jnp.mean(dyn * on, axis=-1, keepdims=True))
            dq_pairs = [jnp.zeros((128, 128), F32)] * 4
            dsink = jnp.zeros((8, 128), F32)
            for kvh in range(2):
                qm, p, psink, og_ = saved[kvh]
                dog = _att_stack(do, kvh, F32)
                delta = jnp.sum(dog * jnp.where(jnp.concatenate([half] * 4, axis=0) == kvh, og_, 0.0), axis=-1, keepdims=True)
                dogb, pb = dog.astype(BF16), p.astype(BF16)
                dvpad[pl.ds(st, 256), :] += _dot_tn(pb, dogb)
                dp = _dot_nt(dogb, v2)
                ds = (p * (dp - delta) * (64 ** -0.5)).astype(BF16)
                sd = psink * delta
                for g in range(4):
                    dsink = dsink - jnp.where(head_row == kvh * 4 + g, jnp.sum(sd[g * 128:(g + 1) * 128]), 0.0)
                dq_pairs = _att_unstack(dq_pairs, _dot(ds, k2), kvh)
                dkpad[pl.ds(st, 256), :] += _dot_tn(ds, qm)
            dsink_ref[...] += dsink
            dq_ref[pl.ds(st, 128), :] = jnp.concatenate(dq_pairs, axis=1).astype(BF16)
            return carry

        lax.fori_loop(0, S // 128, step, 0)
        dk_ref[...] = dkpad[128:, :].astype(BF16)
        dv_ref[...] = dvpad[128:, :].astype(BF16)

    return pl.pallas_call(
        body, name=name, grid=(B,),
        in_specs=[q_spec, k_spec, v_spec, do_spec, _SMEM, _full((1, ATT_W))],
        out_specs=[tok, kv, kv, _full((8, 128)), _full((1, ATT_W))],
        out_shape=[jax.ShapeDtypeStruct((B, S, ATT_W), BF16), jax.ShapeDtypeStruct((B, S, KV_W), BF16),
                   jax.ShapeDtypeStruct((B, S, KV_W), BF16), jax.ShapeDtypeStruct((8, 128), F32),
                   jax.ShapeDtypeStruct((1, ATT_W), F32)],
        scratch_shapes=[pltpu.VMEM((S + 128, KV_W), BF16)] * 2 + [pltpu.VMEM((S + 128, KV_W), F32)] * 2,
        compiler_params=pltpu.CompilerParams(dimension_semantics=("arbitrary",)),
    )(P, P, P, dcat, sinks, og)


CONV_TC = 256
CONV_RC = 64


def _conv_taps(ext, r0):
    return [ext[pl.ds(r0 + 8 - k, CONV_RC), :] for k in range(4)]


def _conv_pre(taps, w_ref, b_ref):
    acc = b_ref[...] + w_ref[3:4, :] * taps[0]
    for k in range(1, 4):
        acc = acc + w_ref[3 - k:4 - k, :] * taps[k]
    return acc


def _conv_fwd(P, w8, b, name):
    B, S, _ = P.shape
    nj = CONV_CH // CONV_TC
    x_spec = pl.BlockSpec((None, S, CONV_TC), lambda b_, j: (b_, 0, OFF["xbc"] // CONV_TC + j))
    tok = pl.BlockSpec((None, S, CONV_TC), lambda b_, j: (b_, 0, j))

    def body(x_ref, w_ref, b_ref, o_ref, ext):
        ext[0:8, :] = jnp.zeros((8, CONV_TC), F32)
        ext[8:, :] = x_ref[...]
        for r0 in range(0, S, CONV_RC):
            pre = _conv_pre(_conv_taps(ext, r0), w_ref, b_ref)
            o_ref[pl.ds(r0, CONV_RC), :] = pre * _sigmoid(pre)

    return pl.pallas_call(
        body, name=name, grid=(B, nj),
        in_specs=[x_spec, pl.BlockSpec((8, CONV_TC), lambda b_, j: (0, j)), pl.BlockSpec((1, CONV_TC), lambda b_, j: (0, j))],
        out_specs=tok, out_shape=jax.ShapeDtypeStruct((B, S, CONV_CH), F32),
        scratch_shapes=[pltpu.VMEM((S + 8, CONV_TC), F32)])(P, w8, b)


def _conv_bwd(P, dact, w8, b, name):
    B, S, _ = P.shape
    nj = CONV_CH // CONV_TC
    x_spec = pl.BlockSpec((None, S, CONV_TC), lambda j, b_: (b_, 0, OFF["xbc"] // CONV_TC + j))
    tok = pl.BlockSpec((None, S, CONV_TC), lambda j, b_: (b_, 0, j))
    w_spec = pl.BlockSpec((8, CONV_TC), lambda j, b_: (0, j))
    b_spec = pl.BlockSpec((1, CONV_TC), lambda j, b_: (0, j))

    def body(x_ref, d_ref, w_ref, b_ref, dx_ref, dw_ref, db_ref, ext, extd):
        @pl.when(pl.program_id(1) == 0)
        def _():
            dw_ref[...] = jnp.zeros((8, CONV_TC), F32)
            db_ref[...] = jnp.zeros((1, CONV_TC), F32)

        ext[0:8, :] = jnp.zeros((8, CONV_TC), F32)
        ext[8:, :] = x_ref[...]
        extd[pl.ds(8 + S, 8), :] = jnp.zeros((8, CONV_TC), F32)
        db = jnp.zeros((1, CONV_TC), F32)
        dws = [jnp.zeros((1, CONV_TC), F32)] * 4
        for r0 in range(0, S, CONV_RC):
            taps = _conv_taps(ext, r0)
            pre = _conv_pre(taps, w_ref, b_ref)
            sg = _sigmoid(pre)
            dpre = d_ref[pl.ds(r0, CONV_RC), :] * (sg * (1.0 + pre * (1.0 - sg)))
            extd[pl.ds(8 + r0, CONV_RC), :] = dpre
            db = db + jnp.sum(dpre, axis=0, keepdims=True)
            dws = [dws[i] + jnp.sum(dpre * taps[3 - i], axis=0, keepdims=True) for i in range(4)]
        for r0 in range(0, S, CONV_RC):
            dx = w_ref[3:4, :] * extd[pl.ds(8 + r0, CONV_RC), :]
            for k in range(1, 4):
                dx = dx + w_ref[3 - k:4 - k, :] * extd[pl.ds(8 + r0 + k, CONV_RC), :]
            dx_ref[pl.ds(r0, CONV_RC), :] = dx.astype(BF16)
        db_ref[...] += db
        sub = lax.broadcasted_iota(jnp.int32, (8, CONV_TC), 0)
        dw_ref[...] += sum(jnp.where(sub == i, dws[i], 0.0) for i in range(4))

    return pl.pallas_call(
        body, name=name, grid=(nj, B), in_specs=[x_spec, tok, w_spec, b_spec], out_specs=[tok, w_spec, b_spec],
        out_shape=[jax.ShapeDtypeStruct((B, S, CONV_CH), BF16), jax.ShapeDtypeStruct((8, CONV_CH), F32),
                   jax.ShapeDtypeStruct((1, CONV_CH), F32)],
        scratch_shapes=[pltpu.VMEM((S + 8, CONV_TC), F32), pltpu.VMEM((S + 16, CONV_TC), F32)],
        compiler_params=pltpu.CompilerParams(dimension_semantics=("arbitrary", "arbitrary")),
    )(P, dact, w8, b)


def _ssd_consts():
    hd = np.arange(SSM_W) // SSM_HD
    E = (np.arange(128)[:, None] == hd[None, :]).astype(np.float32)
    tri = (np.arange(128)[:, None] >= np.arange(128)[None, :]).astype(np.float32)
    return jnp.asarray(E, BF16), jnp.asarray(E.T, BF16), jnp.asarray(tri, BF16), jnp.asarray(tri.T, BF16)


def _pieces(x, n):
    out, r = [], x
    for _ in range(n):
        p = r.astype(BF16)
        out.append(p)
        r = r - p.astype(F32)
    return out


def _dot01(x, m01, n):
    return sum(_dot(p, m01) for p in _pieces(x, n))


def _dot01_left(m01, x, n):
    return sum(_dot(m01, p) for p in _pieces(x, n))


def _ssd_pre(xa, dtraw, bias, alog, E, tri):
    lane = lax.broadcasted_iota(jnp.int32, (128, 128), 1)
    pre = dtraw + bias
    dtp = jnp.where(lane < SSM_H, jnp.maximum(pre, 0.0) + jnp.log(1.0 + jnp.exp(-jnp.abs(pre))), 0.0)
    a = -jnp.exp(alog)
    acs = _dot01_left(tri, dtp * a, 3)
    acsT = acs.T
    dtE, acsE = _dot01(dtp, E, 2), _dot01(acs, E, 3)
    X = xa[:, :SSM_W]
    xdt = X * dtE
    wE = jnp.exp(acsE[127:128, :] - acsE)
    eE = jnp.exp(acsE)
    cdE = eE[127:128, :]
    return dict(pre=pre, dtp=dtp, a=a, acs=acs, acsT=acsT, dtE=dtE, acsE=acsE, cdE=cdE, X=X, xdt=xdt, wE=wE, eE=eE)


def _ssd_decay(c, h):
    lm = lax.broadcasted_iota(jnp.int32, (128, 128), 0) >= lax.broadcasted_iota(jnp.int32, (128, 128), 1)
    return jnp.exp(jnp.where(lm, c["acs"][:, h:h + 1] - c["acsT"][h:h + 1, :], NEG_INF))


def _ssd_pair_operands(c, CB, h0):
    lane = lax.broadcasted_iota(jnp.int32, (128, 128), 1)
    L0, L1 = _ssd_decay(c, h0), _ssd_decay(c, h0 + 1)
    M = jnp.concatenate([CB * L0, CB * L1], axis=1).astype(BF16)
    xp = c["xdt"][:, h0 * 64:h0 * 64 + 128]
    BD = jnp.concatenate([jnp.where(lane < 64, xp, 0.0), jnp.where(lane >= 64, xp, 0.0)], axis=0).astype(BF16)
    return L0, L1, M, BD


def _ssd_y(c, xa, state_ref, dskipE):
    per_group, ys = [], []
    for g in range(SSM_G):
        gs = slice(g * 512, (g + 1) * 512)
        Bb = xa[:, SSM_W + g * 128:SSM_W + (g + 1) * 128].astype(BF16)
        Cb = xa[:, SSM_W + 256 + g * 128:SSM_W + 256 + (g + 1) * 128].astype(BF16)
        CB = _dot_nt(Cb, Bb)
        Sg = state_ref[:, gs]
        yoff = _dot(Cb, Sg.astype(BF16)) * c["eE"][:, gs]
        ydiag, pairs = [], []
        for j in range(4):
            ops = _ssd_pair_operands(c, CB, g * 8 + 2 * j)
            pairs.append(ops)
            ydiag.append(_dot(ops[2], ops[3]))
        ys.append(jnp.concatenate(ydiag, axis=1) + yoff)
        per_group.append(dict(Bb=Bb, Cb=Cb, CB=CB, Sg=Sg, yoff=yoff, pairs=pairs))
    Y = jnp.concatenate(ys, axis=1) + c["X"] * dskipE
    return Y, per_group


def _ssd_specs(S, rev):
    nc = S // CHUNK
    cm = (lambda b, i: (b, nc - 1 - i)) if rev else (lambda b, i: (b, i))
    xa = pl.BlockSpec((None, CHUNK, CONV_CH), lambda b, i: cm(b, i) + (0,))
    z = [pl.BlockSpec((None, CHUNK, 256), lambda b, i, q=q: cm(b, i) + (OFF["z"] // 256 + q,)) for q in range(4)]
    dt = pl.BlockSpec((None, CHUNK, 128), lambda b, i: cm(b, i) + (OFF["dt"] // 128,))
    tok = pl.BlockSpec((None, CHUNK, SSM_W), lambda b, i: cm(b, i) + (0,))
    st = pl.BlockSpec((None, None, 128, SSM_W), lambda b, i: cm(b, i) + (0, 0))
    return nc, xa, z, dt, tok, st


def _ssd_fwd(xact, P, bias, alog, dskipE, ng, cat, name):
    B, S, _ = P.shape
    nc, xa_spec, z_specs, dt_spec, _, st_spec = _ssd_specs(S, False)
    tok = pl.BlockSpec((None, CHUNK, SSM_W), lambda b, i: (b, i, 1))
    E, _, tri, _ = _ssd_consts()

    def body(xa_ref, z0, z1, z2, z3, dt_ref, bias_ref, alog_ref, dsk_ref, ng_ref, E_ref, tri_ref, cat_ref, o_ref, sp_ref, state):
        @pl.when(pl.program_id(1) == 0)
        def _():
            state[...] = jnp.zeros((128, SSM_W), F32)

        sp_ref[...] = state[...]
        xa = xa_ref[...]
        c = _ssd_pre(xa, dt_ref[...], bias_ref[...], alog_ref[...], E_ref[...], tri_ref[...])
        Y, groups = _ssd_y(c, xa, state, dsk_ref[...])
        Z = (c["xdt"] * c["wE"]).astype(BF16)
        for g in range(SSM_G):
            gs = slice(g * 512, (g + 1) * 512)
            state[:, gs] = groups[g]["Sg"] * c["cdE"][:, gs] + _dot_tn(groups[g]["Bb"], Z[:, gs])
        zv = jnp.concatenate([z0[...], z1[...], z2[...], z3[...]], axis=1)
        yz = Y * (zv * _sigmoid(zv))
        outs = []
        for g in range(SSM_G):
            yg = yz[:, g * 512:(g + 1) * 512]
            outs.append(yg * lax.rsqrt(jnp.mean(yg * yg, axis=-1, keepdims=True) + EPS))
        o_ref[...] = (jnp.concatenate(outs, axis=1) * ng_ref[...]).astype(BF16)

    return pl.pallas_call(
        body, name=name, grid=(B, nc),
        in_specs=[xa_spec] + z_specs + [dt_spec, _full((1, 128)), _full((1, 128)), _full((1, SSM_W)), _full((1, SSM_W)),
                                        _full((128, SSM_W)), _full((128, 128)), _ANY],
        out_specs=[tok, st_spec],
        out_shape=[jax.ShapeDtypeStruct(cat.shape, BF16), jax.ShapeDtypeStruct((B, nc, 128, SSM_W), F32)],
        scratch_shapes=[pltpu.VMEM((128, SSM_W), F32)], input_output_aliases={12: 0},
        compiler_params=pltpu.CompilerParams(dimension_semantics=("arbitrary", "arbitrary")),
    )(xact, P, P, P, P, P, bias, alog, dskipE, ng, E, tri, cat)


def _ssd_bwd(xact, P, sprev, dcat, bias, alog, dskipE, ng, name):
    B, S, _ = P.shape
    nc, xa_spec, z_specs, dt_spec, tok, st_spec = _ssd_specs(S, True)
    do_spec = pl.BlockSpec((None, CHUNK, SSM_W), lambda b, i: (b, nc - 1 - i, 1))
    E, ET, tri, triT = _ssd_consts()
    dt_out = pl.BlockSpec((None, CHUNK, 128), lambda b, i: (b, nc - 1 - i, 0))

    def body(xa_ref, z0, z1, z2, z3, dt_ref, sp_ref, do_ref, bias_ref, alog_ref, dsk_ref, ng_ref, E_ref, ET_ref, tri_ref,
             triT_ref, dxa_ref, dz_ref, ddt_ref, dbias_ref, dalog_ref, ddsk_ref, dng_ref, dstate):
        first = (pl.program_id(0) == 0) & (pl.program_id(1) == 0)

        @pl.when(first)
        def _():
            for ref in (dbias_ref, dalog_ref, ddsk_ref, dng_ref):
                ref[...] = jnp.zeros(ref.shape, F32)

        @pl.when(pl.program_id(1) == 0)
        def _():
            dstate[...] = jnp.zeros((128, SSM_W), F32)

        xa, ETm = xa_ref[...], ET_ref[...]
        c = _ssd_pre(xa, dt_ref[...], bias_ref[...], alog_ref[...], E_ref[...], tri_ref[...])
        Y, groups = _ssd_y(c, xa, sp_ref, dsk_ref[...])
        X, xdt = c["X"], c["xdt"]
        zv = jnp.concatenate([z0[...], z1[...], z2[...], z3[...]], axis=1)
        sg = _sigmoid(zv)
        zs = zv * sg
        yz = Y * zs
        dout = do_ref[...]
        dyz = []
        for g in range(SSM_G):
            gs = slice(g * 512, (g + 1) * 512)
            yg = yz[:, gs]
            r = lax.rsqrt(jnp.mean(yg * yg, axis=-1, keepdims=True) + EPS)
            yn = yg * r
            dng_ref[:, gs] += jnp.sum(dout[:, gs] * yn, axis=0, keepdims=True)
            dyn = dout[:, gs] * ng_ref[:, gs]
            dyz.append(r * (dyn - yn * jnp.mean(dyn * yn, axis=-1, keepdims=True)))
        dyz = jnp.concatenate(dyz, axis=1)
        dz_ref[...] = (dyz * Y * (sg * (1.0 + zv * (1.0 - sg)))).astype(BF16)
        dY = dyz * zs
        ddsk_ref[...] += jnp.sum(dY * X, axis=0, keepdims=True)
        dX = dY * dsk_ref[...]
        lane = lax.broadcasted_iota(jnp.int32, (128, 128), 1)
        sub = lax.broadcasted_iota(jnp.int32, (128, 128), 0)
        colform = jnp.zeros((128, 128), F32)
        rowform = jnp.zeros((128, 128), F32)
        dxdt, gacsE, dBC = [], [], []
        for g in range(SSM_G):
            gs = slice(g * 512, (g + 1) * 512)
            G = groups[g]
            Bb, Cb, CB, Sg = G["Bb"], G["Cb"], G["CB"], G["Sg"]
            dYg = dY[:, gs]
            dQ = (dYg * c["eE"][:, gs]).astype(BF16)
            dSn = dstate[:, gs]
            dSnb = dSn.astype(BF16)
            cd = c["cdE"][:, gs]
            dC = _dot_nt(dQ, Sg.astype(BF16))
            dSprev = _dot_tn(Cb, dQ) + dSn * cd
            t1 = jnp.broadcast_to(jnp.sum(dSn * Sg * cd, axis=0, keepdims=True), (8, 512))
            colform = colform + jnp.where(sub == 127, _dot01(t1, ETm[gs, :], 2)[0:1, :], 0.0)
            Zg = xdt[:, gs] * c["wE"][:, gs]
            dZ = _dot(Bb, dSnb)
            dB = _dot_nt(Zg.astype(BF16), dSnb)
            U = dZ * Zg
            ga = dYg * G["yoff"] - U
            ga = ga + jnp.where(lax.broadcasted_iota(jnp.int32, (128, 512), 0) == 127, jnp.sum(U, axis=0, keepdims=True), 0.0)
            gacsE.append(ga)
            dxg = [None] * 4
            dCB = jnp.zeros((128, 128), F32)
            for j in range(4):
                h0 = g * 8 + 2 * j
                L0, L1, M, BD = G["pairs"][j]
                dYp = dYg[:, j * 128:(j + 1) * 128].astype(BF16)
                dM = _dot_nt(dYp, BD)
                dBD = _dot_tn(M, dYp)
                dxg[j] = jnp.where(lane < 64, dBD[:128], dBD[128:])
                for t, (h, L) in enumerate(((h0, L0), (h0 + 1, L1))):
                    dMh = dM[:, t * 128:(t + 1) * 128]
                    dCB = dCB + dMh * L
                    Gh = dMh * CB * L
                    colform = colform + jnp.where(lane == h, jnp.sum(Gh, axis=1, keepdims=True), 0.0)
                    rowform = rowform - jnp.where(sub == h, jnp.sum(Gh, axis=0, keepdims=True), 0.0)
            dCBb = dCB.astype(BF16)
            dC = dC + _dot(dCBb, Bb)
            dB = dB + _dot_tn(dCBb, Cb)
            dxdt.append(jnp.concatenate(dxg, axis=1) + dZ * c["wE"][:, gs])
            dBC.append((dB, dC))
            dstate[:, gs] = dSprev
        dxdt = jnp.concatenate(dxdt, axis=1)
        dX = dX + dxdt * c["dtE"]
        ddt = _dot01(dxdt * X, ETm, 2)
        dacs = colform + rowform.T + _dot01(jnp.concatenate(gacsE, axis=1), ETm, 2)
        dda = _dot01_left(triT_ref[...], dacs, 2)
        ddt = ddt + dda * c["a"]
        dalog_ref[...] += jnp.sum(dda * c["dtp"], axis=0, keepdims=True) * c["a"]
        ddtraw = jnp.where(lane < SSM_H, ddt * _sigmoid(c["pre"]), 0.0)
        dbias_ref[...] += jnp.sum(ddtraw, axis=0, keepdims=True)
        ddt_ref[...] = ddtraw.astype(BF16)
        dxa_ref[...] = jnp.concatenate([dX, dBC[0][0], dBC[1][0], dBC[0][1], dBC[1][1]], axis=1)

    p128, p1k = _full((1, 128)), _full((1, SSM_W))
    return pl.pallas_call(
        body, name=name, grid=(B, nc),
        in_specs=[xa_spec] + z_specs + [dt_spec, st_spec, do_spec, p128, p128, p1k, p1k,
                                        _full((128, SSM_W)), _full((SSM_W, 128)), _full((128, 128)), _full((128, 128))],
        out_specs=[xa_spec, tok, dt_out, p128, p128, p1k, p1k],
        out_shape=[jax.ShapeDtypeStruct((B, S, CONV_CH), F32), jax.ShapeDtypeStruct((B, S, SSM_W), BF16),
                   jax.ShapeDtypeStruct((B, S, 128), BF16), jax.ShapeDtypeStruct((1, 128), F32),
                   jax.ShapeDtypeStruct((1, 128), F32), jax.ShapeDtypeStruct((1, SSM_W), F32),
                   jax.ShapeDtypeStruct((1, SSM_W), F32)],
        scratch_shapes=[pltpu.VMEM((128, SSM_W), F32)],
        compiler_params=pltpu.CompilerParams(dimension_semantics=("arbitrary", "arbitrary")),
    )(xact, P, P, P, P, P, sprev, dcat, bias, alog, dskipE, ng, E, ET, tri, triT)


def _adamw(w, parts, m, v, name, tr=512, row0=0, prev=None):
    Rtot, C = w.shape
    ns, R = parts.shape[0], parts.shape[1]
    tr = min(tr, R)
    assert R % tr == 0 and row0 % tr == 0
    off = row0 // tr
    c1 = 1.0 / (1.0 - ADAM_B1 ** ADAM_STEP)
    c2 = 1.0 / (1.0 - ADAM_B2 ** ADAM_STEP)

    def body(w_ref, p_ref, m_ref, v_ref, *rest):
        g_ref, d_ref, mo_ref, vo_ref = rest[-4:]
        g = p_ref[0].astype(F32)
        for s in range(1, ns):
            g = g + p_ref[s].astype(F32)
        mn = ADAM_B1 * m_ref[...] + (1.0 - ADAM_B1) * g
        vn = ADAM_B2 * v_ref[...] + (1.0 - ADAM_B2) * (g * g)
        g_ref[...] = g
        mo_ref[...] = mn
        vo_ref[...] = vn
        d_ref[...] = -ADAM_LR * ((mn * c1) / (jnp.sqrt(vn * c2) + ADAM_EPS) + ADAM_WD * w_ref[...])

    blk = pl.BlockSpec((tr, C), lambda i: (i + off, 0))
    extra = [] if prev is None else list(prev)
    return pl.pallas_call(
        body, name=name, grid=(R // tr,),
        in_specs=[blk, pl.BlockSpec((ns, tr, C), lambda i: (0, i, 0)), blk, blk] + [pl.BlockSpec(memory_space=pl.ANY)] * len(extra),
        out_specs=[blk] * 4, out_shape=[jax.ShapeDtypeStruct((Rtot, C), F32)] * 4,
        input_output_aliases={4 + k: k for k in range(len(extra))})(w, parts, m, v, *extra)


_SMALL = ("ada_b", "norm1_g", "gm_ln_g", "gm_ln_b", "gm_ws", "gm_bs", "gm_norm_g", "attn_sinks", "attn_norm_g", "conv_b",
          "dt_bias", "a_log", "d_skip", "ssm_norm_g", "norm2_g", "final_norm_g")


def _pack(arrs):
    flat = []
    for a in arrs:
        f = a.reshape(-1).astype(F32)
        flat.append(jnp.pad(f, (0, (-f.shape[0]) % 1024)))
    return jnp.concatenate(flat).reshape(-1, 128)


def _unpack(pack, like):
    out, o = [], 0
    flat = pack.reshape(-1)
    for a in like:
        n = int(np.prod(a.shape))
        out.append(flat[o:o + n].reshape(a.shape))
        o += n + (-n) % 1024
    return out


def kernel(x, c, ada_w, ada_b, norm1_g, w_in, gm_ln_g, gm_ln_b, gm_ws, gm_bs, gm_norm_g, attn_sinks, attn_norm_g, conv_w, conv_b, dt_bias, a_log, d_skip, ssm_norm_g, w_out, norm2_g, w_mlp1, w_mlp2, final_norm_g, loss_target, m_ada_w, m_ada_b, m_norm1_g, m_w_in, m_gm_ln_g, m_gm_ln_b, m_gm_ws, m_gm_bs, m_gm_norm_g, m_attn_sinks, m_attn_norm_g, m_conv_w, m_conv_b, m_dt_bias, m_a_log, m_d_skip, m_ssm_norm_g, m_w_out, m_norm2_g, m_w_mlp1, m_w_mlp2, m_final_norm_g, v_ada_w, v_ada_b, v_norm1_g, v_w_in, v_gm_ln_g, v_gm_ln_b, v_gm_ws, v_gm_bs, v_gm_norm_g, v_attn_sinks, v_attn_norm_g, v_conv_w, v_conv_b, v_dt_bias, v_a_log, v_d_skip, v_ssm_norm_g, v_w_out, v_norm2_g, v_w_mlp1, v_w_mlp2, v_final_norm_g):
    args = dict(locals())
    B, S, _ = x.shape
    T = B * S
    L = DEPTH
    me = 4 * lax.axis_index("x") + 2 * lax.axis_index("y") + lax.axis_index("c")

    gath = _gather2([c, conv_w], "ag_c")
    big = ("w_in", "w_out", "w_mlp1", "w_mlp2")
    chain = [(n, l) for l in range(L) for n in ("w_in", "w_mlp1", "w_out", "w_mlp2")]
    inflight = {}

    def start_next(order):
        if not chain:
            return jnp.zeros((8, 128), F32)
        n, l = chain.pop(0)
        sems, land_thru, token = _gather_start(zone[n, l], order, f"ag_start_{n}{l}")
        inflight[n, l] = (sems, land_thru)
        return token

    def gathered(n, l, after):
        land = _gather_wait(*inflight.pop((n, l)), after, f"ag_wait_{n}{l}")
        return _gather_finish(land, f"ag_fin_{n}{l}")

    me1 = me.astype(jnp.int32).reshape(1)
    zone = {(n, l): _landing_zone(args[n], l, me1, f"ag_zone_{n}{l}") for n, l in chain}
    later_zones = [zone[k] for k in chain[1:]]

    tok = start_next(gath[0])
    c_all = gath[0].reshape(NDEV * B, D) + tok[0, 0]
    c_act = (c_all * jax.nn.sigmoid(c_all)).astype(BF16)
    nb_rows = c_act.shape[0]
    c_pad = jnp.pad(c_act, ((0, 128 - nb_rows), (0, 0)))
    adw = ada_w.astype(BF16)
    mod_part = jnp.stack([_mm(c_pad, adw[l], mode="nn", name=f"mod{l}", tn=768)[:nb_rows] for l in range(L)])
    mod_all = _gather_small([mod_part], "ag_mod", order=later_zones)[0]
    mod_mine = lax.dynamic_slice_in_dim(mod_all, me * B, B, axis=2)
    mod = jnp.transpose(mod_mine, (1, 2, 0, 3)).reshape(L, B, 6 * D) + ada_b[:, None, :]
    mods = [[mod[l][:, None, i * D:(i + 1) * D] for i in range(6)] for l in range(L)]

    win_g, wout_g, w1_g, w2_g = [None] * L, [None] * L, [None] * L, [None] * L

    tril = jnp.tril(jnp.ones((128, 128), F32))
    row = lambda a: a.reshape(1, -1)
    pad128 = lambda a: jnp.pad(a.reshape(1, -1), ((0, 0), (0, 128 - a.shape[-1])))
    small = []
    for l in range(L):
        wt = gm_ws[l] * tril
        small.append(dict(
            lng=row(gm_ln_g[l]), lnb=row(gm_ln_b[l]), wt=wt.astype(BF16), wtT=jnp.swapaxes(wt, 1, 2).astype(BF16),
            bsx=jnp.repeat(gm_bs[l].T, 128, axis=1), gog=row(gm_norm_g[l]), sinks=attn_sinks[l], aog=row(attn_norm_g[l]),
            bias=pad128(dt_bias[l]), alog=pad128(a_log[l]), dskE=jnp.repeat(d_skip[l], SSM_HD).reshape(1, SSM_W),
            sng=row(ssm_norm_g[l]), cb=row(conv_b[l])))
    convw_all = jnp.transpose(gath[1], (1, 2, 0, 3)).reshape(L, 4, CONV_CH)
    convw8 = jnp.pad(convw_all, ((0, 0), (0, 4), (0, 0)))

    saved = []
    xl = x
    h = _norm_fwd(xl, row(norm1_g[0]), mods[0][1], mods[0][0], "norm1_f0")
    for l in range(L):
        sm = small[l]
        if l == 0:
            g_in = gathered("w_in", 0, h)
            tok = start_next(g_in)
        win_g[l] = _shards_to_cols(g_in, f"w_in_cols{l}")
        P = _mm(h.reshape(T, D), win_g[l], mode="nn", name=f"proj_in{l}", tn=1536, order=tok).reshape(B, S, PW)
        cat = _gmlp_fwd(P, sm["lng"], sm["lnb"], sm["wt"], sm["bsx"], sm["gog"], f"gmlp_f{l}")
        cat = _attn_fwd(P, sm["sinks"], sm["aog"], cat, f"attn_f{l}")
        xact = _conv_fwd(P, convw8[l], sm["cb"], f"conv_f{l}")
        w1_g[l] = gathered("w_mlp1", l, xact)
        tok = start_next(w1_g[l])
        cat, sprev = _ssd_fwd(xact, P, sm["bias"], sm["alog"], sm["dskE"], sm["sng"] + tok[0:1, 0:1], cat, f"ssd_f{l}")
        g_out = gathered("w_out", l, cat)
        tok = start_next(g_out)
        wout_g[l] = g_out.reshape(D, D)
        mix = _mm(cat.reshape(T, D), wout_g[l], mode="nn", name=f"proj_out{l}", order=tok).reshape(B, S, D)
        x_mid, h2 = _norm_fwd(xl, row(norm2_g[l]), mods[l][4], mods[l][3], f"norm2_f{l}", resid=(mix, mods[l][2]))
        a_act, r_act = _mm(h2.reshape(T, D), w1_g[l], mode="nn", name=f"mlp1_{l}", out_dtypes=(BF16, BF16), col_blocked_b=True,
                           epilogue=lambda acc: (acc, jnp.square(jnp.maximum(acc, 0.0))))
        g_2 = gathered("w_mlp2", l, r_act)
        tok = start_next(g_2)
        w2_g[l] = g_2.reshape(DFF, D)
        m2 = _mm(r_act, w2_g[l], mode="nn", name=f"mlp2_{l}", order=tok, tk=4096).reshape(B, S, D)
        saved.append(dict(x_in=xl, h=h, P=P, xact=xact, sprev=sprev, cat=cat, mix=mix, x_mid=x_mid, h2=h2, a=a_act, r=r_act, m2=m2))
        if l + 1 < L:
            g_in = gathered("w_in", l + 1, m2)
            tok = start_next(g_in)
            xl, h = _norm_fwd(x_mid, row(norm1_g[l + 1]) + tok[0, 0], mods[l + 1][1], mods[l + 1][0], f"norm1_f{l + 1}",
                              resid=(m2, mods[l][5]))

    sv = saved[L - 1]
    nb = _norm_bwd(sv["x_mid"], row(final_norm_g), "final_b", tgt=loss_target, br=sv["m2"], gate=mods[L - 1][5], x_is_prev=True)
    loss_part, g_final = nb["loss"], nb["dg"]
    dmod, gsm, gconvw = [None] * L, [None] * L, [None] * L
    core = lax.axis_index("c").astype(jnp.int32).reshape(1)
    reducing = []

    def reduce_start(n, l, sent, after):
        p, from_sib = _pair_wait(*sent[:3], after, f"rs_pair_wait_{n}{l}")
        s, land = _pair_add(p, from_sib, core, f"rs_add_{n}{l}")
        return reduce_exchange(n, l, s, land, after)

    def reduce_exchange(n, l, s, land, order):
        sems, s_thru, land_thru, token = _chipsum_start(s, land, order, f"rs_start_{n}{l}")
        reducing.append((n, l, sems, s_thru, land_thru))
        return token

    other = 1 - core

    for l in reversed(range(L)):
        sv, sm = saved[l], small[l]
        dm2, dxo, dg2 = nb["dbr"].reshape(T, D), nb["dx"], nb["dgate"]
        da = _mm(dm2, w2_g[l], mode="nt", name=f"mlp2_dx{l}", out_dtypes=(BF16,), extras=(sv["a"],),
                 epilogue=lambda acc, a: (acc * (2.0 * jnp.maximum(a.astype(F32), 0.0)),))
        h2f = sv["h2"].reshape(T, D)
        sent2 = _sibling_start(_dw_half(sv["r"], dm2, other, axis="m", name=f"mlp2_dw_sib{l}"), da, f"rs_sib_start_w_mlp2{l}")
        dh2 = _mm(da, w1_g[l], mode="nt", name=f"mlp1_dx{l}", col_blocked_b=True, order=sent2[3]).reshape(B, S, D)
        from_sib = _sibling_wait(*sent2[:3], dh2, f"rs_sib_wait_w_mlp2{l}")
        sent1 = _sibling_start(_dw_half(h2f, da, other, axis="n", name=f"mlp1_dw_sib{l}", order=from_sib), da,
                               f"rs_sib_start_w_mlp1{l}")
        s2, land2 = _dw_half(sv["r"], dm2, core, axis="m", name=f"mlp2_dw_own{l}", add=from_sib, order=sent1[3])
        tok = reduce_exchange("w_mlp2", l, s2, land2, da)
        nb2 = _norm_bwd(sv["x_mid"], row(norm2_g[l]) + tok[0, 0], f"norm2_b{l}", sc=mods[l][4], dh=dh2, dres=dxo, br=sv["mix"],
                        gate=mods[l][2])
        dmix = nb2["dbr"].reshape(T, D)
        from_sib = _sibling_wait(*sent1[:3], dmix, f"rs_sib_wait_w_mlp1{l}")
        s1, land1 = _dw_half(h2f, da, core, axis="n", name=f"mlp1_dw_own{l}", add=from_sib)
        tok = reduce_exchange("w_mlp1", l, s1, land1, dmix)
        dcat = _mm(dmix, wout_g[l], mode="nt", name=f"proj_out_dx{l}", order=tok).reshape(B, S, D)
        du, dv, dlng, dlnb, dws, dbsx, dgog = _gmlp_bwd(sv["P"], dcat, sm["lng"], sm["lnb"], sm["wt"], sm["wtT"], sm["bsx"],
                                                        sm["gog"], f"gmlp_b{l}")
        dq, dk, dvv, dsink, daog = _attn_bwd(sv["P"], dcat, sm["sinks"], sm["aog"], f"attn_b{l}")
        dwo = _mm(sv["cat"].reshape(T, D), dmix, mode="tn", name=f"proj_out_dw{l}", out_dtypes=(BF16,), tk=2048,
                  order=dq).reshape(4, 2, D // NDEV, D)
        sent = _pair_start(dwo, dmix, f"rs_pair_start_w_out{l}")
        dxa, dz, ddt, dbias, dalog, ddsk, dsng = _ssd_bwd(sv["xact"], sv["P"], sv["sprev"], dcat, sm["bias"], sm["alog"],
                                                          sm["dskE"], sm["sng"] + sent[3][0:1, 0:1], f"ssd_b{l}")
        tok = reduce_start("w_out", l, sent, dxa)
        dxbc, dcw, dcb = _conv_bwd(sv["P"], dxa, convw8[l], sm["cb"] + tok[0:1, 0:1], f"conv_b{l}")
        dP = jnp.concatenate([du, dv, dq, dk, dvv, dz, dxbc, ddt, jnp.zeros((B, S, PW - OFF["dt"] - 128), BF16)],
                             axis=-1).reshape(T, PW)
        dwin = _mm(sv["h"].reshape(T, D), dP, mode="tn", name=f"proj_in_dw{l}", out_dtypes=(BF16,), tn=1536, tk=2048)
        dwin = _cols_to_shards(dwin, f"w_in_dshards{l}").reshape(4, 2, D, IN_W // NDEV)
        sent = _pair_start(dwin, dP, f"rs_pair_start_w_in{l}")
        dh = _mm(dP, win_g[l], mode="nt", name=f"proj_in_dx{l}", tk=2304, order=sent[3]).reshape(B, S, D)
        tok = reduce_start("w_in", l, sent, dh)
        nb = _norm_bwd(sv["x_in"], row(norm1_g[l]) + tok[0, 0], f"norm1_b{l}", sc=mods[l][1], dh=dh, dres=nb2["dx"],
                       br=saved[l - 1]["m2"] if l > 0 else None, gate=mods[l - 1][5] if l > 0 else None)
        dmod[l] = jnp.concatenate([nb["dsh"], nb["dsc"], nb2["dgate"], nb2["dsh"], nb2["dsc"], dg2], axis=-1)
        gconvw[l] = dcw[:4]
        gsm[l] = dict(
            ada_b=jnp.sum(dmod[l], axis=(0, 1)), norm1_g=nb["dg"], gm_ln_g=dlng, gm_ln_b=dlnb, gm_ws=dws,
            gm_bs=dbsx.reshape(128, GM_H, 128).sum(-1).T, gm_norm_g=dgog, attn_sinks=dsink[:, 0], attn_norm_g=daog,
            conv_b=dcb, dt_bias=dbias[0, :SSM_H], a_log=dalog[0, :SSM_H], d_skip=ddsk.reshape(SSM_H, SSM_HD).sum(-1),
            ssm_norm_g=dsng, norm2_g=nb2["dg"])
    grad_x = nb["dx"]

    big_res, after = dict.fromkeys(big), grad_x
    tile_rows = dict(w_in=256, w_out=256, w_mlp1=256, w_mlp2=128)

    def finish_reduce(n, l, sems, s_thru, land_thru, after):
        parts = _chipsum_wait(sems, s_thru, land_thru, after, f"rs_wait_{n}{l}")
        w = args[n]
        big_res[n] = _adamw(w.reshape(-1, w.shape[-1]), parts, args["m_" + n].reshape(-1, w.shape[-1]),
                            args["v_" + n].reshape(-1, w.shape[-1]), f"adamw_{n}{l}", tr=tile_rows[n], row0=l * w.shape[1],
                            prev=big_res[n])
        return big_res[n][0]

    for item in reducing[:-1]:
        after = finish_reduce(*item, after)

    per_layer = [n for n in _SMALL if n != "final_norm_g"]
    g_small = [jnp.stack([gsm[l][n].reshape(args[n].shape[1:]) for l in range(L)]) for n in per_layer] + [g_final.reshape(D)]
    zc = jnp.zeros((L, 4, CONV_CH), F32)
    z1 = jnp.zeros((1, 128), F32)
    gpack = _pack([loss_part] + g_small + [jnp.stack(gconvw)])
    got = _gather2([jnp.stack(dmod).reshape(L, B, 6 * D), gpack], "ag_small", order=after)
    like = [z1] + [args[n] for n in _SMALL] + [zc]
    packs = [_pack([z1] + [args[p + n] for n in _SMALL] + [zc]) for p in ("", "m_", "v_")]
    sres = [_unpack(p, like) for p in _adamw(packs[0], got[1], packs[1], packs[2], "adamw_small", tr=gpack.shape[0])]
    res = {n: [r[1 + i] for r in sres] for i, n in enumerate(_SMALL)}
    loss = sres[0][0][0, 0]
    gcw = lax.dynamic_slice_in_dim(sres[0][-1], me * (CONV_CH // NDEV), CONV_CH // NDEV, axis=2)

    def update(name, parts, tr):
        w = args[name]
        r = _adamw(w.reshape(-1, w.shape[-1]), parts, args["m_" + name].reshape(-1, w.shape[-1]),
                   args["v_" + name].reshape(-1, w.shape[-1]), "adamw_" + name, tr=tr)
        res[name] = [a.reshape(w.shape) for a in r]

    update("conv_w", gcw.reshape(1, L * 4, CONV_CH // NDEV), L * 4)

    dmod_all = jnp.transpose(got[0], (1, 0, 2, 3)).reshape(L, NDEV * B, 6 * D)
    dm_mine = lax.dynamic_slice_in_dim(dmod_all, me * (6 * D // NDEV), 6 * D // NDEV, axis=2)
    dm_pad = jnp.pad(dm_mine, ((0, 0), (0, 128 - nb_rows), (0, 0))).astype(BF16)
    g_adaw = jnp.stack([_mm(c_pad, dm_pad[l], mode="tn", name=f"ada_dw{l}", tn=768) for l in range(L)])
    update("ada_w", g_adaw.reshape(1, L * D, 6 * D // NDEV), 256)

    finish_reduce(*reducing[-1], res["ada_w"][0])
    for n in big:
        res[n] = [a.reshape(args[n].shape) for a in big_res[n]]

    names = ['ada_w', 'ada_b', 'norm1_g', 'w_in', 'gm_ln_g', 'gm_ln_b', 'gm_ws', 'gm_bs', 'gm_norm_g', 'attn_sinks',
             'attn_norm_g', 'conv_w', 'conv_b', 'dt_bias', 'a_log', 'd_skip', 'ssm_norm_g', 'w_out', 'norm2_g', 'w_mlp1',
             'w_mlp2', 'final_norm_g']
    return (loss, grad_x, *[res[n][0] for n in names], *[res[n][1] for n in names], *[res[n][2] for n in names],
            *[res[n][3] for n in names])
```

```python
import functools

import jax
import jax.numpy as jnp
import numpy as np
from jax import lax
from jax.experimental import pallas as pl
from jax.experimental.pallas import tpu as pltpu

F32, BF16 = jnp.float32, jnp.bfloat16
HI = lax.Precision.HIGHEST
MESH = pl.DeviceIdType.MESH
NDEV = 8

D = 2048
DEPTH = 2
CHUNK = 128
GM_W, GM_H = 512, 4
ATT_W, KV_W, ATT_H = 512, 128, 8
SSM_W, SSM_H, SSM_HD, SSM_G = 1024, 16, 64, 2
CONV_CH = 1536
IN_W = 4368
DFF = 8192
EPS = 1e-6
NEG_INF = -1e30
GELU_K = 0.7978845608028654
GELU_C = 0.044715

_ORIG = (("u", 512), ("v", 512), ("q", 512), ("k", 128), ("vv", 128), ("z", 1024), ("xbc", 1536), ("dt", 16))
OFF = dict(u=0, v=512, q=1024, k=1536, vv=1664, z=1792, xbc=2816, dt=4352)
PW = 4608

ADAM_LR, ADAM_B1, ADAM_B2, ADAM_EPS, ADAM_WD, ADAM_STEP = 0.001, 0.9, 0.999, 1e-08, 0.01, 10


def _shards_to_cols(g, name, tr=256):
    n, R, C = g.shape

    def body(g_ref, o_ref):
        o_ref[...] = jnp.concatenate([g_ref[s] for s in range(n)] + [jnp.zeros((tr, PW - n * C), g.dtype)], axis=1)

    return pl.pallas_call(body, name=name, grid=(R // tr,), in_specs=[pl.BlockSpec((n, tr, C), lambda i: (0, i, 0))],
                          out_specs=pl.BlockSpec((tr, PW), lambda i: (i, 0)), out_shape=jax.ShapeDtypeStruct((R, PW), g.dtype))(g)


def _cols_to_my_shards(w, w_sib, core, name, tr=256):
    R, C = w.shape[0], IN_W // NDEV

    def body(core_ref, w_ref, s_ref, o_ref, o2_ref):
        x = w_ref[...].astype(F32) + s_ref[...].astype(F32)
        mine_is_odd = core_ref[0] == 1
        for q in range(4):
            blk = jnp.where(mine_is_odd, x[:, C * (2 * q + 1):C * (2 * q + 2)], x[:, C * 2 * q:C * (2 * q + 1)]).astype(o_ref.dtype)
            o_ref[q] = blk
            o2_ref[q] = blk

    row = pl.BlockSpec((tr, PW), lambda i, c: (i, 0))
    out = pl.BlockSpec((4, tr, C), lambda i, c: (0, i, 0))
    return pl.pallas_call(
        body, name=name, out_shape=[jax.ShapeDtypeStruct((4, R, C), w.dtype)] * 2,
        grid_spec=pltpu.PrefetchScalarGridSpec(num_scalar_prefetch=1, grid=(R // tr,), in_specs=[row, row], out_specs=[out, out]),
    )(core, w, w_sib)


def _sigmoid(x):
    return 1.0 / (1.0 + jnp.exp(-x))


def _gelu(x):
    return 0.5 * x * (1.0 + jnp.tanh(GELU_K * (x + GELU_C * x * x * x)))


def _gelu_grad(x):
    t = jnp.tanh(GELU_K * (x + GELU_C * x * x * x))
    return 0.5 * (1.0 + t) + 0.5 * x * (1.0 - t * t) * GELU_K * (1.0 + 3.0 * GELU_C * x * x)


def _dot(a, b, prec=None):
    return jnp.dot(a, b, precision=prec, preferred_element_type=F32)


def _dot_nt(a, b, prec=None):
    return lax.dot_general(a, b, (((1,), (1,)), ((), ())), precision=prec, preferred_element_type=F32)


def _dot_tn(a, b, prec=None):
    return lax.dot_general(a, b, (((0,), (0,)), ((), ())), precision=prec, preferred_element_type=F32)


def _full(shape):
    return pl.BlockSpec(shape, lambda *_: (0,) * len(shape))


_HBM = pl.BlockSpec(memory_space=pltpu.HBM)


def _me():
    return lax.axis_index("x"), lax.axis_index("y"), lax.axis_index("c")


def _peer(k):
    x, y, c = _me()
    px = 1 - x if k & 4 else x
    py = 1 - y if k & 2 else y
    pc = 1 - c if k & 1 else c
    return (px, py, pc), 4 * px + 2 * py + pc


def _gather_small(xs, name, order=()):
    n = len(xs)

    def body(*refs):
        ins, outs = refs[:n], refs[-n - 3:-3]
        send, recv, loc = refs[-3:]
        x, y, c = _me()
        me = 4 * x + 2 * y + c
        started = []
        for i in range(n):
            own = pltpu.make_async_copy(ins[i], outs[i].at[me], loc.at[i])
            own.start()
            started.append(own)
        for k in range(1, NDEV):
            dev, lin = _peer(k)
            for i in range(n):
                pltpu.make_async_remote_copy(
                    src_ref=ins[i], dst_ref=outs[i].at[me],
                    send_sem=send.at[i, k - 1], recv_sem=recv.at[i, k - 1], device_id=dev, device_id_type=MESH).start()
        for k in range(1, NDEV):
            dev, lin = _peer(k)
            for i in range(n):
                pltpu.make_async_remote_copy(
                    src_ref=ins[i], dst_ref=outs[i].at[lin],
                    send_sem=send.at[i, k - 1], recv_sem=recv.at[i, k - 1], device_id=dev, device_id_type=MESH).wait()
        for own in started:
            own.wait()

    extra = list(order)
    return pl.pallas_call(
        body, name=name, out_shape=[jax.ShapeDtypeStruct((NDEV,) + a.shape, a.dtype) for a in xs],
        in_specs=[_HBM] * n + [pl.BlockSpec(memory_space=pl.ANY)] * len(extra), out_specs=[_HBM] * n,
        scratch_shapes=[pltpu.SemaphoreType.DMA((n, NDEV - 1)), pltpu.SemaphoreType.DMA((n, NDEV - 1)),
                        pltpu.SemaphoreType.DMA((n,))],
        compiler_params=pltpu.CompilerParams(has_side_effects=True),
    )(*xs, *extra)


def _chips():
    x, y, c = _me()
    return x, y, c, [(1 - x, y), (x, 1 - y), (1 - x, 1 - y)]


def _gather2(xs, name, order=None):
    n = len(xs)
    extra = [] if order is None else [order]

    def body(*refs):
        ins, outs = refs[:n], refs[-n - 3:-3]
        send, recv, loc = refs[-3:]
        x, y, c, chips = _chips()
        me, sib = (x, y, c), (x, y, 1 - c)

        def cp(i, k, block, to, src=None):
            slot = outs[i].at[4 * block[0] + 2 * block[1] + block[2]]
            return pltpu.make_async_remote_copy(src_ref=slot if src is None else src, dst_ref=slot, send_sem=send.at[i, k],
                                                recv_sem=recv.at[i, k], device_id=to, device_id_type=MESH)

        sent = []
        for i in range(n):
            for j, chip in enumerate(chips):
                sent.append(cp(i, 1 + j, me, (*chip, c), src=ins[i]))
            sent.append(cp(i, 0, me, sib, src=ins[i]))
        for s in sent:
            s.start()
        own = [pltpu.make_async_copy(ins[i], outs[i].at[4 * x + 2 * y + c], loc.at[i]) for i in range(n)]
        for o in own:
            o.start()
        for j, chip in enumerate(chips):
            for i in range(n):
                cp(i, 1 + j, (*chip, c), me).wait_recv()
                fwd = cp(i, 4 + j, (*chip, c), sib)
                fwd.start()
                sent.append(fwd)
        for i in range(n):
            cp(i, 0, sib, me).wait_recv()
            for j, chip in enumerate(chips):
                cp(i, 4 + j, (*chip, 1 - c), me).wait_recv()
        for s in sent:
            s.wait_send()
        for o in own:
            o.wait()

    return pl.pallas_call(
        body, name=name, out_shape=[jax.ShapeDtypeStruct((NDEV,) + a.shape, a.dtype) for a in xs],
        in_specs=[_HBM] * n + [pl.BlockSpec(memory_space=pl.ANY)] * len(extra), out_specs=[_HBM] * n,
        scratch_shapes=[pltpu.SemaphoreType.DMA((n, 7)), pltpu.SemaphoreType.DMA((n, 7)), pltpu.SemaphoreType.DMA((n,))],
        compiler_params=pltpu.CompilerParams(has_side_effects=True),
    )(*xs, *extra)


def _pair_add(p, r1, core, name, tr=256):
    _, _, R, C = p.shape
    tr = min(tr, R)

    def body(core_ref, p_ref, r_ref, o_ref, o2_ref):
        s = (p_ref[...].astype(F32) + r_ref[...].astype(F32)).astype(o_ref.dtype)
        o_ref[...] = s
        o2_ref[...] = s

    blk = pl.BlockSpec((None, tr, C), lambda ch, i, core_ref: (ch, i, 0))
    return pl.pallas_call(
        body, name=name, out_shape=[jax.ShapeDtypeStruct((4, R, C), p.dtype)] * 2,
        grid_spec=pltpu.PrefetchScalarGridSpec(
            num_scalar_prefetch=1, grid=(4, R // tr),
            in_specs=[pl.BlockSpec((None, None, tr, C), lambda ch, i, core_ref: (ch, core_ref[0], i, 0)), blk],
            out_specs=[blk, blk]),
    )(core, p, r1)


_SEM = pl.BlockSpec(memory_space=pltpu.SEMAPHORE)
_ANY = pl.BlockSpec(memory_space=pl.ANY)
_DATAFLOW = pltpu.SideEffectType.DATAFLOW_SIDE_EFFECTING


def _hbm(a):
    return pltpu.with_memory_space_constraint(a, pltpu.HBM)


def _gather_targets():
    x, y, c, chips = _chips()
    return 4 * x + 2 * y + c, [(x, y, 1 - c)] + [(*chip, c) for chip in chips]


def _landing_zone(w, l, me, name, tr=512):
    _, R, C = w.shape
    tr = min(tr, R)

    def body(me_ref, w_ref, o_ref):
        o_ref[...] = w_ref[...].astype(BF16)

    return pl.pallas_call(
        body, name=name, out_shape=jax.ShapeDtypeStruct((NDEV, R, C), BF16),
        grid_spec=pltpu.PrefetchScalarGridSpec(
            num_scalar_prefetch=1, grid=(R // tr,), in_specs=[pl.BlockSpec((None, tr, C), lambda i, me_ref: (l, i, 0))],
            out_specs=pl.BlockSpec((None, tr, C), lambda i, me_ref: (me_ref[0], i, 0))),
    )(me, w)


def _gather_start(land, order, name):
    def body(land_ref, order_ref, *rest):
        sems, token = rest[:8], rest[9]
        me, targets = _gather_targets()
        for k, to in enumerate(targets):
            pltpu.make_async_remote_copy(src_ref=land_ref.at[me], dst_ref=land_ref.at[me], send_sem=sems[k],
                                         recv_sem=sems[4 + k], device_id=to, device_id_type=MESH).start()
        token[...] = jnp.zeros_like(token)

    outs = pl.pallas_call(
        body, name=name,
        out_shape=(pltpu.SemaphoreType.DMA(()),) * 8 + (pltpu.HBM(land.shape, land.dtype), jax.ShapeDtypeStruct((8, 128), F32)),
        in_specs=(_HBM, _ANY), out_specs=(_SEM,) * 8 + (_HBM, pl.BlockSpec(memory_space=pltpu.VMEM)),
        input_output_aliases={0: 8}, compiler_params=pltpu.CompilerParams(has_side_effects=_DATAFLOW),
    )(_hbm(land), order)
    return outs[:8], outs[8], outs[9]


def _gather_wait(sems, land_thru, after, name):
    def body(land_ref, *rest):
        sems_ = rest[:8]
        me, targets = _gather_targets()
        for k, to in enumerate(targets):
            cp = pltpu.make_async_remote_copy(src_ref=land_ref.at[me], dst_ref=land_ref.at[me], send_sem=sems_[k],
                                              recv_sem=sems_[4 + k], device_id=to, device_id_type=MESH)
            cp.wait_send()
            cp.wait_recv()

    return pl.pallas_call(
        body, name=name, out_shape=pltpu.HBM(land_thru.shape, land_thru.dtype),
        in_specs=(_HBM,) + (_SEM,) * 8 + (_ANY,), out_specs=_HBM, input_output_aliases={0: 0},
        compiler_params=pltpu.CompilerParams(has_side_effects=_DATAFLOW),
    )(land_thru, *sems, after)


def _gather_finish(land, name):
    def body(land_ref, out, send, recv):
        x, y, c, chips = _chips()
        fwd = [pltpu.make_async_remote_copy(src_ref=out.at[4 * px + 2 * py + c], dst_ref=out.at[4 * px + 2 * py + c],
                                            send_sem=send.at[j], recv_sem=recv.at[j], device_id=(x, y, 1 - c), device_id_type=MESH)
               for j, (px, py) in enumerate(chips)]
        for cp in fwd:
            cp.start()
        for j, (px, py) in enumerate(chips):
            slot = out.at[4 * px + 2 * py + 1 - c]
            pltpu.make_async_remote_copy(src_ref=slot, dst_ref=slot, send_sem=send.at[j], recv_sem=recv.at[j],
                                         device_id=(x, y, 1 - c), device_id_type=MESH).wait()

    return pl.pallas_call(
        body, name=name, out_shape=jax.ShapeDtypeStruct(land.shape, land.dtype),
        in_specs=[_HBM], out_specs=_HBM, input_output_aliases={0: 0},
        scratch_shapes=[pltpu.SemaphoreType.DMA((3,)), pltpu.SemaphoreType.DMA((3,))],
        compiler_params=pltpu.CompilerParams(has_side_effects=True),
    )(land)


def _chip_targets():
    x, y, c, chips = _chips()
    return 2 * x + y, [((px, py, c), 2 * px + py) for px, py in chips]


def _chipsum_start(s, land, order, name):
    def body(s_ref, land_ref, order_ref, *rest):
        sems, token = rest[:6], rest[8]
        mine, targets = _chip_targets()
        for k, (to, ch) in enumerate(targets):
            pltpu.make_async_remote_copy(src_ref=s_ref.at[ch], dst_ref=land_ref.at[mine], send_sem=sems[k], recv_sem=sems[3 + k],
                                         device_id=to, device_id_type=MESH).start()
        token[...] = jnp.zeros_like(token)

    outs = pl.pallas_call(
        body, name=name,
        out_shape=(pltpu.SemaphoreType.DMA(()),) * 6 + (pltpu.HBM(s.shape, s.dtype), pltpu.HBM(land.shape, land.dtype),
                                                        jax.ShapeDtypeStruct((8, 128), F32)),
        in_specs=(_HBM, _HBM, _ANY), out_specs=(_SEM,) * 6 + (_HBM, _HBM, pl.BlockSpec(memory_space=pltpu.VMEM)),
        input_output_aliases={0: 6, 1: 7}, compiler_params=pltpu.CompilerParams(has_side_effects=_DATAFLOW),
    )(_hbm(s), _hbm(land), order)
    return outs[:6], outs[6], outs[7], outs[8]


def _chipsum_wait(sems, s_thru, land_thru, after, name):
    def body(s_ref, land_ref, *rest):
        sems_ = rest[:6]
        mine, targets = _chip_targets()
        for k, (to, ch) in enumerate(targets):
            cp = pltpu.make_async_remote_copy(src_ref=s_ref.at[ch], dst_ref=land_ref.at[ch], send_sem=sems_[k], recv_sem=sems_[3 + k],
                                              device_id=to, device_id_type=MESH)
            cp.wait_send()
            cp.wait_recv()

    return pl.pallas_call(
        body, name=name, out_shape=(pltpu.HBM(s_thru.shape, s_thru.dtype), pltpu.HBM(land_thru.shape, land_thru.dtype)),
        in_specs=(_HBM, _HBM) + (_SEM,) * 6 + (_ANY,), out_specs=(_HBM, _HBM), input_output_aliases={0: 0, 1: 1},
        compiler_params=pltpu.CompilerParams(has_side_effects=_DATAFLOW),
    )(s_thru, land_thru, *sems, after)[1]


def _pair_start(p, order, name):
    def body(p_ref, land_ref, order_ref, *rest):
        sems, token = rest[:8], rest[10]
        x, y, c = _me()
        for ch in range(4):
            pltpu.make_async_remote_copy(src_ref=p_ref.at[ch, 1 - c], dst_ref=land_ref.at[ch], send_sem=sems[ch],
                                         recv_sem=sems[4 + ch], device_id=(x, y, 1 - c), device_id_type=MESH).start()
        token[...] = jnp.zeros_like(token)

    land = lax.empty((4,) + p.shape[2:], p.dtype)
    outs = pl.pallas_call(
        body, name=name,
        out_shape=(pltpu.SemaphoreType.DMA(()),) * 8 + (pltpu.HBM(p.shape, p.dtype), pltpu.HBM(land.shape, land.dtype),
                                                        jax.ShapeDtypeStruct((8, 128), F32)),
        in_specs=(_HBM, _HBM, _ANY), out_specs=(_SEM,) * 8 + (_HBM, _HBM, pl.BlockSpec(memory_space=pltpu.VMEM)),
        input_output_aliases={0: 8, 1: 9}, compiler_params=pltpu.CompilerParams(has_side_effects=_DATAFLOW),
    )(_hbm(p), _hbm(land), order)
    return outs[:8], outs[8], outs[9], outs[10]


def _pair_wait(sems, p_thru, land_thru, after, name):
    def body(p_ref, land_ref, *rest):
        sems_ = rest[:8]
        x, y, c = _me()
        for ch in range(4):
            cp = pltpu.make_async_remote_copy(src_ref=p_ref.at[ch, 1 - c], dst_ref=land_ref.at[ch], send_sem=sems_[ch],
                                              recv_sem=sems_[4 + ch], device_id=(x, y, 1 - c), device_id_type=MESH)
            cp.wait_send()
            cp.wait_recv()

    return pl.pallas_call(
        body, name=name, out_shape=(pltpu.HBM(p_thru.shape, p_thru.dtype), pltpu.HBM(land_thru.shape, land_thru.dtype)),
        in_specs=(_HBM, _HBM) + (_SEM,) * 8 + (_ANY,), out_specs=(_HBM, _HBM), input_output_aliases={0: 0, 1: 1},
        compiler_params=pltpu.CompilerParams(has_side_effects=_DATAFLOW),
    )(p_thru, land_thru, *sems, after)


def _sibling_start(p, order, name):
    def body(p_ref, land_ref, order_ref, send_sem, recv_sem, p_thru, land_thru, token):
        x, y, c = _me()
        pltpu.make_async_remote_copy(src_ref=p_ref, dst_ref=land_ref, send_sem=send_sem, recv_sem=recv_sem,
                                     device_id=(x, y, 1 - c), device_id_type=MESH).start()
        token[...] = jnp.zeros_like(token)

    land = lax.empty(p.shape, p.dtype)
    outs = pl.pallas_call(
        body, name=name,
        out_shape=(pltpu.SemaphoreType.DMA(()),) * 2 + (pltpu.HBM(p.shape, p.dtype), pltpu.HBM(p.shape, p.dtype),
                                                        jax.ShapeDtypeStruct((8, 128), F32)),
        in_specs=(_HBM, _HBM, _ANY), out_specs=(_SEM,) * 2 + (_HBM, _HBM, pl.BlockSpec(memory_space=pltpu.VMEM)),
        input_output_aliases={0: 2, 1: 3}, compiler_params=pltpu.CompilerParams(has_side_effects=_DATAFLOW),
    )(_hbm(p), _hbm(land), order)
    return outs[:2], outs[2], outs[3], outs[4]


def _sibling_wait(sems, p_thru, land_thru, after, name):
    def body(p_ref, land_ref, send_sem, recv_sem, after_ref, p_dead, got_ref):
        x, y, c = _me()
        cp = pltpu.make_async_remote_copy(src_ref=p_ref, dst_ref=land_ref, send_sem=send_sem, recv_sem=recv_sem,
                                          device_id=(x, y, 1 - c), device_id_type=MESH)
        cp.wait_send()
        cp.wait_recv()

    return pl.pallas_call(
        body, name=name, out_shape=(pltpu.HBM(p_thru.shape, p_thru.dtype), pltpu.HBM(land_thru.shape, land_thru.dtype)),
        in_specs=(_HBM, _HBM, _SEM, _SEM, _ANY), out_specs=(_HBM, _HBM), input_output_aliases={0: 0, 1: 1},
        compiler_params=pltpu.CompilerParams(has_side_effects=_DATAFLOW),
    )(p_thru, land_thru, *sems, after)


def _mm(a, b, *, mode, name, out_dtypes=(F32,), epilogue=None, extras=(), tm=1024, tn=1024, tk=2048,
        col_blocked_b=False, col_blocked_out=False, order=None):
    CB = 1024
    if col_blocked_b:
        assert mode in ("nn", "nt") and b.shape[2] == CB
        (M, K), N = a.shape, (b.shape[0] * CB if mode == "nn" else b.shape[1])
        assert mode == "nn" or tk % CB == 0
        tn = CB if mode == "nn" else tn
    elif mode == "nn":
        (M, K), N = a.shape, b.shape[1]
    elif mode == "nt":
        (M, K), N = a.shape, b.shape[0]
    else:
        (K, M), N = a.shape, b.shape[1]
    if col_blocked_out:
        assert len(out_dtypes) == 1 and N % CB == 0
        tn = CB
    tm, tn, tk = min(tm, M), min(tn, N), min(tk, K)
    assert M % tm == 0 and N % tn == 0 and K % tk == 0, (M, N, K, tm, tn, tk)
    nk = K // tk
    ne, no = len(extras), len(out_dtypes)
    dims = {"nn": (((1,), (0,)), ((), ())), "nt": (((1,), (1,)), ((), ())), "tn": (((0,), (0,)), ((), ()))}[mode]

    no_ = 0 if order is None else 1

    def body(a_ref, b_ref, *rest):
        rest = rest[no_:]
        ex, outs = rest[:ne], rest[ne:ne + no]

        def finish(acc):
            res = epilogue(acc, *[e[...] for e in ex]) if epilogue is not None else (acc,)
            for o, r in zip(outs, res):
                o[...] = r.astype(o.dtype)

        if col_blocked_b and mode == "nt":
            part = sum(lax.dot_general(a_ref[:, q * CB:(q + 1) * CB], b_ref[q], dims, preferred_element_type=F32)
                       for q in range(tk // CB))
        else:
            part = lax.dot_general(a_ref[...], b_ref[...], dims, preferred_element_type=F32)
        if nk == 1:
            finish(part)
        else:
            acc_ref = rest[-1]
            k = pl.program_id(2)

            @pl.when(k == 0)
            def _():
                acc_ref[...] = part

            @pl.when(k > 0)
            def _():
                acc_ref[...] += part

            @pl.when(k == nk - 1)
            def _():
                finish(acc_ref[...])

    a_spec = {"nn": pl.BlockSpec((tm, tk), lambda i, j, k: (i, k)), "nt": pl.BlockSpec((tm, tk), lambda i, j, k: (i, k)),
              "tn": pl.BlockSpec((tk, tm), lambda i, j, k: (k, i))}[mode]
    b_spec = {"nn": pl.BlockSpec((tk, tn), lambda i, j, k: (k, j)), "nt": pl.BlockSpec((tn, tk), lambda i, j, k: (j, k)),
              "tn": pl.BlockSpec((tk, tn), lambda i, j, k: (k, j))}[mode]
    if col_blocked_b:
        b_spec = (pl.BlockSpec((None, tk, CB), lambda i, j, k: (j, k, 0)) if mode == "nn"
                  else pl.BlockSpec((tk // CB, tn, CB), lambda i, j, k: (k, j, 0)))
    e_spec = pl.BlockSpec((tm, tn), lambda i, j, k: (i, j))
    o_spec, o_dims = e_spec, (M, N)
    if col_blocked_out:
        o_spec, o_dims = pl.BlockSpec((None, tm, CB), lambda i, j, k: (j, i, 0)), (N // CB, M, CB)
    outs = pl.pallas_call(
        body, name=name, grid=(M // tm, N // tn, nk),
        in_specs=[a_spec, b_spec] + [_ANY] * no_ + [e_spec] * ne, out_specs=[o_spec] * no,
        out_shape=[jax.ShapeDtypeStruct(o_dims, dt) for dt in out_dtypes],
        scratch_shapes=[pltpu.VMEM((tm, tn), F32)] if nk > 1 else [],
        compiler_params=pltpu.CompilerParams(dimension_semantics=("parallel", "parallel", "arbitrary")),
    )(a, b, *([] if order is None else [order]), *extras)
    return outs if no > 1 else outs[0]


def _dw_half(a, b, side, *, axis, name, add=None, order=None, tile=1024, tk=2048):
    (K, M), N = a.shape, b.shape[1]
    tk = min(tk, K)
    nk = K // tk
    if axis == "m":
        tm, tn = tile, min(N, 1024)
        grid, o_dims = (4, N // tn, nk), (4, tile, N)
        a_spec = pl.BlockSpec((tk, tm), lambda q, j, k, s: (k, 2 * q + s[0]))
        b_spec = pl.BlockSpec((tk, tn), lambda q, j, k, s: (k, j))
        o_spec = pl.BlockSpec((None, tm, tn), lambda q, j, k, s: (q, 0, j))
    else:
        tm, tn = min(M, 1024), tile
        grid, o_dims = (M // tm, 4, nk), (4, M, tile)
        a_spec = pl.BlockSpec((tk, tm), lambda i, q, k, s: (k, i))
        b_spec = pl.BlockSpec((tk, tn), lambda i, q, k, s: (k, 2 * q + s[0]))
        o_spec = pl.BlockSpec((None, tm, tn), lambda i, q, k, s: (q, i, 0))
    n_order, n_add = int(order is not None), int(add is not None)
    n_out = 1 + n_add

    def body(s_ref, a_ref, b_ref, *rest):
        rest = rest[n_order:]
        outs, acc_ref = rest[n_add:n_add + n_out], rest[-1]
        k = pl.program_id(2)
        part = _dot_tn(a_ref[...], b_ref[...])

        @pl.when(k == 0)
        def _():
            acc_ref[...] = part

        @pl.when(k > 0)
        def _():
            acc_ref[...] += part

        @pl.when(k == nk - 1)
        def _():
            res = acc_ref[...] + rest[0][...].astype(F32) if n_add else acc_ref[...]
            for o in outs:
                o[...] = res.astype(o.dtype)

    outs = pl.pallas_call(
        body, name=name, out_shape=[jax.ShapeDtypeStruct(o_dims, BF16)] * n_out,
        grid_spec=pltpu.PrefetchScalarGridSpec(
            num_scalar_prefetch=1, grid=grid, in_specs=[a_spec, b_spec] + [_ANY] * n_order + [o_spec] * n_add,
            out_specs=[o_spec] * n_out, scratch_shapes=[pltpu.VMEM((tm, tn), F32)]),
        compiler_params=pltpu.CompilerParams(dimension_semantics=("arbitrary", "arbitrary", "arbitrary")),
    )(side, a, b, *([order] if n_order else []), *([add] if n_add else []))
    return outs if n_add else outs[0]


def _norm_fwd(x, g, sc, sh, name, resid=None):
    B, S, Dm = x.shape
    ts = min(S, 256)
    tok = pl.BlockSpec((None, ts, Dm), lambda b, i: (b, i, 0))
    row = pl.BlockSpec((None, 1, Dm), lambda b, i: (b, 0, 0))
    par = pl.BlockSpec((1, Dm), lambda b, i: (0, 0))

    def body(*refs):
        if resid is not None:
            x_ref, br_ref, gt_ref, g_ref, sc_ref, sh_ref, xo_ref, h_ref = refs
            xv = x_ref[...] + gt_ref[...] * br_ref[...]
            xo_ref[...] = xv
        else:
            x_ref, g_ref, sc_ref, sh_ref, h_ref = refs
            xv = x_ref[...]
        r = lax.rsqrt(jnp.mean(xv * xv, axis=-1, keepdims=True) + EPS)
        h_ref[...] = ((xv * r * g_ref[...]) * (1.0 + sc_ref[...]) + sh_ref[...]).astype(BF16)

    h_shape = jax.ShapeDtypeStruct((B, S, Dm), BF16)
    if resid is not None:
        return pl.pallas_call(body, name=name, grid=(B, S // ts), in_specs=[tok, tok, row, par, row, row],
                              out_specs=[tok, tok], out_shape=[jax.ShapeDtypeStruct((B, S, Dm), F32), h_shape],
                              )(x, resid[0], resid[1], g, sc, sh)
    return pl.pallas_call(body, name=name, grid=(B, S // ts), in_specs=[tok, par, row, row], out_specs=tok,
                          out_shape=h_shape)(x, g, sc, sh)


def _norm_bwd(x, g, name, *, sc=None, dh=None, dres=None, tgt=None, br=None, gate=None, x_is_prev=False):
    B, S, Dm = x.shape
    ts = min(S, 256)
    final = tgt is not None
    has_br = br is not None
    tok = pl.BlockSpec((None, ts, Dm), lambda b, i: (b, i, 0))
    row = pl.BlockSpec((None, 1, Dm), lambda b, i: (b, 0, 0))
    par = pl.BlockSpec((1, Dm), lambda b, i: (0, 0))
    ins, in_specs = [x, g], [tok, par]
    if final:
        ins, in_specs = ins + [tgt], in_specs + [tok]
    else:
        ins, in_specs = ins + [sc, dh], in_specs + [row, tok]
    if dres is not None:
        ins, in_specs = ins + [dres], in_specs + [tok]
    if has_br:
        ins, in_specs = ins + [br, gate], in_specs + [tok, row]
    n_in = len(ins)
    out_shape = [jax.ShapeDtypeStruct((B, S, Dm), F32), jax.ShapeDtypeStruct((1, Dm), F32)]
    out_specs = [tok, par]
    if final:
        out_shape.append(jax.ShapeDtypeStruct((1, 128), F32))
        out_specs.append(pl.BlockSpec((1, 128), lambda b, i: (0, 0)))
    else:
        out_shape += [jax.ShapeDtypeStruct((B, 1, Dm), F32)] * 2
        out_specs += [row, row]
    if has_br:
        out_shape += [jax.ShapeDtypeStruct((B, S, Dm), BF16), jax.ShapeDtypeStruct((B, 1, Dm), F32)]
        out_specs += [tok, row]

    def body(*refs):
        it = iter(refs[:n_in])
        outs = iter(refs[n_in:])
        x_ref, g_ref = next(it), next(it)
        b, i = pl.program_id(0), pl.program_id(1)
        first, first_row = (b == 0) & (i == 0), i == 0
        xv, gv = x_ref[...], g_ref[...]
        if x_is_prev:
            xv = xv + refs[n_in - 1][...] * refs[n_in - 2][...]
        r = lax.rsqrt(jnp.mean(xv * xv, axis=-1, keepdims=True) + EPS)
        n = xv * r
        dx_ref, dg_ref = next(outs), next(outs)

        def acc(ref, val, init):
            @pl.when(init)
            def _():
                ref[...] = val

            @pl.when(jnp.logical_not(init))
            def _():
                ref[...] += val

        if final:
            t_ref = next(it)
            loss_ref = next(outs)
            e = n * gv - t_ref[...]
            acc(loss_ref, jnp.zeros((1, 128), F32) + 0.5 * jnp.sum(e * e) / Dm, first)
            dyg = e * (1.0 / Dm)
        else:
            sc_ref, dh_ref = next(it), next(it)
            dsc_ref, dsh_ref = next(outs), next(outs)
            dhv = dh_ref[...].astype(F32)
            acc(dsh_ref, jnp.sum(dhv, axis=0, keepdims=True), first_row)
            acc(dsc_ref, jnp.sum(dhv * (n * gv), axis=0, keepdims=True), first_row)
            dyg = dhv * (1.0 + sc_ref[...])
        acc(dg_ref, jnp.sum(dyg * n, axis=0, keepdims=True), first)
        dn = dyg * gv
        dx = r * (dn - n * jnp.mean(dn * n, axis=-1, keepdims=True))
        if dres is not None:
            dx = dx + next(it)[...]
        dx_ref[...] = dx
        if has_br:
            br_ref, gt_ref = next(it), next(it)
            dbr_ref, dgt_ref = next(outs), next(outs)
            dbr_ref[...] = (dx * gt_ref[...]).astype(BF16)
            acc(dgt_ref, jnp.sum(dx * br_ref[...], axis=0, keepdims=True), first_row)

    outs = pl.pallas_call(body, name=name, grid=(B, S // ts), in_specs=in_specs, out_specs=out_specs, out_shape=out_shape,
                          compiler_params=pltpu.CompilerParams(dimension_semantics=("arbitrary", "arbitrary")))(*ins)
    res = dict(dx=outs[0], dg=outs[1])
    if final:
        res["loss"] = outs[2]
    else:
        res["dsc"], res["dsh"] = outs[2], outs[3]
    if has_br:
        res["dbr"], res["dgate"] = outs[-2], outs[-1]
    return res


def _gm_heads(vg, lng, lnb):
    res = []
    for h in range(GM_H):
        sl = slice(h * 128, (h + 1) * 128)
        vh = vg[:, sl]
        xc = vh - jnp.mean(vh, axis=-1, keepdims=True)
        rstd = lax.rsqrt(jnp.mean(xc * xc, axis=-1, keepdims=True) + 1e-5)
        xhat = xc * rstd
        res.append((xhat, rstd, xhat * lng[:, sl] + lnb[:, sl]))
    return res


def _gm_gate(heads, wt_ref, bsx, nch):
    cols = []
    for h in range(GM_H):
        vn = heads[h][2].astype(BF16)
        rows = [_dot(wt_ref[h], vn[c * CHUNK:(c + 1) * CHUNK]) + bsx[:, h * 128:(h + 1) * 128] for c in range(nch)]
        cols.append(jnp.concatenate(rows, axis=0) if nch > 1 else rows[0])
    return jnp.concatenate(cols, axis=1)


def _gm_specs(S):
    tb = min(S, 512)
    u = pl.BlockSpec((None, tb, GM_W), lambda b, i: (b, i, OFF["u"] // GM_W))
    v = pl.BlockSpec((None, tb, GM_W), lambda b, i: (b, i, OFF["v"] // GM_W))
    tok = pl.BlockSpec((None, tb, GM_W), lambda b, i: (b, i, 0))
    return tb, u, v, tok


def _gmlp_fwd(P, lng, lnb, wt, bsx, og, name):
    B, S, _ = P.shape
    tb, u_spec, v_spec, tok = _gm_specs(S)
    nch = tb // CHUNK

    def body(u_ref, v_ref, lng_ref, lnb_ref, wt_ref, bsx_ref, og_ref, o_ref):
        heads = _gm_heads(_gelu(v_ref[...]), lng_ref[...], lnb_ref[...])
        y = _gelu(u_ref[...]) * _gm_gate(heads, wt_ref, bsx_ref[...], nch)
        r = lax.rsqrt(jnp.mean(y * y, axis=-1, keepdims=True) + EPS)
        o_ref[...] = (y * r * og_ref[...]).astype(BF16)

    return pl.pallas_call(
        body, name=name, grid=(B, S // tb),
        in_specs=[u_spec, v_spec, _full((1, GM_W)), _full((1, GM_W)), _full((GM_H, 128, 128)), _full((128, GM_W)), _full((1, GM_W))],
        out_specs=tok, out_shape=jax.ShapeDtypeStruct((B, S, GM_W + ATT_W + SSM_W), BF16))(P, P, lng, lnb, wt, bsx, og)


def _gmlp_bwd(P, dcat, lng, lnb, wt, wtT, bsx, og, name):
    B, S, _ = P.shape
    tb, u_spec, v_spec, tok = _gm_specs(S)
    nch = tb // CHUNK
    do_spec = pl.BlockSpec((None, tb, GM_W), lambda b, i: (b, i, 0))

    def body(u_ref, v_ref, do_ref, lng_ref, lnb_ref, wt_ref, wtT_ref, bsx_ref, og_ref,
             du_ref, dv_ref, dlng_ref, dlnb_ref, dws_ref, dbsx_ref, dog_ref):
        first = (pl.program_id(0) == 0) & (pl.program_id(1) == 0)

        @pl.when(first)
        def _():
            for ref in (dlng_ref, dlnb_ref, dws_ref, dbsx_ref, dog_ref):
                ref[...] = jnp.zeros(ref.shape, F32)

        u, v, lng = u_ref[...], v_ref[...], lng_ref[...]
        ug = _gelu(u)
        heads = _gm_heads(_gelu(v), lng, lnb_ref[...])
        gate = _gm_gate(heads, wt_ref, bsx_ref[...], nch)
        y = ug * gate
        r = lax.rsqrt(jnp.mean(y * y, axis=-1, keepdims=True) + EPS)
        yn = y * r
        dout = do_ref[...].astype(F32)
        dog_ref[...] += jnp.sum(dout * yn, axis=0, keepdims=True)
        dyn = dout * og_ref[...]
        dy = r * (dyn - yn * jnp.mean(dyn * yn, axis=-1, keepdims=True))
        du_ref[...] = (dy * gate * _gelu_grad(u)).astype(BF16)
        dgate = dy * ug
        tril = lax.broadcasted_iota(jnp.int32, (128, 128), 0) >= lax.broadcasted_iota(jnp.int32, (128, 128), 1)
        dvg = []
        for h in range(GM_H):
            sl = slice(h * 128, (h + 1) * 128)
            xhat, rstd, vn = heads[h]
            vnb = vn.astype(BF16)
            dgh = dgate[:, sl]
            dgb = dgh.astype(BF16)
            dbs = jnp.zeros((128, 128), F32)
            dw = jnp.zeros((128, 128), F32)
            dvn = []
            for c in range(nch):
                rs = slice(c * CHUNK, (c + 1) * CHUNK)
                dbs = dbs + dgh[rs]
                dw = dw + _dot_nt(dgb[rs], vnb[rs])
                dvn.append(_dot(wtT_ref[h], dgb[rs]))
            dvn = jnp.concatenate(dvn, axis=0) if nch > 1 else dvn[0]
            dbsx_ref[:, sl] += dbs
            dws_ref[h] += jnp.where(tril, dw, 0.0)
            dlng_ref[:, sl] += jnp.sum(dvn * xhat, axis=0, keepdims=True)
            dlnb_ref[:, sl] += jnp.sum(dvn, axis=0, keepdims=True)
            dxh = dvn * lng[:, sl]
            dvg.append(rstd * (dxh - jnp.mean(dxh, axis=-1, keepdims=True) - xhat * jnp.mean(dxh * xhat, axis=-1, keepdims=True)))
        dv_ref[...] = (jnp.concatenate(dvg, axis=1) * _gelu_grad(v)).astype(BF16)

    p512, w3 = _full((1, GM_W)), _full((GM_H, 128, 128))
    return pl.pallas_call(
        body, name=name, grid=(B, S // tb),
        in_specs=[u_spec, v_spec, do_spec, p512, p512, w3, w3, _full((128, GM_W)), p512],
        out_specs=[tok, tok, p512, p512, w3, _full((128, GM_W)), p512],
        out_shape=[jax.ShapeDtypeStruct((B, S, GM_W), BF16)] * 2 + [
            jax.ShapeDtypeStruct((1, GM_W), F32), jax.ShapeDtypeStruct((1, GM_W), F32),
            jax.ShapeDtypeStruct((GM_H, 128, 128), F32), jax.ShapeDtypeStruct((128, GM_W), F32),
            jax.ShapeDtypeStruct((1, GM_W), F32)],
        compiler_params=pltpu.CompilerParams(dimension_semantics=("arbitrary", "arbitrary")),
    )(P, P, dcat, lng, lnb, wt, wtT, bsx, og)


def _lane_half():
    return lax.broadcasted_iota(jnp.int32, (128, 128), 1) // 64


def _att_stack(x, kvh, dtype):
    half = _lane_half()
    rows = []
    for g in range(4):
        i = kvh * 4 + g
        pair = x[:, (i // 2) * 128:(i // 2 + 1) * 128]
        if i % 2 != kvh:
            pair = pltpu.roll(pair, 64, 1)
        rows.append(jnp.where(half == kvh, pair, 0.0))
    return jnp.concatenate(rows, axis=0).astype(dtype)


def _att_unstack(pairs, y, kvh):
    half = _lane_half()
    for g in range(4):
        i = kvh * 4 + g
        piece = y[g * 128:(g + 1) * 128]
        if i % 2 != kvh:
            piece = pltpu.roll(piece, 64, 1)
        pairs[i // 2] = jnp.where(half == i % 2, piece, pairs[i // 2])
    return pairs


def _att_probs(qb, k2, st, sink_ref, kvh):
    qm = _att_stack(qb, kvh, BF16)
    s = _dot_nt(qm, k2) * (64 ** -0.5)
    qi = lax.broadcasted_iota(jnp.int32, (512, 256), 0) % 128
    kj = lax.broadcasted_iota(jnp.int32, (512, 256), 1)
    diff = qi + 128 - kj
    valid = (diff >= 0) & (diff < 128) & (st + kj - 128 >= 0)
    s = jnp.where(valid, s, NEG_INF)
    grp = lax.broadcasted_iota(jnp.int32, (512, 1), 0) // 128
    sink = jnp.zeros((512, 1), F32)
    for g in range(4):
        sink = jnp.where(grp == g, sink_ref[kvh * 4 + g], sink)
    m = jnp.maximum(jnp.max(s, axis=-1, keepdims=True), sink)
    e = jnp.exp(s - m)
    esink = jnp.exp(sink - m)
    inv = 1.0 / (jnp.sum(e, axis=-1, keepdims=True) + esink)
    return qm, e * inv, esink * inv


def _att_specs(S):
    q = pl.BlockSpec((None, S, ATT_W), lambda b: (b, 0, OFF["q"] // ATT_W))
    k = pl.BlockSpec((None, S, KV_W), lambda b: (b, 0, OFF["k"] // KV_W))
    v = pl.BlockSpec((None, S, KV_W), lambda b: (b, 0, OFF["vv"] // KV_W))
    tok = pl.BlockSpec((None, S, ATT_W), lambda b: (b, 0, 0))
    kv = pl.BlockSpec((None, S, KV_W), lambda b: (b, 0, 0))
    return q, k, v, tok, kv


_SMEM = pl.BlockSpec(memory_space=pltpu.SMEM)


def _attn_fwd(P, sinks, og, cat, name):
    B, S, _ = P.shape
    q_spec, k_spec, v_spec, _, _ = _att_specs(S)
    tok = pl.BlockSpec((None, S, ATT_W), lambda b: (b, 0, GM_W // ATT_W))

    def body(q_ref, k_ref, v_ref, sink_ref, og_ref, cat_ref, o_ref, kpad, vpad):
        kpad[0:128, :] = jnp.zeros((128, KV_W), BF16)
        vpad[0:128, :] = jnp.zeros((128, KV_W), BF16)
        kpad[128:, :] = k_ref[...].astype(BF16)
        vpad[128:, :] = v_ref[...].astype(BF16)

        def step(n, carry):
            st = pl.multiple_of(n * 128, 128)
            qb = q_ref[pl.ds(st, 128), :]
            k2, v2 = kpad[pl.ds(st, 256), :], vpad[pl.ds(st, 256), :]
            pairs = [jnp.zeros((128, 128), F32)] * 4
            for kvh in range(2):
                _, p, _ = _att_probs(qb, k2, st, sink_ref, kvh)
                pairs = _att_unstack(pairs, _dot(p.astype(BF16), v2), kvh)
            o = jnp.concatenate(pairs, axis=1)
            r = lax.rsqrt(jnp.mean(o * o, axis=-1, keepdims=True) + EPS)
            o_ref[pl.ds(st, 128), :] = (o * r * og_ref[...]).astype(BF16)
            return carry

        lax.fori_loop(0, S // 128, step, 0)

    return pl.pallas_call(
        body, name=name, grid=(B,), in_specs=[q_spec, k_spec, v_spec, _SMEM, _full((1, ATT_W)), _ANY], out_specs=tok,
        out_shape=jax.ShapeDtypeStruct(cat.shape, BF16), input_output_aliases={5: 0},
        scratch_shapes=[pltpu.VMEM((S + 128, KV_W), BF16)] * 2)(P, P, P, sinks, og, cat)


def _attn_bwd(P, dcat, sinks, og, name):
    B, S, _ = P.shape
    q_spec, k_spec, v_spec, tok, kv = _att_specs(S)
    do_spec = pl.BlockSpec((None, S, ATT_W), lambda b: (b, 0, GM_W // ATT_W))

    def body(q_ref, k_ref, v_ref, do_ref, sink_ref, og_ref, dq_ref, dk_ref, dv_ref, dsink_ref, dog_ref,
             kpad, vpad, dkpad, dvpad):
        @pl.when(pl.program_id(0) == 0)
        def _():
            dsink_ref[...] = jnp.zeros((8, 128), F32)
            dog_ref[...] = jnp.zeros((1, ATT_W), F32)

        kpad[0:128, :] = jnp.zeros((128, KV_W), BF16)
        vpad[0:128, :] = jnp.zeros((128, KV_W), BF16)
        kpad[128:, :] = k_ref[...].astype(BF16)
        vpad[128:, :] = v_ref[...].astype(BF16)
        dkpad[...] = jnp.zeros((S + 128, KV_W), F32)
        dvpad[...] = jnp.zeros((S + 128, KV_W), F32)
        half = _lane_half()
        head_row = lax.broadcasted_iota(jnp.int32, (8, 128), 0)

        def step(n, carry):
            st = pl.multiple_of(n * 128, 128)
            qb = q_ref[pl.ds(st, 128), :]
            k2, v2 = kpad[pl.ds(st, 256), :], vpad[pl.ds(st, 256), :]
            saved, pairs = [], [jnp.zeros((128, 128), F32)] * 4
            for kvh in range(2):
                qm, p, psink = _att_probs(qb, k2, st, sink_ref, kvh)
                o = _dot(p.astype(BF16), v2)
                saved.append((qm, p, psink, o))
                pairs = _att_unstack(pairs, o, kvh)
            o = jnp.concatenate(pairs, axis=1)
            r = lax.rsqrt(jnp.mean(o * o, axis=-1, keepdims=True) + EPS)
            on = o * r
            dout = do_ref[pl.ds(st, 128), :].astype(F32)
            dog_ref[...] += jnp.sum(dout * on, axis=0, keepdims=True)
            dyn = dout * og_ref[...]
            do = r * (dyn - on * jnp.mean(dyn * on, axis=-1, keepdims=True))
            dq_pairs = [jnp.zeros((128, 128), F32)] * 4
            dsink = jnp.zeros((8, 128), F32)
            for kvh in range(2):
                qm, p, psink, og_ = saved[kvh]
                dog = _att_stack(do, kvh, F32)
                delta = jnp.sum(dog * jnp.where(jnp.concatenate([half] * 4, axis=0) == kvh, og_, 0.0), axis=-1, keepdims=True)
                dogb, pb = dog.astype(BF16), p.astype(BF16)
                dvpad[pl.ds(st, 256), :] += _dot_tn(pb, dogb)
                dp = _dot_nt(dogb, v2)
                ds = (p * (dp - delta) * (64 ** -0.5)).astype(BF16)
                sd = psink * delta
                for g in range(4):
                    dsink = dsink - jnp.where(head_row == kvh * 4 + g, jnp.sum(sd[g * 128:(g + 1) * 128]), 0.0)
                dq_pairs = _att_unstack(dq_pairs, _dot(ds, k2), kvh)
                dkpad[pl.ds(st, 256), :] += _dot_tn(ds, qm)
            dsink_ref[...] += dsink
            dq_ref[pl.ds(st, 128), :] = jnp.concatenate(dq_pairs, axis=1).astype(BF16)
            return carry

        lax.fori_loop(0, S // 128, step, 0)
        dk_ref[...] = dkpad[128:, :].astype(BF16)
        dv_ref[...] = dvpad[128:, :].astype(BF16)

    return pl.pallas_call(
        body, name=name, grid=(B,),
        in_specs=[q_spec, k_spec, v_spec, do_spec, _SMEM, _full((1, ATT_W))],
        out_specs=[tok, kv, kv, _full((8, 128)), _full((1, ATT_W))],
        out_shape=[jax.ShapeDtypeStruct((B, S, ATT_W), BF16), jax.ShapeDtypeStruct((B, S, KV_W), BF16),
                   jax.ShapeDtypeStruct((B, S, KV_W), BF16), jax.ShapeDtypeStruct((8, 128), F32),
                   jax.ShapeDtypeStruct((1, ATT_W), F32)],
        scratch_shapes=[pltpu.VMEM((S + 128, KV_W), BF16)] * 2 + [pltpu.VMEM((S + 128, KV_W), F32)] * 2,
        compiler_params=pltpu.CompilerParams(dimension_semantics=("arbitrary",)),
    )(P, P, P, dcat, sinks, og)


CONV_TC = 256
CONV_RC = 64


def _conv_taps(ext, r0):
    return [ext[pl.ds(r0 + 8 - k, CONV_RC), :] for k in range(4)]


def _conv_pre(taps, w_ref, b_ref):
    acc = b_ref[...] + w_ref[3:4, :] * taps[0]
    for k in range(1, 4):
        acc = acc + w_ref[3 - k:4 - k, :] * taps[k]
    return acc


def _conv_fwd(P, w8, b, name):
    B, S, _ = P.shape
    nj = CONV_CH // CONV_TC
    x_spec = pl.BlockSpec((None, S, CONV_TC), lambda b_, j: (b_, 0, OFF["xbc"] // CONV_TC + j))
    tok = pl.BlockSpec((None, S, CONV_TC), lambda b_, j: (b_, 0, j))

    def body(x_ref, w_ref, b_ref, o_ref, ext):
        ext[0:8, :] = jnp.zeros((8, CONV_TC), F32)
        ext[8:, :] = x_ref[...]
        for r0 in range(0, S, CONV_RC):
            pre = _conv_pre(_conv_taps(ext, r0), w_ref, b_ref)
            o_ref[pl.ds(r0, CONV_RC), :] = pre * _sigmoid(pre)

    return pl.pallas_call(
        body, name=name, grid=(B, nj),
        in_specs=[x_spec, pl.BlockSpec((8, CONV_TC), lambda b_, j: (0, j)), pl.BlockSpec((1, CONV_TC), lambda b_, j: (0, j))],
        out_specs=tok, out_shape=jax.ShapeDtypeStruct((B, S, CONV_CH), F32),
        scratch_shapes=[pltpu.VMEM((S + 8, CONV_TC), F32)])(P, w8, b)


def _conv_bwd(P, dact, w8, b, name):
    B, S, _ = P.shape
    nj = CONV_CH // CONV_TC
    x_spec = pl.BlockSpec((None, S, CONV_TC), lambda j, b_: (b_, 0, OFF["xbc"] // CONV_TC + j))
    tok = pl.BlockSpec((None, S, CONV_TC), lambda j, b_: (b_, 0, j))
    w_spec = pl.BlockSpec((8, CONV_TC), lambda j, b_: (0, j))
    b_spec = pl.BlockSpec((1, CONV_TC), lambda j, b_: (0, j))

    def body(x_ref, d_ref, w_ref, b_ref, dx_ref, dw_ref, db_ref, ext, extd):
        @pl.when(pl.program_id(1) == 0)
        def _():
            dw_ref[...] = jnp.zeros((8, CONV_TC), F32)
            db_ref[...] = jnp.zeros((1, CONV_TC), F32)

        ext[0:8, :] = jnp.zeros((8, CONV_TC), F32)
        ext[8:, :] = x_ref[...]
        extd[pl.ds(8 + S, 8), :] = jnp.zeros((8, CONV_TC), F32)
        db = jnp.zeros((1, CONV_TC), F32)
        dws = [jnp.zeros((1, CONV_TC), F32)] * 4
        for r0 in range(0, S, CONV_RC):
            taps = _conv_taps(ext, r0)
            pre = _conv_pre(taps, w_ref, b_ref)
            sg = _sigmoid(pre)
            dpre = d_ref[pl.ds(r0, CONV_RC), :] * (sg * (1.0 + pre * (1.0 - sg)))
            extd[pl.ds(8 + r0, CONV_RC), :] = dpre
            db = db + jnp.sum(dpre, axis=0, keepdims=True)
            dws = [dws[i] + jnp.sum(dpre * taps[3 - i], axis=0, keepdims=True) for i in range(4)]
        for r0 in range(0, S, CONV_RC):
            dx = w_ref[3:4, :] * extd[pl.ds(8 + r0, CONV_RC), :]
            for k in range(1, 4):
                dx = dx + w_ref[3 - k:4 - k, :] * extd[pl.ds(8 + r0 + k, CONV_RC), :]
            dx_ref[pl.ds(r0, CONV_RC), :] = dx.astype(BF16)
        db_ref[...] += db
        sub = lax.broadcasted_iota(jnp.int32, (8, CONV_TC), 0)
        dw_ref[...] += sum(jnp.where(sub == i, dws[i], 0.0) for i in range(4))

    return pl.pallas_call(
        body, name=name, grid=(nj, B), in_specs=[x_spec, tok, w_spec, b_spec], out_specs=[tok, w_spec, b_spec],
        out_shape=[jax.ShapeDtypeStruct((B, S, CONV_CH), BF16), jax.ShapeDtypeStruct((8, CONV_CH), F32),
                   jax.ShapeDtypeStruct((1, CONV_CH), F32)],
        scratch_shapes=[pltpu.VMEM((S + 8, CONV_TC), F32), pltpu.VMEM((S + 16, CONV_TC), F32)],
        compiler_params=pltpu.CompilerParams(dimension_semantics=("arbitrary", "arbitrary")),
    )(P, dact, w8, b)


def _ssd_consts():
    hd = np.arange(SSM_W) // SSM_HD
    E = (np.arange(128)[:, None] == hd[None, :]).astype(np.float32)
    tri = (np.arange(128)[:, None] >= np.arange(128)[None, :]).astype(np.float32)
    return jnp.asarray(E, BF16), jnp.asarray(E.T, BF16), jnp.asarray(tri, BF16), jnp.asarray(tri.T, BF16)


def _pieces(x, n):
    out, r = [], x
    for _ in range(n):
        p = r.astype(BF16)
        out.append(p)
        r = r - p.astype(F32)
    return out


def _dot01(x, m01, n):
    return sum(_dot(p, m01) for p in _pieces(x, n))


def _dot01_left(m01, x, n):
    return sum(_dot(m01, p) for p in _pieces(x, n))


def _ssd_pre(xa, dtraw, bias, alog, E, tri):
    lane = lax.broadcasted_iota(jnp.int32, (128, 128), 1)
    pre = dtraw + bias
    dtp = jnp.where(lane < SSM_H, jnp.maximum(pre, 0.0) + jnp.log(1.0 + jnp.exp(-jnp.abs(pre))), 0.0)
    a = -jnp.exp(alog)
    acs = _dot01_left(tri, dtp * a, 3)
    acsT = acs.T
    dtE, acsE = _dot01(dtp, E, 2), _dot01(acs, E, 3)
    X = xa[:, :SSM_W]
    xdt = X * dtE
    wE = jnp.exp(acsE[127:128, :] - acsE)
    eE = jnp.exp(acsE)
    cdE = eE[127:128, :]
    return dict(pre=pre, dtp=dtp, a=a, acs=acs, acsT=acsT, dtE=dtE, acsE=acsE, cdE=cdE, X=X, xdt=xdt, wE=wE, eE=eE)


def _ssd_decay(c, h):
    lm = lax.broadcasted_iota(jnp.int32, (128, 128), 0) >= lax.broadcasted_iota(jnp.int32, (128, 128), 1)
    return jnp.exp(jnp.where(lm, c["acs"][:, h:h + 1] - c["acsT"][h:h + 1, :], NEG_INF))


def _ssd_pair_operands(c, CB, h0):
    lane = lax.broadcasted_iota(jnp.int32, (128, 128), 1)
    L0, L1 = _ssd_decay(c, h0), _ssd_decay(c, h0 + 1)
    M = jnp.concatenate([CB * L0, CB * L1], axis=1).astype(BF16)
    xp = c["xdt"][:, h0 * 64:h0 * 64 + 128]
    BD = jnp.concatenate([jnp.where(lane < 64, xp, 0.0), jnp.where(lane >= 64, xp, 0.0)], axis=0).astype(BF16)
    return L0, L1, M, BD


def _ssd_y(c, xa, state_ref, dskipE):
    per_group, ys = [], []
    for g in range(SSM_G):
        gs = slice(g * 512, (g + 1) * 512)
        Bb = xa[:, SSM_W + g * 128:SSM_W + (g + 1) * 128].astype(BF16)
        Cb = xa[:, SSM_W + 256 + g * 128:SSM_W + 256 + (g + 1) * 128].astype(BF16)
        CB = _dot_nt(Cb, Bb)
        Sg = state_ref[:, gs]
        yoff = _dot(Cb, Sg.astype(BF16)) * c["eE"][:, gs]
        ydiag, pairs = [], []
        for j in range(4):
            ops = _ssd_pair_operands(c, CB, g * 8 + 2 * j)
            pairs.append(ops)
            ydiag.append(_dot(ops[2], ops[3]))
        ys.append(jnp.concatenate(ydiag, axis=1) + yoff)
        per_group.append(dict(Bb=Bb, Cb=Cb, CB=CB, Sg=Sg, yoff=yoff, pairs=pairs))
    Y = jnp.concatenate(ys, axis=1) + c["X"] * dskipE
    return Y, per_group


def _ssd_specs(S, rev):
    nc = S // CHUNK
    cm = (lambda b, i: (b, nc - 1 - i)) if rev else (lambda b, i: (b, i))
    xa = pl.BlockSpec((None, CHUNK, CONV_CH), lambda b, i: cm(b, i) + (0,))
    z = [pl.BlockSpec((None, CHUNK, 256), lambda b, i, q=q: cm(b, i) + (OFF["z"] // 256 + q,)) for q in range(4)]
    dt = pl.BlockSpec((None, CHUNK, 128), lambda b, i: cm(b, i) + (OFF["dt"] // 128,))
    tok = pl.BlockSpec((None, CHUNK, SSM_W), lambda b, i: cm(b, i) + (0,))
    st = pl.BlockSpec((None, None, 128, SSM_W), lambda b, i: cm(b, i) + (0, 0))
    return nc, xa, z, dt, tok, st


def _ssd_fwd(xact, P, bias, alog, dskipE, ng, cat, name):
    B, S, _ = P.shape
    nc, xa_spec, z_specs, dt_spec, _, st_spec = _ssd_specs(S, False)
    tok = pl.BlockSpec((None, CHUNK, SSM_W), lambda b, i: (b, i, 1))
    E, _, tri, _ = _ssd_consts()

    def body(xa_ref, z0, z1, z2, z3, dt_ref, bias_ref, alog_ref, dsk_ref, ng_ref, E_ref, tri_ref, cat_ref, o_ref, sp_ref, state):
        @pl.when(pl.program_id(1) == 0)
        def _():
            state[...] = jnp.zeros((128, SSM_W), F32)

        sp_ref[...] = state[...]
        xa = xa_ref[...]
        c = _ssd_pre(xa, dt_ref[...], bias_ref[...], alog_ref[...], E_ref[...], tri_ref[...])
        Y, groups = _ssd_y(c, xa, state, dsk_ref[...])
        Z = (c["xdt"] * c["wE"]).astype(BF16)
        for g in range(SSM_G):
            gs = slice(g * 512, (g + 1) * 512)
            state[:, gs] = groups[g]["Sg"] * c["cdE"][:, gs] + _dot_tn(groups[g]["Bb"], Z[:, gs])
        zv = jnp.concatenate([z0[...], z1[...], z2[...], z3[...]], axis=1)
        yz = Y * (zv * _sigmoid(zv))
        outs = []
        for g in range(SSM_G):
            yg = yz[:, g * 512:(g + 1) * 512]
            outs.append(yg * lax.rsqrt(jnp.mean(yg * yg, axis=-1, keepdims=True) + EPS))
        o_ref[...] = (jnp.concatenate(outs, axis=1) * ng_ref[...]).astype(BF16)

    return pl.pallas_call(
        body, name=name, grid=(B, nc),
        in_specs=[xa_spec] + z_specs + [dt_spec, _full((1, 128)), _full((1, 128)), _full((1, SSM_W)), _full((1, SSM_W)),
                                        _full((128, SSM_W)), _full((128, 128)), _ANY],
        out_specs=[tok, st_spec],
        out_shape=[jax.ShapeDtypeStruct(cat.shape, BF16), jax.ShapeDtypeStruct((B, nc, 128, SSM_W), F32)],
        scratch_shapes=[pltpu.VMEM((128, SSM_W), F32)], input_output_aliases={12: 0},
        compiler_params=pltpu.CompilerParams(dimension_semantics=("arbitrary", "arbitrary")),
    )(xact, P, P, P, P, P, bias, alog, dskipE, ng, E, tri, cat)


def _ssd_bwd(xact, P, sprev, dcat, bias, alog, dskipE, ng, name):
    B, S, _ = P.shape
    nc, xa_spec, z_specs, dt_spec, tok, st_spec = _ssd_specs(S, True)
    do_spec = pl.BlockSpec((None, CHUNK, SSM_W), lambda b, i: (b, nc - 1 - i, 1))
    E, ET, tri, triT = _ssd_consts()
    dt_out = pl.BlockSpec((None, CHUNK, 128), lambda b, i: (b, nc - 1 - i, 0))

    def body(xa_ref, z0, z1, z2, z3, dt_ref, sp_ref, do_ref, bias_ref, alog_ref, dsk_ref, ng_ref, E_ref, ET_ref, tri_ref,
             triT_ref, dxa_ref, dz_ref, ddt_ref, dbias_ref, dalog_ref, ddsk_ref, dng_ref, dstate):
        first = (pl.program_id(0) == 0) & (pl.program_id(1) == 0)

        @pl.when(first)
        def _():
            for ref in (dbias_ref, dalog_ref, ddsk_ref, dng_ref):
                ref[...] = jnp.zeros(ref.shape, F32)

        @pl.when(pl.program_id(1) == 0)
        def _():
            dstate[...] = jnp.zeros((128, SSM_W), F32)

        xa, ETm = xa_ref[...], ET_ref[...]
        c = _ssd_pre(xa, dt_ref[...], bias_ref[...], alog_ref[...], E_ref[...], tri_ref[...])
        Y, groups = _ssd_y(c, xa, sp_ref, dsk_ref[...])
        X, xdt = c["X"], c["xdt"]
        zv = jnp.concatenate([z0[...], z1[...], z2[...], z3[...]], axis=1)
        sg = _sigmoid(zv)
        zs = zv * sg
        yz = Y * zs
        dout = do_ref[...].astype(F32)
        dyz = []
        for g in range(SSM_G):
            gs = slice(g * 512, (g + 1) * 512)
            yg = yz[:, gs]
            r = lax.rsqrt(jnp.mean(yg * yg, axis=-1, keepdims=True) + EPS)
            yn = yg * r
            dng_ref[:, gs] += jnp.sum(dout[:, gs] * yn, axis=0, keepdims=True)
            dyn = dout[:, gs] * ng_ref[:, gs]
            dyz.append(r * (dyn - yn * jnp.mean(dyn * yn, axis=-1, keepdims=True)))
        dyz = jnp.concatenate(dyz, axis=1)
        dz_ref[...] = (dyz * Y * (sg * (1.0 + zv * (1.0 - sg)))).astype(BF16)
        dY = dyz * zs
        ddsk_ref[...] += jnp.sum(dY * X, axis=0, keepdims=True)
        dX = dY * dsk_ref[...]
        lane = lax.broadcasted_iota(jnp.int32, (128, 128), 1)
        sub = lax.broadcasted_iota(jnp.int32, (128, 128), 0)
        colform = jnp.zeros((128, 128), F32)
        rowform = jnp.zeros((128, 128), F32)
        dxdt, gacsE, dBC = [], [], []
        for g in range(SSM_G):
            gs = slice(g * 512, (g + 1) * 512)
            G = groups[g]
            Bb, Cb, CB, Sg = G["Bb"], G["Cb"], G["CB"], G["Sg"]
            dYg = dY[:, gs]
            dQ = (dYg * c["eE"][:, gs]).astype(BF16)
            dSn = dstate[:, gs]
            dSnb = dSn.astype(BF16)
            cd = c["cdE"][:, gs]
            dC = _dot_nt(dQ, Sg.astype(BF16))
            dSprev = _dot_tn(Cb, dQ) + dSn * cd
            t1 = jnp.broadcast_to(jnp.sum(dSn * Sg * cd, axis=0, keepdims=True), (8, 512))
            colform = colform + jnp.where(sub == 127, _dot01(t1, ETm[gs, :], 2)[0:1, :], 0.0)
            Zg = xdt[:, gs] * c["wE"][:, gs]
            dZ = _dot(Bb, dSnb)
            dB = _dot_nt(Zg.astype(BF16), dSnb)
            U = dZ * Zg
            ga = dYg * G["yoff"] - U
            ga = ga + jnp.where(lax.broadcasted_iota(jnp.int32, (128, 512), 0) == 127, jnp.sum(U, axis=0, keepdims=True), 0.0)
            gacsE.append(ga)
            dxg = [None] * 4
            dCB = jnp.zeros((128, 128), F32)
            for j in range(4):
                h0 = g * 8 + 2 * j
                L0, L1, M, BD = G["pairs"][j]
                dYp = dYg[:, j * 128:(j + 1) * 128].astype(BF16)
                dM = _dot_nt(dYp, BD)
                dBD = _dot_tn(M, dYp)
                dxg[j] = jnp.where(lane < 64, dBD[:128], dBD[128:])
                for t, (h, L) in enumerate(((h0, L0), (h0 + 1, L1))):
                    dMh = dM[:, t * 128:(t + 1) * 128]
                    dCB = dCB + dMh * L
                    Gh = dMh * CB * L
                    colform = colform + jnp.where(lane == h, jnp.sum(Gh, axis=1, keepdims=True), 0.0)
                    rowform = rowform - jnp.where(sub == h, jnp.sum(Gh, axis=0, keepdims=True), 0.0)
            dCBb = dCB.astype(BF16)
            dC = dC + _dot(dCBb, Bb)
            dB = dB + _dot_tn(dCBb, Cb)
            dxdt.append(jnp.concatenate(dxg, axis=1) + dZ * c["wE"][:, gs])
            dBC.append((dB, dC))
            dstate[:, gs] = dSprev
        dxdt = jnp.concatenate(dxdt, axis=1)
        dX = dX + dxdt * c["dtE"]
        ddt = _dot01(dxdt * X, ETm, 2)
        dacs = colform + rowform.T + _dot01(jnp.concatenate(gacsE, axis=1), ETm, 2)
        dda = _dot01_left(triT_ref[...], dacs, 2)
        ddt = ddt + dda * c["a"]
        dalog_ref[...] += jnp.sum(dda * c["dtp"], axis=0, keepdims=True) * c["a"]
        ddtraw = jnp.where(lane < SSM_H, ddt * _sigmoid(c["pre"]), 0.0)
        dbias_ref[...] += jnp.sum(ddtraw, axis=0, keepdims=True)
        ddt_ref[...] = ddtraw.astype(BF16)
        dxa_ref[...] = jnp.concatenate([dX, dBC[0][0], dBC[1][0], dBC[0][1], dBC[1][1]], axis=1)

    p128, p1k = _full((1, 128)), _full((1, SSM_W))
    return pl.pallas_call(
        body, name=name, grid=(B, nc),
        in_specs=[xa_spec] + z_specs + [dt_spec, st_spec, do_spec, p128, p128, p1k, p1k,
                                        _full((128, SSM_W)), _full((SSM_W, 128)), _full((128, 128)), _full((128, 128))],
        out_specs=[xa_spec, tok, dt_out, p128, p128, p1k, p1k],
        out_shape=[jax.ShapeDtypeStruct((B, S, CONV_CH), F32), jax.ShapeDtypeStruct((B, S, SSM_W), BF16),
                   jax.ShapeDtypeStruct((B, S, 128), BF16), jax.ShapeDtypeStruct((1, 128), F32),
                   jax.ShapeDtypeStruct((1, 128), F32), jax.ShapeDtypeStruct((1, SSM_W), F32),
                   jax.ShapeDtypeStruct((1, SSM_W), F32)],
        scratch_shapes=[pltpu.VMEM((128, SSM_W), F32)],
        compiler_params=pltpu.CompilerParams(dimension_semantics=("arbitrary", "arbitrary")),
    )(xact, P, P, P, P, P, sprev, dcat, bias, alog, dskipE, ng, E, ET, tri, triT)


def _adamw(w, parts, m, v, name, tr=512, row0=0, prev=None):
    Rtot, C = w.shape
    ns, R = parts.shape[0], parts.shape[1]
    tr = min(tr, R)
    assert R % tr == 0 and row0 % tr == 0
    off = row0 // tr
    c1 = 1.0 / (1.0 - ADAM_B1 ** ADAM_STEP)
    c2 = 1.0 / (1.0 - ADAM_B2 ** ADAM_STEP)

    def body(w_ref, p_ref, m_ref, v_ref, *rest):
        g_ref, d_ref, mo_ref, vo_ref = rest[-4:]
        g = p_ref[0].astype(F32)
        for s in range(1, ns):
            g = g + p_ref[s].astype(F32)
        mn = ADAM_B1 * m_ref[...] + (1.0 - ADAM_B1) * g
        vn = ADAM_B2 * v_ref[...] + (1.0 - ADAM_B2) * (g * g)
        g_ref[...] = g
        mo_ref[...] = mn
        vo_ref[...] = vn
        d_ref[...] = -ADAM_LR * ((mn * c1) / (jnp.sqrt(vn * c2) + ADAM_EPS) + ADAM_WD * w_ref[...])

    blk = pl.BlockSpec((tr, C), lambda i: (i + off, 0))
    extra = [] if prev is None else list(prev)
    return pl.pallas_call(
        body, name=name, grid=(R // tr,),
        in_specs=[blk, pl.BlockSpec((ns, tr, C), lambda i: (0, i, 0)), blk, blk] + [pl.BlockSpec(memory_space=pl.ANY)] * len(extra),
        out_specs=[blk] * 4, out_shape=[jax.ShapeDtypeStruct((Rtot, C), F32)] * 4,
        input_output_aliases={4 + k: k for k in range(len(extra))})(w, parts, m, v, *extra)


_SMALL = ("ada_b", "norm1_g", "gm_ln_g", "gm_ln_b", "gm_ws", "gm_bs", "gm_norm_g", "attn_sinks", "attn_norm_g", "conv_b",
          "dt_bias", "a_log", "d_skip", "ssm_norm_g", "norm2_g", "final_norm_g")


def _pack(arrs):
    flat = []
    for a in arrs:
        f = a.reshape(-1).astype(F32)
        flat.append(jnp.pad(f, (0, (-f.shape[0]) % 1024)))
    return jnp.concatenate(flat).reshape(-1, 128)


def _unpack(pack, like):
    out, o = [], 0
    flat = pack.reshape(-1)
    for a in like:
        n = int(np.prod(a.shape))
        out.append(flat[o:o + n].reshape(a.shape))
        o += n + (-n) % 1024
    return out


def kernel(x, c, ada_w, ada_b, norm1_g, w_in, gm_ln_g, gm_ln_b, gm_ws, gm_bs, gm_norm_g, attn_sinks, attn_norm_g, conv_w, conv_b, dt_bias, a_log, d_skip, ssm_norm_g, w_out, norm2_g, w_mlp1, w_mlp2, final_norm_g, loss_target, m_ada_w, m_ada_b, m_norm1_g, m_w_in, m_gm_ln_g, m_gm_ln_b, m_gm_ws, m_gm_bs, m_gm_norm_g, m_attn_sinks, m_attn_norm_g, m_conv_w, m_conv_b, m_dt_bias, m_a_log, m_d_skip, m_ssm_norm_g, m_w_out, m_norm2_g, m_w_mlp1, m_w_mlp2, m_final_norm_g, v_ada_w, v_ada_b, v_norm1_g, v_w_in, v_gm_ln_g, v_gm_ln_b, v_gm_ws, v_gm_bs, v_gm_norm_g, v_attn_sinks, v_attn_norm_g, v_conv_w, v_conv_b, v_dt_bias, v_a_log, v_d_skip, v_ssm_norm_g, v_w_out, v_norm2_g, v_w_mlp1, v_w_mlp2, v_final_norm_g):
    args = dict(locals())
    B, S, _ = x.shape
    T = B * S
    L = DEPTH
    me = 4 * lax.axis_index("x") + 2 * lax.axis_index("y") + lax.axis_index("c")

    gath = _gather2([c, conv_w], "ag_c")
    big = ("w_in", "w_out", "w_mlp1", "w_mlp2")
    chain = [(n, l) for l in range(L) for n in ("w_in", "w_mlp1", "w_out", "w_mlp2")]
    inflight = {}

    def start_next(order):
        if not chain:
            return jnp.zeros((8, 128), F32)
        n, l = chain.pop(0)
        sems, land_thru, token = _gather_start(zone[n, l], order, f"ag_start_{n}{l}")
        inflight[n, l] = (sems, land_thru)
        return token

    def gathered(n, l, after):
        land = _gather_wait(*inflight.pop((n, l)), after, f"ag_wait_{n}{l}")
        return _gather_finish(land, f"ag_fin_{n}{l}")

    me1 = me.astype(jnp.int32).reshape(1)
    zone = {(n, l): _landing_zone(args[n], l, me1, f"ag_zone_{n}{l}") for n, l in chain}
    later_zones = [zone[k] for k in chain[1:]]

    tok = start_next(gath[0])
    c_all = gath[0].reshape(NDEV * B, D) + tok[0, 0]
    c_act = (c_all * jax.nn.sigmoid(c_all)).astype(BF16)
    nb_rows = c_act.shape[0]
    c_pad = jnp.pad(c_act, ((0, 128 - nb_rows), (0, 0)))
    adw = ada_w.astype(BF16)
    mod_part = jnp.stack([_mm(c_pad, adw[l], mode="nn", name=f"mod{l}", tn=768)[:nb_rows] for l in range(L)])
    mod_all = _gather_small([mod_part], "ag_mod", order=later_zones)[0]
    mod_mine = lax.dynamic_slice_in_dim(mod_all, me * B, B, axis=2)
    mod = jnp.transpose(mod_mine, (1, 2, 0, 3)).reshape(L, B, 6 * D) + ada_b[:, None, :]
    mods = [[mod[l][:, None, i * D:(i + 1) * D] for i in range(6)] for l in range(L)]

    win_g, wout_g, w1_g, w2_g = [None] * L, [None] * L, [None] * L, [None] * L

    tril = jnp.tril(jnp.ones((128, 128), F32))
    row = lambda a: a.reshape(1, -1)
    pad128 = lambda a: jnp.pad(a.reshape(1, -1), ((0, 0), (0, 128 - a.shape[-1])))
    small = []
    for l in range(L):
        wt = gm_ws[l] * tril
        small.append(dict(
            lng=row(gm_ln_g[l]), lnb=row(gm_ln_b[l]), wt=wt.astype(BF16), wtT=jnp.swapaxes(wt, 1, 2).astype(BF16),
            bsx=jnp.repeat(gm_bs[l].T, 128, axis=1), gog=row(gm_norm_g[l]), sinks=attn_sinks[l], aog=row(attn_norm_g[l]),
            bias=pad128(dt_bias[l]), alog=pad128(a_log[l]), dskE=jnp.repeat(d_skip[l], SSM_HD).reshape(1, SSM_W),
            sng=row(ssm_norm_g[l]), cb=row(conv_b[l])))
    convw_all = jnp.transpose(gath[1], (1, 2, 0, 3)).reshape(L, 4, CONV_CH)
    convw8 = jnp.pad(convw_all, ((0, 0), (0, 4), (0, 0)))

    saved = []
    xl = x
    h = _norm_fwd(xl, row(norm1_g[0]), mods[0][1], mods[0][0], "norm1_f0")
    for l in range(L):
        sm = small[l]
        if l == 0:
            g_in = gathered("w_in", 0, h)
            tok = start_next(g_in)
        win_g[l] = _shards_to_cols(g_in, f"w_in_cols{l}")
        P = _mm(h.reshape(T, D), win_g[l], mode="nn", name=f"proj_in{l}", tn=1536, order=tok).reshape(B, S, PW)
        cat = _gmlp_fwd(P, sm["lng"], sm["lnb"], sm["wt"], sm["bsx"], sm["gog"], f"gmlp_f{l}")
        cat = _attn_fwd(P, sm["sinks"], sm["aog"], cat, f"attn_f{l}")
        xact = _conv_fwd(P, convw8[l], sm["cb"], f"conv_f{l}")
        w1_g[l] = gathered("w_mlp1", l, xact)
        tok = start_next(w1_g[l])
        cat, sprev = _ssd_fwd(xact, P, sm["bias"], sm["alog"], sm["dskE"], sm["sng"] + tok[0:1, 0:1], cat, f"ssd_f{l}")
        g_out = gathered("w_out", l, cat)
        tok = start_next(g_out)
        wout_g[l] = g_out.reshape(D, D)
        mix = _mm(cat.reshape(T, D), wout_g[l], mode="nn", name=f"proj_out{l}", order=tok).reshape(B, S, D)
        x_mid, h2 = _norm_fwd(xl, row(norm2_g[l]), mods[l][4], mods[l][3], f"norm2_f{l}", resid=(mix, mods[l][2]))
        a_act, r_act = _mm(h2.reshape(T, D), w1_g[l], mode="nn", name=f"mlp1_{l}", out_dtypes=(BF16, BF16), col_blocked_b=True,
                           epilogue=lambda acc: (acc, jnp.square(jnp.maximum(acc, 0.0))))
        g_2 = gathered("w_mlp2", l, r_act)
        tok = start_next(g_2)
        w2_g[l] = g_2.reshape(DFF, D)
        m2 = _mm(r_act, w2_g[l], mode="nn", name=f"mlp2_{l}", order=tok, tk=4096).reshape(B, S, D)
        saved.append(dict(x_in=xl, h=h, P=P, xact=xact, sprev=sprev, cat=cat, mix=mix, x_mid=x_mid, h2=h2, a=a_act, r=r_act, m2=m2))
        if l + 1 < L:
            g_in = gathered("w_in", l + 1, m2)
            tok = start_next(g_in)
            xl, h = _norm_fwd(x_mid, row(norm1_g[l + 1]) + tok[0, 0], mods[l + 1][1], mods[l + 1][0], f"norm1_f{l + 1}",
                              resid=(m2, mods[l][5]))

    sv = saved[L - 1]
    nb = _norm_bwd(sv["x_mid"], row(final_norm_g), "final_b", tgt=loss_target, br=sv["m2"], gate=mods[L - 1][5], x_is_prev=True)
    loss_part, g_final = nb["loss"], nb["dg"]
    dmod, gsm, gconvw = [None] * L, [None] * L, [None] * L
    core = lax.axis_index("c").astype(jnp.int32).reshape(1)
    reducing = []

    def reduce_start(n, l, sent, after):
        p, from_sib = _pair_wait(*sent[:3], after, f"rs_pair_wait_{n}{l}")
        s, land = _pair_add(p, from_sib, core, f"rs_add_{n}{l}")
        return reduce_exchange(n, l, s, land, after)

    def reduce_exchange(n, l, s, land, order):
        sems, s_thru, land_thru, token = _chipsum_start(s, land, order, f"rs_start_{n}{l}")
        reducing.append((n, l, sems, s_thru, land_thru))
        return token

    other = 1 - core

    for l in reversed(range(L)):
        sv, sm = saved[l], small[l]
        dm2, dxo, dg2 = nb["dbr"].reshape(T, D), nb["dx"], nb["dgate"]
        da = _mm(dm2, w2_g[l], mode="nt", name=f"mlp2_dx{l}", out_dtypes=(BF16,), extras=(sv["a"],),
                 epilogue=lambda acc, a: (acc * (2.0 * jnp.maximum(a.astype(F32), 0.0)),))
        h2f = sv["h2"].reshape(T, D)
        sent2 = _sibling_start(_dw_half(sv["r"], dm2, other, axis="m", name=f"mlp2_dw_sib{l}"), da, f"rs_sib_start_w_mlp2{l}")
        dh2 = _mm(da, w1_g[l], mode="nt", name=f"mlp1_dx{l}", col_blocked_b=True, order=sent2[3],
                  out_dtypes=(BF16,)).reshape(B, S, D)
        from_sib = _sibling_wait(*sent2[:3], dh2, f"rs_sib_wait_w_mlp2{l}")[1]
        sent1 = _sibling_start(_dw_half(h2f, da, other, axis="n", name=f"mlp1_dw_sib{l}", order=from_sib), da,
                               f"rs_sib_start_w_mlp1{l}")
        s2, land2 = _dw_half(sv["r"], dm2, core, axis="m", name=f"mlp2_dw_own{l}", add=from_sib, order=sent1[3])
        tok = reduce_exchange("w_mlp2", l, s2, land2, da)
        nb2 = _norm_bwd(sv["x_mid"], row(norm2_g[l]) + tok[0, 0], f"norm2_b{l}", sc=mods[l][4], dh=dh2, dres=dxo, br=sv["mix"],
                        gate=mods[l][2])
        dmix = nb2["dbr"].reshape(T, D)
        from_sib = _sibling_wait(*sent1[:3], dmix, f"rs_sib_wait_w_mlp1{l}")[1]
        s1, land1 = _dw_half(h2f, da, core, axis="n", name=f"mlp1_dw_own{l}", add=from_sib)
        tok = reduce_exchange("w_mlp1", l, s1, land1, dmix)
        dcat = _mm(dmix, wout_g[l], mode="nt", name=f"proj_out_dx{l}", order=tok, out_dtypes=(BF16,)).reshape(B, S, D)
        du, dv, dlng, dlnb, dws, dbsx, dgog = _gmlp_bwd(sv["P"], dcat, sm["lng"], sm["lnb"], sm["wt"], sm["wtT"], sm["bsx"],
                                                        sm["gog"], f"gmlp_b{l}")
        dq, dk, dvv, dsink, daog = _attn_bwd(sv["P"], dcat, sm["sinks"], sm["aog"], f"attn_b{l}")
        dwo = _mm(sv["cat"].reshape(T, D), dmix, mode="tn", name=f"proj_out_dw{l}", out_dtypes=(BF16,), tk=2048,
                  order=dq).reshape(4, 2, D // NDEV, D)
        sent = _pair_start(dwo, dmix, f"rs_pair_start_w_out{l}")
        dxa, dz, ddt, dbias, dalog, ddsk, dsng = _ssd_bwd(sv["xact"], sv["P"], sv["sprev"], dcat, sm["bias"], sm["alog"],
                                                          sm["dskE"], sm["sng"] + sent[3][0:1, 0:1], f"ssd_b{l}")
        tok = reduce_start("w_out", l, sent, dxa)
        dxbc, dcw, dcb = _conv_bwd(sv["P"], dxa, convw8[l], sm["cb"] + tok[0:1, 0:1], f"conv_b{l}")
        dP = jnp.concatenate([du, dv, dq, dk, dvv, dz, dxbc, ddt, jnp.zeros((B, S, PW - OFF["dt"] - 128), BF16)],
                             axis=-1).reshape(T, PW)
        dwin = _mm(sv["h"].reshape(T, D), dP, mode="tn", name=f"proj_in_dw{l}", out_dtypes=(BF16,), tn=1536, tk=2048)
        sent = _sibling_start(dwin, dP, f"rs_sib_start_w_in{l}")
        dh = _mm(dP, win_g[l], mode="nt", name=f"proj_in_dx{l}", tk=2304, order=sent[3], out_dtypes=(BF16,)).reshape(B, S, D)
        s_in, land_in = _cols_to_my_shards(*_sibling_wait(*sent[:3], dh, f"rs_sib_wait_w_in{l}"), core, f"w_in_dshards{l}")
        tok = reduce_exchange("w_in", l, s_in, land_in, dh)
        nb = _norm_bwd(sv["x_in"], row(norm1_g[l]) + tok[0, 0], f"norm1_b{l}", sc=mods[l][1], dh=dh, dres=nb2["dx"],
                       br=saved[l - 1]["m2"] if l > 0 else None, gate=mods[l - 1][5] if l > 0 else None)
        dmod[l] = jnp.concatenate([nb["dsh"], nb["dsc"], nb2["dgate"], nb2["dsh"], nb2["dsc"], dg2], axis=-1)
        gconvw[l] = dcw[:4]
        gsm[l] = dict(
            ada_b=jnp.sum(dmod[l], axis=(0, 1)), norm1_g=nb["dg"], gm_ln_g=dlng, gm_ln_b=dlnb, gm_ws=dws,
            gm_bs=dbsx.reshape(128, GM_H, 128).sum(-1).T, gm_norm_g=dgog, attn_sinks=dsink[:, 0], attn_norm_g=daog,
            conv_b=dcb, dt_bias=dbias[0, :SSM_H], a_log=dalog[0, :SSM_H], d_skip=ddsk.reshape(SSM_H, SSM_HD).sum(-1),
            ssm_norm_g=dsng, norm2_g=nb2["dg"])
    grad_x = nb["dx"]

    big_res, after = dict.fromkeys(big), grad_x
    tile_rows = dict(w_in=256, w_out=256, w_mlp1=256, w_mlp2=128)

    def finish_reduce(n, l, sems, s_thru, land_thru, after):
        parts = _chipsum_wait(sems, s_thru, land_thru, after, f"rs_wait_{n}{l}")
        w = args[n]
        big_res[n] = _adamw(w.reshape(-1, w.shape[-1]), parts, args["m_" + n].reshape(-1, w.shape[-1]),
                            args["v_" + n].reshape(-1, w.shape[-1]), f"adamw_{n}{l}", tr=tile_rows[n], row0=l * w.shape[1],
                            prev=big_res[n])
        return big_res[n][0]

    for item in reducing[:-1]:
        after = finish_reduce(*item, after)

    per_layer = [n for n in _SMALL if n != "final_norm_g"]
    g_small = [jnp.stack([gsm[l][n].reshape(args[n].shape[1:]) for l in range(L)]) for n in per_layer] + [g_final.reshape(D)]
    zc = jnp.zeros((L, 4, CONV_CH), F32)
    z1 = jnp.zeros((1, 128), F32)
    gpack = _pack([loss_part] + g_small + [jnp.stack(gconvw)])
    got = _gather2([jnp.stack(dmod).reshape(L, B, 6 * D), gpack], "ag_small", order=after)
    like = [z1] + [args[n] for n in _SMALL] + [zc]
    packs = [_pack([z1] + [args[p + n] for n in _SMALL] + [zc]) for p in ("", "m_", "v_")]
    sres = [_unpack(p, like) for p in _adamw(packs[0], got[1], packs[1], packs[2], "adamw_small", tr=gpack.shape[0])]
    res = {n: [r[1 + i] for r in sres] for i, n in enumerate(_SMALL)}
    loss = sres[0][0][0, 0]
    gcw = lax.dynamic_slice_in_dim(sres[0][-1], me * (CONV_CH // NDEV), CONV_CH // NDEV, axis=2)

    def update(name, parts, tr):
        w = args[name]
        r = _adamw(w.reshape(-1, w.shape[-1]), parts, args["m_" + name].reshape(-1, w.shape[-1]),
                   args["v_" + name].reshape(-1, w.shape[-1]), "adamw_" + name, tr=tr)
        res[name] = [a.reshape(w.shape) for a in r]

    update("conv_w", gcw.reshape(1, L * 4, CONV_CH // NDEV), L * 4)

    dmod_all = jnp.transpose(got[0], (1, 0, 2, 3)).reshape(L, NDEV * B, 6 * D)
    dm_mine = lax.dynamic_slice_in_dim(dmod_all, me * (6 * D // NDEV), 6 * D // NDEV, axis=2)
    dm_pad = jnp.pad(dm_mine, ((0, 0), (0, 128 - nb_rows), (0, 0))).astype(BF16)
    g_adaw = jnp.stack([_mm(c_pad, dm_pad[l], mode="tn", name=f"ada_dw{l}", tn=768) for l in range(L)])
    update("ada_w", g_adaw.reshape(1, L * D, 6 * D // NDEV), 256)

    finish_reduce(*reducing[-1], res["ada_w"][0])
    for n in big:
        res[n] = [a.reshape(args[n].shape) for a in big_res[n]]

    names = ['ada_w', 'ada_b', 'norm1_g', 'w_in', 'gm_ln_g', 'gm_ln_b', 'gm_ws', 'gm_bs', 'gm_norm_g', 'attn_sinks',
             'attn_norm_g', 'conv_w', 'conv_b', 'dt_bias', 'a_log', 'd_skip', 'ssm_norm_g', 'w_out', 'norm2_g', 'w_mlp1',
             'w_mlp2', 'final_norm_g']
    return (loss, grad_x, *[res[n][0] for n in names], *[res[n][1] for n in names], *[res[n][2] for n in names],
            *[res[n][3] for n in names])
```

```python
import functools

import jax
import jax.numpy as jnp
import numpy as np
from jax import lax
from jax.experimental import pallas as pl
from jax.experimental.pallas import tpu as pltpu

F32, BF16 = jnp.float32, jnp.bfloat16
HI = lax.Precision.HIGHEST
MESH = pl.DeviceIdType.MESH
NDEV = 8

D = 2048
DEPTH = 2
CHUNK = 128
GM_W, GM_H = 512, 4
ATT_W, KV_W, ATT_H = 512, 128, 8
SSM_W, SSM_H, SSM_HD, SSM_G = 1024, 16, 64, 2
CONV_CH = 1536
IN_W = 4368
DFF = 8192
EPS = 1e-6
NEG_INF = -1e30
GELU_K = 0.7978845608028654
GELU_C = 0.044715

_ORIG = (("u", 512), ("v", 512), ("q", 512), ("k", 128), ("vv", 128), ("z", 1024), ("xbc", 1536), ("dt", 16))
OFF = dict(u=0, v=512, q=1024, k=1536, vv=1664, z=1792, xbc=2816, dt=4352)
PW = 4608

ADAM_LR, ADAM_B1, ADAM_B2, ADAM_EPS, ADAM_WD, ADAM_STEP = 0.001, 0.9, 0.999, 1e-08, 0.01, 10


def _shards_to_cols(g, name, tr=256):
    n, R, C = g.shape

    def body(g_ref, o_ref):
        o_ref[...] = jnp.concatenate([g_ref[s] for s in range(n)] + [jnp.zeros((tr, PW - n * C), g.dtype)], axis=1)

    return pl.pallas_call(body, name=name, grid=(R // tr,), in_specs=[pl.BlockSpec((n, tr, C), lambda i: (0, i, 0))],
                          out_specs=pl.BlockSpec((tr, PW), lambda i: (i, 0)), out_shape=jax.ShapeDtypeStruct((R, PW), g.dtype))(g)


def _cols_to_my_shards(w, w_sib, core, name, tr=256):
    R, C = w.shape[0], IN_W // NDEV

    def body(core_ref, w_ref, s_ref, o_ref, o2_ref):
        x = w_ref[...].astype(F32) + s_ref[...].astype(F32)
        mine_is_odd = core_ref[0] == 1
        for q in range(4):
            blk = jnp.where(mine_is_odd, x[:, C * (2 * q + 1):C * (2 * q + 2)], x[:, C * 2 * q:C * (2 * q + 1)]).astype(o_ref.dtype)
            o_ref[q] = blk
            o2_ref[q] = blk

    row = pl.BlockSpec((tr, PW), lambda i, c: (i, 0))
    out = pl.BlockSpec((4, tr, C), lambda i, c: (0, i, 0))
    return pl.pallas_call(
        body, name=name, out_shape=[jax.ShapeDtypeStruct((4, R, C), w.dtype)] * 2,
        grid_spec=pltpu.PrefetchScalarGridSpec(num_scalar_prefetch=1, grid=(R // tr,), in_specs=[row, row], out_specs=[out, out]),
    )(core, w, w_sib)


def _sigmoid(x):
    return 1.0 / (1.0 + jnp.exp(-x))


def _gelu(x):
    return 0.5 * x * (1.0 + jnp.tanh(GELU_K * (x + GELU_C * x * x * x)))


def _gelu_grad(x):
    t = jnp.tanh(GELU_K * (x + GELU_C * x * x * x))
    return 0.5 * (1.0 + t) + 0.5 * x * (1.0 - t * t) * GELU_K * (1.0 + 3.0 * GELU_C * x * x)


def _dot(a, b, prec=None):
    return jnp.dot(a, b, precision=prec, preferred_element_type=F32)


def _dot_nt(a, b, prec=None):
    return lax.dot_general(a, b, (((1,), (1,)), ((), ())), precision=prec, preferred_element_type=F32)


def _dot_tn(a, b, prec=None):
    return lax.dot_general(a, b, (((0,), (0,)), ((), ())), precision=prec, preferred_element_type=F32)


def _full(shape):
    return pl.BlockSpec(shape, lambda *_: (0,) * len(shape))


_HBM = pl.BlockSpec(memory_space=pltpu.HBM)


def _me():
    return lax.axis_index("x"), lax.axis_index("y"), lax.axis_index("c")


def _peer(k):
    x, y, c = _me()
    px = 1 - x if k & 4 else x
    py = 1 - y if k & 2 else y
    pc = 1 - c if k & 1 else c
    return (px, py, pc), 4 * px + 2 * py + pc


def _gather_small(xs, name, order=()):
    n = len(xs)

    def body(*refs):
        ins, outs = refs[:n], refs[-n - 3:-3]
        send, recv, loc = refs[-3:]
        x, y, c = _me()
        me = 4 * x + 2 * y + c
        started = []
        for i in range(n):
            own = pltpu.make_async_copy(ins[i], outs[i].at[me], loc.at[i])
            own.start()
            started.append(own)
        for k in range(1, NDEV):
            dev, lin = _peer(k)
            for i in range(n):
                pltpu.make_async_remote_copy(
                    src_ref=ins[i], dst_ref=outs[i].at[me],
                    send_sem=send.at[i, k - 1], recv_sem=recv.at[i, k - 1], device_id=dev, device_id_type=MESH).start()
        for k in range(1, NDEV):
            dev, lin = _peer(k)
            for i in range(n):
                pltpu.make_async_remote_copy(
                    src_ref=ins[i], dst_ref=outs[i].at[lin],
                    send_sem=send.at[i, k - 1], recv_sem=recv.at[i, k - 1], device_id=dev, device_id_type=MESH).wait()
        for own in started:
            own.wait()

    extra = list(order)
    return pl.pallas_call(
        body, name=name, out_shape=[jax.ShapeDtypeStruct((NDEV,) + a.shape, a.dtype) for a in xs],
        in_specs=[_HBM] * n + [pl.BlockSpec(memory_space=pl.ANY)] * len(extra), out_specs=[_HBM] * n,
        scratch_shapes=[pltpu.SemaphoreType.DMA((n, NDEV - 1)), pltpu.SemaphoreType.DMA((n, NDEV - 1)),
                        pltpu.SemaphoreType.DMA((n,))],
        compiler_params=pltpu.CompilerParams(has_side_effects=True),
    )(*xs, *extra)


def _chips():
    x, y, c = _me()
    return x, y, c, [(1 - x, y), (x, 1 - y), (1 - x, 1 - y)]


def _gather2(xs, name, order=None):
    n = len(xs)
    extra = [] if order is None else [order]

    def body(*refs):
        ins, outs = refs[:n], refs[-n - 3:-3]
        send, recv, loc = refs[-3:]
        x, y, c, chips = _chips()
        me, sib = (x, y, c), (x, y, 1 - c)

        def cp(i, k, block, to, src=None):
            slot = outs[i].at[4 * block[0] + 2 * block[1] + block[2]]
            return pltpu.make_async_remote_copy(src_ref=slot if src is None else src, dst_ref=slot, send_sem=send.at[i, k],
                                                recv_sem=recv.at[i, k], device_id=to, device_id_type=MESH)

        sent = []
        for i in range(n):
            for j, chip in enumerate(chips):
                sent.append(cp(i, 1 + j, me, (*chip, c), src=ins[i]))
            sent.append(cp(i, 0, me, sib, src=ins[i]))
        for s in sent:
            s.start()
        own = [pltpu.make_async_copy(ins[i], outs[i].at[4 * x + 2 * y + c], loc.at[i]) for i in range(n)]
        for o in own:
            o.start()
        for j, chip in enumerate(chips):
            for i in range(n):
                cp(i, 1 + j, (*chip, c), me).wait_recv()
                fwd = cp(i, 4 + j, (*chip, c), sib)
                fwd.start()
                sent.append(fwd)
        for i in range(n):
            cp(i, 0, sib, me).wait_recv()
            for j, chip in enumerate(chips):
                cp(i, 4 + j, (*chip, 1 - c), me).wait_recv()
        for s in sent:
            s.wait_send()
        for o in own:
            o.wait()

    return pl.pallas_call(
        body, name=name, out_shape=[jax.ShapeDtypeStruct((NDEV,) + a.shape, a.dtype) for a in xs],
        in_specs=[_HBM] * n + [pl.BlockSpec(memory_space=pl.ANY)] * len(extra), out_specs=[_HBM] * n,
        scratch_shapes=[pltpu.SemaphoreType.DMA((n, 7)), pltpu.SemaphoreType.DMA((n, 7)), pltpu.SemaphoreType.DMA((n,))],
        compiler_params=pltpu.CompilerParams(has_side_effects=True),
    )(*xs, *extra)


def _pair_add(p, r1, core, name, tr=256):
    _, _, R, C = p.shape
    tr = min(tr, R)

    def body(core_ref, p_ref, r_ref, o_ref, o2_ref):
        s = (p_ref[...].astype(F32) + r_ref[...].astype(F32)).astype(o_ref.dtype)
        o_ref[...] = s
        o2_ref[...] = s

    blk = pl.BlockSpec((None, tr, C), lambda ch, i, core_ref: (ch, i, 0))
    return pl.pallas_call(
        body, name=name, out_shape=[jax.ShapeDtypeStruct((4, R, C), p.dtype)] * 2,
        grid_spec=pltpu.PrefetchScalarGridSpec(
            num_scalar_prefetch=1, grid=(4, R // tr),
            in_specs=[pl.BlockSpec((None, None, tr, C), lambda ch, i, core_ref: (ch, core_ref[0], i, 0)), blk],
            out_specs=[blk, blk]),
    )(core, p, r1)


_SEM = pl.BlockSpec(memory_space=pltpu.SEMAPHORE)
_ANY = pl.BlockSpec(memory_space=pl.ANY)
_DATAFLOW = pltpu.SideEffectType.DATAFLOW_SIDE_EFFECTING


def _hbm(a):
    return pltpu.with_memory_space_constraint(a, pltpu.HBM)


def _gather_targets():
    x, y, c, chips = _chips()
    return 4 * x + 2 * y + c, [(x, y, 1 - c)] + [(*chip, c) for chip in chips]


def _landing_zone(w, l, me, name, tr=512):
    _, R, C = w.shape
    tr = min(tr, R)

    def body(me_ref, w_ref, o_ref):
        o_ref[...] = w_ref[...].astype(BF16)

    return pl.pallas_call(
        body, name=name, out_shape=jax.ShapeDtypeStruct((NDEV, R, C), BF16),
        grid_spec=pltpu.PrefetchScalarGridSpec(
            num_scalar_prefetch=1, grid=(R // tr,), in_specs=[pl.BlockSpec((None, tr, C), lambda i, me_ref: (l, i, 0))],
            out_specs=pl.BlockSpec((None, tr, C), lambda i, me_ref: (me_ref[0], i, 0))),
    )(me, w)


def _gather_start(land, order, name):
    def body(land_ref, order_ref, *rest):
        sems, token = rest[:8], rest[9]
        me, targets = _gather_targets()
        for k, to in enumerate(targets):
            pltpu.make_async_remote_copy(src_ref=land_ref.at[me], dst_ref=land_ref.at[me], send_sem=sems[k],
                                         recv_sem=sems[4 + k], device_id=to, device_id_type=MESH).start()
        token[...] = jnp.zeros_like(token)

    outs = pl.pallas_call(
        body, name=name,
        out_shape=(pltpu.SemaphoreType.DMA(()),) * 8 + (pltpu.HBM(land.shape, land.dtype), jax.ShapeDtypeStruct((8, 128), F32)),
        in_specs=(_HBM, _ANY), out_specs=(_SEM,) * 8 + (_HBM, pl.BlockSpec(memory_space=pltpu.VMEM)),
        input_output_aliases={0: 8}, compiler_params=pltpu.CompilerParams(has_side_effects=_DATAFLOW),
    )(_hbm(land), order)
    return outs[:8], outs[8], outs[9]


def _gather_wait(sems, land_thru, after, name):
    def body(land_ref, *rest):
        sems_ = rest[:8]
        me, targets = _gather_targets()
        for k, to in enumerate(targets):
            cp = pltpu.make_async_remote_copy(src_ref=land_ref.at[me], dst_ref=land_ref.at[me], send_sem=sems_[k],
                                              recv_sem=sems_[4 + k], device_id=to, device_id_type=MESH)
            cp.wait_send()
            cp.wait_recv()

    return pl.pallas_call(
        body, name=name, out_shape=pltpu.HBM(land_thru.shape, land_thru.dtype),
        in_specs=(_HBM,) + (_SEM,) * 8 + (_ANY,), out_specs=_HBM, input_output_aliases={0: 0},
        compiler_params=pltpu.CompilerParams(has_side_effects=_DATAFLOW),
    )(land_thru, *sems, after)


def _gather_finish(land, name):
    def body(land_ref, out, send, recv):
        x, y, c, chips = _chips()
        fwd = [pltpu.make_async_remote_copy(src_ref=out.at[4 * px + 2 * py + c], dst_ref=out.at[4 * px + 2 * py + c],
                                            send_sem=send.at[j], recv_sem=recv.at[j], device_id=(x, y, 1 - c), device_id_type=MESH)
               for j, (px, py) in enumerate(chips)]
        for cp in fwd:
            cp.start()
        for j, (px, py) in enumerate(chips):
            slot = out.at[4 * px + 2 * py + 1 - c]
            pltpu.make_async_remote_copy(src_ref=slot, dst_ref=slot, send_sem=send.at[j], recv_sem=recv.at[j],
                                         device_id=(x, y, 1 - c), device_id_type=MESH).wait()

    return pl.pallas_call(
        body, name=name, out_shape=jax.ShapeDtypeStruct(land.shape, land.dtype),
        in_specs=[_HBM], out_specs=_HBM, input_output_aliases={0: 0},
        scratch_shapes=[pltpu.SemaphoreType.DMA((3,)), pltpu.SemaphoreType.DMA((3,))],
        compiler_params=pltpu.CompilerParams(has_side_effects=True),
    )(land)


def _forward_start(land, order, name):
    def body(land_ref, order_ref, *rest):
        sems, token = rest[:6], rest[7]
        x, y, c, chips = _chips()
        for j, (px, py) in enumerate(chips):
            slot = land_ref.at[4 * px + 2 * py + c]
            pltpu.make_async_remote_copy(src_ref=slot, dst_ref=slot, send_sem=sems[j], recv_sem=sems[3 + j],
                                         device_id=(x, y, 1 - c), device_id_type=MESH).start()
        token[...] = jnp.zeros_like(token)

    outs = pl.pallas_call(
        body, name=name,
        out_shape=(pltpu.SemaphoreType.DMA(()),) * 6 + (pltpu.HBM(land.shape, land.dtype), jax.ShapeDtypeStruct((8, 128), F32)),
        in_specs=(_HBM, _ANY), out_specs=(_SEM,) * 6 + (_HBM, pl.BlockSpec(memory_space=pltpu.VMEM)),
        input_output_aliases={0: 6}, compiler_params=pltpu.CompilerParams(has_side_effects=_DATAFLOW),
    )(_hbm(land), order)
    return outs[:6], outs[6], outs[7]


def _forward_wait(sems, land_thru, after, name):
    def body(land_ref, *rest):
        sems_ = rest[:6]
        x, y, c, chips = _chips()
        for j, (px, py) in enumerate(chips):
            cp = pltpu.make_async_remote_copy(src_ref=land_ref.at[4 * px + 2 * py + c], dst_ref=land_ref.at[4 * px + 2 * py + 1 - c],
                                              send_sem=sems_[j], recv_sem=sems_[3 + j], device_id=(x, y, 1 - c),
                                              device_id_type=MESH)
            cp.wait_send()
            cp.wait_recv()

    return pl.pallas_call(
        body, name=name, out_shape=pltpu.HBM(land_thru.shape, land_thru.dtype),
        in_specs=(_HBM,) + (_SEM,) * 6 + (_ANY,), out_specs=_HBM, input_output_aliases={0: 0},
        compiler_params=pltpu.CompilerParams(has_side_effects=_DATAFLOW),
    )(land_thru, *sems, after)


def _chip_targets():
    x, y, c, chips = _chips()
    return 2 * x + y, [((px, py, c), 2 * px + py) for px, py in chips]


def _chipsum_start(s, land, order, name):
    def body(s_ref, land_ref, order_ref, *rest):
        sems, token = rest[:6], rest[8]
        mine, targets = _chip_targets()
        for k, (to, ch) in enumerate(targets):
            pltpu.make_async_remote_copy(src_ref=s_ref.at[ch], dst_ref=land_ref.at[mine], send_sem=sems[k], recv_sem=sems[3 + k],
                                         device_id=to, device_id_type=MESH).start()
        token[...] = jnp.zeros_like(token)

    outs = pl.pallas_call(
        body, name=name,
        out_shape=(pltpu.SemaphoreType.DMA(()),) * 6 + (pltpu.HBM(s.shape, s.dtype), pltpu.HBM(land.shape, land.dtype),
                                                        jax.ShapeDtypeStruct((8, 128), F32)),
        in_specs=(_HBM, _HBM, _ANY), out_specs=(_SEM,) * 6 + (_HBM, _HBM, pl.BlockSpec(memory_space=pltpu.VMEM)),
        input_output_aliases={0: 6, 1: 7}, compiler_params=pltpu.CompilerParams(has_side_effects=_DATAFLOW),
    )(_hbm(s), _hbm(land), order)
    return outs[:6], outs[6], outs[7], outs[8]


def _chipsum_wait(sems, s_thru, land_thru, after, name):
    def body(s_ref, land_ref, *rest):
        sems_ = rest[:6]
        mine, targets = _chip_targets()
        for k, (to, ch) in enumerate(targets):
            cp = pltpu.make_async_remote_copy(src_ref=s_ref.at[ch], dst_ref=land_ref.at[ch], send_sem=sems_[k], recv_sem=sems_[3 + k],
                                              device_id=to, device_id_type=MESH)
            cp.wait_send()
            cp.wait_recv()

    return pl.pallas_call(
        body, name=name, out_shape=(pltpu.HBM(s_thru.shape, s_thru.dtype), pltpu.HBM(land_thru.shape, land_thru.dtype)),
        in_specs=(_HBM, _HBM) + (_SEM,) * 6 + (_ANY,), out_specs=(_HBM, _HBM), input_output_aliases={0: 0, 1: 1},
        compiler_params=pltpu.CompilerParams(has_side_effects=_DATAFLOW),
    )(s_thru, land_thru, *sems, after)[1]


def _pair_start(p, order, name):
    def body(p_ref, land_ref, order_ref, *rest):
        sems, token = rest[:8], rest[10]
        x, y, c = _me()
        for ch in range(4):
            pltpu.make_async_remote_copy(src_ref=p_ref.at[ch, 1 - c], dst_ref=land_ref.at[ch], send_sem=sems[ch],
                                         recv_sem=sems[4 + ch], device_id=(x, y, 1 - c), device_id_type=MESH).start()
        token[...] = jnp.zeros_like(token)

    land = lax.empty((4,) + p.shape[2:], p.dtype)
    outs = pl.pallas_call(
        body, name=name,
        out_shape=(pltpu.SemaphoreType.DMA(()),) * 8 + (pltpu.HBM(p.shape, p.dtype), pltpu.HBM(land.shape, land.dtype),
                                                        jax.ShapeDtypeStruct((8, 128), F32)),
        in_specs=(_HBM, _HBM, _ANY), out_specs=(_SEM,) * 8 + (_HBM, _HBM, pl.BlockSpec(memory_space=pltpu.VMEM)),
        input_output_aliases={0: 8, 1: 9}, compiler_params=pltpu.CompilerParams(has_side_effects=_DATAFLOW),
    )(_hbm(p), _hbm(land), order)
    return outs[:8], outs[8], outs[9], outs[10]


def _pair_wait(sems, p_thru, land_thru, after, name):
    def body(p_ref, land_ref, *rest):
        sems_ = rest[:8]
        x, y, c = _me()
        for ch in range(4):
            cp = pltpu.make_async_remote_copy(src_ref=p_ref.at[ch, 1 - c], dst_ref=land_ref.at[ch], send_sem=sems_[ch],
                                              recv_sem=sems_[4 + ch], device_id=(x, y, 1 - c), device_id_type=MESH)
            cp.wait_send()
            cp.wait_recv()

    return pl.pallas_call(
        body, name=name, out_shape=(pltpu.HBM(p_thru.shape, p_thru.dtype), pltpu.HBM(land_thru.shape, land_thru.dtype)),
        in_specs=(_HBM, _HBM) + (_SEM,) * 8 + (_ANY,), out_specs=(_HBM, _HBM), input_output_aliases={0: 0, 1: 1},
        compiler_params=pltpu.CompilerParams(has_side_effects=_DATAFLOW),
    )(p_thru, land_thru, *sems, after)


def _sibling_start(p, order, name):
    def body(p_ref, land_ref, order_ref, send_sem, recv_sem, p_thru, land_thru, token):
        x, y, c = _me()
        pltpu.make_async_remote_copy(src_ref=p_ref, dst_ref=land_ref, send_sem=send_sem, recv_sem=recv_sem,
                                     device_id=(x, y, 1 - c), device_id_type=MESH).start()
        token[...] = jnp.zeros_like(token)

    land = lax.empty(p.shape, p.dtype)
    outs = pl.pallas_call(
        body, name=name,
        out_shape=(pltpu.SemaphoreType.DMA(()),) * 2 + (pltpu.HBM(p.shape, p.dtype), pltpu.HBM(p.shape, p.dtype),
                                                        jax.ShapeDtypeStruct((8, 128), F32)),
        in_specs=(_HBM, _HBM, _ANY), out_specs=(_SEM,) * 2 + (_HBM, _HBM, pl.BlockSpec(memory_space=pltpu.VMEM)),
        input_output_aliases={0: 2, 1: 3}, compiler_params=pltpu.CompilerParams(has_side_effects=_DATAFLOW),
    )(_hbm(p), _hbm(land), order)
    return outs[:2], outs[2], outs[3], outs[4]


def _sibling_wait(sems, p_thru, land_thru, after, name):
    def body(p_ref, land_ref, send_sem, recv_sem, after_ref, p_dead, got_ref):
        x, y, c = _me()
        cp = pltpu.make_async_remote_copy(src_ref=p_ref, dst_ref=land_ref, send_sem=send_sem, recv_sem=recv_sem,
                                          device_id=(x, y, 1 - c), device_id_type=MESH)
        cp.wait_send()
        cp.wait_recv()

    return pl.pallas_call(
        body, name=name, out_shape=(pltpu.HBM(p_thru.shape, p_thru.dtype), pltpu.HBM(land_thru.shape, land_thru.dtype)),
        in_specs=(_HBM, _HBM, _SEM, _SEM, _ANY), out_specs=(_HBM, _HBM), input_output_aliases={0: 0, 1: 1},
        compiler_params=pltpu.CompilerParams(has_side_effects=_DATAFLOW),
    )(p_thru, land_thru, *sems, after)


def _mm(a, b, *, mode, name, out_dtypes=(F32,), epilogue=None, extras=(), tm=1024, tn=1024, tk=2048,
        col_blocked_b=False, col_blocked_out=False, order=None):
    CB = 1024
    if col_blocked_b:
        assert mode in ("nn", "nt") and b.shape[2] == CB
        (M, K), N = a.shape, (b.shape[0] * CB if mode == "nn" else b.shape[1])
        assert mode == "nn" or tk % CB == 0
        tn = CB if mode == "nn" else tn
    elif mode == "nn":
        (M, K), N = a.shape, b.shape[1]
    elif mode == "nt":
        (M, K), N = a.shape, b.shape[0]
    else:
        (K, M), N = a.shape, b.shape[1]
    if col_blocked_out:
        assert len(out_dtypes) == 1 and N % CB == 0
        tn = CB
    tm, tn, tk = min(tm, M), min(tn, N), min(tk, K)
    assert M % tm == 0 and N % tn == 0 and K % tk == 0, (M, N, K, tm, tn, tk)
    nk = K // tk
    ne, no = len(extras), len(out_dtypes)
    dims = {"nn": (((1,), (0,)), ((), ())), "nt": (((1,), (1,)), ((), ())), "tn": (((0,), (0,)), ((), ()))}[mode]

    no_ = 0 if order is None else 1

    def body(a_ref, b_ref, *rest):
        rest = rest[no_:]
        ex, outs = rest[:ne], rest[ne:ne + no]

        def finish(acc):
            res = epilogue(acc, *[e[...] for e in ex]) if epilogue is not None else (acc,)
            for o, r in zip(outs, res):
                o[...] = r.astype(o.dtype)

        if col_blocked_b and mode == "nt":
            part = sum(lax.dot_general(a_ref[:, q * CB:(q + 1) * CB], b_ref[q], dims, preferred_element_type=F32)
                       for q in range(tk // CB))
        else:
            part = lax.dot_general(a_ref[...], b_ref[...], dims, preferred_element_type=F32)
        if nk == 1:
            finish(part)
        else:
            acc_ref = rest[-1]
            k = pl.program_id(2)

            @pl.when(k == 0)
            def _():
                acc_ref[...] = part

            @pl.when(k > 0)
            def _():
                acc_ref[...] += part

            @pl.when(k == nk - 1)
            def _():
                finish(acc_ref[...])

    a_spec = {"nn": pl.BlockSpec((tm, tk), lambda i, j, k: (i, k)), "nt": pl.BlockSpec((tm, tk), lambda i, j, k: (i, k)),
              "tn": pl.BlockSpec((tk, tm), lambda i, j, k: (k, i))}[mode]
    b_spec = {"nn": pl.BlockSpec((tk, tn), lambda i, j, k: (k, j)), "nt": pl.BlockSpec((tn, tk), lambda i, j, k: (j, k)),
              "tn": pl.BlockSpec((tk, tn), lambda i, j, k: (k, j))}[mode]
    if col_blocked_b:
        b_spec = (pl.BlockSpec((None, tk, CB), lambda i, j, k: (j, k, 0)) if mode == "nn"
                  else pl.BlockSpec((tk // CB, tn, CB), lambda i, j, k: (k, j, 0)))
    e_spec = pl.BlockSpec((tm, tn), lambda i, j, k: (i, j))
    o_spec, o_dims = e_spec, (M, N)
    if col_blocked_out:
        o_spec, o_dims = pl.BlockSpec((None, tm, CB), lambda i, j, k: (j, i, 0)), (N // CB, M, CB)
    outs = pl.pallas_call(
        body, name=name, grid=(M // tm, N // tn, nk),
        in_specs=[a_spec, b_spec] + [_ANY] * no_ + [e_spec] * ne, out_specs=[o_spec] * no,
        out_shape=[jax.ShapeDtypeStruct(o_dims, dt) for dt in out_dtypes],
        scratch_shapes=[pltpu.VMEM((tm, tn), F32)] if nk > 1 else [],
        compiler_params=pltpu.CompilerParams(dimension_semantics=("parallel", "parallel", "arbitrary")),
    )(a, b, *([] if order is None else [order]), *extras)
    return outs if no > 1 else outs[0]


def _dw_half(a, b, side, *, axis, name, add=None, order=None, tile=1024, tk=2048):
    (K, M), N = a.shape, b.shape[1]
    tk = min(tk, K)
    nk = K // tk
    if axis == "m":
        tm, tn = tile, min(N, 1024)
        grid, o_dims = (4, N // tn, nk), (4, tile, N)
        a_spec = pl.BlockSpec((tk, tm), lambda q, j, k, s: (k, 2 * q + s[0]))
        b_spec = pl.BlockSpec((tk, tn), lambda q, j, k, s: (k, j))
        o_spec = pl.BlockSpec((None, tm, tn), lambda q, j, k, s: (q, 0, j))
    else:
        tm, tn = min(M, 1024), tile
        grid, o_dims = (M // tm, 4, nk), (4, M, tile)
        a_spec = pl.BlockSpec((tk, tm), lambda i, q, k, s: (k, i))
        b_spec = pl.BlockSpec((tk, tn), lambda i, q, k, s: (k, 2 * q + s[0]))
        o_spec = pl.BlockSpec((None, tm, tn), lambda i, q, k, s: (q, i, 0))
    n_order, n_add = int(order is not None), int(add is not None)
    n_out = 1 + n_add

    def body(s_ref, a_ref, b_ref, *rest):
        rest = rest[n_order:]
        outs, acc_ref = rest[n_add:n_add + n_out], rest[-1]
        k = pl.program_id(2)
        part = _dot_tn(a_ref[...], b_ref[...])

        @pl.when(k == 0)
        def _():
            acc_ref[...] = part

        @pl.when(k > 0)
        def _():
            acc_ref[...] += part

        @pl.when(k == nk - 1)
        def _():
            res = acc_ref[...] + rest[0][...].astype(F32) if n_add else acc_ref[...]
            for o in outs:
                o[...] = res.astype(o.dtype)

    outs = pl.pallas_call(
        body, name=name, out_shape=[jax.ShapeDtypeStruct(o_dims, BF16)] * n_out,
        grid_spec=pltpu.PrefetchScalarGridSpec(
            num_scalar_prefetch=1, grid=grid, in_specs=[a_spec, b_spec] + [_ANY] * n_order + [o_spec] * n_add,
            out_specs=[o_spec] * n_out, scratch_shapes=[pltpu.VMEM((tm, tn), F32)]),
        compiler_params=pltpu.CompilerParams(dimension_semantics=("arbitrary", "arbitrary", "arbitrary")),
    )(side, a, b, *([order] if n_order else []), *([add] if n_add else []))
    return outs if n_add else outs[0]


def _norm_fwd(x, g, sc, sh, name, resid=None):
    B, S, Dm = x.shape
    ts = min(S, 256)
    tok = pl.BlockSpec((None, ts, Dm), lambda b, i: (b, i, 0))
    row = pl.BlockSpec((None, 1, Dm), lambda b, i: (b, 0, 0))
    par = pl.BlockSpec((1, Dm), lambda b, i: (0, 0))

    def body(*refs):
        if resid is not None:
            x_ref, br_ref, gt_ref, g_ref, sc_ref, sh_ref, xo_ref, h_ref = refs
            xv = x_ref[...] + gt_ref[...] * br_ref[...]
            xo_ref[...] = xv
        else:
            x_ref, g_ref, sc_ref, sh_ref, h_ref = refs
            xv = x_ref[...]
        r = lax.rsqrt(jnp.mean(xv * xv, axis=-1, keepdims=True) + EPS)
        h_ref[...] = ((xv * r * g_ref[...]) * (1.0 + sc_ref[...]) + sh_ref[...]).astype(BF16)

    h_shape = jax.ShapeDtypeStruct((B, S, Dm), BF16)
    if resid is not None:
        return pl.pallas_call(body, name=name, grid=(B, S // ts), in_specs=[tok, tok, row, par, row, row],
                              out_specs=[tok, tok], out_shape=[jax.ShapeDtypeStruct((B, S, Dm), F32), h_shape],
                              )(x, resid[0], resid[1], g, sc, sh)
    return pl.pallas_call(body, name=name, grid=(B, S // ts), in_specs=[tok, par, row, row], out_specs=tok,
                          out_shape=h_shape)(x, g, sc, sh)


def _norm_bwd(x, g, name, *, sc=None, dh=None, dres=None, tgt=None, br=None, gate=None, x_is_prev=False):
    B, S, Dm = x.shape
    ts = min(S, 256)
    final = tgt is not None
    has_br = br is not None
    tok = pl.BlockSpec((None, ts, Dm), lambda b, i: (b, i, 0))
    row = pl.BlockSpec((None, 1, Dm), lambda b, i: (b, 0, 0))
    par = pl.BlockSpec((1, Dm), lambda b, i: (0, 0))
    ins, in_specs = [x, g], [tok, par]
    if final:
        ins, in_specs = ins + [tgt], in_specs + [tok]
    else:
        ins, in_specs = ins + [sc, dh], in_specs + [row, tok]
    if dres is not None:
        ins, in_specs = ins + [dres], in_specs + [tok]
    if has_br:
        ins, in_specs = ins + [br, gate], in_specs + [tok, row]
    n_in = len(ins)
    out_shape = [jax.ShapeDtypeStruct((B, S, Dm), F32), jax.ShapeDtypeStruct((1, Dm), F32)]
    out_specs = [tok, par]
    if final:
        out_shape.append(jax.ShapeDtypeStruct((1, 128), F32))
        out_specs.append(pl.BlockSpec((1, 128), lambda b, i: (0, 0)))
    else:
        out_shape += [jax.ShapeDtypeStruct((B, 1, Dm), F32)] * 2
        out_specs += [row, row]
    if has_br:
        out_shape += [jax.ShapeDtypeStruct((B, S, Dm), BF16), jax.ShapeDtypeStruct((B, 1, Dm), F32)]
        out_specs += [tok, row]

    def body(*refs):
        it = iter(refs[:n_in])
        outs = iter(refs[n_in:])
        x_ref, g_ref = next(it), next(it)
        b, i = pl.program_id(0), pl.program_id(1)
        first, first_row = (b == 0) & (i == 0), i == 0
        xv, gv = x_ref[...], g_ref[...]
        if x_is_prev:
            xv = xv + refs[n_in - 1][...] * refs[n_in - 2][...]
        r = lax.rsqrt(jnp.mean(xv * xv, axis=-1, keepdims=True) + EPS)
        n = xv * r
        dx_ref, dg_ref = next(outs), next(outs)

        def acc(ref, val, init):
            @pl.when(init)
            def _():
                ref[...] = val

            @pl.when(jnp.logical_not(init))
            def _():
                ref[...] += val

        if final:
            t_ref = next(it)
            loss_ref = next(outs)
            e = n * gv - t_ref[...]
            acc(loss_ref, jnp.zeros((1, 128), F32) + 0.5 * jnp.sum(e * e) / Dm, first)
            dyg = e * (1.0 / Dm)
        else:
            sc_ref, dh_ref = next(it), next(it)
            dsc_ref, dsh_ref = next(outs), next(outs)
            dhv = dh_ref[...].astype(F32)
            acc(dsh_ref, jnp.sum(dhv, axis=0, keepdims=True), first_row)
            acc(dsc_ref, jnp.sum(dhv * (n * gv), axis=0, keepdims=True), first_row)
            dyg = dhv * (1.0 + sc_ref[...])
        acc(dg_ref, jnp.sum(dyg * n, axis=0, keepdims=True), first)
        dn = dyg * gv
        dx = r * (dn - n * jnp.mean(dn * n, axis=-1, keepdims=True))
        if dres is not None:
            dx = dx + next(it)[...]
        dx_ref[...] = dx
        if has_br:
            br_ref, gt_ref = next(it), next(it)
            dbr_ref, dgt_ref = next(outs), next(outs)
            dbr_ref[...] = (dx * gt_ref[...]).astype(BF16)
            acc(dgt_ref, jnp.sum(dx * br_ref[...], axis=0, keepdims=True), first_row)

    outs = pl.pallas_call(body, name=name, grid=(B, S // ts), in_specs=in_specs, out_specs=out_specs, out_shape=out_shape,
                          compiler_params=pltpu.CompilerParams(dimension_semantics=("arbitrary", "arbitrary")))(*ins)
    res = dict(dx=outs[0], dg=outs[1])
    if final:
        res["loss"] = outs[2]
    else:
        res["dsc"], res["dsh"] = outs[2], outs[3]
    if has_br:
        res["dbr"], res["dgate"] = outs[-2], outs[-1]
    return res


def _gm_heads(vg, lng, lnb):
    res = []
    for h in range(GM_H):
        sl = slice(h * 128, (h + 1) * 128)
        vh = vg[:, sl]
        xc = vh - jnp.mean(vh, axis=-1, keepdims=True)
        rstd = lax.rsqrt(jnp.mean(xc * xc, axis=-1, keepdims=True) + 1e-5)
        xhat = xc * rstd
        res.append((xhat, rstd, xhat * lng[:, sl] + lnb[:, sl]))
    return res


def _gm_gate(heads, wt_ref, bsx, nch):
    cols = []
    for h in range(GM_H):
        vn = heads[h][2].astype(BF16)
        rows = [_dot(wt_ref[h], vn[c * CHUNK:(c + 1) * CHUNK]) + bsx[:, h * 128:(h + 1) * 128] for c in range(nch)]
        cols.append(jnp.concatenate(rows, axis=0) if nch > 1 else rows[0])
    return jnp.concatenate(cols, axis=1)


def _gm_specs(S):
    tb = min(S, 512)
    u = pl.BlockSpec((None, tb, GM_W), lambda b, i: (b, i, OFF["u"] // GM_W))
    v = pl.BlockSpec((None, tb, GM_W), lambda b, i: (b, i, OFF["v"] // GM_W))
    tok = pl.BlockSpec((None, tb, GM_W), lambda b, i: (b, i, 0))
    return tb, u, v, tok


def _gmlp_fwd(P, lng, lnb, wt, bsx, og, name):
    B, S, _ = P.shape
    tb, u_spec, v_spec, tok = _gm_specs(S)
    nch = tb // CHUNK

    def body(u_ref, v_ref, lng_ref, lnb_ref, wt_ref, bsx_ref, og_ref, o_ref):
        heads = _gm_heads(_gelu(v_ref[...]), lng_ref[...], lnb_ref[...])
        y = _gelu(u_ref[...]) * _gm_gate(heads, wt_ref, bsx_ref[...], nch)
        r = lax.rsqrt(jnp.mean(y * y, axis=-1, keepdims=True) + EPS)
        o_ref[...] = (y * r * og_ref[...]).astype(BF16)

    return pl.pallas_call(
        body, name=name, grid=(B, S // tb),
        in_specs=[u_spec, v_spec, _full((1, GM_W)), _full((1, GM_W)), _full((GM_H, 128, 128)), _full((128, GM_W)), _full((1, GM_W))],
        out_specs=tok, out_shape=jax.ShapeDtypeStruct((B, S, GM_W + ATT_W + SSM_W), BF16))(P, P, lng, lnb, wt, bsx, og)


def _gmlp_bwd(P, dcat, lng, lnb, wt, wtT, bsx, og, name):
    B, S, _ = P.shape
    tb, u_spec, v_spec, tok = _gm_specs(S)
    nch = tb // CHUNK
    do_spec = pl.BlockSpec((None, tb, GM_W), lambda b, i: (b, i, 0))

    def body(u_ref, v_ref, do_ref, lng_ref, lnb_ref, wt_ref, wtT_ref, bsx_ref, og_ref,
             du_ref, dv_ref, dlng_ref, dlnb_ref, dws_ref, dbsx_ref, dog_ref):
        first = (pl.program_id(0) == 0) & (pl.program_id(1) == 0)

        @pl.when(first)
        def _():
            for ref in (dlng_ref, dlnb_ref, dws_ref, dbsx_ref, dog_ref):
                ref[...] = jnp.zeros(ref.shape, F32)

        u, v, lng = u_ref[...], v_ref[...], lng_ref[...]
        ug = _gelu(u)
        heads = _gm_heads(_gelu(v), lng, lnb_ref[...])
        gate = _gm_gate(heads, wt_ref, bsx_ref[...], nch)
        y = ug * gate
        r = lax.rsqrt(jnp.mean(y * y, axis=-1, keepdims=True) + EPS)
        yn = y * r
        dout = do_ref[...].astype(F32)
        dog_ref[...] += jnp.sum(dout * yn, axis=0, keepdims=True)
        dyn = dout * og_ref[...]
        dy = r * (dyn - yn * jnp.mean(dyn * yn, axis=-1, keepdims=True))
        du_ref[...] = (dy * gate * _gelu_grad(u)).astype(BF16)
        dgate = dy * ug
        tril = lax.broadcasted_iota(jnp.int32, (128, 128), 0) >= lax.broadcasted_iota(jnp.int32, (128, 128), 1)
        dvg = []
        for h in range(GM_H):
            sl = slice(h * 128, (h + 1) * 128)
            xhat, rstd, vn = heads[h]
            vnb = vn.astype(BF16)
            dgh = dgate[:, sl]
            dgb = dgh.astype(BF16)
            dbs = jnp.zeros((128, 128), F32)
            dw = jnp.zeros((128, 128), F32)
            dvn = []
            for c in range(nch):
                rs = slice(c * CHUNK, (c + 1) * CHUNK)
                dbs = dbs + dgh[rs]
                dw = dw + _dot_nt(dgb[rs], vnb[rs])
                dvn.append(_dot(wtT_ref[h], dgb[rs]))
            dvn = jnp.concatenate(dvn, axis=0) if nch > 1 else dvn[0]
            dbsx_ref[:, sl] += dbs
            dws_ref[h] += jnp.where(tril, dw, 0.0)
            dlng_ref[:, sl] += jnp.sum(dvn * xhat, axis=0, keepdims=True)
            dlnb_ref[:, sl] += jnp.sum(dvn, axis=0, keepdims=True)
            dxh = dvn * lng[:, sl]
            dvg.append(rstd * (dxh - jnp.mean(dxh, axis=-1, keepdims=True) - xhat * jnp.mean(dxh * xhat, axis=-1, keepdims=True)))
        dv_ref[...] = (jnp.concatenate(dvg, axis=1) * _gelu_grad(v)).astype(BF16)

    p512, w3 = _full((1, GM_W)), _full((GM_H, 128, 128))
    return pl.pallas_call(
        body, name=name, grid=(B, S // tb),
        in_specs=[u_spec, v_spec, do_spec, p512, p512, w3, w3, _full((128, GM_W)), p512],
        out_specs=[tok, tok, p512, p512, w3, _full((128, GM_W)), p512],
        out_shape=[jax.ShapeDtypeStruct((B, S, GM_W), BF16)] * 2 + [
            jax.ShapeDtypeStruct((1, GM_W), F32), jax.ShapeDtypeStruct((1, GM_W), F32),
            jax.ShapeDtypeStruct((GM_H, 128, 128), F32), jax.ShapeDtypeStruct((128, GM_W), F32),
            jax.ShapeDtypeStruct((1, GM_W), F32)],
        compiler_params=pltpu.CompilerParams(dimension_semantics=("arbitrary", "arbitrary")),
    )(P, P, dcat, lng, lnb, wt, wtT, bsx, og)


def _lane_half():
    return lax.broadcasted_iota(jnp.int32, (128, 128), 1) // 64


def _att_stack(x, kvh, dtype):
    half = _lane_half()
    rows = []
    for g in range(4):
        i = kvh * 4 + g
        pair = x[:, (i // 2) * 128:(i // 2 + 1) * 128]
        if i % 2 != kvh:
            pair = pltpu.roll(pair, 64, 1)
        rows.append(jnp.where(half == kvh, pair, 0.0))
    return jnp.concatenate(rows, axis=0).astype(dtype)


def _att_unstack(pairs, y, kvh):
    half = _lane_half()
    for g in range(4):
        i = kvh * 4 + g
        piece = y[g * 128:(g + 1) * 128]
        if i % 2 != kvh:
            piece = pltpu.roll(piece, 64, 1)
        pairs[i // 2] = jnp.where(half == i % 2, piece, pairs[i // 2])
    return pairs


def _att_probs(qb, k2, st, sink_ref, kvh):
    qm = _att_stack(qb, kvh, BF16)
    s = _dot_nt(qm, k2) * (64 ** -0.5)
    qi = lax.broadcasted_iota(jnp.int32, (512, 256), 0) % 128
    kj = lax.broadcasted_iota(jnp.int32, (512, 256), 1)
    diff = qi + 128 - kj
    valid = (diff >= 0) & (diff < 128) & (st + kj - 128 >= 0)
    s = jnp.where(valid, s, NEG_INF)
    grp = lax.broadcasted_iota(jnp.int32, (512, 1), 0) // 128
    sink = jnp.zeros((512, 1), F32)
    for g in range(4):
        sink = jnp.where(grp == g, sink_ref[kvh * 4 + g], sink)
    m = jnp.maximum(jnp.max(s, axis=-1, keepdims=True), sink)
    e = jnp.exp(s - m)
    esink = jnp.exp(sink - m)
    inv = 1.0 / (jnp.sum(e, axis=-1, keepdims=True) + esink)
    return qm, e * inv, esink * inv


def _att_specs(S):
    q = pl.BlockSpec((None, S, ATT_W), lambda b: (b, 0, OFF["q"] // ATT_W))
    k = pl.BlockSpec((None, S, KV_W), lambda b: (b, 0, OFF["k"] // KV_W))
    v = pl.BlockSpec((None, S, KV_W), lambda b: (b, 0, OFF["vv"] // KV_W))
    tok = pl.BlockSpec((None, S, ATT_W), lambda b: (b, 0, 0))
    kv = pl.BlockSpec((None, S, KV_W), lambda b: (b, 0, 0))
    return q, k, v, tok, kv


_SMEM = pl.BlockSpec(memory_space=pltpu.SMEM)


def _attn_fwd(P, sinks, og, cat, name):
    B, S, _ = P.shape
    q_spec, k_spec, v_spec, _, _ = _att_specs(S)
    tok = pl.BlockSpec((None, S, ATT_W), lambda b: (b, 0, GM_W // ATT_W))

    def body(q_ref, k_ref, v_ref, sink_ref, og_ref, cat_ref, o_ref, kpad, vpad):
        kpad[0:128, :] = jnp.zeros((128, KV_W), BF16)
        vpad[0:128, :] = jnp.zeros((128, KV_W), BF16)
        kpad[128:, :] = k_ref[...].astype(BF16)
        vpad[128:, :] = v_ref[...].astype(BF16)

        def step(n, carry):
            st = pl.multiple_of(n * 128, 128)
            qb = q_ref[pl.ds(st, 128), :]
            k2, v2 = kpad[pl.ds(st, 256), :], vpad[pl.ds(st, 256), :]
            pairs = [jnp.zeros((128, 128), F32)] * 4
            for kvh in range(2):
                _, p, _ = _att_probs(qb, k2, st, sink_ref, kvh)
                pairs = _att_unstack(pairs, _dot(p.astype(BF16), v2), kvh)
            o = jnp.concatenate(pairs, axis=1)
            r = lax.rsqrt(jnp.mean(o * o, axis=-1, keepdims=True) + EPS)
            o_ref[pl.ds(st, 128), :] = (o * r * og_ref[...]).astype(BF16)
            return carry

        lax.fori_loop(0, S // 128, step, 0)

    return pl.pallas_call(
        body, name=name, grid=(B,), in_specs=[q_spec, k_spec, v_spec, _SMEM, _full((1, ATT_W)), _ANY], out_specs=tok,
        out_shape=jax.ShapeDtypeStruct(cat.shape, BF16), input_output_aliases={5: 0},
        scratch_shapes=[pltpu.VMEM((S + 128, KV_W), BF16)] * 2)(P, P, P, sinks, og, cat)


def _attn_bwd(P, dcat, sinks, og, name):
    B, S, _ = P.shape
    q_spec, k_spec, v_spec, tok, kv = _att_specs(S)
    do_spec = pl.BlockSpec((None, S, ATT_W), lambda b: (b, 0, GM_W // ATT_W))

    def body(q_ref, k_ref, v_ref, do_ref, sink_ref, og_ref, dq_ref, dk_ref, dv_ref, dsink_ref, dog_ref,
             kpad, vpad, dkpad, dvpad):
        @pl.when(pl.program_id(0) == 0)
        def _():
            dsink_ref[...] = jnp.zeros((8, 128), F32)
            dog_ref[...] = jnp.zeros((1, ATT_W), F32)

        kpad[0:128, :] = jnp.zeros((128, KV_W), BF16)
        vpad[0:128, :] = jnp.zeros((128, KV_W), BF16)
        kpad[128:, :] = k_ref[...].astype(BF16)
        vpad[128:, :] = v_ref[...].astype(BF16)
        dkpad[...] = jnp.zeros((S + 128, KV_W), F32)
        dvpad[...] = jnp.zeros((S + 128, KV_W), F32)
        half = _lane_half()
        head_row = lax.broadcasted_iota(jnp.int32, (8, 128), 0)

        def step(n, carry):
            st = pl.multiple_of(n * 128, 128)
            qb = q_ref[pl.ds(st, 128), :]
            k2, v2 = kpad[pl.ds(st, 256), :], vpad[pl.ds(st, 256), :]
            saved, pairs = [], [jnp.zeros((128, 128), F32)] * 4
            for kvh in range(2):
                qm, p, psink = _att_probs(qb, k2, st, sink_ref, kvh)
                o = _dot(p.astype(BF16), v2)
                saved.append((qm, p, psink, o))
                pairs = _att_unstack(pairs, o, kvh)
            o = jnp.concatenate(pairs, axis=1)
            r = lax.rsqrt(jnp.mean(o * o, axis=-1, keepdims=True) + EPS)
            on = o * r
            dout = do_ref[pl.ds(st, 128), :].astype(F32)
            dog_ref[...] += jnp.sum(dout * on, axis=0, keepdims=True)
            dyn = dout * og_ref[...]
            do = r * (dyn - on * jnp.mean(dyn * on, axis=-1, keepdims=True))
            dq_pairs = [jnp.zeros((128, 128), F32)] * 4
            dsink = jnp.zeros((8, 128), F32)
            for kvh in range(2):
                qm, p, psink, og_ = saved[kvh]
                dog = _att_stack(do, kvh, F32)
                delta = jnp.sum(dog * jnp.where(jnp.concatenate([half] * 4, axis=0) == kvh, og_, 0.0), axis=-1, keepdims=True)
                dogb, pb = dog.astype(BF16), p.astype(BF16)
                dvpad[pl.ds(st, 256), :] += _dot_tn(pb, dogb)
                dp = _dot_nt(dogb, v2)
                ds = (p * (dp - delta) * (64 ** -0.5)).astype(BF16)
                sd = psink * delta
                for g in range(4):
                    dsink = dsink - jnp.where(head_row == kvh * 4 + g, jnp.sum(sd[g * 128:(g + 1) * 128]), 0.0)
                dq_pairs = _att_unstack(dq_pairs, _dot(ds, k2), kvh)
                dkpad[pl.ds(st, 256), :] += _dot_tn(ds, qm)
            dsink_ref[...] += dsink
            dq_ref[pl.ds(st, 128), :] = jnp.concatenate(dq_pairs, axis=1).astype(BF16)
            return carry

        lax.fori_loop(0, S // 128, step, 0)
        dk_ref[...] = dkpad[128:, :].astype(BF16)
        dv_ref[...] = dvpad[128:, :].astype(BF16)

    return pl.pallas_call(
        body, name=name, grid=(B,),
        in_specs=[q_spec, k_spec, v_spec, do_spec, _SMEM, _full((1, ATT_W))],
        out_specs=[tok, kv, kv, _full((8, 128)), _full((1, ATT_W))],
        out_shape=[jax.ShapeDtypeStruct((B, S, ATT_W), BF16), jax.ShapeDtypeStruct((B, S, KV_W), BF16),
                   jax.ShapeDtypeStruct((B, S, KV_W), BF16), jax.ShapeDtypeStruct((8, 128), F32),
                   jax.ShapeDtypeStruct((1, ATT_W), F32)],
        scratch_shapes=[pltpu.VMEM((S + 128, KV_W), BF16)] * 2 + [pltpu.VMEM((S + 128, KV_W), F32)] * 2,
        compiler_params=pltpu.CompilerParams(dimension_semantics=("arbitrary",)),
    )(P, P, P, dcat, sinks, og)


CONV_TC = 256
CONV_RC = 64


def _conv_taps(ext, r0):
    return [ext[pl.ds(r0 + 8 - k, CONV_RC), :] for k in range(4)]


def _conv_pre(taps, w_ref, b_ref):
    acc = b_ref[...] + w_ref[3:4, :] * taps[0]
    for k in range(1, 4):
        acc = acc + w_ref[3 - k:4 - k, :] * taps[k]
    return acc


def _conv_fwd(P, w8, b, name):
    B, S, _ = P.shape
    nj = CONV_CH // CONV_TC
    x_spec = pl.BlockSpec((None, S, CONV_TC), lambda b_, j: (b_, 0, OFF["xbc"] // CONV_TC + j))
    tok = pl.BlockSpec((None, S, CONV_TC), lambda b_, j: (b_, 0, j))

    def body(x_ref, w_ref, b_ref, o_ref, ext):
        ext[0:8, :] = jnp.zeros((8, CONV_TC), F32)
        ext[8:, :] = x_ref[...]
        for r0 in range(0, S, CONV_RC):
            pre = _conv_pre(_conv_taps(ext, r0), w_ref, b_ref)
            o_ref[pl.ds(r0, CONV_RC), :] = pre * _sigmoid(pre)

    return pl.pallas_call(
        body, name=name, grid=(B, nj),
        in_specs=[x_spec, pl.BlockSpec((8, CONV_TC), lambda b_, j: (0, j)), pl.BlockSpec((1, CONV_TC), lambda b_, j: (0, j))],
        out_specs=tok, out_shape=jax.ShapeDtypeStruct((B, S, CONV_CH), F32),
        scratch_shapes=[pltpu.VMEM((S + 8, CONV_TC), F32)])(P, w8, b)


def _conv_bwd(P, dact, w8, b, name):
    B, S, _ = P.shape
    nj = CONV_CH // CONV_TC
    x_spec = pl.BlockSpec((None, S, CONV_TC), lambda j, b_: (b_, 0, OFF["xbc"] // CONV_TC + j))
    tok = pl.BlockSpec((None, S, CONV_TC), lambda j, b_: (b_, 0, j))
    w_spec = pl.BlockSpec((8, CONV_TC), lambda j, b_: (0, j))
    b_spec = pl.BlockSpec((1, CONV_TC), lambda j, b_: (0, j))

    def body(x_ref, d_ref, w_ref, b_ref, dx_ref, dw_ref, db_ref, ext, extd):
        @pl.when(pl.program_id(1) == 0)
        def _():
            dw_ref[...] = jnp.zeros((8, CONV_TC), F32)
            db_ref[...] = jnp.zeros((1, CONV_TC), F32)

        ext[0:8, :] = jnp.zeros((8, CONV_TC), F32)
        ext[8:, :] = x_ref[...]
        extd[pl.ds(8 + S, 8), :] = jnp.zeros((8, CONV_TC), F32)
        db = jnp.zeros((1, CONV_TC), F32)
        dws = [jnp.zeros((1, CONV_TC), F32)] * 4
        for r0 in range(0, S, CONV_RC):
            taps = _conv_taps(ext, r0)
            pre = _conv_pre(taps, w_ref, b_ref)
            sg = _sigmoid(pre)
            dpre = d_ref[pl.ds(r0, CONV_RC), :] * (sg * (1.0 + pre * (1.0 - sg)))
            extd[pl.ds(8 + r0, CONV_RC), :] = dpre
            db = db + jnp.sum(dpre, axis=0, keepdims=True)
            dws = [dws[i] + jnp.sum(dpre * taps[3 - i], axis=0, keepdims=True) for i in range(4)]
        for r0 in range(0, S, CONV_RC):
            dx = w_ref[3:4, :] * extd[pl.ds(8 + r0, CONV_RC), :]
            for k in range(1, 4):
                dx = dx + w_ref[3 - k:4 - k, :] * extd[pl.ds(8 + r0 + k, CONV_RC), :]
            dx_ref[pl.ds(r0, CONV_RC), :] = dx.astype(BF16)
        db_ref[...] += db
        sub = lax.broadcasted_iota(jnp.int32, (8, CONV_TC), 0)
        dw_ref[...] += sum(jnp.where(sub == i, dws[i], 0.0) for i in range(4))

    return pl.pallas_call(
        body, name=name, grid=(nj, B), in_specs=[x_spec, tok, w_spec, b_spec], out_specs=[tok, w_spec, b_spec],
        out_shape=[jax.ShapeDtypeStruct((B, S, CONV_CH), BF16), jax.ShapeDtypeStruct((8, CONV_CH), F32),
                   jax.ShapeDtypeStruct((1, CONV_CH), F32)],
        scratch_shapes=[pltpu.VMEM((S + 8, CONV_TC), F32), pltpu.VMEM((S + 16, CONV_TC), F32)],
        compiler_params=pltpu.CompilerParams(dimension_semantics=("arbitrary", "arbitrary")),
    )(P, dact, w8, b)


def _ssd_consts():
    hd = np.arange(SSM_W) // SSM_HD
    E = (np.arange(128)[:, None] == hd[None, :]).astype(np.float32)
    tri = (np.arange(128)[:, None] >= np.arange(128)[None, :]).astype(np.float32)
    return jnp.asarray(E, BF16), jnp.asarray(E.T, BF16), jnp.asarray(tri, BF16), jnp.asarray(tri.T, BF16)


def _pieces(x, n):
    out, r = [], x
    for _ in range(n):
        p = r.astype(BF16)
        out.append(p)
        r = r - p.astype(F32)
    return out


def _dot01(x, m01, n):
    return sum(_dot(p, m01) for p in _pieces(x, n))


def _dot01_left(m01, x, n):
    return sum(_dot(m01, p) for p in _pieces(x, n))


def _ssd_pre(xa, dtraw, bias, alog, E, tri):
    lane = lax.broadcasted_iota(jnp.int32, (128, 128), 1)
    pre = dtraw + bias
    dtp = jnp.where(lane < SSM_H, jnp.maximum(pre, 0.0) + jnp.log(1.0 + jnp.exp(-jnp.abs(pre))), 0.0)
    a = -jnp.exp(alog)
    acs = _dot01_left(tri, dtp * a, 3)
    acsT = acs.T
    dtE, acsE = _dot01(dtp, E, 2), _dot01(acs, E, 3)
    X = xa[:, :SSM_W]
    xdt = X * dtE
    wE = jnp.exp(acsE[127:128, :] - acsE)
    eE = jnp.exp(acsE)
    cdE = eE[127:128, :]
    return dict(pre=pre, dtp=dtp, a=a, acs=acs, acsT=acsT, dtE=dtE, acsE=acsE, cdE=cdE, X=X, xdt=xdt, wE=wE, eE=eE)


def _ssd_decay(c, h):
    lm = lax.broadcasted_iota(jnp.int32, (128, 128), 0) >= lax.broadcasted_iota(jnp.int32, (128, 128), 1)
    return jnp.exp(jnp.where(lm, c["acs"][:, h:h + 1] - c["acsT"][h:h + 1, :], NEG_INF))


def _ssd_pair_operands(c, CB, h0):
    lane = lax.broadcasted_iota(jnp.int32, (128, 128), 1)
    L0, L1 = _ssd_decay(c, h0), _ssd_decay(c, h0 + 1)
    M = jnp.concatenate([CB * L0, CB * L1], axis=1).astype(BF16)
    xp = c["xdt"][:, h0 * 64:h0 * 64 + 128]
    BD = jnp.concatenate([jnp.where(lane < 64, xp, 0.0), jnp.where(lane >= 64, xp, 0.0)], axis=0).astype(BF16)
    return L0, L1, M, BD


def _ssd_y(c, xa, state_ref, dskipE):
    per_group, ys = [], []
    for g in range(SSM_G):
        gs = slice(g * 512, (g + 1) * 512)
        Bb = xa[:, SSM_W + g * 128:SSM_W + (g + 1) * 128].astype(BF16)
        Cb = xa[:, SSM_W + 256 + g * 128:SSM_W + 256 + (g + 1) * 128].astype(BF16)
        CB = _dot_nt(Cb, Bb)
        Sg = state_ref[:, gs]
        yoff = _dot(Cb, Sg.astype(BF16)) * c["eE"][:, gs]
        ydiag, pairs = [], []
        for j in range(4):
            ops = _ssd_pair_operands(c, CB, g * 8 + 2 * j)
            pairs.append(ops)
            ydiag.append(_dot(ops[2], ops[3]))
        ys.append(jnp.concatenate(ydiag, axis=1) + yoff)
        per_group.append(dict(Bb=Bb, Cb=Cb, CB=CB, Sg=Sg, yoff=yoff, pairs=pairs))
    Y = jnp.concatenate(ys, axis=1) + c["X"] * dskipE
    return Y, per_group


def _ssd_specs(S, rev):
    nc = S // CHUNK
    cm = (lambda b, i: (b, nc - 1 - i)) if rev else (lambda b, i: (b, i))
    xa = pl.BlockSpec((None, CHUNK, CONV_CH), lambda b, i: cm(b, i) + (0,))
    z = [pl.BlockSpec((None, CHUNK, 256), lambda b, i, q=q: cm(b, i) + (OFF["z"] // 256 + q,)) for q in range(4)]
    dt = pl.BlockSpec((None, CHUNK, 128), lambda b, i: cm(b, i) + (OFF["dt"] // 128,))
    tok = pl.BlockSpec((None, CHUNK, SSM_W), lambda b, i: cm(b, i) + (0,))
    st = pl.BlockSpec((None, None, 128, SSM_W), lambda b, i: cm(b, i) + (0, 0))
    return nc, xa, z, dt, tok, st


def _ssd_fwd(xact, P, bias, alog, dskipE, ng, cat, name):
    B, S, _ = P.shape
    nc, xa_spec, z_specs, dt_spec, _, st_spec = _ssd_specs(S, False)
    tok = pl.BlockSpec((None, CHUNK, SSM_W), lambda b, i: (b, i, 1))
    E, _, tri, _ = _ssd_consts()

    def body(xa_ref, z0, z1, z2, z3, dt_ref, bias_ref, alog_ref, dsk_ref, ng_ref, E_ref, tri_ref, cat_ref, o_ref, sp_ref, state):
        @pl.when(pl.program_id(1) == 0)
        def _():
            state[...] = jnp.zeros((128, SSM_W), F32)

        sp_ref[...] = state[...]
        xa = xa_ref[...]
        c = _ssd_pre(xa, dt_ref[...], bias_ref[...], alog_ref[...], E_ref[...], tri_ref[...])
        Y, groups = _ssd_y(c, xa, state, dsk_ref[...])
        Z = (c["xdt"] * c["wE"]).astype(BF16)
        for g in range(SSM_G):
            gs = slice(g * 512, (g + 1) * 512)
            state[:, gs] = groups[g]["Sg"] * c["cdE"][:, gs] + _dot_tn(groups[g]["Bb"], Z[:, gs])
        zv = jnp.concatenate([z0[...], z1[...], z2[...], z3[...]], axis=1)
        yz = Y * (zv * _sigmoid(zv))
        outs = []
        for g in range(SSM_G):
            yg = yz[:, g * 512:(g + 1) * 512]
            outs.append(yg * lax.rsqrt(jnp.mean(yg * yg, axis=-1, keepdims=True) + EPS))
        o_ref[...] = (jnp.concatenate(outs, axis=1) * ng_ref[...]).astype(BF16)

    return pl.pallas_call(
        body, name=name, grid=(B, nc),
        in_specs=[xa_spec] + z_specs + [dt_spec, _full((1, 128)), _full((1, 128)), _full((1, SSM_W)), _full((1, SSM_W)),
                                        _full((128, SSM_W)), _full((128, 128)), _ANY],
        out_specs=[tok, st_spec],
        out_shape=[jax.ShapeDtypeStruct(cat.shape, BF16), jax.ShapeDtypeStruct((B, nc, 128, SSM_W), F32)],
        scratch_shapes=[pltpu.VMEM((128, SSM_W), F32)], input_output_aliases={12: 0},
        compiler_params=pltpu.CompilerParams(dimension_semantics=("arbitrary", "arbitrary")),
    )(xact, P, P, P, P, P, bias, alog, dskipE, ng, E, tri, cat)


def _ssd_bwd(xact, P, sprev, dcat, bias, alog, dskipE, ng, name):
    B, S, _ = P.shape
    nc, xa_spec, z_specs, dt_spec, tok, st_spec = _ssd_specs(S, True)
    do_spec = pl.BlockSpec((None, CHUNK, SSM_W), lambda b, i: (b, nc - 1 - i, 1))
    E, ET, tri, triT = _ssd_consts()
    dt_out = pl.BlockSpec((None, CHUNK, 128), lambda b, i: (b, nc - 1 - i, 0))

    def body(xa_ref, z0, z1, z2, z3, dt_ref, sp_ref, do_ref, bias_ref, alog_ref, dsk_ref, ng_ref, E_ref, ET_ref, tri_ref,
             triT_ref, dxa_ref, dz_ref, ddt_ref, dbias_ref, dalog_ref, ddsk_ref, dng_ref, dstate):
        first = (pl.program_id(0) == 0) & (pl.program_id(1) == 0)

        @pl.when(first)
        def _():
            for ref in (dbias_ref, dalog_ref, ddsk_ref, dng_ref):
                ref[...] = jnp.zeros(ref.shape, F32)

        @pl.when(pl.program_id(1) == 0)
        def _():
            dstate[...] = jnp.zeros((128, SSM_W), F32)

        xa, ETm = xa_ref[...], ET_ref[...]
        c = _ssd_pre(xa, dt_ref[...], bias_ref[...], alog_ref[...], E_ref[...], tri_ref[...])
        Y, groups = _ssd_y(c, xa, sp_ref, dsk_ref[...])
        X, xdt = c["X"], c["xdt"]
        zv = jnp.concatenate([z0[...], z1[...], z2[...], z3[...]], axis=1)
        sg = _sigmoid(zv)
        zs = zv * sg
        yz = Y * zs
        dout = do_ref[...].astype(F32)
        dyz = []
        for g in range(SSM_G):
            gs = slice(g * 512, (g + 1) * 512)
            yg = yz[:, gs]
            r = lax.rsqrt(jnp.mean(yg * yg, axis=-1, keepdims=True) + EPS)
            yn = yg * r
            dng_ref[:, gs] += jnp.sum(dout[:, gs] * yn, axis=0, keepdims=True)
            dyn = dout[:, gs] * ng_ref[:, gs]
            dyz.append(r * (dyn - yn * jnp.mean(dyn * yn, axis=-1, keepdims=True)))
        dyz = jnp.concatenate(dyz, axis=1)
        dz_ref[...] = (dyz * Y * (sg * (1.0 + zv * (1.0 - sg)))).astype(BF16)
        dY = dyz * zs
        ddsk_ref[...] += jnp.sum(dY * X, axis=0, keepdims=True)
        dX = dY * dsk_ref[...]
        lane = lax.broadcasted_iota(jnp.int32, (128, 128), 1)
        sub = lax.broadcasted_iota(jnp.int32, (128, 128), 0)
        colform = jnp.zeros((128, 128), F32)
        rowform = jnp.zeros((128, 128), F32)
        dxdt, gacsE, dBC = [], [], []
        for g in range(SSM_G):
            gs = slice(g * 512, (g + 1) * 512)
            G = groups[g]
            Bb, Cb, CB, Sg = G["Bb"], G["Cb"], G["CB"], G["Sg"]
            dYg = dY[:, gs]
            dQ = (dYg * c["eE"][:, gs]).astype(BF16)
            dSn = dstate[:, gs]
            dSnb = dSn.astype(BF16)
            cd = c["cdE"][:, gs]
            dC = _dot_nt(dQ, Sg.astype(BF16))
            dSprev = _dot_tn(Cb, dQ) + dSn * cd
            t1 = jnp.broadcast_to(jnp.sum(dSn * Sg * cd, axis=0, keepdims=True), (8, 512))
            colform = colform + jnp.where(sub == 127, _dot01(t1, ETm[gs, :], 2)[0:1, :], 0.0)
            Zg = xdt[:, gs] * c["wE"][:, gs]
            dZ = _dot(Bb, dSnb)
            dB = _dot_nt(Zg.astype(BF16), dSnb)
            U = dZ * Zg
            ga = dYg * G["yoff"] - U
            ga = ga + jnp.where(lax.broadcasted_iota(jnp.int32, (128, 512), 0) == 127, jnp.sum(U, axis=0, keepdims=True), 0.0)
            gacsE.append(ga)
            dxg = [None] * 4
            dCB = jnp.zeros((128, 128), F32)
            for j in range(4):
                h0 = g * 8 + 2 * j
                L0, L1, M, BD = G["pairs"][j]
                dYp = dYg[:, j * 128:(j + 1) * 128].astype(BF16)
                dM = _dot_nt(dYp, BD)
                dBD = _dot_tn(M, dYp)
                dxg[j] = jnp.where(lane < 64, dBD[:128], dBD[128:])
                for t, (h, L) in enumerate(((h0, L0), (h0 + 1, L1))):
                    dMh = dM[:, t * 128:(t + 1) * 128]
                    dCB = dCB + dMh * L
                    Gh = dMh * CB * L
                    colform = colform + jnp.where(lane == h, jnp.sum(Gh, axis=1, keepdims=True), 0.0)
                    rowform = rowform - jnp.where(sub == h, jnp.sum(Gh, axis=0, keepdims=True), 0.0)
            dCBb = dCB.astype(BF16)
            dC = dC + _dot(dCBb, Bb)
            dB = dB + _dot_tn(dCBb, Cb)
            dxdt.append(jnp.concatenate(dxg, axis=1) + dZ * c["wE"][:, gs])
            dBC.append((dB, dC))
            dstate[:, gs] = dSprev
        dxdt = jnp.concatenate(dxdt, axis=1)
        dX = dX + dxdt * c["dtE"]
        ddt = _dot01(dxdt * X, ETm, 2)
        dacs = colform + rowform.T + _dot01(jnp.concatenate(gacsE, axis=1), ETm, 2)
        dda = _dot01_left(triT_ref[...], dacs, 2)
        ddt = ddt + dda * c["a"]
        dalog_ref[...] += jnp.sum(dda * c["dtp"], axis=0, keepdims=True) * c["a"]
        ddtraw = jnp.where(lane < SSM_H, ddt * _sigmoid(c["pre"]), 0.0)
        dbias_ref[...] += jnp.sum(ddtraw, axis=0, keepdims=True)
        ddt_ref[...] = ddtraw.astype(BF16)
        dxa_ref[...] = jnp.concatenate([dX, dBC[0][0], dBC[1][0], dBC[0][1], dBC[1][1]], axis=1)

    p128, p1k = _full((1, 128)), _full((1, SSM_W))
    return pl.pallas_call(
        body, name=name, grid=(B, nc),
        in_specs=[xa_spec] + z_specs + [dt_spec, st_spec, do_spec, p128, p128, p1k, p1k,
                                        _full((128, SSM_W)), _full((SSM_W, 128)), _full((128, 128)), _full((128, 128))],
        out_specs=[xa_spec, tok, dt_out, p128, p128, p1k, p1k],
        out_shape=[jax.ShapeDtypeStruct((B, S, CONV_CH), F32), jax.ShapeDtypeStruct((B, S, SSM_W), BF16),
                   jax.ShapeDtypeStruct((B, S, 128), BF16), jax.ShapeDtypeStruct((1, 128), F32),
                   jax.ShapeDtypeStruct((1, 128), F32), jax.ShapeDtypeStruct((1, SSM_W), F32),
                   jax.ShapeDtypeStruct((1, SSM_W), F32)],
        scratch_shapes=[pltpu.VMEM((128, SSM_W), F32)],
        compiler_params=pltpu.CompilerParams(dimension_semantics=("arbitrary", "arbitrary")),
    )(xact, P, P, P, P, P, sprev, dcat, bias, alog, dskipE, ng, E, ET, tri, triT)


def _adamw(w, parts, m, v, name, tr=512, row0=0, prev=None):
    Rtot, C = w.shape
    ns, R = parts.shape[0], parts.shape[1]
    tr = min(tr, R)
    assert R % tr == 0 and row0 % tr == 0
    off = row0 // tr
    c1 = 1.0 / (1.0 - ADAM_B1 ** ADAM_STEP)
    c2 = 1.0 / (1.0 - ADAM_B2 ** ADAM_STEP)

    def body(w_ref, p_ref, m_ref, v_ref, *rest):
        g_ref, d_ref, mo_ref, vo_ref = rest[-4:]
        g = p_ref[0].astype(F32)
        for s in range(1, ns):
            g = g + p_ref[s].astype(F32)
        mn = ADAM_B1 * m_ref[...] + (1.0 - ADAM_B1) * g
        vn = ADAM_B2 * v_ref[...] + (1.0 - ADAM_B2) * (g * g)
        g_ref[...] = g
        mo_ref[...] = mn
        vo_ref[...] = vn
        d_ref[...] = -ADAM_LR * ((mn * c1) / (jnp.sqrt(vn * c2) + ADAM_EPS) + ADAM_WD * w_ref[...])

    blk = pl.BlockSpec((tr, C), lambda i: (i + off, 0))
    extra = [] if prev is None else list(prev)
    return pl.pallas_call(
        body, name=name, grid=(R // tr,),
        in_specs=[blk, pl.BlockSpec((ns, tr, C), lambda i: (0, i, 0)), blk, blk] + [pl.BlockSpec(memory_space=pl.ANY)] * len(extra),
        out_specs=[blk] * 4, out_shape=[jax.ShapeDtypeStruct((Rtot, C), F32)] * 4,
        input_output_aliases={4 + k: k for k in range(len(extra))})(w, parts, m, v, *extra)


_SMALL = ("ada_b", "norm1_g", "gm_ln_g", "gm_ln_b", "gm_ws", "gm_bs", "gm_norm_g", "attn_sinks", "attn_norm_g", "conv_b",
          "dt_bias", "a_log", "d_skip", "ssm_norm_g", "norm2_g", "final_norm_g")


def _pack(arrs):
    flat = []
    for a in arrs:
        f = a.reshape(-1).astype(F32)
        flat.append(jnp.pad(f, (0, (-f.shape[0]) % 1024)))
    return jnp.concatenate(flat).reshape(-1, 128)


def _unpack(pack, like):
    out, o = [], 0
    flat = pack.reshape(-1)
    for a in like:
        n = int(np.prod(a.shape))
        out.append(flat[o:o + n].reshape(a.shape))
        o += n + (-n) % 1024
    return out


def kernel(x, c, ada_w, ada_b, norm1_g, w_in, gm_ln_g, gm_ln_b, gm_ws, gm_bs, gm_norm_g, attn_sinks, attn_norm_g, conv_w, conv_b, dt_bias, a_log, d_skip, ssm_norm_g, w_out, norm2_g, w_mlp1, w_mlp2, final_norm_g, loss_target, m_ada_w, m_ada_b, m_norm1_g, m_w_in, m_gm_ln_g, m_gm_ln_b, m_gm_ws, m_gm_bs, m_gm_norm_g, m_attn_sinks, m_attn_norm_g, m_conv_w, m_conv_b, m_dt_bias, m_a_log, m_d_skip, m_ssm_norm_g, m_w_out, m_norm2_g, m_w_mlp1, m_w_mlp2, m_final_norm_g, v_ada_w, v_ada_b, v_norm1_g, v_w_in, v_gm_ln_g, v_gm_ln_b, v_gm_ws, v_gm_bs, v_gm_norm_g, v_attn_sinks, v_attn_norm_g, v_conv_w, v_conv_b, v_dt_bias, v_a_log, v_d_skip, v_ssm_norm_g, v_w_out, v_norm2_g, v_w_mlp1, v_w_mlp2, v_final_norm_g):
    args = dict(locals())
    B, S, _ = x.shape
    T = B * S
    L = DEPTH
    me = 4 * lax.axis_index("x") + 2 * lax.axis_index("y") + lax.axis_index("c")

    gath = _gather2([c, conv_w], "ag_c")
    big = ("w_in", "w_out", "w_mlp1", "w_mlp2")
    chain = [(n, l) for l in range(L) for n in ("w_in", "w_mlp1", "w_out", "w_mlp2")]
    inflight = {}

    def start_next(order):
        if not chain:
            return jnp.zeros((8, 128), F32)
        n, l = chain.pop(0)
        sems, land_thru, token = _gather_start(zone[n, l], order, f"ag_start_{n}{l}")
        inflight[n, l] = (sems, land_thru)
        return token

    def gathered(n, l, after):
        land = _gather_wait(*inflight.pop((n, l)), after, f"ag_wait_{n}{l}")
        return _gather_finish(land, f"ag_fin_{n}{l}")

    forwarding = {}

    def arrived(n, l, after):
        land = _gather_wait(*inflight.pop((n, l)), after, f"ag_wait_{n}{l}")
        sems, land_thru, token = _forward_start(land, after, f"ag_fwd_start_{n}{l}")
        forwarding[n, l] = (sems, land_thru)
        return token

    def ready(n, l, after):
        return _forward_wait(*forwarding.pop((n, l)), after, f"ag_fwd_wait_{n}{l}")

    me1 = me.astype(jnp.int32).reshape(1)
    zone = {(n, l): _landing_zone(args[n], l, me1, f"ag_zone_{n}{l}") for n, l in chain}
    later_zones = [zone[k] for k in chain[1:]]

    tok = start_next(gath[0])
    c_all = gath[0].reshape(NDEV * B, D) + tok[0, 0]
    c_act = (c_all * jax.nn.sigmoid(c_all)).astype(BF16)
    nb_rows = c_act.shape[0]
    c_pad = jnp.pad(c_act, ((0, 128 - nb_rows), (0, 0)))
    adw = ada_w.astype(BF16)
    mod_part = jnp.stack([_mm(c_pad, adw[l], mode="nn", name=f"mod{l}", tn=768)[:nb_rows] for l in range(L)])
    mod_all = _gather_small([mod_part], "ag_mod", order=later_zones)[0]
    mod_mine = lax.dynamic_slice_in_dim(mod_all, me * B, B, axis=2)
    mod = jnp.transpose(mod_mine, (1, 2, 0, 3)).reshape(L, B, 6 * D) + ada_b[:, None, :]
    mods = [[mod[l][:, None, i * D:(i + 1) * D] for i in range(6)] for l in range(L)]

    win_g, wout_g, w1_g, w2_g = [None] * L, [None] * L, [None] * L, [None] * L

    tril = jnp.tril(jnp.ones((128, 128), F32))
    row = lambda a: a.reshape(1, -1)
    pad128 = lambda a: jnp.pad(a.reshape(1, -1), ((0, 0), (0, 128 - a.shape[-1])))
    small = []
    for l in range(L):
        wt = gm_ws[l] * tril
        small.append(dict(
            lng=row(gm_ln_g[l]), lnb=row(gm_ln_b[l]), wt=wt.astype(BF16), wtT=jnp.swapaxes(wt, 1, 2).astype(BF16),
            bsx=jnp.repeat(gm_bs[l].T, 128, axis=1), gog=row(gm_norm_g[l]), sinks=attn_sinks[l], aog=row(attn_norm_g[l]),
            bias=pad128(dt_bias[l]), alog=pad128(a_log[l]), dskE=jnp.repeat(d_skip[l], SSM_HD).reshape(1, SSM_W),
            sng=row(ssm_norm_g[l]), cb=row(conv_b[l])))
    convw_all = jnp.transpose(gath[1], (1, 2, 0, 3)).reshape(L, 4, CONV_CH)
    convw8 = jnp.pad(convw_all, ((0, 0), (0, 4), (0, 0)))

    saved = []
    xl = x
    h = _norm_fwd(xl, row(norm1_g[0]), mods[0][1], mods[0][0], "norm1_f0")
    for l in range(L):
        sm = small[l]
        if l == 0:
            g_in = gathered("w_in", 0, h)
            tok = start_next(g_in)
        win_g[l] = _shards_to_cols(g_in, f"w_in_cols{l}")
        P = _mm(h.reshape(T, D), win_g[l], mode="nn", name=f"proj_in{l}", tn=1536, order=tok).reshape(B, S, PW)
        cat = _gmlp_fwd(P, sm["lng"], sm["lnb"], sm["wt"], sm["bsx"], sm["gog"], f"gmlp_f{l}")
        cat = _attn_fwd(P, sm["sinks"], sm["aog"], cat, f"attn_f{l}")
        xact = _conv_fwd(P, convw8[l], sm["cb"], f"conv_f{l}")
        tok = start_next(arrived("w_mlp1", l, xact))
        cat, sprev = _ssd_fwd(xact, P, sm["bias"], sm["alog"], sm["dskE"], sm["sng"] + tok[0:1, 0:1], cat, f"ssd_f{l}")
        g_out = gathered("w_out", l, cat)
        tok = start_next(g_out)
        wout_g[l] = g_out.reshape(D, D)
        mix = _mm(cat.reshape(T, D), wout_g[l], mode="nn", name=f"proj_out{l}", order=tok).reshape(B, S, D)
        x_mid, h2 = _norm_fwd(xl, row(norm2_g[l]), mods[l][4], mods[l][3], f"norm2_f{l}", resid=(mix, mods[l][2]))
        w1_g[l] = ready("w_mlp1", l, h2)
        a_act, r_act = _mm(h2.reshape(T, D), w1_g[l], mode="nn", name=f"mlp1_{l}", out_dtypes=(BF16, BF16), col_blocked_b=True,
                           epilogue=lambda acc: (acc, jnp.square(jnp.maximum(acc, 0.0))))
        g_2 = gathered("w_mlp2", l, r_act)
        tok = start_next(g_2)
        w2_g[l] = g_2.reshape(DFF, D)
        m2 = _mm(r_act, w2_g[l], mode="nn", name=f"mlp2_{l}", order=tok, tk=4096).reshape(B, S, D)
        saved.append(dict(x_in=xl, h=h, P=P, xact=xact, sprev=sprev, cat=cat, mix=mix, x_mid=x_mid, h2=h2, a=a_act, r=r_act, m2=m2))
        if l + 1 < L:
            tok = start_next(arrived("w_in", l + 1, m2))
            xl, h = _norm_fwd(x_mid, row(norm1_g[l + 1]) + tok[0, 0], mods[l + 1][1], mods[l + 1][0], f"norm1_f{l + 1}",
                              resid=(m2, mods[l][5]))
            g_in = ready("w_in", l + 1, h)

    sv = saved[L - 1]
    nb = _norm_bwd(sv["x_mid"], row(final_norm_g), "final_b", tgt=loss_target, br=sv["m2"], gate=mods[L - 1][5], x_is_prev=True)
    loss_part, g_final = nb["loss"], nb["dg"]
    dmod, gsm, gconvw = [None] * L, [None] * L, [None] * L
    core = lax.axis_index("c").astype(jnp.int32).reshape(1)
    reducing = []

    def reduce_start(n, l, sent, after):
        p, from_sib = _pair_wait(*sent[:3], after, f"rs_pair_wait_{n}{l}")
        s, land = _pair_add(p, from_sib, core, f"rs_add_{n}{l}")
        return reduce_exchange(n, l, s, land, after)

    def reduce_exchange(n, l, s, land, order):
        sems, s_thru, land_thru, token = _chipsum_start(s, land, order, f"rs_start_{n}{l}")
        reducing.append((n, l, sems, s_thru, land_thru))
        return token

    other = 1 - core

    for l in reversed(range(L)):
        sv, sm = saved[l], small[l]
        dm2, dxo, dg2 = nb["dbr"].reshape(T, D), nb["dx"], nb["dgate"]
        da = _mm(dm2, w2_g[l], mode="nt", name=f"mlp2_dx{l}", out_dtypes=(BF16,), extras=(sv["a"],),
                 epilogue=lambda acc, a: (acc * (2.0 * jnp.maximum(a.astype(F32), 0.0)),))
        h2f = sv["h2"].reshape(T, D)
        sent2 = _sibling_start(_dw_half(sv["r"], dm2, other, axis="m", name=f"mlp2_dw_sib{l}"), da, f"rs_sib_start_w_mlp2{l}")
        dh2 = _mm(da, w1_g[l], mode="nt", name=f"mlp1_dx{l}", col_blocked_b=True, order=sent2[3],
                  out_dtypes=(BF16,)).reshape(B, S, D)
        from_sib = _sibling_wait(*sent2[:3], dh2, f"rs_sib_wait_w_mlp2{l}")[1]
        sent1 = _sibling_start(_dw_half(h2f, da, other, axis="n", name=f"mlp1_dw_sib{l}", order=from_sib), da,
                               f"rs_sib_start_w_mlp1{l}")
        s2, land2 = _dw_half(sv["r"], dm2, core, axis="m", name=f"mlp2_dw_own{l}", add=from_sib, order=sent1[3])
        tok = reduce_exchange("w_mlp2", l, s2, land2, da)
        nb2 = _norm_bwd(sv["x_mid"], row(norm2_g[l]) + tok[0, 0], f"norm2_b{l}", sc=mods[l][4], dh=dh2, dres=dxo, br=sv["mix"],
                        gate=mods[l][2])
        dmix = nb2["dbr"].reshape(T, D)
        from_sib = _sibling_wait(*sent1[:3], dmix, f"rs_sib_wait_w_mlp1{l}")[1]
        s1, land1 = _dw_half(h2f, da, core, axis="n", name=f"mlp1_dw_own{l}", add=from_sib)
        tok = reduce_exchange("w_mlp1", l, s1, land1, dmix)
        dcat = _mm(dmix, wout_g[l], mode="nt", name=f"proj_out_dx{l}", order=tok, out_dtypes=(BF16,)).reshape(B, S, D)
        du, dv, dlng, dlnb, dws, dbsx, dgog = _gmlp_bwd(sv["P"], dcat, sm["lng"], sm["lnb"], sm["wt"], sm["wtT"], sm["bsx"],
                                                        sm["gog"], f"gmlp_b{l}")
        dq, dk, dvv, dsink, daog = _attn_bwd(sv["P"], dcat, sm["sinks"], sm["aog"], f"attn_b{l}")
        dwo = _mm(sv["cat"].reshape(T, D), dmix, mode="tn", name=f"proj_out_dw{l}", out_dtypes=(BF16,), tk=2048,
                  order=dq).reshape(4, 2, D // NDEV, D)
        sent = _pair_start(dwo, dmix, f"rs_pair_start_w_out{l}")
        dxa, dz, ddt, dbias, dalog, ddsk, dsng = _ssd_bwd(sv["xact"], sv["P"], sv["sprev"], dcat, sm["bias"], sm["alog"],
                                                          sm["dskE"], sm["sng"] + sent[3][0:1, 0:1], f"ssd_b{l}")
        tok = reduce_start("w_out", l, sent, dxa)
        dxbc, dcw, dcb = _conv_bwd(sv["P"], dxa, convw8[l], sm["cb"] + tok[0:1, 0:1], f"conv_b{l}")
        dP = jnp.concatenate([du, dv, dq, dk, dvv, dz, dxbc, ddt, jnp.zeros((B, S, PW - OFF["dt"] - 128), BF16)],
                             axis=-1).reshape(T, PW)
        dwin = _mm(sv["h"].reshape(T, D), dP, mode="tn", name=f"proj_in_dw{l}", out_dtypes=(BF16,), tn=1536, tk=2048)
        sent = _sibling_start(dwin, dP, f"rs_sib_start_w_in{l}")
        dh = _mm(dP, win_g[l], mode="nt", name=f"proj_in_dx{l}", tk=2304, order=sent[3], out_dtypes=(BF16,)).reshape(B, S, D)
        s_in, land_in = _cols_to_my_shards(*_sibling_wait(*sent[:3], dh, f"rs_sib_wait_w_in{l}"), core, f"w_in_dshards{l}")
        tok = reduce_exchange("w_in", l, s_in, land_in, dh)
        nb = _norm_bwd(sv["x_in"], row(norm1_g[l]) + tok[0, 0], f"norm1_b{l}", sc=mods[l][1], dh=dh, dres=nb2["dx"],
                       br=saved[l - 1]["m2"] if l > 0 else None, gate=mods[l - 1][5] if l > 0 else None)
        dmod[l] = jnp.concatenate([nb["dsh"], nb["dsc"], nb2["dgate"], nb2["dsh"], nb2["dsc"], dg2], axis=-1)
        gconvw[l] = dcw[:4]
        gsm[l] = dict(
            ada_b=jnp.sum(dmod[l], axis=(0, 1)), norm1_g=nb["dg"], gm_ln_g=dlng, gm_ln_b=dlnb, gm_ws=dws,
            gm_bs=dbsx.reshape(128, GM_H, 128).sum(-1).T, gm_norm_g=dgog, attn_sinks=dsink[:, 0], attn_norm_g=daog,
            conv_b=dcb, dt_bias=dbias[0, :SSM_H], a_log=dalog[0, :SSM_H], d_skip=ddsk.reshape(SSM_H, SSM_HD).sum(-1),
            ssm_norm_g=dsng, norm2_g=nb2["dg"])
    grad_x = nb["dx"]

    big_res, after = dict.fromkeys(big), grad_x
    tile_rows = dict(w_in=256, w_out=256, w_mlp1=256, w_mlp2=128)

    def finish_reduce(n, l, sems, s_thru, land_thru, after):
        parts = _chipsum_wait(sems, s_thru, land_thru, after, f"rs_wait_{n}{l}")
        w = args[n]
        big_res[n] = _adamw(w.reshape(-1, w.shape[-1]), parts, args["m_" + n].reshape(-1, w.shape[-1]),
                            args["v_" + n].reshape(-1, w.shape[-1]), f"adamw_{n}{l}", tr=tile_rows[n], row0=l * w.shape[1],
                            prev=big_res[n])
        return big_res[n][0]

    for item in reducing[:-1]:
        after = finish_reduce(*item, after)

    per_layer = [n for n in _SMALL if n != "final_norm_g"]
    g_small = [jnp.stack([gsm[l][n].reshape(args[n].shape[1:]) for l in range(L)]) for n in per_layer] + [g_final.reshape(D)]
    zc = jnp.zeros((L, 4, CONV_CH), F32)
    z1 = jnp.zeros((1, 128), F32)
    gpack = _pack([loss_part] + g_small + [jnp.stack(gconvw)])
    got = _gather2([jnp.stack(dmod).reshape(L, B, 6 * D), gpack], "ag_small", order=after)
    like = [z1] + [args[n] for n in _SMALL] + [zc]
    packs = [_pack([z1] + [args[p + n] for n in _SMALL] + [zc]) for p in ("", "m_", "v_")]
    sres = [_unpack(p, like) for p in _adamw(packs[0], got[1], packs[1], packs[2], "adamw_small", tr=gpack.shape[0])]
    res = {n: [r[1 + i] for r in sres] for i, n in enumerate(_SMALL)}
    loss = sres[0][0][0, 0]
    gcw = lax.dynamic_slice_in_dim(sres[0][-1], me * (CONV_CH // NDEV), CONV_CH // NDEV, axis=2)

    def update(name, parts, tr):
        w = args[name]
        r = _adamw(w.reshape(-1, w.shape[-1]), parts, args["m_" + name].reshape(-1, w.shape[-1]),
                   args["v_" + name].reshape(-1, w.shape[-1]), "adamw_" + name, tr=tr)
        res[name] = [a.reshape(w.shape) for a in r]

    update("conv_w", gcw.reshape(1, L * 4, CONV_CH // NDEV), L * 4)

    dmod_all = jnp.transpose(got[0], (1, 0, 2, 3)).reshape(L, NDEV * B, 6 * D)
    dm_mine = lax.dynamic_slice_in_dim(dmod_all, me * (6 * D // NDEV), 6 * D // NDEV, axis=2)
    dm_pad = jnp.pad(dm_mine, ((0, 0), (0, 128 - nb_rows), (0, 0))).astype(BF16)
    g_adaw = jnp.stack([_mm(c_pad, dm_pad[l], mode="tn", name=f"ada_dw{l}", tn=768) for l in range(L)])
    update("ada_w", g_adaw.reshape(1, L * D, 6 * D // NDEV), 256)

    finish_reduce(*reducing[-1], res["ada_w"][0])
    for n in big:
        res[n] = [a.reshape(args[n].shape) for a in big_res[n]]

    names = ['ada_w', 'ada_b', 'norm1_g', 'w_in', 'gm_ln_g', 'gm_ln_b', 'gm_ws', 'gm_bs', 'gm_norm_g', 'attn_sinks',
             'attn_norm_g', 'conv_w', 'conv_b', 'dt_bias', 'a_log', 'd_skip', 'ssm_norm_g', 'w_out', 'norm2_g', 'w_mlp1',
             'w_mlp2', 'final_norm_g']
    return (loss, grad_x, *[res[n][0] for n in names], *[res[n][1] for n in names], *[res[n][2] for n in names],
            *[res[n][3] for n in names])
```

```python
import functools

import jax
import jax.numpy as jnp
import numpy as np
from jax import lax
from jax.experimental import pallas as pl
from jax.experimental.pallas import tpu as pltpu

F32, BF16 = jnp.float32, jnp.bfloat16
HI = lax.Precision.HIGHEST
MESH = pl.DeviceIdType.MESH
NDEV = 8

D = 2048
DEPTH = 2
CHUNK = 128
GM_W, GM_H = 512, 4
ATT_W, KV_W, ATT_H = 512, 128, 8
SSM_W, SSM_H, SSM_HD, SSM_G = 1024, 16, 64, 2
CONV_CH = 1536
IN_W = 4368
DFF = 8192
EPS = 1e-6
NEG_INF = -1e30
GELU_K = 0.7978845608028654
GELU_C = 0.044715

_ORIG = (("u", 512), ("v", 512), ("q", 512), ("k", 128), ("vv", 128), ("z", 1024), ("xbc", 1536), ("dt", 16))
OFF = dict(u=0, v=512, q=1024, k=1536, vv=1664, z=1792, xbc=2816, dt=4352)
PW = 4608

ADAM_LR, ADAM_B1, ADAM_B2, ADAM_EPS, ADAM_WD, ADAM_STEP = 0.001, 0.9, 0.999, 1e-08, 0.01, 10


def _shards_to_cols(g, name, tr=256):
    n, R, C = g.shape

    def body(g_ref, o_ref):
        o_ref[...] = jnp.concatenate([g_ref[s] for s in range(n)] + [jnp.zeros((tr, PW - n * C), g.dtype)], axis=1)

    return pl.pallas_call(body, name=name, grid=(R // tr,), in_specs=[pl.BlockSpec((n, tr, C), lambda i: (0, i, 0))],
                          out_specs=pl.BlockSpec((tr, PW), lambda i: (i, 0)), out_shape=jax.ShapeDtypeStruct((R, PW), g.dtype))(g)


def _cols_to_my_shards(w, w_sib, core, name, tr=256):
    R, C = w.shape[0], IN_W // NDEV

    def body(core_ref, w_ref, s_ref, o_ref, o2_ref):
        x = w_ref[...].astype(F32) + s_ref[...].astype(F32)
        mine_is_odd = core_ref[0] == 1
        for q in range(4):
            blk = jnp.where(mine_is_odd, x[:, C * (2 * q + 1):C * (2 * q + 2)], x[:, C * 2 * q:C * (2 * q + 1)]).astype(o_ref.dtype)
            o_ref[q] = blk
            o2_ref[q] = blk

    row = pl.BlockSpec((tr, PW), lambda i, c: (i, 0))
    out = pl.BlockSpec((4, tr, C), lambda i, c: (0, i, 0))
    return pl.pallas_call(
        body, name=name, out_shape=[jax.ShapeDtypeStruct((4, R, C), w.dtype)] * 2,
        grid_spec=pltpu.PrefetchScalarGridSpec(num_scalar_prefetch=1, grid=(R // tr,), in_specs=[row, row], out_specs=[out, out]),
    )(core, w, w_sib)


def _sigmoid(x):
    return 1.0 / (1.0 + jnp.exp(-x))


def _gelu(x):
    return 0.5 * x * (1.0 + jnp.tanh(GELU_K * (x + GELU_C * x * x * x)))


def _gelu_grad(x):
    t = jnp.tanh(GELU_K * (x + GELU_C * x * x * x))
    return 0.5 * (1.0 + t) + 0.5 * x * (1.0 - t * t) * GELU_K * (1.0 + 3.0 * GELU_C * x * x)


def _dot(a, b, prec=None):
    return jnp.dot(a, b, precision=prec, preferred_element_type=F32)


def _dot_nt(a, b, prec=None):
    return lax.dot_general(a, b, (((1,), (1,)), ((), ())), precision=prec, preferred_element_type=F32)


def _dot_tn(a, b, prec=None):
    return lax.dot_general(a, b, (((0,), (0,)), ((), ())), precision=prec, preferred_element_type=F32)


def _full(shape):
    return pl.BlockSpec(shape, lambda *_: (0,) * len(shape))


_HBM = pl.BlockSpec(memory_space=pltpu.HBM)


def _me():
    return lax.axis_index("x"), lax.axis_index("y"), lax.axis_index("c")


def _peer(k):
    x, y, c = _me()
    px = 1 - x if k & 4 else x
    py = 1 - y if k & 2 else y
    pc = 1 - c if k & 1 else c
    return (px, py, pc), 4 * px + 2 * py + pc


def _gather_small(xs, name, order=()):
    n = len(xs)

    def body(*refs):
        ins, outs = refs[:n], refs[-n - 3:-3]
        send, recv, loc = refs[-3:]
        x, y, c = _me()
        me = 4 * x + 2 * y + c
        started = []
        for i in range(n):
            own = pltpu.make_async_copy(ins[i], outs[i].at[me], loc.at[i])
            own.start()
            started.append(own)
        for k in range(1, NDEV):
            dev, lin = _peer(k)
            for i in range(n):
                pltpu.make_async_remote_copy(
                    src_ref=ins[i], dst_ref=outs[i].at[me],
                    send_sem=send.at[i, k - 1], recv_sem=recv.at[i, k - 1], device_id=dev, device_id_type=MESH).start()
        for k in range(1, NDEV):
            dev, lin = _peer(k)
            for i in range(n):
                pltpu.make_async_remote_copy(
                    src_ref=ins[i], dst_ref=outs[i].at[lin],
                    send_sem=send.at[i, k - 1], recv_sem=recv.at[i, k - 1], device_id=dev, device_id_type=MESH).wait()
        for own in started:
            own.wait()

    extra = list(order)
    return pl.pallas_call(
        body, name=name, out_shape=[jax.ShapeDtypeStruct((NDEV,) + a.shape, a.dtype) for a in xs],
        in_specs=[_HBM] * n + [pl.BlockSpec(memory_space=pl.ANY)] * len(extra), out_specs=[_HBM] * n,
        scratch_shapes=[pltpu.SemaphoreType.DMA((n, NDEV - 1)), pltpu.SemaphoreType.DMA((n, NDEV - 1)),
                        pltpu.SemaphoreType.DMA((n,))],
        compiler_params=pltpu.CompilerParams(has_side_effects=True),
    )(*xs, *extra)


def _gather_small_start(lands, order, name):
    n = len(lands)

    def body(*refs):
        ins, sems, token = refs[:n], refs[n + 1:n + 1 + 14 * n], refs[-1]
        x, y, c = _me()
        me = 4 * x + 2 * y + c
        for i in range(n):
            for k in range(1, NDEV):
                dev, _ = _peer(k)
                pltpu.make_async_remote_copy(src_ref=ins[i].at[me], dst_ref=ins[i].at[me], send_sem=sems[14 * i + k - 1],
                                             recv_sem=sems[14 * i + 7 + k - 1], device_id=dev, device_id_type=MESH).start()
        token[...] = jnp.zeros_like(token)

    outs = pl.pallas_call(
        body, name=name,
        out_shape=(pltpu.SemaphoreType.DMA(()),) * (14 * n) + tuple(pltpu.HBM(a.shape, a.dtype) for a in lands)
        + (jax.ShapeDtypeStruct((8, 128), F32),),
        in_specs=(_HBM,) * n + (_ANY,), out_specs=(_SEM,) * (14 * n) + (_HBM,) * n + (pl.BlockSpec(memory_space=pltpu.VMEM),),
        input_output_aliases={i: 14 * n + i for i in range(n)}, compiler_params=pltpu.CompilerParams(has_side_effects=_DATAFLOW),
    )(*[_hbm(a) for a in lands], order)
    return outs[:14 * n], outs[14 * n:15 * n], outs[-1]


def _gather_small_wait(sems, lands_thru, after, name):
    n = len(lands_thru)

    def body(*refs):
        ins, sems_ = refs[:n], refs[n:n + 14 * n]
        x, y, c = _me()
        me = 4 * x + 2 * y + c
        for i in range(n):
            for k in range(1, NDEV):
                dev, lin = _peer(k)
                cp = pltpu.make_async_remote_copy(src_ref=ins[i].at[me], dst_ref=ins[i].at[lin], send_sem=sems_[14 * i + k - 1],
                                                  recv_sem=sems_[14 * i + 7 + k - 1], device_id=dev, device_id_type=MESH)
                cp.wait_send()
                cp.wait_recv()

    outs = pl.pallas_call(
        body, name=name, out_shape=tuple(pltpu.HBM(a.shape, a.dtype) for a in lands_thru),
        in_specs=(_HBM,) * n + (_SEM,) * (14 * n) + (_ANY,), out_specs=(_HBM,) * n,
        input_output_aliases={i: i for i in range(n)}, compiler_params=pltpu.CompilerParams(has_side_effects=_DATAFLOW),
    )(*lands_thru, *sems, after)
    return outs


def _chips():
    x, y, c = _me()
    return x, y, c, [(1 - x, y), (x, 1 - y), (1 - x, 1 - y)]


def _gather2(xs, name, order=None):
    n = len(xs)
    extra = [] if order is None else [order]

    def body(*refs):
        ins, outs = refs[:n], refs[-n - 3:-3]
        send, recv, loc = refs[-3:]
        x, y, c, chips = _chips()
        me, sib = (x, y, c), (x, y, 1 - c)

        def cp(i, k, block, to, src=None):
            slot = outs[i].at[4 * block[0] + 2 * block[1] + block[2]]
            return pltpu.make_async_remote_copy(src_ref=slot if src is None else src, dst_ref=slot, send_sem=send.at[i, k],
                                                recv_sem=recv.at[i, k], device_id=to, device_id_type=MESH)

        sent = []
        for i in range(n):
            for j, chip in enumerate(chips):
                sent.append(cp(i, 1 + j, me, (*chip, c), src=ins[i]))
            sent.append(cp(i, 0, me, sib, src=ins[i]))
        for s in sent:
            s.start()
        own = [pltpu.make_async_copy(ins[i], outs[i].at[4 * x + 2 * y + c], loc.at[i]) for i in range(n)]
        for o in own:
            o.start()
        for j, chip in enumerate(chips):
            for i in range(n):
                cp(i, 1 + j, (*chip, c), me).wait_recv()
                fwd = cp(i, 4 + j, (*chip, c), sib)
                fwd.start()
                sent.append(fwd)
        for i in range(n):
            cp(i, 0, sib, me).wait_recv()
            for j, chip in enumerate(chips):
                cp(i, 4 + j, (*chip, 1 - c), me).wait_recv()
        for s in sent:
            s.wait_send()
        for o in own:
            o.wait()

    return pl.pallas_call(
        body, name=name, out_shape=[jax.ShapeDtypeStruct((NDEV,) + a.shape, a.dtype) for a in xs],
        in_specs=[_HBM] * n + [pl.BlockSpec(memory_space=pl.ANY)] * len(extra), out_specs=[_HBM] * n,
        scratch_shapes=[pltpu.SemaphoreType.DMA((n, 7)), pltpu.SemaphoreType.DMA((n, 7)), pltpu.SemaphoreType.DMA((n,))],
        compiler_params=pltpu.CompilerParams(has_side_effects=True),
    )(*xs, *extra)


def _pair_add(p, r1, core, name, tr=256):
    _, _, R, C = p.shape
    tr = min(tr, R)

    def body(core_ref, p_ref, r_ref, o_ref, o2_ref):
        s = (p_ref[...].astype(F32) + r_ref[...].astype(F32)).astype(o_ref.dtype)
        o_ref[...] = s
        o2_ref[...] = s

    blk = pl.BlockSpec((None, tr, C), lambda ch, i, core_ref: (ch, i, 0))
    return pl.pallas_call(
        body, name=name, out_shape=[jax.ShapeDtypeStruct((4, R, C), p.dtype)] * 2,
        grid_spec=pltpu.PrefetchScalarGridSpec(
            num_scalar_prefetch=1, grid=(4, R // tr),
            in_specs=[pl.BlockSpec((None, None, tr, C), lambda ch, i, core_ref: (ch, core_ref[0], i, 0)), blk],
            out_specs=[blk, blk]),
    )(core, p, r1)


_SEM = pl.BlockSpec(memory_space=pltpu.SEMAPHORE)
_ANY = pl.BlockSpec(memory_space=pl.ANY)
_DATAFLOW = pltpu.SideEffectType.DATAFLOW_SIDE_EFFECTING


def _hbm(a):
    return pltpu.with_memory_space_constraint(a, pltpu.HBM)


def _gather_targets():
    x, y, c, chips = _chips()
    return 4 * x + 2 * y + c, [(x, y, 1 - c)] + [(*chip, c) for chip in chips]


def _landing_zone(w, l, me, name, tr=512, dtype=BF16):
    _, R, C = w.shape
    tr = min(tr, R)

    def body(me_ref, w_ref, o_ref):
        o_ref[...] = w_ref[...].astype(dtype)

    return pl.pallas_call(
        body, name=name, out_shape=jax.ShapeDtypeStruct((NDEV, R, C), dtype),
        grid_spec=pltpu.PrefetchScalarGridSpec(
            num_scalar_prefetch=1, grid=(R // tr,), in_specs=[pl.BlockSpec((None, tr, C), lambda i, me_ref: (l, i, 0))],
            out_specs=pl.BlockSpec((None, tr, C), lambda i, me_ref: (me_ref[0], i, 0))),
    )(me, w)


def _gather_start(land, order, name):
    def body(land_ref, order_ref, *rest):
        sems, token = rest[:8], rest[9]
        me, targets = _gather_targets()
        for k, to in enumerate(targets):
            pltpu.make_async_remote_copy(src_ref=land_ref.at[me], dst_ref=land_ref.at[me], send_sem=sems[k],
                                         recv_sem=sems[4 + k], device_id=to, device_id_type=MESH).start()
        token[...] = jnp.zeros_like(token)

    outs = pl.pallas_call(
        body, name=name,
        out_shape=(pltpu.SemaphoreType.DMA(()),) * 8 + (pltpu.HBM(land.shape, land.dtype), jax.ShapeDtypeStruct((8, 128), F32)),
        in_specs=(_HBM, _ANY), out_specs=(_SEM,) * 8 + (_HBM, pl.BlockSpec(memory_space=pltpu.VMEM)),
        input_output_aliases={0: 8}, compiler_params=pltpu.CompilerParams(has_side_effects=_DATAFLOW),
    )(_hbm(land), order)
    return outs[:8], outs[8], outs[9]


def _gather_wait(sems, land_thru, after, name):
    def body(land_ref, *rest):
        sems_ = rest[:8]
        me, targets = _gather_targets()
        for k, to in enumerate(targets):
            cp = pltpu.make_async_remote_copy(src_ref=land_ref.at[me], dst_ref=land_ref.at[me], send_sem=sems_[k],
                                              recv_sem=sems_[4 + k], device_id=to, device_id_type=MESH)
            cp.wait_send()
            cp.wait_recv()

    return pl.pallas_call(
        body, name=name, out_shape=pltpu.HBM(land_thru.shape, land_thru.dtype),
        in_specs=(_HBM,) + (_SEM,) * 8 + (_ANY,), out_specs=_HBM, input_output_aliases={0: 0},
        compiler_params=pltpu.CompilerParams(has_side_effects=_DATAFLOW),
    )(land_thru, *sems, after)


def _gather_finish(land, name):
    def body(land_ref, out, send, recv):
        x, y, c, chips = _chips()
        fwd = [pltpu.make_async_remote_copy(src_ref=out.at[4 * px + 2 * py + c], dst_ref=out.at[4 * px + 2 * py + c],
                                            send_sem=send.at[j], recv_sem=recv.at[j], device_id=(x, y, 1 - c), device_id_type=MESH)
               for j, (px, py) in enumerate(chips)]
        for cp in fwd:
            cp.start()
        for j, (px, py) in enumerate(chips):
            slot = out.at[4 * px + 2 * py + 1 - c]
            pltpu.make_async_remote_copy(src_ref=slot, dst_ref=slot, send_sem=send.at[j], recv_sem=recv.at[j],
                                         device_id=(x, y, 1 - c), device_id_type=MESH).wait()

    return pl.pallas_call(
        body, name=name, out_shape=jax.ShapeDtypeStruct(land.shape, land.dtype),
        in_specs=[_HBM], out_specs=_HBM, input_output_aliases={0: 0},
        scratch_shapes=[pltpu.SemaphoreType.DMA((3,)), pltpu.SemaphoreType.DMA((3,))],
        compiler_params=pltpu.CompilerParams(has_side_effects=True),
    )(land)


def _forward_start(land, order, name):
    def body(land_ref, order_ref, *rest):
        sems, token = rest[:6], rest[7]
        x, y, c, chips = _chips()
        for j, (px, py) in enumerate(chips):
            slot = land_ref.at[4 * px + 2 * py + c]
            pltpu.make_async_remote_copy(src_ref=slot, dst_ref=slot, send_sem=sems[j], recv_sem=sems[3 + j],
                                         device_id=(x, y, 1 - c), device_id_type=MESH).start()
        token[...] = jnp.zeros_like(token)

    outs = pl.pallas_call(
        body, name=name,
        out_shape=(pltpu.SemaphoreType.DMA(()),) * 6 + (pltpu.HBM(land.shape, land.dtype), jax.ShapeDtypeStruct((8, 128), F32)),
        in_specs=(_HBM, _ANY), out_specs=(_SEM,) * 6 + (_HBM, pl.BlockSpec(memory_space=pltpu.VMEM)),
        input_output_aliases={0: 6}, compiler_params=pltpu.CompilerParams(has_side_effects=_DATAFLOW),
    )(_hbm(land), order)
    return outs[:6], outs[6], outs[7]


def _forward_wait(sems, land_thru, after, name):
    def body(land_ref, *rest):
        sems_ = rest[:6]
        x, y, c, chips = _chips()
        for j, (px, py) in enumerate(chips):
            cp = pltpu.make_async_remote_copy(src_ref=land_ref.at[4 * px + 2 * py + c], dst_ref=land_ref.at[4 * px + 2 * py + 1 - c],
                                              send_sem=sems_[j], recv_sem=sems_[3 + j], device_id=(x, y, 1 - c),
                                              device_id_type=MESH)
            cp.wait_send()
            cp.wait_recv()

    return pl.pallas_call(
        body, name=name, out_shape=pltpu.HBM(land_thru.shape, land_thru.dtype),
        in_specs=(_HBM,) + (_SEM,) * 6 + (_ANY,), out_specs=_HBM, input_output_aliases={0: 0},
        compiler_params=pltpu.CompilerParams(has_side_effects=_DATAFLOW),
    )(land_thru, *sems, after)


def _chip_targets():
    x, y, c, chips = _chips()
    return 2 * x + y, [((px, py, c), 2 * px + py) for px, py in chips]


def _chipsum_start(s, land, order, name):
    def body(s_ref, land_ref, order_ref, *rest):
        sems, token = rest[:6], rest[8]
        mine, targets = _chip_targets()
        for k, (to, ch) in enumerate(targets):
            pltpu.make_async_remote_copy(src_ref=s_ref.at[ch], dst_ref=land_ref.at[mine], send_sem=sems[k], recv_sem=sems[3 + k],
                                         device_id=to, device_id_type=MESH).start()
        token[...] = jnp.zeros_like(token)

    outs = pl.pallas_call(
        body, name=name,
        out_shape=(pltpu.SemaphoreType.DMA(()),) * 6 + (pltpu.HBM(s.shape, s.dtype), pltpu.HBM(land.shape, land.dtype),
                                                        jax.ShapeDtypeStruct((8, 128), F32)),
        in_specs=(_HBM, _HBM, _ANY), out_specs=(_SEM,) * 6 + (_HBM, _HBM, pl.BlockSpec(memory_space=pltpu.VMEM)),
        input_output_aliases={0: 6, 1: 7}, compiler_params=pltpu.CompilerParams(has_side_effects=_DATAFLOW),
    )(_hbm(s), _hbm(land), order)
    return outs[:6], outs[6], outs[7], outs[8]


def _chipsum_wait(sems, s_thru, land_thru, after, name):
    def body(s_ref, land_ref, *rest):
        sems_ = rest[:6]
        mine, targets = _chip_targets()
        for k, (to, ch) in enumerate(targets):
            cp = pltpu.make_async_remote_copy(src_ref=s_ref.at[ch], dst_ref=land_ref.at[ch], send_sem=sems_[k], recv_sem=sems_[3 + k],
                                              device_id=to, device_id_type=MESH)
            cp.wait_send()
            cp.wait_recv()

    return pl.pallas_call(
        body, name=name, out_shape=(pltpu.HBM(s_thru.shape, s_thru.dtype), pltpu.HBM(land_thru.shape, land_thru.dtype)),
        in_specs=(_HBM, _HBM) + (_SEM,) * 6 + (_ANY,), out_specs=(_HBM, _HBM), input_output_aliases={0: 0, 1: 1},
        compiler_params=pltpu.CompilerParams(has_side_effects=_DATAFLOW),
    )(s_thru, land_thru, *sems, after)[1]


def _pair_start(p, order, name):
    def body(p_ref, land_ref, order_ref, *rest):
        sems, token = rest[:8], rest[10]
        x, y, c = _me()
        for ch in range(4):
            pltpu.make_async_remote_copy(src_ref=p_ref.at[ch, 1 - c], dst_ref=land_ref.at[ch], send_sem=sems[ch],
                                         recv_sem=sems[4 + ch], device_id=(x, y, 1 - c), device_id_type=MESH).start()
        token[...] = jnp.zeros_like(token)

    land = lax.empty((4,) + p.shape[2:], p.dtype)
    outs = pl.pallas_call(
        body, name=name,
        out_shape=(pltpu.SemaphoreType.DMA(()),) * 8 + (pltpu.HBM(p.shape, p.dtype), pltpu.HBM(land.shape, land.dtype),
                                                        jax.ShapeDtypeStruct((8, 128), F32)),
        in_specs=(_HBM, _HBM, _ANY), out_specs=(_SEM,) * 8 + (_HBM, _HBM, pl.BlockSpec(memory_space=pltpu.VMEM)),
        input_output_aliases={0: 8, 1: 9}, compiler_params=pltpu.CompilerParams(has_side_effects=_DATAFLOW),
    )(_hbm(p), _hbm(land), order)
    return outs[:8], outs[8], outs[9], outs[10]


def _pair_wait(sems, p_thru, land_thru, after, name):
    def body(p_ref, land_ref, *rest):
        sems_ = rest[:8]
        x, y, c = _me()
        for ch in range(4):
            cp = pltpu.make_async_remote_copy(src_ref=p_ref.at[ch, 1 - c], dst_ref=land_ref.at[ch], send_sem=sems_[ch],
                                              recv_sem=sems_[4 + ch], device_id=(x, y, 1 - c), device_id_type=MESH)
            cp.wait_send()
            cp.wait_recv()

    return pl.pallas_call(
        body, name=name, out_shape=(pltpu.HBM(p_thru.shape, p_thru.dtype), pltpu.HBM(land_thru.shape, land_thru.dtype)),
        in_specs=(_HBM, _HBM) + (_SEM,) * 8 + (_ANY,), out_specs=(_HBM, _HBM), input_output_aliases={0: 0, 1: 1},
        compiler_params=pltpu.CompilerParams(has_side_effects=_DATAFLOW),
    )(p_thru, land_thru, *sems, after)


def _sibling_start(p, order, name):
    def body(p_ref, land_ref, order_ref, send_sem, recv_sem, p_thru, land_thru, token):
        x, y, c = _me()
        pltpu.make_async_remote_copy(src_ref=p_ref, dst_ref=land_ref, send_sem=send_sem, recv_sem=recv_sem,
                                     device_id=(x, y, 1 - c), device_id_type=MESH).start()
        token[...] = jnp.zeros_like(token)

    land = lax.empty(p.shape, p.dtype)
    outs = pl.pallas_call(
        body, name=name,
        out_shape=(pltpu.SemaphoreType.DMA(()),) * 2 + (pltpu.HBM(p.shape, p.dtype), pltpu.HBM(p.shape, p.dtype),
                                                        jax.ShapeDtypeStruct((8, 128), F32)),
        in_specs=(_HBM, _HBM, _ANY), out_specs=(_SEM,) * 2 + (_HBM, _HBM, pl.BlockSpec(memory_space=pltpu.VMEM)),
        input_output_aliases={0: 2, 1: 3}, compiler_params=pltpu.CompilerParams(has_side_effects=_DATAFLOW),
    )(_hbm(p), _hbm(land), order)
    return outs[:2], outs[2], outs[3], outs[4]


def _sibling_wait(sems, p_thru, land_thru, after, name):
    def body(p_ref, land_ref, send_sem, recv_sem, after_ref, p_dead, got_ref):
        x, y, c = _me()
        cp = pltpu.make_async_remote_copy(src_ref=p_ref, dst_ref=land_ref, send_sem=send_sem, recv_sem=recv_sem,
                                          device_id=(x, y, 1 - c), device_id_type=MESH)
        cp.wait_send()
        cp.wait_recv()

    return pl.pallas_call(
        body, name=name, out_shape=(pltpu.HBM(p_thru.shape, p_thru.dtype), pltpu.HBM(land_thru.shape, land_thru.dtype)),
        in_specs=(_HBM, _HBM, _SEM, _SEM, _ANY), out_specs=(_HBM, _HBM), input_output_aliases={0: 0, 1: 1},
        compiler_params=pltpu.CompilerParams(has_side_effects=_DATAFLOW),
    )(p_thru, land_thru, *sems, after)


def _mm(a, b, *, mode, name, out_dtypes=(F32,), epilogue=None, extras=(), tm=1024, tn=1024, tk=2048,
        col_blocked_b=False, col_blocked_out=False, order=None):
    CB = 1024
    if col_blocked_b:
        assert mode in ("nn", "nt") and b.shape[2] == CB
        (M, K), N = a.shape, (b.shape[0] * CB if mode == "nn" else b.shape[1])
        assert mode == "nn" or tk % CB == 0
        tn = CB if mode == "nn" else tn
    elif mode == "nn":
        (M, K), N = a.shape, b.shape[1]
    elif mode == "nt":
        (M, K), N = a.shape, b.shape[0]
    else:
        (K, M), N = a.shape, b.shape[1]
    if col_blocked_out:
        assert len(out_dtypes) == 1 and N % CB == 0
        tn = CB
    tm, tn, tk = min(tm, M), min(tn, N), min(tk, K)
    assert M % tm == 0 and N % tn == 0 and K % tk == 0, (M, N, K, tm, tn, tk)
    nk = K // tk
    ne, no = len(extras), len(out_dtypes)
    dims = {"nn": (((1,), (0,)), ((), ())), "nt": (((1,), (1,)), ((), ())), "tn": (((0,), (0,)), ((), ()))}[mode]

    no_ = 0 if order is None else 1

    def body(a_ref, b_ref, *rest):
        rest = rest[no_:]
        ex, outs = rest[:ne], rest[ne:ne + no]

        def finish(acc):
            res = epilogue(acc, *[e[...] for e in ex]) if epilogue is not None else (acc,)
            for o, r in zip(outs, res):
                o[...] = r.astype(o.dtype)

        if col_blocked_b and mode == "nt":
            part = sum(lax.dot_general(a_ref[:, q * CB:(q + 1) * CB], b_ref[q], dims, preferred_element_type=F32)
                       for q in range(tk // CB))
        else:
            part = lax.dot_general(a_ref[...], b_ref[...], dims, preferred_element_type=F32)
        if nk == 1:
            finish(part)
        else:
            acc_ref = rest[-1]
            k = pl.program_id(2)

            @pl.when(k == 0)
            def _():
                acc_ref[...] = part

            @pl.when(k > 0)
            def _():
                acc_ref[...] += part

            @pl.when(k == nk - 1)
            def _():
                finish(acc_ref[...])

    a_spec = {"nn": pl.BlockSpec((tm, tk), lambda i, j, k: (i, k)), "nt": pl.BlockSpec((tm, tk), lambda i, j, k: (i, k)),
              "tn": pl.BlockSpec((tk, tm), lambda i, j, k: (k, i))}[mode]
    b_spec = {"nn": pl.BlockSpec((tk, tn), lambda i, j, k: (k, j)), "nt": pl.BlockSpec((tn, tk), lambda i, j, k: (j, k)),
              "tn": pl.BlockSpec((tk, tn), lambda i, j, k: (k, j))}[mode]
    if col_blocked_b:
        b_spec = (pl.BlockSpec((None, tk, CB), lambda i, j, k: (j, k, 0)) if mode == "nn"
                  else pl.BlockSpec((tk // CB, tn, CB), lambda i, j, k: (k, j, 0)))
    e_spec = pl.BlockSpec((tm, tn), lambda i, j, k: (i, j))
    o_spec, o_dims = e_spec, (M, N)
    if col_blocked_out:
        o_spec, o_dims = pl.BlockSpec((None, tm, CB), lambda i, j, k: (j, i, 0)), (N // CB, M, CB)
    outs = pl.pallas_call(
        body, name=name, grid=(M // tm, N // tn, nk),
        in_specs=[a_spec, b_spec] + [_ANY] * no_ + [e_spec] * ne, out_specs=[o_spec] * no,
        out_shape=[jax.ShapeDtypeStruct(o_dims, dt) for dt in out_dtypes],
        scratch_shapes=[pltpu.VMEM((tm, tn), F32)] if nk > 1 else [],
        compiler_params=pltpu.CompilerParams(dimension_semantics=("parallel", "parallel", "arbitrary")),
    )(a, b, *([] if order is None else [order]), *extras)
    return outs if no > 1 else outs[0]


def _dw_half(a, b, side, *, axis, name, add=None, order=None, tile=1024, tk=2048):
    (K, M), N = a.shape, b.shape[1]
    tk = min(tk, K)
    nk = K // tk
    if axis == "m":
        tm, tn = tile, min(N, 1024)
        grid, o_dims = (4, N // tn, nk), (4, tile, N)
        a_spec = pl.BlockSpec((tk, tm), lambda q, j, k, s: (k, 2 * q + s[0]))
        b_spec = pl.BlockSpec((tk, tn), lambda q, j, k, s: (k, j))
        o_spec = pl.BlockSpec((None, tm, tn), lambda q, j, k, s: (q, 0, j))
    else:
        tm, tn = min(M, 1024), tile
        grid, o_dims = (M // tm, 4, nk), (4, M, tile)
        a_spec = pl.BlockSpec((tk, tm), lambda i, q, k, s: (k, i))
        b_spec = pl.BlockSpec((tk, tn), lambda i, q, k, s: (k, 2 * q + s[0]))
        o_spec = pl.BlockSpec((None, tm, tn), lambda i, q, k, s: (q, i, 0))
    n_order, n_add = int(order is not None), int(add is not None)
    n_out = 1 + n_add

    def body(s_ref, a_ref, b_ref, *rest):
        rest = rest[n_order:]
        outs, acc_ref = rest[n_add:n_add + n_out], rest[-1]
        k = pl.program_id(2)
        part = _dot_tn(a_ref[...], b_ref[...])

        @pl.when(k == 0)
        def _():
            acc_ref[...] = part

        @pl.when(k > 0)
        def _():
            acc_ref[...] += part

        @pl.when(k == nk - 1)
        def _():
            res = acc_ref[...] + rest[0][...].astype(F32) if n_add else acc_ref[...]
            for o in outs:
                o[...] = res.astype(o.dtype)

    outs = pl.pallas_call(
        body, name=name, out_shape=[jax.ShapeDtypeStruct(o_dims, BF16)] * n_out,
        grid_spec=pltpu.PrefetchScalarGridSpec(
            num_scalar_prefetch=1, grid=grid, in_specs=[a_spec, b_spec] + [_ANY] * n_order + [o_spec] * n_add,
            out_specs=[o_spec] * n_out, scratch_shapes=[pltpu.VMEM((tm, tn), F32)]),
        compiler_params=pltpu.CompilerParams(dimension_semantics=("arbitrary", "arbitrary", "arbitrary")),
    )(side, a, b, *([order] if n_order else []), *([add] if n_add else []))
    return outs if n_add else outs[0]


def _norm_fwd(x, g, sc, sh, name, resid=None):
    B, S, Dm = x.shape
    ts = min(S, 256)
    tok = pl.BlockSpec((None, ts, Dm), lambda b, i: (b, i, 0))
    row = pl.BlockSpec((None, 1, Dm), lambda b, i: (b, 0, 0))
    par = pl.BlockSpec((1, Dm), lambda b, i: (0, 0))

    def body(*refs):
        if resid is not None:
            x_ref, br_ref, gt_ref, g_ref, sc_ref, sh_ref, xo_ref, h_ref = refs
            xv = x_ref[...] + gt_ref[...] * br_ref[...]
            xo_ref[...] = xv
        else:
            x_ref, g_ref, sc_ref, sh_ref, h_ref = refs
            xv = x_ref[...]
        r = lax.rsqrt(jnp.mean(xv * xv, axis=-1, keepdims=True) + EPS)
        h_ref[...] = ((xv * r * g_ref[...]) * (1.0 + sc_ref[...]) + sh_ref[...]).astype(BF16)

    h_shape = jax.ShapeDtypeStruct((B, S, Dm), BF16)
    if resid is not None:
        return pl.pallas_call(body, name=name, grid=(B, S // ts), in_specs=[tok, tok, row, par, row, row],
                              out_specs=[tok, tok], out_shape=[jax.ShapeDtypeStruct((B, S, Dm), F32), h_shape],
                              )(x, resid[0], resid[1], g, sc, sh)
    return pl.pallas_call(body, name=name, grid=(B, S // ts), in_specs=[tok, par, row, row], out_specs=tok,
                          out_shape=h_shape)(x, g, sc, sh)


def _norm_bwd(x, g, name, *, sc=None, dh=None, dres=None, tgt=None, br=None, gate=None, x_is_prev=False):
    B, S, Dm = x.shape
    ts = min(S, 256)
    final = tgt is not None
    has_br = br is not None
    tok = pl.BlockSpec((None, ts, Dm), lambda b, i: (b, i, 0))
    row = pl.BlockSpec((None, 1, Dm), lambda b, i: (b, 0, 0))
    par = pl.BlockSpec((1, Dm), lambda b, i: (0, 0))
    ins, in_specs = [x, g], [tok, par]
    if final:
        ins, in_specs = ins + [tgt], in_specs + [tok]
    else:
        ins, in_specs = ins + [sc, dh], in_specs + [row, tok]
    if dres is not None:
        ins, in_specs = ins + [dres], in_specs + [tok]
    if has_br:
        ins, in_specs = ins + [br, gate], in_specs + [tok, row]
    n_in = len(ins)
    out_shape = [jax.ShapeDtypeStruct((B, S, Dm), F32), jax.ShapeDtypeStruct((1, Dm), F32)]
    out_specs = [tok, par]
    if final:
        out_shape.append(jax.ShapeDtypeStruct((1, 128), F32))
        out_specs.append(pl.BlockSpec((1, 128), lambda b, i: (0, 0)))
    else:
        out_shape += [jax.ShapeDtypeStruct((B, 1, Dm), F32)] * 2
        out_specs += [row, row]
    if has_br:
        out_shape += [jax.ShapeDtypeStruct((B, S, Dm), BF16), jax.ShapeDtypeStruct((B, 1, Dm), F32)]
        out_specs += [tok, row]

    def body(*refs):
        it = iter(refs[:n_in])
        outs = iter(refs[n_in:])
        x_ref, g_ref = next(it), next(it)
        b, i = pl.program_id(0), pl.program_id(1)
        first, first_row = (b == 0) & (i == 0), i == 0
        xv, gv = x_ref[...], g_ref[...]
        if x_is_prev:
            xv = xv + refs[n_in - 1][...] * refs[n_in - 2][...]
        r = lax.rsqrt(jnp.mean(xv * xv, axis=-1, keepdims=True) + EPS)
        n = xv * r
        dx_ref, dg_ref = next(outs), next(outs)

        def acc(ref, val, init):
            @pl.when(init)
            def _():
                ref[...] = val

            @pl.when(jnp.logical_not(init))
            def _():
                ref[...] += val

        if final:
            t_ref = next(it)
            loss_ref = next(outs)
            e = n * gv - t_ref[...]
            acc(loss_ref, jnp.zeros((1, 128), F32) + 0.5 * jnp.sum(e * e) / Dm, first)
            dyg = e * (1.0 / Dm)
        else:
            sc_ref, dh_ref = next(it), next(it)
            dsc_ref, dsh_ref = next(outs), next(outs)
            dhv = dh_ref[...].astype(F32)
            acc(dsh_ref, jnp.sum(dhv, axis=0, keepdims=True), first_row)
            acc(dsc_ref, jnp.sum(dhv * (n * gv), axis=0, keepdims=True), first_row)
            dyg = dhv * (1.0 + sc_ref[...])
        acc(dg_ref, jnp.sum(dyg * n, axis=0, keepdims=True), first)
        dn = dyg * gv
        dx = r * (dn - n * jnp.mean(dn * n, axis=-1, keepdims=True))
        if dres is not None:
            dx = dx + next(it)[...]
        dx_ref[...] = dx
        if has_br:
            br_ref, gt_ref = next(it), next(it)
            dbr_ref, dgt_ref = next(outs), next(outs)
            dbr_ref[...] = (dx * gt_ref[...]).astype(BF16)
            acc(dgt_ref, jnp.sum(dx * br_ref[...], axis=0, keepdims=True), first_row)

    outs = pl.pallas_call(body, name=name, grid=(B, S // ts), in_specs=in_specs, out_specs=out_specs, out_shape=out_shape,
                          compiler_params=pltpu.CompilerParams(dimension_semantics=("arbitrary", "arbitrary")))(*ins)
    res = dict(dx=outs[0], dg=outs[1])
    if final:
        res["loss"] = outs[2]
    else:
        res["dsc"], res["dsh"] = outs[2], outs[3]
    if has_br:
        res["dbr"], res["dgate"] = outs[-2], outs[-1]
    return res


def _gm_heads(vg, lng, lnb):
    res = []
    for h in range(GM_H):
        sl = slice(h * 128, (h + 1) * 128)
        vh = vg[:, sl]
        xc = vh - jnp.mean(vh, axis=-1, keepdims=True)
        rstd = lax.rsqrt(jnp.mean(xc * xc, axis=-1, keepdims=True) + 1e-5)
        xhat = xc * rstd
        res.append((xhat, rstd, xhat * lng[:, sl] + lnb[:, sl]))
    return res


def _gm_gate(heads, wt_ref, bsx, nch):
    cols = []
    for h in range(GM_H):
        vn = heads[h][2].astype(BF16)
        rows = [_dot(wt_ref[h], vn[c * CHUNK:(c + 1) * CHUNK]) + bsx[:, h * 128:(h + 1) * 128] for c in range(nch)]
        cols.append(jnp.concatenate(rows, axis=0) if nch > 1 else rows[0])
    return jnp.concatenate(cols, axis=1)


def _gm_specs(S):
    tb = min(S, 512)
    u = pl.BlockSpec((None, tb, GM_W), lambda b, i: (b, i, OFF["u"] // GM_W))
    v = pl.BlockSpec((None, tb, GM_W), lambda b, i: (b, i, OFF["v"] // GM_W))
    tok = pl.BlockSpec((None, tb, GM_W), lambda b, i: (b, i, 0))
    return tb, u, v, tok


def _gmlp_fwd(P, lng, lnb, wt, bsx, og, name):
    B, S, _ = P.shape
    tb, u_spec, v_spec, tok = _gm_specs(S)
    nch = tb // CHUNK

    def body(u_ref, v_ref, lng_ref, lnb_ref, wt_ref, bsx_ref, og_ref, o_ref):
        heads = _gm_heads(_gelu(v_ref[...]), lng_ref[...], lnb_ref[...])
        y = _gelu(u_ref[...]) * _gm_gate(heads, wt_ref, bsx_ref[...], nch)
        r = lax.rsqrt(jnp.mean(y * y, axis=-1, keepdims=True) + EPS)
        o_ref[...] = (y * r * og_ref[...]).astype(BF16)

    return pl.pallas_call(
        body, name=name, grid=(B, S // tb),
        in_specs=[u_spec, v_spec, _full((1, GM_W)), _full((1, GM_W)), _full((GM_H, 128, 128)), _full((128, GM_W)), _full((1, GM_W))],
        out_specs=tok, out_shape=jax.ShapeDtypeStruct((B, S, GM_W + ATT_W + SSM_W), BF16))(P, P, lng, lnb, wt, bsx, og)


def _gmlp_bwd(P, dcat, lng, lnb, wt, wtT, bsx, og, name):
    B, S, _ = P.shape
    tb, u_spec, v_spec, tok = _gm_specs(S)
    nch = tb // CHUNK
    do_spec = pl.BlockSpec((None, tb, GM_W), lambda b, i: (b, i, 0))

    def body(u_ref, v_ref, do_ref, lng_ref, lnb_ref, wt_ref, wtT_ref, bsx_ref, og_ref,
             du_ref, dv_ref, dlng_ref, dlnb_ref, dws_ref, dbsx_ref, dog_ref):
        first = (pl.program_id(0) == 0) & (pl.program_id(1) == 0)

        @pl.when(first)
        def _():
            for ref in (dlng_ref, dlnb_ref, dws_ref, dbsx_ref, dog_ref):
                ref[...] = jnp.zeros(ref.shape, F32)

        u, v, lng = u_ref[...], v_ref[...], lng_ref[...]
        ug = _gelu(u)
        heads = _gm_heads(_gelu(v), lng, lnb_ref[...])
        gate = _gm_gate(heads, wt_ref, bsx_ref[...], nch)
        y = ug * gate
        r = lax.rsqrt(jnp.mean(y * y, axis=-1, keepdims=True) + EPS)
        yn = y * r
        dout = do_ref[...].astype(F32)
        dog_ref[...] += jnp.sum(dout * yn, axis=0, keepdims=True)
        dyn = dout * og_ref[...]
        dy = r * (dyn - yn * jnp.mean(dyn * yn, axis=-1, keepdims=True))
        du_ref[...] = (dy * gate * _gelu_grad(u)).astype(BF16)
        dgate = dy * ug
        tril = lax.broadcasted_iota(jnp.int32, (128, 128), 0) >= lax.broadcasted_iota(jnp.int32, (128, 128), 1)
        dvg = []
        for h in range(GM_H):
            sl = slice(h * 128, (h + 1) * 128)
            xhat, rstd, vn = heads[h]
            vnb = vn.astype(BF16)
            dgh = dgate[:, sl]
            dgb = dgh.astype(BF16)
            dbs = jnp.zeros((128, 128), F32)
            dw = jnp.zeros((128, 128), F32)
            dvn = []
            for c in range(nch):
                rs = slice(c * CHUNK, (c + 1) * CHUNK)
                dbs = dbs + dgh[rs]
                dw = dw + _dot_nt(dgb[rs], vnb[rs])
                dvn.append(_dot(wtT_ref[h], dgb[rs]))
            dvn = jnp.concatenate(dvn, axis=0) if nch > 1 else dvn[0]
            dbsx_ref[:, sl] += dbs
            dws_ref[h] += jnp.where(tril, dw, 0.0)
            dlng_ref[:, sl] += jnp.sum(dvn * xhat, axis=0, keepdims=True)
            dlnb_ref[:, sl] += jnp.sum(dvn, axis=0, keepdims=True)
            dxh = dvn * lng[:, sl]
            dvg.append(rstd * (dxh - jnp.mean(dxh, axis=-1, keepdims=True) - xhat * jnp.mean(dxh * xhat, axis=-1, keepdims=True)))
        dv_ref[...] = (jnp.concatenate(dvg, axis=1) * _gelu_grad(v)).astype(BF16)

    p512, w3 = _full((1, GM_W)), _full((GM_H, 128, 128))
    return pl.pallas_call(
        body, name=name, grid=(B, S // tb),
        in_specs=[u_spec, v_spec, do_spec, p512, p512, w3, w3, _full((128, GM_W)), p512],
        out_specs=[tok, tok, p512, p512, w3, _full((128, GM_W)), p512],
        out_shape=[jax.ShapeDtypeStruct((B, S, GM_W), BF16)] * 2 + [
            jax.ShapeDtypeStruct((1, GM_W), F32), jax.ShapeDtypeStruct((1, GM_W), F32),
            jax.ShapeDtypeStruct((GM_H, 128, 128), F32), jax.ShapeDtypeStruct((128, GM_W), F32),
            jax.ShapeDtypeStruct((1, GM_W), F32)],
        compiler_params=pltpu.CompilerParams(dimension_semantics=("arbitrary", "arbitrary")),
    )(P, P, dcat, lng, lnb, wt, wtT, bsx, og)


def _lane_half():
    return lax.broadcasted_iota(jnp.int32, (128, 128), 1) // 64


def _att_stack(x, kvh, dtype):
    half = _lane_half()
    rows = []
    for g in range(4):
        i = kvh * 4 + g
        pair = x[:, (i // 2) * 128:(i // 2 + 1) * 128]
        if i % 2 != kvh:
            pair = pltpu.roll(pair, 64, 1)
        rows.append(jnp.where(half == kvh, pair, 0.0))
    return jnp.concatenate(rows, axis=0).astype(dtype)


def _att_unstack(pairs, y, kvh):
    half = _lane_half()
    for g in range(4):
        i = kvh * 4 + g
        piece = y[g * 128:(g + 1) * 128]
        if i % 2 != kvh:
            piece = pltpu.roll(piece, 64, 1)
        pairs[i // 2] = jnp.where(half == i % 2, piece, pairs[i // 2])
    return pairs


def _att_probs(qb, k2, st, sink_ref, kvh):
    qm = _att_stack(qb, kvh, BF16)
    s = _dot_nt(qm, k2) * (64 ** -0.5)
    qi = lax.broadcasted_iota(jnp.int32, (512, 256), 0) % 128
    kj = lax.broadcasted_iota(jnp.int32, (512, 256), 1)
    diff = qi + 128 - kj
    valid = (diff >= 0) & (diff < 128) & (st + kj - 128 >= 0)
    s = jnp.where(valid, s, NEG_INF)
    grp = lax.broadcasted_iota(jnp.int32, (512, 1), 0) // 128
    sink = jnp.zeros((512, 1), F32)
    for g in range(4):
        sink = jnp.where(grp == g, sink_ref[kvh * 4 + g], sink)
    m = jnp.maximum(jnp.max(s, axis=-1, keepdims=True), sink)
    e = jnp.exp(s - m)
    esink = jnp.exp(sink - m)
    inv = 1.0 / (jnp.sum(e, axis=-1, keepdims=True) + esink)
    return qm, e * inv, esink * inv


def _att_specs(S):
    q = pl.BlockSpec((None, S, ATT_W), lambda b: (b, 0, OFF["q"] // ATT_W))
    k = pl.BlockSpec((None, S, KV_W), lambda b: (b, 0, OFF["k"] // KV_W))
    v = pl.BlockSpec((None, S, KV_W), lambda b: (b, 0, OFF["vv"] // KV_W))
    tok = pl.BlockSpec((None, S, ATT_W), lambda b: (b, 0, 0))
    kv = pl.BlockSpec((None, S, KV_W), lambda b: (b, 0, 0))
    return q, k, v, tok, kv


_SMEM = pl.BlockSpec(memory_space=pltpu.SMEM)


def _attn_fwd(P, sinks, og, cat, name):
    B, S, _ = P.shape
    q_spec, k_spec, v_spec, _, _ = _att_specs(S)
    tok = pl.BlockSpec((None, S, ATT_W), lambda b: (b, 0, GM_W // ATT_W))

    def body(q_ref, k_ref, v_ref, sink_ref, og_ref, cat_ref, o_ref, kpad, vpad):
        kpad[0:128, :] = jnp.zeros((128, KV_W), BF16)
        vpad[0:128, :] = jnp.zeros((128, KV_W), BF16)
        kpad[128:, :] = k_ref[...].astype(BF16)
        vpad[128:, :] = v_ref[...].astype(BF16)

        def step(n, carry):
            st = pl.multiple_of(n * 128, 128)
            qb = q_ref[pl.ds(st, 128), :]
            k2, v2 = kpad[pl.ds(st, 256), :], vpad[pl.ds(st, 256), :]
            pairs = [jnp.zeros((128, 128), F32)] * 4
            for kvh in range(2):
                _, p, _ = _att_probs(qb, k2, st, sink_ref, kvh)
                pairs = _att_unstack(pairs, _dot(p.astype(BF16), v2), kvh)
            o = jnp.concatenate(pairs, axis=1)
            r = lax.rsqrt(jnp.mean(o * o, axis=-1, keepdims=True) + EPS)
            o_ref[pl.ds(st, 128), :] = (o * r * og_ref[...]).astype(BF16)
            return carry

        lax.fori_loop(0, S // 128, step, 0)

    return pl.pallas_call(
        body, name=name, grid=(B,), in_specs=[q_spec, k_spec, v_spec, _SMEM, _full((1, ATT_W)), _ANY], out_specs=tok,
        out_shape=jax.ShapeDtypeStruct(cat.shape, BF16), input_output_aliases={5: 0},
        scratch_shapes=[pltpu.VMEM((S + 128, KV_W), BF16)] * 2)(P, P, P, sinks, og, cat)


def _attn_bwd(P, dcat, sinks, og, name):
    B, S, _ = P.shape
    q_spec, k_spec, v_spec, tok, kv = _att_specs(S)
    do_spec = pl.BlockSpec((None, S, ATT_W), lambda b: (b, 0, GM_W // ATT_W))

    def body(q_ref, k_ref, v_ref, do_ref, sink_ref, og_ref, dq_ref, dk_ref, dv_ref, dsink_ref, dog_ref,
             kpad, vpad, dkpad, dvpad):
        @pl.when(pl.program_id(0) == 0)
        def _():
            dsink_ref[...] = jnp.zeros((8, 128), F32)
            dog_ref[...] = jnp.zeros((1, ATT_W), F32)

        kpad[0:128, :] = jnp.zeros((128, KV_W), BF16)
        vpad[0:128, :] = jnp.zeros((128, KV_W), BF16)
        kpad[128:, :] = k_ref[...].astype(BF16)
        vpad[128:, :] = v_ref[...].astype(BF16)
        dkpad[...] = jnp.zeros((S + 128, KV_W), F32)
        dvpad[...] = jnp.zeros((S + 128, KV_W), F32)
        half = _lane_half()
        head_row = lax.broadcasted_iota(jnp.int32, (8, 128), 0)

        def step(n, carry):
            st = pl.multiple_of(n * 128, 128)
            qb = q_ref[pl.ds(st, 128), :]
            k2, v2 = kpad[pl.ds(st, 256), :], vpad[pl.ds(st, 256), :]
            saved, pairs = [], [jnp.zeros((128, 128), F32)] * 4
            for kvh in range(2):
                qm, p, psink = _att_probs(qb, k2, st, sink_ref, kvh)
                o = _dot(p.astype(BF16), v2)
                saved.append((qm, p, psink, o))
                pairs = _att_unstack(pairs, o, kvh)
            o = jnp.concatenate(pairs, axis=1)
            r = lax.rsqrt(jnp.mean(o * o, axis=-1, keepdims=True) + EPS)
            on = o * r
            dout = do_ref[pl.ds(st, 128), :].astype(F32)
            dog_ref[...] += jnp.sum(dout * on, axis=0, keepdims=True)
            dyn = dout * og_ref[...]
            do = r * (dyn - on * jnp.mean(dyn * on, axis=-1, keepdims=True))
            dq_pairs = [jnp.zeros((128, 128), F32)] * 4
            dsink = jnp.zeros((8, 128), F32)
            for kvh in range(2):
                qm, p, psink, og_ = saved[kvh]
                dog = _att_stack(do, kvh, F32)
                delta = jnp.sum(dog * jnp.where(jnp.concatenate([half] * 4, axis=0) == kvh, og_, 0.0), axis=-1, keepdims=True)
                dogb, pb = dog.astype(BF16), p.astype(BF16)
                dvpad[pl.ds(st, 256), :] += _dot_tn(pb, dogb)
                dp = _dot_nt(dogb, v2)
                ds = (p * (dp - delta) * (64 ** -0.5)).astype(BF16)
                sd = psink * delta
                for g in range(4):
                    dsink = dsink - jnp.where(head_row == kvh * 4 + g, jnp.sum(sd[g * 128:(g + 1) * 128]), 0.0)
                dq_pairs = _att_unstack(dq_pairs, _dot(ds, k2), kvh)
                dkpad[pl.ds(st, 256), :] += _dot_tn(ds, qm)
            dsink_ref[...] += dsink
            dq_ref[pl.ds(st, 128), :] = jnp.concatenate(dq_pairs, axis=1).astype(BF16)
            return carry

        lax.fori_loop(0, S // 128, step, 0)
        dk_ref[...] = dkpad[128:, :].astype(BF16)
        dv_ref[...] = dvpad[128:, :].astype(BF16)

    return pl.pallas_call(
        body, name=name, grid=(B,),
        in_specs=[q_spec, k_spec, v_spec, do_spec, _SMEM, _full((1, ATT_W))],
        out_specs=[tok, kv, kv, _full((8, 128)), _full((1, ATT_W))],
        out_shape=[jax.ShapeDtypeStruct((B, S, ATT_W), BF16), jax.ShapeDtypeStruct((B, S, KV_W), BF16),
                   jax.ShapeDtypeStruct((B, S, KV_W), BF16), jax.ShapeDtypeStruct((8, 128), F32),
                   jax.ShapeDtypeStruct((1, ATT_W), F32)],
        scratch_shapes=[pltpu.VMEM((S + 128, KV_W), BF16)] * 2 + [pltpu.VMEM((S + 128, KV_W), F32)] * 2,
        compiler_params=pltpu.CompilerParams(dimension_semantics=("arbitrary",)),
    )(P, P, P, dcat, sinks, og)


CONV_TC = 256
CONV_RC = 64


def _conv_taps(ext, r0):
    return [ext[pl.ds(r0 + 8 - k, CONV_RC), :] for k in range(4)]


def _conv_pre(taps, w_ref, b_ref):
    acc = b_ref[...] + w_ref[3:4, :] * taps[0]
    for k in range(1, 4):
        acc = acc + w_ref[3 - k:4 - k, :] * taps[k]
    return acc


def _conv_fwd(P, w8, b, name):
    B, S, _ = P.shape
    nj = CONV_CH // CONV_TC
    x_spec = pl.BlockSpec((None, S, CONV_TC), lambda b_, j: (b_, 0, OFF["xbc"] // CONV_TC + j))
    tok = pl.BlockSpec((None, S, CONV_TC), lambda b_, j: (b_, 0, j))

    def body(x_ref, w_ref, b_ref, o_ref, ext):
        ext[0:8, :] = jnp.zeros((8, CONV_TC), F32)
        ext[8:, :] = x_ref[...]
        for r0 in range(0, S, CONV_RC):
            pre = _conv_pre(_conv_taps(ext, r0), w_ref, b_ref)
            o_ref[pl.ds(r0, CONV_RC), :] = pre * _sigmoid(pre)

    return pl.pallas_call(
        body, name=name, grid=(B, nj),
        in_specs=[x_spec, pl.BlockSpec((8, CONV_TC), lambda b_, j: (0, j)), pl.BlockSpec((1, CONV_TC), lambda b_, j: (0, j))],
        out_specs=tok, out_shape=jax.ShapeDtypeStruct((B, S, CONV_CH), F32),
        scratch_shapes=[pltpu.VMEM((S + 8, CONV_TC), F32)])(P, w8, b)


def _conv_bwd(P, dact, w8, b, name):
    B, S, _ = P.shape
    nj = CONV_CH // CONV_TC
    x_spec = pl.BlockSpec((None, S, CONV_TC), lambda j, b_: (b_, 0, OFF["xbc"] // CONV_TC + j))
    tok = pl.BlockSpec((None, S, CONV_TC), lambda j, b_: (b_, 0, j))
    w_spec = pl.BlockSpec((8, CONV_TC), lambda j, b_: (0, j))
    b_spec = pl.BlockSpec((1, CONV_TC), lambda j, b_: (0, j))

    def body(x_ref, d_ref, w_ref, b_ref, dx_ref, dw_ref, db_ref, ext, extd):
        @pl.when(pl.program_id(1) == 0)
        def _():
            dw_ref[...] = jnp.zeros((8, CONV_TC), F32)
            db_ref[...] = jnp.zeros((1, CONV_TC), F32)

        ext[0:8, :] = jnp.zeros((8, CONV_TC), F32)
        ext[8:, :] = x_ref[...]
        extd[pl.ds(8 + S, 8), :] = jnp.zeros((8, CONV_TC), F32)
        db = jnp.zeros((1, CONV_TC), F32)
        dws = [jnp.zeros((1, CONV_TC), F32)] * 4
        for r0 in range(0, S, CONV_RC):
            taps = _conv_taps(ext, r0)
            pre = _conv_pre(taps, w_ref, b_ref)
            sg = _sigmoid(pre)
            dpre = d_ref[pl.ds(r0, CONV_RC), :] * (sg * (1.0 + pre * (1.0 - sg)))
            extd[pl.ds(8 + r0, CONV_RC), :] = dpre
            db = db + jnp.sum(dpre, axis=0, keepdims=True)
            dws = [dws[i] + jnp.sum(dpre * taps[3 - i], axis=0, keepdims=True) for i in range(4)]
        for r0 in range(0, S, CONV_RC):
            dx = w_ref[3:4, :] * extd[pl.ds(8 + r0, CONV_RC), :]
            for k in range(1, 4):
                dx = dx + w_ref[3 - k:4 - k, :] * extd[pl.ds(8 + r0 + k, CONV_RC), :]
            dx_ref[pl.ds(r0, CONV_RC), :] = dx.astype(BF16)
        db_ref[...] += db
        sub = lax.broadcasted_iota(jnp.int32, (8, CONV_TC), 0)
        dw_ref[...] += sum(jnp.where(sub == i, dws[i], 0.0) for i in range(4))

    return pl.pallas_call(
        body, name=name, grid=(nj, B), in_specs=[x_spec, tok, w_spec, b_spec], out_specs=[tok, w_spec, b_spec],
        out_shape=[jax.ShapeDtypeStruct((B, S, CONV_CH), BF16), jax.ShapeDtypeStruct((8, CONV_CH), F32),
                   jax.ShapeDtypeStruct((1, CONV_CH), F32)],
        scratch_shapes=[pltpu.VMEM((S + 8, CONV_TC), F32), pltpu.VMEM((S + 16, CONV_TC), F32)],
        compiler_params=pltpu.CompilerParams(dimension_semantics=("arbitrary", "arbitrary")),
    )(P, dact, w8, b)


def _ssd_consts():
    hd = np.arange(SSM_W) // SSM_HD
    E = (np.arange(128)[:, None] == hd[None, :]).astype(np.float32)
    tri = (np.arange(128)[:, None] >= np.arange(128)[None, :]).astype(np.float32)
    return jnp.asarray(E, BF16), jnp.asarray(E.T, BF16), jnp.asarray(tri, BF16), jnp.asarray(tri.T, BF16)


def _pieces(x, n):
    out, r = [], x
    for _ in range(n):
        p = r.astype(BF16)
        out.append(p)
        r = r - p.astype(F32)
    return out


def _dot01(x, m01, n):
    return sum(_dot(p, m01) for p in _pieces(x, n))


def _dot01_left(m01, x, n):
    return sum(_dot(m01, p) for p in _pieces(x, n))


def _ssd_pre(xa, dtraw, bias, alog, E, tri):
    lane = lax.broadcasted_iota(jnp.int32, (128, 128), 1)
    pre = dtraw + bias
    dtp = jnp.where(lane < SSM_H, jnp.maximum(pre, 0.0) + jnp.log(1.0 + jnp.exp(-jnp.abs(pre))), 0.0)
    a = -jnp.exp(alog)
    acs = _dot01_left(tri, dtp * a, 3)
    acsT = acs.T
    dtE, acsE = _dot01(dtp, E, 2), _dot01(acs, E, 3)
    X = xa[:, :SSM_W]
    xdt = X * dtE
    wE = jnp.exp(acsE[127:128, :] - acsE)
    eE = jnp.exp(acsE)
    cdE = eE[127:128, :]
    return dict(pre=pre, dtp=dtp, a=a, acs=acs, acsT=acsT, dtE=dtE, acsE=acsE, cdE=cdE, X=X, xdt=xdt, wE=wE, eE=eE)


def _ssd_decay(c, h):
    lm = lax.broadcasted_iota(jnp.int32, (128, 128), 0) >= lax.broadcasted_iota(jnp.int32, (128, 128), 1)
    return jnp.exp(jnp.where(lm, c["acs"][:, h:h + 1] - c["acsT"][h:h + 1, :], NEG_INF))


def _ssd_pair_operands(c, CB, h0):
    lane = lax.broadcasted_iota(jnp.int32, (128, 128), 1)
    L0, L1 = _ssd_decay(c, h0), _ssd_decay(c, h0 + 1)
    M = jnp.concatenate([CB * L0, CB * L1], axis=1).astype(BF16)
    xp = c["xdt"][:, h0 * 64:h0 * 64 + 128]
    BD = jnp.concatenate([jnp.where(lane < 64, xp, 0.0), jnp.where(lane >= 64, xp, 0.0)], axis=0).astype(BF16)
    return L0, L1, M, BD


def _ssd_y(c, xa, state_ref, dskipE):
    per_group, ys = [], []
    for g in range(SSM_G):
        gs = slice(g * 512, (g + 1) * 512)
        Bb = xa[:, SSM_W + g * 128:SSM_W + (g + 1) * 128].astype(BF16)
        Cb = xa[:, SSM_W + 256 + g * 128:SSM_W + 256 + (g + 1) * 128].astype(BF16)
        CB = _dot_nt(Cb, Bb)
        Sg = state_ref[:, gs]
        yoff = _dot(Cb, Sg.astype(BF16)) * c["eE"][:, gs]
        ydiag, pairs = [], []
        for j in range(4):
            ops = _ssd_pair_operands(c, CB, g * 8 + 2 * j)
            pairs.append(ops)
            ydiag.append(_dot(ops[2], ops[3]))
        ys.append(jnp.concatenate(ydiag, axis=1) + yoff)
        per_group.append(dict(Bb=Bb, Cb=Cb, CB=CB, Sg=Sg, yoff=yoff, pairs=pairs))
    Y = jnp.concatenate(ys, axis=1) + c["X"] * dskipE
    return Y, per_group


def _ssd_specs(S, rev):
    nc = S // CHUNK
    cm = (lambda b, i: (b, nc - 1 - i)) if rev else (lambda b, i: (b, i))
    xa = pl.BlockSpec((None, CHUNK, CONV_CH), lambda b, i: cm(b, i) + (0,))
    z = [pl.BlockSpec((None, CHUNK, 256), lambda b, i, q=q: cm(b, i) + (OFF["z"] // 256 + q,)) for q in range(4)]
    dt = pl.BlockSpec((None, CHUNK, 128), lambda b, i: cm(b, i) + (OFF["dt"] // 128,))
    tok = pl.BlockSpec((None, CHUNK, SSM_W), lambda b, i: cm(b, i) + (0,))
    st = pl.BlockSpec((None, None, 128, SSM_W), lambda b, i: cm(b, i) + (0, 0))
    return nc, xa, z, dt, tok, st


def _ssd_fwd(xact, P, bias, alog, dskipE, ng, cat, name):
    B, S, _ = P.shape
    nc, xa_spec, z_specs, dt_spec, _, st_spec = _ssd_specs(S, False)
    tok = pl.BlockSpec((None, CHUNK, SSM_W), lambda b, i: (b, i, 1))
    E, _, tri, _ = _ssd_consts()

    def body(xa_ref, z0, z1, z2, z3, dt_ref, bias_ref, alog_ref, dsk_ref, ng_ref, E_ref, tri_ref, cat_ref, o_ref, sp_ref, state):
        @pl.when(pl.program_id(1) == 0)
        def _():
            state[...] = jnp.zeros((128, SSM_W), F32)

        sp_ref[...] = state[...]
        xa = xa_ref[...]
        c = _ssd_pre(xa, dt_ref[...], bias_ref[...], alog_ref[...], E_ref[...], tri_ref[...])
        Y, groups = _ssd_y(c, xa, state, dsk_ref[...])
        Z = (c["xdt"] * c["wE"]).astype(BF16)
        for g in range(SSM_G):
            gs = slice(g * 512, (g + 1) * 512)
            state[:, gs] = groups[g]["Sg"] * c["cdE"][:, gs] + _dot_tn(groups[g]["Bb"], Z[:, gs])
        zv = jnp.concatenate([z0[...], z1[...], z2[...], z3[...]], axis=1)
        yz = Y * (zv * _sigmoid(zv))
        outs = []
        for g in range(SSM_G):
            yg = yz[:, g * 512:(g + 1) * 512]
            outs.append(yg * lax.rsqrt(jnp.mean(yg * yg, axis=-1, keepdims=True) + EPS))
        o_ref[...] = (jnp.concatenate(outs, axis=1) * ng_ref[...]).astype(BF16)

    return pl.pallas_call(
        body, name=name, grid=(B, nc),
        in_specs=[xa_spec] + z_specs + [dt_spec, _full((1, 128)), _full((1, 128)), _full((1, SSM_W)), _full((1, SSM_W)),
                                        _full((128, SSM_W)), _full((128, 128)), _ANY],
        out_specs=[tok, st_spec],
        out_shape=[jax.ShapeDtypeStruct(cat.shape, BF16), jax.ShapeDtypeStruct((B, nc, 128, SSM_W), F32)],
        scratch_shapes=[pltpu.VMEM((128, SSM_W), F32)], input_output_aliases={12: 0},
        compiler_params=pltpu.CompilerParams(dimension_semantics=("arbitrary", "arbitrary")),
    )(xact, P, P, P, P, P, bias, alog, dskipE, ng, E, tri, cat)


def _ssd_bwd(xact, P, sprev, dcat, bias, alog, dskipE, ng, name):
    B, S, _ = P.shape
    nc, xa_spec, z_specs, dt_spec, tok, st_spec = _ssd_specs(S, True)
    do_spec = pl.BlockSpec((None, CHUNK, SSM_W), lambda b, i: (b, nc - 1 - i, 1))
    E, ET, tri, triT = _ssd_consts()
    dt_out = pl.BlockSpec((None, CHUNK, 128), lambda b, i: (b, nc - 1 - i, 0))

    def body(xa_ref, z0, z1, z2, z3, dt_ref, sp_ref, do_ref, bias_ref, alog_ref, dsk_ref, ng_ref, E_ref, ET_ref, tri_ref,
             triT_ref, dxa_ref, dz_ref, ddt_ref, dbias_ref, dalog_ref, ddsk_ref, dng_ref, dstate):
        first = (pl.program_id(0) == 0) & (pl.program_id(1) == 0)

        @pl.when(first)
        def _():
            for ref in (dbias_ref, dalog_ref, ddsk_ref, dng_ref):
                ref[...] = jnp.zeros(ref.shape, F32)

        @pl.when(pl.program_id(1) == 0)
        def _():
            dstate[...] = jnp.zeros((128, SSM_W), F32)

        xa, ETm = xa_ref[...], ET_ref[...]
        c = _ssd_pre(xa, dt_ref[...], bias_ref[...], alog_ref[...], E_ref[...], tri_ref[...])
        Y, groups = _ssd_y(c, xa, sp_ref, dsk_ref[...])
        X, xdt = c["X"], c["xdt"]
        zv = jnp.concatenate([z0[...], z1[...], z2[...], z3[...]], axis=1)
        sg = _sigmoid(zv)
        zs = zv * sg
        yz = Y * zs
        dout = do_ref[...].astype(F32)
        dyz = []
        for g in range(SSM_G):
            gs = slice(g * 512, (g + 1) * 512)
            yg = yz[:, gs]
            r = lax.rsqrt(jnp.mean(yg * yg, axis=-1, keepdims=True) + EPS)
            yn = yg * r
            dng_ref[:, gs] += jnp.sum(dout[:, gs] * yn, axis=0, keepdims=True)
            dyn = dout[:, gs] * ng_ref[:, gs]
            dyz.append(r * (dyn - yn * jnp.mean(dyn * yn, axis=-1, keepdims=True)))
        dyz = jnp.concatenate(dyz, axis=1)
        dz_ref[...] = (dyz * Y * (sg * (1.0 + zv * (1.0 - sg)))).astype(BF16)
        dY = dyz * zs
        ddsk_ref[...] += jnp.sum(dY * X, axis=0, keepdims=True)
        dX = dY * dsk_ref[...]
        lane = lax.broadcasted_iota(jnp.int32, (128, 128), 1)
        sub = lax.broadcasted_iota(jnp.int32, (128, 128), 0)
        colform = jnp.zeros((128, 128), F32)
        rowform = jnp.zeros((128, 128), F32)
        dxdt, gacsE, dBC = [], [], []
        for g in range(SSM_G):
            gs = slice(g * 512, (g + 1) * 512)
            G = groups[g]
            Bb, Cb, CB, Sg = G["Bb"], G["Cb"], G["CB"], G["Sg"]
            dYg = dY[:, gs]
            dQ = (dYg * c["eE"][:, gs]).astype(BF16)
            dSn = dstate[:, gs]
            dSnb = dSn.astype(BF16)
            cd = c["cdE"][:, gs]
            dC = _dot_nt(dQ, Sg.astype(BF16))
            dSprev = _dot_tn(Cb, dQ) + dSn * cd
            t1 = jnp.broadcast_to(jnp.sum(dSn * Sg * cd, axis=0, keepdims=True), (8, 512))
            colform = colform + jnp.where(sub == 127, _dot01(t1, ETm[gs, :], 2)[0:1, :], 0.0)
            Zg = xdt[:, gs] * c["wE"][:, gs]
            dZ = _dot(Bb, dSnb)
            dB = _dot_nt(Zg.astype(BF16), dSnb)
            U = dZ * Zg
            ga = dYg * G["yoff"] - U
            ga = ga + jnp.where(lax.broadcasted_iota(jnp.int32, (128, 512), 0) == 127, jnp.sum(U, axis=0, keepdims=True), 0.0)
            gacsE.append(ga)
            dxg = [None] * 4
            dCB = jnp.zeros((128, 128), F32)
            for j in range(4):
                h0 = g * 8 + 2 * j
                L0, L1, M, BD = G["pairs"][j]
                dYp = dYg[:, j * 128:(j + 1) * 128].astype(BF16)
                dM = _dot_nt(dYp, BD)
                dBD = _dot_tn(M, dYp)
                dxg[j] = jnp.where(lane < 64, dBD[:128], dBD[128:])
                for t, (h, L) in enumerate(((h0, L0), (h0 + 1, L1))):
                    dMh = dM[:, t * 128:(t + 1) * 128]
                    dCB = dCB + dMh * L
                    Gh = dMh * CB * L
                    colform = colform + jnp.where(lane == h, jnp.sum(Gh, axis=1, keepdims=True), 0.0)
                    rowform = rowform - jnp.where(sub == h, jnp.sum(Gh, axis=0, keepdims=True), 0.0)
            dCBb = dCB.astype(BF16)
            dC = dC + _dot(dCBb, Bb)
            dB = dB + _dot_tn(dCBb, Cb)
            dxdt.append(jnp.concatenate(dxg, axis=1) + dZ * c["wE"][:, gs])
            dBC.append((dB, dC))
            dstate[:, gs] = dSprev
        dxdt = jnp.concatenate(dxdt, axis=1)
        dX = dX + dxdt * c["dtE"]
        ddt = _dot01(dxdt * X, ETm, 2)
        dacs = colform + rowform.T + _dot01(jnp.concatenate(gacsE, axis=1), ETm, 2)
        dda = _dot01_left(triT_ref[...], dacs, 2)
        ddt = ddt + dda * c["a"]
        dalog_ref[...] += jnp.sum(dda * c["dtp"], axis=0, keepdims=True) * c["a"]
        ddtraw = jnp.where(lane < SSM_H, ddt * _sigmoid(c["pre"]), 0.0)
        dbias_ref[...] += jnp.sum(ddtraw, axis=0, keepdims=True)
        ddt_ref[...] = ddtraw.astype(BF16)
        dxa_ref[...] = jnp.concatenate([dX, dBC[0][0], dBC[1][0], dBC[0][1], dBC[1][1]], axis=1)

    p128, p1k = _full((1, 128)), _full((1, SSM_W))
    return pl.pallas_call(
        body, name=name, grid=(B, nc),
        in_specs=[xa_spec] + z_specs + [dt_spec, st_spec, do_spec, p128, p128, p1k, p1k,
                                        _full((128, SSM_W)), _full((SSM_W, 128)), _full((128, 128)), _full((128, 128))],
        out_specs=[xa_spec, tok, dt_out, p128, p128, p1k, p1k],
        out_shape=[jax.ShapeDtypeStruct((B, S, CONV_CH), F32), jax.ShapeDtypeStruct((B, S, SSM_W), BF16),
                   jax.ShapeDtypeStruct((B, S, 128), BF16), jax.ShapeDtypeStruct((1, 128), F32),
                   jax.ShapeDtypeStruct((1, 128), F32), jax.ShapeDtypeStruct((1, SSM_W), F32),
                   jax.ShapeDtypeStruct((1, SSM_W), F32)],
        scratch_shapes=[pltpu.VMEM((128, SSM_W), F32)],
        compiler_params=pltpu.CompilerParams(dimension_semantics=("arbitrary", "arbitrary")),
    )(xact, P, P, P, P, P, sprev, dcat, bias, alog, dskipE, ng, E, ET, tri, triT)


def _adamw(w, parts, m, v, name, tr=512, row0=0, prev=None):
    Rtot, C = w.shape
    ns, R = parts.shape[0], parts.shape[1]
    tr = min(tr, R)
    assert R % tr == 0 and row0 % tr == 0
    off = row0 // tr
    c1 = 1.0 / (1.0 - ADAM_B1 ** ADAM_STEP)
    c2 = 1.0 / (1.0 - ADAM_B2 ** ADAM_STEP)

    def body(w_ref, p_ref, m_ref, v_ref, *rest):
        g_ref, d_ref, mo_ref, vo_ref = rest[-4:]
        g = p_ref[0].astype(F32)
        for s in range(1, ns):
            g = g + p_ref[s].astype(F32)
        mn = ADAM_B1 * m_ref[...] + (1.0 - ADAM_B1) * g
        vn = ADAM_B2 * v_ref[...] + (1.0 - ADAM_B2) * (g * g)
        g_ref[...] = g
        mo_ref[...] = mn
        vo_ref[...] = vn
        d_ref[...] = -ADAM_LR * ((mn * c1) / (jnp.sqrt(vn * c2) + ADAM_EPS) + ADAM_WD * w_ref[...])

    blk = pl.BlockSpec((tr, C), lambda i: (i + off, 0))
    extra = [] if prev is None else list(prev)
    return pl.pallas_call(
        body, name=name, grid=(R // tr,),
        in_specs=[blk, pl.BlockSpec((ns, tr, C), lambda i: (0, i, 0)), blk, blk] + [pl.BlockSpec(memory_space=pl.ANY)] * len(extra),
        out_specs=[blk] * 4, out_shape=[jax.ShapeDtypeStruct((Rtot, C), F32)] * 4,
        input_output_aliases={4 + k: k for k in range(len(extra))})(w, parts, m, v, *extra)


_SMALL = ("ada_b", "norm1_g", "gm_ln_g", "gm_ln_b", "gm_ws", "gm_bs", "gm_norm_g", "attn_sinks", "attn_norm_g", "conv_b",
          "dt_bias", "a_log", "d_skip", "ssm_norm_g", "norm2_g", "final_norm_g")


def _pack(arrs):
    flat = []
    for a in arrs:
        f = a.reshape(-1).astype(F32)
        flat.append(jnp.pad(f, (0, (-f.shape[0]) % 1024)))
    return jnp.concatenate(flat).reshape(-1, 128)


def _unpack(pack, like):
    out, o = [], 0
    flat = pack.reshape(-1)
    for a in like:
        n = int(np.prod(a.shape))
        out.append(flat[o:o + n].reshape(a.shape))
        o += n + (-n) % 1024
    return out


def kernel(x, c, ada_w, ada_b, norm1_g, w_in, gm_ln_g, gm_ln_b, gm_ws, gm_bs, gm_norm_g, attn_sinks, attn_norm_g, conv_w, conv_b, dt_bias, a_log, d_skip, ssm_norm_g, w_out, norm2_g, w_mlp1, w_mlp2, final_norm_g, loss_target, m_ada_w, m_ada_b, m_norm1_g, m_w_in, m_gm_ln_g, m_gm_ln_b, m_gm_ws, m_gm_bs, m_gm_norm_g, m_attn_sinks, m_attn_norm_g, m_conv_w, m_conv_b, m_dt_bias, m_a_log, m_d_skip, m_ssm_norm_g, m_w_out, m_norm2_g, m_w_mlp1, m_w_mlp2, m_final_norm_g, v_ada_w, v_ada_b, v_norm1_g, v_w_in, v_gm_ln_g, v_gm_ln_b, v_gm_ws, v_gm_bs, v_gm_norm_g, v_attn_sinks, v_attn_norm_g, v_conv_w, v_conv_b, v_dt_bias, v_a_log, v_d_skip, v_ssm_norm_g, v_w_out, v_norm2_g, v_w_mlp1, v_w_mlp2, v_final_norm_g):
    args = dict(locals())
    B, S, _ = x.shape
    T = B * S
    L = DEPTH
    me = 4 * lax.axis_index("x") + 2 * lax.axis_index("y") + lax.axis_index("c")

    gath = _gather2([c, conv_w], "ag_c")
    big = ("w_in", "w_out", "w_mlp1", "w_mlp2")
    chain = [(n, l) for l in range(L) for n in ("w_in", "w_mlp1", "w_out", "w_mlp2")]
    inflight = {}

    def start_next(order):
        if not chain:
            return jnp.zeros((8, 128), F32)
        n, l = chain.pop(0)
        sems, land_thru, token = _gather_start(zone[n, l], order, f"ag_start_{n}{l}")
        inflight[n, l] = (sems, land_thru)
        return token

    def gathered(n, l, after):
        land = _gather_wait(*inflight.pop((n, l)), after, f"ag_wait_{n}{l}")
        return _gather_finish(land, f"ag_fin_{n}{l}")

    forwarding = {}

    def arrived(n, l, after):
        land = _gather_wait(*inflight.pop((n, l)), after, f"ag_wait_{n}{l}")
        sems, land_thru, token = _forward_start(land, after, f"ag_fwd_start_{n}{l}")
        forwarding[n, l] = (sems, land_thru)
        return token

    def ready(n, l, after):
        return _forward_wait(*forwarding.pop((n, l)), after, f"ag_fwd_wait_{n}{l}")

    me1 = me.astype(jnp.int32).reshape(1)
    zone = {(n, l): _landing_zone(args[n], l, me1, f"ag_zone_{n}{l}") for n, l in chain}
    later_zones = [zone[k] for k in chain[1:]]

    tok = start_next(gath[0])
    c_all = gath[0].reshape(NDEV * B, D) + tok[0, 0]
    c_act = (c_all * jax.nn.sigmoid(c_all)).astype(BF16)
    nb_rows = c_act.shape[0]
    c_pad = jnp.pad(c_act, ((0, 128 - nb_rows), (0, 0)))
    adw = ada_w.astype(BF16)
    mod_part = jnp.stack([_mm(c_pad, adw[l], mode="nn", name=f"mod{l}", tn=768)[:nb_rows] for l in range(L)])
    mod_all = _gather_small([mod_part], "ag_mod", order=later_zones)[0]
    mod_mine = lax.dynamic_slice_in_dim(mod_all, me * B, B, axis=2)
    mod = jnp.transpose(mod_mine, (1, 2, 0, 3)).reshape(L, B, 6 * D) + ada_b[:, None, :]
    mods = [[mod[l][:, None, i * D:(i + 1) * D] for i in range(6)] for l in range(L)]

    win_g, wout_g, w1_g, w2_g = [None] * L, [None] * L, [None] * L, [None] * L

    tril = jnp.tril(jnp.ones((128, 128), F32))
    row = lambda a: a.reshape(1, -1)
    pad128 = lambda a: jnp.pad(a.reshape(1, -1), ((0, 0), (0, 128 - a.shape[-1])))
    small = []
    for l in range(L):
        wt = gm_ws[l] * tril
        small.append(dict(
            lng=row(gm_ln_g[l]), lnb=row(gm_ln_b[l]), wt=wt.astype(BF16), wtT=jnp.swapaxes(wt, 1, 2).astype(BF16),
            bsx=jnp.repeat(gm_bs[l].T, 128, axis=1), gog=row(gm_norm_g[l]), sinks=attn_sinks[l], aog=row(attn_norm_g[l]),
            bias=pad128(dt_bias[l]), alog=pad128(a_log[l]), dskE=jnp.repeat(d_skip[l], SSM_HD).reshape(1, SSM_W),
            sng=row(ssm_norm_g[l]), cb=row(conv_b[l])))
    convw_all = jnp.transpose(gath[1], (1, 2, 0, 3)).reshape(L, 4, CONV_CH)
    convw8 = jnp.pad(convw_all, ((0, 0), (0, 4), (0, 0)))

    saved = []
    xl = x
    h = _norm_fwd(xl, row(norm1_g[0]), mods[0][1], mods[0][0], "norm1_f0")
    for l in range(L):
        sm = small[l]
        if l == 0:
            g_in = gathered("w_in", 0, h)
            tok = start_next(g_in)
        win_g[l] = _shards_to_cols(g_in, f"w_in_cols{l}")
        P = _mm(h.reshape(T, D), win_g[l], mode="nn", name=f"proj_in{l}", tn=1536, order=tok).reshape(B, S, PW)
        cat = _gmlp_fwd(P, sm["lng"], sm["lnb"], sm["wt"], sm["bsx"], sm["gog"], f"gmlp_f{l}")
        cat = _attn_fwd(P, sm["sinks"], sm["aog"], cat, f"attn_f{l}")
        xact = _conv_fwd(P, convw8[l], sm["cb"], f"conv_f{l}")
        tok = start_next(arrived("w_mlp1", l, xact))
        cat, sprev = _ssd_fwd(xact, P, sm["bias"], sm["alog"], sm["dskE"], sm["sng"] + tok[0:1, 0:1], cat, f"ssd_f{l}")
        g_out = gathered("w_out", l, cat)
        tok = start_next(g_out)
        wout_g[l] = g_out.reshape(D, D)
        mix = _mm(cat.reshape(T, D), wout_g[l], mode="nn", name=f"proj_out{l}", order=tok).reshape(B, S, D)
        x_mid, h2 = _norm_fwd(xl, row(norm2_g[l]), mods[l][4], mods[l][3], f"norm2_f{l}", resid=(mix, mods[l][2]))
        w1_g[l] = ready("w_mlp1", l, h2)
        a_act, r_act = _mm(h2.reshape(T, D), w1_g[l], mode="nn", name=f"mlp1_{l}", out_dtypes=(BF16, BF16), col_blocked_b=True,
                           epilogue=lambda acc: (acc, jnp.square(jnp.maximum(acc, 0.0))))
        g_2 = gathered("w_mlp2", l, r_act)
        tok = start_next(g_2)
        w2_g[l] = g_2.reshape(DFF, D)
        m2 = _mm(r_act, w2_g[l], mode="nn", name=f"mlp2_{l}", order=tok, tk=4096).reshape(B, S, D)
        saved.append(dict(x_in=xl, h=h, P=P, xact=xact, sprev=sprev, cat=cat, mix=mix, x_mid=x_mid, h2=h2, a=a_act, r=r_act, m2=m2))
        if l + 1 < L:
            tok = start_next(arrived("w_in", l + 1, m2))
            xl, h = _norm_fwd(x_mid, row(norm1_g[l + 1]) + tok[0, 0], mods[l + 1][1], mods[l + 1][0], f"norm1_f{l + 1}",
                              resid=(m2, mods[l][5]))
            g_in = ready("w_in", l + 1, h)

    sv = saved[L - 1]
    nb = _norm_bwd(sv["x_mid"], row(final_norm_g), "final_b", tgt=loss_target, br=sv["m2"], gate=mods[L - 1][5], x_is_prev=True)
    loss_part, g_final = nb["loss"], nb["dg"]
    dmod, gsm, gconvw = [None] * L, [None] * L, [None] * L
    core = lax.axis_index("c").astype(jnp.int32).reshape(1)
    reducing = []

    def reduce_start(n, l, sent, after):
        p, from_sib = _pair_wait(*sent[:3], after, f"rs_pair_wait_{n}{l}")
        s, land = _pair_add(p, from_sib, core, f"rs_add_{n}{l}")
        return reduce_exchange(n, l, s, land, after)

    def reduce_exchange(n, l, s, land, order):
        sems, s_thru, land_thru, token = _chipsum_start(s, land, order, f"rs_start_{n}{l}")
        reducing.append((n, l, sems, s_thru, land_thru))
        return token

    other = 1 - core

    for l in reversed(range(L)):
        sv, sm = saved[l], small[l]
        dm2, dxo, dg2 = nb["dbr"].reshape(T, D), nb["dx"], nb["dgate"]
        da = _mm(dm2, w2_g[l], mode="nt", name=f"mlp2_dx{l}", out_dtypes=(BF16,), extras=(sv["a"],),
                 epilogue=lambda acc, a: (acc * (2.0 * jnp.maximum(a.astype(F32), 0.0)),))
        h2f = sv["h2"].reshape(T, D)
        sent2 = _sibling_start(_dw_half(sv["r"], dm2, other, axis="m", name=f"mlp2_dw_sib{l}"), da, f"rs_sib_start_w_mlp2{l}")
        dh2 = _mm(da, w1_g[l], mode="nt", name=f"mlp1_dx{l}", col_blocked_b=True, order=sent2[3],
                  out_dtypes=(BF16,)).reshape(B, S, D)
        from_sib = _sibling_wait(*sent2[:3], dh2, f"rs_sib_wait_w_mlp2{l}")[1]
        sent1 = _sibling_start(_dw_half(h2f, da, other, axis="n", name=f"mlp1_dw_sib{l}", order=from_sib), da,
                               f"rs_sib_start_w_mlp1{l}")
        s2, land2 = _dw_half(sv["r"], dm2, core, axis="m", name=f"mlp2_dw_own{l}", add=from_sib, order=sent1[3])
        tok = reduce_exchange("w_mlp2", l, s2, land2, da)
        nb2 = _norm_bwd(sv["x_mid"], row(norm2_g[l]) + tok[0, 0], f"norm2_b{l}", sc=mods[l][4], dh=dh2, dres=dxo, br=sv["mix"],
                        gate=mods[l][2])
        dmix = nb2["dbr"].reshape(T, D)
        from_sib = _sibling_wait(*sent1[:3], dmix, f"rs_sib_wait_w_mlp1{l}")[1]
        s1, land1 = _dw_half(h2f, da, core, axis="n", name=f"mlp1_dw_own{l}", add=from_sib)
        tok = reduce_exchange("w_mlp1", l, s1, land1, dmix)
        dcat = _mm(dmix, wout_g[l], mode="nt", name=f"proj_out_dx{l}", order=tok, out_dtypes=(BF16,)).reshape(B, S, D)
        du, dv, dlng, dlnb, dws, dbsx, dgog = _gmlp_bwd(sv["P"], dcat, sm["lng"], sm["lnb"], sm["wt"], sm["wtT"], sm["bsx"],
                                                        sm["gog"], f"gmlp_b{l}")
        dq, dk, dvv, dsink, daog = _attn_bwd(sv["P"], dcat, sm["sinks"], sm["aog"], f"attn_b{l}")
        dwo = _mm(sv["cat"].reshape(T, D), dmix, mode="tn", name=f"proj_out_dw{l}", out_dtypes=(BF16,), tk=2048,
                  order=dq).reshape(4, 2, D // NDEV, D)
        sent = _pair_start(dwo, dmix, f"rs_pair_start_w_out{l}")
        dxa, dz, ddt, dbias, dalog, ddsk, dsng = _ssd_bwd(sv["xact"], sv["P"], sv["sprev"], dcat, sm["bias"], sm["alog"],
                                                          sm["dskE"], sm["sng"] + sent[3][0:1, 0:1], f"ssd_b{l}")
        tok = reduce_start("w_out", l, sent, dxa)
        dxbc, dcw, dcb = _conv_bwd(sv["P"], dxa, convw8[l], sm["cb"] + tok[0:1, 0:1], f"conv_b{l}")
        dP = jnp.concatenate([du, dv, dq, dk, dvv, dz, dxbc, ddt, jnp.zeros((B, S, PW - OFF["dt"] - 128), BF16)],
                             axis=-1).reshape(T, PW)
        dwin = _mm(sv["h"].reshape(T, D), dP, mode="tn", name=f"proj_in_dw{l}", out_dtypes=(BF16,), tn=1536, tk=2048)
        sent = _sibling_start(dwin, dP, f"rs_sib_start_w_in{l}")
        dh = _mm(dP, win_g[l], mode="nt", name=f"proj_in_dx{l}", tk=2304, order=sent[3], out_dtypes=(BF16,)).reshape(B, S, D)
        s_in, land_in = _cols_to_my_shards(*_sibling_wait(*sent[:3], dh, f"rs_sib_wait_w_in{l}"), core, f"w_in_dshards{l}")
        tok = reduce_exchange("w_in", l, s_in, land_in, dh)
        nb = _norm_bwd(sv["x_in"], row(norm1_g[l]) + tok[0, 0], f"norm1_b{l}", sc=mods[l][1], dh=dh, dres=nb2["dx"],
                       br=saved[l - 1]["m2"] if l > 0 else None, gate=mods[l - 1][5] if l > 0 else None)
        dmod[l] = jnp.concatenate([nb["dsh"], nb["dsc"], nb2["dgate"], nb2["dsh"], nb2["dsc"], dg2], axis=-1)
        gconvw[l] = dcw[:4]
        gsm[l] = dict(
            ada_b=jnp.sum(dmod[l], axis=(0, 1)), norm1_g=nb["dg"], gm_ln_g=dlng, gm_ln_b=dlnb, gm_ws=dws,
            gm_bs=dbsx.reshape(128, GM_H, 128).sum(-1).T, gm_norm_g=dgog, attn_sinks=dsink[:, 0], attn_norm_g=daog,
            conv_b=dcb, dt_bias=dbias[0, :SSM_H], a_log=dalog[0, :SSM_H], d_skip=ddsk.reshape(SSM_H, SSM_HD).sum(-1),
            ssm_norm_g=dsng, norm2_g=nb2["dg"])
    grad_x = nb["dx"]

    big_res, after = dict.fromkeys(big), grad_x
    tile_rows = dict(w_in=256, w_out=256, w_mlp1=256, w_mlp2=128)

    def finish_reduce(n, l, sems, s_thru, land_thru, after):
        parts = _chipsum_wait(sems, s_thru, land_thru, after, f"rs_wait_{n}{l}")
        w = args[n]
        big_res[n] = _adamw(w.reshape(-1, w.shape[-1]), parts, args["m_" + n].reshape(-1, w.shape[-1]),
                            args["v_" + n].reshape(-1, w.shape[-1]), f"adamw_{n}{l}", tr=tile_rows[n], row0=l * w.shape[1],
                            prev=big_res[n])
        return big_res[n][0]

    per_layer = [n for n in _SMALL if n != "final_norm_g"]
    g_small = [jnp.stack([gsm[l][n].reshape(args[n].shape[1:]) for l in range(L)]) for n in per_layer] + [g_final.reshape(D)]
    zc = jnp.zeros((L, 4, CONV_CH), F32)
    z1 = jnp.zeros((1, 128), F32)
    gpack = _pack([loss_part] + g_small + [jnp.stack(gconvw)])
    small_zones = [_landing_zone(jnp.stack(dmod).reshape(1, L * B, 6 * D), 0, me1, "ag_zone_dmod", dtype=F32),
                   _landing_zone(gpack[None], 0, me1, "ag_zone_small", tr=gpack.shape[0], dtype=F32)]
    small_sems, small_thru, after = _gather_small_start(small_zones, grad_x, "ag_small_start")

    for item in reducing[:-1]:
        after = finish_reduce(*item, after)

    got = _gather_small_wait(small_sems, small_thru, after, "ag_small_wait")
    got = [got[0].reshape(NDEV, L, B, 6 * D), got[1]]
    like = [z1] + [args[n] for n in _SMALL] + [zc]
    packs = [_pack([z1] + [args[p + n] for n in _SMALL] + [zc]) for p in ("", "m_", "v_")]
    sres = [_unpack(p, like) for p in _adamw(packs[0], got[1], packs[1], packs[2], "adamw_small", tr=gpack.shape[0])]
    res = {n: [r[1 + i] for r in sres] for i, n in enumerate(_SMALL)}
    loss = sres[0][0][0, 0]
    gcw = lax.dynamic_slice_in_dim(sres[0][-1], me * (CONV_CH // NDEV), CONV_CH // NDEV, axis=2)

    def update(name, parts, tr):
        w = args[name]
        r = _adamw(w.reshape(-1, w.shape[-1]), parts, args["m_" + name].reshape(-1, w.shape[-1]),
                   args["v_" + name].reshape(-1, w.shape[-1]), "adamw_" + name, tr=tr)
        res[name] = [a.reshape(w.shape) for a in r]

    update("conv_w", gcw.reshape(1, L * 4, CONV_CH // NDEV), L * 4)

    dmod_all = jnp.transpose(got[0], (1, 0, 2, 3)).reshape(L, NDEV * B, 6 * D)
    dm_mine = lax.dynamic_slice_in_dim(dmod_all, me * (6 * D // NDEV), 6 * D // NDEV, axis=2)
    dm_pad = jnp.pad(dm_mine, ((0, 0), (0, 128 - nb_rows), (0, 0))).astype(BF16)
    g_adaw = jnp.stack([_mm(c_pad, dm_pad[l], mode="tn", name=f"ada_dw{l}", tn=768) for l in range(L)])
    update("ada_w", g_adaw.reshape(1, L * D, 6 * D // NDEV), 256)

    finish_reduce(*reducing[-1], res["ada_w"][0])
    for n in big:
        res[n] = [a.reshape(args[n].shape) for a in big_res[n]]

    names = ['ada_w', 'ada_b', 'norm1_g', 'w_in', 'gm_ln_g', 'gm_ln_b', 'gm_ws', 'gm_bs', 'gm_norm_g', 'attn_sinks',
             'attn_norm_g', 'conv_w', 'conv_b', 'dt_bias', 'a_log', 'd_skip', 'ssm_norm_g', 'w_out', 'norm2_g', 'w_mlp1',
             'w_mlp2', 'final_norm_g']
    return (loss, grad_x, *[res[n][0] for n in names], *[res[n][1] for n in names], *[res[n][2] for n in names],
            *[res[n][3] for n in names])
```

```python
import functools

import jax
import jax.numpy as jnp
import numpy as np
from jax import lax
from jax.experimental import pallas as pl
from jax.experimental.pallas import tpu as pltpu

F32, BF16 = jnp.float32, jnp.bfloat16
HI = lax.Precision.HIGHEST
MESH = pl.DeviceIdType.MESH
NDEV = 8

D = 2048
DEPTH = 2
CHUNK = 128
GM_W, GM_H = 512, 4
ATT_W, KV_W, ATT_H = 512, 128, 8
SSM_W, SSM_H, SSM_HD, SSM_G = 1024, 16, 64, 2
CONV_CH = 1536
IN_W = 4368
DFF = 8192
EPS = 1e-6
NEG_INF = -1e30
GELU_K = 0.7978845608028654
GELU_C = 0.044715

_ORIG = (("u", 512), ("v", 512), ("q", 512), ("k", 128), ("vv", 128), ("z", 1024), ("xbc", 1536), ("dt", 16))
OFF = dict(u=0, v=512, q=1024, k=1536, vv=1664, z=1792, xbc=2816, dt=4352)
PW = 4608

ADAM_LR, ADAM_B1, ADAM_B2, ADAM_EPS, ADAM_WD, ADAM_STEP = 0.001, 0.9, 0.999, 1e-08, 0.01, 10


def _shards_to_cols(g, name, tr=256):
    n, R, C = g.shape

    def body(g_ref, o_ref):
        o_ref[...] = jnp.concatenate([g_ref[s] for s in range(n)] + [jnp.zeros((tr, PW - n * C), g.dtype)], axis=1)

    return pl.pallas_call(body, name=name, grid=(R // tr,), in_specs=[pl.BlockSpec((n, tr, C), lambda i: (0, i, 0))],
                          out_specs=pl.BlockSpec((tr, PW), lambda i: (i, 0)), out_shape=jax.ShapeDtypeStruct((R, PW), g.dtype))(g)


def _cols_to_my_shards(w, w_sib, core, name, tr=256):
    R, C = w.shape[0], IN_W // NDEV

    def body(core_ref, w_ref, s_ref, o_ref, o2_ref):
        x = w_ref[...].astype(F32) + s_ref[...].astype(F32)
        mine_is_odd = core_ref[0] == 1
        for q in range(4):
            blk = jnp.where(mine_is_odd, x[:, C * (2 * q + 1):C * (2 * q + 2)], x[:, C * 2 * q:C * (2 * q + 1)]).astype(o_ref.dtype)
            o_ref[q] = blk
            o2_ref[q] = blk

    row = pl.BlockSpec((tr, PW), lambda i, c: (i, 0))
    out = pl.BlockSpec((4, tr, C), lambda i, c: (0, i, 0))
    return pl.pallas_call(
        body, name=name, out_shape=[jax.ShapeDtypeStruct((4, R, C), w.dtype)] * 2,
        grid_spec=pltpu.PrefetchScalarGridSpec(num_scalar_prefetch=1, grid=(R // tr,), in_specs=[row, row], out_specs=[out, out]),
    )(core, w, w_sib)


def _sigmoid(x):
    return 0.5 * (jnp.tanh(0.5 * x) + 1.0)


def _gelu(x):
    return 0.5 * x * (1.0 + jnp.tanh(GELU_K * (x + GELU_C * x * x * x)))


def _gelu_grad(x):
    t = jnp.tanh(GELU_K * (x + GELU_C * x * x * x))
    return 0.5 * (1.0 + t) + 0.5 * x * (1.0 - t * t) * GELU_K * (1.0 + 3.0 * GELU_C * x * x)


def _dot(a, b, prec=None):
    return jnp.dot(a, b, precision=prec, preferred_element_type=F32)


def _dot_nt(a, b, prec=None):
    return lax.dot_general(a, b, (((1,), (1,)), ((), ())), precision=prec, preferred_element_type=F32)


def _dot_tn(a, b, prec=None):
    return lax.dot_general(a, b, (((0,), (0,)), ((), ())), precision=prec, preferred_element_type=F32)


def _full(shape):
    return pl.BlockSpec(shape, lambda *_: (0,) * len(shape))


_HBM = pl.BlockSpec(memory_space=pltpu.HBM)


def _me():
    return lax.axis_index("x"), lax.axis_index("y"), lax.axis_index("c")


def _peer(k):
    x, y, c = _me()
    px = 1 - x if k & 4 else x
    py = 1 - y if k & 2 else y
    pc = 1 - c if k & 1 else c
    return (px, py, pc), 4 * px + 2 * py + pc


def _gather_small(xs, name, order=()):
    n = len(xs)

    def body(*refs):
        ins, outs = refs[:n], refs[-n - 3:-3]
        send, recv, loc = refs[-3:]
        x, y, c = _me()
        me = 4 * x + 2 * y + c
        started = []
        for i in range(n):
            own = pltpu.make_async_copy(ins[i], outs[i].at[me], loc.at[i])
            own.start()
            started.append(own)
        for k in range(1, NDEV):
            dev, lin = _peer(k)
            for i in range(n):
                pltpu.make_async_remote_copy(
                    src_ref=ins[i], dst_ref=outs[i].at[me],
                    send_sem=send.at[i, k - 1], recv_sem=recv.at[i, k - 1], device_id=dev, device_id_type=MESH).start()
        for k in range(1, NDEV):
            dev, lin = _peer(k)
            for i in range(n):
                pltpu.make_async_remote_copy(
                    src_ref=ins[i], dst_ref=outs[i].at[lin],
                    send_sem=send.at[i, k - 1], recv_sem=recv.at[i, k - 1], device_id=dev, device_id_type=MESH).wait()
        for own in started:
            own.wait()

    extra = list(order)
    return pl.pallas_call(
        body, name=name, out_shape=[jax.ShapeDtypeStruct((NDEV,) + a.shape, a.dtype) for a in xs],
        in_specs=[_HBM] * n + [pl.BlockSpec(memory_space=pl.ANY)] * len(extra), out_specs=[_HBM] * n,
        scratch_shapes=[pltpu.SemaphoreType.DMA((n, NDEV - 1)), pltpu.SemaphoreType.DMA((n, NDEV - 1)),
                        pltpu.SemaphoreType.DMA((n,))],
        compiler_params=pltpu.CompilerParams(has_side_effects=True),
    )(*xs, *extra)


def _gather_small_start(lands, order, name):
    n = len(lands)

    def body(*refs):
        ins, sems, token = refs[:n], refs[n + 1:n + 1 + 14 * n], refs[-1]
        x, y, c = _me()
        me = 4 * x + 2 * y + c
        for i in range(n):
            for k in range(1, NDEV):
                dev, _ = _peer(k)
                pltpu.make_async_remote_copy(src_ref=ins[i].at[me], dst_ref=ins[i].at[me], send_sem=sems[14 * i + k - 1],
                                             recv_sem=sems[14 * i + 7 + k - 1], device_id=dev, device_id_type=MESH).start()
        token[...] = jnp.zeros_like(token)

    outs = pl.pallas_call(
        body, name=name,
        out_shape=(pltpu.SemaphoreType.DMA(()),) * (14 * n) + tuple(pltpu.HBM(a.shape, a.dtype) for a in lands)
        + (jax.ShapeDtypeStruct((8, 128), F32),),
        in_specs=(_HBM,) * n + (_ANY,), out_specs=(_SEM,) * (14 * n) + (_HBM,) * n + (pl.BlockSpec(memory_space=pltpu.VMEM),),
        input_output_aliases={i: 14 * n + i for i in range(n)}, compiler_params=pltpu.CompilerParams(has_side_effects=_DATAFLOW),
    )(*[_hbm(a) for a in lands], order)
    return outs[:14 * n], outs[14 * n:15 * n], outs[-1]


def _gather_small_wait(sems, lands_thru, after, name):
    n = len(lands_thru)

    def body(*refs):
        ins, sems_ = refs[:n], refs[n:n + 14 * n]
        x, y, c = _me()
        me = 4 * x + 2 * y + c
        for i in range(n):
            for k in range(1, NDEV):
                dev, lin = _peer(k)
                cp = pltpu.make_async_remote_copy(src_ref=ins[i].at[me], dst_ref=ins[i].at[lin], send_sem=sems_[14 * i + k - 1],
                                                  recv_sem=sems_[14 * i + 7 + k - 1], device_id=dev, device_id_type=MESH)
                cp.wait_send()
                cp.wait_recv()

    outs = pl.pallas_call(
        body, name=name, out_shape=tuple(pltpu.HBM(a.shape, a.dtype) for a in lands_thru),
        in_specs=(_HBM,) * n + (_SEM,) * (14 * n) + (_ANY,), out_specs=(_HBM,) * n,
        input_output_aliases={i: i for i in range(n)}, compiler_params=pltpu.CompilerParams(has_side_effects=_DATAFLOW),
    )(*lands_thru, *sems, after)
    return outs


def _chips():
    x, y, c = _me()
    return x, y, c, [(1 - x, y), (x, 1 - y), (1 - x, 1 - y)]


def _gather2(xs, name, order=None):
    n = len(xs)
    extra = [] if order is None else [order]

    def body(*refs):
        ins, outs = refs[:n], refs[-n - 3:-3]
        send, recv, loc = refs[-3:]
        x, y, c, chips = _chips()
        me, sib = (x, y, c), (x, y, 1 - c)

        def cp(i, k, block, to, src=None):
            slot = outs[i].at[4 * block[0] + 2 * block[1] + block[2]]
            return pltpu.make_async_remote_copy(src_ref=slot if src is None else src, dst_ref=slot, send_sem=send.at[i, k],
                                                recv_sem=recv.at[i, k], device_id=to, device_id_type=MESH)

        sent = []
        for i in range(n):
            for j, chip in enumerate(chips):
                sent.append(cp(i, 1 + j, me, (*chip, c), src=ins[i]))
            sent.append(cp(i, 0, me, sib, src=ins[i]))
        for s in sent:
            s.start()
        own = [pltpu.make_async_copy(ins[i], outs[i].at[4 * x + 2 * y + c], loc.at[i]) for i in range(n)]
        for o in own:
            o.start()
        for j, chip in enumerate(chips):
            for i in range(n):
                cp(i, 1 + j, (*chip, c), me).wait_recv()
                fwd = cp(i, 4 + j, (*chip, c), sib)
                fwd.start()
                sent.append(fwd)
        for i in range(n):
            cp(i, 0, sib, me).wait_recv()
            for j, chip in enumerate(chips):
                cp(i, 4 + j, (*chip, 1 - c), me).wait_recv()
        for s in sent:
            s.wait_send()
        for o in own:
            o.wait()

    return pl.pallas_call(
        body, name=name, out_shape=[jax.ShapeDtypeStruct((NDEV,) + a.shape, a.dtype) for a in xs],
        in_specs=[_HBM] * n + [pl.BlockSpec(memory_space=pl.ANY)] * len(extra), out_specs=[_HBM] * n,
        scratch_shapes=[pltpu.SemaphoreType.DMA((n, 7)), pltpu.SemaphoreType.DMA((n, 7)), pltpu.SemaphoreType.DMA((n,))],
        compiler_params=pltpu.CompilerParams(has_side_effects=True),
    )(*xs, *extra)


def _pair_add(p, r1, core, name, tr=256):
    _, _, R, C = p.shape
    tr = min(tr, R)

    def body(core_ref, p_ref, r_ref, o_ref, o2_ref):
        s = (p_ref[...].astype(F32) + r_ref[...].astype(F32)).astype(o_ref.dtype)
        o_ref[...] = s
        o2_ref[...] = s

    blk = pl.BlockSpec((None, tr, C), lambda ch, i, core_ref: (ch, i, 0))
    return pl.pallas_call(
        body, name=name, out_shape=[jax.ShapeDtypeStruct((4, R, C), p.dtype)] * 2,
        grid_spec=pltpu.PrefetchScalarGridSpec(
            num_scalar_prefetch=1, grid=(4, R // tr),
            in_specs=[pl.BlockSpec((None, None, tr, C), lambda ch, i, core_ref: (ch, core_ref[0], i, 0)), blk],
            out_specs=[blk, blk]),
    )(core, p, r1)


_SEM = pl.BlockSpec(memory_space=pltpu.SEMAPHORE)
_ANY = pl.BlockSpec(memory_space=pl.ANY)
_DATAFLOW = pltpu.SideEffectType.DATAFLOW_SIDE_EFFECTING


def _hbm(a):
    return pltpu.with_memory_space_constraint(a, pltpu.HBM)


def _gather_targets():
    x, y, c, chips = _chips()
    return 4 * x + 2 * y + c, [(x, y, 1 - c)] + [(*chip, c) for chip in chips]


def _landing_zone(w, l, me, name, tr=512, dtype=BF16):
    _, R, C = w.shape
    tr = min(tr, R)

    def body(me_ref, w_ref, o_ref):
        o_ref[...] = w_ref[...].astype(dtype)

    return pl.pallas_call(
        body, name=name, out_shape=jax.ShapeDtypeStruct((NDEV, R, C), dtype),
        grid_spec=pltpu.PrefetchScalarGridSpec(
            num_scalar_prefetch=1, grid=(R // tr,), in_specs=[pl.BlockSpec((None, tr, C), lambda i, me_ref: (l, i, 0))],
            out_specs=pl.BlockSpec((None, tr, C), lambda i, me_ref: (me_ref[0], i, 0))),
    )(me, w)


def _gather_start(land, order, name):
    def body(land_ref, order_ref, *rest):
        sems, token = rest[:8], rest[9]
        me, targets = _gather_targets()
        for k, to in enumerate(targets):
            pltpu.make_async_remote_copy(src_ref=land_ref.at[me], dst_ref=land_ref.at[me], send_sem=sems[k],
                                         recv_sem=sems[4 + k], device_id=to, device_id_type=MESH).start()
        token[...] = jnp.zeros_like(token)

    outs = pl.pallas_call(
        body, name=name,
        out_shape=(pltpu.SemaphoreType.DMA(()),) * 8 + (pltpu.HBM(land.shape, land.dtype), jax.ShapeDtypeStruct((8, 128), F32)),
        in_specs=(_HBM, _ANY), out_specs=(_SEM,) * 8 + (_HBM, pl.BlockSpec(memory_space=pltpu.VMEM)),
        input_output_aliases={0: 8}, compiler_params=pltpu.CompilerParams(has_side_effects=_DATAFLOW),
    )(_hbm(land), order)
    return outs[:8], outs[8], outs[9]


def _gather_wait(sems, land_thru, after, name):
    def body(land_ref, *rest):
        sems_ = rest[:8]
        me, targets = _gather_targets()
        for k, to in enumerate(targets):
            cp = pltpu.make_async_remote_copy(src_ref=land_ref.at[me], dst_ref=land_ref.at[me], send_sem=sems_[k],
                                              recv_sem=sems_[4 + k], device_id=to, device_id_type=MESH)
            cp.wait_send()
            cp.wait_recv()

    return pl.pallas_call(
        body, name=name, out_shape=pltpu.HBM(land_thru.shape, land_thru.dtype),
        in_specs=(_HBM,) + (_SEM,) * 8 + (_ANY,), out_specs=_HBM, input_output_aliases={0: 0},
        compiler_params=pltpu.CompilerParams(has_side_effects=_DATAFLOW),
    )(land_thru, *sems, after)


def _gather_finish(land, name):
    def body(land_ref, out, send, recv):
        x, y, c, chips = _chips()
        fwd = [pltpu.make_async_remote_copy(src_ref=out.at[4 * px + 2 * py + c], dst_ref=out.at[4 * px + 2 * py + c],
                                            send_sem=send.at[j], recv_sem=recv.at[j], device_id=(x, y, 1 - c), device_id_type=MESH)
               for j, (px, py) in enumerate(chips)]
        for cp in fwd:
            cp.start()
        for j, (px, py) in enumerate(chips):
            slot = out.at[4 * px + 2 * py + 1 - c]
            pltpu.make_async_remote_copy(src_ref=slot, dst_ref=slot, send_sem=send.at[j], recv_sem=recv.at[j],
                                         device_id=(x, y, 1 - c), device_id_type=MESH).wait()

    return pl.pallas_call(
        body, name=name, out_shape=jax.ShapeDtypeStruct(land.shape, land.dtype),
        in_specs=[_HBM], out_specs=_HBM, input_output_aliases={0: 0},
        scratch_shapes=[pltpu.SemaphoreType.DMA((3,)), pltpu.SemaphoreType.DMA((3,))],
        compiler_params=pltpu.CompilerParams(has_side_effects=True),
    )(land)


def _forward_start(land, order, name):
    def body(land_ref, order_ref, *rest):
        sems, token = rest[:6], rest[7]
        x, y, c, chips = _chips()
        for j, (px, py) in enumerate(chips):
            slot = land_ref.at[4 * px + 2 * py + c]
            pltpu.make_async_remote_copy(src_ref=slot, dst_ref=slot, send_sem=sems[j], recv_sem=sems[3 + j],
                                         device_id=(x, y, 1 - c), device_id_type=MESH).start()
        token[...] = jnp.zeros_like(token)

    outs = pl.pallas_call(
        body, name=name,
        out_shape=(pltpu.SemaphoreType.DMA(()),) * 6 + (pltpu.HBM(land.shape, land.dtype), jax.ShapeDtypeStruct((8, 128), F32)),
        in_specs=(_HBM, _ANY), out_specs=(_SEM,) * 6 + (_HBM, pl.BlockSpec(memory_space=pltpu.VMEM)),
        input_output_aliases={0: 6}, compiler_params=pltpu.CompilerParams(has_side_effects=_DATAFLOW),
    )(_hbm(land), order)
    return outs[:6], outs[6], outs[7]


def _forward_wait(sems, land_thru, after, name):
    def body(land_ref, *rest):
        sems_ = rest[:6]
        x, y, c, chips = _chips()
        for j, (px, py) in enumerate(chips):
            cp = pltpu.make_async_remote_copy(src_ref=land_ref.at[4 * px + 2 * py + c], dst_ref=land_ref.at[4 * px + 2 * py + 1 - c],
                                              send_sem=sems_[j], recv_sem=sems_[3 + j], device_id=(x, y, 1 - c),
                                              device_id_type=MESH)
            cp.wait_send()
            cp.wait_recv()

    return pl.pallas_call(
        body, name=name, out_shape=pltpu.HBM(land_thru.shape, land_thru.dtype),
        in_specs=(_HBM,) + (_SEM,) * 6 + (_ANY,), out_specs=_HBM, input_output_aliases={0: 0},
        compiler_params=pltpu.CompilerParams(has_side_effects=_DATAFLOW),
    )(land_thru, *sems, after)


def _chip_targets():
    x, y, c, chips = _chips()
    return 2 * x + y, [((px, py, c), 2 * px + py) for px, py in chips]


def _chipsum_start(s, land, order, name):
    def body(s_ref, land_ref, order_ref, *rest):
        sems, token = rest[:6], rest[8]
        mine, targets = _chip_targets()
        for k, (to, ch) in enumerate(targets):
            pltpu.make_async_remote_copy(src_ref=s_ref.at[ch], dst_ref=land_ref.at[mine], send_sem=sems[k], recv_sem=sems[3 + k],
                                         device_id=to, device_id_type=MESH).start()
        token[...] = jnp.zeros_like(token)

    outs = pl.pallas_call(
        body, name=name,
        out_shape=(pltpu.SemaphoreType.DMA(()),) * 6 + (pltpu.HBM(s.shape, s.dtype), pltpu.HBM(land.shape, land.dtype),
                                                        jax.ShapeDtypeStruct((8, 128), F32)),
        in_specs=(_HBM, _HBM, _ANY), out_specs=(_SEM,) * 6 + (_HBM, _HBM, pl.BlockSpec(memory_space=pltpu.VMEM)),
        input_output_aliases={0: 6, 1: 7}, compiler_params=pltpu.CompilerParams(has_side_effects=_DATAFLOW),
    )(_hbm(s), _hbm(land), order)
    return outs[:6], outs[6], outs[7], outs[8]


def _chipsum_wait(sems, s_thru, land_thru, after, name):
    def body(s_ref, land_ref, *rest):
        sems_ = rest[:6]
        mine, targets = _chip_targets()
        for k, (to, ch) in enumerate(targets):
            cp = pltpu.make_async_remote_copy(src_ref=s_ref.at[ch], dst_ref=land_ref.at[ch], send_sem=sems_[k], recv_sem=sems_[3 + k],
                                              device_id=to, device_id_type=MESH)
            cp.wait_send()
            cp.wait_recv()

    return pl.pallas_call(
        body, name=name, out_shape=(pltpu.HBM(s_thru.shape, s_thru.dtype), pltpu.HBM(land_thru.shape, land_thru.dtype)),
        in_specs=(_HBM, _HBM) + (_SEM,) * 6 + (_ANY,), out_specs=(_HBM, _HBM), input_output_aliases={0: 0, 1: 1},
        compiler_params=pltpu.CompilerParams(has_side_effects=_DATAFLOW),
    )(s_thru, land_thru, *sems, after)[1]


def _pair_start(p, order, name):
    def body(p_ref, land_ref, order_ref, *rest):
        sems, token = rest[:8], rest[10]
        x, y, c = _me()
        for ch in range(4):
            pltpu.make_async_remote_copy(src_ref=p_ref.at[ch, 1 - c], dst_ref=land_ref.at[ch], send_sem=sems[ch],
                                         recv_sem=sems[4 + ch], device_id=(x, y, 1 - c), device_id_type=MESH).start()
        token[...] = jnp.zeros_like(token)

    land = lax.empty((4,) + p.shape[2:], p.dtype)
    outs = pl.pallas_call(
        body, name=name,
        out_shape=(pltpu.SemaphoreType.DMA(()),) * 8 + (pltpu.HBM(p.shape, p.dtype), pltpu.HBM(land.shape, land.dtype),
                                                        jax.ShapeDtypeStruct((8, 128), F32)),
        in_specs=(_HBM, _HBM, _ANY), out_specs=(_SEM,) * 8 + (_HBM, _HBM, pl.BlockSpec(memory_space=pltpu.VMEM)),
        input_output_aliases={0: 8, 1: 9}, compiler_params=pltpu.CompilerParams(has_side_effects=_DATAFLOW),
    )(_hbm(p), _hbm(land), order)
    return outs[:8], outs[8], outs[9], outs[10]


def _pair_wait(sems, p_thru, land_thru, after, name):
    def body(p_ref, land_ref, *rest):
        sems_ = rest[:8]
        x, y, c = _me()
        for ch in range(4):
            cp = pltpu.make_async_remote_copy(src_ref=p_ref.at[ch, 1 - c], dst_ref=land_ref.at[ch], send_sem=sems_[ch],
                                              recv_sem=sems_[4 + ch], device_id=(x, y, 1 - c), device_id_type=MESH)
            cp.wait_send()
            cp.wait_recv()

    return pl.pallas_call(
        body, name=name, out_shape=(pltpu.HBM(p_thru.shape, p_thru.dtype), pltpu.HBM(land_thru.shape, land_thru.dtype)),
        in_specs=(_HBM, _HBM) + (_SEM,) * 8 + (_ANY,), out_specs=(_HBM, _HBM), input_output_aliases={0: 0, 1: 1},
        compiler_params=pltpu.CompilerParams(has_side_effects=_DATAFLOW),
    )(p_thru, land_thru, *sems, after)


def _sibling_start(p, order, name):
    def body(p_ref, land_ref, order_ref, send_sem, recv_sem, p_thru, land_thru, token):
        x, y, c = _me()
        pltpu.make_async_remote_copy(src_ref=p_ref, dst_ref=land_ref, send_sem=send_sem, recv_sem=recv_sem,
                                     device_id=(x, y, 1 - c), device_id_type=MESH).start()
        token[...] = jnp.zeros_like(token)

    land = lax.empty(p.shape, p.dtype)
    outs = pl.pallas_call(
        body, name=name,
        out_shape=(pltpu.SemaphoreType.DMA(()),) * 2 + (pltpu.HBM(p.shape, p.dtype), pltpu.HBM(p.shape, p.dtype),
                                                        jax.ShapeDtypeStruct((8, 128), F32)),
        in_specs=(_HBM, _HBM, _ANY), out_specs=(_SEM,) * 2 + (_HBM, _HBM, pl.BlockSpec(memory_space=pltpu.VMEM)),
        input_output_aliases={0: 2, 1: 3}, compiler_params=pltpu.CompilerParams(has_side_effects=_DATAFLOW),
    )(_hbm(p), _hbm(land), order)
    return outs[:2], outs[2], outs[3], outs[4]


def _sibling_wait(sems, p_thru, land_thru, after, name):
    def body(p_ref, land_ref, send_sem, recv_sem, after_ref, p_dead, got_ref):
        x, y, c = _me()
        cp = pltpu.make_async_remote_copy(src_ref=p_ref, dst_ref=land_ref, send_sem=send_sem, recv_sem=recv_sem,
                                          device_id=(x, y, 1 - c), device_id_type=MESH)
        cp.wait_send()
        cp.wait_recv()

    return pl.pallas_call(
        body, name=name, out_shape=(pltpu.HBM(p_thru.shape, p_thru.dtype), pltpu.HBM(land_thru.shape, land_thru.dtype)),
        in_specs=(_HBM, _HBM, _SEM, _SEM, _ANY), out_specs=(_HBM, _HBM), input_output_aliases={0: 0, 1: 1},
        compiler_params=pltpu.CompilerParams(has_side_effects=_DATAFLOW),
    )(p_thru, land_thru, *sems, after)


def _mm(a, b, *, mode, name, out_dtypes=(F32,), epilogue=None, extras=(), tm=1024, tn=1024, tk=2048,
        col_blocked_b=False, col_blocked_out=False, order=None):
    CB = 1024
    if col_blocked_b:
        assert mode in ("nn", "nt") and b.shape[2] == CB
        (M, K), N = a.shape, (b.shape[0] * CB if mode == "nn" else b.shape[1])
        assert mode == "nn" or tk % CB == 0
        tn = CB if mode == "nn" else tn
    elif mode == "nn":
        (M, K), N = a.shape, b.shape[1]
    elif mode == "nt":
        (M, K), N = a.shape, b.shape[0]
    else:
        (K, M), N = a.shape, b.shape[1]
    if col_blocked_out:
        assert len(out_dtypes) == 1 and N % CB == 0
        tn = CB
    tm, tn, tk = min(tm, M), min(tn, N), min(tk, K)
    assert M % tm == 0 and N % tn == 0 and K % tk == 0, (M, N, K, tm, tn, tk)
    nk = K // tk
    ne, no = len(extras), len(out_dtypes)
    dims = {"nn": (((1,), (0,)), ((), ())), "nt": (((1,), (1,)), ((), ())), "tn": (((0,), (0,)), ((), ()))}[mode]

    no_ = 0 if order is None else 1

    def body(a_ref, b_ref, *rest):
        rest = rest[no_:]
        ex, outs = rest[:ne], rest[ne:ne + no]

        def finish(acc):
            res = epilogue(acc, *[e[...] for e in ex]) if epilogue is not None else (acc,)
            for o, r in zip(outs, res):
                o[...] = r.astype(o.dtype)

        if col_blocked_b and mode == "nt":
            part = sum(lax.dot_general(a_ref[:, q * CB:(q + 1) * CB], b_ref[q], dims, preferred_element_type=F32)
                       for q in range(tk // CB))
        else:
            part = lax.dot_general(a_ref[...], b_ref[...], dims, preferred_element_type=F32)
        if nk == 1:
            finish(part)
        else:
            acc_ref = rest[-1]
            k = pl.program_id(2)

            @pl.when(k == 0)
            def _():
                acc_ref[...] = part

            @pl.when(k > 0)
            def _():
                acc_ref[...] += part

            @pl.when(k == nk - 1)
            def _():
                finish(acc_ref[...])

    a_spec = {"nn": pl.BlockSpec((tm, tk), lambda i, j, k: (i, k)), "nt": pl.BlockSpec((tm, tk), lambda i, j, k: (i, k)),
              "tn": pl.BlockSpec((tk, tm), lambda i, j, k: (k, i))}[mode]
    b_spec = {"nn": pl.BlockSpec((tk, tn), lambda i, j, k: (k, j)), "nt": pl.BlockSpec((tn, tk), lambda i, j, k: (j, k)),
              "tn": pl.BlockSpec((tk, tn), lambda i, j, k: (k, j))}[mode]
    if col_blocked_b:
        b_spec = (pl.BlockSpec((None, tk, CB), lambda i, j, k: (j, k, 0)) if mode == "nn"
                  else pl.BlockSpec((tk // CB, tn, CB), lambda i, j, k: (k, j, 0)))
    e_spec = pl.BlockSpec((tm, tn), lambda i, j, k: (i, j))
    o_spec, o_dims = e_spec, (M, N)
    if col_blocked_out:
        o_spec, o_dims = pl.BlockSpec((None, tm, CB), lambda i, j, k: (j, i, 0)), (N // CB, M, CB)
    outs = pl.pallas_call(
        body, name=name, grid=(M // tm, N // tn, nk),
        in_specs=[a_spec, b_spec] + [_ANY] * no_ + [e_spec] * ne, out_specs=[o_spec] * no,
        out_shape=[jax.ShapeDtypeStruct(o_dims, dt) for dt in out_dtypes],
        scratch_shapes=[pltpu.VMEM((tm, tn), F32)] if nk > 1 else [],
        compiler_params=pltpu.CompilerParams(dimension_semantics=("parallel", "parallel", "arbitrary")),
    )(a, b, *([] if order is None else [order]), *extras)
    return outs if no > 1 else outs[0]


def _dw_half(a, b, side, *, axis, name, add=None, order=None, tile=1024, tk=2048):
    (K, M), N = a.shape, b.shape[1]
    tk = min(tk, K)
    nk = K // tk
    if axis == "m":
        tm, tn = tile, min(N, 1024)
        grid, o_dims = (4, N // tn, nk), (4, tile, N)
        a_spec = pl.BlockSpec((tk, tm), lambda q, j, k, s: (k, 2 * q + s[0]))
        b_spec = pl.BlockSpec((tk, tn), lambda q, j, k, s: (k, j))
        o_spec = pl.BlockSpec((None, tm, tn), lambda q, j, k, s: (q, 0, j))
    else:
        tm, tn = min(M, 1024), tile
        grid, o_dims = (M // tm, 4, nk), (4, M, tile)
        a_spec = pl.BlockSpec((tk, tm), lambda i, q, k, s: (k, i))
        b_spec = pl.BlockSpec((tk, tn), lambda i, q, k, s: (k, 2 * q + s[0]))
        o_spec = pl.BlockSpec((None, tm, tn), lambda i, q, k, s: (q, i, 0))
    n_order, n_add = int(order is not None), int(add is not None)
    n_out = 1 + n_add

    def body(s_ref, a_ref, b_ref, *rest):
        rest = rest[n_order:]
        outs, acc_ref = rest[n_add:n_add + n_out], rest[-1]
        k = pl.program_id(2)
        part = _dot_tn(a_ref[...], b_ref[...])

        @pl.when(k == 0)
        def _():
            acc_ref[...] = part

        @pl.when(k > 0)
        def _():
            acc_ref[...] += part

        @pl.when(k == nk - 1)
        def _():
            res = acc_ref[...] + rest[0][...].astype(F32) if n_add else acc_ref[...]
            for o in outs:
                o[...] = res.astype(o.dtype)

    outs = pl.pallas_call(
        body, name=name, out_shape=[jax.ShapeDtypeStruct(o_dims, BF16)] * n_out,
        grid_spec=pltpu.PrefetchScalarGridSpec(
            num_scalar_prefetch=1, grid=grid, in_specs=[a_spec, b_spec] + [_ANY] * n_order + [o_spec] * n_add,
            out_specs=[o_spec] * n_out, scratch_shapes=[pltpu.VMEM((tm, tn), F32)]),
        compiler_params=pltpu.CompilerParams(dimension_semantics=("arbitrary", "arbitrary", "arbitrary")),
    )(side, a, b, *([order] if n_order else []), *([add] if n_add else []))
    return outs if n_add else outs[0]


def _norm_fwd(x, g, sc, sh, name, resid=None):
    B, S, Dm = x.shape
    ts = min(S, 256)
    tok = pl.BlockSpec((None, ts, Dm), lambda b, i: (b, i, 0))
    row = pl.BlockSpec((None, 1, Dm), lambda b, i: (b, 0, 0))
    par = pl.BlockSpec((1, Dm), lambda b, i: (0, 0))

    def body(*refs):
        if resid is not None:
            x_ref, br_ref, gt_ref, g_ref, sc_ref, sh_ref, xo_ref, h_ref = refs
            xv = x_ref[...] + gt_ref[...] * br_ref[...]
            xo_ref[...] = xv
        else:
            x_ref, g_ref, sc_ref, sh_ref, h_ref = refs
            xv = x_ref[...]
        r = lax.rsqrt(jnp.mean(xv * xv, axis=-1, keepdims=True) + EPS)
        h_ref[...] = ((xv * r * g_ref[...]) * (1.0 + sc_ref[...]) + sh_ref[...]).astype(BF16)

    h_shape = jax.ShapeDtypeStruct((B, S, Dm), BF16)
    if resid is not None:
        return pl.pallas_call(body, name=name, grid=(B, S // ts), in_specs=[tok, tok, row, par, row, row],
                              out_specs=[tok, tok], out_shape=[jax.ShapeDtypeStruct((B, S, Dm), F32), h_shape],
                              )(x, resid[0], resid[1], g, sc, sh)
    return pl.pallas_call(body, name=name, grid=(B, S // ts), in_specs=[tok, par, row, row], out_specs=tok,
                          out_shape=h_shape)(x, g, sc, sh)


def _norm_bwd(x, g, name, *, sc=None, dh=None, dres=None, tgt=None, br=None, gate=None, x_is_prev=False):
    B, S, Dm = x.shape
    ts = min(S, 256)
    final = tgt is not None
    has_br = br is not None
    tok = pl.BlockSpec((None, ts, Dm), lambda b, i: (b, i, 0))
    row = pl.BlockSpec((None, 1, Dm), lambda b, i: (b, 0, 0))
    par = pl.BlockSpec((1, Dm), lambda b, i: (0, 0))
    ins, in_specs = [x, g], [tok, par]
    if final:
        ins, in_specs = ins + [tgt], in_specs + [tok]
    else:
        ins, in_specs = ins + [sc, dh], in_specs + [row, tok]
    if dres is not None:
        ins, in_specs = ins + [dres], in_specs + [tok]
    if has_br:
        ins, in_specs = ins + [br, gate], in_specs + [tok, row]
    n_in = len(ins)
    out_shape = [jax.ShapeDtypeStruct((B, S, Dm), F32), jax.ShapeDtypeStruct((1, Dm), F32)]
    out_specs = [tok, par]
    if final:
        out_shape.append(jax.ShapeDtypeStruct((1, 128), F32))
        out_specs.append(pl.BlockSpec((1, 128), lambda b, i: (0, 0)))
    else:
        out_shape += [jax.ShapeDtypeStruct((B, 1, Dm), F32)] * 2
        out_specs += [row, row]
    if has_br:
        out_shape += [jax.ShapeDtypeStruct((B, S, Dm), BF16), jax.ShapeDtypeStruct((B, 1, Dm), F32)]
        out_specs += [tok, row]

    def body(*refs):
        it = iter(refs[:n_in])
        outs = iter(refs[n_in:])
        x_ref, g_ref = next(it), next(it)
        b, i = pl.program_id(0), pl.program_id(1)
        first, first_row = (b == 0) & (i == 0), i == 0
        xv, gv = x_ref[...], g_ref[...]
        if x_is_prev:
            xv = xv + refs[n_in - 1][...] * refs[n_in - 2][...]
        r = lax.rsqrt(jnp.mean(xv * xv, axis=-1, keepdims=True) + EPS)
        n = xv * r
        dx_ref, dg_ref = next(outs), next(outs)

        def acc(ref, val, init):
            @pl.when(init)
            def _():
                ref[...] = val

            @pl.when(jnp.logical_not(init))
            def _():
                ref[...] += val

        if final:
            t_ref = next(it)
            loss_ref = next(outs)
            e = n * gv - t_ref[...]
            acc(loss_ref, jnp.zeros((1, 128), F32) + 0.5 * jnp.sum(e * e) / Dm, first)
            dyg = e * (1.0 / Dm)
        else:
            sc_ref, dh_ref = next(it), next(it)
            dsc_ref, dsh_ref = next(outs), next(outs)
            dhv = dh_ref[...].astype(F32)
            acc(dsh_ref, jnp.sum(dhv, axis=0, keepdims=True), first_row)
            acc(dsc_ref, jnp.sum(dhv * (n * gv), axis=0, keepdims=True), first_row)
            dyg = dhv * (1.0 + sc_ref[...])
        acc(dg_ref, jnp.sum(dyg * n, axis=0, keepdims=True), first)
        dn = dyg * gv
        dx = r * (dn - n * jnp.mean(dn * n, axis=-1, keepdims=True))
        if dres is not None:
            dx = dx + next(it)[...]
        dx_ref[...] = dx
        if has_br:
            br_ref, gt_ref = next(it), next(it)
            dbr_ref, dgt_ref = next(outs), next(outs)
            dbr_ref[...] = (dx * gt_ref[...]).astype(BF16)
            acc(dgt_ref, jnp.sum(dx * br_ref[...], axis=0, keepdims=True), first_row)

    outs = pl.pallas_call(body, name=name, grid=(B, S // ts), in_specs=in_specs, out_specs=out_specs, out_shape=out_shape,
                          compiler_params=pltpu.CompilerParams(dimension_semantics=("arbitrary", "arbitrary")))(*ins)
    res = dict(dx=outs[0], dg=outs[1])
    if final:
        res["loss"] = outs[2]
    else:
        res["dsc"], res["dsh"] = outs[2], outs[3]
    if has_br:
        res["dbr"], res["dgate"] = outs[-2], outs[-1]
    return res


def _gm_heads(vg, lng, lnb):
    res = []
    for h in range(GM_H):
        sl = slice(h * 128, (h + 1) * 128)
        vh = vg[:, sl]
        xc = vh - jnp.mean(vh, axis=-1, keepdims=True)
        rstd = lax.rsqrt(jnp.mean(xc * xc, axis=-1, keepdims=True) + 1e-5)
        xhat = xc * rstd
        res.append((xhat, rstd, xhat * lng[:, sl] + lnb[:, sl]))
    return res


def _gm_gate(heads, wt_ref, bsx, nch):
    cols = []
    for h in range(GM_H):
        vn = heads[h][2].astype(BF16)
        rows = [_dot(wt_ref[h], vn[c * CHUNK:(c + 1) * CHUNK]) + bsx[:, h * 128:(h + 1) * 128] for c in range(nch)]
        cols.append(jnp.concatenate(rows, axis=0) if nch > 1 else rows[0])
    return jnp.concatenate(cols, axis=1)


def _gm_specs(S):
    tb = min(S, 512)
    u = pl.BlockSpec((None, tb, GM_W), lambda b, i: (b, i, OFF["u"] // GM_W))
    v = pl.BlockSpec((None, tb, GM_W), lambda b, i: (b, i, OFF["v"] // GM_W))
    tok = pl.BlockSpec((None, tb, GM_W), lambda b, i: (b, i, 0))
    return tb, u, v, tok


def _gmlp_fwd(P, lng, lnb, wt, bsx, og, name):
    B, S, _ = P.shape
    tb, u_spec, v_spec, tok = _gm_specs(S)
    nch = tb // CHUNK

    def body(u_ref, v_ref, lng_ref, lnb_ref, wt_ref, bsx_ref, og_ref, o_ref):
        heads = _gm_heads(_gelu(v_ref[...]), lng_ref[...], lnb_ref[...])
        y = _gelu(u_ref[...]) * _gm_gate(heads, wt_ref, bsx_ref[...], nch)
        r = lax.rsqrt(jnp.mean(y * y, axis=-1, keepdims=True) + EPS)
        o_ref[...] = (y * r * og_ref[...]).astype(BF16)

    return pl.pallas_call(
        body, name=name, grid=(B, S // tb),
        in_specs=[u_spec, v_spec, _full((1, GM_W)), _full((1, GM_W)), _full((GM_H, 128, 128)), _full((128, GM_W)), _full((1, GM_W))],
        out_specs=tok, out_shape=jax.ShapeDtypeStruct((B, S, GM_W + ATT_W + SSM_W), BF16))(P, P, lng, lnb, wt, bsx, og)


def _gmlp_bwd(P, dcat, lng, lnb, wt, wtT, bsx, og, name):
    B, S, _ = P.shape
    tb, u_spec, v_spec, tok = _gm_specs(S)
    nch = tb // CHUNK
    do_spec = pl.BlockSpec((None, tb, GM_W), lambda b, i: (b, i, 0))

    def body(u_ref, v_ref, do_ref, lng_ref, lnb_ref, wt_ref, wtT_ref, bsx_ref, og_ref,
             du_ref, dv_ref, dlng_ref, dlnb_ref, dws_ref, dbsx_ref, dog_ref):
        first = (pl.program_id(0) == 0) & (pl.program_id(1) == 0)

        @pl.when(first)
        def _():
            for ref in (dlng_ref, dlnb_ref, dws_ref, dbsx_ref, dog_ref):
                ref[...] = jnp.zeros(ref.shape, F32)

        u, v, lng = u_ref[...], v_ref[...], lng_ref[...]
        ug = _gelu(u)
        heads = _gm_heads(_gelu(v), lng, lnb_ref[...])
        gate = _gm_gate(heads, wt_ref, bsx_ref[...], nch)
        y = ug * gate
        r = lax.rsqrt(jnp.mean(y * y, axis=-1, keepdims=True) + EPS)
        yn = y * r
        dout = do_ref[...].astype(F32)
        dog_ref[...] += jnp.sum(dout * yn, axis=0, keepdims=True)
        dyn = dout * og_ref[...]
        dy = r * (dyn - yn * jnp.mean(dyn * yn, axis=-1, keepdims=True))
        du_ref[...] = (dy * gate * _gelu_grad(u)).astype(BF16)
        dgate = dy * ug
        tril = lax.broadcasted_iota(jnp.int32, (128, 128), 0) >= lax.broadcasted_iota(jnp.int32, (128, 128), 1)
        dvg = []
        for h in range(GM_H):
            sl = slice(h * 128, (h + 1) * 128)
            xhat, rstd, vn = heads[h]
            vnb = vn.astype(BF16)
            dgh = dgate[:, sl]
            dgb = dgh.astype(BF16)
            dbs = jnp.zeros((128, 128), F32)
            dw = jnp.zeros((128, 128), F32)
            dvn = []
            for c in range(nch):
                rs = slice(c * CHUNK, (c + 1) * CHUNK)
                dbs = dbs + dgh[rs]
                dw = dw + _dot_nt(dgb[rs], vnb[rs])
                dvn.append(_dot(wtT_ref[h], dgb[rs]))
            dvn = jnp.concatenate(dvn, axis=0) if nch > 1 else dvn[0]
            dbsx_ref[:, sl] += dbs
            dws_ref[h] += jnp.where(tril, dw, 0.0)
            dlng_ref[:, sl] += jnp.sum(dvn * xhat, axis=0, keepdims=True)
            dlnb_ref[:, sl] += jnp.sum(dvn, axis=0, keepdims=True)
            dxh = dvn * lng[:, sl]
            dvg.append(rstd * (dxh - jnp.mean(dxh, axis=-1, keepdims=True) - xhat * jnp.mean(dxh * xhat, axis=-1, keepdims=True)))
        dv_ref[...] = (jnp.concatenate(dvg, axis=1) * _gelu_grad(v)).astype(BF16)

    p512, w3 = _full((1, GM_W)), _full((GM_H, 128, 128))
    return pl.pallas_call(
        body, name=name, grid=(B, S // tb),
        in_specs=[u_spec, v_spec, do_spec, p512, p512, w3, w3, _full((128, GM_W)), p512],
        out_specs=[tok, tok, p512, p512, w3, _full((128, GM_W)), p512],
        out_shape=[jax.ShapeDtypeStruct((B, S, GM_W), BF16)] * 2 + [
            jax.ShapeDtypeStruct((1, GM_W), F32), jax.ShapeDtypeStruct((1, GM_W), F32),
            jax.ShapeDtypeStruct((GM_H, 128, 128), F32), jax.ShapeDtypeStruct((128, GM_W), F32),
            jax.ShapeDtypeStruct((1, GM_W), F32)],
        compiler_params=pltpu.CompilerParams(dimension_semantics=("arbitrary", "arbitrary")),
    )(P, P, dcat, lng, lnb, wt, wtT, bsx, og)


def _lane_half():
    return lax.broadcasted_iota(jnp.int32, (128, 128), 1) // 64


def _att_stack(x, kvh, dtype):
    half = _lane_half()
    rows = []
    for g in range(4):
        i = kvh * 4 + g
        pair = x[:, (i // 2) * 128:(i // 2 + 1) * 128]
        if i % 2 != kvh:
            pair = pltpu.roll(pair, 64, 1)
        rows.append(jnp.where(half == kvh, pair, 0.0))
    return jnp.concatenate(rows, axis=0).astype(dtype)


def _att_unstack(pairs, y, kvh):
    half = _lane_half()
    for g in range(4):
        i = kvh * 4 + g
        piece = y[g * 128:(g + 1) * 128]
        if i % 2 != kvh:
            piece = pltpu.roll(piece, 64, 1)
        pairs[i // 2] = jnp.where(half == i % 2, piece, pairs[i // 2])
    return pairs


def _att_probs(qb, k2, st, sink_ref, kvh):
    qm = _att_stack(qb, kvh, BF16)
    s = _dot_nt(qm, k2) * (64 ** -0.5)
    qi = lax.broadcasted_iota(jnp.int32, (512, 256), 0) % 128
    kj = lax.broadcasted_iota(jnp.int32, (512, 256), 1)
    diff = qi + 128 - kj
    valid = (diff >= 0) & (diff < 128) & (st + kj - 128 >= 0)
    s = jnp.where(valid, s, NEG_INF)
    grp = lax.broadcasted_iota(jnp.int32, (512, 1), 0) // 128
    sink = jnp.zeros((512, 1), F32)
    for g in range(4):
        sink = jnp.where(grp == g, sink_ref[kvh * 4 + g], sink)
    m = jnp.maximum(jnp.max(s, axis=-1, keepdims=True), sink)
    e = jnp.exp(s - m)
    esink = jnp.exp(sink - m)
    inv = 1.0 / (jnp.sum(e, axis=-1, keepdims=True) + esink)
    return qm, e * inv, esink * inv


def _att_specs(S):
    q = pl.BlockSpec((None, S, ATT_W), lambda b: (b, 0, OFF["q"] // ATT_W))
    k = pl.BlockSpec((None, S, KV_W), lambda b: (b, 0, OFF["k"] // KV_W))
    v = pl.BlockSpec((None, S, KV_W), lambda b: (b, 0, OFF["vv"] // KV_W))
    tok = pl.BlockSpec((None, S, ATT_W), lambda b: (b, 0, 0))
    kv = pl.BlockSpec((None, S, KV_W), lambda b: (b, 0, 0))
    return q, k, v, tok, kv


_SMEM = pl.BlockSpec(memory_space=pltpu.SMEM)


def _attn_fwd(P, sinks, og, cat, name):
    B, S, _ = P.shape
    q_spec, k_spec, v_spec, _, _ = _att_specs(S)
    tok = pl.BlockSpec((None, S, ATT_W), lambda b: (b, 0, GM_W // ATT_W))

    def body(q_ref, k_ref, v_ref, sink_ref, og_ref, cat_ref, o_ref, kpad, vpad):
        kpad[0:128, :] = jnp.zeros((128, KV_W), BF16)
        vpad[0:128, :] = jnp.zeros((128, KV_W), BF16)
        kpad[128:, :] = k_ref[...].astype(BF16)
        vpad[128:, :] = v_ref[...].astype(BF16)

        def step(n, carry):
            st = pl.multiple_of(n * 128, 128)
            qb = q_ref[pl.ds(st, 128), :]
            k2, v2 = kpad[pl.ds(st, 256), :], vpad[pl.ds(st, 256), :]
            pairs = [jnp.zeros((128, 128), F32)] * 4
            for kvh in range(2):
                _, p, _ = _att_probs(qb, k2, st, sink_ref, kvh)
                pairs = _att_unstack(pairs, _dot(p.astype(BF16), v2), kvh)
            o = jnp.concatenate(pairs, axis=1)
            r = lax.rsqrt(jnp.mean(o * o, axis=-1, keepdims=True) + EPS)
            o_ref[pl.ds(st, 128), :] = (o * r * og_ref[...]).astype(BF16)
            return carry

        lax.fori_loop(0, S // 128, step, 0)

    return pl.pallas_call(
        body, name=name, grid=(B,), in_specs=[q_spec, k_spec, v_spec, _SMEM, _full((1, ATT_W)), _ANY], out_specs=tok,
        out_shape=jax.ShapeDtypeStruct(cat.shape, BF16), input_output_aliases={5: 0},
        scratch_shapes=[pltpu.VMEM((S + 128, KV_W), BF16)] * 2)(P, P, P, sinks, og, cat)


def _attn_bwd(P, dcat, sinks, og, name):
    B, S, _ = P.shape
    q_spec, k_spec, v_spec, tok, kv = _att_specs(S)
    do_spec = pl.BlockSpec((None, S, ATT_W), lambda b: (b, 0, GM_W // ATT_W))

    def body(q_ref, k_ref, v_ref, do_ref, sink_ref, og_ref, dq_ref, dk_ref, dv_ref, dsink_ref, dog_ref,
             kpad, vpad, dkpad, dvpad):
        @pl.when(pl.program_id(0) == 0)
        def _():
            dsink_ref[...] = jnp.zeros((8, 128), F32)
            dog_ref[...] = jnp.zeros((1, ATT_W), F32)

        kpad[0:128, :] = jnp.zeros((128, KV_W), BF16)
        vpad[0:128, :] = jnp.zeros((128, KV_W), BF16)
        kpad[128:, :] = k_ref[...].astype(BF16)
        vpad[128:, :] = v_ref[...].astype(BF16)
        dkpad[...] = jnp.zeros((S + 128, KV_W), F32)
        dvpad[...] = jnp.zeros((S + 128, KV_W), F32)
        half = _lane_half()
        head_row = lax.broadcasted_iota(jnp.int32, (8, 128), 0)

        def step(n, carry):
            st = pl.multiple_of(n * 128, 128)
            qb = q_ref[pl.ds(st, 128), :]
            k2, v2 = kpad[pl.ds(st, 256), :], vpad[pl.ds(st, 256), :]
            saved, pairs = [], [jnp.zeros((128, 128), F32)] * 4
            for kvh in range(2):
                qm, p, psink = _att_probs(qb, k2, st, sink_ref, kvh)
                o = _dot(p.astype(BF16), v2)
                saved.append((qm, p, psink, o))
                pairs = _att_unstack(pairs, o, kvh)
            o = jnp.concatenate(pairs, axis=1)
            r = lax.rsqrt(jnp.mean(o * o, axis=-1, keepdims=True) + EPS)
            on = o * r
            dout = do_ref[pl.ds(st, 128), :].astype(F32)
            dog_ref[...] += jnp.sum(dout * on, axis=0, keepdims=True)
            dyn = dout * og_ref[...]
            do = r * (dyn - on * jnp.mean(dyn * on, axis=-1, keepdims=True))
            dq_pairs = [jnp.zeros((128, 128), F32)] * 4
            dsink = jnp.zeros((8, 128), F32)
            for kvh in range(2):
                qm, p, psink, og_ = saved[kvh]
                dog = _att_stack(do, kvh, F32)
                delta = jnp.sum(dog * jnp.where(jnp.concatenate([half] * 4, axis=0) == kvh, og_, 0.0), axis=-1, keepdims=True)
                dogb, pb = dog.astype(BF16), p.astype(BF16)
                dvpad[pl.ds(st, 256), :] += _dot_tn(pb, dogb)
                dp = _dot_nt(dogb, v2)
                ds = (p * (dp - delta) * (64 ** -0.5)).astype(BF16)
                sd = psink * delta
                for g in range(4):
                    dsink = dsink - jnp.where(head_row == kvh * 4 + g, jnp.sum(sd[g * 128:(g + 1) * 128]), 0.0)
                dq_pairs = _att_unstack(dq_pairs, _dot(ds, k2), kvh)
                dkpad[pl.ds(st, 256), :] += _dot_tn(ds, qm)
            dsink_ref[...] += dsink
            dq_ref[pl.ds(st, 128), :] = jnp.concatenate(dq_pairs, axis=1).astype(BF16)
            return carry

        lax.fori_loop(0, S // 128, step, 0)
        dk_ref[...] = dkpad[128:, :].astype(BF16)
        dv_ref[...] = dvpad[128:, :].astype(BF16)

    return pl.pallas_call(
        body, name=name, grid=(B,),
        in_specs=[q_spec, k_spec, v_spec, do_spec, _SMEM, _full((1, ATT_W))],
        out_specs=[tok, kv, kv, _full((8, 128)), _full((1, ATT_W))],
        out_shape=[jax.ShapeDtypeStruct((B, S, ATT_W), BF16), jax.ShapeDtypeStruct((B, S, KV_W), BF16),
                   jax.ShapeDtypeStruct((B, S, KV_W), BF16), jax.ShapeDtypeStruct((8, 128), F32),
                   jax.ShapeDtypeStruct((1, ATT_W), F32)],
        scratch_shapes=[pltpu.VMEM((S + 128, KV_W), BF16)] * 2 + [pltpu.VMEM((S + 128, KV_W), F32)] * 2,
        compiler_params=pltpu.CompilerParams(dimension_semantics=("arbitrary",)),
    )(P, P, P, dcat, sinks, og)


CONV_TC = 256
CONV_RC = 64


def _conv_taps(ext, r0):
    return [ext[pl.ds(r0 + 8 - k, CONV_RC), :] for k in range(4)]


def _conv_pre(taps, w_ref, b_ref):
    acc = b_ref[...] + w_ref[3:4, :] * taps[0]
    for k in range(1, 4):
        acc = acc + w_ref[3 - k:4 - k, :] * taps[k]
    return acc


def _conv_fwd(P, w8, b, name):
    B, S, _ = P.shape
    nj = CONV_CH // CONV_TC
    x_spec = pl.BlockSpec((None, S, CONV_TC), lambda b_, j: (b_, 0, OFF["xbc"] // CONV_TC + j))
    tok = pl.BlockSpec((None, S, CONV_TC), lambda b_, j: (b_, 0, j))

    def body(x_ref, w_ref, b_ref, o_ref, ext):
        ext[0:8, :] = jnp.zeros((8, CONV_TC), F32)
        ext[8:, :] = x_ref[...]
        for r0 in range(0, S, CONV_RC):
            pre = _conv_pre(_conv_taps(ext, r0), w_ref, b_ref)
            o_ref[pl.ds(r0, CONV_RC), :] = pre * _sigmoid(pre)

    return pl.pallas_call(
        body, name=name, grid=(B, nj),
        in_specs=[x_spec, pl.BlockSpec((8, CONV_TC), lambda b_, j: (0, j)), pl.BlockSpec((1, CONV_TC), lambda b_, j: (0, j))],
        out_specs=tok, out_shape=jax.ShapeDtypeStruct((B, S, CONV_CH), F32),
        scratch_shapes=[pltpu.VMEM((S + 8, CONV_TC), F32)])(P, w8, b)


def _conv_bwd(P, dact, w8, b, name):
    B, S, _ = P.shape
    nj = CONV_CH // CONV_TC
    x_spec = pl.BlockSpec((None, S, CONV_TC), lambda j, b_: (b_, 0, OFF["xbc"] // CONV_TC + j))
    tok = pl.BlockSpec((None, S, CONV_TC), lambda j, b_: (b_, 0, j))
    w_spec = pl.BlockSpec((8, CONV_TC), lambda j, b_: (0, j))
    b_spec = pl.BlockSpec((1, CONV_TC), lambda j, b_: (0, j))

    def body(x_ref, d_ref, w_ref, b_ref, dx_ref, dw_ref, db_ref, ext, extd):
        @pl.when(pl.program_id(1) == 0)
        def _():
            dw_ref[...] = jnp.zeros((8, CONV_TC), F32)
            db_ref[...] = jnp.zeros((1, CONV_TC), F32)

        ext[0:8, :] = jnp.zeros((8, CONV_TC), F32)
        ext[8:, :] = x_ref[...]
        extd[pl.ds(8 + S, 8), :] = jnp.zeros((8, CONV_TC), F32)
        db = jnp.zeros((1, CONV_TC), F32)
        dws = [jnp.zeros((1, CONV_TC), F32)] * 4
        for r0 in range(0, S, CONV_RC):
            taps = _conv_taps(ext, r0)
            pre = _conv_pre(taps, w_ref, b_ref)
            sg = _sigmoid(pre)
            dpre = d_ref[pl.ds(r0, CONV_RC), :] * (sg * (1.0 + pre * (1.0 - sg)))
            extd[pl.ds(8 + r0, CONV_RC), :] = dpre
            db = db + jnp.sum(dpre, axis=0, keepdims=True)
            dws = [dws[i] + jnp.sum(dpre * taps[3 - i], axis=0, keepdims=True) for i in range(4)]
        for r0 in range(0, S, CONV_RC):
            dx = w_ref[3:4, :] * extd[pl.ds(8 + r0, CONV_RC), :]
            for k in range(1, 4):
                dx = dx + w_ref[3 - k:4 - k, :] * extd[pl.ds(8 + r0 + k, CONV_RC), :]
            dx_ref[pl.ds(r0, CONV_RC), :] = dx.astype(BF16)
        db_ref[...] += db
        sub = lax.broadcasted_iota(jnp.int32, (8, CONV_TC), 0)
        dw_ref[...] += sum(jnp.where(sub == i, dws[i], 0.0) for i in range(4))

    return pl.pallas_call(
        body, name=name, grid=(nj, B), in_specs=[x_spec, tok, w_spec, b_spec], out_specs=[tok, w_spec, b_spec],
        out_shape=[jax.ShapeDtypeStruct((B, S, CONV_CH), BF16), jax.ShapeDtypeStruct((8, CONV_CH), F32),
                   jax.ShapeDtypeStruct((1, CONV_CH), F32)],
        scratch_shapes=[pltpu.VMEM((S + 8, CONV_TC), F32), pltpu.VMEM((S + 16, CONV_TC), F32)],
        compiler_params=pltpu.CompilerParams(dimension_semantics=("arbitrary", "arbitrary")),
    )(P, dact, w8, b)


def _ssd_consts():
    hd = np.arange(SSM_W) // SSM_HD
    E = (np.arange(128)[:, None] == hd[None, :]).astype(np.float32)
    tri = (np.arange(128)[:, None] >= np.arange(128)[None, :]).astype(np.float32)
    return jnp.asarray(E, BF16), jnp.asarray(E.T, BF16), jnp.asarray(tri, BF16), jnp.asarray(tri.T, BF16)


def _pieces(x, n):
    out, r = [], x
    for _ in range(n):
        p = r.astype(BF16)
        out.append(p)
        r = r - p.astype(F32)
    return out


def _dot01(x, m01, n):
    return sum(_dot(p, m01) for p in _pieces(x, n))


def _dot01_left(m01, x, n):
    return sum(_dot(m01, p) for p in _pieces(x, n))


def _ssd_pre(xa, dtraw, bias, alog, E, tri):
    lane = lax.broadcasted_iota(jnp.int32, (128, 128), 1)
    pre = dtraw + bias
    dtp = jnp.where(lane < SSM_H, jnp.maximum(pre, 0.0) + jnp.log(1.0 + jnp.exp(-jnp.abs(pre))), 0.0)
    a = -jnp.exp(alog)
    acs = _dot01_left(tri, dtp * a, 3)
    acsT = acs.T
    dtE, acsE = _dot01(dtp, E, 2), _dot01(acs, E, 3)
    X = xa[:, :SSM_W]
    xdt = X * dtE
    wE = jnp.exp(acsE[127:128, :] - acsE)
    eE = jnp.exp(acsE)
    cdE = eE[127:128, :]
    return dict(pre=pre, dtp=dtp, a=a, acs=acs, acsT=acsT, dtE=dtE, acsE=acsE, cdE=cdE, X=X, xdt=xdt, wE=wE, eE=eE)


def _ssd_decay(c, h):
    lm = lax.broadcasted_iota(jnp.int32, (128, 128), 0) >= lax.broadcasted_iota(jnp.int32, (128, 128), 1)
    return jnp.exp(jnp.where(lm, c["acs"][:, h:h + 1] - c["acsT"][h:h + 1, :], NEG_INF))


def _ssd_pair_operands(c, CB, h0):
    lane = lax.broadcasted_iota(jnp.int32, (128, 128), 1)
    L0, L1 = _ssd_decay(c, h0), _ssd_decay(c, h0 + 1)
    M = jnp.concatenate([CB * L0, CB * L1], axis=1).astype(BF16)
    xp = c["xdt"][:, h0 * 64:h0 * 64 + 128]
    BD = jnp.concatenate([jnp.where(lane < 64, xp, 0.0), jnp.where(lane >= 64, xp, 0.0)], axis=0).astype(BF16)
    return L0, L1, M, BD


def _ssd_y(c, xa, state_ref, dskipE):
    per_group, ys = [], []
    for g in range(SSM_G):
        gs = slice(g * 512, (g + 1) * 512)
        Bb = xa[:, SSM_W + g * 128:SSM_W + (g + 1) * 128].astype(BF16)
        Cb = xa[:, SSM_W + 256 + g * 128:SSM_W + 256 + (g + 1) * 128].astype(BF16)
        CB = _dot_nt(Cb, Bb)
        Sg = state_ref[:, gs]
        yoff = _dot(Cb, Sg.astype(BF16)) * c["eE"][:, gs]
        ydiag, pairs = [], []
        for j in range(4):
            ops = _ssd_pair_operands(c, CB, g * 8 + 2 * j)
            pairs.append(ops)
            ydiag.append(_dot(ops[2], ops[3]))
        ys.append(jnp.concatenate(ydiag, axis=1) + yoff)
        per_group.append(dict(Bb=Bb, Cb=Cb, CB=CB, Sg=Sg, yoff=yoff, pairs=pairs))
    Y = jnp.concatenate(ys, axis=1) + c["X"] * dskipE
    return Y, per_group


def _ssd_specs(S, rev):
    nc = S // CHUNK
    cm = (lambda b, i: (b, nc - 1 - i)) if rev else (lambda b, i: (b, i))
    xa = pl.BlockSpec((None, CHUNK, CONV_CH), lambda b, i: cm(b, i) + (0,))
    z = [pl.BlockSpec((None, CHUNK, 256), lambda b, i, q=q: cm(b, i) + (OFF["z"] // 256 + q,)) for q in range(4)]
    dt = pl.BlockSpec((None, CHUNK, 128), lambda b, i: cm(b, i) + (OFF["dt"] // 128,))
    tok = pl.BlockSpec((None, CHUNK, SSM_W), lambda b, i: cm(b, i) + (0,))
    st = pl.BlockSpec((None, None, 128, SSM_W), lambda b, i: cm(b, i) + (0, 0))
    return nc, xa, z, dt, tok, st


def _ssd_fwd(xact, P, bias, alog, dskipE, ng, cat, name):
    B, S, _ = P.shape
    nc, xa_spec, z_specs, dt_spec, _, st_spec = _ssd_specs(S, False)
    tok = pl.BlockSpec((None, CHUNK, SSM_W), lambda b, i: (b, i, 1))
    E, _, tri, _ = _ssd_consts()

    def body(xa_ref, z0, z1, z2, z3, dt_ref, bias_ref, alog_ref, dsk_ref, ng_ref, E_ref, tri_ref, cat_ref, o_ref, sp_ref, state):
        @pl.when(pl.program_id(1) == 0)
        def _():
            state[...] = jnp.zeros((128, SSM_W), F32)

        sp_ref[...] = state[...]
        xa = xa_ref[...]
        c = _ssd_pre(xa, dt_ref[...], bias_ref[...], alog_ref[...], E_ref[...], tri_ref[...])
        Y, groups = _ssd_y(c, xa, state, dsk_ref[...])
        Z = (c["xdt"] * c["wE"]).astype(BF16)
        for g in range(SSM_G):
            gs = slice(g * 512, (g + 1) * 512)
            state[:, gs] = groups[g]["Sg"] * c["cdE"][:, gs] + _dot_tn(groups[g]["Bb"], Z[:, gs])
        zv = jnp.concatenate([z0[...], z1[...], z2[...], z3[...]], axis=1)
        yz = Y * (zv * _sigmoid(zv))
        outs = []
        for g in range(SSM_G):
            yg = yz[:, g * 512:(g + 1) * 512]
            outs.append(yg * lax.rsqrt(jnp.mean(yg * yg, axis=-1, keepdims=True) + EPS))
        o_ref[...] = (jnp.concatenate(outs, axis=1) * ng_ref[...]).astype(BF16)

    return pl.pallas_call(
        body, name=name, grid=(B, nc),
        in_specs=[xa_spec] + z_specs + [dt_spec, _full((1, 128)), _full((1, 128)), _full((1, SSM_W)), _full((1, SSM_W)),
                                        _full((128, SSM_W)), _full((128, 128)), _ANY],
        out_specs=[tok, st_spec],
        out_shape=[jax.ShapeDtypeStruct(cat.shape, BF16), jax.ShapeDtypeStruct((B, nc, 128, SSM_W), F32)],
        scratch_shapes=[pltpu.VMEM((128, SSM_W), F32)], input_output_aliases={12: 0},
        compiler_params=pltpu.CompilerParams(dimension_semantics=("arbitrary", "arbitrary")),
    )(xact, P, P, P, P, P, bias, alog, dskipE, ng, E, tri, cat)


def _ssd_bwd(xact, P, sprev, dcat, bias, alog, dskipE, ng, name):
    B, S, _ = P.shape
    nc, xa_spec, z_specs, dt_spec, tok, st_spec = _ssd_specs(S, True)
    do_spec = pl.BlockSpec((None, CHUNK, SSM_W), lambda b, i: (b, nc - 1 - i, 1))
    E, ET, tri, triT = _ssd_consts()
    dt_out = pl.BlockSpec((None, CHUNK, 128), lambda b, i: (b, nc - 1 - i, 0))

    def body(xa_ref, z0, z1, z2, z3, dt_ref, sp_ref, do_ref, bias_ref, alog_ref, dsk_ref, ng_ref, E_ref, ET_ref, tri_ref,
             triT_ref, dxa_ref, dz_ref, ddt_ref, dbias_ref, dalog_ref, ddsk_ref, dng_ref, dstate):
        first = (pl.program_id(0) == 0) & (pl.program_id(1) == 0)

        @pl.when(first)
        def _():
            for ref in (dbias_ref, dalog_ref, ddsk_ref, dng_ref):
                ref[...] = jnp.zeros(ref.shape, F32)

        @pl.when(pl.program_id(1) == 0)
        def _():
            dstate[...] = jnp.zeros((128, SSM_W), F32)

        xa, ETm = xa_ref[...], ET_ref[...]
        c = _ssd_pre(xa, dt_ref[...], bias_ref[...], alog_ref[...], E_ref[...], tri_ref[...])
        Y, groups = _ssd_y(c, xa, sp_ref, dsk_ref[...])
        X, xdt = c["X"], c["xdt"]
        zv = jnp.concatenate([z0[...], z1[...], z2[...], z3[...]], axis=1)
        sg = _sigmoid(zv)
        zs = zv * sg
        yz = Y * zs
        dout = do_ref[...].astype(F32)
        dyz = []
        for g in range(SSM_G):
            gs = slice(g * 512, (g + 1) * 512)
            yg = yz[:, gs]
            r = lax.rsqrt(jnp.mean(yg * yg, axis=-1, keepdims=True) + EPS)
            yn = yg * r
            dng_ref[:, gs] += jnp.sum(dout[:, gs] * yn, axis=0, keepdims=True)
            dyn = dout[:, gs] * ng_ref[:, gs]
            dyz.append(r * (dyn - yn * jnp.mean(dyn * yn, axis=-1, keepdims=True)))
        dyz = jnp.concatenate(dyz, axis=1)
        dz_ref[...] = (dyz * Y * (sg * (1.0 + zv * (1.0 - sg)))).astype(BF16)
        dY = dyz * zs
        ddsk_ref[...] += jnp.sum(dY * X, axis=0, keepdims=True)
        dX = dY * dsk_ref[...]
        lane = lax.broadcasted_iota(jnp.int32, (128, 128), 1)
        sub = lax.broadcasted_iota(jnp.int32, (128, 128), 0)
        colform = jnp.zeros((128, 128), F32)
        rowform = jnp.zeros((128, 128), F32)
        dxdt, gacsE, dBC = [], [], []
        for g in range(SSM_G):
            gs = slice(g * 512, (g + 1) * 512)
            G = groups[g]
            Bb, Cb, CB, Sg = G["Bb"], G["Cb"], G["CB"], G["Sg"]
            dYg = dY[:, gs]
            dQ = (dYg * c["eE"][:, gs]).astype(BF16)
            dSn = dstate[:, gs]
            dSnb = dSn.astype(BF16)
            cd = c["cdE"][:, gs]
            dC = _dot_nt(dQ, Sg.astype(BF16))
            dSprev = _dot_tn(Cb, dQ) + dSn * cd
            t1 = jnp.broadcast_to(jnp.sum(dSn * Sg * cd, axis=0, keepdims=True), (8, 512))
            colform = colform + jnp.where(sub == 127, _dot01(t1, ETm[gs, :], 2)[0:1, :], 0.0)
            Zg = xdt[:, gs] * c["wE"][:, gs]
            dZ = _dot(Bb, dSnb)
            dB = _dot_nt(Zg.astype(BF16), dSnb)
            U = dZ * Zg
            ga = dYg * G["yoff"] - U
            ga = ga + jnp.where(lax.broadcasted_iota(jnp.int32, (128, 512), 0) == 127, jnp.sum(U, axis=0, keepdims=True), 0.0)
            gacsE.append(ga)
            dxg = [None] * 4
            dCB = jnp.zeros((128, 128), F32)
            for j in range(4):
                h0 = g * 8 + 2 * j
                L0, L1, M, BD = G["pairs"][j]
                dYp = dYg[:, j * 128:(j + 1) * 128].astype(BF16)
                dM = _dot_nt(dYp, BD)
                dBD = _dot_tn(M, dYp)
                dxg[j] = jnp.where(lane < 64, dBD[:128], dBD[128:])
                for t, (h, L) in enumerate(((h0, L0), (h0 + 1, L1))):
                    dMh = dM[:, t * 128:(t + 1) * 128]
                    dCB = dCB + dMh * L
                    Gh = dMh * CB * L
                    colform = colform + jnp.where(lane == h, jnp.sum(Gh, axis=1, keepdims=True), 0.0)
                    rowform = rowform - jnp.where(sub == h, jnp.sum(Gh, axis=0, keepdims=True), 0.0)
            dCBb = dCB.astype(BF16)
            dC = dC + _dot(dCBb, Bb)
            dB = dB + _dot_tn(dCBb, Cb)
            dxdt.append(jnp.concatenate(dxg, axis=1) + dZ * c["wE"][:, gs])
            dBC.append((dB, dC))
            dstate[:, gs] = dSprev
        dxdt = jnp.concatenate(dxdt, axis=1)
        dX = dX + dxdt * c["dtE"]
        ddt = _dot01(dxdt * X, ETm, 2)
        dacs = colform + rowform.T + _dot01(jnp.concatenate(gacsE, axis=1), ETm, 2)
        dda = _dot01_left(triT_ref[...], dacs, 2)
        ddt = ddt + dda * c["a"]
        dalog_ref[...] += jnp.sum(dda * c["dtp"], axis=0, keepdims=True) * c["a"]
        ddtraw = jnp.where(lane < SSM_H, ddt * _sigmoid(c["pre"]), 0.0)
        dbias_ref[...] += jnp.sum(ddtraw, axis=0, keepdims=True)
        ddt_ref[...] = ddtraw.astype(BF16)
        dxa_ref[...] = jnp.concatenate([dX, dBC[0][0], dBC[1][0], dBC[0][1], dBC[1][1]], axis=1)

    p128, p1k = _full((1, 128)), _full((1, SSM_W))
    return pl.pallas_call(
        body, name=name, grid=(B, nc),
        in_specs=[xa_spec] + z_specs + [dt_spec, st_spec, do_spec, p128, p128, p1k, p1k,
                                        _full((128, SSM_W)), _full((SSM_W, 128)), _full((128, 128)), _full((128, 128))],
        out_specs=[xa_spec, tok, dt_out, p128, p128, p1k, p1k],
        out_shape=[jax.ShapeDtypeStruct((B, S, CONV_CH), F32), jax.ShapeDtypeStruct((B, S, SSM_W), BF16),
                   jax.ShapeDtypeStruct((B, S, 128), BF16), jax.ShapeDtypeStruct((1, 128), F32),
                   jax.ShapeDtypeStruct((1, 128), F32), jax.ShapeDtypeStruct((1, SSM_W), F32),
                   jax.ShapeDtypeStruct((1, SSM_W), F32)],
        scratch_shapes=[pltpu.VMEM((128, SSM_W), F32)],
        compiler_params=pltpu.CompilerParams(dimension_semantics=("arbitrary", "arbitrary")),
    )(xact, P, P, P, P, P, sprev, dcat, bias, alog, dskipE, ng, E, ET, tri, triT)


def _adamw(w, parts, m, v, name, tr=512, row0=0, prev=None):
    Rtot, C = w.shape
    ns, R = parts.shape[0], parts.shape[1]
    tr = min(tr, R)
    assert R % tr == 0 and row0 % tr == 0
    off = row0 // tr
    c1 = 1.0 / (1.0 - ADAM_B1 ** ADAM_STEP)
    c2 = 1.0 / (1.0 - ADAM_B2 ** ADAM_STEP)

    def body(w_ref, p_ref, m_ref, v_ref, *rest):
        g_ref, d_ref, mo_ref, vo_ref = rest[-4:]
        g = p_ref[0].astype(F32)
        for s in range(1, ns):
            g = g + p_ref[s].astype(F32)
        mn = ADAM_B1 * m_ref[...] + (1.0 - ADAM_B1) * g
        vn = ADAM_B2 * v_ref[...] + (1.0 - ADAM_B2) * (g * g)
        g_ref[...] = g
        mo_ref[...] = mn
        vo_ref[...] = vn
        d_ref[...] = -ADAM_LR * ((mn * c1) / (jnp.sqrt(vn * c2) + ADAM_EPS) + ADAM_WD * w_ref[...])

    blk = pl.BlockSpec((tr, C), lambda i: (i + off, 0))
    extra = [] if prev is None else list(prev)
    return pl.pallas_call(
        body, name=name, grid=(R // tr,),
        in_specs=[blk, pl.BlockSpec((ns, tr, C), lambda i: (0, i, 0)), blk, blk] + [pl.BlockSpec(memory_space=pl.ANY)] * len(extra),
        out_specs=[blk] * 4, out_shape=[jax.ShapeDtypeStruct((Rtot, C), F32)] * 4,
        input_output_aliases={4 + k: k for k in range(len(extra))})(w, parts, m, v, *extra)


_SMALL = ("ada_b", "norm1_g", "gm_ln_g", "gm_ln_b", "gm_ws", "gm_bs", "gm_norm_g", "attn_sinks", "attn_norm_g", "conv_b",
          "dt_bias", "a_log", "d_skip", "ssm_norm_g", "norm2_g", "final_norm_g")


def _pack(arrs):
    flat = []
    for a in arrs:
        f = a.reshape(-1).astype(F32)
        flat.append(jnp.pad(f, (0, (-f.shape[0]) % 1024)))
    return jnp.concatenate(flat).reshape(-1, 128)


def _unpack(pack, like):
    out, o = [], 0
    flat = pack.reshape(-1)
    for a in like:
        n = int(np.prod(a.shape))
        out.append(flat[o:o + n].reshape(a.shape))
        o += n + (-n) % 1024
    return out


def kernel(x, c, ada_w, ada_b, norm1_g, w_in, gm_ln_g, gm_ln_b, gm_ws, gm_bs, gm_norm_g, attn_sinks, attn_norm_g, conv_w, conv_b, dt_bias, a_log, d_skip, ssm_norm_g, w_out, norm2_g, w_mlp1, w_mlp2, final_norm_g, loss_target, m_ada_w, m_ada_b, m_norm1_g, m_w_in, m_gm_ln_g, m_gm_ln_b, m_gm_ws, m_gm_bs, m_gm_norm_g, m_attn_sinks, m_attn_norm_g, m_conv_w, m_conv_b, m_dt_bias, m_a_log, m_d_skip, m_ssm_norm_g, m_w_out, m_norm2_g, m_w_mlp1, m_w_mlp2, m_final_norm_g, v_ada_w, v_ada_b, v_norm1_g, v_w_in, v_gm_ln_g, v_gm_ln_b, v_gm_ws, v_gm_bs, v_gm_norm_g, v_attn_sinks, v_attn_norm_g, v_conv_w, v_conv_b, v_dt_bias, v_a_log, v_d_skip, v_ssm_norm_g, v_w_out, v_norm2_g, v_w_mlp1, v_w_mlp2, v_final_norm_g):
    args = dict(locals())
    B, S, _ = x.shape
    T = B * S
    L = DEPTH
    me = 4 * lax.axis_index("x") + 2 * lax.axis_index("y") + lax.axis_index("c")

    gath = _gather2([c, conv_w], "ag_c")
    big = ("w_in", "w_out", "w_mlp1", "w_mlp2")
    chain = [(n, l) for l in range(L) for n in ("w_in", "w_mlp1", "w_out", "w_mlp2")]
    inflight = {}

    def start_next(order):
        if not chain:
            return jnp.zeros((8, 128), F32)
        n, l = chain.pop(0)
        sems, land_thru, token = _gather_start(zone[n, l], order, f"ag_start_{n}{l}")
        inflight[n, l] = (sems, land_thru)
        return token

    def gathered(n, l, after):
        land = _gather_wait(*inflight.pop((n, l)), after, f"ag_wait_{n}{l}")
        return _gather_finish(land, f"ag_fin_{n}{l}")

    forwarding = {}

    def arrived(n, l, after):
        land = _gather_wait(*inflight.pop((n, l)), after, f"ag_wait_{n}{l}")
        sems, land_thru, token = _forward_start(land, after, f"ag_fwd_start_{n}{l}")
        forwarding[n, l] = (sems, land_thru)
        return token

    def ready(n, l, after):
        return _forward_wait(*forwarding.pop((n, l)), after, f"ag_fwd_wait_{n}{l}")

    me1 = me.astype(jnp.int32).reshape(1)
    zone = {(n, l): _landing_zone(args[n], l, me1, f"ag_zone_{n}{l}") for n, l in chain}
    later_zones = [zone[k] for k in chain[1:]]

    tok = start_next(gath[0])
    c_all = gath[0].reshape(NDEV * B, D) + tok[0, 0]
    c_act = (c_all * jax.nn.sigmoid(c_all)).astype(BF16)
    nb_rows = c_act.shape[0]
    c_pad = jnp.pad(c_act, ((0, 128 - nb_rows), (0, 0)))
    adw = ada_w.astype(BF16)
    mod_part = jnp.stack([_mm(c_pad, adw[l], mode="nn", name=f"mod{l}", tn=768)[:nb_rows] for l in range(L)])
    mod_all = _gather_small([mod_part], "ag_mod", order=later_zones)[0]
    mod_mine = lax.dynamic_slice_in_dim(mod_all, me * B, B, axis=2)
    mod = jnp.transpose(mod_mine, (1, 2, 0, 3)).reshape(L, B, 6 * D) + ada_b[:, None, :]
    mods = [[mod[l][:, None, i * D:(i + 1) * D] for i in range(6)] for l in range(L)]

    win_g, wout_g, w1_g, w2_g = [None] * L, [None] * L, [None] * L, [None] * L

    tril = jnp.tril(jnp.ones((128, 128), F32))
    row = lambda a: a.reshape(1, -1)
    pad128 = lambda a: jnp.pad(a.reshape(1, -1), ((0, 0), (0, 128 - a.shape[-1])))
    small = []
    for l in range(L):
        wt = gm_ws[l] * tril
        small.append(dict(
            lng=row(gm_ln_g[l]), lnb=row(gm_ln_b[l]), wt=wt.astype(BF16), wtT=jnp.swapaxes(wt, 1, 2).astype(BF16),
            bsx=jnp.repeat(gm_bs[l].T, 128, axis=1), gog=row(gm_norm_g[l]), sinks=attn_sinks[l], aog=row(attn_norm_g[l]),
            bias=pad128(dt_bias[l]), alog=pad128(a_log[l]), dskE=jnp.repeat(d_skip[l], SSM_HD).reshape(1, SSM_W),
            sng=row(ssm_norm_g[l]), cb=row(conv_b[l])))
    convw_all = jnp.transpose(gath[1], (1, 2, 0, 3)).reshape(L, 4, CONV_CH)
    convw8 = jnp.pad(convw_all, ((0, 0), (0, 4), (0, 0)))

    saved = []
    xl = x
    h = _norm_fwd(xl, row(norm1_g[0]), mods[0][1], mods[0][0], "norm1_f0")
    for l in range(L):
        sm = small[l]
        if l == 0:
            g_in = gathered("w_in", 0, h)
            tok = start_next(g_in)
        win_g[l] = _shards_to_cols(g_in, f"w_in_cols{l}")
        P = _mm(h.reshape(T, D), win_g[l], mode="nn", name=f"proj_in{l}", tn=1536, order=tok).reshape(B, S, PW)
        cat = _gmlp_fwd(P, sm["lng"], sm["lnb"], sm["wt"], sm["bsx"], sm["gog"], f"gmlp_f{l}")
        cat = _attn_fwd(P, sm["sinks"], sm["aog"], cat, f"attn_f{l}")
        xact = _conv_fwd(P, convw8[l], sm["cb"], f"conv_f{l}")
        tok = start_next(arrived("w_mlp1", l, xact))
        cat, sprev = _ssd_fwd(xact, P, sm["bias"], sm["alog"], sm["dskE"], sm["sng"] + tok[0:1, 0:1], cat, f"ssd_f{l}")
        g_out = gathered("w_out", l, cat)
        tok = start_next(g_out)
        wout_g[l] = g_out.reshape(D, D)
        mix = _mm(cat.reshape(T, D), wout_g[l], mode="nn", name=f"proj_out{l}", order=tok).reshape(B, S, D)
        x_mid, h2 = _norm_fwd(xl, row(norm2_g[l]), mods[l][4], mods[l][3], f"norm2_f{l}", resid=(mix, mods[l][2]))
        w1_g[l] = ready("w_mlp1", l, h2)
        a_act, r_act = _mm(h2.reshape(T, D), w1_g[l], mode="nn", name=f"mlp1_{l}", out_dtypes=(BF16, BF16), col_blocked_b=True,
                           epilogue=lambda acc: (acc, jnp.square(jnp.maximum(acc, 0.0))))
        g_2 = gathered("w_mlp2", l, r_act)
        tok = start_next(g_2)
        w2_g[l] = g_2.reshape(DFF, D)
        m2 = _mm(r_act, w2_g[l], mode="nn", name=f"mlp2_{l}", order=tok, tk=4096).reshape(B, S, D)
        saved.append(dict(x_in=xl, h=h, P=P, xact=xact, sprev=sprev, cat=cat, mix=mix, x_mid=x_mid, h2=h2, a=a_act, r=r_act, m2=m2))
        if l + 1 < L:
            tok = start_next(arrived("w_in", l + 1, m2))
            xl, h = _norm_fwd(x_mid, row(norm1_g[l + 1]) + tok[0, 0], mods[l + 1][1], mods[l + 1][0], f"norm1_f{l + 1}",
                              resid=(m2, mods[l][5]))
            g_in = ready("w_in", l + 1, h)

    sv = saved[L - 1]
    nb = _norm_bwd(sv["x_mid"], row(final_norm_g), "final_b", tgt=loss_target, br=sv["m2"], gate=mods[L - 1][5], x_is_prev=True)
    loss_part, g_final = nb["loss"], nb["dg"]
    dmod, gsm, gconvw = [None] * L, [None] * L, [None] * L
    core = lax.axis_index("c").astype(jnp.int32).reshape(1)
    reducing = []

    def reduce_start(n, l, sent, after):
        p, from_sib = _pair_wait(*sent[:3], after, f"rs_pair_wait_{n}{l}")
        s, land = _pair_add(p, from_sib, core, f"rs_add_{n}{l}")
        return reduce_exchange(n, l, s, land, after)

    def reduce_exchange(n, l, s, land, order):
        sems, s_thru, land_thru, token = _chipsum_start(s, land, order, f"rs_start_{n}{l}")
        reducing.append((n, l, sems, s_thru, land_thru))
        return token

    other = 1 - core

    for l in reversed(range(L)):
        sv, sm = saved[l], small[l]
        dm2, dxo, dg2 = nb["dbr"].reshape(T, D), nb["dx"], nb["dgate"]
        da = _mm(dm2, w2_g[l], mode="nt", name=f"mlp2_dx{l}", out_dtypes=(BF16,), extras=(sv["a"],),
                 epilogue=lambda acc, a: (acc * (2.0 * jnp.maximum(a.astype(F32), 0.0)),))
        h2f = sv["h2"].reshape(T, D)
        sent2 = _sibling_start(_dw_half(sv["r"], dm2, other, axis="m", name=f"mlp2_dw_sib{l}"), da, f"rs_sib_start_w_mlp2{l}")
        dh2 = _mm(da, w1_g[l], mode="nt", name=f"mlp1_dx{l}", col_blocked_b=True, order=sent2[3],
                  out_dtypes=(BF16,)).reshape(B, S, D)
        from_sib = _sibling_wait(*sent2[:3], dh2, f"rs_sib_wait_w_mlp2{l}")[1]
        sent1 = _sibling_start(_dw_half(h2f, da, other, axis="n", name=f"mlp1_dw_sib{l}", order=from_sib), da,
                               f"rs_sib_start_w_mlp1{l}")
        s2, land2 = _dw_half(sv["r"], dm2, core, axis="m", name=f"mlp2_dw_own{l}", add=from_sib, order=sent1[3])
        tok = reduce_exchange("w_mlp2", l, s2, land2, da)
        nb2 = _norm_bwd(sv["x_mid"], row(norm2_g[l]) + tok[0, 0], f"norm2_b{l}", sc=mods[l][4], dh=dh2, dres=dxo, br=sv["mix"],
                        gate=mods[l][2])
        dmix = nb2["dbr"].reshape(T, D)
        from_sib = _sibling_wait(*sent1[:3], dmix, f"rs_sib_wait_w_mlp1{l}")[1]
        s1, land1 = _dw_half(h2f, da, core, axis="n", name=f"mlp1_dw_own{l}", add=from_sib)
        tok = reduce_exchange("w_mlp1", l, s1, land1, dmix)
        dcat = _mm(dmix, wout_g[l], mode="nt", name=f"proj_out_dx{l}", order=tok, out_dtypes=(BF16,)).reshape(B, S, D)
        du, dv, dlng, dlnb, dws, dbsx, dgog = _gmlp_bwd(sv["P"], dcat, sm["lng"], sm["lnb"], sm["wt"], sm["wtT"], sm["bsx"],
                                                        sm["gog"], f"gmlp_b{l}")
        dq, dk, dvv, dsink, daog = _attn_bwd(sv["P"], dcat, sm["sinks"], sm["aog"], f"attn_b{l}")
        dwo = _mm(sv["cat"].reshape(T, D), dmix, mode="tn", name=f"proj_out_dw{l}", out_dtypes=(BF16,), tk=2048,
                  order=dq).reshape(4, 2, D // NDEV, D)
        sent = _pair_start(dwo, dmix, f"rs_pair_start_w_out{l}")
        dxa, dz, ddt, dbias, dalog, ddsk, dsng = _ssd_bwd(sv["xact"], sv["P"], sv["sprev"], dcat, sm["bias"], sm["alog"],
                                                          sm["dskE"], sm["sng"] + sent[3][0:1, 0:1], f"ssd_b{l}")
        tok = reduce_start("w_out", l, sent, dxa)
        dxbc, dcw, dcb = _conv_bwd(sv["P"], dxa, convw8[l], sm["cb"] + tok[0:1, 0:1], f"conv_b{l}")
        dP = jnp.concatenate([du, dv, dq, dk, dvv, dz, dxbc, ddt, jnp.zeros((B, S, PW - OFF["dt"] - 128), BF16)],
                             axis=-1).reshape(T, PW)
        dwin = _mm(sv["h"].reshape(T, D), dP, mode="tn", name=f"proj_in_dw{l}", out_dtypes=(BF16,), tn=1536, tk=2048)
        sent = _sibling_start(dwin, dP, f"rs_sib_start_w_in{l}")
        dh = _mm(dP, win_g[l], mode="nt", name=f"proj_in_dx{l}", tk=2304, order=sent[3], out_dtypes=(BF16,)).reshape(B, S, D)
        s_in, land_in = _cols_to_my_shards(*_sibling_wait(*sent[:3], dh, f"rs_sib_wait_w_in{l}"), core, f"w_in_dshards{l}")
        tok = reduce_exchange("w_in", l, s_in, land_in, dh)
        nb = _norm_bwd(sv["x_in"], row(norm1_g[l]) + tok[0, 0], f"norm1_b{l}", sc=mods[l][1], dh=dh, dres=nb2["dx"],
                       br=saved[l - 1]["m2"] if l > 0 else None, gate=mods[l - 1][5] if l > 0 else None)
        dmod[l] = jnp.concatenate([nb["dsh"], nb["dsc"], nb2["dgate"], nb2["dsh"], nb2["dsc"], dg2], axis=-1)
        gconvw[l] = dcw[:4]
        gsm[l] = dict(
            ada_b=jnp.sum(dmod[l], axis=(0, 1)), norm1_g=nb["dg"], gm_ln_g=dlng, gm_ln_b=dlnb, gm_ws=dws,
            gm_bs=dbsx.reshape(128, GM_H, 128).sum(-1).T, gm_norm_g=dgog, attn_sinks=dsink[:, 0], attn_norm_g=daog,
            conv_b=dcb, dt_bias=dbias[0, :SSM_H], a_log=dalog[0, :SSM_H], d_skip=ddsk.reshape(SSM_H, SSM_HD).sum(-1),
            ssm_norm_g=dsng, norm2_g=nb2["dg"])
    grad_x = nb["dx"]

    big_res, after = dict.fromkeys(big), grad_x
    tile_rows = dict(w_in=256, w_out=256, w_mlp1=256, w_mlp2=128)

    def finish_reduce(n, l, sems, s_thru, land_thru, after):
        parts = _chipsum_wait(sems, s_thru, land_thru, after, f"rs_wait_{n}{l}")
        w = args[n]
        big_res[n] = _adamw(w.reshape(-1, w.shape[-1]), parts, args["m_" + n].reshape(-1, w.shape[-1]),
                            args["v_" + n].reshape(-1, w.shape[-1]), f"adamw_{n}{l}", tr=tile_rows[n], row0=l * w.shape[1],
                            prev=big_res[n])
        return big_res[n][0]

    per_layer = [n for n in _SMALL if n != "final_norm_g"]
    g_small = [jnp.stack([gsm[l][n].reshape(args[n].shape[1:]) for l in range(L)]) for n in per_layer] + [g_final.reshape(D)]
    zc = jnp.zeros((L, 4, CONV_CH), F32)
    z1 = jnp.zeros((1, 128), F32)
    gpack = _pack([loss_part] + g_small + [jnp.stack(gconvw)])
    small_zones = [_landing_zone(jnp.stack(dmod).reshape(1, L * B, 6 * D), 0, me1, "ag_zone_dmod", dtype=F32),
                   _landing_zone(gpack[None], 0, me1, "ag_zone_small", tr=gpack.shape[0], dtype=F32)]
    small_sems, small_thru, after = _gather_small_start(small_zones, grad_x, "ag_small_start")

    for item in reducing[:-1]:
        after = finish_reduce(*item, after)

    got = _gather_small_wait(small_sems, small_thru, after, "ag_small_wait")
    got = [got[0].reshape(NDEV, L, B, 6 * D), got[1]]
    like = [z1] + [args[n] for n in _SMALL] + [zc]
    packs = [_pack([z1] + [args[p + n] for n in _SMALL] + [zc]) for p in ("", "m_", "v_")]
    sres = [_unpack(p, like) for p in _adamw(packs[0], got[1], packs[1], packs[2], "adamw_small", tr=gpack.shape[0])]
    res = {n: [r[1 + i] for r in sres] for i, n in enumerate(_SMALL)}
    loss = sres[0][0][0, 0]
    gcw = lax.dynamic_slice_in_dim(sres[0][-1], me * (CONV_CH // NDEV), CONV_CH // NDEV, axis=2)

    def update(name, parts, tr):
        w = args[name]
        r = _adamw(w.reshape(-1, w.shape[-1]), parts, args["m_" + name].reshape(-1, w.shape[-1]),
                   args["v_" + name].reshape(-1, w.shape[-1]), "adamw_" + name, tr=tr)
        res[name] = [a.reshape(w.shape) for a in r]

    update("conv_w", gcw.reshape(1, L * 4, CONV_CH // NDEV), L * 4)

    dmod_all = jnp.transpose(got[0], (1, 0, 2, 3)).reshape(L, NDEV * B, 6 * D)
    dm_mine = lax.dynamic_slice_in_dim(dmod_all, me * (6 * D // NDEV), 6 * D // NDEV, axis=2)
    dm_pad = jnp.pad(dm_mine, ((0, 0), (0, 128 - nb_rows), (0, 0))).astype(BF16)
    g_adaw = jnp.stack([_mm(c_pad, dm_pad[l], mode="tn", name=f"ada_dw{l}", tn=768) for l in range(L)])
    update("ada_w", g_adaw.reshape(1, L * D, 6 * D // NDEV), 256)

    finish_reduce(*reducing[-1], res["ada_w"][0])
    for n in big:
        res[n] = [a.reshape(args[n].shape) for a in big_res[n]]

    names = ['ada_w', 'ada_b', 'norm1_g', 'w_in', 'gm_ln_g', 'gm_ln_b', 'gm_ws', 'gm_bs', 'gm_norm_g', 'attn_sinks',
             'attn_norm_g', 'conv_w', 'conv_b', 'dt_bias', 'a_log', 'd_skip', 'ssm_norm_g', 'w_out', 'norm2_g', 'w_mlp1',
             'w_mlp2', 'final_norm_g']
    return (loss, grad_x, *[res[n][0] for n in names], *[res[n][1] for n in names], *[res[n][2] for n in names],
            *[res[n][3] for n in names])
```

```python
import jax
import jax.numpy as jnp
import numpy as np
from jax import lax
from jax.experimental import pallas as pl
from jax.experimental.pallas import tpu as pltpu

F32, BF16 = jnp.float32, jnp.bfloat16
MESH = pl.DeviceIdType.MESH
NDEV = 8

D = 2048
DEPTH = 2
CHUNK = 128
GM_W, GM_H = 512, 4
ATT_W, KV_W, ATT_H = 512, 128, 8
SSM_W, SSM_H, SSM_HD, SSM_G = 1024, 16, 64, 2
CONV_CH = 1536
IN_W = 4368
DFF = 8192
EPS = 1e-6
NEG_INF = -1e30
GELU_K = 0.7978845608028654
GELU_C = 0.044715

_ORIG = (("u", 512), ("v", 512), ("q", 512), ("k", 128), ("vv", 128), ("z", 1024), ("xbc", 1536), ("dt", 16))
OFF = dict(u=0, v=512, q=1024, k=1536, vv=1664, z=1792, xbc=2816, dt=4352)
PW = 4608

ADAM_LR, ADAM_B1, ADAM_B2, ADAM_EPS, ADAM_WD, ADAM_STEP = 0.001, 0.9, 0.999, 1e-08, 0.01, 10


def _shards_to_cols(g, name, tr=256):
    n, R, C = g.shape

    def body(g_ref, o_ref):
        o_ref[...] = jnp.concatenate([g_ref[s] for s in range(n)] + [jnp.zeros((tr, PW - n * C), g.dtype)], axis=1)

    return pl.pallas_call(body, name=name, grid=(R // tr,), in_specs=[pl.BlockSpec((n, tr, C), lambda i: (0, i, 0))],
                          out_specs=pl.BlockSpec((tr, PW), lambda i: (i, 0)), out_shape=jax.ShapeDtypeStruct((R, PW), g.dtype))(g)


def _cols_to_my_shards(w, w_sib, core, name, tr=256):
    R, C = w.shape[0], IN_W // NDEV

    def body(core_ref, w_ref, s_ref, o_ref, o2_ref):
        x = w_ref[...].astype(F32) + s_ref[...].astype(F32)
        mine_is_odd = core_ref[0] == 1
        for q in range(4):
            blk = jnp.where(mine_is_odd, x[:, C * (2 * q + 1):C * (2 * q + 2)], x[:, C * 2 * q:C * (2 * q + 1)]).astype(o_ref.dtype)
            o_ref[q] = blk
            o2_ref[q] = blk

    row = pl.BlockSpec((tr, PW), lambda i, c: (i, 0))
    out = pl.BlockSpec((4, tr, C), lambda i, c: (0, i, 0))
    return pl.pallas_call(
        body, name=name, out_shape=[jax.ShapeDtypeStruct((4, R, C), w.dtype)] * 2,
        grid_spec=pltpu.PrefetchScalarGridSpec(num_scalar_prefetch=1, grid=(R // tr,), in_specs=[row, row], out_specs=[out, out]),
    )(core, w, w_sib)


def _sigmoid(x):
    return 0.5 * (jnp.tanh(0.5 * x) + 1.0)


def _gelu(x):
    return 0.5 * x * (1.0 + jnp.tanh(GELU_K * (x + GELU_C * x * x * x)))


def _gelu_grad(x):
    t = jnp.tanh(GELU_K * (x + GELU_C * x * x * x))
    return 0.5 * (1.0 + t) + 0.5 * x * (1.0 - t * t) * GELU_K * (1.0 + 3.0 * GELU_C * x * x)


def _dot(a, b, prec=None):
    return jnp.dot(a, b, precision=prec, preferred_element_type=F32)


def _dot_nt(a, b, prec=None):
    return lax.dot_general(a, b, (((1,), (1,)), ((), ())), precision=prec, preferred_element_type=F32)


def _dot_tn(a, b, prec=None):
    return lax.dot_general(a, b, (((0,), (0,)), ((), ())), precision=prec, preferred_element_type=F32)


def _full(shape):
    return pl.BlockSpec(shape, lambda *_: (0,) * len(shape))


_HBM = pl.BlockSpec(memory_space=pltpu.HBM)


def _me():
    return lax.axis_index("x"), lax.axis_index("y"), lax.axis_index("c")


def _peer(k):
    x, y, c = _me()
    px = 1 - x if k & 4 else x
    py = 1 - y if k & 2 else y
    pc = 1 - c if k & 1 else c
    return (px, py, pc), 4 * px + 2 * py + pc


def _gather_small(xs, name, order=()):
    n = len(xs)

    def body(*refs):
        ins, outs = refs[:n], refs[-n - 3:-3]
        send, recv, loc = refs[-3:]
        x, y, c = _me()
        me = 4 * x + 2 * y + c
        started = []
        for i in range(n):
            own = pltpu.make_async_copy(ins[i], outs[i].at[me], loc.at[i])
            own.start()
            started.append(own)
        for k in range(1, NDEV):
            dev, lin = _peer(k)
            for i in range(n):
                pltpu.make_async_remote_copy(
                    src_ref=ins[i], dst_ref=outs[i].at[me],
                    send_sem=send.at[i, k - 1], recv_sem=recv.at[i, k - 1], device_id=dev, device_id_type=MESH).start()
        for k in range(1, NDEV):
            dev, lin = _peer(k)
            for i in range(n):
                pltpu.make_async_remote_copy(
                    src_ref=ins[i], dst_ref=outs[i].at[lin],
                    send_sem=send.at[i, k - 1], recv_sem=recv.at[i, k - 1], device_id=dev, device_id_type=MESH).wait()
        for own in started:
            own.wait()

    extra = list(order)
    return pl.pallas_call(
        body, name=name, out_shape=[jax.ShapeDtypeStruct((NDEV,) + a.shape, a.dtype) for a in xs],
        in_specs=[_HBM] * n + [pl.BlockSpec(memory_space=pl.ANY)] * len(extra), out_specs=[_HBM] * n,
        scratch_shapes=[pltpu.SemaphoreType.DMA((n, NDEV - 1)), pltpu.SemaphoreType.DMA((n, NDEV - 1)),
                        pltpu.SemaphoreType.DMA((n,))],
        compiler_params=pltpu.CompilerParams(has_side_effects=True),
    )(*xs, *extra)


def _gather_small_start(lands, order, name):
    n = len(lands)

    def body(*refs):
        ins, sems, token = refs[:n], refs[n + 1:n + 1 + 14 * n], refs[-1]
        x, y, c = _me()
        me = 4 * x + 2 * y + c
        for i in range(n):
            for k in range(1, NDEV):
                dev, _ = _peer(k)
                pltpu.make_async_remote_copy(src_ref=ins[i].at[me], dst_ref=ins[i].at[me], send_sem=sems[14 * i + k - 1],
                                             recv_sem=sems[14 * i + 7 + k - 1], device_id=dev, device_id_type=MESH).start()
        token[...] = jnp.zeros_like(token)

    outs = pl.pallas_call(
        body, name=name,
        out_shape=(pltpu.SemaphoreType.DMA(()),) * (14 * n) + tuple(pltpu.HBM(a.shape, a.dtype) for a in lands)
        + (jax.ShapeDtypeStruct((8, 128), F32),),
        in_specs=(_HBM,) * n + (_ANY,), out_specs=(_SEM,) * (14 * n) + (_HBM,) * n + (pl.BlockSpec(memory_space=pltpu.VMEM),),
        input_output_aliases={i: 14 * n + i for i in range(n)}, compiler_params=pltpu.CompilerParams(has_side_effects=_DATAFLOW),
    )(*[_hbm(a) for a in lands], order)
    return outs[:14 * n], outs[14 * n:15 * n], outs[-1]


def _gather_small_wait(sems, lands_thru, after, name):
    n = len(lands_thru)

    def body(*refs):
        ins, sems_ = refs[:n], refs[n:n + 14 * n]
        x, y, c = _me()
        me = 4 * x + 2 * y + c
        for i in range(n):
            for k in range(1, NDEV):
                dev, lin = _peer(k)
                cp = pltpu.make_async_remote_copy(src_ref=ins[i].at[me], dst_ref=ins[i].at[lin], send_sem=sems_[14 * i + k - 1],
                                                  recv_sem=sems_[14 * i + 7 + k - 1], device_id=dev, device_id_type=MESH)
                cp.wait_send()
                cp.wait_recv()

    outs = pl.pallas_call(
        body, name=name, out_shape=tuple(pltpu.HBM(a.shape, a.dtype) for a in lands_thru),
        in_specs=(_HBM,) * n + (_SEM,) * (14 * n) + (_ANY,), out_specs=(_HBM,) * n,
        input_output_aliases={i: i for i in range(n)}, compiler_params=pltpu.CompilerParams(has_side_effects=_DATAFLOW),
    )(*lands_thru, *sems, after)
    return outs


def _chips():
    x, y, c = _me()
    return x, y, c, [(1 - x, y), (x, 1 - y), (1 - x, 1 - y)]


def _gather2(xs, name, order=None):
    n = len(xs)
    extra = [] if order is None else [order]

    def body(*refs):
        ins, outs = refs[:n], refs[-n - 3:-3]
        send, recv, loc = refs[-3:]
        x, y, c, chips = _chips()
        me, sib = (x, y, c), (x, y, 1 - c)

        def cp(i, k, block, to, src=None):
            slot = outs[i].at[4 * block[0] + 2 * block[1] + block[2]]
            return pltpu.make_async_remote_copy(src_ref=slot if src is None else src, dst_ref=slot, send_sem=send.at[i, k],
                                                recv_sem=recv.at[i, k], device_id=to, device_id_type=MESH)

        sent = []
        for i in range(n):
            for j, chip in enumerate(chips):
                sent.append(cp(i, 1 + j, me, (*chip, c), src=ins[i]))
            sent.append(cp(i, 0, me, sib, src=ins[i]))
        for s in sent:
            s.start()
        own = [pltpu.make_async_copy(ins[i], outs[i].at[4 * x + 2 * y + c], loc.at[i]) for i in range(n)]
        for o in own:
            o.start()
        for j, chip in enumerate(chips):
            for i in range(n):
                cp(i, 1 + j, (*chip, c), me).wait_recv()
                fwd = cp(i, 4 + j, (*chip, c), sib)
                fwd.start()
                sent.append(fwd)
        for i in range(n):
            cp(i, 0, sib, me).wait_recv()
            for j, chip in enumerate(chips):
                cp(i, 4 + j, (*chip, 1 - c), me).wait_recv()
        for s in sent:
            s.wait_send()
        for o in own:
            o.wait()

    return pl.pallas_call(
        body, name=name, out_shape=[jax.ShapeDtypeStruct((NDEV,) + a.shape, a.dtype) for a in xs],
        in_specs=[_HBM] * n + [pl.BlockSpec(memory_space=pl.ANY)] * len(extra), out_specs=[_HBM] * n,
        scratch_shapes=[pltpu.SemaphoreType.DMA((n, 7)), pltpu.SemaphoreType.DMA((n, 7)), pltpu.SemaphoreType.DMA((n,))],
        compiler_params=pltpu.CompilerParams(has_side_effects=True),
    )(*xs, *extra)


def _pair_add(p, r1, core, name, tr=256):
    _, _, R, C = p.shape
    tr = min(tr, R)

    def body(core_ref, p_ref, r_ref, o_ref, o2_ref):
        s = (p_ref[...].astype(F32) + r_ref[...].astype(F32)).astype(o_ref.dtype)
        o_ref[...] = s
        o2_ref[...] = s

    blk = pl.BlockSpec((None, tr, C), lambda ch, i, core_ref: (ch, i, 0))
    return pl.pallas_call(
        body, name=name, out_shape=[jax.ShapeDtypeStruct((4, R, C), p.dtype)] * 2,
        grid_spec=pltpu.PrefetchScalarGridSpec(
            num_scalar_prefetch=1, grid=(4, R // tr),
            in_specs=[pl.BlockSpec((None, None, tr, C), lambda ch, i, core_ref: (ch, core_ref[0], i, 0)), blk],
            out_specs=[blk, blk]),
    )(core, p, r1)


_SEM = pl.BlockSpec(memory_space=pltpu.SEMAPHORE)
_ANY = pl.BlockSpec(memory_space=pl.ANY)
_DATAFLOW = pltpu.SideEffectType.DATAFLOW_SIDE_EFFECTING


def _hbm(a):
    return pltpu.with_memory_space_constraint(a, pltpu.HBM)


def _gather_targets():
    x, y, c, chips = _chips()
    return 4 * x + 2 * y + c, [(x, y, 1 - c)] + [(*chip, c) for chip in chips]


def _landing_zone(w, l, me, name, tr=512, dtype=BF16):
    _, R, C = w.shape
    tr = min(tr, R)

    def body(me_ref, w_ref, o_ref):
        o_ref[...] = w_ref[...].astype(dtype)

    return pl.pallas_call(
        body, name=name, out_shape=jax.ShapeDtypeStruct((NDEV, R, C), dtype),
        grid_spec=pltpu.PrefetchScalarGridSpec(
            num_scalar_prefetch=1, grid=(R // tr,), in_specs=[pl.BlockSpec((None, tr, C), lambda i, me_ref: (l, i, 0))],
            out_specs=pl.BlockSpec((None, tr, C), lambda i, me_ref: (me_ref[0], i, 0))),
    )(me, w)


def _gather_start(land, order, name):
    def body(land_ref, order_ref, *rest):
        sems, token = rest[:8], rest[9]
        me, targets = _gather_targets()
        for k, to in enumerate(targets):
            pltpu.make_async_remote_copy(src_ref=land_ref.at[me], dst_ref=land_ref.at[me], send_sem=sems[k],
                                         recv_sem=sems[4 + k], device_id=to, device_id_type=MESH).start()
        token[...] = jnp.zeros_like(token)

    outs = pl.pallas_call(
        body, name=name,
        out_shape=(pltpu.SemaphoreType.DMA(()),) * 8 + (pltpu.HBM(land.shape, land.dtype), jax.ShapeDtypeStruct((8, 128), F32)),
        in_specs=(_HBM, _ANY), out_specs=(_SEM,) * 8 + (_HBM, pl.BlockSpec(memory_space=pltpu.VMEM)),
        input_output_aliases={0: 8}, compiler_params=pltpu.CompilerParams(has_side_effects=_DATAFLOW),
    )(_hbm(land), order)
    return outs[:8], outs[8], outs[9]


def _gather_wait(sems, land_thru, after, name):
    def body(land_ref, *rest):
        sems_ = rest[:8]
        me, targets = _gather_targets()
        for k, to in enumerate(targets):
            cp = pltpu.make_async_remote_copy(src_ref=land_ref.at[me], dst_ref=land_ref.at[me], send_sem=sems_[k],
                                              recv_sem=sems_[4 + k], device_id=to, device_id_type=MESH)
            cp.wait_send()
            cp.wait_recv()

    return pl.pallas_call(
        body, name=name, out_shape=pltpu.HBM(land_thru.shape, land_thru.dtype),
        in_specs=(_HBM,) + (_SEM,) * 8 + (_ANY,), out_specs=_HBM, input_output_aliases={0: 0},
        compiler_params=pltpu.CompilerParams(has_side_effects=_DATAFLOW),
    )(land_thru, *sems, after)


def _gather_finish(land, name):
    def body(land_ref, out, send, recv):
        x, y, c, chips = _chips()
        fwd = [pltpu.make_async_remote_copy(src_ref=out.at[4 * px + 2 * py + c], dst_ref=out.at[4 * px + 2 * py + c],
                                            send_sem=send.at[j], recv_sem=recv.at[j], device_id=(x, y, 1 - c), device_id_type=MESH)
               for j, (px, py) in enumerate(chips)]
        for cp in fwd:
            cp.start()
        for j, (px, py) in enumerate(chips):
            slot = out.at[4 * px + 2 * py + 1 - c]
            pltpu.make_async_remote_copy(src_ref=slot, dst_ref=slot, send_sem=send.at[j], recv_sem=recv.at[j],
                                         device_id=(x, y, 1 - c), device_id_type=MESH).wait()

    return pl.pallas_call(
        body, name=name, out_shape=jax.ShapeDtypeStruct(land.shape, land.dtype),
        in_specs=[_HBM], out_specs=_HBM, input_output_aliases={0: 0},
        scratch_shapes=[pltpu.SemaphoreType.DMA((3,)), pltpu.SemaphoreType.DMA((3,))],
        compiler_params=pltpu.CompilerParams(has_side_effects=True),
    )(land)


def _forward_start(land, order, name):
    def body(land_ref, order_ref, *rest):
        sems, token = rest[:6], rest[7]
        x, y, c, chips = _chips()
        for j, (px, py) in enumerate(chips):
            slot = land_ref.at[4 * px + 2 * py + c]
            pltpu.make_async_remote_copy(src_ref=slot, dst_ref=slot, send_sem=sems[j], recv_sem=sems[3 + j],
                                         device_id=(x, y, 1 - c), device_id_type=MESH).start()
        token[...] = jnp.zeros_like(token)

    outs = pl.pallas_call(
        body, name=name,
        out_shape=(pltpu.SemaphoreType.DMA(()),) * 6 + (pltpu.HBM(land.shape, land.dtype), jax.ShapeDtypeStruct((8, 128), F32)),
        in_specs=(_HBM, _ANY), out_specs=(_SEM,) * 6 + (_HBM, pl.BlockSpec(memory_space=pltpu.VMEM)),
        input_output_aliases={0: 6}, compiler_params=pltpu.CompilerParams(has_side_effects=_DATAFLOW),
    )(_hbm(land), order)
    return outs[:6], outs[6], outs[7]


def _forward_wait(sems, land_thru, after, name):
    def body(land_ref, *rest):
        sems_ = rest[:6]
        x, y, c, chips = _chips()
        for j, (px, py) in enumerate(chips):
            cp = pltpu.make_async_remote_copy(src_ref=land_ref.at[4 * px + 2 * py + c], dst_ref=land_ref.at[4 * px + 2 * py + 1 - c],
                                              send_sem=sems_[j], recv_sem=sems_[3 + j], device_id=(x, y, 1 - c),
                                              device_id_type=MESH)
            cp.wait_send()
            cp.wait_recv()

    return pl.pallas_call(
        body, name=name, out_shape=pltpu.HBM(land_thru.shape, land_thru.dtype),
        in_specs=(_HBM,) + (_SEM,) * 6 + (_ANY,), out_specs=_HBM, input_output_aliases={0: 0},
        compiler_params=pltpu.CompilerParams(has_side_effects=_DATAFLOW),
    )(land_thru, *sems, after)


def _chip_targets():
    x, y, c, chips = _chips()
    return 2 * x + y, [((px, py, c), 2 * px + py) for px, py in chips]


def _chipsum_start(s, land, order, name):
    def body(s_ref, land_ref, order_ref, *rest):
        sems, token = rest[:6], rest[8]
        mine, targets = _chip_targets()
        for k, (to, ch) in enumerate(targets):
            pltpu.make_async_remote_copy(src_ref=s_ref.at[ch], dst_ref=land_ref.at[mine], send_sem=sems[k], recv_sem=sems[3 + k],
                                         device_id=to, device_id_type=MESH).start()
        token[...] = jnp.zeros_like(token)

    outs = pl.pallas_call(
        body, name=name,
        out_shape=(pltpu.SemaphoreType.DMA(()),) * 6 + (pltpu.HBM(s.shape, s.dtype), pltpu.HBM(land.shape, land.dtype),
                                                        jax.ShapeDtypeStruct((8, 128), F32)),
        in_specs=(_HBM, _HBM, _ANY), out_specs=(_SEM,) * 6 + (_HBM, _HBM, pl.BlockSpec(memory_space=pltpu.VMEM)),
        input_output_aliases={0: 6, 1: 7}, compiler_params=pltpu.CompilerParams(has_side_effects=_DATAFLOW),
    )(_hbm(s), _hbm(land), order)
    return outs[:6], outs[6], outs[7], outs[8]


def _chipsum_wait(sems, s_thru, land_thru, after, name):
    def body(s_ref, land_ref, *rest):
        sems_ = rest[:6]
        mine, targets = _chip_targets()
        for k, (to, ch) in enumerate(targets):
            cp = pltpu.make_async_remote_copy(src_ref=s_ref.at[ch], dst_ref=land_ref.at[ch], send_sem=sems_[k], recv_sem=sems_[3 + k],
                                              device_id=to, device_id_type=MESH)
            cp.wait_send()
            cp.wait_recv()

    return pl.pallas_call(
        body, name=name, out_shape=(pltpu.HBM(s_thru.shape, s_thru.dtype), pltpu.HBM(land_thru.shape, land_thru.dtype)),
        in_specs=(_HBM, _HBM) + (_SEM,) * 6 + (_ANY,), out_specs=(_HBM, _HBM), input_output_aliases={0: 0, 1: 1},
        compiler_params=pltpu.CompilerParams(has_side_effects=_DATAFLOW),
    )(s_thru, land_thru, *sems, after)[1]


def _pair_start(p, order, name):
    def body(p_ref, land_ref, order_ref, *rest):
        sems, token = rest[:8], rest[10]
        x, y, c = _me()
        for ch in range(4):
            pltpu.make_async_remote_copy(src_ref=p_ref.at[ch, 1 - c], dst_ref=land_ref.at[ch], send_sem=sems[ch],
                                         recv_sem=sems[4 + ch], device_id=(x, y, 1 - c), device_id_type=MESH).start()
        token[...] = jnp.zeros_like(token)

    land = lax.empty((4,) + p.shape[2:], p.dtype)
    outs = pl.pallas_call(
        body, name=name,
        out_shape=(pltpu.SemaphoreType.DMA(()),) * 8 + (pltpu.HBM(p.shape, p.dtype), pltpu.HBM(land.shape, land.dtype),
                                                        jax.ShapeDtypeStruct((8, 128), F32)),
        in_specs=(_HBM, _HBM, _ANY), out_specs=(_SEM,) * 8 + (_HBM, _HBM, pl.BlockSpec(memory_space=pltpu.VMEM)),
        input_output_aliases={0: 8, 1: 9}, compiler_params=pltpu.CompilerParams(has_side_effects=_DATAFLOW),
    )(_hbm(p), _hbm(land), order)
    return outs[:8], outs[8], outs[9], outs[10]


def _pair_wait(sems, p_thru, land_thru, after, name):
    def body(p_ref, land_ref, *rest):
        sems_ = rest[:8]
        x, y, c = _me()
        for ch in range(4):
            cp = pltpu.make_async_remote_copy(src_ref=p_ref.at[ch, 1 - c], dst_ref=land_ref.at[ch], send_sem=sems_[ch],
                                              recv_sem=sems_[4 + ch], device_id=(x, y, 1 - c), device_id_type=MESH)
            cp.wait_send()
            cp.wait_recv()

    return pl.pallas_call(
        body, name=name, out_shape=(pltpu.HBM(p_thru.shape, p_thru.dtype), pltpu.HBM(land_thru.shape, land_thru.dtype)),
        in_specs=(_HBM, _HBM) + (_SEM,) * 8 + (_ANY,), out_specs=(_HBM, _HBM), input_output_aliases={0: 0, 1: 1},
        compiler_params=pltpu.CompilerParams(has_side_effects=_DATAFLOW),
    )(p_thru, land_thru, *sems, after)


def _sibling_start(p, order, name):
    def body(p_ref, land_ref, order_ref, send_sem, recv_sem, p_thru, land_thru, token):
        x, y, c = _me()
        pltpu.make_async_remote_copy(src_ref=p_ref, dst_ref=land_ref, send_sem=send_sem, recv_sem=recv_sem,
                                     device_id=(x, y, 1 - c), device_id_type=MESH).start()
        token[...] = jnp.zeros_like(token)

    land = lax.empty(p.shape, p.dtype)
    outs = pl.pallas_call(
        body, name=name,
        out_shape=(pltpu.SemaphoreType.DMA(()),) * 2 + (pltpu.HBM(p.shape, p.dtype), pltpu.HBM(p.shape, p.dtype),
                                                        jax.ShapeDtypeStruct((8, 128), F32)),
        in_specs=(_HBM, _HBM, _ANY), out_specs=(_SEM,) * 2 + (_HBM, _HBM, pl.BlockSpec(memory_space=pltpu.VMEM)),
        input_output_aliases={0: 2, 1: 3}, compiler_params=pltpu.CompilerParams(has_side_effects=_DATAFLOW),
    )(_hbm(p), _hbm(land), order)
    return outs[:2], outs[2], outs[3], outs[4]


def _sibling_wait(sems, p_thru, land_thru, after, name):
    def body(p_ref, land_ref, send_sem, recv_sem, after_ref, p_dead, got_ref):
        x, y, c = _me()
        cp = pltpu.make_async_remote_copy(src_ref=p_ref, dst_ref=land_ref, send_sem=send_sem, recv_sem=recv_sem,
                                          device_id=(x, y, 1 - c), device_id_type=MESH)
        cp.wait_send()
        cp.wait_recv()

    return pl.pallas_call(
        body, name=name, out_shape=(pltpu.HBM(p_thru.shape, p_thru.dtype), pltpu.HBM(land_thru.shape, land_thru.dtype)),
        in_specs=(_HBM, _HBM, _SEM, _SEM, _ANY), out_specs=(_HBM, _HBM), input_output_aliases={0: 0, 1: 1},
        compiler_params=pltpu.CompilerParams(has_side_effects=_DATAFLOW),
    )(p_thru, land_thru, *sems, after)


def _mm(a, b, *, mode, name, out_dtypes=(F32,), epilogue=None, extras=(), tm=1024, tn=1024, tk=2048,
        col_blocked_b=False, col_blocked_out=False, order=None):
    CB = 1024
    if col_blocked_b:
        assert mode in ("nn", "nt") and b.shape[2] == CB
        (M, K), N = a.shape, (b.shape[0] * CB if mode == "nn" else b.shape[1])
        assert mode == "nn" or tk % CB == 0
        tn = CB if mode == "nn" else tn
    elif mode == "nn":
        (M, K), N = a.shape, b.shape[1]
    elif mode == "nt":
        (M, K), N = a.shape, b.shape[0]
    else:
        (K, M), N = a.shape, b.shape[1]
    if col_blocked_out:
        assert len(out_dtypes) == 1 and N % CB == 0
        tn = CB
    tm, tn, tk = min(tm, M), min(tn, N), min(tk, K)
    assert M % tm == 0 and N % tn == 0 and K % tk == 0, (M, N, K, tm, tn, tk)
    nk = K // tk
    ne, no = len(extras), len(out_dtypes)
    dims = {"nn": (((1,), (0,)), ((), ())), "nt": (((1,), (1,)), ((), ())), "tn": (((0,), (0,)), ((), ()))}[mode]

    no_ = 0 if order is None else 1

    def body(a_ref, b_ref, *rest):
        rest = rest[no_:]
        ex, outs = rest[:ne], rest[ne:ne + no]

        def finish(acc):
            res = epilogue(acc, *[e[...] for e in ex]) if epilogue is not None else (acc,)
            for o, r in zip(outs, res):
                o[...] = r.astype(o.dtype)

        if col_blocked_b and mode == "nt":
            part = sum(lax.dot_general(a_ref[:, q * CB:(q + 1) * CB], b_ref[q], dims, preferred_element_type=F32)
                       for q in range(tk // CB))
        else:
            part = lax.dot_general(a_ref[...], b_ref[...], dims, preferred_element_type=F32)
        if nk == 1:
            finish(part)
        else:
            acc_ref = rest[-1]
            k = pl.program_id(2)

            @pl.when(k == 0)
            def _():
                acc_ref[...] = part

            @pl.when(k > 0)
            def _():
                acc_ref[...] += part

            @pl.when(k == nk - 1)
            def _():
                finish(acc_ref[...])

    a_spec = {"nn": pl.BlockSpec((tm, tk), lambda i, j, k: (i, k)), "nt": pl.BlockSpec((tm, tk), lambda i, j, k: (i, k)),
              "tn": pl.BlockSpec((tk, tm), lambda i, j, k: (k, i))}[mode]
    b_spec = {"nn": pl.BlockSpec((tk, tn), lambda i, j, k: (k, j)), "nt": pl.BlockSpec((tn, tk), lambda i, j, k: (j, k)),
              "tn": pl.BlockSpec((tk, tn), lambda i, j, k: (k, j))}[mode]
    if col_blocked_b:
        b_spec = (pl.BlockSpec((None, tk, CB), lambda i, j, k: (j, k, 0)) if mode == "nn"
                  else pl.BlockSpec((tk // CB, tn, CB), lambda i, j, k: (k, j, 0)))
    e_spec = pl.BlockSpec((tm, tn), lambda i, j, k: (i, j))
    o_spec, o_dims = e_spec, (M, N)
    if col_blocked_out:
        o_spec, o_dims = pl.BlockSpec((None, tm, CB), lambda i, j, k: (j, i, 0)), (N // CB, M, CB)
    outs = pl.pallas_call(
        body, name=name, grid=(M // tm, N // tn, nk),
        in_specs=[a_spec, b_spec] + [_ANY] * no_ + [e_spec] * ne, out_specs=[o_spec] * no,
        out_shape=[jax.ShapeDtypeStruct(o_dims, dt) for dt in out_dtypes],
        scratch_shapes=[pltpu.VMEM((tm, tn), F32)] if nk > 1 else [],
        compiler_params=pltpu.CompilerParams(dimension_semantics=("parallel", "parallel", "arbitrary")),
    )(a, b, *([] if order is None else [order]), *extras)
    return outs if no > 1 else outs[0]


def _dw_half(a, b, side, *, axis, name, add=None, order=None, tile=1024, tk=2048):
    (K, M), N = a.shape, b.shape[1]
    tk = min(tk, K)
    nk = K // tk
    if axis == "m":
        tm, tn = tile, min(N, 1024)
        grid, o_dims = (4, N // tn, nk), (4, tile, N)
        a_spec = pl.BlockSpec((tk, tm), lambda q, j, k, s: (k, 2 * q + s[0]))
        b_spec = pl.BlockSpec((tk, tn), lambda q, j, k, s: (k, j))
        o_spec = pl.BlockSpec((None, tm, tn), lambda q, j, k, s: (q, 0, j))
    else:
        tm, tn = min(M, 1024), tile
        grid, o_dims = (M // tm, 4, nk), (4, M, tile)
        a_spec = pl.BlockSpec((tk, tm), lambda i, q, k, s: (k, i))
        b_spec = pl.BlockSpec((tk, tn), lambda i, q, k, s: (k, 2 * q + s[0]))
        o_spec = pl.BlockSpec((None, tm, tn), lambda i, q, k, s: (q, i, 0))
    n_order, n_add = int(order is not None), int(add is not None)
    n_out = 1 + n_add

    def body(s_ref, a_ref, b_ref, *rest):
        rest = rest[n_order:]
        outs, acc_ref = rest[n_add:n_add + n_out], rest[-1]
        k = pl.program_id(2)
        part = _dot_tn(a_ref[...], b_ref[...])

        @pl.when(k == 0)
        def _():
            acc_ref[...] = part

        @pl.when(k > 0)
        def _():
            acc_ref[...] += part

        @pl.when(k == nk - 1)
        def _():
            res = acc_ref[...] + rest[0][...].astype(F32) if n_add else acc_ref[...]
            for o in outs:
                o[...] = res.astype(o.dtype)

    outs = pl.pallas_call(
        body, name=name, out_shape=[jax.ShapeDtypeStruct(o_dims, BF16)] * n_out,
        grid_spec=pltpu.PrefetchScalarGridSpec(
            num_scalar_prefetch=1, grid=grid, in_specs=[a_spec, b_spec] + [_ANY] * n_order + [o_spec] * n_add,
            out_specs=[o_spec] * n_out, scratch_shapes=[pltpu.VMEM((tm, tn), F32)]),
        compiler_params=pltpu.CompilerParams(dimension_semantics=("arbitrary", "arbitrary", "arbitrary")),
    )(side, a, b, *([order] if n_order else []), *([add] if n_add else []))
    return outs if n_add else outs[0]


def _norm_fwd(x, g, sc, sh, name, resid=None):
    B, S, Dm = x.shape
    ts = min(S, 256)
    tok = pl.BlockSpec((None, ts, Dm), lambda b, i: (b, i, 0))
    row = pl.BlockSpec((None, 1, Dm), lambda b, i: (b, 0, 0))
    par = pl.BlockSpec((1, Dm), lambda b, i: (0, 0))

    def body(*refs):
        if resid is not None:
            x_ref, br_ref, gt_ref, g_ref, sc_ref, sh_ref, xo_ref, h_ref = refs
            xv = x_ref[...] + gt_ref[...] * br_ref[...]
            xo_ref[...] = xv
        else:
            x_ref, g_ref, sc_ref, sh_ref, h_ref = refs
            xv = x_ref[...]
        r = lax.rsqrt(jnp.mean(xv * xv, axis=-1, keepdims=True) + EPS)
        h_ref[...] = ((xv * r * g_ref[...]) * (1.0 + sc_ref[...]) + sh_ref[...]).astype(BF16)

    h_shape = jax.ShapeDtypeStruct((B, S, Dm), BF16)
    if resid is not None:
        return pl.pallas_call(body, name=name, grid=(B, S // ts), in_specs=[tok, tok, row, par, row, row],
                              out_specs=[tok, tok], out_shape=[jax.ShapeDtypeStruct((B, S, Dm), F32), h_shape],
                              )(x, resid[0], resid[1], g, sc, sh)
    return pl.pallas_call(body, name=name, grid=(B, S // ts), in_specs=[tok, par, row, row], out_specs=tok,
                          out_shape=h_shape)(x, g, sc, sh)


def _norm_bwd(x, g, name, *, sc=None, dh=None, dres=None, tgt=None, br=None, gate=None, x_is_prev=False):
    B, S, Dm = x.shape
    ts = min(S, 256)
    final = tgt is not None
    has_br = br is not None
    tok = pl.BlockSpec((None, ts, Dm), lambda b, i: (b, i, 0))
    row = pl.BlockSpec((None, 1, Dm), lambda b, i: (b, 0, 0))
    par = pl.BlockSpec((1, Dm), lambda b, i: (0, 0))
    ins, in_specs = [x, g], [tok, par]
    if final:
        ins, in_specs = ins + [tgt], in_specs + [tok]
    else:
        ins, in_specs = ins + [sc, dh], in_specs + [row, tok]
    if dres is not None:
        ins, in_specs = ins + [dres], in_specs + [tok]
    if has_br:
        ins, in_specs = ins + [br, gate], in_specs + [tok, row]
    n_in = len(ins)
    out_shape = [jax.ShapeDtypeStruct((B, S, Dm), F32), jax.ShapeDtypeStruct((1, Dm), F32)]
    out_specs = [tok, par]
    if final:
        out_shape.append(jax.ShapeDtypeStruct((1, 128), F32))
        out_specs.append(pl.BlockSpec((1, 128), lambda b, i: (0, 0)))
    else:
        out_shape += [jax.ShapeDtypeStruct((B, 1, Dm), F32)] * 2
        out_specs += [row, row]
    if has_br:
        out_shape += [jax.ShapeDtypeStruct((B, S, Dm), BF16), jax.ShapeDtypeStruct((B, 1, Dm), F32)]
        out_specs += [tok, row]

    def body(*refs):
        it = iter(refs[:n_in])
        outs = iter(refs[n_in:])
        x_ref, g_ref = next(it), next(it)
        b, i = pl.program_id(0), pl.program_id(1)
        first, first_row = (b == 0) & (i == 0), i == 0
        xv, gv = x_ref[...], g_ref[...]
        if x_is_prev:
            xv = xv + refs[n_in - 1][...] * refs[n_in - 2][...]
        r = lax.rsqrt(jnp.mean(xv * xv, axis=-1, keepdims=True) + EPS)
        n = xv * r
        dx_ref, dg_ref = next(outs), next(outs)

        def acc(ref, val, init):
            @pl.when(init)
            def _():
                ref[...] = val

            @pl.when(jnp.logical_not(init))
            def _():
                ref[...] += val

        if final:
            t_ref = next(it)
            loss_ref = next(outs)
            e = n * gv - t_ref[...]
            acc(loss_ref, jnp.zeros((1, 128), F32) + 0.5 * jnp.sum(e * e) / Dm, first)
            dyg = e * (1.0 / Dm)
        else:
            sc_ref, dh_ref = next(it), next(it)
            dsc_ref, dsh_ref = next(outs), next(outs)
            dhv = dh_ref[...].astype(F32)
            acc(dsh_ref, jnp.sum(dhv, axis=0, keepdims=True), first_row)
            acc(dsc_ref, jnp.sum(dhv * (n * gv), axis=0, keepdims=True), first_row)
            dyg = dhv * (1.0 + sc_ref[...])
        acc(dg_ref, jnp.sum(dyg * n, axis=0, keepdims=True), first)
        dn = dyg * gv
        dx = r * (dn - n * jnp.mean(dn * n, axis=-1, keepdims=True))
        if dres is not None:
            dx = dx + next(it)[...]
        dx_ref[...] = dx
        if has_br:
            br_ref, gt_ref = next(it), next(it)
            dbr_ref, dgt_ref = next(outs), next(outs)
            dbr_ref[...] = (dx * gt_ref[...]).astype(BF16)
            acc(dgt_ref, jnp.sum(dx * br_ref[...], axis=0, keepdims=True), first_row)

    outs = pl.pallas_call(body, name=name, grid=(B, S // ts), in_specs=in_specs, out_specs=out_specs, out_shape=out_shape,
                          compiler_params=pltpu.CompilerParams(dimension_semantics=("arbitrary", "arbitrary")))(*ins)
    res = dict(dx=outs[0], dg=outs[1])
    if final:
        res["loss"] = outs[2]
    else:
        res["dsc"], res["dsh"] = outs[2], outs[3]
    if has_br:
        res["dbr"], res["dgate"] = outs[-2], outs[-1]
    return res


def _gm_heads(vg, lng, lnb):
    res = []
    for h in range(GM_H):
        sl = slice(h * 128, (h + 1) * 128)
        vh = vg[:, sl]
        xc = vh - jnp.mean(vh, axis=-1, keepdims=True)
        rstd = lax.rsqrt(jnp.mean(xc * xc, axis=-1, keepdims=True) + 1e-5)
        xhat = xc * rstd
        res.append((xhat, rstd, xhat * lng[:, sl] + lnb[:, sl]))
    return res


def _gm_gate(heads, wt_ref, bsx, nch):
    cols = []
    for h in range(GM_H):
        vn = heads[h][2].astype(BF16)
        rows = [_dot(wt_ref[h], vn[c * CHUNK:(c + 1) * CHUNK]) + bsx[:, h * 128:(h + 1) * 128] for c in range(nch)]
        cols.append(jnp.concatenate(rows, axis=0) if nch > 1 else rows[0])
    return jnp.concatenate(cols, axis=1)


def _gm_specs(S):
    tb = min(S, 512)
    u = pl.BlockSpec((None, tb, GM_W), lambda b, i: (b, i, OFF["u"] // GM_W))
    v = pl.BlockSpec((None, tb, GM_W), lambda b, i: (b, i, OFF["v"] // GM_W))
    tok = pl.BlockSpec((None, tb, GM_W), lambda b, i: (b, i, 0))
    return tb, u, v, tok


def _gmlp_fwd(P, lng, lnb, wt, bsx, og, name):
    B, S, _ = P.shape
    tb, u_spec, v_spec, tok = _gm_specs(S)
    nch = tb // CHUNK

    def body(u_ref, v_ref, lng_ref, lnb_ref, wt_ref, bsx_ref, og_ref, o_ref):
        heads = _gm_heads(_gelu(v_ref[...]), lng_ref[...], lnb_ref[...])
        y = _gelu(u_ref[...]) * _gm_gate(heads, wt_ref, bsx_ref[...], nch)
        r = lax.rsqrt(jnp.mean(y * y, axis=-1, keepdims=True) + EPS)
        o_ref[...] = (y * r * og_ref[...]).astype(BF16)

    return pl.pallas_call(
        body, name=name, grid=(B, S // tb),
        in_specs=[u_spec, v_spec, _full((1, GM_W)), _full((1, GM_W)), _full((GM_H, 128, 128)), _full((128, GM_W)), _full((1, GM_W))],
        out_specs=tok, out_shape=jax.ShapeDtypeStruct((B, S, GM_W + ATT_W + SSM_W), BF16))(P, P, lng, lnb, wt, bsx, og)


def _gmlp_bwd(P, dcat, lng, lnb, wt, wtT, bsx, og, name):
    B, S, _ = P.shape
    tb, u_spec, v_spec, tok = _gm_specs(S)
    nch = tb // CHUNK
    do_spec = pl.BlockSpec((None, tb, GM_W), lambda b, i: (b, i, 0))

    def body(u_ref, v_ref, do_ref, lng_ref, lnb_ref, wt_ref, wtT_ref, bsx_ref, og_ref,
             du_ref, dv_ref, dlng_ref, dlnb_ref, dws_ref, dbsx_ref, dog_ref):
        first = (pl.program_id(0) == 0) & (pl.program_id(1) == 0)

        @pl.when(first)
        def _():
            for ref in (dlng_ref, dlnb_ref, dws_ref, dbsx_ref, dog_ref):
                ref[...] = jnp.zeros(ref.shape, F32)

        u, v, lng = u_ref[...], v_ref[...], lng_ref[...]
        ug = _gelu(u)
        heads = _gm_heads(_gelu(v), lng, lnb_ref[...])
        gate = _gm_gate(heads, wt_ref, bsx_ref[...], nch)
        y = ug * gate
        r = lax.rsqrt(jnp.mean(y * y, axis=-1, keepdims=True) + EPS)
        yn = y * r
        dout = do_ref[...].astype(F32)
        dog_ref[...] += jnp.sum(dout * yn, axis=0, keepdims=True)
        dyn = dout * og_ref[...]
        dy = r * (dyn - yn * jnp.mean(dyn * yn, axis=-1, keepdims=True))
        du_ref[...] = (dy * gate * _gelu_grad(u)).astype(BF16)
        dgate = dy * ug
        tril = lax.broadcasted_iota(jnp.int32, (128, 128), 0) >= lax.broadcasted_iota(jnp.int32, (128, 128), 1)
        dvg = []
        for h in range(GM_H):
            sl = slice(h * 128, (h + 1) * 128)
            xhat, rstd, vn = heads[h]
            vnb = vn.astype(BF16)
            dgh = dgate[:, sl]
            dgb = dgh.astype(BF16)
            dbs = jnp.zeros((128, 128), F32)
            dw = jnp.zeros((128, 128), F32)
            dvn = []
            for c in range(nch):
                rs = slice(c * CHUNK, (c + 1) * CHUNK)
                dbs = dbs + dgh[rs]
                dw = dw + _dot_nt(dgb[rs], vnb[rs])
                dvn.append(_dot(wtT_ref[h], dgb[rs]))
            dvn = jnp.concatenate(dvn, axis=0) if nch > 1 else dvn[0]
            dbsx_ref[:, sl] += dbs
            dws_ref[h] += jnp.where(tril, dw, 0.0)
            dlng_ref[:, sl] += jnp.sum(dvn * xhat, axis=0, keepdims=True)
            dlnb_ref[:, sl] += jnp.sum(dvn, axis=0, keepdims=True)
            dxh = dvn * lng[:, sl]
            dvg.append(rstd * (dxh - jnp.mean(dxh, axis=-1, keepdims=True) - xhat * jnp.mean(dxh * xhat, axis=-1, keepdims=True)))
        dv_ref[...] = (jnp.concatenate(dvg, axis=1) * _gelu_grad(v)).astype(BF16)

    p512, w3 = _full((1, GM_W)), _full((GM_H, 128, 128))
    return pl.pallas_call(
        body, name=name, grid=(B, S // tb),
        in_specs=[u_spec, v_spec, do_spec, p512, p512, w3, w3, _full((128, GM_W)), p512],
        out_specs=[tok, tok, p512, p512, w3, _full((128, GM_W)), p512],
        out_shape=[jax.ShapeDtypeStruct((B, S, GM_W), BF16)] * 2 + [
            jax.ShapeDtypeStruct((1, GM_W), F32), jax.ShapeDtypeStruct((1, GM_W), F32),
            jax.ShapeDtypeStruct((GM_H, 128, 128), F32), jax.ShapeDtypeStruct((128, GM_W), F32),
            jax.ShapeDtypeStruct((1, GM_W), F32)],
        compiler_params=pltpu.CompilerParams(dimension_semantics=("arbitrary", "arbitrary")),
    )(P, P, dcat, lng, lnb, wt, wtT, bsx, og)


def _lane_half():
    return lax.broadcasted_iota(jnp.int32, (128, 128), 1) // 64


def _att_stack(x, kvh, dtype):
    half = _lane_half()
    rows = []
    for g in range(4):
        i = kvh * 4 + g
        pair = x[:, (i // 2) * 128:(i // 2 + 1) * 128]
        if i % 2 != kvh:
            pair = pltpu.roll(pair, 64, 1)
        rows.append(jnp.where(half == kvh, pair, 0.0))
    return jnp.concatenate(rows, axis=0).astype(dtype)


def _att_unstack(pairs, y, kvh):
    half = _lane_half()
    for g in range(4):
        i = kvh * 4 + g
        piece = y[g * 128:(g + 1) * 128]
        if i % 2 != kvh:
            piece = pltpu.roll(piece, 64, 1)
        pairs[i // 2] = jnp.where(half == i % 2, piece, pairs[i // 2])
    return pairs


def _att_fill_bias(bias_ref):
    qi = lax.broadcasted_iota(jnp.int32, (512, 256), 0) % 128
    kj = lax.broadcasted_iota(jnp.int32, (512, 256), 1)
    diff = qi + 128 - kj
    band = (diff >= 0) & (diff < 128)
    bias_ref[0:512, :] = jnp.where(band, 0.0, NEG_INF)
    bias_ref[512:1024, :] = jnp.where(band & (kj >= 128), 0.0, NEG_INF)


def _att_bias(bias_ref, n):
    return bias_ref[pl.ds(pl.multiple_of(jnp.where(n == 0, 512, 0), 512), 512), :]


def _att_probs(qb, k2, bias, sink_ref, kvh):
    qm = _att_stack(qb, kvh, BF16)
    s = _dot_nt(qm, k2) * (64 ** -0.5) + bias
    grp = lax.broadcasted_iota(jnp.int32, (512, 1), 0) // 128
    sink = jnp.zeros((512, 1), F32)
    for g in range(4):
        sink = jnp.where(grp == g, sink_ref[kvh * 4 + g], sink)
    m = jnp.maximum(jnp.max(s, axis=-1, keepdims=True), sink)
    e = jnp.exp(s - m)
    esink = jnp.exp(sink - m)
    inv = 1.0 / (jnp.sum(e, axis=-1, keepdims=True) + esink)
    return qm, e * inv, esink * inv


def _att_specs(S):
    q = pl.BlockSpec((None, S, ATT_W), lambda b: (b, 0, OFF["q"] // ATT_W))
    k = pl.BlockSpec((None, S, KV_W), lambda b: (b, 0, OFF["k"] // KV_W))
    v = pl.BlockSpec((None, S, KV_W), lambda b: (b, 0, OFF["vv"] // KV_W))
    tok = pl.BlockSpec((None, S, ATT_W), lambda b: (b, 0, 0))
    kv = pl.BlockSpec((None, S, KV_W), lambda b: (b, 0, 0))
    return q, k, v, tok, kv


_SMEM = pl.BlockSpec(memory_space=pltpu.SMEM)


def _attn_fwd(P, sinks, og, cat, name):
    B, S, _ = P.shape
    q_spec, k_spec, v_spec, _, _ = _att_specs(S)
    tok = pl.BlockSpec((None, S, ATT_W), lambda b: (b, 0, GM_W // ATT_W))

    def body(q_ref, k_ref, v_ref, sink_ref, og_ref, cat_ref, o_ref, kpad, vpad, bias_ref):
        _att_fill_bias(bias_ref)
        kpad[0:128, :] = jnp.zeros((128, KV_W), BF16)
        vpad[0:128, :] = jnp.zeros((128, KV_W), BF16)
        kpad[128:, :] = k_ref[...].astype(BF16)
        vpad[128:, :] = v_ref[...].astype(BF16)

        def step(n, carry):
            st = pl.multiple_of(n * 128, 128)
            qb = q_ref[pl.ds(st, 128), :]
            k2, v2 = kpad[pl.ds(st, 256), :], vpad[pl.ds(st, 256), :]
            pairs = [jnp.zeros((128, 128), F32)] * 4
            bias = _att_bias(bias_ref, n)
            for kvh in range(2):
                _, p, _ = _att_probs(qb, k2, bias, sink_ref, kvh)
                pairs = _att_unstack(pairs, _dot(p.astype(BF16), v2), kvh)
            o = jnp.concatenate(pairs, axis=1)
            r = lax.rsqrt(jnp.mean(o * o, axis=-1, keepdims=True) + EPS)
            o_ref[pl.ds(st, 128), :] = (o * r * og_ref[...]).astype(BF16)
            return carry

        lax.fori_loop(0, S // 128, step, 0)

    return pl.pallas_call(
        body, name=name, grid=(B,), in_specs=[q_spec, k_spec, v_spec, _SMEM, _full((1, ATT_W)), _ANY], out_specs=tok,
        out_shape=jax.ShapeDtypeStruct(cat.shape, BF16), input_output_aliases={5: 0},
        scratch_shapes=[pltpu.VMEM((S + 128, KV_W), BF16)] * 2 + [pltpu.VMEM((1024, 256), F32)])(P, P, P, sinks, og, cat)


def _attn_bwd(P, dcat, sinks, og, name):
    B, S, _ = P.shape
    q_spec, k_spec, v_spec, tok, kv = _att_specs(S)
    do_spec = pl.BlockSpec((None, S, ATT_W), lambda b: (b, 0, GM_W // ATT_W))

    def body(q_ref, k_ref, v_ref, do_ref, sink_ref, og_ref, dq_ref, dk_ref, dv_ref, dsink_ref, dog_ref,
             kpad, vpad, dkpad, dvpad, bias_ref):
        _att_fill_bias(bias_ref)

        @pl.when(pl.program_id(0) == 0)
        def _():
            dsink_ref[...] = jnp.zeros((8, 128), F32)
            dog_ref[...] = jnp.zeros((1, ATT_W), F32)

        kpad[0:128, :] = jnp.zeros((128, KV_W), BF16)
        vpad[0:128, :] = jnp.zeros((128, KV_W), BF16)
        kpad[128:, :] = k_ref[...].astype(BF16)
        vpad[128:, :] = v_ref[...].astype(BF16)
        dkpad[...] = jnp.zeros((S + 128, KV_W), F32)
        dvpad[...] = jnp.zeros((S + 128, KV_W), F32)
        half = _lane_half()
        head_row = lax.broadcasted_iota(jnp.int32, (8, 128), 0)

        def step(n, carry):
            st = pl.multiple_of(n * 128, 128)
            qb = q_ref[pl.ds(st, 128), :]
            k2, v2 = kpad[pl.ds(st, 256), :], vpad[pl.ds(st, 256), :]
            saved, pairs = [], [jnp.zeros((128, 128), F32)] * 4
            bias = _att_bias(bias_ref, n)
            for kvh in range(2):
                qm, p, psink = _att_probs(qb, k2, bias, sink_ref, kvh)
                o = _dot(p.astype(BF16), v2)
                saved.append((qm, p, psink, o))
                pairs = _att_unstack(pairs, o, kvh)
            o = jnp.concatenate(pairs, axis=1)
            r = lax.rsqrt(jnp.mean(o * o, axis=-1, keepdims=True) + EPS)
            on = o * r
            dout = do_ref[pl.ds(st, 128), :].astype(F32)
            dog_ref[...] += jnp.sum(dout * on, axis=0, keepdims=True)
            dyn = dout * og_ref[...]
            do = r * (dyn - on * jnp.mean(dyn * on, axis=-1, keepdims=True))
            dq_pairs = [jnp.zeros((128, 128), F32)] * 4
            dsink = jnp.zeros((8, 128), F32)
            for kvh in range(2):
                qm, p, psink, og_ = saved[kvh]
                dog = _att_stack(do, kvh, F32)
                delta = jnp.sum(dog * jnp.where(jnp.concatenate([half] * 4, axis=0) == kvh, og_, 0.0), axis=-1, keepdims=True)
                dogb, pb = dog.astype(BF16), p.astype(BF16)
                dvpad[pl.ds(st, 256), :] += _dot_tn(pb, dogb)
                dp = _dot_nt(dogb, v2)
                ds = (p * (dp - delta) * (64 ** -0.5)).astype(BF16)
                sd = psink * delta
                for g in range(4):
                    dsink = dsink - jnp.where(head_row == kvh * 4 + g, jnp.sum(sd[g * 128:(g + 1) * 128]), 0.0)
                dq_pairs = _att_unstack(dq_pairs, _dot(ds, k2), kvh)
                dkpad[pl.ds(st, 256), :] += _dot_tn(ds, qm)
            dsink_ref[...] += dsink
            dq_ref[pl.ds(st, 128), :] = jnp.concatenate(dq_pairs, axis=1).astype(BF16)
            return carry

        lax.fori_loop(0, S // 128, step, 0)
        dk_ref[...] = dkpad[128:, :].astype(BF16)
        dv_ref[...] = dvpad[128:, :].astype(BF16)

    return pl.pallas_call(
        body, name=name, grid=(B,),
        in_specs=[q_spec, k_spec, v_spec, do_spec, _SMEM, _full((1, ATT_W))],
        out_specs=[tok, kv, kv, _full((8, 128)), _full((1, ATT_W))],
        out_shape=[jax.ShapeDtypeStruct((B, S, ATT_W), BF16), jax.ShapeDtypeStruct((B, S, KV_W), BF16),
                   jax.ShapeDtypeStruct((B, S, KV_W), BF16), jax.ShapeDtypeStruct((8, 128), F32),
                   jax.ShapeDtypeStruct((1, ATT_W), F32)],
        scratch_shapes=[pltpu.VMEM((S + 128, KV_W), BF16)] * 2 + [pltpu.VMEM((S + 128, KV_W), F32)] * 2
        + [pltpu.VMEM((1024, 256), F32)],
        compiler_params=pltpu.CompilerParams(dimension_semantics=("arbitrary",)),
    )(P, P, P, dcat, sinks, og)


CONV_TC = 256
CONV_RC = 64


def _conv_taps(ext, r0):
    return [ext[pl.ds(r0 + 8 - k, CONV_RC), :] for k in range(4)]


def _conv_pre(taps, w_ref, b_ref):
    acc = b_ref[...] + w_ref[3:4, :] * taps[0]
    for k in range(1, 4):
        acc = acc + w_ref[3 - k:4 - k, :] * taps[k]
    return acc


def _conv_fwd(P, w8, b, name):
    B, S, _ = P.shape
    nj = CONV_CH // CONV_TC
    x_spec = pl.BlockSpec((None, S, CONV_TC), lambda b_, j: (b_, 0, OFF["xbc"] // CONV_TC + j))
    tok = pl.BlockSpec((None, S, CONV_TC), lambda b_, j: (b_, 0, j))

    def body(x_ref, w_ref, b_ref, o_ref, ext):
        ext[0:8, :] = jnp.zeros((8, CONV_TC), F32)
        ext[8:, :] = x_ref[...]
        for r0 in range(0, S, CONV_RC):
            pre = _conv_pre(_conv_taps(ext, r0), w_ref, b_ref)
            o_ref[pl.ds(r0, CONV_RC), :] = pre * _sigmoid(pre)

    return pl.pallas_call(
        body, name=name, grid=(B, nj),
        in_specs=[x_spec, pl.BlockSpec((8, CONV_TC), lambda b_, j: (0, j)), pl.BlockSpec((1, CONV_TC), lambda b_, j: (0, j))],
        out_specs=tok, out_shape=jax.ShapeDtypeStruct((B, S, CONV_CH), F32),
        scratch_shapes=[pltpu.VMEM((S + 8, CONV_TC), F32)])(P, w8, b)


def _conv_bwd(P, dact, w8, b, name):
    B, S, _ = P.shape
    nj = CONV_CH // CONV_TC
    x_spec = pl.BlockSpec((None, S, CONV_TC), lambda j, b_: (b_, 0, OFF["xbc"] // CONV_TC + j))
    tok = pl.BlockSpec((None, S, CONV_TC), lambda j, b_: (b_, 0, j))
    w_spec = pl.BlockSpec((8, CONV_TC), lambda j, b_: (0, j))
    b_spec = pl.BlockSpec((1, CONV_TC), lambda j, b_: (0, j))

    def body(x_ref, d_ref, w_ref, b_ref, dx_ref, dw_ref, db_ref, ext, extd):
        @pl.when(pl.program_id(1) == 0)
        def _():
            dw_ref[...] = jnp.zeros((8, CONV_TC), F32)
            db_ref[...] = jnp.zeros((1, CONV_TC), F32)

        ext[0:8, :] = jnp.zeros((8, CONV_TC), F32)
        ext[8:, :] = x_ref[...]
        extd[pl.ds(8 + S, 8), :] = jnp.zeros((8, CONV_TC), F32)
        db = jnp.zeros((1, CONV_TC), F32)
        dws = [jnp.zeros((1, CONV_TC), F32)] * 4
        for r0 in range(0, S, CONV_RC):
            taps = _conv_taps(ext, r0)
            pre = _conv_pre(taps, w_ref, b_ref)
            sg = _sigmoid(pre)
            dpre = d_ref[pl.ds(r0, CONV_RC), :] * (sg * (1.0 + pre * (1.0 - sg)))
            extd[pl.ds(8 + r0, CONV_RC), :] = dpre
            db = db + jnp.sum(dpre, axis=0, keepdims=True)
            dws = [dws[i] + jnp.sum(dpre * taps[3 - i], axis=0, keepdims=True) for i in range(4)]
        for r0 in range(0, S, CONV_RC):
            dx = w_ref[3:4, :] * extd[pl.ds(8 + r0, CONV_RC), :]
            for k in range(1, 4):
                dx = dx + w_ref[3 - k:4 - k, :] * extd[pl.ds(8 + r0 + k, CONV_RC), :]
            dx_ref[pl.ds(r0, CONV_RC), :] = dx.astype(BF16)
        db_ref[...] += db
        sub = lax.broadcasted_iota(jnp.int32, (8, CONV_TC), 0)
        dw_ref[...] += sum(jnp.where(sub == i, dws[i], 0.0) for i in range(4))

    return pl.pallas_call(
        body, name=name, grid=(nj, B), in_specs=[x_spec, tok, w_spec, b_spec], out_specs=[tok, w_spec, b_spec],
        out_shape=[jax.ShapeDtypeStruct((B, S, CONV_CH), BF16), jax.ShapeDtypeStruct((8, CONV_CH), F32),
                   jax.ShapeDtypeStruct((1, CONV_CH), F32)],
        scratch_shapes=[pltpu.VMEM((S + 8, CONV_TC), F32), pltpu.VMEM((S + 16, CONV_TC), F32)],
        compiler_params=pltpu.CompilerParams(dimension_semantics=("arbitrary", "arbitrary")),
    )(P, dact, w8, b)


def _ssd_consts():
    hd = np.arange(SSM_W) // SSM_HD
    E = (np.arange(128)[:, None] == hd[None, :]).astype(np.float32)
    tri = (np.arange(128)[:, None] >= np.arange(128)[None, :]).astype(np.float32)
    return jnp.asarray(E, BF16), jnp.asarray(E.T, BF16), jnp.asarray(tri, BF16), jnp.asarray(tri.T, BF16)


def _pieces(x, n):
    out, r = [], x
    for _ in range(n):
        p = r.astype(BF16)
        out.append(p)
        r = r - p.astype(F32)
    return out


def _dot01(x, m01, n):
    return sum(_dot(p, m01) for p in _pieces(x, n))


def _dot01_left(m01, x, n):
    return sum(_dot(m01, p) for p in _pieces(x, n))


def _ssd_pre(xa, dtraw, bias, alog, E, tri):
    lane = lax.broadcasted_iota(jnp.int32, (128, 128), 1)
    pre = dtraw + bias
    dtp = jnp.where(lane < SSM_H, jnp.maximum(pre, 0.0) + jnp.log(1.0 + jnp.exp(-jnp.abs(pre))), 0.0)
    a = -jnp.exp(alog)
    acs = _dot01_left(tri, dtp * a, 3)
    acsT = acs.T
    dtE, acsE = _dot01(dtp, E, 2), _dot01(acs, E, 3)
    X = xa[:, :SSM_W]
    xdt = X * dtE
    wE = jnp.exp(acsE[127:128, :] - acsE)
    eE = jnp.exp(acsE)
    cdE = eE[127:128, :]
    return dict(pre=pre, dtp=dtp, a=a, acs=acs, acsT=acsT, dtE=dtE, acsE=acsE, cdE=cdE, X=X, xdt=xdt, wE=wE, eE=eE)


def _ssd_decay(c, h):
    lm = lax.broadcasted_iota(jnp.int32, (128, 128), 0) >= lax.broadcasted_iota(jnp.int32, (128, 128), 1)
    return jnp.exp(jnp.where(lm, c["acs"][:, h:h + 1] - c["acsT"][h:h + 1, :], NEG_INF))


def _ssd_pair_operands(c, CB, h0):
    lane = lax.broadcasted_iota(jnp.int32, (128, 128), 1)
    L0, L1 = _ssd_decay(c, h0), _ssd_decay(c, h0 + 1)
    M = jnp.concatenate([CB * L0, CB * L1], axis=1).astype(BF16)
    xp = c["xdt"][:, h0 * 64:h0 * 64 + 128]
    BD = jnp.concatenate([jnp.where(lane < 64, xp, 0.0), jnp.where(lane >= 64, xp, 0.0)], axis=0).astype(BF16)
    return L0, L1, M, BD


def _ssd_y(c, xa, state_ref, dskipE):
    per_group, ys = [], []
    for g in range(SSM_G):
        gs = slice(g * 512, (g + 1) * 512)
        Bb = xa[:, SSM_W + g * 128:SSM_W + (g + 1) * 128].astype(BF16)
        Cb = xa[:, SSM_W + 256 + g * 128:SSM_W + 256 + (g + 1) * 128].astype(BF16)
        CB = _dot_nt(Cb, Bb)
        Sg = state_ref[:, gs]
        yoff = _dot(Cb, Sg.astype(BF16)) * c["eE"][:, gs]
        ydiag, pairs = [], []
        for j in range(4):
            ops = _ssd_pair_operands(c, CB, g * 8 + 2 * j)
            pairs.append(ops)
            ydiag.append(_dot(ops[2], ops[3]))
        ys.append(jnp.concatenate(ydiag, axis=1) + yoff)
        per_group.append(dict(Bb=Bb, Cb=Cb, CB=CB, Sg=Sg, yoff=yoff, pairs=pairs))
    Y = jnp.concatenate(ys, axis=1) + c["X"] * dskipE
    return Y, per_group


def _ssd_specs(S, rev):
    nc = S // CHUNK
    cm = (lambda b, i: (b, nc - 1 - i)) if rev else (lambda b, i: (b, i))
    xa = pl.BlockSpec((None, CHUNK, CONV_CH), lambda b, i: cm(b, i) + (0,))
    z = [pl.BlockSpec((None, CHUNK, 256), lambda b, i, q=q: cm(b, i) + (OFF["z"] // 256 + q,)) for q in range(4)]
    dt = pl.BlockSpec((None, CHUNK, 128), lambda b, i: cm(b, i) + (OFF["dt"] // 128,))
    tok = pl.BlockSpec((None, CHUNK, SSM_W), lambda b, i: cm(b, i) + (0,))
    st = pl.BlockSpec((None, None, 128, SSM_W), lambda b, i: cm(b, i) + (0, 0))
    return nc, xa, z, dt, tok, st


def _ssd_fwd(xact, P, bias, alog, dskipE, ng, cat, name):
    B, S, _ = P.shape
    nc, xa_spec, z_specs, dt_spec, _, st_spec = _ssd_specs(S, False)
    tok = pl.BlockSpec((None, CHUNK, SSM_W), lambda b, i: (b, i, 1))
    E, _, tri, _ = _ssd_consts()

    def body(xa_ref, z0, z1, z2, z3, dt_ref, bias_ref, alog_ref, dsk_ref, ng_ref, E_ref, tri_ref, cat_ref, o_ref, sp_ref, state):
        @pl.when(pl.program_id(1) == 0)
        def _():
            state[...] = jnp.zeros((128, SSM_W), F32)

        sp_ref[...] = state[...]
        xa = xa_ref[...]
        c = _ssd_pre(xa, dt_ref[...], bias_ref[...], alog_ref[...], E_ref[...], tri_ref[...])
        Y, groups = _ssd_y(c, xa, state, dsk_ref[...])
        Z = (c["xdt"] * c["wE"]).astype(BF16)
        for g in range(SSM_G):
            gs = slice(g * 512, (g + 1) * 512)
            state[:, gs] = groups[g]["Sg"] * c["cdE"][:, gs] + _dot_tn(groups[g]["Bb"], Z[:, gs])
        zv = jnp.concatenate([z0[...], z1[...], z2[...], z3[...]], axis=1)
        yz = Y * (zv * _sigmoid(zv))
        outs = []
        for g in range(SSM_G):
            yg = yz[:, g * 512:(g + 1) * 512]
            outs.append(yg * lax.rsqrt(jnp.mean(yg * yg, axis=-1, keepdims=True) + EPS))
        o_ref[...] = (jnp.concatenate(outs, axis=1) * ng_ref[...]).astype(BF16)

    return pl.pallas_call(
        body, name=name, grid=(B, nc),
        in_specs=[xa_spec] + z_specs + [dt_spec, _full((1, 128)), _full((1, 128)), _full((1, SSM_W)), _full((1, SSM_W)),
                                        _full((128, SSM_W)), _full((128, 128)), _ANY],
        out_specs=[tok, st_spec],
        out_shape=[jax.ShapeDtypeStruct(cat.shape, BF16), jax.ShapeDtypeStruct((B, nc, 128, SSM_W), F32)],
        scratch_shapes=[pltpu.VMEM((128, SSM_W), F32)], input_output_aliases={12: 0},
        compiler_params=pltpu.CompilerParams(dimension_semantics=("arbitrary", "arbitrary")),
    )(xact, P, P, P, P, P, bias, alog, dskipE, ng, E, tri, cat)


def _ssd_bwd(xact, P, sprev, dcat, bias, alog, dskipE, ng, name):
    B, S, _ = P.shape
    nc, xa_spec, z_specs, dt_spec, tok, st_spec = _ssd_specs(S, True)
    do_spec = pl.BlockSpec((None, CHUNK, SSM_W), lambda b, i: (b, nc - 1 - i, 1))
    E, ET, tri, triT = _ssd_consts()
    dt_out = pl.BlockSpec((None, CHUNK, 128), lambda b, i: (b, nc - 1 - i, 0))

    def body(xa_ref, z0, z1, z2, z3, dt_ref, sp_ref, do_ref, bias_ref, alog_ref, dsk_ref, ng_ref, E_ref, ET_ref, tri_ref,
             triT_ref, dxa_ref, dz_ref, ddt_ref, dbias_ref, dalog_ref, ddsk_ref, dng_ref, dstate):
        first = (pl.program_id(0) == 0) & (pl.program_id(1) == 0)

        @pl.when(first)
        def _():
            for ref in (dbias_ref, dalog_ref, ddsk_ref, dng_ref):
                ref[...] = jnp.zeros(ref.shape, F32)

        @pl.when(pl.program_id(1) == 0)
        def _():
            dstate[...] = jnp.zeros((128, SSM_W), F32)

        xa, ETm = xa_ref[...], ET_ref[...]
        c = _ssd_pre(xa, dt_ref[...], bias_ref[...], alog_ref[...], E_ref[...], tri_ref[...])
        Y, groups = _ssd_y(c, xa, sp_ref, dsk_ref[...])
        X, xdt = c["X"], c["xdt"]
        zv = jnp.concatenate([z0[...], z1[...], z2[...], z3[...]], axis=1)
        sg = _sigmoid(zv)
        zs = zv * sg
        yz = Y * zs
        dout = do_ref[...].astype(F32)
        dyz = []
        for g in range(SSM_G):
            gs = slice(g * 512, (g + 1) * 512)
            yg = yz[:, gs]
            r = lax.rsqrt(jnp.mean(yg * yg, axis=-1, keepdims=True) + EPS)
            yn = yg * r
            dng_ref[:, gs] += jnp.sum(dout[:, gs] * yn, axis=0, keepdims=True)
            dyn = dout[:, gs] * ng_ref[:, gs]
            dyz.append(r * (dyn - yn * jnp.mean(dyn * yn, axis=-1, keepdims=True)))
        dyz = jnp.concatenate(dyz, axis=1)
        dz_ref[...] = (dyz * Y * (sg * (1.0 + zv * (1.0 - sg)))).astype(BF16)
        dY = dyz * zs
        ddsk_ref[...] += jnp.sum(dY * X, axis=0, keepdims=True)
        dX = dY * dsk_ref[...]
        lane = lax.broadcasted_iota(jnp.int32, (128, 128), 1)
        sub = lax.broadcasted_iota(jnp.int32, (128, 128), 0)
        colform = jnp.zeros((128, 128), F32)
        rowform = jnp.zeros((128, 128), F32)
        dxdt, gacsE, dBC = [], [], []
        for g in range(SSM_G):
            gs = slice(g * 512, (g + 1) * 512)
            G = groups[g]
            Bb, Cb, CB, Sg = G["Bb"], G["Cb"], G["CB"], G["Sg"]
            dYg = dY[:, gs]
            dQ = (dYg * c["eE"][:, gs]).astype(BF16)
            dSn = dstate[:, gs]
            dSnb = dSn.astype(BF16)
            cd = c["cdE"][:, gs]
            dC = _dot_nt(dQ, Sg.astype(BF16))
            dSprev = _dot_tn(Cb, dQ) + dSn * cd
            t1 = jnp.broadcast_to(jnp.sum(dSn * Sg * cd, axis=0, keepdims=True), (8, 512))
            colform = colform + jnp.where(sub == 127, _dot01(t1, ETm[gs, :], 2)[0:1, :], 0.0)
            Zg = xdt[:, gs] * c["wE"][:, gs]
            dZ = _dot(Bb, dSnb)
            dB = _dot_nt(Zg.astype(BF16), dSnb)
            U = dZ * Zg
            ga = dYg * G["yoff"] - U
            ga = ga + jnp.where(lax.broadcasted_iota(jnp.int32, (128, 512), 0) == 127, jnp.sum(U, axis=0, keepdims=True), 0.0)
            gacsE.append(ga)
            dxg = [None] * 4
            dCB = jnp.zeros((128, 128), F32)
            for j in range(4):
                h0 = g * 8 + 2 * j
                L0, L1, M, BD = G["pairs"][j]
                dYp = dYg[:, j * 128:(j + 1) * 128].astype(BF16)
                dM = _dot_nt(dYp, BD)
                dBD = _dot_tn(M, dYp)
                dxg[j] = jnp.where(lane < 64, dBD[:128], dBD[128:])
                for t, (h, L) in enumerate(((h0, L0), (h0 + 1, L1))):
                    dMh = dM[:, t * 128:(t + 1) * 128]
                    dCB = dCB + dMh * L
                    Gh = dMh * CB * L
                    colform = colform + jnp.where(lane == h, jnp.sum(Gh, axis=1, keepdims=True), 0.0)
                    rowform = rowform - jnp.where(sub == h, jnp.sum(Gh, axis=0, keepdims=True), 0.0)
            dCBb = dCB.astype(BF16)
            dC = dC + _dot(dCBb, Bb)
            dB = dB + _dot_tn(dCBb, Cb)
            dxdt.append(jnp.concatenate(dxg, axis=1) + dZ * c["wE"][:, gs])
            dBC.append((dB, dC))
            dstate[:, gs] = dSprev
        dxdt = jnp.concatenate(dxdt, axis=1)
        dX = dX + dxdt * c["dtE"]
        ddt = _dot01(dxdt * X, ETm, 2)
        dacs = colform + rowform.T + _dot01(jnp.concatenate(gacsE, axis=1), ETm, 2)
        dda = _dot01_left(triT_ref[...], dacs, 2)
        ddt = ddt + dda * c["a"]
        dalog_ref[...] += jnp.sum(dda * c["dtp"], axis=0, keepdims=True) * c["a"]
        ddtraw = jnp.where(lane < SSM_H, ddt * _sigmoid(c["pre"]), 0.0)
        dbias_ref[...] += jnp.sum(ddtraw, axis=0, keepdims=True)
        ddt_ref[...] = ddtraw.astype(BF16)
        dxa_ref[...] = jnp.concatenate([dX, dBC[0][0], dBC[1][0], dBC[0][1], dBC[1][1]], axis=1)

    p128, p1k = _full((1, 128)), _full((1, SSM_W))
    return pl.pallas_call(
        body, name=name, grid=(B, nc),
        in_specs=[xa_spec] + z_specs + [dt_spec, st_spec, do_spec, p128, p128, p1k, p1k,
                                        _full((128, SSM_W)), _full((SSM_W, 128)), _full((128, 128)), _full((128, 128))],
        out_specs=[xa_spec, tok, dt_out, p128, p128, p1k, p1k],
        out_shape=[jax.ShapeDtypeStruct((B, S, CONV_CH), F32), jax.ShapeDtypeStruct((B, S, SSM_W), BF16),
                   jax.ShapeDtypeStruct((B, S, 128), BF16), jax.ShapeDtypeStruct((1, 128), F32),
                   jax.ShapeDtypeStruct((1, 128), F32), jax.ShapeDtypeStruct((1, SSM_W), F32),
                   jax.ShapeDtypeStruct((1, SSM_W), F32)],
        scratch_shapes=[pltpu.VMEM((128, SSM_W), F32)],
        compiler_params=pltpu.CompilerParams(dimension_semantics=("arbitrary", "arbitrary")),
    )(xact, P, P, P, P, P, sprev, dcat, bias, alog, dskipE, ng, E, ET, tri, triT)


def _adamw(w, parts, m, v, name, tr=512, row0=0, prev=None):
    Rtot, C = w.shape
    ns, R = parts.shape[0], parts.shape[1]
    tr = min(tr, R)
    assert R % tr == 0 and row0 % tr == 0
    off = row0 // tr
    c1 = 1.0 / (1.0 - ADAM_B1 ** ADAM_STEP)
    c2 = 1.0 / (1.0 - ADAM_B2 ** ADAM_STEP)

    def body(w_ref, p_ref, m_ref, v_ref, *rest):
        g_ref, d_ref, mo_ref, vo_ref = rest[-4:]
        g = p_ref[0].astype(F32)
        for s in range(1, ns):
            g = g + p_ref[s].astype(F32)
        mn = ADAM_B1 * m_ref[...] + (1.0 - ADAM_B1) * g
        vn = ADAM_B2 * v_ref[...] + (1.0 - ADAM_B2) * (g * g)
        g_ref[...] = g
        mo_ref[...] = mn
        vo_ref[...] = vn
        d_ref[...] = -ADAM_LR * ((mn * c1) / (jnp.sqrt(vn * c2) + ADAM_EPS) + ADAM_WD * w_ref[...])

    blk = pl.BlockSpec((tr, C), lambda i: (i + off, 0))
    extra = [] if prev is None else list(prev)
    return pl.pallas_call(
        body, name=name, grid=(R // tr,),
        in_specs=[blk, pl.BlockSpec((ns, tr, C), lambda i: (0, i, 0)), blk, blk] + [pl.BlockSpec(memory_space=pl.ANY)] * len(extra),
        out_specs=[blk] * 4, out_shape=[jax.ShapeDtypeStruct((Rtot, C), F32)] * 4,
        input_output_aliases={4 + k: k for k in range(len(extra))})(w, parts, m, v, *extra)


_SMALL = ("ada_b", "norm1_g", "gm_ln_g", "gm_ln_b", "gm_ws", "gm_bs", "gm_norm_g", "attn_sinks", "attn_norm_g", "conv_b",
          "dt_bias", "a_log", "d_skip", "ssm_norm_g", "norm2_g", "final_norm_g")


def _pack(arrs):
    flat = []
    for a in arrs:
        f = a.reshape(-1).astype(F32)
        flat.append(jnp.pad(f, (0, (-f.shape[0]) % 1024)))
    return jnp.concatenate(flat).reshape(-1, 128)


def _unpack(pack, like):
    out, o = [], 0
    flat = pack.reshape(-1)
    for a in like:
        n = int(np.prod(a.shape))
        out.append(flat[o:o + n].reshape(a.shape))
        o += n + (-n) % 1024
    return out


def kernel(x, c, ada_w, ada_b, norm1_g, w_in, gm_ln_g, gm_ln_b, gm_ws, gm_bs, gm_norm_g, attn_sinks, attn_norm_g, conv_w, conv_b, dt_bias, a_log, d_skip, ssm_norm_g, w_out, norm2_g, w_mlp1, w_mlp2, final_norm_g, loss_target, m_ada_w, m_ada_b, m_norm1_g, m_w_in, m_gm_ln_g, m_gm_ln_b, m_gm_ws, m_gm_bs, m_gm_norm_g, m_attn_sinks, m_attn_norm_g, m_conv_w, m_conv_b, m_dt_bias, m_a_log, m_d_skip, m_ssm_norm_g, m_w_out, m_norm2_g, m_w_mlp1, m_w_mlp2, m_final_norm_g, v_ada_w, v_ada_b, v_norm1_g, v_w_in, v_gm_ln_g, v_gm_ln_b, v_gm_ws, v_gm_bs, v_gm_norm_g, v_attn_sinks, v_attn_norm_g, v_conv_w, v_conv_b, v_dt_bias, v_a_log, v_d_skip, v_ssm_norm_g, v_w_out, v_norm2_g, v_w_mlp1, v_w_mlp2, v_final_norm_g):
    args = dict(locals())
    B, S, _ = x.shape
    T = B * S
    L = DEPTH
    me = 4 * lax.axis_index("x") + 2 * lax.axis_index("y") + lax.axis_index("c")

    gath = _gather2([c, conv_w], "ag_c")
    big = ("w_in", "w_out", "w_mlp1", "w_mlp2")
    chain = [(n, l) for l in range(L) for n in ("w_in", "w_mlp1", "w_out", "w_mlp2")]
    inflight = {}

    def start_next(order):
        if not chain:
            return jnp.zeros((8, 128), F32)
        n, l = chain.pop(0)
        sems, land_thru, token = _gather_start(zone[n, l], order, f"ag_start_{n}{l}")
        inflight[n, l] = (sems, land_thru)
        return token

    def gathered(n, l, after):
        land = _gather_wait(*inflight.pop((n, l)), after, f"ag_wait_{n}{l}")
        return _gather_finish(land, f"ag_fin_{n}{l}")

    forwarding = {}

    def arrived(n, l, after):
        land = _gather_wait(*inflight.pop((n, l)), after, f"ag_wait_{n}{l}")
        sems, land_thru, token = _forward_start(land, after, f"ag_fwd_start_{n}{l}")
        forwarding[n, l] = (sems, land_thru)
        return token

    def ready(n, l, after):
        return _forward_wait(*forwarding.pop((n, l)), after, f"ag_fwd_wait_{n}{l}")

    me1 = me.astype(jnp.int32).reshape(1)
    zone = {(n, l): _landing_zone(args[n], l, me1, f"ag_zone_{n}{l}") for n, l in chain}
    later_zones = [zone[k] for k in chain[1:]]

    tok = start_next(gath[0])
    c_all = gath[0].reshape(NDEV * B, D) + tok[0, 0]
    c_act = (c_all * jax.nn.sigmoid(c_all)).astype(BF16)
    nb_rows = c_act.shape[0]
    c_pad = jnp.pad(c_act, ((0, 128 - nb_rows), (0, 0)))
    adw = ada_w.astype(BF16)
    mod_part = jnp.stack([_mm(c_pad, adw[l], mode="nn", name=f"mod{l}", tn=768)[:nb_rows] for l in range(L)])
    mod_all = _gather_small([mod_part], "ag_mod", order=later_zones)[0]
    mod_mine = lax.dynamic_slice_in_dim(mod_all, me * B, B, axis=2)
    mod = jnp.transpose(mod_mine, (1, 2, 0, 3)).reshape(L, B, 6 * D) + ada_b[:, None, :]
    mods = [[mod[l][:, None, i * D:(i + 1) * D] for i in range(6)] for l in range(L)]

    win_g, wout_g, w1_g, w2_g = [None] * L, [None] * L, [None] * L, [None] * L

    tril = jnp.tril(jnp.ones((128, 128), F32))
    row = lambda a: a.reshape(1, -1)
    pad128 = lambda a: jnp.pad(a.reshape(1, -1), ((0, 0), (0, 128 - a.shape[-1])))
    small = []
    for l in range(L):
        wt = gm_ws[l] * tril
        small.append(dict(
            lng=row(gm_ln_g[l]), lnb=row(gm_ln_b[l]), wt=wt.astype(BF16), wtT=jnp.swapaxes(wt, 1, 2).astype(BF16),
            bsx=jnp.repeat(gm_bs[l].T, 128, axis=1), gog=row(gm_norm_g[l]), sinks=attn_sinks[l], aog=row(attn_norm_g[l]),
            bias=pad128(dt_bias[l]), alog=pad128(a_log[l]), dskE=jnp.repeat(d_skip[l], SSM_HD).reshape(1, SSM_W),
            sng=row(ssm_norm_g[l]), cb=row(conv_b[l])))
    convw_all = jnp.transpose(gath[1], (1, 2, 0, 3)).reshape(L, 4, CONV_CH)
    convw8 = jnp.pad(convw_all, ((0, 0), (0, 4), (0, 0)))

    saved = []
    xl = x
    g_in = gathered("w_in", 0, mod)
    tok = start_next(g_in)
    h = _norm_fwd(xl, row(norm1_g[0]) + tok[0, 0], mods[0][1], mods[0][0], "norm1_f0")
    for l in range(L):
        sm = small[l]
        win_g[l] = _shards_to_cols(g_in, f"w_in_cols{l}")
        P = _mm(h.reshape(T, D), win_g[l], mode="nn", name=f"proj_in{l}", tn=1536, order=tok).reshape(B, S, PW)
        cat = _gmlp_fwd(P, sm["lng"], sm["lnb"], sm["wt"], sm["bsx"], sm["gog"], f"gmlp_f{l}")
        cat = _attn_fwd(P, sm["sinks"], sm["aog"], cat, f"attn_f{l}")
        xact = _conv_fwd(P, convw8[l], sm["cb"], f"conv_f{l}")
        tok = start_next(arrived("w_mlp1", l, xact))
        cat, sprev = _ssd_fwd(xact, P, sm["bias"], sm["alog"], sm["dskE"], sm["sng"] + tok[0:1, 0:1], cat, f"ssd_f{l}")
        g_out = gathered("w_out", l, cat)
        tok = start_next(g_out)
        wout_g[l] = g_out.reshape(D, D)
        mix = _mm(cat.reshape(T, D), wout_g[l], mode="nn", name=f"proj_out{l}", order=tok).reshape(B, S, D)
        x_mid, h2 = _norm_fwd(xl, row(norm2_g[l]), mods[l][4], mods[l][3], f"norm2_f{l}", resid=(mix, mods[l][2]))
        w1_g[l] = ready("w_mlp1", l, h2)
        a_act, r_act = _mm(h2.reshape(T, D), w1_g[l], mode="nn", name=f"mlp1_{l}", out_dtypes=(BF16, BF16), col_blocked_b=True,
                           epilogue=lambda acc: (acc, jnp.square(jnp.maximum(acc, 0.0))))
        g_2 = gathered("w_mlp2", l, r_act)
        tok = start_next(g_2)
        w2_g[l] = g_2.reshape(DFF, D)
        m2 = _mm(r_act, w2_g[l], mode="nn", name=f"mlp2_{l}", order=tok, tk=4096).reshape(B, S, D)
        saved.append(dict(x_in=xl, h=h, P=P, xact=xact, sprev=sprev, cat=cat, mix=mix, x_mid=x_mid, h2=h2, a=a_act, r=r_act, m2=m2))
        if l + 1 < L:
            tok = start_next(arrived("w_in", l + 1, m2))
            xl, h = _norm_fwd(x_mid, row(norm1_g[l + 1]) + tok[0, 0], mods[l + 1][1], mods[l + 1][0], f"norm1_f{l + 1}",
                              resid=(m2, mods[l][5]))
            g_in = ready("w_in", l + 1, h)

    sv = saved[L - 1]
    nb = _norm_bwd(sv["x_mid"], row(final_norm_g), "final_b", tgt=loss_target, br=sv["m2"], gate=mods[L - 1][5], x_is_prev=True)
    loss_part, g_final = nb["loss"], nb["dg"]
    dmod, gsm, gconvw = [None] * L, [None] * L, [None] * L
    core = lax.axis_index("c").astype(jnp.int32).reshape(1)
    reducing = []

    def reduce_start(n, l, sent, after):
        p, from_sib = _pair_wait(*sent[:3], after, f"rs_pair_wait_{n}{l}")
        s, land = _pair_add(p, from_sib, core, f"rs_add_{n}{l}")
        return reduce_exchange(n, l, s, land, after)

    def reduce_exchange(n, l, s, land, order):
        sems, s_thru, land_thru, token = _chipsum_start(s, land, order, f"rs_start_{n}{l}")
        reducing.append((n, l, sems, s_thru, land_thru))
        return token

    other = 1 - core

    for l in reversed(range(L)):
        sv, sm = saved[l], small[l]
        dm2, dxo, dg2 = nb["dbr"].reshape(T, D), nb["dx"], nb["dgate"]
        da = _mm(dm2, w2_g[l], mode="nt", name=f"mlp2_dx{l}", out_dtypes=(BF16,), extras=(sv["a"],),
                 epilogue=lambda acc, a: (acc * (2.0 * jnp.maximum(a.astype(F32), 0.0)),))
        h2f = sv["h2"].reshape(T, D)
        sent2 = _sibling_start(_dw_half(sv["r"], dm2, other, axis="m", name=f"mlp2_dw_sib{l}"), da, f"rs_sib_start_w_mlp2{l}")
        dh2 = _mm(da, w1_g[l], mode="nt", name=f"mlp1_dx{l}", col_blocked_b=True, order=sent2[3],
                  out_dtypes=(BF16,)).reshape(B, S, D)
        from_sib = _sibling_wait(*sent2[:3], dh2, f"rs_sib_wait_w_mlp2{l}")[1]
        sent1 = _sibling_start(_dw_half(h2f, da, other, axis="n", name=f"mlp1_dw_sib{l}", order=from_sib), da,
                               f"rs_sib_start_w_mlp1{l}")
        s2, land2 = _dw_half(sv["r"], dm2, core, axis="m", name=f"mlp2_dw_own{l}", add=from_sib, order=sent1[3])
        tok = reduce_exchange("w_mlp2", l, s2, land2, da)
        nb2 = _norm_bwd(sv["x_mid"], row(norm2_g[l]) + tok[0, 0], f"norm2_b{l}", sc=mods[l][4], dh=dh2, dres=dxo, br=sv["mix"],
                        gate=mods[l][2])
        dmix = nb2["dbr"].reshape(T, D)
        from_sib = _sibling_wait(*sent1[:3], dmix, f"rs_sib_wait_w_mlp1{l}")[1]
        s1, land1 = _dw_half(h2f, da, core, axis="n", name=f"mlp1_dw_own{l}", add=from_sib)
        tok = reduce_exchange("w_mlp1", l, s1, land1, dmix)
        dcat = _mm(dmix, wout_g[l], mode="nt", name=f"proj_out_dx{l}", order=tok, out_dtypes=(BF16,)).reshape(B, S, D)
        du, dv, dlng, dlnb, dws, dbsx, dgog = _gmlp_bwd(sv["P"], dcat, sm["lng"], sm["lnb"], sm["wt"], sm["wtT"], sm["bsx"],
                                                        sm["gog"], f"gmlp_b{l}")
        dq, dk, dvv, dsink, daog = _attn_bwd(sv["P"], dcat, sm["sinks"], sm["aog"], f"attn_b{l}")
        dwo = _mm(sv["cat"].reshape(T, D), dmix, mode="tn", name=f"proj_out_dw{l}", out_dtypes=(BF16,), tk=2048,
                  order=dq).reshape(4, 2, D // NDEV, D)
        sent = _pair_start(dwo, dmix, f"rs_pair_start_w_out{l}")
        dxa, dz, ddt, dbias, dalog, ddsk, dsng = _ssd_bwd(sv["xact"], sv["P"], sv["sprev"], dcat, sm["bias"], sm["alog"],
                                                          sm["dskE"], sm["sng"] + sent[3][0:1, 0:1], f"ssd_b{l}")
        tok = reduce_start("w_out", l, sent, dxa)
        dxbc, dcw, dcb = _conv_bwd(sv["P"], dxa, convw8[l], sm["cb"] + tok[0:1, 0:1], f"conv_b{l}")
        dP = jnp.concatenate([du, dv, dq, dk, dvv, dz, dxbc, ddt, jnp.zeros((B, S, PW - OFF["dt"] - 128), BF16)],
                             axis=-1).reshape(T, PW)
        dwin = _mm(sv["h"].reshape(T, D), dP, mode="tn", name=f"proj_in_dw{l}", out_dtypes=(BF16,), tn=1536, tk=2048)
        sent = _sibling_start(dwin, dP, f"rs_sib_start_w_in{l}")
        dh = _mm(dP, win_g[l], mode="nt", name=f"proj_in_dx{l}", tk=2304, order=sent[3], out_dtypes=(BF16,)).reshape(B, S, D)
        s_in, land_in = _cols_to_my_shards(*_sibling_wait(*sent[:3], dh, f"rs_sib_wait_w_in{l}"), core, f"w_in_dshards{l}")
        tok = reduce_exchange("w_in", l, s_in, land_in, dh)
        nb = _norm_bwd(sv["x_in"], row(norm1_g[l]) + tok[0, 0], f"norm1_b{l}", sc=mods[l][1], dh=dh, dres=nb2["dx"],
                       br=saved[l - 1]["m2"] if l > 0 else None, gate=mods[l - 1][5] if l > 0 else None)
        dmod[l] = jnp.concatenate([nb["dsh"], nb["dsc"], nb2["dgate"], nb2["dsh"], nb2["dsc"], dg2], axis=-1)
        gconvw[l] = dcw[:4]
        gsm[l] = dict(
            ada_b=jnp.sum(dmod[l], axis=(0, 1)), norm1_g=nb["dg"], gm_ln_g=dlng, gm_ln_b=dlnb, gm_ws=dws,
            gm_bs=dbsx.reshape(128, GM_H, 128).sum(-1).T, gm_norm_g=dgog, attn_sinks=dsink[:, 0], attn_norm_g=daog,
            conv_b=dcb, dt_bias=dbias[0, :SSM_H], a_log=dalog[0, :SSM_H], d_skip=ddsk.reshape(SSM_H, SSM_HD).sum(-1),
            ssm_norm_g=dsng, norm2_g=nb2["dg"])
    grad_x = nb["dx"]

    big_res, after = dict.fromkeys(big), grad_x
    tile_rows = dict(w_in=256, w_out=256, w_mlp1=256, w_mlp2=128)

    def finish_reduce(n, l, sems, s_thru, land_thru, after):
        parts = _chipsum_wait(sems, s_thru, land_thru, after, f"rs_wait_{n}{l}")
        w = args[n]
        big_res[n] = _adamw(w.reshape(-1, w.shape[-1]), parts, args["m_" + n].reshape(-1, w.shape[-1]),
                            args["v_" + n].reshape(-1, w.shape[-1]), f"adamw_{n}{l}", tr=tile_rows[n], row0=l * w.shape[1],
                            prev=big_res[n])
        return big_res[n][0]

    per_layer = [n for n in _SMALL if n != "final_norm_g"]
    g_small = [jnp.stack([gsm[l][n].reshape(args[n].shape[1:]) for l in range(L)]) for n in per_layer] + [g_final.reshape(D)]
    zc = jnp.zeros((L, 4, CONV_CH), F32)
    z1 = jnp.zeros((1, 128), F32)
    gpack = _pack([loss_part] + g_small + [jnp.stack(gconvw)])
    small_zones = [_landing_zone(jnp.stack(dmod).reshape(1, L * B, 6 * D), 0, me1, "ag_zone_dmod", dtype=F32),
                   _landing_zone(gpack[None], 0, me1, "ag_zone_small", tr=gpack.shape[0], dtype=F32)]
    small_sems, small_thru, after = _gather_small_start(small_zones, grad_x, "ag_small_start")

    for item in reducing[:-1]:
        after = finish_reduce(*item, after)

    got = _gather_small_wait(small_sems, small_thru, after, "ag_small_wait")
    got = [got[0].reshape(NDEV, L, B, 6 * D), got[1]]
    like = [z1] + [args[n] for n in _SMALL] + [zc]
    packs = [_pack([z1] + [args[p + n] for n in _SMALL] + [zc]) for p in ("", "m_", "v_")]
    sres = [_unpack(p, like) for p in _adamw(packs[0], got[1], packs[1], packs[2], "adamw_small", tr=gpack.shape[0])]
    res = {n: [r[1 + i] for r in sres] for i, n in enumerate(_SMALL)}
    loss = sres[0][0][0, 0]
    gcw = lax.dynamic_slice_in_dim(sres[0][-1], me * (CONV_CH // NDEV), CONV_CH // NDEV, axis=2)

    def update(name, parts, tr):
        w = args[name]
        r = _adamw(w.reshape(-1, w.shape[-1]), parts, args["m_" + name].reshape(-1, w.shape[-1]),
                   args["v_" + name].reshape(-1, w.shape[-1]), "adamw_" + name, tr=tr)
        res[name] = [a.reshape(w.shape) for a in r]

    update("conv_w", gcw.reshape(1, L * 4, CONV_CH // NDEV), L * 4)

    dmod_all = jnp.transpose(got[0], (1, 0, 2, 3)).reshape(L, NDEV * B, 6 * D)
    dm_mine = lax.dynamic_slice_in_dim(dmod_all, me * (6 * D // NDEV), 6 * D // NDEV, axis=2)
    dm_pad = jnp.pad(dm_mine, ((0, 0), (0, 128 - nb_rows), (0, 0))).astype(BF16)
    g_adaw = jnp.stack([_mm(c_pad, dm_pad[l], mode="tn", name=f"ada_dw{l}", tn=768) for l in range(L)])
    update("ada_w", g_adaw.reshape(1, L * D, 6 * D // NDEV), 256)

    finish_reduce(*reducing[-1], res["ada_w"][0])
    for n in big:
        res[n] = [a.reshape(args[n].shape) for a in big_res[n]]

    names = ['ada_w', 'ada_b', 'norm1_g', 'w_in', 'gm_ln_g', 'gm_ln_b', 'gm_ws', 'gm_bs', 'gm_norm_g', 'attn_sinks',
             'attn_norm_g', 'conv_w', 'conv_b', 'dt_bias', 'a_log', 'd_skip', 'ssm_norm_g', 'w_out', 'norm2_g', 'w_mlp1',
             'w_mlp2', 'final_norm_g']
    return (loss, grad_x, *[res[n][0] for n in names], *[res[n][1] for n in names], *[res[n][2] for n in names],
            *[res[n][3] for n in names])
```

```python
import jax
import jax.numpy as jnp
import numpy as np
from jax import lax
from jax.experimental import pallas as pl
from jax.experimental.pallas import tpu as pltpu

F32, BF16 = jnp.float32, jnp.bfloat16
MESH = pl.DeviceIdType.MESH
NDEV = 8

D = 2048
DEPTH = 2
CHUNK = 128
GM_W, GM_H = 512, 4
ATT_W, KV_W, ATT_H = 512, 128, 8
SSM_W, SSM_H, SSM_HD, SSM_G = 1024, 16, 64, 2
CONV_CH = 1536
IN_W = 4368
DFF = 8192
EPS = 1e-6
NEG_INF = -1e30
GELU_K = 0.7978845608028654
GELU_C = 0.044715

_ORIG = (("u", 512), ("v", 512), ("q", 512), ("k", 128), ("vv", 128), ("z", 1024), ("xbc", 1536), ("dt", 16))
OFF = dict(u=0, v=512, q=1024, k=1536, vv=1664, z=1792, xbc=2816, dt=4352)
PW = 4608

ADAM_LR, ADAM_B1, ADAM_B2, ADAM_EPS, ADAM_WD, ADAM_STEP = 0.001, 0.9, 0.999, 1e-08, 0.01, 10


def _concat_cols(pieces, width, name, ts=256):
    B, S, _ = pieces[0].shape
    ws = [p.shape[-1] for p in pieces]
    dt = pieces[0].dtype
    n = len(pieces)

    def body(*refs):
        cols = [r[...] for r in refs[:n]]
        if width > sum(ws):
            cols.append(jnp.zeros((ts, width - sum(ws)), dt))
        refs[n][...] = jnp.concatenate(cols, axis=1)

    return pl.pallas_call(
        body, name=name, grid=(B, S // ts), in_specs=[pl.BlockSpec((None, ts, w), lambda b, i: (b, i, 0)) for w in ws],
        out_specs=pl.BlockSpec((None, ts, width), lambda b, i: (b, i, 0)), out_shape=jax.ShapeDtypeStruct((B, S, width), dt))(*pieces)


def _shards_to_cols(g, name, tr=256):
    n, R, C = g.shape

    def body(g_ref, o_ref):
        o_ref[...] = jnp.concatenate([g_ref[s] for s in range(n)] + [jnp.zeros((tr, PW - n * C), g.dtype)], axis=1)

    return pl.pallas_call(body, name=name, grid=(R // tr,), in_specs=[pl.BlockSpec((n, tr, C), lambda i: (0, i, 0))],
                          out_specs=pl.BlockSpec((tr, PW), lambda i: (i, 0)), out_shape=jax.ShapeDtypeStruct((R, PW), g.dtype))(g)


def _cols_to_my_shards(w, w_sib, core, name, tr=256):
    R, C = w.shape[0], IN_W // NDEV

    def body(core_ref, w_ref, s_ref, o_ref, o2_ref):
        x = w_ref[...].astype(F32) + s_ref[...].astype(F32)
        mine_is_odd = core_ref[0] == 1
        for q in range(4):
            blk = jnp.where(mine_is_odd, x[:, C * (2 * q + 1):C * (2 * q + 2)], x[:, C * 2 * q:C * (2 * q + 1)]).astype(o_ref.dtype)
            o_ref[q] = blk
            o2_ref[q] = blk

    row = pl.BlockSpec((tr, PW), lambda i, c: (i, 0))
    out = pl.BlockSpec((4, tr, C), lambda i, c: (0, i, 0))
    return pl.pallas_call(
        body, name=name, out_shape=[jax.ShapeDtypeStruct((4, R, C), w.dtype)] * 2,
        grid_spec=pltpu.PrefetchScalarGridSpec(num_scalar_prefetch=1, grid=(R // tr,), in_specs=[row, row], out_specs=[out, out]),
    )(core, w, w_sib)


def _sigmoid(x):
    return 0.5 * (jnp.tanh(0.5 * x) + 1.0)


def _gelu(x):
    return 0.5 * x * (1.0 + jnp.tanh(GELU_K * (x + GELU_C * x * x * x)))


def _gelu_grad(x):
    t = jnp.tanh(GELU_K * (x + GELU_C * x * x * x))
    return 0.5 * (1.0 + t) + 0.5 * x * (1.0 - t * t) * GELU_K * (1.0 + 3.0 * GELU_C * x * x)


def _dot(a, b, prec=None):
    return jnp.dot(a, b, precision=prec, preferred_element_type=F32)


def _dot_nt(a, b, prec=None):
    return lax.dot_general(a, b, (((1,), (1,)), ((), ())), precision=prec, preferred_element_type=F32)


def _dot_tn(a, b, prec=None):
    return lax.dot_general(a, b, (((0,), (0,)), ((), ())), precision=prec, preferred_element_type=F32)


def _full(shape):
    return pl.BlockSpec(shape, lambda *_: (0,) * len(shape))


_HBM = pl.BlockSpec(memory_space=pltpu.HBM)


def _me():
    return lax.axis_index("x"), lax.axis_index("y"), lax.axis_index("c")


def _peer(k):
    x, y, c = _me()
    px = 1 - x if k & 4 else x
    py = 1 - y if k & 2 else y
    pc = 1 - c if k & 1 else c
    return (px, py, pc), 4 * px + 2 * py + pc


def _gather_small(xs, name, order=()):
    n = len(xs)

    def body(*refs):
        ins, outs = refs[:n], refs[-n - 3:-3]
        send, recv, loc = refs[-3:]
        x, y, c = _me()
        me = 4 * x + 2 * y + c
        started = []
        for i in range(n):
            own = pltpu.make_async_copy(ins[i], outs[i].at[me], loc.at[i])
            own.start()
            started.append(own)
        for k in range(1, NDEV):
            dev, lin = _peer(k)
            for i in range(n):
                pltpu.make_async_remote_copy(
                    src_ref=ins[i], dst_ref=outs[i].at[me],
                    send_sem=send.at[i, k - 1], recv_sem=recv.at[i, k - 1], device_id=dev, device_id_type=MESH).start()
        for k in range(1, NDEV):
            dev, lin = _peer(k)
            for i in range(n):
                pltpu.make_async_remote_copy(
                    src_ref=ins[i], dst_ref=outs[i].at[lin],
                    send_sem=send.at[i, k - 1], recv_sem=recv.at[i, k - 1], device_id=dev, device_id_type=MESH).wait()
        for own in started:
            own.wait()

    extra = list(order)
    return pl.pallas_call(
        body, name=name, out_shape=[jax.ShapeDtypeStruct((NDEV,) + a.shape, a.dtype) for a in xs],
        in_specs=[_HBM] * n + [pl.BlockSpec(memory_space=pl.ANY)] * len(extra), out_specs=[_HBM] * n,
        scratch_shapes=[pltpu.SemaphoreType.DMA((n, NDEV - 1)), pltpu.SemaphoreType.DMA((n, NDEV - 1)),
                        pltpu.SemaphoreType.DMA((n,))],
        compiler_params=pltpu.CompilerParams(has_side_effects=True),
    )(*xs, *extra)


def _gather_small_start(lands, order, name):
    n = len(lands)

    def body(*refs):
        ins, sems, token = refs[:n], refs[n + 1:n + 1 + 14 * n], refs[-1]
        x, y, c = _me()
        me = 4 * x + 2 * y + c
        for i in range(n):
            for k in range(1, NDEV):
                dev, _ = _peer(k)
                pltpu.make_async_remote_copy(src_ref=ins[i].at[me], dst_ref=ins[i].at[me], send_sem=sems[14 * i + k - 1],
                                             recv_sem=sems[14 * i + 7 + k - 1], device_id=dev, device_id_type=MESH).start()
        token[...] = jnp.zeros_like(token)

    outs = pl.pallas_call(
        body, name=name,
        out_shape=(pltpu.SemaphoreType.DMA(()),) * (14 * n) + tuple(pltpu.HBM(a.shape, a.dtype) for a in lands)
        + (jax.ShapeDtypeStruct((8, 128), F32),),
        in_specs=(_HBM,) * n + (_ANY,), out_specs=(_SEM,) * (14 * n) + (_HBM,) * n + (pl.BlockSpec(memory_space=pltpu.VMEM),),
        input_output_aliases={i: 14 * n + i for i in range(n)}, compiler_params=pltpu.CompilerParams(has_side_effects=_DATAFLOW),
    )(*[_hbm(a) for a in lands], order)
    return outs[:14 * n], outs[14 * n:15 * n], outs[-1]


def _gather_small_wait(sems, lands_thru, after, name):
    n = len(lands_thru)

    def body(*refs):
        ins, sems_ = refs[:n], refs[n:n + 14 * n]
        x, y, c = _me()
        me = 4 * x + 2 * y + c
        for i in range(n):
            for k in range(1, NDEV):
                dev, lin = _peer(k)
                cp = pltpu.make_async_remote_copy(src_ref=ins[i].at[me], dst_ref=ins[i].at[lin], send_sem=sems_[14 * i + k - 1],
                                                  recv_sem=sems_[14 * i + 7 + k - 1], device_id=dev, device_id_type=MESH)
                cp.wait_send()
                cp.wait_recv()

    outs = pl.pallas_call(
        body, name=name, out_shape=tuple(pltpu.HBM(a.shape, a.dtype) for a in lands_thru),
        in_specs=(_HBM,) * n + (_SEM,) * (14 * n) + (_ANY,), out_specs=(_HBM,) * n,
        input_output_aliases={i: i for i in range(n)}, compiler_params=pltpu.CompilerParams(has_side_effects=_DATAFLOW),
    )(*lands_thru, *sems, after)
    return outs


def _chips():
    x, y, c = _me()
    return x, y, c, [(1 - x, y), (x, 1 - y), (1 - x, 1 - y)]


def _gather2(xs, name, order=None):
    n = len(xs)
    extra = [] if order is None else [order]

    def body(*refs):
        ins, outs = refs[:n], refs[-n - 3:-3]
        send, recv, loc = refs[-3:]
        x, y, c, chips = _chips()
        me, sib = (x, y, c), (x, y, 1 - c)

        def cp(i, k, block, to, src=None):
            slot = outs[i].at[4 * block[0] + 2 * block[1] + block[2]]
            return pltpu.make_async_remote_copy(src_ref=slot if src is None else src, dst_ref=slot, send_sem=send.at[i, k],
                                                recv_sem=recv.at[i, k], device_id=to, device_id_type=MESH)

        sent = []
        for i in range(n):
            for j, chip in enumerate(chips):
                sent.append(cp(i, 1 + j, me, (*chip, c), src=ins[i]))
            sent.append(cp(i, 0, me, sib, src=ins[i]))
        for s in sent:
            s.start()
        own = [pltpu.make_async_copy(ins[i], outs[i].at[4 * x + 2 * y + c], loc.at[i]) for i in range(n)]
        for o in own:
            o.start()
        for j, chip in enumerate(chips):
            for i in range(n):
                cp(i, 1 + j, (*chip, c), me).wait_recv()
                fwd = cp(i, 4 + j, (*chip, c), sib)
                fwd.start()
                sent.append(fwd)
        for i in range(n):
            cp(i, 0, sib, me).wait_recv()
            for j, chip in enumerate(chips):
                cp(i, 4 + j, (*chip, 1 - c), me).wait_recv()
        for s in sent:
            s.wait_send()
        for o in own:
            o.wait()

    return pl.pallas_call(
        body, name=name, out_shape=[jax.ShapeDtypeStruct((NDEV,) + a.shape, a.dtype) for a in xs],
        in_specs=[_HBM] * n + [pl.BlockSpec(memory_space=pl.ANY)] * len(extra), out_specs=[_HBM] * n,
        scratch_shapes=[pltpu.SemaphoreType.DMA((n, 7)), pltpu.SemaphoreType.DMA((n, 7)), pltpu.SemaphoreType.DMA((n,))],
        compiler_params=pltpu.CompilerParams(has_side_effects=True),
    )(*xs, *extra)


def _pair_add(p, r1, core, name, tr=256):
    _, _, R, C = p.shape
    tr = min(tr, R)

    def body(core_ref, p_ref, r_ref, o_ref, o2_ref):
        s = (p_ref[...].astype(F32) + r_ref[...].astype(F32)).astype(o_ref.dtype)
        o_ref[...] = s
        o2_ref[...] = s

    blk = pl.BlockSpec((None, tr, C), lambda ch, i, core_ref: (ch, i, 0))
    return pl.pallas_call(
        body, name=name, out_shape=[jax.ShapeDtypeStruct((4, R, C), p.dtype)] * 2,
        grid_spec=pltpu.PrefetchScalarGridSpec(
            num_scalar_prefetch=1, grid=(4, R // tr),
            in_specs=[pl.BlockSpec((None, None, tr, C), lambda ch, i, core_ref: (ch, core_ref[0], i, 0)), blk],
            out_specs=[blk, blk]),
    )(core, p, r1)


_SEM = pl.BlockSpec(memory_space=pltpu.SEMAPHORE)
_ANY = pl.BlockSpec(memory_space=pl.ANY)
_DATAFLOW = pltpu.SideEffectType.DATAFLOW_SIDE_EFFECTING


def _hbm(a):
    return pltpu.with_memory_space_constraint(a, pltpu.HBM)


def _gather_targets():
    x, y, c, chips = _chips()
    return 4 * x + 2 * y + c, [(x, y, 1 - c)] + [(*chip, c) for chip in chips]


def _landing_zone(w, l, me, name, tr=512, dtype=BF16):
    _, R, C = w.shape
    tr = min(tr, R)

    def body(me_ref, w_ref, o_ref):
        o_ref[...] = w_ref[...].astype(dtype)

    return pl.pallas_call(
        body, name=name, out_shape=jax.ShapeDtypeStruct((NDEV, R, C), dtype),
        grid_spec=pltpu.PrefetchScalarGridSpec(
            num_scalar_prefetch=1, grid=(R // tr,), in_specs=[pl.BlockSpec((None, tr, C), lambda i, me_ref: (l, i, 0))],
            out_specs=pl.BlockSpec((None, tr, C), lambda i, me_ref: (me_ref[0], i, 0))),
    )(me, w)


def _gather_start(land, order, name):
    def body(land_ref, order_ref, *rest):
        sems, token = rest[:8], rest[9]
        me, targets = _gather_targets()
        for k, to in enumerate(targets):
            pltpu.make_async_remote_copy(src_ref=land_ref.at[me], dst_ref=land_ref.at[me], send_sem=sems[k],
                                         recv_sem=sems[4 + k], device_id=to, device_id_type=MESH).start()
        token[...] = jnp.zeros_like(token)

    outs = pl.pallas_call(
        body, name=name,
        out_shape=(pltpu.SemaphoreType.DMA(()),) * 8 + (pltpu.HBM(land.shape, land.dtype), jax.ShapeDtypeStruct((8, 128), F32)),
        in_specs=(_HBM, _ANY), out_specs=(_SEM,) * 8 + (_HBM, pl.BlockSpec(memory_space=pltpu.VMEM)),
        input_output_aliases={0: 8}, compiler_params=pltpu.CompilerParams(has_side_effects=_DATAFLOW),
    )(_hbm(land), order)
    return outs[:8], outs[8], outs[9]


def _gather_wait(sems, land_thru, after, name):
    def body(land_ref, *rest):
        sems_ = rest[:8]
        me, targets = _gather_targets()
        for k, to in enumerate(targets):
            cp = pltpu.make_async_remote_copy(src_ref=land_ref.at[me], dst_ref=land_ref.at[me], send_sem=sems_[k],
                                              recv_sem=sems_[4 + k], device_id=to, device_id_type=MESH)
            cp.wait_send()
            cp.wait_recv()

    return pl.pallas_call(
        body, name=name, out_shape=pltpu.HBM(land_thru.shape, land_thru.dtype),
        in_specs=(_HBM,) + (_SEM,) * 8 + (_ANY,), out_specs=_HBM, input_output_aliases={0: 0},
        compiler_params=pltpu.CompilerParams(has_side_effects=_DATAFLOW),
    )(land_thru, *sems, after)


def _gather_finish(land, name):
    def body(land_ref, out, send, recv):
        x, y, c, chips = _chips()
        fwd = [pltpu.make_async_remote_copy(src_ref=out.at[4 * px + 2 * py + c], dst_ref=out.at[4 * px + 2 * py + c],
                                            send_sem=send.at[j], recv_sem=recv.at[j], device_id=(x, y, 1 - c), device_id_type=MESH)
               for j, (px, py) in enumerate(chips)]
        for cp in fwd:
            cp.start()
        for j, (px, py) in enumerate(chips):
            slot = out.at[4 * px + 2 * py + 1 - c]
            pltpu.make_async_remote_copy(src_ref=slot, dst_ref=slot, send_sem=send.at[j], recv_sem=recv.at[j],
                                         device_id=(x, y, 1 - c), device_id_type=MESH).wait()

    return pl.pallas_call(
        body, name=name, out_shape=jax.ShapeDtypeStruct(land.shape, land.dtype),
        in_specs=[_HBM], out_specs=_HBM, input_output_aliases={0: 0},
        scratch_shapes=[pltpu.SemaphoreType.DMA((3,)), pltpu.SemaphoreType.DMA((3,))],
        compiler_params=pltpu.CompilerParams(has_side_effects=True),
    )(land)


def _forward_start(land, order, name):
    def body(land_ref, order_ref, *rest):
        sems, token = rest[:6], rest[7]
        x, y, c, chips = _chips()
        for j, (px, py) in enumerate(chips):
            slot = land_ref.at[4 * px + 2 * py + c]
            pltpu.make_async_remote_copy(src_ref=slot, dst_ref=slot, send_sem=sems[j], recv_sem=sems[3 + j],
                                         device_id=(x, y, 1 - c), device_id_type=MESH).start()
        token[...] = jnp.zeros_like(token)

    outs = pl.pallas_call(
        body, name=name,
        out_shape=(pltpu.SemaphoreType.DMA(()),) * 6 + (pltpu.HBM(land.shape, land.dtype), jax.ShapeDtypeStruct((8, 128), F32)),
        in_specs=(_HBM, _ANY), out_specs=(_SEM,) * 6 + (_HBM, pl.BlockSpec(memory_space=pltpu.VMEM)),
        input_output_aliases={0: 6}, compiler_params=pltpu.CompilerParams(has_side_effects=_DATAFLOW),
    )(_hbm(land), order)
    return outs[:6], outs[6], outs[7]


def _forward_wait(sems, land_thru, after, name):
    def body(land_ref, *rest):
        sems_ = rest[:6]
        x, y, c, chips = _chips()
        for j, (px, py) in enumerate(chips):
            cp = pltpu.make_async_remote_copy(src_ref=land_ref.at[4 * px + 2 * py + c], dst_ref=land_ref.at[4 * px + 2 * py + 1 - c],
                                              send_sem=sems_[j], recv_sem=sems_[3 + j], device_id=(x, y, 1 - c),
                                              device_id_type=MESH)
            cp.wait_send()
            cp.wait_recv()

    return pl.pallas_call(
        body, name=name, out_shape=pltpu.HBM(land_thru.shape, land_thru.dtype),
        in_specs=(_HBM,) + (_SEM,) * 6 + (_ANY,), out_specs=_HBM, input_output_aliases={0: 0},
        compiler_params=pltpu.CompilerParams(has_side_effects=_DATAFLOW),
    )(land_thru, *sems, after)


def _chip_targets():
    x, y, c, chips = _chips()
    return 2 * x + y, [((px, py, c), 2 * px + py) for px, py in chips]


def _chipsum_start(s, land, order, name):
    def body(s_ref, land_ref, order_ref, *rest):
        sems, token = rest[:6], rest[8]
        mine, targets = _chip_targets()
        for k, (to, ch) in enumerate(targets):
            pltpu.make_async_remote_copy(src_ref=s_ref.at[ch], dst_ref=land_ref.at[mine], send_sem=sems[k], recv_sem=sems[3 + k],
                                         device_id=to, device_id_type=MESH).start()
        token[...] = jnp.zeros_like(token)

    outs = pl.pallas_call(
        body, name=name,
        out_shape=(pltpu.SemaphoreType.DMA(()),) * 6 + (pltpu.HBM(s.shape, s.dtype), pltpu.HBM(land.shape, land.dtype),
                                                        jax.ShapeDtypeStruct((8, 128), F32)),
        in_specs=(_HBM, _HBM, _ANY), out_specs=(_SEM,) * 6 + (_HBM, _HBM, pl.BlockSpec(memory_space=pltpu.VMEM)),
        input_output_aliases={0: 6, 1: 7}, compiler_params=pltpu.CompilerParams(has_side_effects=_DATAFLOW),
    )(_hbm(s), _hbm(land), order)
    return outs[:6], outs[6], outs[7], outs[8]


def _chipsum_wait(sems, s_thru, land_thru, after, name):
    def body(s_ref, land_ref, *rest):
        sems_ = rest[:6]
        mine, targets = _chip_targets()
        for k, (to, ch) in enumerate(targets):
            cp = pltpu.make_async_remote_copy(src_ref=s_ref.at[ch], dst_ref=land_ref.at[ch], send_sem=sems_[k], recv_sem=sems_[3 + k],
                                              device_id=to, device_id_type=MESH)
            cp.wait_send()
            cp.wait_recv()

    return pl.pallas_call(
        body, name=name, out_shape=(pltpu.HBM(s_thru.shape, s_thru.dtype), pltpu.HBM(land_thru.shape, land_thru.dtype)),
        in_specs=(_HBM, _HBM) + (_SEM,) * 6 + (_ANY,), out_specs=(_HBM, _HBM), input_output_aliases={0: 0, 1: 1},
        compiler_params=pltpu.CompilerParams(has_side_effects=_DATAFLOW),
    )(s_thru, land_thru, *sems, after)[1]


def _pair_start(p, order, name):
    def body(p_ref, land_ref, order_ref, *rest):
        sems, token = rest[:8], rest[10]
        x, y, c = _me()
        for ch in range(4):
            pltpu.make_async_remote_copy(src_ref=p_ref.at[ch, 1 - c], dst_ref=land_ref.at[ch], send_sem=sems[ch],
                                         recv_sem=sems[4 + ch], device_id=(x, y, 1 - c), device_id_type=MESH).start()
        token[...] = jnp.zeros_like(token)

    land = lax.empty((4,) + p.shape[2:], p.dtype)
    outs = pl.pallas_call(
        body, name=name,
        out_shape=(pltpu.SemaphoreType.DMA(()),) * 8 + (pltpu.HBM(p.shape, p.dtype), pltpu.HBM(land.shape, land.dtype),
                                                        jax.ShapeDtypeStruct((8, 128), F32)),
        in_specs=(_HBM, _HBM, _ANY), out_specs=(_SEM,) * 8 + (_HBM, _HBM, pl.BlockSpec(memory_space=pltpu.VMEM)),
        input_output_aliases={0: 8, 1: 9}, compiler_params=pltpu.CompilerParams(has_side_effects=_DATAFLOW),
    )(_hbm(p), _hbm(land), order)
    return outs[:8], outs[8], outs[9], outs[10]


def _pair_wait(sems, p_thru, land_thru, after, name):
    def body(p_ref, land_ref, *rest):
        sems_ = rest[:8]
        x, y, c = _me()
        for ch in range(4):
            cp = pltpu.make_async_remote_copy(src_ref=p_ref.at[ch, 1 - c], dst_ref=land_ref.at[ch], send_sem=sems_[ch],
                                              recv_sem=sems_[4 + ch], device_id=(x, y, 1 - c), device_id_type=MESH)
            cp.wait_send()
            cp.wait_recv()

    return pl.pallas_call(
        body, name=name, out_shape=(pltpu.HBM(p_thru.shape, p_thru.dtype), pltpu.HBM(land_thru.shape, land_thru.dtype)),
        in_specs=(_HBM, _HBM) + (_SEM,) * 8 + (_ANY,), out_specs=(_HBM, _HBM), input_output_aliases={0: 0, 1: 1},
        compiler_params=pltpu.CompilerParams(has_side_effects=_DATAFLOW),
    )(p_thru, land_thru, *sems, after)


def _sibling_start(p, order, name):
    def body(p_ref, land_ref, order_ref, send_sem, recv_sem, p_thru, land_thru, token):
        x, y, c = _me()
        pltpu.make_async_remote_copy(src_ref=p_ref, dst_ref=land_ref, send_sem=send_sem, recv_sem=recv_sem,
                                     device_id=(x, y, 1 - c), device_id_type=MESH).start()
        token[...] = jnp.zeros_like(token)

    land = lax.empty(p.shape, p.dtype)
    outs = pl.pallas_call(
        body, name=name,
        out_shape=(pltpu.SemaphoreType.DMA(()),) * 2 + (pltpu.HBM(p.shape, p.dtype), pltpu.HBM(p.shape, p.dtype),
                                                        jax.ShapeDtypeStruct((8, 128), F32)),
        in_specs=(_HBM, _HBM, _ANY), out_specs=(_SEM,) * 2 + (_HBM, _HBM, pl.BlockSpec(memory_space=pltpu.VMEM)),
        input_output_aliases={0: 2, 1: 3}, compiler_params=pltpu.CompilerParams(has_side_effects=_DATAFLOW),
    )(_hbm(p), _hbm(land), order)
    return outs[:2], outs[2], outs[3], outs[4]


def _sibling_wait(sems, p_thru, land_thru, after, name):
    def body(p_ref, land_ref, send_sem, recv_sem, after_ref, p_dead, got_ref):
        x, y, c = _me()
        cp = pltpu.make_async_remote_copy(src_ref=p_ref, dst_ref=land_ref, send_sem=send_sem, recv_sem=recv_sem,
                                          device_id=(x, y, 1 - c), device_id_type=MESH)
        cp.wait_send()
        cp.wait_recv()

    return pl.pallas_call(
        body, name=name, out_shape=(pltpu.HBM(p_thru.shape, p_thru.dtype), pltpu.HBM(land_thru.shape, land_thru.dtype)),
        in_specs=(_HBM, _HBM, _SEM, _SEM, _ANY), out_specs=(_HBM, _HBM), input_output_aliases={0: 0, 1: 1},
        compiler_params=pltpu.CompilerParams(has_side_effects=_DATAFLOW),
    )(p_thru, land_thru, *sems, after)


def _mm(a, b, *, mode, name, out_dtypes=(F32,), epilogue=None, extras=(), tm=1024, tn=1024, tk=2048,
        col_blocked_b=False, col_blocked_out=False, order=None):
    CB = 1024
    if col_blocked_b:
        assert mode in ("nn", "nt") and b.shape[2] == CB
        (M, K), N = a.shape, (b.shape[0] * CB if mode == "nn" else b.shape[1])
        assert mode == "nn" or tk % CB == 0
        tn = CB if mode == "nn" else tn
    elif mode == "nn":
        (M, K), N = a.shape, b.shape[1]
    elif mode == "nt":
        (M, K), N = a.shape, b.shape[0]
    else:
        (K, M), N = a.shape, b.shape[1]
    if col_blocked_out:
        assert len(out_dtypes) == 1 and N % CB == 0
        tn = CB
    tm, tn, tk = min(tm, M), min(tn, N), min(tk, K)
    assert M % tm == 0 and N % tn == 0 and K % tk == 0, (M, N, K, tm, tn, tk)
    nk = K // tk
    ne, no = len(extras), len(out_dtypes)
    dims = {"nn": (((1,), (0,)), ((), ())), "nt": (((1,), (1,)), ((), ())), "tn": (((0,), (0,)), ((), ()))}[mode]

    no_ = 0 if order is None else 1

    def body(a_ref, b_ref, *rest):
        rest = rest[no_:]
        ex, outs = rest[:ne], rest[ne:ne + no]

        def finish(acc):
            res = epilogue(acc, *[e[...] for e in ex]) if epilogue is not None else (acc,)
            for o, r in zip(outs, res):
                o[...] = r.astype(o.dtype)

        if col_blocked_b and mode == "nt":
            part = sum(lax.dot_general(a_ref[:, q * CB:(q + 1) * CB], b_ref[q], dims, preferred_element_type=F32)
                       for q in range(tk // CB))
        else:
            part = lax.dot_general(a_ref[...], b_ref[...].astype(BF16), dims, preferred_element_type=F32)
        if nk == 1:
            finish(part)
        else:
            acc_ref = rest[-1]
            k = pl.program_id(2)

            @pl.when(k == 0)
            def _():
                acc_ref[...] = part

            @pl.when(k > 0)
            def _():
                acc_ref[...] += part

            @pl.when(k == nk - 1)
            def _():
                finish(acc_ref[...])

    a_spec = {"nn": pl.BlockSpec((tm, tk), lambda i, j, k: (i, k)), "nt": pl.BlockSpec((tm, tk), lambda i, j, k: (i, k)),
              "tn": pl.BlockSpec((tk, tm), lambda i, j, k: (k, i))}[mode]
    b_spec = {"nn": pl.BlockSpec((tk, tn), lambda i, j, k: (k, j)), "nt": pl.BlockSpec((tn, tk), lambda i, j, k: (j, k)),
              "tn": pl.BlockSpec((tk, tn), lambda i, j, k: (k, j))}[mode]
    if col_blocked_b:
        b_spec = (pl.BlockSpec((None, tk, CB), lambda i, j, k: (j, k, 0)) if mode == "nn"
                  else pl.BlockSpec((tk // CB, tn, CB), lambda i, j, k: (k, j, 0)))
    e_spec = pl.BlockSpec((tm, tn), lambda i, j, k: (i, j))
    o_spec, o_dims = e_spec, (M, N)
    if col_blocked_out:
        o_spec, o_dims = pl.BlockSpec((None, tm, CB), lambda i, j, k: (j, i, 0)), (N // CB, M, CB)
    outs = pl.pallas_call(
        body, name=name, grid=(M // tm, N // tn, nk),
        in_specs=[a_spec, b_spec] + [_ANY] * no_ + [e_spec] * ne, out_specs=[o_spec] * no,
        out_shape=[jax.ShapeDtypeStruct(o_dims, dt) for dt in out_dtypes],
        scratch_shapes=[pltpu.VMEM((tm, tn), F32)] if nk > 1 else [],
        compiler_params=pltpu.CompilerParams(dimension_semantics=("parallel", "parallel", "arbitrary")),
    )(a, b, *([] if order is None else [order]), *extras)
    return outs if no > 1 else outs[0]


def _dw_half(a, b, side, *, axis, name, add=None, order=None, tile=1024, tk=2048):
    (K, M), N = a.shape, b.shape[1]
    tk = min(tk, K)
    nk = K // tk
    if axis == "m":
        tm, tn = tile, min(N, 1024)
        grid, o_dims = (4, N // tn, nk), (4, tile, N)
        a_spec = pl.BlockSpec((tk, tm), lambda q, j, k, s: (k, 2 * q + s[0]))
        b_spec = pl.BlockSpec((tk, tn), lambda q, j, k, s: (k, j))
        o_spec = pl.BlockSpec((None, tm, tn), lambda q, j, k, s: (q, 0, j))
    else:
        tm, tn = min(M, 1024), tile
        grid, o_dims = (M // tm, 4, nk), (4, M, tile)
        a_spec = pl.BlockSpec((tk, tm), lambda i, q, k, s: (k, i))
        b_spec = pl.BlockSpec((tk, tn), lambda i, q, k, s: (k, 2 * q + s[0]))
        o_spec = pl.BlockSpec((None, tm, tn), lambda i, q, k, s: (q, i, 0))
    n_order, n_add = int(order is not None), int(add is not None)
    n_out = 1 + n_add

    def body(s_ref, a_ref, b_ref, *rest):
        rest = rest[n_order:]
        outs, acc_ref = rest[n_add:n_add + n_out], rest[-1]
        k = pl.program_id(2)
        part = _dot_tn(a_ref[...], b_ref[...])

        @pl.when(k == 0)
        def _():
            acc_ref[...] = part

        @pl.when(k > 0)
        def _():
            acc_ref[...] += part

        @pl.when(k == nk - 1)
        def _():
            res = acc_ref[...] + rest[0][...].astype(F32) if n_add else acc_ref[...]
            for o in outs:
                o[...] = res.astype(o.dtype)

    outs = pl.pallas_call(
        body, name=name, out_shape=[jax.ShapeDtypeStruct(o_dims, BF16)] * n_out,
        grid_spec=pltpu.PrefetchScalarGridSpec(
            num_scalar_prefetch=1, grid=grid, in_specs=[a_spec, b_spec] + [_ANY] * n_order + [o_spec] * n_add,
            out_specs=[o_spec] * n_out, scratch_shapes=[pltpu.VMEM((tm, tn), F32)]),
        compiler_params=pltpu.CompilerParams(dimension_semantics=("arbitrary", "arbitrary", "arbitrary")),
    )(side, a, b, *([order] if n_order else []), *([add] if n_add else []))
    return outs if n_add else outs[0]


def _norm_fwd(x, g, sc, sh, name, resid=None):
    B, S, Dm = x.shape
    ts = min(S, 256)
    tok = pl.BlockSpec((None, ts, Dm), lambda b, i: (b, i, 0))
    row = pl.BlockSpec((None, 1, Dm), lambda b, i: (b, 0, 0))
    par = pl.BlockSpec((1, Dm), lambda b, i: (0, 0))

    def body(*refs):
        if resid is not None:
            x_ref, br_ref, gt_ref, g_ref, sc_ref, sh_ref, xo_ref, h_ref = refs
            xv = x_ref[...] + gt_ref[...] * br_ref[...]
            xo_ref[...] = xv
        else:
            x_ref, g_ref, sc_ref, sh_ref, h_ref = refs
            xv = x_ref[...]
        r = lax.rsqrt(jnp.mean(xv * xv, axis=-1, keepdims=True) + EPS)
        h_ref[...] = ((xv * r * g_ref[...]) * (1.0 + sc_ref[...]) + sh_ref[...]).astype(BF16)

    h_shape = jax.ShapeDtypeStruct((B, S, Dm), BF16)
    if resid is not None:
        return pl.pallas_call(body, name=name, grid=(B, S // ts), in_specs=[tok, tok, row, par, row, row],
                              out_specs=[tok, tok], out_shape=[jax.ShapeDtypeStruct((B, S, Dm), F32), h_shape],
                              )(x, resid[0], resid[1], g, sc, sh)
    return pl.pallas_call(body, name=name, grid=(B, S // ts), in_specs=[tok, par, row, row], out_specs=tok,
                          out_shape=h_shape)(x, g, sc, sh)


def _norm_bwd(x, g, name, *, sc=None, dh=None, dres=None, tgt=None, br=None, gate=None, x_is_prev=False):
    B, S, Dm = x.shape
    ts = min(S, 256)
    final = tgt is not None
    has_br = br is not None
    tok = pl.BlockSpec((None, ts, Dm), lambda b, i: (b, i, 0))
    row = pl.BlockSpec((None, 1, Dm), lambda b, i: (b, 0, 0))
    par = pl.BlockSpec((1, Dm), lambda b, i: (0, 0))
    ins, in_specs = [x, g], [tok, par]
    if final:
        ins, in_specs = ins + [tgt], in_specs + [tok]
    else:
        ins, in_specs = ins + [sc, dh], in_specs + [row, tok]
    if dres is not None:
        ins, in_specs = ins + [dres], in_specs + [tok]
    if has_br:
        ins, in_specs = ins + [br, gate], in_specs + [tok, row]
    n_in = len(ins)
    out_shape = [jax.ShapeDtypeStruct((B, S, Dm), F32), jax.ShapeDtypeStruct((1, Dm), F32)]
    out_specs = [tok, par]
    if final:
        out_shape.append(jax.ShapeDtypeStruct((1, 128), F32))
        out_specs.append(pl.BlockSpec((1, 128), lambda b, i: (0, 0)))
    else:
        out_shape += [jax.ShapeDtypeStruct((B, 1, Dm), F32)] * 2
        out_specs += [row, row]
    if has_br:
        out_shape += [jax.ShapeDtypeStruct((B, S, Dm), BF16), jax.ShapeDtypeStruct((B, 1, Dm), F32)]
        out_specs += [tok, row]

    def body(*refs):
        it = iter(refs[:n_in])
        outs = iter(refs[n_in:])
        x_ref, g_ref = next(it), next(it)
        b, i = pl.program_id(0), pl.program_id(1)
        first, first_row = (b == 0) & (i == 0), i == 0
        xv, gv = x_ref[...], g_ref[...]
        if x_is_prev:
            xv = xv + refs[n_in - 1][...] * refs[n_in - 2][...]
        r = lax.rsqrt(jnp.mean(xv * xv, axis=-1, keepdims=True) + EPS)
        n = xv * r
        dx_ref, dg_ref = next(outs), next(outs)

        def acc(ref, val, init):
            @pl.when(init)
            def _():
                ref[...] = val

            @pl.when(jnp.logical_not(init))
            def _():
                ref[...] += val

        if final:
            t_ref = next(it)
            loss_ref = next(outs)
            e = n * gv - t_ref[...]
            acc(loss_ref, jnp.zeros((1, 128), F32) + 0.5 * jnp.sum(e * e) / Dm, first)
            dyg = e * (1.0 / Dm)
        else:
            sc_ref, dh_ref = next(it), next(it)
            dsc_ref, dsh_ref = next(outs), next(outs)
            dhv = dh_ref[...].astype(F32)
            acc(dsh_ref, jnp.sum(dhv, axis=0, keepdims=True), first_row)
            acc(dsc_ref, jnp.sum(dhv * (n * gv), axis=0, keepdims=True), first_row)
            dyg = dhv * (1.0 + sc_ref[...])
        acc(dg_ref, jnp.sum(dyg * n, axis=0, keepdims=True), first)
        dn = dyg * gv
        dx = r * (dn - n * jnp.mean(dn * n, axis=-1, keepdims=True))
        if dres is not None:
            dx = dx + next(it)[...]
        dx_ref[...] = dx
        if has_br:
            br_ref, gt_ref = next(it), next(it)
            dbr_ref, dgt_ref = next(outs), next(outs)
            dbr_ref[...] = (dx * gt_ref[...]).astype(BF16)
            acc(dgt_ref, jnp.sum(dx * br_ref[...], axis=0, keepdims=True), first_row)

    outs = pl.pallas_call(body, name=name, grid=(B, S // ts), in_specs=in_specs, out_specs=out_specs, out_shape=out_shape,
                          compiler_params=pltpu.CompilerParams(dimension_semantics=("arbitrary", "arbitrary")))(*ins)
    res = dict(dx=outs[0], dg=outs[1])
    if final:
        res["loss"] = outs[2]
    else:
        res["dsc"], res["dsh"] = outs[2], outs[3]
    if has_br:
        res["dbr"], res["dgate"] = outs[-2], outs[-1]
    return res


def _gm_heads(vg, lng, lnb):
    res = []
    for h in range(GM_H):
        sl = slice(h * 128, (h + 1) * 128)
        vh = vg[:, sl]
        xc = vh - jnp.mean(vh, axis=-1, keepdims=True)
        rstd = lax.rsqrt(jnp.mean(xc * xc, axis=-1, keepdims=True) + 1e-5)
        xhat = xc * rstd
        res.append((xhat, rstd, xhat * lng[:, sl] + lnb[:, sl]))
    return res


def _gm_gate(heads, wt_ref, bsx, nch):
    cols = []
    for h in range(GM_H):
        vn = heads[h][2].astype(BF16)
        rows = [_dot(wt_ref[h], vn[c * CHUNK:(c + 1) * CHUNK]) + bsx[:, h * 128:(h + 1) * 128] for c in range(nch)]
        cols.append(jnp.concatenate(rows, axis=0) if nch > 1 else rows[0])
    return jnp.concatenate(cols, axis=1)


def _gm_specs(S):
    tb = min(S, 512)
    u = pl.BlockSpec((None, tb, GM_W), lambda b, i: (b, i, OFF["u"] // GM_W))
    v = pl.BlockSpec((None, tb, GM_W), lambda b, i: (b, i, OFF["v"] // GM_W))
    tok = pl.BlockSpec((None, tb, GM_W), lambda b, i: (b, i, 0))
    return tb, u, v, tok


def _gmlp_fwd(P, lng, lnb, wt, bsx, og, name):
    B, S, _ = P.shape
    tb, u_spec, v_spec, tok = _gm_specs(S)
    nch = tb // CHUNK

    def body(u_ref, v_ref, lng_ref, lnb_ref, wt_ref, bsx_ref, og_ref, o_ref):
        heads = _gm_heads(_gelu(v_ref[...]), lng_ref[...], lnb_ref[...])
        y = _gelu(u_ref[...]) * _gm_gate(heads, wt_ref, bsx_ref[...], nch)
        r = lax.rsqrt(jnp.mean(y * y, axis=-1, keepdims=True) + EPS)
        o_ref[...] = (y * r * og_ref[...]).astype(BF16)

    return pl.pallas_call(
        body, name=name, grid=(B, S // tb),
        in_specs=[u_spec, v_spec, _full((1, GM_W)), _full((1, GM_W)), _full((GM_H, 128, 128)), _full((128, GM_W)), _full((1, GM_W))],
        out_specs=tok, out_shape=jax.ShapeDtypeStruct((B, S, GM_W + ATT_W + SSM_W), BF16))(P, P, lng, lnb, wt, bsx, og)


def _gmlp_bwd(P, dcat, lng, lnb, wt, wtT, bsx, og, name):
    B, S, _ = P.shape
    tb, u_spec, v_spec, tok = _gm_specs(S)
    nch = tb // CHUNK
    do_spec = pl.BlockSpec((None, tb, GM_W), lambda b, i: (b, i, 0))

    def body(u_ref, v_ref, do_ref, lng_ref, lnb_ref, wt_ref, wtT_ref, bsx_ref, og_ref,
             du_ref, dv_ref, dlng_ref, dlnb_ref, dws_ref, dbsx_ref, dog_ref):
        first = (pl.program_id(0) == 0) & (pl.program_id(1) == 0)

        @pl.when(first)
        def _():
            for ref in (dlng_ref, dlnb_ref, dws_ref, dbsx_ref, dog_ref):
                ref[...] = jnp.zeros(ref.shape, F32)

        u, v, lng = u_ref[...], v_ref[...], lng_ref[...]
        ug = _gelu(u)
        heads = _gm_heads(_gelu(v), lng, lnb_ref[...])
        gate = _gm_gate(heads, wt_ref, bsx_ref[...], nch)
        y = ug * gate
        r = lax.rsqrt(jnp.mean(y * y, axis=-1, keepdims=True) + EPS)
        yn = y * r
        dout = do_ref[...].astype(F32)
        dog_ref[...] += jnp.sum(dout * yn, axis=0, keepdims=True)
        dyn = dout * og_ref[...]
        dy = r * (dyn - yn * jnp.mean(dyn * yn, axis=-1, keepdims=True))
        du_ref[...] = (dy * gate * _gelu_grad(u)).astype(BF16)
        dgate = dy * ug
        tril = lax.broadcasted_iota(jnp.int32, (128, 128), 0) >= lax.broadcasted_iota(jnp.int32, (128, 128), 1)
        dvg = []
        for h in range(GM_H):
            sl = slice(h * 128, (h + 1) * 128)
            xhat, rstd, vn = heads[h]
            vnb = vn.astype(BF16)
            dgh = dgate[:, sl]
            dgb = dgh.astype(BF16)
            dbs = jnp.zeros((128, 128), F32)
            dw = jnp.zeros((128, 128), F32)
            dvn = []
            for c in range(nch):
                rs = slice(c * CHUNK, (c + 1) * CHUNK)
                dbs = dbs + dgh[rs]
                dw = dw + _dot_nt(dgb[rs], vnb[rs])
                dvn.append(_dot(wtT_ref[h], dgb[rs]))
            dvn = jnp.concatenate(dvn, axis=0) if nch > 1 else dvn[0]
            dbsx_ref[:, sl] += dbs
            dws_ref[h] += jnp.where(tril, dw, 0.0)
            dlng_ref[:, sl] += jnp.sum(dvn * xhat, axis=0, keepdims=True)
            dlnb_ref[:, sl] += jnp.sum(dvn, axis=0, keepdims=True)
            dxh = dvn * lng[:, sl]
            dvg.append(rstd * (dxh - jnp.mean(dxh, axis=-1, keepdims=True) - xhat * jnp.mean(dxh * xhat, axis=-1, keepdims=True)))
        dv_ref[...] = (jnp.concatenate(dvg, axis=1) * _gelu_grad(v)).astype(BF16)

    p512, w3 = _full((1, GM_W)), _full((GM_H, 128, 128))
    return pl.pallas_call(
        body, name=name, grid=(B, S // tb),
        in_specs=[u_spec, v_spec, do_spec, p512, p512, w3, w3, _full((128, GM_W)), p512],
        out_specs=[tok, tok, p512, p512, w3, _full((128, GM_W)), p512],
        out_shape=[jax.ShapeDtypeStruct((B, S, GM_W), BF16)] * 2 + [
            jax.ShapeDtypeStruct((1, GM_W), F32), jax.ShapeDtypeStruct((1, GM_W), F32),
            jax.ShapeDtypeStruct((GM_H, 128, 128), F32), jax.ShapeDtypeStruct((128, GM_W), F32),
            jax.ShapeDtypeStruct((1, GM_W), F32)],
        compiler_params=pltpu.CompilerParams(dimension_semantics=("arbitrary", "arbitrary")),
    )(P, P, dcat, lng, lnb, wt, wtT, bsx, og)


def _lane_half():
    return lax.broadcasted_iota(jnp.int32, (128, 128), 1) // 64


def _att_stack(x, kvh, dtype):
    half = _lane_half()
    rows = []
    for g in range(4):
        i = kvh * 4 + g
        pair = x[:, (i // 2) * 128:(i // 2 + 1) * 128]
        if i % 2 != kvh:
            pair = pltpu.roll(pair, 64, 1)
        rows.append(jnp.where(half == kvh, pair, 0.0))
    return jnp.concatenate(rows, axis=0).astype(dtype)


def _att_unstack(pairs, y, kvh):
    half = _lane_half()
    for g in range(4):
        i = kvh * 4 + g
        piece = y[g * 128:(g + 1) * 128]
        if i % 2 != kvh:
            piece = pltpu.roll(piece, 64, 1)
        pairs[i // 2] = jnp.where(half == i % 2, piece, pairs[i // 2])
    return pairs


def _att_fill_bias(bias_ref):
    qi = lax.broadcasted_iota(jnp.int32, (512, 256), 0) % 128
    kj = lax.broadcasted_iota(jnp.int32, (512, 256), 1)
    diff = qi + 128 - kj
    band = (diff >= 0) & (diff < 128)
    bias_ref[0:512, :] = jnp.where(band, 0.0, NEG_INF)
    bias_ref[512:1024, :] = jnp.where(band & (kj >= 128), 0.0, NEG_INF)


def _att_bias(bias_ref, n):
    return bias_ref[pl.ds(pl.multiple_of(jnp.where(n == 0, 512, 0), 512), 512), :]


def _att_probs(qb, k2, bias, sink_ref, kvh):
    qm = _att_stack(qb, kvh, BF16)
    s = _dot_nt(qm, k2) * (64 ** -0.5) + bias
    grp = lax.broadcasted_iota(jnp.int32, (512, 1), 0) // 128
    sink = jnp.zeros((512, 1), F32)
    for g in range(4):
        sink = jnp.where(grp == g, sink_ref[kvh * 4 + g], sink)
    m = jnp.maximum(jnp.max(s, axis=-1, keepdims=True), sink)
    e = jnp.exp(s - m)
    esink = jnp.exp(sink - m)
    inv = 1.0 / (jnp.sum(e, axis=-1, keepdims=True) + esink)
    return qm, e * inv, esink * inv


def _att_specs(S):
    q = pl.BlockSpec((None, S, ATT_W), lambda b: (b, 0, OFF["q"] // ATT_W))
    k = pl.BlockSpec((None, S, KV_W), lambda b: (b, 0, OFF["k"] // KV_W))
    v = pl.BlockSpec((None, S, KV_W), lambda b: (b, 0, OFF["vv"] // KV_W))
    tok = pl.BlockSpec((None, S, ATT_W), lambda b: (b, 0, 0))
    kv = pl.BlockSpec((None, S, KV_W), lambda b: (b, 0, 0))
    return q, k, v, tok, kv


_SMEM = pl.BlockSpec(memory_space=pltpu.SMEM)


def _attn_fwd(P, sinks, og, cat, name):
    B, S, _ = P.shape
    q_spec, k_spec, v_spec, _, _ = _att_specs(S)
    tok = pl.BlockSpec((None, S, ATT_W), lambda b: (b, 0, GM_W // ATT_W))

    def body(q_ref, k_ref, v_ref, sink_ref, og_ref, cat_ref, o_ref, kpad, vpad, bias_ref):
        _att_fill_bias(bias_ref)
        kpad[0:128, :] = jnp.zeros((128, KV_W), BF16)
        vpad[0:128, :] = jnp.zeros((128, KV_W), BF16)
        kpad[128:, :] = k_ref[...].astype(BF16)
        vpad[128:, :] = v_ref[...].astype(BF16)

        def step(n, carry):
            st = pl.multiple_of(n * 128, 128)
            qb = q_ref[pl.ds(st, 128), :]
            k2, v2 = kpad[pl.ds(st, 256), :], vpad[pl.ds(st, 256), :]
            pairs = [jnp.zeros((128, 128), F32)] * 4
            bias = _att_bias(bias_ref, n)
            for kvh in range(2):
                _, p, _ = _att_probs(qb, k2, bias, sink_ref, kvh)
                pairs = _att_unstack(pairs, _dot(p.astype(BF16), v2), kvh)
            o = jnp.concatenate(pairs, axis=1)
            r = lax.rsqrt(jnp.mean(o * o, axis=-1, keepdims=True) + EPS)
            o_ref[pl.ds(st, 128), :] = (o * r * og_ref[...]).astype(BF16)
            return carry

        lax.fori_loop(0, S // 128, step, 0)

    return pl.pallas_call(
        body, name=name, grid=(B,), in_specs=[q_spec, k_spec, v_spec, _SMEM, _full((1, ATT_W)), _ANY], out_specs=tok,
        out_shape=jax.ShapeDtypeStruct(cat.shape, BF16), input_output_aliases={5: 0},
        scratch_shapes=[pltpu.VMEM((S + 128, KV_W), BF16)] * 2 + [pltpu.VMEM((1024, 256), F32)])(P, P, P, sinks, og, cat)


def _attn_bwd(P, dcat, sinks, og, name):
    B, S, _ = P.shape
    q_spec, k_spec, v_spec, tok, kv = _att_specs(S)
    do_spec = pl.BlockSpec((None, S, ATT_W), lambda b: (b, 0, GM_W // ATT_W))

    def body(q_ref, k_ref, v_ref, do_ref, sink_ref, og_ref, dq_ref, dk_ref, dv_ref, dsink_ref, dog_ref,
             kpad, vpad, dkpad, dvpad, bias_ref):
        _att_fill_bias(bias_ref)

        @pl.when(pl.program_id(0) == 0)
        def _():
            dsink_ref[...] = jnp.zeros((8, 128), F32)
            dog_ref[...] = jnp.zeros((1, ATT_W), F32)

        kpad[0:128, :] = jnp.zeros((128, KV_W), BF16)
        vpad[0:128, :] = jnp.zeros((128, KV_W), BF16)
        kpad[128:, :] = k_ref[...].astype(BF16)
        vpad[128:, :] = v_ref[...].astype(BF16)
        dkpad[...] = jnp.zeros((S + 128, KV_W), F32)
        dvpad[...] = jnp.zeros((S + 128, KV_W), F32)
        half = _lane_half()
        head_row = lax.broadcasted_iota(jnp.int32, (8, 128), 0)

        def step(n, carry):
            st = pl.multiple_of(n * 128, 128)
            qb = q_ref[pl.ds(st, 128), :]
            k2, v2 = kpad[pl.ds(st, 256), :], vpad[pl.ds(st, 256), :]
            saved, pairs = [], [jnp.zeros((128, 128), F32)] * 4
            bias = _att_bias(bias_ref, n)
            for kvh in range(2):
                qm, p, psink = _att_probs(qb, k2, bias, sink_ref, kvh)
                o = _dot(p.astype(BF16), v2)
                saved.append((qm, p, psink, o))
                pairs = _att_unstack(pairs, o, kvh)
            o = jnp.concatenate(pairs, axis=1)
            r = lax.rsqrt(jnp.mean(o * o, axis=-1, keepdims=True) + EPS)
            on = o * r
            dout = do_ref[pl.ds(st, 128), :].astype(F32)
            dog_ref[...] += jnp.sum(dout * on, axis=0, keepdims=True)
            dyn = dout * og_ref[...]
            do = r * (dyn - on * jnp.mean(dyn * on, axis=-1, keepdims=True))
            dq_pairs = [jnp.zeros((128, 128), F32)] * 4
            dsink = jnp.zeros((8, 128), F32)
            for kvh in range(2):
                qm, p, psink, og_ = saved[kvh]
                dog = _att_stack(do, kvh, F32)
                delta = jnp.sum(dog * jnp.where(jnp.concatenate([half] * 4, axis=0) == kvh, og_, 0.0), axis=-1, keepdims=True)
                dogb, pb = dog.astype(BF16), p.astype(BF16)
                dvpad[pl.ds(st, 256), :] += _dot_tn(pb, dogb)
                dp = _dot_nt(dogb, v2)
                ds = (p * (dp - delta) * (64 ** -0.5)).astype(BF16)
                sd = psink * delta
                for g in range(4):
                    dsink = dsink - jnp.where(head_row == kvh * 4 + g, jnp.sum(sd[g * 128:(g + 1) * 128]), 0.0)
                dq_pairs = _att_unstack(dq_pairs, _dot(ds, k2), kvh)
                dkpad[pl.ds(st, 256), :] += _dot_tn(ds, qm)
            dsink_ref[...] += dsink
            dq_ref[pl.ds(st, 128), :] = jnp.concatenate(dq_pairs, axis=1).astype(BF16)
            return carry

        lax.fori_loop(0, S // 128, step, 0)
        dk_ref[...] = dkpad[128:, :].astype(BF16)
        dv_ref[...] = dvpad[128:, :].astype(BF16)

    return pl.pallas_call(
        body, name=name, grid=(B,),
        in_specs=[q_spec, k_spec, v_spec, do_spec, _SMEM, _full((1, ATT_W))],
        out_specs=[tok, kv, kv, _full((8, 128)), _full((1, ATT_W))],
        out_shape=[jax.ShapeDtypeStruct((B, S, ATT_W), BF16), jax.ShapeDtypeStruct((B, S, KV_W), BF16),
                   jax.ShapeDtypeStruct((B, S, KV_W), BF16), jax.ShapeDtypeStruct((8, 128), F32),
                   jax.ShapeDtypeStruct((1, ATT_W), F32)],
        scratch_shapes=[pltpu.VMEM((S + 128, KV_W), BF16)] * 2 + [pltpu.VMEM((S + 128, KV_W), F32)] * 2
        + [pltpu.VMEM((1024, 256), F32)],
        compiler_params=pltpu.CompilerParams(dimension_semantics=("arbitrary",)),
    )(P, P, P, dcat, sinks, og)


CONV_TC = 256
CONV_RC = 64


def _conv_taps(ext, r0):
    return [ext[pl.ds(r0 + 8 - k, CONV_RC), :] for k in range(4)]


def _conv_pre(taps, w_ref, b_ref):
    acc = b_ref[...] + w_ref[3:4, :] * taps[0]
    for k in range(1, 4):
        acc = acc + w_ref[3 - k:4 - k, :] * taps[k]
    return acc


def _conv_fwd(P, w8, b, name):
    B, S, _ = P.shape
    nj = CONV_CH // CONV_TC
    x_spec = pl.BlockSpec((None, S, CONV_TC), lambda b_, j: (b_, 0, OFF["xbc"] // CONV_TC + j))
    tok = pl.BlockSpec((None, S, CONV_TC), lambda b_, j: (b_, 0, j))

    def body(x_ref, w_ref, b_ref, o_ref, ext):
        ext[0:8, :] = jnp.zeros((8, CONV_TC), F32)
        ext[8:, :] = x_ref[...]
        for r0 in range(0, S, CONV_RC):
            pre = _conv_pre(_conv_taps(ext, r0), w_ref, b_ref)
            o_ref[pl.ds(r0, CONV_RC), :] = pre * _sigmoid(pre)

    return pl.pallas_call(
        body, name=name, grid=(B, nj),
        in_specs=[x_spec, pl.BlockSpec((8, CONV_TC), lambda b_, j: (0, j)), pl.BlockSpec((1, CONV_TC), lambda b_, j: (0, j))],
        out_specs=tok, out_shape=jax.ShapeDtypeStruct((B, S, CONV_CH), F32),
        scratch_shapes=[pltpu.VMEM((S + 8, CONV_TC), F32)])(P, w8, b)


def _conv_bwd(P, dact, w8, b, name):
    B, S, _ = P.shape
    nj = CONV_CH // CONV_TC
    x_spec = pl.BlockSpec((None, S, CONV_TC), lambda j, b_: (b_, 0, OFF["xbc"] // CONV_TC + j))
    tok = pl.BlockSpec((None, S, CONV_TC), lambda j, b_: (b_, 0, j))
    w_spec = pl.BlockSpec((8, CONV_TC), lambda j, b_: (0, j))
    b_spec = pl.BlockSpec((1, CONV_TC), lambda j, b_: (0, j))

    def body(x_ref, d_ref, w_ref, b_ref, dx_ref, dw_ref, db_ref, ext, extd):
        @pl.when(pl.program_id(1) == 0)
        def _():
            dw_ref[...] = jnp.zeros((8, CONV_TC), F32)
            db_ref[...] = jnp.zeros((1, CONV_TC), F32)

        ext[0:8, :] = jnp.zeros((8, CONV_TC), F32)
        ext[8:, :] = x_ref[...]
        extd[pl.ds(8 + S, 8), :] = jnp.zeros((8, CONV_TC), F32)
        db = jnp.zeros((1, CONV_TC), F32)
        dws = [jnp.zeros((1, CONV_TC), F32)] * 4
        for r0 in range(0, S, CONV_RC):
            taps = _conv_taps(ext, r0)
            pre = _conv_pre(taps, w_ref, b_ref)
            sg = _sigmoid(pre)
            dpre = d_ref[pl.ds(r0, CONV_RC), :] * (sg * (1.0 + pre * (1.0 - sg)))
            extd[pl.ds(8 + r0, CONV_RC), :] = dpre
            db = db + jnp.sum(dpre, axis=0, keepdims=True)
            dws = [dws[i] + jnp.sum(dpre * taps[3 - i], axis=0, keepdims=True) for i in range(4)]
        for r0 in range(0, S, CONV_RC):
            dx = w_ref[3:4, :] * extd[pl.ds(8 + r0, CONV_RC), :]
            for k in range(1, 4):
                dx = dx + w_ref[3 - k:4 - k, :] * extd[pl.ds(8 + r0 + k, CONV_RC), :]
            dx_ref[pl.ds(r0, CONV_RC), :] = dx.astype(BF16)
        db_ref[...] += db
        sub = lax.broadcasted_iota(jnp.int32, (8, CONV_TC), 0)
        dw_ref[...] += sum(jnp.where(sub == i, dws[i], 0.0) for i in range(4))

    return pl.pallas_call(
        body, name=name, grid=(nj, B), in_specs=[x_spec, tok, w_spec, b_spec], out_specs=[tok, w_spec, b_spec],
        out_shape=[jax.ShapeDtypeStruct((B, S, CONV_CH), BF16), jax.ShapeDtypeStruct((8, CONV_CH), F32),
                   jax.ShapeDtypeStruct((1, CONV_CH), F32)],
        scratch_shapes=[pltpu.VMEM((S + 8, CONV_TC), F32), pltpu.VMEM((S + 16, CONV_TC), F32)],
        compiler_params=pltpu.CompilerParams(dimension_semantics=("arbitrary", "arbitrary")),
    )(P, dact, w8, b)


def _ssd_consts():
    hd = np.arange(SSM_W) // SSM_HD
    E = (np.arange(128)[:, None] == hd[None, :]).astype(np.float32)
    tri = (np.arange(128)[:, None] >= np.arange(128)[None, :]).astype(np.float32)
    return jnp.asarray(E, BF16), jnp.asarray(E.T, BF16), jnp.asarray(tri, BF16), jnp.asarray(tri.T, BF16)


def _pieces(x, n):
    out, r = [], x
    for _ in range(n):
        p = r.astype(BF16)
        out.append(p)
        r = r - p.astype(F32)
    return out


def _dot01(x, m01, n):
    return sum(_dot(p, m01) for p in _pieces(x, n))


def _dot01_left(m01, x, n):
    return sum(_dot(m01, p) for p in _pieces(x, n))


def _ssd_pre(xa, dtraw, bias, alog, E, tri):
    lane = lax.broadcasted_iota(jnp.int32, (128, 128), 1)
    pre = dtraw + bias
    dtp = jnp.where(lane < SSM_H, jnp.maximum(pre, 0.0) + jnp.log(1.0 + jnp.exp(-jnp.abs(pre))), 0.0)
    a = -jnp.exp(alog)
    acs = _dot01_left(tri, dtp * a, 3)
    acsT = acs.T
    dtE, acsE = _dot01(dtp, E, 2), _dot01(acs, E, 3)
    X = xa[:, :SSM_W]
    xdt = X * dtE
    wE = jnp.exp(acsE[127:128, :] - acsE)
    eE = jnp.exp(acsE)
    cdE = eE[127:128, :]
    return dict(pre=pre, dtp=dtp, a=a, acs=acs, acsT=acsT, dtE=dtE, acsE=acsE, cdE=cdE, X=X, xdt=xdt, wE=wE, eE=eE)


def _ssd_decay(c, h):
    lm = lax.broadcasted_iota(jnp.int32, (128, 128), 0) >= lax.broadcasted_iota(jnp.int32, (128, 128), 1)
    return jnp.exp(jnp.where(lm, c["acs"][:, h:h + 1] - c["acsT"][h:h + 1, :], NEG_INF))


def _ssd_pair_operands(c, CB, h0):
    lane = lax.broadcasted_iota(jnp.int32, (128, 128), 1)
    L0, L1 = _ssd_decay(c, h0), _ssd_decay(c, h0 + 1)
    M = jnp.concatenate([CB * L0, CB * L1], axis=1).astype(BF16)
    xp = c["xdt"][:, h0 * 64:h0 * 64 + 128]
    BD = jnp.concatenate([jnp.where(lane < 64, xp, 0.0), jnp.where(lane >= 64, xp, 0.0)], axis=0).astype(BF16)
    return L0, L1, M, BD


def _ssd_y(c, xa, state_ref, dskipE):
    per_group, ys = [], []
    for g in range(SSM_G):
        gs = slice(g * 512, (g + 1) * 512)
        Bb = xa[:, SSM_W + g * 128:SSM_W + (g + 1) * 128].astype(BF16)
        Cb = xa[:, SSM_W + 256 + g * 128:SSM_W + 256 + (g + 1) * 128].astype(BF16)
        CB = _dot_nt(Cb, Bb)
        Sg = state_ref[:, gs]
        yoff = _dot(Cb, Sg.astype(BF16)) * c["eE"][:, gs]
        ydiag, pairs = [], []
        for j in range(4):
            ops = _ssd_pair_operands(c, CB, g * 8 + 2 * j)
            pairs.append(ops)
            ydiag.append(_dot(ops[2], ops[3]))
        ys.append(jnp.concatenate(ydiag, axis=1) + yoff)
        per_group.append(dict(Bb=Bb, Cb=Cb, CB=CB, Sg=Sg, yoff=yoff, pairs=pairs))
    Y = jnp.concatenate(ys, axis=1) + c["X"] * dskipE
    return Y, per_group


def _ssd_specs(S, rev):
    nc = S // CHUNK
    cm = (lambda b, i: (b, nc - 1 - i)) if rev else (lambda b, i: (b, i))
    xa = pl.BlockSpec((None, CHUNK, CONV_CH), lambda b, i: cm(b, i) + (0,))
    z = [pl.BlockSpec((None, CHUNK, 256), lambda b, i, q=q: cm(b, i) + (OFF["z"] // 256 + q,)) for q in range(4)]
    dt = pl.BlockSpec((None, CHUNK, 128), lambda b, i: cm(b, i) + (OFF["dt"] // 128,))
    tok = pl.BlockSpec((None, CHUNK, SSM_W), lambda b, i: cm(b, i) + (0,))
    st = pl.BlockSpec((None, None, 128, SSM_W), lambda b, i: cm(b, i) + (0, 0))
    return nc, xa, z, dt, tok, st


def _ssd_fwd(xact, P, bias, alog, dskipE, ng, cat, name):
    B, S, _ = P.shape
    nc, xa_spec, z_specs, dt_spec, _, st_spec = _ssd_specs(S, False)
    tok = pl.BlockSpec((None, CHUNK, SSM_W), lambda b, i: (b, i, 1))
    E, _, tri, _ = _ssd_consts()

    def body(xa_ref, z0, z1, z2, z3, dt_ref, bias_ref, alog_ref, dsk_ref, ng_ref, E_ref, tri_ref, cat_ref, o_ref, sp_ref, state):
        @pl.when(pl.program_id(1) == 0)
        def _():
            state[...] = jnp.zeros((128, SSM_W), F32)

        sp_ref[...] = state[...]
        xa = xa_ref[...]
        c = _ssd_pre(xa, dt_ref[...], bias_ref[...], alog_ref[...], E_ref[...], tri_ref[...])
        Y, groups = _ssd_y(c, xa, state, dsk_ref[...])
        Z = (c["xdt"] * c["wE"]).astype(BF16)
        for g in range(SSM_G):
            gs = slice(g * 512, (g + 1) * 512)
            state[:, gs] = groups[g]["Sg"] * c["cdE"][:, gs] + _dot_tn(groups[g]["Bb"], Z[:, gs])
        zv = jnp.concatenate([z0[...], z1[...], z2[...], z3[...]], axis=1)
        yz = Y * (zv * _sigmoid(zv))
        outs = []
        for g in range(SSM_G):
            yg = yz[:, g * 512:(g + 1) * 512]
            outs.append(yg * lax.rsqrt(jnp.mean(yg * yg, axis=-1, keepdims=True) + EPS))
        o_ref[...] = (jnp.concatenate(outs, axis=1) * ng_ref[...]).astype(BF16)

    return pl.pallas_call(
        body, name=name, grid=(B, nc),
        in_specs=[xa_spec] + z_specs + [dt_spec, _full((1, 128)), _full((1, 128)), _full((1, SSM_W)), _full((1, SSM_W)),
                                        _full((128, SSM_W)), _full((128, 128)), _ANY],
        out_specs=[tok, st_spec],
        out_shape=[jax.ShapeDtypeStruct(cat.shape, BF16), jax.ShapeDtypeStruct((B, nc, 128, SSM_W), F32)],
        scratch_shapes=[pltpu.VMEM((128, SSM_W), F32)], input_output_aliases={12: 0},
        compiler_params=pltpu.CompilerParams(dimension_semantics=("arbitrary", "arbitrary")),
    )(xact, P, P, P, P, P, bias, alog, dskipE, ng, E, tri, cat)


def _ssd_bwd(xact, P, sprev, dcat, bias, alog, dskipE, ng, name):
    B, S, _ = P.shape
    nc, xa_spec, z_specs, dt_spec, tok, st_spec = _ssd_specs(S, True)
    do_spec = pl.BlockSpec((None, CHUNK, SSM_W), lambda b, i: (b, nc - 1 - i, 1))
    E, ET, tri, triT = _ssd_consts()
    dt_out = pl.BlockSpec((None, CHUNK, 128), lambda b, i: (b, nc - 1 - i, 0))

    def body(xa_ref, z0, z1, z2, z3, dt_ref, sp_ref, do_ref, bias_ref, alog_ref, dsk_ref, ng_ref, E_ref, ET_ref, tri_ref,
             triT_ref, dxa_ref, dz_ref, ddt_ref, dbias_ref, dalog_ref, ddsk_ref, dng_ref, dstate):
        first = (pl.program_id(0) == 0) & (pl.program_id(1) == 0)

        @pl.when(first)
        def _():
            for ref in (dbias_ref, dalog_ref, ddsk_ref, dng_ref):
                ref[...] = jnp.zeros(ref.shape, F32)

        @pl.when(pl.program_id(1) == 0)
        def _():
            dstate[...] = jnp.zeros((128, SSM_W), F32)

        xa, ETm = xa_ref[...], ET_ref[...]
        c = _ssd_pre(xa, dt_ref[...], bias_ref[...], alog_ref[...], E_ref[...], tri_ref[...])
        Y, groups = _ssd_y(c, xa, sp_ref, dsk_ref[...])
        X, xdt = c["X"], c["xdt"]
        zv = jnp.concatenate([z0[...], z1[...], z2[...], z3[...]], axis=1)
        sg = _sigmoid(zv)
        zs = zv * sg
        yz = Y * zs
        dout = do_ref[...].astype(F32)
        dyz = []
        for g in range(SSM_G):
            gs = slice(g * 512, (g + 1) * 512)
            yg = yz[:, gs]
            r = lax.rsqrt(jnp.mean(yg * yg, axis=-1, keepdims=True) + EPS)
            yn = yg * r
            dng_ref[:, gs] += jnp.sum(dout[:, gs] * yn, axis=0, keepdims=True)
            dyn = dout[:, gs] * ng_ref[:, gs]
            dyz.append(r * (dyn - yn * jnp.mean(dyn * yn, axis=-1, keepdims=True)))
        dyz = jnp.concatenate(dyz, axis=1)
        dz_ref[...] = (dyz * Y * (sg * (1.0 + zv * (1.0 - sg)))).astype(BF16)
        dY = dyz * zs
        ddsk_ref[...] += jnp.sum(dY * X, axis=0, keepdims=True)
        dX = dY * dsk_ref[...]
        lane = lax.broadcasted_iota(jnp.int32, (128, 128), 1)
        sub = lax.broadcasted_iota(jnp.int32, (128, 128), 0)
        colform = jnp.zeros((128, 128), F32)
        rowform = jnp.zeros((128, 128), F32)
        dxdt, gacsE, dBC = [], [], []
        for g in range(SSM_G):
            gs = slice(g * 512, (g + 1) * 512)
            G = groups[g]
            Bb, Cb, CB, Sg = G["Bb"], G["Cb"], G["CB"], G["Sg"]
            dYg = dY[:, gs]
            dQ = (dYg * c["eE"][:, gs]).astype(BF16)
            dSn = dstate[:, gs]
            dSnb = dSn.astype(BF16)
            cd = c["cdE"][:, gs]
            dC = _dot_nt(dQ, Sg.astype(BF16))
            dSprev = _dot_tn(Cb, dQ) + dSn * cd
            t1 = jnp.broadcast_to(jnp.sum(dSn * Sg * cd, axis=0, keepdims=True), (8, 512))
            colform = colform + jnp.where(sub == 127, _dot01(t1, ETm[gs, :], 2)[0:1, :], 0.0)
            Zg = xdt[:, gs] * c["wE"][:, gs]
            dZ = _dot(Bb, dSnb)
            dB = _dot_nt(Zg.astype(BF16), dSnb)
            U = dZ * Zg
            ga = dYg * G["yoff"] - U
            ga = ga + jnp.where(lax.broadcasted_iota(jnp.int32, (128, 512), 0) == 127, jnp.sum(U, axis=0, keepdims=True), 0.0)
            gacsE.append(ga)
            dxg = [None] * 4
            dCB = jnp.zeros((128, 128), F32)
            for j in range(4):
                h0 = g * 8 + 2 * j
                L0, L1, M, BD = G["pairs"][j]
                dYp = dYg[:, j * 128:(j + 1) * 128].astype(BF16)
                dM = _dot_nt(dYp, BD)
                dBD = _dot_tn(M, dYp)
                dxg[j] = jnp.where(lane < 64, dBD[:128], dBD[128:])
                for t, (h, L) in enumerate(((h0, L0), (h0 + 1, L1))):
                    dMh = dM[:, t * 128:(t + 1) * 128]
                    dCB = dCB + dMh * L
                    Gh = dMh * CB * L
                    colform = colform + jnp.where(lane == h, jnp.sum(Gh, axis=1, keepdims=True), 0.0)
                    rowform = rowform - jnp.where(sub == h, jnp.sum(Gh, axis=0, keepdims=True), 0.0)
            dCBb = dCB.astype(BF16)
            dC = dC + _dot(dCBb, Bb)
            dB = dB + _dot_tn(dCBb, Cb)
            dxdt.append(jnp.concatenate(dxg, axis=1) + dZ * c["wE"][:, gs])
            dBC.append((dB, dC))
            dstate[:, gs] = dSprev
        dxdt = jnp.concatenate(dxdt, axis=1)
        dX = dX + dxdt * c["dtE"]
        ddt = _dot01(dxdt * X, ETm, 2)
        dacs = colform + rowform.T + _dot01(jnp.concatenate(gacsE, axis=1), ETm, 2)
        dda = _dot01_left(triT_ref[...], dacs, 2)
        ddt = ddt + dda * c["a"]
        dalog_ref[...] += jnp.sum(dda * c["dtp"], axis=0, keepdims=True) * c["a"]
        ddtraw = jnp.where(lane < SSM_H, ddt * _sigmoid(c["pre"]), 0.0)
        dbias_ref[...] += jnp.sum(ddtraw, axis=0, keepdims=True)
        ddt_ref[...] = ddtraw.astype(BF16)
        dxa_ref[...] = jnp.concatenate([dX, dBC[0][0], dBC[1][0], dBC[0][1], dBC[1][1]], axis=1)

    p128, p1k = _full((1, 128)), _full((1, SSM_W))
    return pl.pallas_call(
        body, name=name, grid=(B, nc),
        in_specs=[xa_spec] + z_specs + [dt_spec, st_spec, do_spec, p128, p128, p1k, p1k,
                                        _full((128, SSM_W)), _full((SSM_W, 128)), _full((128, 128)), _full((128, 128))],
        out_specs=[xa_spec, tok, dt_out, p128, p128, p1k, p1k],
        out_shape=[jax.ShapeDtypeStruct((B, S, CONV_CH), F32), jax.ShapeDtypeStruct((B, S, SSM_W), BF16),
                   jax.ShapeDtypeStruct((B, S, 128), BF16), jax.ShapeDtypeStruct((1, 128), F32),
                   jax.ShapeDtypeStruct((1, 128), F32), jax.ShapeDtypeStruct((1, SSM_W), F32),
                   jax.ShapeDtypeStruct((1, SSM_W), F32)],
        scratch_shapes=[pltpu.VMEM((128, SSM_W), F32)],
        compiler_params=pltpu.CompilerParams(dimension_semantics=("arbitrary", "arbitrary")),
    )(xact, P, P, P, P, P, sprev, dcat, bias, alog, dskipE, ng, E, ET, tri, triT)


def _adamw(w, parts, m, v, name, tr=512, layer=0, prev=None):
    Ltot, R, C = w.shape
    ns = parts.shape[0]
    tr = min(tr, R)
    assert R % tr == 0 and parts.shape[1:] == (R, C)
    c1 = 1.0 / (1.0 - ADAM_B1 ** ADAM_STEP)
    c2 = 1.0 / (1.0 - ADAM_B2 ** ADAM_STEP)

    def body(w_ref, p_ref, m_ref, v_ref, *rest):
        g_ref, d_ref, mo_ref, vo_ref = rest[-4:]
        g = p_ref[0].astype(F32)
        for s in range(1, ns):
            g = g + p_ref[s].astype(F32)
        mn = ADAM_B1 * m_ref[...] + (1.0 - ADAM_B1) * g
        vn = ADAM_B2 * v_ref[...] + (1.0 - ADAM_B2) * (g * g)
        g_ref[...] = g
        mo_ref[...] = mn
        vo_ref[...] = vn
        d_ref[...] = -ADAM_LR * ((mn * c1) / (jnp.sqrt(vn * c2) + ADAM_EPS) + ADAM_WD * w_ref[...])

    blk = pl.BlockSpec((None, tr, C), lambda i: (layer, i, 0))
    extra = [] if prev is None else list(prev)
    return pl.pallas_call(
        body, name=name, grid=(R // tr,),
        in_specs=[blk, pl.BlockSpec((ns, tr, C), lambda i: (0, i, 0)), blk, blk] + [pl.BlockSpec(memory_space=pl.ANY)] * len(extra),
        out_specs=[blk] * 4, out_shape=[jax.ShapeDtypeStruct((Ltot, R, C), F32)] * 4,
        input_output_aliases={4 + k: k for k in range(len(extra))})(w, parts, m, v, *extra)


_SMALL = ("ada_b", "norm1_g", "gm_ln_g", "gm_ln_b", "gm_ws", "gm_bs", "gm_norm_g", "attn_sinks", "attn_norm_g", "conv_b",
          "dt_bias", "a_log", "d_skip", "ssm_norm_g", "norm2_g", "final_norm_g")


def _pack(arrs):
    flat = []
    for a in arrs:
        f = a.reshape(-1).astype(F32)
        flat.append(jnp.pad(f, (0, (-f.shape[0]) % 1024)))
    return jnp.concatenate(flat).reshape(-1, 128)


def _unpack(pack, like):
    out, r = [], 0
    for a in like:
        n = int(np.prod(a.shape))
        rows = (n + 1023) // 1024 * 8
        out.append(lax.slice(pack, (r, 0), (r + rows, 128)).reshape(-1)[:n].reshape(a.shape))
        r += rows
    return out


def kernel(x, c, ada_w, ada_b, norm1_g, w_in, gm_ln_g, gm_ln_b, gm_ws, gm_bs, gm_norm_g, attn_sinks, attn_norm_g, conv_w, conv_b, dt_bias, a_log, d_skip, ssm_norm_g, w_out, norm2_g, w_mlp1, w_mlp2, final_norm_g, loss_target, m_ada_w, m_ada_b, m_norm1_g, m_w_in, m_gm_ln_g, m_gm_ln_b, m_gm_ws, m_gm_bs, m_gm_norm_g, m_attn_sinks, m_attn_norm_g, m_conv_w, m_conv_b, m_dt_bias, m_a_log, m_d_skip, m_ssm_norm_g, m_w_out, m_norm2_g, m_w_mlp1, m_w_mlp2, m_final_norm_g, v_ada_w, v_ada_b, v_norm1_g, v_w_in, v_gm_ln_g, v_gm_ln_b, v_gm_ws, v_gm_bs, v_gm_norm_g, v_attn_sinks, v_attn_norm_g, v_conv_w, v_conv_b, v_dt_bias, v_a_log, v_d_skip, v_ssm_norm_g, v_w_out, v_norm2_g, v_w_mlp1, v_w_mlp2, v_final_norm_g):
    args = dict(locals())
    B, S, _ = x.shape
    T = B * S
    L = DEPTH
    me = 4 * lax.axis_index("x") + 2 * lax.axis_index("y") + lax.axis_index("c")

    gath = _gather2([c, conv_w], "ag_c")
    big = ("w_in", "w_out", "w_mlp1", "w_mlp2")
    chain = [(n, l) for l in range(L) for n in ("w_in", "w_mlp1", "w_out", "w_mlp2")]
    inflight = {}

    def start_next(order):
        if not chain:
            return jnp.zeros((8, 128), F32)
        n, l = chain.pop(0)
        sems, land_thru, token = _gather_start(zone[n, l], order, f"ag_start_{n}{l}")
        inflight[n, l] = (sems, land_thru)
        return token

    def gathered(n, l, after):
        land = _gather_wait(*inflight.pop((n, l)), after, f"ag_wait_{n}{l}")
        return _gather_finish(land, f"ag_fin_{n}{l}")

    forwarding = {}

    def arrived(n, l, after):
        land = _gather_wait(*inflight.pop((n, l)), after, f"ag_wait_{n}{l}")
        sems, land_thru, token = _forward_start(land, after, f"ag_fwd_start_{n}{l}")
        forwarding[n, l] = (sems, land_thru)
        return token

    def ready(n, l, after):
        return _forward_wait(*forwarding.pop((n, l)), after, f"ag_fwd_wait_{n}{l}")

    me1 = me.astype(jnp.int32).reshape(1)
    zone = {(n, l): _landing_zone(args[n], l, me1, f"ag_zone_{n}{l}") for n, l in chain}
    later_zones = [zone[k] for k in chain[1:]]

    tok = start_next(gath[0])
    c_all = gath[0].reshape(NDEV * B, D) + tok[0, 0]
    c_act = (c_all * jax.nn.sigmoid(c_all)).astype(BF16)
    nb_rows = c_act.shape[0]
    c_pad = jnp.pad(c_act, ((0, 128 - nb_rows), (0, 0)))
    adw = ada_w.astype(BF16)
    mod_part = jnp.stack([_mm(c_pad, adw[l], mode="nn", name=f"mod{l}", tn=768)[:nb_rows] for l in range(L)])
    mod_all = _gather_small([mod_part], "ag_mod", order=later_zones)[0]
    mod_mine = lax.dynamic_slice_in_dim(mod_all, me * B, B, axis=2)
    mod = jnp.transpose(mod_mine, (1, 2, 0, 3)).reshape(L, B, 6 * D) + ada_b[:, None, :]
    mods = [[mod[l][:, None, i * D:(i + 1) * D] for i in range(6)] for l in range(L)]

    win_g, wout_g, w1_g, w2_g = [None] * L, [None] * L, [None] * L, [None] * L

    tril = jnp.tril(jnp.ones((128, 128), F32))
    row = lambda a: a.reshape(1, -1)
    pad128 = lambda a: jnp.pad(a.reshape(1, -1), ((0, 0), (0, 128 - a.shape[-1])))
    small = []
    for l in range(L):
        wt = gm_ws[l] * tril
        small.append(dict(
            lng=row(gm_ln_g[l]), lnb=row(gm_ln_b[l]), wt=wt.astype(BF16), wtT=jnp.swapaxes(wt, 1, 2).astype(BF16),
            bsx=jnp.repeat(gm_bs[l].T, 128, axis=1), gog=row(gm_norm_g[l]), sinks=attn_sinks[l], aog=row(attn_norm_g[l]),
            bias=pad128(dt_bias[l]), alog=pad128(a_log[l]), dskE=jnp.repeat(d_skip[l], SSM_HD).reshape(1, SSM_W),
            sng=row(ssm_norm_g[l]), cb=row(conv_b[l])))
    convw_all = jnp.transpose(gath[1], (1, 2, 0, 3)).reshape(L, 4, CONV_CH)
    convw8 = jnp.pad(convw_all, ((0, 0), (0, 4), (0, 0)))

    saved = []
    xl = x
    g_in = gathered("w_in", 0, mod)
    tok = start_next(g_in)
    h = _norm_fwd(xl, row(norm1_g[0]) + tok[0, 0], mods[0][1], mods[0][0], "norm1_f0")
    for l in range(L):
        sm = small[l]
        win_g[l] = _shards_to_cols(g_in, f"w_in_cols{l}")
        P = _mm(h.reshape(T, D), win_g[l], mode="nn", name=f"proj_in{l}", tn=1536, order=tok).reshape(B, S, PW)
        cat = _gmlp_fwd(P, sm["lng"], sm["lnb"], sm["wt"], sm["bsx"], sm["gog"], f"gmlp_f{l}")
        cat = _attn_fwd(P, sm["sinks"], sm["aog"], cat, f"attn_f{l}")
        xact = _conv_fwd(P, convw8[l], sm["cb"], f"conv_f{l}")
        tok = start_next(arrived("w_mlp1", l, xact))
        cat, sprev = _ssd_fwd(xact, P, sm["bias"], sm["alog"], sm["dskE"], sm["sng"] + tok[0:1, 0:1], cat, f"ssd_f{l}")
        g_out = gathered("w_out", l, cat)
        tok = start_next(g_out)
        wout_g[l] = g_out.reshape(D, D)
        mix = _mm(cat.reshape(T, D), wout_g[l], mode="nn", name=f"proj_out{l}", order=tok).reshape(B, S, D)
        x_mid, h2 = _norm_fwd(xl, row(norm2_g[l]), mods[l][4], mods[l][3], f"norm2_f{l}", resid=(mix, mods[l][2]))
        w1_g[l] = ready("w_mlp1", l, h2)
        a_act, r_act = _mm(h2.reshape(T, D), w1_g[l], mode="nn", name=f"mlp1_{l}", out_dtypes=(BF16, BF16), col_blocked_b=True,
                           epilogue=lambda acc: (acc, jnp.square(jnp.maximum(acc, 0.0))))
        g_2 = gathered("w_mlp2", l, r_act)
        tok = start_next(g_2)
        w2_g[l] = g_2.reshape(DFF, D)
        m2 = _mm(r_act, w2_g[l], mode="nn", name=f"mlp2_{l}", order=tok, tk=4096).reshape(B, S, D)
        saved.append(dict(x_in=xl, h=h, P=P, xact=xact, sprev=sprev, cat=cat, mix=mix, x_mid=x_mid, h2=h2, a=a_act, r=r_act, m2=m2))
        if l + 1 < L:
            tok = start_next(arrived("w_in", l + 1, m2))
            xl, h = _norm_fwd(x_mid, row(norm1_g[l + 1]) + tok[0, 0], mods[l + 1][1], mods[l + 1][0], f"norm1_f{l + 1}",
                              resid=(m2, mods[l][5]))
            g_in = ready("w_in", l + 1, h)

    sv = saved[L - 1]
    nb = _norm_bwd(sv["x_mid"], row(final_norm_g), "final_b", tgt=loss_target, br=sv["m2"], gate=mods[L - 1][5], x_is_prev=True)
    loss_part, g_final = nb["loss"], nb["dg"]
    dmod, gsm, gconvw = [None] * L, [None] * L, [None] * L
    core = lax.axis_index("c").astype(jnp.int32).reshape(1)
    reducing = []

    def reduce_start(n, l, sent, after):
        p, from_sib = _pair_wait(*sent[:3], after, f"rs_pair_wait_{n}{l}")
        s, land = _pair_add(p, from_sib, core, f"rs_add_{n}{l}")
        return reduce_exchange(n, l, s, land, after)

    def reduce_exchange(n, l, s, land, order):
        sems, s_thru, land_thru, token = _chipsum_start(s, land, order, f"rs_start_{n}{l}")
        reducing.append((n, l, sems, s_thru, land_thru))
        return token

    other = 1 - core

    for l in reversed(range(L)):
        sv, sm = saved[l], small[l]
        dm2, dxo, dg2 = nb["dbr"].reshape(T, D), nb["dx"], nb["dgate"]
        da = _mm(dm2, w2_g[l], mode="nt", name=f"mlp2_dx{l}", out_dtypes=(BF16,), extras=(sv["a"],),
                 epilogue=lambda acc, a: (acc * (2.0 * jnp.maximum(a.astype(F32), 0.0)),))
        h2f = sv["h2"].reshape(T, D)
        sent2 = _sibling_start(_dw_half(sv["r"], dm2, other, axis="m", name=f"mlp2_dw_sib{l}"), da, f"rs_sib_start_w_mlp2{l}")
        dh2 = _mm(da, w1_g[l], mode="nt", name=f"mlp1_dx{l}", col_blocked_b=True, order=sent2[3],
                  out_dtypes=(BF16,)).reshape(B, S, D)
        from_sib = _sibling_wait(*sent2[:3], dh2, f"rs_sib_wait_w_mlp2{l}")[1]
        sent1 = _sibling_start(_dw_half(h2f, da, other, axis="n", name=f"mlp1_dw_sib{l}", order=from_sib), da,
                               f"rs_sib_start_w_mlp1{l}")
        s2, land2 = _dw_half(sv["r"], dm2, core, axis="m", name=f"mlp2_dw_own{l}", add=from_sib, order=sent1[3])
        tok = reduce_exchange("w_mlp2", l, s2, land2, da)
        nb2 = _norm_bwd(sv["x_mid"], row(norm2_g[l]) + tok[0, 0], f"norm2_b{l}", sc=mods[l][4], dh=dh2, dres=dxo, br=sv["mix"],
                        gate=mods[l][2])
        dmix = nb2["dbr"].reshape(T, D)
        from_sib = _sibling_wait(*sent1[:3], dmix, f"rs_sib_wait_w_mlp1{l}")[1]
        s1, land1 = _dw_half(h2f, da, core, axis="n", name=f"mlp1_dw_own{l}", add=from_sib)
        tok = reduce_exchange("w_mlp1", l, s1, land1, dmix)
        dcat = _mm(dmix, wout_g[l], mode="nt", name=f"proj_out_dx{l}", order=tok, out_dtypes=(BF16,)).reshape(B, S, D)
        du, dv, dlng, dlnb, dws, dbsx, dgog = _gmlp_bwd(sv["P"], dcat, sm["lng"], sm["lnb"], sm["wt"], sm["wtT"], sm["bsx"],
                                                        sm["gog"], f"gmlp_b{l}")
        dq, dk, dvv, dsink, daog = _attn_bwd(sv["P"], dcat, sm["sinks"], sm["aog"], f"attn_b{l}")
        dwo = _mm(sv["cat"].reshape(T, D), dmix, mode="tn", name=f"proj_out_dw{l}", out_dtypes=(BF16,), tk=2048,
                  order=dq).reshape(4, 2, D // NDEV, D)
        sent = _pair_start(dwo, dmix, f"rs_pair_start_w_out{l}")
        dxa, dz, ddt, dbias, dalog, ddsk, dsng = _ssd_bwd(sv["xact"], sv["P"], sv["sprev"], dcat, sm["bias"], sm["alog"],
                                                          sm["dskE"], sm["sng"] + sent[3][0:1, 0:1], f"ssd_b{l}")
        tok = reduce_start("w_out", l, sent, dxa)
        dxbc, dcw, dcb = _conv_bwd(sv["P"], dxa, convw8[l], sm["cb"] + tok[0:1, 0:1], f"conv_b{l}")
        dP = _concat_cols([du, dv, dq, dk, dvv, dz, dxbc, ddt], PW, f"dproj_cols{l}").reshape(T, PW)
        dwin = _mm(sv["h"].reshape(T, D), dP, mode="tn", name=f"proj_in_dw{l}", out_dtypes=(BF16,), tn=1536, tk=2048)
        sent = _sibling_start(dwin, dP, f"rs_sib_start_w_in{l}")
        dh = _mm(dP, win_g[l], mode="nt", name=f"proj_in_dx{l}", tk=2304, order=sent[3], out_dtypes=(BF16,)).reshape(B, S, D)
        s_in, land_in = _cols_to_my_shards(*_sibling_wait(*sent[:3], dh, f"rs_sib_wait_w_in{l}"), core, f"w_in_dshards{l}")
        tok = reduce_exchange("w_in", l, s_in, land_in, dh)
        nb = _norm_bwd(sv["x_in"], row(norm1_g[l]) + tok[0, 0], f"norm1_b{l}", sc=mods[l][1], dh=dh, dres=nb2["dx"],
                       br=saved[l - 1]["m2"] if l > 0 else None, gate=mods[l - 1][5] if l > 0 else None)
        dmod[l] = jnp.concatenate([nb["dsh"], nb["dsc"], nb2["dgate"], nb2["dsh"], nb2["dsc"], dg2], axis=-1)
        gconvw[l] = dcw[:4]
        gsm[l] = dict(
            ada_b=jnp.sum(dmod[l], axis=(0, 1)), norm1_g=nb["dg"], gm_ln_g=dlng, gm_ln_b=dlnb, gm_ws=dws,
            gm_bs=dbsx.reshape(128, GM_H, 128).sum(-1).T, gm_norm_g=dgog, attn_sinks=dsink[:, 0], attn_norm_g=daog,
            conv_b=dcb, dt_bias=dbias[0, :SSM_H], a_log=dalog[0, :SSM_H], d_skip=ddsk.reshape(SSM_H, SSM_HD).sum(-1),
            ssm_norm_g=dsng, norm2_g=nb2["dg"])
    grad_x = nb["dx"]

    big_res, after = dict.fromkeys(big), grad_x
    tile_rows = dict(w_in=256, w_out=256, w_mlp1=256, w_mlp2=128)

    def finish_reduce(n, l, sems, s_thru, land_thru, after):
        parts = _chipsum_wait(sems, s_thru, land_thru, after, f"rs_wait_{n}{l}")
        big_res[n] = _adamw(args[n], parts, args["m_" + n], args["v_" + n], f"adamw_{n}{l}", tr=tile_rows[n], layer=l,
                            prev=big_res[n])
        return big_res[n][0]

    per_layer = [n for n in _SMALL if n != "final_norm_g"]
    g_small = [jnp.stack([gsm[l][n].reshape(args[n].shape[1:]) for l in range(L)]) for n in per_layer] + [g_final.reshape(D)]
    zc = jnp.zeros((L, 4, CONV_CH), F32)
    z1 = jnp.zeros((1, 128), F32)
    gpack = _pack([loss_part] + g_small + [jnp.stack(gconvw)])
    small_zones = [_landing_zone(jnp.stack(dmod).reshape(1, L * B, 6 * D), 0, me1, "ag_zone_dmod", dtype=F32),
                   _landing_zone(gpack[None], 0, me1, "ag_zone_small", tr=gpack.shape[0], dtype=F32)]
    small_sems, small_thru, after = _gather_small_start(small_zones, grad_x, "ag_small_start")

    for item in reducing[:-1]:
        after = finish_reduce(*item, after)

    got = _gather_small_wait(small_sems, small_thru, after, "ag_small_wait")
    got = [got[0].reshape(NDEV, L, B, 6 * D), got[1]]
    like = [z1] + [args[n] for n in _SMALL] + [zc]
    packs = [_pack([z1] + [args[p + n] for n in _SMALL] + [zc]) for p in ("", "m_", "v_")]
    sres = [_unpack(p[0], like) for p in _adamw(packs[0][None], got[1], packs[1][None], packs[2][None], "adamw_small",
                                                tr=gpack.shape[0])]
    res = {n: [r[1 + i] for r in sres] for i, n in enumerate(_SMALL)}
    loss = sres[0][0][0, 0]
    gcw = lax.dynamic_slice_in_dim(sres[0][-1], me * (CONV_CH // NDEV), CONV_CH // NDEV, axis=2)

    def update(name, grads, tr):
        r = None
        for l, g in enumerate(grads):
            r = _adamw(args[name], g[None], args["m_" + name], args["v_" + name], f"adamw_{name}{l}", tr=tr, layer=l, prev=r)
        res[name] = r

    update("conv_w", [gcw[l] for l in range(L)], 4)

    dmod_all = jnp.transpose(got[0], (1, 0, 2, 3)).reshape(L, NDEV * B, 6 * D)
    dm_mine = lax.dynamic_slice_in_dim(dmod_all, me * (6 * D // NDEV), 6 * D // NDEV, axis=2)
    dm_pad = jnp.pad(dm_mine, ((0, 0), (0, 128 - nb_rows), (0, 0))).astype(BF16)
    update("ada_w", [_mm(c_pad, dm_pad[l], mode="tn", name=f"ada_dw{l}", tn=768) for l in range(L)], 256)

    finish_reduce(*reducing[-1], res["ada_w"][0])
    for n in big:
        res[n] = [a.reshape(args[n].shape) for a in big_res[n]]

    names = ['ada_w', 'ada_b', 'norm1_g', 'w_in', 'gm_ln_g', 'gm_ln_b', 'gm_ws', 'gm_bs', 'gm_norm_g', 'attn_sinks',
             'attn_norm_g', 'conv_w', 'conv_b', 'dt_bias', 'a_log', 'd_skip', 'ssm_norm_g', 'w_out', 'norm2_g', 'w_mlp1',
             'w_mlp2', 'final_norm_g']
    return (loss, grad_x, *[res[n][0] for n in names], *[res[n][1] for n in names], *[res[n][2] for n in names],
            *[res[n][3] for n in names])
```

```python
import jax
import jax.numpy as jnp
import numpy as np
from jax import lax
from jax.experimental import pallas as pl
from jax.experimental.pallas import tpu as pltpu

F32, BF16 = jnp.float32, jnp.bfloat16
MESH = pl.DeviceIdType.MESH
NDEV = 8

D = 2048
DEPTH = 2
CHUNK = 128
GM_W, GM_H = 512, 4
ATT_W, KV_W = 512, 128
SSM_W, SSM_H, SSM_HD, SSM_G = 1024, 16, 64, 2
CONV_CH = 1536
IN_W = 4368
DFF = 8192
EPS = 1e-6
NEG_INF = -1e30
GELU_K = 0.7978845608028654
GELU_C = 0.044715

_ORIG = (("u", 512), ("v", 512), ("q", 512), ("k", 128), ("vv", 128), ("z", 1024), ("xbc", 1536), ("dt", 16))
OFF = dict(u=0, v=512, q=1024, k=1536, vv=1664, z=1792, xbc=2816, dt=4352)
PW = 4608

ADAM_LR, ADAM_B1, ADAM_B2, ADAM_EPS, ADAM_WD, ADAM_STEP = 0.001, 0.9, 0.999, 1e-08, 0.01, 10


def _concat_cols(pieces, width, name, ts=256):
    B, S, _ = pieces[0].shape
    ws = [p.shape[-1] for p in pieces]
    dt = pieces[0].dtype
    n = len(pieces)

    def body(*refs):
        cols = [r[...] for r in refs[:n]]
        if width > sum(ws):
            cols.append(jnp.zeros((ts, width - sum(ws)), dt))
        refs[n][...] = jnp.concatenate(cols, axis=1)

    return pl.pallas_call(
        body, name=name, grid=(B, S // ts), in_specs=[pl.BlockSpec((None, ts, w), lambda b, i: (b, i, 0)) for w in ws],
        out_specs=pl.BlockSpec((None, ts, width), lambda b, i: (b, i, 0)), out_shape=jax.ShapeDtypeStruct((B, S, width), dt))(*pieces)


def _shards_to_cols(g, name, tr=256):
    n, R, C = g.shape

    def body(g_ref, o_ref):
        o_ref[...] = jnp.concatenate([g_ref[s] for s in range(n)] + [jnp.zeros((tr, PW - n * C), g.dtype)], axis=1)

    return pl.pallas_call(body, name=name, grid=(R // tr,), in_specs=[pl.BlockSpec((n, tr, C), lambda i: (0, i, 0))],
                          out_specs=pl.BlockSpec((tr, PW), lambda i: (i, 0)), out_shape=jax.ShapeDtypeStruct((R, PW), g.dtype))(g)


def _cols_to_my_shards(w, w_sib, core, name, tr=256):
    R, C = w.shape[0], IN_W // NDEV

    def body(core_ref, w_ref, s_ref, o_ref, o2_ref):
        x = w_ref[...].astype(F32) + s_ref[...].astype(F32)
        mine_is_odd = core_ref[0] == 1
        for q in range(4):
            blk = jnp.where(mine_is_odd, x[:, C * (2 * q + 1):C * (2 * q + 2)], x[:, C * 2 * q:C * (2 * q + 1)]).astype(o_ref.dtype)
            o_ref[q] = blk
            o2_ref[q] = blk

    row = pl.BlockSpec((tr, PW), lambda i, c: (i, 0))
    out = pl.BlockSpec((4, tr, C), lambda i, c: (0, i, 0))
    return pl.pallas_call(
        body, name=name, out_shape=[jax.ShapeDtypeStruct((4, R, C), w.dtype)] * 2,
        grid_spec=pltpu.PrefetchScalarGridSpec(num_scalar_prefetch=1, grid=(R // tr,), in_specs=[row, row], out_specs=[out, out]),
    )(core, w, w_sib)


def _sigmoid(x):
    return 0.5 * (jnp.tanh(0.5 * x) + 1.0)


def _gelu(x):
    return 0.5 * x * (1.0 + jnp.tanh(GELU_K * (x + GELU_C * x * x * x)))


def _gelu_grad(x):
    t = jnp.tanh(GELU_K * (x + GELU_C * x * x * x))
    return 0.5 * (1.0 + t) + 0.5 * x * (1.0 - t * t) * GELU_K * (1.0 + 3.0 * GELU_C * x * x)


def _dot(a, b, prec=None):
    return jnp.dot(a, b, precision=prec, preferred_element_type=F32)


def _dot_nt(a, b, prec=None):
    return lax.dot_general(a, b, (((1,), (1,)), ((), ())), precision=prec, preferred_element_type=F32)


def _dot_tn(a, b, prec=None):
    return lax.dot_general(a, b, (((0,), (0,)), ((), ())), precision=prec, preferred_element_type=F32)


def _full(shape):
    return pl.BlockSpec(shape, lambda *_: (0,) * len(shape))


_HBM = pl.BlockSpec(memory_space=pltpu.HBM)


def _me():
    return lax.axis_index("x"), lax.axis_index("y"), lax.axis_index("c")


def _peer(k):
    x, y, c = _me()
    px = 1 - x if k & 4 else x
    py = 1 - y if k & 2 else y
    pc = 1 - c if k & 1 else c
    return (px, py, pc), 4 * px + 2 * py + pc


def _gather_small(xs, name, order=()):
    n = len(xs)

    def body(*refs):
        ins, outs = refs[:n], refs[-n - 3:-3]
        send, recv, loc = refs[-3:]
        x, y, c = _me()
        me = 4 * x + 2 * y + c
        started = []
        for i in range(n):
            own = pltpu.make_async_copy(ins[i], outs[i].at[me], loc.at[i])
            own.start()
            started.append(own)
        for k in range(1, NDEV):
            dev, lin = _peer(k)
            for i in range(n):
                pltpu.make_async_remote_copy(
                    src_ref=ins[i], dst_ref=outs[i].at[me],
                    send_sem=send.at[i, k - 1], recv_sem=recv.at[i, k - 1], device_id=dev, device_id_type=MESH).start()
        for k in range(1, NDEV):
            dev, lin = _peer(k)
            for i in range(n):
                pltpu.make_async_remote_copy(
                    src_ref=ins[i], dst_ref=outs[i].at[lin],
                    send_sem=send.at[i, k - 1], recv_sem=recv.at[i, k - 1], device_id=dev, device_id_type=MESH).wait()
        for own in started:
            own.wait()

    extra = list(order)
    return pl.pallas_call(
        body, name=name, out_shape=[jax.ShapeDtypeStruct((NDEV,) + a.shape, a.dtype) for a in xs],
        in_specs=[_HBM] * n + [pl.BlockSpec(memory_space=pl.ANY)] * len(extra), out_specs=[_HBM] * n,
        scratch_shapes=[pltpu.SemaphoreType.DMA((n, NDEV - 1)), pltpu.SemaphoreType.DMA((n, NDEV - 1)),
                        pltpu.SemaphoreType.DMA((n,))],
        compiler_params=pltpu.CompilerParams(has_side_effects=True),
    )(*xs, *extra)


def _gather_small_start(lands, order, name):
    n = len(lands)

    def body(*refs):
        ins, sems, token = refs[:n], refs[n + 1:n + 1 + 14 * n], refs[-1]
        x, y, c = _me()
        me = 4 * x + 2 * y + c
        for i in range(n):
            for k in range(1, NDEV):
                dev, _ = _peer(k)
                pltpu.make_async_remote_copy(src_ref=ins[i].at[me], dst_ref=ins[i].at[me], send_sem=sems[14 * i + k - 1],
                                             recv_sem=sems[14 * i + 7 + k - 1], device_id=dev, device_id_type=MESH).start()
        token[...] = jnp.zeros_like(token)

    outs = pl.pallas_call(
        body, name=name,
        out_shape=(pltpu.SemaphoreType.DMA(()),) * (14 * n) + tuple(pltpu.HBM(a.shape, a.dtype) for a in lands)
        + (jax.ShapeDtypeStruct((8, 128), F32),),
        in_specs=(_HBM,) * n + (_ANY,), out_specs=(_SEM,) * (14 * n) + (_HBM,) * n + (pl.BlockSpec(memory_space=pltpu.VMEM),),
        input_output_aliases={i: 14 * n + i for i in range(n)}, compiler_params=pltpu.CompilerParams(has_side_effects=_DATAFLOW),
    )(*[_hbm(a) for a in lands], order)
    return outs[:14 * n], outs[14 * n:15 * n], outs[-1]


def _gather_small_wait(sems, lands_thru, after, name):
    n = len(lands_thru)

    def body(*refs):
        ins, sems_ = refs[:n], refs[n:n + 14 * n]
        x, y, c = _me()
        me = 4 * x + 2 * y + c
        for i in range(n):
            for k in range(1, NDEV):
                dev, lin = _peer(k)
                cp = pltpu.make_async_remote_copy(src_ref=ins[i].at[me], dst_ref=ins[i].at[lin], send_sem=sems_[14 * i + k - 1],
                                                  recv_sem=sems_[14 * i + 7 + k - 1], device_id=dev, device_id_type=MESH)
                cp.wait_send()
                cp.wait_recv()

    outs = pl.pallas_call(
        body, name=name, out_shape=tuple(pltpu.HBM(a.shape, a.dtype) for a in lands_thru),
        in_specs=(_HBM,) * n + (_SEM,) * (14 * n) + (_ANY,), out_specs=(_HBM,) * n,
        input_output_aliases={i: i for i in range(n)}, compiler_params=pltpu.CompilerParams(has_side_effects=_DATAFLOW),
    )(*lands_thru, *sems, after)
    return outs


def _chips():
    x, y, c = _me()
    return x, y, c, [(1 - x, y), (x, 1 - y), (1 - x, 1 - y)]


def _gather2(xs, name, order=None):
    n = len(xs)
    extra = [] if order is None else [order]

    def body(*refs):
        ins, outs = refs[:n], refs[-n - 3:-3]
        send, recv, loc = refs[-3:]
        x, y, c, chips = _chips()
        me, sib = (x, y, c), (x, y, 1 - c)

        def cp(i, k, block, to, src=None):
            slot = outs[i].at[4 * block[0] + 2 * block[1] + block[2]]
            return pltpu.make_async_remote_copy(src_ref=slot if src is None else src, dst_ref=slot, send_sem=send.at[i, k],
                                                recv_sem=recv.at[i, k], device_id=to, device_id_type=MESH)

        sent = []
        for i in range(n):
            for j, chip in enumerate(chips):
                sent.append(cp(i, 1 + j, me, (*chip, c), src=ins[i]))
            sent.append(cp(i, 0, me, sib, src=ins[i]))
        for s in sent:
            s.start()
        own = [pltpu.make_async_copy(ins[i], outs[i].at[4 * x + 2 * y + c], loc.at[i]) for i in range(n)]
        for o in own:
            o.start()
        for j, chip in enumerate(chips):
            for i in range(n):
                cp(i, 1 + j, (*chip, c), me).wait_recv()
                fwd = cp(i, 4 + j, (*chip, c), sib)
                fwd.start()
                sent.append(fwd)
        for i in range(n):
            cp(i, 0, sib, me).wait_recv()
            for j, chip in enumerate(chips):
                cp(i, 4 + j, (*chip, 1 - c), me).wait_recv()
        for s in sent:
            s.wait_send()
        for o in own:
            o.wait()

    return pl.pallas_call(
        body, name=name, out_shape=[jax.ShapeDtypeStruct((NDEV,) + a.shape, a.dtype) for a in xs],
        in_specs=[_HBM] * n + [pl.BlockSpec(memory_space=pl.ANY)] * len(extra), out_specs=[_HBM] * n,
        scratch_shapes=[pltpu.SemaphoreType.DMA((n, 7)), pltpu.SemaphoreType.DMA((n, 7)), pltpu.SemaphoreType.DMA((n,))],
        compiler_params=pltpu.CompilerParams(has_side_effects=True),
    )(*xs, *extra)


def _pair_add(p, r1, core, name, tr=256):
    _, _, R, C = p.shape
    tr = min(tr, R)

    def body(core_ref, p_ref, r_ref, o_ref, o2_ref):
        s = (p_ref[...].astype(F32) + r_ref[...].astype(F32)).astype(o_ref.dtype)
        o_ref[...] = s
        o2_ref[...] = s

    blk = pl.BlockSpec((None, tr, C), lambda ch, i, core_ref: (ch, i, 0))
    return pl.pallas_call(
        body, name=name, out_shape=[jax.ShapeDtypeStruct((4, R, C), p.dtype)] * 2,
        grid_spec=pltpu.PrefetchScalarGridSpec(
            num_scalar_prefetch=1, grid=(4, R // tr),
            in_specs=[pl.BlockSpec((None, None, tr, C), lambda ch, i, core_ref: (ch, core_ref[0], i, 0)), blk],
            out_specs=[blk, blk]),
    )(core, p, r1)


_SEM = pl.BlockSpec(memory_space=pltpu.SEMAPHORE)
_ANY = pl.BlockSpec(memory_space=pl.ANY)
_DATAFLOW = pltpu.SideEffectType.DATAFLOW_SIDE_EFFECTING


def _hbm(a):
    return pltpu.with_memory_space_constraint(a, pltpu.HBM)


def _gather_targets():
    x, y, c, chips = _chips()
    return 4 * x + 2 * y + c, [(x, y, 1 - c)] + [(*chip, c) for chip in chips]


def _landing_zone(w, l, me, name, tr=512, dtype=BF16):
    _, R, C = w.shape
    tr = min(tr, R)

    def body(me_ref, w_ref, o_ref):
        o_ref[...] = w_ref[...].astype(dtype)

    return pl.pallas_call(
        body, name=name, out_shape=jax.ShapeDtypeStruct((NDEV, R, C), dtype),
        grid_spec=pltpu.PrefetchScalarGridSpec(
            num_scalar_prefetch=1, grid=(R // tr,), in_specs=[pl.BlockSpec((None, tr, C), lambda i, me_ref: (l, i, 0))],
            out_specs=pl.BlockSpec((None, tr, C), lambda i, me_ref: (me_ref[0], i, 0))),
    )(me, w)


def _gather_start(land, order, name):
    def body(land_ref, order_ref, *rest):
        sems, token = rest[:8], rest[9]
        me, targets = _gather_targets()
        for k, to in enumerate(targets):
            pltpu.make_async_remote_copy(src_ref=land_ref.at[me], dst_ref=land_ref.at[me], send_sem=sems[k],
                                         recv_sem=sems[4 + k], device_id=to, device_id_type=MESH).start()
        token[...] = jnp.zeros_like(token)

    outs = pl.pallas_call(
        body, name=name,
        out_shape=(pltpu.SemaphoreType.DMA(()),) * 8 + (pltpu.HBM(land.shape, land.dtype), jax.ShapeDtypeStruct((8, 128), F32)),
        in_specs=(_HBM, _ANY), out_specs=(_SEM,) * 8 + (_HBM, pl.BlockSpec(memory_space=pltpu.VMEM)),
        input_output_aliases={0: 8}, compiler_params=pltpu.CompilerParams(has_side_effects=_DATAFLOW),
    )(_hbm(land), order)
    return outs[:8], outs[8], outs[9]


def _gather_wait(sems, land_thru, after, name):
    def body(land_ref, *rest):
        sems_ = rest[:8]
        me, targets = _gather_targets()
        for k, to in enumerate(targets):
            cp = pltpu.make_async_remote_copy(src_ref=land_ref.at[me], dst_ref=land_ref.at[me], send_sem=sems_[k],
                                              recv_sem=sems_[4 + k], device_id=to, device_id_type=MESH)
            cp.wait_send()
            cp.wait_recv()

    return pl.pallas_call(
        body, name=name, out_shape=pltpu.HBM(land_thru.shape, land_thru.dtype),
        in_specs=(_HBM,) + (_SEM,) * 8 + (_ANY,), out_specs=_HBM, input_output_aliases={0: 0},
        compiler_params=pltpu.CompilerParams(has_side_effects=_DATAFLOW),
    )(land_thru, *sems, after)


def _gather_finish(land, name):
    def body(land_ref, out, send, recv):
        x, y, c, chips = _chips()
        fwd = [pltpu.make_async_remote_copy(src_ref=out.at[4 * px + 2 * py + c], dst_ref=out.at[4 * px + 2 * py + c],
                                            send_sem=send.at[j], recv_sem=recv.at[j], device_id=(x, y, 1 - c), device_id_type=MESH)
               for j, (px, py) in enumerate(chips)]
        for cp in fwd:
            cp.start()
        for j, (px, py) in enumerate(chips):
            slot = out.at[4 * px + 2 * py + 1 - c]
            pltpu.make_async_remote_copy(src_ref=slot, dst_ref=slot, send_sem=send.at[j], recv_sem=recv.at[j],
                                         device_id=(x, y, 1 - c), device_id_type=MESH).wait()

    return pl.pallas_call(
        body, name=name, out_shape=jax.ShapeDtypeStruct(land.shape, land.dtype),
        in_specs=[_HBM], out_specs=_HBM, input_output_aliases={0: 0},
        scratch_shapes=[pltpu.SemaphoreType.DMA((3,)), pltpu.SemaphoreType.DMA((3,))],
        compiler_params=pltpu.CompilerParams(has_side_effects=True),
    )(land)


def _forward_start(land, order, name):
    def body(land_ref, order_ref, *rest):
        sems, token = rest[:6], rest[7]
        x, y, c, chips = _chips()
        for j, (px, py) in enumerate(chips):
            slot = land_ref.at[4 * px + 2 * py + c]
            pltpu.make_async_remote_copy(src_ref=slot, dst_ref=slot, send_sem=sems[j], recv_sem=sems[3 + j],
                                         device_id=(x, y, 1 - c), device_id_type=MESH).start()
        token[...] = jnp.zeros_like(token)

    outs = pl.pallas_call(
        body, name=name,
        out_shape=(pltpu.SemaphoreType.DMA(()),) * 6 + (pltpu.HBM(land.shape, land.dtype), jax.ShapeDtypeStruct((8, 128), F32)),
        in_specs=(_HBM, _ANY), out_specs=(_SEM,) * 6 + (_HBM, pl.BlockSpec(memory_space=pltpu.VMEM)),
        input_output_aliases={0: 6}, compiler_params=pltpu.CompilerParams(has_side_effects=_DATAFLOW),
    )(_hbm(land), order)
    return outs[:6], outs[6], outs[7]


def _forward_wait(sems, land_thru, after, name):
    def body(land_ref, *rest):
        sems_ = rest[:6]
        x, y, c, chips = _chips()
        for j, (px, py) in enumerate(chips):
            cp = pltpu.make_async_remote_copy(src_ref=land_ref.at[4 * px + 2 * py + c], dst_ref=land_ref.at[4 * px + 2 * py + 1 - c],
                                              send_sem=sems_[j], recv_sem=sems_[3 + j], device_id=(x, y, 1 - c),
                                              device_id_type=MESH)
            cp.wait_send()
            cp.wait_recv()

    return pl.pallas_call(
        body, name=name, out_shape=pltpu.HBM(land_thru.shape, land_thru.dtype),
        in_specs=(_HBM,) + (_SEM,) * 6 + (_ANY,), out_specs=_HBM, input_output_aliases={0: 0},
        compiler_params=pltpu.CompilerParams(has_side_effects=_DATAFLOW),
    )(land_thru, *sems, after)


def _chip_targets():
    x, y, c, chips = _chips()
    return 2 * x + y, [((px, py, c), 2 * px + py) for px, py in chips]


def _chipsum_start(s, land, order, name):
    def body(s_ref, land_ref, order_ref, *rest):
        sems, token = rest[:6], rest[8]
        mine, targets = _chip_targets()
        for k, (to, ch) in enumerate(targets):
            pltpu.make_async_remote_copy(src_ref=s_ref.at[ch], dst_ref=land_ref.at[mine], send_sem=sems[k], recv_sem=sems[3 + k],
                                         device_id=to, device_id_type=MESH).start()
        token[...] = jnp.zeros_like(token)

    outs = pl.pallas_call(
        body, name=name,
        out_shape=(pltpu.SemaphoreType.DMA(()),) * 6 + (pltpu.HBM(s.shape, s.dtype), pltpu.HBM(land.shape, land.dtype),
                                                        jax.ShapeDtypeStruct((8, 128), F32)),
        in_specs=(_HBM, _HBM, _ANY), out_specs=(_SEM,) * 6 + (_HBM, _HBM, pl.BlockSpec(memory_space=pltpu.VMEM)),
        input_output_aliases={0: 6, 1: 7}, compiler_params=pltpu.CompilerParams(has_side_effects=_DATAFLOW),
    )(_hbm(s), _hbm(land), order)
    return outs[:6], outs[6], outs[7], outs[8]


def _chipsum_wait(sems, s_thru, land_thru, after, name):
    def body(s_ref, land_ref, *rest):
        sems_ = rest[:6]
        mine, targets = _chip_targets()
        for k, (to, ch) in enumerate(targets):
            cp = pltpu.make_async_remote_copy(src_ref=s_ref.at[ch], dst_ref=land_ref.at[ch], send_sem=sems_[k], recv_sem=sems_[3 + k],
                                              device_id=to, device_id_type=MESH)
            cp.wait_send()
            cp.wait_recv()

    return pl.pallas_call(
        body, name=name, out_shape=(pltpu.HBM(s_thru.shape, s_thru.dtype), pltpu.HBM(land_thru.shape, land_thru.dtype)),
        in_specs=(_HBM, _HBM) + (_SEM,) * 6 + (_ANY,), out_specs=(_HBM, _HBM), input_output_aliases={0: 0, 1: 1},
        compiler_params=pltpu.CompilerParams(has_side_effects=_DATAFLOW),
    )(s_thru, land_thru, *sems, after)[1]


def _pair_start(p, order, name):
    def body(p_ref, land_ref, order_ref, *rest):
        sems, token = rest[:8], rest[10]
        x, y, c = _me()
        for ch in range(4):
            pltpu.make_async_remote_copy(src_ref=p_ref.at[ch, 1 - c], dst_ref=land_ref.at[ch], send_sem=sems[ch],
                                         recv_sem=sems[4 + ch], device_id=(x, y, 1 - c), device_id_type=MESH).start()
        token[...] = jnp.zeros_like(token)

    land = lax.empty((4,) + p.shape[2:], p.dtype)
    outs = pl.pallas_call(
        body, name=name,
        out_shape=(pltpu.SemaphoreType.DMA(()),) * 8 + (pltpu.HBM(p.shape, p.dtype), pltpu.HBM(land.shape, land.dtype),
                                                        jax.ShapeDtypeStruct((8, 128), F32)),
        in_specs=(_HBM, _HBM, _ANY), out_specs=(_SEM,) * 8 + (_HBM, _HBM, pl.BlockSpec(memory_space=pltpu.VMEM)),
        input_output_aliases={0: 8, 1: 9}, compiler_params=pltpu.CompilerParams(has_side_effects=_DATAFLOW),
    )(_hbm(p), _hbm(land), order)
    return outs[:8], outs[8], outs[9], outs[10]


def _pair_wait(sems, p_thru, land_thru, after, name):
    def body(p_ref, land_ref, *rest):
        sems_ = rest[:8]
        x, y, c = _me()
        for ch in range(4):
            cp = pltpu.make_async_remote_copy(src_ref=p_ref.at[ch, 1 - c], dst_ref=land_ref.at[ch], send_sem=sems_[ch],
                                              recv_sem=sems_[4 + ch], device_id=(x, y, 1 - c), device_id_type=MESH)
            cp.wait_send()
            cp.wait_recv()

    return pl.pallas_call(
        body, name=name, out_shape=(pltpu.HBM(p_thru.shape, p_thru.dtype), pltpu.HBM(land_thru.shape, land_thru.dtype)),
        in_specs=(_HBM, _HBM) + (_SEM,) * 8 + (_ANY,), out_specs=(_HBM, _HBM), input_output_aliases={0: 0, 1: 1},
        compiler_params=pltpu.CompilerParams(has_side_effects=_DATAFLOW),
    )(p_thru, land_thru, *sems, after)


def _sibling_start(p, order, name):
    def body(p_ref, land_ref, order_ref, send_sem, recv_sem, p_thru, land_thru, token):
        x, y, c = _me()
        pltpu.make_async_remote_copy(src_ref=p_ref, dst_ref=land_ref, send_sem=send_sem, recv_sem=recv_sem,
                                     device_id=(x, y, 1 - c), device_id_type=MESH).start()
        token[...] = jnp.zeros_like(token)

    land = lax.empty(p.shape, p.dtype)
    outs = pl.pallas_call(
        body, name=name,
        out_shape=(pltpu.SemaphoreType.DMA(()),) * 2 + (pltpu.HBM(p.shape, p.dtype), pltpu.HBM(p.shape, p.dtype),
                                                        jax.ShapeDtypeStruct((8, 128), F32)),
        in_specs=(_HBM, _HBM, _ANY), out_specs=(_SEM,) * 2 + (_HBM, _HBM, pl.BlockSpec(memory_space=pltpu.VMEM)),
        input_output_aliases={0: 2, 1: 3}, compiler_params=pltpu.CompilerParams(has_side_effects=_DATAFLOW),
    )(_hbm(p), _hbm(land), order)
    return outs[:2], outs[2], outs[3], outs[4]


def _sibling_wait(sems, p_thru, land_thru, after, name):
    def body(p_ref, land_ref, send_sem, recv_sem, after_ref, p_dead, got_ref):
        x, y, c = _me()
        cp = pltpu.make_async_remote_copy(src_ref=p_ref, dst_ref=land_ref, send_sem=send_sem, recv_sem=recv_sem,
                                          device_id=(x, y, 1 - c), device_id_type=MESH)
        cp.wait_send()
        cp.wait_recv()

    return pl.pallas_call(
        body, name=name, out_shape=(pltpu.HBM(p_thru.shape, p_thru.dtype), pltpu.HBM(land_thru.shape, land_thru.dtype)),
        in_specs=(_HBM, _HBM, _SEM, _SEM, _ANY), out_specs=(_HBM, _HBM), input_output_aliases={0: 0, 1: 1},
        compiler_params=pltpu.CompilerParams(has_side_effects=_DATAFLOW),
    )(p_thru, land_thru, *sems, after)


def _mm(a, b, *, mode, name, out_dtypes=(F32,), epilogue=None, extras=(), tm=1024, tn=1024, tk=2048,
        col_blocked_b=False, col_blocked_out=False, order=None):
    CB = 1024
    if col_blocked_b:
        assert mode in ("nn", "nt") and b.shape[2] == CB
        (M, K), N = a.shape, (b.shape[0] * CB if mode == "nn" else b.shape[1])
        assert mode == "nn" or tk % CB == 0
        tn = CB if mode == "nn" else tn
    elif mode == "nn":
        (M, K), N = a.shape, b.shape[1]
    elif mode == "nt":
        (M, K), N = a.shape, b.shape[0]
    else:
        (K, M), N = a.shape, b.shape[1]
    if col_blocked_out:
        assert len(out_dtypes) == 1 and N % CB == 0
        tn = CB
    tm, tn, tk = min(tm, M), min(tn, N), min(tk, K)
    assert M % tm == 0 and N % tn == 0 and K % tk == 0, (M, N, K, tm, tn, tk)
    nk = K // tk
    ne, no = len(extras), len(out_dtypes)
    dims = {"nn": (((1,), (0,)), ((), ())), "nt": (((1,), (1,)), ((), ())), "tn": (((0,), (0,)), ((), ()))}[mode]

    no_ = 0 if order is None else 1

    def body(a_ref, b_ref, *rest):
        rest = rest[no_:]
        ex, outs = rest[:ne], rest[ne:ne + no]

        def finish(acc):
            res = epilogue(acc, *[e[...] for e in ex]) if epilogue is not None else (acc,)
            for o, r in zip(outs, res):
                o[...] = r.astype(o.dtype)

        if col_blocked_b and mode == "nt":
            part = sum(lax.dot_general(a_ref[:, q * CB:(q + 1) * CB], b_ref[q], dims, preferred_element_type=F32)
                       for q in range(tk // CB))
        else:
            part = lax.dot_general(a_ref[...], b_ref[...].astype(BF16), dims, preferred_element_type=F32)
        if nk == 1:
            finish(part)
        else:
            acc_ref = rest[-1]
            k = pl.program_id(2)

            @pl.when(k == 0)
            def _():
                acc_ref[...] = part

            @pl.when(k > 0)
            def _():
                acc_ref[...] += part

            @pl.when(k == nk - 1)
            def _():
                finish(acc_ref[...])

    a_spec = {"nn": pl.BlockSpec((tm, tk), lambda i, j, k: (i, k)), "nt": pl.BlockSpec((tm, tk), lambda i, j, k: (i, k)),
              "tn": pl.BlockSpec((tk, tm), lambda i, j, k: (k, i))}[mode]
    b_spec = {"nn": pl.BlockSpec((tk, tn), lambda i, j, k: (k, j)), "nt": pl.BlockSpec((tn, tk), lambda i, j, k: (j, k)),
              "tn": pl.BlockSpec((tk, tn), lambda i, j, k: (k, j))}[mode]
    if col_blocked_b:
        b_spec = (pl.BlockSpec((None, tk, CB), lambda i, j, k: (j, k, 0)) if mode == "nn"
                  else pl.BlockSpec((tk // CB, tn, CB), lambda i, j, k: (k, j, 0)))
    e_spec = pl.BlockSpec((tm, tn), lambda i, j, k: (i, j))
    o_spec, o_dims = e_spec, (M, N)
    if col_blocked_out:
        o_spec, o_dims = pl.BlockSpec((None, tm, CB), lambda i, j, k: (j, i, 0)), (N // CB, M, CB)
    outs = pl.pallas_call(
        body, name=name, grid=(M // tm, N // tn, nk),
        in_specs=[a_spec, b_spec] + [_ANY] * no_ + [e_spec] * ne, out_specs=[o_spec] * no,
        out_shape=[jax.ShapeDtypeStruct(o_dims, dt) for dt in out_dtypes],
        scratch_shapes=[pltpu.VMEM((tm, tn), F32)] if nk > 1 else [],
        compiler_params=pltpu.CompilerParams(dimension_semantics=("parallel", "parallel", "arbitrary")),
    )(a, b, *([] if order is None else [order]), *extras)
    return outs if no > 1 else outs[0]


def _dw_half(a, b, side, *, axis, name, add=None, order=None, tile=1024, tk=2048):
    (K, M), N = a.shape, b.shape[1]
    tk = min(tk, K)
    nk = K // tk
    if axis == "m":
        tm, tn = tile, min(N, 1024)
        grid, o_dims = (4, N // tn, nk), (4, tile, N)
        a_spec = pl.BlockSpec((tk, tm), lambda q, j, k, s: (k, 2 * q + s[0]))
        b_spec = pl.BlockSpec((tk, tn), lambda q, j, k, s: (k, j))
        o_spec = pl.BlockSpec((None, tm, tn), lambda q, j, k, s: (q, 0, j))
    else:
        tm, tn = min(M, 1024), tile
        grid, o_dims = (M // tm, 4, nk), (4, M, tile)
        a_spec = pl.BlockSpec((tk, tm), lambda i, q, k, s: (k, i))
        b_spec = pl.BlockSpec((tk, tn), lambda i, q, k, s: (k, 2 * q + s[0]))
        o_spec = pl.BlockSpec((None, tm, tn), lambda i, q, k, s: (q, i, 0))
    n_order, n_add = int(order is not None), int(add is not None)
    n_out = 1 + n_add

    def body(s_ref, a_ref, b_ref, *rest):
        rest = rest[n_order:]
        outs, acc_ref = rest[n_add:n_add + n_out], rest[-1]
        k = pl.program_id(2)
        part = _dot_tn(a_ref[...], b_ref[...])

        @pl.when(k == 0)
        def _():
            acc_ref[...] = part

        @pl.when(k > 0)
        def _():
            acc_ref[...] += part

        @pl.when(k == nk - 1)
        def _():
            res = acc_ref[...] + rest[0][...].astype(F32) if n_add else acc_ref[...]
            for o in outs:
                o[...] = res.astype(o.dtype)

    outs = pl.pallas_call(
        body, name=name, out_shape=[jax.ShapeDtypeStruct(o_dims, BF16)] * n_out,
        grid_spec=pltpu.PrefetchScalarGridSpec(
            num_scalar_prefetch=1, grid=grid, in_specs=[a_spec, b_spec] + [_ANY] * n_order + [o_spec] * n_add,
            out_specs=[o_spec] * n_out, scratch_shapes=[pltpu.VMEM((tm, tn), F32)]),
        compiler_params=pltpu.CompilerParams(dimension_semantics=("arbitrary", "arbitrary", "arbitrary")),
    )(side, a, b, *([order] if n_order else []), *([add] if n_add else []))
    return outs if n_add else outs[0]


def _norm_fwd(x, g, sc, sh, name, resid=None):
    B, S, Dm = x.shape
    ts = min(S, 256)
    tok = pl.BlockSpec((None, ts, Dm), lambda b, i: (b, i, 0))
    row = pl.BlockSpec((None, 1, Dm), lambda b, i: (b, 0, 0))
    par = pl.BlockSpec((1, Dm), lambda b, i: (0, 0))

    def body(*refs):
        if resid is not None:
            x_ref, br_ref, gt_ref, g_ref, sc_ref, sh_ref, xo_ref, h_ref = refs
            xv = x_ref[...] + gt_ref[...] * br_ref[...]
            xo_ref[...] = xv
        else:
            x_ref, g_ref, sc_ref, sh_ref, h_ref = refs
            xv = x_ref[...]
        r = lax.rsqrt(jnp.mean(xv * xv, axis=-1, keepdims=True) + EPS)
        h_ref[...] = ((xv * r * g_ref[...]) * (1.0 + sc_ref[...]) + sh_ref[...]).astype(BF16)

    h_shape = jax.ShapeDtypeStruct((B, S, Dm), BF16)
    if resid is not None:
        return pl.pallas_call(body, name=name, grid=(B, S // ts), in_specs=[tok, tok, row, par, row, row],
                              out_specs=[tok, tok], out_shape=[jax.ShapeDtypeStruct((B, S, Dm), F32), h_shape],
                              )(x, resid[0], resid[1], g, sc, sh)
    return pl.pallas_call(body, name=name, grid=(B, S // ts), in_specs=[tok, par, row, row], out_specs=tok,
                          out_shape=h_shape)(x, g, sc, sh)


def _norm_bwd(x, g, name, *, sc=None, dh=None, dres=None, tgt=None, br=None, gate=None, x_is_prev=False):
    B, S, Dm = x.shape
    ts = min(S, 256)
    final = tgt is not None
    has_br = br is not None
    tok = pl.BlockSpec((None, ts, Dm), lambda b, i: (b, i, 0))
    row = pl.BlockSpec((None, 1, Dm), lambda b, i: (b, 0, 0))
    par = pl.BlockSpec((1, Dm), lambda b, i: (0, 0))
    ins, in_specs = [x, g], [tok, par]
    if final:
        ins, in_specs = ins + [tgt], in_specs + [tok]
    else:
        ins, in_specs = ins + [sc, dh], in_specs + [row, tok]
    if dres is not None:
        ins, in_specs = ins + [dres], in_specs + [tok]
    if has_br:
        ins, in_specs = ins + [br, gate], in_specs + [tok, row]
    n_in = len(ins)
    out_shape = [jax.ShapeDtypeStruct((B, S, Dm), F32), jax.ShapeDtypeStruct((1, Dm), F32)]
    out_specs = [tok, par]
    if final:
        out_shape.append(jax.ShapeDtypeStruct((1, 128), F32))
        out_specs.append(pl.BlockSpec((1, 128), lambda b, i: (0, 0)))
    else:
        out_shape += [jax.ShapeDtypeStruct((B, 1, Dm), F32)] * 2
        out_specs += [row, row]
    if has_br:
        out_shape += [jax.ShapeDtypeStruct((B, S, Dm), BF16), jax.ShapeDtypeStruct((B, 1, Dm), F32)]
        out_specs += [tok, row]

    def body(*refs):
        it = iter(refs[:n_in])
        outs = iter(refs[n_in:])
        x_ref, g_ref = next(it), next(it)
        b, i = pl.program_id(0), pl.program_id(1)
        first, first_row = (b == 0) & (i == 0), i == 0
        xv, gv = x_ref[...], g_ref[...]
        if x_is_prev:
            xv = xv + refs[n_in - 1][...] * refs[n_in - 2][...]
        r = lax.rsqrt(jnp.mean(xv * xv, axis=-1, keepdims=True) + EPS)
        n = xv * r
        dx_ref, dg_ref = next(outs), next(outs)

        def acc(ref, val, init):
            @pl.when(init)
            def _():
                ref[...] = val

            @pl.when(jnp.logical_not(init))
            def _():
                ref[...] += val

        if final:
            t_ref = next(it)
            loss_ref = next(outs)
            e = n * gv - t_ref[...]
            acc(loss_ref, jnp.zeros((1, 128), F32) + 0.5 * jnp.sum(e * e) / Dm, first)
            dyg = e * (1.0 / Dm)
        else:
            sc_ref, dh_ref = next(it), next(it)
            dsc_ref, dsh_ref = next(outs), next(outs)
            dhv = dh_ref[...].astype(F32)
            acc(dsh_ref, jnp.sum(dhv, axis=0, keepdims=True), first_row)
            acc(dsc_ref, jnp.sum(dhv * (n * gv), axis=0, keepdims=True), first_row)
            dyg = dhv * (1.0 + sc_ref[...])
        acc(dg_ref, jnp.sum(dyg * n, axis=0, keepdims=True), first)
        dn = dyg * gv
        dx = r * (dn - n * jnp.mean(dn * n, axis=-1, keepdims=True))
        if dres is not None:
            dx = dx + next(it)[...]
        dx_ref[...] = dx
        if has_br:
            br_ref, gt_ref = next(it), next(it)
            dbr_ref, dgt_ref = next(outs), next(outs)
            dbr_ref[...] = (dx * gt_ref[...]).astype(BF16)
            acc(dgt_ref, jnp.sum(dx * br_ref[...], axis=0, keepdims=True), first_row)

    outs = pl.pallas_call(body, name=name, grid=(B, S // ts), in_specs=in_specs, out_specs=out_specs, out_shape=out_shape,
                          compiler_params=pltpu.CompilerParams(dimension_semantics=("arbitrary", "arbitrary")))(*ins)
    res = dict(dx=outs[0], dg=outs[1])
    if final:
        res["loss"] = outs[2]
    else:
        res["dsc"], res["dsh"] = outs[2], outs[3]
    if has_br:
        res["dbr"], res["dgate"] = outs[-2], outs[-1]
    return res


def _gm_heads(vg, lng, lnb):
    res = []
    for h in range(GM_H):
        sl = slice(h * 128, (h + 1) * 128)
        vh = vg[:, sl]
        xc = vh - jnp.mean(vh, axis=-1, keepdims=True)
        rstd = lax.rsqrt(jnp.mean(xc * xc, axis=-1, keepdims=True) + 1e-5)
        xhat = xc * rstd
        res.append((xhat, rstd, xhat * lng[:, sl] + lnb[:, sl]))
    return res


def _gm_gate(heads, wt_ref, bsx, nch):
    cols = []
    for h in range(GM_H):
        vn = heads[h][2].astype(BF16)
        rows = [_dot(wt_ref[h], vn[c * CHUNK:(c + 1) * CHUNK]) + bsx[:, h * 128:(h + 1) * 128] for c in range(nch)]
        cols.append(jnp.concatenate(rows, axis=0) if nch > 1 else rows[0])
    return jnp.concatenate(cols, axis=1)


def _gm_specs(S):
    tb = min(S, 512)
    u = pl.BlockSpec((None, tb, GM_W), lambda b, i: (b, i, OFF["u"] // GM_W))
    v = pl.BlockSpec((None, tb, GM_W), lambda b, i: (b, i, OFF["v"] // GM_W))
    tok = pl.BlockSpec((None, tb, GM_W), lambda b, i: (b, i, 0))
    return tb, u, v, tok


def _gmlp_fwd(P, lng, lnb, wt, bsx, og, name):
    B, S, _ = P.shape
    tb, u_spec, v_spec, tok = _gm_specs(S)
    nch = tb // CHUNK

    def body(u_ref, v_ref, lng_ref, lnb_ref, wt_ref, bsx_ref, og_ref, o_ref):
        heads = _gm_heads(_gelu(v_ref[...]), lng_ref[...], lnb_ref[...])
        y = _gelu(u_ref[...]) * _gm_gate(heads, wt_ref, bsx_ref[...], nch)
        r = lax.rsqrt(jnp.mean(y * y, axis=-1, keepdims=True) + EPS)
        o_ref[...] = (y * r * og_ref[...]).astype(BF16)

    return pl.pallas_call(
        body, name=name, grid=(B, S // tb),
        in_specs=[u_spec, v_spec, _full((1, GM_W)), _full((1, GM_W)), _full((GM_H, 128, 128)), _full((128, GM_W)), _full((1, GM_W))],
        out_specs=tok, out_shape=jax.ShapeDtypeStruct((B, S, GM_W + ATT_W + SSM_W), BF16))(P, P, lng, lnb, wt, bsx, og)


def _gmlp_bwd(P, dcat, lng, lnb, wt, wtT, bsx, og, name):
    B, S, _ = P.shape
    tb, u_spec, v_spec, tok = _gm_specs(S)
    nch = tb // CHUNK
    do_spec = pl.BlockSpec((None, tb, GM_W), lambda b, i: (b, i, 0))

    def body(u_ref, v_ref, do_ref, lng_ref, lnb_ref, wt_ref, wtT_ref, bsx_ref, og_ref,
             du_ref, dv_ref, dlng_ref, dlnb_ref, dws_ref, dbsx_ref, dog_ref):
        first = (pl.program_id(0) == 0) & (pl.program_id(1) == 0)

        @pl.when(first)
        def _():
            for ref in (dlng_ref, dlnb_ref, dws_ref, dbsx_ref, dog_ref):
                ref[...] = jnp.zeros(ref.shape, F32)

        u, v, lng = u_ref[...], v_ref[...], lng_ref[...]
        ug = _gelu(u)
        heads = _gm_heads(_gelu(v), lng, lnb_ref[...])
        gate = _gm_gate(heads, wt_ref, bsx_ref[...], nch)
        y = ug * gate
        r = lax.rsqrt(jnp.mean(y * y, axis=-1, keepdims=True) + EPS)
        yn = y * r
        dout = do_ref[...].astype(F32)
        dog_ref[...] += jnp.sum(dout * yn, axis=0, keepdims=True)
        dyn = dout * og_ref[...]
        dy = r * (dyn - yn * jnp.mean(dyn * yn, axis=-1, keepdims=True))
        du_ref[...] = (dy * gate * _gelu_grad(u)).astype(BF16)
        dgate = dy * ug
        tril = lax.broadcasted_iota(jnp.int32, (128, 128), 0) >= lax.broadcasted_iota(jnp.int32, (128, 128), 1)
        dvg = []
        for h in range(GM_H):
            sl = slice(h * 128, (h + 1) * 128)
            xhat, rstd, vn = heads[h]
            vnb = vn.astype(BF16)
            dgh = dgate[:, sl]
            dgb = dgh.astype(BF16)
            dbs = jnp.zeros((128, 128), F32)
            dw = jnp.zeros((128, 128), F32)
            dvn = []
            for c in range(nch):
                rs = slice(c * CHUNK, (c + 1) * CHUNK)
                dbs = dbs + dgh[rs]
                dw = dw + _dot_nt(dgb[rs], vnb[rs])
                dvn.append(_dot(wtT_ref[h], dgb[rs]))
            dvn = jnp.concatenate(dvn, axis=0) if nch > 1 else dvn[0]
            dbsx_ref[:, sl] += dbs
            dws_ref[h] += jnp.where(tril, dw, 0.0)
            dlng_ref[:, sl] += jnp.sum(dvn * xhat, axis=0, keepdims=True)
            dlnb_ref[:, sl] += jnp.sum(dvn, axis=0, keepdims=True)
            dxh = dvn * lng[:, sl]
            dvg.append(rstd * (dxh - jnp.mean(dxh, axis=-1, keepdims=True) - xhat * jnp.mean(dxh * xhat, axis=-1, keepdims=True)))
        dv_ref[...] = (jnp.concatenate(dvg, axis=1) * _gelu_grad(v)).astype(BF16)

    p512, w3 = _full((1, GM_W)), _full((GM_H, 128, 128))
    return pl.pallas_call(
        body, name=name, grid=(B, S // tb),
        in_specs=[u_spec, v_spec, do_spec, p512, p512, w3, w3, _full((128, GM_W)), p512],
        out_specs=[tok, tok, p512, p512, w3, _full((128, GM_W)), p512],
        out_shape=[jax.ShapeDtypeStruct((B, S, GM_W), BF16)] * 2 + [
            jax.ShapeDtypeStruct((1, GM_W), F32), jax.ShapeDtypeStruct((1, GM_W), F32),
            jax.ShapeDtypeStruct((GM_H, 128, 128), F32), jax.ShapeDtypeStruct((128, GM_W), F32),
            jax.ShapeDtypeStruct((1, GM_W), F32)],
        compiler_params=pltpu.CompilerParams(dimension_semantics=("arbitrary", "arbitrary")),
    )(P, P, dcat, lng, lnb, wt, wtT, bsx, og)


def _lane_half():
    return lax.broadcasted_iota(jnp.int32, (128, 128), 1) // 64


def _att_stack(x, kvh, dtype):
    half = _lane_half()
    rows = []
    for g in range(4):
        i = kvh * 4 + g
        pair = x[:, (i // 2) * 128:(i // 2 + 1) * 128]
        if i % 2 != kvh:
            pair = pltpu.roll(pair, 64, 1)
        rows.append(jnp.where(half == kvh, pair, 0.0))
    return jnp.concatenate(rows, axis=0).astype(dtype)


def _att_unstack(pairs, y, kvh):
    half = _lane_half()
    for g in range(4):
        i = kvh * 4 + g
        piece = y[g * 128:(g + 1) * 128]
        if i % 2 != kvh:
            piece = pltpu.roll(piece, 64, 1)
        pairs[i // 2] = jnp.where(half == i % 2, piece, pairs[i // 2])
    return pairs


def _att_fill_bias(bias_ref):
    qi = lax.broadcasted_iota(jnp.int32, (512, 256), 0) % 128
    kj = lax.broadcasted_iota(jnp.int32, (512, 256), 1)
    diff = qi + 128 - kj
    band = (diff >= 0) & (diff < 128)
    bias_ref[0:512, :] = jnp.where(band, 0.0, NEG_INF)
    bias_ref[512:1024, :] = jnp.where(band & (kj >= 128), 0.0, NEG_INF)


def _att_bias(bias_ref, n):
    return bias_ref[pl.ds(pl.multiple_of(jnp.where(n == 0, 512, 0), 512), 512), :]


def _att_probs(qb, k2, bias, sink_ref, kvh):
    qm = _att_stack(qb, kvh, BF16)
    s = _dot_nt(qm, k2) * (64 ** -0.5) + bias
    grp = lax.broadcasted_iota(jnp.int32, (512, 1), 0) // 128
    sink = jnp.zeros((512, 1), F32)
    for g in range(4):
        sink = jnp.where(grp == g, sink_ref[kvh * 4 + g], sink)
    m = jnp.maximum(jnp.max(s, axis=-1, keepdims=True), sink)
    e = jnp.exp(s - m)
    esink = jnp.exp(sink - m)
    inv = 1.0 / (jnp.sum(e, axis=-1, keepdims=True) + esink)
    return qm, e * inv, esink * inv


def _att_specs(S):
    q = pl.BlockSpec((None, S, ATT_W), lambda b: (b, 0, OFF["q"] // ATT_W))
    k = pl.BlockSpec((None, S, KV_W), lambda b: (b, 0, OFF["k"] // KV_W))
    v = pl.BlockSpec((None, S, KV_W), lambda b: (b, 0, OFF["vv"] // KV_W))
    tok = pl.BlockSpec((None, S, ATT_W), lambda b: (b, 0, 0))
    kv = pl.BlockSpec((None, S, KV_W), lambda b: (b, 0, 0))
    return q, k, v, tok, kv


_SMEM = pl.BlockSpec(memory_space=pltpu.SMEM)


def _attn_fwd(P, sinks, og, cat, name):
    B, S, _ = P.shape
    q_spec, k_spec, v_spec, _, _ = _att_specs(S)
    tok = pl.BlockSpec((None, S, ATT_W), lambda b: (b, 0, GM_W // ATT_W))

    def body(q_ref, k_ref, v_ref, sink_ref, og_ref, cat_ref, o_ref, kpad, vpad, bias_ref):
        _att_fill_bias(bias_ref)
        kpad[0:128, :] = jnp.zeros((128, KV_W), BF16)
        vpad[0:128, :] = jnp.zeros((128, KV_W), BF16)
        kpad[128:, :] = k_ref[...].astype(BF16)
        vpad[128:, :] = v_ref[...].astype(BF16)

        def step(n, carry):
            st = pl.multiple_of(n * 128, 128)
            qb = q_ref[pl.ds(st, 128), :]
            k2, v2 = kpad[pl.ds(st, 256), :], vpad[pl.ds(st, 256), :]
            pairs = [jnp.zeros((128, 128), F32)] * 4
            bias = _att_bias(bias_ref, n)
            for kvh in range(2):
                _, p, _ = _att_probs(qb, k2, bias, sink_ref, kvh)
                pairs = _att_unstack(pairs, _dot(p.astype(BF16), v2), kvh)
            o = jnp.concatenate(pairs, axis=1)
            r = lax.rsqrt(jnp.mean(o * o, axis=-1, keepdims=True) + EPS)
            o_ref[pl.ds(st, 128), :] = (o * r * og_ref[...]).astype(BF16)
            return carry

        lax.fori_loop(0, S // 128, step, 0)

    return pl.pallas_call(
        body, name=name, grid=(B,), in_specs=[q_spec, k_spec, v_spec, _SMEM, _full((1, ATT_W)), _ANY], out_specs=tok,
        out_shape=jax.ShapeDtypeStruct(cat.shape, BF16), input_output_aliases={5: 0},
        scratch_shapes=[pltpu.VMEM((S + 128, KV_W), BF16)] * 2 + [pltpu.VMEM((1024, 256), F32)])(P, P, P, sinks, og, cat)


def _attn_bwd(P, dcat, sinks, og, name):
    B, S, _ = P.shape
    q_spec, k_spec, v_spec, tok, kv = _att_specs(S)
    do_spec = pl.BlockSpec((None, S, ATT_W), lambda b: (b, 0, GM_W // ATT_W))

    def body(q_ref, k_ref, v_ref, do_ref, sink_ref, og_ref, dq_ref, dk_ref, dv_ref, dsink_ref, dog_ref,
             kpad, vpad, dkpad, dvpad, bias_ref):
        _att_fill_bias(bias_ref)

        @pl.when(pl.program_id(0) == 0)
        def _():
            dsink_ref[...] = jnp.zeros((8, 128), F32)
            dog_ref[...] = jnp.zeros((1, ATT_W), F32)

        kpad[0:128, :] = jnp.zeros((128, KV_W), BF16)
        vpad[0:128, :] = jnp.zeros((128, KV_W), BF16)
        kpad[128:, :] = k_ref[...].astype(BF16)
        vpad[128:, :] = v_ref[...].astype(BF16)
        dkpad[...] = jnp.zeros((S + 128, KV_W), F32)
        dvpad[...] = jnp.zeros((S + 128, KV_W), F32)
        half = _lane_half()
        head_row = lax.broadcasted_iota(jnp.int32, (8, 128), 0)

        def step(n, carry):
            st = pl.multiple_of(n * 128, 128)
            qb = q_ref[pl.ds(st, 128), :]
            k2, v2 = kpad[pl.ds(st, 256), :], vpad[pl.ds(st, 256), :]
            saved, pairs = [], [jnp.zeros((128, 128), F32)] * 4
            bias = _att_bias(bias_ref, n)
            for kvh in range(2):
                qm, p, psink = _att_probs(qb, k2, bias, sink_ref, kvh)
                o = _dot(p.astype(BF16), v2)
                saved.append((qm, p, psink, o))
                pairs = _att_unstack(pairs, o, kvh)
            o = jnp.concatenate(pairs, axis=1)
            r = lax.rsqrt(jnp.mean(o * o, axis=-1, keepdims=True) + EPS)
            on = o * r
            dout = do_ref[pl.ds(st, 128), :].astype(F32)
            dog_ref[...] += jnp.sum(dout * on, axis=0, keepdims=True)
            dyn = dout * og_ref[...]
            do = r * (dyn - on * jnp.mean(dyn * on, axis=-1, keepdims=True))
            dq_pairs = [jnp.zeros((128, 128), F32)] * 4
            dsink = jnp.zeros((8, 128), F32)
            for kvh in range(2):
                qm, p, psink, og_ = saved[kvh]
                dog = _att_stack(do, kvh, F32)
                delta = jnp.sum(dog * jnp.where(jnp.concatenate([half] * 4, axis=0) == kvh, og_, 0.0), axis=-1, keepdims=True)
                dogb, pb = dog.astype(BF16), p.astype(BF16)
                dvpad[pl.ds(st, 256), :] += _dot_tn(pb, dogb)
                dp = _dot_nt(dogb, v2)
                ds = (p * (dp - delta) * (64 ** -0.5)).astype(BF16)
                sd = psink * delta
                for g in range(4):
                    dsink = dsink - jnp.where(head_row == kvh * 4 + g, jnp.sum(sd[g * 128:(g + 1) * 128]), 0.0)
                dq_pairs = _att_unstack(dq_pairs, _dot(ds, k2), kvh)
                dkpad[pl.ds(st, 256), :] += _dot_tn(ds, qm)
            dsink_ref[...] += dsink
            dq_ref[pl.ds(st, 128), :] = jnp.concatenate(dq_pairs, axis=1).astype(BF16)
            return carry

        lax.fori_loop(0, S // 128, step, 0)
        dk_ref[...] = dkpad[128:, :].astype(BF16)
        dv_ref[...] = dvpad[128:, :].astype(BF16)

    return pl.pallas_call(
        body, name=name, grid=(B,),
        in_specs=[q_spec, k_spec, v_spec, do_spec, _SMEM, _full((1, ATT_W))],
        out_specs=[tok, kv, kv, _full((8, 128)), _full((1, ATT_W))],
        out_shape=[jax.ShapeDtypeStruct((B, S, ATT_W), BF16), jax.ShapeDtypeStruct((B, S, KV_W), BF16),
                   jax.ShapeDtypeStruct((B, S, KV_W), BF16), jax.ShapeDtypeStruct((8, 128), F32),
                   jax.ShapeDtypeStruct((1, ATT_W), F32)],
        scratch_shapes=[pltpu.VMEM((S + 128, KV_W), BF16)] * 2 + [pltpu.VMEM((S + 128, KV_W), F32)] * 2
        + [pltpu.VMEM((1024, 256), F32)],
        compiler_params=pltpu.CompilerParams(dimension_semantics=("arbitrary",)),
    )(P, P, P, dcat, sinks, og)


CONV_TC = 256
CONV_RC = 32


def _conv_taps(ext, r0):
    return [ext[pl.ds(r0 + 8 - k, CONV_RC), :] for k in range(4)]


def _conv_pre(taps, w_ref, b_ref):
    acc = b_ref[...] + w_ref[3:4, :] * taps[0]
    for k in range(1, 4):
        acc = acc + w_ref[3 - k:4 - k, :] * taps[k]
    return acc


def _conv_fwd(P, w8, b, name):
    B, S, _ = P.shape
    nj = CONV_CH // CONV_TC
    x_spec = pl.BlockSpec((None, S, CONV_TC), lambda b_, j: (b_, 0, OFF["xbc"] // CONV_TC + j))
    tok = pl.BlockSpec((None, S, CONV_TC), lambda b_, j: (b_, 0, j))

    def body(x_ref, w_ref, b_ref, o_ref, ext):
        ext[0:8, :] = jnp.zeros((8, CONV_TC), F32)
        ext[8:, :] = x_ref[...]
        for r0 in range(0, S, CONV_RC):
            pre = _conv_pre(_conv_taps(ext, r0), w_ref, b_ref)
            o_ref[pl.ds(r0, CONV_RC), :] = pre * _sigmoid(pre)

    return pl.pallas_call(
        body, name=name, grid=(B, nj),
        in_specs=[x_spec, pl.BlockSpec((8, CONV_TC), lambda b_, j: (0, j)), pl.BlockSpec((1, CONV_TC), lambda b_, j: (0, j))],
        out_specs=tok, out_shape=jax.ShapeDtypeStruct((B, S, CONV_CH), F32),
        scratch_shapes=[pltpu.VMEM((S + 8, CONV_TC), F32)])(P, w8, b)


def _conv_bwd(P, dact, w8, b, name):
    B, S, _ = P.shape
    nj = CONV_CH // CONV_TC
    x_spec = pl.BlockSpec((None, S, CONV_TC), lambda j, b_: (b_, 0, OFF["xbc"] // CONV_TC + j))
    tok = pl.BlockSpec((None, S, CONV_TC), lambda j, b_: (b_, 0, j))
    w_spec = pl.BlockSpec((8, CONV_TC), lambda j, b_: (0, j))
    b_spec = pl.BlockSpec((1, CONV_TC), lambda j, b_: (0, j))

    def body(x_ref, d_ref, w_ref, b_ref, dx_ref, dw_ref, db_ref, ext, extd):
        @pl.when(pl.program_id(1) == 0)
        def _():
            dw_ref[...] = jnp.zeros((8, CONV_TC), F32)
            db_ref[...] = jnp.zeros((1, CONV_TC), F32)

        ext[0:8, :] = jnp.zeros((8, CONV_TC), F32)
        ext[8:, :] = x_ref[...]
        extd[pl.ds(8 + S, 8), :] = jnp.zeros((8, CONV_TC), F32)
        db = jnp.zeros((1, CONV_TC), F32)
        dws = [jnp.zeros((1, CONV_TC), F32)] * 4
        for r0 in range(0, S, CONV_RC):
            taps = _conv_taps(ext, r0)
            pre = _conv_pre(taps, w_ref, b_ref)
            sg = _sigmoid(pre)
            dpre = d_ref[pl.ds(r0, CONV_RC), :] * (sg * (1.0 + pre * (1.0 - sg)))
            extd[pl.ds(8 + r0, CONV_RC), :] = dpre
            db = db + jnp.sum(dpre, axis=0, keepdims=True)
            dws = [dws[i] + jnp.sum(dpre * taps[3 - i], axis=0, keepdims=True) for i in range(4)]
        for r0 in range(0, S, CONV_RC):
            dx = w_ref[3:4, :] * extd[pl.ds(8 + r0, CONV_RC), :]
            for k in range(1, 4):
                dx = dx + w_ref[3 - k:4 - k, :] * extd[pl.ds(8 + r0 + k, CONV_RC), :]
            dx_ref[pl.ds(r0, CONV_RC), :] = dx.astype(BF16)
        db_ref[...] += db
        sub = lax.broadcasted_iota(jnp.int32, (8, CONV_TC), 0)
        dw_ref[...] += sum(jnp.where(sub == i, dws[i], 0.0) for i in range(4))

    return pl.pallas_call(
        body, name=name, grid=(nj, B), in_specs=[x_spec, tok, w_spec, b_spec], out_specs=[tok, w_spec, b_spec],
        out_shape=[jax.ShapeDtypeStruct((B, S, CONV_CH), BF16), jax.ShapeDtypeStruct((8, CONV_CH), F32),
                   jax.ShapeDtypeStruct((1, CONV_CH), F32)],
        scratch_shapes=[pltpu.VMEM((S + 8, CONV_TC), F32), pltpu.VMEM((S + 16, CONV_TC), F32)],
        compiler_params=pltpu.CompilerParams(dimension_semantics=("arbitrary", "arbitrary")),
    )(P, dact, w8, b)


def _ssd_consts():
    hd = np.arange(SSM_W) // SSM_HD
    E = (np.arange(128)[:, None] == hd[None, :]).astype(np.float32)
    tri = (np.arange(128)[:, None] >= np.arange(128)[None, :]).astype(np.float32)
    return jnp.asarray(E, BF16), jnp.asarray(E.T, BF16), jnp.asarray(tri, BF16), jnp.asarray(tri.T, BF16)


def _pieces(x, n):
    out, r = [], x
    for _ in range(n):
        p = r.astype(BF16)
        out.append(p)
        r = r - p.astype(F32)
    return out


def _dot01(x, m01, n):
    return sum(_dot(p, m01) for p in _pieces(x, n))


def _dot01_left(m01, x, n):
    return sum(_dot(m01, p) for p in _pieces(x, n))


def _ssd_pre(xa, dtraw, bias, alog, E, tri):
    lane = lax.broadcasted_iota(jnp.int32, (128, 128), 1)
    pre = dtraw + bias
    dtp = jnp.where(lane < SSM_H, jnp.maximum(pre, 0.0) + jnp.log(1.0 + jnp.exp(-jnp.abs(pre))), 0.0)
    a = -jnp.exp(alog)
    acs = _dot01_left(tri, dtp * a, 3)
    acsT = acs.T
    dtE, acsE = _dot01(dtp, E, 2), _dot01(acs, E, 3)
    X = xa[:, :SSM_W]
    xdt = X * dtE
    wE = jnp.exp(acsE[127:128, :] - acsE)
    eE = jnp.exp(acsE)
    cdE = eE[127:128, :]
    return dict(pre=pre, dtp=dtp, a=a, acs=acs, acsT=acsT, dtE=dtE, acsE=acsE, cdE=cdE, X=X, xdt=xdt, wE=wE, eE=eE)


def _ssd_decay(c, h):
    lm = lax.broadcasted_iota(jnp.int32, (128, 128), 0) >= lax.broadcasted_iota(jnp.int32, (128, 128), 1)
    return jnp.exp(jnp.where(lm, c["acs"][:, h:h + 1] - c["acsT"][h:h + 1, :], NEG_INF))


def _ssd_pair_operands(c, CB, h0):
    lane = lax.broadcasted_iota(jnp.int32, (128, 128), 1)
    L0, L1 = _ssd_decay(c, h0), _ssd_decay(c, h0 + 1)
    M = jnp.concatenate([CB * L0, CB * L1], axis=1).astype(BF16)
    xp = c["xdt"][:, h0 * 64:h0 * 64 + 128]
    BD = jnp.concatenate([jnp.where(lane < 64, xp, 0.0), jnp.where(lane >= 64, xp, 0.0)], axis=0).astype(BF16)
    return L0, L1, M, BD


def _ssd_y(c, xa, state_ref, dskipE):
    per_group, ys = [], []
    for g in range(SSM_G):
        gs = slice(g * 512, (g + 1) * 512)
        Bb = xa[:, SSM_W + g * 128:SSM_W + (g + 1) * 128].astype(BF16)
        Cb = xa[:, SSM_W + 256 + g * 128:SSM_W + 256 + (g + 1) * 128].astype(BF16)
        CB = _dot_nt(Cb, Bb)
        Sg = state_ref[:, gs]
        yoff = _dot(Cb, Sg.astype(BF16)) * c["eE"][:, gs]
        ydiag, pairs = [], []
        for j in range(4):
            ops = _ssd_pair_operands(c, CB, g * 8 + 2 * j)
            pairs.append(ops)
            ydiag.append(_dot(ops[2], ops[3]))
        ys.append(jnp.concatenate(ydiag, axis=1) + yoff)
        per_group.append(dict(Bb=Bb, Cb=Cb, CB=CB, Sg=Sg, yoff=yoff, pairs=pairs))
    Y = jnp.concatenate(ys, axis=1) + c["X"] * dskipE
    return Y, per_group


def _ssd_specs(S, rev):
    nc = S // CHUNK
    cm = (lambda b, i: (b, nc - 1 - i)) if rev else (lambda b, i: (b, i))
    xa = pl.BlockSpec((None, CHUNK, CONV_CH), lambda b, i: cm(b, i) + (0,))
    z = [pl.BlockSpec((None, CHUNK, 256), lambda b, i, q=q: cm(b, i) + (OFF["z"] // 256 + q,)) for q in range(4)]
    dt = pl.BlockSpec((None, CHUNK, 128), lambda b, i: cm(b, i) + (OFF["dt"] // 128,))
    tok = pl.BlockSpec((None, CHUNK, SSM_W), lambda b, i: cm(b, i) + (0,))
    st = pl.BlockSpec((None, None, 128, SSM_W), lambda b, i: cm(b, i) + (0, 0))
    return nc, xa, z, dt, tok, st


def _ssd_fwd(xact, P, bias, alog, dskipE, ng, cat, name):
    B, S, _ = P.shape
    nc, xa_spec, z_specs, dt_spec, _, st_spec = _ssd_specs(S, False)
    tok = pl.BlockSpec((None, CHUNK, SSM_W), lambda b, i: (b, i, 1))
    E, _, tri, _ = _ssd_consts()

    def body(xa_ref, z0, z1, z2, z3, dt_ref, bias_ref, alog_ref, dsk_ref, ng_ref, E_ref, tri_ref, cat_ref, o_ref, sp_ref, state):
        @pl.when(pl.program_id(1) == 0)
        def _():
            state[...] = jnp.zeros((128, SSM_W), F32)

        sp_ref[...] = state[...]
        xa = xa_ref[...]
        c = _ssd_pre(xa, dt_ref[...], bias_ref[...], alog_ref[...], E_ref[...], tri_ref[...])
        Y, groups = _ssd_y(c, xa, state, dsk_ref[...])
        Z = (c["xdt"] * c["wE"]).astype(BF16)
        for g in range(SSM_G):
            gs = slice(g * 512, (g + 1) * 512)
            state[:, gs] = groups[g]["Sg"] * c["cdE"][:, gs] + _dot_tn(groups[g]["Bb"], Z[:, gs])
        zv = jnp.concatenate([z0[...], z1[...], z2[...], z3[...]], axis=1)
        yz = Y * (zv * _sigmoid(zv))
        outs = []
        for g in range(SSM_G):
            yg = yz[:, g * 512:(g + 1) * 512]
            outs.append(yg * lax.rsqrt(jnp.mean(yg * yg, axis=-1, keepdims=True) + EPS))
        o_ref[...] = (jnp.concatenate(outs, axis=1) * ng_ref[...]).astype(BF16)

    return pl.pallas_call(
        body, name=name, grid=(B, nc),
        in_specs=[xa_spec] + z_specs + [dt_spec, _full((1, 128)), _full((1, 128)), _full((1, SSM_W)), _full((1, SSM_W)),
                                        _full((128, SSM_W)), _full((128, 128)), _ANY],
        out_specs=[tok, st_spec],
        out_shape=[jax.ShapeDtypeStruct(cat.shape, BF16), jax.ShapeDtypeStruct((B, nc, 128, SSM_W), F32)],
        scratch_shapes=[pltpu.VMEM((128, SSM_W), F32)], input_output_aliases={12: 0},
        compiler_params=pltpu.CompilerParams(dimension_semantics=("arbitrary", "arbitrary")),
    )(xact, P, P, P, P, P, bias, alog, dskipE, ng, E, tri, cat)


def _ssd_bwd(xact, P, sprev, dcat, bias, alog, dskipE, ng, name):
    B, S, _ = P.shape
    nc, xa_spec, z_specs, dt_spec, tok, st_spec = _ssd_specs(S, True)
    do_spec = pl.BlockSpec((None, CHUNK, SSM_W), lambda b, i: (b, nc - 1 - i, 1))
    E, ET, tri, triT = _ssd_consts()
    dt_out = pl.BlockSpec((None, CHUNK, 128), lambda b, i: (b, nc - 1 - i, 0))

    def body(xa_ref, z0, z1, z2, z3, dt_ref, sp_ref, do_ref, bias_ref, alog_ref, dsk_ref, ng_ref, E_ref, ET_ref, tri_ref,
             triT_ref, dxa_ref, dz_ref, ddt_ref, dbias_ref, dalog_ref, ddsk_ref, dng_ref, dstate):
        first = (pl.program_id(0) == 0) & (pl.program_id(1) == 0)

        @pl.when(first)
        def _():
            for ref in (dbias_ref, dalog_ref, ddsk_ref, dng_ref):
                ref[...] = jnp.zeros(ref.shape, F32)

        @pl.when(pl.program_id(1) == 0)
        def _():
            dstate[...] = jnp.zeros((128, SSM_W), F32)

        xa, ETm = xa_ref[...], ET_ref[...]
        c = _ssd_pre(xa, dt_ref[...], bias_ref[...], alog_ref[...], E_ref[...], tri_ref[...])
        Y, groups = _ssd_y(c, xa, sp_ref, dsk_ref[...])
        X, xdt = c["X"], c["xdt"]
        zv = jnp.concatenate([z0[...], z1[...], z2[...], z3[...]], axis=1)
        sg = _sigmoid(zv)
        zs = zv * sg
        yz = Y * zs
        dout = do_ref[...].astype(F32)
        dyz = []
        for g in range(SSM_G):
            gs = slice(g * 512, (g + 1) * 512)
            yg = yz[:, gs]
            r = lax.rsqrt(jnp.mean(yg * yg, axis=-1, keepdims=True) + EPS)
            yn = yg * r
            dng_ref[:, gs] += jnp.sum(dout[:, gs] * yn, axis=0, keepdims=True)
            dyn = dout[:, gs] * ng_ref[:, gs]
            dyz.append(r * (dyn - yn * jnp.mean(dyn * yn, axis=-1, keepdims=True)))
        dyz = jnp.concatenate(dyz, axis=1)
        dz_ref[...] = (dyz * Y * (sg * (1.0 + zv * (1.0 - sg)))).astype(BF16)
        dY = dyz * zs
        ddsk_ref[...] += jnp.sum(dY * X, axis=0, keepdims=True)
        dX = dY * dsk_ref[...]
        lane = lax.broadcasted_iota(jnp.int32, (128, 128), 1)
        sub = lax.broadcasted_iota(jnp.int32, (128, 128), 0)
        colform = jnp.zeros((128, 128), F32)
        rowform = jnp.zeros((128, 128), F32)
        dxdt, gacsE, dBC = [], [], []
        for g in range(SSM_G):
            gs = slice(g * 512, (g + 1) * 512)
            G = groups[g]
            Bb, Cb, CB, Sg = G["Bb"], G["Cb"], G["CB"], G["Sg"]
            dYg = dY[:, gs]
            dQ = (dYg * c["eE"][:, gs]).astype(BF16)
            dSn = dstate[:, gs]
            dSnb = dSn.astype(BF16)
            cd = c["cdE"][:, gs]
            dC = _dot_nt(dQ, Sg.astype(BF16))
            dSprev = _dot_tn(Cb, dQ) + dSn * cd
            t1 = jnp.broadcast_to(jnp.sum(dSn * Sg * cd, axis=0, keepdims=True), (8, 512))
            colform = colform + jnp.where(sub == 127, _dot01(t1, ETm[gs, :], 2)[0:1, :], 0.0)
            Zg = xdt[:, gs] * c["wE"][:, gs]
            dZ = _dot(Bb, dSnb)
            dB = _dot_nt(Zg.astype(BF16), dSnb)
            U = dZ * Zg
            ga = dYg * G["yoff"] - U
            ga = ga + jnp.where(lax.broadcasted_iota(jnp.int32, (128, 512), 0) == 127, jnp.sum(U, axis=0, keepdims=True), 0.0)
            gacsE.append(ga)
            dxg = [None] * 4
            dCB = jnp.zeros((128, 128), F32)
            for j in range(4):
                h0 = g * 8 + 2 * j
                L0, L1, M, BD = G["pairs"][j]
                dYp = dYg[:, j * 128:(j + 1) * 128].astype(BF16)
                dM = _dot_nt(dYp, BD)
                dBD = _dot_tn(M, dYp)
                dxg[j] = jnp.where(lane < 64, dBD[:128], dBD[128:])
                for t, (h, L) in enumerate(((h0, L0), (h0 + 1, L1))):
                    dMh = dM[:, t * 128:(t + 1) * 128]
                    dCB = dCB + dMh * L
                    Gh = dMh * CB * L
                    colform = colform + jnp.where(lane == h, jnp.sum(Gh, axis=1, keepdims=True), 0.0)
                    rowform = rowform - jnp.where(sub == h, jnp.sum(Gh, axis=0, keepdims=True), 0.0)
            dCBb = dCB.astype(BF16)
            dC = dC + _dot(dCBb, Bb)
            dB = dB + _dot_tn(dCBb, Cb)
            dxdt.append(jnp.concatenate(dxg, axis=1) + dZ * c["wE"][:, gs])
            dBC.append((dB, dC))
            dstate[:, gs] = dSprev
        dxdt = jnp.concatenate(dxdt, axis=1)
        dX = dX + dxdt * c["dtE"]
        ddt = _dot01(dxdt * X, ETm, 2)
        dacs = colform + rowform.T + _dot01(jnp.concatenate(gacsE, axis=1), ETm, 2)
        dda = _dot01_left(triT_ref[...], dacs, 2)
        ddt = ddt + dda * c["a"]
        dalog_ref[...] += jnp.sum(dda * c["dtp"], axis=0, keepdims=True) * c["a"]
        ddtraw = jnp.where(lane < SSM_H, ddt * _sigmoid(c["pre"]), 0.0)
        dbias_ref[...] += jnp.sum(ddtraw, axis=0, keepdims=True)
        ddt_ref[...] = ddtraw.astype(BF16)
        dxa_ref[...] = jnp.concatenate([dX, dBC[0][0], dBC[1][0], dBC[0][1], dBC[1][1]], axis=1)

    p128, p1k = _full((1, 128)), _full((1, SSM_W))
    return pl.pallas_call(
        body, name=name, grid=(B, nc),
        in_specs=[xa_spec] + z_specs + [dt_spec, st_spec, do_spec, p128, p128, p1k, p1k,
                                        _full((128, SSM_W)), _full((SSM_W, 128)), _full((128, 128)), _full((128, 128))],
        out_specs=[xa_spec, tok, dt_out, p128, p128, p1k, p1k],
        out_shape=[jax.ShapeDtypeStruct((B, S, CONV_CH), F32), jax.ShapeDtypeStruct((B, S, SSM_W), BF16),
                   jax.ShapeDtypeStruct((B, S, 128), BF16), jax.ShapeDtypeStruct((1, 128), F32),
                   jax.ShapeDtypeStruct((1, 128), F32), jax.ShapeDtypeStruct((1, SSM_W), F32),
                   jax.ShapeDtypeStruct((1, SSM_W), F32)],
        scratch_shapes=[pltpu.VMEM((128, SSM_W), F32)],
        compiler_params=pltpu.CompilerParams(dimension_semantics=("arbitrary", "arbitrary")),
    )(xact, P, P, P, P, P, sprev, dcat, bias, alog, dskipE, ng, E, ET, tri, triT)


def _adamw(w, parts, m, v, name, tr=512, layer=0, prev=None):
    Ltot, R, C = w.shape
    ns = parts.shape[0]
    tr = min(tr, R)
    assert R % tr == 0 and parts.shape[1:] == (R, C)
    c1 = 1.0 / (1.0 - ADAM_B1 ** ADAM_STEP)
    c2 = 1.0 / (1.0 - ADAM_B2 ** ADAM_STEP)

    def body(w_ref, p_ref, m_ref, v_ref, *rest):
        g_ref, d_ref, mo_ref, vo_ref = rest[-4:]
        g = p_ref[0].astype(F32)
        for s in range(1, ns):
            g = g + p_ref[s].astype(F32)
        mn = ADAM_B1 * m_ref[...] + (1.0 - ADAM_B1) * g
        vn = ADAM_B2 * v_ref[...] + (1.0 - ADAM_B2) * (g * g)
        g_ref[...] = g
        mo_ref[...] = mn
        vo_ref[...] = vn
        d_ref[...] = -ADAM_LR * ((mn * c1) / (jnp.sqrt(vn * c2) + ADAM_EPS) + ADAM_WD * w_ref[...])

    blk = pl.BlockSpec((None, tr, C), lambda i: (layer, i, 0))
    extra = [] if prev is None else list(prev)
    return pl.pallas_call(
        body, name=name, grid=(R // tr,),
        in_specs=[blk, pl.BlockSpec((ns, tr, C), lambda i: (0, i, 0)), blk, blk] + [pl.BlockSpec(memory_space=pl.ANY)] * len(extra),
        out_specs=[blk] * 4, out_shape=[jax.ShapeDtypeStruct((Ltot, R, C), F32)] * 4,
        input_output_aliases={4 + k: k for k in range(len(extra))})(w, parts, m, v, *extra)


_SMALL = ("ada_b", "norm1_g", "gm_ln_g", "gm_ln_b", "gm_ws", "gm_bs", "gm_norm_g", "attn_sinks", "attn_norm_g", "conv_b",
          "dt_bias", "a_log", "d_skip", "ssm_norm_g", "norm2_g", "final_norm_g")


def _pack(arrs):
    flat = []
    for a in arrs:
        f = a.reshape(-1).astype(F32)
        flat.append(jnp.pad(f, (0, (-f.shape[0]) % 1024)))
    return jnp.concatenate(flat).reshape(-1, 128)


def _unpack(pack, like):
    out, r = [], 0
    for a in like:
        n = int(np.prod(a.shape))
        rows = (n + 1023) // 1024 * 8
        out.append(lax.slice(pack, (r, 0), (r + rows, 128)).reshape(-1)[:n].reshape(a.shape))
        r += rows
    return out


def kernel(x, c, ada_w, ada_b, norm1_g, w_in, gm_ln_g, gm_ln_b, gm_ws, gm_bs, gm_norm_g, attn_sinks, attn_norm_g, conv_w, conv_b, dt_bias, a_log, d_skip, ssm_norm_g, w_out, norm2_g, w_mlp1, w_mlp2, final_norm_g, loss_target, m_ada_w, m_ada_b, m_norm1_g, m_w_in, m_gm_ln_g, m_gm_ln_b, m_gm_ws, m_gm_bs, m_gm_norm_g, m_attn_sinks, m_attn_norm_g, m_conv_w, m_conv_b, m_dt_bias, m_a_log, m_d_skip, m_ssm_norm_g, m_w_out, m_norm2_g, m_w_mlp1, m_w_mlp2, m_final_norm_g, v_ada_w, v_ada_b, v_norm1_g, v_w_in, v_gm_ln_g, v_gm_ln_b, v_gm_ws, v_gm_bs, v_gm_norm_g, v_attn_sinks, v_attn_norm_g, v_conv_w, v_conv_b, v_dt_bias, v_a_log, v_d_skip, v_ssm_norm_g, v_w_out, v_norm2_g, v_w_mlp1, v_w_mlp2, v_final_norm_g):
    args = dict(locals())
    B, S, _ = x.shape
    T = B * S
    L = DEPTH
    me = 4 * lax.axis_index("x") + 2 * lax.axis_index("y") + lax.axis_index("c")

    gath = _gather2([c, conv_w], "ag_c")
    big = ("w_in", "w_out", "w_mlp1", "w_mlp2")
    chain = [(n, l) for l in range(L) for n in ("w_in", "w_mlp1", "w_out", "w_mlp2")]
    inflight = {}

    def start_next(order):
        if not chain:
            return jnp.zeros((8, 128), F32)
        n, l = chain.pop(0)
        sems, land_thru, token = _gather_start(zone[n, l], order, f"ag_start_{n}{l}")
        inflight[n, l] = (sems, land_thru)
        return token

    def gathered(n, l, after):
        land = _gather_wait(*inflight.pop((n, l)), after, f"ag_wait_{n}{l}")
        return _gather_finish(land, f"ag_fin_{n}{l}")

    forwarding = {}

    def arrived(n, l, after):
        land = _gather_wait(*inflight.pop((n, l)), after, f"ag_wait_{n}{l}")
        sems, land_thru, token = _forward_start(land, after, f"ag_fwd_start_{n}{l}")
        forwarding[n, l] = (sems, land_thru)
        return token

    def ready(n, l, after):
        return _forward_wait(*forwarding.pop((n, l)), after, f"ag_fwd_wait_{n}{l}")

    me1 = me.astype(jnp.int32).reshape(1)
    zone = {(n, l): _landing_zone(args[n], l, me1, f"ag_zone_{n}{l}") for n, l in chain}
    later_zones = [zone[k] for k in chain[1:]]

    tok = start_next(gath[0])
    c_all = gath[0].reshape(NDEV * B, D) + tok[0, 0]
    c_act = (c_all * jax.nn.sigmoid(c_all)).astype(BF16)
    nb_rows = c_act.shape[0]
    c_pad = jnp.pad(c_act, ((0, 128 - nb_rows), (0, 0)))
    adw = ada_w.astype(BF16)
    mod_part = jnp.stack([_mm(c_pad, adw[l], mode="nn", name=f"mod{l}", tn=768)[:nb_rows] for l in range(L)])
    mod_all = _gather_small([mod_part], "ag_mod", order=later_zones)[0]
    mod_mine = lax.dynamic_slice_in_dim(mod_all, me * B, B, axis=2)
    mod = jnp.transpose(mod_mine, (1, 2, 0, 3)).reshape(L, B, 6 * D) + ada_b[:, None, :]
    mods = [[mod[l][:, None, i * D:(i + 1) * D] for i in range(6)] for l in range(L)]

    win_g, wout_g, w1_g, w2_g = [None] * L, [None] * L, [None] * L, [None] * L

    tril = jnp.tril(jnp.ones((128, 128), F32))
    row = lambda a: a.reshape(1, -1)
    pad128 = lambda a: jnp.pad(a.reshape(1, -1), ((0, 0), (0, 128 - a.shape[-1])))
    small = []
    for l in range(L):
        wt = gm_ws[l] * tril
        small.append(dict(
            lng=row(gm_ln_g[l]), lnb=row(gm_ln_b[l]), wt=wt.astype(BF16), wtT=jnp.swapaxes(wt, 1, 2).astype(BF16),
            bsx=jnp.repeat(gm_bs[l].T, 128, axis=1), gog=row(gm_norm_g[l]), sinks=attn_sinks[l], aog=row(attn_norm_g[l]),
            bias=pad128(dt_bias[l]), alog=pad128(a_log[l]), dskE=jnp.repeat(d_skip[l], SSM_HD).reshape(1, SSM_W),
            sng=row(ssm_norm_g[l]), cb=row(conv_b[l])))
    convw_all = jnp.transpose(gath[1], (1, 2, 0, 3)).reshape(L, 4, CONV_CH)
    convw8 = jnp.pad(convw_all, ((0, 0), (0, 4), (0, 0)))

    saved = []
    xl = x
    g_in = gathered("w_in", 0, mod)
    tok = start_next(g_in)
    h = _norm_fwd(xl, row(norm1_g[0]) + tok[0, 0], mods[0][1], mods[0][0], "norm1_f0")
    for l in range(L):
        sm = small[l]
        win_g[l] = _shards_to_cols(g_in, f"w_in_cols{l}")
        P = _mm(h.reshape(T, D), win_g[l], mode="nn", name=f"proj_in{l}", tn=1536, order=tok).reshape(B, S, PW)
        cat = _gmlp_fwd(P, sm["lng"], sm["lnb"], sm["wt"], sm["bsx"], sm["gog"], f"gmlp_f{l}")
        cat = _attn_fwd(P, sm["sinks"], sm["aog"], cat, f"attn_f{l}")
        xact = _conv_fwd(P, convw8[l], sm["cb"], f"conv_f{l}")
        tok = start_next(arrived("w_mlp1", l, xact))
        cat, sprev = _ssd_fwd(xact, P, sm["bias"], sm["alog"], sm["dskE"], sm["sng"] + tok[0:1, 0:1], cat, f"ssd_f{l}")
        g_out = gathered("w_out", l, cat)
        tok = start_next(g_out)
        wout_g[l] = g_out.reshape(D, D)
        mix = _mm(cat.reshape(T, D), wout_g[l], mode="nn", name=f"proj_out{l}", order=tok).reshape(B, S, D)
        x_mid, h2 = _norm_fwd(xl, row(norm2_g[l]), mods[l][4], mods[l][3], f"norm2_f{l}", resid=(mix, mods[l][2]))
        w1_g[l] = ready("w_mlp1", l, h2)
        a_act, r_act = _mm(h2.reshape(T, D), w1_g[l], mode="nn", name=f"mlp1_{l}", out_dtypes=(BF16, BF16), col_blocked_b=True,
                           epilogue=lambda acc: (acc, jnp.square(jnp.maximum(acc, 0.0))))
        g_2 = gathered("w_mlp2", l, r_act)
        tok = start_next(g_2)
        w2_g[l] = g_2.reshape(DFF, D)
        m2 = _mm(r_act, w2_g[l], mode="nn", name=f"mlp2_{l}", order=tok, tk=4096).reshape(B, S, D)
        saved.append(dict(x_in=xl, h=h, P=P, xact=xact, sprev=sprev, cat=cat, mix=mix, x_mid=x_mid, h2=h2, a=a_act, r=r_act, m2=m2))
        if l + 1 < L:
            tok = start_next(arrived("w_in", l + 1, m2))
            xl, h = _norm_fwd(x_mid, row(norm1_g[l + 1]) + tok[0, 0], mods[l + 1][1], mods[l + 1][0], f"norm1_f{l + 1}",
                              resid=(m2, mods[l][5]))
            g_in = ready("w_in", l + 1, h)

    sv = saved[L - 1]
    nb = _norm_bwd(sv["x_mid"], row(final_norm_g), "final_b", tgt=loss_target, br=sv["m2"], gate=mods[L - 1][5], x_is_prev=True)
    loss_part, g_final = nb["loss"], nb["dg"]
    dmod, gsm, gconvw = [None] * L, [None] * L, [None] * L
    core = lax.axis_index("c").astype(jnp.int32).reshape(1)
    reducing = []

    def reduce_start(n, l, sent, after):
        p, from_sib = _pair_wait(*sent[:3], after, f"rs_pair_wait_{n}{l}")
        s, land = _pair_add(p, from_sib, core, f"rs_add_{n}{l}")
        return reduce_exchange(n, l, s, land, after)

    def reduce_exchange(n, l, s, land, order):
        sems, s_thru, land_thru, token = _chipsum_start(s, land, order, f"rs_start_{n}{l}")
        reducing.append((n, l, sems, s_thru, land_thru))
        return token

    other = 1 - core

    for l in reversed(range(L)):
        sv, sm = saved[l], small[l]
        dm2, dxo, dg2 = nb["dbr"].reshape(T, D), nb["dx"], nb["dgate"]
        da = _mm(dm2, w2_g[l], mode="nt", name=f"mlp2_dx{l}", out_dtypes=(BF16,), extras=(sv["a"],),
                 epilogue=lambda acc, a: (acc * (2.0 * jnp.maximum(a.astype(F32), 0.0)),))
        h2f = sv["h2"].reshape(T, D)
        sent2 = _sibling_start(_dw_half(sv["r"], dm2, other, axis="m", name=f"mlp2_dw_sib{l}"), da, f"rs_sib_start_w_mlp2{l}")
        dh2 = _mm(da, w1_g[l], mode="nt", name=f"mlp1_dx{l}", col_blocked_b=True, order=sent2[3],
                  out_dtypes=(BF16,)).reshape(B, S, D)
        from_sib = _sibling_wait(*sent2[:3], dh2, f"rs_sib_wait_w_mlp2{l}")[1]
        sent1 = _sibling_start(_dw_half(h2f, da, other, axis="n", name=f"mlp1_dw_sib{l}", order=from_sib), da,
                               f"rs_sib_start_w_mlp1{l}")
        s2, land2 = _dw_half(sv["r"], dm2, core, axis="m", name=f"mlp2_dw_own{l}", add=from_sib, order=sent1[3])
        tok = reduce_exchange("w_mlp2", l, s2, land2, da)
        nb2 = _norm_bwd(sv["x_mid"], row(norm2_g[l]) + tok[0, 0], f"norm2_b{l}", sc=mods[l][4], dh=dh2, dres=dxo, br=sv["mix"],
                        gate=mods[l][2])
        dmix = nb2["dbr"].reshape(T, D)
        from_sib = _sibling_wait(*sent1[:3], dmix, f"rs_sib_wait_w_mlp1{l}")[1]
        s1, land1 = _dw_half(h2f, da, core, axis="n", name=f"mlp1_dw_own{l}", add=from_sib)
        tok = reduce_exchange("w_mlp1", l, s1, land1, dmix)
        dcat = _mm(dmix, wout_g[l], mode="nt", name=f"proj_out_dx{l}", order=tok, out_dtypes=(BF16,)).reshape(B, S, D)
        du, dv, dlng, dlnb, dws, dbsx, dgog = _gmlp_bwd(sv["P"], dcat, sm["lng"], sm["lnb"], sm["wt"], sm["wtT"], sm["bsx"],
                                                        sm["gog"], f"gmlp_b{l}")
        dq, dk, dvv, dsink, daog = _attn_bwd(sv["P"], dcat, sm["sinks"], sm["aog"], f"attn_b{l}")
        dwo = _mm(sv["cat"].reshape(T, D), dmix, mode="tn", name=f"proj_out_dw{l}", out_dtypes=(BF16,), tk=2048,
                  order=dq).reshape(4, 2, D // NDEV, D)
        sent = _pair_start(dwo, dmix, f"rs_pair_start_w_out{l}")
        dxa, dz, ddt, dbias, dalog, ddsk, dsng = _ssd_bwd(sv["xact"], sv["P"], sv["sprev"], dcat, sm["bias"], sm["alog"],
                                                          sm["dskE"], sm["sng"] + sent[3][0:1, 0:1], f"ssd_b{l}")
        tok = reduce_start("w_out", l, sent, dxa)
        dxbc, dcw, dcb = _conv_bwd(sv["P"], dxa, convw8[l], sm["cb"] + tok[0:1, 0:1], f"conv_b{l}")
        dP = _concat_cols([du, dv, dq, dk, dvv, dz, dxbc, ddt], PW, f"dproj_cols{l}").reshape(T, PW)
        dwin = _mm(sv["h"].reshape(T, D), dP, mode="tn", name=f"proj_in_dw{l}", out_dtypes=(BF16,), tn=1536, tk=2048)
        sent = _sibling_start(dwin, dP, f"rs_sib_start_w_in{l}")
        dh = _mm(dP, win_g[l], mode="nt", name=f"proj_in_dx{l}", tk=2304, order=sent[3], out_dtypes=(BF16,)).reshape(B, S, D)
        s_in, land_in = _cols_to_my_shards(*_sibling_wait(*sent[:3], dh, f"rs_sib_wait_w_in{l}"), core, f"w_in_dshards{l}")
        tok = reduce_exchange("w_in", l, s_in, land_in, dh)
        nb = _norm_bwd(sv["x_in"], row(norm1_g[l]) + tok[0, 0], f"norm1_b{l}", sc=mods[l][1], dh=dh, dres=nb2["dx"],
                       br=saved[l - 1]["m2"] if l > 0 else None, gate=mods[l - 1][5] if l > 0 else None)
        dmod[l] = jnp.concatenate([nb["dsh"], nb["dsc"], nb2["dgate"], nb2["dsh"], nb2["dsc"], dg2], axis=-1)
        gconvw[l] = dcw[:4]
        gsm[l] = dict(
            ada_b=jnp.sum(dmod[l], axis=(0, 1)), norm1_g=nb["dg"], gm_ln_g=dlng, gm_ln_b=dlnb, gm_ws=dws,
            gm_bs=dbsx.reshape(128, GM_H, 128).sum(-1).T, gm_norm_g=dgog, attn_sinks=dsink[:, 0], attn_norm_g=daog,
            conv_b=dcb, dt_bias=dbias[0, :SSM_H], a_log=dalog[0, :SSM_H], d_skip=ddsk.reshape(SSM_H, SSM_HD).sum(-1),
            ssm_norm_g=dsng, norm2_g=nb2["dg"])
    grad_x = nb["dx"]

    big_res, after = dict.fromkeys(big), grad_x
    tile_rows = dict(w_in=256, w_out=256, w_mlp1=256, w_mlp2=128)

    def finish_reduce(n, l, sems, s_thru, land_thru, after):
        parts = _chipsum_wait(sems, s_thru, land_thru, after, f"rs_wait_{n}{l}")
        big_res[n] = _adamw(args[n], parts, args["m_" + n], args["v_" + n], f"adamw_{n}{l}", tr=tile_rows[n], layer=l,
                            prev=big_res[n])
        return big_res[n][0]

    per_layer = [n for n in _SMALL if n != "final_norm_g"]
    g_small = [jnp.stack([gsm[l][n].reshape(args[n].shape[1:]) for l in range(L)]) for n in per_layer] + [g_final.reshape(D)]
    zc = jnp.zeros((L, 4, CONV_CH), F32)
    z1 = jnp.zeros((1, 128), F32)
    gpack = _pack([loss_part] + g_small + [jnp.stack(gconvw)])
    small_zones = [_landing_zone(jnp.stack(dmod).reshape(1, L * B, 6 * D), 0, me1, "ag_zone_dmod", dtype=F32),
                   _landing_zone(gpack[None], 0, me1, "ag_zone_small", tr=gpack.shape[0], dtype=F32)]
    small_sems, small_thru, after = _gather_small_start(small_zones, grad_x, "ag_small_start")

    for item in reducing[:-1]:
        after = finish_reduce(*item, after)

    got = _gather_small_wait(small_sems, small_thru, after, "ag_small_wait")
    got = [got[0].reshape(NDEV, L, B, 6 * D), got[1]]
    like = [z1] + [args[n] for n in _SMALL] + [zc]
    packs = [_pack([z1] + [args[p + n] for n in _SMALL] + [zc]) for p in ("", "m_", "v_")]
    sres = [_unpack(p[0], like) for p in _adamw(packs[0][None], got[1], packs[1][None], packs[2][None], "adamw_small",
                                                tr=gpack.shape[0])]
    res = {n: [r[1 + i] for r in sres] for i, n in enumerate(_SMALL)}
    loss = sres[0][0][0, 0]
    gcw = lax.dynamic_slice_in_dim(sres[0][-1], me * (CONV_CH // NDEV), CONV_CH // NDEV, axis=2)

    def update(name, grads, tr):
        r = None
        for l, g in enumerate(grads):
            r = _adamw(args[name], g[None], args["m_" + name], args["v_" + name], f"adamw_{name}{l}", tr=tr, layer=l, prev=r)
        res[name] = r

    update("conv_w", [gcw[l] for l in range(L)], 4)

    dmod_all = jnp.transpose(got[0], (1, 0, 2, 3)).reshape(L, NDEV * B, 6 * D)
    dm_mine = lax.dynamic_slice_in_dim(dmod_all, me * (6 * D // NDEV), 6 * D // NDEV, axis=2)
    dm_pad = jnp.pad(dm_mine, ((0, 0), (0, 128 - nb_rows), (0, 0))).astype(BF16)
    update("ada_w", [_mm(c_pad, dm_pad[l], mode="tn", name=f"ada_dw{l}", tn=768) for l in range(L)], 256)

    finish_reduce(*reducing[-1], res["ada_w"][0])
    for n in big:
        res[n] = [a.reshape(args[n].shape) for a in big_res[n]]

    names = ['ada_w', 'ada_b', 'norm1_g', 'w_in', 'gm_ln_g', 'gm_ln_b', 'gm_ws', 'gm_bs', 'gm_norm_g', 'attn_sinks',
             'attn_norm_g', 'conv_w', 'conv_b', 'dt_bias', 'a_log', 'd_skip', 'ssm_norm_g', 'w_out', 'norm2_g', 'w_mlp1',
             'w_mlp2', 'final_norm_g']
    return (loss, grad_x, *[res[n][0] for n in names], *[res[n][1] for n in names], *[res[n][2] for n in names],
            *[res[n][3] for n in names])
```

```python
import jax
import jax.numpy as jnp
import numpy as np
from jax import lax
from jax.experimental import pallas as pl
from jax.experimental.pallas import tpu as pltpu

F32, BF16 = jnp.float32, jnp.bfloat16
MESH = pl.DeviceIdType.MESH
NDEV = 8

D = 2048
DEPTH = 2
CHUNK = 128
GM_W, GM_H = 512, 4
ATT_W, KV_W = 512, 128
SSM_W, SSM_H, SSM_HD, SSM_G = 1024, 16, 64, 2
CONV_CH = 1536
IN_W = 4368
DFF = 8192
EPS = 1e-6
NEG_INF = -1e30
GELU_K = 0.7978845608028654
GELU_C = 0.044715

_ORIG = (("u", 512), ("v", 512), ("q", 512), ("k", 128), ("vv", 128), ("z", 1024), ("xbc", 1536), ("dt", 16))
OFF = dict(u=0, v=512, q=1024, k=1536, vv=1664, z=1792, xbc=2816, dt=4352)
PW = 4608

ADAM_LR, ADAM_B1, ADAM_B2, ADAM_EPS, ADAM_WD, ADAM_STEP = 0.001, 0.9, 0.999, 1e-08, 0.01, 10


def _concat_cols(pieces, width, name, ts=256):
    B, S, _ = pieces[0].shape
    ws = [p.shape[-1] for p in pieces]
    dt = pieces[0].dtype
    n = len(pieces)

    def body(*refs):
        cols = [r[...] for r in refs[:n]]
        if width > sum(ws):
            cols.append(jnp.zeros((ts, width - sum(ws)), dt))
        refs[n][...] = jnp.concatenate(cols, axis=1)

    return pl.pallas_call(
        body, name=name, grid=(B, S // ts), in_specs=[pl.BlockSpec((None, ts, w), lambda b, i: (b, i, 0)) for w in ws],
        out_specs=pl.BlockSpec((None, ts, width), lambda b, i: (b, i, 0)), out_shape=jax.ShapeDtypeStruct((B, S, width), dt))(*pieces)


def _shards_to_cols(g, name, tr=256):
    n, R, C = g.shape

    def body(g_ref, o_ref):
        o_ref[...] = jnp.concatenate([g_ref[s] for s in range(n)] + [jnp.zeros((tr, PW - n * C), g.dtype)], axis=1)

    return pl.pallas_call(body, name=name, grid=(R // tr,), in_specs=[pl.BlockSpec((n, tr, C), lambda i: (0, i, 0))],
                          out_specs=pl.BlockSpec((tr, PW), lambda i: (i, 0)), out_shape=jax.ShapeDtypeStruct((R, PW), g.dtype))(g)


def _cols_to_my_shards(w, w_sib, core, name, tr=256):
    R, C = w.shape[0], IN_W // NDEV

    def body(core_ref, w_ref, s_ref, o_ref, o2_ref):
        x = w_ref[...].astype(F32) + s_ref[...].astype(F32)
        mine_is_odd = core_ref[0] == 1
        for q in range(4):
            blk = jnp.where(mine_is_odd, x[:, C * (2 * q + 1):C * (2 * q + 2)], x[:, C * 2 * q:C * (2 * q + 1)]).astype(o_ref.dtype)
            o_ref[q] = blk
            o2_ref[q] = blk

    row = pl.BlockSpec((tr, PW), lambda i, c: (i, 0))
    out = pl.BlockSpec((4, tr, C), lambda i, c: (0, i, 0))
    return pl.pallas_call(
        body, name=name, out_shape=[jax.ShapeDtypeStruct((4, R, C), w.dtype)] * 2,
        grid_spec=pltpu.PrefetchScalarGridSpec(num_scalar_prefetch=1, grid=(R // tr,), in_specs=[row, row], out_specs=[out, out]),
    )(core, w, w_sib)


def _sigmoid(x):
    return 0.5 * (jnp.tanh(0.5 * x) + 1.0)


def _gelu(x):
    return 0.5 * x * (1.0 + jnp.tanh(GELU_K * (x + GELU_C * x * x * x)))


def _gelu_grad(x):
    t = jnp.tanh(GELU_K * (x + GELU_C * x * x * x))
    return 0.5 * (1.0 + t) + 0.5 * x * (1.0 - t * t) * GELU_K * (1.0 + 3.0 * GELU_C * x * x)


def _dot(a, b, prec=None):
    return jnp.dot(a, b, precision=prec, preferred_element_type=F32)


def _dot_nt(a, b, prec=None):
    return lax.dot_general(a, b, (((1,), (1,)), ((), ())), precision=prec, preferred_element_type=F32)


def _dot_tn(a, b, prec=None):
    return lax.dot_general(a, b, (((0,), (0,)), ((), ())), precision=prec, preferred_element_type=F32)


def _full(shape):
    return pl.BlockSpec(shape, lambda *_: (0,) * len(shape))


_HBM = pl.BlockSpec(memory_space=pltpu.HBM)


def _me():
    return lax.axis_index("x"), lax.axis_index("y"), lax.axis_index("c")


def _peer(k):
    x, y, c = _me()
    px = 1 - x if k & 4 else x
    py = 1 - y if k & 2 else y
    pc = 1 - c if k & 1 else c
    return (px, py, pc), 4 * px + 2 * py + pc


def _gather_small(xs, name, order=()):
    n = len(xs)

    def body(*refs):
        ins, outs = refs[:n], refs[-n - 3:-3]
        send, recv, loc = refs[-3:]
        x, y, c = _me()
        me = 4 * x + 2 * y + c
        started = []
        for i in range(n):
            own = pltpu.make_async_copy(ins[i], outs[i].at[me], loc.at[i])
            own.start()
            started.append(own)
        for k in range(1, NDEV):
            dev, lin = _peer(k)
            for i in range(n):
                pltpu.make_async_remote_copy(
                    src_ref=ins[i], dst_ref=outs[i].at[me],
                    send_sem=send.at[i, k - 1], recv_sem=recv.at[i, k - 1], device_id=dev, device_id_type=MESH).start()
        for k in range(1, NDEV):
            dev, lin = _peer(k)
            for i in range(n):
                pltpu.make_async_remote_copy(
                    src_ref=ins[i], dst_ref=outs[i].at[lin],
                    send_sem=send.at[i, k - 1], recv_sem=recv.at[i, k - 1], device_id=dev, device_id_type=MESH).wait()
        for own in started:
            own.wait()

    extra = list(order)
    return pl.pallas_call(
        body, name=name, out_shape=[jax.ShapeDtypeStruct((NDEV,) + a.shape, a.dtype) for a in xs],
        in_specs=[_HBM] * n + [pl.BlockSpec(memory_space=pl.ANY)] * len(extra), out_specs=[_HBM] * n,
        scratch_shapes=[pltpu.SemaphoreType.DMA((n, NDEV - 1)), pltpu.SemaphoreType.DMA((n, NDEV - 1)),
                        pltpu.SemaphoreType.DMA((n,))],
        compiler_params=pltpu.CompilerParams(has_side_effects=True),
    )(*xs, *extra)


def _gather_small_start(lands, order, name):
    n = len(lands)

    def body(*refs):
        ins, sems, token = refs[:n], refs[n + 1:n + 1 + 14 * n], refs[-1]
        x, y, c = _me()
        me = 4 * x + 2 * y + c
        for i in range(n):
            for k in range(1, NDEV):
                dev, _ = _peer(k)
                pltpu.make_async_remote_copy(src_ref=ins[i].at[me], dst_ref=ins[i].at[me], send_sem=sems[14 * i + k - 1],
                                             recv_sem=sems[14 * i + 7 + k - 1], device_id=dev, device_id_type=MESH).start()
        token[...] = jnp.zeros_like(token)

    outs = pl.pallas_call(
        body, name=name,
        out_shape=(pltpu.SemaphoreType.DMA(()),) * (14 * n) + tuple(pltpu.HBM(a.shape, a.dtype) for a in lands)
        + (jax.ShapeDtypeStruct((8, 128), F32),),
        in_specs=(_HBM,) * n + (_ANY,), out_specs=(_SEM,) * (14 * n) + (_HBM,) * n + (pl.BlockSpec(memory_space=pltpu.VMEM),),
        input_output_aliases={i: 14 * n + i for i in range(n)}, compiler_params=pltpu.CompilerParams(has_side_effects=_DATAFLOW),
    )(*[_hbm(a) for a in lands], order)
    return outs[:14 * n], outs[14 * n:15 * n], outs[-1]


def _gather_small_wait(sems, lands_thru, after, name):
    n = len(lands_thru)

    def body(*refs):
        ins, sems_ = refs[:n], refs[n:n + 14 * n]
        x, y, c = _me()
        me = 4 * x + 2 * y + c
        for i in range(n):
            for k in range(1, NDEV):
                dev, lin = _peer(k)
                cp = pltpu.make_async_remote_copy(src_ref=ins[i].at[me], dst_ref=ins[i].at[lin], send_sem=sems_[14 * i + k - 1],
                                                  recv_sem=sems_[14 * i + 7 + k - 1], device_id=dev, device_id_type=MESH)
                cp.wait_send()
                cp.wait_recv()

    outs = pl.pallas_call(
        body, name=name, out_shape=tuple(pltpu.HBM(a.shape, a.dtype) for a in lands_thru),
        in_specs=(_HBM,) * n + (_SEM,) * (14 * n) + (_ANY,), out_specs=(_HBM,) * n,
        input_output_aliases={i: i for i in range(n)}, compiler_params=pltpu.CompilerParams(has_side_effects=_DATAFLOW),
    )(*lands_thru, *sems, after)
    return outs


def _chips():
    x, y, c = _me()
    return x, y, c, [(1 - x, y), (x, 1 - y), (1 - x, 1 - y)]


def _gather2(xs, name, order=None):
    n = len(xs)
    extra = [] if order is None else [order]

    def body(*refs):
        ins, outs = refs[:n], refs[-n - 3:-3]
        send, recv, loc = refs[-3:]
        x, y, c, chips = _chips()
        me, sib = (x, y, c), (x, y, 1 - c)

        def cp(i, k, block, to, src=None):
            slot = outs[i].at[4 * block[0] + 2 * block[1] + block[2]]
            return pltpu.make_async_remote_copy(src_ref=slot if src is None else src, dst_ref=slot, send_sem=send.at[i, k],
                                                recv_sem=recv.at[i, k], device_id=to, device_id_type=MESH)

        sent = []
        for i in range(n):
            for j, chip in enumerate(chips):
                sent.append(cp(i, 1 + j, me, (*chip, c), src=ins[i]))
            sent.append(cp(i, 0, me, sib, src=ins[i]))
        for s in sent:
            s.start()
        own = [pltpu.make_async_copy(ins[i], outs[i].at[4 * x + 2 * y + c], loc.at[i]) for i in range(n)]
        for o in own:
            o.start()
        for j, chip in enumerate(chips):
            for i in range(n):
                cp(i, 1 + j, (*chip, c), me).wait_recv()
                fwd = cp(i, 4 + j, (*chip, c), sib)
                fwd.start()
                sent.append(fwd)
        for i in range(n):
            cp(i, 0, sib, me).wait_recv()
            for j, chip in enumerate(chips):
                cp(i, 4 + j, (*chip, 1 - c), me).wait_recv()
        for s in sent:
            s.wait_send()
        for o in own:
            o.wait()

    return pl.pallas_call(
        body, name=name, out_shape=[jax.ShapeDtypeStruct((NDEV,) + a.shape, a.dtype) for a in xs],
        in_specs=[_HBM] * n + [pl.BlockSpec(memory_space=pl.ANY)] * len(extra), out_specs=[_HBM] * n,
        scratch_shapes=[pltpu.SemaphoreType.DMA((n, 7)), pltpu.SemaphoreType.DMA((n, 7)), pltpu.SemaphoreType.DMA((n,))],
        compiler_params=pltpu.CompilerParams(has_side_effects=True),
    )(*xs, *extra)


def _pair_add(p, r1, core, name, tr=256):
    _, _, R, C = p.shape
    tr = min(tr, R)

    def body(core_ref, p_ref, r_ref, o_ref, o2_ref):
        s = (p_ref[...].astype(F32) + r_ref[...].astype(F32)).astype(o_ref.dtype)
        o_ref[...] = s
        o2_ref[...] = s

    blk = pl.BlockSpec((None, tr, C), lambda ch, i, core_ref: (ch, i, 0))
    return pl.pallas_call(
        body, name=name, out_shape=[jax.ShapeDtypeStruct((4, R, C), p.dtype)] * 2,
        grid_spec=pltpu.PrefetchScalarGridSpec(
            num_scalar_prefetch=1, grid=(4, R // tr),
            in_specs=[pl.BlockSpec((None, None, tr, C), lambda ch, i, core_ref: (ch, core_ref[0], i, 0)), blk],
            out_specs=[blk, blk]),
    )(core, p, r1)


_SEM = pl.BlockSpec(memory_space=pltpu.SEMAPHORE)
_ANY = pl.BlockSpec(memory_space=pl.ANY)
_DATAFLOW = pltpu.SideEffectType.DATAFLOW_SIDE_EFFECTING


def _hbm(a):
    return pltpu.with_memory_space_constraint(a, pltpu.HBM)


def _gather_targets():
    x, y, c, chips = _chips()
    return 4 * x + 2 * y + c, [(x, y, 1 - c)] + [(*chip, c) for chip in chips]


def _landing_zone(w, l, me, name, tr=512, dtype=BF16):
    _, R, C = w.shape
    tr = min(tr, R)

    def body(me_ref, w_ref, o_ref):
        o_ref[...] = w_ref[...].astype(dtype)

    return pl.pallas_call(
        body, name=name, out_shape=jax.ShapeDtypeStruct((NDEV, R, C), dtype),
        grid_spec=pltpu.PrefetchScalarGridSpec(
            num_scalar_prefetch=1, grid=(R // tr,), in_specs=[pl.BlockSpec((None, tr, C), lambda i, me_ref: (l, i, 0))],
            out_specs=pl.BlockSpec((None, tr, C), lambda i, me_ref: (me_ref[0], i, 0))),
    )(me, w)


def _gather_start(land, order, name):
    def body(land_ref, order_ref, *rest):
        sems, token = rest[:8], rest[9]
        me, targets = _gather_targets()
        for k, to in enumerate(targets):
            pltpu.make_async_remote_copy(src_ref=land_ref.at[me], dst_ref=land_ref.at[me], send_sem=sems[k],
                                         recv_sem=sems[4 + k], device_id=to, device_id_type=MESH).start()
        token[...] = jnp.zeros_like(token)

    outs = pl.pallas_call(
        body, name=name,
        out_shape=(pltpu.SemaphoreType.DMA(()),) * 8 + (pltpu.HBM(land.shape, land.dtype), jax.ShapeDtypeStruct((8, 128), F32)),
        in_specs=(_HBM, _ANY), out_specs=(_SEM,) * 8 + (_HBM, pl.BlockSpec(memory_space=pltpu.VMEM)),
        input_output_aliases={0: 8}, compiler_params=pltpu.CompilerParams(has_side_effects=_DATAFLOW),
    )(_hbm(land), order)
    return outs[:8], outs[8], outs[9]


def _gather_wait(sems, land_thru, after, name):
    def body(land_ref, *rest):
        sems_ = rest[:8]
        me, targets = _gather_targets()
        for k, to in enumerate(targets):
            cp = pltpu.make_async_remote_copy(src_ref=land_ref.at[me], dst_ref=land_ref.at[me], send_sem=sems_[k],
                                              recv_sem=sems_[4 + k], device_id=to, device_id_type=MESH)
            cp.wait_send()
            cp.wait_recv()

    return pl.pallas_call(
        body, name=name, out_shape=pltpu.HBM(land_thru.shape, land_thru.dtype),
        in_specs=(_HBM,) + (_SEM,) * 8 + (_ANY,), out_specs=_HBM, input_output_aliases={0: 0},
        compiler_params=pltpu.CompilerParams(has_side_effects=_DATAFLOW),
    )(land_thru, *sems, after)


def _gather_finish(land, name):
    def body(land_ref, out, send, recv):
        x, y, c, chips = _chips()
        fwd = [pltpu.make_async_remote_copy(src_ref=out.at[4 * px + 2 * py + c], dst_ref=out.at[4 * px + 2 * py + c],
                                            send_sem=send.at[j], recv_sem=recv.at[j], device_id=(x, y, 1 - c), device_id_type=MESH)
               for j, (px, py) in enumerate(chips)]
        for cp in fwd:
            cp.start()
        for j, (px, py) in enumerate(chips):
            slot = out.at[4 * px + 2 * py + 1 - c]
            pltpu.make_async_remote_copy(src_ref=slot, dst_ref=slot, send_sem=send.at[j], recv_sem=recv.at[j],
                                         device_id=(x, y, 1 - c), device_id_type=MESH).wait()

    return pl.pallas_call(
        body, name=name, out_shape=jax.ShapeDtypeStruct(land.shape, land.dtype),
        in_specs=[_HBM], out_specs=_HBM, input_output_aliases={0: 0},
        scratch_shapes=[pltpu.SemaphoreType.DMA((3,)), pltpu.SemaphoreType.DMA((3,))],
        compiler_params=pltpu.CompilerParams(has_side_effects=True),
    )(land)


def _forward_start(land, order, name):
    def body(land_ref, order_ref, *rest):
        sems, token = rest[:6], rest[7]
        x, y, c, chips = _chips()
        for j, (px, py) in enumerate(chips):
            slot = land_ref.at[4 * px + 2 * py + c]
            pltpu.make_async_remote_copy(src_ref=slot, dst_ref=slot, send_sem=sems[j], recv_sem=sems[3 + j],
                                         device_id=(x, y, 1 - c), device_id_type=MESH).start()
        token[...] = jnp.zeros_like(token)

    outs = pl.pallas_call(
        body, name=name,
        out_shape=(pltpu.SemaphoreType.DMA(()),) * 6 + (pltpu.HBM(land.shape, land.dtype), jax.ShapeDtypeStruct((8, 128), F32)),
        in_specs=(_HBM, _ANY), out_specs=(_SEM,) * 6 + (_HBM, pl.BlockSpec(memory_space=pltpu.VMEM)),
        input_output_aliases={0: 6}, compiler_params=pltpu.CompilerParams(has_side_effects=_DATAFLOW),
    )(_hbm(land), order)
    return outs[:6], outs[6], outs[7]


def _forward_wait(sems, land_thru, after, name):
    def body(land_ref, *rest):
        sems_ = rest[:6]
        x, y, c, chips = _chips()
        for j, (px, py) in enumerate(chips):
            cp = pltpu.make_async_remote_copy(src_ref=land_ref.at[4 * px + 2 * py + c], dst_ref=land_ref.at[4 * px + 2 * py + 1 - c],
                                              send_sem=sems_[j], recv_sem=sems_[3 + j], device_id=(x, y, 1 - c),
                                              device_id_type=MESH)
            cp.wait_send()
            cp.wait_recv()

    return pl.pallas_call(
        body, name=name, out_shape=pltpu.HBM(land_thru.shape, land_thru.dtype),
        in_specs=(_HBM,) + (_SEM,) * 6 + (_ANY,), out_specs=_HBM, input_output_aliases={0: 0},
        compiler_params=pltpu.CompilerParams(has_side_effects=_DATAFLOW),
    )(land_thru, *sems, after)


def _chip_targets():
    x, y, c, chips = _chips()
    return 2 * x + y, [((px, py, c), 2 * px + py) for px, py in chips]


def _chipsum_start(s, land, order, name):
    def body(s_ref, land_ref, order_ref, *rest):
        sems, token = rest[:6], rest[8]
        mine, targets = _chip_targets()
        for k, (to, ch) in enumerate(targets):
            pltpu.make_async_remote_copy(src_ref=s_ref.at[ch], dst_ref=land_ref.at[mine], send_sem=sems[k], recv_sem=sems[3 + k],
                                         device_id=to, device_id_type=MESH).start()
        token[...] = jnp.zeros_like(token)

    outs = pl.pallas_call(
        body, name=name,
        out_shape=(pltpu.SemaphoreType.DMA(()),) * 6 + (pltpu.HBM(s.shape, s.dtype), pltpu.HBM(land.shape, land.dtype),
                                                        jax.ShapeDtypeStruct((8, 128), F32)),
        in_specs=(_HBM, _HBM, _ANY), out_specs=(_SEM,) * 6 + (_HBM, _HBM, pl.BlockSpec(memory_space=pltpu.VMEM)),
        input_output_aliases={0: 6, 1: 7}, compiler_params=pltpu.CompilerParams(has_side_effects=_DATAFLOW),
    )(_hbm(s), _hbm(land), order)
    return outs[:6], outs[6], outs[7], outs[8]


def _chipsum_wait(sems, s_thru, land_thru, after, name):
    def body(s_ref, land_ref, *rest):
        sems_ = rest[:6]
        mine, targets = _chip_targets()
        for k, (to, ch) in enumerate(targets):
            cp = pltpu.make_async_remote_copy(src_ref=s_ref.at[ch], dst_ref=land_ref.at[ch], send_sem=sems_[k], recv_sem=sems_[3 + k],
                                              device_id=to, device_id_type=MESH)
            cp.wait_send()
            cp.wait_recv()

    return pl.pallas_call(
        body, name=name, out_shape=(pltpu.HBM(s_thru.shape, s_thru.dtype), pltpu.HBM(land_thru.shape, land_thru.dtype)),
        in_specs=(_HBM, _HBM) + (_SEM,) * 6 + (_ANY,), out_specs=(_HBM, _HBM), input_output_aliases={0: 0, 1: 1},
        compiler_params=pltpu.CompilerParams(has_side_effects=_DATAFLOW),
    )(s_thru, land_thru, *sems, after)[1]


def _pair_start(p, order, name):
    def body(p_ref, land_ref, order_ref, *rest):
        sems, token = rest[:8], rest[10]
        x, y, c = _me()
        for ch in range(4):
            pltpu.make_async_remote_copy(src_ref=p_ref.at[ch, 1 - c], dst_ref=land_ref.at[ch], send_sem=sems[ch],
                                         recv_sem=sems[4 + ch], device_id=(x, y, 1 - c), device_id_type=MESH).start()
        token[...] = jnp.zeros_like(token)

    land = lax.empty((4,) + p.shape[2:], p.dtype)
    outs = pl.pallas_call(
        body, name=name,
        out_shape=(pltpu.SemaphoreType.DMA(()),) * 8 + (pltpu.HBM(p.shape, p.dtype), pltpu.HBM(land.shape, land.dtype),
                                                        jax.ShapeDtypeStruct((8, 128), F32)),
        in_specs=(_HBM, _HBM, _ANY), out_specs=(_SEM,) * 8 + (_HBM, _HBM, pl.BlockSpec(memory_space=pltpu.VMEM)),
        input_output_aliases={0: 8, 1: 9}, compiler_params=pltpu.CompilerParams(has_side_effects=_DATAFLOW),
    )(_hbm(p), _hbm(land), order)
    return outs[:8], outs[8], outs[9], outs[10]


def _pair_wait(sems, p_thru, land_thru, after, name):
    def body(p_ref, land_ref, *rest):
        sems_ = rest[:8]
        x, y, c = _me()
        for ch in range(4):
            cp = pltpu.make_async_remote_copy(src_ref=p_ref.at[ch, 1 - c], dst_ref=land_ref.at[ch], send_sem=sems_[ch],
                                              recv_sem=sems_[4 + ch], device_id=(x, y, 1 - c), device_id_type=MESH)
            cp.wait_send()
            cp.wait_recv()

    return pl.pallas_call(
        body, name=name, out_shape=(pltpu.HBM(p_thru.shape, p_thru.dtype), pltpu.HBM(land_thru.shape, land_thru.dtype)),
        in_specs=(_HBM, _HBM) + (_SEM,) * 8 + (_ANY,), out_specs=(_HBM, _HBM), input_output_aliases={0: 0, 1: 1},
        compiler_params=pltpu.CompilerParams(has_side_effects=_DATAFLOW),
    )(p_thru, land_thru, *sems, after)


def _sibling_start(p, order, name):
    def body(p_ref, land_ref, order_ref, send_sem, recv_sem, p_thru, land_thru, token):
        x, y, c = _me()
        pltpu.make_async_remote_copy(src_ref=p_ref, dst_ref=land_ref, send_sem=send_sem, recv_sem=recv_sem,
                                     device_id=(x, y, 1 - c), device_id_type=MESH).start()
        token[...] = jnp.zeros_like(token)

    land = lax.empty(p.shape, p.dtype)
    outs = pl.pallas_call(
        body, name=name,
        out_shape=(pltpu.SemaphoreType.DMA(()),) * 2 + (pltpu.HBM(p.shape, p.dtype), pltpu.HBM(p.shape, p.dtype),
                                                        jax.ShapeDtypeStruct((8, 128), F32)),
        in_specs=(_HBM, _HBM, _ANY), out_specs=(_SEM,) * 2 + (_HBM, _HBM, pl.BlockSpec(memory_space=pltpu.VMEM)),
        input_output_aliases={0: 2, 1: 3}, compiler_params=pltpu.CompilerParams(has_side_effects=_DATAFLOW),
    )(_hbm(p), _hbm(land), order)
    return outs[:2], outs[2], outs[3], outs[4]


def _sibling_wait(sems, p_thru, land_thru, after, name):
    def body(p_ref, land_ref, send_sem, recv_sem, after_ref, p_dead, got_ref):
        x, y, c = _me()
        cp = pltpu.make_async_remote_copy(src_ref=p_ref, dst_ref=land_ref, send_sem=send_sem, recv_sem=recv_sem,
                                          device_id=(x, y, 1 - c), device_id_type=MESH)
        cp.wait_send()
        cp.wait_recv()

    return pl.pallas_call(
        body, name=name, out_shape=(pltpu.HBM(p_thru.shape, p_thru.dtype), pltpu.HBM(land_thru.shape, land_thru.dtype)),
        in_specs=(_HBM, _HBM, _SEM, _SEM, _ANY), out_specs=(_HBM, _HBM), input_output_aliases={0: 0, 1: 1},
        compiler_params=pltpu.CompilerParams(has_side_effects=_DATAFLOW),
    )(p_thru, land_thru, *sems, after)


def _mm(a, b, *, mode, name, out_dtypes=(F32,), epilogue=None, extras=(), tm=1024, tn=1024, tk=2048,
        col_blocked_b=False, col_blocked_out=False, order=None):
    CB = 1024
    if col_blocked_b:
        assert mode in ("nn", "nt") and b.shape[2] == CB
        (M, K), N = a.shape, (b.shape[0] * CB if mode == "nn" else b.shape[1])
        assert mode == "nn" or tk % CB == 0
        tn = CB if mode == "nn" else tn
    elif mode == "nn":
        (M, K), N = a.shape, b.shape[1]
    elif mode == "nt":
        (M, K), N = a.shape, b.shape[0]
    else:
        (K, M), N = a.shape, b.shape[1]
    if col_blocked_out:
        assert len(out_dtypes) == 1 and N % CB == 0
        tn = CB
    tm, tn, tk = min(tm, M), min(tn, N), min(tk, K)
    assert M % tm == 0 and N % tn == 0 and K % tk == 0, (M, N, K, tm, tn, tk)
    nk = K // tk
    ne, no = len(extras), len(out_dtypes)
    dims = {"nn": (((1,), (0,)), ((), ())), "nt": (((1,), (1,)), ((), ())), "tn": (((0,), (0,)), ((), ()))}[mode]

    no_ = 0 if order is None else 1

    def body(a_ref, b_ref, *rest):
        rest = rest[no_:]
        ex, outs = rest[:ne], rest[ne:ne + no]

        def finish(acc):
            res = epilogue(acc, *[e[...] for e in ex]) if epilogue is not None else (acc,)
            for o, r in zip(outs, res):
                o[...] = r.astype(o.dtype)

        if col_blocked_b and mode == "nt":
            part = sum(lax.dot_general(a_ref[:, q * CB:(q + 1) * CB], b_ref[q], dims, preferred_element_type=F32)
                       for q in range(tk // CB))
        else:
            part = lax.dot_general(a_ref[...], b_ref[...].astype(BF16), dims, preferred_element_type=F32)
        if nk == 1:
            finish(part)
        else:
            acc_ref = rest[-1]
            k = pl.program_id(2)

            @pl.when(k == 0)
            def _():
                acc_ref[...] = part

            @pl.when(k > 0)
            def _():
                acc_ref[...] += part

            @pl.when(k == nk - 1)
            def _():
                finish(acc_ref[...])

    a_spec = {"nn": pl.BlockSpec((tm, tk), lambda i, j, k: (i, k)), "nt": pl.BlockSpec((tm, tk), lambda i, j, k: (i, k)),
              "tn": pl.BlockSpec((tk, tm), lambda i, j, k: (k, i))}[mode]
    b_spec = {"nn": pl.BlockSpec((tk, tn), lambda i, j, k: (k, j)), "nt": pl.BlockSpec((tn, tk), lambda i, j, k: (j, k)),
              "tn": pl.BlockSpec((tk, tn), lambda i, j, k: (k, j))}[mode]
    if col_blocked_b:
        b_spec = (pl.BlockSpec((None, tk, CB), lambda i, j, k: (j, k, 0)) if mode == "nn"
                  else pl.BlockSpec((tk // CB, tn, CB), lambda i, j, k: (k, j, 0)))
    e_spec = pl.BlockSpec((tm, tn), lambda i, j, k: (i, j))
    o_spec, o_dims = e_spec, (M, N)
    if col_blocked_out:
        o_spec, o_dims = pl.BlockSpec((None, tm, CB), lambda i, j, k: (j, i, 0)), (N // CB, M, CB)
    outs = pl.pallas_call(
        body, name=name, grid=(M // tm, N // tn, nk),
        in_specs=[a_spec, b_spec] + [_ANY] * no_ + [e_spec] * ne, out_specs=[o_spec] * no,
        out_shape=[jax.ShapeDtypeStruct(o_dims, dt) for dt in out_dtypes],
        scratch_shapes=[pltpu.VMEM((tm, tn), F32)] if nk > 1 else [],
        compiler_params=pltpu.CompilerParams(dimension_semantics=("parallel", "parallel", "arbitrary")),
    )(a, b, *([] if order is None else [order]), *extras)
    return outs if no > 1 else outs[0]


def _dw_half(a, b, side, *, axis, name, add=None, order=None, tile=1024, tk=4096):
    (K, M), N = a.shape, b.shape[1]
    tk = min(tk, K)
    nk = K // tk
    if axis == "m":
        tm, tn = tile, min(N, 1024)
        grid, o_dims = (4, N // tn, nk), (4, tile, N)
        a_spec = pl.BlockSpec((tk, tm), lambda q, j, k, s: (k, 2 * q + s[0]))
        b_spec = pl.BlockSpec((tk, tn), lambda q, j, k, s: (k, j))
        o_spec = pl.BlockSpec((None, tm, tn), lambda q, j, k, s: (q, 0, j))
    else:
        tm, tn = min(M, 1024), tile
        grid, o_dims = (M // tm, 4, nk), (4, M, tile)
        a_spec = pl.BlockSpec((tk, tm), lambda i, q, k, s: (k, i))
        b_spec = pl.BlockSpec((tk, tn), lambda i, q, k, s: (k, 2 * q + s[0]))
        o_spec = pl.BlockSpec((None, tm, tn), lambda i, q, k, s: (q, i, 0))
    n_order, n_add = int(order is not None), int(add is not None)
    n_out = 1 + n_add

    def body(s_ref, a_ref, b_ref, *rest):
        rest = rest[n_order:]
        outs = rest[n_add:n_add + n_out]

        def finish(acc):
            res = acc + rest[0][...].astype(F32) if n_add else acc
            for o in outs:
                o[...] = res.astype(o.dtype)

        part = _dot_tn(a_ref[...], b_ref[...])
        if nk == 1:
            finish(part)
        else:
            acc_ref, k = rest[-1], pl.program_id(2)

            @pl.when(k == 0)
            def _():
                acc_ref[...] = part

            @pl.when(k > 0)
            def _():
                acc_ref[...] += part

            @pl.when(k == nk - 1)
            def _():
                finish(acc_ref[...])

    outs = pl.pallas_call(
        body, name=name, out_shape=[jax.ShapeDtypeStruct(o_dims, BF16)] * n_out,
        grid_spec=pltpu.PrefetchScalarGridSpec(
            num_scalar_prefetch=1, grid=grid, in_specs=[a_spec, b_spec] + [_ANY] * n_order + [o_spec] * n_add,
            out_specs=[o_spec] * n_out, scratch_shapes=[pltpu.VMEM((tm, tn), F32)] if nk > 1 else []),
        compiler_params=pltpu.CompilerParams(dimension_semantics=("arbitrary", "arbitrary", "arbitrary")),
    )(side, a, b, *([order] if n_order else []), *([add] if n_add else []))
    return outs if n_add else outs[0]


def _norm_fwd(x, g, sc, sh, name, resid=None):
    B, S, Dm = x.shape
    ts = min(S, 256)
    tok = pl.BlockSpec((None, ts, Dm), lambda b, i: (b, i, 0))
    row = pl.BlockSpec((None, 1, Dm), lambda b, i: (b, 0, 0))
    par = pl.BlockSpec((1, Dm), lambda b, i: (0, 0))

    def body(*refs):
        if resid is not None:
            x_ref, br_ref, gt_ref, g_ref, sc_ref, sh_ref, xo_ref, h_ref = refs
            xv = x_ref[...] + gt_ref[...] * br_ref[...]
            xo_ref[...] = xv
        else:
            x_ref, g_ref, sc_ref, sh_ref, h_ref = refs
            xv = x_ref[...]
        r = lax.rsqrt(jnp.mean(xv * xv, axis=-1, keepdims=True) + EPS)
        h_ref[...] = ((xv * r * g_ref[...]) * (1.0 + sc_ref[...]) + sh_ref[...]).astype(BF16)

    h_shape = jax.ShapeDtypeStruct((B, S, Dm), BF16)
    if resid is not None:
        return pl.pallas_call(body, name=name, grid=(B, S // ts), in_specs=[tok, tok, row, par, row, row],
                              out_specs=[tok, tok], out_shape=[jax.ShapeDtypeStruct((B, S, Dm), F32), h_shape],
                              )(x, resid[0], resid[1], g, sc, sh)
    return pl.pallas_call(body, name=name, grid=(B, S // ts), in_specs=[tok, par, row, row], out_specs=tok,
                          out_shape=h_shape)(x, g, sc, sh)


def _norm_bwd(x, g, name, *, sc=None, dh=None, dres=None, tgt=None, br=None, gate=None, x_is_prev=False):
    B, S, Dm = x.shape
    ts = min(S, 256)
    final = tgt is not None
    has_br = br is not None
    tok = pl.BlockSpec((None, ts, Dm), lambda b, i: (b, i, 0))
    row = pl.BlockSpec((None, 1, Dm), lambda b, i: (b, 0, 0))
    par = pl.BlockSpec((1, Dm), lambda b, i: (0, 0))
    ins, in_specs = [x, g], [tok, par]
    if final:
        ins, in_specs = ins + [tgt], in_specs + [tok]
    else:
        ins, in_specs = ins + [sc, dh], in_specs + [row, tok]
    if dres is not None:
        ins, in_specs = ins + [dres], in_specs + [tok]
    if has_br:
        ins, in_specs = ins + [br, gate], in_specs + [tok, row]
    n_in = len(ins)
    out_shape = [jax.ShapeDtypeStruct((B, S, Dm), F32), jax.ShapeDtypeStruct((1, Dm), F32)]
    out_specs = [tok, par]
    if final:
        out_shape.append(jax.ShapeDtypeStruct((1, 128), F32))
        out_specs.append(pl.BlockSpec((1, 128), lambda b, i: (0, 0)))
    else:
        out_shape += [jax.ShapeDtypeStruct((B, 1, Dm), F32)] * 2
        out_specs += [row, row]
    if has_br:
        out_shape += [jax.ShapeDtypeStruct((B, S, Dm), BF16), jax.ShapeDtypeStruct((B, 1, Dm), F32)]
        out_specs += [tok, row]

    def body(*refs):
        it = iter(refs[:n_in])
        outs = iter(refs[n_in:])
        x_ref, g_ref = next(it), next(it)
        b, i = pl.program_id(0), pl.program_id(1)
        first, first_row = (b == 0) & (i == 0), i == 0
        xv, gv = x_ref[...], g_ref[...]
        if x_is_prev:
            xv = xv + refs[n_in - 1][...] * refs[n_in - 2][...]
        r = lax.rsqrt(jnp.mean(xv * xv, axis=-1, keepdims=True) + EPS)
        n = xv * r
        dx_ref, dg_ref = next(outs), next(outs)

        def acc(ref, val, init):
            @pl.when(init)
            def _():
                ref[...] = val

            @pl.when(jnp.logical_not(init))
            def _():
                ref[...] += val

        if final:
            t_ref = next(it)
            loss_ref = next(outs)
            e = n * gv - t_ref[...]
            acc(loss_ref, jnp.zeros((1, 128), F32) + 0.5 * jnp.sum(e * e) / Dm, first)
            dyg = e * (1.0 / Dm)
        else:
            sc_ref, dh_ref = next(it), next(it)
            dsc_ref, dsh_ref = next(outs), next(outs)
            dhv = dh_ref[...].astype(F32)
            acc(dsh_ref, jnp.sum(dhv, axis=0, keepdims=True), first_row)
            acc(dsc_ref, jnp.sum(dhv * (n * gv), axis=0, keepdims=True), first_row)
            dyg = dhv * (1.0 + sc_ref[...])
        acc(dg_ref, jnp.sum(dyg * n, axis=0, keepdims=True), first)
        dn = dyg * gv
        dx = r * (dn - n * jnp.mean(dn * n, axis=-1, keepdims=True))
        if dres is not None:
            dx = dx + next(it)[...]
        dx_ref[...] = dx
        if has_br:
            br_ref, gt_ref = next(it), next(it)
            dbr_ref, dgt_ref = next(outs), next(outs)
            dbr_ref[...] = (dx * gt_ref[...]).astype(BF16)
            acc(dgt_ref, jnp.sum(dx * br_ref[...], axis=0, keepdims=True), first_row)

    outs = pl.pallas_call(body, name=name, grid=(B, S // ts), in_specs=in_specs, out_specs=out_specs, out_shape=out_shape,
                          compiler_params=pltpu.CompilerParams(dimension_semantics=("arbitrary", "arbitrary")))(*ins)
    res = dict(dx=outs[0], dg=outs[1])
    if final:
        res["loss"] = outs[2]
    else:
        res["dsc"], res["dsh"] = outs[2], outs[3]
    if has_br:
        res["dbr"], res["dgate"] = outs[-2], outs[-1]
    return res


def _gm_heads(vg, lng, lnb):
    res = []
    for h in range(GM_H):
        sl = slice(h * 128, (h + 1) * 128)
        vh = vg[:, sl]
        xc = vh - jnp.mean(vh, axis=-1, keepdims=True)
        rstd = lax.rsqrt(jnp.mean(xc * xc, axis=-1, keepdims=True) + 1e-5)
        xhat = xc * rstd
        res.append((xhat, rstd, xhat * lng[:, sl] + lnb[:, sl]))
    return res


def _gm_gate(heads, wt_ref, bsx, nch):
    cols = []
    for h in range(GM_H):
        vn = heads[h][2].astype(BF16)
        rows = [_dot(wt_ref[h], vn[c * CHUNK:(c + 1) * CHUNK]) + bsx[:, h * 128:(h + 1) * 128] for c in range(nch)]
        cols.append(jnp.concatenate(rows, axis=0) if nch > 1 else rows[0])
    return jnp.concatenate(cols, axis=1)


def _gm_specs(S):
    tb = min(S, 512)
    u = pl.BlockSpec((None, tb, GM_W), lambda b, i: (b, i, OFF["u"] // GM_W))
    v = pl.BlockSpec((None, tb, GM_W), lambda b, i: (b, i, OFF["v"] // GM_W))
    tok = pl.BlockSpec((None, tb, GM_W), lambda b, i: (b, i, 0))
    return tb, u, v, tok


def _gmlp_fwd(P, lng, lnb, wt, bsx, og, name):
    B, S, _ = P.shape
    tb, u_spec, v_spec, tok = _gm_specs(S)
    nch = tb // CHUNK

    def body(u_ref, v_ref, lng_ref, lnb_ref, wt_ref, bsx_ref, og_ref, o_ref):
        heads = _gm_heads(_gelu(v_ref[...]), lng_ref[...], lnb_ref[...])
        y = _gelu(u_ref[...]) * _gm_gate(heads, wt_ref, bsx_ref[...], nch)
        r = lax.rsqrt(jnp.mean(y * y, axis=-1, keepdims=True) + EPS)
        o_ref[...] = (y * r * og_ref[...]).astype(BF16)

    return pl.pallas_call(
        body, name=name, grid=(B, S // tb),
        in_specs=[u_spec, v_spec, _full((1, GM_W)), _full((1, GM_W)), _full((GM_H, 128, 128)), _full((128, GM_W)), _full((1, GM_W))],
        out_specs=tok, out_shape=jax.ShapeDtypeStruct((B, S, GM_W + ATT_W + SSM_W), BF16))(P, P, lng, lnb, wt, bsx, og)


def _gmlp_bwd(P, dcat, lng, lnb, wt, wtT, bsx, og, name):
    B, S, _ = P.shape
    tb, u_spec, v_spec, tok = _gm_specs(S)
    nch = tb // CHUNK
    do_spec = pl.BlockSpec((None, tb, GM_W), lambda b, i: (b, i, 0))

    def body(u_ref, v_ref, do_ref, lng_ref, lnb_ref, wt_ref, wtT_ref, bsx_ref, og_ref,
             du_ref, dv_ref, dlng_ref, dlnb_ref, dws_ref, dbsx_ref, dog_ref):
        first = (pl.program_id(0) == 0) & (pl.program_id(1) == 0)

        @pl.when(first)
        def _():
            for ref in (dlng_ref, dlnb_ref, dws_ref, dbsx_ref, dog_ref):
                ref[...] = jnp.zeros(ref.shape, F32)

        u, v, lng = u_ref[...], v_ref[...], lng_ref[...]
        ug = _gelu(u)
        heads = _gm_heads(_gelu(v), lng, lnb_ref[...])
        gate = _gm_gate(heads, wt_ref, bsx_ref[...], nch)
        y = ug * gate
        r = lax.rsqrt(jnp.mean(y * y, axis=-1, keepdims=True) + EPS)
        yn = y * r
        dout = do_ref[...].astype(F32)
        dog_ref[...] += jnp.sum(dout * yn, axis=0, keepdims=True)
        dyn = dout * og_ref[...]
        dy = r * (dyn - yn * jnp.mean(dyn * yn, axis=-1, keepdims=True))
        du_ref[...] = (dy * gate * _gelu_grad(u)).astype(BF16)
        dgate = dy * ug
        tril = lax.broadcasted_iota(jnp.int32, (128, 128), 0) >= lax.broadcasted_iota(jnp.int32, (128, 128), 1)
        dvg = []
        for h in range(GM_H):
            sl = slice(h * 128, (h + 1) * 128)
            xhat, rstd, vn = heads[h]
            vnb = vn.astype(BF16)
            dgh = dgate[:, sl]
            dgb = dgh.astype(BF16)
            dbs = jnp.zeros((128, 128), F32)
            dw = jnp.zeros((128, 128), F32)
            dvn = []
            for c in range(nch):
                rs = slice(c * CHUNK, (c + 1) * CHUNK)
                dbs = dbs + dgh[rs]
                dw = dw + _dot_nt(dgb[rs], vnb[rs])
                dvn.append(_dot(wtT_ref[h], dgb[rs]))
            dvn = jnp.concatenate(dvn, axis=0) if nch > 1 else dvn[0]
            dbsx_ref[:, sl] += dbs
            dws_ref[h] += jnp.where(tril, dw, 0.0)
            dlng_ref[:, sl] += jnp.sum(dvn * xhat, axis=0, keepdims=True)
            dlnb_ref[:, sl] += jnp.sum(dvn, axis=0, keepdims=True)
            dxh = dvn * lng[:, sl]
            dvg.append(rstd * (dxh - jnp.mean(dxh, axis=-1, keepdims=True) - xhat * jnp.mean(dxh * xhat, axis=-1, keepdims=True)))
        dv_ref[...] = (jnp.concatenate(dvg, axis=1) * _gelu_grad(v)).astype(BF16)

    p512, w3 = _full((1, GM_W)), _full((GM_H, 128, 128))
    return pl.pallas_call(
        body, name=name, grid=(B, S // tb),
        in_specs=[u_spec, v_spec, do_spec, p512, p512, w3, w3, _full((128, GM_W)), p512],
        out_specs=[tok, tok, p512, p512, w3, _full((128, GM_W)), p512],
        out_shape=[jax.ShapeDtypeStruct((B, S, GM_W), BF16)] * 2 + [
            jax.ShapeDtypeStruct((1, GM_W), F32), jax.ShapeDtypeStruct((1, GM_W), F32),
            jax.ShapeDtypeStruct((GM_H, 128, 128), F32), jax.ShapeDtypeStruct((128, GM_W), F32),
            jax.ShapeDtypeStruct((1, GM_W), F32)],
        compiler_params=pltpu.CompilerParams(dimension_semantics=("arbitrary", "arbitrary")),
    )(P, P, dcat, lng, lnb, wt, wtT, bsx, og)


def _lane_half():
    return lax.broadcasted_iota(jnp.int32, (128, 128), 1) // 64


def _att_stack(x, kvh, dtype):
    half = _lane_half()
    rows = []
    for g in range(4):
        i = kvh * 4 + g
        pair = x[:, (i // 2) * 128:(i // 2 + 1) * 128]
        if i % 2 != kvh:
            pair = pltpu.roll(pair, 64, 1)
        rows.append(jnp.where(half == kvh, pair, 0.0))
    return jnp.concatenate(rows, axis=0).astype(dtype)


def _att_unstack(pairs, y, kvh):
    half = _lane_half()
    for g in range(4):
        i = kvh * 4 + g
        piece = y[g * 128:(g + 1) * 128]
        if i % 2 != kvh:
            piece = pltpu.roll(piece, 64, 1)
        pairs[i // 2] = jnp.where(half == i % 2, piece, pairs[i // 2])
    return pairs


def _att_fill_bias(bias_ref):
    qi = lax.broadcasted_iota(jnp.int32, (512, 256), 0) % 128
    kj = lax.broadcasted_iota(jnp.int32, (512, 256), 1)
    diff = qi + 128 - kj
    band = (diff >= 0) & (diff < 128)
    bias_ref[0:512, :] = jnp.where(band, 0.0, NEG_INF)
    bias_ref[512:1024, :] = jnp.where(band & (kj >= 128), 0.0, NEG_INF)


def _att_bias(bias_ref, n):
    return bias_ref[pl.ds(pl.multiple_of(jnp.where(n == 0, 512, 0), 512), 512), :]


def _att_probs(qb, k2, bias, sink_ref, kvh):
    qm = _att_stack(qb, kvh, BF16)
    s = _dot_nt(qm, k2) * (64 ** -0.5) + bias
    grp = lax.broadcasted_iota(jnp.int32, (512, 1), 0) // 128
    sink = jnp.zeros((512, 1), F32)
    for g in range(4):
        sink = jnp.where(grp == g, sink_ref[kvh * 4 + g], sink)
    m = jnp.maximum(jnp.max(s, axis=-1, keepdims=True), sink)
    e = jnp.exp(s - m)
    esink = jnp.exp(sink - m)
    inv = 1.0 / (jnp.sum(e, axis=-1, keepdims=True) + esink)
    return qm, e * inv, esink * inv


def _att_specs(S):
    q = pl.BlockSpec((None, S, ATT_W), lambda b: (b, 0, OFF["q"] // ATT_W))
    k = pl.BlockSpec((None, S, KV_W), lambda b: (b, 0, OFF["k"] // KV_W))
    v = pl.BlockSpec((None, S, KV_W), lambda b: (b, 0, OFF["vv"] // KV_W))
    tok = pl.BlockSpec((None, S, ATT_W), lambda b: (b, 0, 0))
    kv = pl.BlockSpec((None, S, KV_W), lambda b: (b, 0, 0))
    return q, k, v, tok, kv


_SMEM = pl.BlockSpec(memory_space=pltpu.SMEM)


def _attn_fwd(P, sinks, og, cat, name):
    B, S, _ = P.shape
    q_spec, k_spec, v_spec, _, _ = _att_specs(S)
    tok = pl.BlockSpec((None, S, ATT_W), lambda b: (b, 0, GM_W // ATT_W))

    def body(q_ref, k_ref, v_ref, sink_ref, og_ref, cat_ref, o_ref, kpad, vpad, bias_ref):
        _att_fill_bias(bias_ref)
        kpad[0:128, :] = jnp.zeros((128, KV_W), BF16)
        vpad[0:128, :] = jnp.zeros((128, KV_W), BF16)
        kpad[128:, :] = k_ref[...].astype(BF16)
        vpad[128:, :] = v_ref[...].astype(BF16)

        def step(n, carry):
            st = pl.multiple_of(n * 128, 128)
            qb = q_ref[pl.ds(st, 128), :]
            k2, v2 = kpad[pl.ds(st, 256), :], vpad[pl.ds(st, 256), :]
            pairs = [jnp.zeros((128, 128), F32)] * 4
            bias = _att_bias(bias_ref, n)
            for kvh in range(2):
                _, p, _ = _att_probs(qb, k2, bias, sink_ref, kvh)
                pairs = _att_unstack(pairs, _dot(p.astype(BF16), v2), kvh)
            o = jnp.concatenate(pairs, axis=1)
            r = lax.rsqrt(jnp.mean(o * o, axis=-1, keepdims=True) + EPS)
            o_ref[pl.ds(st, 128), :] = (o * r * og_ref[...]).astype(BF16)
            return carry

        lax.fori_loop(0, S // 128, step, 0)

    return pl.pallas_call(
        body, name=name, grid=(B,), in_specs=[q_spec, k_spec, v_spec, _SMEM, _full((1, ATT_W)), _ANY], out_specs=tok,
        out_shape=jax.ShapeDtypeStruct(cat.shape, BF16), input_output_aliases={5: 0},
        scratch_shapes=[pltpu.VMEM((S + 128, KV_W), BF16)] * 2 + [pltpu.VMEM((1024, 256), F32)])(P, P, P, sinks, og, cat)


def _attn_bwd(P, dcat, sinks, og, name):
    B, S, _ = P.shape
    q_spec, k_spec, v_spec, tok, kv = _att_specs(S)
    do_spec = pl.BlockSpec((None, S, ATT_W), lambda b: (b, 0, GM_W // ATT_W))

    def body(q_ref, k_ref, v_ref, do_ref, sink_ref, og_ref, dq_ref, dk_ref, dv_ref, dsink_ref, dog_ref,
             kpad, vpad, dkpad, dvpad, bias_ref):
        _att_fill_bias(bias_ref)

        @pl.when(pl.program_id(0) == 0)
        def _():
            dsink_ref[...] = jnp.zeros((8, 128), F32)
            dog_ref[...] = jnp.zeros((1, ATT_W), F32)

        kpad[0:128, :] = jnp.zeros((128, KV_W), BF16)
        vpad[0:128, :] = jnp.zeros((128, KV_W), BF16)
        kpad[128:, :] = k_ref[...].astype(BF16)
        vpad[128:, :] = v_ref[...].astype(BF16)
        dkpad[...] = jnp.zeros((S + 128, KV_W), F32)
        dvpad[...] = jnp.zeros((S + 128, KV_W), F32)
        half = _lane_half()
        head_row = lax.broadcasted_iota(jnp.int32, (8, 128), 0)

        def step(n, carry):
            st = pl.multiple_of(n * 128, 128)
            qb = q_ref[pl.ds(st, 128), :]
            k2, v2 = kpad[pl.ds(st, 256), :], vpad[pl.ds(st, 256), :]
            saved, pairs = [], [jnp.zeros((128, 128), F32)] * 4
            bias = _att_bias(bias_ref, n)
            for kvh in range(2):
                qm, p, psink = _att_probs(qb, k2, bias, sink_ref, kvh)
                o = _dot(p.astype(BF16), v2)
                saved.append((qm, p, psink, o))
                pairs = _att_unstack(pairs, o, kvh)
            o = jnp.concatenate(pairs, axis=1)
            r = lax.rsqrt(jnp.mean(o * o, axis=-1, keepdims=True) + EPS)
            on = o * r
            dout = do_ref[pl.ds(st, 128), :].astype(F32)
            dog_ref[...] += jnp.sum(dout * on, axis=0, keepdims=True)
            dyn = dout * og_ref[...]
            do = r * (dyn - on * jnp.mean(dyn * on, axis=-1, keepdims=True))
            dq_pairs = [jnp.zeros((128, 128), F32)] * 4
            dsink = jnp.zeros((8, 128), F32)
            for kvh in range(2):
                qm, p, psink, og_ = saved[kvh]
                dog = _att_stack(do, kvh, F32)
                delta = jnp.sum(dog * jnp.where(jnp.concatenate([half] * 4, axis=0) == kvh, og_, 0.0), axis=-1, keepdims=True)
                dogb, pb = dog.astype(BF16), p.astype(BF16)
                dvpad[pl.ds(st, 256), :] += _dot_tn(pb, dogb)
                dp = _dot_nt(dogb, v2)
                ds = (p * (dp - delta) * (64 ** -0.5)).astype(BF16)
                sd = psink * delta
                for g in range(4):
                    dsink = dsink - jnp.where(head_row == kvh * 4 + g, jnp.sum(sd[g * 128:(g + 1) * 128]), 0.0)
                dq_pairs = _att_unstack(dq_pairs, _dot(ds, k2), kvh)
                dkpad[pl.ds(st, 256), :] += _dot_tn(ds, qm)
            dsink_ref[...] += dsink
            dq_ref[pl.ds(st, 128), :] = jnp.concatenate(dq_pairs, axis=1).astype(BF16)
            return carry

        lax.fori_loop(0, S // 128, step, 0)
        dk_ref[...] = dkpad[128:, :].astype(BF16)
        dv_ref[...] = dvpad[128:, :].astype(BF16)

    return pl.pallas_call(
        body, name=name, grid=(B,),
        in_specs=[q_spec, k_spec, v_spec, do_spec, _SMEM, _full((1, ATT_W))],
        out_specs=[tok, kv, kv, _full((8, 128)), _full((1, ATT_W))],
        out_shape=[jax.ShapeDtypeStruct((B, S, ATT_W), BF16), jax.ShapeDtypeStruct((B, S, KV_W), BF16),
                   jax.ShapeDtypeStruct((B, S, KV_W), BF16), jax.ShapeDtypeStruct((8, 128), F32),
                   jax.ShapeDtypeStruct((1, ATT_W), F32)],
        scratch_shapes=[pltpu.VMEM((S + 128, KV_W), BF16)] * 2 + [pltpu.VMEM((S + 128, KV_W), F32)] * 2
        + [pltpu.VMEM((1024, 256), F32)],
        compiler_params=pltpu.CompilerParams(dimension_semantics=("arbitrary",)),
    )(P, P, P, dcat, sinks, og)


CONV_TC = 256
CONV_RC = 64


def _conv_taps(ext, r0):
    return [ext[pl.ds(r0 + 8 - k, CONV_RC), :] for k in range(4)]


def _conv_pre(taps, w_ref, b_ref):
    acc = b_ref[...] + w_ref[3:4, :] * taps[0]
    for k in range(1, 4):
        acc = acc + w_ref[3 - k:4 - k, :] * taps[k]
    return acc


def _conv_fwd(P, w8, b, name):
    B, S, _ = P.shape
    nj = CONV_CH // CONV_TC
    x_spec = pl.BlockSpec((None, S, CONV_TC), lambda b_, j: (b_, 0, OFF["xbc"] // CONV_TC + j))
    tok = pl.BlockSpec((None, S, CONV_TC), lambda b_, j: (b_, 0, j))

    def body(x_ref, w_ref, b_ref, o_ref, ext):
        ext[0:8, :] = jnp.zeros((8, CONV_TC), F32)
        ext[8:, :] = x_ref[...]
        for r0 in range(0, S, CONV_RC):
            pre = _conv_pre(_conv_taps(ext, r0), w_ref, b_ref)
            o_ref[pl.ds(r0, CONV_RC), :] = pre * _sigmoid(pre)

    return pl.pallas_call(
        body, name=name, grid=(B, nj),
        in_specs=[x_spec, pl.BlockSpec((8, CONV_TC), lambda b_, j: (0, j)), pl.BlockSpec((1, CONV_TC), lambda b_, j: (0, j))],
        out_specs=tok, out_shape=jax.ShapeDtypeStruct((B, S, CONV_CH), F32),
        scratch_shapes=[pltpu.VMEM((S + 8, CONV_TC), F32)])(P, w8, b)


def _conv_bwd(P, dact, w8, b, name):
    B, S, _ = P.shape
    nj = CONV_CH // CONV_TC
    x_spec = pl.BlockSpec((None, S, CONV_TC), lambda j, b_: (b_, 0, OFF["xbc"] // CONV_TC + j))
    tok = pl.BlockSpec((None, S, CONV_TC), lambda j, b_: (b_, 0, j))
    w_spec = pl.BlockSpec((8, CONV_TC), lambda j, b_: (0, j))
    b_spec = pl.BlockSpec((1, CONV_TC), lambda j, b_: (0, j))

    def body(x_ref, d_ref, w_ref, b_ref, dx_ref, dw_ref, db_ref, ext, extd):
        @pl.when(pl.program_id(1) == 0)
        def _():
            dw_ref[...] = jnp.zeros((8, CONV_TC), F32)
            db_ref[...] = jnp.zeros((1, CONV_TC), F32)

        ext[0:8, :] = jnp.zeros((8, CONV_TC), F32)
        ext[8:, :] = x_ref[...]
        extd[pl.ds(8 + S, 8), :] = jnp.zeros((8, CONV_TC), F32)
        db = jnp.zeros((1, CONV_TC), F32)
        dws = [jnp.zeros((1, CONV_TC), F32)] * 4
        for r0 in range(0, S, CONV_RC):
            taps = _conv_taps(ext, r0)
            pre = _conv_pre(taps, w_ref, b_ref)
            sg = _sigmoid(pre)
            dpre = d_ref[pl.ds(r0, CONV_RC), :] * (sg * (1.0 + pre * (1.0 - sg)))
            extd[pl.ds(8 + r0, CONV_RC), :] = dpre
            db = db + jnp.sum(dpre, axis=0, keepdims=True)
            dws = [dws[i] + jnp.sum(dpre * taps[3 - i], axis=0, keepdims=True) for i in range(4)]
        for r0 in range(0, S, CONV_RC):
            dx = w_ref[3:4, :] * extd[pl.ds(8 + r0, CONV_RC), :]
            for k in range(1, 4):
                dx = dx + w_ref[3 - k:4 - k, :] * extd[pl.ds(8 + r0 + k, CONV_RC), :]
            dx_ref[pl.ds(r0, CONV_RC), :] = dx.astype(BF16)
        db_ref[...] += db
        sub = lax.broadcasted_iota(jnp.int32, (8, CONV_TC), 0)
        dw_ref[...] += sum(jnp.where(sub == i, dws[i], 0.0) for i in range(4))

    return pl.pallas_call(
        body, name=name, grid=(nj, B), in_specs=[x_spec, tok, w_spec, b_spec], out_specs=[tok, w_spec, b_spec],
        out_shape=[jax.ShapeDtypeStruct((B, S, CONV_CH), BF16), jax.ShapeDtypeStruct((8, CONV_CH), F32),
                   jax.ShapeDtypeStruct((1, CONV_CH), F32)],
        scratch_shapes=[pltpu.VMEM((S + 8, CONV_TC), F32), pltpu.VMEM((S + 16, CONV_TC), F32)],
        compiler_params=pltpu.CompilerParams(dimension_semantics=("arbitrary", "arbitrary")),
    )(P, dact, w8, b)


def _ssd_consts():
    hd = np.arange(SSM_W) // SSM_HD
    E = (np.arange(128)[:, None] == hd[None, :]).astype(np.float32)
    tri = (np.arange(128)[:, None] >= np.arange(128)[None, :]).astype(np.float32)
    return jnp.asarray(E, BF16), jnp.asarray(E.T, BF16), jnp.asarray(tri, BF16), jnp.asarray(tri.T, BF16)


def _pieces(x, n):
    out, r = [], x
    for _ in range(n):
        p = r.astype(BF16)
        out.append(p)
        r = r - p.astype(F32)
    return out


def _dot01(x, m01, n):
    return sum(_dot(p, m01) for p in _pieces(x, n))


def _dot01_left(m01, x, n):
    return sum(_dot(m01, p) for p in _pieces(x, n))


def _ssd_pre(xa, dtraw, bias, alog, E, tri):
    lane = lax.broadcasted_iota(jnp.int32, (128, 128), 1)
    pre = dtraw + bias
    dtp = jnp.where(lane < SSM_H, jnp.maximum(pre, 0.0) + jnp.log(1.0 + jnp.exp(-jnp.abs(pre))), 0.0)
    a = -jnp.exp(alog)
    acs = _dot01_left(tri, dtp * a, 3)
    acsT = acs.T
    dtE, acsE = _dot01(dtp, E, 2), _dot01(acs, E, 3)
    X = xa[:, :SSM_W]
    xdt = X * dtE
    wE = jnp.exp(acsE[127:128, :] - acsE)
    eE = jnp.exp(acsE)
    cdE = eE[127:128, :]
    return dict(pre=pre, dtp=dtp, a=a, acs=acs, acsT=acsT, dtE=dtE, acsE=acsE, cdE=cdE, X=X, xdt=xdt, wE=wE, eE=eE)


def _ssd_decay(c, h):
    lm = lax.broadcasted_iota(jnp.int32, (128, 128), 0) >= lax.broadcasted_iota(jnp.int32, (128, 128), 1)
    return jnp.exp(jnp.where(lm, c["acs"][:, h:h + 1] - c["acsT"][h:h + 1, :], NEG_INF))


def _ssd_pair_operands(c, CB, h0):
    lane = lax.broadcasted_iota(jnp.int32, (128, 128), 1)
    L0, L1 = _ssd_decay(c, h0), _ssd_decay(c, h0 + 1)
    M = jnp.concatenate([CB * L0, CB * L1], axis=1).astype(BF16)
    xp = c["xdt"][:, h0 * 64:h0 * 64 + 128]
    BD = jnp.concatenate([jnp.where(lane < 64, xp, 0.0), jnp.where(lane >= 64, xp, 0.0)], axis=0).astype(BF16)
    return L0, L1, M, BD


def _ssd_y(c, xa, state_ref, dskipE):
    per_group, ys = [], []
    for g in range(SSM_G):
        gs = slice(g * 512, (g + 1) * 512)
        Bb = xa[:, SSM_W + g * 128:SSM_W + (g + 1) * 128].astype(BF16)
        Cb = xa[:, SSM_W + 256 + g * 128:SSM_W + 256 + (g + 1) * 128].astype(BF16)
        CB = _dot_nt(Cb, Bb)
        Sg = state_ref[:, gs]
        yoff = _dot(Cb, Sg.astype(BF16)) * c["eE"][:, gs]
        ydiag, pairs = [], []
        for j in range(4):
            ops = _ssd_pair_operands(c, CB, g * 8 + 2 * j)
            pairs.append(ops)
            ydiag.append(_dot(ops[2], ops[3]))
        ys.append(jnp.concatenate(ydiag, axis=1) + yoff)
        per_group.append(dict(Bb=Bb, Cb=Cb, CB=CB, Sg=Sg, yoff=yoff, pairs=pairs))
    Y = jnp.concatenate(ys, axis=1) + c["X"] * dskipE
    return Y, per_group


def _ssd_specs(S, rev):
    nc = S // CHUNK
    cm = (lambda b, i: (b, nc - 1 - i)) if rev else (lambda b, i: (b, i))
    xa = pl.BlockSpec((None, CHUNK, CONV_CH), lambda b, i: cm(b, i) + (0,))
    z = [pl.BlockSpec((None, CHUNK, 256), lambda b, i, q=q: cm(b, i) + (OFF["z"] // 256 + q,)) for q in range(4)]
    dt = pl.BlockSpec((None, CHUNK, 128), lambda b, i: cm(b, i) + (OFF["dt"] // 128,))
    tok = pl.BlockSpec((None, CHUNK, SSM_W), lambda b, i: cm(b, i) + (0,))
    st = pl.BlockSpec((None, None, 128, SSM_W), lambda b, i: cm(b, i) + (0, 0))
    return nc, xa, z, dt, tok, st


def _ssd_fwd(xact, P, bias, alog, dskipE, ng, cat, name):
    B, S, _ = P.shape
    nc, xa_spec, z_specs, dt_spec, _, st_spec = _ssd_specs(S, False)
    tok = pl.BlockSpec((None, CHUNK, SSM_W), lambda b, i: (b, i, 1))
    E, _, tri, _ = _ssd_consts()

    def body(xa_ref, z0, z1, z2, z3, dt_ref, bias_ref, alog_ref, dsk_ref, ng_ref, E_ref, tri_ref, cat_ref, o_ref, sp_ref, state):
        @pl.when(pl.program_id(1) == 0)
        def _():
            state[...] = jnp.zeros((128, SSM_W), F32)

        sp_ref[...] = state[...]
        xa = xa_ref[...]
        c = _ssd_pre(xa, dt_ref[...], bias_ref[...], alog_ref[...], E_ref[...], tri_ref[...])
        Y, groups = _ssd_y(c, xa, state, dsk_ref[...])
        Z = (c["xdt"] * c["wE"]).astype(BF16)
        for g in range(SSM_G):
            gs = slice(g * 512, (g + 1) * 512)
            state[:, gs] = groups[g]["Sg"] * c["cdE"][:, gs] + _dot_tn(groups[g]["Bb"], Z[:, gs])
        zv = jnp.concatenate([z0[...], z1[...], z2[...], z3[...]], axis=1)
        yz = Y * (zv * _sigmoid(zv))
        outs = []
        for g in range(SSM_G):
            yg = yz[:, g * 512:(g + 1) * 512]
            outs.append(yg * lax.rsqrt(jnp.mean(yg * yg, axis=-1, keepdims=True) + EPS))
        o_ref[...] = (jnp.concatenate(outs, axis=1) * ng_ref[...]).astype(BF16)

    return pl.pallas_call(
        body, name=name, grid=(B, nc),
        in_specs=[xa_spec] + z_specs + [dt_spec, _full((1, 128)), _full((1, 128)), _full((1, SSM_W)), _full((1, SSM_W)),
                                        _full((128, SSM_W)), _full((128, 128)), _ANY],
        out_specs=[tok, st_spec],
        out_shape=[jax.ShapeDtypeStruct(cat.shape, BF16), jax.ShapeDtypeStruct((B, nc, 128, SSM_W), F32)],
        scratch_shapes=[pltpu.VMEM((128, SSM_W), F32)], input_output_aliases={12: 0},
        compiler_params=pltpu.CompilerParams(dimension_semantics=("arbitrary", "arbitrary")),
    )(xact, P, P, P, P, P, bias, alog, dskipE, ng, E, tri, cat)


def _ssd_bwd(xact, P, sprev, dcat, bias, alog, dskipE, ng, name):
    B, S, _ = P.shape
    nc, xa_spec, z_specs, dt_spec, tok, st_spec = _ssd_specs(S, True)
    do_spec = pl.BlockSpec((None, CHUNK, SSM_W), lambda b, i: (b, nc - 1 - i, 1))
    E, ET, tri, triT = _ssd_consts()
    dt_out = pl.BlockSpec((None, CHUNK, 128), lambda b, i: (b, nc - 1 - i, 0))

    def body(xa_ref, z0, z1, z2, z3, dt_ref, sp_ref, do_ref, bias_ref, alog_ref, dsk_ref, ng_ref, E_ref, ET_ref, tri_ref,
             triT_ref, dxa_ref, dz_ref, ddt_ref, dbias_ref, dalog_ref, ddsk_ref, dng_ref, dstate):
        first = (pl.program_id(0) == 0) & (pl.program_id(1) == 0)

        @pl.when(first)
        def _():
            for ref in (dbias_ref, dalog_ref, ddsk_ref, dng_ref):
                ref[...] = jnp.zeros(ref.shape, F32)

        @pl.when(pl.program_id(1) == 0)
        def _():
            dstate[...] = jnp.zeros((128, SSM_W), F32)

        xa, ETm = xa_ref[...], ET_ref[...]
        c = _ssd_pre(xa, dt_ref[...], bias_ref[...], alog_ref[...], E_ref[...], tri_ref[...])
        Y, groups = _ssd_y(c, xa, sp_ref, dsk_ref[...])
        X, xdt = c["X"], c["xdt"]
        zv = jnp.concatenate([z0[...], z1[...], z2[...], z3[...]], axis=1)
        sg = _sigmoid(zv)
        zs = zv * sg
        yz = Y * zs
        dout = do_ref[...].astype(F32)
        dyz = []
        for g in range(SSM_G):
            gs = slice(g * 512, (g + 1) * 512)
            yg = yz[:, gs]
            r = lax.rsqrt(jnp.mean(yg * yg, axis=-1, keepdims=True) + EPS)
            yn = yg * r
            dng_ref[:, gs] += jnp.sum(dout[:, gs] * yn, axis=0, keepdims=True)
            dyn = dout[:, gs] * ng_ref[:, gs]
            dyz.append(r * (dyn - yn * jnp.mean(dyn * yn, axis=-1, keepdims=True)))
        dyz = jnp.concatenate(dyz, axis=1)
        dz_ref[...] = (dyz * Y * (sg * (1.0 + zv * (1.0 - sg)))).astype(BF16)
        dY = dyz * zs
        ddsk_ref[...] += jnp.sum(dY * X, axis=0, keepdims=True)
        dX = dY * dsk_ref[...]
        lane = lax.broadcasted_iota(jnp.int32, (128, 128), 1)
        sub = lax.broadcasted_iota(jnp.int32, (128, 128), 0)
        colform = jnp.zeros((128, 128), F32)
        rowform = jnp.zeros((128, 128), F32)
        dxdt, gacsE, dBC = [], [], []
        for g in range(SSM_G):
            gs = slice(g * 512, (g + 1) * 512)
            G = groups[g]
            Bb, Cb, CB, Sg = G["Bb"], G["Cb"], G["CB"], G["Sg"]
            dYg = dY[:, gs]
            dQ = (dYg * c["eE"][:, gs]).astype(BF16)
            dSn = dstate[:, gs]
            dSnb = dSn.astype(BF16)
            cd = c["cdE"][:, gs]
            dC = _dot_nt(dQ, Sg.astype(BF16))
            dSprev = _dot_tn(Cb, dQ) + dSn * cd
            t1 = jnp.broadcast_to(jnp.sum(dSn * Sg * cd, axis=0, keepdims=True), (8, 512))
            colform = colform + jnp.where(sub == 127, _dot01(t1, ETm[gs, :], 2)[0:1, :], 0.0)
            Zg = xdt[:, gs] * c["wE"][:, gs]
            dZ = _dot(Bb, dSnb)
            dB = _dot_nt(Zg.astype(BF16), dSnb)
            U = dZ * Zg
            ga = dYg * G["yoff"] - U
            ga = ga + jnp.where(lax.broadcasted_iota(jnp.int32, (128, 512), 0) == 127, jnp.sum(U, axis=0, keepdims=True), 0.0)
            gacsE.append(ga)
            dxg = [None] * 4
            dCB = jnp.zeros((128, 128), F32)
            for j in range(4):
                h0 = g * 8 + 2 * j
                L0, L1, M, BD = G["pairs"][j]
                dYp = dYg[:, j * 128:(j + 1) * 128].astype(BF16)
                dM = _dot_nt(dYp, BD)
                dBD = _dot_tn(M, dYp)
                dxg[j] = jnp.where(lane < 64, dBD[:128], dBD[128:])
                for t, (h, L) in enumerate(((h0, L0), (h0 + 1, L1))):
                    dMh = dM[:, t * 128:(t + 1) * 128]
                    dCB = dCB + dMh * L
                    Gh = dMh * CB * L
                    colform = colform + jnp.where(lane == h, jnp.sum(Gh, axis=1, keepdims=True), 0.0)
                    rowform = rowform - jnp.where(sub == h, jnp.sum(Gh, axis=0, keepdims=True), 0.0)
            dCBb = dCB.astype(BF16)
            dC = dC + _dot(dCBb, Bb)
            dB = dB + _dot_tn(dCBb, Cb)
            dxdt.append(jnp.concatenate(dxg, axis=1) + dZ * c["wE"][:, gs])
            dBC.append((dB, dC))
            dstate[:, gs] = dSprev
        dxdt = jnp.concatenate(dxdt, axis=1)
        dX = dX + dxdt * c["dtE"]
        ddt = _dot01(dxdt * X, ETm, 2)
        dacs = colform + rowform.T + _dot01(jnp.concatenate(gacsE, axis=1), ETm, 2)
        dda = _dot01_left(triT_ref[...], dacs, 2)
        ddt = ddt + dda * c["a"]
        dalog_ref[...] += jnp.sum(dda * c["dtp"], axis=0, keepdims=True) * c["a"]
        ddtraw = jnp.where(lane < SSM_H, ddt * _sigmoid(c["pre"]), 0.0)
        dbias_ref[...] += jnp.sum(ddtraw, axis=0, keepdims=True)
        ddt_ref[...] = ddtraw.astype(BF16)
        dxa_ref[...] = jnp.concatenate([dX, dBC[0][0], dBC[1][0], dBC[0][1], dBC[1][1]], axis=1)

    p128, p1k = _full((1, 128)), _full((1, SSM_W))
    return pl.pallas_call(
        body, name=name, grid=(B, nc),
        in_specs=[xa_spec] + z_specs + [dt_spec, st_spec, do_spec, p128, p128, p1k, p1k,
                                        _full((128, SSM_W)), _full((SSM_W, 128)), _full((128, 128)), _full((128, 128))],
        out_specs=[xa_spec, tok, dt_out, p128, p128, p1k, p1k],
        out_shape=[jax.ShapeDtypeStruct((B, S, CONV_CH), F32), jax.ShapeDtypeStruct((B, S, SSM_W), BF16),
                   jax.ShapeDtypeStruct((B, S, 128), BF16), jax.ShapeDtypeStruct((1, 128), F32),
                   jax.ShapeDtypeStruct((1, 128), F32), jax.ShapeDtypeStruct((1, SSM_W), F32),
                   jax.ShapeDtypeStruct((1, SSM_W), F32)],
        scratch_shapes=[pltpu.VMEM((128, SSM_W), F32)],
        compiler_params=pltpu.CompilerParams(dimension_semantics=("arbitrary", "arbitrary")),
    )(xact, P, P, P, P, P, sprev, dcat, bias, alog, dskipE, ng, E, ET, tri, triT)


def _adamw(w, parts, m, v, name, tr=512, layer=0, prev=None):
    Ltot, R, C = w.shape
    ns = parts.shape[0]
    tr = min(tr, R)
    assert R % tr == 0 and parts.shape[1:] == (R, C)
    c1 = 1.0 / (1.0 - ADAM_B1 ** ADAM_STEP)
    c2 = 1.0 / (1.0 - ADAM_B2 ** ADAM_STEP)

    def body(w_ref, p_ref, m_ref, v_ref, *rest):
        g_ref, d_ref, mo_ref, vo_ref = rest[-4:]
        g = p_ref[0].astype(F32)
        for s in range(1, ns):
            g = g + p_ref[s].astype(F32)
        mn = ADAM_B1 * m_ref[...] + (1.0 - ADAM_B1) * g
        vn = ADAM_B2 * v_ref[...] + (1.0 - ADAM_B2) * (g * g)
        g_ref[...] = g
        mo_ref[...] = mn
        vo_ref[...] = vn
        d_ref[...] = -ADAM_LR * ((mn * c1) / (jnp.sqrt(vn * c2) + ADAM_EPS) + ADAM_WD * w_ref[...])

    blk = pl.BlockSpec((None, tr, C), lambda i: (layer, i, 0))
    extra = [] if prev is None else list(prev)
    return pl.pallas_call(
        body, name=name, grid=(R // tr,),
        in_specs=[blk, pl.BlockSpec((ns, tr, C), lambda i: (0, i, 0)), blk, blk] + [pl.BlockSpec(memory_space=pl.ANY)] * len(extra),
        out_specs=[blk] * 4, out_shape=[jax.ShapeDtypeStruct((Ltot, R, C), F32)] * 4,
        input_output_aliases={4 + k: k for k in range(len(extra))})(w, parts, m, v, *extra)


_SMALL = ("ada_b", "norm1_g", "gm_ln_g", "gm_ln_b", "gm_ws", "gm_bs", "gm_norm_g", "attn_sinks", "attn_norm_g", "conv_b",
          "dt_bias", "a_log", "d_skip", "ssm_norm_g", "norm2_g", "final_norm_g")


def _pack(arrs):
    flat = []
    for a in arrs:
        f = a.reshape(-1).astype(F32)
        flat.append(jnp.pad(f, (0, (-f.shape[0]) % 1024)))
    return jnp.concatenate(flat).reshape(-1, 128)


def _unpack(pack, like):
    out, r = [], 0
    for a in like:
        n = int(np.prod(a.shape))
        rows = (n + 1023) // 1024 * 8
        out.append(lax.slice(pack, (r, 0), (r + rows, 128)).reshape(-1)[:n].reshape(a.shape))
        r += rows
    return out


def kernel(x, c, ada_w, ada_b, norm1_g, w_in, gm_ln_g, gm_ln_b, gm_ws, gm_bs, gm_norm_g, attn_sinks, attn_norm_g, conv_w, conv_b, dt_bias, a_log, d_skip, ssm_norm_g, w_out, norm2_g, w_mlp1, w_mlp2, final_norm_g, loss_target, m_ada_w, m_ada_b, m_norm1_g, m_w_in, m_gm_ln_g, m_gm_ln_b, m_gm_ws, m_gm_bs, m_gm_norm_g, m_attn_sinks, m_attn_norm_g, m_conv_w, m_conv_b, m_dt_bias, m_a_log, m_d_skip, m_ssm_norm_g, m_w_out, m_norm2_g, m_w_mlp1, m_w_mlp2, m_final_norm_g, v_ada_w, v_ada_b, v_norm1_g, v_w_in, v_gm_ln_g, v_gm_ln_b, v_gm_ws, v_gm_bs, v_gm_norm_g, v_attn_sinks, v_attn_norm_g, v_conv_w, v_conv_b, v_dt_bias, v_a_log, v_d_skip, v_ssm_norm_g, v_w_out, v_norm2_g, v_w_mlp1, v_w_mlp2, v_final_norm_g):
    args = dict(locals())
    B, S, _ = x.shape
    T = B * S
    L = DEPTH
    me = 4 * lax.axis_index("x") + 2 * lax.axis_index("y") + lax.axis_index("c")

    gath = _gather2([c, conv_w], "ag_c")
    big = ("w_in", "w_out", "w_mlp1", "w_mlp2")
    chain = [(n, l) for l in range(L) for n in ("w_in", "w_mlp1", "w_out", "w_mlp2")]
    inflight = {}

    def start_next(order):
        if not chain:
            return jnp.zeros((8, 128), F32)
        n, l = chain.pop(0)
        sems, land_thru, token = _gather_start(zone[n, l], order, f"ag_start_{n}{l}")
        inflight[n, l] = (sems, land_thru)
        return token

    def gathered(n, l, after):
        land = _gather_wait(*inflight.pop((n, l)), after, f"ag_wait_{n}{l}")
        return _gather_finish(land, f"ag_fin_{n}{l}")

    forwarding = {}

    def arrived(n, l, after):
        land = _gather_wait(*inflight.pop((n, l)), after, f"ag_wait_{n}{l}")
        sems, land_thru, token = _forward_start(land, after, f"ag_fwd_start_{n}{l}")
        forwarding[n, l] = (sems, land_thru)
        return token

    def ready(n, l, after):
        return _forward_wait(*forwarding.pop((n, l)), after, f"ag_fwd_wait_{n}{l}")

    me1 = me.astype(jnp.int32).reshape(1)
    zone = {(n, l): _landing_zone(args[n], l, me1, f"ag_zone_{n}{l}") for n, l in chain}
    later_zones = [zone[k] for k in chain[1:]]

    tok = start_next(gath[0])
    c_all = gath[0].reshape(NDEV * B, D) + tok[0, 0]
    c_act = (c_all * jax.nn.sigmoid(c_all)).astype(BF16)
    nb_rows = c_act.shape[0]
    c_pad = jnp.pad(c_act, ((0, 128 - nb_rows), (0, 0)))
    adw = ada_w.astype(BF16)
    mod_part = jnp.stack([_mm(c_pad, adw[l], mode="nn", name=f"mod{l}", tn=768)[:nb_rows] for l in range(L)])
    mod_all = _gather_small([mod_part], "ag_mod", order=later_zones)[0]
    mod_mine = lax.dynamic_slice_in_dim(mod_all, me * B, B, axis=2)
    mod = jnp.transpose(mod_mine, (1, 2, 0, 3)).reshape(L, B, 6 * D) + ada_b[:, None, :]
    mods = [[mod[l][:, None, i * D:(i + 1) * D] for i in range(6)] for l in range(L)]

    win_g, wout_g, w1_g, w2_g = [None] * L, [None] * L, [None] * L, [None] * L

    tril = jnp.tril(jnp.ones((128, 128), F32))
    row = lambda a: a.reshape(1, -1)
    pad128 = lambda a: jnp.pad(a.reshape(1, -1), ((0, 0), (0, 128 - a.shape[-1])))
    small = []
    for l in range(L):
        wt = gm_ws[l] * tril
        small.append(dict(
            lng=row(gm_ln_g[l]), lnb=row(gm_ln_b[l]), wt=wt.astype(BF16), wtT=jnp.swapaxes(wt, 1, 2).astype(BF16),
            bsx=jnp.repeat(gm_bs[l].T, 128, axis=1), gog=row(gm_norm_g[l]), sinks=attn_sinks[l], aog=row(attn_norm_g[l]),
            bias=pad128(dt_bias[l]), alog=pad128(a_log[l]), dskE=jnp.repeat(d_skip[l], SSM_HD).reshape(1, SSM_W),
            sng=row(ssm_norm_g[l]), cb=row(conv_b[l])))
    convw_all = jnp.transpose(gath[1], (1, 2, 0, 3)).reshape(L, 4, CONV_CH)
    convw8 = jnp.pad(convw_all, ((0, 0), (0, 4), (0, 0)))

    saved = []
    xl = x
    g_in = gathered("w_in", 0, mod)
    tok = start_next(g_in)
    h = _norm_fwd(xl, row(norm1_g[0]) + tok[0, 0], mods[0][1], mods[0][0], "norm1_f0")
    for l in range(L):
        sm = small[l]
        win_g[l] = _shards_to_cols(g_in, f"w_in_cols{l}")
        P = _mm(h.reshape(T, D), win_g[l], mode="nn", name=f"proj_in{l}", tn=1536, order=tok).reshape(B, S, PW)
        cat = _gmlp_fwd(P, sm["lng"], sm["lnb"], sm["wt"], sm["bsx"], sm["gog"], f"gmlp_f{l}")
        cat = _attn_fwd(P, sm["sinks"], sm["aog"], cat, f"attn_f{l}")
        xact = _conv_fwd(P, convw8[l], sm["cb"], f"conv_f{l}")
        tok = start_next(arrived("w_mlp1", l, xact))
        cat, sprev = _ssd_fwd(xact, P, sm["bias"], sm["alog"], sm["dskE"], sm["sng"] + tok[0:1, 0:1], cat, f"ssd_f{l}")
        g_out = gathered("w_out", l, cat)
        tok = start_next(g_out)
        wout_g[l] = g_out.reshape(D, D)
        mix = _mm(cat.reshape(T, D), wout_g[l], mode="nn", name=f"proj_out{l}", order=tok).reshape(B, S, D)
        x_mid, h2 = _norm_fwd(xl, row(norm2_g[l]), mods[l][4], mods[l][3], f"norm2_f{l}", resid=(mix, mods[l][2]))
        w1_g[l] = ready("w_mlp1", l, h2)
        a_act, r_act = _mm(h2.reshape(T, D), w1_g[l], mode="nn", name=f"mlp1_{l}", out_dtypes=(BF16, BF16), col_blocked_b=True,
                           epilogue=lambda acc: (acc, jnp.square(jnp.maximum(acc, 0.0))))
        g_2 = gathered("w_mlp2", l, r_act)
        tok = start_next(g_2)
        w2_g[l] = g_2.reshape(DFF, D)
        m2 = _mm(r_act, w2_g[l], mode="nn", name=f"mlp2_{l}", order=tok, tk=4096).reshape(B, S, D)
        saved.append(dict(x_in=xl, h=h, P=P, xact=xact, sprev=sprev, cat=cat, mix=mix, x_mid=x_mid, h2=h2, a=a_act, r=r_act, m2=m2))
        if l + 1 < L:
            tok = start_next(arrived("w_in", l + 1, m2))
            xl, h = _norm_fwd(x_mid, row(norm1_g[l + 1]) + tok[0, 0], mods[l + 1][1], mods[l + 1][0], f"norm1_f{l + 1}",
                              resid=(m2, mods[l][5]))
            g_in = ready("w_in", l + 1, h)

    sv = saved[L - 1]
    nb = _norm_bwd(sv["x_mid"], row(final_norm_g), "final_b", tgt=loss_target, br=sv["m2"], gate=mods[L - 1][5], x_is_prev=True)
    loss_part, g_final = nb["loss"], nb["dg"]
    dmod, gsm, gconvw = [None] * L, [None] * L, [None] * L
    core = lax.axis_index("c").astype(jnp.int32).reshape(1)
    reducing = []

    def reduce_start(n, l, sent, after):
        p, from_sib = _pair_wait(*sent[:3], after, f"rs_pair_wait_{n}{l}")
        s, land = _pair_add(p, from_sib, core, f"rs_add_{n}{l}")
        return reduce_exchange(n, l, s, land, after)

    def reduce_exchange(n, l, s, land, order):
        sems, s_thru, land_thru, token = _chipsum_start(s, land, order, f"rs_start_{n}{l}")
        reducing.append((n, l, sems, s_thru, land_thru))
        return token

    other = 1 - core

    for l in reversed(range(L)):
        sv, sm = saved[l], small[l]
        dm2, dxo, dg2 = nb["dbr"].reshape(T, D), nb["dx"], nb["dgate"]
        da = _mm(dm2, w2_g[l], mode="nt", name=f"mlp2_dx{l}", out_dtypes=(BF16,), extras=(sv["a"],),
                 epilogue=lambda acc, a: (acc * (2.0 * jnp.maximum(a.astype(F32), 0.0)),))
        h2f = sv["h2"].reshape(T, D)
        sent2 = _sibling_start(_dw_half(sv["r"], dm2, other, axis="m", name=f"mlp2_dw_sib{l}"), da, f"rs_sib_start_w_mlp2{l}")
        dh2 = _mm(da, w1_g[l], mode="nt", name=f"mlp1_dx{l}", col_blocked_b=True, order=sent2[3],
                  out_dtypes=(BF16,)).reshape(B, S, D)
        from_sib = _sibling_wait(*sent2[:3], dh2, f"rs_sib_wait_w_mlp2{l}")[1]
        sent1 = _sibling_start(_dw_half(h2f, da, other, axis="n", name=f"mlp1_dw_sib{l}", order=from_sib), da,
                               f"rs_sib_start_w_mlp1{l}")
        s2, land2 = _dw_half(sv["r"], dm2, core, axis="m", name=f"mlp2_dw_own{l}", add=from_sib, order=sent1[3])
        tok = reduce_exchange("w_mlp2", l, s2, land2, da)
        nb2 = _norm_bwd(sv["x_mid"], row(norm2_g[l]) + tok[0, 0], f"norm2_b{l}", sc=mods[l][4], dh=dh2, dres=dxo, br=sv["mix"],
                        gate=mods[l][2])
        dmix = nb2["dbr"].reshape(T, D)
        from_sib = _sibling_wait(*sent1[:3], dmix, f"rs_sib_wait_w_mlp1{l}")[1]
        s1, land1 = _dw_half(h2f, da, core, axis="n", name=f"mlp1_dw_own{l}", add=from_sib)
        tok = reduce_exchange("w_mlp1", l, s1, land1, dmix)
        dcat = _mm(dmix, wout_g[l], mode="nt", name=f"proj_out_dx{l}", order=tok, out_dtypes=(BF16,)).reshape(B, S, D)
        du, dv, dlng, dlnb, dws, dbsx, dgog = _gmlp_bwd(sv["P"], dcat, sm["lng"], sm["lnb"], sm["wt"], sm["wtT"], sm["bsx"],
                                                        sm["gog"], f"gmlp_b{l}")
        dq, dk, dvv, dsink, daog = _attn_bwd(sv["P"], dcat, sm["sinks"], sm["aog"], f"attn_b{l}")
        dwo = _mm(sv["cat"].reshape(T, D), dmix, mode="tn", name=f"proj_out_dw{l}", out_dtypes=(BF16,), tk=2048,
                  order=dq).reshape(4, 2, D // NDEV, D)
        sent = _pair_start(dwo, dmix, f"rs_pair_start_w_out{l}")
        dxa, dz, ddt, dbias, dalog, ddsk, dsng = _ssd_bwd(sv["xact"], sv["P"], sv["sprev"], dcat, sm["bias"], sm["alog"],
                                                          sm["dskE"], sm["sng"] + sent[3][0:1, 0:1], f"ssd_b{l}")
        tok = reduce_start("w_out", l, sent, dxa)
        dxbc, dcw, dcb = _conv_bwd(sv["P"], dxa, convw8[l], sm["cb"] + tok[0:1, 0:1], f"conv_b{l}")
        dP = _concat_cols([du, dv, dq, dk, dvv, dz, dxbc, ddt], PW, f"dproj_cols{l}").reshape(T, PW)
        dwin = _mm(sv["h"].reshape(T, D), dP, mode="tn", name=f"proj_in_dw{l}", out_dtypes=(BF16,), tn=1536, tk=2048)
        sent = _sibling_start(dwin, dP, f"rs_sib_start_w_in{l}")
        dh = _mm(dP, win_g[l], mode="nt", name=f"proj_in_dx{l}", tk=2304, order=sent[3], out_dtypes=(BF16,)).reshape(B, S, D)
        s_in, land_in = _cols_to_my_shards(*_sibling_wait(*sent[:3], dh, f"rs_sib_wait_w_in{l}"), core, f"w_in_dshards{l}")
        tok = reduce_exchange("w_in", l, s_in, land_in, dh)
        nb = _norm_bwd(sv["x_in"], row(norm1_g[l]) + tok[0, 0], f"norm1_b{l}", sc=mods[l][1], dh=dh, dres=nb2["dx"],
                       br=saved[l - 1]["m2"] if l > 0 else None, gate=mods[l - 1][5] if l > 0 else None)
        dmod[l] = jnp.concatenate([nb["dsh"], nb["dsc"], nb2["dgate"], nb2["dsh"], nb2["dsc"], dg2], axis=-1)
        gconvw[l] = dcw[:4]
        gsm[l] = dict(
            ada_b=jnp.sum(dmod[l], axis=(0, 1)), norm1_g=nb["dg"], gm_ln_g=dlng, gm_ln_b=dlnb, gm_ws=dws,
            gm_bs=dbsx.reshape(128, GM_H, 128).sum(-1).T, gm_norm_g=dgog, attn_sinks=dsink[:, 0], attn_norm_g=daog,
            conv_b=dcb, dt_bias=dbias[0, :SSM_H], a_log=dalog[0, :SSM_H], d_skip=ddsk.reshape(SSM_H, SSM_HD).sum(-1),
            ssm_norm_g=dsng, norm2_g=nb2["dg"])
    grad_x = nb["dx"]

    big_res, after = dict.fromkeys(big), grad_x
    tile_rows = dict(w_in=256, w_out=256, w_mlp1=256, w_mlp2=128)

    def finish_reduce(n, l, sems, s_thru, land_thru, after):
        parts = _chipsum_wait(sems, s_thru, land_thru, after, f"rs_wait_{n}{l}")
        big_res[n] = _adamw(args[n], parts, args["m_" + n], args["v_" + n], f"adamw_{n}{l}", tr=tile_rows[n], layer=l,
                            prev=big_res[n])
        return big_res[n][0]

    per_layer = [n for n in _SMALL if n != "final_norm_g"]
    g_small = [jnp.stack([gsm[l][n].reshape(args[n].shape[1:]) for l in range(L)]) for n in per_layer] + [g_final.reshape(D)]
    zc = jnp.zeros((L, 4, CONV_CH), F32)
    z1 = jnp.zeros((1, 128), F32)
    gpack = _pack([loss_part] + g_small + [jnp.stack(gconvw)])
    small_zones = [_landing_zone(jnp.stack(dmod).reshape(1, L * B, 6 * D), 0, me1, "ag_zone_dmod", dtype=F32),
                   _landing_zone(gpack[None], 0, me1, "ag_zone_small", tr=gpack.shape[0], dtype=F32)]
    small_sems, small_thru, after = _gather_small_start(small_zones, grad_x, "ag_small_start")

    for item in reducing[:-1]:
        after = finish_reduce(*item, after)

    got = _gather_small_wait(small_sems, small_thru, after, "ag_small_wait")
    got = [got[0].reshape(NDEV, L, B, 6 * D), got[1]]
    like = [z1] + [args[n] for n in _SMALL] + [zc]
    packs = [_pack([z1] + [args[p + n] for n in _SMALL] + [zc]) for p in ("", "m_", "v_")]
    sres = [_unpack(p[0], like) for p in _adamw(packs[0][None], got[1], packs[1][None], packs[2][None], "adamw_small",
                                                tr=gpack.shape[0])]
    res = {n: [r[1 + i] for r in sres] for i, n in enumerate(_SMALL)}
    loss = sres[0][0][0, 0]
    gcw = lax.dynamic_slice_in_dim(sres[0][-1], me * (CONV_CH // NDEV), CONV_CH // NDEV, axis=2)

    def update(name, grads, tr):
        r = None
        for l, g in enumerate(grads):
            r = _adamw(args[name], g[None], args["m_" + name], args["v_" + name], f"adamw_{name}{l}", tr=tr, layer=l, prev=r)
        res[name] = r

    update("conv_w", [gcw[l] for l in range(L)], 4)

    dmod_all = jnp.transpose(got[0], (1, 0, 2, 3)).reshape(L, NDEV * B, 6 * D)
    dm_mine = lax.dynamic_slice_in_dim(dmod_all, me * (6 * D // NDEV), 6 * D // NDEV, axis=2)
    dm_pad = jnp.pad(dm_mine, ((0, 0), (0, 128 - nb_rows), (0, 0))).astype(BF16)
    update("ada_w", [_mm(c_pad, dm_pad[l], mode="tn", name=f"ada_dw{l}", tn=768) for l in range(L)], 256)

    finish_reduce(*reducing[-1], res["ada_w"][0])
    for n in big:
        res[n] = [a.reshape(args[n].shape) for a in big_res[n]]

    names = ['ada_w', 'ada_b', 'norm1_g', 'w_in', 'gm_ln_g', 'gm_ln_b', 'gm_ws', 'gm_bs', 'gm_norm_g', 'attn_sinks',
             'attn_norm_g', 'conv_w', 'conv_b', 'dt_bias', 'a_log', 'd_skip', 'ssm_norm_g', 'w_out', 'norm2_g', 'w_mlp1',
             'w_mlp2', 'final_norm_g']
    return (loss, grad_x, *[res[n][0] for n in names], *[res[n][1] for n in names], *[res[n][2] for n in names],
            *[res[n][3] for n in names])
```

```python
import jax
import jax.numpy as jnp
import numpy as np
from jax import lax
from jax.experimental import pallas as pl
from jax.experimental.pallas import tpu as pltpu

F32, BF16 = jnp.float32, jnp.bfloat16
MESH = pl.DeviceIdType.MESH
NDEV = 8

D = 2048
DEPTH = 2
CHUNK = 128
GM_W, GM_H = 512, 4
ATT_W, KV_W = 512, 128
SSM_W, SSM_H, SSM_HD, SSM_G = 1024, 16, 64, 2
CONV_CH = 1536
IN_W = 4368
DFF = 8192
EPS = 1e-6
NEG_INF = -1e30
GELU_K = 0.7978845608028654
GELU_C = 0.044715

_ORIG = (("u", 512), ("v", 512), ("q", 512), ("k", 128), ("vv", 128), ("z", 1024), ("xbc", 1536), ("dt", 16))
OFF = dict(u=0, v=512, q=1024, k=1536, vv=1664, z=1792, xbc=2816, dt=4352)
PW = 4608

ADAM_LR, ADAM_B1, ADAM_B2, ADAM_EPS, ADAM_WD, ADAM_STEP = 0.001, 0.9, 0.999, 1e-08, 0.01, 10


def _concat_cols(pieces, width, name, ts=256):
    B, S, _ = pieces[0].shape
    ws = [p.shape[-1] for p in pieces]
    dt = pieces[0].dtype
    n = len(pieces)

    def body(*refs):
        cols = [r[...] for r in refs[:n]]
        if width > sum(ws):
            cols.append(jnp.zeros((ts, width - sum(ws)), dt))
        refs[n][...] = jnp.concatenate(cols, axis=1)

    return pl.pallas_call(
        body, name=name, grid=(B, S // ts), in_specs=[pl.BlockSpec((None, ts, w), lambda b, i: (b, i, 0)) for w in ws],
        out_specs=pl.BlockSpec((None, ts, width), lambda b, i: (b, i, 0)), out_shape=jax.ShapeDtypeStruct((B, S, width), dt))(*pieces)


def _shards_to_cols(g, name, tr=256):
    n, R, C = g.shape

    def body(g_ref, o_ref):
        o_ref[...] = jnp.concatenate([g_ref[s] for s in range(n)] + [jnp.zeros((tr, PW - n * C), g.dtype)], axis=1)

    return pl.pallas_call(body, name=name, grid=(R // tr,), in_specs=[pl.BlockSpec((n, tr, C), lambda i: (0, i, 0))],
                          out_specs=pl.BlockSpec((tr, PW), lambda i: (i, 0)), out_shape=jax.ShapeDtypeStruct((R, PW), g.dtype))(g)


def _cols_to_my_shards(w, w_sib, core, name, tr=256):
    R, C = w.shape[0], IN_W // NDEV

    def body(core_ref, w_ref, s_ref, o_ref, o2_ref):
        x = w_ref[...].astype(F32) + s_ref[...].astype(F32)
        mine_is_odd = core_ref[0] == 1
        for q in range(4):
            blk = jnp.where(mine_is_odd, x[:, C * (2 * q + 1):C * (2 * q + 2)], x[:, C * 2 * q:C * (2 * q + 1)]).astype(o_ref.dtype)
            o_ref[q] = blk
            o2_ref[q] = blk

    row = pl.BlockSpec((tr, PW), lambda i, c: (i, 0))
    out = pl.BlockSpec((4, tr, C), lambda i, c: (0, i, 0))
    return pl.pallas_call(
        body, name=name, out_shape=[jax.ShapeDtypeStruct((4, R, C), w.dtype)] * 2,
        grid_spec=pltpu.PrefetchScalarGridSpec(num_scalar_prefetch=1, grid=(R // tr,), in_specs=[row, row], out_specs=[out, out]),
    )(core, w, w_sib)


def _sigmoid(x):
    return 0.5 * (jnp.tanh(0.5 * x) + 1.0)


def _gelu(x):
    return 0.5 * x * (1.0 + jnp.tanh(GELU_K * (x + GELU_C * x * x * x)))


def _gelu_grad(x):
    t = jnp.tanh(GELU_K * (x + GELU_C * x * x * x))
    return 0.5 * (1.0 + t) + 0.5 * x * (1.0 - t * t) * GELU_K * (1.0 + 3.0 * GELU_C * x * x)


def _dot(a, b, prec=None):
    return jnp.dot(a, b, precision=prec, preferred_element_type=F32)


def _dot_nt(a, b, prec=None):
    return lax.dot_general(a, b, (((1,), (1,)), ((), ())), precision=prec, preferred_element_type=F32)


def _dot_tn(a, b, prec=None):
    return lax.dot_general(a, b, (((0,), (0,)), ((), ())), precision=prec, preferred_element_type=F32)


def _full(shape):
    return pl.BlockSpec(shape, lambda *_: (0,) * len(shape))


_HBM = pl.BlockSpec(memory_space=pltpu.HBM)


def _me():
    return lax.axis_index("x"), lax.axis_index("y"), lax.axis_index("c")


def _peer(k):
    x, y, c = _me()
    px = 1 - x if k & 4 else x
    py = 1 - y if k & 2 else y
    pc = 1 - c if k & 1 else c
    return (px, py, pc), 4 * px + 2 * py + pc


def _gather_small(xs, name, order=()):
    n = len(xs)

    def body(*refs):
        ins, outs = refs[:n], refs[-n - 3:-3]
        send, recv, loc = refs[-3:]
        x, y, c = _me()
        me = 4 * x + 2 * y + c
        started = []
        for i in range(n):
            own = pltpu.make_async_copy(ins[i], outs[i].at[me], loc.at[i])
            own.start()
            started.append(own)
        for k in range(1, NDEV):
            dev, lin = _peer(k)
            for i in range(n):
                pltpu.make_async_remote_copy(
                    src_ref=ins[i], dst_ref=outs[i].at[me],
                    send_sem=send.at[i, k - 1], recv_sem=recv.at[i, k - 1], device_id=dev, device_id_type=MESH).start()
        for k in range(1, NDEV):
            dev, lin = _peer(k)
            for i in range(n):
                pltpu.make_async_remote_copy(
                    src_ref=ins[i], dst_ref=outs[i].at[lin],
                    send_sem=send.at[i, k - 1], recv_sem=recv.at[i, k - 1], device_id=dev, device_id_type=MESH).wait()
        for own in started:
            own.wait()

    extra = list(order)
    return pl.pallas_call(
        body, name=name, out_shape=[jax.ShapeDtypeStruct((NDEV,) + a.shape, a.dtype) for a in xs],
        in_specs=[_HBM] * n + [pl.BlockSpec(memory_space=pl.ANY)] * len(extra), out_specs=[_HBM] * n,
        scratch_shapes=[pltpu.SemaphoreType.DMA((n, NDEV - 1)), pltpu.SemaphoreType.DMA((n, NDEV - 1)),
                        pltpu.SemaphoreType.DMA((n,))],
        compiler_params=pltpu.CompilerParams(has_side_effects=True),
    )(*xs, *extra)


def _gather_small_start(lands, order, name):
    n = len(lands)

    def body(*refs):
        ins, sems, token = refs[:n], refs[n + 1:n + 1 + 14 * n], refs[-1]
        x, y, c = _me()
        me = 4 * x + 2 * y + c
        for i in range(n):
            for k in range(1, NDEV):
                dev, _ = _peer(k)
                pltpu.make_async_remote_copy(src_ref=ins[i].at[me], dst_ref=ins[i].at[me], send_sem=sems[14 * i + k - 1],
                                             recv_sem=sems[14 * i + 7 + k - 1], device_id=dev, device_id_type=MESH).start()
        token[...] = jnp.zeros_like(token)

    outs = pl.pallas_call(
        body, name=name,
        out_shape=(pltpu.SemaphoreType.DMA(()),) * (14 * n) + tuple(pltpu.HBM(a.shape, a.dtype) for a in lands)
        + (jax.ShapeDtypeStruct((8, 128), F32),),
        in_specs=(_HBM,) * n + (_ANY,), out_specs=(_SEM,) * (14 * n) + (_HBM,) * n + (pl.BlockSpec(memory_space=pltpu.VMEM),),
        input_output_aliases={i: 14 * n + i for i in range(n)}, compiler_params=pltpu.CompilerParams(has_side_effects=_DATAFLOW),
    )(*[_hbm(a) for a in lands], order)
    return outs[:14 * n], outs[14 * n:15 * n], outs[-1]


def _gather_small_wait(sems, lands_thru, after, name):
    n = len(lands_thru)

    def body(*refs):
        ins, sems_ = refs[:n], refs[n:n + 14 * n]
        x, y, c = _me()
        me = 4 * x + 2 * y + c
        for i in range(n):
            for k in range(1, NDEV):
                dev, lin = _peer(k)
                cp = pltpu.make_async_remote_copy(src_ref=ins[i].at[me], dst_ref=ins[i].at[lin], send_sem=sems_[14 * i + k - 1],
                                                  recv_sem=sems_[14 * i + 7 + k - 1], device_id=dev, device_id_type=MESH)
                cp.wait_send()
                cp.wait_recv()

    outs = pl.pallas_call(
        body, name=name, out_shape=tuple(pltpu.HBM(a.shape, a.dtype) for a in lands_thru),
        in_specs=(_HBM,) * n + (_SEM,) * (14 * n) + (_ANY,), out_specs=(_HBM,) * n,
        input_output_aliases={i: i for i in range(n)}, compiler_params=pltpu.CompilerParams(has_side_effects=_DATAFLOW),
    )(*lands_thru, *sems, after)
    return outs


def _chips():
    x, y, c = _me()
    return x, y, c, [(1 - x, y), (x, 1 - y), (1 - x, 1 - y)]


def _gather2(xs, name, order=None):
    n = len(xs)
    extra = [] if order is None else [order]

    def body(*refs):
        ins, outs = refs[:n], refs[-n - 3:-3]
        send, recv, loc = refs[-3:]
        x, y, c, chips = _chips()
        me, sib = (x, y, c), (x, y, 1 - c)

        def cp(i, k, block, to, src=None):
            slot = outs[i].at[4 * block[0] + 2 * block[1] + block[2]]
            return pltpu.make_async_remote_copy(src_ref=slot if src is None else src, dst_ref=slot, send_sem=send.at[i, k],
                                                recv_sem=recv.at[i, k], device_id=to, device_id_type=MESH)

        sent = []
        for i in range(n):
            for j, chip in enumerate(chips):
                sent.append(cp(i, 1 + j, me, (*chip, c), src=ins[i]))
            sent.append(cp(i, 0, me, sib, src=ins[i]))
        for s in sent:
            s.start()
        own = [pltpu.make_async_copy(ins[i], outs[i].at[4 * x + 2 * y + c], loc.at[i]) for i in range(n)]
        for o in own:
            o.start()
        for j, chip in enumerate(chips):
            for i in range(n):
                cp(i, 1 + j, (*chip, c), me).wait_recv()
                fwd = cp(i, 4 + j, (*chip, c), sib)
                fwd.start()
                sent.append(fwd)
        for i in range(n):
            cp(i, 0, sib, me).wait_recv()
            for j, chip in enumerate(chips):
                cp(i, 4 + j, (*chip, 1 - c), me).wait_recv()
        for s in sent:
            s.wait_send()
        for o in own:
            o.wait()

    return pl.pallas_call(
        body, name=name, out_shape=[jax.ShapeDtypeStruct((NDEV,) + a.shape, a.dtype) for a in xs],
        in_specs=[_HBM] * n + [pl.BlockSpec(memory_space=pl.ANY)] * len(extra), out_specs=[_HBM] * n,
        scratch_shapes=[pltpu.SemaphoreType.DMA((n, 7)), pltpu.SemaphoreType.DMA((n, 7)), pltpu.SemaphoreType.DMA((n,))],
        compiler_params=pltpu.CompilerParams(has_side_effects=True),
    )(*xs, *extra)


def _pair_add(p, r1, core, name, tr=256):
    _, _, R, C = p.shape
    tr = min(tr, R)

    def body(core_ref, p_ref, r_ref, o_ref, o2_ref):
        s = (p_ref[...].astype(F32) + r_ref[...].astype(F32)).astype(o_ref.dtype)
        o_ref[...] = s
        o2_ref[...] = s

    blk = pl.BlockSpec((None, tr, C), lambda ch, i, core_ref: (ch, i, 0))
    return pl.pallas_call(
        body, name=name, out_shape=[jax.ShapeDtypeStruct((4, R, C), p.dtype)] * 2,
        grid_spec=pltpu.PrefetchScalarGridSpec(
            num_scalar_prefetch=1, grid=(4, R // tr),
            in_specs=[pl.BlockSpec((None, None, tr, C), lambda ch, i, core_ref: (ch, core_ref[0], i, 0)), blk],
            out_specs=[blk, blk]),
    )(core, p, r1)


_SEM = pl.BlockSpec(memory_space=pltpu.SEMAPHORE)
_ANY = pl.BlockSpec(memory_space=pl.ANY)
_DATAFLOW = pltpu.SideEffectType.DATAFLOW_SIDE_EFFECTING


def _hbm(a):
    return pltpu.with_memory_space_constraint(a, pltpu.HBM)


def _gather_targets():
    x, y, c, chips = _chips()
    return 4 * x + 2 * y + c, [(x, y, 1 - c)] + [(*chip, c) for chip in chips]


def _landing_zone(w, l, me, name, tr=512, dtype=BF16):
    _, R, C = w.shape
    tr = min(tr, R)

    def body(me_ref, w_ref, o_ref):
        o_ref[...] = w_ref[...].astype(dtype)

    return pl.pallas_call(
        body, name=name, out_shape=jax.ShapeDtypeStruct((NDEV, R, C), dtype),
        grid_spec=pltpu.PrefetchScalarGridSpec(
            num_scalar_prefetch=1, grid=(R // tr,), in_specs=[pl.BlockSpec((None, tr, C), lambda i, me_ref: (l, i, 0))],
            out_specs=pl.BlockSpec((None, tr, C), lambda i, me_ref: (me_ref[0], i, 0))),
    )(me, w)


def _gather_start(land, order, name):
    def body(land_ref, order_ref, *rest):
        sems, token = rest[:8], rest[9]
        me, targets = _gather_targets()
        for k, to in enumerate(targets):
            pltpu.make_async_remote_copy(src_ref=land_ref.at[me], dst_ref=land_ref.at[me], send_sem=sems[k],
                                         recv_sem=sems[4 + k], device_id=to, device_id_type=MESH).start()
        token[...] = jnp.zeros_like(token)

    outs = pl.pallas_call(
        body, name=name,
        out_shape=(pltpu.SemaphoreType.DMA(()),) * 8 + (pltpu.HBM(land.shape, land.dtype), jax.ShapeDtypeStruct((8, 128), F32)),
        in_specs=(_HBM, _ANY), out_specs=(_SEM,) * 8 + (_HBM, pl.BlockSpec(memory_space=pltpu.VMEM)),
        input_output_aliases={0: 8}, compiler_params=pltpu.CompilerParams(has_side_effects=_DATAFLOW),
    )(_hbm(land), order)
    return outs[:8], outs[8], outs[9]


def _gather_wait(sems, land_thru, after, name):
    def body(land_ref, *rest):
        sems_ = rest[:8]
        me, targets = _gather_targets()
        for k, to in enumerate(targets):
            cp = pltpu.make_async_remote_copy(src_ref=land_ref.at[me], dst_ref=land_ref.at[me], send_sem=sems_[k],
                                              recv_sem=sems_[4 + k], device_id=to, device_id_type=MESH)
            cp.wait_send()
            cp.wait_recv()

    return pl.pallas_call(
        body, name=name, out_shape=pltpu.HBM(land_thru.shape, land_thru.dtype),
        in_specs=(_HBM,) + (_SEM,) * 8 + (_ANY,), out_specs=_HBM, input_output_aliases={0: 0},
        compiler_params=pltpu.CompilerParams(has_side_effects=_DATAFLOW),
    )(land_thru, *sems, after)


def _gather_finish(land, name):
    def body(land_ref, out, send, recv):
        x, y, c, chips = _chips()
        fwd = [pltpu.make_async_remote_copy(src_ref=out.at[4 * px + 2 * py + c], dst_ref=out.at[4 * px + 2 * py + c],
                                            send_sem=send.at[j], recv_sem=recv.at[j], device_id=(x, y, 1 - c), device_id_type=MESH)
               for j, (px, py) in enumerate(chips)]
        for cp in fwd:
            cp.start()
        for j, (px, py) in enumerate(chips):
            slot = out.at[4 * px + 2 * py + 1 - c]
            pltpu.make_async_remote_copy(src_ref=slot, dst_ref=slot, send_sem=send.at[j], recv_sem=recv.at[j],
                                         device_id=(x, y, 1 - c), device_id_type=MESH).wait()

    return pl.pallas_call(
        body, name=name, out_shape=jax.ShapeDtypeStruct(land.shape, land.dtype),
        in_specs=[_HBM], out_specs=_HBM, input_output_aliases={0: 0},
        scratch_shapes=[pltpu.SemaphoreType.DMA((3,)), pltpu.SemaphoreType.DMA((3,))],
        compiler_params=pltpu.CompilerParams(has_side_effects=True),
    )(land)


def _forward_start(land, order, name):
    def body(land_ref, order_ref, *rest):
        sems, token = rest[:6], rest[7]
        x, y, c, chips = _chips()
        for j, (px, py) in enumerate(chips):
            slot = land_ref.at[4 * px + 2 * py + c]
            pltpu.make_async_remote_copy(src_ref=slot, dst_ref=slot, send_sem=sems[j], recv_sem=sems[3 + j],
                                         device_id=(x, y, 1 - c), device_id_type=MESH).start()
        token[...] = jnp.zeros_like(token)

    outs = pl.pallas_call(
        body, name=name,
        out_shape=(pltpu.SemaphoreType.DMA(()),) * 6 + (pltpu.HBM(land.shape, land.dtype), jax.ShapeDtypeStruct((8, 128), F32)),
        in_specs=(_HBM, _ANY), out_specs=(_SEM,) * 6 + (_HBM, pl.BlockSpec(memory_space=pltpu.VMEM)),
        input_output_aliases={0: 6}, compiler_params=pltpu.CompilerParams(has_side_effects=_DATAFLOW),
    )(_hbm(land), order)
    return outs[:6], outs[6], outs[7]


def _forward_wait(sems, land_thru, after, name):
    def body(land_ref, *rest):
        sems_ = rest[:6]
        x, y, c, chips = _chips()
        for j, (px, py) in enumerate(chips):
            cp = pltpu.make_async_remote_copy(src_ref=land_ref.at[4 * px + 2 * py + c], dst_ref=land_ref.at[4 * px + 2 * py + 1 - c],
                                              send_sem=sems_[j], recv_sem=sems_[3 + j], device_id=(x, y, 1 - c),
                                              device_id_type=MESH)
            cp.wait_send()
            cp.wait_recv()

    return pl.pallas_call(
        body, name=name, out_shape=pltpu.HBM(land_thru.shape, land_thru.dtype),
        in_specs=(_HBM,) + (_SEM,) * 6 + (_ANY,), out_specs=_HBM, input_output_aliases={0: 0},
        compiler_params=pltpu.CompilerParams(has_side_effects=_DATAFLOW),
    )(land_thru, *sems, after)


def _chip_targets():
    x, y, c, chips = _chips()
    return 2 * x + y, [((px, py, c), 2 * px + py) for px, py in chips]


def _chipsum_start(s, land, order, name):
    def body(s_ref, land_ref, order_ref, *rest):
        sems, token = rest[:6], rest[8]
        mine, targets = _chip_targets()
        for k, (to, ch) in enumerate(targets):
            pltpu.make_async_remote_copy(src_ref=s_ref.at[ch], dst_ref=land_ref.at[mine], send_sem=sems[k], recv_sem=sems[3 + k],
                                         device_id=to, device_id_type=MESH).start()
        token[...] = jnp.zeros_like(token)

    outs = pl.pallas_call(
        body, name=name,
        out_shape=(pltpu.SemaphoreType.DMA(()),) * 6 + (pltpu.HBM(s.shape, s.dtype), pltpu.HBM(land.shape, land.dtype),
                                                        jax.ShapeDtypeStruct((8, 128), F32)),
        in_specs=(_HBM, _HBM, _ANY), out_specs=(_SEM,) * 6 + (_HBM, _HBM, pl.BlockSpec(memory_space=pltpu.VMEM)),
        input_output_aliases={0: 6, 1: 7}, compiler_params=pltpu.CompilerParams(has_side_effects=_DATAFLOW),
    )(_hbm(s), _hbm(land), order)
    return outs[:6], outs[6], outs[7], outs[8]


def _chipsum_wait(sems, s_thru, land_thru, after, name):
    def body(s_ref, land_ref, *rest):
        sems_ = rest[:6]
        mine, targets = _chip_targets()
        for k, (to, ch) in enumerate(targets):
            cp = pltpu.make_async_remote_copy(src_ref=s_ref.at[ch], dst_ref=land_ref.at[ch], send_sem=sems_[k], recv_sem=sems_[3 + k],
                                              device_id=to, device_id_type=MESH)
            cp.wait_send()
            cp.wait_recv()

    return pl.pallas_call(
        body, name=name, out_shape=(pltpu.HBM(s_thru.shape, s_thru.dtype), pltpu.HBM(land_thru.shape, land_thru.dtype)),
        in_specs=(_HBM, _HBM) + (_SEM,) * 6 + (_ANY,), out_specs=(_HBM, _HBM), input_output_aliases={0: 0, 1: 1},
        compiler_params=pltpu.CompilerParams(has_side_effects=_DATAFLOW),
    )(s_thru, land_thru, *sems, after)[1]


def _pair_start(p, order, name):
    def body(p_ref, land_ref, order_ref, *rest):
        sems, token = rest[:8], rest[10]
        x, y, c = _me()
        for ch in range(4):
            pltpu.make_async_remote_copy(src_ref=p_ref.at[ch, 1 - c], dst_ref=land_ref.at[ch], send_sem=sems[ch],
                                         recv_sem=sems[4 + ch], device_id=(x, y, 1 - c), device_id_type=MESH).start()
        token[...] = jnp.zeros_like(token)

    land = lax.empty((4,) + p.shape[2:], p.dtype)
    outs = pl.pallas_call(
        body, name=name,
        out_shape=(pltpu.SemaphoreType.DMA(()),) * 8 + (pltpu.HBM(p.shape, p.dtype), pltpu.HBM(land.shape, land.dtype),
                                                        jax.ShapeDtypeStruct((8, 128), F32)),
        in_specs=(_HBM, _HBM, _ANY), out_specs=(_SEM,) * 8 + (_HBM, _HBM, pl.BlockSpec(memory_space=pltpu.VMEM)),
        input_output_aliases={0: 8, 1: 9}, compiler_params=pltpu.CompilerParams(has_side_effects=_DATAFLOW),
    )(_hbm(p), _hbm(land), order)
    return outs[:8], outs[8], outs[9], outs[10]


def _pair_wait(sems, p_thru, land_thru, after, name):
    def body(p_ref, land_ref, *rest):
        sems_ = rest[:8]
        x, y, c = _me()
        for ch in range(4):
            cp = pltpu.make_async_remote_copy(src_ref=p_ref.at[ch, 1 - c], dst_ref=land_ref.at[ch], send_sem=sems_[ch],
                                              recv_sem=sems_[4 + ch], device_id=(x, y, 1 - c), device_id_type=MESH)
            cp.wait_send()
            cp.wait_recv()

    return pl.pallas_call(
        body, name=name, out_shape=(pltpu.HBM(p_thru.shape, p_thru.dtype), pltpu.HBM(land_thru.shape, land_thru.dtype)),
        in_specs=(_HBM, _HBM) + (_SEM,) * 8 + (_ANY,), out_specs=(_HBM, _HBM), input_output_aliases={0: 0, 1: 1},
        compiler_params=pltpu.CompilerParams(has_side_effects=_DATAFLOW),
    )(p_thru, land_thru, *sems, after)


def _sibling_start(p, order, name):
    def body(p_ref, land_ref, order_ref, send_sem, recv_sem, p_thru, land_thru, token):
        x, y, c = _me()
        pltpu.make_async_remote_copy(src_ref=p_ref, dst_ref=land_ref, send_sem=send_sem, recv_sem=recv_sem,
                                     device_id=(x, y, 1 - c), device_id_type=MESH).start()
        token[...] = jnp.zeros_like(token)

    land = lax.empty(p.shape, p.dtype)
    outs = pl.pallas_call(
        body, name=name,
        out_shape=(pltpu.SemaphoreType.DMA(()),) * 2 + (pltpu.HBM(p.shape, p.dtype), pltpu.HBM(p.shape, p.dtype),
                                                        jax.ShapeDtypeStruct((8, 128), F32)),
        in_specs=(_HBM, _HBM, _ANY), out_specs=(_SEM,) * 2 + (_HBM, _HBM, pl.BlockSpec(memory_space=pltpu.VMEM)),
        input_output_aliases={0: 2, 1: 3}, compiler_params=pltpu.CompilerParams(has_side_effects=_DATAFLOW),
    )(_hbm(p), _hbm(land), order)
    return outs[:2], outs[2], outs[3], outs[4]


def _sibling_wait(sems, p_thru, land_thru, after, name):
    def body(p_ref, land_ref, send_sem, recv_sem, after_ref, p_dead, got_ref):
        x, y, c = _me()
        cp = pltpu.make_async_remote_copy(src_ref=p_ref, dst_ref=land_ref, send_sem=send_sem, recv_sem=recv_sem,
                                          device_id=(x, y, 1 - c), device_id_type=MESH)
        cp.wait_send()
        cp.wait_recv()

    return pl.pallas_call(
        body, name=name, out_shape=(pltpu.HBM(p_thru.shape, p_thru.dtype), pltpu.HBM(land_thru.shape, land_thru.dtype)),
        in_specs=(_HBM, _HBM, _SEM, _SEM, _ANY), out_specs=(_HBM, _HBM), input_output_aliases={0: 0, 1: 1},
        compiler_params=pltpu.CompilerParams(has_side_effects=_DATAFLOW),
    )(p_thru, land_thru, *sems, after)


def _mm(a, b, *, mode, name, out_dtypes=(F32,), epilogue=None, extras=(), tm=1024, tn=1024, tk=2048,
        col_blocked_b=False, col_blocked_out=False, order=None):
    CB = 1024
    if col_blocked_b:
        assert mode in ("nn", "nt") and b.shape[2] == CB
        (M, K), N = a.shape, (b.shape[0] * CB if mode == "nn" else b.shape[1])
        assert mode == "nn" or tk % CB == 0
        tn = CB if mode == "nn" else tn
    elif mode == "nn":
        (M, K), N = a.shape, b.shape[1]
    elif mode == "nt":
        (M, K), N = a.shape, b.shape[0]
    else:
        (K, M), N = a.shape, b.shape[1]
    if col_blocked_out:
        assert len(out_dtypes) == 1 and N % CB == 0
        tn = CB
    tm, tn, tk = min(tm, M), min(tn, N), min(tk, K)
    assert M % tm == 0 and N % tn == 0 and K % tk == 0, (M, N, K, tm, tn, tk)
    nk = K // tk
    ne, no = len(extras), len(out_dtypes)
    dims = {"nn": (((1,), (0,)), ((), ())), "nt": (((1,), (1,)), ((), ())), "tn": (((0,), (0,)), ((), ()))}[mode]

    no_ = 0 if order is None else 1

    def body(a_ref, b_ref, *rest):
        rest = rest[no_:]
        ex, outs = rest[:ne], rest[ne:ne + no]

        def finish(acc):
            res = epilogue(acc, *[e[...] for e in ex]) if epilogue is not None else (acc,)
            for o, r in zip(outs, res):
                o[...] = r.astype(o.dtype)

        if col_blocked_b and mode == "nt":
            part = sum(lax.dot_general(a_ref[:, q * CB:(q + 1) * CB], b_ref[q], dims, preferred_element_type=F32)
                       for q in range(tk // CB))
        else:
            part = lax.dot_general(a_ref[...], b_ref[...].astype(BF16), dims, preferred_element_type=F32)
        if nk == 1:
            finish(part)
        else:
            acc_ref = rest[-1]
            k = pl.program_id(2)

            @pl.when(k == 0)
            def _():
                acc_ref[...] = part

            @pl.when(k > 0)
            def _():
                acc_ref[...] += part

            @pl.when(k == nk - 1)
            def _():
                finish(acc_ref[...])

    a_spec = {"nn": pl.BlockSpec((tm, tk), lambda i, j, k: (i, k)), "nt": pl.BlockSpec((tm, tk), lambda i, j, k: (i, k)),
              "tn": pl.BlockSpec((tk, tm), lambda i, j, k: (k, i))}[mode]
    b_spec = {"nn": pl.BlockSpec((tk, tn), lambda i, j, k: (k, j)), "nt": pl.BlockSpec((tn, tk), lambda i, j, k: (j, k)),
              "tn": pl.BlockSpec((tk, tn), lambda i, j, k: (k, j))}[mode]
    if col_blocked_b:
        b_spec = (pl.BlockSpec((None, tk, CB), lambda i, j, k: (j, k, 0)) if mode == "nn"
                  else pl.BlockSpec((tk // CB, tn, CB), lambda i, j, k: (k, j, 0)))
    e_spec = pl.BlockSpec((tm, tn), lambda i, j, k: (i, j))
    o_spec, o_dims = e_spec, (M, N)
    if col_blocked_out:
        o_spec, o_dims = pl.BlockSpec((None, tm, CB), lambda i, j, k: (j, i, 0)), (N // CB, M, CB)
    outs = pl.pallas_call(
        body, name=name, grid=(M // tm, N // tn, nk),
        in_specs=[a_spec, b_spec] + [_ANY] * no_ + [e_spec] * ne, out_specs=[o_spec] * no,
        out_shape=[jax.ShapeDtypeStruct(o_dims, dt) for dt in out_dtypes],
        scratch_shapes=[pltpu.VMEM((tm, tn), F32)] if nk > 1 else [],
        compiler_params=pltpu.CompilerParams(dimension_semantics=("parallel", "parallel", "arbitrary")),
    )(a, b, *([] if order is None else [order]), *extras)
    return outs if no > 1 else outs[0]


def _dw_half(a, b, side, *, axis, name, add=None, order=None, tile=1024, tk=4096):
    (K, M), N = a.shape, b.shape[1]
    tk = min(tk, K)
    nk = K // tk
    if axis == "m":
        tm, tn = tile, min(N, 1024)
        grid, o_dims = (4, N // tn, nk), (4, tile, N)
        a_spec = pl.BlockSpec((tk, tm), lambda q, j, k, s: (k, 2 * q + s[0]))
        b_spec = pl.BlockSpec((tk, tn), lambda q, j, k, s: (k, j))
        o_spec = pl.BlockSpec((None, tm, tn), lambda q, j, k, s: (q, 0, j))
    else:
        tm, tn = min(M, 1024), tile
        grid, o_dims = (M // tm, 4, nk), (4, M, tile)
        a_spec = pl.BlockSpec((tk, tm), lambda i, q, k, s: (k, i))
        b_spec = pl.BlockSpec((tk, tn), lambda i, q, k, s: (k, 2 * q + s[0]))
        o_spec = pl.BlockSpec((None, tm, tn), lambda i, q, k, s: (q, i, 0))
    n_order, n_add = int(order is not None), int(add is not None)
    n_out = 1 + n_add

    def body(s_ref, a_ref, b_ref, *rest):
        rest = rest[n_order:]
        outs = rest[n_add:n_add + n_out]

        def finish(acc):
            res = acc + rest[0][...].astype(F32) if n_add else acc
            for o in outs:
                o[...] = res.astype(o.dtype)

        part = _dot_tn(a_ref[...], b_ref[...])
        if nk == 1:
            finish(part)
        else:
            acc_ref, k = rest[-1], pl.program_id(2)

            @pl.when(k == 0)
            def _():
                acc_ref[...] = part

            @pl.when(k > 0)
            def _():
                acc_ref[...] += part

            @pl.when(k == nk - 1)
            def _():
                finish(acc_ref[...])

    outs = pl.pallas_call(
        body, name=name, out_shape=[jax.ShapeDtypeStruct(o_dims, BF16)] * n_out,
        grid_spec=pltpu.PrefetchScalarGridSpec(
            num_scalar_prefetch=1, grid=grid, in_specs=[a_spec, b_spec] + [_ANY] * n_order + [o_spec] * n_add,
            out_specs=[o_spec] * n_out, scratch_shapes=[pltpu.VMEM((tm, tn), F32)] if nk > 1 else []),
        compiler_params=pltpu.CompilerParams(dimension_semantics=("arbitrary", "arbitrary", "arbitrary")),
    )(side, a, b, *([order] if n_order else []), *([add] if n_add else []))
    return outs if n_add else outs[0]


def _norm_fwd(x, g, sc, sh, name, resid=None):
    B, S, Dm = x.shape
    ts = min(S, 256)
    tok = pl.BlockSpec((None, ts, Dm), lambda b, i: (b, i, 0))
    row = pl.BlockSpec((None, 1, Dm), lambda b, i: (b, 0, 0))
    par = pl.BlockSpec((1, Dm), lambda b, i: (0, 0))

    def body(*refs):
        if resid is not None:
            x_ref, br_ref, gt_ref, g_ref, sc_ref, sh_ref, xo_ref, h_ref = refs
            xv = x_ref[...] + gt_ref[...] * br_ref[...]
            xo_ref[...] = xv
        else:
            x_ref, g_ref, sc_ref, sh_ref, h_ref = refs
            xv = x_ref[...]
        r = lax.rsqrt(jnp.mean(xv * xv, axis=-1, keepdims=True) + EPS)
        h_ref[...] = ((xv * r * g_ref[...]) * (1.0 + sc_ref[...]) + sh_ref[...]).astype(BF16)

    h_shape = jax.ShapeDtypeStruct((B, S, Dm), BF16)
    if resid is not None:
        return pl.pallas_call(body, name=name, grid=(B, S // ts), in_specs=[tok, tok, row, par, row, row],
                              out_specs=[tok, tok], out_shape=[jax.ShapeDtypeStruct((B, S, Dm), F32), h_shape],
                              )(x, resid[0], resid[1], g, sc, sh)
    return pl.pallas_call(body, name=name, grid=(B, S // ts), in_specs=[tok, par, row, row], out_specs=tok,
                          out_shape=h_shape)(x, g, sc, sh)


def _norm_bwd(x, g, name, *, sc=None, dh=None, dres=None, tgt=None, br=None, gate=None, x_is_prev=False):
    B, S, Dm = x.shape
    ts = min(S, 256)
    final = tgt is not None
    has_br = br is not None
    tok = pl.BlockSpec((None, ts, Dm), lambda b, i: (b, i, 0))
    row = pl.BlockSpec((None, 1, Dm), lambda b, i: (b, 0, 0))
    par = pl.BlockSpec((1, Dm), lambda b, i: (0, 0))
    ins, in_specs = [x, g], [tok, par]
    if final:
        ins, in_specs = ins + [tgt], in_specs + [tok]
    else:
        ins, in_specs = ins + [sc, dh], in_specs + [row, tok]
    if dres is not None:
        ins, in_specs = ins + [dres], in_specs + [tok]
    if has_br:
        ins, in_specs = ins + [br, gate], in_specs + [tok, row]
    n_in = len(ins)
    out_shape = [jax.ShapeDtypeStruct((B, S, Dm), F32), jax.ShapeDtypeStruct((1, Dm), F32)]
    out_specs = [tok, par]
    if final:
        out_shape.append(jax.ShapeDtypeStruct((1, 128), F32))
        out_specs.append(pl.BlockSpec((1, 128), lambda b, i: (0, 0)))
    else:
        out_shape += [jax.ShapeDtypeStruct((B, 1, Dm), F32)] * 2
        out_specs += [row, row]
    if has_br:
        out_shape += [jax.ShapeDtypeStruct((B, S, Dm), BF16), jax.ShapeDtypeStruct((B, 1, Dm), F32)]
        out_specs += [tok, row]

    def body(*refs):
        it = iter(refs[:n_in])
        outs = iter(refs[n_in:])
        x_ref, g_ref = next(it), next(it)
        b, i = pl.program_id(0), pl.program_id(1)
        first, first_row = (b == 0) & (i == 0), i == 0
        xv, gv = x_ref[...], g_ref[...]
        if x_is_prev:
            xv = xv + refs[n_in - 1][...] * refs[n_in - 2][...]
        r = lax.rsqrt(jnp.mean(xv * xv, axis=-1, keepdims=True) + EPS)
        n = xv * r
        dx_ref, dg_ref = next(outs), next(outs)

        def acc(ref, val, init):
            @pl.when(init)
            def _():
                ref[...] = val

            @pl.when(jnp.logical_not(init))
            def _():
                ref[...] += val

        if final:
            t_ref = next(it)
            loss_ref = next(outs)
            e = n * gv - t_ref[...]
            acc(loss_ref, jnp.zeros((1, 128), F32) + 0.5 * jnp.sum(e * e) / Dm, first)
            dyg = e * (1.0 / Dm)
        else:
            sc_ref, dh_ref = next(it), next(it)
            dsc_ref, dsh_ref = next(outs), next(outs)
            dhv = dh_ref[...].astype(F32)
            acc(dsh_ref, jnp.sum(dhv, axis=0, keepdims=True), first_row)
            acc(dsc_ref, jnp.sum(dhv * (n * gv), axis=0, keepdims=True), first_row)
            dyg = dhv * (1.0 + sc_ref[...])
        acc(dg_ref, jnp.sum(dyg * n, axis=0, keepdims=True), first)
        dn = dyg * gv
        dx = r * (dn - n * jnp.mean(dn * n, axis=-1, keepdims=True))
        if dres is not None:
            dx = dx + next(it)[...]
        dx_ref[...] = dx
        if has_br:
            br_ref, gt_ref = next(it), next(it)
            dbr_ref, dgt_ref = next(outs), next(outs)
            dbr_ref[...] = (dx * gt_ref[...]).astype(BF16)
            acc(dgt_ref, jnp.sum(dx * br_ref[...], axis=0, keepdims=True), first_row)

    outs = pl.pallas_call(body, name=name, grid=(B, S // ts), in_specs=in_specs, out_specs=out_specs, out_shape=out_shape,
                          compiler_params=pltpu.CompilerParams(dimension_semantics=("arbitrary", "arbitrary")))(*ins)
    res = dict(dx=outs[0], dg=outs[1])
    if final:
        res["loss"] = outs[2]
    else:
        res["dsc"], res["dsh"] = outs[2], outs[3]
    if has_br:
        res["dbr"], res["dgate"] = outs[-2], outs[-1]
    return res


def _gm_heads(vg, lng, lnb):
    res = []
    for h in range(GM_H):
        sl = slice(h * 128, (h + 1) * 128)
        vh = vg[:, sl]
        xc = vh - jnp.mean(vh, axis=-1, keepdims=True)
        rstd = lax.rsqrt(jnp.mean(xc * xc, axis=-1, keepdims=True) + 1e-5)
        xhat = xc * rstd
        res.append((xhat, rstd, xhat * lng[:, sl] + lnb[:, sl]))
    return res


def _gm_gate(heads, wt_ref, bsx, nch):
    cols = []
    for h in range(GM_H):
        vn = heads[h][2].astype(BF16)
        rows = [_dot(wt_ref[h], vn[c * CHUNK:(c + 1) * CHUNK]) + bsx[:, h * 128:(h + 1) * 128] for c in range(nch)]
        cols.append(jnp.concatenate(rows, axis=0) if nch > 1 else rows[0])
    return jnp.concatenate(cols, axis=1)


def _gm_specs(S):
    tb = min(S, 512)
    u = pl.BlockSpec((None, tb, GM_W), lambda b, i: (b, i, OFF["u"] // GM_W))
    v = pl.BlockSpec((None, tb, GM_W), lambda b, i: (b, i, OFF["v"] // GM_W))
    tok = pl.BlockSpec((None, tb, GM_W), lambda b, i: (b, i, 0))
    return tb, u, v, tok


def _gmlp_fwd(P, lng, lnb, wt, bsx, og, name):
    B, S, _ = P.shape
    tb, u_spec, v_spec, tok = _gm_specs(S)
    nch = tb // CHUNK

    def body(u_ref, v_ref, lng_ref, lnb_ref, wt_ref, bsx_ref, og_ref, o_ref):
        heads = _gm_heads(_gelu(v_ref[...]), lng_ref[...], lnb_ref[...])
        y = _gelu(u_ref[...]) * _gm_gate(heads, wt_ref, bsx_ref[...], nch)
        r = lax.rsqrt(jnp.mean(y * y, axis=-1, keepdims=True) + EPS)
        o_ref[...] = (y * r * og_ref[...]).astype(BF16)

    return pl.pallas_call(
        body, name=name, grid=(B, S // tb),
        in_specs=[u_spec, v_spec, _full((1, GM_W)), _full((1, GM_W)), _full((GM_H, 128, 128)), _full((128, GM_W)), _full((1, GM_W))],
        out_specs=tok, out_shape=jax.ShapeDtypeStruct((B, S, GM_W + ATT_W + SSM_W), BF16))(P, P, lng, lnb, wt, bsx, og)


def _gmlp_bwd(P, dcat, lng, lnb, wt, wtT, bsx, og, name):
    B, S, _ = P.shape
    tb, u_spec, v_spec, tok = _gm_specs(S)
    nch = tb // CHUNK
    do_spec = pl.BlockSpec((None, tb, GM_W), lambda b, i: (b, i, 0))

    def body(u_ref, v_ref, do_ref, lng_ref, lnb_ref, wt_ref, wtT_ref, bsx_ref, og_ref,
             du_ref, dv_ref, dlng_ref, dlnb_ref, dws_ref, dbsx_ref, dog_ref):
        first = (pl.program_id(0) == 0) & (pl.program_id(1) == 0)

        @pl.when(first)
        def _():
            for ref in (dlng_ref, dlnb_ref, dws_ref, dbsx_ref, dog_ref):
                ref[...] = jnp.zeros(ref.shape, F32)

        u, v, lng = u_ref[...], v_ref[...], lng_ref[...]
        ug = _gelu(u)
        heads = _gm_heads(_gelu(v), lng, lnb_ref[...])
        gate = _gm_gate(heads, wt_ref, bsx_ref[...], nch)
        y = ug * gate
        r = lax.rsqrt(jnp.mean(y * y, axis=-1, keepdims=True) + EPS)
        yn = y * r
        dout = do_ref[...].astype(F32)
        dog_ref[...] += jnp.sum(dout * yn, axis=0, keepdims=True)
        dyn = dout * og_ref[...]
        dy = r * (dyn - yn * jnp.mean(dyn * yn, axis=-1, keepdims=True))
        du_ref[...] = (dy * gate * _gelu_grad(u)).astype(BF16)
        dgate = dy * ug
        tril = lax.broadcasted_iota(jnp.int32, (128, 128), 0) >= lax.broadcasted_iota(jnp.int32, (128, 128), 1)
        dvg = []
        for h in range(GM_H):
            sl = slice(h * 128, (h + 1) * 128)
            xhat, rstd, vn = heads[h]
            vnb = vn.astype(BF16)
            dgh = dgate[:, sl]
            dgb = dgh.astype(BF16)
            dbs = jnp.zeros((128, 128), F32)
            dw = jnp.zeros((128, 128), F32)
            dvn = []
            for c in range(nch):
                rs = slice(c * CHUNK, (c + 1) * CHUNK)
                dbs = dbs + dgh[rs]
                dw = dw + _dot_nt(dgb[rs], vnb[rs])
                dvn.append(_dot(wtT_ref[h], dgb[rs]))
            dvn = jnp.concatenate(dvn, axis=0) if nch > 1 else dvn[0]
            dbsx_ref[:, sl] += dbs
            dws_ref[h] += jnp.where(tril, dw, 0.0)
            dlng_ref[:, sl] += jnp.sum(dvn * xhat, axis=0, keepdims=True)
            dlnb_ref[:, sl] += jnp.sum(dvn, axis=0, keepdims=True)
            dxh = dvn * lng[:, sl]
            dvg.append(rstd * (dxh - jnp.mean(dxh, axis=-1, keepdims=True) - xhat * jnp.mean(dxh * xhat, axis=-1, keepdims=True)))
        dv_ref[...] = (jnp.concatenate(dvg, axis=1) * _gelu_grad(v)).astype(BF16)

    p512, w3 = _full((1, GM_W)), _full((GM_H, 128, 128))
    return pl.pallas_call(
        body, name=name, grid=(B, S // tb),
        in_specs=[u_spec, v_spec, do_spec, p512, p512, w3, w3, _full((128, GM_W)), p512],
        out_specs=[tok, tok, p512, p512, w3, _full((128, GM_W)), p512],
        out_shape=[jax.ShapeDtypeStruct((B, S, GM_W), BF16)] * 2 + [
            jax.ShapeDtypeStruct((1, GM_W), F32), jax.ShapeDtypeStruct((1, GM_W), F32),
            jax.ShapeDtypeStruct((GM_H, 128, 128), F32), jax.ShapeDtypeStruct((128, GM_W), F32),
            jax.ShapeDtypeStruct((1, GM_W), F32)],
        compiler_params=pltpu.CompilerParams(dimension_semantics=("arbitrary", "arbitrary")),
    )(P, P, dcat, lng, lnb, wt, wtT, bsx, og)


def _lane_half():
    return lax.broadcasted_iota(jnp.int32, (128, 128), 1) // 64


def _att_stack(x, kvh, dtype):
    half = _lane_half()
    rows = []
    for g in range(4):
        i = kvh * 4 + g
        pair = x[:, (i // 2) * 128:(i // 2 + 1) * 128]
        if i % 2 != kvh:
            pair = pltpu.roll(pair, 64, 1)
        rows.append(jnp.where(half == kvh, pair, 0.0))
    return jnp.concatenate(rows, axis=0).astype(dtype)


def _att_unstack(pairs, y, kvh):
    half = _lane_half()
    for g in range(4):
        i = kvh * 4 + g
        piece = y[g * 128:(g + 1) * 128]
        if i % 2 != kvh:
            piece = pltpu.roll(piece, 64, 1)
        pairs[i // 2] = jnp.where(half == i % 2, piece, pairs[i // 2])
    return pairs


def _att_fill_bias(bias_ref):
    qi = lax.broadcasted_iota(jnp.int32, (512, 256), 0) % 128
    kj = lax.broadcasted_iota(jnp.int32, (512, 256), 1)
    diff = qi + 128 - kj
    band = (diff >= 0) & (diff < 128)
    bias_ref[0:512, :] = jnp.where(band, 0.0, NEG_INF)
    bias_ref[512:1024, :] = jnp.where(band & (kj >= 128), 0.0, NEG_INF)


def _att_bias(bias_ref, n):
    return bias_ref[pl.ds(pl.multiple_of(jnp.where(n == 0, 512, 0), 512), 512), :]


def _att_probs(qb, k2, bias, sink_ref, kvh):
    qm = _att_stack(qb, kvh, BF16)
    s = _dot_nt(qm, k2) * (64 ** -0.5) + bias
    grp = lax.broadcasted_iota(jnp.int32, (512, 1), 0) // 128
    sink = jnp.zeros((512, 1), F32)
    for g in range(4):
        sink = jnp.where(grp == g, sink_ref[kvh * 4 + g], sink)
    m = jnp.maximum(jnp.max(s, axis=-1, keepdims=True), sink)
    e = jnp.exp(s - m)
    esink = jnp.exp(sink - m)
    inv = 1.0 / (jnp.sum(e, axis=-1, keepdims=True) + esink)
    return qm, e * inv, esink * inv


def _att_specs(S):
    q = pl.BlockSpec((None, S, ATT_W), lambda b: (b, 0, OFF["q"] // ATT_W))
    k = pl.BlockSpec((None, S, KV_W), lambda b: (b, 0, OFF["k"] // KV_W))
    v = pl.BlockSpec((None, S, KV_W), lambda b: (b, 0, OFF["vv"] // KV_W))
    tok = pl.BlockSpec((None, S, ATT_W), lambda b: (b, 0, 0))
    kv = pl.BlockSpec((None, S, KV_W), lambda b: (b, 0, 0))
    return q, k, v, tok, kv


_SMEM = pl.BlockSpec(memory_space=pltpu.SMEM)


def _attn_fwd(P, sinks, og, cat, name):
    B, S, _ = P.shape
    q_spec, k_spec, v_spec, _, _ = _att_specs(S)
    tok = pl.BlockSpec((None, S, ATT_W), lambda b: (b, 0, GM_W // ATT_W))

    def body(q_ref, k_ref, v_ref, sink_ref, og_ref, cat_ref, o_ref, kpad, vpad, bias_ref):
        _att_fill_bias(bias_ref)
        kpad[0:128, :] = jnp.zeros((128, KV_W), BF16)
        vpad[0:128, :] = jnp.zeros((128, KV_W), BF16)
        kpad[128:, :] = k_ref[...].astype(BF16)
        vpad[128:, :] = v_ref[...].astype(BF16)

        def step(n, carry):
            st = pl.multiple_of(n * 128, 128)
            qb = q_ref[pl.ds(st, 128), :]
            k2, v2 = kpad[pl.ds(st, 256), :], vpad[pl.ds(st, 256), :]
            pairs = [jnp.zeros((128, 128), F32)] * 4
            bias = _att_bias(bias_ref, n)
            for kvh in range(2):
                _, p, _ = _att_probs(qb, k2, bias, sink_ref, kvh)
                pairs = _att_unstack(pairs, _dot(p.astype(BF16), v2), kvh)
            o = jnp.concatenate(pairs, axis=1)
            r = lax.rsqrt(jnp.mean(o * o, axis=-1, keepdims=True) + EPS)
            o_ref[pl.ds(st, 128), :] = (o * r * og_ref[...]).astype(BF16)
            return carry

        lax.fori_loop(0, S // 128, step, 0)

    return pl.pallas_call(
        body, name=name, grid=(B,), in_specs=[q_spec, k_spec, v_spec, _SMEM, _full((1, ATT_W)), _ANY], out_specs=tok,
        out_shape=jax.ShapeDtypeStruct(cat.shape, BF16), input_output_aliases={5: 0},
        scratch_shapes=[pltpu.VMEM((S + 128, KV_W), BF16)] * 2 + [pltpu.VMEM((1024, 256), F32)])(P, P, P, sinks, og, cat)


def _attn_bwd(P, dcat, sinks, og, name):
    B, S, _ = P.shape
    q_spec, k_spec, v_spec, tok, kv = _att_specs(S)
    do_spec = pl.BlockSpec((None, S, ATT_W), lambda b: (b, 0, GM_W // ATT_W))

    def body(q_ref, k_ref, v_ref, do_ref, sink_ref, og_ref, dq_ref, dk_ref, dv_ref, dsink_ref, dog_ref,
             kpad, vpad, dkpad, dvpad, bias_ref):
        _att_fill_bias(bias_ref)

        @pl.when(pl.program_id(0) == 0)
        def _():
            dsink_ref[...] = jnp.zeros((8, 128), F32)
            dog_ref[...] = jnp.zeros((1, ATT_W), F32)

        kpad[0:128, :] = jnp.zeros((128, KV_W), BF16)
        vpad[0:128, :] = jnp.zeros((128, KV_W), BF16)
        kpad[128:, :] = k_ref[...].astype(BF16)
        vpad[128:, :] = v_ref[...].astype(BF16)
        dkpad[...] = jnp.zeros((S + 128, KV_W), F32)
        dvpad[...] = jnp.zeros((S + 128, KV_W), F32)
        half = _lane_half()
        head_row = lax.broadcasted_iota(jnp.int32, (8, 128), 0)

        def step(n, carry):
            st = pl.multiple_of(n * 128, 128)
            qb = q_ref[pl.ds(st, 128), :]
            k2, v2 = kpad[pl.ds(st, 256), :], vpad[pl.ds(st, 256), :]
            saved, pairs = [], [jnp.zeros((128, 128), F32)] * 4
            bias = _att_bias(bias_ref, n)
            for kvh in range(2):
                qm, p, psink = _att_probs(qb, k2, bias, sink_ref, kvh)
                o = _dot(p.astype(BF16), v2)
                saved.append((qm, p, psink, o))
                pairs = _att_unstack(pairs, o, kvh)
            o = jnp.concatenate(pairs, axis=1)
            r = lax.rsqrt(jnp.mean(o * o, axis=-1, keepdims=True) + EPS)
            on = o * r
            dout = do_ref[pl.ds(st, 128), :].astype(F32)
            dog_ref[...] += jnp.sum(dout * on, axis=0, keepdims=True)
            dyn = dout * og_ref[...]
            do = r * (dyn - on * jnp.mean(dyn * on, axis=-1, keepdims=True))
            dq_pairs = [jnp.zeros((128, 128), F32)] * 4
            dsink = jnp.zeros((8, 128), F32)
            for kvh in range(2):
                qm, p, psink, og_ = saved[kvh]
                dog = _att_stack(do, kvh, F32)
                delta = jnp.sum(dog * jnp.where(jnp.concatenate([half] * 4, axis=0) == kvh, og_, 0.0), axis=-1, keepdims=True)
                dogb, pb = dog.astype(BF16), p.astype(BF16)
                dvpad[pl.ds(st, 256), :] += _dot_tn(pb, dogb)
                dp = _dot_nt(dogb, v2)
                ds = (p * (dp - delta) * (64 ** -0.5)).astype(BF16)
                sd = psink * delta
                for g in range(4):
                    dsink = dsink - jnp.where(head_row == kvh * 4 + g, jnp.sum(sd[g * 128:(g + 1) * 128]), 0.0)
                dq_pairs = _att_unstack(dq_pairs, _dot(ds, k2), kvh)
                dkpad[pl.ds(st, 256), :] += _dot_tn(ds, qm)
            dsink_ref[...] += dsink
            dq_ref[pl.ds(st, 128), :] = jnp.concatenate(dq_pairs, axis=1).astype(BF16)
            return carry

        lax.fori_loop(0, S // 128, step, 0)
        dk_ref[...] = dkpad[128:, :].astype(BF16)
        dv_ref[...] = dvpad[128:, :].astype(BF16)

    return pl.pallas_call(
        body, name=name, grid=(B,),
        in_specs=[q_spec, k_spec, v_spec, do_spec, _SMEM, _full((1, ATT_W))],
        out_specs=[tok, kv, kv, _full((8, 128)), _full((1, ATT_W))],
        out_shape=[jax.ShapeDtypeStruct((B, S, ATT_W), BF16), jax.ShapeDtypeStruct((B, S, KV_W), BF16),
                   jax.ShapeDtypeStruct((B, S, KV_W), BF16), jax.ShapeDtypeStruct((8, 128), F32),
                   jax.ShapeDtypeStruct((1, ATT_W), F32)],
        scratch_shapes=[pltpu.VMEM((S + 128, KV_W), BF16)] * 2 + [pltpu.VMEM((S + 128, KV_W), F32)] * 2
        + [pltpu.VMEM((1024, 256), F32)],
        compiler_params=pltpu.CompilerParams(dimension_semantics=("arbitrary",)),
    )(P, P, P, dcat, sinks, og)


CONV_TC = 256
CONV_RC = 64


def _conv_taps(ext, r0):
    return [ext[pl.ds(r0 + 8 - k, CONV_RC), :] for k in range(4)]


def _conv_pre(taps, w_ref, b_ref):
    acc = b_ref[...] + w_ref[3:4, :] * taps[0]
    for k in range(1, 4):
        acc = acc + w_ref[3 - k:4 - k, :] * taps[k]
    return acc


def _conv_fwd(P, w8, b, name):
    B, S, _ = P.shape
    nj = CONV_CH // CONV_TC
    x_spec = pl.BlockSpec((None, S, CONV_TC), lambda b_, j: (b_, 0, OFF["xbc"] // CONV_TC + j))
    tok = pl.BlockSpec((None, S, CONV_TC), lambda b_, j: (b_, 0, j))

    def body(x_ref, w_ref, b_ref, o_ref, ext):
        ext[0:8, :] = jnp.zeros((8, CONV_TC), F32)
        ext[8:, :] = x_ref[...]
        for r0 in range(0, S, CONV_RC):
            pre = _conv_pre(_conv_taps(ext, r0), w_ref, b_ref)
            o_ref[pl.ds(r0, CONV_RC), :] = pre * _sigmoid(pre)

    return pl.pallas_call(
        body, name=name, grid=(B, nj),
        in_specs=[x_spec, pl.BlockSpec((8, CONV_TC), lambda b_, j: (0, j)), pl.BlockSpec((1, CONV_TC), lambda b_, j: (0, j))],
        out_specs=tok, out_shape=jax.ShapeDtypeStruct((B, S, CONV_CH), F32),
        scratch_shapes=[pltpu.VMEM((S + 8, CONV_TC), F32)])(P, w8, b)


def _conv_bwd(P, dact, w8, b, name):
    B, S, _ = P.shape
    nj = CONV_CH // CONV_TC
    x_spec = pl.BlockSpec((None, S, CONV_TC), lambda j, b_: (b_, 0, OFF["xbc"] // CONV_TC + j))
    tok = pl.BlockSpec((None, S, CONV_TC), lambda j, b_: (b_, 0, j))
    w_spec = pl.BlockSpec((8, CONV_TC), lambda j, b_: (0, j))
    b_spec = pl.BlockSpec((1, CONV_TC), lambda j, b_: (0, j))

    def body(x_ref, d_ref, w_ref, b_ref, dx_ref, dw_ref, db_ref, ext, extd):
        @pl.when(pl.program_id(1) == 0)
        def _():
            dw_ref[...] = jnp.zeros((8, CONV_TC), F32)
            db_ref[...] = jnp.zeros((1, CONV_TC), F32)

        ext[0:8, :] = jnp.zeros((8, CONV_TC), F32)
        ext[8:, :] = x_ref[...]
        extd[pl.ds(8 + S, 8), :] = jnp.zeros((8, CONV_TC), F32)
        db = jnp.zeros((1, CONV_TC), F32)
        dws = [jnp.zeros((1, CONV_TC), F32)] * 4
        for r0 in range(0, S, CONV_RC):
            taps = _conv_taps(ext, r0)
            pre = _conv_pre(taps, w_ref, b_ref)
            sg = _sigmoid(pre)
            dpre = d_ref[pl.ds(r0, CONV_RC), :] * (sg * (1.0 + pre * (1.0 - sg)))
            extd[pl.ds(8 + r0, CONV_RC), :] = dpre
            db = db + jnp.sum(dpre, axis=0, keepdims=True)
            dws = [dws[i] + jnp.sum(dpre * taps[3 - i], axis=0, keepdims=True) for i in range(4)]
        for r0 in range(0, S, CONV_RC):
            dx = w_ref[3:4, :] * extd[pl.ds(8 + r0, CONV_RC), :]
            for k in range(1, 4):
                dx = dx + w_ref[3 - k:4 - k, :] * extd[pl.ds(8 + r0 + k, CONV_RC), :]
            dx_ref[pl.ds(r0, CONV_RC), :] = dx.astype(BF16)
        db_ref[...] += db
        sub = lax.broadcasted_iota(jnp.int32, (8, CONV_TC), 0)
        dw_ref[...] += sum(jnp.where(sub == i, dws[i], 0.0) for i in range(4))

    return pl.pallas_call(
        body, name=name, grid=(nj, B), in_specs=[x_spec, tok, w_spec, b_spec], out_specs=[tok, w_spec, b_spec],
        out_shape=[jax.ShapeDtypeStruct((B, S, CONV_CH), BF16), jax.ShapeDtypeStruct((8, CONV_CH), F32),
                   jax.ShapeDtypeStruct((1, CONV_CH), F32)],
        scratch_shapes=[pltpu.VMEM((S + 8, CONV_TC), F32), pltpu.VMEM((S + 16, CONV_TC), F32)],
        compiler_params=pltpu.CompilerParams(dimension_semantics=("arbitrary", "arbitrary")),
    )(P, dact, w8, b)


def _ssd_consts():
    hd = np.arange(SSM_W) // SSM_HD
    E = (np.arange(128)[:, None] == hd[None, :]).astype(np.float32)
    tri = (np.arange(128)[:, None] >= np.arange(128)[None, :]).astype(np.float32)
    return jnp.asarray(E, BF16), jnp.asarray(E.T, BF16), jnp.asarray(tri, BF16), jnp.asarray(tri.T, BF16)


def _pieces(x, n):
    out, r = [], x
    for _ in range(n):
        p = r.astype(BF16)
        out.append(p)
        r = r - p.astype(F32)
    return out


def _dot01(x, m01, n):
    return sum(_dot(p, m01) for p in _pieces(x, n))


def _dot01_left(m01, x, n):
    return sum(_dot(m01, p) for p in _pieces(x, n))


def _ssd_pre(xa, dtraw, bias, alog, E, tri):
    lane = lax.broadcasted_iota(jnp.int32, (128, 128), 1)
    pre = dtraw + bias
    dtp = jnp.where(lane < SSM_H, jnp.maximum(pre, 0.0) + jnp.log(1.0 + jnp.exp(-jnp.abs(pre))), 0.0)
    a = -jnp.exp(alog)
    acs = _dot01_left(tri, dtp * a, 3)
    acsT = acs.T
    dtE, acsE = _dot01(dtp, E, 2), _dot01(acs, E, 3)
    X = xa[:, :SSM_W]
    xdt = X * dtE
    wE = jnp.exp(acsE[127:128, :] - acsE)
    eE = jnp.exp(acsE)
    cdE = eE[127:128, :]
    return dict(pre=pre, dtp=dtp, a=a, acs=acs, acsT=acsT, dtE=dtE, acsE=acsE, cdE=cdE, X=X, xdt=xdt, wE=wE, eE=eE)


def _ssd_decay(c, h):
    lm = lax.broadcasted_iota(jnp.int32, (128, 128), 0) >= lax.broadcasted_iota(jnp.int32, (128, 128), 1)
    return jnp.exp(jnp.where(lm, c["acs"][:, h:h + 1] - c["acsT"][h:h + 1, :], NEG_INF))


def _ssd_pair_operands(c, CB, h0):
    lane = lax.broadcasted_iota(jnp.int32, (128, 128), 1)
    L0, L1 = _ssd_decay(c, h0), _ssd_decay(c, h0 + 1)
    M = jnp.concatenate([CB * L0, CB * L1], axis=1).astype(BF16)
    xp = c["xdt"][:, h0 * 64:h0 * 64 + 128]
    BD = jnp.concatenate([jnp.where(lane < 64, xp, 0.0), jnp.where(lane >= 64, xp, 0.0)], axis=0).astype(BF16)
    return L0, L1, M, BD


def _ssd_y(c, xa, state_ref, dskipE):
    per_group, ys = [], []
    for g in range(SSM_G):
        gs = slice(g * 512, (g + 1) * 512)
        Bb = xa[:, SSM_W + g * 128:SSM_W + (g + 1) * 128].astype(BF16)
        Cb = xa[:, SSM_W + 256 + g * 128:SSM_W + 256 + (g + 1) * 128].astype(BF16)
        CB = _dot_nt(Cb, Bb)
        Sg = state_ref[:, gs]
        yoff = _dot(Cb, Sg.astype(BF16)) * c["eE"][:, gs]
        ydiag, pairs = [], []
        for j in range(4):
            ops = _ssd_pair_operands(c, CB, g * 8 + 2 * j)
            pairs.append(ops)
            ydiag.append(_dot(ops[2], ops[3]))
        ys.append(jnp.concatenate(ydiag, axis=1) + yoff)
        per_group.append(dict(Bb=Bb, Cb=Cb, CB=CB, Sg=Sg, yoff=yoff, pairs=pairs))
    Y = jnp.concatenate(ys, axis=1) + c["X"] * dskipE
    return Y, per_group


def _ssd_specs(S, rev):
    nc = S // CHUNK
    cm = (lambda b, i: (b, nc - 1 - i)) if rev else (lambda b, i: (b, i))
    xa = pl.BlockSpec((None, CHUNK, CONV_CH), lambda b, i: cm(b, i) + (0,))
    z = [pl.BlockSpec((None, CHUNK, 256), lambda b, i, q=q: cm(b, i) + (OFF["z"] // 256 + q,)) for q in range(4)]
    dt = pl.BlockSpec((None, CHUNK, 128), lambda b, i: cm(b, i) + (OFF["dt"] // 128,))
    tok = pl.BlockSpec((None, CHUNK, SSM_W), lambda b, i: cm(b, i) + (0,))
    st = pl.BlockSpec((None, None, 128, SSM_W), lambda b, i: cm(b, i) + (0, 0))
    return nc, xa, z, dt, tok, st


def _ssd_fwd(xact, P, bias, alog, dskipE, ng, cat, name):
    B, S, _ = P.shape
    nc, xa_spec, z_specs, dt_spec, _, st_spec = _ssd_specs(S, False)
    tok = pl.BlockSpec((None, CHUNK, SSM_W), lambda b, i: (b, i, 1))
    E, _, tri, _ = _ssd_consts()

    def body(xa_ref, z0, z1, z2, z3, dt_ref, bias_ref, alog_ref, dsk_ref, ng_ref, E_ref, tri_ref, cat_ref, o_ref, sp_ref, state):
        @pl.when(pl.program_id(1) == 0)
        def _():
            state[...] = jnp.zeros((128, SSM_W), F32)

        sp_ref[...] = state[...]
        xa = xa_ref[...]
        c = _ssd_pre(xa, dt_ref[...], bias_ref[...], alog_ref[...], E_ref[...], tri_ref[...])
        Y, groups = _ssd_y(c, xa, state, dsk_ref[...])
        Z = (c["xdt"] * c["wE"]).astype(BF16)
        for g in range(SSM_G):
            gs = slice(g * 512, (g + 1) * 512)
            state[:, gs] = groups[g]["Sg"] * c["cdE"][:, gs] + _dot_tn(groups[g]["Bb"], Z[:, gs])
        zv = jnp.concatenate([z0[...], z1[...], z2[...], z3[...]], axis=1)
        yz = Y * (zv * _sigmoid(zv))
        outs = []
        for g in range(SSM_G):
            yg = yz[:, g * 512:(g + 1) * 512]
            outs.append(yg * lax.rsqrt(jnp.mean(yg * yg, axis=-1, keepdims=True) + EPS))
        o_ref[...] = (jnp.concatenate(outs, axis=1) * ng_ref[...]).astype(BF16)

    return pl.pallas_call(
        body, name=name, grid=(B, nc),
        in_specs=[xa_spec] + z_specs + [dt_spec, _full((1, 128)), _full((1, 128)), _full((1, SSM_W)), _full((1, SSM_W)),
                                        _full((128, SSM_W)), _full((128, 128)), _ANY],
        out_specs=[tok, st_spec],
        out_shape=[jax.ShapeDtypeStruct(cat.shape, BF16), jax.ShapeDtypeStruct((B, nc, 128, SSM_W), F32)],
        scratch_shapes=[pltpu.VMEM((128, SSM_W), F32)], input_output_aliases={12: 0},
        compiler_params=pltpu.CompilerParams(dimension_semantics=("arbitrary", "arbitrary")),
    )(xact, P, P, P, P, P, bias, alog, dskipE, ng, E, tri, cat)


def _ssd_bwd(xact, P, sprev, dcat, bias, alog, dskipE, ng, name):
    B, S, _ = P.shape
    nc, xa_spec, z_specs, dt_spec, tok, st_spec = _ssd_specs(S, True)
    do_spec = pl.BlockSpec((None, CHUNK, SSM_W), lambda b, i: (b, nc - 1 - i, 1))
    E, ET, tri, triT = _ssd_consts()
    dt_out = pl.BlockSpec((None, CHUNK, 128), lambda b, i: (b, nc - 1 - i, 0))

    def body(xa_ref, z0, z1, z2, z3, dt_ref, sp_ref, do_ref, bias_ref, alog_ref, dsk_ref, ng_ref, E_ref, ET_ref, tri_ref,
             triT_ref, dxa_ref, dz_ref, ddt_ref, dbias_ref, dalog_ref, ddsk_ref, dng_ref, dstate):
        first = (pl.program_id(0) == 0) & (pl.program_id(1) == 0)

        @pl.when(first)
        def _():
            for ref in (dbias_ref, dalog_ref, ddsk_ref, dng_ref):
                ref[...] = jnp.zeros(ref.shape, F32)

        @pl.when(pl.program_id(1) == 0)
        def _():
            dstate[...] = jnp.zeros((128, SSM_W), F32)

        xa, ETm = xa_ref[...], ET_ref[...]
        c = _ssd_pre(xa, dt_ref[...], bias_ref[...], alog_ref[...], E_ref[...], tri_ref[...])
        Y, groups = _ssd_y(c, xa, sp_ref, dsk_ref[...])
        X, xdt = c["X"], c["xdt"]
        zv = jnp.concatenate([z0[...], z1[...], z2[...], z3[...]], axis=1)
        sg = _sigmoid(zv)
        zs = zv * sg
        yz = Y * zs
        dout = do_ref[...].astype(F32)
        dyz = []
        for g in range(SSM_G):
            gs = slice(g * 512, (g + 1) * 512)
            yg = yz[:, gs]
            r = lax.rsqrt(jnp.mean(yg * yg, axis=-1, keepdims=True) + EPS)
            yn = yg * r
            dng_ref[:, gs] += jnp.sum(dout[:, gs] * yn, axis=0, keepdims=True)
            dyn = dout[:, gs] * ng_ref[:, gs]
            dyz.append(r * (dyn - yn * jnp.mean(dyn * yn, axis=-1, keepdims=True)))
        dyz = jnp.concatenate(dyz, axis=1)
        dz_ref[...] = (dyz * Y * (sg * (1.0 + zv * (1.0 - sg)))).astype(BF16)
        dY = dyz * zs
        ddsk_ref[...] += jnp.sum(dY * X, axis=0, keepdims=True)
        dX = dY * dsk_ref[...]
        lane = lax.broadcasted_iota(jnp.int32, (128, 128), 1)
        sub = lax.broadcasted_iota(jnp.int32, (128, 128), 0)
        colform = jnp.zeros((128, 128), F32)
        rowform = jnp.zeros((128, 128), F32)
        dxdt, gacsE, dBC = [], [], []
        for g in range(SSM_G):
            gs = slice(g * 512, (g + 1) * 512)
            G = groups[g]
            Bb, Cb, CB, Sg = G["Bb"], G["Cb"], G["CB"], G["Sg"]
            dYg = dY[:, gs]
            dQ = (dYg * c["eE"][:, gs]).astype(BF16)
            dSn = dstate[:, gs]
            dSnb = dSn.astype(BF16)
            cd = c["cdE"][:, gs]
            dC = _dot_nt(dQ, Sg.astype(BF16))
            dSprev = _dot_tn(Cb, dQ) + dSn * cd
            t1 = jnp.broadcast_to(jnp.sum(dSn * Sg * cd, axis=0, keepdims=True), (8, 512))
            colform = colform + jnp.where(sub == 127, _dot01(t1, ETm[gs, :], 2)[0:1, :], 0.0)
            Zg = xdt[:, gs] * c["wE"][:, gs]
            dZ = _dot(Bb, dSnb)
            dB = _dot_nt(Zg.astype(BF16), dSnb)
            U = dZ * Zg
            ga = dYg * G["yoff"] - U
            ga = ga + jnp.where(lax.broadcasted_iota(jnp.int32, (128, 512), 0) == 127, jnp.sum(U, axis=0, keepdims=True), 0.0)
            gacsE.append(ga)
            dxg = [None] * 4
            dCB = jnp.zeros((128, 128), F32)
            for j in range(4):
                h0 = g * 8 + 2 * j
                L0, L1, M, BD = G["pairs"][j]
                dYp = dYg[:, j * 128:(j + 1) * 128].astype(BF16)
                dM = _dot_nt(dYp, BD)
                dBD = _dot_tn(M, dYp)
                dxg[j] = jnp.where(lane < 64, dBD[:128], dBD[128:])
                for t, (h, L) in enumerate(((h0, L0), (h0 + 1, L1))):
                    dMh = dM[:, t * 128:(t + 1) * 128]
                    dCB = dCB + dMh * L
                    Gh = dMh * CB * L
                    colform = colform + jnp.where(lane == h, jnp.sum(Gh, axis=1, keepdims=True), 0.0)
                    rowform = rowform - jnp.where(sub == h, jnp.sum(Gh, axis=0, keepdims=True), 0.0)
            dCBb = dCB.astype(BF16)
            dC = dC + _dot(dCBb, Bb)
            dB = dB + _dot_tn(dCBb, Cb)
            dxdt.append(jnp.concatenate(dxg, axis=1) + dZ * c["wE"][:, gs])
            dBC.append((dB, dC))
            dstate[:, gs] = dSprev
        dxdt = jnp.concatenate(dxdt, axis=1)
        dX = dX + dxdt * c["dtE"]
        ddt = _dot01(dxdt * X, ETm, 2)
        dacs = colform + rowform.T + _dot01(jnp.concatenate(gacsE, axis=1), ETm, 2)
        dda = _dot01_left(triT_ref[...], dacs, 2)
        ddt = ddt + dda * c["a"]
        dalog_ref[...] += jnp.sum(dda * c["dtp"], axis=0, keepdims=True) * c["a"]
        ddtraw = jnp.where(lane < SSM_H, ddt * _sigmoid(c["pre"]), 0.0)
        dbias_ref[...] += jnp.sum(ddtraw, axis=0, keepdims=True)
        ddt_ref[...] = ddtraw.astype(BF16)
        dxa_ref[...] = jnp.concatenate([dX, dBC[0][0], dBC[1][0], dBC[0][1], dBC[1][1]], axis=1)

    p128, p1k = _full((1, 128)), _full((1, SSM_W))
    return pl.pallas_call(
        body, name=name, grid=(B, nc),
        in_specs=[xa_spec] + z_specs + [dt_spec, st_spec, do_spec, p128, p128, p1k, p1k,
                                        _full((128, SSM_W)), _full((SSM_W, 128)), _full((128, 128)), _full((128, 128))],
        out_specs=[xa_spec, tok, dt_out, p128, p128, p1k, p1k],
        out_shape=[jax.ShapeDtypeStruct((B, S, CONV_CH), F32), jax.ShapeDtypeStruct((B, S, SSM_W), BF16),
                   jax.ShapeDtypeStruct((B, S, 128), BF16), jax.ShapeDtypeStruct((1, 128), F32),
                   jax.ShapeDtypeStruct((1, 128), F32), jax.ShapeDtypeStruct((1, SSM_W), F32),
                   jax.ShapeDtypeStruct((1, SSM_W), F32)],
        scratch_shapes=[pltpu.VMEM((128, SSM_W), F32)],
        compiler_params=pltpu.CompilerParams(dimension_semantics=("arbitrary", "arbitrary")),
    )(xact, P, P, P, P, P, sprev, dcat, bias, alog, dskipE, ng, E, ET, tri, triT)


def _adamw(w, parts, m, v, name, tr=512, layer=0, prev=None):
    Ltot, R, C = w.shape
    ns = parts.shape[0]
    tr = min(tr, R)
    assert R % tr == 0 and parts.shape[1:] == (R, C)
    c1 = 1.0 / (1.0 - ADAM_B1 ** ADAM_STEP)
    c2 = 1.0 / (1.0 - ADAM_B2 ** ADAM_STEP)

    def body(w_ref, p_ref, m_ref, v_ref, *rest):
        g_ref, d_ref, mo_ref, vo_ref = rest[-4:]
        g = p_ref[0].astype(F32)
        for s in range(1, ns):
            g = g + p_ref[s].astype(F32)
        mn = ADAM_B1 * m_ref[...] + (1.0 - ADAM_B1) * g
        vn = ADAM_B2 * v_ref[...] + (1.0 - ADAM_B2) * (g * g)
        g_ref[...] = g
        mo_ref[...] = mn
        vo_ref[...] = vn
        d_ref[...] = -ADAM_LR * ((mn * c1) / (jnp.sqrt(vn * c2) + ADAM_EPS) + ADAM_WD * w_ref[...])

    blk = pl.BlockSpec((None, tr, C), lambda i: (layer, i, 0))
    extra = [] if prev is None else list(prev)
    return pl.pallas_call(
        body, name=name, grid=(R // tr,),
        in_specs=[blk, pl.BlockSpec((ns, tr, C), lambda i: (0, i, 0)), blk, blk] + [pl.BlockSpec(memory_space=pl.ANY)] * len(extra),
        out_specs=[blk] * 4, out_shape=[jax.ShapeDtypeStruct((Ltot, R, C), F32)] * 4,
        input_output_aliases={4 + k: k for k in range(len(extra))})(w, parts, m, v, *extra)


_SMALL = ("ada_b", "norm1_g", "gm_ln_g", "gm_ln_b", "gm_ws", "gm_bs", "gm_norm_g", "attn_sinks", "attn_norm_g", "conv_b",
          "dt_bias", "a_log", "d_skip", "ssm_norm_g", "norm2_g", "final_norm_g")


def _pack(arrs):
    flat = []
    for a in arrs:
        f = a.reshape(-1).astype(F32)
        flat.append(jnp.pad(f, (0, (-f.shape[0]) % 1024)))
    return jnp.concatenate(flat).reshape(-1, 128)


def _unpack(pack, like):
    out, r = [], 0
    for a in like:
        n = int(np.prod(a.shape))
        rows = (n + 1023) // 1024 * 8
        out.append(lax.slice(pack, (r, 0), (r + rows, 128)).reshape(-1)[:n].reshape(a.shape))
        r += rows
    return out


def kernel(x, c, ada_w, ada_b, norm1_g, w_in, gm_ln_g, gm_ln_b, gm_ws, gm_bs, gm_norm_g, attn_sinks, attn_norm_g, conv_w, conv_b, dt_bias, a_log, d_skip, ssm_norm_g, w_out, norm2_g, w_mlp1, w_mlp2, final_norm_g, loss_target, m_ada_w, m_ada_b, m_norm1_g, m_w_in, m_gm_ln_g, m_gm_ln_b, m_gm_ws, m_gm_bs, m_gm_norm_g, m_attn_sinks, m_attn_norm_g, m_conv_w, m_conv_b, m_dt_bias, m_a_log, m_d_skip, m_ssm_norm_g, m_w_out, m_norm2_g, m_w_mlp1, m_w_mlp2, m_final_norm_g, v_ada_w, v_ada_b, v_norm1_g, v_w_in, v_gm_ln_g, v_gm_ln_b, v_gm_ws, v_gm_bs, v_gm_norm_g, v_attn_sinks, v_attn_norm_g, v_conv_w, v_conv_b, v_dt_bias, v_a_log, v_d_skip, v_ssm_norm_g, v_w_out, v_norm2_g, v_w_mlp1, v_w_mlp2, v_final_norm_g):
    args = dict(locals())
    B, S, _ = x.shape
    T = B * S
    L = DEPTH
    me = 4 * lax.axis_index("x") + 2 * lax.axis_index("y") + lax.axis_index("c")

    gath = _gather2([c, conv_w], "ag_c")
    big = ("w_in", "w_out", "w_mlp1", "w_mlp2")
    chain = [(n, l) for l in range(L) for n in ("w_in", "w_mlp1", "w_out", "w_mlp2")]
    inflight = {}

    def start_next(order):
        if not chain:
            return jnp.zeros((8, 128), F32)
        n, l = chain.pop(0)
        sems, land_thru, token = _gather_start(zone[n, l], order, f"ag_start_{n}{l}")
        inflight[n, l] = (sems, land_thru)
        return token

    def gathered(n, l, after):
        land = _gather_wait(*inflight.pop((n, l)), after, f"ag_wait_{n}{l}")
        return _gather_finish(land, f"ag_fin_{n}{l}")

    forwarding = {}

    def arrived(n, l, after):
        land = _gather_wait(*inflight.pop((n, l)), after, f"ag_wait_{n}{l}")
        sems, land_thru, token = _forward_start(land, after, f"ag_fwd_start_{n}{l}")
        forwarding[n, l] = (sems, land_thru)
        return token

    def ready(n, l, after):
        return _forward_wait(*forwarding.pop((n, l)), after, f"ag_fwd_wait_{n}{l}")

    me1 = me.astype(jnp.int32).reshape(1)
    zone = {(n, l): _landing_zone(args[n], l, me1, f"ag_zone_{n}{l}") for n, l in chain}
    later_zones = [zone[k] for k in chain[1:]]

    tok = start_next(gath[0])
    c_all = gath[0].reshape(NDEV * B, D) + tok[0, 0]
    c_act = (c_all * jax.nn.sigmoid(c_all)).astype(BF16)
    nb_rows = c_act.shape[0]
    c_pad = jnp.pad(c_act, ((0, 128 - nb_rows), (0, 0)))
    adw = ada_w.astype(BF16)
    mod_part = jnp.stack([_mm(c_pad, adw[l], mode="nn", name=f"mod{l}", tn=768)[:nb_rows] for l in range(L)])
    mod_all = _gather_small([mod_part], "ag_mod", order=later_zones)[0]
    mod_mine = lax.dynamic_slice_in_dim(mod_all, me * B, B, axis=2)
    mod = jnp.transpose(mod_mine, (1, 2, 0, 3)).reshape(L, B, 6 * D) + ada_b[:, None, :]
    mods = [[mod[l][:, None, i * D:(i + 1) * D] for i in range(6)] for l in range(L)]

    win_g, wout_g, w1_g, w2_g = [None] * L, [None] * L, [None] * L, [None] * L

    tril = jnp.tril(jnp.ones((128, 128), F32))
    row = lambda a: a.reshape(1, -1)
    pad128 = lambda a: jnp.pad(a.reshape(1, -1), ((0, 0), (0, 128 - a.shape[-1])))
    small = []
    for l in range(L):
        wt = gm_ws[l] * tril
        small.append(dict(
            lng=row(gm_ln_g[l]), lnb=row(gm_ln_b[l]), wt=wt.astype(BF16), wtT=jnp.swapaxes(wt, 1, 2).astype(BF16),
            bsx=jnp.repeat(gm_bs[l].T, 128, axis=1), gog=row(gm_norm_g[l]), sinks=attn_sinks[l], aog=row(attn_norm_g[l]),
            bias=pad128(dt_bias[l]), alog=pad128(a_log[l]), dskE=jnp.repeat(d_skip[l], SSM_HD).reshape(1, SSM_W),
            sng=row(ssm_norm_g[l]), cb=row(conv_b[l])))
    convw_all = jnp.transpose(gath[1], (1, 2, 0, 3)).reshape(L, 4, CONV_CH)
    convw8 = jnp.pad(convw_all, ((0, 0), (0, 4), (0, 0)))

    saved = []
    xl = x
    g_in = gathered("w_in", 0, mod)
    tok = start_next(g_in)
    h = _norm_fwd(xl, row(norm1_g[0]) + tok[0, 0], mods[0][1], mods[0][0], "norm1_f0")
    for l in range(L):
        sm = small[l]
        win_g[l] = _shards_to_cols(g_in, f"w_in_cols{l}")
        P = _mm(h.reshape(T, D), win_g[l], mode="nn", name=f"proj_in{l}", tn=1536, order=tok).reshape(B, S, PW)
        cat = _gmlp_fwd(P, sm["lng"], sm["lnb"], sm["wt"], sm["bsx"], sm["gog"], f"gmlp_f{l}")
        cat = _attn_fwd(P, sm["sinks"], sm["aog"], cat, f"attn_f{l}")
        xact = _conv_fwd(P, convw8[l], sm["cb"], f"conv_f{l}")
        tok = start_next(arrived("w_mlp1", l, xact))
        cat, sprev = _ssd_fwd(xact, P, sm["bias"], sm["alog"], sm["dskE"], sm["sng"] + tok[0:1, 0:1], cat, f"ssd_f{l}")
        g_out = gathered("w_out", l, cat)
        tok = start_next(g_out)
        wout_g[l] = g_out.reshape(D, D)
        mix = _mm(cat.reshape(T, D), wout_g[l], mode="nn", name=f"proj_out{l}", order=tok).reshape(B, S, D)
        x_mid, h2 = _norm_fwd(xl, row(norm2_g[l]), mods[l][4], mods[l][3], f"norm2_f{l}", resid=(mix, mods[l][2]))
        w1_g[l] = ready("w_mlp1", l, h2)
        a_act, r_act = _mm(h2.reshape(T, D), w1_g[l], mode="nn", name=f"mlp1_{l}", out_dtypes=(BF16, BF16), col_blocked_b=True,
                           epilogue=lambda acc: (acc, jnp.square(jnp.maximum(acc, 0.0))))
        g_2 = gathered("w_mlp2", l, r_act)
        tok = start_next(g_2)
        w2_g[l] = g_2.reshape(DFF, D)
        m2 = _mm(r_act, w2_g[l], mode="nn", name=f"mlp2_{l}", order=tok, tk=4096).reshape(B, S, D)
        saved.append(dict(x_in=xl, h=h, P=P, xact=xact, sprev=sprev, cat=cat, mix=mix, x_mid=x_mid, h2=h2, a=a_act, r=r_act, m2=m2))
        if l + 1 < L:
            tok = start_next(arrived("w_in", l + 1, m2))
            xl, h = _norm_fwd(x_mid, row(norm1_g[l + 1]) + tok[0, 0], mods[l + 1][1], mods[l + 1][0], f"norm1_f{l + 1}",
                              resid=(m2, mods[l][5]))
            g_in = ready("w_in", l + 1, h)

    sv = saved[L - 1]
    nb = _norm_bwd(sv["x_mid"], row(final_norm_g), "final_b", tgt=loss_target, br=sv["m2"], gate=mods[L - 1][5], x_is_prev=True)
    loss_part, g_final = nb["loss"], nb["dg"]
    dmod, gsm, gconvw = [None] * L, [None] * L, [None] * L
    core = lax.axis_index("c").astype(jnp.int32).reshape(1)
    reducing = []

    def reduce_start(n, l, sent, after):
        p, from_sib = _pair_wait(*sent[:3], after, f"rs_pair_wait_{n}{l}")
        s, land = _pair_add(p, from_sib, core, f"rs_add_{n}{l}")
        return reduce_exchange(n, l, s, land, after)

    def reduce_exchange(n, l, s, land, order):
        sems, s_thru, land_thru, token = _chipsum_start(s, land, order, f"rs_start_{n}{l}")
        reducing.append((n, l, sems, s_thru, land_thru))
        return token

    other = 1 - core

    for l in reversed(range(L)):
        sv, sm = saved[l], small[l]
        dm2, dxo, dg2 = nb["dbr"].reshape(T, D), nb["dx"], nb["dgate"]
        da = _mm(dm2, w2_g[l], mode="nt", name=f"mlp2_dx{l}", out_dtypes=(BF16,), extras=(sv["a"],),
                 epilogue=lambda acc, a: (acc * (2.0 * jnp.maximum(a.astype(F32), 0.0)),))
        h2f = sv["h2"].reshape(T, D)
        sent2 = _sibling_start(_dw_half(sv["r"], dm2, other, axis="m", name=f"mlp2_dw_sib{l}"), da, f"rs_sib_start_w_mlp2{l}")
        dh2 = _mm(da, w1_g[l], mode="nt", name=f"mlp1_dx{l}", col_blocked_b=True, order=sent2[3], tk=4096,
                  out_dtypes=(BF16,)).reshape(B, S, D)
        from_sib = _sibling_wait(*sent2[:3], dh2, f"rs_sib_wait_w_mlp2{l}")[1]
        sent1 = _sibling_start(_dw_half(h2f, da, other, axis="n", name=f"mlp1_dw_sib{l}", order=from_sib), da,
                               f"rs_sib_start_w_mlp1{l}")
        s2, land2 = _dw_half(sv["r"], dm2, core, axis="m", name=f"mlp2_dw_own{l}", add=from_sib, order=sent1[3])
        tok = reduce_exchange("w_mlp2", l, s2, land2, da)
        nb2 = _norm_bwd(sv["x_mid"], row(norm2_g[l]) + tok[0, 0], f"norm2_b{l}", sc=mods[l][4], dh=dh2, dres=dxo, br=sv["mix"],
                        gate=mods[l][2])
        dmix = nb2["dbr"].reshape(T, D)
        from_sib = _sibling_wait(*sent1[:3], dmix, f"rs_sib_wait_w_mlp1{l}")[1]
        s1, land1 = _dw_half(h2f, da, core, axis="n", name=f"mlp1_dw_own{l}", add=from_sib)
        tok = reduce_exchange("w_mlp1", l, s1, land1, dmix)
        dcat = _mm(dmix, wout_g[l], mode="nt", name=f"proj_out_dx{l}", order=tok, out_dtypes=(BF16,)).reshape(B, S, D)
        du, dv, dlng, dlnb, dws, dbsx, dgog = _gmlp_bwd(sv["P"], dcat, sm["lng"], sm["lnb"], sm["wt"], sm["wtT"], sm["bsx"],
                                                        sm["gog"], f"gmlp_b{l}")
        dq, dk, dvv, dsink, daog = _attn_bwd(sv["P"], dcat, sm["sinks"], sm["aog"], f"attn_b{l}")
        dwo = _mm(sv["cat"].reshape(T, D), dmix, mode="tn", name=f"proj_out_dw{l}", out_dtypes=(BF16,), tk=4096,
                  order=dq).reshape(4, 2, D // NDEV, D)
        sent = _pair_start(dwo, dmix, f"rs_pair_start_w_out{l}")
        dxa, dz, ddt, dbias, dalog, ddsk, dsng = _ssd_bwd(sv["xact"], sv["P"], sv["sprev"], dcat, sm["bias"], sm["alog"],
                                                          sm["dskE"], sm["sng"] + sent[3][0:1, 0:1], f"ssd_b{l}")
        tok = reduce_start("w_out", l, sent, dxa)
        dxbc, dcw, dcb = _conv_bwd(sv["P"], dxa, convw8[l], sm["cb"] + tok[0:1, 0:1], f"conv_b{l}")
        dP = _concat_cols([du, dv, dq, dk, dvv, dz, dxbc, ddt], PW, f"dproj_cols{l}").reshape(T, PW)
        dwin = _mm(sv["h"].reshape(T, D), dP, mode="tn", name=f"proj_in_dw{l}", out_dtypes=(BF16,), tn=1536, tk=4096)
        sent = _sibling_start(dwin, dP, f"rs_sib_start_w_in{l}")
        dh = _mm(dP, win_g[l], mode="nt", name=f"proj_in_dx{l}", tk=PW, order=sent[3], out_dtypes=(BF16,)).reshape(B, S, D)
        s_in, land_in = _cols_to_my_shards(*_sibling_wait(*sent[:3], dh, f"rs_sib_wait_w_in{l}"), core, f"w_in_dshards{l}")
        tok = reduce_exchange("w_in", l, s_in, land_in, dh)
        nb = _norm_bwd(sv["x_in"], row(norm1_g[l]) + tok[0, 0], f"norm1_b{l}", sc=mods[l][1], dh=dh, dres=nb2["dx"],
                       br=saved[l - 1]["m2"] if l > 0 else None, gate=mods[l - 1][5] if l > 0 else None)
        dmod[l] = jnp.concatenate([nb["dsh"], nb["dsc"], nb2["dgate"], nb2["dsh"], nb2["dsc"], dg2], axis=-1)
        gconvw[l] = dcw[:4]
        gsm[l] = dict(
            ada_b=jnp.sum(dmod[l], axis=(0, 1)), norm1_g=nb["dg"], gm_ln_g=dlng, gm_ln_b=dlnb, gm_ws=dws,
            gm_bs=dbsx.reshape(128, GM_H, 128).sum(-1).T, gm_norm_g=dgog, attn_sinks=dsink[:, 0], attn_norm_g=daog,
            conv_b=dcb, dt_bias=dbias[0, :SSM_H], a_log=dalog[0, :SSM_H], d_skip=ddsk.reshape(SSM_H, SSM_HD).sum(-1),
            ssm_norm_g=dsng, norm2_g=nb2["dg"])
    grad_x = nb["dx"]

    big_res, after = dict.fromkeys(big), grad_x
    tile_rows = dict(w_in=256, w_out=256, w_mlp1=256, w_mlp2=128)

    def finish_reduce(n, l, sems, s_thru, land_thru, after):
        parts = _chipsum_wait(sems, s_thru, land_thru, after, f"rs_wait_{n}{l}")
        big_res[n] = _adamw(args[n], parts, args["m_" + n], args["v_" + n], f"adamw_{n}{l}", tr=tile_rows[n], layer=l,
                            prev=big_res[n])
        return big_res[n][0]

    per_layer = [n for n in _SMALL if n != "final_norm_g"]
    g_small = [jnp.stack([gsm[l][n].reshape(args[n].shape[1:]) for l in range(L)]) for n in per_layer] + [g_final.reshape(D)]
    zc = jnp.zeros((L, 4, CONV_CH), F32)
    z1 = jnp.zeros((1, 128), F32)
    gpack = _pack([loss_part] + g_small + [jnp.stack(gconvw)])
    small_zones = [_landing_zone(jnp.stack(dmod).reshape(1, L * B, 6 * D), 0, me1, "ag_zone_dmod", dtype=F32),
                   _landing_zone(gpack[None], 0, me1, "ag_zone_small", tr=gpack.shape[0], dtype=F32)]
    small_sems, small_thru, after = _gather_small_start(small_zones, grad_x, "ag_small_start")

    for item in reducing[:-1]:
        after = finish_reduce(*item, after)

    got = _gather_small_wait(small_sems, small_thru, after, "ag_small_wait")
    got = [got[0].reshape(NDEV, L, B, 6 * D), got[1]]
    like = [z1] + [args[n] for n in _SMALL] + [zc]
    packs = [_pack([z1] + [args[p + n] for n in _SMALL] + [zc]) for p in ("", "m_", "v_")]
    sres = [_unpack(p[0], like) for p in _adamw(packs[0][None], got[1], packs[1][None], packs[2][None], "adamw_small",
                                                tr=gpack.shape[0])]
    res = {n: [r[1 + i] for r in sres] for i, n in enumerate(_SMALL)}
    loss = sres[0][0][0, 0]
    gcw = lax.dynamic_slice_in_dim(sres[0][-1], me * (CONV_CH // NDEV), CONV_CH // NDEV, axis=2)

    def update(name, grads, tr):
        r = None
        for l, g in enumerate(grads):
            r = _adamw(args[name], g[None], args["m_" + name], args["v_" + name], f"adamw_{name}{l}", tr=tr, layer=l, prev=r)
        res[name] = r

    update("conv_w", [gcw[l] for l in range(L)], 4)

    dmod_all = jnp.transpose(got[0], (1, 0, 2, 3)).reshape(L, NDEV * B, 6 * D)
    dm_mine = lax.dynamic_slice_in_dim(dmod_all, me * (6 * D // NDEV), 6 * D // NDEV, axis=2)
    dm_pad = jnp.pad(dm_mine, ((0, 0), (0, 128 - nb_rows), (0, 0))).astype(BF16)
    update("ada_w", [_mm(c_pad, dm_pad[l], mode="tn", name=f"ada_dw{l}", tn=768) for l in range(L)], 256)

    finish_reduce(*reducing[-1], res["ada_w"][0])
    for n in big:
        res[n] = [a.reshape(args[n].shape) for a in big_res[n]]

    names = ['ada_w', 'ada_b', 'norm1_g', 'w_in', 'gm_ln_g', 'gm_ln_b', 'gm_ws', 'gm_bs', 'gm_norm_g', 'attn_sinks',
             'attn_norm_g', 'conv_w', 'conv_b', 'dt_bias', 'a_log', 'd_skip', 'ssm_norm_g', 'w_out', 'norm2_g', 'w_mlp1',
             'w_mlp2', 'final_norm_g']
    return (loss, grad_x, *[res[n][0] for n in names], *[res[n][1] for n in names], *[res[n][2] for n in names],
            *[res[n][3] for n in names])
```

```python
import jax
import jax.numpy as jnp
import numpy as np
from jax import lax
from jax.experimental import pallas as pl
from jax.experimental.pallas import tpu as pltpu

F32, BF16 = jnp.float32, jnp.bfloat16
MESH = pl.DeviceIdType.MESH
NDEV = 8

D = 2048
DEPTH = 2
CHUNK = 128
GM_W, GM_H = 512, 4
ATT_W, KV_W = 512, 128
SSM_W, SSM_H, SSM_HD, SSM_G = 1024, 16, 64, 2
CONV_CH = 1536
IN_W = 4368
DFF = 8192
EPS = 1e-6
NEG_INF = -1e30
GELU_K = 0.7978845608028654
GELU_C = 0.044715

_ORIG = (("u", 512), ("v", 512), ("q", 512), ("k", 128), ("vv", 128), ("z", 1024), ("xbc", 1536), ("dt", 16))
OFF = dict(u=0, v=512, q=1024, k=1536, vv=1664, z=1792, xbc=2816, dt=4352)
PW = 4608

ADAM_LR, ADAM_B1, ADAM_B2, ADAM_EPS, ADAM_WD, ADAM_STEP = 0.001, 0.9, 0.999, 1e-08, 0.01, 10


def _concat_cols(pieces, width, name, ts=256):
    B, S, _ = pieces[0].shape
    ws = [p.shape[-1] for p in pieces]
    dt = pieces[0].dtype
    n = len(pieces)

    def body(*refs):
        cols = [r[...] for r in refs[:n]]
        if width > sum(ws):
            cols.append(jnp.zeros((ts, width - sum(ws)), dt))
        refs[n][...] = jnp.concatenate(cols, axis=1)

    return pl.pallas_call(
        body, name=name, grid=(B, S // ts), in_specs=[pl.BlockSpec((None, ts, w), lambda b, i: (b, i, 0)) for w in ws],
        out_specs=pl.BlockSpec((None, ts, width), lambda b, i: (b, i, 0)), out_shape=jax.ShapeDtypeStruct((B, S, width), dt))(*pieces)


def _shards_to_cols(g, name, tr=256):
    n, R, C = g.shape

    def body(g_ref, o_ref):
        o_ref[...] = jnp.concatenate([g_ref[s] for s in range(n)] + [jnp.zeros((tr, PW - n * C), g.dtype)], axis=1)

    return pl.pallas_call(body, name=name, grid=(R // tr,), in_specs=[pl.BlockSpec((n, tr, C), lambda i: (0, i, 0))],
                          out_specs=pl.BlockSpec((tr, PW), lambda i: (i, 0)), out_shape=jax.ShapeDtypeStruct((R, PW), g.dtype))(g)


def _cols_to_my_shards(w, w_sib, core, name, tr=256):
    R, C = w.shape[0], IN_W // NDEV

    def body(core_ref, w_ref, s_ref, o_ref, o2_ref):
        x = w_ref[...].astype(F32) + s_ref[...].astype(F32)
        mine_is_odd = core_ref[0] == 1
        for q in range(4):
            blk = jnp.where(mine_is_odd, x[:, C * (2 * q + 1):C * (2 * q + 2)], x[:, C * 2 * q:C * (2 * q + 1)]).astype(o_ref.dtype)
            o_ref[q] = blk
            o2_ref[q] = blk

    row = pl.BlockSpec((tr, PW), lambda i, c: (i, 0))
    out = pl.BlockSpec((4, tr, C), lambda i, c: (0, i, 0))
    return pl.pallas_call(
        body, name=name, out_shape=[jax.ShapeDtypeStruct((4, R, C), w.dtype)] * 2,
        grid_spec=pltpu.PrefetchScalarGridSpec(num_scalar_prefetch=1, grid=(R // tr,), in_specs=[row, row], out_specs=[out, out]),
    )(core, w, w_sib)


def _sigmoid(x):
    return 0.5 * (jnp.tanh(0.5 * x) + 1.0)


def _gelu(x):
    return 0.5 * x * (1.0 + jnp.tanh(GELU_K * (x + GELU_C * x * x * x)))


def _gelu_grad(x):
    t = jnp.tanh(GELU_K * (x + GELU_C * x * x * x))
    return 0.5 * (1.0 + t) + 0.5 * x * (1.0 - t * t) * GELU_K * (1.0 + 3.0 * GELU_C * x * x)


def _dot(a, b, prec=None):
    return jnp.dot(a, b, precision=prec, preferred_element_type=F32)


def _dot_nt(a, b, prec=None):
    return lax.dot_general(a, b, (((1,), (1,)), ((), ())), precision=prec, preferred_element_type=F32)


def _dot_tn(a, b, prec=None):
    return lax.dot_general(a, b, (((0,), (0,)), ((), ())), precision=prec, preferred_element_type=F32)


def _full(shape):
    return pl.BlockSpec(shape, lambda *_: (0,) * len(shape))


_HBM = pl.BlockSpec(memory_space=pltpu.HBM)


def _me():
    return lax.axis_index("x"), lax.axis_index("y"), lax.axis_index("c")


def _peer(k):
    x, y, c = _me()
    px = 1 - x if k & 4 else x
    py = 1 - y if k & 2 else y
    pc = 1 - c if k & 1 else c
    return (px, py, pc), 4 * px + 2 * py + pc


def _gather_small(xs, name, order=()):
    n = len(xs)

    def body(*refs):
        ins, outs = refs[:n], refs[-n - 3:-3]
        send, recv, loc = refs[-3:]
        x, y, c = _me()
        me = 4 * x + 2 * y + c
        started = []
        for i in range(n):
            own = pltpu.make_async_copy(ins[i], outs[i].at[me], loc.at[i])
            own.start()
            started.append(own)
        for k in range(1, NDEV):
            dev, lin = _peer(k)
            for i in range(n):
                pltpu.make_async_remote_copy(
                    src_ref=ins[i], dst_ref=outs[i].at[me],
                    send_sem=send.at[i, k - 1], recv_sem=recv.at[i, k - 1], device_id=dev, device_id_type=MESH).start()
        for k in range(1, NDEV):
            dev, lin = _peer(k)
            for i in range(n):
                pltpu.make_async_remote_copy(
                    src_ref=ins[i], dst_ref=outs[i].at[lin],
                    send_sem=send.at[i, k - 1], recv_sem=recv.at[i, k - 1], device_id=dev, device_id_type=MESH).wait()
        for own in started:
            own.wait()

    extra = list(order)
    return pl.pallas_call(
        body, name=name, out_shape=[jax.ShapeDtypeStruct((NDEV,) + a.shape, a.dtype) for a in xs],
        in_specs=[_HBM] * n + [pl.BlockSpec(memory_space=pl.ANY)] * len(extra), out_specs=[_HBM] * n,
        scratch_shapes=[pltpu.SemaphoreType.DMA((n, NDEV - 1)), pltpu.SemaphoreType.DMA((n, NDEV - 1)),
                        pltpu.SemaphoreType.DMA((n,))],
        compiler_params=pltpu.CompilerParams(has_side_effects=True),
    )(*xs, *extra)


def _gather_small_start(lands, order, name):
    n = len(lands)

    def body(*refs):
        ins, sems, token = refs[:n], refs[n + 1:n + 1 + 14 * n], refs[-1]
        x, y, c = _me()
        me = 4 * x + 2 * y + c
        for i in range(n):
            for k in range(1, NDEV):
                dev, _ = _peer(k)
                pltpu.make_async_remote_copy(src_ref=ins[i].at[me], dst_ref=ins[i].at[me], send_sem=sems[14 * i + k - 1],
                                             recv_sem=sems[14 * i + 7 + k - 1], device_id=dev, device_id_type=MESH).start()
        token[...] = jnp.zeros_like(token)

    outs = pl.pallas_call(
        body, name=name,
        out_shape=(pltpu.SemaphoreType.DMA(()),) * (14 * n) + tuple(pltpu.HBM(a.shape, a.dtype) for a in lands)
        + (jax.ShapeDtypeStruct((8, 128), F32),),
        in_specs=(_HBM,) * n + (_ANY,), out_specs=(_SEM,) * (14 * n) + (_HBM,) * n + (pl.BlockSpec(memory_space=pltpu.VMEM),),
        input_output_aliases={i: 14 * n + i for i in range(n)}, compiler_params=pltpu.CompilerParams(has_side_effects=_DATAFLOW),
    )(*[_hbm(a) for a in lands], order)
    return outs[:14 * n], outs[14 * n:15 * n], outs[-1]


def _gather_small_wait(sems, lands_thru, after, name):
    n = len(lands_thru)

    def body(*refs):
        ins, sems_ = refs[:n], refs[n:n + 14 * n]
        x, y, c = _me()
        me = 4 * x + 2 * y + c
        for i in range(n):
            for k in range(1, NDEV):
                dev, lin = _peer(k)
                cp = pltpu.make_async_remote_copy(src_ref=ins[i].at[me], dst_ref=ins[i].at[lin], send_sem=sems_[14 * i + k - 1],
                                                  recv_sem=sems_[14 * i + 7 + k - 1], device_id=dev, device_id_type=MESH)
                cp.wait_send()
                cp.wait_recv()

    outs = pl.pallas_call(
        body, name=name, out_shape=tuple(pltpu.HBM(a.shape, a.dtype) for a in lands_thru),
        in_specs=(_HBM,) * n + (_SEM,) * (14 * n) + (_ANY,), out_specs=(_HBM,) * n,
        input_output_aliases={i: i for i in range(n)}, compiler_params=pltpu.CompilerParams(has_side_effects=_DATAFLOW),
    )(*lands_thru, *sems, after)
    return outs


def _chips():
    x, y, c = _me()
    return x, y, c, [(1 - x, y), (x, 1 - y), (1 - x, 1 - y)]


def _gather2(xs, name, order=None):
    n = len(xs)
    extra = [] if order is None else [order]

    def body(*refs):
        ins, outs = refs[:n], refs[-n - 3:-3]
        send, recv, loc = refs[-3:]
        x, y, c, chips = _chips()
        me, sib = (x, y, c), (x, y, 1 - c)

        def cp(i, k, block, to, src=None):
            slot = outs[i].at[4 * block[0] + 2 * block[1] + block[2]]
            return pltpu.make_async_remote_copy(src_ref=slot if src is None else src, dst_ref=slot, send_sem=send.at[i, k],
                                                recv_sem=recv.at[i, k], device_id=to, device_id_type=MESH)

        sent = []
        for i in range(n):
            for j, chip in enumerate(chips):
                sent.append(cp(i, 1 + j, me, (*chip, c), src=ins[i]))
            sent.append(cp(i, 0, me, sib, src=ins[i]))
        for s in sent:
            s.start()
        own = [pltpu.make_async_copy(ins[i], outs[i].at[4 * x + 2 * y + c], loc.at[i]) for i in range(n)]
        for o in own:
            o.start()
        for j, chip in enumerate(chips):
            for i in range(n):
                cp(i, 1 + j, (*chip, c), me).wait_recv()
                fwd = cp(i, 4 + j, (*chip, c), sib)
                fwd.start()
                sent.append(fwd)
        for i in range(n):
            cp(i, 0, sib, me).wait_recv()
            for j, chip in enumerate(chips):
                cp(i, 4 + j, (*chip, 1 - c), me).wait_recv()
        for s in sent:
            s.wait_send()
        for o in own:
            o.wait()

    return pl.pallas_call(
        body, name=name, out_shape=[jax.ShapeDtypeStruct((NDEV,) + a.shape, a.dtype) for a in xs],
        in_specs=[_HBM] * n + [pl.BlockSpec(memory_space=pl.ANY)] * len(extra), out_specs=[_HBM] * n,
        scratch_shapes=[pltpu.SemaphoreType.DMA((n, 7)), pltpu.SemaphoreType.DMA((n, 7)), pltpu.SemaphoreType.DMA((n,))],
        compiler_params=pltpu.CompilerParams(has_side_effects=True),
    )(*xs, *extra)


def _pair_add(p, r1, core, name, tr=256):
    _, _, R, C = p.shape
    tr = min(tr, R)

    def body(core_ref, p_ref, r_ref, o_ref, o2_ref):
        s = (p_ref[...].astype(F32) + r_ref[...].astype(F32)).astype(o_ref.dtype)
        o_ref[...] = s
        o2_ref[...] = s

    blk = pl.BlockSpec((None, tr, C), lambda ch, i, core_ref: (ch, i, 0))
    return pl.pallas_call(
        body, name=name, out_shape=[jax.ShapeDtypeStruct((4, R, C), p.dtype)] * 2,
        grid_spec=pltpu.PrefetchScalarGridSpec(
            num_scalar_prefetch=1, grid=(4, R // tr),
            in_specs=[pl.BlockSpec((None, None, tr, C), lambda ch, i, core_ref: (ch, core_ref[0], i, 0)), blk],
            out_specs=[blk, blk]),
    )(core, p, r1)


_SEM = pl.BlockSpec(memory_space=pltpu.SEMAPHORE)
_ANY = pl.BlockSpec(memory_space=pl.ANY)
_DATAFLOW = pltpu.SideEffectType.DATAFLOW_SIDE_EFFECTING


def _hbm(a):
    return pltpu.with_memory_space_constraint(a, pltpu.HBM)


def _gather_targets():
    x, y, c, chips = _chips()
    return 4 * x + 2 * y + c, [(x, y, 1 - c)] + [(*chip, c) for chip in chips]


def _landing_zone(w, l, me, name, tr=512, dtype=BF16):
    _, R, C = w.shape
    tr = min(tr, R)

    def body(me_ref, w_ref, o_ref):
        o_ref[...] = w_ref[...].astype(dtype)

    return pl.pallas_call(
        body, name=name, out_shape=jax.ShapeDtypeStruct((NDEV, R, C), dtype),
        grid_spec=pltpu.PrefetchScalarGridSpec(
            num_scalar_prefetch=1, grid=(R // tr,), in_specs=[pl.BlockSpec((None, tr, C), lambda i, me_ref: (l, i, 0))],
            out_specs=pl.BlockSpec((None, tr, C), lambda i, me_ref: (me_ref[0], i, 0))),
    )(me, w)


def _gather_start(land, order, name):
    def body(land_ref, order_ref, *rest):
        sems, token = rest[:8], rest[9]
        me, targets = _gather_targets()
        for k, to in enumerate(targets):
            pltpu.make_async_remote_copy(src_ref=land_ref.at[me], dst_ref=land_ref.at[me], send_sem=sems[k],
                                         recv_sem=sems[4 + k], device_id=to, device_id_type=MESH).start()
        token[...] = jnp.zeros_like(token)

    outs = pl.pallas_call(
        body, name=name,
        out_shape=(pltpu.SemaphoreType.DMA(()),) * 8 + (pltpu.HBM(land.shape, land.dtype), jax.ShapeDtypeStruct((8, 128), F32)),
        in_specs=(_HBM, _ANY), out_specs=(_SEM,) * 8 + (_HBM, pl.BlockSpec(memory_space=pltpu.VMEM)),
        input_output_aliases={0: 8}, compiler_params=pltpu.CompilerParams(has_side_effects=_DATAFLOW),
    )(_hbm(land), order)
    return outs[:8], outs[8], outs[9]


def _gather_wait(sems, land_thru, after, name):
    def body(land_ref, *rest):
        sems_ = rest[:8]
        me, targets = _gather_targets()
        for k, to in enumerate(targets):
            cp = pltpu.make_async_remote_copy(src_ref=land_ref.at[me], dst_ref=land_ref.at[me], send_sem=sems_[k],
                                              recv_sem=sems_[4 + k], device_id=to, device_id_type=MESH)
            cp.wait_send()
            cp.wait_recv()

    return pl.pallas_call(
        body, name=name, out_shape=pltpu.HBM(land_thru.shape, land_thru.dtype),
        in_specs=(_HBM,) + (_SEM,) * 8 + (_ANY,), out_specs=_HBM, input_output_aliases={0: 0},
        compiler_params=pltpu.CompilerParams(has_side_effects=_DATAFLOW),
    )(land_thru, *sems, after)


def _gather_finish(land, name):
    def body(land_ref, out, send, recv):
        x, y, c, chips = _chips()
        fwd = [pltpu.make_async_remote_copy(src_ref=out.at[4 * px + 2 * py + c], dst_ref=out.at[4 * px + 2 * py + c],
                                            send_sem=send.at[j], recv_sem=recv.at[j], device_id=(x, y, 1 - c), device_id_type=MESH)
               for j, (px, py) in enumerate(chips)]
        for cp in fwd:
            cp.start()
        for j, (px, py) in enumerate(chips):
            slot = out.at[4 * px + 2 * py + 1 - c]
            pltpu.make_async_remote_copy(src_ref=slot, dst_ref=slot, send_sem=send.at[j], recv_sem=recv.at[j],
                                         device_id=(x, y, 1 - c), device_id_type=MESH).wait()

    return pl.pallas_call(
        body, name=name, out_shape=jax.ShapeDtypeStruct(land.shape, land.dtype),
        in_specs=[_HBM], out_specs=_HBM, input_output_aliases={0: 0},
        scratch_shapes=[pltpu.SemaphoreType.DMA((3,)), pltpu.SemaphoreType.DMA((3,))],
        compiler_params=pltpu.CompilerParams(has_side_effects=True),
    )(land)


def _forward_start(land, order, name):
    def body(land_ref, order_ref, *rest):
        sems, token = rest[:6], rest[7]
        x, y, c, chips = _chips()
        for j, (px, py) in enumerate(chips):
            slot = land_ref.at[4 * px + 2 * py + c]
            pltpu.make_async_remote_copy(src_ref=slot, dst_ref=slot, send_sem=sems[j], recv_sem=sems[3 + j],
                                         device_id=(x, y, 1 - c), device_id_type=MESH).start()
        token[...] = jnp.zeros_like(token)

    outs = pl.pallas_call(
        body, name=name,
        out_shape=(pltpu.SemaphoreType.DMA(()),) * 6 + (pltpu.HBM(land.shape, land.dtype), jax.ShapeDtypeStruct((8, 128), F32)),
        in_specs=(_HBM, _ANY), out_specs=(_SEM,) * 6 + (_HBM, pl.BlockSpec(memory_space=pltpu.VMEM)),
        input_output_aliases={0: 6}, compiler_params=pltpu.CompilerParams(has_side_effects=_DATAFLOW),
    )(_hbm(land), order)
    return outs[:6], outs[6], outs[7]


def _forward_wait(sems, land_thru, after, name):
    def body(land_ref, *rest):
        sems_ = rest[:6]
        x, y, c, chips = _chips()
        for j, (px, py) in enumerate(chips):
            cp = pltpu.make_async_remote_copy(src_ref=land_ref.at[4 * px + 2 * py + c], dst_ref=land_ref.at[4 * px + 2 * py + 1 - c],
                                              send_sem=sems_[j], recv_sem=sems_[3 + j], device_id=(x, y, 1 - c),
                                              device_id_type=MESH)
            cp.wait_send()
            cp.wait_recv()

    return pl.pallas_call(
        body, name=name, out_shape=pltpu.HBM(land_thru.shape, land_thru.dtype),
        in_specs=(_HBM,) + (_SEM,) * 6 + (_ANY,), out_specs=_HBM, input_output_aliases={0: 0},
        compiler_params=pltpu.CompilerParams(has_side_effects=_DATAFLOW),
    )(land_thru, *sems, after)


def _chip_targets():
    x, y, c, chips = _chips()
    return 2 * x + y, [((px, py, c), 2 * px + py) for px, py in chips]


def _chipsum_start(s, land, order, name):
    def body(s_ref, land_ref, order_ref, *rest):
        sems, token = rest[:6], rest[8]
        mine, targets = _chip_targets()
        for k, (to, ch) in enumerate(targets):
            pltpu.make_async_remote_copy(src_ref=s_ref.at[ch], dst_ref=land_ref.at[mine], send_sem=sems[k], recv_sem=sems[3 + k],
                                         device_id=to, device_id_type=MESH).start()
        token[...] = jnp.zeros_like(token)

    outs = pl.pallas_call(
        body, name=name,
        out_shape=(pltpu.SemaphoreType.DMA(()),) * 6 + (pltpu.HBM(s.shape, s.dtype), pltpu.HBM(land.shape, land.dtype),
                                                        jax.ShapeDtypeStruct((8, 128), F32)),
        in_specs=(_HBM, _HBM, _ANY), out_specs=(_SEM,) * 6 + (_HBM, _HBM, pl.BlockSpec(memory_space=pltpu.VMEM)),
        input_output_aliases={0: 6, 1: 7}, compiler_params=pltpu.CompilerParams(has_side_effects=_DATAFLOW),
    )(_hbm(s), _hbm(land), order)
    return outs[:6], outs[6], outs[7], outs[8]


def _chipsum_wait(sems, s_thru, land_thru, after, name):
    def body(s_ref, land_ref, *rest):
        sems_ = rest[:6]
        mine, targets = _chip_targets()
        for k, (to, ch) in enumerate(targets):
            cp = pltpu.make_async_remote_copy(src_ref=s_ref.at[ch], dst_ref=land_ref.at[ch], send_sem=sems_[k], recv_sem=sems_[3 + k],
                                              device_id=to, device_id_type=MESH)
            cp.wait_send()
            cp.wait_recv()

    return pl.pallas_call(
        body, name=name, out_shape=(pltpu.HBM(s_thru.shape, s_thru.dtype), pltpu.HBM(land_thru.shape, land_thru.dtype)),
        in_specs=(_HBM, _HBM) + (_SEM,) * 6 + (_ANY,), out_specs=(_HBM, _HBM), input_output_aliases={0: 0, 1: 1},
        compiler_params=pltpu.CompilerParams(has_side_effects=_DATAFLOW),
    )(s_thru, land_thru, *sems, after)[1]


def _pair_start(p, order, name):
    def body(p_ref, land_ref, order_ref, *rest):
        sems, token = rest[:8], rest[10]
        x, y, c = _me()
        for ch in range(4):
            pltpu.make_async_remote_copy(src_ref=p_ref.at[ch, 1 - c], dst_ref=land_ref.at[ch], send_sem=sems[ch],
                                         recv_sem=sems[4 + ch], device_id=(x, y, 1 - c), device_id_type=MESH).start()
        token[...] = jnp.zeros_like(token)

    land = lax.empty((4,) + p.shape[2:], p.dtype)
    outs = pl.pallas_call(
        body, name=name,
        out_shape=(pltpu.SemaphoreType.DMA(()),) * 8 + (pltpu.HBM(p.shape, p.dtype), pltpu.HBM(land.shape, land.dtype),
                                                        jax.ShapeDtypeStruct((8, 128), F32)),
        in_specs=(_HBM, _HBM, _ANY), out_specs=(_SEM,) * 8 + (_HBM, _HBM, pl.BlockSpec(memory_space=pltpu.VMEM)),
        input_output_aliases={0: 8, 1: 9}, compiler_params=pltpu.CompilerParams(has_side_effects=_DATAFLOW),
    )(_hbm(p), _hbm(land), order)
    return outs[:8], outs[8], outs[9], outs[10]


def _pair_wait(sems, p_thru, land_thru, after, name):
    def body(p_ref, land_ref, *rest):
        sems_ = rest[:8]
        x, y, c = _me()
        for ch in range(4):
            cp = pltpu.make_async_remote_copy(src_ref=p_ref.at[ch, 1 - c], dst_ref=land_ref.at[ch], send_sem=sems_[ch],
                                              recv_sem=sems_[4 + ch], device_id=(x, y, 1 - c), device_id_type=MESH)
            cp.wait_send()
            cp.wait_recv()

    return pl.pallas_call(
        body, name=name, out_shape=(pltpu.HBM(p_thru.shape, p_thru.dtype), pltpu.HBM(land_thru.shape, land_thru.dtype)),
        in_specs=(_HBM, _HBM) + (_SEM,) * 8 + (_ANY,), out_specs=(_HBM, _HBM), input_output_aliases={0: 0, 1: 1},
        compiler_params=pltpu.CompilerParams(has_side_effects=_DATAFLOW),
    )(p_thru, land_thru, *sems, after)


def _sibling_start(p, order, name):
    def body(p_ref, land_ref, order_ref, send_sem, recv_sem, p_thru, land_thru, token):
        x, y, c = _me()
        pltpu.make_async_remote_copy(src_ref=p_ref, dst_ref=land_ref, send_sem=send_sem, recv_sem=recv_sem,
                                     device_id=(x, y, 1 - c), device_id_type=MESH).start()
        token[...] = jnp.zeros_like(token)

    land = lax.empty(p.shape, p.dtype)
    outs = pl.pallas_call(
        body, name=name,
        out_shape=(pltpu.SemaphoreType.DMA(()),) * 2 + (pltpu.HBM(p.shape, p.dtype), pltpu.HBM(p.shape, p.dtype),
                                                        jax.ShapeDtypeStruct((8, 128), F32)),
        in_specs=(_HBM, _HBM, _ANY), out_specs=(_SEM,) * 2 + (_HBM, _HBM, pl.BlockSpec(memory_space=pltpu.VMEM)),
        input_output_aliases={0: 2, 1: 3}, compiler_params=pltpu.CompilerParams(has_side_effects=_DATAFLOW),
    )(_hbm(p), _hbm(land), order)
    return outs[:2], outs[2], outs[3], outs[4]


def _sibling_wait(sems, p_thru, land_thru, after, name):
    def body(p_ref, land_ref, send_sem, recv_sem, after_ref, p_dead, got_ref):
        x, y, c = _me()
        cp = pltpu.make_async_remote_copy(src_ref=p_ref, dst_ref=land_ref, send_sem=send_sem, recv_sem=recv_sem,
                                          device_id=(x, y, 1 - c), device_id_type=MESH)
        cp.wait_send()
        cp.wait_recv()

    return pl.pallas_call(
        body, name=name, out_shape=(pltpu.HBM(p_thru.shape, p_thru.dtype), pltpu.HBM(land_thru.shape, land_thru.dtype)),
        in_specs=(_HBM, _HBM, _SEM, _SEM, _ANY), out_specs=(_HBM, _HBM), input_output_aliases={0: 0, 1: 1},
        compiler_params=pltpu.CompilerParams(has_side_effects=_DATAFLOW),
    )(p_thru, land_thru, *sems, after)


def _mm(a, b, *, mode, name, out_dtypes=(F32,), epilogue=None, extras=(), tm=1024, tn=1024, tk=2048,
        col_blocked_b=False, col_blocked_out=False, order=None):
    CB = 1024
    if col_blocked_b:
        assert mode in ("nn", "nt") and b.shape[2] == CB
        (M, K), N = a.shape, (b.shape[0] * CB if mode == "nn" else b.shape[1])
        assert mode == "nn" or tk % CB == 0
        tn = CB if mode == "nn" else tn
    elif mode == "nn":
        (M, K), N = a.shape, b.shape[1]
    elif mode == "nt":
        (M, K), N = a.shape, b.shape[0]
    else:
        (K, M), N = a.shape, b.shape[1]
    if col_blocked_out:
        assert len(out_dtypes) == 1 and N % CB == 0
        tn = CB
    tm, tn, tk = min(tm, M), min(tn, N), min(tk, K)
    assert M % tm == 0 and N % tn == 0 and K % tk == 0, (M, N, K, tm, tn, tk)
    nk = K // tk
    ne, no = len(extras), len(out_dtypes)
    dims = {"nn": (((1,), (0,)), ((), ())), "nt": (((1,), (1,)), ((), ())), "tn": (((0,), (0,)), ((), ()))}[mode]

    no_ = 0 if order is None else 1

    def body(a_ref, b_ref, *rest):
        rest = rest[no_:]
        ex, outs = rest[:ne], rest[ne:ne + no]

        def finish(acc):
            res = epilogue(acc, *[e[...] for e in ex]) if epilogue is not None else (acc,)
            for o, r in zip(outs, res):
                o[...] = r.astype(o.dtype)

        if col_blocked_b and mode == "nt":
            part = sum(lax.dot_general(a_ref[:, q * CB:(q + 1) * CB], b_ref[q], dims, preferred_element_type=F32)
                       for q in range(tk // CB))
        else:
            part = lax.dot_general(a_ref[...], b_ref[...].astype(BF16), dims, preferred_element_type=F32)
        if nk == 1:
            finish(part)
        else:
            acc_ref = rest[-1]
            k = pl.program_id(2)

            @pl.when(k == 0)
            def _():
                acc_ref[...] = part

            @pl.when(k > 0)
            def _():
                acc_ref[...] += part

            @pl.when(k == nk - 1)
            def _():
                finish(acc_ref[...])

    a_spec = {"nn": pl.BlockSpec((tm, tk), lambda i, j, k: (i, k)), "nt": pl.BlockSpec((tm, tk), lambda i, j, k: (i, k)),
              "tn": pl.BlockSpec((tk, tm), lambda i, j, k: (k, i))}[mode]
    b_spec = {"nn": pl.BlockSpec((tk, tn), lambda i, j, k: (k, j)), "nt": pl.BlockSpec((tn, tk), lambda i, j, k: (j, k)),
              "tn": pl.BlockSpec((tk, tn), lambda i, j, k: (k, j))}[mode]
    if col_blocked_b:
        b_spec = (pl.BlockSpec((None, tk, CB), lambda i, j, k: (j, k, 0)) if mode == "nn"
                  else pl.BlockSpec((tk // CB, tn, CB), lambda i, j, k: (k, j, 0)))
    e_spec = pl.BlockSpec((tm, tn), lambda i, j, k: (i, j))
    o_spec, o_dims = e_spec, (M, N)
    if col_blocked_out:
        o_spec, o_dims = pl.BlockSpec((None, tm, CB), lambda i, j, k: (j, i, 0)), (N // CB, M, CB)
    outs = pl.pallas_call(
        body, name=name, grid=(M // tm, N // tn, nk),
        in_specs=[a_spec, b_spec] + [_ANY] * no_ + [e_spec] * ne, out_specs=[o_spec] * no,
        out_shape=[jax.ShapeDtypeStruct(o_dims, dt) for dt in out_dtypes],
        scratch_shapes=[pltpu.VMEM((tm, tn), F32)] if nk > 1 else [],
        compiler_params=pltpu.CompilerParams(dimension_semantics=("parallel", "parallel", "arbitrary")),
    )(a, b, *([] if order is None else [order]), *extras)
    return outs if no > 1 else outs[0]


def _dw_half(a, b, side, *, axis, name, add=None, order=None, tile=1024, tk=4096):
    (K, M), N = a.shape, b.shape[1]
    tk = min(tk, K)
    nk = K // tk
    if axis == "m":
        tm, tn = tile, min(N, 1024)
        grid, o_dims = (4, N // tn, nk), (4, tile, N)
        a_spec = pl.BlockSpec((tk, tm), lambda q, j, k, s: (k, 2 * q + s[0]))
        b_spec = pl.BlockSpec((tk, tn), lambda q, j, k, s: (k, j))
        o_spec = pl.BlockSpec((None, tm, tn), lambda q, j, k, s: (q, 0, j))
    else:
        tm, tn = min(M, 1024), tile
        grid, o_dims = (M // tm, 4, nk), (4, M, tile)
        a_spec = pl.BlockSpec((tk, tm), lambda i, q, k, s: (k, i))
        b_spec = pl.BlockSpec((tk, tn), lambda i, q, k, s: (k, 2 * q + s[0]))
        o_spec = pl.BlockSpec((None, tm, tn), lambda i, q, k, s: (q, i, 0))
    n_order, n_add = int(order is not None), int(add is not None)
    n_out = 1 + n_add

    def body(s_ref, a_ref, b_ref, *rest):
        rest = rest[n_order:]
        outs = rest[n_add:n_add + n_out]

        def finish(acc):
            res = acc + rest[0][...].astype(F32) if n_add else acc
            for o in outs:
                o[...] = res.astype(o.dtype)

        part = _dot_tn(a_ref[...], b_ref[...])
        if nk == 1:
            finish(part)
        else:
            acc_ref, k = rest[-1], pl.program_id(2)

            @pl.when(k == 0)
            def _():
                acc_ref[...] = part

            @pl.when(k > 0)
            def _():
                acc_ref[...] += part

            @pl.when(k == nk - 1)
            def _():
                finish(acc_ref[...])

    outs = pl.pallas_call(
        body, name=name, out_shape=[jax.ShapeDtypeStruct(o_dims, BF16)] * n_out,
        grid_spec=pltpu.PrefetchScalarGridSpec(
            num_scalar_prefetch=1, grid=grid, in_specs=[a_spec, b_spec] + [_ANY] * n_order + [o_spec] * n_add,
            out_specs=[o_spec] * n_out, scratch_shapes=[pltpu.VMEM((tm, tn), F32)] if nk > 1 else []),
        compiler_params=pltpu.CompilerParams(dimension_semantics=("arbitrary", "arbitrary", "arbitrary")),
    )(side, a, b, *([order] if n_order else []), *([add] if n_add else []))
    return outs if n_add else outs[0]


def _norm_fwd(x, g, sc, sh, name, resid=None):
    B, S, Dm = x.shape
    ts = min(S, 256)
    tok = pl.BlockSpec((None, ts, Dm), lambda b, i: (b, i, 0))
    row = pl.BlockSpec((None, 1, Dm), lambda b, i: (b, 0, 0))
    par = pl.BlockSpec((1, Dm), lambda b, i: (0, 0))

    def body(*refs):
        if resid is not None:
            x_ref, br_ref, gt_ref, g_ref, sc_ref, sh_ref, xo_ref, h_ref = refs
            xv = x_ref[...] + gt_ref[...] * br_ref[...]
            xo_ref[...] = xv
        else:
            x_ref, g_ref, sc_ref, sh_ref, h_ref = refs
            xv = x_ref[...]
        r = lax.rsqrt(jnp.mean(xv * xv, axis=-1, keepdims=True) + EPS)
        h_ref[...] = ((xv * r * g_ref[...]) * (1.0 + sc_ref[...]) + sh_ref[...]).astype(BF16)

    h_shape = jax.ShapeDtypeStruct((B, S, Dm), BF16)
    if resid is not None:
        return pl.pallas_call(body, name=name, grid=(B, S // ts), in_specs=[tok, tok, row, par, row, row],
                              out_specs=[tok, tok], out_shape=[jax.ShapeDtypeStruct((B, S, Dm), F32), h_shape],
                              )(x, resid[0], resid[1], g, sc, sh)
    return pl.pallas_call(body, name=name, grid=(B, S // ts), in_specs=[tok, par, row, row], out_specs=tok,
                          out_shape=h_shape)(x, g, sc, sh)


def _norm_bwd(x, g, name, *, sc=None, dh=None, dres=None, tgt=None, br=None, gate=None, x_is_prev=False):
    B, S, Dm = x.shape
    ts = min(S, 256)
    final = tgt is not None
    has_br = br is not None
    tok = pl.BlockSpec((None, ts, Dm), lambda b, i: (b, i, 0))
    row = pl.BlockSpec((None, 1, Dm), lambda b, i: (b, 0, 0))
    par = pl.BlockSpec((1, Dm), lambda b, i: (0, 0))
    ins, in_specs = [x, g], [tok, par]
    if final:
        ins, in_specs = ins + [tgt], in_specs + [tok]
    else:
        ins, in_specs = ins + [sc, dh], in_specs + [row, tok]
    if dres is not None:
        ins, in_specs = ins + [dres], in_specs + [tok]
    if has_br:
        ins, in_specs = ins + [br, gate], in_specs + [tok, row]
    n_in = len(ins)
    out_shape = [jax.ShapeDtypeStruct((B, S, Dm), F32), jax.ShapeDtypeStruct((1, Dm), F32)]
    out_specs = [tok, par]
    if final:
        out_shape.append(jax.ShapeDtypeStruct((1, 128), F32))
        out_specs.append(pl.BlockSpec((1, 128), lambda b, i: (0, 0)))
    else:
        out_shape += [jax.ShapeDtypeStruct((B, 1, Dm), F32)] * 2
        out_specs += [row, row]
    if has_br:
        out_shape += [jax.ShapeDtypeStruct((B, S, Dm), BF16), jax.ShapeDtypeStruct((B, 1, Dm), F32)]
        out_specs += [tok, row]

    def body(*refs):
        it = iter(refs[:n_in])
        outs = iter(refs[n_in:])
        x_ref, g_ref = next(it), next(it)
        b, i = pl.program_id(0), pl.program_id(1)
        first, first_row = (b == 0) & (i == 0), i == 0
        xv, gv = x_ref[...], g_ref[...]
        if x_is_prev:
            xv = xv + refs[n_in - 1][...] * refs[n_in - 2][...]
        r = lax.rsqrt(jnp.mean(xv * xv, axis=-1, keepdims=True) + EPS)
        n = xv * r
        dx_ref, dg_ref = next(outs), next(outs)

        def acc(ref, val, init):
            @pl.when(init)
            def _():
                ref[...] = val

            @pl.when(jnp.logical_not(init))
            def _():
                ref[...] += val

        if final:
            t_ref = next(it)
            loss_ref = next(outs)
            e = n * gv - t_ref[...]
            acc(loss_ref, jnp.zeros((1, 128), F32) + 0.5 * jnp.sum(e * e) / Dm, first)
            dyg = e * (1.0 / Dm)
        else:
            sc_ref, dh_ref = next(it), next(it)
            dsc_ref, dsh_ref = next(outs), next(outs)
            dhv = dh_ref[...].astype(F32)
            acc(dsh_ref, jnp.sum(dhv, axis=0, keepdims=True), first_row)
            acc(dsc_ref, jnp.sum(dhv * (n * gv), axis=0, keepdims=True), first_row)
            dyg = dhv * (1.0 + sc_ref[...])
        acc(dg_ref, jnp.sum(dyg * n, axis=0, keepdims=True), first)
        dn = dyg * gv
        dx = r * (dn - n * jnp.mean(dn * n, axis=-1, keepdims=True))
        if dres is not None:
            dx = dx + next(it)[...]
        dx_ref[...] = dx
        if has_br:
            br_ref, gt_ref = next(it), next(it)
            dbr_ref, dgt_ref = next(outs), next(outs)
            dbr_ref[...] = (dx * gt_ref[...]).astype(BF16)
            acc(dgt_ref, jnp.sum(dx * br_ref[...], axis=0, keepdims=True), first_row)

    outs = pl.pallas_call(body, name=name, grid=(B, S // ts), in_specs=in_specs, out_specs=out_specs, out_shape=out_shape,
                          compiler_params=pltpu.CompilerParams(dimension_semantics=("arbitrary", "arbitrary")))(*ins)
    res = dict(dx=outs[0], dg=outs[1])
    if final:
        res["loss"] = outs[2]
    else:
        res["dsc"], res["dsh"] = outs[2], outs[3]
    if has_br:
        res["dbr"], res["dgate"] = outs[-2], outs[-1]
    return res


def _gm_heads(vg, lng, lnb):
    res = []
    for h in range(GM_H):
        sl = slice(h * 128, (h + 1) * 128)
        vh = vg[:, sl]
        xc = vh - jnp.mean(vh, axis=-1, keepdims=True)
        rstd = lax.rsqrt(jnp.mean(xc * xc, axis=-1, keepdims=True) + 1e-5)
        xhat = xc * rstd
        res.append((xhat, rstd, xhat * lng[:, sl] + lnb[:, sl]))
    return res


def _gm_gate(heads, wt_ref, bsx, nch):
    cols = []
    for h in range(GM_H):
        vn = heads[h][2].astype(BF16)
        rows = [_dot(wt_ref[h], vn[c * CHUNK:(c + 1) * CHUNK]) + bsx[:, h * 128:(h + 1) * 128] for c in range(nch)]
        cols.append(jnp.concatenate(rows, axis=0) if nch > 1 else rows[0])
    return jnp.concatenate(cols, axis=1)


def _gm_specs(S):
    tb = min(S, 512)
    u = pl.BlockSpec((None, tb, GM_W), lambda b, i: (b, i, OFF["u"] // GM_W))
    v = pl.BlockSpec((None, tb, GM_W), lambda b, i: (b, i, OFF["v"] // GM_W))
    tok = pl.BlockSpec((None, tb, GM_W), lambda b, i: (b, i, 0))
    return tb, u, v, tok


def _gmlp_fwd(P, lng, lnb, wt, bsx, og, name):
    B, S, _ = P.shape
    tb, u_spec, v_spec, tok = _gm_specs(S)
    nch = tb // CHUNK

    def body(u_ref, v_ref, lng_ref, lnb_ref, wt_ref, bsx_ref, og_ref, o_ref):
        heads = _gm_heads(_gelu(v_ref[...]), lng_ref[...], lnb_ref[...])
        y = _gelu(u_ref[...]) * _gm_gate(heads, wt_ref, bsx_ref[...], nch)
        r = lax.rsqrt(jnp.mean(y * y, axis=-1, keepdims=True) + EPS)
        o_ref[...] = (y * r * og_ref[...]).astype(BF16)

    return pl.pallas_call(
        body, name=name, grid=(B, S // tb),
        in_specs=[u_spec, v_spec, _full((1, GM_W)), _full((1, GM_W)), _full((GM_H, 128, 128)), _full((128, GM_W)), _full((1, GM_W))],
        out_specs=tok, out_shape=jax.ShapeDtypeStruct((B, S, GM_W + ATT_W + SSM_W), BF16))(P, P, lng, lnb, wt, bsx, og)


def _gmlp_bwd(P, dcat, lng, lnb, wt, wtT, bsx, og, name):
    B, S, _ = P.shape
    tb, u_spec, v_spec, tok = _gm_specs(S)
    nch = tb // CHUNK
    do_spec = pl.BlockSpec((None, tb, GM_W), lambda b, i: (b, i, 0))

    def body(u_ref, v_ref, do_ref, lng_ref, lnb_ref, wt_ref, wtT_ref, bsx_ref, og_ref,
             du_ref, dv_ref, dlng_ref, dlnb_ref, dws_ref, dbsx_ref, dog_ref):
        first = (pl.program_id(0) == 0) & (pl.program_id(1) == 0)

        @pl.when(first)
        def _():
            for ref in (dlng_ref, dlnb_ref, dws_ref, dbsx_ref, dog_ref):
                ref[...] = jnp.zeros(ref.shape, F32)

        u, v, lng = u_ref[...], v_ref[...], lng_ref[...]
        ug = _gelu(u)
        heads = _gm_heads(_gelu(v), lng, lnb_ref[...])
        gate = _gm_gate(heads, wt_ref, bsx_ref[...], nch)
        y = ug * gate
        r = lax.rsqrt(jnp.mean(y * y, axis=-1, keepdims=True) + EPS)
        yn = y * r
        dout = do_ref[...].astype(F32)
        dog_ref[...] += jnp.sum(dout * yn, axis=0, keepdims=True)
        dyn = dout * og_ref[...]
        dy = r * (dyn - yn * jnp.mean(dyn * yn, axis=-1, keepdims=True))
        du_ref[...] = (dy * gate * _gelu_grad(u)).astype(BF16)
        dgate = dy * ug
        tril = lax.broadcasted_iota(jnp.int32, (128, 128), 0) >= lax.broadcasted_iota(jnp.int32, (128, 128), 1)
        dvg = []
        for h in range(GM_H):
            sl = slice(h * 128, (h + 1) * 128)
            xhat, rstd, vn = heads[h]
            vnb = vn.astype(BF16)
            dgh = dgate[:, sl]
            dgb = dgh.astype(BF16)
            dbs = jnp.zeros((128, 128), F32)
            dw = jnp.zeros((128, 128), F32)
            dvn = []
            for c in range(nch):
                rs = slice(c * CHUNK, (c + 1) * CHUNK)
                dbs = dbs + dgh[rs]
                dw = dw + _dot_nt(dgb[rs], vnb[rs])
                dvn.append(_dot(wtT_ref[h], dgb[rs]))
            dvn = jnp.concatenate(dvn, axis=0) if nch > 1 else dvn[0]
            dbsx_ref[:, sl] += dbs
            dws_ref[h] += jnp.where(tril, dw, 0.0)
            dlng_ref[:, sl] += jnp.sum(dvn * xhat, axis=0, keepdims=True)
            dlnb_ref[:, sl] += jnp.sum(dvn, axis=0, keepdims=True)
            dxh = dvn * lng[:, sl]
            dvg.append(rstd * (dxh - jnp.mean(dxh, axis=-1, keepdims=True) - xhat * jnp.mean(dxh * xhat, axis=-1, keepdims=True)))
        dv_ref[...] = (jnp.concatenate(dvg, axis=1) * _gelu_grad(v)).astype(BF16)

    p512, w3 = _full((1, GM_W)), _full((GM_H, 128, 128))
    return pl.pallas_call(
        body, name=name, grid=(B, S // tb),
        in_specs=[u_spec, v_spec, do_spec, p512, p512, w3, w3, _full((128, GM_W)), p512],
        out_specs=[tok, tok, p512, p512, w3, _full((128, GM_W)), p512],
        out_shape=[jax.ShapeDtypeStruct((B, S, GM_W), BF16)] * 2 + [
            jax.ShapeDtypeStruct((1, GM_W), F32), jax.ShapeDtypeStruct((1, GM_W), F32),
            jax.ShapeDtypeStruct((GM_H, 128, 128), F32), jax.ShapeDtypeStruct((128, GM_W), F32),
            jax.ShapeDtypeStruct((1, GM_W), F32)],
        compiler_params=pltpu.CompilerParams(dimension_semantics=("arbitrary", "arbitrary")),
    )(P, P, dcat, lng, lnb, wt, wtT, bsx, og)


def _lane_half():
    return lax.broadcasted_iota(jnp.int32, (128, 128), 1) // 64


def _att_stack(x, kvh, dtype):
    half = _lane_half()
    rows = []
    for g in range(4):
        i = kvh * 4 + g
        pair = x[:, (i // 2) * 128:(i // 2 + 1) * 128]
        if i % 2 != kvh:
            pair = pltpu.roll(pair, 64, 1)
        rows.append(jnp.where(half == kvh, pair, 0.0))
    return jnp.concatenate(rows, axis=0).astype(dtype)


def _att_unstack(pairs, y, kvh):
    half = _lane_half()
    for g in range(4):
        i = kvh * 4 + g
        piece = y[g * 128:(g + 1) * 128]
        if i % 2 != kvh:
            piece = pltpu.roll(piece, 64, 1)
        pairs[i // 2] = jnp.where(half == i % 2, piece, pairs[i // 2])
    return pairs


def _att_fill_bias(bias_ref):
    qi = lax.broadcasted_iota(jnp.int32, (512, 256), 0) % 128
    kj = lax.broadcasted_iota(jnp.int32, (512, 256), 1)
    diff = qi + 128 - kj
    band = (diff >= 0) & (diff < 128)
    bias_ref[0:512, :] = jnp.where(band, 0.0, NEG_INF)
    bias_ref[512:1024, :] = jnp.where(band & (kj >= 128), 0.0, NEG_INF)


def _att_bias(bias_ref, n):
    return bias_ref[pl.ds(pl.multiple_of(jnp.where(n == 0, 512, 0), 512), 512), :]


def _att_probs(qb, k2, bias, sink_ref, kvh):
    qm = _att_stack(qb, kvh, BF16)
    s = _dot_nt(qm, k2) * (64 ** -0.5) + bias
    grp = lax.broadcasted_iota(jnp.int32, (512, 1), 0) // 128
    sink = jnp.zeros((512, 1), F32)
    for g in range(4):
        sink = jnp.where(grp == g, sink_ref[kvh * 4 + g], sink)
    m = jnp.maximum(jnp.max(s, axis=-1, keepdims=True), sink)
    e = jnp.exp(s - m)
    esink = jnp.exp(sink - m)
    inv = 1.0 / (jnp.sum(e, axis=-1, keepdims=True) + esink)
    return qm, e * inv, esink * inv


def _att_specs(S):
    q = pl.BlockSpec((None, S, ATT_W), lambda b: (b, 0, OFF["q"] // ATT_W))
    k = pl.BlockSpec((None, S, KV_W), lambda b: (b, 0, OFF["k"] // KV_W))
    v = pl.BlockSpec((None, S, KV_W), lambda b: (b, 0, OFF["vv"] // KV_W))
    tok = pl.BlockSpec((None, S, ATT_W), lambda b: (b, 0, 0))
    kv = pl.BlockSpec((None, S, KV_W), lambda b: (b, 0, 0))
    return q, k, v, tok, kv


_SMEM = pl.BlockSpec(memory_space=pltpu.SMEM)


def _attn_fwd(P, sinks, og, cat, name):
    B, S, _ = P.shape
    q_spec, k_spec, v_spec, _, _ = _att_specs(S)
    tok = pl.BlockSpec((None, S, ATT_W), lambda b: (b, 0, GM_W // ATT_W))

    def body(q_ref, k_ref, v_ref, sink_ref, og_ref, cat_ref, o_ref, kpad, vpad, bias_ref):
        _att_fill_bias(bias_ref)
        kpad[0:128, :] = jnp.zeros((128, KV_W), BF16)
        vpad[0:128, :] = jnp.zeros((128, KV_W), BF16)
        kpad[128:, :] = k_ref[...].astype(BF16)
        vpad[128:, :] = v_ref[...].astype(BF16)

        def step(n, carry):
            st = pl.multiple_of(n * 128, 128)
            qb = q_ref[pl.ds(st, 128), :]
            k2, v2 = kpad[pl.ds(st, 256), :], vpad[pl.ds(st, 256), :]
            pairs = [jnp.zeros((128, 128), F32)] * 4
            bias = _att_bias(bias_ref, n)
            for kvh in range(2):
                _, p, _ = _att_probs(qb, k2, bias, sink_ref, kvh)
                pairs = _att_unstack(pairs, _dot(p.astype(BF16), v2), kvh)
            o = jnp.concatenate(pairs, axis=1)
            r = lax.rsqrt(jnp.mean(o * o, axis=-1, keepdims=True) + EPS)
            o_ref[pl.ds(st, 128), :] = (o * r * og_ref[...]).astype(BF16)
            return carry

        lax.fori_loop(0, S // 128, step, 0, unroll=2)

    return pl.pallas_call(
        body, name=name, grid=(B,), in_specs=[q_spec, k_spec, v_spec, _SMEM, _full((1, ATT_W)), _ANY], out_specs=tok,
        out_shape=jax.ShapeDtypeStruct(cat.shape, BF16), input_output_aliases={5: 0},
        scratch_shapes=[pltpu.VMEM((S + 128, KV_W), BF16)] * 2 + [pltpu.VMEM((1024, 256), F32)])(P, P, P, sinks, og, cat)


def _attn_bwd(P, dcat, sinks, og, name):
    B, S, _ = P.shape
    q_spec, k_spec, v_spec, tok, kv = _att_specs(S)
    do_spec = pl.BlockSpec((None, S, ATT_W), lambda b: (b, 0, GM_W // ATT_W))

    def body(q_ref, k_ref, v_ref, do_ref, sink_ref, og_ref, dq_ref, dk_ref, dv_ref, dsink_ref, dog_ref,
             kpad, vpad, dkpad, dvpad, bias_ref):
        _att_fill_bias(bias_ref)

        @pl.when(pl.program_id(0) == 0)
        def _():
            dsink_ref[...] = jnp.zeros((8, 128), F32)
            dog_ref[...] = jnp.zeros((1, ATT_W), F32)

        kpad[0:128, :] = jnp.zeros((128, KV_W), BF16)
        vpad[0:128, :] = jnp.zeros((128, KV_W), BF16)
        kpad[128:, :] = k_ref[...].astype(BF16)
        vpad[128:, :] = v_ref[...].astype(BF16)
        dkpad[...] = jnp.zeros((S + 128, KV_W), F32)
        dvpad[...] = jnp.zeros((S + 128, KV_W), F32)
        half = _lane_half()
        head_row = lax.broadcasted_iota(jnp.int32, (8, 128), 0)

        def step(n, carry):
            st = pl.multiple_of(n * 128, 128)
            qb = q_ref[pl.ds(st, 128), :]
            k2, v2 = kpad[pl.ds(st, 256), :], vpad[pl.ds(st, 256), :]
            saved, pairs = [], [jnp.zeros((128, 128), F32)] * 4
            bias = _att_bias(bias_ref, n)
            for kvh in range(2):
                qm, p, psink = _att_probs(qb, k2, bias, sink_ref, kvh)
                o = _dot(p.astype(BF16), v2)
                saved.append((qm, p, psink, o))
                pairs = _att_unstack(pairs, o, kvh)
            o = jnp.concatenate(pairs, axis=1)
            r = lax.rsqrt(jnp.mean(o * o, axis=-1, keepdims=True) + EPS)
            on = o * r
            dout = do_ref[pl.ds(st, 128), :].astype(F32)
            dog_ref[...] += jnp.sum(dout * on, axis=0, keepdims=True)
            dyn = dout * og_ref[...]
            do = r * (dyn - on * jnp.mean(dyn * on, axis=-1, keepdims=True))
            dq_pairs = [jnp.zeros((128, 128), F32)] * 4
            dsink = jnp.zeros((8, 128), F32)
            for kvh in range(2):
                qm, p, psink, og_ = saved[kvh]
                dog = _att_stack(do, kvh, F32)
                delta = jnp.sum(dog * jnp.where(jnp.concatenate([half] * 4, axis=0) == kvh, og_, 0.0), axis=-1, keepdims=True)
                dogb, pb = dog.astype(BF16), p.astype(BF16)
                dvpad[pl.ds(st, 256), :] += _dot_tn(pb, dogb)
                dp = _dot_nt(dogb, v2)
                ds = (p * (dp - delta) * (64 ** -0.5)).astype(BF16)
                sd = psink * delta
                for g in range(4):
                    dsink = dsink - jnp.where(head_row == kvh * 4 + g, jnp.sum(sd[g * 128:(g + 1) * 128]), 0.0)
                dq_pairs = _att_unstack(dq_pairs, _dot(ds, k2), kvh)
                dkpad[pl.ds(st, 256), :] += _dot_tn(ds, qm)
            dsink_ref[...] += dsink
            dq_ref[pl.ds(st, 128), :] = jnp.concatenate(dq_pairs, axis=1).astype(BF16)
            return carry

        lax.fori_loop(0, S // 128, step, 0, unroll=2)
        dk_ref[...] = dkpad[128:, :].astype(BF16)
        dv_ref[...] = dvpad[128:, :].astype(BF16)

    return pl.pallas_call(
        body, name=name, grid=(B,),
        in_specs=[q_spec, k_spec, v_spec, do_spec, _SMEM, _full((1, ATT_W))],
        out_specs=[tok, kv, kv, _full((8, 128)), _full((1, ATT_W))],
        out_shape=[jax.ShapeDtypeStruct((B, S, ATT_W), BF16), jax.ShapeDtypeStruct((B, S, KV_W), BF16),
                   jax.ShapeDtypeStruct((B, S, KV_W), BF16), jax.ShapeDtypeStruct((8, 128), F32),
                   jax.ShapeDtypeStruct((1, ATT_W), F32)],
        scratch_shapes=[pltpu.VMEM((S + 128, KV_W), BF16)] * 2 + [pltpu.VMEM((S + 128, KV_W), F32)] * 2
        + [pltpu.VMEM((1024, 256), F32)],
        compiler_params=pltpu.CompilerParams(dimension_semantics=("arbitrary",)),
    )(P, P, P, dcat, sinks, og)


CONV_TC = 256
CONV_RC = 64


def _conv_taps(ext, r0):
    return [ext[pl.ds(r0 + 8 - k, CONV_RC), :] for k in range(4)]


def _conv_pre(taps, w_ref, b_ref):
    acc = b_ref[...] + w_ref[3:4, :] * taps[0]
    for k in range(1, 4):
        acc = acc + w_ref[3 - k:4 - k, :] * taps[k]
    return acc


def _conv_fwd(P, w8, b, name):
    B, S, _ = P.shape
    nj = CONV_CH // CONV_TC
    x_spec = pl.BlockSpec((None, S, CONV_TC), lambda b_, j: (b_, 0, OFF["xbc"] // CONV_TC + j))
    tok = pl.BlockSpec((None, S, CONV_TC), lambda b_, j: (b_, 0, j))

    def body(x_ref, w_ref, b_ref, o_ref, ext):
        ext[0:8, :] = jnp.zeros((8, CONV_TC), F32)
        ext[8:, :] = x_ref[...]
        for r0 in range(0, S, CONV_RC):
            pre = _conv_pre(_conv_taps(ext, r0), w_ref, b_ref)
            o_ref[pl.ds(r0, CONV_RC), :] = pre * _sigmoid(pre)

    return pl.pallas_call(
        body, name=name, grid=(B, nj),
        in_specs=[x_spec, pl.BlockSpec((8, CONV_TC), lambda b_, j: (0, j)), pl.BlockSpec((1, CONV_TC), lambda b_, j: (0, j))],
        out_specs=tok, out_shape=jax.ShapeDtypeStruct((B, S, CONV_CH), F32),
        scratch_shapes=[pltpu.VMEM((S + 8, CONV_TC), F32)])(P, w8, b)


def _conv_bwd(P, dact, w8, b, name):
    B, S, _ = P.shape
    nj = CONV_CH // CONV_TC
    x_spec = pl.BlockSpec((None, S, CONV_TC), lambda j, b_: (b_, 0, OFF["xbc"] // CONV_TC + j))
    tok = pl.BlockSpec((None, S, CONV_TC), lambda j, b_: (b_, 0, j))
    w_spec = pl.BlockSpec((8, CONV_TC), lambda j, b_: (0, j))
    b_spec = pl.BlockSpec((1, CONV_TC), lambda j, b_: (0, j))

    def body(x_ref, d_ref, w_ref, b_ref, dx_ref, dw_ref, db_ref, ext, extd):
        @pl.when(pl.program_id(1) == 0)
        def _():
            dw_ref[...] = jnp.zeros((8, CONV_TC), F32)
            db_ref[...] = jnp.zeros((1, CONV_TC), F32)

        ext[0:8, :] = jnp.zeros((8, CONV_TC), F32)
        ext[8:, :] = x_ref[...]
        extd[pl.ds(8 + S, 8), :] = jnp.zeros((8, CONV_TC), F32)
        db = jnp.zeros((1, CONV_TC), F32)
        dws = [jnp.zeros((1, CONV_TC), F32)] * 4
        for r0 in range(0, S, CONV_RC):
            taps = _conv_taps(ext, r0)
            pre = _conv_pre(taps, w_ref, b_ref)
            sg = _sigmoid(pre)
            dpre = d_ref[pl.ds(r0, CONV_RC), :] * (sg * (1.0 + pre * (1.0 - sg)))
            extd[pl.ds(8 + r0, CONV_RC), :] = dpre
            db = db + jnp.sum(dpre, axis=0, keepdims=True)
            dws = [dws[i] + jnp.sum(dpre * taps[3 - i], axis=0, keepdims=True) for i in range(4)]
        for r0 in range(0, S, CONV_RC):
            dx = w_ref[3:4, :] * extd[pl.ds(8 + r0, CONV_RC), :]
            for k in range(1, 4):
                dx = dx + w_ref[3 - k:4 - k, :] * extd[pl.ds(8 + r0 + k, CONV_RC), :]
            dx_ref[pl.ds(r0, CONV_RC), :] = dx.astype(BF16)
        db_ref[...] += db
        sub = lax.broadcasted_iota(jnp.int32, (8, CONV_TC), 0)
        dw_ref[...] += sum(jnp.where(sub == i, dws[i], 0.0) for i in range(4))

    return pl.pallas_call(
        body, name=name, grid=(nj, B), in_specs=[x_spec, tok, w_spec, b_spec], out_specs=[tok, w_spec, b_spec],
        out_shape=[jax.ShapeDtypeStruct((B, S, CONV_CH), BF16), jax.ShapeDtypeStruct((8, CONV_CH), F32),
                   jax.ShapeDtypeStruct((1, CONV_CH), F32)],
        scratch_shapes=[pltpu.VMEM((S + 8, CONV_TC), F32), pltpu.VMEM((S + 16, CONV_TC), F32)],
        compiler_params=pltpu.CompilerParams(dimension_semantics=("arbitrary", "arbitrary")),
    )(P, dact, w8, b)


def _ssd_consts():
    hd = np.arange(SSM_W) // SSM_HD
    E = (np.arange(128)[:, None] == hd[None, :]).astype(np.float32)
    tri = (np.arange(128)[:, None] >= np.arange(128)[None, :]).astype(np.float32)
    return jnp.asarray(E, BF16), jnp.asarray(E.T, BF16), jnp.asarray(tri, BF16), jnp.asarray(tri.T, BF16)


def _pieces(x, n):
    out, r = [], x
    for _ in range(n):
        p = r.astype(BF16)
        out.append(p)
        r = r - p.astype(F32)
    return out


def _dot01(x, m01, n):
    return sum(_dot(p, m01) for p in _pieces(x, n))


def _dot01_left(m01, x, n):
    return sum(_dot(m01, p) for p in _pieces(x, n))


def _ssd_pre(xa, dtraw, bias, alog, E, tri):
    lane = lax.broadcasted_iota(jnp.int32, (128, 128), 1)
    pre = dtraw + bias
    dtp = jnp.where(lane < SSM_H, jnp.maximum(pre, 0.0) + jnp.log(1.0 + jnp.exp(-jnp.abs(pre))), 0.0)
    a = -jnp.exp(alog)
    acs = _dot01_left(tri, dtp * a, 3)
    acsT = acs.T
    dtE, acsE = _dot01(dtp, E, 2), _dot01(acs, E, 3)
    X = xa[:, :SSM_W]
    xdt = X * dtE
    wE = jnp.exp(acsE[127:128, :] - acsE)
    eE = jnp.exp(acsE)
    cdE = eE[127:128, :]
    return dict(pre=pre, dtp=dtp, a=a, acs=acs, acsT=acsT, dtE=dtE, acsE=acsE, cdE=cdE, X=X, xdt=xdt, wE=wE, eE=eE)


def _ssd_decay(c, h):
    lm = lax.broadcasted_iota(jnp.int32, (128, 128), 0) >= lax.broadcasted_iota(jnp.int32, (128, 128), 1)
    return jnp.exp(jnp.where(lm, c["acs"][:, h:h + 1] - c["acsT"][h:h + 1, :], NEG_INF))


def _ssd_pair_operands(c, CB, h0):
    lane = lax.broadcasted_iota(jnp.int32, (128, 128), 1)
    L0, L1 = _ssd_decay(c, h0), _ssd_decay(c, h0 + 1)
    M = jnp.concatenate([CB * L0, CB * L1], axis=1).astype(BF16)
    xp = c["xdt"][:, h0 * 64:h0 * 64 + 128]
    BD = jnp.concatenate([jnp.where(lane < 64, xp, 0.0), jnp.where(lane >= 64, xp, 0.0)], axis=0).astype(BF16)
    return L0, L1, M, BD


def _ssd_y(c, xa, state_ref, dskipE):
    per_group, ys = [], []
    for g in range(SSM_G):
        gs = slice(g * 512, (g + 1) * 512)
        Bb = xa[:, SSM_W + g * 128:SSM_W + (g + 1) * 128].astype(BF16)
        Cb = xa[:, SSM_W + 256 + g * 128:SSM_W + 256 + (g + 1) * 128].astype(BF16)
        CB = _dot_nt(Cb, Bb)
        Sg = state_ref[:, gs]
        yoff = _dot(Cb, Sg.astype(BF16)) * c["eE"][:, gs]
        ydiag, pairs = [], []
        for j in range(4):
            ops = _ssd_pair_operands(c, CB, g * 8 + 2 * j)
            pairs.append(ops)
            ydiag.append(_dot(ops[2], ops[3]))
        ys.append(jnp.concatenate(ydiag, axis=1) + yoff)
        per_group.append(dict(Bb=Bb, Cb=Cb, CB=CB, Sg=Sg, yoff=yoff, pairs=pairs))
    Y = jnp.concatenate(ys, axis=1) + c["X"] * dskipE
    return Y, per_group


def _ssd_specs(S, rev):
    nc = S // CHUNK
    cm = (lambda b, i: (b, nc - 1 - i)) if rev else (lambda b, i: (b, i))
    xa = pl.BlockSpec((None, CHUNK, CONV_CH), lambda b, i: cm(b, i) + (0,))
    z = [pl.BlockSpec((None, CHUNK, 256), lambda b, i, q=q: cm(b, i) + (OFF["z"] // 256 + q,)) for q in range(4)]
    dt = pl.BlockSpec((None, CHUNK, 128), lambda b, i: cm(b, i) + (OFF["dt"] // 128,))
    tok = pl.BlockSpec((None, CHUNK, SSM_W), lambda b, i: cm(b, i) + (0,))
    st = pl.BlockSpec((None, None, 128, SSM_W), lambda b, i: cm(b, i) + (0, 0))
    return nc, xa, z, dt, tok, st


def _ssd_fwd(xact, P, bias, alog, dskipE, ng, cat, name):
    B, S, _ = P.shape
    nc, xa_spec, z_specs, dt_spec, _, st_spec = _ssd_specs(S, False)
    tok = pl.BlockSpec((None, CHUNK, SSM_W), lambda b, i: (b, i, 1))
    E, _, tri, _ = _ssd_consts()

    def body(xa_ref, z0, z1, z2, z3, dt_ref, bias_ref, alog_ref, dsk_ref, ng_ref, E_ref, tri_ref, cat_ref, o_ref, sp_ref, state):
        @pl.when(pl.program_id(1) == 0)
        def _():
            state[...] = jnp.zeros((128, SSM_W), F32)

        sp_ref[...] = state[...]
        xa = xa_ref[...]
        c = _ssd_pre(xa, dt_ref[...], bias_ref[...], alog_ref[...], E_ref[...], tri_ref[...])
        Y, groups = _ssd_y(c, xa, state, dsk_ref[...])
        Z = (c["xdt"] * c["wE"]).astype(BF16)
        for g in range(SSM_G):
            gs = slice(g * 512, (g + 1) * 512)
            state[:, gs] = groups[g]["Sg"] * c["cdE"][:, gs] + _dot_tn(groups[g]["Bb"], Z[:, gs])
        zv = jnp.concatenate([z0[...], z1[...], z2[...], z3[...]], axis=1)
        yz = Y * (zv * _sigmoid(zv))
        outs = []
        for g in range(SSM_G):
            yg = yz[:, g * 512:(g + 1) * 512]
            outs.append(yg * lax.rsqrt(jnp.mean(yg * yg, axis=-1, keepdims=True) + EPS))
        o_ref[...] = (jnp.concatenate(outs, axis=1) * ng_ref[...]).astype(BF16)

    return pl.pallas_call(
        body, name=name, grid=(B, nc),
        in_specs=[xa_spec] + z_specs + [dt_spec, _full((1, 128)), _full((1, 128)), _full((1, SSM_W)), _full((1, SSM_W)),
                                        _full((128, SSM_W)), _full((128, 128)), _ANY],
        out_specs=[tok, st_spec],
        out_shape=[jax.ShapeDtypeStruct(cat.shape, BF16), jax.ShapeDtypeStruct((B, nc, 128, SSM_W), F32)],
        scratch_shapes=[pltpu.VMEM((128, SSM_W), F32)], input_output_aliases={12: 0},
        compiler_params=pltpu.CompilerParams(dimension_semantics=("arbitrary", "arbitrary")),
    )(xact, P, P, P, P, P, bias, alog, dskipE, ng, E, tri, cat)


def _ssd_bwd(xact, P, sprev, dcat, bias, alog, dskipE, ng, name):
    B, S, _ = P.shape
    nc, xa_spec, z_specs, dt_spec, tok, st_spec = _ssd_specs(S, True)
    do_spec = pl.BlockSpec((None, CHUNK, SSM_W), lambda b, i: (b, nc - 1 - i, 1))
    E, ET, tri, triT = _ssd_consts()
    dt_out = pl.BlockSpec((None, CHUNK, 128), lambda b, i: (b, nc - 1 - i, 0))

    def body(xa_ref, z0, z1, z2, z3, dt_ref, sp_ref, do_ref, bias_ref, alog_ref, dsk_ref, ng_ref, E_ref, ET_ref, tri_ref,
             triT_ref, dxa_ref, dz_ref, ddt_ref, dbias_ref, dalog_ref, ddsk_ref, dng_ref, dstate):
        first = (pl.program_id(0) == 0) & (pl.program_id(1) == 0)

        @pl.when(first)
        def _():
            for ref in (dbias_ref, dalog_ref, ddsk_ref, dng_ref):
                ref[...] = jnp.zeros(ref.shape, F32)

        @pl.when(pl.program_id(1) == 0)
        def _():
            dstate[...] = jnp.zeros((128, SSM_W), F32)

        xa, ETm = xa_ref[...], ET_ref[...]
        c = _ssd_pre(xa, dt_ref[...], bias_ref[...], alog_ref[...], E_ref[...], tri_ref[...])
        Y, groups = _ssd_y(c, xa, sp_ref, dsk_ref[...])
        X, xdt = c["X"], c["xdt"]
        zv = jnp.concatenate([z0[...], z1[...], z2[...], z3[...]], axis=1)
        sg = _sigmoid(zv)
        zs = zv * sg
        yz = Y * zs
        dout = do_ref[...].astype(F32)
        dyz = []
        for g in range(SSM_G):
            gs = slice(g * 512, (g + 1) * 512)
            yg = yz[:, gs]
            r = lax.rsqrt(jnp.mean(yg * yg, axis=-1, keepdims=True) + EPS)
            yn = yg * r
            dng_ref[:, gs] += jnp.sum(dout[:, gs] * yn, axis=0, keepdims=True)
            dyn = dout[:, gs] * ng_ref[:, gs]
            dyz.append(r * (dyn - yn * jnp.mean(dyn * yn, axis=-1, keepdims=True)))
        dyz = jnp.concatenate(dyz, axis=1)
        dz_ref[...] = (dyz * Y * (sg * (1.0 + zv * (1.0 - sg)))).astype(BF16)
        dY = dyz * zs
        ddsk_ref[...] += jnp.sum(dY * X, axis=0, keepdims=True)
        dX = dY * dsk_ref[...]
        lane = lax.broadcasted_iota(jnp.int32, (128, 128), 1)
        sub = lax.broadcasted_iota(jnp.int32, (128, 128), 0)
        colform = jnp.zeros((128, 128), F32)
        rowform = jnp.zeros((128, 128), F32)
        dxdt, gacsE, dBC = [], [], []
        for g in range(SSM_G):
            gs = slice(g * 512, (g + 1) * 512)
            G = groups[g]
            Bb, Cb, CB, Sg = G["Bb"], G["Cb"], G["CB"], G["Sg"]
            dYg = dY[:, gs]
            dQ = (dYg * c["eE"][:, gs]).astype(BF16)
            dSn = dstate[:, gs]
            dSnb = dSn.astype(BF16)
            cd = c["cdE"][:, gs]
            dC = _dot_nt(dQ, Sg.astype(BF16))
            dSprev = _dot_tn(Cb, dQ) + dSn * cd
            t1 = jnp.broadcast_to(jnp.sum(dSn * Sg * cd, axis=0, keepdims=True), (8, 512))
            colform = colform + jnp.where(sub == 127, _dot01(t1, ETm[gs, :], 2)[0:1, :], 0.0)
            Zg = xdt[:, gs] * c["wE"][:, gs]
            dZ = _dot(Bb, dSnb)
            dB = _dot_nt(Zg.astype(BF16), dSnb)
            U = dZ * Zg
            ga = dYg * G["yoff"] - U
            ga = ga + jnp.where(lax.broadcasted_iota(jnp.int32, (128, 512), 0) == 127, jnp.sum(U, axis=0, keepdims=True), 0.0)
            gacsE.append(ga)
            dxg = [None] * 4
            dCB = jnp.zeros((128, 128), F32)
            for j in range(4):
                h0 = g * 8 + 2 * j
                L0, L1, M, BD = G["pairs"][j]
                dYp = dYg[:, j * 128:(j + 1) * 128].astype(BF16)
                dM = _dot_nt(dYp, BD)
                dBD = _dot_tn(M, dYp)
                dxg[j] = jnp.where(lane < 64, dBD[:128], dBD[128:])
                for t, (h, L) in enumerate(((h0, L0), (h0 + 1, L1))):
                    dMh = dM[:, t * 128:(t + 1) * 128]
                    dCB = dCB + dMh * L
                    Gh = dMh * CB * L
                    colform = colform + jnp.where(lane == h, jnp.sum(Gh, axis=1, keepdims=True), 0.0)
                    rowform = rowform - jnp.where(sub == h, jnp.sum(Gh, axis=0, keepdims=True), 0.0)
            dCBb = dCB.astype(BF16)
            dC = dC + _dot(dCBb, Bb)
            dB = dB + _dot_tn(dCBb, Cb)
            dxdt.append(jnp.concatenate(dxg, axis=1) + dZ * c["wE"][:, gs])
            dBC.append((dB, dC))
            dstate[:, gs] = dSprev
        dxdt = jnp.concatenate(dxdt, axis=1)
        dX = dX + dxdt * c["dtE"]
        ddt = _dot01(dxdt * X, ETm, 2)
        dacs = colform + rowform.T + _dot01(jnp.concatenate(gacsE, axis=1), ETm, 2)
        dda = _dot01_left(triT_ref[...], dacs, 2)
        ddt = ddt + dda * c["a"]
        dalog_ref[...] += jnp.sum(dda * c["dtp"], axis=0, keepdims=True) * c["a"]
        ddtraw = jnp.where(lane < SSM_H, ddt * _sigmoid(c["pre"]), 0.0)
        dbias_ref[...] += jnp.sum(ddtraw, axis=0, keepdims=True)
        ddt_ref[...] = ddtraw.astype(BF16)
        dxa_ref[...] = jnp.concatenate([dX, dBC[0][0], dBC[1][0], dBC[0][1], dBC[1][1]], axis=1)

    p128, p1k = _full((1, 128)), _full((1, SSM_W))
    return pl.pallas_call(
        body, name=name, grid=(B, nc),
        in_specs=[xa_spec] + z_specs + [dt_spec, st_spec, do_spec, p128, p128, p1k, p1k,
                                        _full((128, SSM_W)), _full((SSM_W, 128)), _full((128, 128)), _full((128, 128))],
        out_specs=[xa_spec, tok, dt_out, p128, p128, p1k, p1k],
        out_shape=[jax.ShapeDtypeStruct((B, S, CONV_CH), F32), jax.ShapeDtypeStruct((B, S, SSM_W), BF16),
                   jax.ShapeDtypeStruct((B, S, 128), BF16), jax.ShapeDtypeStruct((1, 128), F32),
                   jax.ShapeDtypeStruct((1, 128), F32), jax.ShapeDtypeStruct((1, SSM_W), F32),
                   jax.ShapeDtypeStruct((1, SSM_W), F32)],
        scratch_shapes=[pltpu.VMEM((128, SSM_W), F32)],
        compiler_params=pltpu.CompilerParams(dimension_semantics=("arbitrary", "arbitrary")),
    )(xact, P, P, P, P, P, sprev, dcat, bias, alog, dskipE, ng, E, ET, tri, triT)


def _adamw(w, parts, m, v, name, tr=512, layer=0, prev=None):
    Ltot, R, C = w.shape
    ns = parts.shape[0]
    tr = min(tr, R)
    assert R % tr == 0 and parts.shape[1:] == (R, C)
    c1 = 1.0 / (1.0 - ADAM_B1 ** ADAM_STEP)
    c2 = 1.0 / (1.0 - ADAM_B2 ** ADAM_STEP)

    def body(w_ref, p_ref, m_ref, v_ref, *rest):
        g_ref, d_ref, mo_ref, vo_ref = rest[-4:]
        g = p_ref[0].astype(F32)
        for s in range(1, ns):
            g = g + p_ref[s].astype(F32)
        mn = ADAM_B1 * m_ref[...] + (1.0 - ADAM_B1) * g
        vn = ADAM_B2 * v_ref[...] + (1.0 - ADAM_B2) * (g * g)
        g_ref[...] = g
        mo_ref[...] = mn
        vo_ref[...] = vn
        d_ref[...] = -ADAM_LR * ((mn * c1) / (jnp.sqrt(vn * c2) + ADAM_EPS) + ADAM_WD * w_ref[...])

    blk = pl.BlockSpec((None, tr, C), lambda i: (layer, i, 0))
    extra = [] if prev is None else list(prev)
    return pl.pallas_call(
        body, name=name, grid=(R // tr,),
        in_specs=[blk, pl.BlockSpec((ns, tr, C), lambda i: (0, i, 0)), blk, blk] + [pl.BlockSpec(memory_space=pl.ANY)] * len(extra),
        out_specs=[blk] * 4, out_shape=[jax.ShapeDtypeStruct((Ltot, R, C), F32)] * 4,
        input_output_aliases={4 + k: k for k in range(len(extra))})(w, parts, m, v, *extra)


_SMALL = ("ada_b", "norm1_g", "gm_ln_g", "gm_ln_b", "gm_ws", "gm_bs", "gm_norm_g", "attn_sinks", "attn_norm_g", "conv_b",
          "dt_bias", "a_log", "d_skip", "ssm_norm_g", "norm2_g", "final_norm_g")


def _pack(arrs):
    flat = []
    for a in arrs:
        f = a.reshape(-1).astype(F32)
        flat.append(jnp.pad(f, (0, (-f.shape[0]) % 1024)))
    return jnp.concatenate(flat).reshape(-1, 128)


def _unpack(pack, like):
    out, r = [], 0
    for a in like:
        n = int(np.prod(a.shape))
        rows = (n + 1023) // 1024 * 8
        out.append(lax.slice(pack, (r, 0), (r + rows, 128)).reshape(-1)[:n].reshape(a.shape))
        r += rows
    return out


def kernel(x, c, ada_w, ada_b, norm1_g, w_in, gm_ln_g, gm_ln_b, gm_ws, gm_bs, gm_norm_g, attn_sinks, attn_norm_g, conv_w, conv_b, dt_bias, a_log, d_skip, ssm_norm_g, w_out, norm2_g, w_mlp1, w_mlp2, final_norm_g, loss_target, m_ada_w, m_ada_b, m_norm1_g, m_w_in, m_gm_ln_g, m_gm_ln_b, m_gm_ws, m_gm_bs, m_gm_norm_g, m_attn_sinks, m_attn_norm_g, m_conv_w, m_conv_b, m_dt_bias, m_a_log, m_d_skip, m_ssm_norm_g, m_w_out, m_norm2_g, m_w_mlp1, m_w_mlp2, m_final_norm_g, v_ada_w, v_ada_b, v_norm1_g, v_w_in, v_gm_ln_g, v_gm_ln_b, v_gm_ws, v_gm_bs, v_gm_norm_g, v_attn_sinks, v_attn_norm_g, v_conv_w, v_conv_b, v_dt_bias, v_a_log, v_d_skip, v_ssm_norm_g, v_w_out, v_norm2_g, v_w_mlp1, v_w_mlp2, v_final_norm_g):
    args = dict(locals())
    B, S, _ = x.shape
    T = B * S
    L = DEPTH
    me = 4 * lax.axis_index("x") + 2 * lax.axis_index("y") + lax.axis_index("c")

    gath = _gather2([c, conv_w], "ag_c")
    big = ("w_in", "w_out", "w_mlp1", "w_mlp2")
    chain = [(n, l) for l in range(L) for n in ("w_in", "w_mlp1", "w_out", "w_mlp2")]
    inflight = {}

    def start_next(order):
        if not chain:
            return jnp.zeros((8, 128), F32)
        n, l = chain.pop(0)
        sems, land_thru, token = _gather_start(zone[n, l], order, f"ag_start_{n}{l}")
        inflight[n, l] = (sems, land_thru)
        return token

    def gathered(n, l, after):
        land = _gather_wait(*inflight.pop((n, l)), after, f"ag_wait_{n}{l}")
        return _gather_finish(land, f"ag_fin_{n}{l}")

    forwarding = {}

    def arrived(n, l, after):
        land = _gather_wait(*inflight.pop((n, l)), after, f"ag_wait_{n}{l}")
        sems, land_thru, token = _forward_start(land, after, f"ag_fwd_start_{n}{l}")
        forwarding[n, l] = (sems, land_thru)
        return token

    def ready(n, l, after):
        return _forward_wait(*forwarding.pop((n, l)), after, f"ag_fwd_wait_{n}{l}")

    me1 = me.astype(jnp.int32).reshape(1)
    zone = {(n, l): _landing_zone(args[n], l, me1, f"ag_zone_{n}{l}") for n, l in chain}
    later_zones = [zone[k] for k in chain[1:]]

    tok = start_next(gath[0])
    c_all = gath[0].reshape(NDEV * B, D) + tok[0, 0]
    c_act = (c_all * jax.nn.sigmoid(c_all)).astype(BF16)
    nb_rows = c_act.shape[0]
    c_pad = jnp.pad(c_act, ((0, 128 - nb_rows), (0, 0)))
    adw = ada_w.astype(BF16)
    mod_part = jnp.stack([_mm(c_pad, adw[l], mode="nn", name=f"mod{l}", tn=768)[:nb_rows] for l in range(L)])
    mod_all = _gather_small([mod_part], "ag_mod", order=later_zones)[0]
    mod_mine = lax.dynamic_slice_in_dim(mod_all, me * B, B, axis=2)
    mod = jnp.transpose(mod_mine, (1, 2, 0, 3)).reshape(L, B, 6 * D) + ada_b[:, None, :]
    mods = [[mod[l][:, None, i * D:(i + 1) * D] for i in range(6)] for l in range(L)]

    win_g, wout_g, w1_g, w2_g = [None] * L, [None] * L, [None] * L, [None] * L

    tril = jnp.tril(jnp.ones((128, 128), F32))
    row = lambda a: a.reshape(1, -1)
    pad128 = lambda a: jnp.pad(a.reshape(1, -1), ((0, 0), (0, 128 - a.shape[-1])))
    small = []
    for l in range(L):
        wt = gm_ws[l] * tril
        small.append(dict(
            lng=row(gm_ln_g[l]), lnb=row(gm_ln_b[l]), wt=wt.astype(BF16), wtT=jnp.swapaxes(wt, 1, 2).astype(BF16),
            bsx=jnp.repeat(gm_bs[l].T, 128, axis=1), gog=row(gm_norm_g[l]), sinks=attn_sinks[l], aog=row(attn_norm_g[l]),
            bias=pad128(dt_bias[l]), alog=pad128(a_log[l]), dskE=jnp.repeat(d_skip[l], SSM_HD).reshape(1, SSM_W),
            sng=row(ssm_norm_g[l]), cb=row(conv_b[l])))
    convw_all = jnp.transpose(gath[1], (1, 2, 0, 3)).reshape(L, 4, CONV_CH)
    convw8 = jnp.pad(convw_all, ((0, 0), (0, 4), (0, 0)))

    saved = []
    xl = x
    g_in = gathered("w_in", 0, mod)
    tok = start_next(g_in)
    h = _norm_fwd(xl, row(norm1_g[0]) + tok[0, 0], mods[0][1], mods[0][0], "norm1_f0")
    for l in range(L):
        sm = small[l]
        win_g[l] = _shards_to_cols(g_in, f"w_in_cols{l}")
        P = _mm(h.reshape(T, D), win_g[l], mode="nn", name=f"proj_in{l}", tn=1536, order=tok).reshape(B, S, PW)
        cat = _gmlp_fwd(P, sm["lng"], sm["lnb"], sm["wt"], sm["bsx"], sm["gog"], f"gmlp_f{l}")
        cat = _attn_fwd(P, sm["sinks"], sm["aog"], cat, f"attn_f{l}")
        xact = _conv_fwd(P, convw8[l], sm["cb"], f"conv_f{l}")
        tok = start_next(arrived("w_mlp1", l, xact))
        cat, sprev = _ssd_fwd(xact, P, sm["bias"], sm["alog"], sm["dskE"], sm["sng"] + tok[0:1, 0:1], cat, f"ssd_f{l}")
        g_out = gathered("w_out", l, cat)
        tok = start_next(g_out)
        wout_g[l] = g_out.reshape(D, D)
        mix = _mm(cat.reshape(T, D), wout_g[l], mode="nn", name=f"proj_out{l}", order=tok).reshape(B, S, D)
        x_mid, h2 = _norm_fwd(xl, row(norm2_g[l]), mods[l][4], mods[l][3], f"norm2_f{l}", resid=(mix, mods[l][2]))
        w1_g[l] = ready("w_mlp1", l, h2)
        a_act, r_act = _mm(h2.reshape(T, D), w1_g[l], mode="nn", name=f"mlp1_{l}", out_dtypes=(BF16, BF16), col_blocked_b=True,
                           epilogue=lambda acc: (acc, jnp.square(jnp.maximum(acc, 0.0))))
        g_2 = gathered("w_mlp2", l, r_act)
        tok = start_next(g_2)
        w2_g[l] = g_2.reshape(DFF, D)
        m2 = _mm(r_act, w2_g[l], mode="nn", name=f"mlp2_{l}", order=tok, tk=4096).reshape(B, S, D)
        saved.append(dict(x_in=xl, h=h, P=P, xact=xact, sprev=sprev, cat=cat, mix=mix, x_mid=x_mid, h2=h2, a=a_act, r=r_act, m2=m2))
        if l + 1 < L:
            tok = start_next(arrived("w_in", l + 1, m2))
            xl, h = _norm_fwd(x_mid, row(norm1_g[l + 1]) + tok[0, 0], mods[l + 1][1], mods[l + 1][0], f"norm1_f{l + 1}",
                              resid=(m2, mods[l][5]))
            g_in = ready("w_in", l + 1, h)

    sv = saved[L - 1]
    nb = _norm_bwd(sv["x_mid"], row(final_norm_g), "final_b", tgt=loss_target, br=sv["m2"], gate=mods[L - 1][5], x_is_prev=True)
    loss_part, g_final = nb["loss"], nb["dg"]
    dmod, gsm, gconvw = [None] * L, [None] * L, [None] * L
    core = lax.axis_index("c").astype(jnp.int32).reshape(1)
    reducing = []

    def reduce_start(n, l, sent, after):
        p, from_sib = _pair_wait(*sent[:3], after, f"rs_pair_wait_{n}{l}")
        s, land = _pair_add(p, from_sib, core, f"rs_add_{n}{l}")
        return reduce_exchange(n, l, s, land, after)

    def reduce_exchange(n, l, s, land, order):
        sems, s_thru, land_thru, token = _chipsum_start(s, land, order, f"rs_start_{n}{l}")
        reducing.append((n, l, sems, s_thru, land_thru))
        return token

    other = 1 - core

    for l in reversed(range(L)):
        sv, sm = saved[l], small[l]
        dm2, dxo, dg2 = nb["dbr"].reshape(T, D), nb["dx"], nb["dgate"]
        da = _mm(dm2, w2_g[l], mode="nt", name=f"mlp2_dx{l}", out_dtypes=(BF16,), extras=(sv["a"],),
                 epilogue=lambda acc, a: (acc * (2.0 * jnp.maximum(a.astype(F32), 0.0)),))
        h2f = sv["h2"].reshape(T, D)
        sent2 = _sibling_start(_dw_half(sv["r"], dm2, other, axis="m", name=f"mlp2_dw_sib{l}"), da, f"rs_sib_start_w_mlp2{l}")
        dh2 = _mm(da, w1_g[l], mode="nt", name=f"mlp1_dx{l}", col_blocked_b=True, order=sent2[3], tk=4096,
                  out_dtypes=(BF16,)).reshape(B, S, D)
        from_sib = _sibling_wait(*sent2[:3], dh2, f"rs_sib_wait_w_mlp2{l}")[1]
        sent1 = _sibling_start(_dw_half(h2f, da, other, axis="n", name=f"mlp1_dw_sib{l}", order=from_sib), da,
                               f"rs_sib_start_w_mlp1{l}")
        s2, land2 = _dw_half(sv["r"], dm2, core, axis="m", name=f"mlp2_dw_own{l}", add=from_sib, order=sent1[3])
        tok = reduce_exchange("w_mlp2", l, s2, land2, da)
        nb2 = _norm_bwd(sv["x_mid"], row(norm2_g[l]) + tok[0, 0], f"norm2_b{l}", sc=mods[l][4], dh=dh2, dres=dxo, br=sv["mix"],
                        gate=mods[l][2])
        dmix = nb2["dbr"].reshape(T, D)
        from_sib = _sibling_wait(*sent1[:3], dmix, f"rs_sib_wait_w_mlp1{l}")[1]
        s1, land1 = _dw_half(h2f, da, core, axis="n", name=f"mlp1_dw_own{l}", add=from_sib)
        tok = reduce_exchange("w_mlp1", l, s1, land1, dmix)
        dcat = _mm(dmix, wout_g[l], mode="nt", name=f"proj_out_dx{l}", order=tok, out_dtypes=(BF16,)).reshape(B, S, D)
        du, dv, dlng, dlnb, dws, dbsx, dgog = _gmlp_bwd(sv["P"], dcat, sm["lng"], sm["lnb"], sm["wt"], sm["wtT"], sm["bsx"],
                                                        sm["gog"], f"gmlp_b{l}")
        dq, dk, dvv, dsink, daog = _attn_bwd(sv["P"], dcat, sm["sinks"], sm["aog"], f"attn_b{l}")
        dwo = _mm(sv["cat"].reshape(T, D), dmix, mode="tn", name=f"proj_out_dw{l}", out_dtypes=(BF16,), tk=4096,
                  order=dq).reshape(4, 2, D // NDEV, D)
        sent = _pair_start(dwo, dmix, f"rs_pair_start_w_out{l}")
        dxa, dz, ddt, dbias, dalog, ddsk, dsng = _ssd_bwd(sv["xact"], sv["P"], sv["sprev"], dcat, sm["bias"], sm["alog"],
                                                          sm["dskE"], sm["sng"] + sent[3][0:1, 0:1], f"ssd_b{l}")
        tok = reduce_start("w_out", l, sent, dxa)
        dxbc, dcw, dcb = _conv_bwd(sv["P"], dxa, convw8[l], sm["cb"] + tok[0:1, 0:1], f"conv_b{l}")
        dP = _concat_cols([du, dv, dq, dk, dvv, dz, dxbc, ddt], PW, f"dproj_cols{l}").reshape(T, PW)
        dwin = _mm(sv["h"].reshape(T, D), dP, mode="tn", name=f"proj_in_dw{l}", out_dtypes=(BF16,), tn=1536, tk=4096)
        sent = _sibling_start(dwin, dP, f"rs_sib_start_w_in{l}")
        dh = _mm(dP, win_g[l], mode="nt", name=f"proj_in_dx{l}", tk=PW, order=sent[3], out_dtypes=(BF16,)).reshape(B, S, D)
        s_in, land_in = _cols_to_my_shards(*_sibling_wait(*sent[:3], dh, f"rs_sib_wait_w_in{l}"), core, f"w_in_dshards{l}")
        tok = reduce_exchange("w_in", l, s_in, land_in, dh)
        nb = _norm_bwd(sv["x_in"], row(norm1_g[l]) + tok[0, 0], f"norm1_b{l}", sc=mods[l][1], dh=dh, dres=nb2["dx"],
                       br=saved[l - 1]["m2"] if l > 0 else None, gate=mods[l - 1][5] if l > 0 else None)
        dmod[l] = jnp.concatenate([nb["dsh"], nb["dsc"], nb2["dgate"], nb2["dsh"], nb2["dsc"], dg2], axis=-1)
        gconvw[l] = dcw[:4]
        gsm[l] = dict(
            ada_b=jnp.sum(dmod[l], axis=(0, 1)), norm1_g=nb["dg"], gm_ln_g=dlng, gm_ln_b=dlnb, gm_ws=dws,
            gm_bs=dbsx.reshape(128, GM_H, 128).sum(-1).T, gm_norm_g=dgog, attn_sinks=dsink[:, 0], attn_norm_g=daog,
            conv_b=dcb, dt_bias=dbias[0, :SSM_H], a_log=dalog[0, :SSM_H], d_skip=ddsk.reshape(SSM_H, SSM_HD).sum(-1),
            ssm_norm_g=dsng, norm2_g=nb2["dg"])
    grad_x = nb["dx"]

    big_res, after = dict.fromkeys(big), grad_x
    tile_rows = dict(w_in=256, w_out=256, w_mlp1=256, w_mlp2=128)

    def finish_reduce(n, l, sems, s_thru, land_thru, after):
        parts = _chipsum_wait(sems, s_thru, land_thru, after, f"rs_wait_{n}{l}")
        big_res[n] = _adamw(args[n], parts, args["m_" + n], args["v_" + n], f"adamw_{n}{l}", tr=tile_rows[n], layer=l,
                            prev=big_res[n])
        return big_res[n][0]

    per_layer = [n for n in _SMALL if n != "final_norm_g"]
    g_small = [jnp.stack([gsm[l][n].reshape(args[n].shape[1:]) for l in range(L)]) for n in per_layer] + [g_final.reshape(D)]
    zc = jnp.zeros((L, 4, CONV_CH), F32)
    z1 = jnp.zeros((1, 128), F32)
    gpack = _pack([loss_part] + g_small + [jnp.stack(gconvw)])
    small_zones = [_landing_zone(jnp.stack(dmod).reshape(1, L * B, 6 * D), 0, me1, "ag_zone_dmod", dtype=F32),
                   _landing_zone(gpack[None], 0, me1, "ag_zone_small", tr=gpack.shape[0], dtype=F32)]
    small_sems, small_thru, after = _gather_small_start(small_zones, grad_x, "ag_small_start")

    for item in reducing[:-1]:
        after = finish_reduce(*item, after)

    got = _gather_small_wait(small_sems, small_thru, after, "ag_small_wait")
    got = [got[0].reshape(NDEV, L, B, 6 * D), got[1]]
    like = [z1] + [args[n] for n in _SMALL] + [zc]
    packs = [_pack([z1] + [args[p + n] for n in _SMALL] + [zc]) for p in ("", "m_", "v_")]
    sres = [_unpack(p[0], like) for p in _adamw(packs[0][None], got[1], packs[1][None], packs[2][None], "adamw_small",
                                                tr=gpack.shape[0])]
    res = {n: [r[1 + i] for r in sres] for i, n in enumerate(_SMALL)}
    loss = sres[0][0][0, 0]
    gcw = lax.dynamic_slice_in_dim(sres[0][-1], me * (CONV_CH // NDEV), CONV_CH // NDEV, axis=2)

    def update(name, grads, tr):
        r = None
        for l, g in enumerate(grads):
            r = _adamw(args[name], g[None], args["m_" + name], args["v_" + name], f"adamw_{name}{l}", tr=tr, layer=l, prev=r)
        res[name] = r

    update("conv_w", [gcw[l] for l in range(L)], 4)

    dmod_all = jnp.transpose(got[0], (1, 0, 2, 3)).reshape(L, NDEV * B, 6 * D)
    dm_mine = lax.dynamic_slice_in_dim(dmod_all, me * (6 * D // NDEV), 6 * D // NDEV, axis=2)
    dm_pad = jnp.pad(dm_mine, ((0, 0), (0, 128 - nb_rows), (0, 0))).astype(BF16)
    update("ada_w", [_mm(c_pad, dm_pad[l], mode="tn", name=f"ada_dw{l}", tn=768) for l in range(L)], 256)

    finish_reduce(*reducing[-1], res["ada_w"][0])
    for n in big:
        res[n] = [a.reshape(args[n].shape) for a in big_res[n]]

    names = ['ada_w', 'ada_b', 'norm1_g', 'w_in', 'gm_ln_g', 'gm_ln_b', 'gm_ws', 'gm_bs', 'gm_norm_g', 'attn_sinks',
             'attn_norm_g', 'conv_w', 'conv_b', 'dt_bias', 'a_log', 'd_skip', 'ssm_norm_g', 'w_out', 'norm2_g', 'w_mlp1',
             'w_mlp2', 'final_norm_g']
    return (loss, grad_x, *[res[n][0] for n in names], *[res[n][1] for n in names], *[res[n][2] for n in names],
            *[res[n][3] for n in names])
```

```python
import jax
import jax.numpy as jnp
import numpy as np
from jax import lax
from jax.experimental import pallas as pl
from jax.experimental.pallas import tpu as pltpu

F32, BF16 = jnp.float32, jnp.bfloat16
MESH = pl.DeviceIdType.MESH
NDEV = 8

D = 2048
DEPTH = 2
CHUNK = 128
GM_W, GM_H = 512, 4
ATT_W, KV_W = 512, 128
SSM_W, SSM_H, SSM_HD, SSM_G = 1024, 16, 64, 2
CONV_CH = 1536
IN_W = 4368
DFF = 8192
EPS = 1e-6
NEG_INF = -1e30
GELU_K = 0.7978845608028654
GELU_C = 0.044715

_ORIG = (("u", 512), ("v", 512), ("q", 512), ("k", 128), ("vv", 128), ("z", 1024), ("xbc", 1536), ("dt", 16))
OFF = dict(u=0, v=512, q=1024, k=1536, vv=1664, z=1792, xbc=2816, dt=4352)
PW = 4608

ADAM_LR, ADAM_B1, ADAM_B2, ADAM_EPS, ADAM_WD, ADAM_STEP = 0.001, 0.9, 0.999, 1e-08, 0.01, 10


def _concat_cols(pieces, width, name, ts=256):
    B, S, _ = pieces[0].shape
    ws = [p.shape[-1] for p in pieces]
    dt = pieces[0].dtype
    n = len(pieces)

    def body(*refs):
        cols = [r[...] for r in refs[:n]]
        if width > sum(ws):
            cols.append(jnp.zeros((ts, width - sum(ws)), dt))
        refs[n][...] = jnp.concatenate(cols, axis=1)

    return pl.pallas_call(
        body, name=name, grid=(B, S // ts), in_specs=[pl.BlockSpec((None, ts, w), lambda b, i: (b, i, 0)) for w in ws],
        out_specs=pl.BlockSpec((None, ts, width), lambda b, i: (b, i, 0)), out_shape=jax.ShapeDtypeStruct((B, S, width), dt))(*pieces)


def _shards_to_cols(g, name, tr=256):
    n, R, C = g.shape

    def body(g_ref, o_ref):
        o_ref[...] = jnp.concatenate([g_ref[s] for s in range(n)] + [jnp.zeros((tr, PW - n * C), g.dtype)], axis=1)

    return pl.pallas_call(body, name=name, grid=(R // tr,), in_specs=[pl.BlockSpec((n, tr, C), lambda i: (0, i, 0))],
                          out_specs=pl.BlockSpec((tr, PW), lambda i: (i, 0)), out_shape=jax.ShapeDtypeStruct((R, PW), g.dtype))(g)


def _cols_to_my_shards(w, w_sib, core, name, tr=256):
    R, C = w.shape[0], IN_W // NDEV

    def body(core_ref, w_ref, s_ref, o_ref, o2_ref):
        x = w_ref[...].astype(F32) + s_ref[...].astype(F32)
        mine_is_odd = core_ref[0] == 1
        for q in range(4):
            blk = jnp.where(mine_is_odd, x[:, C * (2 * q + 1):C * (2 * q + 2)], x[:, C * 2 * q:C * (2 * q + 1)]).astype(o_ref.dtype)
            o_ref[q] = blk
            o2_ref[q] = blk

    row = pl.BlockSpec((tr, PW), lambda i, c: (i, 0))
    out = pl.BlockSpec((4, tr, C), lambda i, c: (0, i, 0))
    return pl.pallas_call(
        body, name=name, out_shape=[jax.ShapeDtypeStruct((4, R, C), w.dtype)] * 2,
        grid_spec=pltpu.PrefetchScalarGridSpec(num_scalar_prefetch=1, grid=(R // tr,), in_specs=[row, row], out_specs=[out, out]),
    )(core, w, w_sib)


def _sigmoid(x):
    return 0.5 * (jnp.tanh(0.5 * x) + 1.0)


def _gelu(x):
    return 0.5 * x * (1.0 + jnp.tanh(GELU_K * (x + GELU_C * x * x * x)))


def _gelu_grad(x):
    t = jnp.tanh(GELU_K * (x + GELU_C * x * x * x))
    return 0.5 * (1.0 + t) + 0.5 * x * (1.0 - t * t) * GELU_K * (1.0 + 3.0 * GELU_C * x * x)


def _dot(a, b, prec=None):
    return jnp.dot(a, b, precision=prec, preferred_element_type=F32)


def _dot_nt(a, b, prec=None):
    return lax.dot_general(a, b, (((1,), (1,)), ((), ())), precision=prec, preferred_element_type=F32)


def _dot_tn(a, b, prec=None):
    return lax.dot_general(a, b, (((0,), (0,)), ((), ())), precision=prec, preferred_element_type=F32)


def _full(shape):
    return pl.BlockSpec(shape, lambda *_: (0,) * len(shape))


_HBM = pl.BlockSpec(memory_space=pltpu.HBM)


def _me():
    return lax.axis_index("x"), lax.axis_index("y"), lax.axis_index("c")


def _peer(k):
    x, y, c = _me()
    px = 1 - x if k & 4 else x
    py = 1 - y if k & 2 else y
    pc = 1 - c if k & 1 else c
    return (px, py, pc), 4 * px + 2 * py + pc


def _gather_small(xs, name, order=()):
    n = len(xs)

    def body(*refs):
        ins, outs = refs[:n], refs[-n - 3:-3]
        send, recv, loc = refs[-3:]
        x, y, c = _me()
        me = 4 * x + 2 * y + c
        started = []
        for i in range(n):
            own = pltpu.make_async_copy(ins[i], outs[i].at[me], loc.at[i])
            own.start()
            started.append(own)
        for k in range(1, NDEV):
            dev, lin = _peer(k)
            for i in range(n):
                pltpu.make_async_remote_copy(
                    src_ref=ins[i], dst_ref=outs[i].at[me],
                    send_sem=send.at[i, k - 1], recv_sem=recv.at[i, k - 1], device_id=dev, device_id_type=MESH).start()
        for k in range(1, NDEV):
            dev, lin = _peer(k)
            for i in range(n):
                pltpu.make_async_remote_copy(
                    src_ref=ins[i], dst_ref=outs[i].at[lin],
                    send_sem=send.at[i, k - 1], recv_sem=recv.at[i, k - 1], device_id=dev, device_id_type=MESH).wait()
        for own in started:
            own.wait()

    extra = list(order)
    return pl.pallas_call(
        body, name=name, out_shape=[jax.ShapeDtypeStruct((NDEV,) + a.shape, a.dtype) for a in xs],
        in_specs=[_HBM] * n + [pl.BlockSpec(memory_space=pl.ANY)] * len(extra), out_specs=[_HBM] * n,
        scratch_shapes=[pltpu.SemaphoreType.DMA((n, NDEV - 1)), pltpu.SemaphoreType.DMA((n, NDEV - 1)),
                        pltpu.SemaphoreType.DMA((n,))],
        compiler_params=pltpu.CompilerParams(has_side_effects=True),
    )(*xs, *extra)


def _gather_small_start(lands, order, name):
    n = len(lands)

    def body(*refs):
        ins, sems, token = refs[:n], refs[n + 1:n + 1 + 14 * n], refs[-1]
        x, y, c = _me()
        me = 4 * x + 2 * y + c
        for i in range(n):
            for k in range(1, NDEV):
                dev, _ = _peer(k)
                pltpu.make_async_remote_copy(src_ref=ins[i].at[me], dst_ref=ins[i].at[me], send_sem=sems[14 * i + k - 1],
                                             recv_sem=sems[14 * i + 7 + k - 1], device_id=dev, device_id_type=MESH).start()
        token[...] = jnp.zeros_like(token)

    outs = pl.pallas_call(
        body, name=name,
        out_shape=(pltpu.SemaphoreType.DMA(()),) * (14 * n) + tuple(pltpu.HBM(a.shape, a.dtype) for a in lands)
        + (jax.ShapeDtypeStruct((8, 128), F32),),
        in_specs=(_HBM,) * n + (_ANY,), out_specs=(_SEM,) * (14 * n) + (_HBM,) * n + (pl.BlockSpec(memory_space=pltpu.VMEM),),
        input_output_aliases={i: 14 * n + i for i in range(n)}, compiler_params=pltpu.CompilerParams(has_side_effects=_DATAFLOW),
    )(*[_hbm(a) for a in lands], order)
    return outs[:14 * n], outs[14 * n:15 * n], outs[-1]


def _gather_small_wait(sems, lands_thru, after, name):
    n = len(lands_thru)

    def body(*refs):
        ins, sems_ = refs[:n], refs[n:n + 14 * n]
        x, y, c = _me()
        me = 4 * x + 2 * y + c
        for i in range(n):
            for k in range(1, NDEV):
                dev, lin = _peer(k)
                cp = pltpu.make_async_remote_copy(src_ref=ins[i].at[me], dst_ref=ins[i].at[lin], send_sem=sems_[14 * i + k - 1],
                                                  recv_sem=sems_[14 * i + 7 + k - 1], device_id=dev, device_id_type=MESH)
                cp.wait_send()
                cp.wait_recv()

    outs = pl.pallas_call(
        body, name=name, out_shape=tuple(pltpu.HBM(a.shape, a.dtype) for a in lands_thru),
        in_specs=(_HBM,) * n + (_SEM,) * (14 * n) + (_ANY,), out_specs=(_HBM,) * n,
        input_output_aliases={i: i for i in range(n)}, compiler_params=pltpu.CompilerParams(has_side_effects=_DATAFLOW),
    )(*lands_thru, *sems, after)
    return outs


def _chips():
    x, y, c = _me()
    return x, y, c, [(1 - x, y), (x, 1 - y), (1 - x, 1 - y)]


def _gather2(xs, name, order=None):
    n = len(xs)
    extra = [] if order is None else [order]

    def body(*refs):
        ins, outs = refs[:n], refs[-n - 3:-3]
        send, recv, loc = refs[-3:]
        x, y, c, chips = _chips()
        me, sib = (x, y, c), (x, y, 1 - c)

        def cp(i, k, block, to, src=None):
            slot = outs[i].at[4 * block[0] + 2 * block[1] + block[2]]
            return pltpu.make_async_remote_copy(src_ref=slot if src is None else src, dst_ref=slot, send_sem=send.at[i, k],
                                                recv_sem=recv.at[i, k], device_id=to, device_id_type=MESH)

        sent = []
        for i in range(n):
            for j, chip in enumerate(chips):
                sent.append(cp(i, 1 + j, me, (*chip, c), src=ins[i]))
            sent.append(cp(i, 0, me, sib, src=ins[i]))
        for s in sent:
            s.start()
        own = [pltpu.make_async_copy(ins[i], outs[i].at[4 * x + 2 * y + c], loc.at[i]) for i in range(n)]
        for o in own:
            o.start()
        for j, chip in enumerate(chips):
            for i in range(n):
                cp(i, 1 + j, (*chip, c), me).wait_recv()
                fwd = cp(i, 4 + j, (*chip, c), sib)
                fwd.start()
                sent.append(fwd)
        for i in range(n):
            cp(i, 0, sib, me).wait_recv()
            for j, chip in enumerate(chips):
                cp(i, 4 + j, (*chip, 1 - c), me).wait_recv()
        for s in sent:
            s.wait_send()
        for o in own:
            o.wait()

    return pl.pallas_call(
        body, name=name, out_shape=[jax.ShapeDtypeStruct((NDEV,) + a.shape, a.dtype) for a in xs],
        in_specs=[_HBM] * n + [pl.BlockSpec(memory_space=pl.ANY)] * len(extra), out_specs=[_HBM] * n,
        scratch_shapes=[pltpu.SemaphoreType.DMA((n, 7)), pltpu.SemaphoreType.DMA((n, 7)), pltpu.SemaphoreType.DMA((n,))],
        compiler_params=pltpu.CompilerParams(has_side_effects=True),
    )(*xs, *extra)


def _pair_add(p, r1, core, name, tr=256):
    _, _, R, C = p.shape
    tr = min(tr, R)

    def body(core_ref, p_ref, r_ref, o_ref, o2_ref):
        s = (p_ref[...].astype(F32) + r_ref[...].astype(F32)).astype(o_ref.dtype)
        o_ref[...] = s
        o2_ref[...] = s

    blk = pl.BlockSpec((None, tr, C), lambda ch, i, core_ref: (ch, i, 0))
    return pl.pallas_call(
        body, name=name, out_shape=[jax.ShapeDtypeStruct((4, R, C), p.dtype)] * 2,
        grid_spec=pltpu.PrefetchScalarGridSpec(
            num_scalar_prefetch=1, grid=(4, R // tr),
            in_specs=[pl.BlockSpec((None, None, tr, C), lambda ch, i, core_ref: (ch, core_ref[0], i, 0)), blk],
            out_specs=[blk, blk]),
    )(core, p, r1)


_SEM = pl.BlockSpec(memory_space=pltpu.SEMAPHORE)
_ANY = pl.BlockSpec(memory_space=pl.ANY)
_DATAFLOW = pltpu.SideEffectType.DATAFLOW_SIDE_EFFECTING


def _hbm(a):
    return pltpu.with_memory_space_constraint(a, pltpu.HBM)


def _gather_targets():
    x, y, c, chips = _chips()
    return 4 * x + 2 * y + c, [(x, y, 1 - c)] + [(*chip, c) for chip in chips]


def _landing_zone(w, l, me, name, tr=512, dtype=BF16):
    _, R, C = w.shape
    tr = min(tr, R)

    def body(me_ref, w_ref, o_ref):
        o_ref[...] = w_ref[...].astype(dtype)

    return pl.pallas_call(
        body, name=name, out_shape=jax.ShapeDtypeStruct((NDEV, R, C), dtype),
        grid_spec=pltpu.PrefetchScalarGridSpec(
            num_scalar_prefetch=1, grid=(R // tr,), in_specs=[pl.BlockSpec((None, tr, C), lambda i, me_ref: (l, i, 0))],
            out_specs=pl.BlockSpec((None, tr, C), lambda i, me_ref: (me_ref[0], i, 0))),
    )(me, w)


def _gather_start(land, order, name):
    def body(land_ref, order_ref, *rest):
        sems, token = rest[:8], rest[9]
        me, targets = _gather_targets()
        for k, to in enumerate(targets):
            pltpu.make_async_remote_copy(src_ref=land_ref.at[me], dst_ref=land_ref.at[me], send_sem=sems[k],
                                         recv_sem=sems[4 + k], device_id=to, device_id_type=MESH).start()
        token[...] = jnp.zeros_like(token)

    outs = pl.pallas_call(
        body, name=name,
        out_shape=(pltpu.SemaphoreType.DMA(()),) * 8 + (pltpu.HBM(land.shape, land.dtype), jax.ShapeDtypeStruct((8, 128), F32)),
        in_specs=(_HBM, _ANY), out_specs=(_SEM,) * 8 + (_HBM, pl.BlockSpec(memory_space=pltpu.VMEM)),
        input_output_aliases={0: 8}, compiler_params=pltpu.CompilerParams(has_side_effects=_DATAFLOW),
    )(_hbm(land), order)
    return outs[:8], outs[8], outs[9]


def _gather_wait(sems, land_thru, after, name):
    def body(land_ref, *rest):
        sems_ = rest[:8]
        me, targets = _gather_targets()
        for k, to in enumerate(targets):
            cp = pltpu.make_async_remote_copy(src_ref=land_ref.at[me], dst_ref=land_ref.at[me], send_sem=sems_[k],
                                              recv_sem=sems_[4 + k], device_id=to, device_id_type=MESH)
            cp.wait_send()
            cp.wait_recv()

    return pl.pallas_call(
        body, name=name, out_shape=pltpu.HBM(land_thru.shape, land_thru.dtype),
        in_specs=(_HBM,) + (_SEM,) * 8 + (_ANY,), out_specs=_HBM, input_output_aliases={0: 0},
        compiler_params=pltpu.CompilerParams(has_side_effects=_DATAFLOW),
    )(land_thru, *sems, after)


def _gather_finish(land, name):
    def body(land_ref, out, send, recv):
        x, y, c, chips = _chips()
        fwd = [pltpu.make_async_remote_copy(src_ref=out.at[4 * px + 2 * py + c], dst_ref=out.at[4 * px + 2 * py + c],
                                            send_sem=send.at[j], recv_sem=recv.at[j], device_id=(x, y, 1 - c), device_id_type=MESH)
               for j, (px, py) in enumerate(chips)]
        for cp in fwd:
            cp.start()
        for j, (px, py) in enumerate(chips):
            slot = out.at[4 * px + 2 * py + 1 - c]
            pltpu.make_async_remote_copy(src_ref=slot, dst_ref=slot, send_sem=send.at[j], recv_sem=recv.at[j],
                                         device_id=(x, y, 1 - c), device_id_type=MESH).wait()

    return pl.pallas_call(
        body, name=name, out_shape=jax.ShapeDtypeStruct(land.shape, land.dtype),
        in_specs=[_HBM], out_specs=_HBM, input_output_aliases={0: 0},
        scratch_shapes=[pltpu.SemaphoreType.DMA((3,)), pltpu.SemaphoreType.DMA((3,))],
        compiler_params=pltpu.CompilerParams(has_side_effects=True),
    )(land)


def _forward_start(land, order, name):
    def body(land_ref, order_ref, *rest):
        sems, token = rest[:6], rest[7]
        x, y, c, chips = _chips()
        for j, (px, py) in enumerate(chips):
            slot = land_ref.at[4 * px + 2 * py + c]
            pltpu.make_async_remote_copy(src_ref=slot, dst_ref=slot, send_sem=sems[j], recv_sem=sems[3 + j],
                                         device_id=(x, y, 1 - c), device_id_type=MESH).start()
        token[...] = jnp.zeros_like(token)

    outs = pl.pallas_call(
        body, name=name,
        out_shape=(pltpu.SemaphoreType.DMA(()),) * 6 + (pltpu.HBM(land.shape, land.dtype), jax.ShapeDtypeStruct((8, 128), F32)),
        in_specs=(_HBM, _ANY), out_specs=(_SEM,) * 6 + (_HBM, pl.BlockSpec(memory_space=pltpu.VMEM)),
        input_output_aliases={0: 6}, compiler_params=pltpu.CompilerParams(has_side_effects=_DATAFLOW),
    )(_hbm(land), order)
    return outs[:6], outs[6], outs[7]


def _forward_wait(sems, land_thru, after, name):
    def body(land_ref, *rest):
        sems_ = rest[:6]
        x, y, c, chips = _chips()
        for j, (px, py) in enumerate(chips):
            cp = pltpu.make_async_remote_copy(src_ref=land_ref.at[4 * px + 2 * py + c], dst_ref=land_ref.at[4 * px + 2 * py + 1 - c],
                                              send_sem=sems_[j], recv_sem=sems_[3 + j], device_id=(x, y, 1 - c),
                                              device_id_type=MESH)
            cp.wait_send()
            cp.wait_recv()

    return pl.pallas_call(
        body, name=name, out_shape=pltpu.HBM(land_thru.shape, land_thru.dtype),
        in_specs=(_HBM,) + (_SEM,) * 6 + (_ANY,), out_specs=_HBM, input_output_aliases={0: 0},
        compiler_params=pltpu.CompilerParams(has_side_effects=_DATAFLOW),
    )(land_thru, *sems, after)


def _chip_targets():
    x, y, c, chips = _chips()
    return 2 * x + y, [((px, py, c), 2 * px + py) for px, py in chips]


def _chipsum_start(s, land, order, name):
    def body(s_ref, land_ref, order_ref, *rest):
        sems, token = rest[:6], rest[8]
        mine, targets = _chip_targets()
        for k, (to, ch) in enumerate(targets):
            pltpu.make_async_remote_copy(src_ref=s_ref.at[ch], dst_ref=land_ref.at[mine], send_sem=sems[k], recv_sem=sems[3 + k],
                                         device_id=to, device_id_type=MESH).start()
        token[...] = jnp.zeros_like(token)

    outs = pl.pallas_call(
        body, name=name,
        out_shape=(pltpu.SemaphoreType.DMA(()),) * 6 + (pltpu.HBM(s.shape, s.dtype), pltpu.HBM(land.shape, land.dtype),
                                                        jax.ShapeDtypeStruct((8, 128), F32)),
        in_specs=(_HBM, _HBM, _ANY), out_specs=(_SEM,) * 6 + (_HBM, _HBM, pl.BlockSpec(memory_space=pltpu.VMEM)),
        input_output_aliases={0: 6, 1: 7}, compiler_params=pltpu.CompilerParams(has_side_effects=_DATAFLOW),
    )(_hbm(s), _hbm(land), order)
    return outs[:6], outs[6], outs[7], outs[8]


def _chipsum_wait(sems, s_thru, land_thru, after, name):
    def body(s_ref, land_ref, *rest):
        sems_ = rest[:6]
        mine, targets = _chip_targets()
        for k, (to, ch) in enumerate(targets):
            cp = pltpu.make_async_remote_copy(src_ref=s_ref.at[ch], dst_ref=land_ref.at[ch], send_sem=sems_[k], recv_sem=sems_[3 + k],
                                              device_id=to, device_id_type=MESH)
            cp.wait_send()
            cp.wait_recv()

    return pl.pallas_call(
        body, name=name, out_shape=(pltpu.HBM(s_thru.shape, s_thru.dtype), pltpu.HBM(land_thru.shape, land_thru.dtype)),
        in_specs=(_HBM, _HBM) + (_SEM,) * 6 + (_ANY,), out_specs=(_HBM, _HBM), input_output_aliases={0: 0, 1: 1},
        compiler_params=pltpu.CompilerParams(has_side_effects=_DATAFLOW),
    )(s_thru, land_thru, *sems, after)[1]


def _pair_start(p, order, name):
    def body(p_ref, land_ref, order_ref, *rest):
        sems, token = rest[:8], rest[10]
        x, y, c = _me()
        for ch in range(4):
            pltpu.make_async_remote_copy(src_ref=p_ref.at[ch, 1 - c], dst_ref=land_ref.at[ch], send_sem=sems[ch],
                                         recv_sem=sems[4 + ch], device_id=(x, y, 1 - c), device_id_type=MESH).start()
        token[...] = jnp.zeros_like(token)

    land = lax.empty((4,) + p.shape[2:], p.dtype)
    outs = pl.pallas_call(
        body, name=name,
        out_shape=(pltpu.SemaphoreType.DMA(()),) * 8 + (pltpu.HBM(p.shape, p.dtype), pltpu.HBM(land.shape, land.dtype),
                                                        jax.ShapeDtypeStruct((8, 128), F32)),
        in_specs=(_HBM, _HBM, _ANY), out_specs=(_SEM,) * 8 + (_HBM, _HBM, pl.BlockSpec(memory_space=pltpu.VMEM)),
        input_output_aliases={0: 8, 1: 9}, compiler_params=pltpu.CompilerParams(has_side_effects=_DATAFLOW),
    )(_hbm(p), _hbm(land), order)
    return outs[:8], outs[8], outs[9], outs[10]


def _pair_wait(sems, p_thru, land_thru, after, name):
    def body(p_ref, land_ref, *rest):
        sems_ = rest[:8]
        x, y, c = _me()
        for ch in range(4):
            cp = pltpu.make_async_remote_copy(src_ref=p_ref.at[ch, 1 - c], dst_ref=land_ref.at[ch], send_sem=sems_[ch],
                                              recv_sem=sems_[4 + ch], device_id=(x, y, 1 - c), device_id_type=MESH)
            cp.wait_send()
            cp.wait_recv()

    return pl.pallas_call(
        body, name=name, out_shape=(pltpu.HBM(p_thru.shape, p_thru.dtype), pltpu.HBM(land_thru.shape, land_thru.dtype)),
        in_specs=(_HBM, _HBM) + (_SEM,) * 8 + (_ANY,), out_specs=(_HBM, _HBM), input_output_aliases={0: 0, 1: 1},
        compiler_params=pltpu.CompilerParams(has_side_effects=_DATAFLOW),
    )(p_thru, land_thru, *sems, after)


def _sibling_start(p, order, name):
    def body(p_ref, land_ref, order_ref, send_sem, recv_sem, p_thru, land_thru, token):
        x, y, c = _me()
        pltpu.make_async_remote_copy(src_ref=p_ref, dst_ref=land_ref, send_sem=send_sem, recv_sem=recv_sem,
                                     device_id=(x, y, 1 - c), device_id_type=MESH).start()
        token[...] = jnp.zeros_like(token)

    land = lax.empty(p.shape, p.dtype)
    outs = pl.pallas_call(
        body, name=name,
        out_shape=(pltpu.SemaphoreType.DMA(()),) * 2 + (pltpu.HBM(p.shape, p.dtype), pltpu.HBM(p.shape, p.dtype),
                                                        jax.ShapeDtypeStruct((8, 128), F32)),
        in_specs=(_HBM, _HBM, _ANY), out_specs=(_SEM,) * 2 + (_HBM, _HBM, pl.BlockSpec(memory_space=pltpu.VMEM)),
        input_output_aliases={0: 2, 1: 3}, compiler_params=pltpu.CompilerParams(has_side_effects=_DATAFLOW),
    )(_hbm(p), _hbm(land), order)
    return outs[:2], outs[2], outs[3], outs[4]


def _sibling_wait(sems, p_thru, land_thru, after, name):
    def body(p_ref, land_ref, send_sem, recv_sem, after_ref, p_dead, got_ref):
        x, y, c = _me()
        cp = pltpu.make_async_remote_copy(src_ref=p_ref, dst_ref=land_ref, send_sem=send_sem, recv_sem=recv_sem,
                                          device_id=(x, y, 1 - c), device_id_type=MESH)
        cp.wait_send()
        cp.wait_recv()

    return pl.pallas_call(
        body, name=name, out_shape=(pltpu.HBM(p_thru.shape, p_thru.dtype), pltpu.HBM(land_thru.shape, land_thru.dtype)),
        in_specs=(_HBM, _HBM, _SEM, _SEM, _ANY), out_specs=(_HBM, _HBM), input_output_aliases={0: 0, 1: 1},
        compiler_params=pltpu.CompilerParams(has_side_effects=_DATAFLOW),
    )(p_thru, land_thru, *sems, after)


def _mm(a, b, *, mode, name, out_dtypes=(F32,), epilogue=None, extras=(), tm=1024, tn=1024, tk=2048,
        col_blocked_b=False, col_blocked_out=False, order=None):
    CB = 1024
    if col_blocked_b:
        assert mode in ("nn", "nt") and b.shape[2] == CB
        (M, K), N = a.shape, (b.shape[0] * CB if mode == "nn" else b.shape[1])
        assert mode == "nn" or tk % CB == 0
        tn = CB if mode == "nn" else tn
    elif mode == "nn":
        (M, K), N = a.shape, b.shape[1]
    elif mode == "nt":
        (M, K), N = a.shape, b.shape[0]
    else:
        (K, M), N = a.shape, b.shape[1]
    if col_blocked_out:
        assert len(out_dtypes) == 1 and N % CB == 0
        tn = CB
    tm, tn, tk = min(tm, M), min(tn, N), min(tk, K)
    assert M % tm == 0 and N % tn == 0 and K % tk == 0, (M, N, K, tm, tn, tk)
    nk = K // tk
    ne, no = len(extras), len(out_dtypes)
    dims = {"nn": (((1,), (0,)), ((), ())), "nt": (((1,), (1,)), ((), ())), "tn": (((0,), (0,)), ((), ()))}[mode]

    no_ = 0 if order is None else 1

    def body(a_ref, b_ref, *rest):
        rest = rest[no_:]
        ex, outs = rest[:ne], rest[ne:ne + no]

        def finish(acc):
            res = epilogue(acc, *[e[...] for e in ex]) if epilogue is not None else (acc,)
            for o, r in zip(outs, res):
                o[...] = r.astype(o.dtype)

        if col_blocked_b and mode == "nt":
            part = sum(lax.dot_general(a_ref[:, q * CB:(q + 1) * CB], b_ref[q], dims, preferred_element_type=F32)
                       for q in range(tk // CB))
        else:
            part = lax.dot_general(a_ref[...], b_ref[...].astype(BF16), dims, preferred_element_type=F32)
        if nk == 1:
            finish(part)
        else:
            acc_ref = rest[-1]
            k = pl.program_id(2)

            @pl.when(k == 0)
            def _():
                acc_ref[...] = part

            @pl.when(k > 0)
            def _():
                acc_ref[...] += part

            @pl.when(k == nk - 1)
            def _():
                finish(acc_ref[...])

    a_spec = {"nn": pl.BlockSpec((tm, tk), lambda i, j, k: (i, k)), "nt": pl.BlockSpec((tm, tk), lambda i, j, k: (i, k)),
              "tn": pl.BlockSpec((tk, tm), lambda i, j, k: (k, i))}[mode]
    b_spec = {"nn": pl.BlockSpec((tk, tn), lambda i, j, k: (k, j)), "nt": pl.BlockSpec((tn, tk), lambda i, j, k: (j, k)),
              "tn": pl.BlockSpec((tk, tn), lambda i, j, k: (k, j))}[mode]
    if col_blocked_b:
        b_spec = (pl.BlockSpec((None, tk, CB), lambda i, j, k: (j, k, 0)) if mode == "nn"
                  else pl.BlockSpec((tk // CB, tn, CB), lambda i, j, k: (k, j, 0)))
    e_spec = pl.BlockSpec((tm, tn), lambda i, j, k: (i, j))
    o_spec, o_dims = e_spec, (M, N)
    if col_blocked_out:
        o_spec, o_dims = pl.BlockSpec((None, tm, CB), lambda i, j, k: (j, i, 0)), (N // CB, M, CB)
    outs = pl.pallas_call(
        body, name=name, grid=(M // tm, N // tn, nk),
        in_specs=[a_spec, b_spec] + [_ANY] * no_ + [e_spec] * ne, out_specs=[o_spec] * no,
        out_shape=[jax.ShapeDtypeStruct(o_dims, dt) for dt in out_dtypes],
        scratch_shapes=[pltpu.VMEM((tm, tn), F32)] if nk > 1 else [],
        compiler_params=pltpu.CompilerParams(dimension_semantics=("parallel", "parallel", "arbitrary")),
    )(a, b, *([] if order is None else [order]), *extras)
    return outs if no > 1 else outs[0]


def _dw_half(a, b, side, *, axis, name, add=None, order=None, tile=1024, tk=4096):
    (K, M), N = a.shape, b.shape[1]
    tk = min(tk, K)
    nk = K // tk
    if axis == "m":
        tm, tn = tile, min(N, 1024)
        grid, o_dims = (4, N // tn, nk), (4, tile, N)
        a_spec = pl.BlockSpec((tk, tm), lambda q, j, k, s: (k, 2 * q + s[0]))
        b_spec = pl.BlockSpec((tk, tn), lambda q, j, k, s: (k, j))
        o_spec = pl.BlockSpec((None, tm, tn), lambda q, j, k, s: (q, 0, j))
    else:
        tm, tn = min(M, 1024), tile
        grid, o_dims = (M // tm, 4, nk), (4, M, tile)
        a_spec = pl.BlockSpec((tk, tm), lambda i, q, k, s: (k, i))
        b_spec = pl.BlockSpec((tk, tn), lambda i, q, k, s: (k, 2 * q + s[0]))
        o_spec = pl.BlockSpec((None, tm, tn), lambda i, q, k, s: (q, i, 0))
    n_order, n_add = int(order is not None), int(add is not None)
    n_out = 1 + n_add

    def body(s_ref, a_ref, b_ref, *rest):
        rest = rest[n_order:]
        outs = rest[n_add:n_add + n_out]

        def finish(acc):
            res = acc + rest[0][...].astype(F32) if n_add else acc
            for o in outs:
                o[...] = res.astype(o.dtype)

        part = _dot_tn(a_ref[...], b_ref[...])
        if nk == 1:
            finish(part)
        else:
            acc_ref, k = rest[-1], pl.program_id(2)

            @pl.when(k == 0)
            def _():
                acc_ref[...] = part

            @pl.when(k > 0)
            def _():
                acc_ref[...] += part

            @pl.when(k == nk - 1)
            def _():
                finish(acc_ref[...])

    outs = pl.pallas_call(
        body, name=name, out_shape=[jax.ShapeDtypeStruct(o_dims, BF16)] * n_out,
        grid_spec=pltpu.PrefetchScalarGridSpec(
            num_scalar_prefetch=1, grid=grid, in_specs=[a_spec, b_spec] + [_ANY] * n_order + [o_spec] * n_add,
            out_specs=[o_spec] * n_out, scratch_shapes=[pltpu.VMEM((tm, tn), F32)] if nk > 1 else []),
        compiler_params=pltpu.CompilerParams(dimension_semantics=("arbitrary", "arbitrary", "arbitrary")),
    )(side, a, b, *([order] if n_order else []), *([add] if n_add else []))
    return outs if n_add else outs[0]


def _norm_fwd(x, g, sc, sh, name, resid=None):
    B, S, Dm = x.shape
    ts = min(S, 256)
    tok = pl.BlockSpec((None, ts, Dm), lambda b, i: (b, i, 0))
    row = pl.BlockSpec((None, 1, Dm), lambda b, i: (b, 0, 0))
    par = pl.BlockSpec((1, Dm), lambda b, i: (0, 0))

    def body(*refs):
        if resid is not None:
            x_ref, br_ref, gt_ref, g_ref, sc_ref, sh_ref, xo_ref, h_ref = refs
            xv = x_ref[...] + gt_ref[...] * br_ref[...]
            xo_ref[...] = xv
        else:
            x_ref, g_ref, sc_ref, sh_ref, h_ref = refs
            xv = x_ref[...]
        r = lax.rsqrt(jnp.mean(xv * xv, axis=-1, keepdims=True) + EPS)
        h_ref[...] = ((xv * r * g_ref[...]) * (1.0 + sc_ref[...]) + sh_ref[...]).astype(BF16)

    h_shape = jax.ShapeDtypeStruct((B, S, Dm), BF16)
    if resid is not None:
        return pl.pallas_call(body, name=name, grid=(B, S // ts), in_specs=[tok, tok, row, par, row, row],
                              out_specs=[tok, tok], out_shape=[jax.ShapeDtypeStruct((B, S, Dm), F32), h_shape],
                              )(x, resid[0], resid[1], g, sc, sh)
    return pl.pallas_call(body, name=name, grid=(B, S // ts), in_specs=[tok, par, row, row], out_specs=tok,
                          out_shape=h_shape)(x, g, sc, sh)


def _norm_bwd(x, g, name, *, sc=None, dh=None, dres=None, tgt=None, br=None, gate=None, x_is_prev=False):
    B, S, Dm = x.shape
    ts = min(S, 256)
    final = tgt is not None
    has_br = br is not None
    tok = pl.BlockSpec((None, ts, Dm), lambda b, i: (b, i, 0))
    row = pl.BlockSpec((None, 1, Dm), lambda b, i: (b, 0, 0))
    par = pl.BlockSpec((1, Dm), lambda b, i: (0, 0))
    ins, in_specs = [x, g], [tok, par]
    if final:
        ins, in_specs = ins + [tgt], in_specs + [tok]
    else:
        ins, in_specs = ins + [sc, dh], in_specs + [row, tok]
    if dres is not None:
        ins, in_specs = ins + [dres], in_specs + [tok]
    if has_br:
        ins, in_specs = ins + [br, gate], in_specs + [tok, row]
    n_in = len(ins)
    out_shape = [jax.ShapeDtypeStruct((B, S, Dm), F32), jax.ShapeDtypeStruct((1, Dm), F32)]
    out_specs = [tok, par]
    if final:
        out_shape.append(jax.ShapeDtypeStruct((1, 128), F32))
        out_specs.append(pl.BlockSpec((1, 128), lambda b, i: (0, 0)))
    else:
        out_shape += [jax.ShapeDtypeStruct((B, 1, Dm), F32)] * 2
        out_specs += [row, row]
    if has_br:
        out_shape += [jax.ShapeDtypeStruct((B, S, Dm), BF16), jax.ShapeDtypeStruct((B, 1, Dm), F32)]
        out_specs += [tok, row]

    def body(*refs):
        it = iter(refs[:n_in])
        outs = iter(refs[n_in:])
        x_ref, g_ref = next(it), next(it)
        b, i = pl.program_id(0), pl.program_id(1)
        first, first_row = (b == 0) & (i == 0), i == 0
        xv, gv = x_ref[...], g_ref[...]
        if x_is_prev:
            xv = xv + refs[n_in - 1][...] * refs[n_in - 2][...]
        r = lax.rsqrt(jnp.mean(xv * xv, axis=-1, keepdims=True) + EPS)
        n = xv * r
        dx_ref, dg_ref = next(outs), next(outs)

        def acc(ref, val, init):
            @pl.when(init)
            def _():
                ref[...] = val

            @pl.when(jnp.logical_not(init))
            def _():
                ref[...] += val

        if final:
            t_ref = next(it)
            loss_ref = next(outs)
            e = n * gv - t_ref[...]
            acc(loss_ref, jnp.zeros((1, 128), F32) + 0.5 * jnp.sum(e * e) / Dm, first)
            dyg = e * (1.0 / Dm)
        else:
            sc_ref, dh_ref = next(it), next(it)
            dsc_ref, dsh_ref = next(outs), next(outs)
            dhv = dh_ref[...].astype(F32)
            acc(dsh_ref, jnp.sum(dhv, axis=0, keepdims=True), first_row)
            acc(dsc_ref, jnp.sum(dhv * (n * gv), axis=0, keepdims=True), first_row)
            dyg = dhv * (1.0 + sc_ref[...])
        acc(dg_ref, jnp.sum(dyg * n, axis=0, keepdims=True), first)
        dn = dyg * gv
        dx = r * (dn - n * jnp.mean(dn * n, axis=-1, keepdims=True))
        if dres is not None:
            dx = dx + next(it)[...]
        dx_ref[...] = dx
        if has_br:
            br_ref, gt_ref = next(it), next(it)
            dbr_ref, dgt_ref = next(outs), next(outs)
            dbr_ref[...] = (dx * gt_ref[...]).astype(BF16)
            acc(dgt_ref, jnp.sum(dx * br_ref[...], axis=0, keepdims=True), first_row)

    outs = pl.pallas_call(body, name=name, grid=(B, S // ts), in_specs=in_specs, out_specs=out_specs, out_shape=out_shape,
                          compiler_params=pltpu.CompilerParams(dimension_semantics=("arbitrary", "arbitrary")))(*ins)
    res = dict(dx=outs[0], dg=outs[1])
    if final:
        res["loss"] = outs[2]
    else:
        res["dsc"], res["dsh"] = outs[2], outs[3]
    if has_br:
        res["dbr"], res["dgate"] = outs[-2], outs[-1]
    return res


def _gm_heads(vg, lng, lnb):
    res = []
    for h in range(GM_H):
        sl = slice(h * 128, (h + 1) * 128)
        vh = vg[:, sl]
        xc = vh - jnp.mean(vh, axis=-1, keepdims=True)
        rstd = lax.rsqrt(jnp.mean(xc * xc, axis=-1, keepdims=True) + 1e-5)
        xhat = xc * rstd
        res.append((xhat, rstd, xhat * lng[:, sl] + lnb[:, sl]))
    return res


def _gm_gate(heads, wt_ref, bsx, nch):
    cols = []
    for h in range(GM_H):
        vn = heads[h][2].astype(BF16)
        rows = [_dot(wt_ref[h], vn[c * CHUNK:(c + 1) * CHUNK]) + bsx[:, h * 128:(h + 1) * 128] for c in range(nch)]
        cols.append(jnp.concatenate(rows, axis=0) if nch > 1 else rows[0])
    return jnp.concatenate(cols, axis=1)


def _gm_specs(S):
    tb = min(S, 512)
    u = pl.BlockSpec((None, tb, GM_W), lambda b, i: (b, i, OFF["u"] // GM_W))
    v = pl.BlockSpec((None, tb, GM_W), lambda b, i: (b, i, OFF["v"] // GM_W))
    tok = pl.BlockSpec((None, tb, GM_W), lambda b, i: (b, i, 0))
    return tb, u, v, tok


def _gmlp_fwd(P, lng, lnb, wt, bsx, og, name):
    B, S, _ = P.shape
    tb, u_spec, v_spec, tok = _gm_specs(S)
    nch = tb // CHUNK

    def body(u_ref, v_ref, lng_ref, lnb_ref, wt_ref, bsx_ref, og_ref, o_ref):
        heads = _gm_heads(_gelu(v_ref[...]), lng_ref[...], lnb_ref[...])
        y = _gelu(u_ref[...]) * _gm_gate(heads, wt_ref, bsx_ref[...], nch)
        r = lax.rsqrt(jnp.mean(y * y, axis=-1, keepdims=True) + EPS)
        o_ref[...] = (y * r * og_ref[...]).astype(BF16)

    return pl.pallas_call(
        body, name=name, grid=(B, S // tb),
        in_specs=[u_spec, v_spec, _full((1, GM_W)), _full((1, GM_W)), _full((GM_H, 128, 128)), _full((128, GM_W)), _full((1, GM_W))],
        out_specs=tok, out_shape=jax.ShapeDtypeStruct((B, S, GM_W + ATT_W + SSM_W), BF16))(P, P, lng, lnb, wt, bsx, og)


def _gmlp_bwd(P, dcat, lng, lnb, wt, wtT, bsx, og, name):
    B, S, _ = P.shape
    tb, u_spec, v_spec, tok = _gm_specs(S)
    nch = tb // CHUNK
    do_spec = pl.BlockSpec((None, tb, GM_W), lambda b, i: (b, i, 0))

    def body(u_ref, v_ref, do_ref, lng_ref, lnb_ref, wt_ref, wtT_ref, bsx_ref, og_ref,
             du_ref, dv_ref, dlng_ref, dlnb_ref, dws_ref, dbsx_ref, dog_ref):
        first = (pl.program_id(0) == 0) & (pl.program_id(1) == 0)

        @pl.when(first)
        def _():
            for ref in (dlng_ref, dlnb_ref, dws_ref, dbsx_ref, dog_ref):
                ref[...] = jnp.zeros(ref.shape, F32)

        u, v, lng = u_ref[...], v_ref[...], lng_ref[...]
        ug = _gelu(u)
        heads = _gm_heads(_gelu(v), lng, lnb_ref[...])
        gate = _gm_gate(heads, wt_ref, bsx_ref[...], nch)
        y = ug * gate
        r = lax.rsqrt(jnp.mean(y * y, axis=-1, keepdims=True) + EPS)
        yn = y * r
        dout = do_ref[...].astype(F32)
        dog_ref[...] += jnp.sum(dout * yn, axis=0, keepdims=True)
        dyn = dout * og_ref[...]
        dy = r * (dyn - yn * jnp.mean(dyn * yn, axis=-1, keepdims=True))
        du_ref[...] = (dy * gate * _gelu_grad(u)).astype(BF16)
        dgate = dy * ug
        tril = lax.broadcasted_iota(jnp.int32, (128, 128), 0) >= lax.broadcasted_iota(jnp.int32, (128, 128), 1)
        dvg = []
        for h in range(GM_H):
            sl = slice(h * 128, (h + 1) * 128)
            xhat, rstd, vn = heads[h]
            vnb = vn.astype(BF16)
            dgh = dgate[:, sl]
            dgb = dgh.astype(BF16)
            dbs = jnp.zeros((128, 128), F32)
            dw = jnp.zeros((128, 128), F32)
            dvn = []
            for c in range(nch):
                rs = slice(c * CHUNK, (c + 1) * CHUNK)
                dbs = dbs + dgh[rs]
                dw = dw + _dot_nt(dgb[rs], vnb[rs])
                dvn.append(_dot(wtT_ref[h], dgb[rs]))
            dvn = jnp.concatenate(dvn, axis=0) if nch > 1 else dvn[0]
            dbsx_ref[:, sl] += dbs
            dws_ref[h] += jnp.where(tril, dw, 0.0)
            dlng_ref[:, sl] += jnp.sum(dvn * xhat, axis=0, keepdims=True)
            dlnb_ref[:, sl] += jnp.sum(dvn, axis=0, keepdims=True)
            dxh = dvn * lng[:, sl]
            dvg.append(rstd * (dxh - jnp.mean(dxh, axis=-1, keepdims=True) - xhat * jnp.mean(dxh * xhat, axis=-1, keepdims=True)))
        dv_ref[...] = (jnp.concatenate(dvg, axis=1) * _gelu_grad(v)).astype(BF16)

    p512, w3 = _full((1, GM_W)), _full((GM_H, 128, 128))
    return pl.pallas_call(
        body, name=name, grid=(B, S // tb),
        in_specs=[u_spec, v_spec, do_spec, p512, p512, w3, w3, _full((128, GM_W)), p512],
        out_specs=[tok, tok, p512, p512, w3, _full((128, GM_W)), p512],
        out_shape=[jax.ShapeDtypeStruct((B, S, GM_W), BF16)] * 2 + [
            jax.ShapeDtypeStruct((1, GM_W), F32), jax.ShapeDtypeStruct((1, GM_W), F32),
            jax.ShapeDtypeStruct((GM_H, 128, 128), F32), jax.ShapeDtypeStruct((128, GM_W), F32),
            jax.ShapeDtypeStruct((1, GM_W), F32)],
        compiler_params=pltpu.CompilerParams(dimension_semantics=("arbitrary", "arbitrary")),
    )(P, P, dcat, lng, lnb, wt, wtT, bsx, og)


def _lane_half():
    return lax.broadcasted_iota(jnp.int32, (128, 128), 1) // 64


def _att_stack(x, kvh, dtype):
    half = _lane_half()
    rows = []
    for g in range(4):
        i = kvh * 4 + g
        pair = x[:, (i // 2) * 128:(i // 2 + 1) * 128]
        if i % 2 != kvh:
            pair = pltpu.roll(pair, 64, 1)
        rows.append(jnp.where(half == kvh, pair, 0.0))
    return jnp.concatenate(rows, axis=0).astype(dtype)


def _att_unstack(pairs, y, kvh):
    half = _lane_half()
    for g in range(4):
        i = kvh * 4 + g
        piece = y[g * 128:(g + 1) * 128]
        if i % 2 != kvh:
            piece = pltpu.roll(piece, 64, 1)
        pairs[i // 2] = jnp.where(half == i % 2, piece, pairs[i // 2])
    return pairs


def _att_fill_bias(bias_ref):
    qi = lax.broadcasted_iota(jnp.int32, (512, 256), 0) % 128
    kj = lax.broadcasted_iota(jnp.int32, (512, 256), 1)
    diff = qi + 128 - kj
    band = (diff >= 0) & (diff < 128)
    bias_ref[0:512, :] = jnp.where(band, 0.0, NEG_INF)
    bias_ref[512:1024, :] = jnp.where(band & (kj >= 128), 0.0, NEG_INF)


def _att_bias(bias_ref, n):
    return bias_ref[pl.ds(pl.multiple_of(jnp.where(n == 0, 512, 0), 512), 512), :]


def _att_probs(qb, k2, bias, sink_ref, kvh):
    qm = _att_stack(qb, kvh, BF16)
    s = _dot_nt(qm, k2) * (64 ** -0.5) + bias
    grp = lax.broadcasted_iota(jnp.int32, (512, 1), 0) // 128
    sink = jnp.zeros((512, 1), F32)
    for g in range(4):
        sink = jnp.where(grp == g, sink_ref[kvh * 4 + g], sink)
    m = jnp.maximum(jnp.max(s, axis=-1, keepdims=True), sink)
    e = jnp.exp(s - m)
    esink = jnp.exp(sink - m)
    inv = 1.0 / (jnp.sum(e, axis=-1, keepdims=True) + esink)
    return qm, e * inv, esink * inv


def _att_specs(S):
    q = pl.BlockSpec((None, S, ATT_W), lambda b: (b, 0, OFF["q"] // ATT_W))
    k = pl.BlockSpec((None, S, KV_W), lambda b: (b, 0, OFF["k"] // KV_W))
    v = pl.BlockSpec((None, S, KV_W), lambda b: (b, 0, OFF["vv"] // KV_W))
    tok = pl.BlockSpec((None, S, ATT_W), lambda b: (b, 0, 0))
    kv = pl.BlockSpec((None, S, KV_W), lambda b: (b, 0, 0))
    return q, k, v, tok, kv


_SMEM = pl.BlockSpec(memory_space=pltpu.SMEM)


def _attn_fwd(P, sinks, og, cat, name):
    B, S, _ = P.shape
    q_spec, k_spec, v_spec, _, _ = _att_specs(S)
    tok = pl.BlockSpec((None, S, ATT_W), lambda b: (b, 0, GM_W // ATT_W))

    def body(q_ref, k_ref, v_ref, sink_ref, og_ref, cat_ref, o_ref, kpad, vpad, bias_ref):
        _att_fill_bias(bias_ref)
        kpad[0:128, :] = jnp.zeros((128, KV_W), BF16)
        vpad[0:128, :] = jnp.zeros((128, KV_W), BF16)
        kpad[128:, :] = k_ref[...].astype(BF16)
        vpad[128:, :] = v_ref[...].astype(BF16)

        def step(n, carry):
            st = pl.multiple_of(n * 128, 128)
            qb = q_ref[pl.ds(st, 128), :]
            k2, v2 = kpad[pl.ds(st, 256), :], vpad[pl.ds(st, 256), :]
            pairs = [jnp.zeros((128, 128), F32)] * 4
            bias = _att_bias(bias_ref, n)
            for kvh in range(2):
                _, p, _ = _att_probs(qb, k2, bias, sink_ref, kvh)
                pairs = _att_unstack(pairs, _dot(p.astype(BF16), v2), kvh)
            o = jnp.concatenate(pairs, axis=1)
            r = lax.rsqrt(jnp.mean(o * o, axis=-1, keepdims=True) + EPS)
            o_ref[pl.ds(st, 128), :] = (o * r * og_ref[...]).astype(BF16)
            return carry

        lax.fori_loop(0, S // 128, step, 0)

    return pl.pallas_call(
        body, name=name, grid=(B,), in_specs=[q_spec, k_spec, v_spec, _SMEM, _full((1, ATT_W)), _ANY], out_specs=tok,
        out_shape=jax.ShapeDtypeStruct(cat.shape, BF16), input_output_aliases={5: 0},
        scratch_shapes=[pltpu.VMEM((S + 128, KV_W), BF16)] * 2 + [pltpu.VMEM((1024, 256), F32)])(P, P, P, sinks, og, cat)


def _attn_bwd(P, dcat, sinks, og, name):
    B, S, _ = P.shape
    q_spec, k_spec, v_spec, tok, kv = _att_specs(S)
    do_spec = pl.BlockSpec((None, S, ATT_W), lambda b: (b, 0, GM_W // ATT_W))

    def body(q_ref, k_ref, v_ref, do_ref, sink_ref, og_ref, dq_ref, dk_ref, dv_ref, dsink_ref, dog_ref,
             kpad, vpad, dkpad, dvpad, bias_ref):
        _att_fill_bias(bias_ref)

        @pl.when(pl.program_id(0) == 0)
        def _():
            dsink_ref[...] = jnp.zeros((8, 128), F32)
            dog_ref[...] = jnp.zeros((1, ATT_W), F32)

        kpad[0:128, :] = jnp.zeros((128, KV_W), BF16)
        vpad[0:128, :] = jnp.zeros((128, KV_W), BF16)
        kpad[128:, :] = k_ref[...].astype(BF16)
        vpad[128:, :] = v_ref[...].astype(BF16)
        dkpad[...] = jnp.zeros((S + 128, KV_W), F32)
        dvpad[...] = jnp.zeros((S + 128, KV_W), F32)
        half = _lane_half()
        head_row = lax.broadcasted_iota(jnp.int32, (8, 128), 0)

        def step(n, carry):
            st = pl.multiple_of(n * 128, 128)
            qb = q_ref[pl.ds(st, 128), :]
            k2, v2 = kpad[pl.ds(st, 256), :], vpad[pl.ds(st, 256), :]
            saved, pairs = [], [jnp.zeros((128, 128), F32)] * 4
            bias = _att_bias(bias_ref, n)
            for kvh in range(2):
                qm, p, psink = _att_probs(qb, k2, bias, sink_ref, kvh)
                o = _dot(p.astype(BF16), v2)
                saved.append((qm, p, psink, o))
                pairs = _att_unstack(pairs, o, kvh)
            o = jnp.concatenate(pairs, axis=1)
            r = lax.rsqrt(jnp.mean(o * o, axis=-1, keepdims=True) + EPS)
            on = o * r
            dout = do_ref[pl.ds(st, 128), :].astype(F32)
            dog_ref[...] += jnp.sum(dout * on, axis=0, keepdims=True)
            dyn = dout * og_ref[...]
            do = r * (dyn - on * jnp.mean(dyn * on, axis=-1, keepdims=True))
            dq_pairs = [jnp.zeros((128, 128), F32)] * 4
            dsink = jnp.zeros((8, 128), F32)
            for kvh in range(2):
                qm, p, psink, og_ = saved[kvh]
                dog = _att_stack(do, kvh, F32)
                delta = jnp.sum(dog * jnp.where(jnp.concatenate([half] * 4, axis=0) == kvh, og_, 0.0), axis=-1, keepdims=True)
                dogb, pb = dog.astype(BF16), p.astype(BF16)
                dvpad[pl.ds(st, 256), :] += _dot_tn(pb, dogb)
                dp = _dot_nt(dogb, v2)
                ds = (p * (dp - delta) * (64 ** -0.5)).astype(BF16)
                sd = psink * delta
                for g in range(4):
                    dsink = dsink - jnp.where(head_row == kvh * 4 + g, jnp.sum(sd[g * 128:(g + 1) * 128]), 0.0)
                dq_pairs = _att_unstack(dq_pairs, _dot(ds, k2), kvh)
                dkpad[pl.ds(st, 256), :] += _dot_tn(ds, qm)
            dsink_ref[...] += dsink
            dq_ref[pl.ds(st, 128), :] = jnp.concatenate(dq_pairs, axis=1).astype(BF16)
            return carry

        lax.fori_loop(0, S // 128, step, 0)
        dk_ref[...] = dkpad[128:, :].astype(BF16)
        dv_ref[...] = dvpad[128:, :].astype(BF16)

    return pl.pallas_call(
        body, name=name, grid=(B,),
        in_specs=[q_spec, k_spec, v_spec, do_spec, _SMEM, _full((1, ATT_W))],
        out_specs=[tok, kv, kv, _full((8, 128)), _full((1, ATT_W))],
        out_shape=[jax.ShapeDtypeStruct((B, S, ATT_W), BF16), jax.ShapeDtypeStruct((B, S, KV_W), BF16),
                   jax.ShapeDtypeStruct((B, S, KV_W), BF16), jax.ShapeDtypeStruct((8, 128), F32),
                   jax.ShapeDtypeStruct((1, ATT_W), F32)],
        scratch_shapes=[pltpu.VMEM((S + 128, KV_W), BF16)] * 2 + [pltpu.VMEM((S + 128, KV_W), F32)] * 2
        + [pltpu.VMEM((1024, 256), F32)],
        compiler_params=pltpu.CompilerParams(dimension_semantics=("arbitrary",)),
    )(P, P, P, dcat, sinks, og)


CONV_TC = 256
CONV_RC = 64


def _conv_taps(ext, r0):
    return [ext[pl.ds(r0 + 8 - k, CONV_RC), :] for k in range(4)]


def _conv_pre(taps, w_ref, b_ref):
    acc = b_ref[...] + w_ref[3:4, :] * taps[0]
    for k in range(1, 4):
        acc = acc + w_ref[3 - k:4 - k, :] * taps[k]
    return acc


def _conv_fwd(P, w8, b, name):
    B, S, _ = P.shape
    nj = CONV_CH // CONV_TC
    x_spec = pl.BlockSpec((None, S, CONV_TC), lambda b_, j: (b_, 0, OFF["xbc"] // CONV_TC + j))
    tok = pl.BlockSpec((None, S, CONV_TC), lambda b_, j: (b_, 0, j))

    def body(x_ref, w_ref, b_ref, o_ref, ext):
        ext[0:8, :] = jnp.zeros((8, CONV_TC), F32)
        ext[8:, :] = x_ref[...]
        for r0 in range(0, S, CONV_RC):
            pre = _conv_pre(_conv_taps(ext, r0), w_ref, b_ref)
            o_ref[pl.ds(r0, CONV_RC), :] = pre * _sigmoid(pre)

    return pl.pallas_call(
        body, name=name, grid=(B, nj),
        in_specs=[x_spec, pl.BlockSpec((8, CONV_TC), lambda b_, j: (0, j)), pl.BlockSpec((1, CONV_TC), lambda b_, j: (0, j))],
        out_specs=tok, out_shape=jax.ShapeDtypeStruct((B, S, CONV_CH), F32),
        scratch_shapes=[pltpu.VMEM((S + 8, CONV_TC), F32)])(P, w8, b)


def _conv_bwd(P, dact, w8, b, name):
    B, S, _ = P.shape
    nj = CONV_CH // CONV_TC
    x_spec = pl.BlockSpec((None, S, CONV_TC), lambda j, b_: (b_, 0, OFF["xbc"] // CONV_TC + j))
    tok = pl.BlockSpec((None, S, CONV_TC), lambda j, b_: (b_, 0, j))
    w_spec = pl.BlockSpec((8, CONV_TC), lambda j, b_: (0, j))
    b_spec = pl.BlockSpec((1, CONV_TC), lambda j, b_: (0, j))

    def body(x_ref, d_ref, w_ref, b_ref, dx_ref, dw_ref, db_ref, ext, extd):
        @pl.when(pl.program_id(1) == 0)
        def _():
            dw_ref[...] = jnp.zeros((8, CONV_TC), F32)
            db_ref[...] = jnp.zeros((1, CONV_TC), F32)

        ext[0:8, :] = jnp.zeros((8, CONV_TC), F32)
        ext[8:, :] = x_ref[...]
        extd[pl.ds(8 + S, 8), :] = jnp.zeros((8, CONV_TC), F32)
        db = jnp.zeros((1, CONV_TC), F32)
        dws = [jnp.zeros((1, CONV_TC), F32)] * 4
        for r0 in range(0, S, CONV_RC):
            taps = _conv_taps(ext, r0)
            pre = _conv_pre(taps, w_ref, b_ref)
            sg = _sigmoid(pre)
            dpre = d_ref[pl.ds(r0, CONV_RC), :] * (sg * (1.0 + pre * (1.0 - sg)))
            extd[pl.ds(8 + r0, CONV_RC), :] = dpre
            db = db + jnp.sum(dpre, axis=0, keepdims=True)
            dws = [dws[i] + jnp.sum(dpre * taps[3 - i], axis=0, keepdims=True) for i in range(4)]
        for r0 in range(0, S, CONV_RC):
            dx = w_ref[3:4, :] * extd[pl.ds(8 + r0, CONV_RC), :]
            for k in range(1, 4):
                dx = dx + w_ref[3 - k:4 - k, :] * extd[pl.ds(8 + r0 + k, CONV_RC), :]
            dx_ref[pl.ds(r0, CONV_RC), :] = dx.astype(BF16)
        db_ref[...] += db
        sub = lax.broadcasted_iota(jnp.int32, (8, CONV_TC), 0)
        dw_ref[...] += sum(jnp.where(sub == i, dws[i], 0.0) for i in range(4))

    return pl.pallas_call(
        body, name=name, grid=(nj, B), in_specs=[x_spec, tok, w_spec, b_spec], out_specs=[tok, w_spec, b_spec],
        out_shape=[jax.ShapeDtypeStruct((B, S, CONV_CH), BF16), jax.ShapeDtypeStruct((8, CONV_CH), F32),
                   jax.ShapeDtypeStruct((1, CONV_CH), F32)],
        scratch_shapes=[pltpu.VMEM((S + 8, CONV_TC), F32), pltpu.VMEM((S + 16, CONV_TC), F32)],
        compiler_params=pltpu.CompilerParams(dimension_semantics=("arbitrary", "arbitrary")),
    )(P, dact, w8, b)


def _ssd_consts():
    hd = np.arange(SSM_W) // SSM_HD
    E = (np.arange(128)[:, None] == hd[None, :]).astype(np.float32)
    tri = (np.arange(128)[:, None] >= np.arange(128)[None, :]).astype(np.float32)
    return jnp.asarray(E, BF16), jnp.asarray(E.T, BF16), jnp.asarray(tri, BF16), jnp.asarray(tri.T, BF16)


def _pieces(x, n):
    out, r = [], x
    for _ in range(n):
        p = r.astype(BF16)
        out.append(p)
        r = r - p.astype(F32)
    return out


def _dot01(x, m01, n):
    return sum(_dot(p, m01) for p in _pieces(x, n))


def _dot01_left(m01, x, n):
    return sum(_dot(m01, p) for p in _pieces(x, n))


def _ssd_pre(xa, dtraw, bias, alog, E, tri):
    lane = lax.broadcasted_iota(jnp.int32, (128, 128), 1)
    pre = dtraw + bias
    dtp = jnp.where(lane < SSM_H, jnp.maximum(pre, 0.0) + jnp.log(1.0 + jnp.exp(-jnp.abs(pre))), 0.0)
    a = -jnp.exp(alog)
    acs = _dot01_left(tri, dtp * a, 3)
    acsT = acs.T
    dtE, acsE = _dot01(dtp, E, 2), _dot01(acs, E, 3)
    X = xa[:, :SSM_W]
    xdt = X * dtE
    wE = jnp.exp(acsE[127:128, :] - acsE)
    eE = jnp.exp(acsE)
    cdE = eE[127:128, :]
    return dict(pre=pre, dtp=dtp, a=a, acs=acs, acsT=acsT, dtE=dtE, acsE=acsE, cdE=cdE, X=X, xdt=xdt, wE=wE, eE=eE)


def _ssd_decay(c, h):
    lm = lax.broadcasted_iota(jnp.int32, (128, 128), 0) >= lax.broadcasted_iota(jnp.int32, (128, 128), 1)
    return jnp.exp(jnp.where(lm, c["acs"][:, h:h + 1] - c["acsT"][h:h + 1, :], NEG_INF))


def _ssd_pair_operands(c, CB, h0):
    lane = lax.broadcasted_iota(jnp.int32, (128, 128), 1)
    L0, L1 = _ssd_decay(c, h0), _ssd_decay(c, h0 + 1)
    M = jnp.concatenate([CB * L0, CB * L1], axis=1).astype(BF16)
    xp = c["xdt"][:, h0 * 64:h0 * 64 + 128]
    BD = jnp.concatenate([jnp.where(lane < 64, xp, 0.0), jnp.where(lane >= 64, xp, 0.0)], axis=0).astype(BF16)
    return L0, L1, M, BD


def _ssd_y(c, xa, state_ref, dskipE):
    per_group, ys = [], []
    for g in range(SSM_G):
        gs = slice(g * 512, (g + 1) * 512)
        Bb = xa[:, SSM_W + g * 128:SSM_W + (g + 1) * 128].astype(BF16)
        Cb = xa[:, SSM_W + 256 + g * 128:SSM_W + 256 + (g + 1) * 128].astype(BF16)
        CB = _dot_nt(Cb, Bb)
        Sg = state_ref[:, gs]
        yoff = _dot(Cb, Sg.astype(BF16)) * c["eE"][:, gs]
        ydiag, pairs = [], []
        for j in range(4):
            ops = _ssd_pair_operands(c, CB, g * 8 + 2 * j)
            pairs.append(ops)
            ydiag.append(_dot(ops[2], ops[3]))
        ys.append(jnp.concatenate(ydiag, axis=1) + yoff)
        per_group.append(dict(Bb=Bb, Cb=Cb, CB=CB, Sg=Sg, yoff=yoff, pairs=pairs))
    Y = jnp.concatenate(ys, axis=1) + c["X"] * dskipE
    return Y, per_group


def _ssd_specs(S, rev):
    nc = S // CHUNK
    cm = (lambda b, i: (b, nc - 1 - i)) if rev else (lambda b, i: (b, i))
    xa = pl.BlockSpec((None, CHUNK, CONV_CH), lambda b, i: cm(b, i) + (0,))
    z = [pl.BlockSpec((None, CHUNK, 256), lambda b, i, q=q: cm(b, i) + (OFF["z"] // 256 + q,)) for q in range(4)]
    dt = pl.BlockSpec((None, CHUNK, 128), lambda b, i: cm(b, i) + (OFF["dt"] // 128,))
    tok = pl.BlockSpec((None, CHUNK, SSM_W), lambda b, i: cm(b, i) + (0,))
    st = pl.BlockSpec((None, None, 128, SSM_W), lambda b, i: cm(b, i) + (0, 0))
    return nc, xa, z, dt, tok, st


def _ssd_fwd(xact, P, bias, alog, dskipE, ng, cat, name):
    B, S, _ = P.shape
    nc, xa_spec, z_specs, dt_spec, _, st_spec = _ssd_specs(S, False)
    tok = pl.BlockSpec((None, CHUNK, SSM_W), lambda b, i: (b, i, 1))
    E, _, tri, _ = _ssd_consts()

    def body(xa_ref, z0, z1, z2, z3, dt_ref, bias_ref, alog_ref, dsk_ref, ng_ref, E_ref, tri_ref, cat_ref, o_ref, sp_ref, state):
        @pl.when(pl.program_id(1) == 0)
        def _():
            state[...] = jnp.zeros((128, SSM_W), F32)

        sp_ref[...] = state[...]
        xa = xa_ref[...]
        c = _ssd_pre(xa, dt_ref[...], bias_ref[...], alog_ref[...], E_ref[...], tri_ref[...])
        Y, groups = _ssd_y(c, xa, state, dsk_ref[...])
        Z = (c["xdt"] * c["wE"]).astype(BF16)
        for g in range(SSM_G):
            gs = slice(g * 512, (g + 1) * 512)
            state[:, gs] = groups[g]["Sg"] * c["cdE"][:, gs] + _dot_tn(groups[g]["Bb"], Z[:, gs])
        zv = jnp.concatenate([z0[...], z1[...], z2[...], z3[...]], axis=1)
        yz = Y * (zv * _sigmoid(zv))
        outs = []
        for g in range(SSM_G):
            yg = yz[:, g * 512:(g + 1) * 512]
            outs.append(yg * lax.rsqrt(jnp.mean(yg * yg, axis=-1, keepdims=True) + EPS))
        o_ref[...] = (jnp.concatenate(outs, axis=1) * ng_ref[...]).astype(BF16)

    return pl.pallas_call(
        body, name=name, grid=(B, nc),
        in_specs=[xa_spec] + z_specs + [dt_spec, _full((1, 128)), _full((1, 128)), _full((1, SSM_W)), _full((1, SSM_W)),
                                        _full((128, SSM_W)), _full((128, 128)), _ANY],
        out_specs=[tok, st_spec],
        out_shape=[jax.ShapeDtypeStruct(cat.shape, BF16), jax.ShapeDtypeStruct((B, nc, 128, SSM_W), F32)],
        scratch_shapes=[pltpu.VMEM((128, SSM_W), F32)], input_output_aliases={12: 0},
        compiler_params=pltpu.CompilerParams(dimension_semantics=("arbitrary", "arbitrary")),
    )(xact, P, P, P, P, P, bias, alog, dskipE, ng, E, tri, cat)


def _ssd_bwd(xact, P, sprev, dcat, bias, alog, dskipE, ng, name):
    B, S, _ = P.shape
    nc, xa_spec, z_specs, dt_spec, tok, st_spec = _ssd_specs(S, True)
    do_spec = pl.BlockSpec((None, CHUNK, SSM_W), lambda b, i: (b, nc - 1 - i, 1))
    E, ET, tri, triT = _ssd_consts()
    dt_out = pl.BlockSpec((None, CHUNK, 128), lambda b, i: (b, nc - 1 - i, 0))

    def body(xa_ref, z0, z1, z2, z3, dt_ref, sp_ref, do_ref, bias_ref, alog_ref, dsk_ref, ng_ref, E_ref, ET_ref, tri_ref,
             triT_ref, dxa_ref, dz_ref, ddt_ref, dbias_ref, dalog_ref, ddsk_ref, dng_ref, dstate):
        first = (pl.program_id(0) == 0) & (pl.program_id(1) == 0)

        @pl.when(first)
        def _():
            for ref in (dbias_ref, dalog_ref, ddsk_ref, dng_ref):
                ref[...] = jnp.zeros(ref.shape, F32)

        @pl.when(pl.program_id(1) == 0)
        def _():
            dstate[...] = jnp.zeros((128, SSM_W), F32)

        xa, ETm = xa_ref[...], ET_ref[...]
        c = _ssd_pre(xa, dt_ref[...], bias_ref[...], alog_ref[...], E_ref[...], tri_ref[...])
        Y, groups = _ssd_y(c, xa, sp_ref, dsk_ref[...])
        X, xdt = c["X"], c["xdt"]
        zv = jnp.concatenate([z0[...], z1[...], z2[...], z3[...]], axis=1)
        sg = _sigmoid(zv)
        zs = zv * sg
        yz = Y * zs
        dout = do_ref[...].astype(F32)
        dyz = []
        for g in range(SSM_G):
            gs = slice(g * 512, (g + 1) * 512)
            yg = yz[:, gs]
            r = lax.rsqrt(jnp.mean(yg * yg, axis=-1, keepdims=True) + EPS)
            yn = yg * r
            dng_ref[:, gs] += jnp.sum(dout[:, gs] * yn, axis=0, keepdims=True)
            dyn = dout[:, gs] * ng_ref[:, gs]
            dyz.append(r * (dyn - yn * jnp.mean(dyn * yn, axis=-1, keepdims=True)))
        dyz = jnp.concatenate(dyz, axis=1)
        dz_ref[...] = (dyz * Y * (sg * (1.0 + zv * (1.0 - sg)))).astype(BF16)
        dY = dyz * zs
        ddsk_ref[...] += jnp.sum(dY * X, axis=0, keepdims=True)
        dX = dY * dsk_ref[...]
        lane = lax.broadcasted_iota(jnp.int32, (128, 128), 1)
        sub = lax.broadcasted_iota(jnp.int32, (128, 128), 0)
        colform = jnp.zeros((128, 128), F32)
        rowform = jnp.zeros((128, 128), F32)
        dxdt, gacsE, dBC = [], [], []
        for g in range(SSM_G):
            gs = slice(g * 512, (g + 1) * 512)
            G = groups[g]
            Bb, Cb, CB, Sg = G["Bb"], G["Cb"], G["CB"], G["Sg"]
            dYg = dY[:, gs]
            dQ = (dYg * c["eE"][:, gs]).astype(BF16)
            dSn = dstate[:, gs]
            dSnb = dSn.astype(BF16)
            cd = c["cdE"][:, gs]
            dC = _dot_nt(dQ, Sg.astype(BF16))
            dSprev = _dot_tn(Cb, dQ) + dSn * cd
            t1 = jnp.broadcast_to(jnp.sum(dSn * Sg * cd, axis=0, keepdims=True), (8, 512))
            colform = colform + jnp.where(sub == 127, _dot01(t1, ETm[gs, :], 2)[0:1, :], 0.0)
            Zg = xdt[:, gs] * c["wE"][:, gs]
            dZ = _dot(Bb, dSnb)
            dB = _dot_nt(Zg.astype(BF16), dSnb)
            U = dZ * Zg
            ga = dYg * G["yoff"] - U
            ga = ga + jnp.where(lax.broadcasted_iota(jnp.int32, (128, 512), 0) == 127, jnp.sum(U, axis=0, keepdims=True), 0.0)
            gacsE.append(ga)
            dxg = [None] * 4
            dCB = jnp.zeros((128, 128), F32)
            for j in range(4):
                h0 = g * 8 + 2 * j
                L0, L1, M, BD = G["pairs"][j]
                dYp = dYg[:, j * 128:(j + 1) * 128].astype(BF16)
                dM = _dot_nt(dYp, BD)
                dBD = _dot_tn(M, dYp)
                dxg[j] = jnp.where(lane < 64, dBD[:128], dBD[128:])
                for t, (h, L) in enumerate(((h0, L0), (h0 + 1, L1))):
                    dMh = dM[:, t * 128:(t + 1) * 128]
                    dCB = dCB + dMh * L
                    Gh = dMh * CB * L
                    colform = colform + jnp.where(lane == h, jnp.sum(Gh, axis=1, keepdims=True), 0.0)
                    rowform = rowform - jnp.where(sub == h, jnp.sum(Gh, axis=0, keepdims=True), 0.0)
            dCBb = dCB.astype(BF16)
            dC = dC + _dot(dCBb, Bb)
            dB = dB + _dot_tn(dCBb, Cb)
            dxdt.append(jnp.concatenate(dxg, axis=1) + dZ * c["wE"][:, gs])
            dBC.append((dB, dC))
            dstate[:, gs] = dSprev
        dxdt = jnp.concatenate(dxdt, axis=1)
        dX = dX + dxdt * c["dtE"]
        ddt = _dot01(dxdt * X, ETm, 2)
        dacs = colform + rowform.T + _dot01(jnp.concatenate(gacsE, axis=1), ETm, 2)
        dda = _dot01_left(triT_ref[...], dacs, 2)
        ddt = ddt + dda * c["a"]
        dalog_ref[...] += jnp.sum(dda * c["dtp"], axis=0, keepdims=True) * c["a"]
        ddtraw = jnp.where(lane < SSM_H, ddt * _sigmoid(c["pre"]), 0.0)
        dbias_ref[...] += jnp.sum(ddtraw, axis=0, keepdims=True)
        ddt_ref[...] = ddtraw.astype(BF16)
        dxa_ref[...] = jnp.concatenate([dX, dBC[0][0], dBC[1][0], dBC[0][1], dBC[1][1]], axis=1)

    p128, p1k = _full((1, 128)), _full((1, SSM_W))
    return pl.pallas_call(
        body, name=name, grid=(B, nc),
        in_specs=[xa_spec] + z_specs + [dt_spec, st_spec, do_spec, p128, p128, p1k, p1k,
                                        _full((128, SSM_W)), _full((SSM_W, 128)), _full((128, 128)), _full((128, 128))],
        out_specs=[xa_spec, tok, dt_out, p128, p128, p1k, p1k],
        out_shape=[jax.ShapeDtypeStruct((B, S, CONV_CH), F32), jax.ShapeDtypeStruct((B, S, SSM_W), BF16),
                   jax.ShapeDtypeStruct((B, S, 128), BF16), jax.ShapeDtypeStruct((1, 128), F32),
                   jax.ShapeDtypeStruct((1, 128), F32), jax.ShapeDtypeStruct((1, SSM_W), F32),
                   jax.ShapeDtypeStruct((1, SSM_W), F32)],
        scratch_shapes=[pltpu.VMEM((128, SSM_W), F32)],
        compiler_params=pltpu.CompilerParams(dimension_semantics=("arbitrary", "arbitrary")),
    )(xact, P, P, P, P, P, sprev, dcat, bias, alog, dskipE, ng, E, ET, tri, triT)


def _adamw(w, parts, m, v, name, tr=512, layer=0, prev=None):
    Ltot, R, C = w.shape
    ns = parts.shape[0]
    tr = min(tr, R)
    assert R % tr == 0 and parts.shape[1:] == (R, C)
    c1 = 1.0 / (1.0 - ADAM_B1 ** ADAM_STEP)
    c2 = 1.0 / (1.0 - ADAM_B2 ** ADAM_STEP)

    def body(w_ref, p_ref, m_ref, v_ref, *rest):
        g_ref, d_ref, mo_ref, vo_ref = rest[-4:]
        g = p_ref[0].astype(F32)
        for s in range(1, ns):
            g = g + p_ref[s].astype(F32)
        mn = ADAM_B1 * m_ref[...] + (1.0 - ADAM_B1) * g
        vn = ADAM_B2 * v_ref[...] + (1.0 - ADAM_B2) * (g * g)
        g_ref[...] = g
        mo_ref[...] = mn
        vo_ref[...] = vn
        d_ref[...] = -ADAM_LR * ((mn * c1) / (jnp.sqrt(vn * c2) + ADAM_EPS) + ADAM_WD * w_ref[...])

    blk = pl.BlockSpec((None, tr, C), lambda i: (layer, i, 0))
    extra = [] if prev is None else list(prev)
    return pl.pallas_call(
        body, name=name, grid=(R // tr,),
        in_specs=[blk, pl.BlockSpec((ns, tr, C), lambda i: (0, i, 0)), blk, blk] + [pl.BlockSpec(memory_space=pl.ANY)] * len(extra),
        out_specs=[blk] * 4, out_shape=[jax.ShapeDtypeStruct((Ltot, R, C), F32)] * 4,
        input_output_aliases={4 + k: k for k in range(len(extra))})(w, parts, m, v, *extra)


_SMALL = ("ada_b", "norm1_g", "gm_ln_g", "gm_ln_b", "gm_ws", "gm_bs", "gm_norm_g", "attn_sinks", "attn_norm_g", "conv_b",
          "dt_bias", "a_log", "d_skip", "ssm_norm_g", "norm2_g", "final_norm_g")


def _pack(arrs):
    flat = []
    for a in arrs:
        f = a.reshape(-1).astype(F32)
        flat.append(jnp.pad(f, (0, (-f.shape[0]) % 1024)))
    return jnp.concatenate(flat).reshape(-1, 128)


def _unpack(pack, like):
    out, r = [], 0
    for a in like:
        n = int(np.prod(a.shape))
        rows = (n + 1023) // 1024 * 8
        out.append(lax.slice(pack, (r, 0), (r + rows, 128)).reshape(-1)[:n].reshape(a.shape))
        r += rows
    return out


def kernel(x, c, ada_w, ada_b, norm1_g, w_in, gm_ln_g, gm_ln_b, gm_ws, gm_bs, gm_norm_g, attn_sinks, attn_norm_g, conv_w, conv_b, dt_bias, a_log, d_skip, ssm_norm_g, w_out, norm2_g, w_mlp1, w_mlp2, final_norm_g, loss_target, m_ada_w, m_ada_b, m_norm1_g, m_w_in, m_gm_ln_g, m_gm_ln_b, m_gm_ws, m_gm_bs, m_gm_norm_g, m_attn_sinks, m_attn_norm_g, m_conv_w, m_conv_b, m_dt_bias, m_a_log, m_d_skip, m_ssm_norm_g, m_w_out, m_norm2_g, m_w_mlp1, m_w_mlp2, m_final_norm_g, v_ada_w, v_ada_b, v_norm1_g, v_w_in, v_gm_ln_g, v_gm_ln_b, v_gm_ws, v_gm_bs, v_gm_norm_g, v_attn_sinks, v_attn_norm_g, v_conv_w, v_conv_b, v_dt_bias, v_a_log, v_d_skip, v_ssm_norm_g, v_w_out, v_norm2_g, v_w_mlp1, v_w_mlp2, v_final_norm_g):
    args = dict(locals())
    B, S, _ = x.shape
    T = B * S
    L = DEPTH
    me = 4 * lax.axis_index("x") + 2 * lax.axis_index("y") + lax.axis_index("c")

    gath = _gather2([c, conv_w], "ag_c")
    big = ("w_in", "w_out", "w_mlp1", "w_mlp2")
    chain = [(n, l) for l in range(L) for n in ("w_in", "w_mlp1", "w_out", "w_mlp2")]
    inflight = {}

    def start_next(order):
        if not chain:
            return jnp.zeros((8, 128), F32)
        n, l = chain.pop(0)
        sems, land_thru, token = _gather_start(zone[n, l], order, f"ag_start_{n}{l}")
        inflight[n, l] = (sems, land_thru)
        return token

    def gathered(n, l, after):
        land = _gather_wait(*inflight.pop((n, l)), after, f"ag_wait_{n}{l}")
        return _gather_finish(land, f"ag_fin_{n}{l}")

    forwarding = {}

    def arrived(n, l, after):
        land = _gather_wait(*inflight.pop((n, l)), after, f"ag_wait_{n}{l}")
        sems, land_thru, token = _forward_start(land, after, f"ag_fwd_start_{n}{l}")
        forwarding[n, l] = (sems, land_thru)
        return token

    def ready(n, l, after):
        return _forward_wait(*forwarding.pop((n, l)), after, f"ag_fwd_wait_{n}{l}")

    me1 = me.astype(jnp.int32).reshape(1)
    zone = {(n, l): _landing_zone(args[n], l, me1, f"ag_zone_{n}{l}") for n, l in chain}
    later_zones = [zone[k] for k in chain[1:]]

    tok = start_next(gath[0])
    c_all = gath[0].reshape(NDEV * B, D) + tok[0, 0]
    c_act = (c_all * jax.nn.sigmoid(c_all)).astype(BF16)
    nb_rows = c_act.shape[0]
    c_pad = jnp.pad(c_act, ((0, 128 - nb_rows), (0, 0)))
    adw = ada_w.astype(BF16)
    mod_part = jnp.stack([_mm(c_pad, adw[l], mode="nn", name=f"mod{l}", tn=768)[:nb_rows] for l in range(L)])
    mod_all = _gather_small([mod_part], "ag_mod", order=later_zones)[0]
    mod_mine = lax.dynamic_slice_in_dim(mod_all, me * B, B, axis=2)
    mod = jnp.transpose(mod_mine, (1, 2, 0, 3)).reshape(L, B, 6 * D) + ada_b[:, None, :]
    mods = [[mod[l][:, None, i * D:(i + 1) * D] for i in range(6)] for l in range(L)]

    win_g, wout_g, w1_g, w2_g = [None] * L, [None] * L, [None] * L, [None] * L

    tril = jnp.tril(jnp.ones((128, 128), F32))
    row = lambda a: a.reshape(1, -1)
    pad128 = lambda a: jnp.pad(a.reshape(1, -1), ((0, 0), (0, 128 - a.shape[-1])))
    small = []
    for l in range(L):
        wt = gm_ws[l] * tril
        small.append(dict(
            lng=row(gm_ln_g[l]), lnb=row(gm_ln_b[l]), wt=wt.astype(BF16), wtT=jnp.swapaxes(wt, 1, 2).astype(BF16),
            bsx=jnp.repeat(gm_bs[l].T, 128, axis=1), gog=row(gm_norm_g[l]), sinks=attn_sinks[l], aog=row(attn_norm_g[l]),
            bias=pad128(dt_bias[l]), alog=pad128(a_log[l]), dskE=jnp.repeat(d_skip[l], SSM_HD).reshape(1, SSM_W),
            sng=row(ssm_norm_g[l]), cb=row(conv_b[l])))
    convw_all = jnp.transpose(gath[1], (1, 2, 0, 3)).reshape(L, 4, CONV_CH)
    convw8 = jnp.pad(convw_all, ((0, 0), (0, 4), (0, 0)))

    saved = []
    xl = x
    g_in = gathered("w_in", 0, mod)
    tok = start_next(g_in)
    h = _norm_fwd(xl, row(norm1_g[0]) + tok[0, 0], mods[0][1], mods[0][0], "norm1_f0")
    for l in range(L):
        sm = small[l]
        win_g[l] = _shards_to_cols(g_in, f"w_in_cols{l}")
        P = _mm(h.reshape(T, D), win_g[l], mode="nn", name=f"proj_in{l}", tn=1536, order=tok).reshape(B, S, PW)
        cat = _gmlp_fwd(P, sm["lng"], sm["lnb"], sm["wt"], sm["bsx"], sm["gog"], f"gmlp_f{l}")
        cat = _attn_fwd(P, sm["sinks"], sm["aog"], cat, f"attn_f{l}")
        xact = _conv_fwd(P, convw8[l], sm["cb"], f"conv_f{l}")
        tok = start_next(arrived("w_mlp1", l, xact))
        cat, sprev = _ssd_fwd(xact, P, sm["bias"], sm["alog"], sm["dskE"], sm["sng"] + tok[0:1, 0:1], cat, f"ssd_f{l}")
        g_out = gathered("w_out", l, cat)
        tok = start_next(g_out)
        wout_g[l] = g_out.reshape(D, D)
        mix = _mm(cat.reshape(T, D), wout_g[l], mode="nn", name=f"proj_out{l}", order=tok).reshape(B, S, D)
        x_mid, h2 = _norm_fwd(xl, row(norm2_g[l]), mods[l][4], mods[l][3], f"norm2_f{l}", resid=(mix, mods[l][2]))
        w1_g[l] = ready("w_mlp1", l, h2)
        a_act, r_act = _mm(h2.reshape(T, D), w1_g[l], mode="nn", name=f"mlp1_{l}", out_dtypes=(BF16, BF16), col_blocked_b=True,
                           epilogue=lambda acc: (acc, jnp.square(jnp.maximum(acc, 0.0))))
        g_2 = gathered("w_mlp2", l, r_act)
        tok = start_next(g_2)
        w2_g[l] = g_2.reshape(DFF, D)
        m2 = _mm(r_act, w2_g[l], mode="nn", name=f"mlp2_{l}", order=tok, tk=4096).reshape(B, S, D)
        saved.append(dict(x_in=xl, h=h, P=P, xact=xact, sprev=sprev, cat=cat, mix=mix, x_mid=x_mid, h2=h2, a=a_act, r=r_act, m2=m2))
        if l + 1 < L:
            tok = start_next(arrived("w_in", l + 1, m2))
            xl, h = _norm_fwd(x_mid, row(norm1_g[l + 1]) + tok[0, 0], mods[l + 1][1], mods[l + 1][0], f"norm1_f{l + 1}",
                              resid=(m2, mods[l][5]))
            g_in = ready("w_in", l + 1, h)

    sv = saved[L - 1]
    nb = _norm_bwd(sv["x_mid"], row(final_norm_g), "final_b", tgt=loss_target, br=sv["m2"], gate=mods[L - 1][5], x_is_prev=True)
    loss_part, g_final = nb["loss"], nb["dg"]
    dmod, gsm, gconvw = [None] * L, [None] * L, [None] * L
    core = lax.axis_index("c").astype(jnp.int32).reshape(1)
    reducing = []

    def reduce_start(n, l, sent, after):
        p, from_sib = _pair_wait(*sent[:3], after, f"rs_pair_wait_{n}{l}")
        s, land = _pair_add(p, from_sib, core, f"rs_add_{n}{l}")
        return reduce_exchange(n, l, s, land, after)

    def reduce_exchange(n, l, s, land, order):
        sems, s_thru, land_thru, token = _chipsum_start(s, land, order, f"rs_start_{n}{l}")
        reducing.append((n, l, sems, s_thru, land_thru))
        return token

    other = 1 - core

    for l in reversed(range(L)):
        sv, sm = saved[l], small[l]
        dm2, dxo, dg2 = nb["dbr"].reshape(T, D), nb["dx"], nb["dgate"]
        da = _mm(dm2, w2_g[l], mode="nt", name=f"mlp2_dx{l}", out_dtypes=(BF16,), extras=(sv["a"],),
                 epilogue=lambda acc, a: (acc * (2.0 * jnp.maximum(a.astype(F32), 0.0)),))
        h2f = sv["h2"].reshape(T, D)
        sent2 = _sibling_start(_dw_half(sv["r"], dm2, other, axis="m", name=f"mlp2_dw_sib{l}"), da, f"rs_sib_start_w_mlp2{l}")
        dh2 = _mm(da, w1_g[l], mode="nt", name=f"mlp1_dx{l}", col_blocked_b=True, order=sent2[3], tk=4096,
                  out_dtypes=(BF16,)).reshape(B, S, D)
        from_sib = _sibling_wait(*sent2[:3], dh2, f"rs_sib_wait_w_mlp2{l}")[1]
        sent1 = _sibling_start(_dw_half(h2f, da, other, axis="n", name=f"mlp1_dw_sib{l}", order=from_sib), da,
                               f"rs_sib_start_w_mlp1{l}")
        s2, land2 = _dw_half(sv["r"], dm2, core, axis="m", name=f"mlp2_dw_own{l}", add=from_sib, order=sent1[3])
        tok = reduce_exchange("w_mlp2", l, s2, land2, da)
        nb2 = _norm_bwd(sv["x_mid"], row(norm2_g[l]) + tok[0, 0], f"norm2_b{l}", sc=mods[l][4], dh=dh2, dres=dxo, br=sv["mix"],
                        gate=mods[l][2])
        dmix = nb2["dbr"].reshape(T, D)
        from_sib = _sibling_wait(*sent1[:3], dmix, f"rs_sib_wait_w_mlp1{l}")[1]
        s1, land1 = _dw_half(h2f, da, core, axis="n", name=f"mlp1_dw_own{l}", add=from_sib)
        tok = reduce_exchange("w_mlp1", l, s1, land1, dmix)
        dcat = _mm(dmix, wout_g[l], mode="nt", name=f"proj_out_dx{l}", order=tok, out_dtypes=(BF16,)).reshape(B, S, D)
        du, dv, dlng, dlnb, dws, dbsx, dgog = _gmlp_bwd(sv["P"], dcat, sm["lng"], sm["lnb"], sm["wt"], sm["wtT"], sm["bsx"],
                                                        sm["gog"], f"gmlp_b{l}")
        dq, dk, dvv, dsink, daog = _attn_bwd(sv["P"], dcat, sm["sinks"], sm["aog"], f"attn_b{l}")
        catf = sv["cat"].reshape(T, D)
        sent = _sibling_start(_dw_half(catf, dmix, other, axis="m", name=f"proj_out_dw_sib{l}", tile=D // NDEV, order=dq), dmix,
                              f"rs_sib_start_w_out{l}")
        dxa, dz, ddt, dbias, dalog, ddsk, dsng = _ssd_bwd(sv["xact"], sv["P"], sv["sprev"], dcat, sm["bias"], sm["alog"],
                                                          sm["dskE"], sm["sng"] + sent[3][0:1, 0:1], f"ssd_b{l}")
        s_o, land_o = _dw_half(catf, dmix, core, axis="m", name=f"proj_out_dw_own{l}", tile=D // NDEV,
                               add=_sibling_wait(*sent[:3], dxa, f"rs_sib_wait_w_out{l}")[1])
        tok = reduce_exchange("w_out", l, s_o, land_o, dxa)
        dxbc, dcw, dcb = _conv_bwd(sv["P"], dxa, convw8[l], sm["cb"] + tok[0:1, 0:1], f"conv_b{l}")
        dP = _concat_cols([du, dv, dq, dk, dvv, dz, dxbc, ddt], PW, f"dproj_cols{l}").reshape(T, PW)
        dwin = _mm(sv["h"].reshape(T, D), dP, mode="tn", name=f"proj_in_dw{l}", out_dtypes=(BF16,), tn=1536, tk=4096)
        sent = _sibling_start(dwin, dP, f"rs_sib_start_w_in{l}")
        dh = _mm(dP, win_g[l], mode="nt", name=f"proj_in_dx{l}", tk=PW, order=sent[3], out_dtypes=(BF16,)).reshape(B, S, D)
        s_in, land_in = _cols_to_my_shards(*_sibling_wait(*sent[:3], dh, f"rs_sib_wait_w_in{l}"), core, f"w_in_dshards{l}")
        tok = reduce_exchange("w_in", l, s_in, land_in, dh)
        nb = _norm_bwd(sv["x_in"], row(norm1_g[l]) + tok[0, 0], f"norm1_b{l}", sc=mods[l][1], dh=dh, dres=nb2["dx"],
                       br=saved[l - 1]["m2"] if l > 0 else None, gate=mods[l - 1][5] if l > 0 else None)
        dmod[l] = jnp.concatenate([nb["dsh"], nb["dsc"], nb2["dgate"], nb2["dsh"], nb2["dsc"], dg2], axis=-1)
        gconvw[l] = dcw[:4]
        gsm[l] = dict(
            ada_b=jnp.sum(dmod[l], axis=(0, 1)), norm1_g=nb["dg"], gm_ln_g=dlng, gm_ln_b=dlnb, gm_ws=dws,
            gm_bs=dbsx.reshape(128, GM_H, 128).sum(-1).T, gm_norm_g=dgog, attn_sinks=dsink[:, 0], attn_norm_g=daog,
            conv_b=dcb, dt_bias=dbias[0, :SSM_H], a_log=dalog[0, :SSM_H], d_skip=ddsk.reshape(SSM_H, SSM_HD).sum(-1),
            ssm_norm_g=dsng, norm2_g=nb2["dg"])
    grad_x = nb["dx"]

    big_res, after = dict.fromkeys(big), grad_x
    tile_rows = dict(w_in=256, w_out=256, w_mlp1=256, w_mlp2=128)

    def finish_reduce(n, l, sems, s_thru, land_thru, after):
        parts = _chipsum_wait(sems, s_thru, land_thru, after, f"rs_wait_{n}{l}")
        big_res[n] = _adamw(args[n], parts, args["m_" + n], args["v_" + n], f"adamw_{n}{l}", tr=tile_rows[n], layer=l,
                            prev=big_res[n])
        return big_res[n][0]

    per_layer = [n for n in _SMALL if n != "final_norm_g"]
    g_small = [jnp.stack([gsm[l][n].reshape(args[n].shape[1:]) for l in range(L)]) for n in per_layer] + [g_final.reshape(D)]
    zc = jnp.zeros((L, 4, CONV_CH), F32)
    z1 = jnp.zeros((1, 128), F32)
    gpack = _pack([loss_part] + g_small + [jnp.stack(gconvw)])
    small_zones = [_landing_zone(jnp.stack(dmod).reshape(1, L * B, 6 * D), 0, me1, "ag_zone_dmod", dtype=F32),
                   _landing_zone(gpack[None], 0, me1, "ag_zone_small", tr=gpack.shape[0], dtype=F32)]
    small_sems, small_thru, after = _gather_small_start(small_zones, grad_x, "ag_small_start")

    for item in reducing[:-1]:
        after = finish_reduce(*item, after)

    got = _gather_small_wait(small_sems, small_thru, after, "ag_small_wait")
    got = [got[0].reshape(NDEV, L, B, 6 * D), got[1]]
    like = [z1] + [args[n] for n in _SMALL] + [zc]
    packs = [_pack([z1] + [args[p + n] for n in _SMALL] + [zc]) for p in ("", "m_", "v_")]
    sres = [_unpack(p[0], like) for p in _adamw(packs[0][None], got[1], packs[1][None], packs[2][None], "adamw_small",
                                                tr=gpack.shape[0])]
    res = {n: [r[1 + i] for r in sres] for i, n in enumerate(_SMALL)}
    loss = sres[0][0][0, 0]
    gcw = lax.dynamic_slice_in_dim(sres[0][-1], me * (CONV_CH // NDEV), CONV_CH // NDEV, axis=2)

    def update(name, grads, tr):
        r = None
        for l, g in enumerate(grads):
            r = _adamw(args[name], g[None], args["m_" + name], args["v_" + name], f"adamw_{name}{l}", tr=tr, layer=l, prev=r)
        res[name] = r

    update("conv_w", [gcw[l] for l in range(L)], 4)

    dmod_all = jnp.transpose(got[0], (1, 0, 2, 3)).reshape(L, NDEV * B, 6 * D)
    dm_mine = lax.dynamic_slice_in_dim(dmod_all, me * (6 * D // NDEV), 6 * D // NDEV, axis=2)
    dm_pad = jnp.pad(dm_mine, ((0, 0), (0, 128 - nb_rows), (0, 0))).astype(BF16)
    update("ada_w", [_mm(c_pad, dm_pad[l], mode="tn", name=f"ada_dw{l}", tn=768) for l in range(L)], 256)

    finish_reduce(*reducing[-1], res["ada_w"][0])
    for n in big:
        res[n] = [a.reshape(args[n].shape) for a in big_res[n]]

    names = ['ada_w', 'ada_b', 'norm1_g', 'w_in', 'gm_ln_g', 'gm_ln_b', 'gm_ws', 'gm_bs', 'gm_norm_g', 'attn_sinks',
             'attn_norm_g', 'conv_w', 'conv_b', 'dt_bias', 'a_log', 'd_skip', 'ssm_norm_g', 'w_out', 'norm2_g', 'w_mlp1',
             'w_mlp2', 'final_norm_g']
    return (loss, grad_x, *[res[n][0] for n in names], *[res[n][1] for n in names], *[res[n][2] for n in names],
            *[res[n][3] for n in names])
```
